```python
import jax, jax.numpy as jnp
from jax import lax
import numpy as np

D_MODEL = 1024
BATCH = 16
SEQ = 4096
DEPTH = 1

ATTN_HEADS = 8
ATTN_KV_HEADS = 2
ATTN_HEAD_DIM = 64
ATTN_GROUP = ATTN_HEADS // ATTN_KV_HEADS
WINDOW = 128
ATTN_BLOCK = 128
ROT_DIM = ATTN_HEAD_DIM // 4
ROPE_THETA = 500000.0
ATTN_Q_WIDTH = ATTN_HEADS * ATTN_HEAD_DIM
ATTN_KV_WIDTH = ATTN_KV_HEADS * ATTN_HEAD_DIM

DN_HEADS = 4
DN_KEY_DIM = 128
DN_VAL_DIM = 128
DN_CONV = 4
DN_CHUNK = 64
DN_K_WIDTH = DN_HEADS * DN_KEY_DIM
DN_V_WIDTH = DN_HEADS * DN_VAL_DIM
DN_CONV_DIM = 2 * DN_K_WIDTH + DN_V_WIDTH

MIX_WIDTH = ATTN_Q_WIDTH + DN_V_WIDTH

FFN_HIDDEN = ((-(-8 * D_MODEL // 3) + 255) // 256) * 256

NORM_EPS = 1e-6

IN_SPLIT_SIZES = (ATTN_Q_WIDTH, ATTN_KV_WIDTH, ATTN_KV_WIDTH, DN_CONV_DIM,
                  DN_HEADS, DN_HEADS, DN_V_WIDTH, D_MODEL, D_MODEL)
IN_WIDTH = sum(IN_SPLIT_SIZES)

kernel_name = "hybrid_swa_sink_gdn_swiglu_adaln"

F32 = jnp.float32


def rms_norm(x, gain):
    xf = x.astype(F32)
    y = xf * lax.rsqrt(jnp.mean(xf * xf, axis=-1, keepdims=True) + NORM_EPS)
    return (y * gain.astype(F32)).astype(x.dtype)


def l2_norm(x):
    return x * lax.rsqrt(jnp.sum(x * x, axis=-1, keepdims=True) + NORM_EPS)


def modulate(h, shift, scale):
    return h * (1 + scale[:, None, :]) + shift[:, None, :]


def partial_rope(x, cos, sin):
    half = ROT_DIM // 2
    x1 = x[..., :half].astype(F32)
    x2 = x[..., half:ROT_DIM].astype(F32)
    rot = jnp.concatenate([x1 * cos - x2 * sin, x2 * cos + x1 * sin], axis=-1)
    return jnp.concatenate([rot.astype(x.dtype), x[..., ROT_DIM:]], axis=-1)


def sliding_window_attention(q, k, v, sinks):
    B, S = q.shape[0], q.shape[1]
    nb = S // ATTN_BLOCK
    qb = q.astype(F32).reshape(B, nb, ATTN_BLOCK, ATTN_KV_HEADS, ATTN_GROUP, ATTN_HEAD_DIM)

    def with_prev(t):
        tb = t.astype(F32).reshape(B, nb, ATTN_BLOCK, ATTN_KV_HEADS, ATTN_HEAD_DIM)
        prev = jnp.pad(tb, ((0, 0), (1, 0), (0, 0), (0, 0), (0, 0)))[:, :-1]
        return jnp.concatenate([prev, tb], axis=2)

    kb = with_prev(k)
    vb = with_prev(v)
    s = jnp.einsum('bnqhgd,bnkhd->bnhgqk', qb, kb) * (ATTN_HEAD_DIM ** -0.5)
    qi = jnp.arange(ATTN_BLOCK)[:, None] + ATTN_BLOCK
    kj = jnp.arange(2 * ATTN_BLOCK)[None, :]
    dist = qi - kj
    band = (dist >= 0) & (dist < WINDOW)
    key_pos = jnp.arange(nb)[:, None] * ATTN_BLOCK + kj - ATTN_BLOCK
    valid = band[None] & (key_pos >= 0)[:, None, :]
    s = jnp.where(valid[None, :, None, None], s, -jnp.inf)
    sink = sinks.astype(F32).reshape(ATTN_KV_HEADS, ATTN_GROUP)[None, None, :, :, None, None]
    m = jnp.maximum(jnp.max(s, axis=-1, keepdims=True), sink)
    p = jnp.exp(s - m)
    probs = p / (jnp.sum(p, axis=-1, keepdims=True) + jnp.exp(sink - m))
    o = jnp.einsum('bnhgqk,bnkhd->bnqhgd', probs, vb)
    return o.reshape(B, S, ATTN_Q_WIDTH)


def causal_short_conv(x, w):
    y = lax.conv_general_dilated(x, w, window_strides=(1,), padding=[(DN_CONV - 1, 0)],
                                 dimension_numbers=('NWC', 'WIO', 'NWC'),
                                 feature_group_count=x.shape[-1])
    return jax.nn.silu(y)


def gated_delta_rule(q, k, v, g, beta):
    B, S, H, dk = q.shape
    dv = v.shape[-1]
    n = S // DN_CHUNK
    C = DN_CHUNK

    def chunk(t):
        return t.reshape(B, n, C, H, -1).transpose(0, 1, 3, 2, 4)

    q = chunk(q) * (dk ** -0.5)
    k = chunk(k)
    v = chunk(v)
    g = g.reshape(B, n, C, H).transpose(0, 1, 3, 2)
    beta = beta.reshape(B, n, C, H).transpose(0, 1, 3, 2)
    gc = jnp.cumsum(g, axis=-1)
    incl = jnp.tril(jnp.ones((C, C), dtype=bool))
    strict = jnp.tril(jnp.ones((C, C), dtype=bool), -1)
    decay = jnp.exp(jnp.where(incl, gc[..., :, None] - gc[..., None, :], -jnp.inf))
    kb = k * beta[..., None]
    L = jnp.where(strict, jnp.einsum('bnhid,bnhjd->bnhij', kb, k) * decay, 0.0)
    eye = jnp.eye(C, dtype=F32)
    T = lax.linalg.triangular_solve(L + eye, jnp.broadcast_to(eye, L.shape), left_side=True,
                                    lower=True, unit_diagonal=True)
    u = jnp.einsum('bnhij,bnhjd->bnhid', T, v * beta[..., None])
    w = jnp.einsum('bnhij,bnhjd->bnhid', T, kb * jnp.exp(gc)[..., None])
    a_intra = jnp.einsum('bnhid,bnhjd->bnhij', q, k) * decay
    q_dec = q * jnp.exp(gc)[..., None]
    g_last = gc[..., -1:]
    k_dec = k * jnp.exp(g_last - gc)[..., None]
    chunk_decay = jnp.exp(g_last[..., 0])

    def step(state, xs):
        u_c, w_c, q_c, k_c, a_c, d_c = xs
        v_new = u_c - jnp.einsum('bhcd,bhde->bhce', w_c, state)
        o_c = (jnp.einsum('bhcd,bhde->bhce', q_c, state)
               + jnp.einsum('bhij,bhje->bhie', a_c, v_new))
        state = state * d_c[..., None, None] + jnp.einsum('bhcd,bhce->bhde', k_c, v_new)
        return state, o_c

    xs = (jnp.moveaxis(u, 1, 0), jnp.moveaxis(w, 1, 0), jnp.moveaxis(q_dec, 1, 0),
          jnp.moveaxis(k_dec, 1, 0), jnp.moveaxis(a_intra, 1, 0), jnp.moveaxis(chunk_decay, 1, 0))
    state0 = jnp.zeros((B, H, dk, dv), dtype=F32)
    _, o = lax.scan(step, state0, xs)
    return o.transpose(1, 0, 3, 2, 4).reshape(B, S, H, dv)


def token_mixers(h, positions, w_in, conv_w, q_norm_g, k_norm_g, sinks, a_log, dt_bias,
                 dn_norm_g, w_branch, w_out):
    B, S, _ = h.shape
    split_idx = np.cumsum(IN_SPLIT_SIZES)[:-1].tolist()
    proj = h @ w_in
    aq, ak, av, dn_qkv, dn_b, dn_a, dn_z, gate_a, gate_d = jnp.split(proj, split_idx, axis=-1)

    aq = rms_norm(aq.reshape(B, S, ATTN_HEADS, ATTN_HEAD_DIM), q_norm_g)
    ak = rms_norm(ak.reshape(B, S, ATTN_KV_HEADS, ATTN_HEAD_DIM), k_norm_g)
    av = av.reshape(B, S, ATTN_KV_HEADS, ATTN_HEAD_DIM)
    inv_freq = ROPE_THETA ** (-jnp.arange(0, ROT_DIM, 2, dtype=F32) / ROT_DIM)
    ang = positions.astype(F32)[..., None] * inv_freq
    cos = jnp.cos(ang)[:, :, None, :]
    sin = jnp.sin(ang)[:, :, None, :]
    aq = partial_rope(aq, cos, sin)
    ak = partial_rope(ak, cos, sin)
    o_attn = sliding_window_attention(aq, ak, av, sinks)

    qkv = causal_short_conv(dn_qkv, conv_w)
    dq, dk_, dv_ = jnp.split(qkv, [DN_K_WIDTH, 2 * DN_K_WIDTH], axis=-1)
    dq = l2_norm(dq.reshape(B, S, DN_HEADS, DN_KEY_DIM).astype(F32))
    dk_ = l2_norm(dk_.reshape(B, S, DN_HEADS, DN_KEY_DIM).astype(F32))
    dv_ = dv_.reshape(B, S, DN_HEADS, DN_VAL_DIM).astype(F32)
    beta = jax.nn.sigmoid(dn_b.astype(F32))
    g = -jnp.exp(a_log.astype(F32)) * jax.nn.softplus(dn_a.astype(F32) + dt_bias.astype(F32))
    o_dn = gated_delta_rule(dq, dk_, dv_, g, beta)
    o_dn = rms_norm(o_dn, dn_norm_g) * jax.nn.silu(
        dn_z.astype(F32).reshape(B, S, DN_HEADS, DN_VAL_DIM))
    o_dn = o_dn.reshape(B, S, DN_V_WIDTH)

    y_attn = o_attn.astype(h.dtype) @ w_branch[:ATTN_Q_WIDTH]
    y_dn = o_dn.astype(h.dtype) @ w_branch[ATTN_Q_WIDTH:]
    merged = jax.nn.sigmoid(gate_a) * y_attn + jax.nn.sigmoid(gate_d) * y_dn
    return merged @ w_out


def swiglu(h, w_gate_up, w_down):
    gate, up = jnp.split(h @ w_gate_up, 2, axis=-1)
    return (jax.nn.silu(gate) * up) @ w_down


def _fwd_setup_inputs(seed: int = 0) -> dict:
    key = jax.random.key(seed)
    ks = jax.random.split(key, 24)
    D = D_MODEL

    def nrm(k, shape, fan_in):
        return jax.random.normal(k, shape, F32) * (fan_in ** -0.5)

    def gain(k, shape):
        return 1.0 + 0.02 * jax.random.normal(k, shape, F32)

    x = jax.random.normal(ks[0], (BATCH, SEQ, D), F32)
    c = jax.random.normal(ks[1], (BATCH, D), F32)
    offset = jax.random.randint(ks[2], (BATCH, 1), 0, 4096, dtype=jnp.int32)
    positions = offset + jnp.arange(SEQ, dtype=jnp.int32)[None, :]
    dt = jnp.exp(jax.random.uniform(ks[13], (DEPTH, DN_HEADS), F32,
                                    jnp.log(0.001), jnp.log(0.1)))
    return {
        "x": x,
        "c": c,
        "positions": positions,
        "ada_w": nrm(ks[3], (DEPTH, D, 6 * D), D),
        "ada_b": 0.01 * jax.random.normal(ks[4], (DEPTH, 6 * D), F32),
        "norm1_g": gain(ks[5], (DEPTH, D)),
        "w_in": nrm(ks[6], (DEPTH, D, IN_WIDTH), D),
        "conv_w": nrm(ks[7], (DEPTH, DN_CONV, 1, DN_CONV_DIM), DN_CONV),
        "q_norm_g": gain(ks[8], (DEPTH, ATTN_HEAD_DIM)),
        "k_norm_g": gain(ks[9], (DEPTH, ATTN_HEAD_DIM)),
        "sinks": 0.5 * jax.random.normal(ks[10], (DEPTH, ATTN_HEADS), F32),
        "a_log": jnp.log(jax.random.uniform(ks[11], (DEPTH, DN_HEADS), F32, 1.0, 16.0)),
        "dt_bias": dt + jnp.log(-jnp.expm1(-dt)),
        "dn_norm_g": gain(ks[12], (DEPTH, DN_VAL_DIM)),
        "w_branch": nrm(ks[14], (DEPTH, MIX_WIDTH, D), ATTN_Q_WIDTH),
        "w_out": nrm(ks[15], (DEPTH, D, D), D),
        "norm2_g": gain(ks[16], (DEPTH, D)),
        "w_gate_up": nrm(ks[17], (DEPTH, D, 2 * FFN_HIDDEN), D),
        "w_down": nrm(ks[18], (DEPTH, FFN_HIDDEN, D), FFN_HIDDEN),
    }


def _fwd_reference(x, c, positions, ada_w, ada_b, norm1_g, w_in, conv_w, q_norm_g, k_norm_g,
              sinks, a_log, dt_bias, dn_norm_g, w_branch, w_out, norm2_g, w_gate_up, w_down):
    cond = jax.nn.silu(c)
    for l in range(DEPTH):
        mod = cond @ ada_w[l] + ada_b[l]
        shift1, scale1, gate1, shift2, scale2, gate2 = jnp.split(mod, 6, axis=-1)
        h = modulate(rms_norm(x, norm1_g[l]), shift1, scale1)
        x = x + gate1[:, None, :] * token_mixers(
            h, positions, w_in[l], conv_w[l], q_norm_g[l], k_norm_g[l], sinks[l], a_log[l],
            dt_bias[l], dn_norm_g[l], w_branch[l], w_out[l])
        h = modulate(rms_norm(x, norm2_g[l]), shift2, scale2)
        x = x + gate2[:, None, :] * swiglu(h, w_gate_up[l], w_down[l])
    return x


import jax as _jax
import jax.numpy as _jnp

TWIN_FORMAT = 'train_step'
FWD_PARAMS = ['x', 'c', 'positions', 'ada_w', 'ada_b', 'norm1_g', 'w_in', 'conv_w', 'q_norm_g', 'k_norm_g', 'sinks', 'a_log', 'dt_bias', 'dn_norm_g', 'w_branch', 'w_out', 'norm2_g', 'w_gate_up', 'w_down']
TWIN_WEIGHTS = ['ada_w', 'ada_b', 'norm1_g', 'w_in', 'conv_w', 'q_norm_g', 'k_norm_g', 'sinks', 'a_log', 'dt_bias', 'dn_norm_g', 'w_branch', 'w_out', 'norm2_g', 'w_gate_up', 'w_down']
TWIN_DIFF_INPUT = 'x'
TWIN_INPUTS = ['x', 'c', 'positions', 'ada_w', 'ada_b', 'norm1_g', 'w_in', 'conv_w', 'q_norm_g', 'k_norm_g', 'sinks', 'a_log', 'dt_bias', 'dn_norm_g', 'w_branch', 'w_out', 'norm2_g', 'w_gate_up', 'w_down', 'loss_target', 'm_ada_w', 'm_ada_b', 'm_norm1_g', 'm_w_in', 'm_conv_w', 'm_q_norm_g', 'm_k_norm_g', 'm_sinks', 'm_a_log', 'm_dt_bias', 'm_dn_norm_g', 'm_w_branch', 'm_w_out', 'm_norm2_g', 'm_w_gate_up', 'm_w_down', 'v_ada_w', 'v_ada_b', 'v_norm1_g', 'v_w_in', 'v_conv_w', 'v_q_norm_g', 'v_k_norm_g', 'v_sinks', 'v_a_log', 'v_dt_bias', 'v_dn_norm_g', 'v_w_branch', 'v_w_out', 'v_norm2_g', 'v_w_gate_up', 'v_w_down']
TWIN_OUTPUTS = ['loss', 'grad_x', 'grad_ada_w', 'grad_ada_b', 'grad_norm1_g', 'grad_w_in', 'grad_conv_w', 'grad_q_norm_g', 'grad_k_norm_g', 'grad_sinks', 'grad_a_log', 'grad_dt_bias', 'grad_dn_norm_g', 'grad_w_branch', 'grad_w_out', 'grad_norm2_g', 'grad_w_gate_up', 'grad_w_down', 'delta_ada_w', 'delta_ada_b', 'delta_norm1_g', 'delta_w_in', 'delta_conv_w', 'delta_q_norm_g', 'delta_k_norm_g', 'delta_sinks', 'delta_a_log', 'delta_dt_bias', 'delta_dn_norm_g', 'delta_w_branch', 'delta_w_out', 'delta_norm2_g', 'delta_w_gate_up', 'delta_w_down', 'new_m_ada_w', 'new_m_ada_b', 'new_m_norm1_g', 'new_m_w_in', 'new_m_conv_w', 'new_m_q_norm_g', 'new_m_k_norm_g', 'new_m_sinks', 'new_m_a_log', 'new_m_dt_bias', 'new_m_dn_norm_g', 'new_m_w_branch', 'new_m_w_out', 'new_m_norm2_g', 'new_m_w_gate_up', 'new_m_w_down', 'new_v_ada_w', 'new_v_ada_b', 'new_v_norm1_g', 'new_v_w_in', 'new_v_conv_w', 'new_v_q_norm_g', 'new_v_k_norm_g', 'new_v_sinks', 'new_v_a_log', 'new_v_dt_bias', 'new_v_dn_norm_g', 'new_v_w_branch', 'new_v_w_out', 'new_v_norm2_g', 'new_v_w_gate_up', 'new_v_w_down']
TWIN_LEAF_KINDS = {'loss': 'loss', 'grad_x': 'grad_x', 'grad_ada_w': 'grad_w', 'grad_ada_b': 'grad_w', 'grad_norm1_g': 'grad_w', 'grad_w_in': 'grad_w', 'grad_conv_w': 'grad_w', 'grad_q_norm_g': 'grad_w', 'grad_k_norm_g': 'grad_w', 'grad_sinks': 'grad_w', 'grad_a_log': 'grad_w', 'grad_dt_bias': 'grad_w', 'grad_dn_norm_g': 'grad_w', 'grad_w_branch': 'grad_w', 'grad_w_out': 'grad_w', 'grad_norm2_g': 'grad_w', 'grad_w_gate_up': 'grad_w', 'grad_w_down': 'grad_w', 'delta_ada_w': 'delta_w', 'delta_ada_b': 'delta_w', 'delta_norm1_g': 'delta_w', 'delta_w_in': 'delta_w', 'delta_conv_w': 'delta_w', 'delta_q_norm_g': 'delta_w', 'delta_k_norm_g': 'delta_w', 'delta_sinks': 'delta_w', 'delta_a_log': 'delta_w', 'delta_dt_bias': 'delta_w', 'delta_dn_norm_g': 'delta_w', 'delta_w_branch': 'delta_w', 'delta_w_out': 'delta_w', 'delta_norm2_g': 'delta_w', 'delta_w_gate_up': 'delta_w', 'delta_w_down': 'delta_w', 'new_m_ada_w': 'new_m', 'new_m_ada_b': 'new_m', 'new_m_norm1_g': 'new_m', 'new_m_w_in': 'new_m', 'new_m_conv_w': 'new_m', 'new_m_q_norm_g': 'new_m', 'new_m_k_norm_g': 'new_m', 'new_m_sinks': 'new_m', 'new_m_a_log': 'new_m', 'new_m_dt_bias': 'new_m', 'new_m_dn_norm_g': 'new_m', 'new_m_w_branch': 'new_m', 'new_m_w_out': 'new_m', 'new_m_norm2_g': 'new_m', 'new_m_w_gate_up': 'new_m', 'new_m_w_down': 'new_m', 'new_v_ada_w': 'new_v', 'new_v_ada_b': 'new_v', 'new_v_norm1_g': 'new_v', 'new_v_w_in': 'new_v', 'new_v_conv_w': 'new_v', 'new_v_q_norm_g': 'new_v', 'new_v_k_norm_g': 'new_v', 'new_v_sinks': 'new_v', 'new_v_a_log': 'new_v', 'new_v_dt_bias': 'new_v', 'new_v_dn_norm_g': 'new_v', 'new_v_w_branch': 'new_v', 'new_v_w_out': 'new_v', 'new_v_norm2_g': 'new_v', 'new_v_w_gate_up': 'new_v', 'new_v_w_down': 'new_v'}


def _forward(args):
    return _fwd_reference(*[args[k] for k in FWD_PARAMS])


def _output_shape():
    out = _jax.eval_shape(lambda: _forward(_fwd_setup_inputs(0)))
    return out.shape, out.dtype

N_MICROBATCH = 1
ADAM_LR = 0.001
ADAM_B1 = 0.9
ADAM_B2 = 0.999
ADAM_EPS = 1e-08
ADAM_WD = 0.01
ADAM_STEP = 10
PER_EXAMPLE_BATCH_AXIS = {'x': 0, 'c': 0, 'positions': 0, 'loss_target': 0}
SHARED_INPUTS = []
_WEIGHT_DTYPES = {'ada_w': _jnp.float32, 'ada_b': _jnp.float32, 'norm1_g': _jnp.float32, 'w_in': _jnp.float32, 'conv_w': _jnp.float32, 'q_norm_g': _jnp.float32, 'k_norm_g': _jnp.float32, 'sinks': _jnp.float32, 'a_log': _jnp.float32, 'dt_bias': _jnp.float32, 'dn_norm_g': _jnp.float32, 'w_branch': _jnp.float32, 'w_out': _jnp.float32, 'norm2_g': _jnp.float32, 'w_gate_up': _jnp.float32, 'w_down': _jnp.float32}
MOMENT_SCALE = {'ada_w': 1.083240e+01, 'ada_b': 2.308198e+01, 'norm1_g': 5.429342e+00, 'w_in': 3.248417e+00, 'conv_w': 2.312361e+00, 'q_norm_g': 1.494727e+00, 'k_norm_g': 1.470013e+00, 'sinks': 1.307992e+00, 'a_log': 1.887562e+01, 'dt_bias': 1.775521e+01, 'dn_norm_g': 4.201162e+01, 'w_branch': 3.662916e+00, 'w_out': 4.969933e+00, 'norm2_g': 5.939367e+01, 'w_gate_up': 2.665867e+00, 'w_down': 2.156837e+00}


def _to_microbatches(a, axis):
    t = _jnp.moveaxis(a, axis, 0)
    t = t.reshape((N_MICROBATCH, t.shape[0] // N_MICROBATCH) + t.shape[1:])
    return _jnp.moveaxis(t, 1, axis + 1)


def setup_inputs(seed: int = 0) -> dict:
    inp = _fwd_setup_inputs(seed)
    key = _jax.random.fold_in(_jax.random.key(seed), 7919)
    shape, _ = _output_shape()
    out = dict(inp)
    out["loss_target"] = _jax.random.normal(_jax.random.fold_in(key, 0), shape, _jnp.float32)
    for i, name in enumerate(TWIN_WEIGHTS):
        w = inp[name].astype(_jnp.float32)
        if MOMENT_SCALE is None:
            s = _jnp.sqrt(_jnp.mean(_jnp.square(w)) + 1e-30)
        else:
            s = MOMENT_SCALE[name]
        km, kv = _jax.random.split(_jax.random.fold_in(key, i + 1))
        out[name] = w
        out["m_" + name] = s * _jax.random.normal(km, w.shape, _jnp.float32)
        out["v_" + name] = (s * s) * _jax.random.uniform(kv, w.shape, _jnp.float32, 0.5, 1.5)
    if N_MICROBATCH > 1:
        for name, axis in PER_EXAMPLE_BATCH_AXIS.items():
            out[name] = _to_microbatches(out[name], axis)
    return {'x': out['x'], 'c': out['c'], 'positions': out['positions'], 'ada_w': out['ada_w'], 'ada_b': out['ada_b'], 'norm1_g': out['norm1_g'], 'w_in': out['w_in'], 'conv_w': out['conv_w'], 'q_norm_g': out['q_norm_g'], 'k_norm_g': out['k_norm_g'], 'sinks': out['sinks'], 'a_log': out['a_log'], 'dt_bias': out['dt_bias'], 'dn_norm_g': out['dn_norm_g'], 'w_branch': out['w_branch'], 'w_out': out['w_out'], 'norm2_g': out['norm2_g'], 'w_gate_up': out['w_gate_up'], 'w_down': out['w_down'], 'loss_target': out['loss_target'], 'm_ada_w': out['m_ada_w'], 'm_ada_b': out['m_ada_b'], 'm_norm1_g': out['m_norm1_g'], 'm_w_in': out['m_w_in'], 'm_conv_w': out['m_conv_w'], 'm_q_norm_g': out['m_q_norm_g'], 'm_k_norm_g': out['m_k_norm_g'], 'm_sinks': out['m_sinks'], 'm_a_log': out['m_a_log'], 'm_dt_bias': out['m_dt_bias'], 'm_dn_norm_g': out['m_dn_norm_g'], 'm_w_branch': out['m_w_branch'], 'm_w_out': out['m_w_out'], 'm_norm2_g': out['m_norm2_g'], 'm_w_gate_up': out['m_w_gate_up'], 'm_w_down': out['m_w_down'], 'v_ada_w': out['v_ada_w'], 'v_ada_b': out['v_ada_b'], 'v_norm1_g': out['v_norm1_g'], 'v_w_in': out['v_w_in'], 'v_conv_w': out['v_conv_w'], 'v_q_norm_g': out['v_q_norm_g'], 'v_k_norm_g': out['v_k_norm_g'], 'v_sinks': out['v_sinks'], 'v_a_log': out['v_a_log'], 'v_dt_bias': out['v_dt_bias'], 'v_dn_norm_g': out['v_dn_norm_g'], 'v_w_branch': out['v_w_branch'], 'v_w_out': out['v_w_out'], 'v_norm2_g': out['v_norm2_g'], 'v_w_gate_up': out['v_w_gate_up'], 'v_w_down': out['v_w_down']}


def _loss(weights, diff, rest, loss_target):
    with _jax.named_scope("forward"):
        args = {**rest, TWIN_DIFF_INPUT: diff, **{k: w.astype(_WEIGHT_DTYPES[k]) for k, w in weights.items()}}
        y = _forward(args)
    with _jax.named_scope("loss_head"):
        err = _jnp.square(y.astype(_jnp.float32) - loss_target)
        return 0.5 * _jnp.sum(_jnp.mean(err, axis=-1)) if err.ndim else 0.5 * err


def _adamw(w, g, m, v):
    m = ADAM_B1 * m + (1.0 - ADAM_B1) * g
    v = ADAM_B2 * v + (1.0 - ADAM_B2) * _jnp.square(g)
    m_hat = m / (1.0 - ADAM_B1 ** ADAM_STEP)
    v_hat = v / (1.0 - ADAM_B2 ** ADAM_STEP)
    delta = -ADAM_LR * (m_hat / (_jnp.sqrt(v_hat) + ADAM_EPS) + ADAM_WD * w)
    return delta, m, v


def reference(x, c, positions, ada_w, ada_b, norm1_g, w_in, conv_w, q_norm_g, k_norm_g, sinks, a_log, dt_bias, dn_norm_g, w_branch, w_out, norm2_g, w_gate_up, w_down, loss_target, m_ada_w, m_ada_b, m_norm1_g, m_w_in, m_conv_w, m_q_norm_g, m_k_norm_g, m_sinks, m_a_log, m_dt_bias, m_dn_norm_g, m_w_branch, m_w_out, m_norm2_g, m_w_gate_up, m_w_down, v_ada_w, v_ada_b, v_norm1_g, v_w_in, v_conv_w, v_q_norm_g, v_k_norm_g, v_sinks, v_a_log, v_dt_bias, v_dn_norm_g, v_w_branch, v_w_out, v_norm2_g, v_w_gate_up, v_w_down):
    given = dict(x=x, c=c, positions=positions, ada_w=ada_w, ada_b=ada_b, norm1_g=norm1_g, w_in=w_in, conv_w=conv_w, q_norm_g=q_norm_g, k_norm_g=k_norm_g, sinks=sinks, a_log=a_log, dt_bias=dt_bias, dn_norm_g=dn_norm_g, w_branch=w_branch, w_out=w_out, norm2_g=norm2_g, w_gate_up=w_gate_up, w_down=w_down, loss_target=loss_target, m_ada_w=m_ada_w, m_ada_b=m_ada_b, m_norm1_g=m_norm1_g, m_w_in=m_w_in, m_conv_w=m_conv_w, m_q_norm_g=m_q_norm_g, m_k_norm_g=m_k_norm_g, m_sinks=m_sinks, m_a_log=m_a_log, m_dt_bias=m_dt_bias, m_dn_norm_g=m_dn_norm_g, m_w_branch=m_w_branch, m_w_out=m_w_out, m_norm2_g=m_norm2_g, m_w_gate_up=m_w_gate_up, m_w_down=m_w_down, v_ada_w=v_ada_w, v_ada_b=v_ada_b, v_norm1_g=v_norm1_g, v_w_in=v_w_in, v_conv_w=v_conv_w, v_q_norm_g=v_q_norm_g, v_k_norm_g=v_k_norm_g, v_sinks=v_sinks, v_a_log=v_a_log, v_dt_bias=v_dt_bias, v_dn_norm_g=v_dn_norm_g, v_w_branch=v_w_branch, v_w_out=v_w_out, v_norm2_g=v_norm2_g, v_w_gate_up=v_w_gate_up, v_w_down=v_w_down)
    weights = {n: given[n] for n in TWIN_WEIGHTS}
    shared = {n: given[n] for n in SHARED_INPUTS}
    per_example = {n: given[n] for n in ['x', 'c', 'positions']}
    grad_fn = _jax.value_and_grad(_loss, argnums=(0, 1))

    def one_microbatch(ex, loss_target):
        ex = dict(ex)
        diff = ex.pop(TWIN_DIFF_INPUT)
        return grad_fn(weights, diff, {**shared, **ex}, loss_target)

    if N_MICROBATCH == 1:
        loss, (grad_w, grad_x) = one_microbatch(per_example, given["loss_target"])
    else:
        def body(carry, xs):
            loss_sum, grad_sum = carry
            l_k, (gw_k, gx_k) = one_microbatch(xs[0], xs[1])
            with _jax.named_scope("update"):
                return (loss_sum + l_k, _jax.tree.map(_jnp.add, grad_sum, gw_k)), gx_k

        init = (_jnp.zeros((), _jnp.float32), _jax.tree.map(_jnp.zeros_like, weights))
        (loss, grad_w), grad_x = _jax.lax.scan(body, init, (per_example, given["loss_target"]))
    with _jax.named_scope("update"):
        delta_w, new_m, new_v = {}, {}, {}
        for n in TWIN_WEIGHTS:
            delta_w[n], new_m[n], new_v[n] = _adamw(weights[n], grad_w[n], given["m_" + n], given["v_" + n])
    return (loss, grad_x, *[grad_w[n] for n in TWIN_WEIGHTS], *[delta_w[n] for n in TWIN_WEIGHTS],
            *[new_m[n] for n in TWIN_WEIGHTS], *[new_v[n] for n in TWIN_WEIGHTS])
```

```python
import functools

import numpy as np
import jax
import jax.numpy as jnp
from jax import lax
from jax.experimental import pallas as pl
from jax.experimental.pallas import tpu as pltpu

F32 = jnp.float32
BF16 = jnp.bfloat16
HI = lax.Precision.HIGHEST

N_DEV = 8
D = 1024
HQ, HKV, HD = 8, 2, 64
GRP = HQ // HKV
BLK = 128
ROT = HD // 4
THETA = 500000.0
QW, KVW = HQ * HD, HKV * HD
DH, DK = 4, 128
CH = 64
DNW = DH * DK
CONV = 4
CONVW = 3 * DNW
FFN = 2816
EPS = 1e-6
IN_W = QW + 2 * KVW + CONVW + 2 * DH + DNW + 2 * D

LR, B1, B2, AEPS, WD, STEP = 0.001, 0.9, 0.999, 1e-08, 0.01, 10

VMEM_LIMIT = 56 * 1024 * 1024
MESH = pl.DeviceIdType.MESH


def _cparams(sem=None, vmem=VMEM_LIMIT):
    return pltpu.CompilerParams(dimension_semantics=sem, vmem_limit_bytes=vmem)


def _full(shape):
    n = len(shape)
    return pl.BlockSpec(shape, lambda *_: (0,) * n)


def _resident(shape):
    n = len(shape)
    return pl.BlockSpec(shape, lambda *_: (0,) * n, pipeline_mode=pl.Buffered(1))


def _rows(tm, w):
    return pl.BlockSpec((None, tm, w), lambda b, i: (b, i, 0))


def _perb(r, w):
    return pl.BlockSpec((None, r, w), lambda b, i: (b, 0, 0))


def _dot(a, b):
    return jnp.dot(a.astype(BF16), b.astype(BF16), preferred_element_type=F32)


def _dot_nt(a, b):
    return lax.dot_general(a.astype(BF16), b.astype(BF16), (((1,), (1,)), ((), ())), preferred_element_type=F32)


def _dot_tn(a, b):
    return lax.dot_general(a.astype(BF16), b.astype(BF16), (((0,), (0,)), ((), ())), preferred_element_type=F32)


def _dot_hi(a, b):
    return jnp.dot(a, b, preferred_element_type=F32, precision=HI)


def _sigmoid(x):
    return jax.nn.sigmoid(x)


def _silu(x):
    return x * jax.nn.sigmoid(x)


def _rms_mod(x, g, scale, shift):
    r = lax.rsqrt(jnp.mean(x * x, axis=-1, keepdims=True) + EPS)
    return (x * r * g) * (1.0 + scale) + shift


def _tile(S):
    return min(256, S)


def _peer(x, y, c, k):
    px = 1 - x if (k >> 2) & 1 else x
    py = 1 - y if (k >> 1) & 1 else y
    pc = 1 - c if k & 1 else c
    return px, py, pc


def _all_gather_small(v, name):
    r, n = v.shape

    def body(v_ref, out_ref, send_sems, recv_sems, local_sem):
        x, y, c = lax.axis_index("x"), lax.axis_index("y"), lax.axis_index("c")
        me = 4 * x + 2 * y + c
        mine = pltpu.make_async_copy(v_ref, out_ref.at[me], local_sem)
        mine.start()
        sends = []
        for k in range(1, N_DEV):
            cp = pltpu.make_async_remote_copy(
                src_ref=v_ref, dst_ref=out_ref.at[me], send_sem=send_sems.at[k - 1], recv_sem=recv_sems.at[k - 1],
                device_id=_peer(x, y, c, k), device_id_type=MESH)
            cp.start()
            sends.append(cp)
        for k in range(1, N_DEV):
            px, py, pc = _peer(x, y, c, k)
            pltpu.make_async_remote_copy(
                src_ref=v_ref, dst_ref=out_ref.at[4 * px + 2 * py + pc], send_sem=send_sems.at[k - 1],
                recv_sem=recv_sems.at[k - 1], device_id=(px, py, pc), device_id_type=MESH).wait_recv()
        for cp in sends:
            cp.wait_send()
        mine.wait()

    return pl.pallas_call(
        body, name=name,
        out_shape=jax.ShapeDtypeStruct((N_DEV, r, n), v.dtype),
        in_specs=[pl.BlockSpec(memory_space=pltpu.VMEM)],
        out_specs=pl.BlockSpec(memory_space=pltpu.VMEM),
        scratch_shapes=[pltpu.SemaphoreType.DMA((N_DEV - 1,)), pltpu.SemaphoreType.DMA((N_DEV - 1,)), pltpu.SemaphoreType.DMA],
    )(v)


def _all_gather_big(v, name):
    r, n = v.shape

    def body(v_ref, out_ref, send_sems, recv_sems, local_sem):
        x, y, c = lax.axis_index("x"), lax.axis_index("y"), lax.axis_index("c")
        me, sibling = (x, y, c), (x, y, 1 - c)
        chips = [(1 - x, y), (x, 1 - y), (1 - x, 1 - y)]

        def rows(px, py, pc):
            return out_ref.at[4 * px + 2 * py + pc]

        def copy(k, block, to, src=None):
            return pltpu.make_async_remote_copy(
                src_ref=rows(*block) if src is None else src, dst_ref=rows(*block),
                send_sem=send_sems.at[k], recv_sem=recv_sems.at[k], device_id=to, device_id_type=MESH)

        mine = pltpu.make_async_copy(v_ref, rows(*me), local_sem)
        mine.start()
        first = [copy(0, me, sibling, src=v_ref)]
        first += [copy(1 + j, me, (*chip, c), src=v_ref) for j, chip in enumerate(chips)]
        for cp in first:
            cp.start()
        passed = [copy(4 + j, (*chip, c), sibling) for j, chip in enumerate(chips)]
        for j, chip in enumerate(chips):
            copy(1 + j, (*chip, c), me).wait_recv()
            passed[j].start()
        copy(0, sibling, me).wait_recv()
        for j, chip in enumerate(chips):
            copy(4 + j, (*chip, 1 - c), me).wait_recv()
        for cp in first + passed:
            cp.wait_send()
        mine.wait()

    return pl.pallas_call(
        body, name=name,
        out_shape=jax.ShapeDtypeStruct((N_DEV, r, n), v.dtype),
        in_specs=[pl.BlockSpec(memory_space=pl.ANY)],
        out_specs=pl.BlockSpec(memory_space=pl.ANY),
        scratch_shapes=[pltpu.SemaphoreType.DMA((7,)), pltpu.SemaphoreType.DMA((7,)), pltpu.SemaphoreType.DMA],
    )(v)


def _exchange_blocks(g, name):
    _, r, n = g.shape

    def body(g_ref, out_ref, send_sems, recv_sems, local_sem):
        x, y, c = lax.axis_index("x"), lax.axis_index("y"), lax.axis_index("c")
        me = 4 * x + 2 * y + c
        mine = pltpu.make_async_copy(g_ref.at[me], out_ref.at[me], local_sem)
        mine.start()
        sends = []
        for k in range(1, N_DEV):
            px, py, pc = _peer(x, y, c, k)
            cp = pltpu.make_async_remote_copy(
                src_ref=g_ref.at[4 * px + 2 * py + pc], dst_ref=out_ref.at[me], send_sem=send_sems.at[k - 1],
                recv_sem=recv_sems.at[k - 1], device_id=(px, py, pc), device_id_type=MESH)
            cp.start()
            sends.append(cp)
        for k in range(1, N_DEV):
            px, py, pc = _peer(x, y, c, k)
            pltpu.make_async_remote_copy(
                src_ref=g_ref.at[me], dst_ref=out_ref.at[4 * px + 2 * py + pc], send_sem=send_sems.at[k - 1],
                recv_sem=recv_sems.at[k - 1], device_id=(px, py, pc), device_id_type=MESH).wait_recv()
        for cp in sends:
            cp.wait_send()
        mine.wait()

    return pl.pallas_call(
        body, name=name,
        out_shape=jax.ShapeDtypeStruct(g.shape, g.dtype),
        in_specs=[pl.BlockSpec(memory_space=pl.ANY)],
        out_specs=pl.BlockSpec(memory_space=pl.ANY),
        scratch_shapes=[pltpu.SemaphoreType.DMA((N_DEV - 1,)), pltpu.SemaphoreType.DMA((N_DEV - 1,)), pltpu.SemaphoreType.DMA],
    )(g)


def _sum_blocks(g, name):
    _, r, n = g.shape
    tr = 176 if r % 176 == 0 else r

    def body(g_ref, o_ref):
        acc = g_ref[0].astype(F32)
        for d in range(1, N_DEV):
            acc = acc + g_ref[d].astype(F32)
        o_ref[...] = acc

    return pl.pallas_call(
        body, name=name, grid=(r // tr,),
        out_shape=jax.ShapeDtypeStruct((r, n), F32),
        in_specs=[pl.BlockSpec((N_DEV, tr, n), lambda i: (0, i, 0))],
        out_specs=pl.BlockSpec((tr, n), lambda i: (i, 0)),
        compiler_params=_cparams(("arbitrary",)),
    )(g)


def _ada_fwd(c_all, ada_w, ada_b_cols):
    nb, ncol = c_all.shape[0], ada_w.shape[1]

    def body(c_ref, w_ref, b_ref, mod_ref, cond_ref):
        cond = _silu(c_ref[...])
        cond_ref[...] = cond
        mod_ref[...] = _dot_hi(cond, w_ref[...]) + b_ref[...]

    return pl.pallas_call(
        body, name="ada_fwd",
        out_shape=(jax.ShapeDtypeStruct((nb, ncol), F32), jax.ShapeDtypeStruct((nb, D), F32)),
        compiler_params=_cparams(),
    )(c_all, ada_w, ada_b_cols)


def _ada_bwd(cond_all, dmod_all, dmod_cols, smalls):
    ncol, nsm = dmod_cols.shape[1], smalls.shape[1]

    def body(cond_ref, dm_ref, dmc_ref, sm_ref, gw_ref, gb_ref, gs_ref):
        gw_ref[...] = lax.dot_general(cond_ref[...], dmc_ref[...], (((0,), (0,)), ((), ())),
                                      preferred_element_type=F32, precision=HI)
        gb_ref[...] = jnp.sum(dm_ref[...], axis=0, keepdims=True)
        gs_ref[...] = jnp.sum(sm_ref[...], axis=0, keepdims=True)

    return pl.pallas_call(
        body, name="ada_bwd",
        out_shape=(jax.ShapeDtypeStruct((D, ncol), F32), jax.ShapeDtypeStruct((1, 6 * D), F32),
                   jax.ShapeDtypeStruct((1, nsm), F32)),
        compiler_params=_cparams(),
    )(cond_all, dmod_all, dmod_cols, smalls)


def _inproj_fwd(x, mod, g1, ws):
    B, S, _ = x.shape
    tm = _tile(S)
    widths = [w.shape[1] for w in ws]

    def body(x_ref, mod_ref, g_ref, *refs):
        w_refs, h_ref, o_refs = refs[:len(ws)], refs[len(ws)], refs[len(ws) + 1:]
        h = _rms_mod(x_ref[...], g_ref[...], mod_ref[1:2, :], mod_ref[0:1, :]).astype(BF16)
        h_ref[...] = h
        for w_ref, o_ref in zip(w_refs, o_refs):
            o_ref[...] = jnp.dot(h, w_ref[...], preferred_element_type=F32)

    return pl.pallas_call(
        body, name="inproj_fwd", grid=(B, S // tm),
        out_shape=[jax.ShapeDtypeStruct((B, S, D), BF16)] + [jax.ShapeDtypeStruct((B, S, w), F32) for w in widths],
        in_specs=[_rows(tm, D), _perb(6, D), _full((1, D))] + [_resident(w.shape) for w in ws],
        out_specs=[_rows(tm, D)] + [_rows(tm, w) for w in widths],
        compiler_params=_cparams(("parallel", "arbitrary")),
    )(x, mod, g1, *ws)


def _inproj_bwd(x, mod, g1, dx1, dps, ws):
    B, S, _ = x.shape
    tm = _tile(S)
    n = len(ws)

    def body(x_ref, mod_ref, g_ref, dx1_ref, *refs):
        dp_refs, w_refs = refs[:n], refs[n:2 * n]
        gx_ref, dg_ref, dsc_ref, dsh_ref = refs[2 * n:]
        b, i = pl.program_id(0), pl.program_id(1)
        dh = _dot_nt(dp_refs[0][...], w_refs[0][...])
        for dp_ref, w_ref in zip(dp_refs[1:], w_refs[1:]):
            dh = dh + _dot_nt(dp_ref[...], w_ref[...])
        _, vjp = jax.vjp(_rms_mod, x_ref[...], g_ref[...], mod_ref[1:2, :], mod_ref[0:1, :])
        dx, dg, dsc, dsh = vjp(dh)
        gx_ref[...] = dx1_ref[...] + dx

        @pl.when((b == 0) & (i == 0))
        def _():
            dg_ref[...] = jnp.zeros_like(dg_ref)

        @pl.when(i == 0)
        def _():
            dsc_ref[...] = jnp.zeros_like(dsc_ref)
            dsh_ref[...] = jnp.zeros_like(dsh_ref)

        dg_ref[...] += dg
        dsc_ref[...] += dsc
        dsh_ref[...] += dsh

    return pl.pallas_call(
        body, name="inproj_bwd", grid=(B, S // tm),
        out_shape=[jax.ShapeDtypeStruct((B, S, D), F32), jax.ShapeDtypeStruct((1, D), F32),
                   jax.ShapeDtypeStruct((B, 1, D), F32), jax.ShapeDtypeStruct((B, 1, D), F32)],
        in_specs=[_rows(tm, D), _perb(6, D), _full((1, D)), _rows(tm, D)]
                 + [_rows(tm, w.shape[1]) for w in ws] + [_resident(w.shape) for w in ws],
        out_specs=[_rows(tm, D), _full((1, D)), _perb(1, D), _perb(1, D)],
        compiler_params=_cparams(("arbitrary", "arbitrary")),
    )(x, mod, g1, dx1, *dps, *ws)


def _wgrad(a, b, name):
    B, S, K = a.shape
    N = b.shape[2]
    tm = min(512, S)
    tn = 512 if N % 512 == 0 else N
    nt = S // tm

    def body(a_ref, b_ref, o_ref):
        t = pl.program_id(1)

        @pl.when(t == 0)
        def _():
            o_ref[...] = jnp.zeros_like(o_ref)

        o_ref[...] += lax.dot_general(a_ref[...], b_ref[...], (((0,), (0,)), ((), ())), preferred_element_type=F32)

    return pl.pallas_call(
        body, name=name, grid=(N // tn, B * nt),
        out_shape=jax.ShapeDtypeStruct((K, N), F32),
        in_specs=[pl.BlockSpec((None, tm, K), lambda j, t: (t // nt, t % nt, 0)),
                  pl.BlockSpec((None, tm, tn), lambda j, t: (t // nt, t % nt, j))],
        out_specs=pl.BlockSpec((K, tn), lambda j, t: (0, j)),
        compiler_params=_cparams(("parallel", "arbitrary")),
    )(a, b)


def _rope_consts():
    inv_freq = THETA ** (-jnp.arange(0, ROT, 2, dtype=F32) / ROT)
    invf = jnp.concatenate([inv_freq, inv_freq, jnp.zeros((HD - ROT,), F32)])[None, :]
    perm = np.zeros((HD, HD), np.float32)
    half = ROT // 2
    for d in range(half):
        perm[d + half, d] = -1.0
        perm[d, d + half] = 1.0
    return invf, jnp.asarray(perm)


def _head_norm_rope(t, g, cos, sin, perm):
    r = lax.rsqrt(jnp.mean(t * t, axis=-1, keepdims=True) + EPS)
    t = t * r * g
    return t * cos + _dot_hi(t, perm) * sin


def _attn_block(q, kvp, kvc, qg, kg, sinks, cq, sq, ck, sk, perm, valid):
    outs = []
    for j in range(HKV):
        kraw = jnp.concatenate([kvp[:, HD * j:HD * (j + 1)], kvc[:, HD * j:HD * (j + 1)]], axis=0)
        v = jnp.concatenate([kvp[:, KVW + HD * j:KVW + HD * (j + 1)], kvc[:, KVW + HD * j:KVW + HD * (j + 1)]], axis=0)
        k = _head_norm_rope(kraw, kg, ck, sk, perm)
        qs, sk_rows = [], []
        for i in range(GRP):
            h = GRP * j + i
            qs.append(_head_norm_rope(q[:, HD * h:HD * (h + 1)], qg, cq, sq, perm))
            sk_rows.append(jnp.broadcast_to(sinks[:, h:h + 1], (BLK, 1)))
        q4 = jnp.concatenate(qs, axis=0)
        sink = jnp.concatenate(sk_rows, axis=0)
        s = _dot_nt(q4, k) * (HD ** -0.5)
        s = jnp.where(valid, s, -1e30)
        m = jnp.maximum(jnp.max(s, axis=-1, keepdims=True), sink)
        p = jnp.exp(s - m)
        probs = p / (jnp.sum(p, axis=-1, keepdims=True) + jnp.exp(sink - m))
        o4 = _dot(probs, v)
        outs += [o4[BLK * i:BLK * (i + 1), :] for i in range(GRP)]
    return jnp.concatenate(outs, axis=1)


def _attn_tables(posp_ref, posc_ref, invf_ref, n):
    posq = posc_ref[...].astype(F32)
    posk = jnp.concatenate([posp_ref[...].astype(F32), posq], axis=0)
    aq, ak = posq * invf_ref[...], posk * invf_ref[...]
    qi = lax.broadcasted_iota(jnp.int32, (GRP * BLK, 2 * BLK), 0) % BLK + BLK
    kj = lax.broadcasted_iota(jnp.int32, (GRP * BLK, 2 * BLK), 1)
    dist = qi - kj
    valid = (dist >= 0) & (dist < BLK) & ((kj >= BLK) | (n > 0))
    return jnp.cos(aq), jnp.sin(aq), jnp.cos(ak), jnp.sin(ak), valid


def _attn_fwd(aq, akv, pos, qg, kg, sinks, invf, perm):
    B, S, _ = aq.shape
    nb = S // BLK

    def body(q_ref, kvp_ref, kvc_ref, posp_ref, posc_ref, qg_ref, kg_ref, sk_ref, invf_ref, perm_ref, o_ref):
        n = pl.program_id(1)
        cq, sq, ck, sk, valid = _attn_tables(posp_ref, posc_ref, invf_ref, n)
        o_ref[...] = _attn_block(q_ref[...], kvp_ref[...], kvc_ref[...], qg_ref[...], kg_ref[...], sk_ref[...],
                                 cq, sq, ck, sk, perm_ref[...], valid)

    prev = lambda b, n: (b, jnp.maximum(n - 1, 0), 0)
    cur = lambda b, n: (b, n, 0)
    return pl.pallas_call(
        body, name="attn_fwd", grid=(B, nb),
        out_shape=jax.ShapeDtypeStruct((B, S, QW), F32),
        in_specs=[pl.BlockSpec((None, BLK, QW), cur), pl.BlockSpec((None, BLK, 2 * KVW), prev),
                  pl.BlockSpec((None, BLK, 2 * KVW), cur), pl.BlockSpec((None, BLK, 1), prev),
                  pl.BlockSpec((None, BLK, 1), cur), _full((1, HD)), _full((1, HD)), _full((1, HQ)),
                  _full((1, HD)), _full((HD, HD))],
        out_specs=pl.BlockSpec((None, BLK, QW), cur),
        compiler_params=_cparams(("parallel", "arbitrary")),
    )(aq, akv, akv, pos, pos, qg, kg, sinks, invf, perm)


def _attn_bwd(aq, akv, pos, qg, kg, sinks, invf, perm, do):
    B, S, _ = aq.shape
    nb = S // BLK

    def body(q_ref, kvp_ref, kvc_ref, posp_ref, posc_ref, qg_ref, kg_ref, sk_ref, invf_ref, perm_ref, do_ref,
             dq_ref, dkv_ref, dqg_ref, dkg_ref, dsk_ref, carry):
        b, i = pl.program_id(0), pl.program_id(1)
        n = nb - 1 - i
        cq, sq, ck, sk, valid = _attn_tables(posp_ref, posc_ref, invf_ref, n)
        fn = functools.partial(_attn_block, cq=cq, sq=sq, ck=ck, sk=sk, perm=perm_ref[...], valid=valid)
        _, vjp = jax.vjp(fn, q_ref[...], kvp_ref[...], kvc_ref[...], qg_ref[...], kg_ref[...], sk_ref[...])
        dq, dkvp, dkvc, dqg, dkg, dsk = vjp(do_ref[...])

        @pl.when(i == 0)
        def _():
            carry[...] = jnp.zeros_like(carry)

        @pl.when((b == 0) & (i == 0))
        def _():
            dqg_ref[...] = jnp.zeros_like(dqg_ref)
            dkg_ref[...] = jnp.zeros_like(dkg_ref)
            dsk_ref[...] = jnp.zeros_like(dsk_ref)

        dq_ref[...] = dq.astype(BF16)
        dkv_ref[...] = (dkvc + carry[...]).astype(BF16)
        carry[...] = dkvp
        dqg_ref[...] += dqg
        dkg_ref[...] += dkg
        dsk_ref[...] += dsk

    prev = lambda b, i: (b, jnp.maximum(nb - 2 - i, 0), 0)
    cur = lambda b, i: (b, nb - 1 - i, 0)
    return pl.pallas_call(
        body, name="attn_bwd", grid=(B, nb),
        out_shape=[jax.ShapeDtypeStruct((B, S, QW), BF16), jax.ShapeDtypeStruct((B, S, 2 * KVW), BF16),
                   jax.ShapeDtypeStruct((1, HD), F32), jax.ShapeDtypeStruct((1, HD), F32),
                   jax.ShapeDtypeStruct((1, HQ), F32)],
        in_specs=[pl.BlockSpec((None, BLK, QW), cur), pl.BlockSpec((None, BLK, 2 * KVW), prev),
                  pl.BlockSpec((None, BLK, 2 * KVW), cur), pl.BlockSpec((None, BLK, 1), prev),
                  pl.BlockSpec((None, BLK, 1), cur), _full((1, HD)), _full((1, HD)), _full((1, HQ)),
                  _full((1, HD)), _full((HD, HD)), pl.BlockSpec((None, BLK, QW), cur)],
        out_specs=[pl.BlockSpec((None, BLK, QW), cur), pl.BlockSpec((None, BLK, 2 * KVW), cur),
                   _full((1, HD)), _full((1, HD)), _full((1, HQ))],
        scratch_shapes=[pltpu.VMEM((BLK, 2 * KVW), F32)],
        compiler_params=_cparams(("arbitrary", "arbitrary")),
    )(aq, akv, akv, pos, pos, qg, kg, sinks, invf, perm, do)


def _conv_taps(xe, w, rows):
    y = None
    for j in range(CONV):
        sh = pltpu.roll(xe, CONV - 1 - j, 0)[8:8 + rows, :] if j < CONV - 1 else xe[8:8 + rows, :]
        y = sh * w[j:j + 1, :] if y is None else y + sh * w[j:j + 1, :]
    return y


def _conv_fwd(xin, w):
    B, S, C = xin.shape
    tc = min(512, S)
    r8 = tc // 8

    def body(xp_ref, x_ref, w_ref, o_ref):
        i = pl.program_id(1)
        xp = jnp.where(i > 0, xp_ref[...], 0.0)
        xe = jnp.concatenate([xp, x_ref[...]], axis=0)
        o_ref[...] = _silu(_conv_taps(xe, w_ref[...], tc))

    return pl.pallas_call(
        body, name="conv_fwd", grid=(B, S // tc),
        out_shape=jax.ShapeDtypeStruct((B, S, C), F32),
        in_specs=[pl.BlockSpec((None, 8, C), lambda b, i: (b, jnp.maximum(i * r8 - 1, 0), 0)),
                  _rows(tc, C), _full((CONV, C))],
        out_specs=_rows(tc, C),
        compiler_params=_cparams(("parallel", "arbitrary")),
    )(xin, xin, w)


def _conv_bwd(xin, w, dy):
    B, S, C = xin.shape
    tc = min(512, S)
    r8 = tc // 8
    nt = S // tc

    def body(xp_ref, x_ref, xn_ref, dy_ref, dyn_ref, w_ref, dx_ref, dw_ref):
        b, i = pl.program_id(0), pl.program_id(1)
        w = w_ref[...]
        xp = jnp.where(i > 0, xp_ref[...], 0.0)
        xe = jnp.concatenate([xp, x_ref[...], xn_ref[...]], axis=0)
        pre = _conv_taps(xe, w, tc + 8)
        sg = _sigmoid(pre)
        dyn = jnp.where(i < nt - 1, dyn_ref[...], 0.0)
        dpre = jnp.concatenate([dy_ref[...], dyn], axis=0) * (sg * (1.0 + pre * (1.0 - sg)))
        dx = dpre[0:tc, :] * w[CONV - 1:CONV, :]
        for j in range(CONV - 1):
            dx = dx + pltpu.roll(dpre, tc + 8 - (CONV - 1 - j), 0)[0:tc, :] * w[j:j + 1, :]
        dx_ref[...] = dx.astype(BF16)
        dcur = dpre[0:tc, :]
        xe0 = xe[0:8 + tc, :]
        lane_row = lax.broadcasted_iota(jnp.int32, (CONV, C), 0)
        dw = jnp.zeros((CONV, C), F32)
        for j in range(CONV):
            sh = pltpu.roll(xe0, CONV - 1 - j, 0)[8:8 + tc, :] if j < CONV - 1 else xe0[8:8 + tc, :]
            dw = dw + jnp.where(lane_row == j, jnp.sum(sh * dcur, axis=0, keepdims=True), 0.0)

        @pl.when((b == 0) & (i == 0))
        def _():
            dw_ref[...] = jnp.zeros_like(dw_ref)

        dw_ref[...] += dw

    return pl.pallas_call(
        body, name="conv_bwd", grid=(B, nt),
        out_shape=[jax.ShapeDtypeStruct((B, S, C), BF16), jax.ShapeDtypeStruct((CONV, C), F32)],
        in_specs=[pl.BlockSpec((None, 8, C), lambda b, i: (b, jnp.maximum(i * r8 - 1, 0), 0)),
                  _rows(tc, C),
                  pl.BlockSpec((None, 8, C), lambda b, i: (b, jnp.minimum((i + 1) * r8, S // 8 - 1), 0)),
                  _rows(tc, C),
                  pl.BlockSpec((None, 8, C), lambda b, i: (b, jnp.minimum((i + 1) * r8, S // 8 - 1), 0)),
                  _full((CONV, C))],
        out_specs=[_rows(tc, C), _full((CONV, C))],
        compiler_params=_cparams(("arbitrary", "arbitrary")),
    )(xin, xin, xin, dy, dy, w)


def _softplus(x):
    return jnp.maximum(x, 0.0) + jnp.log1p(jnp.exp(-jnp.abs(x)))


def _tri_inverse(L):
    eye = (lax.broadcasted_iota(jnp.int32, (CH, CH), 0) == lax.broadcasted_iota(jnp.int32, (CH, CH), 1)).astype(F32)
    T = eye - L
    P = L
    n = 2
    while n < CH:
        P = _dot_hi(P, P)
        T = T + _dot_hi(T, P)
        n *= 2
    return T


def _dn_head(S0, qr, kr, v, a_raw, b_raw, a_log, dt_b):
    ii = lax.broadcasted_iota(jnp.int32, (CH, CH), 0)
    jj = lax.broadcasted_iota(jnp.int32, (CH, CH), 1)
    incl, strict = ii >= jj, ii > jj
    q = qr * lax.rsqrt(jnp.sum(qr * qr, axis=-1, keepdims=True) + EPS) * (DK ** -0.5)
    k = kr * lax.rsqrt(jnp.sum(kr * kr, axis=-1, keepdims=True) + EPS)
    beta = _sigmoid(b_raw)
    g = -jnp.exp(a_log) * _softplus(a_raw + dt_b)
    gb = jnp.broadcast_to(g, (CH, DK))
    gcb = _dot_hi(incl.astype(F32), gb)
    gc = gcb[:, 0:1]
    gc_row = lax.dot_general(jnp.full((CH, DK), 1.0 / DK, F32), gcb, (((1,), (1,)), ((), ())),
                             preferred_element_type=F32, precision=HI)
    decay = jnp.where(incl, jnp.exp(jnp.where(incl, gc - gc_row, 0.0)), 0.0)
    kb = k * beta
    L = jnp.where(strict, _dot_nt(kb, k) * decay, 0.0)
    T = _tri_inverse(L)
    eg = jnp.exp(gc)
    u = _dot(T, v * beta)
    w = _dot(T, kb * eg)
    a_in = _dot_nt(q, k) * decay
    g_last = gc[CH - 1:CH, :]
    k_dec = k * jnp.exp(g_last - gc)
    v_new = u - _dot(w, S0)
    o = _dot(q * eg, S0) + _dot(a_in, v_new)
    S1 = S0 * jnp.exp(g_last) + _dot_tn(k_dec, v_new)
    return o, S1


def _dn_slices(cq, ba, h):
    return (cq[:, DK * h:DK * (h + 1)], cq[:, DNW + DK * h:DNW + DK * (h + 1)],
            cq[:, 2 * DNW + DK * h:2 * DNW + DK * (h + 1)], ba[:, DH + h:DH + h + 1], ba[:, h:h + 1])


def _dn_fwd(cq, ba, a_log, dt_b):
    B, S, _ = cq.shape
    nc = S // CH

    def body(cq_ref, ba_ref, al_ref, dt_ref, o_ref, st_ref, state):
        i = pl.program_id(1)

        @pl.when(i == 0)
        def _():
            state[...] = jnp.zeros_like(state)

        cqv, bav = cq_ref[...], ba_ref[...]
        outs = []
        for h in range(DH):
            S0 = state[h]
            st_ref[h] = S0
            o, S1 = _dn_head(S0, *_dn_slices(cqv, bav, h), al_ref[:, h:h + 1], dt_ref[:, h:h + 1])
            state[h] = S1
            outs.append(o)
        o_ref[...] = jnp.concatenate(outs, axis=1)

    return pl.pallas_call(
        body, name="dn_fwd", grid=(B, nc),
        out_shape=[jax.ShapeDtypeStruct((B, S, DNW), F32), jax.ShapeDtypeStruct((B, nc, DH, DK, DK), F32)],
        in_specs=[_rows(CH, CONVW), _rows(CH, 2 * DH), _full((1, DH)), _full((1, DH))],
        out_specs=[_rows(CH, DNW), pl.BlockSpec((None, None, DH, DK, DK), lambda b, i: (b, i, 0, 0, 0))],
        scratch_shapes=[pltpu.VMEM((DH, DK, DK), F32)],
        compiler_params=_cparams(("parallel", "arbitrary")),
    )(cq, ba, a_log, dt_b)


def _dn_bwd(cq, ba, a_log, dt_b, states, do):
    B, S, _ = cq.shape
    nc = S // CH

    def body(cq_ref, ba_ref, al_ref, dt_ref, st_ref, do_ref, dcq_ref, dba_ref, dal_ref, ddt_ref, dstate):
        b, i = pl.program_id(0), pl.program_id(1)

        @pl.when(i == 0)
        def _():
            dstate[...] = jnp.zeros_like(dstate)

        @pl.when((b == 0) & (i == 0))
        def _():
            dal_ref[...] = jnp.zeros_like(dal_ref)
            ddt_ref[...] = jnp.zeros_like(ddt_ref)

        cqv, bav, dov = cq_ref[...], ba_ref[...], do_ref[...]
        lane8 = lax.broadcasted_iota(jnp.int32, (CH, 2 * DH), 1)
        lane4 = lax.broadcasted_iota(jnp.int32, (1, DH), 1)
        dqs, dks, dvs = [], [], []
        dba = jnp.zeros((CH, 2 * DH), F32)
        dal = jnp.zeros((1, DH), F32)
        ddt = jnp.zeros((1, DH), F32)
        for h in range(DH):
            _, vjp = jax.vjp(_dn_head, st_ref[h], *_dn_slices(cqv, bav, h), al_ref[:, h:h + 1], dt_ref[:, h:h + 1])
            dS, dq, dk, dv, da, db, dl, dd = vjp((dov[:, DK * h:DK * (h + 1)], dstate[h]))
            dstate[h] = dS
            dqs.append(dq)
            dks.append(dk)
            dvs.append(dv)
            dba = dba + jnp.where(lane8 == h, db, 0.0) + jnp.where(lane8 == DH + h, da, 0.0)
            dal = dal + jnp.where(lane4 == h, dl, 0.0)
            ddt = ddt + jnp.where(lane4 == h, dd, 0.0)
        dcq_ref[...] = jnp.concatenate(dqs + dks + dvs, axis=1)
        dba_ref[...] = dba.astype(BF16)
        dal_ref[...] += dal
        ddt_ref[...] += ddt

    rev = lambda b, i: (b, nc - 1 - i, 0)
    return pl.pallas_call(
        body, name="dn_bwd", grid=(B, nc),
        out_shape=[jax.ShapeDtypeStruct((B, S, CONVW), F32), jax.ShapeDtypeStruct((B, S, 2 * DH), BF16),
                   jax.ShapeDtypeStruct((1, DH), F32), jax.ShapeDtypeStruct((1, DH), F32)],
        in_specs=[pl.BlockSpec((None, CH, CONVW), rev), pl.BlockSpec((None, CH, 2 * DH), rev), _full((1, DH)),
                  _full((1, DH)), pl.BlockSpec((None, None, DH, DK, DK), lambda b, i: (b, nc - 1 - i, 0, 0, 0)),
                  pl.BlockSpec((None, CH, DNW), rev)],
        out_specs=[pl.BlockSpec((None, CH, CONVW), rev), pl.BlockSpec((None, CH, 2 * DH), rev),
                   _full((1, DH)), _full((1, DH))],
        scratch_shapes=[pltpu.VMEM((DH, DK, DK), F32)],
        compiler_params=_cparams(("arbitrary", "arbitrary")),
    )(cq, ba, a_log, dt_b, states, do)


def _gated_norm(o, z, g):
    outs = []
    for h in range(DH):
        t = o[:, DK * h:DK * (h + 1)]
        r = lax.rsqrt(jnp.mean(t * t, axis=-1, keepdims=True) + EPS)
        outs.append(t * r * g * _silu(z[:, DK * h:DK * (h + 1)]))
    return jnp.concatenate(outs, axis=1)


def _mix_fwd(x, o_attn, o_dn, z, ga, gd, mod, dn_g, w_branch, w_out):
    B, S, _ = x.shape
    tm = _tile(S)

    def body(x_ref, oa_ref, od_ref, z_ref, ga_ref, gd_ref, mod_ref, g_ref, wb_ref, wo_ref,
             x1_ref, mix_ref, mg_ref, oab_ref, odb_ref):
        oa = oa_ref[...].astype(BF16)
        od = _gated_norm(od_ref[...], z_ref[...], g_ref[...]).astype(BF16)
        oab_ref[...] = oa
        odb_ref[...] = od
        ya = jnp.dot(oa, wb_ref[0:QW, :], preferred_element_type=F32)
        yd = jnp.dot(od, wb_ref[QW:QW + DNW, :], preferred_element_type=F32)
        merged = (_sigmoid(ga_ref[...]) * ya + _sigmoid(gd_ref[...]) * yd).astype(BF16)
        mg_ref[...] = merged
        mix = jnp.dot(merged, wo_ref[...], preferred_element_type=F32)
        mix_ref[...] = mix
        x1_ref[...] = x_ref[...] + mod_ref[2:3, :] * mix

    return pl.pallas_call(
        body, name="mix_fwd", grid=(B, S // tm),
        out_shape=[jax.ShapeDtypeStruct((B, S, D), F32), jax.ShapeDtypeStruct((B, S, D), F32),
                   jax.ShapeDtypeStruct((B, S, D), BF16), jax.ShapeDtypeStruct((B, S, QW), BF16),
                   jax.ShapeDtypeStruct((B, S, DNW), BF16)],
        in_specs=[_rows(tm, D), _rows(tm, QW), _rows(tm, DNW), _rows(tm, DNW), _rows(tm, D), _rows(tm, D),
                  _perb(6, D), _full((1, DK)), _resident(w_branch.shape), _resident(w_out.shape)],
        out_specs=[_rows(tm, D), _rows(tm, D), _rows(tm, D), _rows(tm, QW), _rows(tm, DNW)],
        compiler_params=_cparams(("parallel", "arbitrary")),
    )(x, o_attn, o_dn, z, ga, gd, mod, dn_g, w_branch, w_out)


def _mix_bwd(dx1, mix, o_attn, o_dn, z, ga, gd, mod, dn_g, w_branch, w_out):
    B, S, _ = dx1.shape
    tm = _tile(S)

    def body(dx1_ref, mix_ref, oa_ref, od_ref, z_ref, ga_ref, gd_ref, mod_ref, g_ref, wb_ref, wo_ref,
             dmix_ref, dya_ref, dyd_ref, dga_ref, dgd_ref, dz_ref, doa_ref, dod_ref, dgate_ref, dg_ref):
        b, i = pl.program_id(0), pl.program_id(1)
        dx1 = dx1_ref[...]
        dmix = (dx1 * mod_ref[2:3, :]).astype(BF16)
        dmix_ref[...] = dmix
        dgate = jnp.sum(dx1 * mix_ref[...], axis=0, keepdims=True)
        dmerged = _dot_nt(dmix, wo_ref[...])
        odn, gn_vjp = jax.vjp(_gated_norm, od_ref[...], z_ref[...], g_ref[...])
        ya = _dot(oa_ref[...], wb_ref[0:QW, :])
        yd = _dot(odn, wb_ref[QW:QW + DNW, :])
        sa, sd = _sigmoid(ga_ref[...]), _sigmoid(gd_ref[...])
        dya = (dmerged * sa).astype(BF16)
        dyd = (dmerged * sd).astype(BF16)
        dya_ref[...] = dya
        dyd_ref[...] = dyd
        dga_ref[...] = (dmerged * ya * sa * (1.0 - sa)).astype(BF16)
        dgd_ref[...] = (dmerged * yd * sd * (1.0 - sd)).astype(BF16)
        doa_ref[...] = _dot_nt(dya, wb_ref[0:QW, :])
        dodn = _dot_nt(dyd, wb_ref[QW:QW + DNW, :])
        dod, dz, dg = gn_vjp(dodn)
        dod_ref[...] = dod
        dz_ref[...] = dz.astype(BF16)

        @pl.when(i == 0)
        def _():
            dgate_ref[...] = jnp.zeros_like(dgate_ref)

        @pl.when((b == 0) & (i == 0))
        def _():
            dg_ref[...] = jnp.zeros_like(dg_ref)

        dgate_ref[...] += dgate
        dg_ref[...] += dg

    return pl.pallas_call(
        body, name="mix_bwd", grid=(B, S // tm),
        out_shape=[jax.ShapeDtypeStruct((B, S, D), BF16), jax.ShapeDtypeStruct((B, S, D), BF16),
                   jax.ShapeDtypeStruct((B, S, D), BF16), jax.ShapeDtypeStruct((B, S, D), BF16),
                   jax.ShapeDtypeStruct((B, S, D), BF16), jax.ShapeDtypeStruct((B, S, DNW), BF16),
                   jax.ShapeDtypeStruct((B, S, QW), F32), jax.ShapeDtypeStruct((B, S, DNW), F32),
                   jax.ShapeDtypeStruct((B, 1, D), F32), jax.ShapeDtypeStruct((1, DK), F32)],
        in_specs=[_rows(tm, D), _rows(tm, D), _rows(tm, QW), _rows(tm, DNW), _rows(tm, DNW), _rows(tm, D),
                  _rows(tm, D), _perb(6, D), _full((1, DK)), _resident(w_branch.shape), _resident(w_out.shape)],
        out_specs=[_rows(tm, D), _rows(tm, D), _rows(tm, D), _rows(tm, D), _rows(tm, D), _rows(tm, DNW),
                   _rows(tm, QW), _rows(tm, DNW), _perb(1, D), _full((1, DK))],
        compiler_params=_cparams(("arbitrary", "arbitrary")),
    )(dx1, mix, o_attn, o_dn, z, ga, gd, mod, dn_g, w_branch, w_out)


def _ffn1_fwd(x1, mod, g2, w_gu):
    B, S, _ = x1.shape
    tm = _tile(S)

    def body(x_ref, mod_ref, g_ref, w_ref, h_ref, gate_ref, up_ref, act_ref):
        h = _rms_mod(x_ref[...], g_ref[...], mod_ref[4:5, :], mod_ref[3:4, :]).astype(BF16)
        h_ref[...] = h
        gate = jnp.dot(h, w_ref[:, 0:FFN], preferred_element_type=F32)
        up = jnp.dot(h, w_ref[:, FFN:2 * FFN], preferred_element_type=F32)
        gate_ref[...] = gate
        up_ref[...] = up
        act_ref[...] = (_silu(gate) * up).astype(BF16)

    return pl.pallas_call(
        body, name="ffn1_fwd", grid=(B, S // tm),
        out_shape=[jax.ShapeDtypeStruct((B, S, D), BF16), jax.ShapeDtypeStruct((B, S, FFN), F32),
                   jax.ShapeDtypeStruct((B, S, FFN), F32), jax.ShapeDtypeStruct((B, S, FFN), BF16)],
        in_specs=[_rows(tm, D), _perb(6, D), _full((1, D)), _resident(w_gu.shape)],
        out_specs=[_rows(tm, D), _rows(tm, FFN), _rows(tm, FFN), _rows(tm, FFN)],
        compiler_params=_cparams(("parallel", "arbitrary")),
    )(x1, mod, g2, w_gu)


def _ffn2_fwd(act, x1, target, mod, w_down):
    B, S, _ = x1.shape
    tm = _tile(S)

    def body(a_ref, x_ref, t_ref, mod_ref, w_ref, dy_ref, loss_ref, dgate_ref):
        b, i = pl.program_id(0), pl.program_id(1)
        y = jnp.dot(a_ref[...], w_ref[...], preferred_element_type=F32)
        err = x_ref[...] + mod_ref[5:6, :] * y - t_ref[...]
        dy = err * (1.0 / D)
        dy_ref[...] = dy

        @pl.when((b == 0) & (i == 0))
        def _():
            loss_ref[...] = jnp.zeros_like(loss_ref)

        @pl.when(i == 0)
        def _():
            dgate_ref[...] = jnp.zeros_like(dgate_ref)

        loss_ref[...] += (0.5 / D) * jnp.sum(err * err)
        dgate_ref[...] += jnp.sum(dy * y, axis=0, keepdims=True)

    return pl.pallas_call(
        body, name="ffn2_fwd", grid=(B, S // tm),
        out_shape=[jax.ShapeDtypeStruct((B, S, D), F32), jax.ShapeDtypeStruct((1, 128), F32),
                   jax.ShapeDtypeStruct((B, 1, D), F32)],
        in_specs=[_rows(tm, FFN), _rows(tm, D), _rows(tm, D), _perb(6, D), _resident(w_down.shape)],
        out_specs=[_rows(tm, D), _full((1, 128)), _perb(1, D)],
        compiler_params=_cparams(("arbitrary", "arbitrary")),
    )(act, x1, target, mod, w_down)


def _ffn2_bwd(dy, gate, up, mod, w_down):
    B, S, _ = dy.shape
    tm = _tile(S)

    def body(dy_ref, gate_ref, up_ref, mod_ref, w_ref, dgu_ref, dyg_ref):
        dyg = (dy_ref[...] * mod_ref[5:6, :]).astype(BF16)
        dyg_ref[...] = dyg
        dact = _dot_nt(dyg, w_ref[...])
        gate, up = gate_ref[...], up_ref[...]
        sg = _sigmoid(gate)
        dgu_ref[:, 0:FFN] = (dact * up * (sg * (1.0 + gate * (1.0 - sg)))).astype(BF16)
        dgu_ref[:, FFN:2 * FFN] = (dact * (gate * sg)).astype(BF16)

    return pl.pallas_call(
        body, name="ffn2_bwd", grid=(B, S // tm),
        out_shape=[jax.ShapeDtypeStruct((B, S, 2 * FFN), BF16), jax.ShapeDtypeStruct((B, S, D), BF16)],
        in_specs=[_rows(tm, D), _rows(tm, FFN), _rows(tm, FFN), _perb(6, D), _resident(w_down.shape)],
        out_specs=[_rows(tm, 2 * FFN), _rows(tm, D)],
        compiler_params=_cparams(("parallel", "arbitrary")),
    )(dy, gate, up, mod, w_down)


def _ffn1_bwd(dgu, x1, dy, mod, g2, w_gu):
    B, S, _ = x1.shape
    tm = _tile(S)

    def body(dgu_ref, x_ref, dy_ref, mod_ref, g_ref, w_ref, dx1_ref, dg_ref, dsc_ref, dsh_ref):
        b, i = pl.program_id(0), pl.program_id(1)
        dh = _dot_nt(dgu_ref[...], w_ref[...])
        _, vjp = jax.vjp(_rms_mod, x_ref[...], g_ref[...], mod_ref[4:5, :], mod_ref[3:4, :])
        dx, dg, dsc, dsh = vjp(dh)
        dx1_ref[...] = dy_ref[...] + dx

        @pl.when((b == 0) & (i == 0))
        def _():
            dg_ref[...] = jnp.zeros_like(dg_ref)

        @pl.when(i == 0)
        def _():
            dsc_ref[...] = jnp.zeros_like(dsc_ref)
            dsh_ref[...] = jnp.zeros_like(dsh_ref)

        dg_ref[...] += dg
        dsc_ref[...] += dsc
        dsh_ref[...] += dsh

    return pl.pallas_call(
        body, name="ffn1_bwd", grid=(B, S // tm),
        out_shape=[jax.ShapeDtypeStruct((B, S, D), F32), jax.ShapeDtypeStruct((1, D), F32),
                   jax.ShapeDtypeStruct((B, 1, D), F32), jax.ShapeDtypeStruct((B, 1, D), F32)],
        in_specs=[_rows(tm, 2 * FFN), _rows(tm, D), _rows(tm, D), _perb(6, D), _full((1, D)), _resident(w_gu.shape)],
        out_specs=[_rows(tm, D), _full((1, D)), _perb(1, D), _perb(1, D)],
        compiler_params=_cparams(("arbitrary", "arbitrary")),
    )(dgu, x1, dy, mod, g2, w_gu)


def _adamw(w, g, m, v, name):
    def body(w_ref, g_ref, m_ref, v_ref, d_ref, nm_ref, nv_ref):
        g = g_ref[...]
        m = B1 * m_ref[...] + (1.0 - B1) * g
        v = B2 * v_ref[...] + (1.0 - B2) * (g * g)
        nm_ref[...] = m
        nv_ref[...] = v
        m_hat = m / (1.0 - B1 ** STEP)
        v_hat = v / (1.0 - B2 ** STEP)
        d_ref[...] = -LR * (m_hat / (jnp.sqrt(v_hat) + AEPS) + WD * w_ref[...])

    sd = jax.ShapeDtypeStruct(w.shape, F32)
    return pl.pallas_call(body, name=name, out_shape=(sd, sd, sd), compiler_params=_cparams())(w, g, m, v)


SHARD_ROWS = (609, 128, 128, 704, 352)
BF16_TILE_ROWS = 16
PART_ROWS = tuple(-(-r // BF16_TILE_ROWS) * BF16_TILE_ROWS for r in SHARD_ROWS)
PACK_ROWS = sum(PART_ROWS)


def _pad_rows(a, rows, axis):
    widths = [(0, 0)] * a.ndim
    widths[axis] = (0, rows - a.shape[axis])
    return jnp.pad(a, widths) if rows != a.shape[axis] else a


def _pack_cols(w, n):
    return w.reshape(w.shape[0], N_DEV, n).transpose(1, 0, 2).reshape(N_DEV, (w.shape[0] * n) // D, D)


def _unpack_cols(p, rows_in, n):
    return p.reshape(N_DEV, rows_in, n).transpose(1, 0, 2).reshape(rows_in, N_DEV * n)


def kernel(x, c, positions, ada_w, ada_b, norm1_g, w_in, conv_w, q_norm_g, k_norm_g, sinks, a_log, dt_bias, dn_norm_g, w_branch, w_out, norm2_g, w_gate_up, w_down, loss_target, m_ada_w, m_ada_b, m_norm1_g, m_w_in, m_conv_w, m_q_norm_g, m_k_norm_g, m_sinks, m_a_log, m_dt_bias, m_dn_norm_g, m_w_branch, m_w_out, m_norm2_g, m_w_gate_up, m_w_down, v_ada_w, v_ada_b, v_norm1_g, v_w_in, v_conv_w, v_q_norm_g, v_k_norm_g, v_sinks, v_a_log, v_dt_bias, v_dn_norm_g, v_w_branch, v_w_out, v_norm2_g, v_w_gate_up, v_w_down):
    B, S, _ = x.shape
    me = 4 * lax.axis_index("x") + 2 * lax.axis_index("y") + lax.axis_index("c")
    n_in, n_gu, n_dn = IN_W // N_DEV, 2 * FFN // N_DEV, FFN // N_DEV

    flat = [w_in[0].reshape(-1, D), w_branch[0], w_out[0], w_gate_up[0].reshape(-1, D), w_down[0]]
    packed = jnp.concatenate([_pad_rows(f.astype(BF16), r, 0) for f, r in zip(flat, PART_ROWS)], axis=0)
    gathered = _all_gather_big(packed, "gather_weights")
    offs = np.cumsum((0,) + PART_ROWS)
    part = [gathered[:, offs[i]:offs[i] + SHARD_ROWS[i], :] for i in range(5)]
    w_in_f = _unpack_cols(part[0], D, n_in)
    w_branch_f = part[1].reshape(D, D)
    w_out_f = part[2].reshape(D, D)
    w_gu_f = _unpack_cols(part[3], D, n_gu)
    w_down_f = part[4].reshape(FFN, D)
    cuts = [(0, QW), (QW, QW + 2 * KVW), (QW + 2 * KVW, QW + 2 * KVW + CONVW)]
    o = QW + 2 * KVW + CONVW
    cuts += [(o, o + 2 * DH), (o + 2 * DH, o + 2 * DH + DNW), (o + 2 * DH + DNW, o + 2 * DH + DNW + D),
             (o + 2 * DH + DNW + D, IN_W)]
    ws_in = [w_in_f[:, a:b] for a, b in cuts]

    c_all = _all_gather_small(c, "gather_c").reshape(N_DEV * B, D)
    ncol = 6 * D // N_DEV
    mod_cols, cond_all = _ada_fwd(c_all, ada_w[0], lax.dynamic_slice(ada_b, (0, me * ncol), (1, ncol)))
    mod_all = _all_gather_small(mod_cols, "gather_mod").transpose(1, 0, 2).reshape(N_DEV * B, 6 * D)
    mod = lax.dynamic_slice(mod_all, (me * B, 0), (B, 6 * D)).reshape(B, 6, D)

    h1, aq, akv, dnx, ba, z, ga, gd = _inproj_fwd(x, mod, norm1_g, ws_in)
    invf, perm = _rope_consts()
    pos3 = positions.reshape(B, S, 1)
    o_attn = _attn_fwd(aq, akv, pos3, q_norm_g, k_norm_g, sinks, invf, perm)
    conv2 = conv_w.reshape(CONV, CONVW // N_DEV)
    conv_all = _all_gather_small(conv2, "gather_conv").transpose(1, 0, 2).reshape(CONV, CONVW)
    cq = _conv_fwd(dnx, conv_all)
    o_dn, states = _dn_fwd(cq, ba, a_log, dt_bias)
    x1, mix, merged, oa_b, od_b = _mix_fwd(x, o_attn, o_dn, z, ga, gd, mod, dn_norm_g, w_branch_f, w_out_f)
    h2, gate, up, act = _ffn1_fwd(x1, mod, norm2_g, w_gu_f)
    dy, loss_part, d_gate2 = _ffn2_fwd(act, x1, loss_target, mod, w_down_f)
    loss = lax.psum(loss_part[0, 0], ("x", "y", "c"))

    dgu, dyg = _ffn2_bwd(dy, gate, up, mod, w_down_f)
    g_w_down = _wgrad(act, dyg, "wgrad_down")
    dx1, d_n2g, d_scale2, d_shift2 = _ffn1_bwd(dgu, x1, dy, mod, norm2_g, w_gu_f)
    g_w_gu = _wgrad(h2, dgu, "wgrad_gate_up")
    dmix, dya, dyd, dga, dgd, dz, d_oa, d_od, d_gate1, d_dng = _mix_bwd(
        dx1, mix, o_attn, o_dn, z, ga, gd, mod, dn_norm_g, w_branch_f, w_out_f)
    g_w_out = _wgrad(merged, dmix, "wgrad_out")
    g_w_branch = jnp.concatenate([_wgrad(oa_b, dya, "wgrad_branch_a"), _wgrad(od_b, dyd, "wgrad_branch_d")], axis=0)
    dcq, dba, d_alog, d_dtb = _dn_bwd(cq, ba, a_log, dt_bias, states, d_od)
    ddnx, d_conv = _conv_bwd(dnx, conv_all, dcq)
    daq, dakv, d_qg, d_kg, d_sinks = _attn_bwd(aq, akv, pos3, q_norm_g, k_norm_g, sinks, invf, perm, d_oa)
    dps = [daq, dakv, ddnx, dba, dz, dga, dgd]
    grad_x, d_n1g, d_scale1, d_shift1 = _inproj_bwd(x, mod, norm1_g, dx1, dps, ws_in)
    names = ["q", "kv", "dn", "ba", "z", "ga", "gd"]
    g_w_in = jnp.concatenate([_wgrad(h1, dp, "wgrad_in_" + nm) for dp, nm in zip(dps, names)], axis=1)

    gparts = [_pack_cols(g_w_in, n_in), g_w_branch.reshape(N_DEV, D // N_DEV, D), g_w_out.reshape(N_DEV, D // N_DEV, D),
              _pack_cols(g_w_gu, n_gu), g_w_down.reshape(N_DEV, n_dn, D)]
    gpack = jnp.concatenate([_pad_rows(p.astype(BF16), r, 1) for p, r in zip(gparts, PART_ROWS)], axis=1)
    gsum = _sum_blocks(_exchange_blocks(gpack, "exchange_grads"), "sum_grads")
    gs = [gsum[offs[i]:offs[i] + SHARD_ROWS[i], :] for i in range(5)]
    grad_w_in = gs[0].reshape(1, D, n_in)
    grad_w_branch = gs[1].reshape(1, D // N_DEV, D)
    grad_w_out = gs[2].reshape(1, D // N_DEV, D)
    grad_w_gu = gs[3].reshape(1, D, n_gu)
    grad_w_down = gs[4].reshape(1, n_dn, D)

    dmod = jnp.concatenate([d_shift1, d_scale1, d_gate1, d_shift2, d_scale2, d_gate2], axis=2).reshape(B, 6 * D)
    small = jnp.concatenate([d_n1g, d_qg, d_kg, d_sinks, d_alog, d_dtb, d_dng, d_n2g, d_conv.reshape(1, CONV * CONVW)], axis=1)
    nsm = small.shape[1]
    width = -(-max(6 * D, nsm) // 128) * 128
    rows = jnp.concatenate([jnp.pad(dmod, ((0, 0), (0, width - 6 * D))), jnp.pad(small, ((0, 8 - B - 1), (0, width - nsm)))], axis=0)
    rows_all = _all_gather_small(rows, "gather_small")
    dmod_all = rows_all[:, 0:B, 0:6 * D].reshape(N_DEV * B, 6 * D)
    dmod_cols = lax.dynamic_slice(dmod_all, (0, me * ncol), (N_DEV * B, ncol))
    grad_ada_w, grad_ada_b, small_sum = _ada_bwd(cond_all, dmod_all, dmod_cols, rows_all[:, B, :])
    sizes = [D, HD, HD, HQ, DH, DH, DK, D]
    so = np.cumsum([0] + sizes)
    g_n1, g_qg, g_kg, g_sk, g_al, g_dt, g_dn, g_n2 = [small_sum[:, so[i]:so[i + 1]] for i in range(8)]
    g_conv_all = small_sum[:, so[8]:so[8] + CONV * CONVW].reshape(CONV, N_DEV, CONVW // N_DEV)
    grad_conv = lax.dynamic_slice(g_conv_all, (0, me, 0), (CONV, 1, CONVW // N_DEV)).reshape(CONV, CONVW // N_DEV)

    big = [(ada_w, grad_ada_w.reshape(ada_w.shape), m_ada_w, v_ada_w), (w_in, grad_w_in, m_w_in, v_w_in),
           (w_branch, grad_w_branch, m_w_branch, v_w_branch), (w_out, grad_w_out, m_w_out, v_w_out),
           (w_gate_up, grad_w_gu, m_w_gate_up, v_w_gate_up), (w_down, grad_w_down, m_w_down, v_w_down)]
    upd = {}
    for nm, (w, g, m, v) in zip(["ada_w", "w_in", "w_branch", "w_out", "w_gate_up", "w_down"], big):
        s2 = w.shape[1:]
        res = _adamw(w.reshape(s2), g.reshape(s2), m.reshape(s2), v.reshape(s2), "adamw_" + nm)
        upd[nm] = tuple(r.reshape(w.shape) for r in res)
    small_names = ["ada_b", "norm1_g", "q_norm_g", "k_norm_g", "sinks", "a_log", "dt_bias", "dn_norm_g", "norm2_g", "conv_w"]
    small_w = [ada_b, norm1_g, q_norm_g, k_norm_g, sinks, a_log, dt_bias, dn_norm_g, norm2_g, conv_w]
    small_g = [grad_ada_b, g_n1, g_qg, g_kg, g_sk, g_al, g_dt, g_dn, g_n2, grad_conv]
    small_m = [m_ada_b, m_norm1_g, m_q_norm_g, m_k_norm_g, m_sinks, m_a_log, m_dt_bias, m_dn_norm_g, m_norm2_g, m_conv_w]
    small_v = [v_ada_b, v_norm1_g, v_q_norm_g, v_k_norm_g, v_sinks, v_a_log, v_dt_bias, v_dn_norm_g, v_norm2_g, v_conv_w]
    cat = lambda arrs: jnp.concatenate([a.reshape(1, -1) for a in arrs], axis=1)
    res = _adamw(cat(small_w), cat(small_g), cat(small_m), cat(small_v), "adamw_small")
    po = np.cumsum([0] + [int(np.prod(w.shape)) for w in small_w])
    grads = {}
    for i, nm in enumerate(small_names):
        upd[nm] = tuple(r[:, po[i]:po[i + 1]].reshape(small_w[i].shape) for r in res)
        grads[nm] = small_g[i].reshape(small_w[i].shape)
    grads.update(ada_w=grad_ada_w.reshape(ada_w.shape), w_in=grad_w_in, w_branch=grad_w_branch, w_out=grad_w_out,
                 w_gate_up=grad_w_gu, w_down=grad_w_down)

    order = ["ada_w", "ada_b", "norm1_g", "w_in", "conv_w", "q_norm_g", "k_norm_g", "sinks", "a_log", "dt_bias",
             "dn_norm_g", "w_branch", "w_out", "norm2_g", "w_gate_up", "w_down"]
    return (loss, grad_x, *[grads[n] for n in order], *[upd[n][0] for n in order],
            *[upd[n][1] for n in order], *[upd[n][2] for n in order])
```

```python
import functools

import numpy as np
import jax
import jax.numpy as jnp
from jax import lax
from jax.experimental import pallas as pl
from jax.experimental.pallas import tpu as pltpu

F32 = jnp.float32
BF16 = jnp.bfloat16
HI = lax.Precision.HIGHEST

N_DEV = 8
D = 1024
HQ, HKV, HD = 8, 2, 64
GRP = HQ // HKV
BLK = 128
ROT = HD // 4
THETA = 500000.0
QW, KVW = HQ * HD, HKV * HD
DH, DK = 4, 128
CH = 64
DNW = DH * DK
CONV = 4
CONVW = 3 * DNW
FFN = 2816
EPS = 1e-6
IN_W = QW + 2 * KVW + CONVW + 2 * DH + DNW + 2 * D

LR, B1, B2, AEPS, WD, STEP = 0.001, 0.9, 0.999, 1e-08, 0.01, 10

VMEM_LIMIT = 56 * 1024 * 1024
MESH = pl.DeviceIdType.MESH


def _cparams(sem=None, vmem=VMEM_LIMIT):
    return pltpu.CompilerParams(dimension_semantics=sem, vmem_limit_bytes=vmem)


def _full(shape):
    n = len(shape)
    return pl.BlockSpec(shape, lambda *_: (0,) * n)


def _resident(shape):
    n = len(shape)
    return pl.BlockSpec(shape, lambda *_: (0,) * n, pipeline_mode=pl.Buffered(1))


def _rows(tm, w):
    return pl.BlockSpec((None, tm, w), lambda b, i: (b, i, 0))


def _perb(r, w):
    return pl.BlockSpec((None, r, w), lambda b, i: (b, 0, 0))


def _dot(a, b):
    return jnp.dot(a.astype(BF16), b.astype(BF16), preferred_element_type=F32)


def _dot_nt(a, b):
    return lax.dot_general(a.astype(BF16), b.astype(BF16), (((1,), (1,)), ((), ())), preferred_element_type=F32)


def _dot_tn(a, b):
    return lax.dot_general(a.astype(BF16), b.astype(BF16), (((0,), (0,)), ((), ())), preferred_element_type=F32)


def _dot_hi(a, b):
    return jnp.dot(a, b, preferred_element_type=F32, precision=HI)


def _sigmoid(x):
    return jax.nn.sigmoid(x)


def _silu(x):
    return x * jax.nn.sigmoid(x)


def _rms_mod(x, g, scale, shift):
    r = lax.rsqrt(jnp.mean(x * x, axis=-1, keepdims=True) + EPS)
    return (x * r * g) * (1.0 + scale) + shift


def _tile(S):
    return min(256, S)


def _peer(x, y, c, k):
    px = 1 - x if (k >> 2) & 1 else x
    py = 1 - y if (k >> 1) & 1 else y
    pc = 1 - c if k & 1 else c
    return px, py, pc


def _all_gather_small(v, name):
    r, n = v.shape

    def body(v_ref, out_ref, send_sems, recv_sems, local_sem):
        x, y, c = lax.axis_index("x"), lax.axis_index("y"), lax.axis_index("c")
        me = 4 * x + 2 * y + c
        mine = pltpu.make_async_copy(v_ref, out_ref.at[me], local_sem)
        mine.start()
        sends = []
        for k in range(1, N_DEV):
            cp = pltpu.make_async_remote_copy(
                src_ref=v_ref, dst_ref=out_ref.at[me], send_sem=send_sems.at[k - 1], recv_sem=recv_sems.at[k - 1],
                device_id=_peer(x, y, c, k), device_id_type=MESH)
            cp.start()
            sends.append(cp)
        for k in range(1, N_DEV):
            px, py, pc = _peer(x, y, c, k)
            pltpu.make_async_remote_copy(
                src_ref=v_ref, dst_ref=out_ref.at[4 * px + 2 * py + pc], send_sem=send_sems.at[k - 1],
                recv_sem=recv_sems.at[k - 1], device_id=(px, py, pc), device_id_type=MESH).wait_recv()
        for cp in sends:
            cp.wait_send()
        mine.wait()

    return pl.pallas_call(
        body, name=name,
        out_shape=jax.ShapeDtypeStruct((N_DEV, r, n), v.dtype),
        in_specs=[pl.BlockSpec(memory_space=pltpu.VMEM)],
        out_specs=pl.BlockSpec(memory_space=pltpu.VMEM),
        scratch_shapes=[pltpu.SemaphoreType.DMA((N_DEV - 1,)), pltpu.SemaphoreType.DMA((N_DEV - 1,)), pltpu.SemaphoreType.DMA],
    )(v)


def _all_gather_big(v, name):
    r, n = v.shape

    def body(v_ref, out_ref, send_sems, recv_sems, local_sem):
        x, y, c = lax.axis_index("x"), lax.axis_index("y"), lax.axis_index("c")
        me, sibling = (x, y, c), (x, y, 1 - c)
        chips = [(1 - x, y), (x, 1 - y), (1 - x, 1 - y)]

        def rows(px, py, pc):
            return out_ref.at[4 * px + 2 * py + pc]

        def copy(k, block, to, src=None):
            return pltpu.make_async_remote_copy(
                src_ref=rows(*block) if src is None else src, dst_ref=rows(*block),
                send_sem=send_sems.at[k], recv_sem=recv_sems.at[k], device_id=to, device_id_type=MESH)

        mine = pltpu.make_async_copy(v_ref, rows(*me), local_sem)
        mine.start()
        first = [copy(0, me, sibling, src=v_ref)]
        first += [copy(1 + j, me, (*chip, c), src=v_ref) for j, chip in enumerate(chips)]
        for cp in first:
            cp.start()
        passed = [copy(4 + j, (*chip, c), sibling) for j, chip in enumerate(chips)]
        for j, chip in enumerate(chips):
            copy(1 + j, (*chip, c), me).wait_recv()
            passed[j].start()
        copy(0, sibling, me).wait_recv()
        for j, chip in enumerate(chips):
            copy(4 + j, (*chip, 1 - c), me).wait_recv()
        for cp in first + passed:
            cp.wait_send()
        mine.wait()

    return pl.pallas_call(
        body, name=name,
        out_shape=jax.ShapeDtypeStruct((N_DEV, r, n), v.dtype),
        in_specs=[pl.BlockSpec(memory_space=pl.ANY)],
        out_specs=pl.BlockSpec(memory_space=pl.ANY),
        scratch_shapes=[pltpu.SemaphoreType.DMA((7,)), pltpu.SemaphoreType.DMA((7,)), pltpu.SemaphoreType.DMA],
    )(v)


def _exchange_blocks(g, name):
    _, r, n = g.shape

    def body(g_ref, out_ref, send_sems, recv_sems, local_sem):
        x, y, c = lax.axis_index("x"), lax.axis_index("y"), lax.axis_index("c")
        me = 4 * x + 2 * y + c
        mine = pltpu.make_async_copy(g_ref.at[me], out_ref.at[me], local_sem)
        mine.start()
        sends = []
        for k in range(1, N_DEV):
            px, py, pc = _peer(x, y, c, k)
            cp = pltpu.make_async_remote_copy(
                src_ref=g_ref.at[4 * px + 2 * py + pc], dst_ref=out_ref.at[me], send_sem=send_sems.at[k - 1],
                recv_sem=recv_sems.at[k - 1], device_id=(px, py, pc), device_id_type=MESH)
            cp.start()
            sends.append(cp)
        for k in range(1, N_DEV):
            px, py, pc = _peer(x, y, c, k)
            pltpu.make_async_remote_copy(
                src_ref=g_ref.at[me], dst_ref=out_ref.at[4 * px + 2 * py + pc], send_sem=send_sems.at[k - 1],
                recv_sem=recv_sems.at[k - 1], device_id=(px, py, pc), device_id_type=MESH).wait_recv()
        for cp in sends:
            cp.wait_send()
        mine.wait()

    return pl.pallas_call(
        body, name=name,
        out_shape=jax.ShapeDtypeStruct(g.shape, g.dtype),
        in_specs=[pl.BlockSpec(memory_space=pl.ANY)],
        out_specs=pl.BlockSpec(memory_space=pl.ANY),
        scratch_shapes=[pltpu.SemaphoreType.DMA((N_DEV - 1,)), pltpu.SemaphoreType.DMA((N_DEV - 1,)), pltpu.SemaphoreType.DMA],
    )(g)


def _sum_blocks(g, name):
    _, r, n = g.shape
    tr = 176 if r % 176 == 0 else r

    def body(g_ref, o_ref):
        acc = g_ref[0].astype(F32)
        for d in range(1, N_DEV):
            acc = acc + g_ref[d].astype(F32)
        o_ref[...] = acc

    return pl.pallas_call(
        body, name=name, grid=(r // tr,),
        out_shape=jax.ShapeDtypeStruct((r, n), F32),
        in_specs=[pl.BlockSpec((N_DEV, tr, n), lambda i: (0, i, 0))],
        out_specs=pl.BlockSpec((tr, n), lambda i: (i, 0)),
        compiler_params=_cparams(("arbitrary",)),
    )(g)


def _ada_fwd(c_all, ada_w, ada_b_cols):
    nb, ncol = c_all.shape[0], ada_w.shape[1]

    def body(c_ref, w_ref, b_ref, mod_ref, cond_ref):
        cond = _silu(c_ref[...])
        cond_ref[...] = cond
        mod_ref[...] = _dot_hi(cond, w_ref[...]) + b_ref[...]

    return pl.pallas_call(
        body, name="ada_fwd",
        out_shape=(jax.ShapeDtypeStruct((nb, ncol), F32), jax.ShapeDtypeStruct((nb, D), F32)),
        compiler_params=_cparams(),
    )(c_all, ada_w, ada_b_cols)


def _ada_bwd(cond_all, dmod_all, dmod_cols, smalls):
    ncol, nsm = dmod_cols.shape[1], smalls.shape[1]

    def body(cond_ref, dm_ref, dmc_ref, sm_ref, gw_ref, gb_ref, gs_ref):
        gw_ref[...] = lax.dot_general(cond_ref[...], dmc_ref[...], (((0,), (0,)), ((), ())),
                                      preferred_element_type=F32, precision=HI)
        gb_ref[...] = jnp.sum(dm_ref[...], axis=0, keepdims=True)
        gs_ref[...] = jnp.sum(sm_ref[...], axis=0, keepdims=True)

    return pl.pallas_call(
        body, name="ada_bwd",
        out_shape=(jax.ShapeDtypeStruct((D, ncol), F32), jax.ShapeDtypeStruct((1, 6 * D), F32),
                   jax.ShapeDtypeStruct((1, nsm), F32)),
        compiler_params=_cparams(),
    )(cond_all, dmod_all, dmod_cols, smalls)


def _inproj_fwd(x, mod, g1, ws):
    B, S, _ = x.shape
    tm = _tile(S)
    widths = [w.shape[1] for w in ws]

    def body(x_ref, mod_ref, g_ref, *refs):
        w_refs, h_ref, o_refs = refs[:len(ws)], refs[len(ws)], refs[len(ws) + 1:]
        h = _rms_mod(x_ref[...], g_ref[...], mod_ref[1:2, :], mod_ref[0:1, :]).astype(BF16)
        h_ref[...] = h
        for w_ref, o_ref in zip(w_refs, o_refs):
            o_ref[...] = jnp.dot(h, w_ref[...], preferred_element_type=F32)

    return pl.pallas_call(
        body, name="inproj_fwd", grid=(B, S // tm),
        out_shape=[jax.ShapeDtypeStruct((B, S, D), BF16)] + [jax.ShapeDtypeStruct((B, S, w), F32) for w in widths],
        in_specs=[_rows(tm, D), _perb(6, D), _full((1, D))] + [_resident(w.shape) for w in ws],
        out_specs=[_rows(tm, D)] + [_rows(tm, w) for w in widths],
        compiler_params=_cparams(("parallel", "arbitrary")),
    )(x, mod, g1, *ws)


def _inproj_bwd(x, mod, g1, dx1, dps, ws):
    B, S, _ = x.shape
    tm = _tile(S)
    n = len(ws)

    def body(x_ref, mod_ref, g_ref, dx1_ref, *refs):
        dp_refs, w_refs = refs[:n], refs[n:2 * n]
        gx_ref, dg_ref, dsc_ref, dsh_ref = refs[2 * n:]
        b, i = pl.program_id(0), pl.program_id(1)
        dh = _dot_nt(dp_refs[0][...], w_refs[0][...])
        for dp_ref, w_ref in zip(dp_refs[1:], w_refs[1:]):
            dh = dh + _dot_nt(dp_ref[...], w_ref[...])
        _, vjp = jax.vjp(_rms_mod, x_ref[...], g_ref[...], mod_ref[1:2, :], mod_ref[0:1, :])
        dx, dg, dsc, dsh = vjp(dh)
        gx_ref[...] = dx1_ref[...] + dx

        @pl.when((b == 0) & (i == 0))
        def _():
            dg_ref[...] = jnp.zeros_like(dg_ref)

        @pl.when(i == 0)
        def _():
            dsc_ref[...] = jnp.zeros_like(dsc_ref)
            dsh_ref[...] = jnp.zeros_like(dsh_ref)

        dg_ref[...] += dg
        dsc_ref[...] += dsc
        dsh_ref[...] += dsh

    return pl.pallas_call(
        body, name="inproj_bwd", grid=(B, S // tm),
        out_shape=[jax.ShapeDtypeStruct((B, S, D), F32), jax.ShapeDtypeStruct((1, D), F32),
                   jax.ShapeDtypeStruct((B, 1, D), F32), jax.ShapeDtypeStruct((B, 1, D), F32)],
        in_specs=[_rows(tm, D), _perb(6, D), _full((1, D)), _rows(tm, D)]
                 + [_rows(tm, w.shape[1]) for w in ws] + [_resident(w.shape) for w in ws],
        out_specs=[_rows(tm, D), _full((1, D)), _perb(1, D), _perb(1, D)],
        compiler_params=_cparams(("arbitrary", "arbitrary")),
    )(x, mod, g1, dx1, *dps, *ws)


def _wgrad(a, b, name):
    B, S, K = a.shape
    N = b.shape[2]
    tm = min(512, S)
    tn = 512 if N % 512 == 0 else N
    nt = S // tm

    def body(a_ref, b_ref, o_ref):
        t = pl.program_id(1)

        @pl.when(t == 0)
        def _():
            o_ref[...] = jnp.zeros_like(o_ref)

        o_ref[...] += lax.dot_general(a_ref[...], b_ref[...], (((0,), (0,)), ((), ())), preferred_element_type=F32)

    return pl.pallas_call(
        body, name=name, grid=(N // tn, B * nt),
        out_shape=jax.ShapeDtypeStruct((K, N), F32),
        in_specs=[pl.BlockSpec((None, tm, K), lambda j, t: (t // nt, t % nt, 0)),
                  pl.BlockSpec((None, tm, tn), lambda j, t: (t // nt, t % nt, j))],
        out_specs=pl.BlockSpec((K, tn), lambda j, t: (0, j)),
        compiler_params=_cparams(("parallel", "arbitrary")),
    )(a, b)


def _rope_consts():
    inv_freq = THETA ** (-jnp.arange(0, ROT, 2, dtype=F32) / ROT)
    invf = jnp.concatenate([inv_freq, inv_freq, jnp.zeros((HD - ROT,), F32)])[None, :]
    perm = np.zeros((HD, HD), np.float32)
    half = ROT // 2
    for d in range(half):
        perm[d + half, d] = -1.0
        perm[d, d + half] = 1.0
    return invf, jnp.asarray(perm)


def _head_norm_rope(t, g, cos, sin, perm):
    r = lax.rsqrt(jnp.mean(t * t, axis=-1, keepdims=True) + EPS)
    t = t * r * g
    return t * cos + _dot_hi(t, perm) * sin


def _attn_block(q, kvp, kvc, qg, kg, sinks, cq, sq, ck, sk, perm, valid):
    outs = []
    for j in range(HKV):
        kraw = jnp.concatenate([kvp[:, HD * j:HD * (j + 1)], kvc[:, HD * j:HD * (j + 1)]], axis=0)
        v = jnp.concatenate([kvp[:, KVW + HD * j:KVW + HD * (j + 1)], kvc[:, KVW + HD * j:KVW + HD * (j + 1)]], axis=0)
        k = _head_norm_rope(kraw, kg, ck, sk, perm)
        qs, sk_rows = [], []
        for i in range(GRP):
            h = GRP * j + i
            qs.append(_head_norm_rope(q[:, HD * h:HD * (h + 1)], qg, cq, sq, perm))
            sk_rows.append(jnp.broadcast_to(sinks[:, h:h + 1], (BLK, 1)))
        q4 = jnp.concatenate(qs, axis=0)
        sink = jnp.concatenate(sk_rows, axis=0)
        s = _dot_nt(q4, k) * (HD ** -0.5)
        s = jnp.where(valid, s, -1e30)
        m = jnp.maximum(jnp.max(s, axis=-1, keepdims=True), sink)
        p = jnp.exp(s - m)
        probs = p / (jnp.sum(p, axis=-1, keepdims=True) + jnp.exp(sink - m))
        o4 = _dot(probs, v)
        outs += [o4[BLK * i:BLK * (i + 1), :] for i in range(GRP)]
    return jnp.concatenate(outs, axis=1)


def _attn_tables(posp_ref, posc_ref, invf_ref, n):
    posq = posc_ref[...].astype(F32)
    posk = jnp.concatenate([posp_ref[...].astype(F32), posq], axis=0)
    aq, ak = posq * invf_ref[...], posk * invf_ref[...]
    qi = lax.broadcasted_iota(jnp.int32, (GRP * BLK, 2 * BLK), 0) % BLK + BLK
    kj = lax.broadcasted_iota(jnp.int32, (GRP * BLK, 2 * BLK), 1)
    dist = qi - kj
    valid = (dist >= 0) & (dist < BLK) & ((kj >= BLK) | (n > 0))
    return jnp.cos(aq), jnp.sin(aq), jnp.cos(ak), jnp.sin(ak), valid


def _attn_fwd(aq, akv, pos, qg, kg, sinks, invf, perm):
    B, S, _ = aq.shape
    nb = S // BLK

    def body(q_ref, kvp_ref, kvc_ref, posp_ref, posc_ref, qg_ref, kg_ref, sk_ref, invf_ref, perm_ref, o_ref):
        n = pl.program_id(1)
        cq, sq, ck, sk, valid = _attn_tables(posp_ref, posc_ref, invf_ref, n)
        o_ref[...] = _attn_block(q_ref[...], kvp_ref[...], kvc_ref[...], qg_ref[...], kg_ref[...], sk_ref[...],
                                 cq, sq, ck, sk, perm_ref[...], valid)

    prev = lambda b, n: (b, jnp.maximum(n - 1, 0), 0)
    cur = lambda b, n: (b, n, 0)
    return pl.pallas_call(
        body, name="attn_fwd", grid=(B, nb),
        out_shape=jax.ShapeDtypeStruct((B, S, QW), F32),
        in_specs=[pl.BlockSpec((None, BLK, QW), cur), pl.BlockSpec((None, BLK, 2 * KVW), prev),
                  pl.BlockSpec((None, BLK, 2 * KVW), cur), pl.BlockSpec((None, BLK, 1), prev),
                  pl.BlockSpec((None, BLK, 1), cur), _full((1, HD)), _full((1, HD)), _full((1, HQ)),
                  _full((1, HD)), _full((HD, HD))],
        out_specs=pl.BlockSpec((None, BLK, QW), cur),
        compiler_params=_cparams(("parallel", "arbitrary")),
    )(aq, akv, akv, pos, pos, qg, kg, sinks, invf, perm)


def _attn_bwd(aq, akv, pos, qg, kg, sinks, invf, perm, do):
    B, S, _ = aq.shape
    nb = S // BLK

    def body(q_ref, kvp_ref, kvc_ref, posp_ref, posc_ref, qg_ref, kg_ref, sk_ref, invf_ref, perm_ref, do_ref,
             dq_ref, dkv_ref, dqg_ref, dkg_ref, dsk_ref, carry):
        b, i = pl.program_id(0), pl.program_id(1)
        n = nb - 1 - i
        cq, sq, ck, sk, valid = _attn_tables(posp_ref, posc_ref, invf_ref, n)
        fn = functools.partial(_attn_block, cq=cq, sq=sq, ck=ck, sk=sk, perm=perm_ref[...], valid=valid)
        _, vjp = jax.vjp(fn, q_ref[...], kvp_ref[...], kvc_ref[...], qg_ref[...], kg_ref[...], sk_ref[...])
        dq, dkvp, dkvc, dqg, dkg, dsk = vjp(do_ref[...])

        @pl.when(i == 0)
        def _():
            carry[...] = jnp.zeros_like(carry)

        @pl.when((b == 0) & (i == 0))
        def _():
            dqg_ref[...] = jnp.zeros_like(dqg_ref)
            dkg_ref[...] = jnp.zeros_like(dkg_ref)
            dsk_ref[...] = jnp.zeros_like(dsk_ref)

        dq_ref[...] = dq.astype(BF16)
        dkv_ref[...] = (dkvc + carry[...]).astype(BF16)
        carry[...] = dkvp
        dqg_ref[...] += dqg
        dkg_ref[...] += dkg
        dsk_ref[...] += dsk

    prev = lambda b, i: (b, jnp.maximum(nb - 2 - i, 0), 0)
    cur = lambda b, i: (b, nb - 1 - i, 0)
    return pl.pallas_call(
        body, name="attn_bwd", grid=(B, nb),
        out_shape=[jax.ShapeDtypeStruct((B, S, QW), BF16), jax.ShapeDtypeStruct((B, S, 2 * KVW), BF16),
                   jax.ShapeDtypeStruct((1, HD), F32), jax.ShapeDtypeStruct((1, HD), F32),
                   jax.ShapeDtypeStruct((1, HQ), F32)],
        in_specs=[pl.BlockSpec((None, BLK, QW), cur), pl.BlockSpec((None, BLK, 2 * KVW), prev),
                  pl.BlockSpec((None, BLK, 2 * KVW), cur), pl.BlockSpec((None, BLK, 1), prev),
                  pl.BlockSpec((None, BLK, 1), cur), _full((1, HD)), _full((1, HD)), _full((1, HQ)),
                  _full((1, HD)), _full((HD, HD)), pl.BlockSpec((None, BLK, QW), cur)],
        out_specs=[pl.BlockSpec((None, BLK, QW), cur), pl.BlockSpec((None, BLK, 2 * KVW), cur),
                   _full((1, HD)), _full((1, HD)), _full((1, HQ))],
        scratch_shapes=[pltpu.VMEM((BLK, 2 * KVW), F32)],
        compiler_params=_cparams(("arbitrary", "arbitrary")),
    )(aq, akv, akv, pos, pos, qg, kg, sinks, invf, perm, do)


def _conv_taps(xe, w, rows):
    y = None
    for j in range(CONV):
        sh = pltpu.roll(xe, CONV - 1 - j, 0)[8:8 + rows, :] if j < CONV - 1 else xe[8:8 + rows, :]
        y = sh * w[j:j + 1, :] if y is None else y + sh * w[j:j + 1, :]
    return y


def _conv_fwd(xin, w):
    B, S, C = xin.shape
    tc = min(512, S)
    r8 = tc // 8

    def body(xp_ref, x_ref, w_ref, o_ref):
        i = pl.program_id(1)
        xp = jnp.where(i > 0, xp_ref[...], 0.0)
        xe = jnp.concatenate([xp, x_ref[...]], axis=0)
        o_ref[...] = _silu(_conv_taps(xe, w_ref[...], tc))

    return pl.pallas_call(
        body, name="conv_fwd", grid=(B, S // tc),
        out_shape=jax.ShapeDtypeStruct((B, S, C), F32),
        in_specs=[pl.BlockSpec((None, 8, C), lambda b, i: (b, jnp.maximum(i * r8 - 1, 0), 0)),
                  _rows(tc, C), _full((CONV, C))],
        out_specs=_rows(tc, C),
        compiler_params=_cparams(("parallel", "arbitrary")),
    )(xin, xin, w)


def _conv_bwd(xin, w, dy):
    B, S, C = xin.shape
    tc = min(512, S)
    r8 = tc // 8
    nt = S // tc

    def body(xp_ref, x_ref, xn_ref, dy_ref, dyn_ref, w_ref, dx_ref, dw_ref):
        b, i = pl.program_id(0), pl.program_id(1)
        w = w_ref[...]
        xp = jnp.where(i > 0, xp_ref[...], 0.0)
        xe = jnp.concatenate([xp, x_ref[...], xn_ref[...]], axis=0)
        pre = _conv_taps(xe, w, tc + 8)
        sg = _sigmoid(pre)
        dyn = jnp.where(i < nt - 1, dyn_ref[...], 0.0)
        dpre = jnp.concatenate([dy_ref[...], dyn], axis=0) * (sg * (1.0 + pre * (1.0 - sg)))
        dx = dpre[0:tc, :] * w[CONV - 1:CONV, :]
        for j in range(CONV - 1):
            dx = dx + pltpu.roll(dpre, tc + 8 - (CONV - 1 - j), 0)[0:tc, :] * w[j:j + 1, :]
        dx_ref[...] = dx.astype(BF16)
        dcur = dpre[0:tc, :]
        xe0 = xe[0:8 + tc, :]
        lane_row = lax.broadcasted_iota(jnp.int32, (CONV, C), 0)
        dw = jnp.zeros((CONV, C), F32)
        for j in range(CONV):
            sh = pltpu.roll(xe0, CONV - 1 - j, 0)[8:8 + tc, :] if j < CONV - 1 else xe0[8:8 + tc, :]
            dw = dw + jnp.where(lane_row == j, jnp.sum(sh * dcur, axis=0, keepdims=True), 0.0)

        @pl.when((b == 0) & (i == 0))
        def _():
            dw_ref[...] = jnp.zeros_like(dw_ref)

        dw_ref[...] += dw

    return pl.pallas_call(
        body, name="conv_bwd", grid=(B, nt),
        out_shape=[jax.ShapeDtypeStruct((B, S, C), BF16), jax.ShapeDtypeStruct((CONV, C), F32)],
        in_specs=[pl.BlockSpec((None, 8, C), lambda b, i: (b, jnp.maximum(i * r8 - 1, 0), 0)),
                  _rows(tc, C),
                  pl.BlockSpec((None, 8, C), lambda b, i: (b, jnp.minimum((i + 1) * r8, S // 8 - 1), 0)),
                  _rows(tc, C),
                  pl.BlockSpec((None, 8, C), lambda b, i: (b, jnp.minimum((i + 1) * r8, S // 8 - 1), 0)),
                  _full((CONV, C))],
        out_specs=[_rows(tc, C), _full((CONV, C))],
        compiler_params=_cparams(("arbitrary", "arbitrary")),
    )(xin, xin, xin, dy, dy, w)


def _softplus(x):
    return jnp.maximum(x, 0.0) + jnp.log1p(jnp.exp(-jnp.abs(x)))


_BMM = (((2,), (1,)), ((0,), (0,)))
_BMM_NT = (((2,), (2,)), ((0,), (0,)))
_BMM_TN = (((1,), (1,)), ((0,), (0,)))


def _bmm(a, b, dims=_BMM):
    return lax.dot_general(a.astype(BF16), b.astype(BF16), dims, preferred_element_type=F32)


def _split(a):
    hi = a.astype(BF16)
    return hi, (a - hi.astype(F32)).astype(BF16)


def _bmm3(a, b, dims=_BMM):
    ah, al = _split(a)
    bh, bl = _split(b)
    d = lambda p, q: lax.dot_general(p, q, dims, preferred_element_type=F32)
    return d(ah, bh) + (d(ah, bl) + d(al, bh))


def _tri_inverse(L):
    eye = (lax.broadcasted_iota(jnp.int32, (CH, CH), 0) == lax.broadcasted_iota(jnp.int32, (CH, CH), 1)).astype(F32)
    T = eye - L
    P = L
    n = 2
    while n < CH:
        P = _bmm3(P, P)
        T = T + _bmm3(T, P)
        n *= 2
    return T


@jax.custom_vjp
def _tri_inverse_known(L, T):
    return T


def _tri_inverse_known_fwd(L, T):
    return T, T


def _tri_inverse_known_bwd(T, dT):
    return -_bmm3(T, _bmm3(dT, T, _BMM_NT), _BMM_TN), jnp.zeros_like(T)


_tri_inverse_known.defvjp(_tri_inverse_known_fwd, _tri_inverse_known_bwd)


def _cumsum_rows(g):
    n = g.shape[0]
    ii = lax.broadcasted_iota(jnp.int32, (n, CH, CH), 1)
    jj = lax.broadcasted_iota(jnp.int32, (n, CH, CH), 2)
    tri = (ii >= jj).astype(BF16)
    g0 = g.astype(BF16)
    r1 = g - g0.astype(F32)
    g1 = r1.astype(BF16)
    g2 = (r1 - g1.astype(F32)).astype(BF16)
    d = lambda q: lax.dot_general(tri, q, _BMM, preferred_element_type=F32)
    return d(g0) + (d(g1) + d(g2))


def _dn_prep(t_known, qr, kr, v, a_raw, b_raw, a_log, dt_b):
    n = qr.shape[0]
    ii = lax.broadcasted_iota(jnp.int32, (n, CH, CH), 1)
    jj = lax.broadcasted_iota(jnp.int32, (n, CH, CH), 2)
    incl, strict = ii >= jj, ii > jj
    q = qr * lax.rsqrt(jnp.sum(qr * qr, axis=-1, keepdims=True) + EPS) * (DK ** -0.5)
    k = kr * lax.rsqrt(jnp.sum(kr * kr, axis=-1, keepdims=True) + EPS)
    beta = _sigmoid(b_raw)
    g = -jnp.exp(a_log) * _softplus(a_raw + dt_b)
    gcb = _cumsum_rows(jnp.broadcast_to(g, (n, CH, DK)))
    gc = gcb[:, :, 0:1]
    gc_row = jnp.swapaxes(gcb, 1, 2)[:, 0:1, 0:CH]
    decay = jnp.where(incl, jnp.exp(jnp.where(incl, gc - gc_row, 0.0)), 0.0)
    kb = k * beta
    L = jnp.where(strict, _bmm(kb, k, _BMM_NT) * decay, 0.0)
    T = _tri_inverse(L) if t_known is None else _tri_inverse_known(L, t_known)
    eg = jnp.exp(gc)
    u = _bmm(T, v * beta)
    w = _bmm(T, kb * eg)
    a_in = _bmm(q, k, _BMM_NT) * decay
    g_last = gc[:, CH - 1:CH, :]
    return u, w, q * eg, k * jnp.exp(g_last - gc), a_in, jnp.exp(g_last), T


def _dn_step(S0, u, w, qd, kd, a_in, cd):
    r = _bmm(jnp.concatenate([w, qd], axis=1), S0)
    v_new = u - r[:, 0:CH, :]
    o = r[:, CH:2 * CH, :] + _bmm(a_in, v_new)
    S1 = S0 * cd + _bmm(kd, v_new, _BMM_TN)
    return o, S1


def _dn_stack(cq, ba, al, dt, G):
    cols = [[] for _ in range(7)]
    for c in range(G):
        rows = slice(CH * c, CH * (c + 1))
        for h in range(DH):
            parts = (cq[rows, DK * h:DK * (h + 1)], cq[rows, DNW + DK * h:DNW + DK * (h + 1)],
                     cq[rows, 2 * DNW + DK * h:2 * DNW + DK * (h + 1)], ba[rows, DH + h:DH + h + 1],
                     ba[rows, h:h + 1], al[:, h:h + 1], dt[:, h:h + 1])
            for col, p in zip(cols, parts):
                col.append(p)
    return tuple(jnp.stack(col) for col in cols)


def _dn_group(S, want):
    g = want
    while (S // CH) % g:
        g //= 2
    return g


def _dn_prep_fwd(cq, ba, a_log, dt_b):
    B, S, _ = cq.shape
    nc = S // CH
    G = _dn_group(S, 4)

    def body(cq_ref, ba_ref, al_ref, dt_ref, u_ref, w_ref, qd_ref, kd_ref, a_ref, t_ref, cd_ref):
        ops = _dn_stack(cq_ref[...], ba_ref[...], al_ref[...], dt_ref[...], G)
        u, w, qd, kd, a_in, cd, T = _dn_prep(None, *ops)
        lane4 = lax.broadcasted_iota(jnp.int32, (1, DH), 1)
        for c in range(G):
            rows = slice(CH * c, CH * (c + 1))
            cdrow = jnp.zeros((1, DH), F32)
            for h in range(DH):
                n = DH * c + h
                lanes = slice(DK * h, DK * (h + 1))
                u_ref[rows, lanes] = u[n]
                w_ref[rows, lanes] = w[n]
                qd_ref[rows, lanes] = qd[n]
                kd_ref[rows, lanes] = kd[n]
                a_ref[rows, CH * h:CH * (h + 1)] = a_in[n]
                t_ref[rows, CH * h:CH * (h + 1)] = T[n]
                cdrow = cdrow + jnp.where(lane4 == h, cd[n], 0.0)
            cd_ref[c] = cdrow

    wide = jax.ShapeDtypeStruct((B, S, DNW), F32)
    sq = jax.ShapeDtypeStruct((B, S, DH * CH), F32)
    return pl.pallas_call(
        body, name="dn_prep_fwd", grid=(B, nc // G),
        out_shape=[wide, wide, wide, wide, sq, sq, jax.ShapeDtypeStruct((B, nc, 1, DH), F32)],
        in_specs=[_rows(G * CH, CONVW), _rows(G * CH, 2 * DH), _full((1, DH)), _full((1, DH))],
        out_specs=[_rows(G * CH, DNW)] * 4 + [_rows(G * CH, DH * CH)] * 2
                  + [pl.BlockSpec((None, G, 1, DH), lambda b, i: (b, i, 0, 0))],
        compiler_params=_cparams(("parallel", "parallel")),
    )(cq, ba, a_log, dt_b)


def _dn_seq_specs(B, nc, rev):
    at = (lambda i: nc - 1 - i) if rev else (lambda i: i)
    wide = pl.BlockSpec((B, CH, DNW), lambda i: (0, at(i), 0))
    a_spec = pl.BlockSpec((B, CH, DH * CH), lambda i: (0, at(i), 0))
    cd_spec = pl.BlockSpec((B, None, 1, DH), lambda i: (0, at(i), 0, 0))
    st_spec = pl.BlockSpec((B, None, DH, DK, DK), lambda i: (0, at(i), 0, 0, 0))
    return wide, a_spec, cd_spec, st_spec


def _dn_step_operands(B, u_ref, w_ref, qd_ref, kd_ref, a_ref, cd_ref):
    pairs = [(b, h) for b in range(B) for h in range(DH)]
    wide = lambda ref: jnp.stack([ref[b, :, DK * h:DK * (h + 1)] for b, h in pairs])
    a_in = jnp.stack([a_ref[b, :, CH * h:CH * (h + 1)] for b, h in pairs])
    cd = jnp.stack([cd_ref[b, :, h:h + 1] for b, h in pairs])
    return wide(u_ref), wide(w_ref), wide(qd_ref), wide(kd_ref), a_in, cd


def _dn_seq_fwd(u, w, qd, kd, a_in, cd):
    B, S, _ = u.shape
    nc = S // CH

    def body(u_ref, w_ref, qd_ref, kd_ref, a_ref, cd_ref, o_ref, st_ref, state):
        @pl.when(pl.program_id(0) == 0)
        def _():
            state[...] = jnp.zeros_like(state)

        S0 = state[...]
        for b in range(B):
            st_ref[b] = S0[DH * b:DH * (b + 1)]
        o, S1 = _dn_step(S0, *_dn_step_operands(B, u_ref, w_ref, qd_ref, kd_ref, a_ref, cd_ref))
        state[...] = S1
        for b in range(B):
            for h in range(DH):
                o_ref[b, :, DK * h:DK * (h + 1)] = o[DH * b + h]

    wide, a_spec, cd_spec, st_spec = _dn_seq_specs(B, nc, False)
    return pl.pallas_call(
        body, name="dn_seq_fwd", grid=(nc,),
        out_shape=[jax.ShapeDtypeStruct((B, S, DNW), F32), jax.ShapeDtypeStruct((B, nc, DH, DK, DK), F32)],
        in_specs=[wide, wide, wide, wide, a_spec, cd_spec],
        out_specs=[wide, st_spec],
        scratch_shapes=[pltpu.VMEM((B * DH, DK, DK), F32)],
        compiler_params=_cparams(("arbitrary",)),
    )(u, w, qd, kd, a_in, cd)


def _dn_seq_bwd(u, w, qd, kd, a_in, cd, states, do):
    B, S, _ = u.shape
    nc = S // CH

    def body(u_ref, w_ref, qd_ref, kd_ref, a_ref, cd_ref, st_ref, do_ref,
             du_ref, dw_ref, dqd_ref, dkd_ref, da_ref, dcd_ref, dstate):
        @pl.when(pl.program_id(0) == 0)
        def _():
            dstate[...] = jnp.zeros_like(dstate)

        lane4 = lax.broadcasted_iota(jnp.int32, (1, DH), 1)
        S0 = jnp.concatenate([st_ref[b] for b in range(B)], axis=0)
        do = jnp.stack([do_ref[b, :, DK * h:DK * (h + 1)] for b in range(B) for h in range(DH)])
        _, vjp = jax.vjp(_dn_step, S0, *_dn_step_operands(B, u_ref, w_ref, qd_ref, kd_ref, a_ref, cd_ref))
        dS, du, dw, dqd, dkd, da, dcd = vjp((do, dstate[...]))
        dstate[...] = dS
        for b in range(B):
            dcdrow = jnp.zeros((1, DH), F32)
            for h in range(DH):
                n = DH * b + h
                lanes = slice(DK * h, DK * (h + 1))
                du_ref[b, :, lanes] = du[n]
                dw_ref[b, :, lanes] = dw[n]
                dqd_ref[b, :, lanes] = dqd[n]
                dkd_ref[b, :, lanes] = dkd[n]
                da_ref[b, :, CH * h:CH * (h + 1)] = da[n]
                dcdrow = dcdrow + jnp.where(lane4 == h, dcd[n], 0.0)
            dcd_ref[b] = dcdrow

    wide, a_spec, cd_spec, st_spec = _dn_seq_specs(B, nc, True)
    sd = jax.ShapeDtypeStruct((B, S, DNW), F32)
    return pl.pallas_call(
        body, name="dn_seq_bwd", grid=(nc,),
        out_shape=[sd, sd, sd, sd, jax.ShapeDtypeStruct((B, S, DH * CH), F32), jax.ShapeDtypeStruct((B, nc, 1, DH), F32)],
        in_specs=[wide, wide, wide, wide, a_spec, cd_spec, st_spec, wide],
        out_specs=[wide, wide, wide, wide, a_spec, cd_spec],
        scratch_shapes=[pltpu.VMEM((B * DH, DK, DK), F32)],
        compiler_params=_cparams(("arbitrary",)),
    )(u, w, qd, kd, a_in, cd, states, do)


def _dn_prep_bwd(cq, ba, a_log, dt_b, t_inv, du, dw, dqd, dkd, da, dcd):
    B, S, _ = cq.shape
    nc = S // CH
    G = _dn_group(S, 4)

    def body(cq_ref, ba_ref, al_ref, dt_ref, t_ref, du_ref, dw_ref, dqd_ref, dkd_ref, da_ref, dcd_ref,
             dcq_ref, dba_ref, dal_ref, ddt_ref):
        @pl.when((pl.program_id(0) == 0) & (pl.program_id(1) == 0))
        def _():
            dal_ref[...] = jnp.zeros_like(dal_ref)
            ddt_ref[...] = jnp.zeros_like(ddt_ref)

        pairs = [(c, h) for c in range(G) for h in range(DH)]
        rows = lambda c: slice(CH * c, CH * (c + 1))
        wide = lambda ref: jnp.stack([ref[rows(c), DK * h:DK * (h + 1)] for c, h in pairs])
        square = lambda ref: jnp.stack([ref[rows(c), CH * h:CH * (h + 1)] for c, h in pairs])
        ops = _dn_stack(cq_ref[...], ba_ref[...], al_ref[...], dt_ref[...], G)
        cots = (wide(du_ref), wide(dw_ref), wide(dqd_ref), wide(dkd_ref), square(da_ref),
                jnp.stack([dcd_ref[c][:, h:h + 1] for c, h in pairs]), jnp.zeros((len(pairs), CH, CH), F32))
        _, vjp = jax.vjp(functools.partial(_dn_prep, square(t_ref)), *ops)
        dq, dk, dv, dar, dbr, dl, dd = vjp(cots)
        lane8 = lax.broadcasted_iota(jnp.int32, (CH, 2 * DH), 1)
        lane4 = lax.broadcasted_iota(jnp.int32, (1, DH), 1)
        dal = jnp.zeros((1, DH), F32)
        ddt = jnp.zeros((1, DH), F32)
        for c in range(G):
            dba = jnp.zeros((CH, 2 * DH), F32)
            for h in range(DH):
                n = DH * c + h
                dcq_ref[rows(c), DK * h:DK * (h + 1)] = dq[n]
                dcq_ref[rows(c), DNW + DK * h:DNW + DK * (h + 1)] = dk[n]
                dcq_ref[rows(c), 2 * DNW + DK * h:2 * DNW + DK * (h + 1)] = dv[n]
                dba = dba + jnp.where(lane8 == h, dbr[n], 0.0) + jnp.where(lane8 == DH + h, dar[n], 0.0)
                dal = dal + jnp.where(lane4 == h, dl[n], 0.0)
                ddt = ddt + jnp.where(lane4 == h, dd[n], 0.0)
            dba_ref[rows(c), :] = dba.astype(BF16)
        dal_ref[...] += dal
        ddt_ref[...] += ddt

    return pl.pallas_call(
        body, name="dn_prep_bwd", grid=(B, nc // G),
        out_shape=[jax.ShapeDtypeStruct((B, S, CONVW), F32), jax.ShapeDtypeStruct((B, S, 2 * DH), BF16),
                   jax.ShapeDtypeStruct((1, DH), F32), jax.ShapeDtypeStruct((1, DH), F32)],
        in_specs=[_rows(G * CH, CONVW), _rows(G * CH, 2 * DH), _full((1, DH)), _full((1, DH)), _rows(G * CH, DH * CH)]
                 + [_rows(G * CH, DNW)] * 4 + [_rows(G * CH, DH * CH),
                                               pl.BlockSpec((None, G, 1, DH), lambda b, i: (b, i, 0, 0))],
        out_specs=[_rows(G * CH, CONVW), _rows(G * CH, 2 * DH), _full((1, DH)), _full((1, DH))],
        compiler_params=_cparams(("arbitrary", "arbitrary")),
    )(cq, ba, a_log, dt_b, t_inv, du, dw, dqd, dkd, da, dcd)


def _gated_norm(o, z, g):
    outs = []
    for h in range(DH):
        t = o[:, DK * h:DK * (h + 1)]
        r = lax.rsqrt(jnp.mean(t * t, axis=-1, keepdims=True) + EPS)
        outs.append(t * r * g * _silu(z[:, DK * h:DK * (h + 1)]))
    return jnp.concatenate(outs, axis=1)


def _mix_fwd(x, o_attn, o_dn, z, ga, gd, mod, dn_g, w_branch, w_out):
    B, S, _ = x.shape
    tm = _tile(S)

    def body(x_ref, oa_ref, od_ref, z_ref, ga_ref, gd_ref, mod_ref, g_ref, wb_ref, wo_ref,
             x1_ref, mix_ref, mg_ref, oab_ref, odb_ref):
        oa = oa_ref[...].astype(BF16)
        od = _gated_norm(od_ref[...], z_ref[...], g_ref[...]).astype(BF16)
        oab_ref[...] = oa
        odb_ref[...] = od
        ya = jnp.dot(oa, wb_ref[0:QW, :], preferred_element_type=F32)
        yd = jnp.dot(od, wb_ref[QW:QW + DNW, :], preferred_element_type=F32)
        merged = (_sigmoid(ga_ref[...]) * ya + _sigmoid(gd_ref[...]) * yd).astype(BF16)
        mg_ref[...] = merged
        mix = jnp.dot(merged, wo_ref[...], preferred_element_type=F32)
        mix_ref[...] = mix
        x1_ref[...] = x_ref[...] + mod_ref[2:3, :] * mix

    return pl.pallas_call(
        body, name="mix_fwd", grid=(B, S // tm),
        out_shape=[jax.ShapeDtypeStruct((B, S, D), F32), jax.ShapeDtypeStruct((B, S, D), F32),
                   jax.ShapeDtypeStruct((B, S, D), BF16), jax.ShapeDtypeStruct((B, S, QW), BF16),
                   jax.ShapeDtypeStruct((B, S, DNW), BF16)],
        in_specs=[_rows(tm, D), _rows(tm, QW), _rows(tm, DNW), _rows(tm, DNW), _rows(tm, D), _rows(tm, D),
                  _perb(6, D), _full((1, DK)), _resident(w_branch.shape), _resident(w_out.shape)],
        out_specs=[_rows(tm, D), _rows(tm, D), _rows(tm, D), _rows(tm, QW), _rows(tm, DNW)],
        compiler_params=_cparams(("parallel", "arbitrary")),
    )(x, o_attn, o_dn, z, ga, gd, mod, dn_g, w_branch, w_out)


def _mix_bwd(dx1, mix, o_attn, o_dn, z, ga, gd, mod, dn_g, w_branch, w_out):
    B, S, _ = dx1.shape
    tm = _tile(S)

    def body(dx1_ref, mix_ref, oa_ref, od_ref, z_ref, ga_ref, gd_ref, mod_ref, g_ref, wb_ref, wo_ref,
             dmix_ref, dya_ref, dyd_ref, dga_ref, dgd_ref, dz_ref, doa_ref, dod_ref, dgate_ref, dg_ref):
        b, i = pl.program_id(0), pl.program_id(1)
        dx1 = dx1_ref[...]
        dmix = (dx1 * mod_ref[2:3, :]).astype(BF16)
        dmix_ref[...] = dmix
        dgate = jnp.sum(dx1 * mix_ref[...], axis=0, keepdims=True)
        dmerged = _dot_nt(dmix, wo_ref[...])
        odn, gn_vjp = jax.vjp(_gated_norm, od_ref[...], z_ref[...], g_ref[...])
        ya = _dot(oa_ref[...], wb_ref[0:QW, :])
        yd = _dot(odn, wb_ref[QW:QW + DNW, :])
        sa, sd = _sigmoid(ga_ref[...]), _sigmoid(gd_ref[...])
        dya = (dmerged * sa).astype(BF16)
        dyd = (dmerged * sd).astype(BF16)
        dya_ref[...] = dya
        dyd_ref[...] = dyd
        dga_ref[...] = (dmerged * ya * sa * (1.0 - sa)).astype(BF16)
        dgd_ref[...] = (dmerged * yd * sd * (1.0 - sd)).astype(BF16)
        doa_ref[...] = _dot_nt(dya, wb_ref[0:QW, :])
        dodn = _dot_nt(dyd, wb_ref[QW:QW + DNW, :])
        dod, dz, dg = gn_vjp(dodn)
        dod_ref[...] = dod
        dz_ref[...] = dz.astype(BF16)

        @pl.when(i == 0)
        def _():
            dgate_ref[...] = jnp.zeros_like(dgate_ref)

        @pl.when((b == 0) & (i == 0))
        def _():
            dg_ref[...] = jnp.zeros_like(dg_ref)

        dgate_ref[...] += dgate
        dg_ref[...] += dg

    return pl.pallas_call(
        body, name="mix_bwd", grid=(B, S // tm),
        out_shape=[jax.ShapeDtypeStruct((B, S, D), BF16), jax.ShapeDtypeStruct((B, S, D), BF16),
                   jax.ShapeDtypeStruct((B, S, D), BF16), jax.ShapeDtypeStruct((B, S, D), BF16),
                   jax.ShapeDtypeStruct((B, S, D), BF16), jax.ShapeDtypeStruct((B, S, DNW), BF16),
                   jax.ShapeDtypeStruct((B, S, QW), F32), jax.ShapeDtypeStruct((B, S, DNW), F32),
                   jax.ShapeDtypeStruct((B, 1, D), F32), jax.ShapeDtypeStruct((1, DK), F32)],
        in_specs=[_rows(tm, D), _rows(tm, D), _rows(tm, QW), _rows(tm, DNW), _rows(tm, DNW), _rows(tm, D),
                  _rows(tm, D), _perb(6, D), _full((1, DK)), _resident(w_branch.shape), _resident(w_out.shape)],
        out_specs=[_rows(tm, D), _rows(tm, D), _rows(tm, D), _rows(tm, D), _rows(tm, D), _rows(tm, DNW),
                   _rows(tm, QW), _rows(tm, DNW), _perb(1, D), _full((1, DK))],
        compiler_params=_cparams(("arbitrary", "arbitrary")),
    )(dx1, mix, o_attn, o_dn, z, ga, gd, mod, dn_g, w_branch, w_out)


def _ffn1_fwd(x1, mod, g2, w_gu):
    B, S, _ = x1.shape
    tm = _tile(S)

    def body(x_ref, mod_ref, g_ref, w_ref, h_ref, gate_ref, up_ref, act_ref):
        h = _rms_mod(x_ref[...], g_ref[...], mod_ref[4:5, :], mod_ref[3:4, :]).astype(BF16)
        h_ref[...] = h
        gate = jnp.dot(h, w_ref[:, 0:FFN], preferred_element_type=F32)
        up = jnp.dot(h, w_ref[:, FFN:2 * FFN], preferred_element_type=F32)
        gate_ref[...] = gate
        up_ref[...] = up
        act_ref[...] = (_silu(gate) * up).astype(BF16)

    return pl.pallas_call(
        body, name="ffn1_fwd", grid=(B, S // tm),
        out_shape=[jax.ShapeDtypeStruct((B, S, D), BF16), jax.ShapeDtypeStruct((B, S, FFN), F32),
                   jax.ShapeDtypeStruct((B, S, FFN), F32), jax.ShapeDtypeStruct((B, S, FFN), BF16)],
        in_specs=[_rows(tm, D), _perb(6, D), _full((1, D)), _resident(w_gu.shape)],
        out_specs=[_rows(tm, D), _rows(tm, FFN), _rows(tm, FFN), _rows(tm, FFN)],
        compiler_params=_cparams(("parallel", "arbitrary")),
    )(x1, mod, g2, w_gu)


def _ffn2_fwd(act, x1, target, mod, w_down):
    B, S, _ = x1.shape
    tm = _tile(S)

    def body(a_ref, x_ref, t_ref, mod_ref, w_ref, dy_ref, loss_ref, dgate_ref):
        b, i = pl.program_id(0), pl.program_id(1)
        y = jnp.dot(a_ref[...], w_ref[...], preferred_element_type=F32)
        err = x_ref[...] + mod_ref[5:6, :] * y - t_ref[...]
        dy = err * (1.0 / D)
        dy_ref[...] = dy

        @pl.when((b == 0) & (i == 0))
        def _():
            loss_ref[...] = jnp.zeros_like(loss_ref)

        @pl.when(i == 0)
        def _():
            dgate_ref[...] = jnp.zeros_like(dgate_ref)

        loss_ref[...] += (0.5 / D) * jnp.sum(err * err)
        dgate_ref[...] += jnp.sum(dy * y, axis=0, keepdims=True)

    return pl.pallas_call(
        body, name="ffn2_fwd", grid=(B, S // tm),
        out_shape=[jax.ShapeDtypeStruct((B, S, D), F32), jax.ShapeDtypeStruct((1, 128), F32),
                   jax.ShapeDtypeStruct((B, 1, D), F32)],
        in_specs=[_rows(tm, FFN), _rows(tm, D), _rows(tm, D), _perb(6, D), _resident(w_down.shape)],
        out_specs=[_rows(tm, D), _full((1, 128)), _perb(1, D)],
        compiler_params=_cparams(("arbitrary", "arbitrary")),
    )(act, x1, target, mod, w_down)


def _ffn2_bwd(dy, gate, up, mod, w_down):
    B, S, _ = dy.shape
    tm = _tile(S)

    def body(dy_ref, gate_ref, up_ref, mod_ref, w_ref, dgu_ref, dyg_ref):
        dyg = (dy_ref[...] * mod_ref[5:6, :]).astype(BF16)
        dyg_ref[...] = dyg
        dact = _dot_nt(dyg, w_ref[...])
        gate, up = gate_ref[...], up_ref[...]
        sg = _sigmoid(gate)
        dgu_ref[:, 0:FFN] = (dact * up * (sg * (1.0 + gate * (1.0 - sg)))).astype(BF16)
        dgu_ref[:, FFN:2 * FFN] = (dact * (gate * sg)).astype(BF16)

    return pl.pallas_call(
        body, name="ffn2_bwd", grid=(B, S // tm),
        out_shape=[jax.ShapeDtypeStruct((B, S, 2 * FFN), BF16), jax.ShapeDtypeStruct((B, S, D), BF16)],
        in_specs=[_rows(tm, D), _rows(tm, FFN), _rows(tm, FFN), _perb(6, D), _resident(w_down.shape)],
        out_specs=[_rows(tm, 2 * FFN), _rows(tm, D)],
        compiler_params=_cparams(("parallel", "arbitrary")),
    )(dy, gate, up, mod, w_down)


def _ffn1_bwd(dgu, x1, dy, mod, g2, w_gu):
    B, S, _ = x1.shape
    tm = _tile(S)

    def body(dgu_ref, x_ref, dy_ref, mod_ref, g_ref, w_ref, dx1_ref, dg_ref, dsc_ref, dsh_ref):
        b, i = pl.program_id(0), pl.program_id(1)
        dh = _dot_nt(dgu_ref[...], w_ref[...])
        _, vjp = jax.vjp(_rms_mod, x_ref[...], g_ref[...], mod_ref[4:5, :], mod_ref[3:4, :])
        dx, dg, dsc, dsh = vjp(dh)
        dx1_ref[...] = dy_ref[...] + dx

        @pl.when((b == 0) & (i == 0))
        def _():
            dg_ref[...] = jnp.zeros_like(dg_ref)

        @pl.when(i == 0)
        def _():
            dsc_ref[...] = jnp.zeros_like(dsc_ref)
            dsh_ref[...] = jnp.zeros_like(dsh_ref)

        dg_ref[...] += dg
        dsc_ref[...] += dsc
        dsh_ref[...] += dsh

    return pl.pallas_call(
        body, name="ffn1_bwd", grid=(B, S // tm),
        out_shape=[jax.ShapeDtypeStruct((B, S, D), F32), jax.ShapeDtypeStruct((1, D), F32),
                   jax.ShapeDtypeStruct((B, 1, D), F32), jax.ShapeDtypeStruct((B, 1, D), F32)],
        in_specs=[_rows(tm, 2 * FFN), _rows(tm, D), _rows(tm, D), _perb(6, D), _full((1, D)), _resident(w_gu.shape)],
        out_specs=[_rows(tm, D), _full((1, D)), _perb(1, D), _perb(1, D)],
        compiler_params=_cparams(("arbitrary", "arbitrary")),
    )(dgu, x1, dy, mod, g2, w_gu)


def _adamw(w, g, m, v, name):
    def body(w_ref, g_ref, m_ref, v_ref, d_ref, nm_ref, nv_ref):
        g = g_ref[...]
        m = B1 * m_ref[...] + (1.0 - B1) * g
        v = B2 * v_ref[...] + (1.0 - B2) * (g * g)
        nm_ref[...] = m
        nv_ref[...] = v
        m_hat = m / (1.0 - B1 ** STEP)
        v_hat = v / (1.0 - B2 ** STEP)
        d_ref[...] = -LR * (m_hat / (jnp.sqrt(v_hat) + AEPS) + WD * w_ref[...])

    sd = jax.ShapeDtypeStruct(w.shape, F32)
    return pl.pallas_call(body, name=name, out_shape=(sd, sd, sd), compiler_params=_cparams())(w, g, m, v)


SHARD_ROWS = (609, 128, 128, 704, 352)
BF16_TILE_ROWS = 16
PART_ROWS = tuple(-(-r // BF16_TILE_ROWS) * BF16_TILE_ROWS for r in SHARD_ROWS)
PACK_ROWS = sum(PART_ROWS)


def _pad_rows(a, rows, axis):
    widths = [(0, 0)] * a.ndim
    widths[axis] = (0, rows - a.shape[axis])
    return jnp.pad(a, widths) if rows != a.shape[axis] else a


def _pack_cols(w, n):
    return w.reshape(w.shape[0], N_DEV, n).transpose(1, 0, 2).reshape(N_DEV, (w.shape[0] * n) // D, D)


def _unpack_cols(p, rows_in, n):
    return p.reshape(N_DEV, rows_in, n).transpose(1, 0, 2).reshape(rows_in, N_DEV * n)


def kernel(x, c, positions, ada_w, ada_b, norm1_g, w_in, conv_w, q_norm_g, k_norm_g, sinks, a_log, dt_bias, dn_norm_g, w_branch, w_out, norm2_g, w_gate_up, w_down, loss_target, m_ada_w, m_ada_b, m_norm1_g, m_w_in, m_conv_w, m_q_norm_g, m_k_norm_g, m_sinks, m_a_log, m_dt_bias, m_dn_norm_g, m_w_branch, m_w_out, m_norm2_g, m_w_gate_up, m_w_down, v_ada_w, v_ada_b, v_norm1_g, v_w_in, v_conv_w, v_q_norm_g, v_k_norm_g, v_sinks, v_a_log, v_dt_bias, v_dn_norm_g, v_w_branch, v_w_out, v_norm2_g, v_w_gate_up, v_w_down):
    B, S, _ = x.shape
    me = 4 * lax.axis_index("x") + 2 * lax.axis_index("y") + lax.axis_index("c")
    n_in, n_gu, n_dn = IN_W // N_DEV, 2 * FFN // N_DEV, FFN // N_DEV

    flat = [w_in[0].reshape(-1, D), w_branch[0], w_out[0], w_gate_up[0].reshape(-1, D), w_down[0]]
    packed = jnp.concatenate([_pad_rows(f.astype(BF16), r, 0) for f, r in zip(flat, PART_ROWS)], axis=0)
    gathered = _all_gather_big(packed, "gather_weights")
    offs = np.cumsum((0,) + PART_ROWS)
    part = [gathered[:, offs[i]:offs[i] + SHARD_ROWS[i], :] for i in range(5)]
    w_in_f = _unpack_cols(part[0], D, n_in)
    w_branch_f = part[1].reshape(D, D)
    w_out_f = part[2].reshape(D, D)
    w_gu_f = _unpack_cols(part[3], D, n_gu)
    w_down_f = part[4].reshape(FFN, D)
    cuts = [(0, QW), (QW, QW + 2 * KVW), (QW + 2 * KVW, QW + 2 * KVW + CONVW)]
    o = QW + 2 * KVW + CONVW
    cuts += [(o, o + 2 * DH), (o + 2 * DH, o + 2 * DH + DNW), (o + 2 * DH + DNW, o + 2 * DH + DNW + D),
             (o + 2 * DH + DNW + D, IN_W)]
    ws_in = [w_in_f[:, a:b] for a, b in cuts]

    c_all = _all_gather_small(c, "gather_c").reshape(N_DEV * B, D)
    ncol = 6 * D // N_DEV
    mod_cols, cond_all = _ada_fwd(c_all, ada_w[0], lax.dynamic_slice(ada_b, (0, me * ncol), (1, ncol)))
    mod_all = _all_gather_small(mod_cols, "gather_mod").transpose(1, 0, 2).reshape(N_DEV * B, 6 * D)
    mod = lax.dynamic_slice(mod_all, (me * B, 0), (B, 6 * D)).reshape(B, 6, D)

    h1, aq, akv, dnx, ba, z, ga, gd = _inproj_fwd(x, mod, norm1_g, ws_in)
    invf, perm = _rope_consts()
    pos3 = positions.reshape(B, S, 1)
    o_attn = _attn_fwd(aq, akv, pos3, q_norm_g, k_norm_g, sinks, invf, perm)
    conv2 = conv_w.reshape(CONV, CONVW // N_DEV)
    conv_all = _all_gather_small(conv2, "gather_conv").transpose(1, 0, 2).reshape(CONV, CONVW)
    cq = _conv_fwd(dnx, conv_all)
    dn_u, dn_w, dn_qd, dn_kd, dn_a, dn_t, dn_cd = _dn_prep_fwd(cq, ba, a_log, dt_bias)
    o_dn, states = _dn_seq_fwd(dn_u, dn_w, dn_qd, dn_kd, dn_a, dn_cd)
    x1, mix, merged, oa_b, od_b = _mix_fwd(x, o_attn, o_dn, z, ga, gd, mod, dn_norm_g, w_branch_f, w_out_f)
    h2, gate, up, act = _ffn1_fwd(x1, mod, norm2_g, w_gu_f)
    dy, loss_part, d_gate2 = _ffn2_fwd(act, x1, loss_target, mod, w_down_f)
    loss = lax.psum(loss_part[0, 0], ("x", "y", "c"))

    dgu, dyg = _ffn2_bwd(dy, gate, up, mod, w_down_f)
    g_w_down = _wgrad(act, dyg, "wgrad_down")
    dx1, d_n2g, d_scale2, d_shift2 = _ffn1_bwd(dgu, x1, dy, mod, norm2_g, w_gu_f)
    g_w_gu = _wgrad(h2, dgu, "wgrad_gate_up")
    dmix, dya, dyd, dga, dgd, dz, d_oa, d_od, d_gate1, d_dng = _mix_bwd(
        dx1, mix, o_attn, o_dn, z, ga, gd, mod, dn_norm_g, w_branch_f, w_out_f)
    g_w_out = _wgrad(merged, dmix, "wgrad_out")
    g_w_branch = jnp.concatenate([_wgrad(oa_b, dya, "wgrad_branch_a"), _wgrad(od_b, dyd, "wgrad_branch_d")], axis=0)
    d_dn = _dn_seq_bwd(dn_u, dn_w, dn_qd, dn_kd, dn_a, dn_cd, states, d_od)
    dcq, dba, d_alog, d_dtb = _dn_prep_bwd(cq, ba, a_log, dt_bias, dn_t, *d_dn)
    ddnx, d_conv = _conv_bwd(dnx, conv_all, dcq)
    daq, dakv, d_qg, d_kg, d_sinks = _attn_bwd(aq, akv, pos3, q_norm_g, k_norm_g, sinks, invf, perm, d_oa)
    dps = [daq, dakv, ddnx, dba, dz, dga, dgd]
    grad_x, d_n1g, d_scale1, d_shift1 = _inproj_bwd(x, mod, norm1_g, dx1, dps, ws_in)
    names = ["q", "kv", "dn", "ba", "z", "ga", "gd"]
    g_w_in = jnp.concatenate([_wgrad(h1, dp, "wgrad_in_" + nm) for dp, nm in zip(dps, names)], axis=1)

    gparts = [_pack_cols(g_w_in, n_in), g_w_branch.reshape(N_DEV, D // N_DEV, D), g_w_out.reshape(N_DEV, D // N_DEV, D),
              _pack_cols(g_w_gu, n_gu), g_w_down.reshape(N_DEV, n_dn, D)]
    gpack = jnp.concatenate([_pad_rows(p.astype(BF16), r, 1) for p, r in zip(gparts, PART_ROWS)], axis=1)
    gsum = _sum_blocks(_exchange_blocks(gpack, "exchange_grads"), "sum_grads")
    gs = [gsum[offs[i]:offs[i] + SHARD_ROWS[i], :] for i in range(5)]
    grad_w_in = gs[0].reshape(1, D, n_in)
    grad_w_branch = gs[1].reshape(1, D // N_DEV, D)
    grad_w_out = gs[2].reshape(1, D // N_DEV, D)
    grad_w_gu = gs[3].reshape(1, D, n_gu)
    grad_w_down = gs[4].reshape(1, n_dn, D)

    dmod = jnp.concatenate([d_shift1, d_scale1, d_gate1, d_shift2, d_scale2, d_gate2], axis=2).reshape(B, 6 * D)
    small = jnp.concatenate([d_n1g, d_qg, d_kg, d_sinks, d_alog, d_dtb, d_dng, d_n2g, d_conv.reshape(1, CONV * CONVW)], axis=1)
    nsm = small.shape[1]
    width = -(-max(6 * D, nsm) // 128) * 128
    rows = jnp.concatenate([jnp.pad(dmod, ((0, 0), (0, width - 6 * D))), jnp.pad(small, ((0, 8 - B - 1), (0, width - nsm)))], axis=0)
    rows_all = _all_gather_small(rows, "gather_small")
    dmod_all = rows_all[:, 0:B, 0:6 * D].reshape(N_DEV * B, 6 * D)
    dmod_cols = lax.dynamic_slice(dmod_all, (0, me * ncol), (N_DEV * B, ncol))
    grad_ada_w, grad_ada_b, small_sum = _ada_bwd(cond_all, dmod_all, dmod_cols, rows_all[:, B, :])
    sizes = [D, HD, HD, HQ, DH, DH, DK, D]
    so = np.cumsum([0] + sizes)
    g_n1, g_qg, g_kg, g_sk, g_al, g_dt, g_dn, g_n2 = [small_sum[:, so[i]:so[i + 1]] for i in range(8)]
    g_conv_all = small_sum[:, so[8]:so[8] + CONV * CONVW].reshape(CONV, N_DEV, CONVW // N_DEV)
    grad_conv = lax.dynamic_slice(g_conv_all, (0, me, 0), (CONV, 1, CONVW // N_DEV)).reshape(CONV, CONVW // N_DEV)

    big = [(ada_w, grad_ada_w.reshape(ada_w.shape), m_ada_w, v_ada_w), (w_in, grad_w_in, m_w_in, v_w_in),
           (w_branch, grad_w_branch, m_w_branch, v_w_branch), (w_out, grad_w_out, m_w_out, v_w_out),
           (w_gate_up, grad_w_gu, m_w_gate_up, v_w_gate_up), (w_down, grad_w_down, m_w_down, v_w_down)]
    upd = {}
    for nm, (w, g, m, v) in zip(["ada_w", "w_in", "w_branch", "w_out", "w_gate_up", "w_down"], big):
        s2 = w.shape[1:]
        res = _adamw(w.reshape(s2), g.reshape(s2), m.reshape(s2), v.reshape(s2), "adamw_" + nm)
        upd[nm] = tuple(r.reshape(w.shape) for r in res)
    small_names = ["ada_b", "norm1_g", "q_norm_g", "k_norm_g", "sinks", "a_log", "dt_bias", "dn_norm_g", "norm2_g", "conv_w"]
    small_w = [ada_b, norm1_g, q_norm_g, k_norm_g, sinks, a_log, dt_bias, dn_norm_g, norm2_g, conv_w]
    small_g = [grad_ada_b, g_n1, g_qg, g_kg, g_sk, g_al, g_dt, g_dn, g_n2, grad_conv]
    small_m = [m_ada_b, m_norm1_g, m_q_norm_g, m_k_norm_g, m_sinks, m_a_log, m_dt_bias, m_dn_norm_g, m_norm2_g, m_conv_w]
    small_v = [v_ada_b, v_norm1_g, v_q_norm_g, v_k_norm_g, v_sinks, v_a_log, v_dt_bias, v_dn_norm_g, v_norm2_g, v_conv_w]
    cat = lambda arrs: jnp.concatenate([a.reshape(1, -1) for a in arrs], axis=1)
    res = _adamw(cat(small_w), cat(small_g), cat(small_m), cat(small_v), "adamw_small")
    po = np.cumsum([0] + [int(np.prod(w.shape)) for w in small_w])
    grads = {}
    for i, nm in enumerate(small_names):
        upd[nm] = tuple(r[:, po[i]:po[i + 1]].reshape(small_w[i].shape) for r in res)
        grads[nm] = small_g[i].reshape(small_w[i].shape)
    grads.update(ada_w=grad_ada_w.reshape(ada_w.shape), w_in=grad_w_in, w_branch=grad_w_branch, w_out=grad_w_out,
                 w_gate_up=grad_w_gu, w_down=grad_w_down)

    order = ["ada_w", "ada_b", "norm1_g", "w_in", "conv_w", "q_norm_g", "k_norm_g", "sinks", "a_log", "dt_bias",
             "dn_norm_g", "w_branch", "w_out", "norm2_g", "w_gate_up", "w_down"]
    return (loss, grad_x, *[grads[n] for n in order], *[upd[n][0] for n in order],
            *[upd[n][1] for n in order], *[upd[n][2] for n in order])
```

```python
import functools

import numpy as np
import jax
import jax.numpy as jnp
from jax import lax
from jax.experimental import pallas as pl
from jax.experimental.pallas import tpu as pltpu

F32 = jnp.float32
BF16 = jnp.bfloat16
HI = lax.Precision.HIGHEST

N_DEV = 8
D = 1024
HQ, HKV, HD = 8, 2, 64
GRP = HQ // HKV
BLK = 128
ROT = HD // 4
THETA = 500000.0
QW, KVW = HQ * HD, HKV * HD
DH, DK = 4, 128
CH = 64
DNW = DH * DK
CONV = 4
CONVW = 3 * DNW
FFN = 2816
EPS = 1e-6
IN_W = QW + 2 * KVW + CONVW + 2 * DH + DNW + 2 * D

LR, B1, B2, AEPS, WD, STEP = 0.001, 0.9, 0.999, 1e-08, 0.01, 10

VMEM_LIMIT = 56 * 1024 * 1024
MESH = pl.DeviceIdType.MESH


def _cparams(sem=None, vmem=VMEM_LIMIT):
    return pltpu.CompilerParams(dimension_semantics=sem, vmem_limit_bytes=vmem)


def _full(shape):
    n = len(shape)
    return pl.BlockSpec(shape, lambda *_: (0,) * n)


def _resident(shape):
    n = len(shape)
    return pl.BlockSpec(shape, lambda *_: (0,) * n, pipeline_mode=pl.Buffered(1))


def _rows(tm, w):
    return pl.BlockSpec((None, tm, w), lambda b, i: (b, i, 0))


def _perb(r, w):
    return pl.BlockSpec((None, r, w), lambda b, i: (b, 0, 0))


def _dot(a, b):
    return jnp.dot(a.astype(BF16), b.astype(BF16), preferred_element_type=F32)


def _dot_nt(a, b):
    return lax.dot_general(a.astype(BF16), b.astype(BF16), (((1,), (1,)), ((), ())), preferred_element_type=F32)


def _dot_tn(a, b):
    return lax.dot_general(a.astype(BF16), b.astype(BF16), (((0,), (0,)), ((), ())), preferred_element_type=F32)


def _dot_hi(a, b):
    return jnp.dot(a, b, preferred_element_type=F32, precision=HI)


def _sigmoid(x):
    return jax.nn.sigmoid(x)


def _silu(x):
    return x * jax.nn.sigmoid(x)


def _rms_mod(x, g, scale, shift):
    r = lax.rsqrt(jnp.mean(x * x, axis=-1, keepdims=True) + EPS)
    return (x * r * g) * (1.0 + scale) + shift


def _tile(S):
    return min(256, S)


def _peer(x, y, c, k):
    px = 1 - x if (k >> 2) & 1 else x
    py = 1 - y if (k >> 1) & 1 else y
    pc = 1 - c if k & 1 else c
    return px, py, pc


def _all_gather_small(v, name):
    r, n = v.shape

    def body(v_ref, out_ref, send_sems, recv_sems, local_sem):
        x, y, c = lax.axis_index("x"), lax.axis_index("y"), lax.axis_index("c")
        me = 4 * x + 2 * y + c
        mine = pltpu.make_async_copy(v_ref, out_ref.at[me], local_sem)
        mine.start()
        sends = []
        for k in range(1, N_DEV):
            cp = pltpu.make_async_remote_copy(
                src_ref=v_ref, dst_ref=out_ref.at[me], send_sem=send_sems.at[k - 1], recv_sem=recv_sems.at[k - 1],
                device_id=_peer(x, y, c, k), device_id_type=MESH)
            cp.start()
            sends.append(cp)
        for k in range(1, N_DEV):
            px, py, pc = _peer(x, y, c, k)
            pltpu.make_async_remote_copy(
                src_ref=v_ref, dst_ref=out_ref.at[4 * px + 2 * py + pc], send_sem=send_sems.at[k - 1],
                recv_sem=recv_sems.at[k - 1], device_id=(px, py, pc), device_id_type=MESH).wait_recv()
        for cp in sends:
            cp.wait_send()
        mine.wait()

    return pl.pallas_call(
        body, name=name,
        out_shape=jax.ShapeDtypeStruct((N_DEV, r, n), v.dtype),
        in_specs=[pl.BlockSpec(memory_space=pltpu.VMEM)],
        out_specs=pl.BlockSpec(memory_space=pltpu.VMEM),
        scratch_shapes=[pltpu.SemaphoreType.DMA((N_DEV - 1,)), pltpu.SemaphoreType.DMA((N_DEV - 1,)), pltpu.SemaphoreType.DMA],
    )(v)


def _all_gather_big(v, name):
    r, n = v.shape

    def body(v_ref, out_ref, send_sems, recv_sems, local_sem):
        x, y, c = lax.axis_index("x"), lax.axis_index("y"), lax.axis_index("c")
        me, sibling = (x, y, c), (x, y, 1 - c)
        chips = [(1 - x, y), (x, 1 - y), (1 - x, 1 - y)]

        def rows(px, py, pc):
            return out_ref.at[4 * px + 2 * py + pc]

        def copy(k, block, to, src=None):
            return pltpu.make_async_remote_copy(
                src_ref=rows(*block) if src is None else src, dst_ref=rows(*block),
                send_sem=send_sems.at[k], recv_sem=recv_sems.at[k], device_id=to, device_id_type=MESH)

        mine = pltpu.make_async_copy(v_ref, rows(*me), local_sem)
        mine.start()
        first = [copy(0, me, sibling, src=v_ref)]
        first += [copy(1 + j, me, (*chip, c), src=v_ref) for j, chip in enumerate(chips)]
        for cp in first:
            cp.start()
        passed = [copy(4 + j, (*chip, c), sibling) for j, chip in enumerate(chips)]
        for j, chip in enumerate(chips):
            copy(1 + j, (*chip, c), me).wait_recv()
            passed[j].start()
        copy(0, sibling, me).wait_recv()
        for j, chip in enumerate(chips):
            copy(4 + j, (*chip, 1 - c), me).wait_recv()
        for cp in first + passed:
            cp.wait_send()
        mine.wait()

    return pl.pallas_call(
        body, name=name,
        out_shape=jax.ShapeDtypeStruct((N_DEV, r, n), v.dtype),
        in_specs=[pl.BlockSpec(memory_space=pl.ANY)],
        out_specs=pl.BlockSpec(memory_space=pl.ANY),
        scratch_shapes=[pltpu.SemaphoreType.DMA((7,)), pltpu.SemaphoreType.DMA((7,)), pltpu.SemaphoreType.DMA],
    )(v)


def _exchange_blocks(g, name):
    _, r, n = g.shape

    def body(g_ref, out_ref, send_sems, recv_sems, local_sem):
        x, y, c = lax.axis_index("x"), lax.axis_index("y"), lax.axis_index("c")
        me = 4 * x + 2 * y + c
        mine = pltpu.make_async_copy(g_ref.at[me], out_ref.at[me], local_sem)
        mine.start()
        sends = []
        for k in range(1, N_DEV):
            px, py, pc = _peer(x, y, c, k)
            cp = pltpu.make_async_remote_copy(
                src_ref=g_ref.at[4 * px + 2 * py + pc], dst_ref=out_ref.at[me], send_sem=send_sems.at[k - 1],
                recv_sem=recv_sems.at[k - 1], device_id=(px, py, pc), device_id_type=MESH)
            cp.start()
            sends.append(cp)
        for k in range(1, N_DEV):
            px, py, pc = _peer(x, y, c, k)
            pltpu.make_async_remote_copy(
                src_ref=g_ref.at[me], dst_ref=out_ref.at[4 * px + 2 * py + pc], send_sem=send_sems.at[k - 1],
                recv_sem=recv_sems.at[k - 1], device_id=(px, py, pc), device_id_type=MESH).wait_recv()
        for cp in sends:
            cp.wait_send()
        mine.wait()

    return pl.pallas_call(
        body, name=name,
        out_shape=jax.ShapeDtypeStruct(g.shape, g.dtype),
        in_specs=[pl.BlockSpec(memory_space=pl.ANY)],
        out_specs=pl.BlockSpec(memory_space=pl.ANY),
        scratch_shapes=[pltpu.SemaphoreType.DMA((N_DEV - 1,)), pltpu.SemaphoreType.DMA((N_DEV - 1,)), pltpu.SemaphoreType.DMA],
    )(g)


def _sum_blocks(g, name):
    _, r, n = g.shape
    tr = 176 if r % 176 == 0 else r

    def body(g_ref, o_ref):
        acc = g_ref[0].astype(F32)
        for d in range(1, N_DEV):
            acc = acc + g_ref[d].astype(F32)
        o_ref[...] = acc

    return pl.pallas_call(
        body, name=name, grid=(r // tr,),
        out_shape=jax.ShapeDtypeStruct((r, n), F32),
        in_specs=[pl.BlockSpec((N_DEV, tr, n), lambda i: (0, i, 0))],
        out_specs=pl.BlockSpec((tr, n), lambda i: (i, 0)),
        compiler_params=_cparams(("arbitrary",)),
    )(g)


def _ada_fwd(c_all, ada_w, ada_b_cols):
    nb, ncol = c_all.shape[0], ada_w.shape[1]

    def body(c_ref, w_ref, b_ref, mod_ref, cond_ref):
        cond = _silu(c_ref[...])
        cond_ref[...] = cond
        mod_ref[...] = _dot_hi(cond, w_ref[...]) + b_ref[...]

    return pl.pallas_call(
        body, name="ada_fwd",
        out_shape=(jax.ShapeDtypeStruct((nb, ncol), F32), jax.ShapeDtypeStruct((nb, D), F32)),
        compiler_params=_cparams(),
    )(c_all, ada_w, ada_b_cols)


def _ada_bwd(cond_all, dmod_all, dmod_cols, smalls):
    ncol, nsm = dmod_cols.shape[1], smalls.shape[1]

    def body(cond_ref, dm_ref, dmc_ref, sm_ref, gw_ref, gb_ref, gs_ref):
        gw_ref[...] = lax.dot_general(cond_ref[...], dmc_ref[...], (((0,), (0,)), ((), ())),
                                      preferred_element_type=F32, precision=HI)
        gb_ref[...] = jnp.sum(dm_ref[...], axis=0, keepdims=True)
        gs_ref[...] = jnp.sum(sm_ref[...], axis=0, keepdims=True)

    return pl.pallas_call(
        body, name="ada_bwd",
        out_shape=(jax.ShapeDtypeStruct((D, ncol), F32), jax.ShapeDtypeStruct((1, 6 * D), F32),
                   jax.ShapeDtypeStruct((1, nsm), F32)),
        compiler_params=_cparams(),
    )(cond_all, dmod_all, dmod_cols, smalls)


def _inproj_fwd(x, mod, g1, ws):
    B, S, _ = x.shape
    tm = _tile(S)
    widths = [w.shape[1] for w in ws]

    def body(x_ref, mod_ref, g_ref, *refs):
        w_refs, h_ref, o_refs = refs[:len(ws)], refs[len(ws)], refs[len(ws) + 1:]
        h = _rms_mod(x_ref[...], g_ref[...], mod_ref[1:2, :], mod_ref[0:1, :]).astype(BF16)
        h_ref[...] = h
        for w_ref, o_ref in zip(w_refs, o_refs):
            o_ref[...] = jnp.dot(h, w_ref[...], preferred_element_type=F32)

    return pl.pallas_call(
        body, name="inproj_fwd", grid=(B, S // tm),
        out_shape=[jax.ShapeDtypeStruct((B, S, D), BF16)] + [jax.ShapeDtypeStruct((B, S, w), F32) for w in widths],
        in_specs=[_rows(tm, D), _perb(6, D), _full((1, D))] + [_resident(w.shape) for w in ws],
        out_specs=[_rows(tm, D)] + [_rows(tm, w) for w in widths],
        compiler_params=_cparams(("parallel", "arbitrary")),
    )(x, mod, g1, *ws)


def _inproj_bwd(x, mod, g1, dx1, dps, ws):
    B, S, _ = x.shape
    tm = _tile(S)
    n = len(ws)

    def body(x_ref, mod_ref, g_ref, dx1_ref, *refs):
        dp_refs, w_refs = refs[:n], refs[n:2 * n]
        gx_ref, dg_ref, dsc_ref, dsh_ref = refs[2 * n:]
        b, i = pl.program_id(0), pl.program_id(1)
        dh = _dot_nt(dp_refs[0][...], w_refs[0][...])
        for dp_ref, w_ref in zip(dp_refs[1:], w_refs[1:]):
            dh = dh + _dot_nt(dp_ref[...], w_ref[...])
        _, vjp = jax.vjp(_rms_mod, x_ref[...], g_ref[...], mod_ref[1:2, :], mod_ref[0:1, :])
        dx, dg, dsc, dsh = vjp(dh)
        gx_ref[...] = dx1_ref[...] + dx

        @pl.when((b == 0) & (i == 0))
        def _():
            dg_ref[...] = jnp.zeros_like(dg_ref)

        @pl.when(i == 0)
        def _():
            dsc_ref[...] = jnp.zeros_like(dsc_ref)
            dsh_ref[...] = jnp.zeros_like(dsh_ref)

        dg_ref[...] += dg
        dsc_ref[...] += dsc
        dsh_ref[...] += dsh

    return pl.pallas_call(
        body, name="inproj_bwd", grid=(B, S // tm),
        out_shape=[jax.ShapeDtypeStruct((B, S, D), F32), jax.ShapeDtypeStruct((1, D), F32),
                   jax.ShapeDtypeStruct((B, 1, D), F32), jax.ShapeDtypeStruct((B, 1, D), F32)],
        in_specs=[_rows(tm, D), _perb(6, D), _full((1, D)), _rows(tm, D)]
                 + [_rows(tm, w.shape[1]) for w in ws] + [_resident(w.shape) for w in ws],
        out_specs=[_rows(tm, D), _full((1, D)), _perb(1, D), _perb(1, D)],
        compiler_params=_cparams(("arbitrary", "arbitrary")),
    )(x, mod, g1, dx1, *dps, *ws)


def _wgrad(a, b, name):
    B, S, K = a.shape
    N = b.shape[2]
    tm = min(512, S)
    tn = 512 if N % 512 == 0 else N
    nt = S // tm

    def body(a_ref, b_ref, o_ref):
        t = pl.program_id(1)

        @pl.when(t == 0)
        def _():
            o_ref[...] = jnp.zeros_like(o_ref)

        o_ref[...] += lax.dot_general(a_ref[...], b_ref[...], (((0,), (0,)), ((), ())), preferred_element_type=F32)

    return pl.pallas_call(
        body, name=name, grid=(N // tn, B * nt),
        out_shape=jax.ShapeDtypeStruct((K, N), F32),
        in_specs=[pl.BlockSpec((None, tm, K), lambda j, t: (t // nt, t % nt, 0)),
                  pl.BlockSpec((None, tm, tn), lambda j, t: (t // nt, t % nt, j))],
        out_specs=pl.BlockSpec((K, tn), lambda j, t: (0, j)),
        compiler_params=_cparams(("parallel", "arbitrary")),
    )(a, b)


LANES = 128


def _attn_consts():
    inv_freq = THETA ** (-jnp.arange(0, ROT, 2, dtype=F32) / ROT)
    head = jnp.concatenate([inv_freq, inv_freq, jnp.zeros((HD - ROT,), F32)])
    invf = jnp.tile(head, LANES // HD)[None, :]
    mean_of = lambda w: jnp.asarray(np.kron(np.eye(w // HD), np.full((HD, HD), 1.0 / HD)), BF16)
    return invf, mean_of(QW), mean_of(KVW)


def _rope_tables(pos, invf):
    B, S, _ = pos.shape
    tr = min(1024, S)

    def body(p_ref, f_ref, c_ref, s_ref):
        ang = p_ref[...].astype(F32) * f_ref[...]
        c_ref[...] = jnp.cos(ang)
        s_ref[...] = jnp.sin(ang)

    sd = jax.ShapeDtypeStruct((B, S, LANES), F32)
    return pl.pallas_call(
        body, name="rope_tables", grid=(B, S // tr), out_shape=[sd, sd],
        in_specs=[_rows(tr, 1), _full((1, LANES))], out_specs=[_rows(tr, LANES), _rows(tr, LANES)],
        compiler_params=_cparams(("parallel", "parallel")),
    )(pos, invf)


def _rope_expand(cos, sin, reps):
    lane = lax.broadcasted_iota(jnp.int32, cos.shape, 1) % HD
    sa = jnp.where((lane >= ROT // 2) & (lane < ROT), sin, 0.0)
    sb = jnp.where(lane < ROT // 2, -sin, 0.0)
    rep = lambda t: jnp.concatenate([t] * reps, axis=1) if reps > 1 else t
    return rep(cos), rep(sa), rep(sb)


@jax.custom_vjp
def _rope(t, cos, sa, sb):
    w = t.shape[1]
    return t * cos + pltpu.roll(t, ROT // 2, 1) * sa + pltpu.roll(t, w - ROT // 2, 1) * sb


def _rope_fwd(t, cos, sa, sb):
    return _rope(t, cos, sa, sb), (cos, sa, sb)


def _rope_bwd(res, d):
    cos, sa, sb = res
    w = d.shape[1]
    dt = d * cos + pltpu.roll(d * sa, w - ROT // 2, 1) + pltpu.roll(d * sb, ROT // 2, 1)
    return dt, jnp.zeros_like(cos), jnp.zeros_like(sa), jnp.zeros_like(sb)


_rope.defvjp(_rope_fwd, _rope_bwd)


def _head_norm(t, g, mean_of):
    hi, lo = _split(t * t)
    ms = jnp.dot(hi, mean_of, preferred_element_type=F32) + jnp.dot(lo, mean_of, preferred_element_type=F32)
    return t * lax.rsqrt(ms + EPS) * g


def _attn_block(q, kvp, kvc, qg, kg, sinks, tq, tk, mq, mk, valid):
    qn = _rope(_head_norm(q, jnp.concatenate([qg] * HQ, axis=1), mq), *tq)
    kv = jnp.concatenate([kvp, kvc], axis=0)
    kn = _rope(_head_norm(kv[:, 0:KVW], jnp.concatenate([kg] * HKV, axis=1), mk), *tk)
    q4 = jnp.stack([jnp.concatenate([qn[:, HD * (GRP * j + i):HD * (GRP * j + i + 1)] for i in range(GRP)], axis=0)
                    for j in range(HKV)])
    k2 = jnp.stack([kn[:, HD * j:HD * (j + 1)] for j in range(HKV)])
    v2 = jnp.stack([kv[:, KVW + HD * j:KVW + HD * (j + 1)] for j in range(HKV)])
    rowblk = lax.broadcasted_iota(jnp.int32, (GRP * BLK, 1), 0) // BLK
    sink = jnp.stack([sum(jnp.where(rowblk == i, sinks[:, GRP * j + i:GRP * j + i + 1], 0.0) for i in range(GRP))
                      for j in range(HKV)])
    s = _bmm(q4, k2, _BMM_NT) * (HD ** -0.5)
    s = jnp.where(valid[None], s, -1e30)
    m = jnp.maximum(jnp.max(s, axis=-1, keepdims=True), sink)
    p = jnp.exp(s - m)
    probs = p / (jnp.sum(p, axis=-1, keepdims=True) + jnp.exp(sink - m))
    o4 = _bmm(probs, v2)
    return jnp.concatenate([o4[j, BLK * i:BLK * (i + 1), :] for j in range(HKV) for i in range(GRP)], axis=1)


def _attn_tables(cp_ref, cc_ref, sp_ref, sc_ref, n):
    tq = _rope_expand(cc_ref[...], sc_ref[...], QW // LANES)
    tk = _rope_expand(jnp.concatenate([cp_ref[...], cc_ref[...]], axis=0),
                      jnp.concatenate([sp_ref[...], sc_ref[...]], axis=0), KVW // LANES)
    qi = lax.broadcasted_iota(jnp.int32, (GRP * BLK, 2 * BLK), 0) % BLK + BLK
    kj = lax.broadcasted_iota(jnp.int32, (GRP * BLK, 2 * BLK), 1)
    dist = qi - kj
    valid = (dist >= 0) & (dist < BLK) & ((kj >= BLK) | (n > 0))
    return tq, tk, valid


def _attn_fwd(aq, akv, cos, sin, qg, kg, sinks, mq, mk):
    B, S, _ = aq.shape
    nb = S // BLK

    def body(q_ref, kvp_ref, kvc_ref, cp_ref, cc_ref, sp_ref, sc_ref, qg_ref, kg_ref, sk_ref, mq_ref, mk_ref, o_ref):
        tq, tk, valid = _attn_tables(cp_ref, cc_ref, sp_ref, sc_ref, pl.program_id(1))
        o_ref[...] = _attn_block(q_ref[...], kvp_ref[...], kvc_ref[...], qg_ref[...], kg_ref[...], sk_ref[...],
                                 tq, tk, mq_ref[...], mk_ref[...], valid)

    prev = lambda b, n: (b, jnp.maximum(n - 1, 0), 0)
    cur = lambda b, n: (b, n, 0)
    return pl.pallas_call(
        body, name="attn_fwd", grid=(B, nb),
        out_shape=jax.ShapeDtypeStruct((B, S, QW), F32),
        in_specs=[pl.BlockSpec((None, BLK, QW), cur), pl.BlockSpec((None, BLK, 2 * KVW), prev),
                  pl.BlockSpec((None, BLK, 2 * KVW), cur), pl.BlockSpec((None, BLK, LANES), prev),
                  pl.BlockSpec((None, BLK, LANES), cur), pl.BlockSpec((None, BLK, LANES), prev),
                  pl.BlockSpec((None, BLK, LANES), cur), _full((1, HD)), _full((1, HD)), _full((1, HQ)),
                  _full((QW, QW)), _full((KVW, KVW))],
        out_specs=pl.BlockSpec((None, BLK, QW), cur),
        compiler_params=_cparams(("parallel", "arbitrary")),
    )(aq, akv, akv, cos, cos, sin, sin, qg, kg, sinks, mq, mk)


def _attn_bwd(aq, akv, cos, sin, qg, kg, sinks, mq, mk, do):
    B, S, _ = aq.shape
    nb = S // BLK

    def body(q_ref, kvp_ref, kvc_ref, cp_ref, cc_ref, sp_ref, sc_ref, qg_ref, kg_ref, sk_ref, mq_ref, mk_ref, do_ref,
             dq_ref, dkv_ref, dqg_ref, dkg_ref, dsk_ref, carry):
        b, i = pl.program_id(0), pl.program_id(1)
        tq, tk, valid = _attn_tables(cp_ref, cc_ref, sp_ref, sc_ref, nb - 1 - i)
        fn = functools.partial(_attn_block, tq=tq, tk=tk, mq=mq_ref[...], mk=mk_ref[...], valid=valid)
        _, vjp = jax.vjp(fn, q_ref[...], kvp_ref[...], kvc_ref[...], qg_ref[...], kg_ref[...], sk_ref[...])
        dq, dkvp, dkvc, dqg, dkg, dsk = vjp(do_ref[...])

        @pl.when(i == 0)
        def _():
            carry[...] = jnp.zeros_like(carry)

        @pl.when((b == 0) & (i == 0))
        def _():
            dqg_ref[...] = jnp.zeros_like(dqg_ref)
            dkg_ref[...] = jnp.zeros_like(dkg_ref)
            dsk_ref[...] = jnp.zeros_like(dsk_ref)

        dq_ref[...] = dq.astype(BF16)
        dkv_ref[...] = (dkvc + carry[...]).astype(BF16)
        carry[...] = dkvp
        dqg_ref[...] += dqg
        dkg_ref[...] += dkg
        dsk_ref[...] += dsk

    prev = lambda b, i: (b, jnp.maximum(nb - 2 - i, 0), 0)
    cur = lambda b, i: (b, nb - 1 - i, 0)
    return pl.pallas_call(
        body, name="attn_bwd", grid=(B, nb),
        out_shape=[jax.ShapeDtypeStruct((B, S, QW), BF16), jax.ShapeDtypeStruct((B, S, 2 * KVW), BF16),
                   jax.ShapeDtypeStruct((1, HD), F32), jax.ShapeDtypeStruct((1, HD), F32),
                   jax.ShapeDtypeStruct((1, HQ), F32)],
        in_specs=[pl.BlockSpec((None, BLK, QW), cur), pl.BlockSpec((None, BLK, 2 * KVW), prev),
                  pl.BlockSpec((None, BLK, 2 * KVW), cur), pl.BlockSpec((None, BLK, LANES), prev),
                  pl.BlockSpec((None, BLK, LANES), cur), pl.BlockSpec((None, BLK, LANES), prev),
                  pl.BlockSpec((None, BLK, LANES), cur), _full((1, HD)), _full((1, HD)), _full((1, HQ)),
                  _full((QW, QW)), _full((KVW, KVW)), pl.BlockSpec((None, BLK, QW), cur)],
        out_specs=[pl.BlockSpec((None, BLK, QW), cur), pl.BlockSpec((None, BLK, 2 * KVW), cur),
                   _full((1, HD)), _full((1, HD)), _full((1, HQ))],
        scratch_shapes=[pltpu.VMEM((BLK, 2 * KVW), F32)],
        compiler_params=_cparams(("arbitrary", "arbitrary")),
    )(aq, akv, akv, cos, cos, sin, sin, qg, kg, sinks, mq, mk, do)


def _conv_taps(xe, w, rows):
    y = None
    for j in range(CONV):
        sh = pltpu.roll(xe, CONV - 1 - j, 0)[8:8 + rows, :] if j < CONV - 1 else xe[8:8 + rows, :]
        y = sh * w[j:j + 1, :] if y is None else y + sh * w[j:j + 1, :]
    return y


def _conv_fwd(xin, w):
    B, S, C = xin.shape
    tc = min(512, S)
    r8 = tc // 8

    def body(xp_ref, x_ref, w_ref, o_ref):
        i = pl.program_id(1)
        xp = jnp.where(i > 0, xp_ref[...], 0.0)
        xe = jnp.concatenate([xp, x_ref[...]], axis=0)
        o_ref[...] = _silu(_conv_taps(xe, w_ref[...], tc))

    return pl.pallas_call(
        body, name="conv_fwd", grid=(B, S // tc),
        out_shape=jax.ShapeDtypeStruct((B, S, C), F32),
        in_specs=[pl.BlockSpec((None, 8, C), lambda b, i: (b, jnp.maximum(i * r8 - 1, 0), 0)),
                  _rows(tc, C), _full((CONV, C))],
        out_specs=_rows(tc, C),
        compiler_params=_cparams(("parallel", "arbitrary")),
    )(xin, xin, w)


def _conv_bwd(xin, w, dy):
    B, S, C = xin.shape
    tc = min(512, S)
    r8 = tc // 8
    nt = S // tc

    def body(xp_ref, x_ref, xn_ref, dy_ref, dyn_ref, w_ref, dx_ref, dw_ref):
        b, i = pl.program_id(0), pl.program_id(1)
        w = w_ref[...]
        xp = jnp.where(i > 0, xp_ref[...], 0.0)
        xe = jnp.concatenate([xp, x_ref[...], xn_ref[...]], axis=0)
        pre = _conv_taps(xe, w, tc + 8)
        sg = _sigmoid(pre)
        dyn = jnp.where(i < nt - 1, dyn_ref[...], 0.0)
        dpre = jnp.concatenate([dy_ref[...], dyn], axis=0) * (sg * (1.0 + pre * (1.0 - sg)))
        dx = dpre[0:tc, :] * w[CONV - 1:CONV, :]
        for j in range(CONV - 1):
            dx = dx + pltpu.roll(dpre, tc + 8 - (CONV - 1 - j), 0)[0:tc, :] * w[j:j + 1, :]
        dx_ref[...] = dx.astype(BF16)
        dcur = dpre[0:tc, :]
        xe0 = xe[0:8 + tc, :]
        lane_row = lax.broadcasted_iota(jnp.int32, (CONV, C), 0)
        dw = jnp.zeros((CONV, C), F32)
        for j in range(CONV):
            sh = pltpu.roll(xe0, CONV - 1 - j, 0)[8:8 + tc, :] if j < CONV - 1 else xe0[8:8 + tc, :]
            dw = dw + jnp.where(lane_row == j, jnp.sum(sh * dcur, axis=0, keepdims=True), 0.0)

        @pl.when((b == 0) & (i == 0))
        def _():
            dw_ref[...] = jnp.zeros_like(dw_ref)

        dw_ref[...] += dw

    return pl.pallas_call(
        body, name="conv_bwd", grid=(B, nt),
        out_shape=[jax.ShapeDtypeStruct((B, S, C), BF16), jax.ShapeDtypeStruct((CONV, C), F32)],
        in_specs=[pl.BlockSpec((None, 8, C), lambda b, i: (b, jnp.maximum(i * r8 - 1, 0), 0)),
                  _rows(tc, C),
                  pl.BlockSpec((None, 8, C), lambda b, i: (b, jnp.minimum((i + 1) * r8, S // 8 - 1), 0)),
                  _rows(tc, C),
                  pl.BlockSpec((None, 8, C), lambda b, i: (b, jnp.minimum((i + 1) * r8, S // 8 - 1), 0)),
                  _full((CONV, C))],
        out_specs=[_rows(tc, C), _full((CONV, C))],
        compiler_params=_cparams(("arbitrary", "arbitrary")),
    )(xin, xin, xin, dy, dy, w)


def _softplus(x):
    return jnp.maximum(x, 0.0) + jnp.log1p(jnp.exp(-jnp.abs(x)))


_BMM = (((2,), (1,)), ((0,), (0,)))
_BMM_NT = (((2,), (2,)), ((0,), (0,)))
_BMM_TN = (((1,), (1,)), ((0,), (0,)))


def _bmm(a, b, dims=_BMM):
    return lax.dot_general(a.astype(BF16), b.astype(BF16), dims, preferred_element_type=F32)


def _split(a):
    hi = a.astype(BF16)
    return hi, (a - hi.astype(F32)).astype(BF16)


def _bmm3(a, b, dims=_BMM):
    ah, al = _split(a)
    bh, bl = _split(b)
    d = lambda p, q: lax.dot_general(p, q, dims, preferred_element_type=F32)
    return d(ah, bh) + (d(ah, bl) + d(al, bh))


def _tri_inverse(L):
    eye = (lax.broadcasted_iota(jnp.int32, (CH, CH), 0) == lax.broadcasted_iota(jnp.int32, (CH, CH), 1)).astype(F32)
    T = eye - L
    P = L
    n = 2
    while n < CH:
        P = _bmm3(P, P)
        T = T + _bmm3(T, P)
        n *= 2
    return T


@jax.custom_vjp
def _tri_inverse_known(L, T):
    return T


def _tri_inverse_known_fwd(L, T):
    return T, T


def _tri_inverse_known_bwd(T, dT):
    return -_bmm3(T, _bmm3(dT, T, _BMM_NT), _BMM_TN), jnp.zeros_like(T)


_tri_inverse_known.defvjp(_tri_inverse_known_fwd, _tri_inverse_known_bwd)


def _cumsum_rows(g):
    n = g.shape[0]
    ii = lax.broadcasted_iota(jnp.int32, (n, CH, CH), 1)
    jj = lax.broadcasted_iota(jnp.int32, (n, CH, CH), 2)
    tri = (ii >= jj).astype(BF16)
    g0 = g.astype(BF16)
    r1 = g - g0.astype(F32)
    g1 = r1.astype(BF16)
    g2 = (r1 - g1.astype(F32)).astype(BF16)
    d = lambda q: lax.dot_general(tri, q, _BMM, preferred_element_type=F32)
    return d(g0) + (d(g1) + d(g2))


def _dn_prep(t_known, qr, kr, v, a_raw, b_raw, a_log, dt_b):
    n = qr.shape[0]
    ii = lax.broadcasted_iota(jnp.int32, (n, CH, CH), 1)
    jj = lax.broadcasted_iota(jnp.int32, (n, CH, CH), 2)
    incl, strict = ii >= jj, ii > jj
    q = qr * lax.rsqrt(jnp.sum(qr * qr, axis=-1, keepdims=True) + EPS) * (DK ** -0.5)
    k = kr * lax.rsqrt(jnp.sum(kr * kr, axis=-1, keepdims=True) + EPS)
    beta = _sigmoid(b_raw)
    g = -jnp.exp(a_log) * _softplus(a_raw + dt_b)
    gcb = _cumsum_rows(jnp.broadcast_to(g, (n, CH, DK)))
    gc = gcb[:, :, 0:1]
    gc_row = jnp.swapaxes(gcb, 1, 2)[:, 0:1, 0:CH]
    decay = jnp.where(incl, jnp.exp(jnp.where(incl, gc - gc_row, 0.0)), 0.0)
    kb = k * beta
    L = jnp.where(strict, _bmm(kb, k, _BMM_NT) * decay, 0.0)
    T = _tri_inverse(L) if t_known is None else _tri_inverse_known(L, t_known)
    eg = jnp.exp(gc)
    u = _bmm(T, v * beta)
    w = _bmm(T, kb * eg)
    a_in = _bmm(q, k, _BMM_NT) * decay
    g_last = gc[:, CH - 1:CH, :]
    return u, w, q * eg, k * jnp.exp(g_last - gc), a_in, jnp.exp(g_last), T


def _dn_step(S0, u, w, qd, kd, a_in, cd):
    r = _bmm(jnp.concatenate([w, qd], axis=1), S0)
    v_new = u - r[:, 0:CH, :]
    o = r[:, CH:2 * CH, :] + _bmm(a_in, v_new)
    S1 = S0 * cd + _bmm(kd, v_new, _BMM_TN)
    return o, S1


def _dn_stack(cq, ba, al, dt, G):
    cols = [[] for _ in range(7)]
    for c in range(G):
        rows = slice(CH * c, CH * (c + 1))
        for h in range(DH):
            parts = (cq[rows, DK * h:DK * (h + 1)], cq[rows, DNW + DK * h:DNW + DK * (h + 1)],
                     cq[rows, 2 * DNW + DK * h:2 * DNW + DK * (h + 1)], ba[rows, DH + h:DH + h + 1],
                     ba[rows, h:h + 1], al[:, h:h + 1], dt[:, h:h + 1])
            for col, p in zip(cols, parts):
                col.append(p)
    return tuple(jnp.stack(col) for col in cols)


def _dn_group(S, want):
    g = want
    while (S // CH) % g:
        g //= 2
    return g


def _dn_prep_fwd(cq, ba, a_log, dt_b):
    B, S, _ = cq.shape
    nc = S // CH
    G = _dn_group(S, 4)

    def body(cq_ref, ba_ref, al_ref, dt_ref, u_ref, w_ref, qd_ref, kd_ref, a_ref, t_ref, cd_ref):
        ops = _dn_stack(cq_ref[...], ba_ref[...], al_ref[...], dt_ref[...], G)
        u, w, qd, kd, a_in, cd, T = _dn_prep(None, *ops)
        lane4 = lax.broadcasted_iota(jnp.int32, (1, DH), 1)
        for c in range(G):
            rows = slice(CH * c, CH * (c + 1))
            cdrow = jnp.zeros((1, DH), F32)
            for h in range(DH):
                n = DH * c + h
                lanes = slice(DK * h, DK * (h + 1))
                u_ref[rows, lanes] = u[n]
                w_ref[rows, lanes] = w[n]
                qd_ref[rows, lanes] = qd[n]
                kd_ref[rows, lanes] = kd[n]
                a_ref[rows, CH * h:CH * (h + 1)] = a_in[n]
                t_ref[rows, CH * h:CH * (h + 1)] = T[n]
                cdrow = cdrow + jnp.where(lane4 == h, cd[n], 0.0)
            cd_ref[c] = cdrow

    wide = jax.ShapeDtypeStruct((B, S, DNW), F32)
    sq = jax.ShapeDtypeStruct((B, S, DH * CH), F32)
    return pl.pallas_call(
        body, name="dn_prep_fwd", grid=(B, nc // G),
        out_shape=[wide, wide, wide, wide, sq, sq, jax.ShapeDtypeStruct((B, nc, 1, DH), F32)],
        in_specs=[_rows(G * CH, CONVW), _rows(G * CH, 2 * DH), _full((1, DH)), _full((1, DH))],
        out_specs=[_rows(G * CH, DNW)] * 4 + [_rows(G * CH, DH * CH)] * 2
                  + [pl.BlockSpec((None, G, 1, DH), lambda b, i: (b, i, 0, 0))],
        compiler_params=_cparams(("parallel", "parallel")),
    )(cq, ba, a_log, dt_b)


def _dn_seq_specs(B, nc, rev):
    at = (lambda i: nc - 1 - i) if rev else (lambda i: i)
    wide = pl.BlockSpec((B, CH, DNW), lambda i: (0, at(i), 0))
    a_spec = pl.BlockSpec((B, CH, DH * CH), lambda i: (0, at(i), 0))
    cd_spec = pl.BlockSpec((B, None, 1, DH), lambda i: (0, at(i), 0, 0))
    st_spec = pl.BlockSpec((B, None, DH, DK, DK), lambda i: (0, at(i), 0, 0, 0))
    return wide, a_spec, cd_spec, st_spec


def _dn_step_operands(B, u_ref, w_ref, qd_ref, kd_ref, a_ref, cd_ref):
    pairs = [(b, h) for b in range(B) for h in range(DH)]
    wide = lambda ref: jnp.stack([ref[b, :, DK * h:DK * (h + 1)] for b, h in pairs])
    a_in = jnp.stack([a_ref[b, :, CH * h:CH * (h + 1)] for b, h in pairs])
    cd = jnp.stack([cd_ref[b, :, h:h + 1] for b, h in pairs])
    return wide(u_ref), wide(w_ref), wide(qd_ref), wide(kd_ref), a_in, cd


def _dn_seq_fwd(u, w, qd, kd, a_in, cd):
    B, S, _ = u.shape
    nc = S // CH

    def body(u_ref, w_ref, qd_ref, kd_ref, a_ref, cd_ref, o_ref, st_ref, state):
        @pl.when(pl.program_id(0) == 0)
        def _():
            state[...] = jnp.zeros_like(state)

        S0 = state[...]
        for b in range(B):
            st_ref[b] = S0[DH * b:DH * (b + 1)]
        o, S1 = _dn_step(S0, *_dn_step_operands(B, u_ref, w_ref, qd_ref, kd_ref, a_ref, cd_ref))
        state[...] = S1
        for b in range(B):
            for h in range(DH):
                o_ref[b, :, DK * h:DK * (h + 1)] = o[DH * b + h]

    wide, a_spec, cd_spec, st_spec = _dn_seq_specs(B, nc, False)
    return pl.pallas_call(
        body, name="dn_seq_fwd", grid=(nc,),
        out_shape=[jax.ShapeDtypeStruct((B, S, DNW), F32), jax.ShapeDtypeStruct((B, nc, DH, DK, DK), F32)],
        in_specs=[wide, wide, wide, wide, a_spec, cd_spec],
        out_specs=[wide, st_spec],
        scratch_shapes=[pltpu.VMEM((B * DH, DK, DK), F32)],
        compiler_params=_cparams(("arbitrary",)),
    )(u, w, qd, kd, a_in, cd)


def _dn_seq_bwd(u, w, qd, kd, a_in, cd, states, do):
    B, S, _ = u.shape
    nc = S // CH

    def body(u_ref, w_ref, qd_ref, kd_ref, a_ref, cd_ref, st_ref, do_ref,
             du_ref, dw_ref, dqd_ref, dkd_ref, da_ref, dcd_ref, dstate):
        @pl.when(pl.program_id(0) == 0)
        def _():
            dstate[...] = jnp.zeros_like(dstate)

        lane4 = lax.broadcasted_iota(jnp.int32, (1, DH), 1)
        S0 = jnp.concatenate([st_ref[b] for b in range(B)], axis=0)
        do = jnp.stack([do_ref[b, :, DK * h:DK * (h + 1)] for b in range(B) for h in range(DH)])
        _, vjp = jax.vjp(_dn_step, S0, *_dn_step_operands(B, u_ref, w_ref, qd_ref, kd_ref, a_ref, cd_ref))
        dS, du, dw, dqd, dkd, da, dcd = vjp((do, dstate[...]))
        dstate[...] = dS
        for b in range(B):
            dcdrow = jnp.zeros((1, DH), F32)
            for h in range(DH):
                n = DH * b + h
                lanes = slice(DK * h, DK * (h + 1))
                du_ref[b, :, lanes] = du[n]
                dw_ref[b, :, lanes] = dw[n]
                dqd_ref[b, :, lanes] = dqd[n]
                dkd_ref[b, :, lanes] = dkd[n]
                da_ref[b, :, CH * h:CH * (h + 1)] = da[n]
                dcdrow = dcdrow + jnp.where(lane4 == h, dcd[n], 0.0)
            dcd_ref[b] = dcdrow

    wide, a_spec, cd_spec, st_spec = _dn_seq_specs(B, nc, True)
    sd = jax.ShapeDtypeStruct((B, S, DNW), F32)
    return pl.pallas_call(
        body, name="dn_seq_bwd", grid=(nc,),
        out_shape=[sd, sd, sd, sd, jax.ShapeDtypeStruct((B, S, DH * CH), F32), jax.ShapeDtypeStruct((B, nc, 1, DH), F32)],
        in_specs=[wide, wide, wide, wide, a_spec, cd_spec, st_spec, wide],
        out_specs=[wide, wide, wide, wide, a_spec, cd_spec],
        scratch_shapes=[pltpu.VMEM((B * DH, DK, DK), F32)],
        compiler_params=_cparams(("arbitrary",)),
    )(u, w, qd, kd, a_in, cd, states, do)


def _dn_prep_bwd(cq, ba, a_log, dt_b, t_inv, du, dw, dqd, dkd, da, dcd):
    B, S, _ = cq.shape
    nc = S // CH
    G = _dn_group(S, 4)

    def body(cq_ref, ba_ref, al_ref, dt_ref, t_ref, du_ref, dw_ref, dqd_ref, dkd_ref, da_ref, dcd_ref,
             dcq_ref, dba_ref, dal_ref, ddt_ref):
        @pl.when((pl.program_id(0) == 0) & (pl.program_id(1) == 0))
        def _():
            dal_ref[...] = jnp.zeros_like(dal_ref)
            ddt_ref[...] = jnp.zeros_like(ddt_ref)

        pairs = [(c, h) for c in range(G) for h in range(DH)]
        rows = lambda c: slice(CH * c, CH * (c + 1))
        wide = lambda ref: jnp.stack([ref[rows(c), DK * h:DK * (h + 1)] for c, h in pairs])
        square = lambda ref: jnp.stack([ref[rows(c), CH * h:CH * (h + 1)] for c, h in pairs])
        ops = _dn_stack(cq_ref[...], ba_ref[...], al_ref[...], dt_ref[...], G)
        cots = (wide(du_ref), wide(dw_ref), wide(dqd_ref), wide(dkd_ref), square(da_ref),
                jnp.stack([dcd_ref[c][:, h:h + 1] for c, h in pairs]), jnp.zeros((len(pairs), CH, CH), F32))
        _, vjp = jax.vjp(functools.partial(_dn_prep, square(t_ref)), *ops)
        dq, dk, dv, dar, dbr, dl, dd = vjp(cots)
        lane8 = lax.broadcasted_iota(jnp.int32, (CH, 2 * DH), 1)
        lane4 = lax.broadcasted_iota(jnp.int32, (1, DH), 1)
        dal = jnp.zeros((1, DH), F32)
        ddt = jnp.zeros((1, DH), F32)
        for c in range(G):
            dba = jnp.zeros((CH, 2 * DH), F32)
            for h in range(DH):
                n = DH * c + h
                dcq_ref[rows(c), DK * h:DK * (h + 1)] = dq[n]
                dcq_ref[rows(c), DNW + DK * h:DNW + DK * (h + 1)] = dk[n]
                dcq_ref[rows(c), 2 * DNW + DK * h:2 * DNW + DK * (h + 1)] = dv[n]
                dba = dba + jnp.where(lane8 == h, dbr[n], 0.0) + jnp.where(lane8 == DH + h, dar[n], 0.0)
                dal = dal + jnp.where(lane4 == h, dl[n], 0.0)
                ddt = ddt + jnp.where(lane4 == h, dd[n], 0.0)
            dba_ref[rows(c), :] = dba.astype(BF16)
        dal_ref[...] += dal
        ddt_ref[...] += ddt

    return pl.pallas_call(
        body, name="dn_prep_bwd", grid=(B, nc // G),
        out_shape=[jax.ShapeDtypeStruct((B, S, CONVW), F32), jax.ShapeDtypeStruct((B, S, 2 * DH), BF16),
                   jax.ShapeDtypeStruct((1, DH), F32), jax.ShapeDtypeStruct((1, DH), F32)],
        in_specs=[_rows(G * CH, CONVW), _rows(G * CH, 2 * DH), _full((1, DH)), _full((1, DH)), _rows(G * CH, DH * CH)]
                 + [_rows(G * CH, DNW)] * 4 + [_rows(G * CH, DH * CH),
                                               pl.BlockSpec((None, G, 1, DH), lambda b, i: (b, i, 0, 0))],
        out_specs=[_rows(G * CH, CONVW), _rows(G * CH, 2 * DH), _full((1, DH)), _full((1, DH))],
        compiler_params=_cparams(("arbitrary", "arbitrary")),
    )(cq, ba, a_log, dt_b, t_inv, du, dw, dqd, dkd, da, dcd)


def _gated_norm(o, z, g):
    outs = []
    for h in range(DH):
        t = o[:, DK * h:DK * (h + 1)]
        r = lax.rsqrt(jnp.mean(t * t, axis=-1, keepdims=True) + EPS)
        outs.append(t * r * g * _silu(z[:, DK * h:DK * (h + 1)]))
    return jnp.concatenate(outs, axis=1)


def _mix_fwd(x, o_attn, o_dn, z, ga, gd, mod, dn_g, w_branch, w_out):
    B, S, _ = x.shape
    tm = _tile(S)

    def body(x_ref, oa_ref, od_ref, z_ref, ga_ref, gd_ref, mod_ref, g_ref, wb_ref, wo_ref,
             x1_ref, mix_ref, mg_ref, oab_ref, odb_ref):
        oa = oa_ref[...].astype(BF16)
        od = _gated_norm(od_ref[...], z_ref[...], g_ref[...]).astype(BF16)
        oab_ref[...] = oa
        odb_ref[...] = od
        ya = jnp.dot(oa, wb_ref[0:QW, :], preferred_element_type=F32)
        yd = jnp.dot(od, wb_ref[QW:QW + DNW, :], preferred_element_type=F32)
        merged = (_sigmoid(ga_ref[...]) * ya + _sigmoid(gd_ref[...]) * yd).astype(BF16)
        mg_ref[...] = merged
        mix = jnp.dot(merged, wo_ref[...], preferred_element_type=F32)
        mix_ref[...] = mix
        x1_ref[...] = x_ref[...] + mod_ref[2:3, :] * mix

    return pl.pallas_call(
        body, name="mix_fwd", grid=(B, S // tm),
        out_shape=[jax.ShapeDtypeStruct((B, S, D), F32), jax.ShapeDtypeStruct((B, S, D), F32),
                   jax.ShapeDtypeStruct((B, S, D), BF16), jax.ShapeDtypeStruct((B, S, QW), BF16),
                   jax.ShapeDtypeStruct((B, S, DNW), BF16)],
        in_specs=[_rows(tm, D), _rows(tm, QW), _rows(tm, DNW), _rows(tm, DNW), _rows(tm, D), _rows(tm, D),
                  _perb(6, D), _full((1, DK)), _resident(w_branch.shape), _resident(w_out.shape)],
        out_specs=[_rows(tm, D), _rows(tm, D), _rows(tm, D), _rows(tm, QW), _rows(tm, DNW)],
        compiler_params=_cparams(("parallel", "arbitrary")),
    )(x, o_attn, o_dn, z, ga, gd, mod, dn_g, w_branch, w_out)


def _mix_bwd(dx1, mix, o_attn, o_dn, z, ga, gd, mod, dn_g, w_branch, w_out):
    B, S, _ = dx1.shape
    tm = _tile(S)

    def body(dx1_ref, mix_ref, oa_ref, od_ref, z_ref, ga_ref, gd_ref, mod_ref, g_ref, wb_ref, wo_ref,
             dmix_ref, dya_ref, dyd_ref, dga_ref, dgd_ref, dz_ref, doa_ref, dod_ref, dgate_ref, dg_ref):
        b, i = pl.program_id(0), pl.program_id(1)
        dx1 = dx1_ref[...]
        dmix = (dx1 * mod_ref[2:3, :]).astype(BF16)
        dmix_ref[...] = dmix
        dgate = jnp.sum(dx1 * mix_ref[...], axis=0, keepdims=True)
        dmerged = _dot_nt(dmix, wo_ref[...])
        odn, gn_vjp = jax.vjp(_gated_norm, od_ref[...], z_ref[...], g_ref[...])
        ya = _dot(oa_ref[...], wb_ref[0:QW, :])
        yd = _dot(odn, wb_ref[QW:QW + DNW, :])
        sa, sd = _sigmoid(ga_ref[...]), _sigmoid(gd_ref[...])
        dya = (dmerged * sa).astype(BF16)
        dyd = (dmerged * sd).astype(BF16)
        dya_ref[...] = dya
        dyd_ref[...] = dyd
        dga_ref[...] = (dmerged * ya * sa * (1.0 - sa)).astype(BF16)
        dgd_ref[...] = (dmerged * yd * sd * (1.0 - sd)).astype(BF16)
        doa_ref[...] = _dot_nt(dya, wb_ref[0:QW, :])
        dodn = _dot_nt(dyd, wb_ref[QW:QW + DNW, :])
        dod, dz, dg = gn_vjp(dodn)
        dod_ref[...] = dod
        dz_ref[...] = dz.astype(BF16)

        @pl.when(i == 0)
        def _():
            dgate_ref[...] = jnp.zeros_like(dgate_ref)

        @pl.when((b == 0) & (i == 0))
        def _():
            dg_ref[...] = jnp.zeros_like(dg_ref)

        dgate_ref[...] += dgate
        dg_ref[...] += dg

    return pl.pallas_call(
        body, name="mix_bwd", grid=(B, S // tm),
        out_shape=[jax.ShapeDtypeStruct((B, S, D), BF16), jax.ShapeDtypeStruct((B, S, D), BF16),
                   jax.ShapeDtypeStruct((B, S, D), BF16), jax.ShapeDtypeStruct((B, S, D), BF16),
                   jax.ShapeDtypeStruct((B, S, D), BF16), jax.ShapeDtypeStruct((B, S, DNW), BF16),
                   jax.ShapeDtypeStruct((B, S, QW), F32), jax.ShapeDtypeStruct((B, S, DNW), F32),
                   jax.ShapeDtypeStruct((B, 1, D), F32), jax.ShapeDtypeStruct((1, DK), F32)],
        in_specs=[_rows(tm, D), _rows(tm, D), _rows(tm, QW), _rows(tm, DNW), _rows(tm, DNW), _rows(tm, D),
                  _rows(tm, D), _perb(6, D), _full((1, DK)), _resident(w_branch.shape), _resident(w_out.shape)],
        out_specs=[_rows(tm, D), _rows(tm, D), _rows(tm, D), _rows(tm, D), _rows(tm, D), _rows(tm, DNW),
                   _rows(tm, QW), _rows(tm, DNW), _perb(1, D), _full((1, DK))],
        compiler_params=_cparams(("arbitrary", "arbitrary")),
    )(dx1, mix, o_attn, o_dn, z, ga, gd, mod, dn_g, w_branch, w_out)


def _ffn1_fwd(x1, mod, g2, w_gu):
    B, S, _ = x1.shape
    tm = _tile(S)

    def body(x_ref, mod_ref, g_ref, w_ref, h_ref, gate_ref, up_ref, act_ref):
        h = _rms_mod(x_ref[...], g_ref[...], mod_ref[4:5, :], mod_ref[3:4, :]).astype(BF16)
        h_ref[...] = h
        gate = jnp.dot(h, w_ref[:, 0:FFN], preferred_element_type=F32)
        up = jnp.dot(h, w_ref[:, FFN:2 * FFN], preferred_element_type=F32)
        gate_ref[...] = gate
        up_ref[...] = up
        act_ref[...] = (_silu(gate) * up).astype(BF16)

    return pl.pallas_call(
        body, name="ffn1_fwd", grid=(B, S // tm),
        out_shape=[jax.ShapeDtypeStruct((B, S, D), BF16), jax.ShapeDtypeStruct((B, S, FFN), F32),
                   jax.ShapeDtypeStruct((B, S, FFN), F32), jax.ShapeDtypeStruct((B, S, FFN), BF16)],
        in_specs=[_rows(tm, D), _perb(6, D), _full((1, D)), _resident(w_gu.shape)],
        out_specs=[_rows(tm, D), _rows(tm, FFN), _rows(tm, FFN), _rows(tm, FFN)],
        compiler_params=_cparams(("parallel", "arbitrary")),
    )(x1, mod, g2, w_gu)


def _ffn2_fwd(act, x1, target, mod, w_down):
    B, S, _ = x1.shape
    tm = _tile(S)

    def body(a_ref, x_ref, t_ref, mod_ref, w_ref, dy_ref, loss_ref, dgate_ref):
        b, i = pl.program_id(0), pl.program_id(1)
        y = jnp.dot(a_ref[...], w_ref[...], preferred_element_type=F32)
        err = x_ref[...] + mod_ref[5:6, :] * y - t_ref[...]
        dy = err * (1.0 / D)
        dy_ref[...] = dy

        @pl.when((b == 0) & (i == 0))
        def _():
            loss_ref[...] = jnp.zeros_like(loss_ref)

        @pl.when(i == 0)
        def _():
            dgate_ref[...] = jnp.zeros_like(dgate_ref)

        loss_ref[...] += (0.5 / D) * jnp.sum(err * err)
        dgate_ref[...] += jnp.sum(dy * y, axis=0, keepdims=True)

    return pl.pallas_call(
        body, name="ffn2_fwd", grid=(B, S // tm),
        out_shape=[jax.ShapeDtypeStruct((B, S, D), F32), jax.ShapeDtypeStruct((1, 128), F32),
                   jax.ShapeDtypeStruct((B, 1, D), F32)],
        in_specs=[_rows(tm, FFN), _rows(tm, D), _rows(tm, D), _perb(6, D), _resident(w_down.shape)],
        out_specs=[_rows(tm, D), _full((1, 128)), _perb(1, D)],
        compiler_params=_cparams(("arbitrary", "arbitrary")),
    )(act, x1, target, mod, w_down)


def _ffn2_bwd(dy, gate, up, mod, w_down):
    B, S, _ = dy.shape
    tm = _tile(S)

    def body(dy_ref, gate_ref, up_ref, mod_ref, w_ref, dgu_ref, dyg_ref):
        dyg = (dy_ref[...] * mod_ref[5:6, :]).astype(BF16)
        dyg_ref[...] = dyg
        dact = _dot_nt(dyg, w_ref[...])
        gate, up = gate_ref[...], up_ref[...]
        sg = _sigmoid(gate)
        dgu_ref[:, 0:FFN] = (dact * up * (sg * (1.0 + gate * (1.0 - sg)))).astype(BF16)
        dgu_ref[:, FFN:2 * FFN] = (dact * (gate * sg)).astype(BF16)

    return pl.pallas_call(
        body, name="ffn2_bwd", grid=(B, S // tm),
        out_shape=[jax.ShapeDtypeStruct((B, S, 2 * FFN), BF16), jax.ShapeDtypeStruct((B, S, D), BF16)],
        in_specs=[_rows(tm, D), _rows(tm, FFN), _rows(tm, FFN), _perb(6, D), _resident(w_down.shape)],
        out_specs=[_rows(tm, 2 * FFN), _rows(tm, D)],
        compiler_params=_cparams(("parallel", "arbitrary")),
    )(dy, gate, up, mod, w_down)


def _ffn1_bwd(dgu, x1, dy, mod, g2, w_gu):
    B, S, _ = x1.shape
    tm = _tile(S)

    def body(dgu_ref, x_ref, dy_ref, mod_ref, g_ref, w_ref, dx1_ref, dg_ref, dsc_ref, dsh_ref):
        b, i = pl.program_id(0), pl.program_id(1)
        dh = _dot_nt(dgu_ref[...], w_ref[...])
        _, vjp = jax.vjp(_rms_mod, x_ref[...], g_ref[...], mod_ref[4:5, :], mod_ref[3:4, :])
        dx, dg, dsc, dsh = vjp(dh)
        dx1_ref[...] = dy_ref[...] + dx

        @pl.when((b == 0) & (i == 0))
        def _():
            dg_ref[...] = jnp.zeros_like(dg_ref)

        @pl.when(i == 0)
        def _():
            dsc_ref[...] = jnp.zeros_like(dsc_ref)
            dsh_ref[...] = jnp.zeros_like(dsh_ref)

        dg_ref[...] += dg
        dsc_ref[...] += dsc
        dsh_ref[...] += dsh

    return pl.pallas_call(
        body, name="ffn1_bwd", grid=(B, S // tm),
        out_shape=[jax.ShapeDtypeStruct((B, S, D), F32), jax.ShapeDtypeStruct((1, D), F32),
                   jax.ShapeDtypeStruct((B, 1, D), F32), jax.ShapeDtypeStruct((B, 1, D), F32)],
        in_specs=[_rows(tm, 2 * FFN), _rows(tm, D), _rows(tm, D), _perb(6, D), _full((1, D)), _resident(w_gu.shape)],
        out_specs=[_rows(tm, D), _full((1, D)), _perb(1, D), _perb(1, D)],
        compiler_params=_cparams(("arbitrary", "arbitrary")),
    )(dgu, x1, dy, mod, g2, w_gu)


def _adamw(w, g, m, v, name):
    def body(w_ref, g_ref, m_ref, v_ref, d_ref, nm_ref, nv_ref):
        g = g_ref[...]
        m = B1 * m_ref[...] + (1.0 - B1) * g
        v = B2 * v_ref[...] + (1.0 - B2) * (g * g)
        nm_ref[...] = m
        nv_ref[...] = v
        m_hat = m / (1.0 - B1 ** STEP)
        v_hat = v / (1.0 - B2 ** STEP)
        d_ref[...] = -LR * (m_hat / (jnp.sqrt(v_hat) + AEPS) + WD * w_ref[...])

    sd = jax.ShapeDtypeStruct(w.shape, F32)
    return pl.pallas_call(body, name=name, out_shape=(sd, sd, sd), compiler_params=_cparams())(w, g, m, v)


SHARD_ROWS = (609, 128, 128, 704, 352)
BF16_TILE_ROWS = 16
PART_ROWS = tuple(-(-r // BF16_TILE_ROWS) * BF16_TILE_ROWS for r in SHARD_ROWS)
PACK_ROWS = sum(PART_ROWS)


def _pad_rows(a, rows, axis):
    widths = [(0, 0)] * a.ndim
    widths[axis] = (0, rows - a.shape[axis])
    return jnp.pad(a, widths) if rows != a.shape[axis] else a


def _pack_cols(w, n):
    return w.reshape(w.shape[0], N_DEV, n).transpose(1, 0, 2).reshape(N_DEV, (w.shape[0] * n) // D, D)


def _unpack_cols(p, rows_in, n):
    return p.reshape(N_DEV, rows_in, n).transpose(1, 0, 2).reshape(rows_in, N_DEV * n)


def kernel(x, c, positions, ada_w, ada_b, norm1_g, w_in, conv_w, q_norm_g, k_norm_g, sinks, a_log, dt_bias, dn_norm_g, w_branch, w_out, norm2_g, w_gate_up, w_down, loss_target, m_ada_w, m_ada_b, m_norm1_g, m_w_in, m_conv_w, m_q_norm_g, m_k_norm_g, m_sinks, m_a_log, m_dt_bias, m_dn_norm_g, m_w_branch, m_w_out, m_norm2_g, m_w_gate_up, m_w_down, v_ada_w, v_ada_b, v_norm1_g, v_w_in, v_conv_w, v_q_norm_g, v_k_norm_g, v_sinks, v_a_log, v_dt_bias, v_dn_norm_g, v_w_branch, v_w_out, v_norm2_g, v_w_gate_up, v_w_down):
    B, S, _ = x.shape
    me = 4 * lax.axis_index("x") + 2 * lax.axis_index("y") + lax.axis_index("c")
    n_in, n_gu, n_dn = IN_W // N_DEV, 2 * FFN // N_DEV, FFN // N_DEV

    flat = [w_in[0].reshape(-1, D), w_branch[0], w_out[0], w_gate_up[0].reshape(-1, D), w_down[0]]
    packed = jnp.concatenate([_pad_rows(f.astype(BF16), r, 0) for f, r in zip(flat, PART_ROWS)], axis=0)
    gathered = _all_gather_big(packed, "gather_weights")
    offs = np.cumsum((0,) + PART_ROWS)
    part = [gathered[:, offs[i]:offs[i] + SHARD_ROWS[i], :] for i in range(5)]
    w_in_f = _unpack_cols(part[0], D, n_in)
    w_branch_f = part[1].reshape(D, D)
    w_out_f = part[2].reshape(D, D)
    w_gu_f = _unpack_cols(part[3], D, n_gu)
    w_down_f = part[4].reshape(FFN, D)
    cuts = [(0, QW), (QW, QW + 2 * KVW), (QW + 2 * KVW, QW + 2 * KVW + CONVW)]
    o = QW + 2 * KVW + CONVW
    cuts += [(o, o + 2 * DH), (o + 2 * DH, o + 2 * DH + DNW), (o + 2 * DH + DNW, o + 2 * DH + DNW + D),
             (o + 2 * DH + DNW + D, IN_W)]
    ws_in = [w_in_f[:, a:b] for a, b in cuts]

    c_all = _all_gather_small(c, "gather_c").reshape(N_DEV * B, D)
    ncol = 6 * D // N_DEV
    mod_cols, cond_all = _ada_fwd(c_all, ada_w[0], lax.dynamic_slice(ada_b, (0, me * ncol), (1, ncol)))
    mod_all = _all_gather_small(mod_cols, "gather_mod").transpose(1, 0, 2).reshape(N_DEV * B, 6 * D)
    mod = lax.dynamic_slice(mod_all, (me * B, 0), (B, 6 * D)).reshape(B, 6, D)

    h1, aq, akv, dnx, ba, z, ga, gd = _inproj_fwd(x, mod, norm1_g, ws_in)
    invf, mean_q, mean_k = _attn_consts()
    rope_cos, rope_sin = _rope_tables(positions.reshape(B, S, 1), invf)
    o_attn = _attn_fwd(aq, akv, rope_cos, rope_sin, q_norm_g, k_norm_g, sinks, mean_q, mean_k)
    conv2 = conv_w.reshape(CONV, CONVW // N_DEV)
    conv_all = _all_gather_small(conv2, "gather_conv").transpose(1, 0, 2).reshape(CONV, CONVW)
    cq = _conv_fwd(dnx, conv_all)
    dn_u, dn_w, dn_qd, dn_kd, dn_a, dn_t, dn_cd = _dn_prep_fwd(cq, ba, a_log, dt_bias)
    o_dn, states = _dn_seq_fwd(dn_u, dn_w, dn_qd, dn_kd, dn_a, dn_cd)
    x1, mix, merged, oa_b, od_b = _mix_fwd(x, o_attn, o_dn, z, ga, gd, mod, dn_norm_g, w_branch_f, w_out_f)
    h2, gate, up, act = _ffn1_fwd(x1, mod, norm2_g, w_gu_f)
    dy, loss_part, d_gate2 = _ffn2_fwd(act, x1, loss_target, mod, w_down_f)
    loss = lax.psum(loss_part[0, 0], ("x", "y", "c"))

    dgu, dyg = _ffn2_bwd(dy, gate, up, mod, w_down_f)
    g_w_down = _wgrad(act, dyg, "wgrad_down")
    dx1, d_n2g, d_scale2, d_shift2 = _ffn1_bwd(dgu, x1, dy, mod, norm2_g, w_gu_f)
    g_w_gu = _wgrad(h2, dgu, "wgrad_gate_up")
    dmix, dya, dyd, dga, dgd, dz, d_oa, d_od, d_gate1, d_dng = _mix_bwd(
        dx1, mix, o_attn, o_dn, z, ga, gd, mod, dn_norm_g, w_branch_f, w_out_f)
    g_w_out = _wgrad(merged, dmix, "wgrad_out")
    g_w_branch = jnp.concatenate([_wgrad(oa_b, dya, "wgrad_branch_a"), _wgrad(od_b, dyd, "wgrad_branch_d")], axis=0)
    d_dn = _dn_seq_bwd(dn_u, dn_w, dn_qd, dn_kd, dn_a, dn_cd, states, d_od)
    dcq, dba, d_alog, d_dtb = _dn_prep_bwd(cq, ba, a_log, dt_bias, dn_t, *d_dn)
    ddnx, d_conv = _conv_bwd(dnx, conv_all, dcq)
    daq, dakv, d_qg, d_kg, d_sinks = _attn_bwd(aq, akv, rope_cos, rope_sin, q_norm_g, k_norm_g, sinks, mean_q, mean_k, d_oa)
    dps = [daq, dakv, ddnx, dba, dz, dga, dgd]
    grad_x, d_n1g, d_scale1, d_shift1 = _inproj_bwd(x, mod, norm1_g, dx1, dps, ws_in)
    names = ["q", "kv", "dn", "ba", "z", "ga", "gd"]
    g_w_in = jnp.concatenate([_wgrad(h1, dp, "wgrad_in_" + nm) for dp, nm in zip(dps, names)], axis=1)

    gparts = [_pack_cols(g_w_in, n_in), g_w_branch.reshape(N_DEV, D // N_DEV, D), g_w_out.reshape(N_DEV, D // N_DEV, D),
              _pack_cols(g_w_gu, n_gu), g_w_down.reshape(N_DEV, n_dn, D)]
    gpack = jnp.concatenate([_pad_rows(p.astype(BF16), r, 1) for p, r in zip(gparts, PART_ROWS)], axis=1)
    gsum = _sum_blocks(_exchange_blocks(gpack, "exchange_grads"), "sum_grads")
    gs = [gsum[offs[i]:offs[i] + SHARD_ROWS[i], :] for i in range(5)]
    grad_w_in = gs[0].reshape(1, D, n_in)
    grad_w_branch = gs[1].reshape(1, D // N_DEV, D)
    grad_w_out = gs[2].reshape(1, D // N_DEV, D)
    grad_w_gu = gs[3].reshape(1, D, n_gu)
    grad_w_down = gs[4].reshape(1, n_dn, D)

    dmod = jnp.concatenate([d_shift1, d_scale1, d_gate1, d_shift2, d_scale2, d_gate2], axis=2).reshape(B, 6 * D)
    small = jnp.concatenate([d_n1g, d_qg, d_kg, d_sinks, d_alog, d_dtb, d_dng, d_n2g, d_conv.reshape(1, CONV * CONVW)], axis=1)
    nsm = small.shape[1]
    width = -(-max(6 * D, nsm) // 128) * 128
    rows = jnp.concatenate([jnp.pad(dmod, ((0, 0), (0, width - 6 * D))), jnp.pad(small, ((0, 8 - B - 1), (0, width - nsm)))], axis=0)
    rows_all = _all_gather_small(rows, "gather_small")
    dmod_all = rows_all[:, 0:B, 0:6 * D].reshape(N_DEV * B, 6 * D)
    dmod_cols = lax.dynamic_slice(dmod_all, (0, me * ncol), (N_DEV * B, ncol))
    grad_ada_w, grad_ada_b, small_sum = _ada_bwd(cond_all, dmod_all, dmod_cols, rows_all[:, B, :])
    sizes = [D, HD, HD, HQ, DH, DH, DK, D]
    so = np.cumsum([0] + sizes)
    g_n1, g_qg, g_kg, g_sk, g_al, g_dt, g_dn, g_n2 = [small_sum[:, so[i]:so[i + 1]] for i in range(8)]
    g_conv_all = small_sum[:, so[8]:so[8] + CONV * CONVW].reshape(CONV, N_DEV, CONVW // N_DEV)
    grad_conv = lax.dynamic_slice(g_conv_all, (0, me, 0), (CONV, 1, CONVW // N_DEV)).reshape(CONV, CONVW // N_DEV)

    big = [(ada_w, grad_ada_w.reshape(ada_w.shape), m_ada_w, v_ada_w), (w_in, grad_w_in, m_w_in, v_w_in),
           (w_branch, grad_w_branch, m_w_branch, v_w_branch), (w_out, grad_w_out, m_w_out, v_w_out),
           (w_gate_up, grad_w_gu, m_w_gate_up, v_w_gate_up), (w_down, grad_w_down, m_w_down, v_w_down)]
    upd = {}
    for nm, (w, g, m, v) in zip(["ada_w", "w_in", "w_branch", "w_out", "w_gate_up", "w_down"], big):
        s2 = w.shape[1:]
        res = _adamw(w.reshape(s2), g.reshape(s2), m.reshape(s2), v.reshape(s2), "adamw_" + nm)
        upd[nm] = tuple(r.reshape(w.shape) for r in res)
    small_names = ["ada_b", "norm1_g", "q_norm_g", "k_norm_g", "sinks", "a_log", "dt_bias", "dn_norm_g", "norm2_g", "conv_w"]
    small_w = [ada_b, norm1_g, q_norm_g, k_norm_g, sinks, a_log, dt_bias, dn_norm_g, norm2_g, conv_w]
    small_g = [grad_ada_b, g_n1, g_qg, g_kg, g_sk, g_al, g_dt, g_dn, g_n2, grad_conv]
    small_m = [m_ada_b, m_norm1_g, m_q_norm_g, m_k_norm_g, m_sinks, m_a_log, m_dt_bias, m_dn_norm_g, m_norm2_g, m_conv_w]
    small_v = [v_ada_b, v_norm1_g, v_q_norm_g, v_k_norm_g, v_sinks, v_a_log, v_dt_bias, v_dn_norm_g, v_norm2_g, v_conv_w]
    cat = lambda arrs: jnp.concatenate([a.reshape(1, -1) for a in arrs], axis=1)
    res = _adamw(cat(small_w), cat(small_g), cat(small_m), cat(small_v), "adamw_small")
    po = np.cumsum([0] + [int(np.prod(w.shape)) for w in small_w])
    grads = {}
    for i, nm in enumerate(small_names):
        upd[nm] = tuple(r[:, po[i]:po[i + 1]].reshape(small_w[i].shape) for r in res)
        grads[nm] = small_g[i].reshape(small_w[i].shape)
    grads.update(ada_w=grad_ada_w.reshape(ada_w.shape), w_in=grad_w_in, w_branch=grad_w_branch, w_out=grad_w_out,
                 w_gate_up=grad_w_gu, w_down=grad_w_down)

    order = ["ada_w", "ada_b", "norm1_g", "w_in", "conv_w", "q_norm_g", "k_norm_g", "sinks", "a_log", "dt_bias",
             "dn_norm_g", "w_branch", "w_out", "norm2_g", "w_gate_up", "w_down"]
    return (loss, grad_x, *[grads[n] for n in order], *[upd[n][0] for n in order],
            *[upd[n][1] for n in order], *[upd[n][2] for n in order])
```

```python
import functools

import numpy as np
import jax
import jax.numpy as jnp
from jax import lax
from jax.experimental import pallas as pl
from jax.experimental.pallas import tpu as pltpu

F32 = jnp.float32
BF16 = jnp.bfloat16
HI = lax.Precision.HIGHEST

N_DEV = 8
D = 1024
HQ, HKV, HD = 8, 2, 64
GRP = HQ // HKV
BLK = 128
ROT = HD // 4
THETA = 500000.0
QW, KVW = HQ * HD, HKV * HD
DH, DK = 4, 128
CH = 64
DNW = DH * DK
CONV = 4
CONVW = 3 * DNW
FFN = 2816
EPS = 1e-6
IN_W = QW + 2 * KVW + CONVW + 2 * DH + DNW + 2 * D

LR, B1, B2, AEPS, WD, STEP = 0.001, 0.9, 0.999, 1e-08, 0.01, 10

VMEM_LIMIT = 56 * 1024 * 1024
MESH = pl.DeviceIdType.MESH


def _cparams(sem=None, vmem=VMEM_LIMIT):
    return pltpu.CompilerParams(dimension_semantics=sem, vmem_limit_bytes=vmem)


def _full(shape):
    n = len(shape)
    return pl.BlockSpec(shape, lambda *_: (0,) * n)


def _resident(shape):
    n = len(shape)
    return pl.BlockSpec(shape, lambda *_: (0,) * n, pipeline_mode=pl.Buffered(1))


def _rows(tm, w):
    return pl.BlockSpec((None, tm, w), lambda b, i: (b, i, 0))


def _stacked(n, tm, w):
    return pl.BlockSpec((None, n, tm, w), lambda b, i: (b, 0, i, 0))


def _perb(r, w):
    return pl.BlockSpec((None, r, w), lambda b, i: (b, 0, 0))


def _dot(a, b):
    return jnp.dot(a.astype(BF16), b.astype(BF16), preferred_element_type=F32)


def _dot_nt(a, b):
    return lax.dot_general(a.astype(BF16), b.astype(BF16), (((1,), (1,)), ((), ())), preferred_element_type=F32)


def _dot_tn(a, b):
    return lax.dot_general(a.astype(BF16), b.astype(BF16), (((0,), (0,)), ((), ())), preferred_element_type=F32)


def _dot_hi(a, b):
    return jnp.dot(a, b, preferred_element_type=F32, precision=HI)


def _sigmoid(x):
    return jax.nn.sigmoid(x)


def _silu(x):
    return x * jax.nn.sigmoid(x)


def _rms_mod(x, g, scale, shift):
    r = lax.rsqrt(jnp.mean(x * x, axis=-1, keepdims=True) + EPS)
    return (x * r * g) * (1.0 + scale) + shift


def _tile(S):
    return min(256, S)


def _peer(x, y, c, k):
    px = 1 - x if (k >> 2) & 1 else x
    py = 1 - y if (k >> 1) & 1 else y
    pc = 1 - c if k & 1 else c
    return px, py, pc


def _all_gather_small(v, name):
    r, n = v.shape

    def body(v_ref, out_ref, send_sems, recv_sems, local_sem):
        x, y, c = lax.axis_index("x"), lax.axis_index("y"), lax.axis_index("c")
        me = 4 * x + 2 * y + c
        mine = pltpu.make_async_copy(v_ref, out_ref.at[me], local_sem)
        mine.start()
        sends = []
        for k in range(1, N_DEV):
            cp = pltpu.make_async_remote_copy(
                src_ref=v_ref, dst_ref=out_ref.at[me], send_sem=send_sems.at[k - 1], recv_sem=recv_sems.at[k - 1],
                device_id=_peer(x, y, c, k), device_id_type=MESH)
            cp.start()
            sends.append(cp)
        for k in range(1, N_DEV):
            px, py, pc = _peer(x, y, c, k)
            pltpu.make_async_remote_copy(
                src_ref=v_ref, dst_ref=out_ref.at[4 * px + 2 * py + pc], send_sem=send_sems.at[k - 1],
                recv_sem=recv_sems.at[k - 1], device_id=(px, py, pc), device_id_type=MESH).wait_recv()
        for cp in sends:
            cp.wait_send()
        mine.wait()

    return pl.pallas_call(
        body, name=name,
        out_shape=jax.ShapeDtypeStruct((N_DEV, r, n), v.dtype),
        in_specs=[pl.BlockSpec(memory_space=pltpu.VMEM)],
        out_specs=pl.BlockSpec(memory_space=pltpu.VMEM),
        scratch_shapes=[pltpu.SemaphoreType.DMA((N_DEV - 1,)), pltpu.SemaphoreType.DMA((N_DEV - 1,)), pltpu.SemaphoreType.DMA],
    )(v)


def _all_gather_big(vs, name):
    na = len(vs)

    def body(*refs):
        v_refs, out_refs = refs[:na], refs[na:2 * na]
        send_sems, recv_sems, local_sems = refs[2 * na:]
        x, y, c = lax.axis_index("x"), lax.axis_index("y"), lax.axis_index("c")
        me, sibling = (x, y, c), (x, y, 1 - c)
        chips = [(1 - x, y), (x, 1 - y), (1 - x, 1 - y)]

        def rows(a, px, py, pc):
            return out_refs[a].at[4 * px + 2 * py + pc]

        def copy(a, k, block, to, src=None):
            return pltpu.make_async_remote_copy(
                src_ref=rows(a, *block) if src is None else src, dst_ref=rows(a, *block),
                send_sem=send_sems.at[7 * a + k], recv_sem=recv_sems.at[7 * a + k], device_id=to, device_id_type=MESH)

        mine = [pltpu.make_async_copy(v_refs[a], rows(a, *me), local_sems.at[a]) for a in range(na)]
        for cp in mine:
            cp.start()
        first = []
        for a in range(na):
            first.append(copy(a, 0, me, sibling, src=v_refs[a]))
            first += [copy(a, 1 + j, me, (*chip, c), src=v_refs[a]) for j, chip in enumerate(chips)]
        for cp in first:
            cp.start()
        passed = []
        for j, chip in enumerate(chips):
            for a in range(na):
                copy(a, 1 + j, (*chip, c), me).wait_recv()
                forward = copy(a, 4 + j, (*chip, c), sibling)
                forward.start()
                passed.append(forward)
        for a in range(na):
            copy(a, 0, sibling, me).wait_recv()
            for j, chip in enumerate(chips):
                copy(a, 4 + j, (*chip, 1 - c), me).wait_recv()
        for cp in first + passed:
            cp.wait_send()
        for cp in mine:
            cp.wait()

    return pl.pallas_call(
        body, name=name,
        out_shape=[jax.ShapeDtypeStruct((N_DEV,) + v.shape, v.dtype) for v in vs],
        in_specs=[pl.BlockSpec(memory_space=pl.ANY)] * na,
        out_specs=[pl.BlockSpec(memory_space=pl.ANY)] * na,
        scratch_shapes=[pltpu.SemaphoreType.DMA((7 * na,)), pltpu.SemaphoreType.DMA((7 * na,)),
                        pltpu.SemaphoreType.DMA((na,))],
    )(*vs)


def _exchange_blocks(gs, name):
    na = len(gs)

    def body(*refs):
        g_refs, out_refs = refs[:na], refs[na:2 * na]
        send_sems, recv_sems, local_sems = refs[2 * na:]
        x, y, c = lax.axis_index("x"), lax.axis_index("y"), lax.axis_index("c")
        me = 4 * x + 2 * y + c
        mine = [pltpu.make_async_copy(g_refs[a].at[me], out_refs[a].at[me], local_sems.at[a]) for a in range(na)]
        for cp in mine:
            cp.start()
        sends = []
        for k in range(1, N_DEV):
            px, py, pc = _peer(x, y, c, k)
            for a in range(na):
                cp = pltpu.make_async_remote_copy(
                    src_ref=g_refs[a].at[4 * px + 2 * py + pc], dst_ref=out_refs[a].at[me],
                    send_sem=send_sems.at[7 * a + k - 1], recv_sem=recv_sems.at[7 * a + k - 1],
                    device_id=(px, py, pc), device_id_type=MESH)
                cp.start()
                sends.append(cp)
        for k in range(1, N_DEV):
            px, py, pc = _peer(x, y, c, k)
            for a in range(na):
                pltpu.make_async_remote_copy(
                    src_ref=g_refs[a].at[me], dst_ref=out_refs[a].at[4 * px + 2 * py + pc],
                    send_sem=send_sems.at[7 * a + k - 1], recv_sem=recv_sems.at[7 * a + k - 1],
                    device_id=(px, py, pc), device_id_type=MESH).wait_recv()
        for cp in sends:
            cp.wait_send()
        for cp in mine:
            cp.wait()

    return pl.pallas_call(
        body, name=name,
        out_shape=[jax.ShapeDtypeStruct(g.shape, g.dtype) for g in gs],
        in_specs=[pl.BlockSpec(memory_space=pl.ANY)] * na,
        out_specs=[pl.BlockSpec(memory_space=pl.ANY)] * na,
        scratch_shapes=[pltpu.SemaphoreType.DMA((7 * na,)), pltpu.SemaphoreType.DMA((7 * na,)),
                        pltpu.SemaphoreType.DMA((na,))],
    )(*gs)


def _sum_blocks(g, name):
    _, r, n = g.shape
    tr = 256 if r % 256 == 0 else r

    def body(g_ref, o_ref):
        acc = g_ref[0].astype(F32)
        for d in range(1, N_DEV):
            acc = acc + g_ref[d].astype(F32)
        o_ref[...] = acc

    return pl.pallas_call(
        body, name=name, grid=(r // tr,),
        out_shape=jax.ShapeDtypeStruct((1, r, n), F32),
        in_specs=[pl.BlockSpec((N_DEV, tr, n), lambda i: (0, i, 0))],
        out_specs=pl.BlockSpec((None, tr, n), lambda i: (0, i, 0)),
        compiler_params=_cparams(("arbitrary",)),
    )(g)


def _ada_fwd(c_all, ada_w, ada_b_cols):
    nb, ncol = c_all.shape[0], ada_w.shape[1]

    def body(c_ref, w_ref, b_ref, mod_ref, cond_ref):
        cond = _silu(c_ref[...])
        cond_ref[...] = cond
        mod_ref[...] = _dot_hi(cond, w_ref[...]) + b_ref[...]

    return pl.pallas_call(
        body, name="ada_fwd",
        out_shape=(jax.ShapeDtypeStruct((nb, ncol), F32), jax.ShapeDtypeStruct((nb, D), F32)),
        compiler_params=_cparams(),
    )(c_all, ada_w, ada_b_cols)


def _ada_bwd(cond_all, dmod_all, dmod_cols, smalls):
    ncol, nsm = dmod_cols.shape[1], smalls.shape[1]

    def body(cond_ref, dm_ref, dmc_ref, sm_ref, gw_ref, gb_ref, gs_ref):
        gw_ref[...] = lax.dot_general(cond_ref[...], dmc_ref[...], (((0,), (0,)), ((), ())),
                                      preferred_element_type=F32, precision=HI)
        gb_ref[...] = jnp.sum(dm_ref[...], axis=0, keepdims=True)
        gs_ref[...] = jnp.sum(sm_ref[...], axis=0, keepdims=True)

    return pl.pallas_call(
        body, name="ada_bwd",
        out_shape=(jax.ShapeDtypeStruct((D, ncol), F32), jax.ShapeDtypeStruct((1, 6 * D), F32),
                   jax.ShapeDtypeStruct((1, nsm), F32)),
        compiler_params=_cparams(),
    )(cond_all, dmod_all, dmod_cols, smalls)


IN_CUTS = (0, QW, QW + 2 * KVW, QW + 2 * KVW + CONVW, QW + 2 * KVW + CONVW + 2 * DH,
           QW + 2 * KVW + CONVW + 2 * DH + DNW, QW + 2 * KVW + CONVW + 2 * DH + DNW + D, IN_W)
IN_WIDTHS = tuple(b - a for a, b in zip(IN_CUTS[:-1], IN_CUTS[1:]))
IN_SHARD = IN_W // N_DEV


def _inproj_fwd(x, mod, g1, w_blk):
    B, S, _ = x.shape
    tm = _tile(S)

    def body(x_ref, mod_ref, g_ref, w_ref, h_ref, *o_refs):
        h = _rms_mod(x_ref[...], g_ref[...], mod_ref[1:2, :], mod_ref[0:1, :]).astype(BF16)
        h_ref[...] = h
        full = jnp.concatenate([jnp.dot(h, w_ref[j], preferred_element_type=F32) for j in range(N_DEV)], axis=1)
        for o_ref, lo, hi in zip(o_refs, IN_CUTS[:-1], IN_CUTS[1:]):
            o_ref[...] = full[:, lo:hi]

    return pl.pallas_call(
        body, name="inproj_fwd", grid=(B, S // tm),
        out_shape=[jax.ShapeDtypeStruct((B, S, D), BF16)] + [jax.ShapeDtypeStruct((B, S, w), F32) for w in IN_WIDTHS],
        in_specs=[_rows(tm, D), _perb(6, D), _full((1, D)), _resident(w_blk.shape)],
        out_specs=[_rows(tm, D)] + [_rows(tm, w) for w in IN_WIDTHS],
        compiler_params=_cparams(("parallel", "arbitrary")),
    )(x, mod, g1, w_blk)


def _inproj_bwd(x, mod, g1, dx1, dps, w_blk):
    B, S, _ = x.shape
    tm = _tile(S)
    n = len(dps)

    def body(x_ref, mod_ref, g_ref, dx1_ref, *refs):
        dp_refs, w_ref = refs[:n], refs[n]
        dblk_ref, gx_ref, dg_ref, dsc_ref, dsh_ref = refs[n + 1:]
        b, i = pl.program_id(0), pl.program_id(1)
        full = jnp.concatenate([r[...].astype(F32) for r in dp_refs], axis=1)
        dh = None
        for j in range(N_DEV):
            blk = full[:, IN_SHARD * j:IN_SHARD * (j + 1)].astype(BF16)
            dblk_ref[j] = blk
            t = _dot_nt(blk, w_ref[j])
            dh = t if dh is None else dh + t
        _, vjp = jax.vjp(_rms_mod, x_ref[...], g_ref[...], mod_ref[1:2, :], mod_ref[0:1, :])
        dx, dg, dsc, dsh = vjp(dh)
        gx_ref[...] = dx1_ref[...] + dx

        @pl.when((b == 0) & (i == 0))
        def _():
            dg_ref[...] = jnp.zeros_like(dg_ref)

        @pl.when(i == 0)
        def _():
            dsc_ref[...] = jnp.zeros_like(dsc_ref)
            dsh_ref[...] = jnp.zeros_like(dsh_ref)

        dg_ref[...] += dg
        dsc_ref[...] += dsc
        dsh_ref[...] += dsh

    return pl.pallas_call(
        body, name="inproj_bwd", grid=(B, S // tm),
        out_shape=[jax.ShapeDtypeStruct((B, N_DEV, S, IN_SHARD), BF16), jax.ShapeDtypeStruct((B, S, D), F32),
                   jax.ShapeDtypeStruct((1, D), F32), jax.ShapeDtypeStruct((B, 1, D), F32),
                   jax.ShapeDtypeStruct((B, 1, D), F32)],
        in_specs=[_rows(tm, D), _perb(6, D), _full((1, D)), _rows(tm, D)]
                 + [_rows(tm, w) for w in IN_WIDTHS] + [_resident(w_blk.shape)],
        out_specs=[pl.BlockSpec((None, N_DEV, tm, IN_SHARD), lambda b, i: (b, 0, i, 0)), _rows(tm, D),
                   _full((1, D)), _perb(1, D), _perb(1, D)],
        compiler_params=_cparams(("arbitrary", "arbitrary")),
    )(x, mod, g1, dx1, *dps, w_blk)


def _wgrad(a, b, name):
    B, na, S, K = a.shape
    nb, N = b.shape[1], b.shape[3]
    G = max(na, nb)
    tm = min(512, S)
    nt = S // tm
    last = B * nt - 1

    def body(a_ref, b_ref, o_ref, acc):
        t = pl.program_id(1)

        @pl.when(t == 0)
        def _():
            acc[...] = jnp.zeros_like(acc)

        acc[...] += lax.dot_general(a_ref[...], b_ref[...], (((0,), (0,)), ((), ())), preferred_element_type=F32)

        @pl.when(t == last)
        def _():
            o_ref[...] = acc[...].astype(BF16)

    return pl.pallas_call(
        body, name=name, grid=(G, B * nt),
        out_shape=jax.ShapeDtypeStruct((G, K, N), BF16),
        in_specs=[pl.BlockSpec((None, None, tm, K), lambda g, t: (t // nt, g if na > 1 else 0, t % nt, 0)),
                  pl.BlockSpec((None, None, tm, N), lambda g, t: (t // nt, g if nb > 1 else 0, t % nt, 0))],
        out_specs=pl.BlockSpec((None, K, N), lambda g, t: (g, 0, 0)),
        scratch_shapes=[pltpu.VMEM((K, N), F32)],
        compiler_params=_cparams(("parallel", "arbitrary")),
    )(a, b)


LANES = 128


def _attn_consts():
    inv_freq = THETA ** (-jnp.arange(0, ROT, 2, dtype=F32) / ROT)
    head = jnp.concatenate([inv_freq, inv_freq, jnp.zeros((HD - ROT,), F32)])
    invf = jnp.tile(head, LANES // HD)[None, :]
    mean_of = lambda w: jnp.asarray(np.kron(np.eye(w // HD), np.full((HD, HD), 1.0 / HD)), BF16)
    return invf, mean_of(QW), mean_of(KVW)


def _rope_tables(pos, invf):
    B, S, _ = pos.shape
    tr = min(1024, S)

    def body(p_ref, f_ref, c_ref, s_ref):
        ang = p_ref[...].astype(F32) * f_ref[...]
        c_ref[...] = jnp.cos(ang)
        s_ref[...] = jnp.sin(ang)

    sd = jax.ShapeDtypeStruct((B, S, LANES), F32)
    return pl.pallas_call(
        body, name="rope_tables", grid=(B, S // tr), out_shape=[sd, sd],
        in_specs=[_rows(tr, 1), _full((1, LANES))], out_specs=[_rows(tr, LANES), _rows(tr, LANES)],
        compiler_params=_cparams(("parallel", "parallel")),
    )(pos, invf)


def _rope_expand(cos, sin, reps):
    lane = lax.broadcasted_iota(jnp.int32, cos.shape, 1) % HD
    sa = jnp.where((lane >= ROT // 2) & (lane < ROT), sin, 0.0)
    sb = jnp.where(lane < ROT // 2, -sin, 0.0)
    rep = lambda t: jnp.concatenate([t] * reps, axis=1) if reps > 1 else t
    return rep(cos), rep(sa), rep(sb)


@jax.custom_vjp
def _rope(t, cos, sa, sb):
    w = t.shape[1]
    return t * cos + pltpu.roll(t, ROT // 2, 1) * sa + pltpu.roll(t, w - ROT // 2, 1) * sb


def _rope_fwd(t, cos, sa, sb):
    return _rope(t, cos, sa, sb), (cos, sa, sb)


def _rope_bwd(res, d):
    cos, sa, sb = res
    w = d.shape[1]
    dt = d * cos + pltpu.roll(d * sa, w - ROT // 2, 1) + pltpu.roll(d * sb, ROT // 2, 1)
    return dt, jnp.zeros_like(cos), jnp.zeros_like(sa), jnp.zeros_like(sb)


_rope.defvjp(_rope_fwd, _rope_bwd)


def _head_norm(t, g, mean_of):
    hi, lo = _split(t * t)
    ms = jnp.dot(hi, mean_of, preferred_element_type=F32) + jnp.dot(lo, mean_of, preferred_element_type=F32)
    return t * lax.rsqrt(ms + EPS) * g


def _attn_block(q, kvp, kvc, qg, kg, sinks, tq, tk, mq, mk, valid):
    qn = _rope(_head_norm(q, jnp.concatenate([qg] * HQ, axis=1), mq), *tq)
    kv = jnp.concatenate([kvp, kvc], axis=0)
    kn = _rope(_head_norm(kv[:, 0:KVW], jnp.concatenate([kg] * HKV, axis=1), mk), *tk)
    q4 = jnp.stack([jnp.concatenate([qn[:, HD * (GRP * j + i):HD * (GRP * j + i + 1)] for i in range(GRP)], axis=0)
                    for j in range(HKV)])
    k2 = jnp.stack([kn[:, HD * j:HD * (j + 1)] for j in range(HKV)])
    v2 = jnp.stack([kv[:, KVW + HD * j:KVW + HD * (j + 1)] for j in range(HKV)])
    rowblk = lax.broadcasted_iota(jnp.int32, (GRP * BLK, 1), 0) // BLK
    sink = jnp.stack([sum(jnp.where(rowblk == i, sinks[:, GRP * j + i:GRP * j + i + 1], 0.0) for i in range(GRP))
                      for j in range(HKV)])
    s = _bmm(q4, k2, _BMM_NT) * (HD ** -0.5)
    s = jnp.where(valid[None], s, -1e30)
    m = jnp.maximum(jnp.max(s, axis=-1, keepdims=True), sink)
    p = jnp.exp(s - m)
    probs = p / (jnp.sum(p, axis=-1, keepdims=True) + jnp.exp(sink - m))
    o4 = _bmm(probs, v2)
    return jnp.concatenate([o4[j, BLK * i:BLK * (i + 1), :] for j in range(HKV) for i in range(GRP)], axis=1)


def _attn_tables(cp_ref, cc_ref, sp_ref, sc_ref, n):
    tq = _rope_expand(cc_ref[...], sc_ref[...], QW // LANES)
    tk = _rope_expand(jnp.concatenate([cp_ref[...], cc_ref[...]], axis=0),
                      jnp.concatenate([sp_ref[...], sc_ref[...]], axis=0), KVW // LANES)
    qi = lax.broadcasted_iota(jnp.int32, (GRP * BLK, 2 * BLK), 0) % BLK + BLK
    kj = lax.broadcasted_iota(jnp.int32, (GRP * BLK, 2 * BLK), 1)
    dist = qi - kj
    valid = (dist >= 0) & (dist < BLK) & ((kj >= BLK) | (n > 0))
    return tq, tk, valid


def _attn_fwd(aq, akv, cos, sin, qg, kg, sinks, mq, mk):
    B, S, _ = aq.shape
    nb = S // BLK

    def body(q_ref, kvp_ref, kvc_ref, cp_ref, cc_ref, sp_ref, sc_ref, qg_ref, kg_ref, sk_ref, mq_ref, mk_ref, o_ref):
        tq, tk, valid = _attn_tables(cp_ref, cc_ref, sp_ref, sc_ref, pl.program_id(1))
        o_ref[...] = _attn_block(q_ref[...], kvp_ref[...], kvc_ref[...], qg_ref[...], kg_ref[...], sk_ref[...],
                                 tq, tk, mq_ref[...], mk_ref[...], valid)

    prev = lambda b, n: (b, jnp.maximum(n - 1, 0), 0)
    cur = lambda b, n: (b, n, 0)
    return pl.pallas_call(
        body, name="attn_fwd", grid=(B, nb),
        out_shape=jax.ShapeDtypeStruct((B, S, QW), F32),
        in_specs=[pl.BlockSpec((None, BLK, QW), cur), pl.BlockSpec((None, BLK, 2 * KVW), prev),
                  pl.BlockSpec((None, BLK, 2 * KVW), cur), pl.BlockSpec((None, BLK, LANES), prev),
                  pl.BlockSpec((None, BLK, LANES), cur), pl.BlockSpec((None, BLK, LANES), prev),
                  pl.BlockSpec((None, BLK, LANES), cur), _full((1, HD)), _full((1, HD)), _full((1, HQ)),
                  _full((QW, QW)), _full((KVW, KVW))],
        out_specs=pl.BlockSpec((None, BLK, QW), cur),
        compiler_params=_cparams(("parallel", "arbitrary")),
    )(aq, akv, akv, cos, cos, sin, sin, qg, kg, sinks, mq, mk)


def _attn_bwd(aq, akv, cos, sin, qg, kg, sinks, mq, mk, do):
    B, S, _ = aq.shape
    nb = S // BLK

    def body(q_ref, kvp_ref, kvc_ref, cp_ref, cc_ref, sp_ref, sc_ref, qg_ref, kg_ref, sk_ref, mq_ref, mk_ref, do_ref,
             dq_ref, dkv_ref, dqg_ref, dkg_ref, dsk_ref, carry):
        b, i = pl.program_id(0), pl.program_id(1)
        tq, tk, valid = _attn_tables(cp_ref, cc_ref, sp_ref, sc_ref, nb - 1 - i)
        fn = functools.partial(_attn_block, tq=tq, tk=tk, mq=mq_ref[...], mk=mk_ref[...], valid=valid)
        _, vjp = jax.vjp(fn, q_ref[...], kvp_ref[...], kvc_ref[...], qg_ref[...], kg_ref[...], sk_ref[...])
        dq, dkvp, dkvc, dqg, dkg, dsk = vjp(do_ref[...])

        @pl.when(i == 0)
        def _():
            carry[...] = jnp.zeros_like(carry)

        @pl.when((b == 0) & (i == 0))
        def _():
            dqg_ref[...] = jnp.zeros_like(dqg_ref)
            dkg_ref[...] = jnp.zeros_like(dkg_ref)
            dsk_ref[...] = jnp.zeros_like(dsk_ref)

        dq_ref[...] = dq.astype(BF16)
        dkv_ref[...] = (dkvc + carry[...]).astype(BF16)
        carry[...] = dkvp
        dqg_ref[...] += dqg
        dkg_ref[...] += dkg
        dsk_ref[...] += dsk

    prev = lambda b, i: (b, jnp.maximum(nb - 2 - i, 0), 0)
    cur = lambda b, i: (b, nb - 1 - i, 0)
    return pl.pallas_call(
        body, name="attn_bwd", grid=(B, nb),
        out_shape=[jax.ShapeDtypeStruct((B, S, QW), BF16), jax.ShapeDtypeStruct((B, S, 2 * KVW), BF16),
                   jax.ShapeDtypeStruct((1, HD), F32), jax.ShapeDtypeStruct((1, HD), F32),
                   jax.ShapeDtypeStruct((1, HQ), F32)],
        in_specs=[pl.BlockSpec((None, BLK, QW), cur), pl.BlockSpec((None, BLK, 2 * KVW), prev),
                  pl.BlockSpec((None, BLK, 2 * KVW), cur), pl.BlockSpec((None, BLK, LANES), prev),
                  pl.BlockSpec((None, BLK, LANES), cur), pl.BlockSpec((None, BLK, LANES), prev),
                  pl.BlockSpec((None, BLK, LANES), cur), _full((1, HD)), _full((1, HD)), _full((1, HQ)),
                  _full((QW, QW)), _full((KVW, KVW)), pl.BlockSpec((None, BLK, QW), cur)],
        out_specs=[pl.BlockSpec((None, BLK, QW), cur), pl.BlockSpec((None, BLK, 2 * KVW), cur),
                   _full((1, HD)), _full((1, HD)), _full((1, HQ))],
        scratch_shapes=[pltpu.VMEM((BLK, 2 * KVW), F32)],
        compiler_params=_cparams(("arbitrary", "arbitrary")),
    )(aq, akv, akv, cos, cos, sin, sin, qg, kg, sinks, mq, mk, do)


def _conv_taps(xe, w, rows):
    y = None
    for j in range(CONV):
        sh = pltpu.roll(xe, CONV - 1 - j, 0)[8:8 + rows, :] if j < CONV - 1 else xe[8:8 + rows, :]
        y = sh * w[j:j + 1, :] if y is None else y + sh * w[j:j + 1, :]
    return y


def _conv_fwd(xin, w):
    B, S, C = xin.shape
    tc = min(512, S)
    r8 = tc // 8

    def body(xp_ref, x_ref, w_ref, o_ref):
        i = pl.program_id(1)
        xp = jnp.where(i > 0, xp_ref[...], 0.0)
        xe = jnp.concatenate([xp, x_ref[...]], axis=0)
        o_ref[...] = _silu(_conv_taps(xe, w_ref[...], tc))

    return pl.pallas_call(
        body, name="conv_fwd", grid=(B, S // tc),
        out_shape=jax.ShapeDtypeStruct((B, S, C), F32),
        in_specs=[pl.BlockSpec((None, 8, C), lambda b, i: (b, jnp.maximum(i * r8 - 1, 0), 0)),
                  _rows(tc, C), _full((CONV, C))],
        out_specs=_rows(tc, C),
        compiler_params=_cparams(("parallel", "arbitrary")),
    )(xin, xin, w)


def _conv_bwd(xin, w, dy):
    B, S, C = xin.shape
    tc = min(512, S)
    r8 = tc // 8
    nt = S // tc

    def body(xp_ref, x_ref, xn_ref, dy_ref, dyn_ref, w_ref, dx_ref, dw_ref):
        b, i = pl.program_id(0), pl.program_id(1)
        w = w_ref[...]
        xp = jnp.where(i > 0, xp_ref[...], 0.0)
        xe = jnp.concatenate([xp, x_ref[...], xn_ref[...]], axis=0)
        pre = _conv_taps(xe, w, tc + 8)
        sg = _sigmoid(pre)
        dyn = jnp.where(i < nt - 1, dyn_ref[...], 0.0)
        dpre = jnp.concatenate([dy_ref[...], dyn], axis=0) * (sg * (1.0 + pre * (1.0 - sg)))
        dx = dpre[0:tc, :] * w[CONV - 1:CONV, :]
        for j in range(CONV - 1):
            dx = dx + pltpu.roll(dpre, tc + 8 - (CONV - 1 - j), 0)[0:tc, :] * w[j:j + 1, :]
        dx_ref[...] = dx.astype(BF16)
        dcur = dpre[0:tc, :]
        xe0 = xe[0:8 + tc, :]
        lane_row = lax.broadcasted_iota(jnp.int32, (CONV, C), 0)
        dw = jnp.zeros((CONV, C), F32)
        for j in range(CONV):
            sh = pltpu.roll(xe0, CONV - 1 - j, 0)[8:8 + tc, :] if j < CONV - 1 else xe0[8:8 + tc, :]
            dw = dw + jnp.where(lane_row == j, jnp.sum(sh * dcur, axis=0, keepdims=True), 0.0)

        @pl.when((b == 0) & (i == 0))
        def _():
            dw_ref[...] = jnp.zeros_like(dw_ref)

        dw_ref[...] += dw

    return pl.pallas_call(
        body, name="conv_bwd", grid=(B, nt),
        out_shape=[jax.ShapeDtypeStruct((B, S, C), BF16), jax.ShapeDtypeStruct((CONV, C), F32)],
        in_specs=[pl.BlockSpec((None, 8, C), lambda b, i: (b, jnp.maximum(i * r8 - 1, 0), 0)),
                  _rows(tc, C),
                  pl.BlockSpec((None, 8, C), lambda b, i: (b, jnp.minimum((i + 1) * r8, S // 8 - 1), 0)),
                  _rows(tc, C),
                  pl.BlockSpec((None, 8, C), lambda b, i: (b, jnp.minimum((i + 1) * r8, S // 8 - 1), 0)),
                  _full((CONV, C))],
        out_specs=[_rows(tc, C), _full((CONV, C))],
        compiler_params=_cparams(("arbitrary", "arbitrary")),
    )(xin, xin, xin, dy, dy, w)


def _softplus(x):
    return jnp.maximum(x, 0.0) + jnp.log1p(jnp.exp(-jnp.abs(x)))


_BMM = (((2,), (1,)), ((0,), (0,)))
_BMM_NT = (((2,), (2,)), ((0,), (0,)))
_BMM_TN = (((1,), (1,)), ((0,), (0,)))


def _bmm(a, b, dims=_BMM):
    return lax.dot_general(a.astype(BF16), b.astype(BF16), dims, preferred_element_type=F32)


def _split(a):
    hi = a.astype(BF16)
    return hi, (a - hi.astype(F32)).astype(BF16)


def _bmm3(a, b, dims=_BMM):
    ah, al = _split(a)
    bh, bl = _split(b)
    d = lambda p, q: lax.dot_general(p, q, dims, preferred_element_type=F32)
    return d(ah, bh) + (d(ah, bl) + d(al, bh))


def _tri_inverse(L):
    eye = (lax.broadcasted_iota(jnp.int32, (CH, CH), 0) == lax.broadcasted_iota(jnp.int32, (CH, CH), 1)).astype(F32)
    T = eye - L
    P = L
    n = 2
    while n < CH:
        P = _bmm3(P, P)
        T = T + _bmm3(T, P)
        n *= 2
    return T


@jax.custom_vjp
def _tri_inverse_known(L, T):
    return T


def _tri_inverse_known_fwd(L, T):
    return T, T


def _tri_inverse_known_bwd(T, dT):
    return -_bmm3(T, _bmm3(dT, T, _BMM_NT), _BMM_TN), jnp.zeros_like(T)


_tri_inverse_known.defvjp(_tri_inverse_known_fwd, _tri_inverse_known_bwd)


def _cumsum_rows(g):
    n = g.shape[0]
    ii = lax.broadcasted_iota(jnp.int32, (n, CH, CH), 1)
    jj = lax.broadcasted_iota(jnp.int32, (n, CH, CH), 2)
    tri = (ii >= jj).astype(BF16)
    g0 = g.astype(BF16)
    r1 = g - g0.astype(F32)
    g1 = r1.astype(BF16)
    g2 = (r1 - g1.astype(F32)).astype(BF16)
    d = lambda q: lax.dot_general(tri, q, _BMM, preferred_element_type=F32)
    return d(g0) + (d(g1) + d(g2))


def _dn_prep(t_known, qr, kr, v, a_raw, b_raw, a_log, dt_b):
    n = qr.shape[0]
    ii = lax.broadcasted_iota(jnp.int32, (n, CH, CH), 1)
    jj = lax.broadcasted_iota(jnp.int32, (n, CH, CH), 2)
    incl, strict = ii >= jj, ii > jj
    q = qr * lax.rsqrt(jnp.sum(qr * qr, axis=-1, keepdims=True) + EPS) * (DK ** -0.5)
    k = kr * lax.rsqrt(jnp.sum(kr * kr, axis=-1, keepdims=True) + EPS)
    beta = _sigmoid(b_raw)
    g = -jnp.exp(a_log) * _softplus(a_raw + dt_b)
    gcb = _cumsum_rows(jnp.broadcast_to(g, (n, CH, DK)))
    gc = gcb[:, :, 0:1]
    gc_row = jnp.swapaxes(gcb, 1, 2)[:, 0:1, 0:CH]
    decay = jnp.where(incl, jnp.exp(jnp.where(incl, gc - gc_row, 0.0)), 0.0)
    kb = k * beta
    L = jnp.where(strict, _bmm(kb, k, _BMM_NT) * decay, 0.0)
    T = _tri_inverse(L) if t_known is None else _tri_inverse_known(L, t_known)
    eg = jnp.exp(gc)
    u = _bmm(T, v * beta)
    w = _bmm(T, kb * eg)
    a_in = _bmm(q, k, _BMM_NT) * decay
    g_last = gc[:, CH - 1:CH, :]
    return u, w, q * eg, k * jnp.exp(g_last - gc), a_in, jnp.exp(g_last), T


def _dn_step(S0, u, w, qd, kd, a_in, cd):
    r = _bmm(jnp.concatenate([w, qd], axis=1), S0)
    v_new = u - r[:, 0:CH, :]
    o = r[:, CH:2 * CH, :] + _bmm(a_in, v_new)
    S1 = S0 * cd + _bmm(kd, v_new, _BMM_TN)
    return o, S1


def _dn_stack(cq, ba, al, dt, G):
    cols = [[] for _ in range(7)]
    for c in range(G):
        rows = slice(CH * c, CH * (c + 1))
        for h in range(DH):
            parts = (cq[rows, DK * h:DK * (h + 1)], cq[rows, DNW + DK * h:DNW + DK * (h + 1)],
                     cq[rows, 2 * DNW + DK * h:2 * DNW + DK * (h + 1)], ba[rows, DH + h:DH + h + 1],
                     ba[rows, h:h + 1], al[:, h:h + 1], dt[:, h:h + 1])
            for col, p in zip(cols, parts):
                col.append(p)
    return tuple(jnp.stack(col) for col in cols)


def _dn_group(S, want):
    g = want
    while (S // CH) % g:
        g //= 2
    return g


def _dn_prep_fwd(cq, ba, a_log, dt_b):
    B, S, _ = cq.shape
    nc = S // CH
    G = _dn_group(S, 4)

    def body(cq_ref, ba_ref, al_ref, dt_ref, u_ref, w_ref, qd_ref, kd_ref, a_ref, t_ref, cd_ref):
        ops = _dn_stack(cq_ref[...], ba_ref[...], al_ref[...], dt_ref[...], G)
        u, w, qd, kd, a_in, cd, T = _dn_prep(None, *ops)
        lane4 = lax.broadcasted_iota(jnp.int32, (1, DH), 1)
        for c in range(G):
            rows = slice(CH * c, CH * (c + 1))
            cdrow = jnp.zeros((1, DH), F32)
            for h in range(DH):
                n = DH * c + h
                lanes = slice(DK * h, DK * (h + 1))
                u_ref[rows, lanes] = u[n]
                w_ref[rows, lanes] = w[n]
                qd_ref[rows, lanes] = qd[n]
                kd_ref[rows, lanes] = kd[n]
                a_ref[rows, CH * h:CH * (h + 1)] = a_in[n]
                t_ref[rows, CH * h:CH * (h + 1)] = T[n]
                cdrow = cdrow + jnp.where(lane4 == h, cd[n], 0.0)
            cd_ref[c] = cdrow

    wide = jax.ShapeDtypeStruct((B, S, DNW), F32)
    sq = jax.ShapeDtypeStruct((B, S, DH * CH), F32)
    return pl.pallas_call(
        body, name="dn_prep_fwd", grid=(B, nc // G),
        out_shape=[wide, wide, wide, wide, sq, sq, jax.ShapeDtypeStruct((B, nc, 1, DH), F32)],
        in_specs=[_rows(G * CH, CONVW), _rows(G * CH, 2 * DH), _full((1, DH)), _full((1, DH))],
        out_specs=[_rows(G * CH, DNW)] * 4 + [_rows(G * CH, DH * CH)] * 2
                  + [pl.BlockSpec((None, G, 1, DH), lambda b, i: (b, i, 0, 0))],
        compiler_params=_cparams(("parallel", "parallel")),
    )(cq, ba, a_log, dt_b)


def _dn_seq_specs(B, nc, rev):
    at = (lambda i: nc - 1 - i) if rev else (lambda i: i)
    wide = pl.BlockSpec((B, CH, DNW), lambda i: (0, at(i), 0))
    a_spec = pl.BlockSpec((B, CH, DH * CH), lambda i: (0, at(i), 0))
    cd_spec = pl.BlockSpec((B, None, 1, DH), lambda i: (0, at(i), 0, 0))
    st_spec = pl.BlockSpec((B, None, DH, DK, DK), lambda i: (0, at(i), 0, 0, 0))
    return wide, a_spec, cd_spec, st_spec


def _dn_step_operands(B, u_ref, w_ref, qd_ref, kd_ref, a_ref, cd_ref):
    pairs = [(b, h) for b in range(B) for h in range(DH)]
    wide = lambda ref: jnp.stack([ref[b, :, DK * h:DK * (h + 1)] for b, h in pairs])
    a_in = jnp.stack([a_ref[b, :, CH * h:CH * (h + 1)] for b, h in pairs])
    cd = jnp.stack([cd_ref[b, :, h:h + 1] for b, h in pairs])
    return wide(u_ref), wide(w_ref), wide(qd_ref), wide(kd_ref), a_in, cd


def _dn_seq_fwd(u, w, qd, kd, a_in, cd):
    B, S, _ = u.shape
    nc = S // CH

    def body(u_ref, w_ref, qd_ref, kd_ref, a_ref, cd_ref, o_ref, st_ref, state):
        @pl.when(pl.program_id(0) == 0)
        def _():
            state[...] = jnp.zeros_like(state)

        S0 = state[...]
        for b in range(B):
            st_ref[b] = S0[DH * b:DH * (b + 1)]
        o, S1 = _dn_step(S0, *_dn_step_operands(B, u_ref, w_ref, qd_ref, kd_ref, a_ref, cd_ref))
        state[...] = S1
        for b in range(B):
            for h in range(DH):
                o_ref[b, :, DK * h:DK * (h + 1)] = o[DH * b + h]

    wide, a_spec, cd_spec, st_spec = _dn_seq_specs(B, nc, False)
    return pl.pallas_call(
        body, name="dn_seq_fwd", grid=(nc,),
        out_shape=[jax.ShapeDtypeStruct((B, S, DNW), F32), jax.ShapeDtypeStruct((B, nc, DH, DK, DK), F32)],
        in_specs=[wide, wide, wide, wide, a_spec, cd_spec],
        out_specs=[wide, st_spec],
        scratch_shapes=[pltpu.VMEM((B * DH, DK, DK), F32)],
        compiler_params=_cparams(("arbitrary",)),
    )(u, w, qd, kd, a_in, cd)


def _dn_seq_bwd(u, w, qd, kd, a_in, cd, states, do):
    B, S, _ = u.shape
    nc = S // CH

    def body(u_ref, w_ref, qd_ref, kd_ref, a_ref, cd_ref, st_ref, do_ref,
             du_ref, dw_ref, dqd_ref, dkd_ref, da_ref, dcd_ref, dstate):
        @pl.when(pl.program_id(0) == 0)
        def _():
            dstate[...] = jnp.zeros_like(dstate)

        lane4 = lax.broadcasted_iota(jnp.int32, (1, DH), 1)
        S0 = jnp.concatenate([st_ref[b] for b in range(B)], axis=0)
        do = jnp.stack([do_ref[b, :, DK * h:DK * (h + 1)] for b in range(B) for h in range(DH)])
        _, vjp = jax.vjp(_dn_step, S0, *_dn_step_operands(B, u_ref, w_ref, qd_ref, kd_ref, a_ref, cd_ref))
        dS, du, dw, dqd, dkd, da, dcd = vjp((do, dstate[...]))
        dstate[...] = dS
        for b in range(B):
            dcdrow = jnp.zeros((1, DH), F32)
            for h in range(DH):
                n = DH * b + h
                lanes = slice(DK * h, DK * (h + 1))
                du_ref[b, :, lanes] = du[n]
                dw_ref[b, :, lanes] = dw[n]
                dqd_ref[b, :, lanes] = dqd[n]
                dkd_ref[b, :, lanes] = dkd[n]
                da_ref[b, :, CH * h:CH * (h + 1)] = da[n]
                dcdrow = dcdrow + jnp.where(lane4 == h, dcd[n], 0.0)
            dcd_ref[b] = dcdrow

    wide, a_spec, cd_spec, st_spec = _dn_seq_specs(B, nc, True)
    sd = jax.ShapeDtypeStruct((B, S, DNW), F32)
    return pl.pallas_call(
        body, name="dn_seq_bwd", grid=(nc,),
        out_shape=[sd, sd, sd, sd, jax.ShapeDtypeStruct((B, S, DH * CH), F32), jax.ShapeDtypeStruct((B, nc, 1, DH), F32)],
        in_specs=[wide, wide, wide, wide, a_spec, cd_spec, st_spec, wide],
        out_specs=[wide, wide, wide, wide, a_spec, cd_spec],
        scratch_shapes=[pltpu.VMEM((B * DH, DK, DK), F32)],
        compiler_params=_cparams(("arbitrary",)),
    )(u, w, qd, kd, a_in, cd, states, do)


def _dn_prep_bwd(cq, ba, a_log, dt_b, t_inv, du, dw, dqd, dkd, da, dcd):
    B, S, _ = cq.shape
    nc = S // CH
    G = _dn_group(S, 4)

    def body(cq_ref, ba_ref, al_ref, dt_ref, t_ref, du_ref, dw_ref, dqd_ref, dkd_ref, da_ref, dcd_ref,
             dcq_ref, dba_ref, dal_ref, ddt_ref):
        @pl.when((pl.program_id(0) == 0) & (pl.program_id(1) == 0))
        def _():
            dal_ref[...] = jnp.zeros_like(dal_ref)
            ddt_ref[...] = jnp.zeros_like(ddt_ref)

        pairs = [(c, h) for c in range(G) for h in range(DH)]
        rows = lambda c: slice(CH * c, CH * (c + 1))
        wide = lambda ref: jnp.stack([ref[rows(c), DK * h:DK * (h + 1)] for c, h in pairs])
        square = lambda ref: jnp.stack([ref[rows(c), CH * h:CH * (h + 1)] for c, h in pairs])
        ops = _dn_stack(cq_ref[...], ba_ref[...], al_ref[...], dt_ref[...], G)
        cots = (wide(du_ref), wide(dw_ref), wide(dqd_ref), wide(dkd_ref), square(da_ref),
                jnp.stack([dcd_ref[c][:, h:h + 1] for c, h in pairs]), jnp.zeros((len(pairs), CH, CH), F32))
        _, vjp = jax.vjp(functools.partial(_dn_prep, square(t_ref)), *ops)
        dq, dk, dv, dar, dbr, dl, dd = vjp(cots)
        lane8 = lax.broadcasted_iota(jnp.int32, (CH, 2 * DH), 1)
        lane4 = lax.broadcasted_iota(jnp.int32, (1, DH), 1)
        dal = jnp.zeros((1, DH), F32)
        ddt = jnp.zeros((1, DH), F32)
        for c in range(G):
            dba = jnp.zeros((CH, 2 * DH), F32)
            for h in range(DH):
                n = DH * c + h
                dcq_ref[rows(c), DK * h:DK * (h + 1)] = dq[n]
                dcq_ref[rows(c), DNW + DK * h:DNW + DK * (h + 1)] = dk[n]
                dcq_ref[rows(c), 2 * DNW + DK * h:2 * DNW + DK * (h + 1)] = dv[n]
                dba = dba + jnp.where(lane8 == h, dbr[n], 0.0) + jnp.where(lane8 == DH + h, dar[n], 0.0)
                dal = dal + jnp.where(lane4 == h, dl[n], 0.0)
                ddt = ddt + jnp.where(lane4 == h, dd[n], 0.0)
            dba_ref[rows(c), :] = dba.astype(BF16)
        dal_ref[...] += dal
        ddt_ref[...] += ddt

    return pl.pallas_call(
        body, name="dn_prep_bwd", grid=(B, nc // G),
        out_shape=[jax.ShapeDtypeStruct((B, S, CONVW), F32), jax.ShapeDtypeStruct((B, S, 2 * DH), BF16),
                   jax.ShapeDtypeStruct((1, DH), F32), jax.ShapeDtypeStruct((1, DH), F32)],
        in_specs=[_rows(G * CH, CONVW), _rows(G * CH, 2 * DH), _full((1, DH)), _full((1, DH)), _rows(G * CH, DH * CH)]
                 + [_rows(G * CH, DNW)] * 4 + [_rows(G * CH, DH * CH),
                                               pl.BlockSpec((None, G, 1, DH), lambda b, i: (b, i, 0, 0))],
        out_specs=[_rows(G * CH, CONVW), _rows(G * CH, 2 * DH), _full((1, DH)), _full((1, DH))],
        compiler_params=_cparams(("arbitrary", "arbitrary")),
    )(cq, ba, a_log, dt_b, t_inv, du, dw, dqd, dkd, da, dcd)


def _gated_norm(o, z, g):
    outs = []
    for h in range(DH):
        t = o[:, DK * h:DK * (h + 1)]
        r = lax.rsqrt(jnp.mean(t * t, axis=-1, keepdims=True) + EPS)
        outs.append(t * r * g * _silu(z[:, DK * h:DK * (h + 1)]))
    return jnp.concatenate(outs, axis=1)


def _mix_fwd(x, o_attn, o_dn, z, ga, gd, mod, dn_g, w_branch, w_out):
    B, S, _ = x.shape
    tm = _tile(S)

    def body(x_ref, oa_ref, od_ref, z_ref, ga_ref, gd_ref, mod_ref, g_ref, wb_ref, wo_ref,
             x1_ref, mix_ref, mg_ref, ob_ref):
        oa = oa_ref[...].astype(BF16)
        od = _gated_norm(od_ref[...], z_ref[...], g_ref[...]).astype(BF16)
        ob_ref[0] = oa
        ob_ref[1] = od
        ya = jnp.dot(oa, wb_ref[0:QW, :], preferred_element_type=F32)
        yd = jnp.dot(od, wb_ref[QW:QW + DNW, :], preferred_element_type=F32)
        merged = (_sigmoid(ga_ref[...]) * ya + _sigmoid(gd_ref[...]) * yd).astype(BF16)
        mg_ref[...] = merged
        mix = jnp.dot(merged, wo_ref[...], preferred_element_type=F32)
        mix_ref[...] = mix
        x1_ref[...] = x_ref[...] + mod_ref[2:3, :] * mix

    return pl.pallas_call(
        body, name="mix_fwd", grid=(B, S // tm),
        out_shape=[jax.ShapeDtypeStruct((B, S, D), F32), jax.ShapeDtypeStruct((B, S, D), F32),
                   jax.ShapeDtypeStruct((B, S, D), BF16), jax.ShapeDtypeStruct((B, 2, S, QW), BF16)],
        in_specs=[_rows(tm, D), _rows(tm, QW), _rows(tm, DNW), _rows(tm, DNW), _rows(tm, D), _rows(tm, D),
                  _perb(6, D), _full((1, DK)), _resident(w_branch.shape), _resident(w_out.shape)],
        out_specs=[_rows(tm, D), _rows(tm, D), _rows(tm, D), _stacked(2, tm, QW)],
        compiler_params=_cparams(("parallel", "arbitrary")),
    )(x, o_attn, o_dn, z, ga, gd, mod, dn_g, w_branch, w_out)


def _mix_bwd(dx1, mix, o_attn, o_dn, z, ga, gd, mod, dn_g, w_branch, w_out):
    B, S, _ = dx1.shape
    tm = _tile(S)

    def body(dx1_ref, mix_ref, oa_ref, od_ref, z_ref, ga_ref, gd_ref, mod_ref, g_ref, wb_ref, wo_ref,
             dmix_ref, dyo_ref, dga_ref, dgd_ref, dz_ref, doa_ref, dod_ref, dgate_ref, dg_ref):
        b, i = pl.program_id(0), pl.program_id(1)
        dx1 = dx1_ref[...]
        dmix = (dx1 * mod_ref[2:3, :]).astype(BF16)
        dmix_ref[...] = dmix
        dgate = jnp.sum(dx1 * mix_ref[...], axis=0, keepdims=True)
        dmerged = _dot_nt(dmix, wo_ref[...])
        odn, gn_vjp = jax.vjp(_gated_norm, od_ref[...], z_ref[...], g_ref[...])
        ya = _dot(oa_ref[...], wb_ref[0:QW, :])
        yd = _dot(odn, wb_ref[QW:QW + DNW, :])
        sa, sd = _sigmoid(ga_ref[...]), _sigmoid(gd_ref[...])
        dya = (dmerged * sa).astype(BF16)
        dyd = (dmerged * sd).astype(BF16)
        dyo_ref[0] = dya
        dyo_ref[1] = dyd
        dga_ref[...] = (dmerged * ya * sa * (1.0 - sa)).astype(BF16)
        dgd_ref[...] = (dmerged * yd * sd * (1.0 - sd)).astype(BF16)
        doa_ref[...] = _dot_nt(dya, wb_ref[0:QW, :])
        dodn = _dot_nt(dyd, wb_ref[QW:QW + DNW, :])
        dod, dz, dg = gn_vjp(dodn)
        dod_ref[...] = dod
        dz_ref[...] = dz.astype(BF16)

        @pl.when(i == 0)
        def _():
            dgate_ref[...] = jnp.zeros_like(dgate_ref)

        @pl.when((b == 0) & (i == 0))
        def _():
            dg_ref[...] = jnp.zeros_like(dg_ref)

        dgate_ref[...] += dgate
        dg_ref[...] += dg

    return pl.pallas_call(
        body, name="mix_bwd", grid=(B, S // tm),
        out_shape=[jax.ShapeDtypeStruct((B, S, D), BF16), jax.ShapeDtypeStruct((B, 2, S, D), BF16),
                   jax.ShapeDtypeStruct((B, S, D), BF16), jax.ShapeDtypeStruct((B, S, D), BF16),
                   jax.ShapeDtypeStruct((B, S, DNW), BF16),
                   jax.ShapeDtypeStruct((B, S, QW), F32), jax.ShapeDtypeStruct((B, S, DNW), F32),
                   jax.ShapeDtypeStruct((B, 1, D), F32), jax.ShapeDtypeStruct((1, DK), F32)],
        in_specs=[_rows(tm, D), _rows(tm, D), _rows(tm, QW), _rows(tm, DNW), _rows(tm, DNW), _rows(tm, D),
                  _rows(tm, D), _perb(6, D), _full((1, DK)), _resident(w_branch.shape), _resident(w_out.shape)],
        out_specs=[_rows(tm, D), _stacked(2, tm, D), _rows(tm, D), _rows(tm, D), _rows(tm, DNW),
                   _rows(tm, QW), _rows(tm, DNW), _perb(1, D), _full((1, DK))],
        compiler_params=_cparams(("arbitrary", "arbitrary")),
    )(dx1, mix, o_attn, o_dn, z, ga, gd, mod, dn_g, w_branch, w_out)


GU_SHARD = 2 * FFN // N_DEV
GU_HALF = N_DEV // 2


def _ffn1_fwd(x1, mod, g2, w_gu):
    B, S, _ = x1.shape
    tm = _tile(S)

    def body(x_ref, mod_ref, g_ref, w_ref, h_ref, gate_ref, up_ref, act_ref):
        h = _rms_mod(x_ref[...], g_ref[...], mod_ref[4:5, :], mod_ref[3:4, :]).astype(BF16)
        h_ref[...] = h
        for j in range(GU_HALF):
            gate = jnp.dot(h, w_ref[j], preferred_element_type=F32)
            up = jnp.dot(h, w_ref[GU_HALF + j], preferred_element_type=F32)
            gate_ref[j] = gate
            up_ref[j] = up
            act_ref[j] = (_silu(gate) * up).astype(BF16)

    blk = lambda dt: jax.ShapeDtypeStruct((B, GU_HALF, S, GU_SHARD), dt)
    return pl.pallas_call(
        body, name="ffn1_fwd", grid=(B, S // tm),
        out_shape=[jax.ShapeDtypeStruct((B, S, D), BF16), blk(F32), blk(F32), blk(BF16)],
        in_specs=[_rows(tm, D), _perb(6, D), _full((1, D)), _resident(w_gu.shape)],
        out_specs=[_rows(tm, D)] + [_stacked(GU_HALF, tm, GU_SHARD)] * 3,
        compiler_params=_cparams(("parallel", "arbitrary")),
    )(x1, mod, g2, w_gu)


def _ffn2_fwd(act, x1, target, mod, w_down):
    B, S, _ = x1.shape
    tm = _tile(S)

    def body(a_ref, x_ref, t_ref, mod_ref, w_ref, dy_ref, loss_ref, dgate_ref):
        b, i = pl.program_id(0), pl.program_id(1)
        y = jnp.dot(a_ref[0], w_ref[0], preferred_element_type=F32)
        for j in range(1, GU_HALF):
            y = y + jnp.dot(a_ref[j], w_ref[j], preferred_element_type=F32)
        err = x_ref[...] + mod_ref[5:6, :] * y - t_ref[...]
        dy = err * (1.0 / D)
        dy_ref[...] = dy

        @pl.when((b == 0) & (i == 0))
        def _():
            loss_ref[...] = jnp.zeros_like(loss_ref)

        @pl.when(i == 0)
        def _():
            dgate_ref[...] = jnp.zeros_like(dgate_ref)

        loss_ref[...] += (0.5 / D) * jnp.sum(err * err)
        dgate_ref[...] += jnp.sum(dy * y, axis=0, keepdims=True)

    return pl.pallas_call(
        body, name="ffn2_fwd", grid=(B, S // tm),
        out_shape=[jax.ShapeDtypeStruct((B, S, D), F32), jax.ShapeDtypeStruct((1, 128), F32),
                   jax.ShapeDtypeStruct((B, 1, D), F32)],
        in_specs=[_stacked(GU_HALF, tm, GU_SHARD), _rows(tm, D), _rows(tm, D), _perb(6, D), _resident(w_down.shape)],
        out_specs=[_rows(tm, D), _full((1, 128)), _perb(1, D)],
        compiler_params=_cparams(("arbitrary", "arbitrary")),
    )(act, x1, target, mod, w_down)


def _ffn2_bwd(dy, gate, up, mod, w_down):
    B, S, _ = dy.shape
    tm = _tile(S)

    def body(dy_ref, gate_ref, up_ref, mod_ref, w_ref, dgu_ref, dyg_ref):
        dyg = (dy_ref[...] * mod_ref[5:6, :]).astype(BF16)
        dyg_ref[...] = dyg
        for j in range(GU_HALF):
            dact = _dot_nt(dyg, w_ref[j])
            gate, up = gate_ref[j], up_ref[j]
            sg = _sigmoid(gate)
            dgu_ref[j] = (dact * up * (sg * (1.0 + gate * (1.0 - sg)))).astype(BF16)
            dgu_ref[GU_HALF + j] = (dact * (gate * sg)).astype(BF16)

    return pl.pallas_call(
        body, name="ffn2_bwd", grid=(B, S // tm),
        out_shape=[jax.ShapeDtypeStruct((B, N_DEV, S, GU_SHARD), BF16), jax.ShapeDtypeStruct((B, S, D), BF16)],
        in_specs=[_rows(tm, D), _stacked(GU_HALF, tm, GU_SHARD), _stacked(GU_HALF, tm, GU_SHARD), _perb(6, D),
                  _resident(w_down.shape)],
        out_specs=[_stacked(N_DEV, tm, GU_SHARD), _rows(tm, D)],
        compiler_params=_cparams(("parallel", "arbitrary")),
    )(dy, gate, up, mod, w_down)


def _ffn1_bwd(dgu, x1, dy, mod, g2, w_gu):
    B, S, _ = x1.shape
    tm = _tile(S)

    def body(dgu_ref, x_ref, dy_ref, mod_ref, g_ref, w_ref, dx1_ref, dg_ref, dsc_ref, dsh_ref):
        b, i = pl.program_id(0), pl.program_id(1)
        dh = _dot_nt(dgu_ref[0], w_ref[0])
        for j in range(1, N_DEV):
            dh = dh + _dot_nt(dgu_ref[j], w_ref[j])
        _, vjp = jax.vjp(_rms_mod, x_ref[...], g_ref[...], mod_ref[4:5, :], mod_ref[3:4, :])
        dx, dg, dsc, dsh = vjp(dh)
        dx1_ref[...] = dy_ref[...] + dx

        @pl.when((b == 0) & (i == 0))
        def _():
            dg_ref[...] = jnp.zeros_like(dg_ref)

        @pl.when(i == 0)
        def _():
            dsc_ref[...] = jnp.zeros_like(dsc_ref)
            dsh_ref[...] = jnp.zeros_like(dsh_ref)

        dg_ref[...] += dg
        dsc_ref[...] += dsc
        dsh_ref[...] += dsh

    return pl.pallas_call(
        body, name="ffn1_bwd", grid=(B, S // tm),
        out_shape=[jax.ShapeDtypeStruct((B, S, D), F32), jax.ShapeDtypeStruct((1, D), F32),
                   jax.ShapeDtypeStruct((B, 1, D), F32), jax.ShapeDtypeStruct((B, 1, D), F32)],
        in_specs=[_stacked(N_DEV, tm, GU_SHARD), _rows(tm, D), _rows(tm, D), _perb(6, D), _full((1, D)),
                  _resident(w_gu.shape)],
        out_specs=[_rows(tm, D), _full((1, D)), _perb(1, D), _perb(1, D)],
        compiler_params=_cparams(("arbitrary", "arbitrary")),
    )(dgu, x1, dy, mod, g2, w_gu)


def _adamw(w, g, m, v, name):
    def body(w_ref, g_ref, m_ref, v_ref, d_ref, nm_ref, nv_ref):
        g = g_ref[...]
        m = B1 * m_ref[...] + (1.0 - B1) * g
        v = B2 * v_ref[...] + (1.0 - B2) * (g * g)
        nm_ref[...] = m
        nv_ref[...] = v
        m_hat = m / (1.0 - B1 ** STEP)
        v_hat = v / (1.0 - B2 ** STEP)
        d_ref[...] = -LR * (m_hat / (jnp.sqrt(v_hat) + AEPS) + WD * w_ref[...])

    sd = jax.ShapeDtypeStruct(w.shape, F32)
    return pl.pallas_call(body, name=name, out_shape=(sd, sd, sd), compiler_params=_cparams())(w, g, m, v)


def kernel(x, c, positions, ada_w, ada_b, norm1_g, w_in, conv_w, q_norm_g, k_norm_g, sinks, a_log, dt_bias, dn_norm_g, w_branch, w_out, norm2_g, w_gate_up, w_down, loss_target, m_ada_w, m_ada_b, m_norm1_g, m_w_in, m_conv_w, m_q_norm_g, m_k_norm_g, m_sinks, m_a_log, m_dt_bias, m_dn_norm_g, m_w_branch, m_w_out, m_norm2_g, m_w_gate_up, m_w_down, v_ada_w, v_ada_b, v_norm1_g, v_w_in, v_conv_w, v_q_norm_g, v_k_norm_g, v_sinks, v_a_log, v_dt_bias, v_dn_norm_g, v_w_branch, v_w_out, v_norm2_g, v_w_gate_up, v_w_down):
    B, S, _ = x.shape
    me = 4 * lax.axis_index("x") + 2 * lax.axis_index("y") + lax.axis_index("c")

    shards = [w[0].astype(BF16) for w in (w_in, w_branch, w_out, w_gate_up, w_down)]
    w_in_b, w_branch_g, w_out_g, w_gu_b, w_down_g = _all_gather_big(shards, "gather_weights")
    w_branch_f = w_branch_g.reshape(D, D)
    w_out_f = w_out_g.reshape(D, D)
    w_down_b = w_down_g.reshape(GU_HALF, GU_SHARD, D)

    c_all = _all_gather_small(c, "gather_c").reshape(N_DEV * B, D)
    ncol = 6 * D // N_DEV
    mod_cols, cond_all = _ada_fwd(c_all, ada_w[0], lax.dynamic_slice(ada_b, (0, me * ncol), (1, ncol)))
    mod_all = _all_gather_small(mod_cols, "gather_mod").transpose(1, 0, 2).reshape(N_DEV * B, 6 * D)
    mod = lax.dynamic_slice(mod_all, (me * B, 0), (B, 6 * D)).reshape(B, 6, D)

    h1, aq, akv, dnx, ba, z, ga, gd = _inproj_fwd(x, mod, norm1_g, w_in_b)
    invf, mean_q, mean_k = _attn_consts()
    rope_cos, rope_sin = _rope_tables(positions.reshape(B, S, 1), invf)
    o_attn = _attn_fwd(aq, akv, rope_cos, rope_sin, q_norm_g, k_norm_g, sinks, mean_q, mean_k)
    conv2 = conv_w.reshape(CONV, CONVW // N_DEV)
    conv_all = _all_gather_small(conv2, "gather_conv").transpose(1, 0, 2).reshape(CONV, CONVW)
    cq = _conv_fwd(dnx, conv_all)
    dn_u, dn_w, dn_qd, dn_kd, dn_a, dn_t, dn_cd = _dn_prep_fwd(cq, ba, a_log, dt_bias)
    o_dn, states = _dn_seq_fwd(dn_u, dn_w, dn_qd, dn_kd, dn_a, dn_cd)
    x1, mix, merged, ob = _mix_fwd(x, o_attn, o_dn, z, ga, gd, mod, dn_norm_g, w_branch_f, w_out_f)
    h2, gate, up, act = _ffn1_fwd(x1, mod, norm2_g, w_gu_b)
    dy, loss_part, d_gate2 = _ffn2_fwd(act, x1, loss_target, mod, w_down_b)
    loss = lax.psum(loss_part[0, 0], ("x", "y", "c"))

    one = lambda t: t.reshape(B, 1, S, t.shape[-1])
    dgu, dyg = _ffn2_bwd(dy, gate, up, mod, w_down_b)
    g_w_down = _wgrad(act, one(dyg), "wgrad_down")
    dx1, d_n2g, d_scale2, d_shift2 = _ffn1_bwd(dgu, x1, dy, mod, norm2_g, w_gu_b)
    g_w_gu = _wgrad(one(h2), dgu, "wgrad_gate_up")
    dmix, dyo, dga, dgd, dz, d_oa, d_od, d_gate1, d_dng = _mix_bwd(
        dx1, mix, o_attn, o_dn, z, ga, gd, mod, dn_norm_g, w_branch_f, w_out_f)
    g_w_out = _wgrad(one(merged), one(dmix), "wgrad_out")
    g_w_branch = _wgrad(ob, dyo, "wgrad_branch")
    d_dn = _dn_seq_bwd(dn_u, dn_w, dn_qd, dn_kd, dn_a, dn_cd, states, d_od)
    dcq, dba, d_alog, d_dtb = _dn_prep_bwd(cq, ba, a_log, dt_bias, dn_t, *d_dn)
    ddnx, d_conv = _conv_bwd(dnx, conv_all, dcq)
    daq, dakv, d_qg, d_kg, d_sinks = _attn_bwd(aq, akv, rope_cos, rope_sin, q_norm_g, k_norm_g, sinks, mean_q, mean_k, d_oa)
    dps = [daq, dakv, ddnx, dba, dz, dga, dgd]
    dblk, grad_x, d_n1g, d_scale1, d_shift1 = _inproj_bwd(x, mod, norm1_g, dx1, dps, w_in_b)
    g_w_in = _wgrad(one(h1), dblk, "wgrad_in")

    gparts = [g_w_in, g_w_branch.reshape(N_DEV, D // N_DEV, D), g_w_out.reshape(N_DEV, D // N_DEV, D),
              g_w_gu, g_w_down.reshape(N_DEV, FFN // N_DEV, D)]
    received = _exchange_blocks(gparts, "exchange_grads")
    grad_w_in, grad_w_branch, grad_w_out, grad_w_gu, grad_w_down = [
        _sum_blocks(r, "sum_grads_" + nm) for r, nm in zip(received, ["in", "branch", "out", "gate_up", "down"])]

    dmod = jnp.concatenate([d_shift1, d_scale1, d_gate1, d_shift2, d_scale2, d_gate2], axis=2).reshape(B, 6 * D)
    small = jnp.concatenate([d_n1g, d_qg, d_kg, d_sinks, d_alog, d_dtb, d_dng, d_n2g, d_conv.reshape(1, CONV * CONVW)], axis=1)
    nsm = small.shape[1]
    width = -(-max(6 * D, nsm) // 128) * 128
    rows = jnp.concatenate([jnp.pad(dmod, ((0, 0), (0, width - 6 * D))), jnp.pad(small, ((0, 8 - B - 1), (0, width - nsm)))], axis=0)
    rows_all = _all_gather_small(rows, "gather_small")
    dmod_all = rows_all[:, 0:B, 0:6 * D].reshape(N_DEV * B, 6 * D)
    dmod_cols = lax.dynamic_slice(dmod_all, (0, me * ncol), (N_DEV * B, ncol))
    grad_ada_w, grad_ada_b, small_sum = _ada_bwd(cond_all, dmod_all, dmod_cols, rows_all[:, B, :])
    sizes = [D, HD, HD, HQ, DH, DH, DK, D]
    so = np.cumsum([0] + sizes)
    g_n1, g_qg, g_kg, g_sk, g_al, g_dt, g_dn, g_n2 = [small_sum[:, so[i]:so[i + 1]] for i in range(8)]
    g_conv_all = small_sum[:, so[8]:so[8] + CONV * CONVW].reshape(CONV, N_DEV, CONVW // N_DEV)
    grad_conv = lax.dynamic_slice(g_conv_all, (0, me, 0), (CONV, 1, CONVW // N_DEV)).reshape(CONV, CONVW // N_DEV)

    big = [(ada_w, grad_ada_w.reshape(ada_w.shape), m_ada_w, v_ada_w), (w_in, grad_w_in, m_w_in, v_w_in),
           (w_branch, grad_w_branch, m_w_branch, v_w_branch), (w_out, grad_w_out, m_w_out, v_w_out),
           (w_gate_up, grad_w_gu, m_w_gate_up, v_w_gate_up), (w_down, grad_w_down, m_w_down, v_w_down)]
    upd = {}
    for nm, (w, g, m, v) in zip(["ada_w", "w_in", "w_branch", "w_out", "w_gate_up", "w_down"], big):
        upd[nm] = _adamw(w, g, m, v, "adamw_" + nm)
    small_names = ["ada_b", "norm1_g", "q_norm_g", "k_norm_g", "sinks", "a_log", "dt_bias", "dn_norm_g", "norm2_g", "conv_w"]
    small_w = [ada_b, norm1_g, q_norm_g, k_norm_g, sinks, a_log, dt_bias, dn_norm_g, norm2_g, conv_w]
    small_g = [grad_ada_b, g_n1, g_qg, g_kg, g_sk, g_al, g_dt, g_dn, g_n2, grad_conv]
    small_m = [m_ada_b, m_norm1_g, m_q_norm_g, m_k_norm_g, m_sinks, m_a_log, m_dt_bias, m_dn_norm_g, m_norm2_g, m_conv_w]
    small_v = [v_ada_b, v_norm1_g, v_q_norm_g, v_k_norm_g, v_sinks, v_a_log, v_dt_bias, v_dn_norm_g, v_norm2_g, v_conv_w]
    cat = lambda arrs: jnp.concatenate([a.reshape(1, -1) for a in arrs], axis=1)
    res = _adamw(cat(small_w), cat(small_g), cat(small_m), cat(small_v), "adamw_small")
    po = np.cumsum([0] + [int(np.prod(w.shape)) for w in small_w])
    grads = {}
    for i, nm in enumerate(small_names):
        upd[nm] = tuple(r[:, po[i]:po[i + 1]].reshape(small_w[i].shape) for r in res)
        grads[nm] = small_g[i].reshape(small_w[i].shape)
    grads.update(ada_w=grad_ada_w.reshape(ada_w.shape), w_in=grad_w_in, w_branch=grad_w_branch, w_out=grad_w_out,
                 w_gate_up=grad_w_gu, w_down=grad_w_down)

    order = ["ada_w", "ada_b", "norm1_g", "w_in", "conv_w", "q_norm_g", "k_norm_g", "sinks", "a_log", "dt_bias",
             "dn_norm_g", "w_branch", "w_out", "norm2_g", "w_gate_up", "w_down"]
    return (loss, grad_x, *[grads[n] for n in order], *[upd[n][0] for n in order],
            *[upd[n][1] for n in order], *[upd[n][2] for n in order])
```

```python
import functools

import numpy as np
import jax
import jax.numpy as jnp
from jax import lax
from jax.experimental import pallas as pl
from jax.experimental.pallas import tpu as pltpu

F32 = jnp.float32
BF16 = jnp.bfloat16
HI = lax.Precision.HIGHEST

N_DEV = 8
D = 1024
HQ, HKV, HD = 8, 2, 64
GRP = HQ // HKV
BLK = 128
ROT = HD // 4
THETA = 500000.0
QW, KVW = HQ * HD, HKV * HD
DH, DK = 4, 128
CH = 64
DNW = DH * DK
CONV = 4
CONVW = 3 * DNW
FFN = 2816
EPS = 1e-6
IN_W = QW + 2 * KVW + CONVW + 2 * DH + DNW + 2 * D

LR, B1, B2, AEPS, WD, STEP = 0.001, 0.9, 0.999, 1e-08, 0.01, 10

VMEM_LIMIT = 56 * 1024 * 1024
MESH = pl.DeviceIdType.MESH


def _cparams(sem=None, vmem=VMEM_LIMIT):
    return pltpu.CompilerParams(dimension_semantics=sem, vmem_limit_bytes=vmem)


def _full(shape):
    n = len(shape)
    return pl.BlockSpec(shape, lambda *_: (0,) * n)


def _resident(shape):
    n = len(shape)
    return pl.BlockSpec(shape, lambda *_: (0,) * n, pipeline_mode=pl.Buffered(1))


def _rows(tm, w):
    return pl.BlockSpec((None, tm, w), lambda b, i: (b, i, 0))


def _stacked(n, tm, w):
    return pl.BlockSpec((None, n, tm, w), lambda b, i: (b, 0, i, 0))


def _perb(r, w):
    return pl.BlockSpec((None, r, w), lambda b, i: (b, 0, 0))


def _dot(a, b):
    return jnp.dot(a.astype(BF16), b.astype(BF16), preferred_element_type=F32)


def _dot_nt(a, b):
    return lax.dot_general(a.astype(BF16), b.astype(BF16), (((1,), (1,)), ((), ())), preferred_element_type=F32)


def _dot_tn(a, b):
    return lax.dot_general(a.astype(BF16), b.astype(BF16), (((0,), (0,)), ((), ())), preferred_element_type=F32)


def _dot_hi(a, b):
    return jnp.dot(a, b, preferred_element_type=F32, precision=HI)


def _sigmoid(x):
    return jax.nn.sigmoid(x)


def _silu(x):
    return x * jax.nn.sigmoid(x)


def _rms_mod(x, g, scale, shift):
    r = lax.rsqrt(jnp.mean(x * x, axis=-1, keepdims=True) + EPS)
    return (x * r * g) * (1.0 + scale) + shift


def _tile(S):
    return min(256, S)


def _peer(x, y, c, k):
    px = 1 - x if (k >> 2) & 1 else x
    py = 1 - y if (k >> 1) & 1 else y
    pc = 1 - c if k & 1 else c
    return px, py, pc


def _all_gather_small(v, name):
    r, n = v.shape

    def body(v_ref, out_ref, send_sems, recv_sems, local_sem):
        x, y, c = lax.axis_index("x"), lax.axis_index("y"), lax.axis_index("c")
        me = 4 * x + 2 * y + c
        mine = pltpu.make_async_copy(v_ref, out_ref.at[me], local_sem)
        mine.start()
        sends = []
        for k in range(1, N_DEV):
            cp = pltpu.make_async_remote_copy(
                src_ref=v_ref, dst_ref=out_ref.at[me], send_sem=send_sems.at[k - 1], recv_sem=recv_sems.at[k - 1],
                device_id=_peer(x, y, c, k), device_id_type=MESH)
            cp.start()
            sends.append(cp)
        for k in range(1, N_DEV):
            px, py, pc = _peer(x, y, c, k)
            pltpu.make_async_remote_copy(
                src_ref=v_ref, dst_ref=out_ref.at[4 * px + 2 * py + pc], send_sem=send_sems.at[k - 1],
                recv_sem=recv_sems.at[k - 1], device_id=(px, py, pc), device_id_type=MESH).wait_recv()
        for cp in sends:
            cp.wait_send()
        mine.wait()

    return pl.pallas_call(
        body, name=name,
        out_shape=jax.ShapeDtypeStruct((N_DEV, r, n), v.dtype),
        in_specs=[pl.BlockSpec(memory_space=pltpu.VMEM)],
        out_specs=pl.BlockSpec(memory_space=pltpu.VMEM),
        scratch_shapes=[pltpu.SemaphoreType.DMA((N_DEV - 1,)), pltpu.SemaphoreType.DMA((N_DEV - 1,)), pltpu.SemaphoreType.DMA],
    )(v)


def _all_gather_big(vs, name):
    na = len(vs)

    def body(*refs):
        v_refs, out_refs = refs[:na], refs[na:2 * na]
        send_sems, recv_sems, local_sems = refs[2 * na:]
        x, y, c = lax.axis_index("x"), lax.axis_index("y"), lax.axis_index("c")
        me, sibling = (x, y, c), (x, y, 1 - c)
        chips = [(1 - x, y), (x, 1 - y), (1 - x, 1 - y)]

        def rows(a, px, py, pc):
            return out_refs[a].at[4 * px + 2 * py + pc]

        def copy(a, k, block, to, src=None):
            return pltpu.make_async_remote_copy(
                src_ref=rows(a, *block) if src is None else src, dst_ref=rows(a, *block),
                send_sem=send_sems.at[7 * a + k], recv_sem=recv_sems.at[7 * a + k], device_id=to, device_id_type=MESH)

        mine = [pltpu.make_async_copy(v_refs[a], rows(a, *me), local_sems.at[a]) for a in range(na)]
        for cp in mine:
            cp.start()
        first = []
        for a in range(na):
            first.append(copy(a, 0, me, sibling, src=v_refs[a]))
            first += [copy(a, 1 + j, me, (*chip, c), src=v_refs[a]) for j, chip in enumerate(chips)]
        for cp in first:
            cp.start()
        passed = []
        for j, chip in enumerate(chips):
            for a in range(na):
                copy(a, 1 + j, (*chip, c), me).wait_recv()
                forward = copy(a, 4 + j, (*chip, c), sibling)
                forward.start()
                passed.append(forward)
        for a in range(na):
            copy(a, 0, sibling, me).wait_recv()
            for j, chip in enumerate(chips):
                copy(a, 4 + j, (*chip, 1 - c), me).wait_recv()
        for cp in first + passed:
            cp.wait_send()
        for cp in mine:
            cp.wait()

    return pl.pallas_call(
        body, name=name,
        out_shape=[jax.ShapeDtypeStruct((N_DEV,) + v.shape, v.dtype) for v in vs],
        in_specs=[pl.BlockSpec(memory_space=pl.ANY)] * na,
        out_specs=[pl.BlockSpec(memory_space=pl.ANY)] * na,
        scratch_shapes=[pltpu.SemaphoreType.DMA((7 * na,)), pltpu.SemaphoreType.DMA((7 * na,)),
                        pltpu.SemaphoreType.DMA((na,))],
    )(*vs)


def _exchange_blocks(gs, name):
    na = len(gs)

    def body(*refs):
        g_refs, out_refs = refs[:na], refs[na:2 * na]
        send_sems, recv_sems, local_sems = refs[2 * na:]
        x, y, c = lax.axis_index("x"), lax.axis_index("y"), lax.axis_index("c")
        me = 4 * x + 2 * y + c
        mine = [pltpu.make_async_copy(g_refs[a].at[me], out_refs[a].at[me], local_sems.at[a]) for a in range(na)]
        for cp in mine:
            cp.start()
        sends = []
        for k in range(1, N_DEV):
            px, py, pc = _peer(x, y, c, k)
            for a in range(na):
                cp = pltpu.make_async_remote_copy(
                    src_ref=g_refs[a].at[4 * px + 2 * py + pc], dst_ref=out_refs[a].at[me],
                    send_sem=send_sems.at[7 * a + k - 1], recv_sem=recv_sems.at[7 * a + k - 1],
                    device_id=(px, py, pc), device_id_type=MESH)
                cp.start()
                sends.append(cp)
        for k in range(1, N_DEV):
            px, py, pc = _peer(x, y, c, k)
            for a in range(na):
                pltpu.make_async_remote_copy(
                    src_ref=g_refs[a].at[me], dst_ref=out_refs[a].at[4 * px + 2 * py + pc],
                    send_sem=send_sems.at[7 * a + k - 1], recv_sem=recv_sems.at[7 * a + k - 1],
                    device_id=(px, py, pc), device_id_type=MESH).wait_recv()
        for cp in sends:
            cp.wait_send()
        for cp in mine:
            cp.wait()

    return pl.pallas_call(
        body, name=name,
        out_shape=[jax.ShapeDtypeStruct(g.shape, g.dtype) for g in gs],
        in_specs=[pl.BlockSpec(memory_space=pl.ANY)] * na,
        out_specs=[pl.BlockSpec(memory_space=pl.ANY)] * na,
        scratch_shapes=[pltpu.SemaphoreType.DMA((7 * na,)), pltpu.SemaphoreType.DMA((7 * na,)),
                        pltpu.SemaphoreType.DMA((na,))],
    )(*gs)


_HBM = pl.BlockSpec(memory_space=pltpu.HBM)
_SEM = pl.BlockSpec(memory_space=pltpu.SEMAPHORE)
_EFFECT = pltpu.SideEffectType.DATAFLOW_SIDE_EFFECTING


def _place_own(block, me):
    land = lax.empty((N_DEV,) + block.shape, block.dtype)
    return lax.dynamic_update_slice(land, block[None], (me,) + (0,) * block.ndim)


def _copies_start(srcs, lands, scatter, name):
    na = len(srcs)

    def body(*refs):
        src_refs, land_refs = refs[:na], refs[na:2 * na]
        sems = refs[2 * na:4 * na]
        token = refs[-1]
        x, y, c = lax.axis_index("x"), lax.axis_index("y"), lax.axis_index("c")
        me = 4 * x + 2 * y + c
        for a in range(na):
            for k in range(1, N_DEV):
                px, py, pc = _peer(x, y, c, k)
                src = src_refs[a].at[4 * px + 2 * py + pc] if scatter else src_refs[a]
                pltpu.make_async_remote_copy(
                    src_ref=src, dst_ref=land_refs[a].at[me], send_sem=sems[2 * a], recv_sem=sems[2 * a + 1],
                    device_id=(px, py, pc), device_id_type=MESH).start()
        token[...] = jnp.zeros_like(token)

    hbm = lambda t: pltpu.HBM(t.shape, t.dtype)
    out = pl.pallas_call(
        body, name=name,
        out_shape=tuple([pltpu.SemaphoreType.DMA(())] * (2 * na) + [hbm(t) for t in srcs] + [hbm(t) for t in lands]
                        + [jax.ShapeDtypeStruct((8, 128), F32)]),
        in_specs=[_HBM] * (2 * na),
        out_specs=tuple([_SEM] * (2 * na) + [_HBM] * (2 * na) + [pl.BlockSpec(memory_space=pltpu.VMEM)]),
        input_output_aliases={i: 2 * na + i for i in range(2 * na)},
        compiler_params=pltpu.CompilerParams(has_side_effects=_EFFECT),
    )(*[pltpu.with_memory_space_constraint(t, pltpu.HBM) for t in list(srcs) + list(lands)])
    return out[:2 * na], out[2 * na:3 * na], out[3 * na:4 * na], out[-1]


def _exchange_start(gs, me, name):
    own = [lax.dynamic_index_in_dim(g, me, 0, keepdims=False) for g in gs]
    return _copies_start(gs, [_place_own(o, me) for o in own], True, name)


def _copies_wait(sems, srcs, lands, after, name):
    na = len(srcs)

    def body(*refs):
        land_refs = refs[na:2 * na]
        sem_refs = refs[2 * na:4 * na]
        x, y, c = lax.axis_index("x"), lax.axis_index("y"), lax.axis_index("c")
        for a in range(na):
            seven = land_refs[a].at[pl.ds(0, N_DEV - 1)]
            copy = pltpu.make_async_remote_copy(
                src_ref=seven, dst_ref=seven, send_sem=sem_refs[2 * a], recv_sem=sem_refs[2 * a + 1],
                device_id=(x, y, c), device_id_type=MESH)
            copy.wait_send()
            copy.wait_recv()

    hbm = lambda t: pltpu.HBM(t.shape, t.dtype)
    out = pl.pallas_call(
        body, name=name,
        out_shape=tuple([hbm(t) for t in srcs] + [hbm(t) for t in lands]),
        in_specs=[_HBM] * (2 * na) + [_SEM] * (2 * na) + [pl.BlockSpec(memory_space=pl.ANY)],
        out_specs=tuple([_HBM] * (2 * na)),
        input_output_aliases={i: i for i in range(2 * na)},
        compiler_params=pltpu.CompilerParams(has_side_effects=_EFFECT),
    )(*srcs, *lands, *sems, after)
    return out[na:]


def _sum_blocks(g, name):
    _, r, n = g.shape
    tr = 256 if r % 256 == 0 else r

    def body(g_ref, o_ref):
        acc = g_ref[0].astype(F32)
        for d in range(1, N_DEV):
            acc = acc + g_ref[d].astype(F32)
        o_ref[...] = acc

    return pl.pallas_call(
        body, name=name, grid=(r // tr,),
        out_shape=jax.ShapeDtypeStruct((1, r, n), F32),
        in_specs=[pl.BlockSpec((N_DEV, tr, n), lambda i: (0, i, 0))],
        out_specs=pl.BlockSpec((None, tr, n), lambda i: (0, i, 0)),
        compiler_params=_cparams(("arbitrary",)),
    )(g)


def _ada_fwd(c_all, ada_w, ada_b_cols):
    nb, ncol = c_all.shape[0], ada_w.shape[1]

    def body(c_ref, w_ref, b_ref, mod_ref, cond_ref):
        cond = _silu(c_ref[...])
        cond_ref[...] = cond
        mod_ref[...] = _dot_hi(cond, w_ref[...]) + b_ref[...]

    return pl.pallas_call(
        body, name="ada_fwd",
        out_shape=(jax.ShapeDtypeStruct((nb, ncol), F32), jax.ShapeDtypeStruct((nb, D), F32)),
        compiler_params=_cparams(),
    )(c_all, ada_w, ada_b_cols)


def _ada_bwd(cond_all, dmod_all, dmod_cols, smalls):
    ncol, nsm = dmod_cols.shape[1], smalls.shape[1]

    def body(cond_ref, dm_ref, dmc_ref, sm_ref, gw_ref, gb_ref, gs_ref):
        gw_ref[...] = lax.dot_general(cond_ref[...], dmc_ref[...], (((0,), (0,)), ((), ())),
                                      preferred_element_type=F32, precision=HI)
        gb_ref[...] = jnp.sum(dm_ref[...], axis=0, keepdims=True)
        gs_ref[...] = jnp.sum(sm_ref[...], axis=0, keepdims=True)

    return pl.pallas_call(
        body, name="ada_bwd",
        out_shape=(jax.ShapeDtypeStruct((D, ncol), F32), jax.ShapeDtypeStruct((1, 6 * D), F32),
                   jax.ShapeDtypeStruct((1, nsm), F32)),
        compiler_params=_cparams(),
    )(cond_all, dmod_all, dmod_cols, smalls)


IN_CUTS = (0, QW, QW + 2 * KVW, QW + 2 * KVW + CONVW, QW + 2 * KVW + CONVW + 2 * DH,
           QW + 2 * KVW + CONVW + 2 * DH + DNW, QW + 2 * KVW + CONVW + 2 * DH + DNW + D, IN_W)
IN_WIDTHS = tuple(b - a for a, b in zip(IN_CUTS[:-1], IN_CUTS[1:]))
IN_SHARD = IN_W // N_DEV


def _inproj_fwd(x, mod, g1, w_blk):
    B, S, _ = x.shape
    tm = _tile(S)

    def body(x_ref, mod_ref, g_ref, w_ref, h_ref, *o_refs):
        h = _rms_mod(x_ref[...], g_ref[...], mod_ref[1:2, :], mod_ref[0:1, :]).astype(BF16)
        h_ref[...] = h
        full = jnp.concatenate([jnp.dot(h, w_ref[j], preferred_element_type=F32) for j in range(N_DEV)], axis=1)
        for o_ref, lo, hi in zip(o_refs, IN_CUTS[:-1], IN_CUTS[1:]):
            o_ref[...] = full[:, lo:hi]

    return pl.pallas_call(
        body, name="inproj_fwd", grid=(B, S // tm),
        out_shape=[jax.ShapeDtypeStruct((B, S, D), BF16)] + [jax.ShapeDtypeStruct((B, S, w), F32) for w in IN_WIDTHS],
        in_specs=[_rows(tm, D), _perb(6, D), _full((1, D)), _resident(w_blk.shape)],
        out_specs=[_rows(tm, D)] + [_rows(tm, w) for w in IN_WIDTHS],
        compiler_params=_cparams(("parallel", "arbitrary")),
    )(x, mod, g1, w_blk)


def _inproj_bwd(x, mod, g1, dx1, dps, w_blk):
    B, S, _ = x.shape
    tm = _tile(S)
    n = len(dps)

    def body(x_ref, mod_ref, g_ref, dx1_ref, *refs):
        dp_refs, w_ref = refs[:n], refs[n]
        dblk_ref, gx_ref, dg_ref, dsc_ref, dsh_ref = refs[n + 1:]
        b, i = pl.program_id(0), pl.program_id(1)
        full = jnp.concatenate([r[...].astype(F32) for r in dp_refs], axis=1)
        dh = None
        for j in range(N_DEV):
            blk = full[:, IN_SHARD * j:IN_SHARD * (j + 1)].astype(BF16)
            dblk_ref[j] = blk
            t = _dot_nt(blk, w_ref[j])
            dh = t if dh is None else dh + t
        _, vjp = jax.vjp(_rms_mod, x_ref[...], g_ref[...], mod_ref[1:2, :], mod_ref[0:1, :])
        dx, dg, dsc, dsh = vjp(dh)
        gx_ref[...] = dx1_ref[...] + dx

        @pl.when((b == 0) & (i == 0))
        def _():
            dg_ref[...] = jnp.zeros_like(dg_ref)

        @pl.when(i == 0)
        def _():
            dsc_ref[...] = jnp.zeros_like(dsc_ref)
            dsh_ref[...] = jnp.zeros_like(dsh_ref)

        dg_ref[...] += dg
        dsc_ref[...] += dsc
        dsh_ref[...] += dsh

    return pl.pallas_call(
        body, name="inproj_bwd", grid=(B, S // tm),
        out_shape=[jax.ShapeDtypeStruct((B, N_DEV, S, IN_SHARD), BF16), jax.ShapeDtypeStruct((B, S, D), F32),
                   jax.ShapeDtypeStruct((1, D), F32), jax.ShapeDtypeStruct((B, 1, D), F32),
                   jax.ShapeDtypeStruct((B, 1, D), F32)],
        in_specs=[_rows(tm, D), _perb(6, D), _full((1, D)), _rows(tm, D)]
                 + [_rows(tm, w) for w in IN_WIDTHS] + [_resident(w_blk.shape)],
        out_specs=[pl.BlockSpec((None, N_DEV, tm, IN_SHARD), lambda b, i: (b, 0, i, 0)), _rows(tm, D),
                   _full((1, D)), _perb(1, D), _perb(1, D)],
        compiler_params=_cparams(("arbitrary", "arbitrary")),
    )(x, mod, g1, dx1, *dps, w_blk)


def _wgrad(a, b, name):
    B, na, S, K = a.shape
    nb, N = b.shape[1], b.shape[3]
    G = max(na, nb)
    tm = min(512, S)
    nt = S // tm
    last = B * nt - 1

    def body(a_ref, b_ref, o_ref, acc):
        t = pl.program_id(1)

        @pl.when(t == 0)
        def _():
            acc[...] = jnp.zeros_like(acc)

        acc[...] += lax.dot_general(a_ref[...], b_ref[...], (((0,), (0,)), ((), ())), preferred_element_type=F32)

        @pl.when(t == last)
        def _():
            o_ref[...] = acc[...].astype(BF16)

    return pl.pallas_call(
        body, name=name, grid=(G, B * nt),
        out_shape=jax.ShapeDtypeStruct((G, K, N), BF16),
        in_specs=[pl.BlockSpec((None, None, tm, K), lambda g, t: (t // nt, g if na > 1 else 0, t % nt, 0)),
                  pl.BlockSpec((None, None, tm, N), lambda g, t: (t // nt, g if nb > 1 else 0, t % nt, 0))],
        out_specs=pl.BlockSpec((None, K, N), lambda g, t: (g, 0, 0)),
        scratch_shapes=[pltpu.VMEM((K, N), F32)],
        compiler_params=_cparams(("parallel", "arbitrary")),
    )(a, b)


LANES = 128


def _attn_consts():
    inv_freq = THETA ** (-jnp.arange(0, ROT, 2, dtype=F32) / ROT)
    head = jnp.concatenate([inv_freq, inv_freq, jnp.zeros((HD - ROT,), F32)])
    invf = jnp.tile(head, LANES // HD)[None, :]
    mean_of = lambda w: jnp.asarray(np.kron(np.eye(w // HD), np.full((HD, HD), 1.0 / HD)), BF16)
    return invf, mean_of(QW), mean_of(KVW)


def _rope_tables(pos, invf):
    B, S, _ = pos.shape
    tr = min(1024, S)

    def body(p_ref, f_ref, c_ref, s_ref):
        ang = p_ref[...].astype(F32) * f_ref[...]
        c_ref[...] = jnp.cos(ang)
        s_ref[...] = jnp.sin(ang)

    sd = jax.ShapeDtypeStruct((B, S, LANES), F32)
    return pl.pallas_call(
        body, name="rope_tables", grid=(B, S // tr), out_shape=[sd, sd],
        in_specs=[_rows(tr, 1), _full((1, LANES))], out_specs=[_rows(tr, LANES), _rows(tr, LANES)],
        compiler_params=_cparams(("parallel", "parallel")),
    )(pos, invf)


def _rope_expand(cos, sin, reps):
    lane = lax.broadcasted_iota(jnp.int32, cos.shape, 1) % HD
    sa = jnp.where((lane >= ROT // 2) & (lane < ROT), sin, 0.0)
    sb = jnp.where(lane < ROT // 2, -sin, 0.0)
    rep = lambda t: jnp.concatenate([t] * reps, axis=1) if reps > 1 else t
    return rep(cos), rep(sa), rep(sb)


@jax.custom_vjp
def _rope(t, cos, sa, sb):
    w = t.shape[1]
    return t * cos + pltpu.roll(t, ROT // 2, 1) * sa + pltpu.roll(t, w - ROT // 2, 1) * sb


def _rope_fwd(t, cos, sa, sb):
    return _rope(t, cos, sa, sb), (cos, sa, sb)


def _rope_bwd(res, d):
    cos, sa, sb = res
    w = d.shape[1]
    dt = d * cos + pltpu.roll(d * sa, w - ROT // 2, 1) + pltpu.roll(d * sb, ROT // 2, 1)
    return dt, jnp.zeros_like(cos), jnp.zeros_like(sa), jnp.zeros_like(sb)


_rope.defvjp(_rope_fwd, _rope_bwd)


def _head_norm(t, g, mean_of):
    hi, lo = _split(t * t)
    ms = jnp.dot(hi, mean_of, preferred_element_type=F32) + jnp.dot(lo, mean_of, preferred_element_type=F32)
    return t * lax.rsqrt(ms + EPS) * g


def _attn_block(q, kvp, kvc, qg, kg, sinks, tq, tk, mq, mk, valid):
    qn = _rope(_head_norm(q, jnp.concatenate([qg] * HQ, axis=1), mq), *tq)
    kv = jnp.concatenate([kvp, kvc], axis=0)
    kn = _rope(_head_norm(kv[:, 0:KVW], jnp.concatenate([kg] * HKV, axis=1), mk), *tk)
    q4 = jnp.stack([jnp.concatenate([qn[:, HD * (GRP * j + i):HD * (GRP * j + i + 1)] for i in range(GRP)], axis=0)
                    for j in range(HKV)])
    k2 = jnp.stack([kn[:, HD * j:HD * (j + 1)] for j in range(HKV)])
    v2 = jnp.stack([kv[:, KVW + HD * j:KVW + HD * (j + 1)] for j in range(HKV)])
    rowblk = lax.broadcasted_iota(jnp.int32, (GRP * BLK, 1), 0) // BLK
    sink = jnp.stack([sum(jnp.where(rowblk == i, sinks[:, GRP * j + i:GRP * j + i + 1], 0.0) for i in range(GRP))
                      for j in range(HKV)])
    s = _bmm(q4, k2, _BMM_NT) * (HD ** -0.5)
    s = jnp.where(valid[None], s, -1e30)
    m = jnp.maximum(jnp.max(s, axis=-1, keepdims=True), sink)
    p = jnp.exp(s - m)
    probs = p / (jnp.sum(p, axis=-1, keepdims=True) + jnp.exp(sink - m))
    o4 = _bmm(probs, v2)
    return jnp.concatenate([o4[j, BLK * i:BLK * (i + 1), :] for j in range(HKV) for i in range(GRP)], axis=1)


def _attn_tables(cp_ref, cc_ref, sp_ref, sc_ref, n):
    tq = _rope_expand(cc_ref[...], sc_ref[...], QW // LANES)
    tk = _rope_expand(jnp.concatenate([cp_ref[...], cc_ref[...]], axis=0),
                      jnp.concatenate([sp_ref[...], sc_ref[...]], axis=0), KVW // LANES)
    qi = lax.broadcasted_iota(jnp.int32, (GRP * BLK, 2 * BLK), 0) % BLK + BLK
    kj = lax.broadcasted_iota(jnp.int32, (GRP * BLK, 2 * BLK), 1)
    dist = qi - kj
    valid = (dist >= 0) & (dist < BLK) & ((kj >= BLK) | (n > 0))
    return tq, tk, valid


def _attn_fwd(aq, akv, cos, sin, qg, kg, sinks, mq, mk):
    B, S, _ = aq.shape
    nb = S // BLK

    def body(q_ref, kvp_ref, kvc_ref, cp_ref, cc_ref, sp_ref, sc_ref, qg_ref, kg_ref, sk_ref, mq_ref, mk_ref, o_ref):
        tq, tk, valid = _attn_tables(cp_ref, cc_ref, sp_ref, sc_ref, pl.program_id(1))
        o_ref[...] = _attn_block(q_ref[...], kvp_ref[...], kvc_ref[...], qg_ref[...], kg_ref[...], sk_ref[...],
                                 tq, tk, mq_ref[...], mk_ref[...], valid)

    prev = lambda b, n: (b, jnp.maximum(n - 1, 0), 0)
    cur = lambda b, n: (b, n, 0)
    return pl.pallas_call(
        body, name="attn_fwd", grid=(B, nb),
        out_shape=jax.ShapeDtypeStruct((B, S, QW), F32),
        in_specs=[pl.BlockSpec((None, BLK, QW), cur), pl.BlockSpec((None, BLK, 2 * KVW), prev),
                  pl.BlockSpec((None, BLK, 2 * KVW), cur), pl.BlockSpec((None, BLK, LANES), prev),
                  pl.BlockSpec((None, BLK, LANES), cur), pl.BlockSpec((None, BLK, LANES), prev),
                  pl.BlockSpec((None, BLK, LANES), cur), _full((1, HD)), _full((1, HD)), _full((1, HQ)),
                  _full((QW, QW)), _full((KVW, KVW))],
        out_specs=pl.BlockSpec((None, BLK, QW), cur),
        compiler_params=_cparams(("parallel", "arbitrary")),
    )(aq, akv, akv, cos, cos, sin, sin, qg, kg, sinks, mq, mk)


def _attn_bwd(aq, akv, cos, sin, qg, kg, sinks, mq, mk, do):
    B, S, _ = aq.shape
    nb = S // BLK

    def body(q_ref, kvp_ref, kvc_ref, cp_ref, cc_ref, sp_ref, sc_ref, qg_ref, kg_ref, sk_ref, mq_ref, mk_ref, do_ref,
             dq_ref, dkv_ref, dqg_ref, dkg_ref, dsk_ref, carry):
        b, i = pl.program_id(0), pl.program_id(1)
        tq, tk, valid = _attn_tables(cp_ref, cc_ref, sp_ref, sc_ref, nb - 1 - i)
        fn = functools.partial(_attn_block, tq=tq, tk=tk, mq=mq_ref[...], mk=mk_ref[...], valid=valid)
        _, vjp = jax.vjp(fn, q_ref[...], kvp_ref[...], kvc_ref[...], qg_ref[...], kg_ref[...], sk_ref[...])
        dq, dkvp, dkvc, dqg, dkg, dsk = vjp(do_ref[...])

        @pl.when(i == 0)
        def _():
            carry[...] = jnp.zeros_like(carry)

        @pl.when((b == 0) & (i == 0))
        def _():
            dqg_ref[...] = jnp.zeros_like(dqg_ref)
            dkg_ref[...] = jnp.zeros_like(dkg_ref)
            dsk_ref[...] = jnp.zeros_like(dsk_ref)

        dq_ref[...] = dq.astype(BF16)
        dkv_ref[...] = (dkvc + carry[...]).astype(BF16)
        carry[...] = dkvp
        dqg_ref[...] += dqg
        dkg_ref[...] += dkg
        dsk_ref[...] += dsk

    prev = lambda b, i: (b, jnp.maximum(nb - 2 - i, 0), 0)
    cur = lambda b, i: (b, nb - 1 - i, 0)
    return pl.pallas_call(
        body, name="attn_bwd", grid=(B, nb),
        out_shape=[jax.ShapeDtypeStruct((B, S, QW), BF16), jax.ShapeDtypeStruct((B, S, 2 * KVW), BF16),
                   jax.ShapeDtypeStruct((1, HD), F32), jax.ShapeDtypeStruct((1, HD), F32),
                   jax.ShapeDtypeStruct((1, HQ), F32)],
        in_specs=[pl.BlockSpec((None, BLK, QW), cur), pl.BlockSpec((None, BLK, 2 * KVW), prev),
                  pl.BlockSpec((None, BLK, 2 * KVW), cur), pl.BlockSpec((None, BLK, LANES), prev),
                  pl.BlockSpec((None, BLK, LANES), cur), pl.BlockSpec((None, BLK, LANES), prev),
                  pl.BlockSpec((None, BLK, LANES), cur), _full((1, HD)), _full((1, HD)), _full((1, HQ)),
                  _full((QW, QW)), _full((KVW, KVW)), pl.BlockSpec((None, BLK, QW), cur)],
        out_specs=[pl.BlockSpec((None, BLK, QW), cur), pl.BlockSpec((None, BLK, 2 * KVW), cur),
                   _full((1, HD)), _full((1, HD)), _full((1, HQ))],
        scratch_shapes=[pltpu.VMEM((BLK, 2 * KVW), F32)],
        compiler_params=_cparams(("arbitrary", "arbitrary")),
    )(aq, akv, akv, cos, cos, sin, sin, qg, kg, sinks, mq, mk, do)


def _conv_taps(xe, w, rows):
    y = None
    for j in range(CONV):
        sh = pltpu.roll(xe, CONV - 1 - j, 0)[8:8 + rows, :] if j < CONV - 1 else xe[8:8 + rows, :]
        y = sh * w[j:j + 1, :] if y is None else y + sh * w[j:j + 1, :]
    return y


def _conv_fwd(xin, w):
    B, S, C = xin.shape
    tc = min(512, S)
    r8 = tc // 8

    def body(xp_ref, x_ref, w_ref, o_ref):
        i = pl.program_id(1)
        xp = jnp.where(i > 0, xp_ref[...], 0.0)
        xe = jnp.concatenate([xp, x_ref[...]], axis=0)
        o_ref[...] = _silu(_conv_taps(xe, w_ref[...], tc))

    return pl.pallas_call(
        body, name="conv_fwd", grid=(B, S // tc),
        out_shape=jax.ShapeDtypeStruct((B, S, C), F32),
        in_specs=[pl.BlockSpec((None, 8, C), lambda b, i: (b, jnp.maximum(i * r8 - 1, 0), 0)),
                  _rows(tc, C), _full((CONV, C))],
        out_specs=_rows(tc, C),
        compiler_params=_cparams(("parallel", "arbitrary")),
    )(xin, xin, w)


def _conv_bwd(xin, w, dy):
    B, S, C = xin.shape
    tc = min(512, S)
    r8 = tc // 8
    nt = S // tc

    def body(xp_ref, x_ref, xn_ref, dy_ref, dyn_ref, w_ref, dx_ref, dw_ref):
        b, i = pl.program_id(0), pl.program_id(1)
        w = w_ref[...]
        xp = jnp.where(i > 0, xp_ref[...], 0.0)
        xe = jnp.concatenate([xp, x_ref[...], xn_ref[...]], axis=0)
        pre = _conv_taps(xe, w, tc + 8)
        sg = _sigmoid(pre)
        dyn = jnp.where(i < nt - 1, dyn_ref[...], 0.0)
        dpre = jnp.concatenate([dy_ref[...], dyn], axis=0) * (sg * (1.0 + pre * (1.0 - sg)))
        dx = dpre[0:tc, :] * w[CONV - 1:CONV, :]
        for j in range(CONV - 1):
            dx = dx + pltpu.roll(dpre, tc + 8 - (CONV - 1 - j), 0)[0:tc, :] * w[j:j + 1, :]
        dx_ref[...] = dx.astype(BF16)
        dcur = dpre[0:tc, :]
        xe0 = xe[0:8 + tc, :]
        lane_row = lax.broadcasted_iota(jnp.int32, (CONV, C), 0)
        dw = jnp.zeros((CONV, C), F32)
        for j in range(CONV):
            sh = pltpu.roll(xe0, CONV - 1 - j, 0)[8:8 + tc, :] if j < CONV - 1 else xe0[8:8 + tc, :]
            dw = dw + jnp.where(lane_row == j, jnp.sum(sh * dcur, axis=0, keepdims=True), 0.0)

        @pl.when((b == 0) & (i == 0))
        def _():
            dw_ref[...] = jnp.zeros_like(dw_ref)

        dw_ref[...] += dw

    return pl.pallas_call(
        body, name="conv_bwd", grid=(B, nt),
        out_shape=[jax.ShapeDtypeStruct((B, S, C), BF16), jax.ShapeDtypeStruct((CONV, C), F32)],
        in_specs=[pl.BlockSpec((None, 8, C), lambda b, i: (b, jnp.maximum(i * r8 - 1, 0), 0)),
                  _rows(tc, C),
                  pl.BlockSpec((None, 8, C), lambda b, i: (b, jnp.minimum((i + 1) * r8, S // 8 - 1), 0)),
                  _rows(tc, C),
                  pl.BlockSpec((None, 8, C), lambda b, i: (b, jnp.minimum((i + 1) * r8, S // 8 - 1), 0)),
                  _full((CONV, C))],
        out_specs=[_rows(tc, C), _full((CONV, C))],
        compiler_params=_cparams(("arbitrary", "arbitrary")),
    )(xin, xin, xin, dy, dy, w)


def _softplus(x):
    return jnp.maximum(x, 0.0) + jnp.log1p(jnp.exp(-jnp.abs(x)))


_BMM = (((2,), (1,)), ((0,), (0,)))
_BMM_NT = (((2,), (2,)), ((0,), (0,)))
_BMM_TN = (((1,), (1,)), ((0,), (0,)))


def _bmm(a, b, dims=_BMM):
    return lax.dot_general(a.astype(BF16), b.astype(BF16), dims, preferred_element_type=F32)


def _split(a):
    hi = a.astype(BF16)
    return hi, (a - hi.astype(F32)).astype(BF16)


def _bmm3(a, b, dims=_BMM):
    ah, al = _split(a)
    bh, bl = _split(b)
    d = lambda p, q: lax.dot_general(p, q, dims, preferred_element_type=F32)
    return d(ah, bh) + (d(ah, bl) + d(al, bh))


def _tri_inverse(L):
    eye = (lax.broadcasted_iota(jnp.int32, (CH, CH), 0) == lax.broadcasted_iota(jnp.int32, (CH, CH), 1)).astype(F32)
    T = eye - L
    P = L
    n = 2
    while n < CH:
        P = _bmm3(P, P)
        T = T + _bmm3(T, P)
        n *= 2
    return T


@jax.custom_vjp
def _tri_inverse_known(L, T):
    return T


def _tri_inverse_known_fwd(L, T):
    return T, T


def _tri_inverse_known_bwd(T, dT):
    return -_bmm3(T, _bmm3(dT, T, _BMM_NT), _BMM_TN), jnp.zeros_like(T)


_tri_inverse_known.defvjp(_tri_inverse_known_fwd, _tri_inverse_known_bwd)


def _cumsum_rows(g):
    n = g.shape[0]
    ii = lax.broadcasted_iota(jnp.int32, (n, CH, CH), 1)
    jj = lax.broadcasted_iota(jnp.int32, (n, CH, CH), 2)
    tri = (ii >= jj).astype(BF16)
    g0 = g.astype(BF16)
    r1 = g - g0.astype(F32)
    g1 = r1.astype(BF16)
    g2 = (r1 - g1.astype(F32)).astype(BF16)
    d = lambda q: lax.dot_general(tri, q, _BMM, preferred_element_type=F32)
    return d(g0) + (d(g1) + d(g2))


def _dn_prep(t_known, qr, kr, v, a_raw, b_raw, a_log, dt_b):
    n = qr.shape[0]
    ii = lax.broadcasted_iota(jnp.int32, (n, CH, CH), 1)
    jj = lax.broadcasted_iota(jnp.int32, (n, CH, CH), 2)
    incl, strict = ii >= jj, ii > jj
    q = qr * lax.rsqrt(jnp.sum(qr * qr, axis=-1, keepdims=True) + EPS) * (DK ** -0.5)
    k = kr * lax.rsqrt(jnp.sum(kr * kr, axis=-1, keepdims=True) + EPS)
    beta = _sigmoid(b_raw)
    g = -jnp.exp(a_log) * _softplus(a_raw + dt_b)
    gcb = _cumsum_rows(jnp.broadcast_to(g, (n, CH, DK)))
    gc = gcb[:, :, 0:1]
    gc_row = jnp.swapaxes(gcb, 1, 2)[:, 0:1, 0:CH]
    decay = jnp.where(incl, jnp.exp(jnp.where(incl, gc - gc_row, 0.0)), 0.0)
    kb = k * beta
    L = jnp.where(strict, _bmm(kb, k, _BMM_NT) * decay, 0.0)
    T = _tri_inverse(L) if t_known is None else _tri_inverse_known(L, t_known)
    eg = jnp.exp(gc)
    u = _bmm(T, v * beta)
    w = _bmm(T, kb * eg)
    a_in = _bmm(q, k, _BMM_NT) * decay
    g_last = gc[:, CH - 1:CH, :]
    return u, w, q * eg, k * jnp.exp(g_last - gc), a_in, jnp.exp(g_last), T


def _dn_step(S0, u, w, qd, kd, a_in, cd):
    r = _bmm(jnp.concatenate([w, qd], axis=1), S0)
    v_new = u - r[:, 0:CH, :]
    o = r[:, CH:2 * CH, :] + _bmm(a_in, v_new)
    S1 = S0 * cd + _bmm(kd, v_new, _BMM_TN)
    return o, S1


def _dn_stack(cq, ba, al, dt, G):
    cols = [[] for _ in range(7)]
    for c in range(G):
        rows = slice(CH * c, CH * (c + 1))
        for h in range(DH):
            parts = (cq[rows, DK * h:DK * (h + 1)], cq[rows, DNW + DK * h:DNW + DK * (h + 1)],
                     cq[rows, 2 * DNW + DK * h:2 * DNW + DK * (h + 1)], ba[rows, DH + h:DH + h + 1],
                     ba[rows, h:h + 1], al[:, h:h + 1], dt[:, h:h + 1])
            for col, p in zip(cols, parts):
                col.append(p)
    return tuple(jnp.stack(col) for col in cols)


def _dn_group(S, want):
    g = want
    while (S // CH) % g:
        g //= 2
    return g


def _dn_prep_fwd(cq, ba, a_log, dt_b):
    B, S, _ = cq.shape
    nc = S // CH
    G = _dn_group(S, 4)

    def body(cq_ref, ba_ref, al_ref, dt_ref, u_ref, w_ref, qd_ref, kd_ref, a_ref, t_ref, cd_ref):
        ops = _dn_stack(cq_ref[...], ba_ref[...], al_ref[...], dt_ref[...], G)
        u, w, qd, kd, a_in, cd, T = _dn_prep(None, *ops)
        lane4 = lax.broadcasted_iota(jnp.int32, (1, DH), 1)
        for c in range(G):
            rows = slice(CH * c, CH * (c + 1))
            cdrow = jnp.zeros((1, DH), F32)
            for h in range(DH):
                n = DH * c + h
                lanes = slice(DK * h, DK * (h + 1))
                u_ref[rows, lanes] = u[n]
                w_ref[rows, lanes] = w[n]
                qd_ref[rows, lanes] = qd[n]
                kd_ref[rows, lanes] = kd[n]
                a_ref[rows, CH * h:CH * (h + 1)] = a_in[n]
                t_ref[rows, CH * h:CH * (h + 1)] = T[n]
                cdrow = cdrow + jnp.where(lane4 == h, cd[n], 0.0)
            cd_ref[c] = cdrow

    wide = jax.ShapeDtypeStruct((B, S, DNW), F32)
    sq = jax.ShapeDtypeStruct((B, S, DH * CH), F32)
    return pl.pallas_call(
        body, name="dn_prep_fwd", grid=(B, nc // G),
        out_shape=[wide, wide, wide, wide, sq, sq, jax.ShapeDtypeStruct((B, nc, 1, DH), F32)],
        in_specs=[_rows(G * CH, CONVW), _rows(G * CH, 2 * DH), _full((1, DH)), _full((1, DH))],
        out_specs=[_rows(G * CH, DNW)] * 4 + [_rows(G * CH, DH * CH)] * 2
                  + [pl.BlockSpec((None, G, 1, DH), lambda b, i: (b, i, 0, 0))],
        compiler_params=_cparams(("parallel", "parallel")),
    )(cq, ba, a_log, dt_b)


def _dn_seq_specs(B, nc, rev):
    at = (lambda i: nc - 1 - i) if rev else (lambda i: i)
    wide = pl.BlockSpec((B, CH, DNW), lambda i: (0, at(i), 0))
    a_spec = pl.BlockSpec((B, CH, DH * CH), lambda i: (0, at(i), 0))
    cd_spec = pl.BlockSpec((B, None, 1, DH), lambda i: (0, at(i), 0, 0))
    st_spec = pl.BlockSpec((B, None, DH, DK, DK), lambda i: (0, at(i), 0, 0, 0))
    return wide, a_spec, cd_spec, st_spec


def _dn_step_operands(B, u_ref, w_ref, qd_ref, kd_ref, a_ref, cd_ref):
    pairs = [(b, h) for b in range(B) for h in range(DH)]
    wide = lambda ref: jnp.stack([ref[b, :, DK * h:DK * (h + 1)] for b, h in pairs])
    a_in = jnp.stack([a_ref[b, :, CH * h:CH * (h + 1)] for b, h in pairs])
    cd = jnp.stack([cd_ref[b, :, h:h + 1] for b, h in pairs])
    return wide(u_ref), wide(w_ref), wide(qd_ref), wide(kd_ref), a_in, cd


def _dn_seq_fwd(u, w, qd, kd, a_in, cd):
    B, S, _ = u.shape
    nc = S // CH

    def body(u_ref, w_ref, qd_ref, kd_ref, a_ref, cd_ref, o_ref, st_ref, state):
        @pl.when(pl.program_id(0) == 0)
        def _():
            state[...] = jnp.zeros_like(state)

        S0 = state[...]
        for b in range(B):
            st_ref[b] = S0[DH * b:DH * (b + 1)]
        o, S1 = _dn_step(S0, *_dn_step_operands(B, u_ref, w_ref, qd_ref, kd_ref, a_ref, cd_ref))
        state[...] = S1
        for b in range(B):
            for h in range(DH):
                o_ref[b, :, DK * h:DK * (h + 1)] = o[DH * b + h]

    wide, a_spec, cd_spec, st_spec = _dn_seq_specs(B, nc, False)
    return pl.pallas_call(
        body, name="dn_seq_fwd", grid=(nc,),
        out_shape=[jax.ShapeDtypeStruct((B, S, DNW), F32), jax.ShapeDtypeStruct((B, nc, DH, DK, DK), F32)],
        in_specs=[wide, wide, wide, wide, a_spec, cd_spec],
        out_specs=[wide, st_spec],
        scratch_shapes=[pltpu.VMEM((B * DH, DK, DK), F32)],
        compiler_params=_cparams(("arbitrary",)),
    )(u, w, qd, kd, a_in, cd)


def _dn_seq_bwd(u, w, qd, kd, a_in, cd, states, do):
    B, S, _ = u.shape
    nc = S // CH

    def body(u_ref, w_ref, qd_ref, kd_ref, a_ref, cd_ref, st_ref, do_ref,
             du_ref, dw_ref, dqd_ref, dkd_ref, da_ref, dcd_ref, dstate):
        @pl.when(pl.program_id(0) == 0)
        def _():
            dstate[...] = jnp.zeros_like(dstate)

        lane4 = lax.broadcasted_iota(jnp.int32, (1, DH), 1)
        S0 = jnp.concatenate([st_ref[b] for b in range(B)], axis=0)
        do = jnp.stack([do_ref[b, :, DK * h:DK * (h + 1)] for b in range(B) for h in range(DH)])
        _, vjp = jax.vjp(_dn_step, S0, *_dn_step_operands(B, u_ref, w_ref, qd_ref, kd_ref, a_ref, cd_ref))
        dS, du, dw, dqd, dkd, da, dcd = vjp((do, dstate[...]))
        dstate[...] = dS
        for b in range(B):
            dcdrow = jnp.zeros((1, DH), F32)
            for h in range(DH):
                n = DH * b + h
                lanes = slice(DK * h, DK * (h + 1))
                du_ref[b, :, lanes] = du[n]
                dw_ref[b, :, lanes] = dw[n]
                dqd_ref[b, :, lanes] = dqd[n]
                dkd_ref[b, :, lanes] = dkd[n]
                da_ref[b, :, CH * h:CH * (h + 1)] = da[n]
                dcdrow = dcdrow + jnp.where(lane4 == h, dcd[n], 0.0)
            dcd_ref[b] = dcdrow

    wide, a_spec, cd_spec, st_spec = _dn_seq_specs(B, nc, True)
    sd = jax.ShapeDtypeStruct((B, S, DNW), F32)
    return pl.pallas_call(
        body, name="dn_seq_bwd", grid=(nc,),
        out_shape=[sd, sd, sd, sd, jax.ShapeDtypeStruct((B, S, DH * CH), F32), jax.ShapeDtypeStruct((B, nc, 1, DH), F32)],
        in_specs=[wide, wide, wide, wide, a_spec, cd_spec, st_spec, wide],
        out_specs=[wide, wide, wide, wide, a_spec, cd_spec],
        scratch_shapes=[pltpu.VMEM((B * DH, DK, DK), F32)],
        compiler_params=_cparams(("arbitrary",)),
    )(u, w, qd, kd, a_in, cd, states, do)


def _dn_prep_bwd(cq, ba, a_log, dt_b, t_inv, du, dw, dqd, dkd, da, dcd):
    B, S, _ = cq.shape
    nc = S // CH
    G = _dn_group(S, 4)

    def body(cq_ref, ba_ref, al_ref, dt_ref, t_ref, du_ref, dw_ref, dqd_ref, dkd_ref, da_ref, dcd_ref,
             dcq_ref, dba_ref, dal_ref, ddt_ref):
        @pl.when((pl.program_id(0) == 0) & (pl.program_id(1) == 0))
        def _():
            dal_ref[...] = jnp.zeros_like(dal_ref)
            ddt_ref[...] = jnp.zeros_like(ddt_ref)

        pairs = [(c, h) for c in range(G) for h in range(DH)]
        rows = lambda c: slice(CH * c, CH * (c + 1))
        wide = lambda ref: jnp.stack([ref[rows(c), DK * h:DK * (h + 1)] for c, h in pairs])
        square = lambda ref: jnp.stack([ref[rows(c), CH * h:CH * (h + 1)] for c, h in pairs])
        ops = _dn_stack(cq_ref[...], ba_ref[...], al_ref[...], dt_ref[...], G)
        cots = (wide(du_ref), wide(dw_ref), wide(dqd_ref), wide(dkd_ref), square(da_ref),
                jnp.stack([dcd_ref[c][:, h:h + 1] for c, h in pairs]), jnp.zeros((len(pairs), CH, CH), F32))
        _, vjp = jax.vjp(functools.partial(_dn_prep, square(t_ref)), *ops)
        dq, dk, dv, dar, dbr, dl, dd = vjp(cots)
        lane8 = lax.broadcasted_iota(jnp.int32, (CH, 2 * DH), 1)
        lane4 = lax.broadcasted_iota(jnp.int32, (1, DH), 1)
        dal = jnp.zeros((1, DH), F32)
        ddt = jnp.zeros((1, DH), F32)
        for c in range(G):
            dba = jnp.zeros((CH, 2 * DH), F32)
            for h in range(DH):
                n = DH * c + h
                dcq_ref[rows(c), DK * h:DK * (h + 1)] = dq[n]
                dcq_ref[rows(c), DNW + DK * h:DNW + DK * (h + 1)] = dk[n]
                dcq_ref[rows(c), 2 * DNW + DK * h:2 * DNW + DK * (h + 1)] = dv[n]
                dba = dba + jnp.where(lane8 == h, dbr[n], 0.0) + jnp.where(lane8 == DH + h, dar[n], 0.0)
                dal = dal + jnp.where(lane4 == h, dl[n], 0.0)
                ddt = ddt + jnp.where(lane4 == h, dd[n], 0.0)
            dba_ref[rows(c), :] = dba.astype(BF16)
        dal_ref[...] += dal
        ddt_ref[...] += ddt

    return pl.pallas_call(
        body, name="dn_prep_bwd", grid=(B, nc // G),
        out_shape=[jax.ShapeDtypeStruct((B, S, CONVW), F32), jax.ShapeDtypeStruct((B, S, 2 * DH), BF16),
                   jax.ShapeDtypeStruct((1, DH), F32), jax.ShapeDtypeStruct((1, DH), F32)],
        in_specs=[_rows(G * CH, CONVW), _rows(G * CH, 2 * DH), _full((1, DH)), _full((1, DH)), _rows(G * CH, DH * CH)]
                 + [_rows(G * CH, DNW)] * 4 + [_rows(G * CH, DH * CH),
                                               pl.BlockSpec((None, G, 1, DH), lambda b, i: (b, i, 0, 0))],
        out_specs=[_rows(G * CH, CONVW), _rows(G * CH, 2 * DH), _full((1, DH)), _full((1, DH))],
        compiler_params=_cparams(("arbitrary", "arbitrary")),
    )(cq, ba, a_log, dt_b, t_inv, du, dw, dqd, dkd, da, dcd)


def _gated_norm(o, z, g):
    outs = []
    for h in range(DH):
        t = o[:, DK * h:DK * (h + 1)]
        r = lax.rsqrt(jnp.mean(t * t, axis=-1, keepdims=True) + EPS)
        outs.append(t * r * g * _silu(z[:, DK * h:DK * (h + 1)]))
    return jnp.concatenate(outs, axis=1)


def _mix_fwd(x, o_attn, o_dn, z, ga, gd, mod, dn_g, w_branch, w_out):
    B, S, _ = x.shape
    tm = _tile(S)

    def body(x_ref, oa_ref, od_ref, z_ref, ga_ref, gd_ref, mod_ref, g_ref, wb_ref, wo_ref,
             x1_ref, mix_ref, mg_ref, ob_ref):
        oa = oa_ref[...].astype(BF16)
        od = _gated_norm(od_ref[...], z_ref[...], g_ref[...]).astype(BF16)
        ob_ref[0] = oa
        ob_ref[1] = od
        ya = jnp.dot(oa, wb_ref[0:QW, :], preferred_element_type=F32)
        yd = jnp.dot(od, wb_ref[QW:QW + DNW, :], preferred_element_type=F32)
        merged = (_sigmoid(ga_ref[...]) * ya + _sigmoid(gd_ref[...]) * yd).astype(BF16)
        mg_ref[...] = merged
        mix = jnp.dot(merged, wo_ref[...], preferred_element_type=F32)
        mix_ref[...] = mix
        x1_ref[...] = x_ref[...] + mod_ref[2:3, :] * mix

    return pl.pallas_call(
        body, name="mix_fwd", grid=(B, S // tm),
        out_shape=[jax.ShapeDtypeStruct((B, S, D), F32), jax.ShapeDtypeStruct((B, S, D), F32),
                   jax.ShapeDtypeStruct((B, S, D), BF16), jax.ShapeDtypeStruct((B, 2, S, QW), BF16)],
        in_specs=[_rows(tm, D), _rows(tm, QW), _rows(tm, DNW), _rows(tm, DNW), _rows(tm, D), _rows(tm, D),
                  _perb(6, D), _full((1, DK)), _resident(w_branch.shape), _resident(w_out.shape)],
        out_specs=[_rows(tm, D), _rows(tm, D), _rows(tm, D), _stacked(2, tm, QW)],
        compiler_params=_cparams(("parallel", "arbitrary")),
    )(x, o_attn, o_dn, z, ga, gd, mod, dn_g, w_branch, w_out)


def _mix_bwd(dx1, mix, o_attn, o_dn, z, ga, gd, mod, dn_g, w_branch, w_out):
    B, S, _ = dx1.shape
    tm = _tile(S)

    def body(dx1_ref, mix_ref, oa_ref, od_ref, z_ref, ga_ref, gd_ref, mod_ref, g_ref, wb_ref, wo_ref,
             dmix_ref, dyo_ref, dga_ref, dgd_ref, dz_ref, doa_ref, dod_ref, dgate_ref, dg_ref):
        b, i = pl.program_id(0), pl.program_id(1)
        dx1 = dx1_ref[...]
        dmix = (dx1 * mod_ref[2:3, :]).astype(BF16)
        dmix_ref[...] = dmix
        dgate = jnp.sum(dx1 * mix_ref[...], axis=0, keepdims=True)
        dmerged = _dot_nt(dmix, wo_ref[...])
        odn, gn_vjp = jax.vjp(_gated_norm, od_ref[...], z_ref[...], g_ref[...])
        ya = _dot(oa_ref[...], wb_ref[0:QW, :])
        yd = _dot(odn, wb_ref[QW:QW + DNW, :])
        sa, sd = _sigmoid(ga_ref[...]), _sigmoid(gd_ref[...])
        dya = (dmerged * sa).astype(BF16)
        dyd = (dmerged * sd).astype(BF16)
        dyo_ref[0] = dya
        dyo_ref[1] = dyd
        dga_ref[...] = (dmerged * ya * sa * (1.0 - sa)).astype(BF16)
        dgd_ref[...] = (dmerged * yd * sd * (1.0 - sd)).astype(BF16)
        doa_ref[...] = _dot_nt(dya, wb_ref[0:QW, :])
        dodn = _dot_nt(dyd, wb_ref[QW:QW + DNW, :])
        dod, dz, dg = gn_vjp(dodn)
        dod_ref[...] = dod
        dz_ref[...] = dz.astype(BF16)

        @pl.when(i == 0)
        def _():
            dgate_ref[...] = jnp.zeros_like(dgate_ref)

        @pl.when((b == 0) & (i == 0))
        def _():
            dg_ref[...] = jnp.zeros_like(dg_ref)

        dgate_ref[...] += dgate
        dg_ref[...] += dg

    return pl.pallas_call(
        body, name="mix_bwd", grid=(B, S // tm),
        out_shape=[jax.ShapeDtypeStruct((B, S, D), BF16), jax.ShapeDtypeStruct((B, 2, S, D), BF16),
                   jax.ShapeDtypeStruct((B, S, D), BF16), jax.ShapeDtypeStruct((B, S, D), BF16),
                   jax.ShapeDtypeStruct((B, S, DNW), BF16),
                   jax.ShapeDtypeStruct((B, S, QW), F32), jax.ShapeDtypeStruct((B, S, DNW), F32),
                   jax.ShapeDtypeStruct((B, 1, D), F32), jax.ShapeDtypeStruct((1, DK), F32)],
        in_specs=[_rows(tm, D), _rows(tm, D), _rows(tm, QW), _rows(tm, DNW), _rows(tm, DNW), _rows(tm, D),
                  _rows(tm, D), _perb(6, D), _full((1, DK)), _resident(w_branch.shape), _resident(w_out.shape)],
        out_specs=[_rows(tm, D), _stacked(2, tm, D), _rows(tm, D), _rows(tm, D), _rows(tm, DNW),
                   _rows(tm, QW), _rows(tm, DNW), _perb(1, D), _full((1, DK))],
        compiler_params=_cparams(("arbitrary", "arbitrary")),
    )(dx1, mix, o_attn, o_dn, z, ga, gd, mod, dn_g, w_branch, w_out)


GU_SHARD = 2 * FFN // N_DEV
GU_HALF = N_DEV // 2


def _ffn1_fwd(x1, mod, g2, w_gu):
    B, S, _ = x1.shape
    tm = _tile(S)

    def body(x_ref, mod_ref, g_ref, w_ref, h_ref, gate_ref, up_ref, act_ref):
        h = _rms_mod(x_ref[...], g_ref[...], mod_ref[4:5, :], mod_ref[3:4, :]).astype(BF16)
        h_ref[...] = h
        for j in range(GU_HALF):
            gate = jnp.dot(h, w_ref[j], preferred_element_type=F32)
            up = jnp.dot(h, w_ref[GU_HALF + j], preferred_element_type=F32)
            gate_ref[j] = gate
            up_ref[j] = up
            act_ref[j] = (_silu(gate) * up).astype(BF16)

    blk = lambda dt: jax.ShapeDtypeStruct((B, GU_HALF, S, GU_SHARD), dt)
    return pl.pallas_call(
        body, name="ffn1_fwd", grid=(B, S // tm),
        out_shape=[jax.ShapeDtypeStruct((B, S, D), BF16), blk(F32), blk(F32), blk(BF16)],
        in_specs=[_rows(tm, D), _perb(6, D), _full((1, D)), _resident(w_gu.shape)],
        out_specs=[_rows(tm, D)] + [_stacked(GU_HALF, tm, GU_SHARD)] * 3,
        compiler_params=_cparams(("parallel", "arbitrary")),
    )(x1, mod, g2, w_gu)


def _ffn2_fwd(act, x1, target, mod, w_down):
    B, S, _ = x1.shape
    tm = _tile(S)

    def body(a_ref, x_ref, t_ref, mod_ref, w_ref, dy_ref, loss_ref, dgate_ref):
        b, i = pl.program_id(0), pl.program_id(1)
        y = jnp.dot(a_ref[0], w_ref[0], preferred_element_type=F32)
        for j in range(1, GU_HALF):
            y = y + jnp.dot(a_ref[j], w_ref[j], preferred_element_type=F32)
        err = x_ref[...] + mod_ref[5:6, :] * y - t_ref[...]
        dy = err * (1.0 / D)
        dy_ref[...] = dy

        @pl.when((b == 0) & (i == 0))
        def _():
            loss_ref[...] = jnp.zeros_like(loss_ref)

        @pl.when(i == 0)
        def _():
            dgate_ref[...] = jnp.zeros_like(dgate_ref)

        loss_ref[...] += (0.5 / D) * jnp.sum(err * err)
        dgate_ref[...] += jnp.sum(dy * y, axis=0, keepdims=True)

    return pl.pallas_call(
        body, name="ffn2_fwd", grid=(B, S // tm),
        out_shape=[jax.ShapeDtypeStruct((B, S, D), F32), jax.ShapeDtypeStruct((1, 128), F32),
                   jax.ShapeDtypeStruct((B, 1, D), F32)],
        in_specs=[_stacked(GU_HALF, tm, GU_SHARD), _rows(tm, D), _rows(tm, D), _perb(6, D), _resident(w_down.shape)],
        out_specs=[_rows(tm, D), _full((1, 128)), _perb(1, D)],
        compiler_params=_cparams(("arbitrary", "arbitrary")),
    )(act, x1, target, mod, w_down)


def _ffn2_bwd(dy, gate, up, mod, w_down):
    B, S, _ = dy.shape
    tm = _tile(S)

    def body(dy_ref, gate_ref, up_ref, mod_ref, w_ref, dgu_ref, dyg_ref):
        dyg = (dy_ref[...] * mod_ref[5:6, :]).astype(BF16)
        dyg_ref[...] = dyg
        for j in range(GU_HALF):
            dact = _dot_nt(dyg, w_ref[j])
            gate, up = gate_ref[j], up_ref[j]
            sg = _sigmoid(gate)
            dgu_ref[j] = (dact * up * (sg * (1.0 + gate * (1.0 - sg)))).astype(BF16)
            dgu_ref[GU_HALF + j] = (dact * (gate * sg)).astype(BF16)

    return pl.pallas_call(
        body, name="ffn2_bwd", grid=(B, S // tm),
        out_shape=[jax.ShapeDtypeStruct((B, N_DEV, S, GU_SHARD), BF16), jax.ShapeDtypeStruct((B, S, D), BF16)],
        in_specs=[_rows(tm, D), _stacked(GU_HALF, tm, GU_SHARD), _stacked(GU_HALF, tm, GU_SHARD), _perb(6, D),
                  _resident(w_down.shape)],
        out_specs=[_stacked(N_DEV, tm, GU_SHARD), _rows(tm, D)],
        compiler_params=_cparams(("parallel", "arbitrary")),
    )(dy, gate, up, mod, w_down)


def _ffn1_bwd(dgu, x1, dy, mod, g2, w_gu):
    B, S, _ = x1.shape
    tm = _tile(S)

    def body(dgu_ref, x_ref, dy_ref, mod_ref, g_ref, w_ref, dx1_ref, dg_ref, dsc_ref, dsh_ref):
        b, i = pl.program_id(0), pl.program_id(1)
        dh = _dot_nt(dgu_ref[0], w_ref[0])
        for j in range(1, N_DEV):
            dh = dh + _dot_nt(dgu_ref[j], w_ref[j])
        _, vjp = jax.vjp(_rms_mod, x_ref[...], g_ref[...], mod_ref[4:5, :], mod_ref[3:4, :])
        dx, dg, dsc, dsh = vjp(dh)
        dx1_ref[...] = dy_ref[...] + dx

        @pl.when((b == 0) & (i == 0))
        def _():
            dg_ref[...] = jnp.zeros_like(dg_ref)

        @pl.when(i == 0)
        def _():
            dsc_ref[...] = jnp.zeros_like(dsc_ref)
            dsh_ref[...] = jnp.zeros_like(dsh_ref)

        dg_ref[...] += dg
        dsc_ref[...] += dsc
        dsh_ref[...] += dsh

    return pl.pallas_call(
        body, name="ffn1_bwd", grid=(B, S // tm),
        out_shape=[jax.ShapeDtypeStruct((B, S, D), F32), jax.ShapeDtypeStruct((1, D), F32),
                   jax.ShapeDtypeStruct((B, 1, D), F32), jax.ShapeDtypeStruct((B, 1, D), F32)],
        in_specs=[_stacked(N_DEV, tm, GU_SHARD), _rows(tm, D), _rows(tm, D), _perb(6, D), _full((1, D)),
                  _resident(w_gu.shape)],
        out_specs=[_rows(tm, D), _full((1, D)), _perb(1, D), _perb(1, D)],
        compiler_params=_cparams(("arbitrary", "arbitrary")),
    )(dgu, x1, dy, mod, g2, w_gu)


def _adamw(w, g, m, v, name):
    def body(w_ref, g_ref, m_ref, v_ref, d_ref, nm_ref, nv_ref):
        g = g_ref[...]
        m = B1 * m_ref[...] + (1.0 - B1) * g
        v = B2 * v_ref[...] + (1.0 - B2) * (g * g)
        nm_ref[...] = m
        nv_ref[...] = v
        m_hat = m / (1.0 - B1 ** STEP)
        v_hat = v / (1.0 - B2 ** STEP)
        d_ref[...] = -LR * (m_hat / (jnp.sqrt(v_hat) + AEPS) + WD * w_ref[...])

    sd = jax.ShapeDtypeStruct(w.shape, F32)
    return pl.pallas_call(body, name=name, out_shape=(sd, sd, sd), compiler_params=_cparams())(w, g, m, v)


def kernel(x, c, positions, ada_w, ada_b, norm1_g, w_in, conv_w, q_norm_g, k_norm_g, sinks, a_log, dt_bias, dn_norm_g, w_branch, w_out, norm2_g, w_gate_up, w_down, loss_target, m_ada_w, m_ada_b, m_norm1_g, m_w_in, m_conv_w, m_q_norm_g, m_k_norm_g, m_sinks, m_a_log, m_dt_bias, m_dn_norm_g, m_w_branch, m_w_out, m_norm2_g, m_w_gate_up, m_w_down, v_ada_w, v_ada_b, v_norm1_g, v_w_in, v_conv_w, v_q_norm_g, v_k_norm_g, v_sinks, v_a_log, v_dt_bias, v_dn_norm_g, v_w_branch, v_w_out, v_norm2_g, v_w_gate_up, v_w_down):
    B, S, _ = x.shape
    me = 4 * lax.axis_index("x") + 2 * lax.axis_index("y") + lax.axis_index("c")

    shards = [w[0].astype(BF16) for w in (w_in, w_branch, w_out, w_gate_up, w_down)]
    w_sems, w_srcs, w_lands, w_token = _copies_start(shards, [_place_own(s, me) for s in shards], False, "gather_start")

    c_all = _all_gather_small(c + w_token[0, 0], "gather_c").reshape(N_DEV * B, D)
    ncol = 6 * D // N_DEV
    mod_cols, cond_all = _ada_fwd(c_all, ada_w[0], lax.dynamic_slice(ada_b, (0, me * ncol), (1, ncol)))
    mod_all = _all_gather_small(mod_cols, "gather_mod").transpose(1, 0, 2).reshape(N_DEV * B, 6 * D)
    mod = lax.dynamic_slice(mod_all, (me * B, 0), (B, 6 * D)).reshape(B, 6, D)

    (w_in_b,) = _copies_wait(w_sems[:2], w_srcs[:1], w_lands[:1], mod, "gather_wait_in")
    h1, aq, akv, dnx, ba, z, ga, gd = _inproj_fwd(x, mod, norm1_g, w_in_b)
    invf, mean_q, mean_k = _attn_consts()
    rope_cos, rope_sin = _rope_tables(positions.reshape(B, S, 1), invf)
    o_attn = _attn_fwd(aq, akv, rope_cos, rope_sin, q_norm_g, k_norm_g, sinks, mean_q, mean_k)
    conv2 = conv_w.reshape(CONV, CONVW // N_DEV)
    conv_all = _all_gather_small(conv2, "gather_conv").transpose(1, 0, 2).reshape(CONV, CONVW)
    cq = _conv_fwd(dnx, conv_all)
    dn_u, dn_w, dn_qd, dn_kd, dn_a, dn_t, dn_cd = _dn_prep_fwd(cq, ba, a_log, dt_bias)
    o_dn, states = _dn_seq_fwd(dn_u, dn_w, dn_qd, dn_kd, dn_a, dn_cd)
    w_branch_g, w_out_g, w_gu_b, w_down_g = _copies_wait(w_sems[2:], w_srcs[1:], w_lands[1:], o_dn, "gather_wait_rest")
    w_branch_f = w_branch_g.reshape(D, D)
    w_out_f = w_out_g.reshape(D, D)
    w_down_b = w_down_g.reshape(GU_HALF, GU_SHARD, D)
    x1, mix, merged, ob = _mix_fwd(x, o_attn, o_dn, z, ga, gd, mod, dn_norm_g, w_branch_f, w_out_f)
    h2, gate, up, act = _ffn1_fwd(x1, mod, norm2_g, w_gu_b)
    dy, loss_part, d_gate2 = _ffn2_fwd(act, x1, loss_target, mod, w_down_b)
    loss = lax.psum(loss_part[0, 0], ("x", "y", "c"))

    one = lambda t: t.reshape(B, 1, S, t.shape[-1])
    dgu, dyg = _ffn2_bwd(dy, gate, up, mod, w_down_b)
    g_w_down = _wgrad(act, one(dyg), "wgrad_down")
    dx1, d_n2g, d_scale2, d_shift2 = _ffn1_bwd(dgu, x1, dy, mod, norm2_g, w_gu_b)
    g_w_gu = _wgrad(one(h2), dgu, "wgrad_gate_up")
    ffn = _exchange_start([g_w_gu, g_w_down.reshape(N_DEV, FFN // N_DEV, D)], me, "exchange_ffn_start")
    dmix, dyo, dga, dgd, dz, d_oa, d_od, d_gate1, d_dng = _mix_bwd(
        dx1, mix, o_attn, o_dn, z, ga, gd, mod, dn_norm_g + ffn[3][0, 0], w_branch_f, w_out_f)
    g_w_out = _wgrad(one(merged), one(dmix), "wgrad_out")
    g_w_branch = _wgrad(ob, dyo, "wgrad_branch")
    mixer = _exchange_start([g_w_branch.reshape(N_DEV, D // N_DEV, D), g_w_out.reshape(N_DEV, D // N_DEV, D)], me,
                            "exchange_mix_start")
    d_dn = _dn_seq_bwd(dn_u, dn_w, dn_qd, dn_kd, dn_a, dn_cd, states, d_od)
    dcq, dba, d_alog, d_dtb = _dn_prep_bwd(cq, ba, a_log + mixer[3][0, 0], dt_bias, dn_t, *d_dn)
    ddnx, d_conv = _conv_bwd(dnx, conv_all, dcq)
    daq, dakv, d_qg, d_kg, d_sinks = _attn_bwd(aq, akv, rope_cos, rope_sin, q_norm_g, k_norm_g, sinks, mean_q, mean_k, d_oa)
    dps = [daq, dakv, ddnx, dba, dz, dga, dgd]
    dblk, grad_x, d_n1g, d_scale1, d_shift1 = _inproj_bwd(x, mod, norm1_g, dx1, dps, w_in_b)
    g_w_in = _wgrad(one(h1), dblk, "wgrad_in")
    proj = _exchange_start([g_w_in], me, "exchange_in_start")

    dmod = jnp.concatenate([d_shift1, d_scale1, d_gate1, d_shift2, d_scale2, d_gate2], axis=2).reshape(B, 6 * D)
    small = jnp.concatenate([d_n1g + proj[3][0, 0], d_qg, d_kg, d_sinks, d_alog, d_dtb, d_dng, d_n2g,
                             d_conv.reshape(1, CONV * CONVW)], axis=1)
    nsm = small.shape[1]
    width = -(-max(6 * D, nsm) // 128) * 128
    rows = jnp.concatenate([jnp.pad(dmod, ((0, 0), (0, width - 6 * D))), jnp.pad(small, ((0, 8 - B - 1), (0, width - nsm)))], axis=0)
    rows_all = _all_gather_small(rows, "gather_small")
    dmod_all = rows_all[:, 0:B, 0:6 * D].reshape(N_DEV * B, 6 * D)
    dmod_cols = lax.dynamic_slice(dmod_all, (0, me * ncol), (N_DEV * B, ncol))
    grad_ada_w, grad_ada_b, small_sum = _ada_bwd(cond_all, dmod_all, dmod_cols, rows_all[:, B, :])
    sizes = [D, HD, HD, HQ, DH, DH, DK, D]
    so = np.cumsum([0] + sizes)
    g_n1, g_qg, g_kg, g_sk, g_al, g_dt, g_dn, g_n2 = [small_sum[:, so[i]:so[i + 1]] for i in range(8)]
    g_conv_all = small_sum[:, so[8]:so[8] + CONV * CONVW].reshape(CONV, N_DEV, CONVW // N_DEV)
    grad_conv = lax.dynamic_slice(g_conv_all, (0, me, 0), (CONV, 1, CONVW // N_DEV)).reshape(CONV, CONVW // N_DEV)

    grad_w_gu, grad_w_down = [_sum_blocks(r, "sum_grads_" + nm) for r, nm in zip(
        _copies_wait(*ffn[:3], small_sum, "exchange_ffn_wait"), ["gate_up", "down"])]
    grad_w_branch, grad_w_out = [_sum_blocks(r, "sum_grads_" + nm) for r, nm in zip(
        _copies_wait(*mixer[:3], small_sum, "exchange_mix_wait"), ["branch", "out"])]
    (grad_w_in,) = [_sum_blocks(r, "sum_grads_in") for r in _copies_wait(*proj[:3], small_sum, "exchange_in_wait")]

    big = [(ada_w, grad_ada_w.reshape(ada_w.shape), m_ada_w, v_ada_w), (w_in, grad_w_in, m_w_in, v_w_in),
           (w_branch, grad_w_branch, m_w_branch, v_w_branch), (w_out, grad_w_out, m_w_out, v_w_out),
           (w_gate_up, grad_w_gu, m_w_gate_up, v_w_gate_up), (w_down, grad_w_down, m_w_down, v_w_down)]
    upd = {}
    for nm, (w, g, m, v) in zip(["ada_w", "w_in", "w_branch", "w_out", "w_gate_up", "w_down"], big):
        upd[nm] = _adamw(w, g, m, v, "adamw_" + nm)
    small_names = ["ada_b", "norm1_g", "q_norm_g", "k_norm_g", "sinks", "a_log", "dt_bias", "dn_norm_g", "norm2_g", "conv_w"]
    small_w = [ada_b, norm1_g, q_norm_g, k_norm_g, sinks, a_log, dt_bias, dn_norm_g, norm2_g, conv_w]
    small_g = [grad_ada_b, g_n1, g_qg, g_kg, g_sk, g_al, g_dt, g_dn, g_n2, grad_conv]
    small_m = [m_ada_b, m_norm1_g, m_q_norm_g, m_k_norm_g, m_sinks, m_a_log, m_dt_bias, m_dn_norm_g, m_norm2_g, m_conv_w]
    small_v = [v_ada_b, v_norm1_g, v_q_norm_g, v_k_norm_g, v_sinks, v_a_log, v_dt_bias, v_dn_norm_g, v_norm2_g, v_conv_w]
    cat = lambda arrs: jnp.concatenate([a.reshape(1, -1) for a in arrs], axis=1)
    res = _adamw(cat(small_w), cat(small_g), cat(small_m), cat(small_v), "adamw_small")
    po = np.cumsum([0] + [int(np.prod(w.shape)) for w in small_w])
    grads = {}
    for i, nm in enumerate(small_names):
        upd[nm] = tuple(r[:, po[i]:po[i + 1]].reshape(small_w[i].shape) for r in res)
        grads[nm] = small_g[i].reshape(small_w[i].shape)
    grads.update(ada_w=grad_ada_w.reshape(ada_w.shape), w_in=grad_w_in, w_branch=grad_w_branch, w_out=grad_w_out,
                 w_gate_up=grad_w_gu, w_down=grad_w_down)

    order = ["ada_w", "ada_b", "norm1_g", "w_in", "conv_w", "q_norm_g", "k_norm_g", "sinks", "a_log", "dt_bias",
             "dn_norm_g", "w_branch", "w_out", "norm2_g", "w_gate_up", "w_down"]
    return (loss, grad_x, *[grads[n] for n in order], *[upd[n][0] for n in order],
            *[upd[n][1] for n in order], *[upd[n][2] for n in order])
```

```python
import functools

import numpy as np
import jax
import jax.numpy as jnp
from jax import lax
from jax.experimental import pallas as pl
from jax.experimental.pallas import tpu as pltpu

F32 = jnp.float32
BF16 = jnp.bfloat16
HI = lax.Precision.HIGHEST

N_DEV = 8
D = 1024
HQ, HKV, HD = 8, 2, 64
GRP = HQ // HKV
BLK = 128
ROT = HD // 4
THETA = 500000.0
QW, KVW = HQ * HD, HKV * HD
DH, DK = 4, 128
CH = 64
DNW = DH * DK
CONV = 4
CONVW = 3 * DNW
FFN = 2816
EPS = 1e-6
IN_W = QW + 2 * KVW + CONVW + 2 * DH + DNW + 2 * D

LR, B1, B2, AEPS, WD, STEP = 0.001, 0.9, 0.999, 1e-08, 0.01, 10

VMEM_LIMIT = 56 * 1024 * 1024
MESH = pl.DeviceIdType.MESH


def _cparams(sem=None, vmem=VMEM_LIMIT):
    return pltpu.CompilerParams(dimension_semantics=sem, vmem_limit_bytes=vmem)


def _full(shape):
    n = len(shape)
    return pl.BlockSpec(shape, lambda *_: (0,) * n)


def _resident(shape):
    n = len(shape)
    return pl.BlockSpec(shape, lambda *_: (0,) * n, pipeline_mode=pl.Buffered(1))


def _rows(tm, w):
    return pl.BlockSpec((None, tm, w), lambda b, i: (b, i, 0))


def _stacked(n, tm, w):
    return pl.BlockSpec((None, n, tm, w), lambda b, i: (b, 0, i, 0))


def _perb(r, w):
    return pl.BlockSpec((None, r, w), lambda b, i: (b, 0, 0))


def _dot(a, b):
    return jnp.dot(a.astype(BF16), b.astype(BF16), preferred_element_type=F32)


def _dot_nt(a, b):
    return lax.dot_general(a.astype(BF16), b.astype(BF16), (((1,), (1,)), ((), ())), preferred_element_type=F32)


def _dot_tn(a, b):
    return lax.dot_general(a.astype(BF16), b.astype(BF16), (((0,), (0,)), ((), ())), preferred_element_type=F32)


def _dot_hi(a, b):
    return jnp.dot(a, b, preferred_element_type=F32, precision=HI)


def _sigmoid(x):
    return jax.nn.sigmoid(x)


def _silu(x):
    return x * jax.nn.sigmoid(x)


def _rms_mod(x, g, scale, shift):
    r = lax.rsqrt(jnp.mean(x * x, axis=-1, keepdims=True) + EPS)
    return (x * r * g) * (1.0 + scale) + shift


def _tile(S):
    return min(256, S)


def _peer(x, y, c, k):
    px = 1 - x if (k >> 2) & 1 else x
    py = 1 - y if (k >> 1) & 1 else y
    pc = 1 - c if k & 1 else c
    return px, py, pc


def _all_gather_small(v, name):
    r, n = v.shape

    def body(v_ref, out_ref, send_sems, recv_sems, local_sem):
        x, y, c = lax.axis_index("x"), lax.axis_index("y"), lax.axis_index("c")
        me = 4 * x + 2 * y + c
        mine = pltpu.make_async_copy(v_ref, out_ref.at[me], local_sem)
        mine.start()
        sends = []
        for k in range(1, N_DEV):
            cp = pltpu.make_async_remote_copy(
                src_ref=v_ref, dst_ref=out_ref.at[me], send_sem=send_sems.at[k - 1], recv_sem=recv_sems.at[k - 1],
                device_id=_peer(x, y, c, k), device_id_type=MESH)
            cp.start()
            sends.append(cp)
        for k in range(1, N_DEV):
            px, py, pc = _peer(x, y, c, k)
            pltpu.make_async_remote_copy(
                src_ref=v_ref, dst_ref=out_ref.at[4 * px + 2 * py + pc], send_sem=send_sems.at[k - 1],
                recv_sem=recv_sems.at[k - 1], device_id=(px, py, pc), device_id_type=MESH).wait_recv()
        for cp in sends:
            cp.wait_send()
        mine.wait()

    return pl.pallas_call(
        body, name=name,
        out_shape=jax.ShapeDtypeStruct((N_DEV, r, n), v.dtype),
        in_specs=[pl.BlockSpec(memory_space=pltpu.VMEM)],
        out_specs=pl.BlockSpec(memory_space=pltpu.VMEM),
        scratch_shapes=[pltpu.SemaphoreType.DMA((N_DEV - 1,)), pltpu.SemaphoreType.DMA((N_DEV - 1,)), pltpu.SemaphoreType.DMA],
    )(v)


def _all_gather_big(vs, name):
    na = len(vs)

    def body(*refs):
        v_refs, out_refs = refs[:na], refs[na:2 * na]
        send_sems, recv_sems, local_sems = refs[2 * na:]
        x, y, c = lax.axis_index("x"), lax.axis_index("y"), lax.axis_index("c")
        me, sibling = (x, y, c), (x, y, 1 - c)
        chips = [(1 - x, y), (x, 1 - y), (1 - x, 1 - y)]

        def rows(a, px, py, pc):
            return out_refs[a].at[4 * px + 2 * py + pc]

        def copy(a, k, block, to, src=None):
            return pltpu.make_async_remote_copy(
                src_ref=rows(a, *block) if src is None else src, dst_ref=rows(a, *block),
                send_sem=send_sems.at[7 * a + k], recv_sem=recv_sems.at[7 * a + k], device_id=to, device_id_type=MESH)

        mine = [pltpu.make_async_copy(v_refs[a], rows(a, *me), local_sems.at[a]) for a in range(na)]
        for cp in mine:
            cp.start()
        first = []
        for a in range(na):
            first.append(copy(a, 0, me, sibling, src=v_refs[a]))
            first += [copy(a, 1 + j, me, (*chip, c), src=v_refs[a]) for j, chip in enumerate(chips)]
        for cp in first:
            cp.start()
        passed = []
        for j, chip in enumerate(chips):
            for a in range(na):
                copy(a, 1 + j, (*chip, c), me).wait_recv()
                forward = copy(a, 4 + j, (*chip, c), sibling)
                forward.start()
                passed.append(forward)
        for a in range(na):
            copy(a, 0, sibling, me).wait_recv()
            for j, chip in enumerate(chips):
                copy(a, 4 + j, (*chip, 1 - c), me).wait_recv()
        for cp in first + passed:
            cp.wait_send()
        for cp in mine:
            cp.wait()

    return pl.pallas_call(
        body, name=name,
        out_shape=[jax.ShapeDtypeStruct((N_DEV,) + v.shape, v.dtype) for v in vs],
        in_specs=[pl.BlockSpec(memory_space=pl.ANY)] * na,
        out_specs=[pl.BlockSpec(memory_space=pl.ANY)] * na,
        scratch_shapes=[pltpu.SemaphoreType.DMA((7 * na,)), pltpu.SemaphoreType.DMA((7 * na,)),
                        pltpu.SemaphoreType.DMA((na,))],
    )(*vs)


def _exchange_blocks(gs, name):
    na = len(gs)

    def body(*refs):
        g_refs, out_refs = refs[:na], refs[na:2 * na]
        send_sems, recv_sems, local_sems = refs[2 * na:]
        x, y, c = lax.axis_index("x"), lax.axis_index("y"), lax.axis_index("c")
        me = 4 * x + 2 * y + c
        mine = [pltpu.make_async_copy(g_refs[a].at[me], out_refs[a].at[me], local_sems.at[a]) for a in range(na)]
        for cp in mine:
            cp.start()
        sends = []
        for k in range(1, N_DEV):
            px, py, pc = _peer(x, y, c, k)
            for a in range(na):
                cp = pltpu.make_async_remote_copy(
                    src_ref=g_refs[a].at[4 * px + 2 * py + pc], dst_ref=out_refs[a].at[me],
                    send_sem=send_sems.at[7 * a + k - 1], recv_sem=recv_sems.at[7 * a + k - 1],
                    device_id=(px, py, pc), device_id_type=MESH)
                cp.start()
                sends.append(cp)
        for k in range(1, N_DEV):
            px, py, pc = _peer(x, y, c, k)
            for a in range(na):
                pltpu.make_async_remote_copy(
                    src_ref=g_refs[a].at[me], dst_ref=out_refs[a].at[4 * px + 2 * py + pc],
                    send_sem=send_sems.at[7 * a + k - 1], recv_sem=recv_sems.at[7 * a + k - 1],
                    device_id=(px, py, pc), device_id_type=MESH).wait_recv()
        for cp in sends:
            cp.wait_send()
        for cp in mine:
            cp.wait()

    return pl.pallas_call(
        body, name=name,
        out_shape=[jax.ShapeDtypeStruct(g.shape, g.dtype) for g in gs],
        in_specs=[pl.BlockSpec(memory_space=pl.ANY)] * na,
        out_specs=[pl.BlockSpec(memory_space=pl.ANY)] * na,
        scratch_shapes=[pltpu.SemaphoreType.DMA((7 * na,)), pltpu.SemaphoreType.DMA((7 * na,)),
                        pltpu.SemaphoreType.DMA((na,))],
    )(*gs)


_HBM = pl.BlockSpec(memory_space=pltpu.HBM)
_SEM = pl.BlockSpec(memory_space=pltpu.SEMAPHORE)
_EFFECT = pltpu.SideEffectType.DATAFLOW_SIDE_EFFECTING


def _place_own(block, me):
    land = lax.empty((N_DEV,) + block.shape, block.dtype)
    return lax.dynamic_update_slice(land, block[None], (me,) + (0,) * block.ndim)


def _copies_start(srcs, lands, scatter, after, name):
    na = len(srcs)
    afters = tuple(after) if isinstance(after, (tuple, list)) else (after,)

    def body(*refs):
        src_refs, land_refs = refs[:na], refs[na:2 * na]
        sems = refs[2 * na + len(afters):4 * na + len(afters)]
        token = refs[-1]
        x, y, c = lax.axis_index("x"), lax.axis_index("y"), lax.axis_index("c")
        me = 4 * x + 2 * y + c
        for a in range(na):
            for k in range(1, N_DEV):
                px, py, pc = _peer(x, y, c, k)
                src = src_refs[a].at[4 * px + 2 * py + pc] if scatter else src_refs[a]
                pltpu.make_async_remote_copy(
                    src_ref=src, dst_ref=land_refs[a].at[me], send_sem=sems[2 * a], recv_sem=sems[2 * a + 1],
                    device_id=(px, py, pc), device_id_type=MESH).start()
        token[...] = jnp.zeros_like(token)

    hbm = lambda t: pltpu.HBM(t.shape, t.dtype)
    out = pl.pallas_call(
        body, name=name,
        out_shape=tuple([pltpu.SemaphoreType.DMA(())] * (2 * na) + [hbm(t) for t in srcs] + [hbm(t) for t in lands]
                        + [jax.ShapeDtypeStruct((8, 128), F32)]),
        in_specs=[_HBM] * (2 * na) + [pl.BlockSpec(memory_space=pl.ANY)] * len(afters),
        out_specs=tuple([_SEM] * (2 * na) + [_HBM] * (2 * na) + [pl.BlockSpec(memory_space=pltpu.VMEM)]),
        input_output_aliases={i: 2 * na + i for i in range(2 * na)},
        compiler_params=pltpu.CompilerParams(has_side_effects=_EFFECT),
    )(*[pltpu.with_memory_space_constraint(t, pltpu.HBM) for t in list(srcs) + list(lands)], *afters)
    return out[:2 * na], out[2 * na:3 * na], out[3 * na:4 * na], out[-1]


def _exchange_start(gs, me, after, name):
    own = [lax.dynamic_index_in_dim(g, me, 0, keepdims=False) for g in gs]
    return _copies_start(gs, [_place_own(o, me) for o in own], True, after, name)


def _copies_wait(sems, srcs, lands, after, name):
    na = len(srcs)

    def body(*refs):
        land_refs = refs[na:2 * na]
        sem_refs = refs[2 * na:4 * na]
        x, y, c = lax.axis_index("x"), lax.axis_index("y"), lax.axis_index("c")
        for a in range(na):
            seven = land_refs[a].at[pl.ds(0, N_DEV - 1)]
            copy = pltpu.make_async_remote_copy(
                src_ref=seven, dst_ref=seven, send_sem=sem_refs[2 * a], recv_sem=sem_refs[2 * a + 1],
                device_id=(x, y, c), device_id_type=MESH)
            copy.wait_send()
            copy.wait_recv()

    hbm = lambda t: pltpu.HBM(t.shape, t.dtype)
    out = pl.pallas_call(
        body, name=name,
        out_shape=tuple([hbm(t) for t in srcs] + [hbm(t) for t in lands]),
        in_specs=[_HBM] * (2 * na) + [_SEM] * (2 * na) + [pl.BlockSpec(memory_space=pl.ANY)],
        out_specs=tuple([_HBM] * (2 * na)),
        input_output_aliases={i: i for i in range(2 * na)},
        compiler_params=pltpu.CompilerParams(has_side_effects=_EFFECT),
    )(*srcs, *lands, *sems, after)
    return out[na:]


def _sum_blocks(g, name):
    _, r, n = g.shape
    tr = 256 if r % 256 == 0 else r

    def body(g_ref, o_ref):
        acc = g_ref[0].astype(F32)
        for d in range(1, N_DEV):
            acc = acc + g_ref[d].astype(F32)
        o_ref[...] = acc

    return pl.pallas_call(
        body, name=name, grid=(r // tr,),
        out_shape=jax.ShapeDtypeStruct((1, r, n), F32),
        in_specs=[pl.BlockSpec((N_DEV, tr, n), lambda i: (0, i, 0))],
        out_specs=pl.BlockSpec((None, tr, n), lambda i: (0, i, 0)),
        compiler_params=_cparams(("arbitrary",)),
    )(g)


def _ada_fwd(c_all, ada_w, ada_b_cols):
    nb, ncol = c_all.shape[0], ada_w.shape[1]

    def body(c_ref, w_ref, b_ref, mod_ref, cond_ref):
        cond = _silu(c_ref[...])
        cond_ref[...] = cond
        mod_ref[...] = _dot_hi(cond, w_ref[...]) + b_ref[...]

    return pl.pallas_call(
        body, name="ada_fwd",
        out_shape=(jax.ShapeDtypeStruct((nb, ncol), F32), jax.ShapeDtypeStruct((nb, D), F32)),
        compiler_params=_cparams(),
    )(c_all, ada_w, ada_b_cols)


def _ada_bwd(cond_all, dmod_all, dmod_cols, smalls):
    ncol, nsm = dmod_cols.shape[1], smalls.shape[1]

    def body(cond_ref, dm_ref, dmc_ref, sm_ref, gw_ref, gb_ref, gs_ref):
        gw_ref[...] = lax.dot_general(cond_ref[...], dmc_ref[...], (((0,), (0,)), ((), ())),
                                      preferred_element_type=F32, precision=HI)
        gb_ref[...] = jnp.sum(dm_ref[...], axis=0, keepdims=True)
        gs_ref[...] = jnp.sum(sm_ref[...], axis=0, keepdims=True)

    return pl.pallas_call(
        body, name="ada_bwd",
        out_shape=(jax.ShapeDtypeStruct((D, ncol), F32), jax.ShapeDtypeStruct((1, 6 * D), F32),
                   jax.ShapeDtypeStruct((1, nsm), F32)),
        compiler_params=_cparams(),
    )(cond_all, dmod_all, dmod_cols, smalls)


IN_CUTS = (0, QW, QW + 2 * KVW, QW + 2 * KVW + CONVW, QW + 2 * KVW + CONVW + 2 * DH,
           QW + 2 * KVW + CONVW + 2 * DH + DNW, QW + 2 * KVW + CONVW + 2 * DH + DNW + D, IN_W)
IN_WIDTHS = tuple(b - a for a, b in zip(IN_CUTS[:-1], IN_CUTS[1:]))
IN_SHARD = IN_W // N_DEV


def _inproj_fwd(x, mod, g1, w_blk):
    B, S, _ = x.shape
    tm = _tile(S)

    def body(x_ref, mod_ref, g_ref, w_ref, h_ref, *o_refs):
        h = _rms_mod(x_ref[...], g_ref[...], mod_ref[1:2, :], mod_ref[0:1, :]).astype(BF16)
        h_ref[...] = h
        full = jnp.concatenate([jnp.dot(h, w_ref[j], preferred_element_type=F32) for j in range(N_DEV)], axis=1)
        for o_ref, lo, hi in zip(o_refs, IN_CUTS[:-1], IN_CUTS[1:]):
            o_ref[...] = full[:, lo:hi]

    return pl.pallas_call(
        body, name="inproj_fwd", grid=(B, S // tm),
        out_shape=[jax.ShapeDtypeStruct((B, S, D), BF16)] + [jax.ShapeDtypeStruct((B, S, w), F32) for w in IN_WIDTHS],
        in_specs=[_rows(tm, D), _perb(6, D), _full((1, D)), _resident(w_blk.shape)],
        out_specs=[_rows(tm, D)] + [_rows(tm, w) for w in IN_WIDTHS],
        compiler_params=_cparams(("parallel", "arbitrary")),
    )(x, mod, g1, w_blk)


def _inproj_bwd(x, mod, g1, dx1, dps, w_blk):
    B, S, _ = x.shape
    tm = _tile(S)
    n = len(dps)

    def body(x_ref, mod_ref, g_ref, dx1_ref, *refs):
        dp_refs, w_ref = refs[:n], refs[n]
        dblk_ref, gx_ref, dg_ref, dsc_ref, dsh_ref = refs[n + 1:]
        b, i = pl.program_id(0), pl.program_id(1)
        full = jnp.concatenate([r[...].astype(F32) for r in dp_refs], axis=1)
        dh = None
        for j in range(N_DEV):
            blk = full[:, IN_SHARD * j:IN_SHARD * (j + 1)].astype(BF16)
            dblk_ref[j] = blk
            t = _dot_nt(blk, w_ref[j])
            dh = t if dh is None else dh + t
        _, vjp = jax.vjp(_rms_mod, x_ref[...], g_ref[...], mod_ref[1:2, :], mod_ref[0:1, :])
        dx, dg, dsc, dsh = vjp(dh)
        gx_ref[...] = dx1_ref[...] + dx

        @pl.when((b == 0) & (i == 0))
        def _():
            dg_ref[...] = jnp.zeros_like(dg_ref)

        @pl.when(i == 0)
        def _():
            dsc_ref[...] = jnp.zeros_like(dsc_ref)
            dsh_ref[...] = jnp.zeros_like(dsh_ref)

        dg_ref[...] += dg
        dsc_ref[...] += dsc
        dsh_ref[...] += dsh

    return pl.pallas_call(
        body, name="inproj_bwd", grid=(B, S // tm),
        out_shape=[jax.ShapeDtypeStruct((B, N_DEV, S, IN_SHARD), BF16), jax.ShapeDtypeStruct((B, S, D), F32),
                   jax.ShapeDtypeStruct((1, D), F32), jax.ShapeDtypeStruct((B, 1, D), F32),
                   jax.ShapeDtypeStruct((B, 1, D), F32)],
        in_specs=[_rows(tm, D), _perb(6, D), _full((1, D)), _rows(tm, D)]
                 + [_rows(tm, w) for w in IN_WIDTHS] + [_resident(w_blk.shape)],
        out_specs=[pl.BlockSpec((None, N_DEV, tm, IN_SHARD), lambda b, i: (b, 0, i, 0)), _rows(tm, D),
                   _full((1, D)), _perb(1, D), _perb(1, D)],
        compiler_params=_cparams(("arbitrary", "arbitrary")),
    )(x, mod, g1, dx1, *dps, w_blk)


def _wgrad(a, b, name, after=None):
    B, na, S, K = a.shape
    nb, N = b.shape[1], b.shape[3]
    G = max(na, nb)
    tm = min(512, S)
    nt = S // tm
    last = B * nt - 1

    def body(a_ref, b_ref, *rest):
        o_ref, acc = rest[-2:]
        t = pl.program_id(1)

        @pl.when(t == 0)
        def _():
            acc[...] = jnp.zeros_like(acc)

        acc[...] += lax.dot_general(a_ref[...], b_ref[...], (((0,), (0,)), ((), ())), preferred_element_type=F32)

        @pl.when(t == last)
        def _():
            o_ref[...] = acc[...].astype(BF16)

    return pl.pallas_call(
        body, name=name, grid=(G, B * nt),
        out_shape=jax.ShapeDtypeStruct((G, K, N), BF16),
        in_specs=[pl.BlockSpec((None, None, tm, K), lambda g, t: (t // nt, g if na > 1 else 0, t % nt, 0)),
                  pl.BlockSpec((None, None, tm, N), lambda g, t: (t // nt, g if nb > 1 else 0, t % nt, 0))]
                 + ([] if after is None else [pl.BlockSpec(memory_space=pl.ANY)]),
        out_specs=pl.BlockSpec((None, K, N), lambda g, t: (g, 0, 0)),
        scratch_shapes=[pltpu.VMEM((K, N), F32)],
        compiler_params=_cparams(("parallel", "arbitrary")),
    )(*((a, b) if after is None else (a, b, after)))


LANES = 128


def _attn_consts():
    inv_freq = THETA ** (-jnp.arange(0, ROT, 2, dtype=F32) / ROT)
    head = jnp.concatenate([inv_freq, inv_freq, jnp.zeros((HD - ROT,), F32)])
    invf = jnp.tile(head, LANES // HD)[None, :]
    mean_of = lambda w: jnp.asarray(np.kron(np.eye(w // HD), np.full((HD, HD), 1.0 / HD)), BF16)
    return invf, mean_of(QW), mean_of(KVW)


def _rope_tables(pos, invf):
    B, S, _ = pos.shape
    tr = min(1024, S)

    def body(p_ref, f_ref, c_ref, s_ref):
        ang = p_ref[...].astype(F32) * f_ref[...]
        c_ref[...] = jnp.cos(ang)
        s_ref[...] = jnp.sin(ang)

    sd = jax.ShapeDtypeStruct((B, S, LANES), F32)
    return pl.pallas_call(
        body, name="rope_tables", grid=(B, S // tr), out_shape=[sd, sd],
        in_specs=[_rows(tr, 1), _full((1, LANES))], out_specs=[_rows(tr, LANES), _rows(tr, LANES)],
        compiler_params=_cparams(("parallel", "parallel")),
    )(pos, invf)


def _rope_expand(cos, sin, reps):
    lane = lax.broadcasted_iota(jnp.int32, cos.shape, 1) % HD
    sa = jnp.where((lane >= ROT // 2) & (lane < ROT), sin, 0.0)
    sb = jnp.where(lane < ROT // 2, -sin, 0.0)
    rep = lambda t: jnp.concatenate([t] * reps, axis=1) if reps > 1 else t
    return rep(cos), rep(sa), rep(sb)


@jax.custom_vjp
def _rope(t, cos, sa, sb):
    w = t.shape[1]
    return t * cos + pltpu.roll(t, ROT // 2, 1) * sa + pltpu.roll(t, w - ROT // 2, 1) * sb


def _rope_fwd(t, cos, sa, sb):
    return _rope(t, cos, sa, sb), (cos, sa, sb)


def _rope_bwd(res, d):
    cos, sa, sb = res
    w = d.shape[1]
    dt = d * cos + pltpu.roll(d * sa, w - ROT // 2, 1) + pltpu.roll(d * sb, ROT // 2, 1)
    return dt, jnp.zeros_like(cos), jnp.zeros_like(sa), jnp.zeros_like(sb)


_rope.defvjp(_rope_fwd, _rope_bwd)


def _head_norm(t, g, mean_of):
    hi, lo = _split(t * t)
    ms = jnp.dot(hi, mean_of, preferred_element_type=F32) + jnp.dot(lo, mean_of, preferred_element_type=F32)
    return t * lax.rsqrt(ms + EPS) * g


def _attn_block(q, kvp, kvc, qg, kg, sinks, tq, tk, mq, mk, valid):
    qn = _rope(_head_norm(q, jnp.concatenate([qg] * HQ, axis=1), mq), *tq)
    kv = jnp.concatenate([kvp, kvc], axis=0)
    kn = _rope(_head_norm(kv[:, 0:KVW], jnp.concatenate([kg] * HKV, axis=1), mk), *tk)
    q4 = jnp.stack([jnp.concatenate([qn[:, HD * (GRP * j + i):HD * (GRP * j + i + 1)] for i in range(GRP)], axis=0)
                    for j in range(HKV)])
    k2 = jnp.stack([kn[:, HD * j:HD * (j + 1)] for j in range(HKV)])
    v2 = jnp.stack([kv[:, KVW + HD * j:KVW + HD * (j + 1)] for j in range(HKV)])
    rowblk = lax.broadcasted_iota(jnp.int32, (GRP * BLK, 1), 0) // BLK
    sink = jnp.stack([sum(jnp.where(rowblk == i, sinks[:, GRP * j + i:GRP * j + i + 1], 0.0) for i in range(GRP))
                      for j in range(HKV)])
    s = _bmm(q4, k2, _BMM_NT) * (HD ** -0.5)
    s = jnp.where(valid[None], s, -1e30)
    m = jnp.maximum(jnp.max(s, axis=-1, keepdims=True), sink)
    p = jnp.exp(s - m)
    probs = p / (jnp.sum(p, axis=-1, keepdims=True) + jnp.exp(sink - m))
    o4 = _bmm(probs, v2)
    return jnp.concatenate([o4[j, BLK * i:BLK * (i + 1), :] for j in range(HKV) for i in range(GRP)], axis=1)


def _attn_tables(cp_ref, cc_ref, sp_ref, sc_ref, n):
    tq = _rope_expand(cc_ref[...], sc_ref[...], QW // LANES)
    tk = _rope_expand(jnp.concatenate([cp_ref[...], cc_ref[...]], axis=0),
                      jnp.concatenate([sp_ref[...], sc_ref[...]], axis=0), KVW // LANES)
    qi = lax.broadcasted_iota(jnp.int32, (GRP * BLK, 2 * BLK), 0) % BLK + BLK
    kj = lax.broadcasted_iota(jnp.int32, (GRP * BLK, 2 * BLK), 1)
    dist = qi - kj
    valid = (dist >= 0) & (dist < BLK) & ((kj >= BLK) | (n > 0))
    return tq, tk, valid


def _attn_fwd(aq, akv, cos, sin, qg, kg, sinks, mq, mk):
    B, S, _ = aq.shape
    nb = S // BLK

    def body(q_ref, kvp_ref, kvc_ref, cp_ref, cc_ref, sp_ref, sc_ref, qg_ref, kg_ref, sk_ref, mq_ref, mk_ref, o_ref):
        tq, tk, valid = _attn_tables(cp_ref, cc_ref, sp_ref, sc_ref, pl.program_id(1))
        o_ref[...] = _attn_block(q_ref[...], kvp_ref[...], kvc_ref[...], qg_ref[...], kg_ref[...], sk_ref[...],
                                 tq, tk, mq_ref[...], mk_ref[...], valid)

    prev = lambda b, n: (b, jnp.maximum(n - 1, 0), 0)
    cur = lambda b, n: (b, n, 0)
    return pl.pallas_call(
        body, name="attn_fwd", grid=(B, nb),
        out_shape=jax.ShapeDtypeStruct((B, S, QW), F32),
        in_specs=[pl.BlockSpec((None, BLK, QW), cur), pl.BlockSpec((None, BLK, 2 * KVW), prev),
                  pl.BlockSpec((None, BLK, 2 * KVW), cur), pl.BlockSpec((None, BLK, LANES), prev),
                  pl.BlockSpec((None, BLK, LANES), cur), pl.BlockSpec((None, BLK, LANES), prev),
                  pl.BlockSpec((None, BLK, LANES), cur), _full((1, HD)), _full((1, HD)), _full((1, HQ)),
                  _full((QW, QW)), _full((KVW, KVW))],
        out_specs=pl.BlockSpec((None, BLK, QW), cur),
        compiler_params=_cparams(("parallel", "arbitrary")),
    )(aq, akv, akv, cos, cos, sin, sin, qg, kg, sinks, mq, mk)


def _attn_bwd(aq, akv, cos, sin, qg, kg, sinks, mq, mk, do):
    B, S, _ = aq.shape
    nb = S // BLK

    def body(q_ref, kvp_ref, kvc_ref, cp_ref, cc_ref, sp_ref, sc_ref, qg_ref, kg_ref, sk_ref, mq_ref, mk_ref, do_ref,
             dq_ref, dkv_ref, dqg_ref, dkg_ref, dsk_ref, carry):
        b, i = pl.program_id(0), pl.program_id(1)
        tq, tk, valid = _attn_tables(cp_ref, cc_ref, sp_ref, sc_ref, nb - 1 - i)
        fn = functools.partial(_attn_block, tq=tq, tk=tk, mq=mq_ref[...], mk=mk_ref[...], valid=valid)
        _, vjp = jax.vjp(fn, q_ref[...], kvp_ref[...], kvc_ref[...], qg_ref[...], kg_ref[...], sk_ref[...])
        dq, dkvp, dkvc, dqg, dkg, dsk = vjp(do_ref[...])

        @pl.when(i == 0)
        def _():
            carry[...] = jnp.zeros_like(carry)

        @pl.when((b == 0) & (i == 0))
        def _():
            dqg_ref[...] = jnp.zeros_like(dqg_ref)
            dkg_ref[...] = jnp.zeros_like(dkg_ref)
            dsk_ref[...] = jnp.zeros_like(dsk_ref)

        dq_ref[...] = dq.astype(BF16)
        dkv_ref[...] = (dkvc + carry[...]).astype(BF16)
        carry[...] = dkvp
        dqg_ref[...] += dqg
        dkg_ref[...] += dkg
        dsk_ref[...] += dsk

    prev = lambda b, i: (b, jnp.maximum(nb - 2 - i, 0), 0)
    cur = lambda b, i: (b, nb - 1 - i, 0)
    return pl.pallas_call(
        body, name="attn_bwd", grid=(B, nb),
        out_shape=[jax.ShapeDtypeStruct((B, S, QW), BF16), jax.ShapeDtypeStruct((B, S, 2 * KVW), BF16),
                   jax.ShapeDtypeStruct((1, HD), F32), jax.ShapeDtypeStruct((1, HD), F32),
                   jax.ShapeDtypeStruct((1, HQ), F32)],
        in_specs=[pl.BlockSpec((None, BLK, QW), cur), pl.BlockSpec((None, BLK, 2 * KVW), prev),
                  pl.BlockSpec((None, BLK, 2 * KVW), cur), pl.BlockSpec((None, BLK, LANES), prev),
                  pl.BlockSpec((None, BLK, LANES), cur), pl.BlockSpec((None, BLK, LANES), prev),
                  pl.BlockSpec((None, BLK, LANES), cur), _full((1, HD)), _full((1, HD)), _full((1, HQ)),
                  _full((QW, QW)), _full((KVW, KVW)), pl.BlockSpec((None, BLK, QW), cur)],
        out_specs=[pl.BlockSpec((None, BLK, QW), cur), pl.BlockSpec((None, BLK, 2 * KVW), cur),
                   _full((1, HD)), _full((1, HD)), _full((1, HQ))],
        scratch_shapes=[pltpu.VMEM((BLK, 2 * KVW), F32)],
        compiler_params=_cparams(("arbitrary", "arbitrary")),
    )(aq, akv, akv, cos, cos, sin, sin, qg, kg, sinks, mq, mk, do)


def _conv_taps(xe, w, rows):
    y = None
    for j in range(CONV):
        sh = pltpu.roll(xe, CONV - 1 - j, 0)[8:8 + rows, :] if j < CONV - 1 else xe[8:8 + rows, :]
        y = sh * w[j:j + 1, :] if y is None else y + sh * w[j:j + 1, :]
    return y


def _conv_fwd(xin, w):
    B, S, C = xin.shape
    tc = min(512, S)
    r8 = tc // 8

    def body(xp_ref, x_ref, w_ref, o_ref):
        i = pl.program_id(1)
        xp = jnp.where(i > 0, xp_ref[...], 0.0)
        xe = jnp.concatenate([xp, x_ref[...]], axis=0)
        o_ref[...] = _silu(_conv_taps(xe, w_ref[...], tc))

    return pl.pallas_call(
        body, name="conv_fwd", grid=(B, S // tc),
        out_shape=jax.ShapeDtypeStruct((B, S, C), F32),
        in_specs=[pl.BlockSpec((None, 8, C), lambda b, i: (b, jnp.maximum(i * r8 - 1, 0), 0)),
                  _rows(tc, C), _full((CONV, C))],
        out_specs=_rows(tc, C),
        compiler_params=_cparams(("parallel", "arbitrary")),
    )(xin, xin, w)


def _conv_bwd(xin, w, dy):
    B, S, C = xin.shape
    tc = min(512, S)
    r8 = tc // 8
    nt = S // tc

    def body(xp_ref, x_ref, xn_ref, dy_ref, dyn_ref, w_ref, dx_ref, dw_ref):
        b, i = pl.program_id(0), pl.program_id(1)
        w = w_ref[...]
        xp = jnp.where(i > 0, xp_ref[...], 0.0)
        xe = jnp.concatenate([xp, x_ref[...], xn_ref[...]], axis=0)
        pre = _conv_taps(xe, w, tc + 8)
        sg = _sigmoid(pre)
        dyn = jnp.where(i < nt - 1, dyn_ref[...], 0.0)
        dpre = jnp.concatenate([dy_ref[...], dyn], axis=0) * (sg * (1.0 + pre * (1.0 - sg)))
        dx = dpre[0:tc, :] * w[CONV - 1:CONV, :]
        for j in range(CONV - 1):
            dx = dx + pltpu.roll(dpre, tc + 8 - (CONV - 1 - j), 0)[0:tc, :] * w[j:j + 1, :]
        dx_ref[...] = dx.astype(BF16)
        dcur = dpre[0:tc, :]
        xe0 = xe[0:8 + tc, :]
        lane_row = lax.broadcasted_iota(jnp.int32, (CONV, C), 0)
        dw = jnp.zeros((CONV, C), F32)
        for j in range(CONV):
            sh = pltpu.roll(xe0, CONV - 1 - j, 0)[8:8 + tc, :] if j < CONV - 1 else xe0[8:8 + tc, :]
            dw = dw + jnp.where(lane_row == j, jnp.sum(sh * dcur, axis=0, keepdims=True), 0.0)

        @pl.when((b == 0) & (i == 0))
        def _():
            dw_ref[...] = jnp.zeros_like(dw_ref)

        dw_ref[...] += dw

    return pl.pallas_call(
        body, name="conv_bwd", grid=(B, nt),
        out_shape=[jax.ShapeDtypeStruct((B, S, C), BF16), jax.ShapeDtypeStruct((CONV, C), F32)],
        in_specs=[pl.BlockSpec((None, 8, C), lambda b, i: (b, jnp.maximum(i * r8 - 1, 0), 0)),
                  _rows(tc, C),
                  pl.BlockSpec((None, 8, C), lambda b, i: (b, jnp.minimum((i + 1) * r8, S // 8 - 1), 0)),
                  _rows(tc, C),
                  pl.BlockSpec((None, 8, C), lambda b, i: (b, jnp.minimum((i + 1) * r8, S // 8 - 1), 0)),
                  _full((CONV, C))],
        out_specs=[_rows(tc, C), _full((CONV, C))],
        compiler_params=_cparams(("arbitrary", "arbitrary")),
    )(xin, xin, xin, dy, dy, w)


def _softplus(x):
    return jnp.maximum(x, 0.0) + jnp.log1p(jnp.exp(-jnp.abs(x)))


_BMM = (((2,), (1,)), ((0,), (0,)))
_BMM_NT = (((2,), (2,)), ((0,), (0,)))
_BMM_TN = (((1,), (1,)), ((0,), (0,)))


def _bmm(a, b, dims=_BMM):
    return lax.dot_general(a.astype(BF16), b.astype(BF16), dims, preferred_element_type=F32)


def _split(a):
    hi = a.astype(BF16)
    return hi, (a - hi.astype(F32)).astype(BF16)


def _bmm3(a, b, dims=_BMM):
    ah, al = _split(a)
    bh, bl = _split(b)
    d = lambda p, q: lax.dot_general(p, q, dims, preferred_element_type=F32)
    return d(ah, bh) + (d(ah, bl) + d(al, bh))


def _tri_inverse(L):
    eye = (lax.broadcasted_iota(jnp.int32, (CH, CH), 0) == lax.broadcasted_iota(jnp.int32, (CH, CH), 1)).astype(F32)
    T = eye - L
    P = L
    n = 2
    while n < CH:
        P = _bmm3(P, P)
        T = T + _bmm3(T, P)
        n *= 2
    return T


@jax.custom_vjp
def _tri_inverse_known(L, T):
    return T


def _tri_inverse_known_fwd(L, T):
    return T, T


def _tri_inverse_known_bwd(T, dT):
    return -_bmm3(T, _bmm3(dT, T, _BMM_NT), _BMM_TN), jnp.zeros_like(T)


_tri_inverse_known.defvjp(_tri_inverse_known_fwd, _tri_inverse_known_bwd)


def _cumsum_rows(g):
    n = g.shape[0]
    ii = lax.broadcasted_iota(jnp.int32, (n, CH, CH), 1)
    jj = lax.broadcasted_iota(jnp.int32, (n, CH, CH), 2)
    tri = (ii >= jj).astype(BF16)
    g0 = g.astype(BF16)
    r1 = g - g0.astype(F32)
    g1 = r1.astype(BF16)
    g2 = (r1 - g1.astype(F32)).astype(BF16)
    d = lambda q: lax.dot_general(tri, q, _BMM, preferred_element_type=F32)
    return d(g0) + (d(g1) + d(g2))


def _dn_prep(t_known, qr, kr, v, a_raw, b_raw, a_log, dt_b):
    n = qr.shape[0]
    ii = lax.broadcasted_iota(jnp.int32, (n, CH, CH), 1)
    jj = lax.broadcasted_iota(jnp.int32, (n, CH, CH), 2)
    incl, strict = ii >= jj, ii > jj
    q = qr * lax.rsqrt(jnp.sum(qr * qr, axis=-1, keepdims=True) + EPS) * (DK ** -0.5)
    k = kr * lax.rsqrt(jnp.sum(kr * kr, axis=-1, keepdims=True) + EPS)
    beta = _sigmoid(b_raw)
    g = -jnp.exp(a_log) * _softplus(a_raw + dt_b)
    gcb = _cumsum_rows(jnp.broadcast_to(g, (n, CH, DK)))
    gc = gcb[:, :, 0:1]
    gc_row = jnp.swapaxes(gcb, 1, 2)[:, 0:1, 0:CH]
    decay = jnp.where(incl, jnp.exp(jnp.where(incl, gc - gc_row, 0.0)), 0.0)
    kb = k * beta
    L = jnp.where(strict, _bmm(kb, k, _BMM_NT) * decay, 0.0)
    T = _tri_inverse(L) if t_known is None else _tri_inverse_known(L, t_known)
    eg = jnp.exp(gc)
    u = _bmm(T, v * beta)
    w = _bmm(T, kb * eg)
    a_in = _bmm(q, k, _BMM_NT) * decay
    g_last = gc[:, CH - 1:CH, :]
    return u, w, q * eg, k * jnp.exp(g_last - gc), a_in, jnp.exp(g_last), T


def _dn_step(S0, u, w, qd, kd, a_in, cd):
    r = _bmm(jnp.concatenate([w, qd], axis=1), S0)
    v_new = u - r[:, 0:CH, :]
    o = r[:, CH:2 * CH, :] + _bmm(a_in, v_new)
    S1 = S0 * cd + _bmm(kd, v_new, _BMM_TN)
    return o, S1


def _dn_stack(cq, ba, al, dt, G):
    cols = [[] for _ in range(7)]
    for c in range(G):
        rows = slice(CH * c, CH * (c + 1))
        for h in range(DH):
            parts = (cq[rows, DK * h:DK * (h + 1)], cq[rows, DNW + DK * h:DNW + DK * (h + 1)],
                     cq[rows, 2 * DNW + DK * h:2 * DNW + DK * (h + 1)], ba[rows, DH + h:DH + h + 1],
                     ba[rows, h:h + 1], al[:, h:h + 1], dt[:, h:h + 1])
            for col, p in zip(cols, parts):
                col.append(p)
    return tuple(jnp.stack(col) for col in cols)


def _dn_group(S, want):
    g = want
    while (S // CH) % g:
        g //= 2
    return g


def _dn_prep_fwd(cq, ba, a_log, dt_b):
    B, S, _ = cq.shape
    nc = S // CH
    G = _dn_group(S, 4)

    def body(cq_ref, ba_ref, al_ref, dt_ref, u_ref, w_ref, qd_ref, kd_ref, a_ref, t_ref, cd_ref):
        ops = _dn_stack(cq_ref[...], ba_ref[...], al_ref[...], dt_ref[...], G)
        u, w, qd, kd, a_in, cd, T = _dn_prep(None, *ops)
        lane4 = lax.broadcasted_iota(jnp.int32, (1, DH), 1)
        for c in range(G):
            rows = slice(CH * c, CH * (c + 1))
            cdrow = jnp.zeros((1, DH), F32)
            for h in range(DH):
                n = DH * c + h
                lanes = slice(DK * h, DK * (h + 1))
                u_ref[rows, lanes] = u[n]
                w_ref[rows, lanes] = w[n]
                qd_ref[rows, lanes] = qd[n]
                kd_ref[rows, lanes] = kd[n]
                a_ref[rows, CH * h:CH * (h + 1)] = a_in[n]
                t_ref[rows, CH * h:CH * (h + 1)] = T[n]
                cdrow = cdrow + jnp.where(lane4 == h, cd[n], 0.0)
            cd_ref[c] = cdrow

    wide = jax.ShapeDtypeStruct((B, S, DNW), F32)
    sq = jax.ShapeDtypeStruct((B, S, DH * CH), F32)
    return pl.pallas_call(
        body, name="dn_prep_fwd", grid=(B, nc // G),
        out_shape=[wide, wide, wide, wide, sq, sq, jax.ShapeDtypeStruct((B, nc, 1, DH), F32)],
        in_specs=[_rows(G * CH, CONVW), _rows(G * CH, 2 * DH), _full((1, DH)), _full((1, DH))],
        out_specs=[_rows(G * CH, DNW)] * 4 + [_rows(G * CH, DH * CH)] * 2
                  + [pl.BlockSpec((None, G, 1, DH), lambda b, i: (b, i, 0, 0))],
        compiler_params=_cparams(("parallel", "parallel")),
    )(cq, ba, a_log, dt_b)


def _dn_seq_specs(B, nc, rev):
    at = (lambda i: nc - 1 - i) if rev else (lambda i: i)
    wide = pl.BlockSpec((B, CH, DNW), lambda i: (0, at(i), 0))
    a_spec = pl.BlockSpec((B, CH, DH * CH), lambda i: (0, at(i), 0))
    cd_spec = pl.BlockSpec((B, None, 1, DH), lambda i: (0, at(i), 0, 0))
    st_spec = pl.BlockSpec((B, None, DH, DK, DK), lambda i: (0, at(i), 0, 0, 0))
    return wide, a_spec, cd_spec, st_spec


def _dn_step_operands(B, u_ref, w_ref, qd_ref, kd_ref, a_ref, cd_ref):
    pairs = [(b, h) for b in range(B) for h in range(DH)]
    wide = lambda ref: jnp.stack([ref[b, :, DK * h:DK * (h + 1)] for b, h in pairs])
    a_in = jnp.stack([a_ref[b, :, CH * h:CH * (h + 1)] for b, h in pairs])
    cd = jnp.stack([cd_ref[b, :, h:h + 1] for b, h in pairs])
    return wide(u_ref), wide(w_ref), wide(qd_ref), wide(kd_ref), a_in, cd


def _dn_seq_fwd(u, w, qd, kd, a_in, cd):
    B, S, _ = u.shape
    nc = S // CH

    def body(u_ref, w_ref, qd_ref, kd_ref, a_ref, cd_ref, o_ref, st_ref, state):
        @pl.when(pl.program_id(0) == 0)
        def _():
            state[...] = jnp.zeros_like(state)

        S0 = state[...]
        for b in range(B):
            st_ref[b] = S0[DH * b:DH * (b + 1)]
        o, S1 = _dn_step(S0, *_dn_step_operands(B, u_ref, w_ref, qd_ref, kd_ref, a_ref, cd_ref))
        state[...] = S1
        for b in range(B):
            for h in range(DH):
                o_ref[b, :, DK * h:DK * (h + 1)] = o[DH * b + h]

    wide, a_spec, cd_spec, st_spec = _dn_seq_specs(B, nc, False)
    return pl.pallas_call(
        body, name="dn_seq_fwd", grid=(nc,),
        out_shape=[jax.ShapeDtypeStruct((B, S, DNW), F32), jax.ShapeDtypeStruct((B, nc, DH, DK, DK), F32)],
        in_specs=[wide, wide, wide, wide, a_spec, cd_spec],
        out_specs=[wide, st_spec],
        scratch_shapes=[pltpu.VMEM((B * DH, DK, DK), F32)],
        compiler_params=_cparams(("arbitrary",)),
    )(u, w, qd, kd, a_in, cd)


def _dn_seq_bwd(u, w, qd, kd, a_in, cd, states, do):
    B, S, _ = u.shape
    nc = S // CH

    def body(u_ref, w_ref, qd_ref, kd_ref, a_ref, cd_ref, st_ref, do_ref,
             du_ref, dw_ref, dqd_ref, dkd_ref, da_ref, dcd_ref, dstate):
        @pl.when(pl.program_id(0) == 0)
        def _():
            dstate[...] = jnp.zeros_like(dstate)

        lane4 = lax.broadcasted_iota(jnp.int32, (1, DH), 1)
        S0 = jnp.concatenate([st_ref[b] for b in range(B)], axis=0)
        do = jnp.stack([do_ref[b, :, DK * h:DK * (h + 1)] for b in range(B) for h in range(DH)])
        _, vjp = jax.vjp(_dn_step, S0, *_dn_step_operands(B, u_ref, w_ref, qd_ref, kd_ref, a_ref, cd_ref))
        dS, du, dw, dqd, dkd, da, dcd = vjp((do, dstate[...]))
        dstate[...] = dS
        for b in range(B):
            dcdrow = jnp.zeros((1, DH), F32)
            for h in range(DH):
                n = DH * b + h
                lanes = slice(DK * h, DK * (h + 1))
                du_ref[b, :, lanes] = du[n]
                dw_ref[b, :, lanes] = dw[n]
                dqd_ref[b, :, lanes] = dqd[n]
                dkd_ref[b, :, lanes] = dkd[n]
                da_ref[b, :, CH * h:CH * (h + 1)] = da[n]
                dcdrow = dcdrow + jnp.where(lane4 == h, dcd[n], 0.0)
            dcd_ref[b] = dcdrow

    wide, a_spec, cd_spec, st_spec = _dn_seq_specs(B, nc, True)
    sd = jax.ShapeDtypeStruct((B, S, DNW), F32)
    return pl.pallas_call(
        body, name="dn_seq_bwd", grid=(nc,),
        out_shape=[sd, sd, sd, sd, jax.ShapeDtypeStruct((B, S, DH * CH), F32), jax.ShapeDtypeStruct((B, nc, 1, DH), F32)],
        in_specs=[wide, wide, wide, wide, a_spec, cd_spec, st_spec, wide],
        out_specs=[wide, wide, wide, wide, a_spec, cd_spec],
        scratch_shapes=[pltpu.VMEM((B * DH, DK, DK), F32)],
        compiler_params=_cparams(("arbitrary",)),
    )(u, w, qd, kd, a_in, cd, states, do)


def _dn_prep_bwd(cq, ba, a_log, dt_b, t_inv, du, dw, dqd, dkd, da, dcd):
    B, S, _ = cq.shape
    nc = S // CH
    G = _dn_group(S, 4)

    def body(cq_ref, ba_ref, al_ref, dt_ref, t_ref, du_ref, dw_ref, dqd_ref, dkd_ref, da_ref, dcd_ref,
             dcq_ref, dba_ref, dal_ref, ddt_ref):
        @pl.when((pl.program_id(0) == 0) & (pl.program_id(1) == 0))
        def _():
            dal_ref[...] = jnp.zeros_like(dal_ref)
            ddt_ref[...] = jnp.zeros_like(ddt_ref)

        pairs = [(c, h) for c in range(G) for h in range(DH)]
        rows = lambda c: slice(CH * c, CH * (c + 1))
        wide = lambda ref: jnp.stack([ref[rows(c), DK * h:DK * (h + 1)] for c, h in pairs])
        square = lambda ref: jnp.stack([ref[rows(c), CH * h:CH * (h + 1)] for c, h in pairs])
        ops = _dn_stack(cq_ref[...], ba_ref[...], al_ref[...], dt_ref[...], G)
        cots = (wide(du_ref), wide(dw_ref), wide(dqd_ref), wide(dkd_ref), square(da_ref),
                jnp.stack([dcd_ref[c][:, h:h + 1] for c, h in pairs]), jnp.zeros((len(pairs), CH, CH), F32))
        _, vjp = jax.vjp(functools.partial(_dn_prep, square(t_ref)), *ops)
        dq, dk, dv, dar, dbr, dl, dd = vjp(cots)
        lane8 = lax.broadcasted_iota(jnp.int32, (CH, 2 * DH), 1)
        lane4 = lax.broadcasted_iota(jnp.int32, (1, DH), 1)
        dal = jnp.zeros((1, DH), F32)
        ddt = jnp.zeros((1, DH), F32)
        for c in range(G):
            dba = jnp.zeros((CH, 2 * DH), F32)
            for h in range(DH):
                n = DH * c + h
                dcq_ref[rows(c), DK * h:DK * (h + 1)] = dq[n]
                dcq_ref[rows(c), DNW + DK * h:DNW + DK * (h + 1)] = dk[n]
                dcq_ref[rows(c), 2 * DNW + DK * h:2 * DNW + DK * (h + 1)] = dv[n]
                dba = dba + jnp.where(lane8 == h, dbr[n], 0.0) + jnp.where(lane8 == DH + h, dar[n], 0.0)
                dal = dal + jnp.where(lane4 == h, dl[n], 0.0)
                ddt = ddt + jnp.where(lane4 == h, dd[n], 0.0)
            dba_ref[rows(c), :] = dba.astype(BF16)
        dal_ref[...] += dal
        ddt_ref[...] += ddt

    return pl.pallas_call(
        body, name="dn_prep_bwd", grid=(B, nc // G),
        out_shape=[jax.ShapeDtypeStruct((B, S, CONVW), F32), jax.ShapeDtypeStruct((B, S, 2 * DH), BF16),
                   jax.ShapeDtypeStruct((1, DH), F32), jax.ShapeDtypeStruct((1, DH), F32)],
        in_specs=[_rows(G * CH, CONVW), _rows(G * CH, 2 * DH), _full((1, DH)), _full((1, DH)), _rows(G * CH, DH * CH)]
                 + [_rows(G * CH, DNW)] * 4 + [_rows(G * CH, DH * CH),
                                               pl.BlockSpec((None, G, 1, DH), lambda b, i: (b, i, 0, 0))],
        out_specs=[_rows(G * CH, CONVW), _rows(G * CH, 2 * DH), _full((1, DH)), _full((1, DH))],
        compiler_params=_cparams(("arbitrary", "arbitrary")),
    )(cq, ba, a_log, dt_b, t_inv, du, dw, dqd, dkd, da, dcd)


def _gated_norm(o, z, g):
    outs = []
    for h in range(DH):
        t = o[:, DK * h:DK * (h + 1)]
        r = lax.rsqrt(jnp.mean(t * t, axis=-1, keepdims=True) + EPS)
        outs.append(t * r * g * _silu(z[:, DK * h:DK * (h + 1)]))
    return jnp.concatenate(outs, axis=1)


def _mix_fwd(x, o_attn, o_dn, z, ga, gd, mod, dn_g, w_branch, w_out):
    B, S, _ = x.shape
    tm = _tile(S)

    def body(x_ref, oa_ref, od_ref, z_ref, ga_ref, gd_ref, mod_ref, g_ref, wb_ref, wo_ref,
             x1_ref, mix_ref, mg_ref, ob_ref):
        oa = oa_ref[...].astype(BF16)
        od = _gated_norm(od_ref[...], z_ref[...], g_ref[...]).astype(BF16)
        ob_ref[0] = oa
        ob_ref[1] = od
        ya = jnp.dot(oa, wb_ref[0:QW, :], preferred_element_type=F32)
        yd = jnp.dot(od, wb_ref[QW:QW + DNW, :], preferred_element_type=F32)
        merged = (_sigmoid(ga_ref[...]) * ya + _sigmoid(gd_ref[...]) * yd).astype(BF16)
        mg_ref[...] = merged
        mix = jnp.dot(merged, wo_ref[...], preferred_element_type=F32)
        mix_ref[...] = mix
        x1_ref[...] = x_ref[...] + mod_ref[2:3, :] * mix

    return pl.pallas_call(
        body, name="mix_fwd", grid=(B, S // tm),
        out_shape=[jax.ShapeDtypeStruct((B, S, D), F32), jax.ShapeDtypeStruct((B, S, D), F32),
                   jax.ShapeDtypeStruct((B, S, D), BF16), jax.ShapeDtypeStruct((B, 2, S, QW), BF16)],
        in_specs=[_rows(tm, D), _rows(tm, QW), _rows(tm, DNW), _rows(tm, DNW), _rows(tm, D), _rows(tm, D),
                  _perb(6, D), _full((1, DK)), _resident(w_branch.shape), _resident(w_out.shape)],
        out_specs=[_rows(tm, D), _rows(tm, D), _rows(tm, D), _stacked(2, tm, QW)],
        compiler_params=_cparams(("parallel", "arbitrary")),
    )(x, o_attn, o_dn, z, ga, gd, mod, dn_g, w_branch, w_out)


def _mix_bwd(dx1, mix, o_attn, o_dn, z, ga, gd, mod, dn_g, w_branch, w_out):
    B, S, _ = dx1.shape
    tm = _tile(S)

    def body(dx1_ref, mix_ref, oa_ref, od_ref, z_ref, ga_ref, gd_ref, mod_ref, g_ref, wb_ref, wo_ref,
             dmix_ref, dyo_ref, dga_ref, dgd_ref, dz_ref, doa_ref, dod_ref, dgate_ref, dg_ref):
        b, i = pl.program_id(0), pl.program_id(1)
        dx1 = dx1_ref[...]
        dmix = (dx1 * mod_ref[2:3, :]).astype(BF16)
        dmix_ref[...] = dmix
        dgate = jnp.sum(dx1 * mix_ref[...], axis=0, keepdims=True)
        dmerged = _dot_nt(dmix, wo_ref[...])
        odn, gn_vjp = jax.vjp(_gated_norm, od_ref[...], z_ref[...], g_ref[...])
        ya = _dot(oa_ref[...], wb_ref[0:QW, :])
        yd = _dot(odn, wb_ref[QW:QW + DNW, :])
        sa, sd = _sigmoid(ga_ref[...]), _sigmoid(gd_ref[...])
        dya = (dmerged * sa).astype(BF16)
        dyd = (dmerged * sd).astype(BF16)
        dyo_ref[0] = dya
        dyo_ref[1] = dyd
        dga_ref[...] = (dmerged * ya * sa * (1.0 - sa)).astype(BF16)
        dgd_ref[...] = (dmerged * yd * sd * (1.0 - sd)).astype(BF16)
        doa_ref[...] = _dot_nt(dya, wb_ref[0:QW, :])
        dodn = _dot_nt(dyd, wb_ref[QW:QW + DNW, :])
        dod, dz, dg = gn_vjp(dodn)
        dod_ref[...] = dod
        dz_ref[...] = dz.astype(BF16)

        @pl.when(i == 0)
        def _():
            dgate_ref[...] = jnp.zeros_like(dgate_ref)

        @pl.when((b == 0) & (i == 0))
        def _():
            dg_ref[...] = jnp.zeros_like(dg_ref)

        dgate_ref[...] += dgate
        dg_ref[...] += dg

    return pl.pallas_call(
        body, name="mix_bwd", grid=(B, S // tm),
        out_shape=[jax.ShapeDtypeStruct((B, S, D), BF16), jax.ShapeDtypeStruct((B, 2, S, D), BF16),
                   jax.ShapeDtypeStruct((B, S, D), BF16), jax.ShapeDtypeStruct((B, S, D), BF16),
                   jax.ShapeDtypeStruct((B, S, DNW), BF16),
                   jax.ShapeDtypeStruct((B, S, QW), F32), jax.ShapeDtypeStruct((B, S, DNW), F32),
                   jax.ShapeDtypeStruct((B, 1, D), F32), jax.ShapeDtypeStruct((1, DK), F32)],
        in_specs=[_rows(tm, D), _rows(tm, D), _rows(tm, QW), _rows(tm, DNW), _rows(tm, DNW), _rows(tm, D),
                  _rows(tm, D), _perb(6, D), _full((1, DK)), _resident(w_branch.shape), _resident(w_out.shape)],
        out_specs=[_rows(tm, D), _stacked(2, tm, D), _rows(tm, D), _rows(tm, D), _rows(tm, DNW),
                   _rows(tm, QW), _rows(tm, DNW), _perb(1, D), _full((1, DK))],
        compiler_params=_cparams(("arbitrary", "arbitrary")),
    )(dx1, mix, o_attn, o_dn, z, ga, gd, mod, dn_g, w_branch, w_out)


GU_SHARD = 2 * FFN // N_DEV
GU_HALF = N_DEV // 2


def _ffn1_fwd(x1, mod, g2, w_gu):
    B, S, _ = x1.shape
    tm = _tile(S)

    def body(x_ref, mod_ref, g_ref, w_ref, h_ref, gate_ref, up_ref, act_ref):
        h = _rms_mod(x_ref[...], g_ref[...], mod_ref[4:5, :], mod_ref[3:4, :]).astype(BF16)
        h_ref[...] = h
        for j in range(GU_HALF):
            gate = jnp.dot(h, w_ref[j], preferred_element_type=F32)
            up = jnp.dot(h, w_ref[GU_HALF + j], preferred_element_type=F32)
            gate_ref[j] = gate
            up_ref[j] = up
            act_ref[j] = (_silu(gate) * up).astype(BF16)

    blk = lambda dt: jax.ShapeDtypeStruct((B, GU_HALF, S, GU_SHARD), dt)
    return pl.pallas_call(
        body, name="ffn1_fwd", grid=(B, S // tm),
        out_shape=[jax.ShapeDtypeStruct((B, S, D), BF16), blk(F32), blk(F32), blk(BF16)],
        in_specs=[_rows(tm, D), _perb(6, D), _full((1, D)), _resident(w_gu.shape)],
        out_specs=[_rows(tm, D)] + [_stacked(GU_HALF, tm, GU_SHARD)] * 3,
        compiler_params=_cparams(("parallel", "arbitrary")),
    )(x1, mod, g2, w_gu)


def _ffn2_fwd(act, x1, target, mod, w_down):
    B, S, _ = x1.shape
    tm = _tile(S)

    def body(a_ref, x_ref, t_ref, mod_ref, w_ref, dy_ref, loss_ref, dgate_ref):
        b, i = pl.program_id(0), pl.program_id(1)
        y = jnp.dot(a_ref[0], w_ref[0], preferred_element_type=F32)
        for j in range(1, GU_HALF):
            y = y + jnp.dot(a_ref[j], w_ref[j], preferred_element_type=F32)
        err = x_ref[...] + mod_ref[5:6, :] * y - t_ref[...]
        dy = err * (1.0 / D)
        dy_ref[...] = dy

        @pl.when((b == 0) & (i == 0))
        def _():
            loss_ref[...] = jnp.zeros_like(loss_ref)

        @pl.when(i == 0)
        def _():
            dgate_ref[...] = jnp.zeros_like(dgate_ref)

        loss_ref[...] += (0.5 / D) * jnp.sum(err * err)
        dgate_ref[...] += jnp.sum(dy * y, axis=0, keepdims=True)

    return pl.pallas_call(
        body, name="ffn2_fwd", grid=(B, S // tm),
        out_shape=[jax.ShapeDtypeStruct((B, S, D), F32), jax.ShapeDtypeStruct((1, 128), F32),
                   jax.ShapeDtypeStruct((B, 1, D), F32)],
        in_specs=[_stacked(GU_HALF, tm, GU_SHARD), _rows(tm, D), _rows(tm, D), _perb(6, D), _resident(w_down.shape)],
        out_specs=[_rows(tm, D), _full((1, 128)), _perb(1, D)],
        compiler_params=_cparams(("arbitrary", "arbitrary")),
    )(act, x1, target, mod, w_down)


def _ffn2_bwd(dy, gate, up, mod, w_down):
    B, S, _ = dy.shape
    tm = _tile(S)

    def body(dy_ref, gate_ref, up_ref, mod_ref, w_ref, dgu_ref, dyg_ref):
        dyg = (dy_ref[...] * mod_ref[5:6, :]).astype(BF16)
        dyg_ref[...] = dyg
        for j in range(GU_HALF):
            dact = _dot_nt(dyg, w_ref[j])
            gate, up = gate_ref[j], up_ref[j]
            sg = _sigmoid(gate)
            dgu_ref[j] = (dact * up * (sg * (1.0 + gate * (1.0 - sg)))).astype(BF16)
            dgu_ref[GU_HALF + j] = (dact * (gate * sg)).astype(BF16)

    return pl.pallas_call(
        body, name="ffn2_bwd", grid=(B, S // tm),
        out_shape=[jax.ShapeDtypeStruct((B, N_DEV, S, GU_SHARD), BF16), jax.ShapeDtypeStruct((B, S, D), BF16)],
        in_specs=[_rows(tm, D), _stacked(GU_HALF, tm, GU_SHARD), _stacked(GU_HALF, tm, GU_SHARD), _perb(6, D),
                  _resident(w_down.shape)],
        out_specs=[_stacked(N_DEV, tm, GU_SHARD), _rows(tm, D)],
        compiler_params=_cparams(("parallel", "arbitrary")),
    )(dy, gate, up, mod, w_down)


def _ffn1_bwd(dgu, x1, dy, mod, g2, w_gu):
    B, S, _ = x1.shape
    tm = _tile(S)

    def body(dgu_ref, x_ref, dy_ref, mod_ref, g_ref, w_ref, dx1_ref, dg_ref, dsc_ref, dsh_ref):
        b, i = pl.program_id(0), pl.program_id(1)
        dh = _dot_nt(dgu_ref[0], w_ref[0])
        for j in range(1, N_DEV):
            dh = dh + _dot_nt(dgu_ref[j], w_ref[j])
        _, vjp = jax.vjp(_rms_mod, x_ref[...], g_ref[...], mod_ref[4:5, :], mod_ref[3:4, :])
        dx, dg, dsc, dsh = vjp(dh)
        dx1_ref[...] = dy_ref[...] + dx

        @pl.when((b == 0) & (i == 0))
        def _():
            dg_ref[...] = jnp.zeros_like(dg_ref)

        @pl.when(i == 0)
        def _():
            dsc_ref[...] = jnp.zeros_like(dsc_ref)
            dsh_ref[...] = jnp.zeros_like(dsh_ref)

        dg_ref[...] += dg
        dsc_ref[...] += dsc
        dsh_ref[...] += dsh

    return pl.pallas_call(
        body, name="ffn1_bwd", grid=(B, S // tm),
        out_shape=[jax.ShapeDtypeStruct((B, S, D), F32), jax.ShapeDtypeStruct((1, D), F32),
                   jax.ShapeDtypeStruct((B, 1, D), F32), jax.ShapeDtypeStruct((B, 1, D), F32)],
        in_specs=[_stacked(N_DEV, tm, GU_SHARD), _rows(tm, D), _rows(tm, D), _perb(6, D), _full((1, D)),
                  _resident(w_gu.shape)],
        out_specs=[_rows(tm, D), _full((1, D)), _perb(1, D), _perb(1, D)],
        compiler_params=_cparams(("arbitrary", "arbitrary")),
    )(dgu, x1, dy, mod, g2, w_gu)


def _adamw(w, g, m, v, name):
    def body(w_ref, g_ref, m_ref, v_ref, d_ref, nm_ref, nv_ref):
        g = g_ref[...]
        m = B1 * m_ref[...] + (1.0 - B1) * g
        v = B2 * v_ref[...] + (1.0 - B2) * (g * g)
        nm_ref[...] = m
        nv_ref[...] = v
        m_hat = m / (1.0 - B1 ** STEP)
        v_hat = v / (1.0 - B2 ** STEP)
        d_ref[...] = -LR * (m_hat / (jnp.sqrt(v_hat) + AEPS) + WD * w_ref[...])

    sd = jax.ShapeDtypeStruct(w.shape, F32)
    return pl.pallas_call(body, name=name, out_shape=(sd, sd, sd), compiler_params=_cparams())(w, g, m, v)


def kernel(x, c, positions, ada_w, ada_b, norm1_g, w_in, conv_w, q_norm_g, k_norm_g, sinks, a_log, dt_bias, dn_norm_g, w_branch, w_out, norm2_g, w_gate_up, w_down, loss_target, m_ada_w, m_ada_b, m_norm1_g, m_w_in, m_conv_w, m_q_norm_g, m_k_norm_g, m_sinks, m_a_log, m_dt_bias, m_dn_norm_g, m_w_branch, m_w_out, m_norm2_g, m_w_gate_up, m_w_down, v_ada_w, v_ada_b, v_norm1_g, v_w_in, v_conv_w, v_q_norm_g, v_k_norm_g, v_sinks, v_a_log, v_dt_bias, v_dn_norm_g, v_w_branch, v_w_out, v_norm2_g, v_w_gate_up, v_w_down):
    B, S, _ = x.shape
    me = 4 * lax.axis_index("x") + 2 * lax.axis_index("y") + lax.axis_index("c")

    shards = [w[0].astype(BF16) for w in (w_in, w_branch, w_out, w_gate_up, w_down)]

    c_all = _all_gather_small(c, "gather_c").reshape(N_DEV * B, D)
    ncol = 6 * D // N_DEV
    mod_cols, cond_all = _ada_fwd(c_all, ada_w[0], lax.dynamic_slice(ada_b, (0, me * ncol), (1, ncol)))
    mod_all = _all_gather_small(mod_cols, "gather_mod").transpose(1, 0, 2).reshape(N_DEV * B, 6 * D)
    mod = lax.dynamic_slice(mod_all, (me * B, 0), (B, 6 * D)).reshape(B, 6, D)
    conv2 = conv_w.reshape(CONV, CONVW // N_DEV)
    conv_all = _all_gather_small(conv2, "gather_conv").transpose(1, 0, 2).reshape(CONV, CONVW)

    w_sems, w_srcs, w_lands, _ = _copies_start(shards, [_place_own(s, me) for s in shards], False, (mod, conv_all),
                                              "gather_start")

    (w_in_b,) = _copies_wait(w_sems[:2], w_srcs[:1], w_lands[:1], mod, "gather_wait_in")
    h1, aq, akv, dnx, ba, z, ga, gd = _inproj_fwd(x, mod, norm1_g, w_in_b)
    invf, mean_q, mean_k = _attn_consts()
    rope_cos, rope_sin = _rope_tables(positions.reshape(B, S, 1), invf)
    o_attn = _attn_fwd(aq, akv, rope_cos, rope_sin, q_norm_g, k_norm_g, sinks, mean_q, mean_k)
    cq = _conv_fwd(dnx, conv_all)
    dn_u, dn_w, dn_qd, dn_kd, dn_a, dn_t, dn_cd = _dn_prep_fwd(cq, ba, a_log, dt_bias)
    o_dn, states = _dn_seq_fwd(dn_u, dn_w, dn_qd, dn_kd, dn_a, dn_cd)
    w_branch_g, w_out_g, w_gu_b, w_down_g = _copies_wait(w_sems[2:], w_srcs[1:], w_lands[1:], o_dn, "gather_wait_rest")
    w_branch_f = w_branch_g.reshape(D, D)
    w_out_f = w_out_g.reshape(D, D)
    w_down_b = w_down_g.reshape(GU_HALF, GU_SHARD, D)
    x1, mix, merged, ob = _mix_fwd(x, o_attn, o_dn, z, ga, gd, mod, dn_norm_g, w_branch_f, w_out_f)
    h2, gate, up, act = _ffn1_fwd(x1, mod, norm2_g, w_gu_b)
    dy, loss_part, d_gate2 = _ffn2_fwd(act, x1, loss_target, mod, w_down_b)
    loss = lax.psum(loss_part[0, 0], ("x", "y", "c"))

    one = lambda t: t.reshape(B, 1, S, t.shape[-1])
    dgu, dyg = _ffn2_bwd(dy, gate, up, mod, w_down_b)
    g_w_down = _wgrad(act, one(dyg), "wgrad_down")
    dx1, d_n2g, d_scale2, d_shift2 = _ffn1_bwd(dgu, x1, dy, mod, norm2_g, w_gu_b)
    g_w_gu = _wgrad(one(h2), dgu, "wgrad_gate_up")
    ffn = _exchange_start([g_w_gu, g_w_down.reshape(N_DEV, FFN // N_DEV, D)], me, dx1, "exchange_ffn_start")
    dmix, dyo, dga, dgd, dz, d_oa, d_od, d_gate1, d_dng = _mix_bwd(
        dx1, mix, o_attn, o_dn, z, ga, gd, mod, dn_norm_g + ffn[3][0, 0], w_branch_f, w_out_f)
    d_dn = _dn_seq_bwd(dn_u, dn_w, dn_qd, dn_kd, dn_a, dn_cd, states, d_od)
    dcq, dba, d_alog, d_dtb = _dn_prep_bwd(cq, ba, a_log, dt_bias, dn_t, *d_dn)
    ddnx, d_conv = _conv_bwd(dnx, conv_all, dcq)
    daq, dakv, d_qg, d_kg, d_sinks = _attn_bwd(aq, akv, rope_cos, rope_sin, q_norm_g, k_norm_g, sinks, mean_q, mean_k, d_oa)
    dps = [daq, dakv, ddnx, dba, dz, dga, dgd]
    dblk, grad_x, d_n1g, d_scale1, d_shift1 = _inproj_bwd(x, mod, norm1_g, dx1, dps, w_in_b)

    dmod = jnp.concatenate([d_shift1, d_scale1, d_gate1, d_shift2, d_scale2, d_gate2], axis=2).reshape(B, 6 * D)
    small = jnp.concatenate([d_n1g, d_qg, d_kg, d_sinks, d_alog, d_dtb, d_dng, d_n2g, d_conv.reshape(1, CONV * CONVW)], axis=1)
    nsm = small.shape[1]
    width = -(-max(6 * D, nsm) // 128) * 128
    rows = jnp.concatenate([jnp.pad(dmod, ((0, 0), (0, width - 6 * D))), jnp.pad(small, ((0, 8 - B - 1), (0, width - nsm)))], axis=0)
    rows_all = _all_gather_small(rows, "gather_small")
    dmod_all = rows_all[:, 0:B, 0:6 * D].reshape(N_DEV * B, 6 * D)
    dmod_cols = lax.dynamic_slice(dmod_all, (0, me * ncol), (N_DEV * B, ncol))
    grad_ada_w, grad_ada_b, small_sum = _ada_bwd(cond_all, dmod_all, dmod_cols, rows_all[:, B, :])
    sizes = [D, HD, HD, HQ, DH, DH, DK, D]
    so = np.cumsum([0] + sizes)
    g_n1, g_qg, g_kg, g_sk, g_al, g_dt, g_dn, g_n2 = [small_sum[:, so[i]:so[i + 1]] for i in range(8)]
    g_conv_all = small_sum[:, so[8]:so[8] + CONV * CONVW].reshape(CONV, N_DEV, CONVW // N_DEV)
    grad_conv = lax.dynamic_slice(g_conv_all, (0, me, 0), (CONV, 1, CONVW // N_DEV)).reshape(CONV, CONVW // N_DEV)

    g_w_in = _wgrad(one(h1), dblk, "wgrad_in", after=small_sum)
    proj = _exchange_start([g_w_in], me, small_sum, "exchange_in_start")
    g_w_out = _wgrad(one(merged), one(dmix), "wgrad_out", after=proj[3])
    g_w_branch = _wgrad(ob, dyo, "wgrad_branch", after=proj[3])
    mixer = _exchange_start([g_w_branch.reshape(N_DEV, D // N_DEV, D), g_w_out.reshape(N_DEV, D // N_DEV, D)], me,
                            proj[3], "exchange_mix_start")

    grad_w_gu, grad_w_down = [_sum_blocks(r, "sum_grads_" + nm) for r, nm in zip(
        _copies_wait(*ffn[:3], mixer[3], "exchange_ffn_wait"), ["gate_up", "down"])]
    (grad_w_in,) = [_sum_blocks(r, "sum_grads_in") for r in _copies_wait(*proj[:3], grad_w_gu, "exchange_in_wait")]
    grad_w_branch, grad_w_out = [_sum_blocks(r, "sum_grads_" + nm) for r, nm in zip(
        _copies_wait(*mixer[:3], grad_w_in, "exchange_mix_wait"), ["branch", "out"])]

    big = [(ada_w, grad_ada_w.reshape(ada_w.shape), m_ada_w, v_ada_w), (w_in, grad_w_in, m_w_in, v_w_in),
           (w_branch, grad_w_branch, m_w_branch, v_w_branch), (w_out, grad_w_out, m_w_out, v_w_out),
           (w_gate_up, grad_w_gu, m_w_gate_up, v_w_gate_up), (w_down, grad_w_down, m_w_down, v_w_down)]
    upd = {}
    for nm, (w, g, m, v) in zip(["ada_w", "w_in", "w_branch", "w_out", "w_gate_up", "w_down"], big):
        upd[nm] = _adamw(w, g, m, v, "adamw_" + nm)
    small_names = ["ada_b", "norm1_g", "q_norm_g", "k_norm_g", "sinks", "a_log", "dt_bias", "dn_norm_g", "norm2_g", "conv_w"]
    small_w = [ada_b, norm1_g, q_norm_g, k_norm_g, sinks, a_log, dt_bias, dn_norm_g, norm2_g, conv_w]
    small_g = [grad_ada_b, g_n1, g_qg, g_kg, g_sk, g_al, g_dt, g_dn, g_n2, grad_conv]
    small_m = [m_ada_b, m_norm1_g, m_q_norm_g, m_k_norm_g, m_sinks, m_a_log, m_dt_bias, m_dn_norm_g, m_norm2_g, m_conv_w]
    small_v = [v_ada_b, v_norm1_g, v_q_norm_g, v_k_norm_g, v_sinks, v_a_log, v_dt_bias, v_dn_norm_g, v_norm2_g, v_conv_w]
    cat = lambda arrs: jnp.concatenate([a.reshape(1, -1) for a in arrs], axis=1)
    res = _adamw(cat(small_w), cat(small_g), cat(small_m), cat(small_v), "adamw_small")
    po = np.cumsum([0] + [int(np.prod(w.shape)) for w in small_w])
    grads = {}
    for i, nm in enumerate(small_names):
        upd[nm] = tuple(r[:, po[i]:po[i + 1]].reshape(small_w[i].shape) for r in res)
        grads[nm] = small_g[i].reshape(small_w[i].shape)
    grads.update(ada_w=grad_ada_w.reshape(ada_w.shape), w_in=grad_w_in, w_branch=grad_w_branch, w_out=grad_w_out,
                 w_gate_up=grad_w_gu, w_down=grad_w_down)

    order = ["ada_w", "ada_b", "norm1_g", "w_in", "conv_w", "q_norm_g", "k_norm_g", "sinks", "a_log", "dt_bias",
             "dn_norm_g", "w_branch", "w_out", "norm2_g", "w_gate_up", "w_down"]
    return (loss, grad_x, *[grads[n] for n in order], *[upd[n][0] for n in order],
            *[upd[n][1] for n in order], *[upd[n][2] for n in order])
```

```python
import functools

import numpy as np
import jax
import jax.numpy as jnp
from jax import lax
from jax.experimental import pallas as pl
from jax.experimental.pallas import tpu as pltpu

F32 = jnp.float32
BF16 = jnp.bfloat16
HI = lax.Precision.HIGHEST

N_DEV = 8
D = 1024
HQ, HKV, HD = 8, 2, 64
GRP = HQ // HKV
BLK = 128
ROT = HD // 4
THETA = 500000.0
QW, KVW = HQ * HD, HKV * HD
DH, DK = 4, 128
CH = 64
DNW = DH * DK
CONV = 4
CONVW = 3 * DNW
FFN = 2816
EPS = 1e-6
IN_W = QW + 2 * KVW + CONVW + 2 * DH + DNW + 2 * D

LR, B1, B2, AEPS, WD, STEP = 0.001, 0.9, 0.999, 1e-08, 0.01, 10

VMEM_LIMIT = 56 * 1024 * 1024
MESH = pl.DeviceIdType.MESH


def _cparams(sem=None, vmem=VMEM_LIMIT):
    return pltpu.CompilerParams(dimension_semantics=sem, vmem_limit_bytes=vmem)


def _full(shape):
    n = len(shape)
    return pl.BlockSpec(shape, lambda *_: (0,) * n)


def _resident(shape):
    n = len(shape)
    return pl.BlockSpec(shape, lambda *_: (0,) * n, pipeline_mode=pl.Buffered(1))


def _rows(tm, w):
    return pl.BlockSpec((None, tm, w), lambda b, i: (b, i, 0))


def _stacked(n, tm, w):
    return pl.BlockSpec((None, n, tm, w), lambda b, i: (b, 0, i, 0))


def _perb(r, w):
    return pl.BlockSpec((None, r, w), lambda b, i: (b, 0, 0))


def _dot(a, b):
    return jnp.dot(a.astype(BF16), b.astype(BF16), preferred_element_type=F32)


def _dot_nt(a, b):
    return lax.dot_general(a.astype(BF16), b.astype(BF16), (((1,), (1,)), ((), ())), preferred_element_type=F32)


def _dot_tn(a, b):
    return lax.dot_general(a.astype(BF16), b.astype(BF16), (((0,), (0,)), ((), ())), preferred_element_type=F32)


def _dot_hi(a, b):
    return jnp.dot(a, b, preferred_element_type=F32, precision=HI)


def _sigmoid(x):
    return jax.nn.sigmoid(x)


def _silu(x):
    return x * jax.nn.sigmoid(x)


def _rms_mod(x, g, scale, shift):
    r = lax.rsqrt(jnp.mean(x * x, axis=-1, keepdims=True) + EPS)
    return (x * r * g) * (1.0 + scale) + shift


def _tile(S):
    return min(256, S)


def _peer(x, y, c, k):
    px = 1 - x if (k >> 2) & 1 else x
    py = 1 - y if (k >> 1) & 1 else y
    pc = 1 - c if k & 1 else c
    return px, py, pc


def _all_gather_small(v, name):
    r, n = v.shape

    def body(v_ref, out_ref, send_sems, recv_sems, local_sem):
        x, y, c = lax.axis_index("x"), lax.axis_index("y"), lax.axis_index("c")
        me = 4 * x + 2 * y + c
        mine = pltpu.make_async_copy(v_ref, out_ref.at[me], local_sem)
        mine.start()
        sends = []
        for k in range(1, N_DEV):
            cp = pltpu.make_async_remote_copy(
                src_ref=v_ref, dst_ref=out_ref.at[me], send_sem=send_sems.at[k - 1], recv_sem=recv_sems.at[k - 1],
                device_id=_peer(x, y, c, k), device_id_type=MESH)
            cp.start()
            sends.append(cp)
        for k in range(1, N_DEV):
            px, py, pc = _peer(x, y, c, k)
            pltpu.make_async_remote_copy(
                src_ref=v_ref, dst_ref=out_ref.at[4 * px + 2 * py + pc], send_sem=send_sems.at[k - 1],
                recv_sem=recv_sems.at[k - 1], device_id=(px, py, pc), device_id_type=MESH).wait_recv()
        for cp in sends:
            cp.wait_send()
        mine.wait()

    return pl.pallas_call(
        body, name=name,
        out_shape=jax.ShapeDtypeStruct((N_DEV, r, n), v.dtype),
        in_specs=[pl.BlockSpec(memory_space=pltpu.VMEM)],
        out_specs=pl.BlockSpec(memory_space=pltpu.VMEM),
        scratch_shapes=[pltpu.SemaphoreType.DMA((N_DEV - 1,)), pltpu.SemaphoreType.DMA((N_DEV - 1,)), pltpu.SemaphoreType.DMA],
    )(v)


def _all_gather_big(vs, name):
    na = len(vs)

    def body(*refs):
        v_refs, out_refs = refs[:na], refs[na:2 * na]
        send_sems, recv_sems, local_sems = refs[2 * na:]
        x, y, c = lax.axis_index("x"), lax.axis_index("y"), lax.axis_index("c")
        me, sibling = (x, y, c), (x, y, 1 - c)
        chips = [(1 - x, y), (x, 1 - y), (1 - x, 1 - y)]

        def rows(a, px, py, pc):
            return out_refs[a].at[4 * px + 2 * py + pc]

        def copy(a, k, block, to, src=None):
            return pltpu.make_async_remote_copy(
                src_ref=rows(a, *block) if src is None else src, dst_ref=rows(a, *block),
                send_sem=send_sems.at[7 * a + k], recv_sem=recv_sems.at[7 * a + k], device_id=to, device_id_type=MESH)

        mine = [pltpu.make_async_copy(v_refs[a], rows(a, *me), local_sems.at[a]) for a in range(na)]
        for cp in mine:
            cp.start()
        first = []
        for a in range(na):
            first.append(copy(a, 0, me, sibling, src=v_refs[a]))
            first += [copy(a, 1 + j, me, (*chip, c), src=v_refs[a]) for j, chip in enumerate(chips)]
        for cp in first:
            cp.start()
        passed = []
        for j, chip in enumerate(chips):
            for a in range(na):
                copy(a, 1 + j, (*chip, c), me).wait_recv()
                forward = copy(a, 4 + j, (*chip, c), sibling)
                forward.start()
                passed.append(forward)
        for a in range(na):
            copy(a, 0, sibling, me).wait_recv()
            for j, chip in enumerate(chips):
                copy(a, 4 + j, (*chip, 1 - c), me).wait_recv()
        for cp in first + passed:
            cp.wait_send()
        for cp in mine:
            cp.wait()

    return pl.pallas_call(
        body, name=name,
        out_shape=[jax.ShapeDtypeStruct((N_DEV,) + v.shape, v.dtype) for v in vs],
        in_specs=[pl.BlockSpec(memory_space=pl.ANY)] * na,
        out_specs=[pl.BlockSpec(memory_space=pl.ANY)] * na,
        scratch_shapes=[pltpu.SemaphoreType.DMA((7 * na,)), pltpu.SemaphoreType.DMA((7 * na,)),
                        pltpu.SemaphoreType.DMA((na,))],
    )(*vs)


def _exchange_blocks(gs, name):
    na = len(gs)

    def body(*refs):
        g_refs, out_refs = refs[:na], refs[na:2 * na]
        send_sems, recv_sems, local_sems = refs[2 * na:]
        x, y, c = lax.axis_index("x"), lax.axis_index("y"), lax.axis_index("c")
        me = 4 * x + 2 * y + c
        mine = [pltpu.make_async_copy(g_refs[a].at[me], out_refs[a].at[me], local_sems.at[a]) for a in range(na)]
        for cp in mine:
            cp.start()
        sends = []
        for k in range(1, N_DEV):
            px, py, pc = _peer(x, y, c, k)
            for a in range(na):
                cp = pltpu.make_async_remote_copy(
                    src_ref=g_refs[a].at[4 * px + 2 * py + pc], dst_ref=out_refs[a].at[me],
                    send_sem=send_sems.at[7 * a + k - 1], recv_sem=recv_sems.at[7 * a + k - 1],
                    device_id=(px, py, pc), device_id_type=MESH)
                cp.start()
                sends.append(cp)
        for k in range(1, N_DEV):
            px, py, pc = _peer(x, y, c, k)
            for a in range(na):
                pltpu.make_async_remote_copy(
                    src_ref=g_refs[a].at[me], dst_ref=out_refs[a].at[4 * px + 2 * py + pc],
                    send_sem=send_sems.at[7 * a + k - 1], recv_sem=recv_sems.at[7 * a + k - 1],
                    device_id=(px, py, pc), device_id_type=MESH).wait_recv()
        for cp in sends:
            cp.wait_send()
        for cp in mine:
            cp.wait()

    return pl.pallas_call(
        body, name=name,
        out_shape=[jax.ShapeDtypeStruct(g.shape, g.dtype) for g in gs],
        in_specs=[pl.BlockSpec(memory_space=pl.ANY)] * na,
        out_specs=[pl.BlockSpec(memory_space=pl.ANY)] * na,
        scratch_shapes=[pltpu.SemaphoreType.DMA((7 * na,)), pltpu.SemaphoreType.DMA((7 * na,)),
                        pltpu.SemaphoreType.DMA((na,))],
    )(*gs)


_HBM = pl.BlockSpec(memory_space=pltpu.HBM)
_SEM = pl.BlockSpec(memory_space=pltpu.SEMAPHORE)
_EFFECT = pltpu.SideEffectType.DATAFLOW_SIDE_EFFECTING


def _place_own(block, me):
    land = lax.empty((N_DEV,) + block.shape, block.dtype)
    return lax.dynamic_update_slice(land, block[None], (me,) + (0,) * block.ndim)


def _copies_start(srcs, lands, scatter, after, name):
    na = len(srcs)
    afters = tuple(after) if isinstance(after, (tuple, list)) else (after,)

    def body(*refs):
        src_refs, land_refs = refs[:na], refs[na:2 * na]
        sems = refs[2 * na + len(afters):4 * na + len(afters)]
        token = refs[-1]
        x, y, c = lax.axis_index("x"), lax.axis_index("y"), lax.axis_index("c")
        me = 4 * x + 2 * y + c
        for a in range(na):
            for k in range(1, N_DEV):
                px, py, pc = _peer(x, y, c, k)
                src = src_refs[a].at[4 * px + 2 * py + pc] if scatter else src_refs[a]
                pltpu.make_async_remote_copy(
                    src_ref=src, dst_ref=land_refs[a].at[me], send_sem=sems[2 * a], recv_sem=sems[2 * a + 1],
                    device_id=(px, py, pc), device_id_type=MESH).start()
        token[...] = jnp.zeros_like(token)

    hbm = lambda t: pltpu.HBM(t.shape, t.dtype)
    out = pl.pallas_call(
        body, name=name,
        out_shape=tuple([pltpu.SemaphoreType.DMA(())] * (2 * na) + [hbm(t) for t in srcs] + [hbm(t) for t in lands]
                        + [jax.ShapeDtypeStruct((8, 128), F32)]),
        in_specs=[_HBM] * (2 * na) + [pl.BlockSpec(memory_space=pl.ANY)] * len(afters),
        out_specs=tuple([_SEM] * (2 * na) + [_HBM] * (2 * na) + [pl.BlockSpec(memory_space=pltpu.VMEM)]),
        input_output_aliases={i: 2 * na + i for i in range(2 * na)},
        compiler_params=pltpu.CompilerParams(has_side_effects=_EFFECT),
    )(*[pltpu.with_memory_space_constraint(t, pltpu.HBM) for t in list(srcs) + list(lands)], *afters)
    return out[:2 * na], out[2 * na:3 * na], out[3 * na:4 * na], out[-1]


def _exchange_start(gs, me, after, name):
    own = [lax.dynamic_index_in_dim(g, me, 0, keepdims=False) for g in gs]
    return _copies_start(gs, [_place_own(o, me) for o in own], True, after, name)


def _copies_wait(sems, srcs, lands, after, name):
    na = len(srcs)

    def body(*refs):
        land_refs = refs[na:2 * na]
        sem_refs = refs[2 * na:4 * na]
        x, y, c = lax.axis_index("x"), lax.axis_index("y"), lax.axis_index("c")
        for a in range(na):
            seven = land_refs[a].at[pl.ds(0, N_DEV - 1)]
            copy = pltpu.make_async_remote_copy(
                src_ref=seven, dst_ref=seven, send_sem=sem_refs[2 * a], recv_sem=sem_refs[2 * a + 1],
                device_id=(x, y, c), device_id_type=MESH)
            copy.wait_send()
            copy.wait_recv()

    hbm = lambda t: pltpu.HBM(t.shape, t.dtype)
    out = pl.pallas_call(
        body, name=name,
        out_shape=tuple([hbm(t) for t in srcs] + [hbm(t) for t in lands]),
        in_specs=[_HBM] * (2 * na) + [_SEM] * (2 * na) + [pl.BlockSpec(memory_space=pl.ANY)],
        out_specs=tuple([_HBM] * (2 * na)),
        input_output_aliases={i: i for i in range(2 * na)},
        compiler_params=pltpu.CompilerParams(has_side_effects=_EFFECT),
    )(*srcs, *lands, *sems, after)
    return out[na:]


def _sum_blocks(g, name):
    _, r, n = g.shape
    tr = 256 if r % 256 == 0 else r

    def body(g_ref, o_ref):
        acc = g_ref[0].astype(F32)
        for d in range(1, N_DEV):
            acc = acc + g_ref[d].astype(F32)
        o_ref[...] = acc

    return pl.pallas_call(
        body, name=name, grid=(r // tr,),
        out_shape=jax.ShapeDtypeStruct((1, r, n), F32),
        in_specs=[pl.BlockSpec((N_DEV, tr, n), lambda i: (0, i, 0))],
        out_specs=pl.BlockSpec((None, tr, n), lambda i: (0, i, 0)),
        compiler_params=_cparams(("arbitrary",)),
    )(g)


def _ada_fwd(c_all, ada_w, ada_b_cols):
    nb, ncol = c_all.shape[0], ada_w.shape[1]

    def body(c_ref, w_ref, b_ref, mod_ref, cond_ref):
        cond = _silu(c_ref[...])
        cond_ref[...] = cond
        mod_ref[...] = _dot_hi(cond, w_ref[...]) + b_ref[...]

    return pl.pallas_call(
        body, name="ada_fwd",
        out_shape=(jax.ShapeDtypeStruct((nb, ncol), F32), jax.ShapeDtypeStruct((nb, D), F32)),
        compiler_params=_cparams(),
    )(c_all, ada_w, ada_b_cols)


def _ada_bwd(cond_all, dmod_all, dmod_cols, smalls):
    ncol, nsm = dmod_cols.shape[1], smalls.shape[1]

    def body(cond_ref, dm_ref, dmc_ref, sm_ref, gw_ref, gb_ref, gs_ref):
        gw_ref[...] = lax.dot_general(cond_ref[...], dmc_ref[...], (((0,), (0,)), ((), ())),
                                      preferred_element_type=F32, precision=HI)
        gb_ref[...] = jnp.sum(dm_ref[...], axis=0, keepdims=True)
        gs_ref[...] = jnp.sum(sm_ref[...], axis=0, keepdims=True)

    return pl.pallas_call(
        body, name="ada_bwd",
        out_shape=(jax.ShapeDtypeStruct((D, ncol), F32), jax.ShapeDtypeStruct((1, 6 * D), F32),
                   jax.ShapeDtypeStruct((1, nsm), F32)),
        compiler_params=_cparams(),
    )(cond_all, dmod_all, dmod_cols, smalls)


IN_CUTS = (0, QW, QW + 2 * KVW, QW + 2 * KVW + CONVW, QW + 2 * KVW + CONVW + 2 * DH,
           QW + 2 * KVW + CONVW + 2 * DH + DNW, QW + 2 * KVW + CONVW + 2 * DH + DNW + D, IN_W)
IN_WIDTHS = tuple(b - a for a, b in zip(IN_CUTS[:-1], IN_CUTS[1:]))
IN_SHARD = IN_W // N_DEV


def _inproj_fwd(x, mod, g1, w_blk):
    B, S, _ = x.shape
    tm = _tile(S)

    def body(x_ref, mod_ref, g_ref, w_ref, h_ref, *o_refs):
        h = _rms_mod(x_ref[...], g_ref[...], mod_ref[1:2, :], mod_ref[0:1, :]).astype(BF16)
        h_ref[...] = h
        full = jnp.concatenate([jnp.dot(h, w_ref[j], preferred_element_type=F32) for j in range(N_DEV)], axis=1)
        for o_ref, lo, hi in zip(o_refs, IN_CUTS[:-1], IN_CUTS[1:]):
            o_ref[...] = full[:, lo:hi]

    return pl.pallas_call(
        body, name="inproj_fwd", grid=(B, S // tm),
        out_shape=[jax.ShapeDtypeStruct((B, S, D), BF16)] + [jax.ShapeDtypeStruct((B, S, w), F32) for w in IN_WIDTHS],
        in_specs=[_rows(tm, D), _perb(6, D), _full((1, D)), _resident(w_blk.shape)],
        out_specs=[_rows(tm, D)] + [_rows(tm, w) for w in IN_WIDTHS],
        compiler_params=_cparams(("parallel", "arbitrary")),
    )(x, mod, g1, w_blk)


def _inproj_bwd(x, mod, g1, dx1, dps, w_blk):
    B, S, _ = x.shape
    tm = _tile(S)
    n = len(dps)

    def body(x_ref, mod_ref, g_ref, dx1_ref, *refs):
        dp_refs, w_ref = refs[:n], refs[n]
        dblk_ref, gx_ref, dg_ref, dsc_ref, dsh_ref = refs[n + 1:]
        b, i = pl.program_id(0), pl.program_id(1)
        full = jnp.concatenate([r[...].astype(F32) for r in dp_refs], axis=1)
        dh = None
        for j in range(N_DEV):
            blk = full[:, IN_SHARD * j:IN_SHARD * (j + 1)].astype(BF16)
            dblk_ref[j] = blk
            t = _dot_nt(blk, w_ref[j])
            dh = t if dh is None else dh + t
        _, vjp = jax.vjp(_rms_mod, x_ref[...], g_ref[...], mod_ref[1:2, :], mod_ref[0:1, :])
        dx, dg, dsc, dsh = vjp(dh)
        gx_ref[...] = dx1_ref[...] + dx

        @pl.when((b == 0) & (i == 0))
        def _():
            dg_ref[...] = jnp.zeros_like(dg_ref)

        @pl.when(i == 0)
        def _():
            dsc_ref[...] = jnp.zeros_like(dsc_ref)
            dsh_ref[...] = jnp.zeros_like(dsh_ref)

        dg_ref[...] += dg
        dsc_ref[...] += dsc
        dsh_ref[...] += dsh

    return pl.pallas_call(
        body, name="inproj_bwd", grid=(B, S // tm),
        out_shape=[jax.ShapeDtypeStruct((B, N_DEV, S, IN_SHARD), BF16), jax.ShapeDtypeStruct((B, S, D), F32),
                   jax.ShapeDtypeStruct((1, D), F32), jax.ShapeDtypeStruct((B, 1, D), F32),
                   jax.ShapeDtypeStruct((B, 1, D), F32)],
        in_specs=[_rows(tm, D), _perb(6, D), _full((1, D)), _rows(tm, D)]
                 + [_rows(tm, w) for w in IN_WIDTHS] + [_resident(w_blk.shape)],
        out_specs=[pl.BlockSpec((None, N_DEV, tm, IN_SHARD), lambda b, i: (b, 0, i, 0)), _rows(tm, D),
                   _full((1, D)), _perb(1, D), _perb(1, D)],
        compiler_params=_cparams(("arbitrary", "arbitrary")),
    )(x, mod, g1, dx1, *dps, w_blk)


def _wgrad(a, b, name, after=None):
    B, na, S, K = a.shape
    nb, N = b.shape[1], b.shape[3]
    G = max(na, nb)
    tm = min(512, S)
    nt = S // tm
    last = B * nt - 1

    def body(a_ref, b_ref, *rest):
        o_ref, acc = rest[-2:]
        t = pl.program_id(1)

        @pl.when(t == 0)
        def _():
            acc[...] = jnp.zeros_like(acc)

        acc[...] += lax.dot_general(a_ref[...], b_ref[...], (((0,), (0,)), ((), ())), preferred_element_type=F32)

        @pl.when(t == last)
        def _():
            o_ref[...] = acc[...].astype(BF16)

    return pl.pallas_call(
        body, name=name, grid=(G, B * nt),
        out_shape=jax.ShapeDtypeStruct((G, K, N), BF16),
        in_specs=[pl.BlockSpec((None, None, tm, K), lambda g, t: (t // nt, g if na > 1 else 0, t % nt, 0)),
                  pl.BlockSpec((None, None, tm, N), lambda g, t: (t // nt, g if nb > 1 else 0, t % nt, 0))]
                 + ([] if after is None else [pl.BlockSpec(memory_space=pl.ANY)]),
        out_specs=pl.BlockSpec((None, K, N), lambda g, t: (g, 0, 0)),
        scratch_shapes=[pltpu.VMEM((K, N), F32)],
        compiler_params=_cparams(("parallel", "arbitrary")),
    )(*((a, b) if after is None else (a, b, after)))


LANES = 128


def _attn_consts():
    inv_freq = THETA ** (-jnp.arange(0, ROT, 2, dtype=F32) / ROT)
    head = jnp.concatenate([inv_freq, inv_freq, jnp.zeros((HD - ROT,), F32)])
    invf = jnp.tile(head, LANES // HD)[None, :]
    mean_of = lambda w: jnp.asarray(np.kron(np.eye(w // HD), np.full((HD, HD), 1.0 / HD)), BF16)
    return invf, mean_of(QW), mean_of(KVW)


def _rope_tables(pos, invf):
    B, S, _ = pos.shape
    tr = min(1024, S)

    def body(p_ref, f_ref, c_ref, s_ref):
        ang = p_ref[...].astype(F32) * f_ref[...]
        c_ref[...] = jnp.cos(ang)
        s_ref[...] = jnp.sin(ang)

    sd = jax.ShapeDtypeStruct((B, S, LANES), F32)
    return pl.pallas_call(
        body, name="rope_tables", grid=(B, S // tr), out_shape=[sd, sd],
        in_specs=[_rows(tr, 1), _full((1, LANES))], out_specs=[_rows(tr, LANES), _rows(tr, LANES)],
        compiler_params=_cparams(("parallel", "parallel")),
    )(pos, invf)


def _rope_expand(cos, sin, reps):
    lane = lax.broadcasted_iota(jnp.int32, cos.shape, 1) % HD
    sa = jnp.where((lane >= ROT // 2) & (lane < ROT), sin, 0.0)
    sb = jnp.where(lane < ROT // 2, -sin, 0.0)
    rep = lambda t: jnp.concatenate([t] * reps, axis=1) if reps > 1 else t
    return rep(cos), rep(sa), rep(sb)


@jax.custom_vjp
def _rope(t, cos, sa, sb):
    w = t.shape[1]
    return t * cos + pltpu.roll(t, ROT // 2, 1) * sa + pltpu.roll(t, w - ROT // 2, 1) * sb


def _rope_fwd(t, cos, sa, sb):
    return _rope(t, cos, sa, sb), (cos, sa, sb)


def _rope_bwd(res, d):
    cos, sa, sb = res
    w = d.shape[1]
    dt = d * cos + pltpu.roll(d * sa, w - ROT // 2, 1) + pltpu.roll(d * sb, ROT // 2, 1)
    return dt, jnp.zeros_like(cos), jnp.zeros_like(sa), jnp.zeros_like(sb)


_rope.defvjp(_rope_fwd, _rope_bwd)


def _head_norm(t, g, mean_of):
    hi, lo = _split(t * t)
    ms = jnp.dot(hi, mean_of, preferred_element_type=F32) + jnp.dot(lo, mean_of, preferred_element_type=F32)
    return t * lax.rsqrt(ms + EPS) * g


def _attn_block(q, kvp, kvc, qg, kg, sinks, tq, tk, mq, mk, valid):
    qn = _rope(_head_norm(q, jnp.concatenate([qg] * HQ, axis=1), mq), *tq) * (HD ** -0.5)
    kv = jnp.concatenate([kvp, kvc], axis=0)
    kn = _rope(_head_norm(kv[:, 0:KVW], jnp.concatenate([kg] * HKV, axis=1), mk), *tk)
    q4 = jnp.stack([jnp.concatenate([qn[:, HD * (GRP * j + i):HD * (GRP * j + i + 1)] for i in range(GRP)], axis=0)
                    for j in range(HKV)])
    k2 = jnp.stack([kn[:, HD * j:HD * (j + 1)] for j in range(HKV)])
    v2 = jnp.stack([kv[:, KVW + HD * j:KVW + HD * (j + 1)] for j in range(HKV)])
    rowblk = lax.broadcasted_iota(jnp.int32, (GRP * BLK, 1), 0) // BLK
    sink = jnp.stack([sum(jnp.where(rowblk == i, sinks[:, GRP * j + i:GRP * j + i + 1], 0.0) for i in range(GRP))
                      for j in range(HKV)])
    s = jnp.where(valid[None], _bmm(q4, k2, _BMM_NT), -1e30)
    m = lax.stop_gradient(jnp.maximum(jnp.max(s, axis=-1, keepdims=True), sink))
    p = jnp.exp(s - m)
    probs = p * (1.0 / (jnp.sum(p, axis=-1, keepdims=True) + jnp.exp(sink - m)))
    o4 = _bmm(probs, v2)
    return jnp.concatenate([o4[j, BLK * i:BLK * (i + 1), :] for j in range(HKV) for i in range(GRP)], axis=1)


def _attn_tables(cp_ref, cc_ref, sp_ref, sc_ref, n):
    tq = _rope_expand(cc_ref[...], sc_ref[...], QW // LANES)
    tk = _rope_expand(jnp.concatenate([cp_ref[...], cc_ref[...]], axis=0),
                      jnp.concatenate([sp_ref[...], sc_ref[...]], axis=0), KVW // LANES)
    qi = lax.broadcasted_iota(jnp.int32, (GRP * BLK, 2 * BLK), 0) % BLK + BLK
    kj = lax.broadcasted_iota(jnp.int32, (GRP * BLK, 2 * BLK), 1)
    dist = qi - kj
    valid = (dist >= 0) & (dist < BLK) & ((kj >= BLK) | (n > 0))
    return tq, tk, valid


def _attn_fwd(aq, akv, cos, sin, qg, kg, sinks, mq, mk):
    B, S, _ = aq.shape
    nb = S // BLK

    def body(q_ref, kvp_ref, kvc_ref, cp_ref, cc_ref, sp_ref, sc_ref, qg_ref, kg_ref, sk_ref, mq_ref, mk_ref, o_ref):
        tq, tk, valid = _attn_tables(cp_ref, cc_ref, sp_ref, sc_ref, pl.program_id(1))
        o_ref[...] = _attn_block(q_ref[...], kvp_ref[...], kvc_ref[...], qg_ref[...], kg_ref[...], sk_ref[...],
                                 tq, tk, mq_ref[...], mk_ref[...], valid)

    prev = lambda b, n: (b, jnp.maximum(n - 1, 0), 0)
    cur = lambda b, n: (b, n, 0)
    return pl.pallas_call(
        body, name="attn_fwd", grid=(B, nb),
        out_shape=jax.ShapeDtypeStruct((B, S, QW), F32),
        in_specs=[pl.BlockSpec((None, BLK, QW), cur), pl.BlockSpec((None, BLK, 2 * KVW), prev),
                  pl.BlockSpec((None, BLK, 2 * KVW), cur), pl.BlockSpec((None, BLK, LANES), prev),
                  pl.BlockSpec((None, BLK, LANES), cur), pl.BlockSpec((None, BLK, LANES), prev),
                  pl.BlockSpec((None, BLK, LANES), cur), _full((1, HD)), _full((1, HD)), _full((1, HQ)),
                  _full((QW, QW)), _full((KVW, KVW))],
        out_specs=pl.BlockSpec((None, BLK, QW), cur),
        compiler_params=_cparams(("parallel", "arbitrary")),
    )(aq, akv, akv, cos, cos, sin, sin, qg, kg, sinks, mq, mk)


def _attn_bwd(aq, akv, cos, sin, qg, kg, sinks, mq, mk, do):
    B, S, _ = aq.shape
    nb = S // BLK

    def body(q_ref, kvp_ref, kvc_ref, cp_ref, cc_ref, sp_ref, sc_ref, qg_ref, kg_ref, sk_ref, mq_ref, mk_ref, do_ref,
             dq_ref, dkv_ref, dqg_ref, dkg_ref, dsk_ref, carry):
        b, i = pl.program_id(0), pl.program_id(1)
        tq, tk, valid = _attn_tables(cp_ref, cc_ref, sp_ref, sc_ref, nb - 1 - i)
        fn = functools.partial(_attn_block, tq=tq, tk=tk, mq=mq_ref[...], mk=mk_ref[...], valid=valid)
        _, vjp = jax.vjp(fn, q_ref[...], kvp_ref[...], kvc_ref[...], qg_ref[...], kg_ref[...], sk_ref[...])
        dq, dkvp, dkvc, dqg, dkg, dsk = vjp(do_ref[...])

        @pl.when(i == 0)
        def _():
            carry[...] = jnp.zeros_like(carry)

        @pl.when((b == 0) & (i == 0))
        def _():
            dqg_ref[...] = jnp.zeros_like(dqg_ref)
            dkg_ref[...] = jnp.zeros_like(dkg_ref)
            dsk_ref[...] = jnp.zeros_like(dsk_ref)

        dq_ref[...] = dq.astype(BF16)
        dkv_ref[...] = (dkvc + carry[...]).astype(BF16)
        carry[...] = dkvp
        dqg_ref[...] += dqg
        dkg_ref[...] += dkg
        dsk_ref[...] += dsk

    prev = lambda b, i: (b, jnp.maximum(nb - 2 - i, 0), 0)
    cur = lambda b, i: (b, nb - 1 - i, 0)
    return pl.pallas_call(
        body, name="attn_bwd", grid=(B, nb),
        out_shape=[jax.ShapeDtypeStruct((B, S, QW), BF16), jax.ShapeDtypeStruct((B, S, 2 * KVW), BF16),
                   jax.ShapeDtypeStruct((1, HD), F32), jax.ShapeDtypeStruct((1, HD), F32),
                   jax.ShapeDtypeStruct((1, HQ), F32)],
        in_specs=[pl.BlockSpec((None, BLK, QW), cur), pl.BlockSpec((None, BLK, 2 * KVW), prev),
                  pl.BlockSpec((None, BLK, 2 * KVW), cur), pl.BlockSpec((None, BLK, LANES), prev),
                  pl.BlockSpec((None, BLK, LANES), cur), pl.BlockSpec((None, BLK, LANES), prev),
                  pl.BlockSpec((None, BLK, LANES), cur), _full((1, HD)), _full((1, HD)), _full((1, HQ)),
                  _full((QW, QW)), _full((KVW, KVW)), pl.BlockSpec((None, BLK, QW), cur)],
        out_specs=[pl.BlockSpec((None, BLK, QW), cur), pl.BlockSpec((None, BLK, 2 * KVW), cur),
                   _full((1, HD)), _full((1, HD)), _full((1, HQ))],
        scratch_shapes=[pltpu.VMEM((BLK, 2 * KVW), F32)],
        compiler_params=_cparams(("arbitrary", "arbitrary")),
    )(aq, akv, akv, cos, cos, sin, sin, qg, kg, sinks, mq, mk, do)


def _conv_taps(xe, w, rows):
    y = None
    for j in range(CONV):
        sh = pltpu.roll(xe, CONV - 1 - j, 0)[8:8 + rows, :] if j < CONV - 1 else xe[8:8 + rows, :]
        y = sh * w[j:j + 1, :] if y is None else y + sh * w[j:j + 1, :]
    return y


def _conv_fwd(xin, w):
    B, S, C = xin.shape
    tc = min(512, S)
    r8 = tc // 8

    def body(xp_ref, x_ref, w_ref, o_ref):
        i = pl.program_id(1)
        xp = jnp.where(i > 0, xp_ref[...], 0.0)
        xe = jnp.concatenate([xp, x_ref[...]], axis=0)
        o_ref[...] = _silu(_conv_taps(xe, w_ref[...], tc))

    return pl.pallas_call(
        body, name="conv_fwd", grid=(B, S // tc),
        out_shape=jax.ShapeDtypeStruct((B, S, C), F32),
        in_specs=[pl.BlockSpec((None, 8, C), lambda b, i: (b, jnp.maximum(i * r8 - 1, 0), 0)),
                  _rows(tc, C), _full((CONV, C))],
        out_specs=_rows(tc, C),
        compiler_params=_cparams(("parallel", "arbitrary")),
    )(xin, xin, w)


def _conv_bwd(xin, w, dy):
    B, S, C = xin.shape
    tc = min(512, S)
    r8 = tc // 8
    nt = S // tc

    def body(xp_ref, x_ref, xn_ref, dy_ref, dyn_ref, w_ref, dx_ref, dw_ref):
        b, i = pl.program_id(0), pl.program_id(1)
        w = w_ref[...]
        xp = jnp.where(i > 0, xp_ref[...], 0.0)
        xe = jnp.concatenate([xp, x_ref[...], xn_ref[...]], axis=0)
        pre = _conv_taps(xe, w, tc + 8)
        sg = _sigmoid(pre)
        dyn = jnp.where(i < nt - 1, dyn_ref[...], 0.0)
        dpre = jnp.concatenate([dy_ref[...], dyn], axis=0) * (sg * (1.0 + pre * (1.0 - sg)))
        dx = dpre[0:tc, :] * w[CONV - 1:CONV, :]
        for j in range(CONV - 1):
            dx = dx + pltpu.roll(dpre, tc + 8 - (CONV - 1 - j), 0)[0:tc, :] * w[j:j + 1, :]
        dx_ref[...] = dx.astype(BF16)
        dcur = dpre[0:tc, :]
        xe0 = xe[0:8 + tc, :]
        lane_row = lax.broadcasted_iota(jnp.int32, (CONV, C), 0)
        dw = jnp.zeros((CONV, C), F32)
        for j in range(CONV):
            sh = pltpu.roll(xe0, CONV - 1 - j, 0)[8:8 + tc, :] if j < CONV - 1 else xe0[8:8 + tc, :]
            dw = dw + jnp.where(lane_row == j, jnp.sum(sh * dcur, axis=0, keepdims=True), 0.0)

        @pl.when((b == 0) & (i == 0))
        def _():
            dw_ref[...] = jnp.zeros_like(dw_ref)

        dw_ref[...] += dw

    return pl.pallas_call(
        body, name="conv_bwd", grid=(B, nt),
        out_shape=[jax.ShapeDtypeStruct((B, S, C), BF16), jax.ShapeDtypeStruct((CONV, C), F32)],
        in_specs=[pl.BlockSpec((None, 8, C), lambda b, i: (b, jnp.maximum(i * r8 - 1, 0), 0)),
                  _rows(tc, C),
                  pl.BlockSpec((None, 8, C), lambda b, i: (b, jnp.minimum((i + 1) * r8, S // 8 - 1), 0)),
                  _rows(tc, C),
                  pl.BlockSpec((None, 8, C), lambda b, i: (b, jnp.minimum((i + 1) * r8, S // 8 - 1), 0)),
                  _full((CONV, C))],
        out_specs=[_rows(tc, C), _full((CONV, C))],
        compiler_params=_cparams(("arbitrary", "arbitrary")),
    )(xin, xin, xin, dy, dy, w)


def _softplus(x):
    return jnp.maximum(x, 0.0) + jnp.log1p(jnp.exp(-jnp.abs(x)))


_BMM = (((2,), (1,)), ((0,), (0,)))
_BMM_NT = (((2,), (2,)), ((0,), (0,)))
_BMM_TN = (((1,), (1,)), ((0,), (0,)))


def _bmm(a, b, dims=_BMM):
    return lax.dot_general(a.astype(BF16), b.astype(BF16), dims, preferred_element_type=F32)


def _split(a):
    hi = a.astype(BF16)
    return hi, (a - hi.astype(F32)).astype(BF16)


def _bmm3(a, b, dims=_BMM):
    ah, al = _split(a)
    bh, bl = _split(b)
    d = lambda p, q: lax.dot_general(p, q, dims, preferred_element_type=F32)
    return d(ah, bh) + (d(ah, bl) + d(al, bh))


def _tri_inverse(L):
    eye = (lax.broadcasted_iota(jnp.int32, (CH, CH), 0) == lax.broadcasted_iota(jnp.int32, (CH, CH), 1)).astype(F32)
    T = eye - L
    P = L
    n = 2
    while n < CH:
        P = _bmm3(P, P)
        T = T + _bmm3(T, P)
        n *= 2
    return T


@jax.custom_vjp
def _tri_inverse_known(L, T):
    return T


def _tri_inverse_known_fwd(L, T):
    return T, T


def _tri_inverse_known_bwd(T, dT):
    return -_bmm3(T, _bmm3(dT, T, _BMM_NT), _BMM_TN), jnp.zeros_like(T)


_tri_inverse_known.defvjp(_tri_inverse_known_fwd, _tri_inverse_known_bwd)


def _cumsum_rows(g):
    n = g.shape[0]
    ii = lax.broadcasted_iota(jnp.int32, (n, CH, CH), 1)
    jj = lax.broadcasted_iota(jnp.int32, (n, CH, CH), 2)
    tri = (ii >= jj).astype(BF16)
    g0 = g.astype(BF16)
    r1 = g - g0.astype(F32)
    g1 = r1.astype(BF16)
    g2 = (r1 - g1.astype(F32)).astype(BF16)
    d = lambda q: lax.dot_general(tri, q, _BMM, preferred_element_type=F32)
    return d(g0) + (d(g1) + d(g2))


def _dn_prep(t_known, qr, kr, v, a_raw, b_raw, a_log, dt_b):
    n = qr.shape[0]
    ii = lax.broadcasted_iota(jnp.int32, (n, CH, CH), 1)
    jj = lax.broadcasted_iota(jnp.int32, (n, CH, CH), 2)
    incl, strict = ii >= jj, ii > jj
    q = qr * lax.rsqrt(jnp.sum(qr * qr, axis=-1, keepdims=True) + EPS) * (DK ** -0.5)
    k = kr * lax.rsqrt(jnp.sum(kr * kr, axis=-1, keepdims=True) + EPS)
    beta = _sigmoid(b_raw)
    g = -jnp.exp(a_log) * _softplus(a_raw + dt_b)
    gcb = _cumsum_rows(jnp.broadcast_to(g, (n, CH, DK)))
    gc = gcb[:, :, 0:1]
    gc_row = jnp.swapaxes(gcb, 1, 2)[:, 0:1, 0:CH]
    decay = jnp.where(incl, jnp.exp(jnp.where(incl, gc - gc_row, 0.0)), 0.0)
    kb = k * beta
    L = jnp.where(strict, _bmm(kb, k, _BMM_NT) * decay, 0.0)
    T = _tri_inverse(L) if t_known is None else _tri_inverse_known(L, t_known)
    eg = jnp.exp(gc)
    u = _bmm(T, v * beta)
    w = _bmm(T, kb * eg)
    a_in = _bmm(q, k, _BMM_NT) * decay
    g_last = gc[:, CH - 1:CH, :]
    return u, w, q * eg, k * jnp.exp(g_last - gc), a_in, jnp.exp(g_last), T


def _dn_step(S0, u, w, qd, kd, a_in, cd):
    r = _bmm(jnp.concatenate([w, qd], axis=1), S0)
    v_new = u - r[:, 0:CH, :]
    o = r[:, CH:2 * CH, :] + _bmm(a_in, v_new)
    S1 = S0 * cd + _bmm(kd, v_new, _BMM_TN)
    return o, S1


def _dn_stack(cq, ba, al, dt, G):
    cols = [[] for _ in range(7)]
    for c in range(G):
        rows = slice(CH * c, CH * (c + 1))
        for h in range(DH):
            parts = (cq[rows, DK * h:DK * (h + 1)], cq[rows, DNW + DK * h:DNW + DK * (h + 1)],
                     cq[rows, 2 * DNW + DK * h:2 * DNW + DK * (h + 1)], ba[rows, DH + h:DH + h + 1],
                     ba[rows, h:h + 1], al[:, h:h + 1], dt[:, h:h + 1])
            for col, p in zip(cols, parts):
                col.append(p)
    return tuple(jnp.stack(col) for col in cols)


def _dn_group(S, want):
    g = want
    while (S // CH) % g:
        g //= 2
    return g


def _dn_prep_fwd(cq, ba, a_log, dt_b):
    B, S, _ = cq.shape
    nc = S // CH
    G = _dn_group(S, 4)

    def body(cq_ref, ba_ref, al_ref, dt_ref, u_ref, w_ref, qd_ref, kd_ref, a_ref, t_ref, cd_ref):
        ops = _dn_stack(cq_ref[...], ba_ref[...], al_ref[...], dt_ref[...], G)
        u, w, qd, kd, a_in, cd, T = _dn_prep(None, *ops)
        lane4 = lax.broadcasted_iota(jnp.int32, (1, DH), 1)
        for c in range(G):
            rows = slice(CH * c, CH * (c + 1))
            cdrow = jnp.zeros((1, DH), F32)
            for h in range(DH):
                n = DH * c + h
                lanes = slice(DK * h, DK * (h + 1))
                u_ref[rows, lanes] = u[n]
                w_ref[rows, lanes] = w[n]
                qd_ref[rows, lanes] = qd[n]
                kd_ref[rows, lanes] = kd[n]
                a_ref[rows, CH * h:CH * (h + 1)] = a_in[n]
                t_ref[rows, CH * h:CH * (h + 1)] = T[n]
                cdrow = cdrow + jnp.where(lane4 == h, cd[n], 0.0)
            cd_ref[c] = cdrow

    wide = jax.ShapeDtypeStruct((B, S, DNW), F32)
    sq = jax.ShapeDtypeStruct((B, S, DH * CH), F32)
    return pl.pallas_call(
        body, name="dn_prep_fwd", grid=(B, nc // G),
        out_shape=[wide, wide, wide, wide, sq, sq, jax.ShapeDtypeStruct((B, nc, 1, DH), F32)],
        in_specs=[_rows(G * CH, CONVW), _rows(G * CH, 2 * DH), _full((1, DH)), _full((1, DH))],
        out_specs=[_rows(G * CH, DNW)] * 4 + [_rows(G * CH, DH * CH)] * 2
                  + [pl.BlockSpec((None, G, 1, DH), lambda b, i: (b, i, 0, 0))],
        compiler_params=_cparams(("parallel", "parallel")),
    )(cq, ba, a_log, dt_b)


def _dn_seq_specs(B, nc, rev):
    at = (lambda i: nc - 1 - i) if rev else (lambda i: i)
    wide = pl.BlockSpec((B, CH, DNW), lambda i: (0, at(i), 0))
    a_spec = pl.BlockSpec((B, CH, DH * CH), lambda i: (0, at(i), 0))
    cd_spec = pl.BlockSpec((B, None, 1, DH), lambda i: (0, at(i), 0, 0))
    st_spec = pl.BlockSpec((B, None, DH, DK, DK), lambda i: (0, at(i), 0, 0, 0))
    return wide, a_spec, cd_spec, st_spec


def _dn_step_operands(B, u_ref, w_ref, qd_ref, kd_ref, a_ref, cd_ref):
    pairs = [(b, h) for b in range(B) for h in range(DH)]
    wide = lambda ref: jnp.stack([ref[b, :, DK * h:DK * (h + 1)] for b, h in pairs])
    a_in = jnp.stack([a_ref[b, :, CH * h:CH * (h + 1)] for b, h in pairs])
    cd = jnp.stack([cd_ref[b, :, h:h + 1] for b, h in pairs])
    return wide(u_ref), wide(w_ref), wide(qd_ref), wide(kd_ref), a_in, cd


def _dn_seq_fwd(u, w, qd, kd, a_in, cd):
    B, S, _ = u.shape
    nc = S // CH

    def body(u_ref, w_ref, qd_ref, kd_ref, a_ref, cd_ref, o_ref, st_ref, state):
        @pl.when(pl.program_id(0) == 0)
        def _():
            state[...] = jnp.zeros_like(state)

        S0 = state[...]
        for b in range(B):
            st_ref[b] = S0[DH * b:DH * (b + 1)]
        o, S1 = _dn_step(S0, *_dn_step_operands(B, u_ref, w_ref, qd_ref, kd_ref, a_ref, cd_ref))
        state[...] = S1
        for b in range(B):
            for h in range(DH):
                o_ref[b, :, DK * h:DK * (h + 1)] = o[DH * b + h]

    wide, a_spec, cd_spec, st_spec = _dn_seq_specs(B, nc, False)
    return pl.pallas_call(
        body, name="dn_seq_fwd", grid=(nc,),
        out_shape=[jax.ShapeDtypeStruct((B, S, DNW), F32), jax.ShapeDtypeStruct((B, nc, DH, DK, DK), F32)],
        in_specs=[wide, wide, wide, wide, a_spec, cd_spec],
        out_specs=[wide, st_spec],
        scratch_shapes=[pltpu.VMEM((B * DH, DK, DK), F32)],
        compiler_params=_cparams(("arbitrary",)),
    )(u, w, qd, kd, a_in, cd)


def _dn_seq_bwd(u, w, qd, kd, a_in, cd, states, do):
    B, S, _ = u.shape
    nc = S // CH

    def body(u_ref, w_ref, qd_ref, kd_ref, a_ref, cd_ref, st_ref, do_ref,
             du_ref, dw_ref, dqd_ref, dkd_ref, da_ref, dcd_ref, dstate):
        @pl.when(pl.program_id(0) == 0)
        def _():
            dstate[...] = jnp.zeros_like(dstate)

        lane4 = lax.broadcasted_iota(jnp.int32, (1, DH), 1)
        S0 = jnp.concatenate([st_ref[b] for b in range(B)], axis=0)
        do = jnp.stack([do_ref[b, :, DK * h:DK * (h + 1)] for b in range(B) for h in range(DH)])
        _, vjp = jax.vjp(_dn_step, S0, *_dn_step_operands(B, u_ref, w_ref, qd_ref, kd_ref, a_ref, cd_ref))
        dS, du, dw, dqd, dkd, da, dcd = vjp((do, dstate[...]))
        dstate[...] = dS
        for b in range(B):
            dcdrow = jnp.zeros((1, DH), F32)
            for h in range(DH):
                n = DH * b + h
                lanes = slice(DK * h, DK * (h + 1))
                du_ref[b, :, lanes] = du[n]
                dw_ref[b, :, lanes] = dw[n]
                dqd_ref[b, :, lanes] = dqd[n]
                dkd_ref[b, :, lanes] = dkd[n]
                da_ref[b, :, CH * h:CH * (h + 1)] = da[n]
                dcdrow = dcdrow + jnp.where(lane4 == h, dcd[n], 0.0)
            dcd_ref[b] = dcdrow

    wide, a_spec, cd_spec, st_spec = _dn_seq_specs(B, nc, True)
    sd = jax.ShapeDtypeStruct((B, S, DNW), F32)
    return pl.pallas_call(
        body, name="dn_seq_bwd", grid=(nc,),
        out_shape=[sd, sd, sd, sd, jax.ShapeDtypeStruct((B, S, DH * CH), F32), jax.ShapeDtypeStruct((B, nc, 1, DH), F32)],
        in_specs=[wide, wide, wide, wide, a_spec, cd_spec, st_spec, wide],
        out_specs=[wide, wide, wide, wide, a_spec, cd_spec],
        scratch_shapes=[pltpu.VMEM((B * DH, DK, DK), F32)],
        compiler_params=_cparams(("arbitrary",)),
    )(u, w, qd, kd, a_in, cd, states, do)


def _dn_prep_bwd(cq, ba, a_log, dt_b, t_inv, du, dw, dqd, dkd, da, dcd):
    B, S, _ = cq.shape
    nc = S // CH
    G = _dn_group(S, 4)

    def body(cq_ref, ba_ref, al_ref, dt_ref, t_ref, du_ref, dw_ref, dqd_ref, dkd_ref, da_ref, dcd_ref,
             dcq_ref, dba_ref, dal_ref, ddt_ref):
        @pl.when((pl.program_id(0) == 0) & (pl.program_id(1) == 0))
        def _():
            dal_ref[...] = jnp.zeros_like(dal_ref)
            ddt_ref[...] = jnp.zeros_like(ddt_ref)

        pairs = [(c, h) for c in range(G) for h in range(DH)]
        rows = lambda c: slice(CH * c, CH * (c + 1))
        wide = lambda ref: jnp.stack([ref[rows(c), DK * h:DK * (h + 1)] for c, h in pairs])
        square = lambda ref: jnp.stack([ref[rows(c), CH * h:CH * (h + 1)] for c, h in pairs])
        ops = _dn_stack(cq_ref[...], ba_ref[...], al_ref[...], dt_ref[...], G)
        cots = (wide(du_ref), wide(dw_ref), wide(dqd_ref), wide(dkd_ref), square(da_ref),
                jnp.stack([dcd_ref[c][:, h:h + 1] for c, h in pairs]), jnp.zeros((len(pairs), CH, CH), F32))
        _, vjp = jax.vjp(functools.partial(_dn_prep, square(t_ref)), *ops)
        dq, dk, dv, dar, dbr, dl, dd = vjp(cots)
        lane8 = lax.broadcasted_iota(jnp.int32, (CH, 2 * DH), 1)
        lane4 = lax.broadcasted_iota(jnp.int32, (1, DH), 1)
        dal = jnp.zeros((1, DH), F32)
        ddt = jnp.zeros((1, DH), F32)
        for c in range(G):
            dba = jnp.zeros((CH, 2 * DH), F32)
            for h in range(DH):
                n = DH * c + h
                dcq_ref[rows(c), DK * h:DK * (h + 1)] = dq[n]
                dcq_ref[rows(c), DNW + DK * h:DNW + DK * (h + 1)] = dk[n]
                dcq_ref[rows(c), 2 * DNW + DK * h:2 * DNW + DK * (h + 1)] = dv[n]
                dba = dba + jnp.where(lane8 == h, dbr[n], 0.0) + jnp.where(lane8 == DH + h, dar[n], 0.0)
                dal = dal + jnp.where(lane4 == h, dl[n], 0.0)
                ddt = ddt + jnp.where(lane4 == h, dd[n], 0.0)
            dba_ref[rows(c), :] = dba.astype(BF16)
        dal_ref[...] += dal
        ddt_ref[...] += ddt

    return pl.pallas_call(
        body, name="dn_prep_bwd", grid=(B, nc // G),
        out_shape=[jax.ShapeDtypeStruct((B, S, CONVW), F32), jax.ShapeDtypeStruct((B, S, 2 * DH), BF16),
                   jax.ShapeDtypeStruct((1, DH), F32), jax.ShapeDtypeStruct((1, DH), F32)],
        in_specs=[_rows(G * CH, CONVW), _rows(G * CH, 2 * DH), _full((1, DH)), _full((1, DH)), _rows(G * CH, DH * CH)]
                 + [_rows(G * CH, DNW)] * 4 + [_rows(G * CH, DH * CH),
                                               pl.BlockSpec((None, G, 1, DH), lambda b, i: (b, i, 0, 0))],
        out_specs=[_rows(G * CH, CONVW), _rows(G * CH, 2 * DH), _full((1, DH)), _full((1, DH))],
        compiler_params=_cparams(("arbitrary", "arbitrary")),
    )(cq, ba, a_log, dt_b, t_inv, du, dw, dqd, dkd, da, dcd)


def _gated_norm(o, z, g):
    outs = []
    for h in range(DH):
        t = o[:, DK * h:DK * (h + 1)]
        r = lax.rsqrt(jnp.mean(t * t, axis=-1, keepdims=True) + EPS)
        outs.append(t * r * g * _silu(z[:, DK * h:DK * (h + 1)]))
    return jnp.concatenate(outs, axis=1)


def _mix_fwd(x, o_attn, o_dn, z, ga, gd, mod, dn_g, w_branch, w_out):
    B, S, _ = x.shape
    tm = _tile(S)

    def body(x_ref, oa_ref, od_ref, z_ref, ga_ref, gd_ref, mod_ref, g_ref, wb_ref, wo_ref,
             x1_ref, mix_ref, mg_ref, ob_ref):
        oa = oa_ref[...].astype(BF16)
        od = _gated_norm(od_ref[...], z_ref[...], g_ref[...]).astype(BF16)
        ob_ref[0] = oa
        ob_ref[1] = od
        ya = jnp.dot(oa, wb_ref[0:QW, :], preferred_element_type=F32)
        yd = jnp.dot(od, wb_ref[QW:QW + DNW, :], preferred_element_type=F32)
        merged = (_sigmoid(ga_ref[...]) * ya + _sigmoid(gd_ref[...]) * yd).astype(BF16)
        mg_ref[...] = merged
        mix = jnp.dot(merged, wo_ref[...], preferred_element_type=F32)
        mix_ref[...] = mix
        x1_ref[...] = x_ref[...] + mod_ref[2:3, :] * mix

    return pl.pallas_call(
        body, name="mix_fwd", grid=(B, S // tm),
        out_shape=[jax.ShapeDtypeStruct((B, S, D), F32), jax.ShapeDtypeStruct((B, S, D), F32),
                   jax.ShapeDtypeStruct((B, S, D), BF16), jax.ShapeDtypeStruct((B, 2, S, QW), BF16)],
        in_specs=[_rows(tm, D), _rows(tm, QW), _rows(tm, DNW), _rows(tm, DNW), _rows(tm, D), _rows(tm, D),
                  _perb(6, D), _full((1, DK)), _resident(w_branch.shape), _resident(w_out.shape)],
        out_specs=[_rows(tm, D), _rows(tm, D), _rows(tm, D), _stacked(2, tm, QW)],
        compiler_params=_cparams(("parallel", "arbitrary")),
    )(x, o_attn, o_dn, z, ga, gd, mod, dn_g, w_branch, w_out)


def _mix_bwd(dx1, mix, o_attn, o_dn, z, ga, gd, mod, dn_g, w_branch, w_out):
    B, S, _ = dx1.shape
    tm = _tile(S)

    def body(dx1_ref, mix_ref, oa_ref, od_ref, z_ref, ga_ref, gd_ref, mod_ref, g_ref, wb_ref, wo_ref,
             dmix_ref, dyo_ref, dga_ref, dgd_ref, dz_ref, doa_ref, dod_ref, dgate_ref, dg_ref):
        b, i = pl.program_id(0), pl.program_id(1)
        dx1 = dx1_ref[...]
        dmix = (dx1 * mod_ref[2:3, :]).astype(BF16)
        dmix_ref[...] = dmix
        dgate = jnp.sum(dx1 * mix_ref[...], axis=0, keepdims=True)
        dmerged = _dot_nt(dmix, wo_ref[...])
        odn, gn_vjp = jax.vjp(_gated_norm, od_ref[...], z_ref[...], g_ref[...])
        ya = _dot(oa_ref[...], wb_ref[0:QW, :])
        yd = _dot(odn, wb_ref[QW:QW + DNW, :])
        sa, sd = _sigmoid(ga_ref[...]), _sigmoid(gd_ref[...])
        dya = (dmerged * sa).astype(BF16)
        dyd = (dmerged * sd).astype(BF16)
        dyo_ref[0] = dya
        dyo_ref[1] = dyd
        dga_ref[...] = (dmerged * ya * sa * (1.0 - sa)).astype(BF16)
        dgd_ref[...] = (dmerged * yd * sd * (1.0 - sd)).astype(BF16)
        doa_ref[...] = _dot_nt(dya, wb_ref[0:QW, :])
        dodn = _dot_nt(dyd, wb_ref[QW:QW + DNW, :])
        dod, dz, dg = gn_vjp(dodn)
        dod_ref[...] = dod
        dz_ref[...] = dz.astype(BF16)

        @pl.when(i == 0)
        def _():
            dgate_ref[...] = jnp.zeros_like(dgate_ref)

        @pl.when((b == 0) & (i == 0))
        def _():
            dg_ref[...] = jnp.zeros_like(dg_ref)

        dgate_ref[...] += dgate
        dg_ref[...] += dg

    return pl.pallas_call(
        body, name="mix_bwd", grid=(B, S // tm),
        out_shape=[jax.ShapeDtypeStruct((B, S, D), BF16), jax.ShapeDtypeStruct((B, 2, S, D), BF16),
                   jax.ShapeDtypeStruct((B, S, D), BF16), jax.ShapeDtypeStruct((B, S, D), BF16),
                   jax.ShapeDtypeStruct((B, S, DNW), BF16),
                   jax.ShapeDtypeStruct((B, S, QW), F32), jax.ShapeDtypeStruct((B, S, DNW), F32),
                   jax.ShapeDtypeStruct((B, 1, D), F32), jax.ShapeDtypeStruct((1, DK), F32)],
        in_specs=[_rows(tm, D), _rows(tm, D), _rows(tm, QW), _rows(tm, DNW), _rows(tm, DNW), _rows(tm, D),
                  _rows(tm, D), _perb(6, D), _full((1, DK)), _resident(w_branch.shape), _resident(w_out.shape)],
        out_specs=[_rows(tm, D), _stacked(2, tm, D), _rows(tm, D), _rows(tm, D), _rows(tm, DNW),
                   _rows(tm, QW), _rows(tm, DNW), _perb(1, D), _full((1, DK))],
        compiler_params=_cparams(("arbitrary", "arbitrary")),
    )(dx1, mix, o_attn, o_dn, z, ga, gd, mod, dn_g, w_branch, w_out)


GU_SHARD = 2 * FFN // N_DEV
GU_HALF = N_DEV // 2


def _ffn1_fwd(x1, mod, g2, w_gu):
    B, S, _ = x1.shape
    tm = _tile(S)

    def body(x_ref, mod_ref, g_ref, w_ref, h_ref, gate_ref, up_ref, act_ref):
        h = _rms_mod(x_ref[...], g_ref[...], mod_ref[4:5, :], mod_ref[3:4, :]).astype(BF16)
        h_ref[...] = h
        for j in range(GU_HALF):
            gate = jnp.dot(h, w_ref[j], preferred_element_type=F32)
            up = jnp.dot(h, w_ref[GU_HALF + j], preferred_element_type=F32)
            gate_ref[j] = gate
            up_ref[j] = up
            act_ref[j] = (_silu(gate) * up).astype(BF16)

    blk = lambda dt: jax.ShapeDtypeStruct((B, GU_HALF, S, GU_SHARD), dt)
    return pl.pallas_call(
        body, name="ffn1_fwd", grid=(B, S // tm),
        out_shape=[jax.ShapeDtypeStruct((B, S, D), BF16), blk(F32), blk(F32), blk(BF16)],
        in_specs=[_rows(tm, D), _perb(6, D), _full((1, D)), _resident(w_gu.shape)],
        out_specs=[_rows(tm, D)] + [_stacked(GU_HALF, tm, GU_SHARD)] * 3,
        compiler_params=_cparams(("parallel", "arbitrary")),
    )(x1, mod, g2, w_gu)


def _ffn2_fwd(act, x1, target, mod, w_down):
    B, S, _ = x1.shape
    tm = _tile(S)

    def body(a_ref, x_ref, t_ref, mod_ref, w_ref, dy_ref, loss_ref, dgate_ref):
        b, i = pl.program_id(0), pl.program_id(1)
        y = jnp.dot(a_ref[0], w_ref[0], preferred_element_type=F32)
        for j in range(1, GU_HALF):
            y = y + jnp.dot(a_ref[j], w_ref[j], preferred_element_type=F32)
        err = x_ref[...] + mod_ref[5:6, :] * y - t_ref[...]
        dy = err * (1.0 / D)
        dy_ref[...] = dy

        @pl.when((b == 0) & (i == 0))
        def _():
            loss_ref[...] = jnp.zeros_like(loss_ref)

        @pl.when(i == 0)
        def _():
            dgate_ref[...] = jnp.zeros_like(dgate_ref)

        loss_ref[...] += (0.5 / D) * jnp.sum(err * err)
        dgate_ref[...] += jnp.sum(dy * y, axis=0, keepdims=True)

    return pl.pallas_call(
        body, name="ffn2_fwd", grid=(B, S // tm),
        out_shape=[jax.ShapeDtypeStruct((B, S, D), F32), jax.ShapeDtypeStruct((1, 128), F32),
                   jax.ShapeDtypeStruct((B, 1, D), F32)],
        in_specs=[_stacked(GU_HALF, tm, GU_SHARD), _rows(tm, D), _rows(tm, D), _perb(6, D), _resident(w_down.shape)],
        out_specs=[_rows(tm, D), _full((1, 128)), _perb(1, D)],
        compiler_params=_cparams(("arbitrary", "arbitrary")),
    )(act, x1, target, mod, w_down)


def _ffn2_bwd(dy, gate, up, mod, w_down):
    B, S, _ = dy.shape
    tm = _tile(S)

    def body(dy_ref, gate_ref, up_ref, mod_ref, w_ref, dgu_ref, dyg_ref):
        dyg = (dy_ref[...] * mod_ref[5:6, :]).astype(BF16)
        dyg_ref[...] = dyg
        for j in range(GU_HALF):
            dact = _dot_nt(dyg, w_ref[j])
            gate, up = gate_ref[j], up_ref[j]
            sg = _sigmoid(gate)
            dgu_ref[j] = (dact * up * (sg * (1.0 + gate * (1.0 - sg)))).astype(BF16)
            dgu_ref[GU_HALF + j] = (dact * (gate * sg)).astype(BF16)

    return pl.pallas_call(
        body, name="ffn2_bwd", grid=(B, S // tm),
        out_shape=[jax.ShapeDtypeStruct((B, N_DEV, S, GU_SHARD), BF16), jax.ShapeDtypeStruct((B, S, D), BF16)],
        in_specs=[_rows(tm, D), _stacked(GU_HALF, tm, GU_SHARD), _stacked(GU_HALF, tm, GU_SHARD), _perb(6, D),
                  _resident(w_down.shape)],
        out_specs=[_stacked(N_DEV, tm, GU_SHARD), _rows(tm, D)],
        compiler_params=_cparams(("parallel", "arbitrary")),
    )(dy, gate, up, mod, w_down)


def _ffn1_bwd(dgu, x1, dy, mod, g2, w_gu):
    B, S, _ = x1.shape
    tm = _tile(S)

    def body(dgu_ref, x_ref, dy_ref, mod_ref, g_ref, w_ref, dx1_ref, dg_ref, dsc_ref, dsh_ref):
        b, i = pl.program_id(0), pl.program_id(1)
        dh = _dot_nt(dgu_ref[0], w_ref[0])
        for j in range(1, N_DEV):
            dh = dh + _dot_nt(dgu_ref[j], w_ref[j])
        _, vjp = jax.vjp(_rms_mod, x_ref[...], g_ref[...], mod_ref[4:5, :], mod_ref[3:4, :])
        dx, dg, dsc, dsh = vjp(dh)
        dx1_ref[...] = dy_ref[...] + dx

        @pl.when((b == 0) & (i == 0))
        def _():
            dg_ref[...] = jnp.zeros_like(dg_ref)

        @pl.when(i == 0)
        def _():
            dsc_ref[...] = jnp.zeros_like(dsc_ref)
            dsh_ref[...] = jnp.zeros_like(dsh_ref)

        dg_ref[...] += dg
        dsc_ref[...] += dsc
        dsh_ref[...] += dsh

    return pl.pallas_call(
        body, name="ffn1_bwd", grid=(B, S // tm),
        out_shape=[jax.ShapeDtypeStruct((B, S, D), F32), jax.ShapeDtypeStruct((1, D), F32),
                   jax.ShapeDtypeStruct((B, 1, D), F32), jax.ShapeDtypeStruct((B, 1, D), F32)],
        in_specs=[_stacked(N_DEV, tm, GU_SHARD), _rows(tm, D), _rows(tm, D), _perb(6, D), _full((1, D)),
                  _resident(w_gu.shape)],
        out_specs=[_rows(tm, D), _full((1, D)), _perb(1, D), _perb(1, D)],
        compiler_params=_cparams(("arbitrary", "arbitrary")),
    )(dgu, x1, dy, mod, g2, w_gu)


def _adamw(w, g, m, v, name):
    def body(w_ref, g_ref, m_ref, v_ref, d_ref, nm_ref, nv_ref):
        g = g_ref[...]
        m = B1 * m_ref[...] + (1.0 - B1) * g
        v = B2 * v_ref[...] + (1.0 - B2) * (g * g)
        nm_ref[...] = m
        nv_ref[...] = v
        m_hat = m / (1.0 - B1 ** STEP)
        v_hat = v / (1.0 - B2 ** STEP)
        d_ref[...] = -LR * (m_hat / (jnp.sqrt(v_hat) + AEPS) + WD * w_ref[...])

    sd = jax.ShapeDtypeStruct(w.shape, F32)
    return pl.pallas_call(body, name=name, out_shape=(sd, sd, sd), compiler_params=_cparams())(w, g, m, v)


def kernel(x, c, positions, ada_w, ada_b, norm1_g, w_in, conv_w, q_norm_g, k_norm_g, sinks, a_log, dt_bias, dn_norm_g, w_branch, w_out, norm2_g, w_gate_up, w_down, loss_target, m_ada_w, m_ada_b, m_norm1_g, m_w_in, m_conv_w, m_q_norm_g, m_k_norm_g, m_sinks, m_a_log, m_dt_bias, m_dn_norm_g, m_w_branch, m_w_out, m_norm2_g, m_w_gate_up, m_w_down, v_ada_w, v_ada_b, v_norm1_g, v_w_in, v_conv_w, v_q_norm_g, v_k_norm_g, v_sinks, v_a_log, v_dt_bias, v_dn_norm_g, v_w_branch, v_w_out, v_norm2_g, v_w_gate_up, v_w_down):
    B, S, _ = x.shape
    me = 4 * lax.axis_index("x") + 2 * lax.axis_index("y") + lax.axis_index("c")

    shards = [w[0].astype(BF16) for w in (w_in, w_branch, w_out, w_gate_up, w_down)]
    in_sems, in_srcs, in_lands, in_token = _copies_start(shards[:1], [_place_own(shards[0], me)], False, c, "gather_in_start")

    c_all = _all_gather_small(c + in_token[0, 0], "gather_c").reshape(N_DEV * B, D)
    ncol = 6 * D // N_DEV
    mod_cols, cond_all = _ada_fwd(c_all, ada_w[0], lax.dynamic_slice(ada_b, (0, me * ncol), (1, ncol)))
    mod_all = _all_gather_small(mod_cols, "gather_mod").transpose(1, 0, 2).reshape(N_DEV * B, 6 * D)
    mod = lax.dynamic_slice(mod_all, (me * B, 0), (B, 6 * D)).reshape(B, 6, D)
    conv2 = conv_w.reshape(CONV, CONVW // N_DEV)
    conv_all = _all_gather_small(conv2, "gather_conv").transpose(1, 0, 2).reshape(CONV, CONVW)

    w_sems, w_srcs, w_lands, _ = _copies_start(shards[1:], [_place_own(s, me) for s in shards[1:]], False,
                                              (mod, conv_all), "gather_rest_start")

    (w_in_b,) = _copies_wait(in_sems, in_srcs, in_lands, mod, "gather_wait_in")
    h1, aq, akv, dnx, ba, z, ga, gd = _inproj_fwd(x, mod, norm1_g, w_in_b)
    invf, mean_q, mean_k = _attn_consts()
    rope_cos, rope_sin = _rope_tables(positions.reshape(B, S, 1), invf)
    o_attn = _attn_fwd(aq, akv, rope_cos, rope_sin, q_norm_g, k_norm_g, sinks, mean_q, mean_k)
    cq = _conv_fwd(dnx, conv_all)
    dn_u, dn_w, dn_qd, dn_kd, dn_a, dn_t, dn_cd = _dn_prep_fwd(cq, ba, a_log, dt_bias)
    o_dn, states = _dn_seq_fwd(dn_u, dn_w, dn_qd, dn_kd, dn_a, dn_cd)
    w_branch_g, w_out_g, w_gu_b, w_down_g = _copies_wait(w_sems, w_srcs, w_lands, o_dn, "gather_wait_rest")
    w_branch_f = w_branch_g.reshape(D, D)
    w_out_f = w_out_g.reshape(D, D)
    w_down_b = w_down_g.reshape(GU_HALF, GU_SHARD, D)
    x1, mix, merged, ob = _mix_fwd(x, o_attn, o_dn, z, ga, gd, mod, dn_norm_g, w_branch_f, w_out_f)
    h2, gate, up, act = _ffn1_fwd(x1, mod, norm2_g, w_gu_b)
    dy, loss_part, d_gate2 = _ffn2_fwd(act, x1, loss_target, mod, w_down_b)
    loss = lax.psum(loss_part[0, 0], ("x", "y", "c"))

    one = lambda t: t.reshape(B, 1, S, t.shape[-1])
    dgu, dyg = _ffn2_bwd(dy, gate, up, mod, w_down_b)
    g_w_down = _wgrad(act, one(dyg), "wgrad_down")
    dx1, d_n2g, d_scale2, d_shift2 = _ffn1_bwd(dgu, x1, dy, mod, norm2_g, w_gu_b)
    g_w_gu = _wgrad(one(h2), dgu, "wgrad_gate_up")
    ffn = _exchange_start([g_w_gu, g_w_down.reshape(N_DEV, FFN // N_DEV, D)], me, dx1, "exchange_ffn_start")
    dmix, dyo, dga, dgd, dz, d_oa, d_od, d_gate1, d_dng = _mix_bwd(
        dx1, mix, o_attn, o_dn, z, ga, gd, mod, dn_norm_g + ffn[3][0, 0], w_branch_f, w_out_f)
    d_dn = _dn_seq_bwd(dn_u, dn_w, dn_qd, dn_kd, dn_a, dn_cd, states, d_od)
    dcq, dba, d_alog, d_dtb = _dn_prep_bwd(cq, ba, a_log, dt_bias, dn_t, *d_dn)
    ddnx, d_conv = _conv_bwd(dnx, conv_all, dcq)
    daq, dakv, d_qg, d_kg, d_sinks = _attn_bwd(aq, akv, rope_cos, rope_sin, q_norm_g, k_norm_g, sinks, mean_q, mean_k, d_oa)
    dps = [daq, dakv, ddnx, dba, dz, dga, dgd]
    dblk, grad_x, d_n1g, d_scale1, d_shift1 = _inproj_bwd(x, mod, norm1_g, dx1, dps, w_in_b)

    dmod = jnp.concatenate([d_shift1, d_scale1, d_gate1, d_shift2, d_scale2, d_gate2], axis=2).reshape(B, 6 * D)
    small = jnp.concatenate([d_n1g, d_qg, d_kg, d_sinks, d_alog, d_dtb, d_dng, d_n2g, d_conv.reshape(1, CONV * CONVW)], axis=1)
    nsm = small.shape[1]
    width = -(-max(6 * D, nsm) // 128) * 128
    rows = jnp.concatenate([jnp.pad(dmod, ((0, 0), (0, width - 6 * D))), jnp.pad(small, ((0, 8 - B - 1), (0, width - nsm)))], axis=0)
    rows_all = _all_gather_small(rows, "gather_small")
    dmod_all = rows_all[:, 0:B, 0:6 * D].reshape(N_DEV * B, 6 * D)
    dmod_cols = lax.dynamic_slice(dmod_all, (0, me * ncol), (N_DEV * B, ncol))
    grad_ada_w, grad_ada_b, small_sum = _ada_bwd(cond_all, dmod_all, dmod_cols, rows_all[:, B, :])
    sizes = [D, HD, HD, HQ, DH, DH, DK, D]
    so = np.cumsum([0] + sizes)
    g_n1, g_qg, g_kg, g_sk, g_al, g_dt, g_dn, g_n2 = [small_sum[:, so[i]:so[i + 1]] for i in range(8)]
    g_conv_all = small_sum[:, so[8]:so[8] + CONV * CONVW].reshape(CONV, N_DEV, CONVW // N_DEV)
    grad_conv = lax.dynamic_slice(g_conv_all, (0, me, 0), (CONV, 1, CONVW // N_DEV)).reshape(CONV, CONVW // N_DEV)

    g_w_in = _wgrad(one(h1), dblk, "wgrad_in", after=small_sum)
    proj = _exchange_start([g_w_in], me, small_sum, "exchange_in_start")
    g_w_out = _wgrad(one(merged), one(dmix), "wgrad_out", after=proj[3])
    g_w_branch = _wgrad(ob, dyo, "wgrad_branch", after=proj[3])
    mixer = _exchange_start([g_w_branch.reshape(N_DEV, D // N_DEV, D), g_w_out.reshape(N_DEV, D // N_DEV, D)], me,
                            proj[3], "exchange_mix_start")

    grad_w_gu, grad_w_down = [_sum_blocks(r, "sum_grads_" + nm) for r, nm in zip(
        _copies_wait(*ffn[:3], mixer[3], "exchange_ffn_wait"), ["gate_up", "down"])]
    (grad_w_in,) = [_sum_blocks(r, "sum_grads_in") for r in _copies_wait(*proj[:3], grad_w_gu, "exchange_in_wait")]
    grad_w_branch, grad_w_out = [_sum_blocks(r, "sum_grads_" + nm) for r, nm in zip(
        _copies_wait(*mixer[:3], grad_w_in, "exchange_mix_wait"), ["branch", "out"])]

    big = [(ada_w, grad_ada_w.reshape(ada_w.shape), m_ada_w, v_ada_w), (w_in, grad_w_in, m_w_in, v_w_in),
           (w_branch, grad_w_branch, m_w_branch, v_w_branch), (w_out, grad_w_out, m_w_out, v_w_out),
           (w_gate_up, grad_w_gu, m_w_gate_up, v_w_gate_up), (w_down, grad_w_down, m_w_down, v_w_down)]
    upd = {}
    for nm, (w, g, m, v) in zip(["ada_w", "w_in", "w_branch", "w_out", "w_gate_up", "w_down"], big):
        upd[nm] = _adamw(w, g, m, v, "adamw_" + nm)
    small_names = ["ada_b", "norm1_g", "q_norm_g", "k_norm_g", "sinks", "a_log", "dt_bias", "dn_norm_g", "norm2_g", "conv_w"]
    small_w = [ada_b, norm1_g, q_norm_g, k_norm_g, sinks, a_log, dt_bias, dn_norm_g, norm2_g, conv_w]
    small_g = [grad_ada_b, g_n1, g_qg, g_kg, g_sk, g_al, g_dt, g_dn, g_n2, grad_conv]
    small_m = [m_ada_b, m_norm1_g, m_q_norm_g, m_k_norm_g, m_sinks, m_a_log, m_dt_bias, m_dn_norm_g, m_norm2_g, m_conv_w]
    small_v = [v_ada_b, v_norm1_g, v_q_norm_g, v_k_norm_g, v_sinks, v_a_log, v_dt_bias, v_dn_norm_g, v_norm2_g, v_conv_w]
    cat = lambda arrs: jnp.concatenate([a.reshape(1, -1) for a in arrs], axis=1)
    res = _adamw(cat(small_w), cat(small_g), cat(small_m), cat(small_v), "adamw_small")
    po = np.cumsum([0] + [int(np.prod(w.shape)) for w in small_w])
    grads = {}
    for i, nm in enumerate(small_names):
        upd[nm] = tuple(r[:, po[i]:po[i + 1]].reshape(small_w[i].shape) for r in res)
        grads[nm] = small_g[i].reshape(small_w[i].shape)
    grads.update(ada_w=grad_ada_w.reshape(ada_w.shape), w_in=grad_w_in, w_branch=grad_w_branch, w_out=grad_w_out,
                 w_gate_up=grad_w_gu, w_down=grad_w_down)

    order = ["ada_w", "ada_b", "norm1_g", "w_in", "conv_w", "q_norm_g", "k_norm_g", "sinks", "a_log", "dt_bias",
             "dn_norm_g", "w_branch", "w_out", "norm2_g", "w_gate_up", "w_down"]
    return (loss, grad_x, *[grads[n] for n in order], *[upd[n][0] for n in order],
            *[upd[n][1] for n in order], *[upd[n][2] for n in order])
```

```python
import functools

import numpy as np
import jax
import jax.numpy as jnp
from jax import lax
from jax.experimental import pallas as pl
from jax.experimental.pallas import tpu as pltpu

F32 = jnp.float32
BF16 = jnp.bfloat16
HI = lax.Precision.HIGHEST

N_DEV = 8
D = 1024
HQ, HKV, HD = 8, 2, 64
GRP = HQ // HKV
BLK = 128
ROT = HD // 4
THETA = 500000.0
QW, KVW = HQ * HD, HKV * HD
DH, DK = 4, 128
CH = 64
DNW = DH * DK
CONV = 4
CONVW = 3 * DNW
FFN = 2816
EPS = 1e-6
IN_W = QW + 2 * KVW + CONVW + 2 * DH + DNW + 2 * D

LR, B1, B2, AEPS, WD, STEP = 0.001, 0.9, 0.999, 1e-08, 0.01, 10

VMEM_LIMIT = 56 * 1024 * 1024
MESH = pl.DeviceIdType.MESH


def _cparams(sem=None, vmem=VMEM_LIMIT):
    return pltpu.CompilerParams(dimension_semantics=sem, vmem_limit_bytes=vmem)


def _full(shape):
    n = len(shape)
    return pl.BlockSpec(shape, lambda *_: (0,) * n)


def _resident(shape):
    n = len(shape)
    return pl.BlockSpec(shape, lambda *_: (0,) * n, pipeline_mode=pl.Buffered(1))


def _rows(tm, w):
    return pl.BlockSpec((None, tm, w), lambda b, i: (b, i, 0))


def _stacked(n, tm, w):
    return pl.BlockSpec((None, n, tm, w), lambda b, i: (b, 0, i, 0))


def _perb(r, w):
    return pl.BlockSpec((None, r, w), lambda b, i: (b, 0, 0))


def _dot(a, b):
    return jnp.dot(a.astype(BF16), b.astype(BF16), preferred_element_type=F32)


def _dot_nt(a, b):
    return lax.dot_general(a.astype(BF16), b.astype(BF16), (((1,), (1,)), ((), ())), preferred_element_type=F32)


def _dot_tn(a, b):
    return lax.dot_general(a.astype(BF16), b.astype(BF16), (((0,), (0,)), ((), ())), preferred_element_type=F32)


def _dot_hi(a, b):
    return jnp.dot(a, b, preferred_element_type=F32, precision=HI)


def _sigmoid(x):
    return jax.nn.sigmoid(x)


def _silu(x):
    return x * jax.nn.sigmoid(x)


def _rms_mod(x, g, scale, shift):
    r = lax.rsqrt(jnp.mean(x * x, axis=-1, keepdims=True) + EPS)
    return (x * r * g) * (1.0 + scale) + shift


def _tile(S, rows=256):
    return min(rows, S)


def _peer(x, y, c, k):
    px = 1 - x if (k >> 2) & 1 else x
    py = 1 - y if (k >> 1) & 1 else y
    pc = 1 - c if k & 1 else c
    return px, py, pc


def _all_gather_small(v, name):
    r, n = v.shape

    def body(v_ref, out_ref, send_sems, recv_sems, local_sem):
        x, y, c = lax.axis_index("x"), lax.axis_index("y"), lax.axis_index("c")
        me = 4 * x + 2 * y + c
        mine = pltpu.make_async_copy(v_ref, out_ref.at[me], local_sem)
        mine.start()
        sends = []
        for k in range(1, N_DEV):
            cp = pltpu.make_async_remote_copy(
                src_ref=v_ref, dst_ref=out_ref.at[me], send_sem=send_sems.at[k - 1], recv_sem=recv_sems.at[k - 1],
                device_id=_peer(x, y, c, k), device_id_type=MESH)
            cp.start()
            sends.append(cp)
        for k in range(1, N_DEV):
            px, py, pc = _peer(x, y, c, k)
            pltpu.make_async_remote_copy(
                src_ref=v_ref, dst_ref=out_ref.at[4 * px + 2 * py + pc], send_sem=send_sems.at[k - 1],
                recv_sem=recv_sems.at[k - 1], device_id=(px, py, pc), device_id_type=MESH).wait_recv()
        for cp in sends:
            cp.wait_send()
        mine.wait()

    return pl.pallas_call(
        body, name=name,
        out_shape=jax.ShapeDtypeStruct((N_DEV, r, n), v.dtype),
        in_specs=[pl.BlockSpec(memory_space=pltpu.VMEM)],
        out_specs=pl.BlockSpec(memory_space=pltpu.VMEM),
        scratch_shapes=[pltpu.SemaphoreType.DMA((N_DEV - 1,)), pltpu.SemaphoreType.DMA((N_DEV - 1,)), pltpu.SemaphoreType.DMA],
    )(v)


def _all_gather_big(vs, name):
    na = len(vs)

    def body(*refs):
        v_refs, out_refs = refs[:na], refs[na:2 * na]
        send_sems, recv_sems, local_sems = refs[2 * na:]
        x, y, c = lax.axis_index("x"), lax.axis_index("y"), lax.axis_index("c")
        me, sibling = (x, y, c), (x, y, 1 - c)
        chips = [(1 - x, y), (x, 1 - y), (1 - x, 1 - y)]

        def rows(a, px, py, pc):
            return out_refs[a].at[4 * px + 2 * py + pc]

        def copy(a, k, block, to, src=None):
            return pltpu.make_async_remote_copy(
                src_ref=rows(a, *block) if src is None else src, dst_ref=rows(a, *block),
                send_sem=send_sems.at[7 * a + k], recv_sem=recv_sems.at[7 * a + k], device_id=to, device_id_type=MESH)

        mine = [pltpu.make_async_copy(v_refs[a], rows(a, *me), local_sems.at[a]) for a in range(na)]
        for cp in mine:
            cp.start()
        first = []
        for a in range(na):
            first.append(copy(a, 0, me, sibling, src=v_refs[a]))
            first += [copy(a, 1 + j, me, (*chip, c), src=v_refs[a]) for j, chip in enumerate(chips)]
        for cp in first:
            cp.start()
        passed = []
        for j, chip in enumerate(chips):
            for a in range(na):
                copy(a, 1 + j, (*chip, c), me).wait_recv()
                forward = copy(a, 4 + j, (*chip, c), sibling)
                forward.start()
                passed.append(forward)
        for a in range(na):
            copy(a, 0, sibling, me).wait_recv()
            for j, chip in enumerate(chips):
                copy(a, 4 + j, (*chip, 1 - c), me).wait_recv()
        for cp in first + passed:
            cp.wait_send()
        for cp in mine:
            cp.wait()

    return pl.pallas_call(
        body, name=name,
        out_shape=[jax.ShapeDtypeStruct((N_DEV,) + v.shape, v.dtype) for v in vs],
        in_specs=[pl.BlockSpec(memory_space=pl.ANY)] * na,
        out_specs=[pl.BlockSpec(memory_space=pl.ANY)] * na,
        scratch_shapes=[pltpu.SemaphoreType.DMA((7 * na,)), pltpu.SemaphoreType.DMA((7 * na,)),
                        pltpu.SemaphoreType.DMA((na,))],
    )(*vs)


def _exchange_blocks(gs, name):
    na = len(gs)

    def body(*refs):
        g_refs, out_refs = refs[:na], refs[na:2 * na]
        send_sems, recv_sems, local_sems = refs[2 * na:]
        x, y, c = lax.axis_index("x"), lax.axis_index("y"), lax.axis_index("c")
        me = 4 * x + 2 * y + c
        mine = [pltpu.make_async_copy(g_refs[a].at[me], out_refs[a].at[me], local_sems.at[a]) for a in range(na)]
        for cp in mine:
            cp.start()
        sends = []
        for k in range(1, N_DEV):
            px, py, pc = _peer(x, y, c, k)
            for a in range(na):
                cp = pltpu.make_async_remote_copy(
                    src_ref=g_refs[a].at[4 * px + 2 * py + pc], dst_ref=out_refs[a].at[me],
                    send_sem=send_sems.at[7 * a + k - 1], recv_sem=recv_sems.at[7 * a + k - 1],
                    device_id=(px, py, pc), device_id_type=MESH)
                cp.start()
                sends.append(cp)
        for k in range(1, N_DEV):
            px, py, pc = _peer(x, y, c, k)
            for a in range(na):
                pltpu.make_async_remote_copy(
                    src_ref=g_refs[a].at[me], dst_ref=out_refs[a].at[4 * px + 2 * py + pc],
                    send_sem=send_sems.at[7 * a + k - 1], recv_sem=recv_sems.at[7 * a + k - 1],
                    device_id=(px, py, pc), device_id_type=MESH).wait_recv()
        for cp in sends:
            cp.wait_send()
        for cp in mine:
            cp.wait()

    return pl.pallas_call(
        body, name=name,
        out_shape=[jax.ShapeDtypeStruct(g.shape, g.dtype) for g in gs],
        in_specs=[pl.BlockSpec(memory_space=pl.ANY)] * na,
        out_specs=[pl.BlockSpec(memory_space=pl.ANY)] * na,
        scratch_shapes=[pltpu.SemaphoreType.DMA((7 * na,)), pltpu.SemaphoreType.DMA((7 * na,)),
                        pltpu.SemaphoreType.DMA((na,))],
    )(*gs)


_HBM = pl.BlockSpec(memory_space=pltpu.HBM)
_SEM = pl.BlockSpec(memory_space=pltpu.SEMAPHORE)
_EFFECT = pltpu.SideEffectType.DATAFLOW_SIDE_EFFECTING


def _place_own(block, me):
    land = lax.empty((N_DEV,) + block.shape, block.dtype)
    return lax.dynamic_update_slice(land, block[None], (me,) + (0,) * block.ndim)


def _copies_start(srcs, lands, scatter, after, name):
    na = len(srcs)
    afters = tuple(after) if isinstance(after, (tuple, list)) else (after,)

    def body(*refs):
        src_refs, land_refs = refs[:na], refs[na:2 * na]
        sems = refs[2 * na + len(afters):4 * na + len(afters)]
        token = refs[-1]
        x, y, c = lax.axis_index("x"), lax.axis_index("y"), lax.axis_index("c")
        me = 4 * x + 2 * y + c
        for a in range(na):
            for k in range(1, N_DEV):
                px, py, pc = _peer(x, y, c, k)
                src = src_refs[a].at[4 * px + 2 * py + pc] if scatter else src_refs[a]
                pltpu.make_async_remote_copy(
                    src_ref=src, dst_ref=land_refs[a].at[me], send_sem=sems[2 * a], recv_sem=sems[2 * a + 1],
                    device_id=(px, py, pc), device_id_type=MESH).start()
        token[...] = jnp.zeros_like(token)

    hbm = lambda t: pltpu.HBM(t.shape, t.dtype)
    out = pl.pallas_call(
        body, name=name,
        out_shape=tuple([pltpu.SemaphoreType.DMA(())] * (2 * na) + [hbm(t) for t in srcs] + [hbm(t) for t in lands]
                        + [jax.ShapeDtypeStruct((8, 128), F32)]),
        in_specs=[_HBM] * (2 * na) + [pl.BlockSpec(memory_space=pl.ANY)] * len(afters),
        out_specs=tuple([_SEM] * (2 * na) + [_HBM] * (2 * na) + [pl.BlockSpec(memory_space=pltpu.VMEM)]),
        input_output_aliases={i: 2 * na + i for i in range(2 * na)},
        compiler_params=pltpu.CompilerParams(has_side_effects=_EFFECT),
    )(*[pltpu.with_memory_space_constraint(t, pltpu.HBM) for t in list(srcs) + list(lands)], *afters)
    return out[:2 * na], out[2 * na:3 * na], out[3 * na:4 * na], out[-1]


def _exchange_start(gs, me, after, name):
    own = [lax.dynamic_index_in_dim(g, me, 0, keepdims=False) for g in gs]
    return _copies_start(gs, [_place_own(o, me) for o in own], True, after, name)


def _copies_wait(sems, srcs, lands, after, name):
    na = len(srcs)

    def body(*refs):
        land_refs = refs[na:2 * na]
        sem_refs = refs[2 * na:4 * na]
        x, y, c = lax.axis_index("x"), lax.axis_index("y"), lax.axis_index("c")
        for a in range(na):
            seven = land_refs[a].at[pl.ds(0, N_DEV - 1)]
            copy = pltpu.make_async_remote_copy(
                src_ref=seven, dst_ref=seven, send_sem=sem_refs[2 * a], recv_sem=sem_refs[2 * a + 1],
                device_id=(x, y, c), device_id_type=MESH)
            copy.wait_send()
            copy.wait_recv()

    hbm = lambda t: pltpu.HBM(t.shape, t.dtype)
    out = pl.pallas_call(
        body, name=name,
        out_shape=tuple([hbm(t) for t in srcs] + [hbm(t) for t in lands]),
        in_specs=[_HBM] * (2 * na) + [_SEM] * (2 * na) + [pl.BlockSpec(memory_space=pl.ANY)],
        out_specs=tuple([_HBM] * (2 * na)),
        input_output_aliases={i: i for i in range(2 * na)},
        compiler_params=pltpu.CompilerParams(has_side_effects=_EFFECT),
    )(*srcs, *lands, *sems, after)
    return out[na:]


def _sum_blocks(g, name):
    _, r, n = g.shape
    tr = 256 if r % 256 == 0 else r

    def body(g_ref, o_ref):
        acc = g_ref[0].astype(F32)
        for d in range(1, N_DEV):
            acc = acc + g_ref[d].astype(F32)
        o_ref[...] = acc

    return pl.pallas_call(
        body, name=name, grid=(r // tr,),
        out_shape=jax.ShapeDtypeStruct((1, r, n), F32),
        in_specs=[pl.BlockSpec((N_DEV, tr, n), lambda i: (0, i, 0))],
        out_specs=pl.BlockSpec((None, tr, n), lambda i: (0, i, 0)),
        compiler_params=_cparams(("arbitrary",)),
    )(g)


def _ada_fwd(c_all, ada_w, ada_b_cols):
    nb, ncol = c_all.shape[0], ada_w.shape[1]

    def body(c_ref, w_ref, b_ref, mod_ref, cond_ref):
        cond = _silu(c_ref[...])
        cond_ref[...] = cond
        mod_ref[...] = _dot_hi(cond, w_ref[...]) + b_ref[...]

    return pl.pallas_call(
        body, name="ada_fwd",
        out_shape=(jax.ShapeDtypeStruct((nb, ncol), F32), jax.ShapeDtypeStruct((nb, D), F32)),
        compiler_params=_cparams(),
    )(c_all, ada_w, ada_b_cols)


def _ada_bwd(cond_all, dmod_all, dmod_cols, smalls):
    ncol, nsm = dmod_cols.shape[1], smalls.shape[1]

    def body(cond_ref, dm_ref, dmc_ref, sm_ref, gw_ref, gb_ref, gs_ref):
        gw_ref[...] = lax.dot_general(cond_ref[...], dmc_ref[...], (((0,), (0,)), ((), ())),
                                      preferred_element_type=F32, precision=HI)
        gb_ref[...] = jnp.sum(dm_ref[...], axis=0, keepdims=True)
        gs_ref[...] = jnp.sum(sm_ref[...], axis=0, keepdims=True)

    return pl.pallas_call(
        body, name="ada_bwd",
        out_shape=(jax.ShapeDtypeStruct((D, ncol), F32), jax.ShapeDtypeStruct((1, 6 * D), F32),
                   jax.ShapeDtypeStruct((1, nsm), F32)),
        compiler_params=_cparams(),
    )(cond_all, dmod_all, dmod_cols, smalls)


IN_CUTS = (0, QW, QW + 2 * KVW, QW + 2 * KVW + CONVW, QW + 2 * KVW + CONVW + 2 * DH,
           QW + 2 * KVW + CONVW + 2 * DH + DNW, QW + 2 * KVW + CONVW + 2 * DH + DNW + D, IN_W)
IN_WIDTHS = tuple(b - a for a, b in zip(IN_CUTS[:-1], IN_CUTS[1:]))
IN_SHARD = IN_W // N_DEV


def _inproj_fwd(x, mod, g1, w_blk):
    B, S, _ = x.shape
    tm = _tile(S)

    def body(x_ref, mod_ref, g_ref, w_ref, h_ref, *o_refs):
        h = _rms_mod(x_ref[...], g_ref[...], mod_ref[1:2, :], mod_ref[0:1, :]).astype(BF16)
        h_ref[...] = h
        full = jnp.concatenate([jnp.dot(h, w_ref[j], preferred_element_type=F32) for j in range(N_DEV)], axis=1)
        for o_ref, lo, hi in zip(o_refs, IN_CUTS[:-1], IN_CUTS[1:]):
            o_ref[...] = full[:, lo:hi]

    return pl.pallas_call(
        body, name="inproj_fwd", grid=(B, S // tm),
        out_shape=[jax.ShapeDtypeStruct((B, S, D), BF16)] + [jax.ShapeDtypeStruct((B, S, w), F32) for w in IN_WIDTHS],
        in_specs=[_rows(tm, D), _perb(6, D), _full((1, D)), _resident(w_blk.shape)],
        out_specs=[_rows(tm, D)] + [_rows(tm, w) for w in IN_WIDTHS],
        compiler_params=_cparams(("parallel", "arbitrary")),
    )(x, mod, g1, w_blk)


def _inproj_bwd(x, mod, g1, dx1, dps, w_blk):
    B, S, _ = x.shape
    tm = _tile(S)
    n = len(dps)

    def body(x_ref, mod_ref, g_ref, dx1_ref, *refs):
        dp_refs, w_ref = refs[:n], refs[n]
        dblk_ref, gx_ref, dg_ref, dsc_ref, dsh_ref = refs[n + 1:]
        b, i = pl.program_id(0), pl.program_id(1)
        full = jnp.concatenate([r[...].astype(F32) for r in dp_refs], axis=1)
        dh = None
        for j in range(N_DEV):
            blk = full[:, IN_SHARD * j:IN_SHARD * (j + 1)].astype(BF16)
            dblk_ref[j] = blk
            t = _dot_nt(blk, w_ref[j])
            dh = t if dh is None else dh + t
        _, vjp = jax.vjp(_rms_mod, x_ref[...], g_ref[...], mod_ref[1:2, :], mod_ref[0:1, :])
        dx, dg, dsc, dsh = vjp(dh)
        gx_ref[...] = dx1_ref[...] + dx

        @pl.when((b == 0) & (i == 0))
        def _():
            dg_ref[...] = jnp.zeros_like(dg_ref)

        @pl.when(i == 0)
        def _():
            dsc_ref[...] = jnp.zeros_like(dsc_ref)
            dsh_ref[...] = jnp.zeros_like(dsh_ref)

        dg_ref[...] += dg
        dsc_ref[...] += dsc
        dsh_ref[...] += dsh

    return pl.pallas_call(
        body, name="inproj_bwd", grid=(B, S // tm),
        out_shape=[jax.ShapeDtypeStruct((B, N_DEV, S, IN_SHARD), BF16), jax.ShapeDtypeStruct((B, S, D), F32),
                   jax.ShapeDtypeStruct((1, D), F32), jax.ShapeDtypeStruct((B, 1, D), F32),
                   jax.ShapeDtypeStruct((B, 1, D), F32)],
        in_specs=[_rows(tm, D), _perb(6, D), _full((1, D)), _rows(tm, D)]
                 + [_rows(tm, w) for w in IN_WIDTHS] + [_resident(w_blk.shape)],
        out_specs=[pl.BlockSpec((None, N_DEV, tm, IN_SHARD), lambda b, i: (b, 0, i, 0)), _rows(tm, D),
                   _full((1, D)), _perb(1, D), _perb(1, D)],
        compiler_params=_cparams(("arbitrary", "arbitrary")),
    )(x, mod, g1, dx1, *dps, w_blk)


def _wgrad(a, b, name, after=None):
    B, na, S, K = a.shape
    nb, N = b.shape[1], b.shape[3]
    G = max(na, nb)
    tm = min(512, S)
    nt = S // tm
    last = B * nt - 1

    def body(a_ref, b_ref, *rest):
        o_ref, acc = rest[-2:]
        t = pl.program_id(1)

        @pl.when(t == 0)
        def _():
            acc[...] = jnp.zeros_like(acc)

        acc[...] += lax.dot_general(a_ref[...], b_ref[...], (((0,), (0,)), ((), ())), preferred_element_type=F32)

        @pl.when(t == last)
        def _():
            o_ref[...] = acc[...].astype(BF16)

    return pl.pallas_call(
        body, name=name, grid=(G, B * nt),
        out_shape=jax.ShapeDtypeStruct((G, K, N), BF16),
        in_specs=[pl.BlockSpec((None, None, tm, K), lambda g, t: (t // nt, g if na > 1 else 0, t % nt, 0)),
                  pl.BlockSpec((None, None, tm, N), lambda g, t: (t // nt, g if nb > 1 else 0, t % nt, 0))]
                 + ([] if after is None else [pl.BlockSpec(memory_space=pl.ANY)]),
        out_specs=pl.BlockSpec((None, K, N), lambda g, t: (g, 0, 0)),
        scratch_shapes=[pltpu.VMEM((K, N), F32)],
        compiler_params=_cparams(("parallel", "arbitrary")),
    )(*((a, b) if after is None else (a, b, after)))


LANES = 128


def _attn_consts():
    inv_freq = THETA ** (-jnp.arange(0, ROT, 2, dtype=F32) / ROT)
    head = jnp.concatenate([inv_freq, inv_freq, jnp.zeros((HD - ROT,), F32)])
    invf = jnp.tile(head, LANES // HD)[None, :]
    mean_of = lambda w: jnp.asarray(np.kron(np.eye(w // HD), np.full((HD, HD), 1.0 / HD)), BF16)
    return invf, mean_of(QW), mean_of(KVW)


def _rope_tables(pos, invf):
    B, S, _ = pos.shape
    tr = min(1024, S)

    def body(p_ref, f_ref, c_ref, s_ref):
        ang = p_ref[...].astype(F32) * f_ref[...]
        c_ref[...] = jnp.cos(ang)
        s_ref[...] = jnp.sin(ang)

    sd = jax.ShapeDtypeStruct((B, S, LANES), F32)
    return pl.pallas_call(
        body, name="rope_tables", grid=(B, S // tr), out_shape=[sd, sd],
        in_specs=[_rows(tr, 1), _full((1, LANES))], out_specs=[_rows(tr, LANES), _rows(tr, LANES)],
        compiler_params=_cparams(("parallel", "parallel")),
    )(pos, invf)


def _rope_expand(cos, sin, reps):
    lane = lax.broadcasted_iota(jnp.int32, cos.shape, 1) % HD
    sa = jnp.where((lane >= ROT // 2) & (lane < ROT), sin, 0.0)
    sb = jnp.where(lane < ROT // 2, -sin, 0.0)
    rep = lambda t: jnp.concatenate([t] * reps, axis=1) if reps > 1 else t
    return rep(cos), rep(sa), rep(sb)


@jax.custom_vjp
def _rope(t, cos, sa, sb):
    w = t.shape[1]
    return t * cos + pltpu.roll(t, ROT // 2, 1) * sa + pltpu.roll(t, w - ROT // 2, 1) * sb


def _rope_fwd(t, cos, sa, sb):
    return _rope(t, cos, sa, sb), (cos, sa, sb)


def _rope_bwd(res, d):
    cos, sa, sb = res
    w = d.shape[1]
    dt = d * cos + pltpu.roll(d * sa, w - ROT // 2, 1) + pltpu.roll(d * sb, ROT // 2, 1)
    return dt, jnp.zeros_like(cos), jnp.zeros_like(sa), jnp.zeros_like(sb)


_rope.defvjp(_rope_fwd, _rope_bwd)


def _head_norm(t, g, mean_of):
    hi, lo = _split(t * t)
    ms = jnp.dot(hi, mean_of, preferred_element_type=F32) + jnp.dot(lo, mean_of, preferred_element_type=F32)
    return t * lax.rsqrt(ms + EPS) * g


def _attn_block(q, kvp, kvc, qg, kg, sinks, tq, tk, mq, mk, valid):
    qn = _rope(_head_norm(q, jnp.concatenate([qg] * HQ, axis=1), mq), *tq) * (HD ** -0.5)
    kv = jnp.concatenate([kvp, kvc], axis=0)
    kn = _rope(_head_norm(kv[:, 0:KVW], jnp.concatenate([kg] * HKV, axis=1), mk), *tk)
    q4 = jnp.stack([jnp.concatenate([qn[:, HD * (GRP * j + i):HD * (GRP * j + i + 1)] for i in range(GRP)], axis=0)
                    for j in range(HKV)])
    k2 = jnp.stack([kn[:, HD * j:HD * (j + 1)] for j in range(HKV)])
    v2 = jnp.stack([kv[:, KVW + HD * j:KVW + HD * (j + 1)] for j in range(HKV)])
    rowblk = lax.broadcasted_iota(jnp.int32, (GRP * BLK, 1), 0) // BLK
    sink = jnp.stack([sum(jnp.where(rowblk == i, sinks[:, GRP * j + i:GRP * j + i + 1], 0.0) for i in range(GRP))
                      for j in range(HKV)])
    s = jnp.where(valid[None], _bmm(q4, k2, _BMM_NT), -1e30)
    m = lax.stop_gradient(jnp.maximum(jnp.max(s, axis=-1, keepdims=True), sink))
    p = jnp.exp(s - m)
    probs = p * (1.0 / (jnp.sum(p, axis=-1, keepdims=True) + jnp.exp(sink - m)))
    o4 = _bmm(probs, v2)
    return jnp.concatenate([o4[j, BLK * i:BLK * (i + 1), :] for j in range(HKV) for i in range(GRP)], axis=1)


def _attn_tables(cp_ref, cc_ref, sp_ref, sc_ref, n):
    tq = _rope_expand(cc_ref[...], sc_ref[...], QW // LANES)
    tk = _rope_expand(jnp.concatenate([cp_ref[...], cc_ref[...]], axis=0),
                      jnp.concatenate([sp_ref[...], sc_ref[...]], axis=0), KVW // LANES)
    qi = lax.broadcasted_iota(jnp.int32, (GRP * BLK, 2 * BLK), 0) % BLK + BLK
    kj = lax.broadcasted_iota(jnp.int32, (GRP * BLK, 2 * BLK), 1)
    dist = qi - kj
    valid = (dist >= 0) & (dist < BLK) & ((kj >= BLK) | (n > 0))
    return tq, tk, valid


def _attn_fwd(aq, akv, cos, sin, qg, kg, sinks, mq, mk):
    B, S, _ = aq.shape
    nb = S // BLK

    def body(q_ref, kvp_ref, kvc_ref, cp_ref, cc_ref, sp_ref, sc_ref, qg_ref, kg_ref, sk_ref, mq_ref, mk_ref, o_ref):
        tq, tk, valid = _attn_tables(cp_ref, cc_ref, sp_ref, sc_ref, pl.program_id(1))
        o_ref[...] = _attn_block(q_ref[...], kvp_ref[...], kvc_ref[...], qg_ref[...], kg_ref[...], sk_ref[...],
                                 tq, tk, mq_ref[...], mk_ref[...], valid)

    prev = lambda b, n: (b, jnp.maximum(n - 1, 0), 0)
    cur = lambda b, n: (b, n, 0)
    return pl.pallas_call(
        body, name="attn_fwd", grid=(B, nb),
        out_shape=jax.ShapeDtypeStruct((B, S, QW), F32),
        in_specs=[pl.BlockSpec((None, BLK, QW), cur), pl.BlockSpec((None, BLK, 2 * KVW), prev),
                  pl.BlockSpec((None, BLK, 2 * KVW), cur), pl.BlockSpec((None, BLK, LANES), prev),
                  pl.BlockSpec((None, BLK, LANES), cur), pl.BlockSpec((None, BLK, LANES), prev),
                  pl.BlockSpec((None, BLK, LANES), cur), _full((1, HD)), _full((1, HD)), _full((1, HQ)),
                  _full((QW, QW)), _full((KVW, KVW))],
        out_specs=pl.BlockSpec((None, BLK, QW), cur),
        compiler_params=_cparams(("parallel", "arbitrary")),
    )(aq, akv, akv, cos, cos, sin, sin, qg, kg, sinks, mq, mk)


def _attn_bwd(aq, akv, cos, sin, qg, kg, sinks, mq, mk, do):
    B, S, _ = aq.shape
    nb = S // BLK

    def body(q_ref, kvp_ref, kvc_ref, cp_ref, cc_ref, sp_ref, sc_ref, qg_ref, kg_ref, sk_ref, mq_ref, mk_ref, do_ref,
             dq_ref, dkv_ref, dqg_ref, dkg_ref, dsk_ref, carry):
        b, i = pl.program_id(0), pl.program_id(1)
        tq, tk, valid = _attn_tables(cp_ref, cc_ref, sp_ref, sc_ref, nb - 1 - i)
        fn = functools.partial(_attn_block, tq=tq, tk=tk, mq=mq_ref[...], mk=mk_ref[...], valid=valid)
        _, vjp = jax.vjp(fn, q_ref[...], kvp_ref[...], kvc_ref[...], qg_ref[...], kg_ref[...], sk_ref[...])
        dq, dkvp, dkvc, dqg, dkg, dsk = vjp(do_ref[...])

        @pl.when(i == 0)
        def _():
            carry[...] = jnp.zeros_like(carry)

        @pl.when((b == 0) & (i == 0))
        def _():
            dqg_ref[...] = jnp.zeros_like(dqg_ref)
            dkg_ref[...] = jnp.zeros_like(dkg_ref)
            dsk_ref[...] = jnp.zeros_like(dsk_ref)

        dq_ref[...] = dq.astype(BF16)
        dkv_ref[...] = (dkvc + carry[...]).astype(BF16)
        carry[...] = dkvp
        dqg_ref[...] += dqg
        dkg_ref[...] += dkg
        dsk_ref[...] += dsk

    prev = lambda b, i: (b, jnp.maximum(nb - 2 - i, 0), 0)
    cur = lambda b, i: (b, nb - 1 - i, 0)
    return pl.pallas_call(
        body, name="attn_bwd", grid=(B, nb),
        out_shape=[jax.ShapeDtypeStruct((B, S, QW), BF16), jax.ShapeDtypeStruct((B, S, 2 * KVW), BF16),
                   jax.ShapeDtypeStruct((1, HD), F32), jax.ShapeDtypeStruct((1, HD), F32),
                   jax.ShapeDtypeStruct((1, HQ), F32)],
        in_specs=[pl.BlockSpec((None, BLK, QW), cur), pl.BlockSpec((None, BLK, 2 * KVW), prev),
                  pl.BlockSpec((None, BLK, 2 * KVW), cur), pl.BlockSpec((None, BLK, LANES), prev),
                  pl.BlockSpec((None, BLK, LANES), cur), pl.BlockSpec((None, BLK, LANES), prev),
                  pl.BlockSpec((None, BLK, LANES), cur), _full((1, HD)), _full((1, HD)), _full((1, HQ)),
                  _full((QW, QW)), _full((KVW, KVW)), pl.BlockSpec((None, BLK, QW), cur)],
        out_specs=[pl.BlockSpec((None, BLK, QW), cur), pl.BlockSpec((None, BLK, 2 * KVW), cur),
                   _full((1, HD)), _full((1, HD)), _full((1, HQ))],
        scratch_shapes=[pltpu.VMEM((BLK, 2 * KVW), F32)],
        compiler_params=_cparams(("arbitrary", "arbitrary")),
    )(aq, akv, akv, cos, cos, sin, sin, qg, kg, sinks, mq, mk, do)


def _conv_taps(xe, w, rows):
    y = None
    for j in range(CONV):
        sh = pltpu.roll(xe, CONV - 1 - j, 0)[8:8 + rows, :] if j < CONV - 1 else xe[8:8 + rows, :]
        y = sh * w[j:j + 1, :] if y is None else y + sh * w[j:j + 1, :]
    return y


def _conv_fwd(xin, w):
    B, S, C = xin.shape
    tc = min(512, S)
    r8 = tc // 8

    def body(xp_ref, x_ref, w_ref, o_ref):
        i = pl.program_id(1)
        xp = jnp.where(i > 0, xp_ref[...], 0.0)
        xe = jnp.concatenate([xp, x_ref[...]], axis=0)
        o_ref[...] = _silu(_conv_taps(xe, w_ref[...], tc))

    return pl.pallas_call(
        body, name="conv_fwd", grid=(B, S // tc),
        out_shape=jax.ShapeDtypeStruct((B, S, C), F32),
        in_specs=[pl.BlockSpec((None, 8, C), lambda b, i: (b, jnp.maximum(i * r8 - 1, 0), 0)),
                  _rows(tc, C), _full((CONV, C))],
        out_specs=_rows(tc, C),
        compiler_params=_cparams(("parallel", "arbitrary")),
    )(xin, xin, w)


def _conv_bwd(xin, w, dy):
    B, S, C = xin.shape
    tc = min(512, S)
    r8 = tc // 8
    nt = S // tc

    def body(xp_ref, x_ref, xn_ref, dy_ref, dyn_ref, w_ref, dx_ref, dw_ref):
        b, i = pl.program_id(0), pl.program_id(1)
        w = w_ref[...]
        xp = jnp.where(i > 0, xp_ref[...], 0.0)
        xe = jnp.concatenate([xp, x_ref[...], xn_ref[...]], axis=0)
        pre = _conv_taps(xe, w, tc + 8)
        sg = _sigmoid(pre)
        dyn = jnp.where(i < nt - 1, dyn_ref[...], 0.0)
        dpre = jnp.concatenate([dy_ref[...], dyn], axis=0) * (sg * (1.0 + pre * (1.0 - sg)))
        dx = dpre[0:tc, :] * w[CONV - 1:CONV, :]
        for j in range(CONV - 1):
            dx = dx + pltpu.roll(dpre, tc + 8 - (CONV - 1 - j), 0)[0:tc, :] * w[j:j + 1, :]
        dx_ref[...] = dx.astype(BF16)
        dcur = dpre[0:tc, :]
        xe0 = xe[0:8 + tc, :]
        lane_row = lax.broadcasted_iota(jnp.int32, (CONV, C), 0)
        dw = jnp.zeros((CONV, C), F32)
        for j in range(CONV):
            sh = pltpu.roll(xe0, CONV - 1 - j, 0)[8:8 + tc, :] if j < CONV - 1 else xe0[8:8 + tc, :]
            dw = dw + jnp.where(lane_row == j, jnp.sum(sh * dcur, axis=0, keepdims=True), 0.0)

        @pl.when((b == 0) & (i == 0))
        def _():
            dw_ref[...] = jnp.zeros_like(dw_ref)

        dw_ref[...] += dw

    return pl.pallas_call(
        body, name="conv_bwd", grid=(B, nt),
        out_shape=[jax.ShapeDtypeStruct((B, S, C), BF16), jax.ShapeDtypeStruct((CONV, C), F32)],
        in_specs=[pl.BlockSpec((None, 8, C), lambda b, i: (b, jnp.maximum(i * r8 - 1, 0), 0)),
                  _rows(tc, C),
                  pl.BlockSpec((None, 8, C), lambda b, i: (b, jnp.minimum((i + 1) * r8, S // 8 - 1), 0)),
                  _rows(tc, C),
                  pl.BlockSpec((None, 8, C), lambda b, i: (b, jnp.minimum((i + 1) * r8, S // 8 - 1), 0)),
                  _full((CONV, C))],
        out_specs=[_rows(tc, C), _full((CONV, C))],
        compiler_params=_cparams(("arbitrary", "arbitrary")),
    )(xin, xin, xin, dy, dy, w)


def _softplus(x):
    return jnp.maximum(x, 0.0) + jnp.log1p(jnp.exp(-jnp.abs(x)))


_BMM = (((2,), (1,)), ((0,), (0,)))
_BMM_NT = (((2,), (2,)), ((0,), (0,)))
_BMM_TN = (((1,), (1,)), ((0,), (0,)))


def _bmm(a, b, dims=_BMM):
    return lax.dot_general(a.astype(BF16), b.astype(BF16), dims, preferred_element_type=F32)


def _split(a):
    hi = a.astype(BF16)
    return hi, (a - hi.astype(F32)).astype(BF16)


def _bmm3(a, b, dims=_BMM):
    ah, al = _split(a)
    bh, bl = _split(b)
    d = lambda p, q: lax.dot_general(p, q, dims, preferred_element_type=F32)
    return d(ah, bh) + (d(ah, bl) + d(al, bh))


TRI_BASE = 8


def _tri_inverse(L):
    ii = lax.broadcasted_iota(jnp.int32, (CH, CH), 0)
    jj = lax.broadcasted_iota(jnp.int32, (CH, CH), 1)
    same = lambda size: (ii // size) == (jj // size)
    diag = jnp.where(same(TRI_BASE), L, 0.0)
    X = (ii == jj).astype(F32) - diag
    P = diag
    n = 2
    while n < TRI_BASE:
        P = _bmm3(P, P)
        X = X + _bmm3(X, P)
        n *= 2
    size = TRI_BASE
    while size < CH:
        joint = jnp.where(same(2 * size) & jnp.logical_not(same(size)), L, 0.0)
        X = X - _bmm3(X, _bmm3(joint, X))
        size *= 2
    return X


@jax.custom_vjp
def _tri_inverse_known(L, T):
    return T


def _tri_inverse_known_fwd(L, T):
    return T, T


def _tri_inverse_known_bwd(T, dT):
    return -_bmm3(T, _bmm3(dT, T, _BMM_NT), _BMM_TN), jnp.zeros_like(T)


_tri_inverse_known.defvjp(_tri_inverse_known_fwd, _tri_inverse_known_bwd)


def _cumsum_rows(g):
    n = g.shape[0]
    ii = lax.broadcasted_iota(jnp.int32, (n, CH, CH), 1)
    jj = lax.broadcasted_iota(jnp.int32, (n, CH, CH), 2)
    tri = (ii >= jj).astype(BF16)
    g0 = g.astype(BF16)
    r1 = g - g0.astype(F32)
    g1 = r1.astype(BF16)
    g2 = (r1 - g1.astype(F32)).astype(BF16)
    d = lambda q: lax.dot_general(tri, q, _BMM, preferred_element_type=F32)
    return d(g0) + (d(g1) + d(g2))


def _row_sums(t):
    n, r, w = t.shape
    hi, lo = _split(t.reshape(n * r, w))
    ones = jnp.ones((w, w), BF16)
    s = jnp.dot(hi, ones, preferred_element_type=F32) + jnp.dot(lo, ones, preferred_element_type=F32)
    return s.reshape(n, r, w)


def _dn_prep(t_known, qr, kr, v, a_raw, b_raw, a_log, dt_b):
    n = qr.shape[0]
    ii = lax.broadcasted_iota(jnp.int32, (n, CH, CH), 1)
    jj = lax.broadcasted_iota(jnp.int32, (n, CH, CH), 2)
    incl, strict = ii >= jj, ii > jj
    q = qr * lax.rsqrt(_row_sums(qr * qr) + EPS) * (DK ** -0.5)
    k = kr * lax.rsqrt(_row_sums(kr * kr) + EPS)
    beta = _sigmoid(b_raw)
    g = -jnp.exp(a_log) * _softplus(a_raw + dt_b)
    gcb = _cumsum_rows(jnp.broadcast_to(g, (n, CH, DK)))
    gc = gcb[:, :, 0:1]
    gc_row = jnp.swapaxes(gcb, 1, 2)[:, 0:1, 0:CH]
    decay = jnp.where(incl, jnp.exp(jnp.where(incl, gc - gc_row, 0.0)), 0.0)
    kb = k * beta
    L = jnp.where(strict, _bmm(kb, k, _BMM_NT) * decay, 0.0)
    T = _tri_inverse(L) if t_known is None else _tri_inverse_known(L, t_known)
    eg = jnp.exp(gc)
    u = _bmm(T, v * beta)
    w = _bmm(T, kb * eg)
    a_in = _bmm(q, k, _BMM_NT) * decay
    g_last = gc[:, CH - 1:CH, :]
    return u, w, q * eg, k * jnp.exp(g_last - gc), a_in, jnp.exp(g_last), T


def _dn_step(S0, u, w, qd, kd, a_in, cd):
    r = _bmm(jnp.concatenate([w, qd], axis=1), S0)
    v_new = u - r[:, 0:CH, :]
    o = r[:, CH:2 * CH, :] + _bmm(a_in, v_new)
    S1 = S0 * cd + _bmm(kd, v_new, _BMM_TN)
    return o, S1


def _dn_stack(cq, ba, al, dt, G):
    cols = [[] for _ in range(7)]
    for c in range(G):
        rows = slice(CH * c, CH * (c + 1))
        for h in range(DH):
            parts = (cq[rows, DK * h:DK * (h + 1)], cq[rows, DNW + DK * h:DNW + DK * (h + 1)],
                     cq[rows, 2 * DNW + DK * h:2 * DNW + DK * (h + 1)], ba[rows, DH + h:DH + h + 1],
                     ba[rows, h:h + 1], al[:, h:h + 1], dt[:, h:h + 1])
            for col, p in zip(cols, parts):
                col.append(p)
    return tuple(jnp.stack(col) for col in cols)


def _dn_group(S, want):
    g = want
    while (S // CH) % g:
        g //= 2
    return g


def _dn_prep_fwd(cq, ba, a_log, dt_b):
    B, S, _ = cq.shape
    nc = S // CH
    G = _dn_group(S, 4)

    def body(cq_ref, ba_ref, al_ref, dt_ref, u_ref, w_ref, qd_ref, kd_ref, a_ref, t_ref, cd_ref):
        ops = _dn_stack(cq_ref[...], ba_ref[...], al_ref[...], dt_ref[...], G)
        u, w, qd, kd, a_in, cd, T = _dn_prep(None, *ops)
        lane4 = lax.broadcasted_iota(jnp.int32, (1, DH), 1)
        for c in range(G):
            rows = slice(CH * c, CH * (c + 1))
            cdrow = jnp.zeros((1, DH), F32)
            for h in range(DH):
                n = DH * c + h
                lanes = slice(DK * h, DK * (h + 1))
                u_ref[rows, lanes] = u[n]
                w_ref[rows, lanes] = w[n]
                qd_ref[rows, lanes] = qd[n]
                kd_ref[rows, lanes] = kd[n]
                a_ref[rows, CH * h:CH * (h + 1)] = a_in[n]
                t_ref[rows, CH * h:CH * (h + 1)] = T[n]
                cdrow = cdrow + jnp.where(lane4 == h, cd[n], 0.0)
            cd_ref[c] = cdrow

    wide = jax.ShapeDtypeStruct((B, S, DNW), F32)
    sq = jax.ShapeDtypeStruct((B, S, DH * CH), F32)
    return pl.pallas_call(
        body, name="dn_prep_fwd", grid=(B, nc // G),
        out_shape=[wide, wide, wide, wide, sq, sq, jax.ShapeDtypeStruct((B, nc, 1, DH), F32)],
        in_specs=[_rows(G * CH, CONVW), _rows(G * CH, 2 * DH), _full((1, DH)), _full((1, DH))],
        out_specs=[_rows(G * CH, DNW)] * 4 + [_rows(G * CH, DH * CH)] * 2
                  + [pl.BlockSpec((None, G, 1, DH), lambda b, i: (b, i, 0, 0))],
        compiler_params=_cparams(("parallel", "parallel")),
    )(cq, ba, a_log, dt_b)


def _dn_seq_specs(B, nc, rev):
    at = (lambda i: nc - 1 - i) if rev else (lambda i: i)
    wide = pl.BlockSpec((B, CH, DNW), lambda i: (0, at(i), 0))
    a_spec = pl.BlockSpec((B, CH, DH * CH), lambda i: (0, at(i), 0))
    cd_spec = pl.BlockSpec((B, None, 1, DH), lambda i: (0, at(i), 0, 0))
    st_spec = pl.BlockSpec((B, None, DH, DK, DK), lambda i: (0, at(i), 0, 0, 0))
    return wide, a_spec, cd_spec, st_spec


def _dn_step_operands(B, u_ref, w_ref, qd_ref, kd_ref, a_ref, cd_ref):
    pairs = [(b, h) for b in range(B) for h in range(DH)]
    wide = lambda ref: jnp.stack([ref[b, :, DK * h:DK * (h + 1)] for b, h in pairs])
    a_in = jnp.stack([a_ref[b, :, CH * h:CH * (h + 1)] for b, h in pairs])
    cd = jnp.stack([cd_ref[b, :, h:h + 1] for b, h in pairs])
    return wide(u_ref), wide(w_ref), wide(qd_ref), wide(kd_ref), a_in, cd


def _dn_seq_fwd(u, w, qd, kd, a_in, cd):
    B, S, _ = u.shape
    nc = S // CH

    def body(u_ref, w_ref, qd_ref, kd_ref, a_ref, cd_ref, o_ref, st_ref, state):
        @pl.when(pl.program_id(0) == 0)
        def _():
            state[...] = jnp.zeros_like(state)

        S0 = state[...]
        for b in range(B):
            st_ref[b] = S0[DH * b:DH * (b + 1)]
        o, S1 = _dn_step(S0, *_dn_step_operands(B, u_ref, w_ref, qd_ref, kd_ref, a_ref, cd_ref))
        state[...] = S1
        for b in range(B):
            for h in range(DH):
                o_ref[b, :, DK * h:DK * (h + 1)] = o[DH * b + h]

    wide, a_spec, cd_spec, st_spec = _dn_seq_specs(B, nc, False)
    return pl.pallas_call(
        body, name="dn_seq_fwd", grid=(nc,),
        out_shape=[jax.ShapeDtypeStruct((B, S, DNW), F32), jax.ShapeDtypeStruct((B, nc, DH, DK, DK), F32)],
        in_specs=[wide, wide, wide, wide, a_spec, cd_spec],
        out_specs=[wide, st_spec],
        scratch_shapes=[pltpu.VMEM((B * DH, DK, DK), F32)],
        compiler_params=_cparams(("arbitrary",)),
    )(u, w, qd, kd, a_in, cd)


def _dn_seq_bwd(u, w, qd, kd, a_in, cd, states, do):
    B, S, _ = u.shape
    nc = S // CH

    def body(u_ref, w_ref, qd_ref, kd_ref, a_ref, cd_ref, st_ref, do_ref,
             du_ref, dw_ref, dqd_ref, dkd_ref, da_ref, dcd_ref, dstate):
        @pl.when(pl.program_id(0) == 0)
        def _():
            dstate[...] = jnp.zeros_like(dstate)

        lane4 = lax.broadcasted_iota(jnp.int32, (1, DH), 1)
        S0 = jnp.concatenate([st_ref[b] for b in range(B)], axis=0)
        do = jnp.stack([do_ref[b, :, DK * h:DK * (h + 1)] for b in range(B) for h in range(DH)])
        _, vjp = jax.vjp(_dn_step, S0, *_dn_step_operands(B, u_ref, w_ref, qd_ref, kd_ref, a_ref, cd_ref))
        dS, du, dw, dqd, dkd, da, dcd = vjp((do, dstate[...]))
        dstate[...] = dS
        for b in range(B):
            dcdrow = jnp.zeros((1, DH), F32)
            for h in range(DH):
                n = DH * b + h
                lanes = slice(DK * h, DK * (h + 1))
                du_ref[b, :, lanes] = du[n]
                dw_ref[b, :, lanes] = dw[n]
                dqd_ref[b, :, lanes] = dqd[n]
                dkd_ref[b, :, lanes] = dkd[n]
                da_ref[b, :, CH * h:CH * (h + 1)] = da[n]
                dcdrow = dcdrow + jnp.where(lane4 == h, dcd[n], 0.0)
            dcd_ref[b] = dcdrow

    wide, a_spec, cd_spec, st_spec = _dn_seq_specs(B, nc, True)
    sd = jax.ShapeDtypeStruct((B, S, DNW), F32)
    return pl.pallas_call(
        body, name="dn_seq_bwd", grid=(nc,),
        out_shape=[sd, sd, sd, sd, jax.ShapeDtypeStruct((B, S, DH * CH), F32), jax.ShapeDtypeStruct((B, nc, 1, DH), F32)],
        in_specs=[wide, wide, wide, wide, a_spec, cd_spec, st_spec, wide],
        out_specs=[wide, wide, wide, wide, a_spec, cd_spec],
        scratch_shapes=[pltpu.VMEM((B * DH, DK, DK), F32)],
        compiler_params=_cparams(("arbitrary",)),
    )(u, w, qd, kd, a_in, cd, states, do)


def _dn_prep_bwd(cq, ba, a_log, dt_b, t_inv, du, dw, dqd, dkd, da, dcd):
    B, S, _ = cq.shape
    nc = S // CH
    G = _dn_group(S, 4)

    def body(cq_ref, ba_ref, al_ref, dt_ref, t_ref, du_ref, dw_ref, dqd_ref, dkd_ref, da_ref, dcd_ref,
             dcq_ref, dba_ref, dal_ref, ddt_ref):
        @pl.when((pl.program_id(0) == 0) & (pl.program_id(1) == 0))
        def _():
            dal_ref[...] = jnp.zeros_like(dal_ref)
            ddt_ref[...] = jnp.zeros_like(ddt_ref)

        pairs = [(c, h) for c in range(G) for h in range(DH)]
        rows = lambda c: slice(CH * c, CH * (c + 1))
        wide = lambda ref: jnp.stack([ref[rows(c), DK * h:DK * (h + 1)] for c, h in pairs])
        square = lambda ref: jnp.stack([ref[rows(c), CH * h:CH * (h + 1)] for c, h in pairs])
        ops = _dn_stack(cq_ref[...], ba_ref[...], al_ref[...], dt_ref[...], G)
        cots = (wide(du_ref), wide(dw_ref), wide(dqd_ref), wide(dkd_ref), square(da_ref),
                jnp.stack([dcd_ref[c][:, h:h + 1] for c, h in pairs]), jnp.zeros((len(pairs), CH, CH), F32))
        _, vjp = jax.vjp(functools.partial(_dn_prep, square(t_ref)), *ops)
        dq, dk, dv, dar, dbr, dl, dd = vjp(cots)
        lane8 = lax.broadcasted_iota(jnp.int32, (CH, 2 * DH), 1)
        lane4 = lax.broadcasted_iota(jnp.int32, (1, DH), 1)
        dal = jnp.zeros((1, DH), F32)
        ddt = jnp.zeros((1, DH), F32)
        for c in range(G):
            dba = jnp.zeros((CH, 2 * DH), F32)
            for h in range(DH):
                n = DH * c + h
                dcq_ref[rows(c), DK * h:DK * (h + 1)] = dq[n]
                dcq_ref[rows(c), DNW + DK * h:DNW + DK * (h + 1)] = dk[n]
                dcq_ref[rows(c), 2 * DNW + DK * h:2 * DNW + DK * (h + 1)] = dv[n]
                dba = dba + jnp.where(lane8 == h, dbr[n], 0.0) + jnp.where(lane8 == DH + h, dar[n], 0.0)
                dal = dal + jnp.where(lane4 == h, dl[n], 0.0)
                ddt = ddt + jnp.where(lane4 == h, dd[n], 0.0)
            dba_ref[rows(c), :] = dba.astype(BF16)
        dal_ref[...] += dal
        ddt_ref[...] += ddt

    return pl.pallas_call(
        body, name="dn_prep_bwd", grid=(B, nc // G),
        out_shape=[jax.ShapeDtypeStruct((B, S, CONVW), F32), jax.ShapeDtypeStruct((B, S, 2 * DH), BF16),
                   jax.ShapeDtypeStruct((1, DH), F32), jax.ShapeDtypeStruct((1, DH), F32)],
        in_specs=[_rows(G * CH, CONVW), _rows(G * CH, 2 * DH), _full((1, DH)), _full((1, DH)), _rows(G * CH, DH * CH)]
                 + [_rows(G * CH, DNW)] * 4 + [_rows(G * CH, DH * CH),
                                               pl.BlockSpec((None, G, 1, DH), lambda b, i: (b, i, 0, 0))],
        out_specs=[_rows(G * CH, CONVW), _rows(G * CH, 2 * DH), _full((1, DH)), _full((1, DH))],
        compiler_params=_cparams(("arbitrary", "arbitrary")),
    )(cq, ba, a_log, dt_b, t_inv, du, dw, dqd, dkd, da, dcd)


def _gated_norm(o, z, g):
    outs = []
    for h in range(DH):
        t = o[:, DK * h:DK * (h + 1)]
        r = lax.rsqrt(jnp.mean(t * t, axis=-1, keepdims=True) + EPS)
        outs.append(t * r * g * _silu(z[:, DK * h:DK * (h + 1)]))
    return jnp.concatenate(outs, axis=1)


def _mix_fwd(x, o_attn, o_dn, z, ga, gd, mod, dn_g, w_branch, w_out):
    B, S, _ = x.shape
    tm = _tile(S, 512)

    def body(x_ref, oa_ref, od_ref, z_ref, ga_ref, gd_ref, mod_ref, g_ref, wb_ref, wo_ref,
             x1_ref, mix_ref, mg_ref, ob_ref):
        oa = oa_ref[...].astype(BF16)
        od = _gated_norm(od_ref[...], z_ref[...], g_ref[...]).astype(BF16)
        ob_ref[0] = oa
        ob_ref[1] = od
        ya = jnp.dot(oa, wb_ref[0:QW, :], preferred_element_type=F32)
        yd = jnp.dot(od, wb_ref[QW:QW + DNW, :], preferred_element_type=F32)
        merged = (_sigmoid(ga_ref[...]) * ya + _sigmoid(gd_ref[...]) * yd).astype(BF16)
        mg_ref[...] = merged
        mix = jnp.dot(merged, wo_ref[...], preferred_element_type=F32)
        mix_ref[...] = mix
        x1_ref[...] = x_ref[...] + mod_ref[2:3, :] * mix

    return pl.pallas_call(
        body, name="mix_fwd", grid=(B, S // tm),
        out_shape=[jax.ShapeDtypeStruct((B, S, D), F32), jax.ShapeDtypeStruct((B, S, D), F32),
                   jax.ShapeDtypeStruct((B, S, D), BF16), jax.ShapeDtypeStruct((B, 2, S, QW), BF16)],
        in_specs=[_rows(tm, D), _rows(tm, QW), _rows(tm, DNW), _rows(tm, DNW), _rows(tm, D), _rows(tm, D),
                  _perb(6, D), _full((1, DK)), _resident(w_branch.shape), _resident(w_out.shape)],
        out_specs=[_rows(tm, D), _rows(tm, D), _rows(tm, D), _stacked(2, tm, QW)],
        compiler_params=_cparams(("parallel", "arbitrary")),
    )(x, o_attn, o_dn, z, ga, gd, mod, dn_g, w_branch, w_out)


def _mix_bwd(dx1, mix, o_attn, o_dn, z, ga, gd, mod, dn_g, w_branch, w_out):
    B, S, _ = dx1.shape
    tm = _tile(S)

    def body(dx1_ref, mix_ref, oa_ref, od_ref, z_ref, ga_ref, gd_ref, mod_ref, g_ref, wb_ref, wo_ref,
             dmix_ref, dyo_ref, dga_ref, dgd_ref, dz_ref, doa_ref, dod_ref, dgate_ref, dg_ref):
        b, i = pl.program_id(0), pl.program_id(1)
        dx1 = dx1_ref[...]
        dmix = (dx1 * mod_ref[2:3, :]).astype(BF16)
        dmix_ref[...] = dmix
        dgate = jnp.sum(dx1 * mix_ref[...], axis=0, keepdims=True)
        dmerged = _dot_nt(dmix, wo_ref[...])
        odn, gn_vjp = jax.vjp(_gated_norm, od_ref[...], z_ref[...], g_ref[...])
        ya = _dot(oa_ref[...], wb_ref[0:QW, :])
        yd = _dot(odn, wb_ref[QW:QW + DNW, :])
        sa, sd = _sigmoid(ga_ref[...]), _sigmoid(gd_ref[...])
        dya = (dmerged * sa).astype(BF16)
        dyd = (dmerged * sd).astype(BF16)
        dyo_ref[0] = dya
        dyo_ref[1] = dyd
        dga_ref[...] = (dmerged * ya * sa * (1.0 - sa)).astype(BF16)
        dgd_ref[...] = (dmerged * yd * sd * (1.0 - sd)).astype(BF16)
        doa_ref[...] = _dot_nt(dya, wb_ref[0:QW, :])
        dodn = _dot_nt(dyd, wb_ref[QW:QW + DNW, :])
        dod, dz, dg = gn_vjp(dodn)
        dod_ref[...] = dod
        dz_ref[...] = dz.astype(BF16)

        @pl.when(i == 0)
        def _():
            dgate_ref[...] = jnp.zeros_like(dgate_ref)

        @pl.when((b == 0) & (i == 0))
        def _():
            dg_ref[...] = jnp.zeros_like(dg_ref)

        dgate_ref[...] += dgate
        dg_ref[...] += dg

    return pl.pallas_call(
        body, name="mix_bwd", grid=(B, S // tm),
        out_shape=[jax.ShapeDtypeStruct((B, S, D), BF16), jax.ShapeDtypeStruct((B, 2, S, D), BF16),
                   jax.ShapeDtypeStruct((B, S, D), BF16), jax.ShapeDtypeStruct((B, S, D), BF16),
                   jax.ShapeDtypeStruct((B, S, DNW), BF16),
                   jax.ShapeDtypeStruct((B, S, QW), F32), jax.ShapeDtypeStruct((B, S, DNW), F32),
                   jax.ShapeDtypeStruct((B, 1, D), F32), jax.ShapeDtypeStruct((1, DK), F32)],
        in_specs=[_rows(tm, D), _rows(tm, D), _rows(tm, QW), _rows(tm, DNW), _rows(tm, DNW), _rows(tm, D),
                  _rows(tm, D), _perb(6, D), _full((1, DK)), _resident(w_branch.shape), _resident(w_out.shape)],
        out_specs=[_rows(tm, D), _stacked(2, tm, D), _rows(tm, D), _rows(tm, D), _rows(tm, DNW),
                   _rows(tm, QW), _rows(tm, DNW), _perb(1, D), _full((1, DK))],
        compiler_params=_cparams(("arbitrary", "arbitrary")),
    )(dx1, mix, o_attn, o_dn, z, ga, gd, mod, dn_g, w_branch, w_out)


GU_SHARD = 2 * FFN // N_DEV
GU_HALF = N_DEV // 2


def _ffn1_fwd(x1, mod, g2, w_gu):
    B, S, _ = x1.shape
    tm = _tile(S)

    def body(x_ref, mod_ref, g_ref, w_ref, h_ref, gate_ref, up_ref, act_ref):
        h = _rms_mod(x_ref[...], g_ref[...], mod_ref[4:5, :], mod_ref[3:4, :]).astype(BF16)
        h_ref[...] = h
        for j in range(GU_HALF):
            gate = jnp.dot(h, w_ref[j], preferred_element_type=F32)
            up = jnp.dot(h, w_ref[GU_HALF + j], preferred_element_type=F32)
            gate_ref[j] = gate
            up_ref[j] = up
            act_ref[j] = (_silu(gate) * up).astype(BF16)

    blk = lambda dt: jax.ShapeDtypeStruct((B, GU_HALF, S, GU_SHARD), dt)
    return pl.pallas_call(
        body, name="ffn1_fwd", grid=(B, S // tm),
        out_shape=[jax.ShapeDtypeStruct((B, S, D), BF16), blk(F32), blk(F32), blk(BF16)],
        in_specs=[_rows(tm, D), _perb(6, D), _full((1, D)), _resident(w_gu.shape)],
        out_specs=[_rows(tm, D)] + [_stacked(GU_HALF, tm, GU_SHARD)] * 3,
        compiler_params=_cparams(("parallel", "arbitrary")),
    )(x1, mod, g2, w_gu)


def _ffn2_fwd(act, x1, target, mod, w_down):
    B, S, _ = x1.shape
    tm = _tile(S, 512)

    def body(a_ref, x_ref, t_ref, mod_ref, w_ref, dy_ref, loss_ref, dgate_ref):
        b, i = pl.program_id(0), pl.program_id(1)
        y = jnp.dot(a_ref[0], w_ref[0], preferred_element_type=F32)
        for j in range(1, GU_HALF):
            y = y + jnp.dot(a_ref[j], w_ref[j], preferred_element_type=F32)
        err = x_ref[...] + mod_ref[5:6, :] * y - t_ref[...]
        dy = err * (1.0 / D)
        dy_ref[...] = dy

        @pl.when((b == 0) & (i == 0))
        def _():
            loss_ref[...] = jnp.zeros_like(loss_ref)

        @pl.when(i == 0)
        def _():
            dgate_ref[...] = jnp.zeros_like(dgate_ref)

        loss_ref[...] += (0.5 / D) * jnp.sum(err * err)
        dgate_ref[...] += jnp.sum(dy * y, axis=0, keepdims=True)

    return pl.pallas_call(
        body, name="ffn2_fwd", grid=(B, S // tm),
        out_shape=[jax.ShapeDtypeStruct((B, S, D), F32), jax.ShapeDtypeStruct((1, 128), F32),
                   jax.ShapeDtypeStruct((B, 1, D), F32)],
        in_specs=[_stacked(GU_HALF, tm, GU_SHARD), _rows(tm, D), _rows(tm, D), _perb(6, D), _resident(w_down.shape)],
        out_specs=[_rows(tm, D), _full((1, 128)), _perb(1, D)],
        compiler_params=_cparams(("arbitrary", "arbitrary")),
    )(act, x1, target, mod, w_down)


def _ffn2_bwd(dy, gate, up, mod, w_down):
    B, S, _ = dy.shape
    tm = _tile(S)

    def body(dy_ref, gate_ref, up_ref, mod_ref, w_ref, dgu_ref, dyg_ref):
        dyg = (dy_ref[...] * mod_ref[5:6, :]).astype(BF16)
        dyg_ref[...] = dyg
        for j in range(GU_HALF):
            dact = _dot_nt(dyg, w_ref[j])
            gate, up = gate_ref[j], up_ref[j]
            sg = _sigmoid(gate)
            dgu_ref[j] = (dact * up * (sg * (1.0 + gate * (1.0 - sg)))).astype(BF16)
            dgu_ref[GU_HALF + j] = (dact * (gate * sg)).astype(BF16)

    return pl.pallas_call(
        body, name="ffn2_bwd", grid=(B, S // tm),
        out_shape=[jax.ShapeDtypeStruct((B, N_DEV, S, GU_SHARD), BF16), jax.ShapeDtypeStruct((B, S, D), BF16)],
        in_specs=[_rows(tm, D), _stacked(GU_HALF, tm, GU_SHARD), _stacked(GU_HALF, tm, GU_SHARD), _perb(6, D),
                  _resident(w_down.shape)],
        out_specs=[_stacked(N_DEV, tm, GU_SHARD), _rows(tm, D)],
        compiler_params=_cparams(("parallel", "arbitrary")),
    )(dy, gate, up, mod, w_down)


def _ffn1_bwd(dgu, x1, dy, mod, g2, w_gu):
    B, S, _ = x1.shape
    tm = _tile(S, 512)

    def body(dgu_ref, x_ref, dy_ref, mod_ref, g_ref, w_ref, dx1_ref, dg_ref, dsc_ref, dsh_ref):
        b, i = pl.program_id(0), pl.program_id(1)
        dh = _dot_nt(dgu_ref[0], w_ref[0])
        for j in range(1, N_DEV):
            dh = dh + _dot_nt(dgu_ref[j], w_ref[j])
        _, vjp = jax.vjp(_rms_mod, x_ref[...], g_ref[...], mod_ref[4:5, :], mod_ref[3:4, :])
        dx, dg, dsc, dsh = vjp(dh)
        dx1_ref[...] = dy_ref[...] + dx

        @pl.when((b == 0) & (i == 0))
        def _():
            dg_ref[...] = jnp.zeros_like(dg_ref)

        @pl.when(i == 0)
        def _():
            dsc_ref[...] = jnp.zeros_like(dsc_ref)
            dsh_ref[...] = jnp.zeros_like(dsh_ref)

        dg_ref[...] += dg
        dsc_ref[...] += dsc
        dsh_ref[...] += dsh

    return pl.pallas_call(
        body, name="ffn1_bwd", grid=(B, S // tm),
        out_shape=[jax.ShapeDtypeStruct((B, S, D), F32), jax.ShapeDtypeStruct((1, D), F32),
                   jax.ShapeDtypeStruct((B, 1, D), F32), jax.ShapeDtypeStruct((B, 1, D), F32)],
        in_specs=[_stacked(N_DEV, tm, GU_SHARD), _rows(tm, D), _rows(tm, D), _perb(6, D), _full((1, D)),
                  _resident(w_gu.shape)],
        out_specs=[_rows(tm, D), _full((1, D)), _perb(1, D), _perb(1, D)],
        compiler_params=_cparams(("arbitrary", "arbitrary")),
    )(dgu, x1, dy, mod, g2, w_gu)


def _adamw(w, g, m, v, name):
    def body(w_ref, g_ref, m_ref, v_ref, d_ref, nm_ref, nv_ref):
        g = g_ref[...]
        m = B1 * m_ref[...] + (1.0 - B1) * g
        v = B2 * v_ref[...] + (1.0 - B2) * (g * g)
        nm_ref[...] = m
        nv_ref[...] = v
        m_hat = m / (1.0 - B1 ** STEP)
        v_hat = v / (1.0 - B2 ** STEP)
        d_ref[...] = -LR * (m_hat / (jnp.sqrt(v_hat) + AEPS) + WD * w_ref[...])

    sd = jax.ShapeDtypeStruct(w.shape, F32)
    return pl.pallas_call(body, name=name, out_shape=(sd, sd, sd), compiler_params=_cparams())(w, g, m, v)


def kernel(x, c, positions, ada_w, ada_b, norm1_g, w_in, conv_w, q_norm_g, k_norm_g, sinks, a_log, dt_bias, dn_norm_g, w_branch, w_out, norm2_g, w_gate_up, w_down, loss_target, m_ada_w, m_ada_b, m_norm1_g, m_w_in, m_conv_w, m_q_norm_g, m_k_norm_g, m_sinks, m_a_log, m_dt_bias, m_dn_norm_g, m_w_branch, m_w_out, m_norm2_g, m_w_gate_up, m_w_down, v_ada_w, v_ada_b, v_norm1_g, v_w_in, v_conv_w, v_q_norm_g, v_k_norm_g, v_sinks, v_a_log, v_dt_bias, v_dn_norm_g, v_w_branch, v_w_out, v_norm2_g, v_w_gate_up, v_w_down):
    B, S, _ = x.shape
    me = 4 * lax.axis_index("x") + 2 * lax.axis_index("y") + lax.axis_index("c")

    shards = [w[0].astype(BF16) for w in (w_in, w_branch, w_out, w_gate_up, w_down)]
    in_sems, in_srcs, in_lands, in_token = _copies_start(shards[:1], [_place_own(shards[0], me)], False, c, "gather_in_start")

    c_all = _all_gather_small(c + in_token[0, 0], "gather_c").reshape(N_DEV * B, D)
    ncol = 6 * D // N_DEV
    mod_cols, cond_all = _ada_fwd(c_all, ada_w[0], lax.dynamic_slice(ada_b, (0, me * ncol), (1, ncol)))
    mod_all = _all_gather_small(mod_cols, "gather_mod").transpose(1, 0, 2).reshape(N_DEV * B, 6 * D)
    mod = lax.dynamic_slice(mod_all, (me * B, 0), (B, 6 * D)).reshape(B, 6, D)
    conv2 = conv_w.reshape(CONV, CONVW // N_DEV)
    conv_all = _all_gather_small(conv2, "gather_conv").transpose(1, 0, 2).reshape(CONV, CONVW)

    w_sems, w_srcs, w_lands, w_token = _copies_start(shards[1:], [_place_own(s, me) for s in shards[1:]], False,
                                                    (mod, conv_all), "gather_rest_start")

    (w_in_b,) = _copies_wait(in_sems, in_srcs, in_lands, mod, "gather_wait_in")
    h1, aq, akv, dnx, ba, z, ga, gd = _inproj_fwd(x, mod, norm1_g + w_token[0, 0], w_in_b)
    invf, mean_q, mean_k = _attn_consts()
    rope_cos, rope_sin = _rope_tables(positions.reshape(B, S, 1), invf)
    o_attn = _attn_fwd(aq, akv, rope_cos, rope_sin, q_norm_g, k_norm_g, sinks, mean_q, mean_k)
    cq = _conv_fwd(dnx, conv_all)
    dn_u, dn_w, dn_qd, dn_kd, dn_a, dn_t, dn_cd = _dn_prep_fwd(cq, ba, a_log, dt_bias)
    o_dn, states = _dn_seq_fwd(dn_u, dn_w, dn_qd, dn_kd, dn_a, dn_cd)
    w_branch_g, w_out_g, w_gu_b, w_down_g = _copies_wait(w_sems, w_srcs, w_lands, o_dn, "gather_wait_rest")
    w_branch_f = w_branch_g.reshape(D, D)
    w_out_f = w_out_g.reshape(D, D)
    w_down_b = w_down_g.reshape(GU_HALF, GU_SHARD, D)
    x1, mix, merged, ob = _mix_fwd(x, o_attn, o_dn, z, ga, gd, mod, dn_norm_g, w_branch_f, w_out_f)
    h2, gate, up, act = _ffn1_fwd(x1, mod, norm2_g, w_gu_b)
    dy, loss_part, d_gate2 = _ffn2_fwd(act, x1, loss_target, mod, w_down_b)
    loss = lax.psum(loss_part[0, 0], ("x", "y", "c"))

    one = lambda t: t.reshape(B, 1, S, t.shape[-1])
    dgu, dyg = _ffn2_bwd(dy, gate, up, mod, w_down_b)
    g_w_down = _wgrad(act, one(dyg), "wgrad_down")
    dx1, d_n2g, d_scale2, d_shift2 = _ffn1_bwd(dgu, x1, dy, mod, norm2_g, w_gu_b)
    g_w_gu = _wgrad(one(h2), dgu, "wgrad_gate_up")
    ffn = _exchange_start([g_w_gu, g_w_down.reshape(N_DEV, FFN // N_DEV, D)], me, dx1, "exchange_ffn_start")
    dmix, dyo, dga, dgd, dz, d_oa, d_od, d_gate1, d_dng = _mix_bwd(
        dx1, mix, o_attn, o_dn, z, ga, gd, mod, dn_norm_g + ffn[3][0, 0], w_branch_f, w_out_f)
    d_dn = _dn_seq_bwd(dn_u, dn_w, dn_qd, dn_kd, dn_a, dn_cd, states, d_od)
    dcq, dba, d_alog, d_dtb = _dn_prep_bwd(cq, ba, a_log, dt_bias, dn_t, *d_dn)
    ddnx, d_conv = _conv_bwd(dnx, conv_all, dcq)
    daq, dakv, d_qg, d_kg, d_sinks = _attn_bwd(aq, akv, rope_cos, rope_sin, q_norm_g, k_norm_g, sinks, mean_q, mean_k, d_oa)
    dps = [daq, dakv, ddnx, dba, dz, dga, dgd]
    dblk, grad_x, d_n1g, d_scale1, d_shift1 = _inproj_bwd(x, mod, norm1_g, dx1, dps, w_in_b)

    dmod = jnp.concatenate([d_shift1, d_scale1, d_gate1, d_shift2, d_scale2, d_gate2], axis=2).reshape(B, 6 * D)
    small = jnp.concatenate([d_n1g, d_qg, d_kg, d_sinks, d_alog, d_dtb, d_dng, d_n2g, d_conv.reshape(1, CONV * CONVW)], axis=1)
    nsm = small.shape[1]
    width = -(-max(6 * D, nsm) // 128) * 128
    rows = jnp.concatenate([jnp.pad(dmod, ((0, 0), (0, width - 6 * D))), jnp.pad(small, ((0, 8 - B - 1), (0, width - nsm)))], axis=0)
    rows_all = _all_gather_small(rows, "gather_small")
    dmod_all = rows_all[:, 0:B, 0:6 * D].reshape(N_DEV * B, 6 * D)
    dmod_cols = lax.dynamic_slice(dmod_all, (0, me * ncol), (N_DEV * B, ncol))
    grad_ada_w, grad_ada_b, small_sum = _ada_bwd(cond_all, dmod_all, dmod_cols, rows_all[:, B, :])
    sizes = [D, HD, HD, HQ, DH, DH, DK, D]
    so = np.cumsum([0] + sizes)
    g_n1, g_qg, g_kg, g_sk, g_al, g_dt, g_dn, g_n2 = [small_sum[:, so[i]:so[i + 1]] for i in range(8)]
    g_conv_all = small_sum[:, so[8]:so[8] + CONV * CONVW].reshape(CONV, N_DEV, CONVW // N_DEV)
    grad_conv = lax.dynamic_slice(g_conv_all, (0, me, 0), (CONV, 1, CONVW // N_DEV)).reshape(CONV, CONVW // N_DEV)

    g_w_in = _wgrad(one(h1), dblk, "wgrad_in", after=small_sum)
    proj = _exchange_start([g_w_in], me, small_sum, "exchange_in_start")
    g_w_out = _wgrad(one(merged), one(dmix), "wgrad_out", after=proj[3])
    g_w_branch = _wgrad(ob, dyo, "wgrad_branch", after=proj[3])
    mixer = _exchange_start([g_w_branch.reshape(N_DEV, D // N_DEV, D), g_w_out.reshape(N_DEV, D // N_DEV, D)], me,
                            proj[3], "exchange_mix_start")

    grad_w_gu, grad_w_down = [_sum_blocks(r, "sum_grads_" + nm) for r, nm in zip(
        _copies_wait(*ffn[:3], mixer[3], "exchange_ffn_wait"), ["gate_up", "down"])]
    (grad_w_in,) = [_sum_blocks(r, "sum_grads_in") for r in _copies_wait(*proj[:3], grad_w_gu, "exchange_in_wait")]
    grad_w_branch, grad_w_out = [_sum_blocks(r, "sum_grads_" + nm) for r, nm in zip(
        _copies_wait(*mixer[:3], grad_w_in, "exchange_mix_wait"), ["branch", "out"])]

    big = [(ada_w, grad_ada_w.reshape(ada_w.shape), m_ada_w, v_ada_w), (w_in, grad_w_in, m_w_in, v_w_in),
           (w_branch, grad_w_branch, m_w_branch, v_w_branch), (w_out, grad_w_out, m_w_out, v_w_out),
           (w_gate_up, grad_w_gu, m_w_gate_up, v_w_gate_up), (w_down, grad_w_down, m_w_down, v_w_down)]
    upd = {}
    for nm, (w, g, m, v) in zip(["ada_w", "w_in", "w_branch", "w_out", "w_gate_up", "w_down"], big):
        upd[nm] = _adamw(w, g, m, v, "adamw_" + nm)
    small_names = ["ada_b", "norm1_g", "q_norm_g", "k_norm_g", "sinks", "a_log", "dt_bias", "dn_norm_g", "norm2_g", "conv_w"]
    small_w = [ada_b, norm1_g, q_norm_g, k_norm_g, sinks, a_log, dt_bias, dn_norm_g, norm2_g, conv_w]
    small_g = [grad_ada_b, g_n1, g_qg, g_kg, g_sk, g_al, g_dt, g_dn, g_n2, grad_conv]
    small_m = [m_ada_b, m_norm1_g, m_q_norm_g, m_k_norm_g, m_sinks, m_a_log, m_dt_bias, m_dn_norm_g, m_norm2_g, m_conv_w]
    small_v = [v_ada_b, v_norm1_g, v_q_norm_g, v_k_norm_g, v_sinks, v_a_log, v_dt_bias, v_dn_norm_g, v_norm2_g, v_conv_w]
    cat = lambda arrs: jnp.concatenate([a.reshape(1, -1) for a in arrs], axis=1)
    res = _adamw(cat(small_w), cat(small_g), cat(small_m), cat(small_v), "adamw_small")
    po = np.cumsum([0] + [int(np.prod(w.shape)) for w in small_w])
    grads = {}
    for i, nm in enumerate(small_names):
        upd[nm] = tuple(r[:, po[i]:po[i + 1]].reshape(small_w[i].shape) for r in res)
        grads[nm] = small_g[i].reshape(small_w[i].shape)
    grads.update(ada_w=grad_ada_w.reshape(ada_w.shape), w_in=grad_w_in, w_branch=grad_w_branch, w_out=grad_w_out,
                 w_gate_up=grad_w_gu, w_down=grad_w_down)

    order = ["ada_w", "ada_b", "norm1_g", "w_in", "conv_w", "q_norm_g", "k_norm_g", "sinks", "a_log", "dt_bias",
             "dn_norm_g", "w_branch", "w_out", "norm2_g", "w_gate_up", "w_down"]
    return (loss, grad_x, *[grads[n] for n in order], *[upd[n][0] for n in order],
            *[upd[n][1] for n in order], *[upd[n][2] for n in order])
```

```python
import functools

import numpy as np
import jax
import jax.numpy as jnp
from jax import lax
from jax.experimental import pallas as pl
from jax.experimental.pallas import tpu as pltpu

F32 = jnp.float32
BF16 = jnp.bfloat16
HI = lax.Precision.HIGHEST

N_DEV = 8
D = 1024
HQ, HKV, HD = 8, 2, 64
GRP = HQ // HKV
BLK = 128
ROT = HD // 4
THETA = 500000.0
QW, KVW = HQ * HD, HKV * HD
DH, DK = 4, 128
CH = 64
DNW = DH * DK
CONV = 4
CONVW = 3 * DNW
FFN = 2816
EPS = 1e-6
IN_W = QW + 2 * KVW + CONVW + 2 * DH + DNW + 2 * D

LR, B1, B2, AEPS, WD, STEP = 0.001, 0.9, 0.999, 1e-08, 0.01, 10

VMEM_LIMIT = 56 * 1024 * 1024
MESH = pl.DeviceIdType.MESH


def _cparams(sem=None, vmem=VMEM_LIMIT):
    return pltpu.CompilerParams(dimension_semantics=sem, vmem_limit_bytes=vmem)


def _full(shape):
    n = len(shape)
    return pl.BlockSpec(shape, lambda *_: (0,) * n)


def _resident(shape):
    n = len(shape)
    return pl.BlockSpec(shape, lambda *_: (0,) * n, pipeline_mode=pl.Buffered(1))


def _rows(tm, w):
    return pl.BlockSpec((None, tm, w), lambda b, i: (b, i, 0))


def _stacked(n, tm, w):
    return pl.BlockSpec((None, n, tm, w), lambda b, i: (b, 0, i, 0))


def _perb(r, w):
    return pl.BlockSpec((None, r, w), lambda b, i: (b, 0, 0))


def _dot(a, b):
    return jnp.dot(a.astype(BF16), b.astype(BF16), preferred_element_type=F32)


def _dot_nt(a, b):
    return lax.dot_general(a.astype(BF16), b.astype(BF16), (((1,), (1,)), ((), ())), preferred_element_type=F32)


def _dot_tn(a, b):
    return lax.dot_general(a.astype(BF16), b.astype(BF16), (((0,), (0,)), ((), ())), preferred_element_type=F32)


def _dot_hi(a, b):
    return jnp.dot(a, b, preferred_element_type=F32, precision=HI)


def _sigmoid(x):
    return jax.nn.sigmoid(x)


def _silu(x):
    return x * jax.nn.sigmoid(x)


def _rms_mod(x, g, scale, shift):
    r = lax.rsqrt(jnp.mean(x * x, axis=-1, keepdims=True) + EPS)
    return (x * r * g) * (1.0 + scale) + shift


def _tile(S, rows=256):
    return min(rows, S)


def _peer(x, y, c, k):
    px = 1 - x if (k >> 2) & 1 else x
    py = 1 - y if (k >> 1) & 1 else y
    pc = 1 - c if k & 1 else c
    return px, py, pc


def _all_gather_small(v, name):
    r, n = v.shape

    def body(v_ref, out_ref, send_sems, recv_sems, local_sem):
        x, y, c = lax.axis_index("x"), lax.axis_index("y"), lax.axis_index("c")
        me = 4 * x + 2 * y + c
        mine = pltpu.make_async_copy(v_ref, out_ref.at[me], local_sem)
        mine.start()
        sends = []
        for k in range(1, N_DEV):
            cp = pltpu.make_async_remote_copy(
                src_ref=v_ref, dst_ref=out_ref.at[me], send_sem=send_sems.at[k - 1], recv_sem=recv_sems.at[k - 1],
                device_id=_peer(x, y, c, k), device_id_type=MESH)
            cp.start()
            sends.append(cp)
        for k in range(1, N_DEV):
            px, py, pc = _peer(x, y, c, k)
            pltpu.make_async_remote_copy(
                src_ref=v_ref, dst_ref=out_ref.at[4 * px + 2 * py + pc], send_sem=send_sems.at[k - 1],
                recv_sem=recv_sems.at[k - 1], device_id=(px, py, pc), device_id_type=MESH).wait_recv()
        for cp in sends:
            cp.wait_send()
        mine.wait()

    return pl.pallas_call(
        body, name=name,
        out_shape=jax.ShapeDtypeStruct((N_DEV, r, n), v.dtype),
        in_specs=[pl.BlockSpec(memory_space=pltpu.VMEM)],
        out_specs=pl.BlockSpec(memory_space=pltpu.VMEM),
        scratch_shapes=[pltpu.SemaphoreType.DMA((N_DEV - 1,)), pltpu.SemaphoreType.DMA((N_DEV - 1,)), pltpu.SemaphoreType.DMA],
    )(v)


def _all_gather_big(vs, name):
    na = len(vs)

    def body(*refs):
        v_refs, out_refs = refs[:na], refs[na:2 * na]
        send_sems, recv_sems, local_sems = refs[2 * na:]
        x, y, c = lax.axis_index("x"), lax.axis_index("y"), lax.axis_index("c")
        me, sibling = (x, y, c), (x, y, 1 - c)
        chips = [(1 - x, y), (x, 1 - y), (1 - x, 1 - y)]

        def rows(a, px, py, pc):
            return out_refs[a].at[4 * px + 2 * py + pc]

        def copy(a, k, block, to, src=None):
            return pltpu.make_async_remote_copy(
                src_ref=rows(a, *block) if src is None else src, dst_ref=rows(a, *block),
                send_sem=send_sems.at[7 * a + k], recv_sem=recv_sems.at[7 * a + k], device_id=to, device_id_type=MESH)

        mine = [pltpu.make_async_copy(v_refs[a], rows(a, *me), local_sems.at[a]) for a in range(na)]
        for cp in mine:
            cp.start()
        first = []
        for a in range(na):
            first.append(copy(a, 0, me, sibling, src=v_refs[a]))
            first += [copy(a, 1 + j, me, (*chip, c), src=v_refs[a]) for j, chip in enumerate(chips)]
        for cp in first:
            cp.start()
        passed = []
        for j, chip in enumerate(chips):
            for a in range(na):
                copy(a, 1 + j, (*chip, c), me).wait_recv()
                forward = copy(a, 4 + j, (*chip, c), sibling)
                forward.start()
                passed.append(forward)
        for a in range(na):
            copy(a, 0, sibling, me).wait_recv()
            for j, chip in enumerate(chips):
                copy(a, 4 + j, (*chip, 1 - c), me).wait_recv()
        for cp in first + passed:
            cp.wait_send()
        for cp in mine:
            cp.wait()

    return pl.pallas_call(
        body, name=name,
        out_shape=[jax.ShapeDtypeStruct((N_DEV,) + v.shape, v.dtype) for v in vs],
        in_specs=[pl.BlockSpec(memory_space=pl.ANY)] * na,
        out_specs=[pl.BlockSpec(memory_space=pl.ANY)] * na,
        scratch_shapes=[pltpu.SemaphoreType.DMA((7 * na,)), pltpu.SemaphoreType.DMA((7 * na,)),
                        pltpu.SemaphoreType.DMA((na,))],
    )(*vs)


def _exchange_blocks(gs, name):
    na = len(gs)

    def body(*refs):
        g_refs, out_refs = refs[:na], refs[na:2 * na]
        send_sems, recv_sems, local_sems = refs[2 * na:]
        x, y, c = lax.axis_index("x"), lax.axis_index("y"), lax.axis_index("c")
        me = 4 * x + 2 * y + c
        mine = [pltpu.make_async_copy(g_refs[a].at[me], out_refs[a].at[me], local_sems.at[a]) for a in range(na)]
        for cp in mine:
            cp.start()
        sends = []
        for k in range(1, N_DEV):
            px, py, pc = _peer(x, y, c, k)
            for a in range(na):
                cp = pltpu.make_async_remote_copy(
                    src_ref=g_refs[a].at[4 * px + 2 * py + pc], dst_ref=out_refs[a].at[me],
                    send_sem=send_sems.at[7 * a + k - 1], recv_sem=recv_sems.at[7 * a + k - 1],
                    device_id=(px, py, pc), device_id_type=MESH)
                cp.start()
                sends.append(cp)
        for k in range(1, N_DEV):
            px, py, pc = _peer(x, y, c, k)
            for a in range(na):
                pltpu.make_async_remote_copy(
                    src_ref=g_refs[a].at[me], dst_ref=out_refs[a].at[4 * px + 2 * py + pc],
                    send_sem=send_sems.at[7 * a + k - 1], recv_sem=recv_sems.at[7 * a + k - 1],
                    device_id=(px, py, pc), device_id_type=MESH).wait_recv()
        for cp in sends:
            cp.wait_send()
        for cp in mine:
            cp.wait()

    return pl.pallas_call(
        body, name=name,
        out_shape=[jax.ShapeDtypeStruct(g.shape, g.dtype) for g in gs],
        in_specs=[pl.BlockSpec(memory_space=pl.ANY)] * na,
        out_specs=[pl.BlockSpec(memory_space=pl.ANY)] * na,
        scratch_shapes=[pltpu.SemaphoreType.DMA((7 * na,)), pltpu.SemaphoreType.DMA((7 * na,)),
                        pltpu.SemaphoreType.DMA((na,))],
    )(*gs)


_HBM = pl.BlockSpec(memory_space=pltpu.HBM)
_SEM = pl.BlockSpec(memory_space=pltpu.SEMAPHORE)
_EFFECT = pltpu.SideEffectType.DATAFLOW_SIDE_EFFECTING


def _place_own(block, me):
    land = lax.empty((N_DEV,) + block.shape, block.dtype)
    return lax.dynamic_update_slice(land, block[None], (me,) + (0,) * block.ndim)


def _copies_start(srcs, lands, scatter, after, name):
    na = len(srcs)
    afters = tuple(after) if isinstance(after, (tuple, list)) else (after,)

    def body(*refs):
        src_refs, land_refs = refs[:na], refs[na:2 * na]
        sems = refs[2 * na + len(afters):4 * na + len(afters)]
        token = refs[-1]
        x, y, c = lax.axis_index("x"), lax.axis_index("y"), lax.axis_index("c")
        me = 4 * x + 2 * y + c
        for a in range(na):
            for k in range(1, N_DEV):
                px, py, pc = _peer(x, y, c, k)
                src = src_refs[a].at[4 * px + 2 * py + pc] if scatter else src_refs[a]
                pltpu.make_async_remote_copy(
                    src_ref=src, dst_ref=land_refs[a].at[me], send_sem=sems[2 * a], recv_sem=sems[2 * a + 1],
                    device_id=(px, py, pc), device_id_type=MESH).start()
        token[...] = jnp.zeros_like(token)

    hbm = lambda t: pltpu.HBM(t.shape, t.dtype)
    out = pl.pallas_call(
        body, name=name,
        out_shape=tuple([pltpu.SemaphoreType.DMA(())] * (2 * na) + [hbm(t) for t in srcs] + [hbm(t) for t in lands]
                        + [jax.ShapeDtypeStruct((8, 128), F32)]),
        in_specs=[_HBM] * (2 * na) + [pl.BlockSpec(memory_space=pl.ANY)] * len(afters),
        out_specs=tuple([_SEM] * (2 * na) + [_HBM] * (2 * na) + [pl.BlockSpec(memory_space=pltpu.VMEM)]),
        input_output_aliases={i: 2 * na + i for i in range(2 * na)},
        compiler_params=pltpu.CompilerParams(has_side_effects=_EFFECT),
    )(*[pltpu.with_memory_space_constraint(t, pltpu.HBM) for t in list(srcs) + list(lands)], *afters)
    return out[:2 * na], out[2 * na:3 * na], out[3 * na:4 * na], out[-1]


def _exchange_start(gs, me, after, name):
    own = [lax.dynamic_index_in_dim(g, me, 0, keepdims=False) for g in gs]
    return _copies_start(gs, [_place_own(o, me) for o in own], True, after, name)


def _copies_wait(sems, srcs, lands, after, name):
    na = len(srcs)

    def body(*refs):
        land_refs = refs[na:2 * na]
        sem_refs = refs[2 * na:4 * na]
        x, y, c = lax.axis_index("x"), lax.axis_index("y"), lax.axis_index("c")
        for a in range(na):
            seven = land_refs[a].at[pl.ds(0, N_DEV - 1)]
            copy = pltpu.make_async_remote_copy(
                src_ref=seven, dst_ref=seven, send_sem=sem_refs[2 * a], recv_sem=sem_refs[2 * a + 1],
                device_id=(x, y, c), device_id_type=MESH)
            copy.wait_send()
            copy.wait_recv()

    hbm = lambda t: pltpu.HBM(t.shape, t.dtype)
    out = pl.pallas_call(
        body, name=name,
        out_shape=tuple([hbm(t) for t in srcs] + [hbm(t) for t in lands]),
        in_specs=[_HBM] * (2 * na) + [_SEM] * (2 * na) + [pl.BlockSpec(memory_space=pl.ANY)],
        out_specs=tuple([_HBM] * (2 * na)),
        input_output_aliases={i: i for i in range(2 * na)},
        compiler_params=pltpu.CompilerParams(has_side_effects=_EFFECT),
    )(*srcs, *lands, *sems, after)
    return out[na:]


def _sum_blocks(g, name):
    _, r, n = g.shape
    tr = 256 if r % 256 == 0 else r

    def body(g_ref, o_ref):
        acc = g_ref[0].astype(F32)
        for d in range(1, N_DEV):
            acc = acc + g_ref[d].astype(F32)
        o_ref[...] = acc

    return pl.pallas_call(
        body, name=name, grid=(r // tr,),
        out_shape=jax.ShapeDtypeStruct((1, r, n), F32),
        in_specs=[pl.BlockSpec((N_DEV, tr, n), lambda i: (0, i, 0))],
        out_specs=pl.BlockSpec((None, tr, n), lambda i: (0, i, 0)),
        compiler_params=_cparams(("arbitrary",)),
    )(g)


def _ada_fwd(c_all, ada_w, ada_b_cols):
    nb, ncol = c_all.shape[0], ada_w.shape[1]

    def body(c_ref, w_ref, b_ref, mod_ref, cond_ref):
        cond = _silu(c_ref[...])
        cond_ref[...] = cond
        mod_ref[...] = _dot_hi(cond, w_ref[...]) + b_ref[...]

    return pl.pallas_call(
        body, name="ada_fwd",
        out_shape=(jax.ShapeDtypeStruct((nb, ncol), F32), jax.ShapeDtypeStruct((nb, D), F32)),
        compiler_params=_cparams(),
    )(c_all, ada_w, ada_b_cols)


def _ada_bwd(cond_all, dmod_all, dmod_cols, smalls):
    ncol, nsm = dmod_cols.shape[1], smalls.shape[1]

    def body(cond_ref, dm_ref, dmc_ref, sm_ref, gw_ref, gb_ref, gs_ref):
        gw_ref[...] = lax.dot_general(cond_ref[...], dmc_ref[...], (((0,), (0,)), ((), ())),
                                      preferred_element_type=F32, precision=HI)
        gb_ref[...] = jnp.sum(dm_ref[...], axis=0, keepdims=True)
        gs_ref[...] = jnp.sum(sm_ref[...], axis=0, keepdims=True)

    return pl.pallas_call(
        body, name="ada_bwd",
        out_shape=(jax.ShapeDtypeStruct((D, ncol), F32), jax.ShapeDtypeStruct((1, 6 * D), F32),
                   jax.ShapeDtypeStruct((1, nsm), F32)),
        compiler_params=_cparams(),
    )(cond_all, dmod_all, dmod_cols, smalls)


IN_CUTS = (0, QW, QW + 2 * KVW, QW + 2 * KVW + CONVW, QW + 2 * KVW + CONVW + 2 * DH,
           QW + 2 * KVW + CONVW + 2 * DH + DNW, QW + 2 * KVW + CONVW + 2 * DH + DNW + D, IN_W)
IN_WIDTHS = tuple(b - a for a, b in zip(IN_CUTS[:-1], IN_CUTS[1:]))
IN_SHARD = IN_W // N_DEV


def _inproj_fwd(x, mod, g1, w_blk):
    B, S, _ = x.shape
    tm = _tile(S)

    def body(x_ref, mod_ref, g_ref, w_ref, h_ref, *o_refs):
        h = _rms_mod(x_ref[...], g_ref[...], mod_ref[1:2, :], mod_ref[0:1, :]).astype(BF16)
        h_ref[...] = h
        full = jnp.concatenate([jnp.dot(h, w_ref[j], preferred_element_type=F32) for j in range(N_DEV)], axis=1)
        for o_ref, lo, hi in zip(o_refs, IN_CUTS[:-1], IN_CUTS[1:]):
            o_ref[...] = full[:, lo:hi]

    return pl.pallas_call(
        body, name="inproj_fwd", grid=(B, S // tm),
        out_shape=[jax.ShapeDtypeStruct((B, S, D), BF16)] + [jax.ShapeDtypeStruct((B, S, w), F32) for w in IN_WIDTHS],
        in_specs=[_rows(tm, D), _perb(6, D), _full((1, D)), _resident(w_blk.shape)],
        out_specs=[_rows(tm, D)] + [_rows(tm, w) for w in IN_WIDTHS],
        compiler_params=_cparams(("parallel", "arbitrary")),
    )(x, mod, g1, w_blk)


def _inproj_bwd(x, mod, g1, dx1, dps, w_blk):
    B, S, _ = x.shape
    tm = _tile(S)
    n = len(dps)

    def body(x_ref, mod_ref, g_ref, dx1_ref, *refs):
        dp_refs, w_ref = refs[:n], refs[n]
        dblk_ref, gx_ref, dg_ref, dsc_ref, dsh_ref = refs[n + 1:]
        b, i = pl.program_id(0), pl.program_id(1)
        full = jnp.concatenate([r[...].astype(F32) for r in dp_refs], axis=1)
        dh = None
        for j in range(N_DEV):
            blk = full[:, IN_SHARD * j:IN_SHARD * (j + 1)].astype(BF16)
            dblk_ref[j] = blk
            t = _dot_nt(blk, w_ref[j])
            dh = t if dh is None else dh + t
        _, vjp = jax.vjp(_rms_mod, x_ref[...], g_ref[...], mod_ref[1:2, :], mod_ref[0:1, :])
        dx, dg, dsc, dsh = vjp(dh)
        gx_ref[...] = dx1_ref[...] + dx

        @pl.when((b == 0) & (i == 0))
        def _():
            dg_ref[...] = jnp.zeros_like(dg_ref)

        @pl.when(i == 0)
        def _():
            dsc_ref[...] = jnp.zeros_like(dsc_ref)
            dsh_ref[...] = jnp.zeros_like(dsh_ref)

        dg_ref[...] += dg
        dsc_ref[...] += dsc
        dsh_ref[...] += dsh

    return pl.pallas_call(
        body, name="inproj_bwd", grid=(B, S // tm),
        out_shape=[jax.ShapeDtypeStruct((B, N_DEV, S, IN_SHARD), BF16), jax.ShapeDtypeStruct((B, S, D), F32),
                   jax.ShapeDtypeStruct((1, D), F32), jax.ShapeDtypeStruct((B, 1, D), F32),
                   jax.ShapeDtypeStruct((B, 1, D), F32)],
        in_specs=[_rows(tm, D), _perb(6, D), _full((1, D)), _rows(tm, D)]
                 + [_rows(tm, w) for w in IN_WIDTHS] + [_resident(w_blk.shape)],
        out_specs=[pl.BlockSpec((None, N_DEV, tm, IN_SHARD), lambda b, i: (b, 0, i, 0)), _rows(tm, D),
                   _full((1, D)), _perb(1, D), _perb(1, D)],
        compiler_params=_cparams(("arbitrary", "arbitrary")),
    )(x, mod, g1, dx1, *dps, w_blk)


def _wgrad(a, b, name, after=None):
    B, na, S, K = a.shape
    nb, N = b.shape[1], b.shape[3]
    G = max(na, nb)
    tm = min(512, S)
    nt = S // tm
    last = B * nt - 1

    def body(a_ref, b_ref, *rest):
        o_ref, acc = rest[-2:]
        t = pl.program_id(1)

        @pl.when(t == 0)
        def _():
            acc[...] = jnp.zeros_like(acc)

        acc[...] += lax.dot_general(a_ref[...], b_ref[...], (((0,), (0,)), ((), ())), preferred_element_type=F32)

        @pl.when(t == last)
        def _():
            o_ref[...] = acc[...].astype(BF16)

    return pl.pallas_call(
        body, name=name, grid=(G, B * nt),
        out_shape=jax.ShapeDtypeStruct((G, K, N), BF16),
        in_specs=[pl.BlockSpec((None, None, tm, K), lambda g, t: (t // nt, g if na > 1 else 0, t % nt, 0)),
                  pl.BlockSpec((None, None, tm, N), lambda g, t: (t // nt, g if nb > 1 else 0, t % nt, 0))]
                 + ([] if after is None else [pl.BlockSpec(memory_space=pl.ANY)]),
        out_specs=pl.BlockSpec((None, K, N), lambda g, t: (g, 0, 0)),
        scratch_shapes=[pltpu.VMEM((K, N), F32)],
        compiler_params=_cparams(("parallel", "arbitrary")),
    )(*((a, b) if after is None else (a, b, after)))


LANES = 128


def _attn_consts():
    inv_freq = THETA ** (-jnp.arange(0, ROT, 2, dtype=F32) / ROT)
    head = jnp.concatenate([inv_freq, inv_freq, jnp.zeros((HD - ROT,), F32)])
    invf = jnp.tile(head, LANES // HD)[None, :]
    mean_of = lambda w: jnp.asarray(np.kron(np.eye(w // HD), np.full((HD, HD), 1.0 / HD)), BF16)
    return invf, mean_of(QW), mean_of(KVW)


def _rope_tables(pos, invf):
    B, S, _ = pos.shape
    tr = min(1024, S)

    def body(p_ref, f_ref, c_ref, s_ref):
        ang = p_ref[...].astype(F32) * f_ref[...]
        c_ref[...] = jnp.cos(ang)
        s_ref[...] = jnp.sin(ang)

    sd = jax.ShapeDtypeStruct((B, S, LANES), F32)
    return pl.pallas_call(
        body, name="rope_tables", grid=(B, S // tr), out_shape=[sd, sd],
        in_specs=[_rows(tr, 1), _full((1, LANES))], out_specs=[_rows(tr, LANES), _rows(tr, LANES)],
        compiler_params=_cparams(("parallel", "parallel")),
    )(pos, invf)


def _rope_expand(cos, sin, reps):
    lane = lax.broadcasted_iota(jnp.int32, cos.shape, 1) % HD
    sa = jnp.where((lane >= ROT // 2) & (lane < ROT), sin, 0.0)
    sb = jnp.where(lane < ROT // 2, -sin, 0.0)
    rep = lambda t: jnp.concatenate([t] * reps, axis=1) if reps > 1 else t
    return rep(cos), rep(sa), rep(sb)


@jax.custom_vjp
def _rope(t, cos, sa, sb):
    w = t.shape[1]
    return t * cos + pltpu.roll(t, ROT // 2, 1) * sa + pltpu.roll(t, w - ROT // 2, 1) * sb


def _rope_fwd(t, cos, sa, sb):
    return _rope(t, cos, sa, sb), (cos, sa, sb)


def _rope_bwd(res, d):
    cos, sa, sb = res
    w = d.shape[1]
    dt = d * cos + pltpu.roll(d * sa, w - ROT // 2, 1) + pltpu.roll(d * sb, ROT // 2, 1)
    return dt, jnp.zeros_like(cos), jnp.zeros_like(sa), jnp.zeros_like(sb)


_rope.defvjp(_rope_fwd, _rope_bwd)


def _head_norm(t, g, mean_of):
    hi, lo = _split(t * t)
    ms = jnp.dot(hi, mean_of, preferred_element_type=F32) + jnp.dot(lo, mean_of, preferred_element_type=F32)
    return t * lax.rsqrt(ms + EPS) * g


def _attn_block(q, kvp, kvc, qg, kg, sinks, tq, tk, mq, mk, valid):
    qn = _rope(_head_norm(q, jnp.concatenate([qg] * HQ, axis=1), mq), *tq) * (HD ** -0.5)
    kv = jnp.concatenate([kvp, kvc], axis=0)
    kn = _rope(_head_norm(kv[:, 0:KVW], jnp.concatenate([kg] * HKV, axis=1), mk), *tk)
    per_tile = LANES // HD
    vT = jnp.transpose(kv[:, KVW:2 * KVW])
    qT = [jnp.transpose(qn[:, LANES * t:LANES * (t + 1)]) for t in range(QW // LANES)]
    head_T = lambda h: qT[h // per_tile][HD * (h % per_tile):HD * (h % per_tile + 1), :]
    none = jnp.zeros((HD, GRP * BLK), F32)
    o_T = []
    for j in range(HKV):
        q4T = jnp.concatenate([head_T(GRP * j + i) for i in range(GRP)], axis=1)
        sT = _dot(kn, jnp.concatenate([q4T, none] if j == 0 else [none, q4T], axis=0))
        sT = jnp.where(valid, sT, -1e30)
        sink = jnp.concatenate([jnp.broadcast_to(sinks[:, GRP * j + i:GRP * j + i + 1], (1, BLK)) for i in range(GRP)], axis=1)
        m = lax.stop_gradient(jnp.maximum(jnp.max(sT, axis=0, keepdims=True), sink))
        pT = jnp.exp(sT - m)
        den = jnp.sum(pT, axis=0, keepdims=True) + jnp.exp(sink - m)
        oT = _dot(vT[HD * j:HD * (j + 1), :], pT) * (1.0 / den)
        o_T += [oT[:, BLK * i:BLK * (i + 1)] for i in range(GRP)]
    return jnp.concatenate([jnp.transpose(jnp.concatenate(o_T[per_tile * t:per_tile * (t + 1)], axis=0))
                            for t in range(QW // LANES)], axis=1)


def _attn_tables(cp_ref, cc_ref, sp_ref, sc_ref, n):
    tq = _rope_expand(cc_ref[...], sc_ref[...], QW // LANES)
    tk = _rope_expand(jnp.concatenate([cp_ref[...], cc_ref[...]], axis=0),
                      jnp.concatenate([sp_ref[...], sc_ref[...]], axis=0), KVW // LANES)
    qi = lax.broadcasted_iota(jnp.int32, (2 * BLK, GRP * BLK), 1) % BLK + BLK
    kj = lax.broadcasted_iota(jnp.int32, (2 * BLK, GRP * BLK), 0)
    dist = qi - kj
    valid = (dist >= 0) & (dist < BLK) & ((kj >= BLK) | (n > 0))
    return tq, tk, valid


def _attn_fwd(aq, akv, cos, sin, qg, kg, sinks, mq, mk):
    B, S, _ = aq.shape
    nb = S // BLK

    def body(q_ref, kvp_ref, kvc_ref, cp_ref, cc_ref, sp_ref, sc_ref, qg_ref, kg_ref, sk_ref, mq_ref, mk_ref, o_ref):
        tq, tk, valid = _attn_tables(cp_ref, cc_ref, sp_ref, sc_ref, pl.program_id(1))
        o_ref[...] = _attn_block(q_ref[...], kvp_ref[...], kvc_ref[...], qg_ref[...], kg_ref[...], sk_ref[...],
                                 tq, tk, mq_ref[...], mk_ref[...], valid)

    prev = lambda b, n: (b, jnp.maximum(n - 1, 0), 0)
    cur = lambda b, n: (b, n, 0)
    return pl.pallas_call(
        body, name="attn_fwd", grid=(B, nb),
        out_shape=jax.ShapeDtypeStruct((B, S, QW), F32),
        in_specs=[pl.BlockSpec((None, BLK, QW), cur), pl.BlockSpec((None, BLK, 2 * KVW), prev),
                  pl.BlockSpec((None, BLK, 2 * KVW), cur), pl.BlockSpec((None, BLK, LANES), prev),
                  pl.BlockSpec((None, BLK, LANES), cur), pl.BlockSpec((None, BLK, LANES), prev),
                  pl.BlockSpec((None, BLK, LANES), cur), _full((1, HD)), _full((1, HD)), _full((1, HQ)),
                  _full((QW, QW)), _full((KVW, KVW))],
        out_specs=pl.BlockSpec((None, BLK, QW), cur),
        compiler_params=_cparams(("parallel", "arbitrary")),
    )(aq, akv, akv, cos, cos, sin, sin, qg, kg, sinks, mq, mk)


def _attn_bwd(aq, akv, cos, sin, qg, kg, sinks, mq, mk, do):
    B, S, _ = aq.shape
    nb = S // BLK

    def body(q_ref, kvp_ref, kvc_ref, cp_ref, cc_ref, sp_ref, sc_ref, qg_ref, kg_ref, sk_ref, mq_ref, mk_ref, do_ref,
             dq_ref, dkv_ref, dqg_ref, dkg_ref, dsk_ref, carry):
        b, i = pl.program_id(0), pl.program_id(1)
        tq, tk, valid = _attn_tables(cp_ref, cc_ref, sp_ref, sc_ref, nb - 1 - i)
        fn = functools.partial(_attn_block, tq=tq, tk=tk, mq=mq_ref[...], mk=mk_ref[...], valid=valid)
        _, vjp = jax.vjp(fn, q_ref[...], kvp_ref[...], kvc_ref[...], qg_ref[...], kg_ref[...], sk_ref[...])
        dq, dkvp, dkvc, dqg, dkg, dsk = vjp(do_ref[...])

        @pl.when(i == 0)
        def _():
            carry[...] = jnp.zeros_like(carry)

        @pl.when((b == 0) & (i == 0))
        def _():
            dqg_ref[...] = jnp.zeros_like(dqg_ref)
            dkg_ref[...] = jnp.zeros_like(dkg_ref)
            dsk_ref[...] = jnp.zeros_like(dsk_ref)

        dq_ref[...] = dq.astype(BF16)
        dkv_ref[...] = (dkvc + carry[...]).astype(BF16)
        carry[...] = dkvp
        dqg_ref[...] += dqg
        dkg_ref[...] += dkg
        dsk_ref[...] += dsk

    prev = lambda b, i: (b, jnp.maximum(nb - 2 - i, 0), 0)
    cur = lambda b, i: (b, nb - 1 - i, 0)
    return pl.pallas_call(
        body, name="attn_bwd", grid=(B, nb),
        out_shape=[jax.ShapeDtypeStruct((B, S, QW), BF16), jax.ShapeDtypeStruct((B, S, 2 * KVW), BF16),
                   jax.ShapeDtypeStruct((1, HD), F32), jax.ShapeDtypeStruct((1, HD), F32),
                   jax.ShapeDtypeStruct((1, HQ), F32)],
        in_specs=[pl.BlockSpec((None, BLK, QW), cur), pl.BlockSpec((None, BLK, 2 * KVW), prev),
                  pl.BlockSpec((None, BLK, 2 * KVW), cur), pl.BlockSpec((None, BLK, LANES), prev),
                  pl.BlockSpec((None, BLK, LANES), cur), pl.BlockSpec((None, BLK, LANES), prev),
                  pl.BlockSpec((None, BLK, LANES), cur), _full((1, HD)), _full((1, HD)), _full((1, HQ)),
                  _full((QW, QW)), _full((KVW, KVW)), pl.BlockSpec((None, BLK, QW), cur)],
        out_specs=[pl.BlockSpec((None, BLK, QW), cur), pl.BlockSpec((None, BLK, 2 * KVW), cur),
                   _full((1, HD)), _full((1, HD)), _full((1, HQ))],
        scratch_shapes=[pltpu.VMEM((BLK, 2 * KVW), F32)],
        compiler_params=_cparams(("arbitrary", "arbitrary")),
    )(aq, akv, akv, cos, cos, sin, sin, qg, kg, sinks, mq, mk, do)


def _conv_taps(xe, w, rows):
    y = None
    for j in range(CONV):
        sh = pltpu.roll(xe, CONV - 1 - j, 0)[8:8 + rows, :] if j < CONV - 1 else xe[8:8 + rows, :]
        y = sh * w[j:j + 1, :] if y is None else y + sh * w[j:j + 1, :]
    return y


def _conv_fwd(xin, w):
    B, S, C = xin.shape
    tc = min(512, S)
    r8 = tc // 8

    def body(xp_ref, x_ref, w_ref, o_ref):
        i = pl.program_id(1)
        xp = jnp.where(i > 0, xp_ref[...], 0.0)
        xe = jnp.concatenate([xp, x_ref[...]], axis=0)
        o_ref[...] = _silu(_conv_taps(xe, w_ref[...], tc))

    return pl.pallas_call(
        body, name="conv_fwd", grid=(B, S // tc),
        out_shape=jax.ShapeDtypeStruct((B, S, C), F32),
        in_specs=[pl.BlockSpec((None, 8, C), lambda b, i: (b, jnp.maximum(i * r8 - 1, 0), 0)),
                  _rows(tc, C), _full((CONV, C))],
        out_specs=_rows(tc, C),
        compiler_params=_cparams(("parallel", "arbitrary")),
    )(xin, xin, w)


def _conv_bwd(xin, w, dy):
    B, S, C = xin.shape
    tc = min(512, S)
    r8 = tc // 8
    nt = S // tc

    def body(xp_ref, x_ref, xn_ref, dy_ref, dyn_ref, w_ref, dx_ref, dw_ref):
        b, i = pl.program_id(0), pl.program_id(1)
        w = w_ref[...]
        xp = jnp.where(i > 0, xp_ref[...], 0.0)
        xe = jnp.concatenate([xp, x_ref[...], xn_ref[...]], axis=0)
        pre = _conv_taps(xe, w, tc + 8)
        sg = _sigmoid(pre)
        dyn = jnp.where(i < nt - 1, dyn_ref[...], 0.0)
        dpre = jnp.concatenate([dy_ref[...], dyn], axis=0) * (sg * (1.0 + pre * (1.0 - sg)))
        dx = dpre[0:tc, :] * w[CONV - 1:CONV, :]
        for j in range(CONV - 1):
            dx = dx + pltpu.roll(dpre, tc + 8 - (CONV - 1 - j), 0)[0:tc, :] * w[j:j + 1, :]
        dx_ref[...] = dx.astype(BF16)
        dcur = dpre[0:tc, :]
        xe0 = xe[0:8 + tc, :]
        lane_row = lax.broadcasted_iota(jnp.int32, (CONV, C), 0)
        dw = jnp.zeros((CONV, C), F32)
        for j in range(CONV):
            sh = pltpu.roll(xe0, CONV - 1 - j, 0)[8:8 + tc, :] if j < CONV - 1 else xe0[8:8 + tc, :]
            dw = dw + jnp.where(lane_row == j, jnp.sum(sh * dcur, axis=0, keepdims=True), 0.0)

        @pl.when((b == 0) & (i == 0))
        def _():
            dw_ref[...] = jnp.zeros_like(dw_ref)

        dw_ref[...] += dw

    return pl.pallas_call(
        body, name="conv_bwd", grid=(B, nt),
        out_shape=[jax.ShapeDtypeStruct((B, S, C), BF16), jax.ShapeDtypeStruct((CONV, C), F32)],
        in_specs=[pl.BlockSpec((None, 8, C), lambda b, i: (b, jnp.maximum(i * r8 - 1, 0), 0)),
                  _rows(tc, C),
                  pl.BlockSpec((None, 8, C), lambda b, i: (b, jnp.minimum((i + 1) * r8, S // 8 - 1), 0)),
                  _rows(tc, C),
                  pl.BlockSpec((None, 8, C), lambda b, i: (b, jnp.minimum((i + 1) * r8, S // 8 - 1), 0)),
                  _full((CONV, C))],
        out_specs=[_rows(tc, C), _full((CONV, C))],
        compiler_params=_cparams(("arbitrary", "arbitrary")),
    )(xin, xin, xin, dy, dy, w)


def _softplus(x):
    return jnp.maximum(x, 0.0) + jnp.log1p(jnp.exp(-jnp.abs(x)))


_BMM = (((2,), (1,)), ((0,), (0,)))
_BMM_NT = (((2,), (2,)), ((0,), (0,)))
_BMM_TN = (((1,), (1,)), ((0,), (0,)))


def _bmm(a, b, dims=_BMM):
    return lax.dot_general(a.astype(BF16), b.astype(BF16), dims, preferred_element_type=F32)


def _split(a):
    hi = a.astype(BF16)
    return hi, (a - hi.astype(F32)).astype(BF16)


def _bmm3(a, b, dims=_BMM):
    ah, al = _split(a)
    bh, bl = _split(b)
    d = lambda p, q: lax.dot_general(p, q, dims, preferred_element_type=F32)
    return d(ah, bh) + (d(ah, bl) + d(al, bh))


TRI_BASE = 8


def _tri_inverse(L):
    ii = lax.broadcasted_iota(jnp.int32, (CH, CH), 0)
    jj = lax.broadcasted_iota(jnp.int32, (CH, CH), 1)
    same = lambda size: (ii // size) == (jj // size)
    diag = jnp.where(same(TRI_BASE), L, 0.0)
    X = (ii == jj).astype(F32) - diag
    P = diag
    n = 2
    while n < TRI_BASE:
        P = _bmm3(P, P)
        X = X + _bmm3(X, P)
        n *= 2
    size = TRI_BASE
    while size < CH:
        joint = jnp.where(same(2 * size) & jnp.logical_not(same(size)), L, 0.0)
        X = X - _bmm3(X, _bmm3(joint, X))
        size *= 2
    return X


@jax.custom_vjp
def _tri_inverse_known(L, T):
    return T


def _tri_inverse_known_fwd(L, T):
    return T, T


def _tri_inverse_known_bwd(T, dT):
    return -_bmm3(T, _bmm3(dT, T, _BMM_NT), _BMM_TN), jnp.zeros_like(T)


_tri_inverse_known.defvjp(_tri_inverse_known_fwd, _tri_inverse_known_bwd)


def _cumsum_rows(g):
    n = g.shape[0]
    ii = lax.broadcasted_iota(jnp.int32, (n, CH, CH), 1)
    jj = lax.broadcasted_iota(jnp.int32, (n, CH, CH), 2)
    tri = (ii >= jj).astype(BF16)
    g0 = g.astype(BF16)
    r1 = g - g0.astype(F32)
    g1 = r1.astype(BF16)
    g2 = (r1 - g1.astype(F32)).astype(BF16)
    d = lambda q: lax.dot_general(tri, q, _BMM, preferred_element_type=F32)
    return d(g0) + (d(g1) + d(g2))


def _row_sums(t):
    n, r, w = t.shape
    hi, lo = _split(t.reshape(n * r, w))
    ones = jnp.ones((w, w), BF16)
    s = jnp.dot(hi, ones, preferred_element_type=F32) + jnp.dot(lo, ones, preferred_element_type=F32)
    return s.reshape(n, r, w)


def _dn_prep(t_known, qr, kr, v, a_raw, b_raw, a_log, dt_b):
    n = qr.shape[0]
    ii = lax.broadcasted_iota(jnp.int32, (n, CH, CH), 1)
    jj = lax.broadcasted_iota(jnp.int32, (n, CH, CH), 2)
    incl, strict = ii >= jj, ii > jj
    q = qr * lax.rsqrt(_row_sums(qr * qr) + EPS) * (DK ** -0.5)
    k = kr * lax.rsqrt(_row_sums(kr * kr) + EPS)
    beta = _sigmoid(b_raw)
    g = -jnp.exp(a_log) * _softplus(a_raw + dt_b)
    gcb = _cumsum_rows(jnp.broadcast_to(g, (n, CH, DK)))
    gc = gcb[:, :, 0:1]
    gc_row = jnp.swapaxes(gcb, 1, 2)[:, 0:1, 0:CH]
    decay = jnp.where(incl, jnp.exp(jnp.where(incl, gc - gc_row, 0.0)), 0.0)
    kb = k * beta
    L = jnp.where(strict, _bmm(kb, k, _BMM_NT) * decay, 0.0)
    T = _tri_inverse(L) if t_known is None else _tri_inverse_known(L, t_known)
    eg = jnp.exp(gc)
    u = _bmm(T, v * beta)
    w = _bmm(T, kb * eg)
    a_in = _bmm(q, k, _BMM_NT) * decay
    g_last = gc[:, CH - 1:CH, :]
    return u, w, q * eg, k * jnp.exp(g_last - gc), a_in, jnp.exp(g_last), T


def _dn_step(S0, u, w, qd, kd, a_in, cd):
    r = _bmm(jnp.concatenate([w, qd], axis=1), S0)
    v_new = u - r[:, 0:CH, :]
    o = r[:, CH:2 * CH, :] + _bmm(a_in, v_new)
    S1 = S0 * cd + _bmm(kd, v_new, _BMM_TN)
    return o, S1


def _dn_stack(cq, ba, al, dt, G):
    cols = [[] for _ in range(7)]
    for c in range(G):
        rows = slice(CH * c, CH * (c + 1))
        for h in range(DH):
            parts = (cq[rows, DK * h:DK * (h + 1)], cq[rows, DNW + DK * h:DNW + DK * (h + 1)],
                     cq[rows, 2 * DNW + DK * h:2 * DNW + DK * (h + 1)], ba[rows, DH + h:DH + h + 1],
                     ba[rows, h:h + 1], al[:, h:h + 1], dt[:, h:h + 1])
            for col, p in zip(cols, parts):
                col.append(p)
    return tuple(jnp.stack(col) for col in cols)


def _dn_group(S, want):
    g = want
    while (S // CH) % g:
        g //= 2
    return g


def _dn_prep_fwd(cq, ba, a_log, dt_b):
    B, S, _ = cq.shape
    nc = S // CH
    G = _dn_group(S, 4)

    def body(cq_ref, ba_ref, al_ref, dt_ref, u_ref, w_ref, qd_ref, kd_ref, a_ref, t_ref, cd_ref):
        ops = _dn_stack(cq_ref[...], ba_ref[...], al_ref[...], dt_ref[...], G)
        u, w, qd, kd, a_in, cd, T = _dn_prep(None, *ops)
        lane4 = lax.broadcasted_iota(jnp.int32, (1, DH), 1)
        for c in range(G):
            rows = slice(CH * c, CH * (c + 1))
            cdrow = jnp.zeros((1, DH), F32)
            for h in range(DH):
                n = DH * c + h
                lanes = slice(DK * h, DK * (h + 1))
                u_ref[rows, lanes] = u[n]
                w_ref[rows, lanes] = w[n]
                qd_ref[rows, lanes] = qd[n]
                kd_ref[rows, lanes] = kd[n]
                a_ref[rows, CH * h:CH * (h + 1)] = a_in[n]
                t_ref[rows, CH * h:CH * (h + 1)] = T[n]
                cdrow = cdrow + jnp.where(lane4 == h, cd[n], 0.0)
            cd_ref[c] = cdrow

    wide = jax.ShapeDtypeStruct((B, S, DNW), F32)
    sq = jax.ShapeDtypeStruct((B, S, DH * CH), F32)
    return pl.pallas_call(
        body, name="dn_prep_fwd", grid=(B, nc // G),
        out_shape=[wide, wide, wide, wide, sq, sq, jax.ShapeDtypeStruct((B, nc, 1, DH), F32)],
        in_specs=[_rows(G * CH, CONVW), _rows(G * CH, 2 * DH), _full((1, DH)), _full((1, DH))],
        out_specs=[_rows(G * CH, DNW)] * 4 + [_rows(G * CH, DH * CH)] * 2
                  + [pl.BlockSpec((None, G, 1, DH), lambda b, i: (b, i, 0, 0))],
        compiler_params=_cparams(("parallel", "parallel")),
    )(cq, ba, a_log, dt_b)


def _dn_seq_specs(B, nc, rev):
    at = (lambda i: nc - 1 - i) if rev else (lambda i: i)
    wide = pl.BlockSpec((B, CH, DNW), lambda i: (0, at(i), 0))
    a_spec = pl.BlockSpec((B, CH, DH * CH), lambda i: (0, at(i), 0))
    cd_spec = pl.BlockSpec((B, None, 1, DH), lambda i: (0, at(i), 0, 0))
    st_spec = pl.BlockSpec((B, None, DH, DK, DK), lambda i: (0, at(i), 0, 0, 0))
    return wide, a_spec, cd_spec, st_spec


def _dn_step_operands(B, u_ref, w_ref, qd_ref, kd_ref, a_ref, cd_ref):
    pairs = [(b, h) for b in range(B) for h in range(DH)]
    wide = lambda ref: jnp.stack([ref[b, :, DK * h:DK * (h + 1)] for b, h in pairs])
    a_in = jnp.stack([a_ref[b, :, CH * h:CH * (h + 1)] for b, h in pairs])
    cd = jnp.stack([cd_ref[b, :, h:h + 1] for b, h in pairs])
    return wide(u_ref), wide(w_ref), wide(qd_ref), wide(kd_ref), a_in, cd


def _dn_seq_fwd(u, w, qd, kd, a_in, cd):
    B, S, _ = u.shape
    nc = S // CH

    def body(u_ref, w_ref, qd_ref, kd_ref, a_ref, cd_ref, o_ref, st_ref, state):
        @pl.when(pl.program_id(0) == 0)
        def _():
            state[...] = jnp.zeros_like(state)

        S0 = state[...]
        for b in range(B):
            st_ref[b] = S0[DH * b:DH * (b + 1)]
        o, S1 = _dn_step(S0, *_dn_step_operands(B, u_ref, w_ref, qd_ref, kd_ref, a_ref, cd_ref))
        state[...] = S1
        for b in range(B):
            for h in range(DH):
                o_ref[b, :, DK * h:DK * (h + 1)] = o[DH * b + h]

    wide, a_spec, cd_spec, st_spec = _dn_seq_specs(B, nc, False)
    return pl.pallas_call(
        body, name="dn_seq_fwd", grid=(nc,),
        out_shape=[jax.ShapeDtypeStruct((B, S, DNW), F32), jax.ShapeDtypeStruct((B, nc, DH, DK, DK), F32)],
        in_specs=[wide, wide, wide, wide, a_spec, cd_spec],
        out_specs=[wide, st_spec],
        scratch_shapes=[pltpu.VMEM((B * DH, DK, DK), F32)],
        compiler_params=_cparams(("arbitrary",)),
    )(u, w, qd, kd, a_in, cd)


def _dn_seq_bwd(u, w, qd, kd, a_in, cd, states, do):
    B, S, _ = u.shape
    nc = S // CH

    def body(u_ref, w_ref, qd_ref, kd_ref, a_ref, cd_ref, st_ref, do_ref,
             du_ref, dw_ref, dqd_ref, dkd_ref, da_ref, dcd_ref, dstate):
        @pl.when(pl.program_id(0) == 0)
        def _():
            dstate[...] = jnp.zeros_like(dstate)

        lane4 = lax.broadcasted_iota(jnp.int32, (1, DH), 1)
        S0 = jnp.concatenate([st_ref[b] for b in range(B)], axis=0)
        do = jnp.stack([do_ref[b, :, DK * h:DK * (h + 1)] for b in range(B) for h in range(DH)])
        _, vjp = jax.vjp(_dn_step, S0, *_dn_step_operands(B, u_ref, w_ref, qd_ref, kd_ref, a_ref, cd_ref))
        dS, du, dw, dqd, dkd, da, dcd = vjp((do, dstate[...]))
        dstate[...] = dS
        for b in range(B):
            dcdrow = jnp.zeros((1, DH), F32)
            for h in range(DH):
                n = DH * b + h
                lanes = slice(DK * h, DK * (h + 1))
                du_ref[b, :, lanes] = du[n]
                dw_ref[b, :, lanes] = dw[n]
                dqd_ref[b, :, lanes] = dqd[n]
                dkd_ref[b, :, lanes] = dkd[n]
                da_ref[b, :, CH * h:CH * (h + 1)] = da[n]
                dcdrow = dcdrow + jnp.where(lane4 == h, dcd[n], 0.0)
            dcd_ref[b] = dcdrow

    wide, a_spec, cd_spec, st_spec = _dn_seq_specs(B, nc, True)
    sd = jax.ShapeDtypeStruct((B, S, DNW), F32)
    return pl.pallas_call(
        body, name="dn_seq_bwd", grid=(nc,),
        out_shape=[sd, sd, sd, sd, jax.ShapeDtypeStruct((B, S, DH * CH), F32), jax.ShapeDtypeStruct((B, nc, 1, DH), F32)],
        in_specs=[wide, wide, wide, wide, a_spec, cd_spec, st_spec, wide],
        out_specs=[wide, wide, wide, wide, a_spec, cd_spec],
        scratch_shapes=[pltpu.VMEM((B * DH, DK, DK), F32)],
        compiler_params=_cparams(("arbitrary",)),
    )(u, w, qd, kd, a_in, cd, states, do)


def _dn_prep_bwd(cq, ba, a_log, dt_b, t_inv, du, dw, dqd, dkd, da, dcd):
    B, S, _ = cq.shape
    nc = S // CH
    G = _dn_group(S, 4)

    def body(cq_ref, ba_ref, al_ref, dt_ref, t_ref, du_ref, dw_ref, dqd_ref, dkd_ref, da_ref, dcd_ref,
             dcq_ref, dba_ref, dal_ref, ddt_ref):
        @pl.when((pl.program_id(0) == 0) & (pl.program_id(1) == 0))
        def _():
            dal_ref[...] = jnp.zeros_like(dal_ref)
            ddt_ref[...] = jnp.zeros_like(ddt_ref)

        pairs = [(c, h) for c in range(G) for h in range(DH)]
        rows = lambda c: slice(CH * c, CH * (c + 1))
        wide = lambda ref: jnp.stack([ref[rows(c), DK * h:DK * (h + 1)] for c, h in pairs])
        square = lambda ref: jnp.stack([ref[rows(c), CH * h:CH * (h + 1)] for c, h in pairs])
        ops = _dn_stack(cq_ref[...], ba_ref[...], al_ref[...], dt_ref[...], G)
        cots = (wide(du_ref), wide(dw_ref), wide(dqd_ref), wide(dkd_ref), square(da_ref),
                jnp.stack([dcd_ref[c][:, h:h + 1] for c, h in pairs]), jnp.zeros((len(pairs), CH, CH), F32))
        _, vjp = jax.vjp(functools.partial(_dn_prep, square(t_ref)), *ops)
        dq, dk, dv, dar, dbr, dl, dd = vjp(cots)
        lane8 = lax.broadcasted_iota(jnp.int32, (CH, 2 * DH), 1)
        lane4 = lax.broadcasted_iota(jnp.int32, (1, DH), 1)
        dal = jnp.zeros((1, DH), F32)
        ddt = jnp.zeros((1, DH), F32)
        for c in range(G):
            dba = jnp.zeros((CH, 2 * DH), F32)
            for h in range(DH):
                n = DH * c + h
                dcq_ref[rows(c), DK * h:DK * (h + 1)] = dq[n]
                dcq_ref[rows(c), DNW + DK * h:DNW + DK * (h + 1)] = dk[n]
                dcq_ref[rows(c), 2 * DNW + DK * h:2 * DNW + DK * (h + 1)] = dv[n]
                dba = dba + jnp.where(lane8 == h, dbr[n], 0.0) + jnp.where(lane8 == DH + h, dar[n], 0.0)
                dal = dal + jnp.where(lane4 == h, dl[n], 0.0)
                ddt = ddt + jnp.where(lane4 == h, dd[n], 0.0)
            dba_ref[rows(c), :] = dba.astype(BF16)
        dal_ref[...] += dal
        ddt_ref[...] += ddt

    return pl.pallas_call(
        body, name="dn_prep_bwd", grid=(B, nc // G),
        out_shape=[jax.ShapeDtypeStruct((B, S, CONVW), F32), jax.ShapeDtypeStruct((B, S, 2 * DH), BF16),
                   jax.ShapeDtypeStruct((1, DH), F32), jax.ShapeDtypeStruct((1, DH), F32)],
        in_specs=[_rows(G * CH, CONVW), _rows(G * CH, 2 * DH), _full((1, DH)), _full((1, DH)), _rows(G * CH, DH * CH)]
                 + [_rows(G * CH, DNW)] * 4 + [_rows(G * CH, DH * CH),
                                               pl.BlockSpec((None, G, 1, DH), lambda b, i: (b, i, 0, 0))],
        out_specs=[_rows(G * CH, CONVW), _rows(G * CH, 2 * DH), _full((1, DH)), _full((1, DH))],
        compiler_params=_cparams(("arbitrary", "arbitrary")),
    )(cq, ba, a_log, dt_b, t_inv, du, dw, dqd, dkd, da, dcd)


def _gated_norm(o, z, g):
    outs = []
    for h in range(DH):
        t = o[:, DK * h:DK * (h + 1)]
        r = lax.rsqrt(jnp.mean(t * t, axis=-1, keepdims=True) + EPS)
        outs.append(t * r * g * _silu(z[:, DK * h:DK * (h + 1)]))
    return jnp.concatenate(outs, axis=1)


def _mix_fwd(x, o_attn, o_dn, z, ga, gd, mod, dn_g, w_branch, w_out):
    B, S, _ = x.shape
    tm = _tile(S, 512)

    def body(x_ref, oa_ref, od_ref, z_ref, ga_ref, gd_ref, mod_ref, g_ref, wb_ref, wo_ref,
             x1_ref, mix_ref, mg_ref, ob_ref):
        oa = oa_ref[...].astype(BF16)
        od = _gated_norm(od_ref[...], z_ref[...], g_ref[...]).astype(BF16)
        ob_ref[0] = oa
        ob_ref[1] = od
        ya = jnp.dot(oa, wb_ref[0:QW, :], preferred_element_type=F32)
        yd = jnp.dot(od, wb_ref[QW:QW + DNW, :], preferred_element_type=F32)
        merged = (_sigmoid(ga_ref[...]) * ya + _sigmoid(gd_ref[...]) * yd).astype(BF16)
        mg_ref[...] = merged
        mix = jnp.dot(merged, wo_ref[...], preferred_element_type=F32)
        mix_ref[...] = mix
        x1_ref[...] = x_ref[...] + mod_ref[2:3, :] * mix

    return pl.pallas_call(
        body, name="mix_fwd", grid=(B, S // tm),
        out_shape=[jax.ShapeDtypeStruct((B, S, D), F32), jax.ShapeDtypeStruct((B, S, D), F32),
                   jax.ShapeDtypeStruct((B, S, D), BF16), jax.ShapeDtypeStruct((B, 2, S, QW), BF16)],
        in_specs=[_rows(tm, D), _rows(tm, QW), _rows(tm, DNW), _rows(tm, DNW), _rows(tm, D), _rows(tm, D),
                  _perb(6, D), _full((1, DK)), _resident(w_branch.shape), _resident(w_out.shape)],
        out_specs=[_rows(tm, D), _rows(tm, D), _rows(tm, D), _stacked(2, tm, QW)],
        compiler_params=_cparams(("parallel", "arbitrary")),
    )(x, o_attn, o_dn, z, ga, gd, mod, dn_g, w_branch, w_out)


def _mix_bwd(dx1, mix, o_attn, o_dn, z, ga, gd, mod, dn_g, w_branch, w_out):
    B, S, _ = dx1.shape
    tm = _tile(S)

    def body(dx1_ref, mix_ref, oa_ref, od_ref, z_ref, ga_ref, gd_ref, mod_ref, g_ref, wb_ref, wo_ref,
             dmix_ref, dyo_ref, dga_ref, dgd_ref, dz_ref, doa_ref, dod_ref, dgate_ref, dg_ref):
        b, i = pl.program_id(0), pl.program_id(1)
        dx1 = dx1_ref[...]
        dmix = (dx1 * mod_ref[2:3, :]).astype(BF16)
        dmix_ref[...] = dmix
        dgate = jnp.sum(dx1 * mix_ref[...], axis=0, keepdims=True)
        dmerged = _dot_nt(dmix, wo_ref[...])
        odn, gn_vjp = jax.vjp(_gated_norm, od_ref[...], z_ref[...], g_ref[...])
        ya = _dot(oa_ref[...], wb_ref[0:QW, :])
        yd = _dot(odn, wb_ref[QW:QW + DNW, :])
        sa, sd = _sigmoid(ga_ref[...]), _sigmoid(gd_ref[...])
        dya = (dmerged * sa).astype(BF16)
        dyd = (dmerged * sd).astype(BF16)
        dyo_ref[0] = dya
        dyo_ref[1] = dyd
        dga_ref[...] = (dmerged * ya * sa * (1.0 - sa)).astype(BF16)
        dgd_ref[...] = (dmerged * yd * sd * (1.0 - sd)).astype(BF16)
        doa_ref[...] = _dot_nt(dya, wb_ref[0:QW, :])
        dodn = _dot_nt(dyd, wb_ref[QW:QW + DNW, :])
        dod, dz, dg = gn_vjp(dodn)
        dod_ref[...] = dod
        dz_ref[...] = dz.astype(BF16)

        @pl.when(i == 0)
        def _():
            dgate_ref[...] = jnp.zeros_like(dgate_ref)

        @pl.when((b == 0) & (i == 0))
        def _():
            dg_ref[...] = jnp.zeros_like(dg_ref)

        dgate_ref[...] += dgate
        dg_ref[...] += dg

    return pl.pallas_call(
        body, name="mix_bwd", grid=(B, S // tm),
        out_shape=[jax.ShapeDtypeStruct((B, S, D), BF16), jax.ShapeDtypeStruct((B, 2, S, D), BF16),
                   jax.ShapeDtypeStruct((B, S, D), BF16), jax.ShapeDtypeStruct((B, S, D), BF16),
                   jax.ShapeDtypeStruct((B, S, DNW), BF16),
                   jax.ShapeDtypeStruct((B, S, QW), F32), jax.ShapeDtypeStruct((B, S, DNW), F32),
                   jax.ShapeDtypeStruct((B, 1, D), F32), jax.ShapeDtypeStruct((1, DK), F32)],
        in_specs=[_rows(tm, D), _rows(tm, D), _rows(tm, QW), _rows(tm, DNW), _rows(tm, DNW), _rows(tm, D),
                  _rows(tm, D), _perb(6, D), _full((1, DK)), _resident(w_branch.shape), _resident(w_out.shape)],
        out_specs=[_rows(tm, D), _stacked(2, tm, D), _rows(tm, D), _rows(tm, D), _rows(tm, DNW),
                   _rows(tm, QW), _rows(tm, DNW), _perb(1, D), _full((1, DK))],
        compiler_params=_cparams(("arbitrary", "arbitrary")),
    )(dx1, mix, o_attn, o_dn, z, ga, gd, mod, dn_g, w_branch, w_out)


GU_SHARD = 2 * FFN // N_DEV
GU_HALF = N_DEV // 2


def _ffn1_fwd(x1, mod, g2, w_gu):
    B, S, _ = x1.shape
    tm = _tile(S)

    def body(x_ref, mod_ref, g_ref, w_ref, h_ref, gate_ref, up_ref, act_ref):
        h = _rms_mod(x_ref[...], g_ref[...], mod_ref[4:5, :], mod_ref[3:4, :]).astype(BF16)
        h_ref[...] = h
        for j in range(GU_HALF):
            gate = jnp.dot(h, w_ref[j], preferred_element_type=F32)
            up = jnp.dot(h, w_ref[GU_HALF + j], preferred_element_type=F32)
            gate_ref[j] = gate
            up_ref[j] = up
            act_ref[j] = (_silu(gate) * up).astype(BF16)

    blk = lambda dt: jax.ShapeDtypeStruct((B, GU_HALF, S, GU_SHARD), dt)
    return pl.pallas_call(
        body, name="ffn1_fwd", grid=(B, S // tm),
        out_shape=[jax.ShapeDtypeStruct((B, S, D), BF16), blk(F32), blk(F32), blk(BF16)],
        in_specs=[_rows(tm, D), _perb(6, D), _full((1, D)), _resident(w_gu.shape)],
        out_specs=[_rows(tm, D)] + [_stacked(GU_HALF, tm, GU_SHARD)] * 3,
        compiler_params=_cparams(("parallel", "arbitrary")),
    )(x1, mod, g2, w_gu)


def _ffn2_fwd(act, x1, target, mod, w_down):
    B, S, _ = x1.shape
    tm = _tile(S, 512)

    def body(a_ref, x_ref, t_ref, mod_ref, w_ref, dy_ref, loss_ref, dgate_ref):
        b, i = pl.program_id(0), pl.program_id(1)
        y = jnp.dot(a_ref[0], w_ref[0], preferred_element_type=F32)
        for j in range(1, GU_HALF):
            y = y + jnp.dot(a_ref[j], w_ref[j], preferred_element_type=F32)
        err = x_ref[...] + mod_ref[5:6, :] * y - t_ref[...]
        dy = err * (1.0 / D)
        dy_ref[...] = dy

        @pl.when((b == 0) & (i == 0))
        def _():
            loss_ref[...] = jnp.zeros_like(loss_ref)

        @pl.when(i == 0)
        def _():
            dgate_ref[...] = jnp.zeros_like(dgate_ref)

        loss_ref[...] += (0.5 / D) * jnp.sum(err * err)
        dgate_ref[...] += jnp.sum(dy * y, axis=0, keepdims=True)

    return pl.pallas_call(
        body, name="ffn2_fwd", grid=(B, S // tm),
        out_shape=[jax.ShapeDtypeStruct((B, S, D), F32), jax.ShapeDtypeStruct((1, 128), F32),
                   jax.ShapeDtypeStruct((B, 1, D), F32)],
        in_specs=[_stacked(GU_HALF, tm, GU_SHARD), _rows(tm, D), _rows(tm, D), _perb(6, D), _resident(w_down.shape)],
        out_specs=[_rows(tm, D), _full((1, 128)), _perb(1, D)],
        compiler_params=_cparams(("arbitrary", "arbitrary")),
    )(act, x1, target, mod, w_down)


def _ffn2_bwd(dy, gate, up, mod, w_down):
    B, S, _ = dy.shape
    tm = _tile(S)

    def body(dy_ref, gate_ref, up_ref, mod_ref, w_ref, dgu_ref, dyg_ref):
        dyg = (dy_ref[...] * mod_ref[5:6, :]).astype(BF16)
        dyg_ref[...] = dyg
        for j in range(GU_HALF):
            dact = _dot_nt(dyg, w_ref[j])
            gate, up = gate_ref[j], up_ref[j]
            sg = _sigmoid(gate)
            dgu_ref[j] = (dact * up * (sg * (1.0 + gate * (1.0 - sg)))).astype(BF16)
            dgu_ref[GU_HALF + j] = (dact * (gate * sg)).astype(BF16)

    return pl.pallas_call(
        body, name="ffn2_bwd", grid=(B, S // tm),
        out_shape=[jax.ShapeDtypeStruct((B, N_DEV, S, GU_SHARD), BF16), jax.ShapeDtypeStruct((B, S, D), BF16)],
        in_specs=[_rows(tm, D), _stacked(GU_HALF, tm, GU_SHARD), _stacked(GU_HALF, tm, GU_SHARD), _perb(6, D),
                  _resident(w_down.shape)],
        out_specs=[_stacked(N_DEV, tm, GU_SHARD), _rows(tm, D)],
        compiler_params=_cparams(("parallel", "arbitrary")),
    )(dy, gate, up, mod, w_down)


def _ffn1_bwd(dgu, x1, dy, mod, g2, w_gu):
    B, S, _ = x1.shape
    tm = _tile(S, 512)

    def body(dgu_ref, x_ref, dy_ref, mod_ref, g_ref, w_ref, dx1_ref, dg_ref, dsc_ref, dsh_ref):
        b, i = pl.program_id(0), pl.program_id(1)
        dh = _dot_nt(dgu_ref[0], w_ref[0])
        for j in range(1, N_DEV):
            dh = dh + _dot_nt(dgu_ref[j], w_ref[j])
        _, vjp = jax.vjp(_rms_mod, x_ref[...], g_ref[...], mod_ref[4:5, :], mod_ref[3:4, :])
        dx, dg, dsc, dsh = vjp(dh)
        dx1_ref[...] = dy_ref[...] + dx

        @pl.when((b == 0) & (i == 0))
        def _():
            dg_ref[...] = jnp.zeros_like(dg_ref)

        @pl.when(i == 0)
        def _():
            dsc_ref[...] = jnp.zeros_like(dsc_ref)
            dsh_ref[...] = jnp.zeros_like(dsh_ref)

        dg_ref[...] += dg
        dsc_ref[...] += dsc
        dsh_ref[...] += dsh

    return pl.pallas_call(
        body, name="ffn1_bwd", grid=(B, S // tm),
        out_shape=[jax.ShapeDtypeStruct((B, S, D), F32), jax.ShapeDtypeStruct((1, D), F32),
                   jax.ShapeDtypeStruct((B, 1, D), F32), jax.ShapeDtypeStruct((B, 1, D), F32)],
        in_specs=[_stacked(N_DEV, tm, GU_SHARD), _rows(tm, D), _rows(tm, D), _perb(6, D), _full((1, D)),
                  _resident(w_gu.shape)],
        out_specs=[_rows(tm, D), _full((1, D)), _perb(1, D), _perb(1, D)],
        compiler_params=_cparams(("arbitrary", "arbitrary")),
    )(dgu, x1, dy, mod, g2, w_gu)


def _adamw(w, g, m, v, name):
    def body(w_ref, g_ref, m_ref, v_ref, d_ref, nm_ref, nv_ref):
        g = g_ref[...]
        m = B1 * m_ref[...] + (1.0 - B1) * g
        v = B2 * v_ref[...] + (1.0 - B2) * (g * g)
        nm_ref[...] = m
        nv_ref[...] = v
        m_hat = m / (1.0 - B1 ** STEP)
        v_hat = v / (1.0 - B2 ** STEP)
        d_ref[...] = -LR * (m_hat / (jnp.sqrt(v_hat) + AEPS) + WD * w_ref[...])

    sd = jax.ShapeDtypeStruct(w.shape, F32)
    return pl.pallas_call(body, name=name, out_shape=(sd, sd, sd), compiler_params=_cparams())(w, g, m, v)


def kernel(x, c, positions, ada_w, ada_b, norm1_g, w_in, conv_w, q_norm_g, k_norm_g, sinks, a_log, dt_bias, dn_norm_g, w_branch, w_out, norm2_g, w_gate_up, w_down, loss_target, m_ada_w, m_ada_b, m_norm1_g, m_w_in, m_conv_w, m_q_norm_g, m_k_norm_g, m_sinks, m_a_log, m_dt_bias, m_dn_norm_g, m_w_branch, m_w_out, m_norm2_g, m_w_gate_up, m_w_down, v_ada_w, v_ada_b, v_norm1_g, v_w_in, v_conv_w, v_q_norm_g, v_k_norm_g, v_sinks, v_a_log, v_dt_bias, v_dn_norm_g, v_w_branch, v_w_out, v_norm2_g, v_w_gate_up, v_w_down):
    B, S, _ = x.shape
    me = 4 * lax.axis_index("x") + 2 * lax.axis_index("y") + lax.axis_index("c")

    shards = [w[0].astype(BF16) for w in (w_in, w_branch, w_out, w_gate_up, w_down)]
    in_sems, in_srcs, in_lands, in_token = _copies_start(shards[:1], [_place_own(shards[0], me)], False, c, "gather_in_start")

    c_all = _all_gather_small(c + in_token[0, 0], "gather_c").reshape(N_DEV * B, D)
    ncol = 6 * D // N_DEV
    mod_cols, cond_all = _ada_fwd(c_all, ada_w[0], lax.dynamic_slice(ada_b, (0, me * ncol), (1, ncol)))
    mod_all = _all_gather_small(mod_cols, "gather_mod").transpose(1, 0, 2).reshape(N_DEV * B, 6 * D)
    mod = lax.dynamic_slice(mod_all, (me * B, 0), (B, 6 * D)).reshape(B, 6, D)
    conv2 = conv_w.reshape(CONV, CONVW // N_DEV)
    conv_all = _all_gather_small(conv2, "gather_conv").transpose(1, 0, 2).reshape(CONV, CONVW)

    w_sems, w_srcs, w_lands, w_token = _copies_start(shards[1:], [_place_own(s, me) for s in shards[1:]], False,
                                                    (mod, conv_all), "gather_rest_start")

    (w_in_b,) = _copies_wait(in_sems, in_srcs, in_lands, mod, "gather_wait_in")
    h1, aq, akv, dnx, ba, z, ga, gd = _inproj_fwd(x, mod, norm1_g + w_token[0, 0], w_in_b)
    invf, mean_q, mean_k = _attn_consts()
    rope_cos, rope_sin = _rope_tables(positions.reshape(B, S, 1), invf)
    o_attn = _attn_fwd(aq, akv, rope_cos, rope_sin, q_norm_g, k_norm_g, sinks, mean_q, mean_k)
    cq = _conv_fwd(dnx, conv_all)
    dn_u, dn_w, dn_qd, dn_kd, dn_a, dn_t, dn_cd = _dn_prep_fwd(cq, ba, a_log, dt_bias)
    o_dn, states = _dn_seq_fwd(dn_u, dn_w, dn_qd, dn_kd, dn_a, dn_cd)
    w_branch_g, w_out_g, w_gu_b, w_down_g = _copies_wait(w_sems, w_srcs, w_lands, o_dn, "gather_wait_rest")
    w_branch_f = w_branch_g.reshape(D, D)
    w_out_f = w_out_g.reshape(D, D)
    w_down_b = w_down_g.reshape(GU_HALF, GU_SHARD, D)
    x1, mix, merged, ob = _mix_fwd(x, o_attn, o_dn, z, ga, gd, mod, dn_norm_g, w_branch_f, w_out_f)
    h2, gate, up, act = _ffn1_fwd(x1, mod, norm2_g, w_gu_b)
    dy, loss_part, d_gate2 = _ffn2_fwd(act, x1, loss_target, mod, w_down_b)
    loss = lax.psum(loss_part[0, 0], ("x", "y", "c"))

    one = lambda t: t.reshape(B, 1, S, t.shape[-1])
    dgu, dyg = _ffn2_bwd(dy, gate, up, mod, w_down_b)
    g_w_down = _wgrad(act, one(dyg), "wgrad_down")
    dx1, d_n2g, d_scale2, d_shift2 = _ffn1_bwd(dgu, x1, dy, mod, norm2_g, w_gu_b)
    g_w_gu = _wgrad(one(h2), dgu, "wgrad_gate_up")
    ffn = _exchange_start([g_w_gu, g_w_down.reshape(N_DEV, FFN // N_DEV, D)], me, dx1, "exchange_ffn_start")
    dmix, dyo, dga, dgd, dz, d_oa, d_od, d_gate1, d_dng = _mix_bwd(
        dx1, mix, o_attn, o_dn, z, ga, gd, mod, dn_norm_g + ffn[3][0, 0], w_branch_f, w_out_f)
    d_dn = _dn_seq_bwd(dn_u, dn_w, dn_qd, dn_kd, dn_a, dn_cd, states, d_od)
    dcq, dba, d_alog, d_dtb = _dn_prep_bwd(cq, ba, a_log, dt_bias, dn_t, *d_dn)
    ddnx, d_conv = _conv_bwd(dnx, conv_all, dcq)
    daq, dakv, d_qg, d_kg, d_sinks = _attn_bwd(aq, akv, rope_cos, rope_sin, q_norm_g, k_norm_g, sinks, mean_q, mean_k, d_oa)
    dps = [daq, dakv, ddnx, dba, dz, dga, dgd]
    dblk, grad_x, d_n1g, d_scale1, d_shift1 = _inproj_bwd(x, mod, norm1_g, dx1, dps, w_in_b)

    dmod = jnp.concatenate([d_shift1, d_scale1, d_gate1, d_shift2, d_scale2, d_gate2], axis=2).reshape(B, 6 * D)
    small = jnp.concatenate([d_n1g, d_qg, d_kg, d_sinks, d_alog, d_dtb, d_dng, d_n2g, d_conv.reshape(1, CONV * CONVW)], axis=1)
    nsm = small.shape[1]
    width = -(-max(6 * D, nsm) // 128) * 128
    rows = jnp.concatenate([jnp.pad(dmod, ((0, 0), (0, width - 6 * D))), jnp.pad(small, ((0, 8 - B - 1), (0, width - nsm)))], axis=0)
    rows_all = _all_gather_small(rows, "gather_small")
    dmod_all = rows_all[:, 0:B, 0:6 * D].reshape(N_DEV * B, 6 * D)
    dmod_cols = lax.dynamic_slice(dmod_all, (0, me * ncol), (N_DEV * B, ncol))
    grad_ada_w, grad_ada_b, small_sum = _ada_bwd(cond_all, dmod_all, dmod_cols, rows_all[:, B, :])
    sizes = [D, HD, HD, HQ, DH, DH, DK, D]
    so = np.cumsum([0] + sizes)
    g_n1, g_qg, g_kg, g_sk, g_al, g_dt, g_dn, g_n2 = [small_sum[:, so[i]:so[i + 1]] for i in range(8)]
    g_conv_all = small_sum[:, so[8]:so[8] + CONV * CONVW].reshape(CONV, N_DEV, CONVW // N_DEV)
    grad_conv = lax.dynamic_slice(g_conv_all, (0, me, 0), (CONV, 1, CONVW // N_DEV)).reshape(CONV, CONVW // N_DEV)

    g_w_in = _wgrad(one(h1), dblk, "wgrad_in", after=small_sum)
    proj = _exchange_start([g_w_in], me, small_sum, "exchange_in_start")
    g_w_out = _wgrad(one(merged), one(dmix), "wgrad_out", after=proj[3])
    g_w_branch = _wgrad(ob, dyo, "wgrad_branch", after=proj[3])
    mixer = _exchange_start([g_w_branch.reshape(N_DEV, D // N_DEV, D), g_w_out.reshape(N_DEV, D // N_DEV, D)], me,
                            proj[3], "exchange_mix_start")

    grad_w_gu, grad_w_down = [_sum_blocks(r, "sum_grads_" + nm) for r, nm in zip(
        _copies_wait(*ffn[:3], mixer[3], "exchange_ffn_wait"), ["gate_up", "down"])]
    (grad_w_in,) = [_sum_blocks(r, "sum_grads_in") for r in _copies_wait(*proj[:3], grad_w_gu, "exchange_in_wait")]
    grad_w_branch, grad_w_out = [_sum_blocks(r, "sum_grads_" + nm) for r, nm in zip(
        _copies_wait(*mixer[:3], grad_w_in, "exchange_mix_wait"), ["branch", "out"])]

    big = [(ada_w, grad_ada_w.reshape(ada_w.shape), m_ada_w, v_ada_w), (w_in, grad_w_in, m_w_in, v_w_in),
           (w_branch, grad_w_branch, m_w_branch, v_w_branch), (w_out, grad_w_out, m_w_out, v_w_out),
           (w_gate_up, grad_w_gu, m_w_gate_up, v_w_gate_up), (w_down, grad_w_down, m_w_down, v_w_down)]
    upd = {}
    for nm, (w, g, m, v) in zip(["ada_w", "w_in", "w_branch", "w_out", "w_gate_up", "w_down"], big):
        upd[nm] = _adamw(w, g, m, v, "adamw_" + nm)
    small_names = ["ada_b", "norm1_g", "q_norm_g", "k_norm_g", "sinks", "a_log", "dt_bias", "dn_norm_g", "norm2_g", "conv_w"]
    small_w = [ada_b, norm1_g, q_norm_g, k_norm_g, sinks, a_log, dt_bias, dn_norm_g, norm2_g, conv_w]
    small_g = [grad_ada_b, g_n1, g_qg, g_kg, g_sk, g_al, g_dt, g_dn, g_n2, grad_conv]
    small_m = [m_ada_b, m_norm1_g, m_q_norm_g, m_k_norm_g, m_sinks, m_a_log, m_dt_bias, m_dn_norm_g, m_norm2_g, m_conv_w]
    small_v = [v_ada_b, v_norm1_g, v_q_norm_g, v_k_norm_g, v_sinks, v_a_log, v_dt_bias, v_dn_norm_g, v_norm2_g, v_conv_w]
    cat = lambda arrs: jnp.concatenate([a.reshape(1, -1) for a in arrs], axis=1)
    res = _adamw(cat(small_w), cat(small_g), cat(small_m), cat(small_v), "adamw_small")
    po = np.cumsum([0] + [int(np.prod(w.shape)) for w in small_w])
    grads = {}
    for i, nm in enumerate(small_names):
        upd[nm] = tuple(r[:, po[i]:po[i + 1]].reshape(small_w[i].shape) for r in res)
        grads[nm] = small_g[i].reshape(small_w[i].shape)
    grads.update(ada_w=grad_ada_w.reshape(ada_w.shape), w_in=grad_w_in, w_branch=grad_w_branch, w_out=grad_w_out,
                 w_gate_up=grad_w_gu, w_down=grad_w_down)

    order = ["ada_w", "ada_b", "norm1_g", "w_in", "conv_w", "q_norm_g", "k_norm_g", "sinks", "a_log", "dt_bias",
             "dn_norm_g", "w_branch", "w_out", "norm2_g", "w_gate_up", "w_down"]
    return (loss, grad_x, *[grads[n] for n in order], *[upd[n][0] for n in order],
            *[upd[n][1] for n in order], *[upd[n][2] for n in order])
```

```python
import functools

import numpy as np
import jax
import jax.numpy as jnp
from jax import lax
from jax.experimental import pallas as pl
from jax.experimental.pallas import tpu as pltpu

F32 = jnp.float32
BF16 = jnp.bfloat16
HI = lax.Precision.HIGHEST

N_DEV = 8
D = 1024
HQ, HKV, HD = 8, 2, 64
GRP = HQ // HKV
BLK = 128
ROT = HD // 4
THETA = 500000.0
QW, KVW = HQ * HD, HKV * HD
DH, DK = 4, 128
CH = 64
DNW = DH * DK
CONV = 4
CONVW = 3 * DNW
FFN = 2816
EPS = 1e-6
IN_W = QW + 2 * KVW + CONVW + 2 * DH + DNW + 2 * D

LR, B1, B2, AEPS, WD, STEP = 0.001, 0.9, 0.999, 1e-08, 0.01, 10

VMEM_LIMIT = 56 * 1024 * 1024
MESH = pl.DeviceIdType.MESH


def _cparams(sem=None, vmem=VMEM_LIMIT):
    return pltpu.CompilerParams(dimension_semantics=sem, vmem_limit_bytes=vmem)


def _full(shape):
    n = len(shape)
    return pl.BlockSpec(shape, lambda *_: (0,) * n)


def _resident(shape):
    n = len(shape)
    return pl.BlockSpec(shape, lambda *_: (0,) * n, pipeline_mode=pl.Buffered(1))


def _rows(tm, w):
    return pl.BlockSpec((None, tm, w), lambda b, i: (b, i, 0))


def _stacked(n, tm, w):
    return pl.BlockSpec((None, n, tm, w), lambda b, i: (b, 0, i, 0))


def _perb(r, w):
    return pl.BlockSpec((None, r, w), lambda b, i: (b, 0, 0))


def _dot(a, b):
    return jnp.dot(a.astype(BF16), b.astype(BF16), preferred_element_type=F32)


def _dot_nt(a, b):
    return lax.dot_general(a.astype(BF16), b.astype(BF16), (((1,), (1,)), ((), ())), preferred_element_type=F32)


def _dot_tn(a, b):
    return lax.dot_general(a.astype(BF16), b.astype(BF16), (((0,), (0,)), ((), ())), preferred_element_type=F32)


def _dot_hi(a, b):
    return jnp.dot(a, b, preferred_element_type=F32, precision=HI)


def _sigmoid(x):
    return jax.nn.sigmoid(x)


def _silu(x):
    return x * jax.nn.sigmoid(x)


def _rms_mod(x, g, scale, shift):
    r = lax.rsqrt(jnp.mean(x * x, axis=-1, keepdims=True) + EPS)
    return (x * r * g) * (1.0 + scale) + shift


def _tile(S, rows=256):
    return min(rows, S)


def _peer(x, y, c, k):
    px = 1 - x if (k >> 2) & 1 else x
    py = 1 - y if (k >> 1) & 1 else y
    pc = 1 - c if k & 1 else c
    return px, py, pc


def _all_gather_small(v, name):
    r, n = v.shape

    def body(v_ref, out_ref, send_sems, recv_sems, local_sem):
        x, y, c = lax.axis_index("x"), lax.axis_index("y"), lax.axis_index("c")
        me = 4 * x + 2 * y + c
        mine = pltpu.make_async_copy(v_ref, out_ref.at[me], local_sem)
        mine.start()
        sends = []
        for k in range(1, N_DEV):
            cp = pltpu.make_async_remote_copy(
                src_ref=v_ref, dst_ref=out_ref.at[me], send_sem=send_sems.at[k - 1], recv_sem=recv_sems.at[k - 1],
                device_id=_peer(x, y, c, k), device_id_type=MESH)
            cp.start()
            sends.append(cp)
        for k in range(1, N_DEV):
            px, py, pc = _peer(x, y, c, k)
            pltpu.make_async_remote_copy(
                src_ref=v_ref, dst_ref=out_ref.at[4 * px + 2 * py + pc], send_sem=send_sems.at[k - 1],
                recv_sem=recv_sems.at[k - 1], device_id=(px, py, pc), device_id_type=MESH).wait_recv()
        for cp in sends:
            cp.wait_send()
        mine.wait()

    return pl.pallas_call(
        body, name=name,
        out_shape=jax.ShapeDtypeStruct((N_DEV, r, n), v.dtype),
        in_specs=[pl.BlockSpec(memory_space=pltpu.VMEM)],
        out_specs=pl.BlockSpec(memory_space=pltpu.VMEM),
        scratch_shapes=[pltpu.SemaphoreType.DMA((N_DEV - 1,)), pltpu.SemaphoreType.DMA((N_DEV - 1,)), pltpu.SemaphoreType.DMA],
    )(v)


def _all_gather_big(vs, name, after=()):
    na, nf = len(vs), len(after)

    def body(*refs):
        v_refs, out_refs = refs[:na], refs[na + nf:2 * na + nf]
        send_sems, recv_sems, local_sems = refs[2 * na + nf:]
        x, y, c = lax.axis_index("x"), lax.axis_index("y"), lax.axis_index("c")
        me, sibling = (x, y, c), (x, y, 1 - c)
        chips = [(1 - x, y), (x, 1 - y), (1 - x, 1 - y)]

        def rows(a, px, py, pc):
            return out_refs[a].at[4 * px + 2 * py + pc]

        def copy(a, k, block, to, src=None):
            return pltpu.make_async_remote_copy(
                src_ref=rows(a, *block) if src is None else src, dst_ref=rows(a, *block),
                send_sem=send_sems.at[7 * a + k], recv_sem=recv_sems.at[7 * a + k], device_id=to, device_id_type=MESH)

        mine = [pltpu.make_async_copy(v_refs[a], rows(a, *me), local_sems.at[a]) for a in range(na)]
        for cp in mine:
            cp.start()
        first = []
        for a in range(na):
            first.append(copy(a, 0, me, sibling, src=v_refs[a]))
            first += [copy(a, 1 + j, me, (*chip, c), src=v_refs[a]) for j, chip in enumerate(chips)]
        for cp in first:
            cp.start()
        passed = []
        for j, chip in enumerate(chips):
            for a in range(na):
                copy(a, 1 + j, (*chip, c), me).wait_recv()
                forward = copy(a, 4 + j, (*chip, c), sibling)
                forward.start()
                passed.append(forward)
        for a in range(na):
            copy(a, 0, sibling, me).wait_recv()
            for j, chip in enumerate(chips):
                copy(a, 4 + j, (*chip, 1 - c), me).wait_recv()
        for cp in first + passed:
            cp.wait_send()
        for cp in mine:
            cp.wait()

    return pl.pallas_call(
        body, name=name,
        out_shape=[jax.ShapeDtypeStruct((N_DEV,) + v.shape, v.dtype) for v in vs],
        in_specs=[pl.BlockSpec(memory_space=pl.ANY)] * (na + nf),
        out_specs=[pl.BlockSpec(memory_space=pl.ANY)] * na,
        scratch_shapes=[pltpu.SemaphoreType.DMA((7 * na,)), pltpu.SemaphoreType.DMA((7 * na,)),
                        pltpu.SemaphoreType.DMA((na,))],
    )(*vs, *after)


_HBM = pl.BlockSpec(memory_space=pltpu.HBM)
_SEM = pl.BlockSpec(memory_space=pltpu.SEMAPHORE)
_EFFECT = pltpu.SideEffectType.DATAFLOW_SIDE_EFFECTING


def _place_own(block, me):
    land = lax.empty((N_DEV,) + block.shape, block.dtype)
    return lax.dynamic_update_slice(land, block[None], (me,) + (0,) * block.ndim)


def _copies_start(srcs, lands, scatter, after, name):
    na = len(srcs)
    afters = tuple(after) if isinstance(after, (tuple, list)) else (after,)

    def body(*refs):
        src_refs, land_refs = refs[:na], refs[na:2 * na]
        sems = refs[2 * na + len(afters):4 * na + len(afters)]
        token = refs[-1]
        x, y, c = lax.axis_index("x"), lax.axis_index("y"), lax.axis_index("c")
        me = 4 * x + 2 * y + c
        for a in range(na):
            for k in range(1, N_DEV):
                px, py, pc = _peer(x, y, c, k)
                src = src_refs[a].at[4 * px + 2 * py + pc] if scatter else src_refs[a]
                pltpu.make_async_remote_copy(
                    src_ref=src, dst_ref=land_refs[a].at[me], send_sem=sems[2 * a], recv_sem=sems[2 * a + 1],
                    device_id=(px, py, pc), device_id_type=MESH).start()
        token[...] = jnp.zeros_like(token)

    hbm = lambda t: pltpu.HBM(t.shape, t.dtype)
    out = pl.pallas_call(
        body, name=name,
        out_shape=tuple([pltpu.SemaphoreType.DMA(())] * (2 * na) + [hbm(t) for t in srcs] + [hbm(t) for t in lands]
                        + [jax.ShapeDtypeStruct((8, 128), F32)]),
        in_specs=[_HBM] * (2 * na) + [pl.BlockSpec(memory_space=pl.ANY)] * len(afters),
        out_specs=tuple([_SEM] * (2 * na) + [_HBM] * (2 * na) + [pl.BlockSpec(memory_space=pltpu.VMEM)]),
        input_output_aliases={i: 2 * na + i for i in range(2 * na)},
        compiler_params=pltpu.CompilerParams(has_side_effects=_EFFECT),
    )(*[pltpu.with_memory_space_constraint(t, pltpu.HBM) for t in list(srcs) + list(lands)], *afters)
    return out[:2 * na], out[2 * na:3 * na], out[3 * na:4 * na], out[-1]


def _exchange_start(gs, me, after, name):
    own = [lax.dynamic_index_in_dim(g, me, 0, keepdims=False) for g in gs]
    return _copies_start(gs, [_place_own(o, me) for o in own], True, after, name)


def _copies_wait(sems, srcs, lands, after, name):
    na = len(srcs)

    def body(*refs):
        land_refs = refs[na:2 * na]
        sem_refs = refs[2 * na:4 * na]
        x, y, c = lax.axis_index("x"), lax.axis_index("y"), lax.axis_index("c")
        for a in range(na):
            seven = land_refs[a].at[pl.ds(0, N_DEV - 1)]
            copy = pltpu.make_async_remote_copy(
                src_ref=seven, dst_ref=seven, send_sem=sem_refs[2 * a], recv_sem=sem_refs[2 * a + 1],
                device_id=(x, y, c), device_id_type=MESH)
            copy.wait_send()
            copy.wait_recv()

    hbm = lambda t: pltpu.HBM(t.shape, t.dtype)
    out = pl.pallas_call(
        body, name=name,
        out_shape=tuple([hbm(t) for t in srcs] + [hbm(t) for t in lands]),
        in_specs=[_HBM] * (2 * na) + [_SEM] * (2 * na) + [pl.BlockSpec(memory_space=pl.ANY)],
        out_specs=tuple([_HBM] * (2 * na)),
        input_output_aliases={i: i for i in range(2 * na)},
        compiler_params=pltpu.CompilerParams(has_side_effects=_EFFECT),
    )(*srcs, *lands, *sems, after)
    return out[na:]


def _adamw_math(w, g, m, v):
    m = B1 * m + (1.0 - B1) * g
    v = B2 * v + (1.0 - B2) * (g * g)
    m_hat = m / (1.0 - B1 ** STEP)
    v_hat = v / (1.0 - B2 ** STEP)
    return -LR * (m_hat / (jnp.sqrt(v_hat) + AEPS) + WD * w), m, v


def _sum_adamw(parts, w, m, v, name):
    _, r, n = parts.shape
    tr = 256 if r % 256 == 0 else r

    def body(p_ref, w_ref, m_ref, v_ref, g_ref, d_ref, nm_ref, nv_ref):
        g = p_ref[0].astype(F32)
        for dev in range(1, N_DEV):
            g = g + p_ref[dev].astype(F32)
        g_ref[...] = g
        d_ref[...], nm_ref[...], nv_ref[...] = _adamw_math(w_ref[...], g, m_ref[...], v_ref[...])

    rows = pl.BlockSpec((None, tr, n), lambda i: (0, i, 0))
    sd = jax.ShapeDtypeStruct((1, r, n), F32)
    return pl.pallas_call(
        body, name=name, grid=(r // tr,), out_shape=(sd, sd, sd, sd),
        in_specs=[pl.BlockSpec((N_DEV, tr, n), lambda i: (0, i, 0)), rows, rows, rows],
        out_specs=(rows, rows, rows, rows),
        compiler_params=_cparams(("parallel",)),
    )(parts, w, m, v)


def _ada_fwd(c_all, ada_w, ada_b_cols):
    nb, ncol = c_all.shape[0], ada_w.shape[1]

    def body(c_ref, w_ref, b_ref, mod_ref, cond_ref):
        cond = _silu(c_ref[...])
        cond_ref[...] = cond
        mod_ref[...] = _dot_hi(cond, w_ref[...]) + b_ref[...]

    return pl.pallas_call(
        body, name="ada_fwd",
        out_shape=(jax.ShapeDtypeStruct((nb, ncol), F32), jax.ShapeDtypeStruct((nb, D), F32)),
        compiler_params=_cparams(),
    )(c_all, ada_w, ada_b_cols)


def _ada_bwd(cond_all, dmod_all, dmod_cols, smalls):
    ncol, nsm = dmod_cols.shape[1], smalls.shape[1]

    def body(cond_ref, dm_ref, dmc_ref, sm_ref, gw_ref, gb_ref, gs_ref):
        gw_ref[...] = lax.dot_general(cond_ref[...], dmc_ref[...], (((0,), (0,)), ((), ())),
                                      preferred_element_type=F32, precision=HI)
        gb_ref[...] = jnp.sum(dm_ref[...], axis=0, keepdims=True)
        gs_ref[...] = jnp.sum(sm_ref[...], axis=0, keepdims=True)

    return pl.pallas_call(
        body, name="ada_bwd",
        out_shape=(jax.ShapeDtypeStruct((D, ncol), F32), jax.ShapeDtypeStruct((1, 6 * D), F32),
                   jax.ShapeDtypeStruct((1, nsm), F32)),
        compiler_params=_cparams(),
    )(cond_all, dmod_all, dmod_cols, smalls)


IN_CUTS = (0, QW, QW + 2 * KVW, QW + 2 * KVW + CONVW, QW + 2 * KVW + CONVW + 2 * DH,
           QW + 2 * KVW + CONVW + 2 * DH + DNW, QW + 2 * KVW + CONVW + 2 * DH + DNW + D, IN_W)
IN_WIDTHS = tuple(b - a for a, b in zip(IN_CUTS[:-1], IN_CUTS[1:]))
IN_SHARD = IN_W // N_DEV


def _inproj_fwd(x, mod, g1, w_blk):
    B, S, _ = x.shape
    tm = _tile(S)

    def body(x_ref, mod_ref, g_ref, w_ref, h_ref, *o_refs):
        h = _rms_mod(x_ref[...], g_ref[...], mod_ref[1:2, :], mod_ref[0:1, :]).astype(BF16)
        h_ref[...] = h
        full = jnp.concatenate([jnp.dot(h, w_ref[j], preferred_element_type=F32) for j in range(N_DEV)], axis=1)
        for o_ref, lo, hi in zip(o_refs, IN_CUTS[:-1], IN_CUTS[1:]):
            o_ref[...] = full[:, lo:hi]

    return pl.pallas_call(
        body, name="inproj_fwd", grid=(B, S // tm),
        out_shape=[jax.ShapeDtypeStruct((B, S, D), BF16)] + [jax.ShapeDtypeStruct((B, S, w), F32) for w in IN_WIDTHS],
        in_specs=[_rows(tm, D), _perb(6, D), _full((1, D)), _resident(w_blk.shape)],
        out_specs=[_rows(tm, D)] + [_rows(tm, w) for w in IN_WIDTHS],
        compiler_params=_cparams(("parallel", "arbitrary")),
    )(x, mod, g1, w_blk)


def _inproj_bwd(x, mod, g1, dx1, dps, w_blk):
    B, S, _ = x.shape
    tm = _tile(S)
    n = len(dps)

    def body(x_ref, mod_ref, g_ref, dx1_ref, *refs):
        dp_refs, w_ref = refs[:n], refs[n]
        dblk_ref, gx_ref, dg_ref, dsc_ref, dsh_ref = refs[n + 1:]
        b, i = pl.program_id(0), pl.program_id(1)
        full = jnp.concatenate([r[...].astype(F32) for r in dp_refs], axis=1)
        dh = None
        for j in range(N_DEV):
            blk = full[:, IN_SHARD * j:IN_SHARD * (j + 1)].astype(BF16)
            dblk_ref[j] = blk
            t = _dot_nt(blk, w_ref[j])
            dh = t if dh is None else dh + t
        _, vjp = jax.vjp(_rms_mod, x_ref[...], g_ref[...], mod_ref[1:2, :], mod_ref[0:1, :])
        dx, dg, dsc, dsh = vjp(dh)
        gx_ref[...] = dx1_ref[...] + dx

        @pl.when((b == 0) & (i == 0))
        def _():
            dg_ref[...] = jnp.zeros_like(dg_ref)

        @pl.when(i == 0)
        def _():
            dsc_ref[...] = jnp.zeros_like(dsc_ref)
            dsh_ref[...] = jnp.zeros_like(dsh_ref)

        dg_ref[...] += dg
        dsc_ref[...] += dsc
        dsh_ref[...] += dsh

    return pl.pallas_call(
        body, name="inproj_bwd", grid=(B, S // tm),
        out_shape=[jax.ShapeDtypeStruct((B, N_DEV, S, IN_SHARD), BF16), jax.ShapeDtypeStruct((B, S, D), F32),
                   jax.ShapeDtypeStruct((1, D), F32), jax.ShapeDtypeStruct((B, 1, D), F32),
                   jax.ShapeDtypeStruct((B, 1, D), F32)],
        in_specs=[_rows(tm, D), _perb(6, D), _full((1, D)), _rows(tm, D)]
                 + [_rows(tm, w) for w in IN_WIDTHS] + [_resident(w_blk.shape)],
        out_specs=[pl.BlockSpec((None, N_DEV, tm, IN_SHARD), lambda b, i: (b, 0, i, 0)), _rows(tm, D),
                   _full((1, D)), _perb(1, D), _perb(1, D)],
        compiler_params=_cparams(("arbitrary", "arbitrary")),
    )(x, mod, g1, dx1, *dps, w_blk)


def _wgrad(a, b, name, after=None):
    B, na, S, K = a.shape
    nb, N = b.shape[1], b.shape[3]
    G = max(na, nb)
    tm = min(512, S)
    nt = S // tm
    last = B * nt - 1

    def body(a_ref, b_ref, *rest):
        o_ref, acc = rest[-2:]
        t = pl.program_id(1)

        @pl.when(t == 0)
        def _():
            acc[...] = jnp.zeros_like(acc)

        acc[...] += lax.dot_general(a_ref[...], b_ref[...], (((0,), (0,)), ((), ())), preferred_element_type=F32)

        @pl.when(t == last)
        def _():
            o_ref[...] = acc[...].astype(BF16)

    return pl.pallas_call(
        body, name=name, grid=(G, B * nt),
        out_shape=jax.ShapeDtypeStruct((G, K, N), BF16),
        in_specs=[pl.BlockSpec((None, None, tm, K), lambda g, t: (t // nt, g if na > 1 else 0, t % nt, 0)),
                  pl.BlockSpec((None, None, tm, N), lambda g, t: (t // nt, g if nb > 1 else 0, t % nt, 0))]
                 + ([] if after is None else [pl.BlockSpec(memory_space=pl.ANY)]),
        out_specs=pl.BlockSpec((None, K, N), lambda g, t: (g, 0, 0)),
        scratch_shapes=[pltpu.VMEM((K, N), F32)],
        compiler_params=_cparams(("parallel", "arbitrary")),
    )(*((a, b) if after is None else (a, b, after)))


LANES = 128


def _attn_consts():
    inv_freq = THETA ** (-jnp.arange(0, ROT, 2, dtype=F32) / ROT)
    head = jnp.concatenate([inv_freq, inv_freq, jnp.zeros((HD - ROT,), F32)])
    invf = jnp.tile(head, LANES // HD)[None, :]
    mean_of = lambda w: jnp.asarray(np.kron(np.eye(w // HD), np.full((HD, HD), 1.0 / HD)), BF16)
    return invf, mean_of(QW), mean_of(KVW)


def _rope_tables(pos, invf):
    B, S, _ = pos.shape
    tr = min(1024, S)

    def body(p_ref, f_ref, c_ref, s_ref):
        ang = p_ref[...].astype(F32) * f_ref[...]
        c_ref[...] = jnp.cos(ang)
        s_ref[...] = jnp.sin(ang)

    sd = jax.ShapeDtypeStruct((B, S, LANES), F32)
    return pl.pallas_call(
        body, name="rope_tables", grid=(B, S // tr), out_shape=[sd, sd],
        in_specs=[_rows(tr, 1), _full((1, LANES))], out_specs=[_rows(tr, LANES), _rows(tr, LANES)],
        compiler_params=_cparams(("parallel", "parallel")),
    )(pos, invf)


def _rope_expand(cos, sin, reps):
    lane = lax.broadcasted_iota(jnp.int32, cos.shape, 1) % HD
    sa = jnp.where((lane >= ROT // 2) & (lane < ROT), sin, 0.0)
    sb = jnp.where(lane < ROT // 2, -sin, 0.0)
    rep = lambda t: jnp.concatenate([t] * reps, axis=1) if reps > 1 else t
    return rep(cos), rep(sa), rep(sb)


@jax.custom_vjp
def _rope(t, cos, sa, sb):
    w = t.shape[1]
    return t * cos + pltpu.roll(t, ROT // 2, 1) * sa + pltpu.roll(t, w - ROT // 2, 1) * sb


def _rope_fwd(t, cos, sa, sb):
    return _rope(t, cos, sa, sb), (cos, sa, sb)


def _rope_bwd(res, d):
    cos, sa, sb = res
    w = d.shape[1]
    dt = d * cos + pltpu.roll(d * sa, w - ROT // 2, 1) + pltpu.roll(d * sb, ROT // 2, 1)
    return dt, jnp.zeros_like(cos), jnp.zeros_like(sa), jnp.zeros_like(sb)


_rope.defvjp(_rope_fwd, _rope_bwd)


def _head_norm(t, g, mean_of):
    hi, lo = _split(t * t)
    ms = jnp.dot(hi, mean_of, preferred_element_type=F32) + jnp.dot(lo, mean_of, preferred_element_type=F32)
    return t * lax.rsqrt(ms + EPS) * g


def _attn_block(q, kvp, kvc, qg, kg, sinks, tq, tk, mq, mk, valid):
    qn = _rope(_head_norm(q, jnp.concatenate([qg] * HQ, axis=1), mq), *tq) * (HD ** -0.5)
    kv = jnp.concatenate([kvp, kvc], axis=0)
    kn = _rope(_head_norm(kv[:, 0:KVW], jnp.concatenate([kg] * HKV, axis=1), mk), *tk)
    per_tile = LANES // HD
    vT = jnp.transpose(kv[:, KVW:2 * KVW])
    qT = [jnp.transpose(qn[:, LANES * t:LANES * (t + 1)]) for t in range(QW // LANES)]
    head_T = lambda h: qT[h // per_tile][HD * (h % per_tile):HD * (h % per_tile + 1), :]
    none = jnp.zeros((HD, GRP * BLK), F32)
    o_T = []
    for j in range(HKV):
        q4T = jnp.concatenate([head_T(GRP * j + i) for i in range(GRP)], axis=1)
        sT = _dot(kn, jnp.concatenate([q4T, none] if j == 0 else [none, q4T], axis=0))
        sT = jnp.where(valid, sT, -1e30)
        sink = jnp.concatenate([jnp.broadcast_to(sinks[:, GRP * j + i:GRP * j + i + 1], (1, BLK)) for i in range(GRP)], axis=1)
        m = lax.stop_gradient(jnp.maximum(jnp.max(sT, axis=0, keepdims=True), sink))
        pT = jnp.exp(sT - m)
        den = jnp.sum(pT, axis=0, keepdims=True) + jnp.exp(sink - m)
        oT = _dot(vT[HD * j:HD * (j + 1), :], pT) * (1.0 / den)
        o_T += [oT[:, BLK * i:BLK * (i + 1)] for i in range(GRP)]
    return jnp.concatenate([jnp.transpose(jnp.concatenate(o_T[per_tile * t:per_tile * (t + 1)], axis=0))
                            for t in range(QW // LANES)], axis=1)


def _attn_tables(cp_ref, cc_ref, sp_ref, sc_ref, n):
    tq = _rope_expand(cc_ref[...], sc_ref[...], QW // LANES)
    tk = _rope_expand(jnp.concatenate([cp_ref[...], cc_ref[...]], axis=0),
                      jnp.concatenate([sp_ref[...], sc_ref[...]], axis=0), KVW // LANES)
    qi = lax.broadcasted_iota(jnp.int32, (2 * BLK, GRP * BLK), 1) % BLK + BLK
    kj = lax.broadcasted_iota(jnp.int32, (2 * BLK, GRP * BLK), 0)
    dist = qi - kj
    valid = (dist >= 0) & (dist < BLK) & ((kj >= BLK) | (n > 0))
    return tq, tk, valid


def _attn_fwd(aq, akv, cos, sin, qg, kg, sinks, mq, mk):
    B, S, _ = aq.shape
    nb = S // BLK

    def body(q_ref, kvp_ref, kvc_ref, cp_ref, cc_ref, sp_ref, sc_ref, qg_ref, kg_ref, sk_ref, mq_ref, mk_ref, o_ref):
        tq, tk, valid = _attn_tables(cp_ref, cc_ref, sp_ref, sc_ref, pl.program_id(1))
        o_ref[...] = _attn_block(q_ref[...], kvp_ref[...], kvc_ref[...], qg_ref[...], kg_ref[...], sk_ref[...],
                                 tq, tk, mq_ref[...], mk_ref[...], valid)

    prev = lambda b, n: (b, jnp.maximum(n - 1, 0), 0)
    cur = lambda b, n: (b, n, 0)
    return pl.pallas_call(
        body, name="attn_fwd", grid=(B, nb),
        out_shape=jax.ShapeDtypeStruct((B, S, QW), F32),
        in_specs=[pl.BlockSpec((None, BLK, QW), cur), pl.BlockSpec((None, BLK, 2 * KVW), prev),
                  pl.BlockSpec((None, BLK, 2 * KVW), cur), pl.BlockSpec((None, BLK, LANES), prev),
                  pl.BlockSpec((None, BLK, LANES), cur), pl.BlockSpec((None, BLK, LANES), prev),
                  pl.BlockSpec((None, BLK, LANES), cur), _full((1, HD)), _full((1, HD)), _full((1, HQ)),
                  _full((QW, QW)), _full((KVW, KVW))],
        out_specs=pl.BlockSpec((None, BLK, QW), cur),
        compiler_params=_cparams(("parallel", "arbitrary")),
    )(aq, akv, akv, cos, cos, sin, sin, qg, kg, sinks, mq, mk)


def _attn_bwd(aq, akv, cos, sin, qg, kg, sinks, mq, mk, do):
    B, S, _ = aq.shape
    nb = S // BLK

    def body(q_ref, kvp_ref, kvc_ref, cp_ref, cc_ref, sp_ref, sc_ref, qg_ref, kg_ref, sk_ref, mq_ref, mk_ref, do_ref,
             dq_ref, dkv_ref, dqg_ref, dkg_ref, dsk_ref, carry):
        b, i = pl.program_id(0), pl.program_id(1)
        tq, tk, valid = _attn_tables(cp_ref, cc_ref, sp_ref, sc_ref, nb - 1 - i)
        fn = functools.partial(_attn_block, tq=tq, tk=tk, mq=mq_ref[...], mk=mk_ref[...], valid=valid)
        _, vjp = jax.vjp(fn, q_ref[...], kvp_ref[...], kvc_ref[...], qg_ref[...], kg_ref[...], sk_ref[...])
        dq, dkvp, dkvc, dqg, dkg, dsk = vjp(do_ref[...])

        @pl.when(i == 0)
        def _():
            carry[...] = jnp.zeros_like(carry)

        @pl.when((b == 0) & (i == 0))
        def _():
            dqg_ref[...] = jnp.zeros_like(dqg_ref)
            dkg_ref[...] = jnp.zeros_like(dkg_ref)
            dsk_ref[...] = jnp.zeros_like(dsk_ref)

        dq_ref[...] = dq.astype(BF16)
        dkv_ref[...] = (dkvc + carry[...]).astype(BF16)
        carry[...] = dkvp
        dqg_ref[...] += dqg
        dkg_ref[...] += dkg
        dsk_ref[...] += dsk

    prev = lambda b, i: (b, jnp.maximum(nb - 2 - i, 0), 0)
    cur = lambda b, i: (b, nb - 1 - i, 0)
    return pl.pallas_call(
        body, name="attn_bwd", grid=(B, nb),
        out_shape=[jax.ShapeDtypeStruct((B, S, QW), BF16), jax.ShapeDtypeStruct((B, S, 2 * KVW), BF16),
                   jax.ShapeDtypeStruct((1, HD), F32), jax.ShapeDtypeStruct((1, HD), F32),
                   jax.ShapeDtypeStruct((1, HQ), F32)],
        in_specs=[pl.BlockSpec((None, BLK, QW), cur), pl.BlockSpec((None, BLK, 2 * KVW), prev),
                  pl.BlockSpec((None, BLK, 2 * KVW), cur), pl.BlockSpec((None, BLK, LANES), prev),
                  pl.BlockSpec((None, BLK, LANES), cur), pl.BlockSpec((None, BLK, LANES), prev),
                  pl.BlockSpec((None, BLK, LANES), cur), _full((1, HD)), _full((1, HD)), _full((1, HQ)),
                  _full((QW, QW)), _full((KVW, KVW)), pl.BlockSpec((None, BLK, QW), cur)],
        out_specs=[pl.BlockSpec((None, BLK, QW), cur), pl.BlockSpec((None, BLK, 2 * KVW), cur),
                   _full((1, HD)), _full((1, HD)), _full((1, HQ))],
        scratch_shapes=[pltpu.VMEM((BLK, 2 * KVW), F32)],
        compiler_params=_cparams(("arbitrary", "arbitrary")),
    )(aq, akv, akv, cos, cos, sin, sin, qg, kg, sinks, mq, mk, do)


def _conv_taps(xe, w, rows):
    y = None
    for j in range(CONV):
        sh = pltpu.roll(xe, CONV - 1 - j, 0)[8:8 + rows, :] if j < CONV - 1 else xe[8:8 + rows, :]
        y = sh * w[j:j + 1, :] if y is None else y + sh * w[j:j + 1, :]
    return y


def _conv_fwd(xin, w):
    B, S, C = xin.shape
    tc = min(512, S)
    r8 = tc // 8

    def body(xp_ref, x_ref, w_ref, o_ref):
        i = pl.program_id(1)
        xp = jnp.where(i > 0, xp_ref[...], 0.0)
        xe = jnp.concatenate([xp, x_ref[...]], axis=0)
        o_ref[...] = _silu(_conv_taps(xe, w_ref[...], tc))

    return pl.pallas_call(
        body, name="conv_fwd", grid=(B, S // tc),
        out_shape=jax.ShapeDtypeStruct((B, S, C), F32),
        in_specs=[pl.BlockSpec((None, 8, C), lambda b, i: (b, jnp.maximum(i * r8 - 1, 0), 0)),
                  _rows(tc, C), _full((CONV, C))],
        out_specs=_rows(tc, C),
        compiler_params=_cparams(("parallel", "arbitrary")),
    )(xin, xin, w)


def _conv_bwd(xin, w, dy):
    B, S, C = xin.shape
    tc = min(512, S)
    r8 = tc // 8
    nt = S // tc

    def body(xp_ref, x_ref, xn_ref, dy_ref, dyn_ref, w_ref, dx_ref, dw_ref):
        b, i = pl.program_id(0), pl.program_id(1)
        w = w_ref[...]
        xp = jnp.where(i > 0, xp_ref[...], 0.0)
        xe = jnp.concatenate([xp, x_ref[...], xn_ref[...]], axis=0)
        pre = _conv_taps(xe, w, tc + 8)
        sg = _sigmoid(pre)
        dyn = jnp.where(i < nt - 1, dyn_ref[...], 0.0)
        dpre = jnp.concatenate([dy_ref[...], dyn], axis=0) * (sg * (1.0 + pre * (1.0 - sg)))
        dx = dpre[0:tc, :] * w[CONV - 1:CONV, :]
        for j in range(CONV - 1):
            dx = dx + pltpu.roll(dpre, tc + 8 - (CONV - 1 - j), 0)[0:tc, :] * w[j:j + 1, :]
        dx_ref[...] = dx.astype(BF16)
        dcur = dpre[0:tc, :]
        xe0 = xe[0:8 + tc, :]
        lane_row = lax.broadcasted_iota(jnp.int32, (CONV, C), 0)
        dw = jnp.zeros((CONV, C), F32)
        for j in range(CONV):
            sh = pltpu.roll(xe0, CONV - 1 - j, 0)[8:8 + tc, :] if j < CONV - 1 else xe0[8:8 + tc, :]
            dw = dw + jnp.where(lane_row == j, jnp.sum(sh * dcur, axis=0, keepdims=True), 0.0)

        @pl.when((b == 0) & (i == 0))
        def _():
            dw_ref[...] = jnp.zeros_like(dw_ref)

        dw_ref[...] += dw

    return pl.pallas_call(
        body, name="conv_bwd", grid=(B, nt),
        out_shape=[jax.ShapeDtypeStruct((B, S, C), BF16), jax.ShapeDtypeStruct((CONV, C), F32)],
        in_specs=[pl.BlockSpec((None, 8, C), lambda b, i: (b, jnp.maximum(i * r8 - 1, 0), 0)),
                  _rows(tc, C),
                  pl.BlockSpec((None, 8, C), lambda b, i: (b, jnp.minimum((i + 1) * r8, S // 8 - 1), 0)),
                  _rows(tc, C),
                  pl.BlockSpec((None, 8, C), lambda b, i: (b, jnp.minimum((i + 1) * r8, S // 8 - 1), 0)),
                  _full((CONV, C))],
        out_specs=[_rows(tc, C), _full((CONV, C))],
        compiler_params=_cparams(("arbitrary", "arbitrary")),
    )(xin, xin, xin, dy, dy, w)


def _softplus(x):
    return jnp.maximum(x, 0.0) + jnp.log1p(jnp.exp(-jnp.abs(x)))


_BMM = (((2,), (1,)), ((0,), (0,)))
_BMM_NT = (((2,), (2,)), ((0,), (0,)))
_BMM_TN = (((1,), (1,)), ((0,), (0,)))


def _bmm(a, b, dims=_BMM):
    return lax.dot_general(a.astype(BF16), b.astype(BF16), dims, preferred_element_type=F32)


def _split(a):
    hi = a.astype(BF16)
    return hi, (a - hi.astype(F32)).astype(BF16)


def _bmm3(a, b, dims=_BMM):
    ah, al = _split(a)
    bh, bl = _split(b)
    d = lambda p, q: lax.dot_general(p, q, dims, preferred_element_type=F32)
    return d(ah, bh) + (d(ah, bl) + d(al, bh))


TRI_BASE = 8


def _tri_inverse(L):
    ii = lax.broadcasted_iota(jnp.int32, (CH, CH), 0)
    jj = lax.broadcasted_iota(jnp.int32, (CH, CH), 1)
    same = lambda size: (ii // size) == (jj // size)
    diag = jnp.where(same(TRI_BASE), L, 0.0)
    X = (ii == jj).astype(F32) - diag
    P = diag
    n = 2
    while n < TRI_BASE:
        P = _bmm3(P, P)
        X = X + _bmm3(X, P)
        n *= 2
    size = TRI_BASE
    while size < CH:
        joint = jnp.where(same(2 * size) & jnp.logical_not(same(size)), L, 0.0)
        X = X - _bmm3(X, _bmm3(joint, X))
        size *= 2
    return X


@jax.custom_vjp
def _tri_inverse_known(L, T):
    return T


def _tri_inverse_known_fwd(L, T):
    return T, T


def _tri_inverse_known_bwd(T, dT):
    return -_bmm3(T, _bmm3(dT, T, _BMM_NT), _BMM_TN), jnp.zeros_like(T)


_tri_inverse_known.defvjp(_tri_inverse_known_fwd, _tri_inverse_known_bwd)


def _cumsum_rows(g):
    n = g.shape[0]
    ii = lax.broadcasted_iota(jnp.int32, (n, CH, CH), 1)
    jj = lax.broadcasted_iota(jnp.int32, (n, CH, CH), 2)
    tri = (ii >= jj).astype(BF16)
    g0 = g.astype(BF16)
    r1 = g - g0.astype(F32)
    g1 = r1.astype(BF16)
    g2 = (r1 - g1.astype(F32)).astype(BF16)
    d = lambda q: lax.dot_general(tri, q, _BMM, preferred_element_type=F32)
    return d(g0) + (d(g1) + d(g2))


def _row_sums(t):
    n, r, w = t.shape
    hi, lo = _split(t.reshape(n * r, w))
    ones = jnp.ones((w, w), BF16)
    s = jnp.dot(hi, ones, preferred_element_type=F32) + jnp.dot(lo, ones, preferred_element_type=F32)
    return s.reshape(n, r, w)


def _dn_prep(t_known, qr, kr, v, a_raw, b_raw, a_log, dt_b):
    n = qr.shape[0]
    ii = lax.broadcasted_iota(jnp.int32, (n, CH, CH), 1)
    jj = lax.broadcasted_iota(jnp.int32, (n, CH, CH), 2)
    incl, strict = ii >= jj, ii > jj
    q = qr * lax.rsqrt(_row_sums(qr * qr) + EPS) * (DK ** -0.5)
    k = kr * lax.rsqrt(_row_sums(kr * kr) + EPS)
    beta = _sigmoid(b_raw)
    g = -jnp.exp(a_log) * _softplus(a_raw + dt_b)
    gcb = _cumsum_rows(jnp.broadcast_to(g, (n, CH, DK)))
    gc = gcb[:, :, 0:1]
    gc_row = jnp.swapaxes(gcb, 1, 2)[:, 0:1, 0:CH]
    decay = jnp.where(incl, jnp.exp(jnp.where(incl, gc - gc_row, 0.0)), 0.0)
    kb = k * beta
    L = jnp.where(strict, _bmm(kb, k, _BMM_NT) * decay, 0.0)
    T = _tri_inverse(L) if t_known is None else _tri_inverse_known(L, t_known)
    eg = jnp.exp(gc)
    u = _bmm(T, v * beta)
    w = _bmm(T, kb * eg)
    a_in = _bmm(q, k, _BMM_NT) * decay
    g_last = gc[:, CH - 1:CH, :]
    return u, w, q * eg, k * jnp.exp(g_last - gc), a_in, jnp.exp(g_last), T


def _dn_step(S0, u, w, qd, kd, a_in, cd):
    r = _bmm(jnp.concatenate([w, qd], axis=1), S0)
    v_new = u - r[:, 0:CH, :]
    o = r[:, CH:2 * CH, :] + _bmm(a_in, v_new)
    S1 = S0 * cd + _bmm(kd, v_new, _BMM_TN)
    return o, S1


def _dn_stack(cq, ba, al, dt, G):
    cols = [[] for _ in range(7)]
    for c in range(G):
        rows = slice(CH * c, CH * (c + 1))
        for h in range(DH):
            parts = (cq[rows, DK * h:DK * (h + 1)], cq[rows, DNW + DK * h:DNW + DK * (h + 1)],
                     cq[rows, 2 * DNW + DK * h:2 * DNW + DK * (h + 1)], ba[rows, DH + h:DH + h + 1],
                     ba[rows, h:h + 1], al[:, h:h + 1], dt[:, h:h + 1])
            for col, p in zip(cols, parts):
                col.append(p)
    return tuple(jnp.stack(col) for col in cols)


def _dn_group(S, want):
    g = want
    while (S // CH) % g:
        g //= 2
    return g


def _dn_prep_fwd(cq, ba, a_log, dt_b):
    B, S, _ = cq.shape
    nc = S // CH
    G = _dn_group(S, 4)

    def body(cq_ref, ba_ref, al_ref, dt_ref, u_ref, w_ref, qd_ref, kd_ref, a_ref, t_ref, cd_ref):
        ops = _dn_stack(cq_ref[...], ba_ref[...], al_ref[...], dt_ref[...], G)
        u, w, qd, kd, a_in, cd, T = _dn_prep(None, *ops)
        lane4 = lax.broadcasted_iota(jnp.int32, (1, DH), 1)
        for c in range(G):
            rows = slice(CH * c, CH * (c + 1))
            cdrow = jnp.zeros((1, DH), F32)
            for h in range(DH):
                n = DH * c + h
                lanes = slice(DK * h, DK * (h + 1))
                u_ref[rows, lanes] = u[n]
                w_ref[rows, lanes] = w[n]
                qd_ref[rows, lanes] = qd[n]
                kd_ref[rows, lanes] = kd[n]
                a_ref[rows, CH * h:CH * (h + 1)] = a_in[n]
                t_ref[rows, CH * h:CH * (h + 1)] = T[n]
                cdrow = cdrow + jnp.where(lane4 == h, cd[n], 0.0)
            cd_ref[c] = cdrow

    wide = jax.ShapeDtypeStruct((B, S, DNW), F32)
    sq = jax.ShapeDtypeStruct((B, S, DH * CH), F32)
    return pl.pallas_call(
        body, name="dn_prep_fwd", grid=(B, nc // G),
        out_shape=[wide, wide, wide, wide, sq, sq, jax.ShapeDtypeStruct((B, nc, 1, DH), F32)],
        in_specs=[_rows(G * CH, CONVW), _rows(G * CH, 2 * DH), _full((1, DH)), _full((1, DH))],
        out_specs=[_rows(G * CH, DNW)] * 4 + [_rows(G * CH, DH * CH)] * 2
                  + [pl.BlockSpec((None, G, 1, DH), lambda b, i: (b, i, 0, 0))],
        compiler_params=_cparams(("parallel", "parallel")),
    )(cq, ba, a_log, dt_b)


def _dn_seq_specs(B, nc, rev):
    at = (lambda i: nc - 1 - i) if rev else (lambda i: i)
    wide = pl.BlockSpec((B, CH, DNW), lambda i: (0, at(i), 0))
    a_spec = pl.BlockSpec((B, CH, DH * CH), lambda i: (0, at(i), 0))
    cd_spec = pl.BlockSpec((B, None, 1, DH), lambda i: (0, at(i), 0, 0))
    st_spec = pl.BlockSpec((B, None, DH, DK, DK), lambda i: (0, at(i), 0, 0, 0))
    return wide, a_spec, cd_spec, st_spec


def _dn_step_operands(B, u_ref, w_ref, qd_ref, kd_ref, a_ref, cd_ref):
    pairs = [(b, h) for b in range(B) for h in range(DH)]
    wide = lambda ref: jnp.stack([ref[b, :, DK * h:DK * (h + 1)] for b, h in pairs])
    a_in = jnp.stack([a_ref[b, :, CH * h:CH * (h + 1)] for b, h in pairs])
    cd = jnp.stack([cd_ref[b, :, h:h + 1] for b, h in pairs])
    return wide(u_ref), wide(w_ref), wide(qd_ref), wide(kd_ref), a_in, cd


def _dn_seq_fwd(u, w, qd, kd, a_in, cd):
    B, S, _ = u.shape
    nc = S // CH

    def body(u_ref, w_ref, qd_ref, kd_ref, a_ref, cd_ref, o_ref, st_ref, state):
        @pl.when(pl.program_id(0) == 0)
        def _():
            state[...] = jnp.zeros_like(state)

        S0 = state[...]
        for b in range(B):
            st_ref[b] = S0[DH * b:DH * (b + 1)]
        o, S1 = _dn_step(S0, *_dn_step_operands(B, u_ref, w_ref, qd_ref, kd_ref, a_ref, cd_ref))
        state[...] = S1
        for b in range(B):
            for h in range(DH):
                o_ref[b, :, DK * h:DK * (h + 1)] = o[DH * b + h]

    wide, a_spec, cd_spec, st_spec = _dn_seq_specs(B, nc, False)
    return pl.pallas_call(
        body, name="dn_seq_fwd", grid=(nc,),
        out_shape=[jax.ShapeDtypeStruct((B, S, DNW), F32), jax.ShapeDtypeStruct((B, nc, DH, DK, DK), F32)],
        in_specs=[wide, wide, wide, wide, a_spec, cd_spec],
        out_specs=[wide, st_spec],
        scratch_shapes=[pltpu.VMEM((B * DH, DK, DK), F32)],
        compiler_params=_cparams(("arbitrary",)),
    )(u, w, qd, kd, a_in, cd)


def _dn_seq_bwd(u, w, qd, kd, a_in, cd, states, do):
    B, S, _ = u.shape
    nc = S // CH

    def body(u_ref, w_ref, qd_ref, kd_ref, a_ref, cd_ref, st_ref, do_ref,
             du_ref, dw_ref, dqd_ref, dkd_ref, da_ref, dcd_ref, dstate):
        @pl.when(pl.program_id(0) == 0)
        def _():
            dstate[...] = jnp.zeros_like(dstate)

        lane4 = lax.broadcasted_iota(jnp.int32, (1, DH), 1)
        S0 = jnp.concatenate([st_ref[b] for b in range(B)], axis=0)
        do = jnp.stack([do_ref[b, :, DK * h:DK * (h + 1)] for b in range(B) for h in range(DH)])
        _, vjp = jax.vjp(_dn_step, S0, *_dn_step_operands(B, u_ref, w_ref, qd_ref, kd_ref, a_ref, cd_ref))
        dS, du, dw, dqd, dkd, da, dcd = vjp((do, dstate[...]))
        dstate[...] = dS
        for b in range(B):
            dcdrow = jnp.zeros((1, DH), F32)
            for h in range(DH):
                n = DH * b + h
                lanes = slice(DK * h, DK * (h + 1))
                du_ref[b, :, lanes] = du[n]
                dw_ref[b, :, lanes] = dw[n]
                dqd_ref[b, :, lanes] = dqd[n]
                dkd_ref[b, :, lanes] = dkd[n]
                da_ref[b, :, CH * h:CH * (h + 1)] = da[n]
                dcdrow = dcdrow + jnp.where(lane4 == h, dcd[n], 0.0)
            dcd_ref[b] = dcdrow

    wide, a_spec, cd_spec, st_spec = _dn_seq_specs(B, nc, True)
    sd = jax.ShapeDtypeStruct((B, S, DNW), F32)
    return pl.pallas_call(
        body, name="dn_seq_bwd", grid=(nc,),
        out_shape=[sd, sd, sd, sd, jax.ShapeDtypeStruct((B, S, DH * CH), F32), jax.ShapeDtypeStruct((B, nc, 1, DH), F32)],
        in_specs=[wide, wide, wide, wide, a_spec, cd_spec, st_spec, wide],
        out_specs=[wide, wide, wide, wide, a_spec, cd_spec],
        scratch_shapes=[pltpu.VMEM((B * DH, DK, DK), F32)],
        compiler_params=_cparams(("arbitrary",)),
    )(u, w, qd, kd, a_in, cd, states, do)


def _dn_prep_bwd(cq, ba, a_log, dt_b, t_inv, du, dw, dqd, dkd, da, dcd):
    B, S, _ = cq.shape
    nc = S // CH
    G = _dn_group(S, 4)

    def body(cq_ref, ba_ref, al_ref, dt_ref, t_ref, du_ref, dw_ref, dqd_ref, dkd_ref, da_ref, dcd_ref,
             dcq_ref, dba_ref, dal_ref, ddt_ref):
        @pl.when((pl.program_id(0) == 0) & (pl.program_id(1) == 0))
        def _():
            dal_ref[...] = jnp.zeros_like(dal_ref)
            ddt_ref[...] = jnp.zeros_like(ddt_ref)

        pairs = [(c, h) for c in range(G) for h in range(DH)]
        rows = lambda c: slice(CH * c, CH * (c + 1))
        wide = lambda ref: jnp.stack([ref[rows(c), DK * h:DK * (h + 1)] for c, h in pairs])
        square = lambda ref: jnp.stack([ref[rows(c), CH * h:CH * (h + 1)] for c, h in pairs])
        ops = _dn_stack(cq_ref[...], ba_ref[...], al_ref[...], dt_ref[...], G)
        cots = (wide(du_ref), wide(dw_ref), wide(dqd_ref), wide(dkd_ref), square(da_ref),
                jnp.stack([dcd_ref[c][:, h:h + 1] for c, h in pairs]), jnp.zeros((len(pairs), CH, CH), F32))
        _, vjp = jax.vjp(functools.partial(_dn_prep, square(t_ref)), *ops)
        dq, dk, dv, dar, dbr, dl, dd = vjp(cots)
        lane8 = lax.broadcasted_iota(jnp.int32, (CH, 2 * DH), 1)
        lane4 = lax.broadcasted_iota(jnp.int32, (1, DH), 1)
        dal = jnp.zeros((1, DH), F32)
        ddt = jnp.zeros((1, DH), F32)
        for c in range(G):
            dba = jnp.zeros((CH, 2 * DH), F32)
            for h in range(DH):
                n = DH * c + h
                dcq_ref[rows(c), DK * h:DK * (h + 1)] = dq[n]
                dcq_ref[rows(c), DNW + DK * h:DNW + DK * (h + 1)] = dk[n]
                dcq_ref[rows(c), 2 * DNW + DK * h:2 * DNW + DK * (h + 1)] = dv[n]
                dba = dba + jnp.where(lane8 == h, dbr[n], 0.0) + jnp.where(lane8 == DH + h, dar[n], 0.0)
                dal = dal + jnp.where(lane4 == h, dl[n], 0.0)
                ddt = ddt + jnp.where(lane4 == h, dd[n], 0.0)
            dba_ref[rows(c), :] = dba.astype(BF16)
        dal_ref[...] += dal
        ddt_ref[...] += ddt

    return pl.pallas_call(
        body, name="dn_prep_bwd", grid=(B, nc // G),
        out_shape=[jax.ShapeDtypeStruct((B, S, CONVW), F32), jax.ShapeDtypeStruct((B, S, 2 * DH), BF16),
                   jax.ShapeDtypeStruct((1, DH), F32), jax.ShapeDtypeStruct((1, DH), F32)],
        in_specs=[_rows(G * CH, CONVW), _rows(G * CH, 2 * DH), _full((1, DH)), _full((1, DH)), _rows(G * CH, DH * CH)]
                 + [_rows(G * CH, DNW)] * 4 + [_rows(G * CH, DH * CH),
                                               pl.BlockSpec((None, G, 1, DH), lambda b, i: (b, i, 0, 0))],
        out_specs=[_rows(G * CH, CONVW), _rows(G * CH, 2 * DH), _full((1, DH)), _full((1, DH))],
        compiler_params=_cparams(("arbitrary", "arbitrary")),
    )(cq, ba, a_log, dt_b, t_inv, du, dw, dqd, dkd, da, dcd)


def _gated_norm(o, z, g):
    outs = []
    for h in range(DH):
        t = o[:, DK * h:DK * (h + 1)]
        r = lax.rsqrt(jnp.mean(t * t, axis=-1, keepdims=True) + EPS)
        outs.append(t * r * g * _silu(z[:, DK * h:DK * (h + 1)]))
    return jnp.concatenate(outs, axis=1)


def _mix_fwd(x, o_attn, o_dn, z, ga, gd, mod, dn_g, w_branch, w_out):
    B, S, _ = x.shape
    tm = _tile(S, 512)

    def body(x_ref, oa_ref, od_ref, z_ref, ga_ref, gd_ref, mod_ref, g_ref, wb_ref, wo_ref,
             x1_ref, mix_ref, mg_ref, ob_ref):
        oa = oa_ref[...].astype(BF16)
        od = _gated_norm(od_ref[...], z_ref[...], g_ref[...]).astype(BF16)
        ob_ref[0] = oa
        ob_ref[1] = od
        ya = jnp.dot(oa, wb_ref[0:QW, :], preferred_element_type=F32)
        yd = jnp.dot(od, wb_ref[QW:QW + DNW, :], preferred_element_type=F32)
        merged = (_sigmoid(ga_ref[...]) * ya + _sigmoid(gd_ref[...]) * yd).astype(BF16)
        mg_ref[...] = merged
        mix = jnp.dot(merged, wo_ref[...], preferred_element_type=F32)
        mix_ref[...] = mix
        x1_ref[...] = x_ref[...] + mod_ref[2:3, :] * mix

    return pl.pallas_call(
        body, name="mix_fwd", grid=(B, S // tm),
        out_shape=[jax.ShapeDtypeStruct((B, S, D), F32), jax.ShapeDtypeStruct((B, S, D), F32),
                   jax.ShapeDtypeStruct((B, S, D), BF16), jax.ShapeDtypeStruct((B, 2, S, QW), BF16)],
        in_specs=[_rows(tm, D), _rows(tm, QW), _rows(tm, DNW), _rows(tm, DNW), _rows(tm, D), _rows(tm, D),
                  _perb(6, D), _full((1, DK)), _resident(w_branch.shape), _resident(w_out.shape)],
        out_specs=[_rows(tm, D), _rows(tm, D), _rows(tm, D), _stacked(2, tm, QW)],
        compiler_params=_cparams(("parallel", "arbitrary")),
    )(x, o_attn, o_dn, z, ga, gd, mod, dn_g, w_branch, w_out)


def _mix_bwd(dx1, mix, o_attn, o_dn, z, ga, gd, mod, dn_g, w_branch, w_out):
    B, S, _ = dx1.shape
    tm = _tile(S)

    def body(dx1_ref, mix_ref, oa_ref, od_ref, z_ref, ga_ref, gd_ref, mod_ref, g_ref, wb_ref, wo_ref,
             dmix_ref, dyo_ref, dga_ref, dgd_ref, dz_ref, doa_ref, dod_ref, dgate_ref, dg_ref):
        b, i = pl.program_id(0), pl.program_id(1)
        dx1 = dx1_ref[...]
        dmix = (dx1 * mod_ref[2:3, :]).astype(BF16)
        dmix_ref[...] = dmix
        dgate = jnp.sum(dx1 * mix_ref[...], axis=0, keepdims=True)
        dmerged = _dot_nt(dmix, wo_ref[...])
        odn, gn_vjp = jax.vjp(_gated_norm, od_ref[...], z_ref[...], g_ref[...])
        ya = _dot(oa_ref[...], wb_ref[0:QW, :])
        yd = _dot(odn, wb_ref[QW:QW + DNW, :])
        sa, sd = _sigmoid(ga_ref[...]), _sigmoid(gd_ref[...])
        dya = (dmerged * sa).astype(BF16)
        dyd = (dmerged * sd).astype(BF16)
        dyo_ref[0] = dya
        dyo_ref[1] = dyd
        dga_ref[...] = (dmerged * ya * sa * (1.0 - sa)).astype(BF16)
        dgd_ref[...] = (dmerged * yd * sd * (1.0 - sd)).astype(BF16)
        doa_ref[...] = _dot_nt(dya, wb_ref[0:QW, :])
        dodn = _dot_nt(dyd, wb_ref[QW:QW + DNW, :])
        dod, dz, dg = gn_vjp(dodn)
        dod_ref[...] = dod
        dz_ref[...] = dz.astype(BF16)

        @pl.when(i == 0)
        def _():
            dgate_ref[...] = jnp.zeros_like(dgate_ref)

        @pl.when((b == 0) & (i == 0))
        def _():
            dg_ref[...] = jnp.zeros_like(dg_ref)

        dgate_ref[...] += dgate
        dg_ref[...] += dg

    return pl.pallas_call(
        body, name="mix_bwd", grid=(B, S // tm),
        out_shape=[jax.ShapeDtypeStruct((B, S, D), BF16), jax.ShapeDtypeStruct((B, 2, S, D), BF16),
                   jax.ShapeDtypeStruct((B, S, D), BF16), jax.ShapeDtypeStruct((B, S, D), BF16),
                   jax.ShapeDtypeStruct((B, S, DNW), BF16),
                   jax.ShapeDtypeStruct((B, S, QW), F32), jax.ShapeDtypeStruct((B, S, DNW), F32),
                   jax.ShapeDtypeStruct((B, 1, D), F32), jax.ShapeDtypeStruct((1, DK), F32)],
        in_specs=[_rows(tm, D), _rows(tm, D), _rows(tm, QW), _rows(tm, DNW), _rows(tm, DNW), _rows(tm, D),
                  _rows(tm, D), _perb(6, D), _full((1, DK)), _resident(w_branch.shape), _resident(w_out.shape)],
        out_specs=[_rows(tm, D), _stacked(2, tm, D), _rows(tm, D), _rows(tm, D), _rows(tm, DNW),
                   _rows(tm, QW), _rows(tm, DNW), _perb(1, D), _full((1, DK))],
        compiler_params=_cparams(("arbitrary", "arbitrary")),
    )(dx1, mix, o_attn, o_dn, z, ga, gd, mod, dn_g, w_branch, w_out)


GU_SHARD = 2 * FFN // N_DEV
GU_HALF = N_DEV // 2


def _ffn1_fwd(x1, mod, g2, w_gu):
    B, S, _ = x1.shape
    tm = _tile(S)

    def body(x_ref, mod_ref, g_ref, w_ref, h_ref, gate_ref, up_ref, act_ref):
        h = _rms_mod(x_ref[...], g_ref[...], mod_ref[4:5, :], mod_ref[3:4, :]).astype(BF16)
        h_ref[...] = h
        for j in range(GU_HALF):
            gate = jnp.dot(h, w_ref[j], preferred_element_type=F32)
            up = jnp.dot(h, w_ref[GU_HALF + j], preferred_element_type=F32)
            gate_ref[j] = gate
            up_ref[j] = up
            act_ref[j] = (_silu(gate) * up).astype(BF16)

    blk = lambda dt: jax.ShapeDtypeStruct((B, GU_HALF, S, GU_SHARD), dt)
    return pl.pallas_call(
        body, name="ffn1_fwd", grid=(B, S // tm),
        out_shape=[jax.ShapeDtypeStruct((B, S, D), BF16), blk(F32), blk(F32), blk(BF16)],
        in_specs=[_rows(tm, D), _perb(6, D), _full((1, D)), _resident(w_gu.shape)],
        out_specs=[_rows(tm, D)] + [_stacked(GU_HALF, tm, GU_SHARD)] * 3,
        compiler_params=_cparams(("parallel", "arbitrary")),
    )(x1, mod, g2, w_gu)


def _ffn2_fwd(act, x1, target, mod, w_down):
    B, S, _ = x1.shape
    tm = _tile(S, 512)

    def body(a_ref, x_ref, t_ref, mod_ref, w_ref, dy_ref, loss_ref, dgate_ref):
        b, i = pl.program_id(0), pl.program_id(1)
        y = jnp.dot(a_ref[0], w_ref[0], preferred_element_type=F32)
        for j in range(1, GU_HALF):
            y = y + jnp.dot(a_ref[j], w_ref[j], preferred_element_type=F32)
        err = x_ref[...] + mod_ref[5:6, :] * y - t_ref[...]
        dy = err * (1.0 / D)
        dy_ref[...] = dy

        @pl.when((b == 0) & (i == 0))
        def _():
            loss_ref[...] = jnp.zeros_like(loss_ref)

        @pl.when(i == 0)
        def _():
            dgate_ref[...] = jnp.zeros_like(dgate_ref)

        loss_ref[...] += (0.5 / D) * jnp.sum(err * err)
        dgate_ref[...] += jnp.sum(dy * y, axis=0, keepdims=True)

    return pl.pallas_call(
        body, name="ffn2_fwd", grid=(B, S // tm),
        out_shape=[jax.ShapeDtypeStruct((B, S, D), F32), jax.ShapeDtypeStruct((1, 128), F32),
                   jax.ShapeDtypeStruct((B, 1, D), F32)],
        in_specs=[_stacked(GU_HALF, tm, GU_SHARD), _rows(tm, D), _rows(tm, D), _perb(6, D), _resident(w_down.shape)],
        out_specs=[_rows(tm, D), _full((1, 128)), _perb(1, D)],
        compiler_params=_cparams(("arbitrary", "arbitrary")),
    )(act, x1, target, mod, w_down)


def _ffn2_bwd(dy, gate, up, mod, w_down):
    B, S, _ = dy.shape
    tm = _tile(S)

    def body(dy_ref, gate_ref, up_ref, mod_ref, w_ref, dgu_ref, dyg_ref):
        dyg = (dy_ref[...] * mod_ref[5:6, :]).astype(BF16)
        dyg_ref[...] = dyg
        for j in range(GU_HALF):
            dact = _dot_nt(dyg, w_ref[j])
            gate, up = gate_ref[j], up_ref[j]
            sg = _sigmoid(gate)
            dgu_ref[j] = (dact * up * (sg * (1.0 + gate * (1.0 - sg)))).astype(BF16)
            dgu_ref[GU_HALF + j] = (dact * (gate * sg)).astype(BF16)

    return pl.pallas_call(
        body, name="ffn2_bwd", grid=(B, S // tm),
        out_shape=[jax.ShapeDtypeStruct((B, N_DEV, S, GU_SHARD), BF16), jax.ShapeDtypeStruct((B, S, D), BF16)],
        in_specs=[_rows(tm, D), _stacked(GU_HALF, tm, GU_SHARD), _stacked(GU_HALF, tm, GU_SHARD), _perb(6, D),
                  _resident(w_down.shape)],
        out_specs=[_stacked(N_DEV, tm, GU_SHARD), _rows(tm, D)],
        compiler_params=_cparams(("parallel", "arbitrary")),
    )(dy, gate, up, mod, w_down)


def _ffn1_bwd(dgu, x1, dy, mod, g2, w_gu):
    B, S, _ = x1.shape
    tm = _tile(S, 512)

    def body(dgu_ref, x_ref, dy_ref, mod_ref, g_ref, w_ref, dx1_ref, dg_ref, dsc_ref, dsh_ref):
        b, i = pl.program_id(0), pl.program_id(1)
        dh = _dot_nt(dgu_ref[0], w_ref[0])
        for j in range(1, N_DEV):
            dh = dh + _dot_nt(dgu_ref[j], w_ref[j])
        _, vjp = jax.vjp(_rms_mod, x_ref[...], g_ref[...], mod_ref[4:5, :], mod_ref[3:4, :])
        dx, dg, dsc, dsh = vjp(dh)
        dx1_ref[...] = dy_ref[...] + dx

        @pl.when((b == 0) & (i == 0))
        def _():
            dg_ref[...] = jnp.zeros_like(dg_ref)

        @pl.when(i == 0)
        def _():
            dsc_ref[...] = jnp.zeros_like(dsc_ref)
            dsh_ref[...] = jnp.zeros_like(dsh_ref)

        dg_ref[...] += dg
        dsc_ref[...] += dsc
        dsh_ref[...] += dsh

    return pl.pallas_call(
        body, name="ffn1_bwd", grid=(B, S // tm),
        out_shape=[jax.ShapeDtypeStruct((B, S, D), F32), jax.ShapeDtypeStruct((1, D), F32),
                   jax.ShapeDtypeStruct((B, 1, D), F32), jax.ShapeDtypeStruct((B, 1, D), F32)],
        in_specs=[_stacked(N_DEV, tm, GU_SHARD), _rows(tm, D), _rows(tm, D), _perb(6, D), _full((1, D)),
                  _resident(w_gu.shape)],
        out_specs=[_rows(tm, D), _full((1, D)), _perb(1, D), _perb(1, D)],
        compiler_params=_cparams(("arbitrary", "arbitrary")),
    )(dgu, x1, dy, mod, g2, w_gu)


def _adamw(w, g, m, v, name):
    def body(w_ref, g_ref, m_ref, v_ref, d_ref, nm_ref, nv_ref):
        d_ref[...], nm_ref[...], nv_ref[...] = _adamw_math(w_ref[...], g_ref[...], m_ref[...], v_ref[...])

    sd = jax.ShapeDtypeStruct(w.shape, F32)
    return pl.pallas_call(body, name=name, out_shape=(sd, sd, sd), compiler_params=_cparams())(w, g, m, v)


def kernel(x, c, positions, ada_w, ada_b, norm1_g, w_in, conv_w, q_norm_g, k_norm_g, sinks, a_log, dt_bias, dn_norm_g, w_branch, w_out, norm2_g, w_gate_up, w_down, loss_target, m_ada_w, m_ada_b, m_norm1_g, m_w_in, m_conv_w, m_q_norm_g, m_k_norm_g, m_sinks, m_a_log, m_dt_bias, m_dn_norm_g, m_w_branch, m_w_out, m_norm2_g, m_w_gate_up, m_w_down, v_ada_w, v_ada_b, v_norm1_g, v_w_in, v_conv_w, v_q_norm_g, v_k_norm_g, v_sinks, v_a_log, v_dt_bias, v_dn_norm_g, v_w_branch, v_w_out, v_norm2_g, v_w_gate_up, v_w_down):
    B, S, _ = x.shape
    me = 4 * lax.axis_index("x") + 2 * lax.axis_index("y") + lax.axis_index("c")

    shards = [w[0].astype(BF16) for w in (w_in, w_branch, w_out, w_gate_up, w_down)]

    c_all = _all_gather_small(c, "gather_c").reshape(N_DEV * B, D)
    ncol = 6 * D // N_DEV
    mod_cols, cond_all = _ada_fwd(c_all, ada_w[0], lax.dynamic_slice(ada_b, (0, me * ncol), (1, ncol)))
    mod_all = _all_gather_small(mod_cols, "gather_mod").transpose(1, 0, 2).reshape(N_DEV * B, 6 * D)
    mod = lax.dynamic_slice(mod_all, (me * B, 0), (B, 6 * D)).reshape(B, 6, D)
    conv2 = conv_w.reshape(CONV, CONVW // N_DEV)
    conv_all = _all_gather_small(conv2, "gather_conv").transpose(1, 0, 2).reshape(CONV, CONVW)

    (w_in_b,) = _all_gather_big(shards[:1], "gather_w_in", after=(mod, conv_all))
    w_sems, w_srcs, w_lands, w_token = _copies_start(shards[1:], [_place_own(s, me) for s in shards[1:]], False,
                                                    w_in_b, "gather_rest_start")

    h1, aq, akv, dnx, ba, z, ga, gd = _inproj_fwd(x, mod, norm1_g + w_token[0, 0], w_in_b)
    invf, mean_q, mean_k = _attn_consts()
    rope_cos, rope_sin = _rope_tables(positions.reshape(B, S, 1), invf)
    o_attn = _attn_fwd(aq, akv, rope_cos, rope_sin, q_norm_g, k_norm_g, sinks, mean_q, mean_k)
    cq = _conv_fwd(dnx, conv_all)
    dn_u, dn_w, dn_qd, dn_kd, dn_a, dn_t, dn_cd = _dn_prep_fwd(cq, ba, a_log, dt_bias)
    o_dn, states = _dn_seq_fwd(dn_u, dn_w, dn_qd, dn_kd, dn_a, dn_cd)
    w_branch_g, w_out_g, w_gu_b, w_down_g = _copies_wait(w_sems, w_srcs, w_lands, o_dn, "gather_wait_rest")
    w_branch_f = w_branch_g.reshape(D, D)
    w_out_f = w_out_g.reshape(D, D)
    w_down_b = w_down_g.reshape(GU_HALF, GU_SHARD, D)
    x1, mix, merged, ob = _mix_fwd(x, o_attn, o_dn, z, ga, gd, mod, dn_norm_g, w_branch_f, w_out_f)
    h2, gate, up, act = _ffn1_fwd(x1, mod, norm2_g, w_gu_b)
    dy, loss_part, d_gate2 = _ffn2_fwd(act, x1, loss_target, mod, w_down_b)
    loss = lax.psum(loss_part[0, 0], ("x", "y", "c"))

    one = lambda t: t.reshape(B, 1, S, t.shape[-1])
    dgu, dyg = _ffn2_bwd(dy, gate, up, mod, w_down_b)
    g_w_down = _wgrad(act, one(dyg), "wgrad_down")
    dx1, d_n2g, d_scale2, d_shift2 = _ffn1_bwd(dgu, x1, dy, mod, norm2_g, w_gu_b)
    g_w_gu = _wgrad(one(h2), dgu, "wgrad_gate_up")
    ffn = _exchange_start([g_w_gu, g_w_down.reshape(N_DEV, FFN // N_DEV, D)], me, dx1, "exchange_ffn_start")
    dmix, dyo, dga, dgd, dz, d_oa, d_od, d_gate1, d_dng = _mix_bwd(
        dx1, mix, o_attn, o_dn, z, ga, gd, mod, dn_norm_g + ffn[3][0, 0], w_branch_f, w_out_f)
    d_dn = _dn_seq_bwd(dn_u, dn_w, dn_qd, dn_kd, dn_a, dn_cd, states, d_od)
    dcq, dba, d_alog, d_dtb = _dn_prep_bwd(cq, ba, a_log, dt_bias, dn_t, *d_dn)
    ddnx, d_conv = _conv_bwd(dnx, conv_all, dcq)
    daq, dakv, d_qg, d_kg, d_sinks = _attn_bwd(aq, akv, rope_cos, rope_sin, q_norm_g, k_norm_g, sinks, mean_q, mean_k, d_oa)
    dps = [daq, dakv, ddnx, dba, dz, dga, dgd]
    dblk, grad_x, d_n1g, d_scale1, d_shift1 = _inproj_bwd(x, mod, norm1_g, dx1, dps, w_in_b)

    dmod = jnp.concatenate([d_shift1, d_scale1, d_gate1, d_shift2, d_scale2, d_gate2], axis=2).reshape(B, 6 * D)
    small = jnp.concatenate([d_n1g, d_qg, d_kg, d_sinks, d_alog, d_dtb, d_dng, d_n2g, d_conv.reshape(1, CONV * CONVW)], axis=1)
    nsm = small.shape[1]
    width = -(-max(6 * D, nsm) // 128) * 128
    rows = jnp.concatenate([jnp.pad(dmod, ((0, 0), (0, width - 6 * D))), jnp.pad(small, ((0, 8 - B - 1), (0, width - nsm)))], axis=0)
    rows_all = _all_gather_small(rows, "gather_small")
    dmod_all = rows_all[:, 0:B, 0:6 * D].reshape(N_DEV * B, 6 * D)
    dmod_cols = lax.dynamic_slice(dmod_all, (0, me * ncol), (N_DEV * B, ncol))
    grad_ada_w, grad_ada_b, small_sum = _ada_bwd(cond_all, dmod_all, dmod_cols, rows_all[:, B, :])
    sizes = [D, HD, HD, HQ, DH, DH, DK, D]
    so = np.cumsum([0] + sizes)
    g_n1, g_qg, g_kg, g_sk, g_al, g_dt, g_dn, g_n2 = [small_sum[:, so[i]:so[i + 1]] for i in range(8)]
    g_conv_all = small_sum[:, so[8]:so[8] + CONV * CONVW].reshape(CONV, N_DEV, CONVW // N_DEV)
    grad_conv = lax.dynamic_slice(g_conv_all, (0, me, 0), (CONV, 1, CONVW // N_DEV)).reshape(CONV, CONVW // N_DEV)

    g_w_in = _wgrad(one(h1), dblk, "wgrad_in", after=small_sum)
    proj = _exchange_start([g_w_in], me, small_sum, "exchange_in_start")
    g_w_out = _wgrad(one(merged), one(dmix), "wgrad_out", after=proj[3])
    g_w_branch = _wgrad(ob, dyo, "wgrad_branch", after=proj[3])
    mixer = _exchange_start([g_w_branch.reshape(N_DEV, D // N_DEV, D), g_w_out.reshape(N_DEV, D // N_DEV, D)], me,
                            proj[3], "exchange_mix_start")

    upd, grads = {}, {}

    def finish(names, parts, weights):
        for nm, p, (w, m, v) in zip(names, parts, weights):
            grads[nm], *upd[nm] = _sum_adamw(p, w, m, v, "update_" + nm)

    finish(["w_gate_up", "w_down"], _copies_wait(*ffn[:3], mixer[3], "exchange_ffn_wait"),
           [(w_gate_up, m_w_gate_up, v_w_gate_up), (w_down, m_w_down, v_w_down)])
    finish(["w_in"], _copies_wait(*proj[:3], grads["w_gate_up"], "exchange_in_wait"), [(w_in, m_w_in, v_w_in)])
    finish(["w_branch", "w_out"], _copies_wait(*mixer[:3], grads["w_in"], "exchange_mix_wait"),
           [(w_branch, m_w_branch, v_w_branch), (w_out, m_w_out, v_w_out)])

    grads["ada_w"] = grad_ada_w.reshape(ada_w.shape)
    upd["ada_w"] = _adamw(ada_w, grads["ada_w"], m_ada_w, v_ada_w, "adamw_ada_w")
    small_names = ["ada_b", "norm1_g", "q_norm_g", "k_norm_g", "sinks", "a_log", "dt_bias", "dn_norm_g", "norm2_g", "conv_w"]
    small_w = [ada_b, norm1_g, q_norm_g, k_norm_g, sinks, a_log, dt_bias, dn_norm_g, norm2_g, conv_w]
    small_g = [grad_ada_b, g_n1, g_qg, g_kg, g_sk, g_al, g_dt, g_dn, g_n2, grad_conv]
    small_m = [m_ada_b, m_norm1_g, m_q_norm_g, m_k_norm_g, m_sinks, m_a_log, m_dt_bias, m_dn_norm_g, m_norm2_g, m_conv_w]
    small_v = [v_ada_b, v_norm1_g, v_q_norm_g, v_k_norm_g, v_sinks, v_a_log, v_dt_bias, v_dn_norm_g, v_norm2_g, v_conv_w]
    cat = lambda arrs: jnp.concatenate([a.reshape(1, -1) for a in arrs], axis=1)
    res = _adamw(cat(small_w), cat(small_g), cat(small_m), cat(small_v), "adamw_small")
    po = np.cumsum([0] + [int(np.prod(w.shape)) for w in small_w])
    for i, nm in enumerate(small_names):
        upd[nm] = tuple(r[:, po[i]:po[i + 1]].reshape(small_w[i].shape) for r in res)
        grads[nm] = small_g[i].reshape(small_w[i].shape)

    order = ["ada_w", "ada_b", "norm1_g", "w_in", "conv_w", "q_norm_g", "k_norm_g", "sinks", "a_log", "dt_bias",
             "dn_norm_g", "w_branch", "w_out", "norm2_g", "w_gate_up", "w_down"]
    return (loss, grad_x, *[grads[n] for n in order], *[upd[n][0] for n in order],
            *[upd[n][1] for n in order], *[upd[n][2] for n in order])
```

```python
import functools

import numpy as np
import jax
import jax.numpy as jnp
from jax import lax
from jax.experimental import pallas as pl
from jax.experimental.pallas import tpu as pltpu

F32 = jnp.float32
BF16 = jnp.bfloat16
HI = lax.Precision.HIGHEST

N_DEV = 8
D = 1024
HQ, HKV, HD = 8, 2, 64
GRP = HQ // HKV
BLK = 128
ROT = HD // 4
THETA = 500000.0
QW, KVW = HQ * HD, HKV * HD
DH, DK = 4, 128
CH = 64
DNW = DH * DK
CONV = 4
CONVW = 3 * DNW
FFN = 2816
EPS = 1e-6
IN_W = QW + 2 * KVW + CONVW + 2 * DH + DNW + 2 * D

LR, B1, B2, AEPS, WD, STEP = 0.001, 0.9, 0.999, 1e-08, 0.01, 10

VMEM_LIMIT = 56 * 1024 * 1024
MESH = pl.DeviceIdType.MESH


def _cparams(sem=None, vmem=VMEM_LIMIT):
    return pltpu.CompilerParams(dimension_semantics=sem, vmem_limit_bytes=vmem)


def _full(shape):
    n = len(shape)
    return pl.BlockSpec(shape, lambda *_: (0,) * n)


def _resident(shape):
    n = len(shape)
    return pl.BlockSpec(shape, lambda *_: (0,) * n, pipeline_mode=pl.Buffered(1))


def _rows(tm, w):
    return pl.BlockSpec((None, tm, w), lambda b, i: (b, i, 0))


def _stacked(n, tm, w):
    return pl.BlockSpec((None, n, tm, w), lambda b, i: (b, 0, i, 0))


def _perb(r, w):
    return pl.BlockSpec((None, r, w), lambda b, i: (b, 0, 0))


def _dot(a, b):
    return jnp.dot(a.astype(BF16), b.astype(BF16), preferred_element_type=F32)


def _dot_nt(a, b):
    return lax.dot_general(a.astype(BF16), b.astype(BF16), (((1,), (1,)), ((), ())), preferred_element_type=F32)


def _dot_tn(a, b):
    return lax.dot_general(a.astype(BF16), b.astype(BF16), (((0,), (0,)), ((), ())), preferred_element_type=F32)


def _dot_hi(a, b):
    return jnp.dot(a, b, preferred_element_type=F32, precision=HI)


def _sigmoid(x):
    return jax.nn.sigmoid(x)


def _silu(x):
    return x * jax.nn.sigmoid(x)


def _rms_mod(x, g, scale, shift):
    r = lax.rsqrt(jnp.mean(x * x, axis=-1, keepdims=True) + EPS)
    return (x * r * g) * (1.0 + scale) + shift


def _tile(S, rows=256):
    return min(rows, S)


def _peer(x, y, c, k):
    px = 1 - x if (k >> 2) & 1 else x
    py = 1 - y if (k >> 1) & 1 else y
    pc = 1 - c if k & 1 else c
    return px, py, pc


def _all_gather_small(v, name):
    r, n = v.shape

    def body(v_ref, out_ref, send_sems, recv_sems, local_sem):
        x, y, c = lax.axis_index("x"), lax.axis_index("y"), lax.axis_index("c")
        me = 4 * x + 2 * y + c
        mine = pltpu.make_async_copy(v_ref, out_ref.at[me], local_sem)
        mine.start()
        sends = []
        for k in range(1, N_DEV):
            cp = pltpu.make_async_remote_copy(
                src_ref=v_ref, dst_ref=out_ref.at[me], send_sem=send_sems.at[k - 1], recv_sem=recv_sems.at[k - 1],
                device_id=_peer(x, y, c, k), device_id_type=MESH)
            cp.start()
            sends.append(cp)
        for k in range(1, N_DEV):
            px, py, pc = _peer(x, y, c, k)
            pltpu.make_async_remote_copy(
                src_ref=v_ref, dst_ref=out_ref.at[4 * px + 2 * py + pc], send_sem=send_sems.at[k - 1],
                recv_sem=recv_sems.at[k - 1], device_id=(px, py, pc), device_id_type=MESH).wait_recv()
        for cp in sends:
            cp.wait_send()
        mine.wait()

    return pl.pallas_call(
        body, name=name,
        out_shape=jax.ShapeDtypeStruct((N_DEV, r, n), v.dtype),
        in_specs=[pl.BlockSpec(memory_space=pltpu.VMEM)],
        out_specs=pl.BlockSpec(memory_space=pltpu.VMEM),
        scratch_shapes=[pltpu.SemaphoreType.DMA((N_DEV - 1,)), pltpu.SemaphoreType.DMA((N_DEV - 1,)), pltpu.SemaphoreType.DMA],
    )(v)


def _all_gather_big(vs, name, after=()):
    na, nf = len(vs), len(after)

    def body(*refs):
        v_refs, out_refs = refs[:na], refs[na + nf:2 * na + nf]
        send_sems, recv_sems, local_sems = refs[2 * na + nf:]
        x, y, c = lax.axis_index("x"), lax.axis_index("y"), lax.axis_index("c")
        me, sibling = (x, y, c), (x, y, 1 - c)
        chips = [(1 - x, y), (x, 1 - y), (1 - x, 1 - y)]

        def rows(a, px, py, pc):
            return out_refs[a].at[4 * px + 2 * py + pc]

        def copy(a, k, block, to, src=None):
            return pltpu.make_async_remote_copy(
                src_ref=rows(a, *block) if src is None else src, dst_ref=rows(a, *block),
                send_sem=send_sems.at[7 * a + k], recv_sem=recv_sems.at[7 * a + k], device_id=to, device_id_type=MESH)

        mine = [pltpu.make_async_copy(v_refs[a], rows(a, *me), local_sems.at[a]) for a in range(na)]
        for cp in mine:
            cp.start()
        first = []
        for a in range(na):
            first.append(copy(a, 0, me, sibling, src=v_refs[a]))
            first += [copy(a, 1 + j, me, (*chip, c), src=v_refs[a]) for j, chip in enumerate(chips)]
        for cp in first:
            cp.start()
        passed = []
        for j, chip in enumerate(chips):
            for a in range(na):
                copy(a, 1 + j, (*chip, c), me).wait_recv()
                forward = copy(a, 4 + j, (*chip, c), sibling)
                forward.start()
                passed.append(forward)
        for a in range(na):
            copy(a, 0, sibling, me).wait_recv()
            for j, chip in enumerate(chips):
                copy(a, 4 + j, (*chip, 1 - c), me).wait_recv()
        for cp in first + passed:
            cp.wait_send()
        for cp in mine:
            cp.wait()

    return pl.pallas_call(
        body, name=name,
        out_shape=[jax.ShapeDtypeStruct((N_DEV,) + v.shape, v.dtype) for v in vs],
        in_specs=[pl.BlockSpec(memory_space=pl.ANY)] * (na + nf),
        out_specs=[pl.BlockSpec(memory_space=pl.ANY)] * na,
        scratch_shapes=[pltpu.SemaphoreType.DMA((7 * na,)), pltpu.SemaphoreType.DMA((7 * na,)),
                        pltpu.SemaphoreType.DMA((na,))],
    )(*vs, *after)


_HBM = pl.BlockSpec(memory_space=pltpu.HBM)
_SEM = pl.BlockSpec(memory_space=pltpu.SEMAPHORE)
_EFFECT = pltpu.SideEffectType.DATAFLOW_SIDE_EFFECTING


def _place_own(block, me):
    land = lax.empty((N_DEV,) + block.shape, block.dtype)
    return lax.dynamic_update_slice(land, block[None], (me,) + (0,) * block.ndim)


def _copies_start(srcs, lands, scatter, after, name):
    na = len(srcs)
    afters = tuple(after) if isinstance(after, (tuple, list)) else (after,)

    def body(*refs):
        src_refs, land_refs = refs[:na], refs[na:2 * na]
        sems = refs[2 * na + len(afters):4 * na + len(afters)]
        token = refs[-1]
        x, y, c = lax.axis_index("x"), lax.axis_index("y"), lax.axis_index("c")
        me = 4 * x + 2 * y + c
        for a in range(na):
            for k in range(1, N_DEV):
                px, py, pc = _peer(x, y, c, k)
                src = src_refs[a].at[4 * px + 2 * py + pc] if scatter else src_refs[a]
                pltpu.make_async_remote_copy(
                    src_ref=src, dst_ref=land_refs[a].at[me], send_sem=sems[2 * a], recv_sem=sems[2 * a + 1],
                    device_id=(px, py, pc), device_id_type=MESH).start()
        token[...] = jnp.zeros_like(token)

    hbm = lambda t: pltpu.HBM(t.shape, t.dtype)
    out = pl.pallas_call(
        body, name=name,
        out_shape=tuple([pltpu.SemaphoreType.DMA(())] * (2 * na) + [hbm(t) for t in srcs] + [hbm(t) for t in lands]
                        + [jax.ShapeDtypeStruct((8, 128), F32)]),
        in_specs=[_HBM] * (2 * na) + [pl.BlockSpec(memory_space=pl.ANY)] * len(afters),
        out_specs=tuple([_SEM] * (2 * na) + [_HBM] * (2 * na) + [pl.BlockSpec(memory_space=pltpu.VMEM)]),
        input_output_aliases={i: 2 * na + i for i in range(2 * na)},
        compiler_params=pltpu.CompilerParams(has_side_effects=_EFFECT),
    )(*[pltpu.with_memory_space_constraint(t, pltpu.HBM) for t in list(srcs) + list(lands)], *afters)
    return out[:2 * na], out[2 * na:3 * na], out[3 * na:4 * na], out[-1]


def _exchange_start(gs, me, after, name):
    own = [lax.dynamic_index_in_dim(g, me, 0, keepdims=False) for g in gs]
    return _copies_start(gs, [_place_own(o, me) for o in own], True, after, name)


def _copies_wait(sems, srcs, lands, after, name):
    na = len(srcs)

    def body(*refs):
        land_refs = refs[na:2 * na]
        sem_refs = refs[2 * na:4 * na]
        x, y, c = lax.axis_index("x"), lax.axis_index("y"), lax.axis_index("c")
        for a in range(na):
            seven = land_refs[a].at[pl.ds(0, N_DEV - 1)]
            copy = pltpu.make_async_remote_copy(
                src_ref=seven, dst_ref=seven, send_sem=sem_refs[2 * a], recv_sem=sem_refs[2 * a + 1],
                device_id=(x, y, c), device_id_type=MESH)
            copy.wait_send()
            copy.wait_recv()

    hbm = lambda t: pltpu.HBM(t.shape, t.dtype)
    out = pl.pallas_call(
        body, name=name,
        out_shape=tuple([hbm(t) for t in srcs] + [hbm(t) for t in lands]),
        in_specs=[_HBM] * (2 * na) + [_SEM] * (2 * na) + [pl.BlockSpec(memory_space=pl.ANY)],
        out_specs=tuple([_HBM] * (2 * na)),
        input_output_aliases={i: i for i in range(2 * na)},
        compiler_params=pltpu.CompilerParams(has_side_effects=_EFFECT),
    )(*srcs, *lands, *sems, after)
    return out[na:]


def _adamw_math(w, g, m, v):
    m = B1 * m + (1.0 - B1) * g
    v = B2 * v + (1.0 - B2) * (g * g)
    m_hat = m / (1.0 - B1 ** STEP)
    v_hat = v / (1.0 - B2 ** STEP)
    return -LR * (m_hat / (jnp.sqrt(v_hat) + AEPS) + WD * w), m, v


def _sum_adamw(parts, w, m, v, name):
    _, r, n = parts.shape
    tr = 256 if r % 256 == 0 else r

    def body(p_ref, w_ref, m_ref, v_ref, g_ref, d_ref, nm_ref, nv_ref):
        g = p_ref[0].astype(F32)
        for dev in range(1, N_DEV):
            g = g + p_ref[dev].astype(F32)
        g_ref[...] = g
        d_ref[...], nm_ref[...], nv_ref[...] = _adamw_math(w_ref[...], g, m_ref[...], v_ref[...])

    rows = pl.BlockSpec((None, tr, n), lambda i: (0, i, 0))
    sd = jax.ShapeDtypeStruct((1, r, n), F32)
    return pl.pallas_call(
        body, name=name, grid=(r // tr,), out_shape=(sd, sd, sd, sd),
        in_specs=[pl.BlockSpec((N_DEV, tr, n), lambda i: (0, i, 0)), rows, rows, rows],
        out_specs=(rows, rows, rows, rows),
        compiler_params=_cparams(("parallel",)),
    )(parts, w, m, v)


def _ada_fwd(c_all, ada_w, ada_b_cols):
    nb, ncol = c_all.shape[0], ada_w.shape[1]

    def body(c_ref, w_ref, b_ref, mod_ref, cond_ref):
        cond = _silu(c_ref[...])
        cond_ref[...] = cond
        mod_ref[...] = _dot_hi(cond, w_ref[...]) + b_ref[...]

    return pl.pallas_call(
        body, name="ada_fwd",
        out_shape=(jax.ShapeDtypeStruct((nb, ncol), F32), jax.ShapeDtypeStruct((nb, D), F32)),
        compiler_params=_cparams(),
    )(c_all, ada_w, ada_b_cols)


def _ada_bwd(cond_all, dmod_all, dmod_cols, smalls):
    ncol, nsm = dmod_cols.shape[1], smalls.shape[1]

    def body(cond_ref, dm_ref, dmc_ref, sm_ref, gw_ref, gb_ref, gs_ref):
        gw_ref[...] = lax.dot_general(cond_ref[...], dmc_ref[...], (((0,), (0,)), ((), ())),
                                      preferred_element_type=F32, precision=HI)
        gb_ref[...] = jnp.sum(dm_ref[...], axis=0, keepdims=True)
        gs_ref[...] = jnp.sum(sm_ref[...], axis=0, keepdims=True)

    return pl.pallas_call(
        body, name="ada_bwd",
        out_shape=(jax.ShapeDtypeStruct((D, ncol), F32), jax.ShapeDtypeStruct((1, 6 * D), F32),
                   jax.ShapeDtypeStruct((1, nsm), F32)),
        compiler_params=_cparams(),
    )(cond_all, dmod_all, dmod_cols, smalls)


IN_CUTS = (0, QW, QW + 2 * KVW, QW + 2 * KVW + CONVW, QW + 2 * KVW + CONVW + 2 * DH,
           QW + 2 * KVW + CONVW + 2 * DH + DNW, QW + 2 * KVW + CONVW + 2 * DH + DNW + D, IN_W)
IN_WIDTHS = tuple(b - a for a, b in zip(IN_CUTS[:-1], IN_CUTS[1:]))
IN_SHARD = IN_W // N_DEV


def _inproj_fwd(x, mod, g1, w_blk):
    B, S, _ = x.shape
    tm = _tile(S)

    def body(x_ref, mod_ref, g_ref, w_ref, h_ref, *o_refs):
        h = _rms_mod(x_ref[...], g_ref[...], mod_ref[1:2, :], mod_ref[0:1, :]).astype(BF16)
        h_ref[...] = h
        full = jnp.concatenate([jnp.dot(h, w_ref[j], preferred_element_type=F32) for j in range(N_DEV)], axis=1)
        for o_ref, lo, hi in zip(o_refs, IN_CUTS[:-1], IN_CUTS[1:]):
            o_ref[...] = full[:, lo:hi]

    return pl.pallas_call(
        body, name="inproj_fwd", grid=(B, S // tm),
        out_shape=[jax.ShapeDtypeStruct((B, S, D), BF16)] + [jax.ShapeDtypeStruct((B, S, w), F32) for w in IN_WIDTHS],
        in_specs=[_rows(tm, D), _perb(6, D), _full((1, D)), _resident(w_blk.shape)],
        out_specs=[_rows(tm, D)] + [_rows(tm, w) for w in IN_WIDTHS],
        compiler_params=_cparams(("parallel", "arbitrary")),
    )(x, mod, g1, w_blk)


def _inproj_bwd(x, mod, g1, dx1, dps, w_blk):
    B, S, _ = x.shape
    tm = _tile(S)
    n = len(dps)

    def body(x_ref, mod_ref, g_ref, dx1_ref, *refs):
        dp_refs, w_ref = refs[:n], refs[n]
        dblk_ref, gx_ref, dg_ref, dsc_ref, dsh_ref = refs[n + 1:]
        b, i = pl.program_id(0), pl.program_id(1)
        full = jnp.concatenate([r[...].astype(F32) for r in dp_refs], axis=1)
        dh = None
        for j in range(N_DEV):
            blk = full[:, IN_SHARD * j:IN_SHARD * (j + 1)].astype(BF16)
            dblk_ref[j] = blk
            t = _dot_nt(blk, w_ref[j])
            dh = t if dh is None else dh + t
        _, vjp = jax.vjp(_rms_mod, x_ref[...], g_ref[...], mod_ref[1:2, :], mod_ref[0:1, :])
        dx, dg, dsc, dsh = vjp(dh)
        gx_ref[...] = dx1_ref[...] + dx

        @pl.when((b == 0) & (i == 0))
        def _():
            dg_ref[...] = jnp.zeros_like(dg_ref)

        @pl.when(i == 0)
        def _():
            dsc_ref[...] = jnp.zeros_like(dsc_ref)
            dsh_ref[...] = jnp.zeros_like(dsh_ref)

        dg_ref[...] += dg
        dsc_ref[...] += dsc
        dsh_ref[...] += dsh

    return pl.pallas_call(
        body, name="inproj_bwd", grid=(B, S // tm),
        out_shape=[jax.ShapeDtypeStruct((B, N_DEV, S, IN_SHARD), BF16), jax.ShapeDtypeStruct((B, S, D), F32),
                   jax.ShapeDtypeStruct((1, D), F32), jax.ShapeDtypeStruct((B, 1, D), F32),
                   jax.ShapeDtypeStruct((B, 1, D), F32)],
        in_specs=[_rows(tm, D), _perb(6, D), _full((1, D)), _rows(tm, D)]
                 + [_rows(tm, w) for w in IN_WIDTHS] + [_resident(w_blk.shape)],
        out_specs=[pl.BlockSpec((None, N_DEV, tm, IN_SHARD), lambda b, i: (b, 0, i, 0)), _rows(tm, D),
                   _full((1, D)), _perb(1, D), _perb(1, D)],
        compiler_params=_cparams(("arbitrary", "arbitrary")),
    )(x, mod, g1, dx1, *dps, w_blk)


def _wgrad(a, b, name, after=None):
    B, na, S, K = a.shape
    nb, N = b.shape[1], b.shape[3]
    G = max(na, nb)
    tm = min(1024, S)
    nt = S // tm
    last = B * nt - 1

    def body(a_ref, b_ref, *rest):
        o_ref, acc = rest[-2:]
        t = pl.program_id(1)

        @pl.when(t == 0)
        def _():
            acc[...] = jnp.zeros_like(acc)

        acc[...] += lax.dot_general(a_ref[...], b_ref[...], (((0,), (0,)), ((), ())), preferred_element_type=F32)

        @pl.when(t == last)
        def _():
            o_ref[...] = acc[...].astype(BF16)

    return pl.pallas_call(
        body, name=name, grid=(G, B * nt),
        out_shape=jax.ShapeDtypeStruct((G, K, N), BF16),
        in_specs=[pl.BlockSpec((None, None, tm, K), lambda g, t: (t // nt, g if na > 1 else 0, t % nt, 0)),
                  pl.BlockSpec((None, None, tm, N), lambda g, t: (t // nt, g if nb > 1 else 0, t % nt, 0))]
                 + ([] if after is None else [pl.BlockSpec(memory_space=pl.ANY)]),
        out_specs=pl.BlockSpec((None, K, N), lambda g, t: (g, 0, 0)),
        scratch_shapes=[pltpu.VMEM((K, N), F32)],
        compiler_params=_cparams(("parallel", "arbitrary")),
    )(*((a, b) if after is None else (a, b, after)))


LANES = 128


def _attn_consts():
    inv_freq = THETA ** (-jnp.arange(0, ROT, 2, dtype=F32) / ROT)
    head = jnp.concatenate([inv_freq, inv_freq, jnp.zeros((HD - ROT,), F32)])
    invf = jnp.tile(head, LANES // HD)[None, :]
    mean_of = lambda w: jnp.asarray(np.kron(np.eye(w // HD), np.full((HD, HD), 1.0 / HD)), BF16)
    return invf, mean_of(QW), mean_of(KVW)


def _rope_tables(pos, invf):
    B, S, _ = pos.shape
    tr = min(1024, S)

    def body(p_ref, f_ref, c_ref, s_ref):
        ang = p_ref[...].astype(F32) * f_ref[...]
        c_ref[...] = jnp.cos(ang)
        s_ref[...] = jnp.sin(ang)

    sd = jax.ShapeDtypeStruct((B, S, LANES), F32)
    return pl.pallas_call(
        body, name="rope_tables", grid=(B, S // tr), out_shape=[sd, sd],
        in_specs=[_rows(tr, 1), _full((1, LANES))], out_specs=[_rows(tr, LANES), _rows(tr, LANES)],
        compiler_params=_cparams(("parallel", "parallel")),
    )(pos, invf)


def _rope_expand(cos, sin, reps):
    lane = lax.broadcasted_iota(jnp.int32, cos.shape, 1) % HD
    sa = jnp.where((lane >= ROT // 2) & (lane < ROT), sin, 0.0)
    sb = jnp.where(lane < ROT // 2, -sin, 0.0)
    rep = lambda t: jnp.concatenate([t] * reps, axis=1) if reps > 1 else t
    return rep(cos), rep(sa), rep(sb)


@jax.custom_vjp
def _rope(t, cos, sa, sb):
    w = t.shape[1]
    return t * cos + pltpu.roll(t, ROT // 2, 1) * sa + pltpu.roll(t, w - ROT // 2, 1) * sb


def _rope_fwd(t, cos, sa, sb):
    return _rope(t, cos, sa, sb), (cos, sa, sb)


def _rope_bwd(res, d):
    cos, sa, sb = res
    w = d.shape[1]
    dt = d * cos + pltpu.roll(d * sa, w - ROT // 2, 1) + pltpu.roll(d * sb, ROT // 2, 1)
    return dt, jnp.zeros_like(cos), jnp.zeros_like(sa), jnp.zeros_like(sb)


_rope.defvjp(_rope_fwd, _rope_bwd)


def _head_norm(t, g, mean_of):
    hi, lo = _split(t * t)
    ms = jnp.dot(hi, mean_of, preferred_element_type=F32) + jnp.dot(lo, mean_of, preferred_element_type=F32)
    return t * lax.rsqrt(ms + EPS) * g


def _attn_block(q, kvp, kvc, qg, kg, sinks, tq, tk, mq, mk, valid):
    qn = _rope(_head_norm(q, jnp.concatenate([qg] * HQ, axis=1), mq), *tq) * (HD ** -0.5)
    kv = jnp.concatenate([kvp, kvc], axis=0)
    kn = _rope(_head_norm(kv[:, 0:KVW], jnp.concatenate([kg] * HKV, axis=1), mk), *tk)
    per_tile = LANES // HD
    vT = jnp.transpose(kv[:, KVW:2 * KVW])
    qT = [jnp.transpose(qn[:, LANES * t:LANES * (t + 1)]) for t in range(QW // LANES)]
    head_T = lambda h: qT[h // per_tile][HD * (h % per_tile):HD * (h % per_tile + 1), :]
    none = jnp.zeros((HD, GRP * BLK), F32)
    o_T = []
    for j in range(HKV):
        q4T = jnp.concatenate([head_T(GRP * j + i) for i in range(GRP)], axis=1)
        sT = _dot(kn, jnp.concatenate([q4T, none] if j == 0 else [none, q4T], axis=0))
        sT = jnp.where(valid, sT, -1e30)
        sink = jnp.concatenate([jnp.broadcast_to(sinks[:, GRP * j + i:GRP * j + i + 1], (1, BLK)) for i in range(GRP)], axis=1)
        m = lax.stop_gradient(jnp.maximum(jnp.max(sT, axis=0, keepdims=True), sink))
        pT = jnp.exp(sT - m)
        den = jnp.sum(pT, axis=0, keepdims=True) + jnp.exp(sink - m)
        oT = _dot(vT[HD * j:HD * (j + 1), :], pT) * (1.0 / den)
        o_T += [oT[:, BLK * i:BLK * (i + 1)] for i in range(GRP)]
    return jnp.concatenate([jnp.transpose(jnp.concatenate(o_T[per_tile * t:per_tile * (t + 1)], axis=0))
                            for t in range(QW // LANES)], axis=1)


def _attn_tables(cp_ref, cc_ref, sp_ref, sc_ref, n):
    tq = _rope_expand(cc_ref[...], sc_ref[...], QW // LANES)
    tk = _rope_expand(jnp.concatenate([cp_ref[...], cc_ref[...]], axis=0),
                      jnp.concatenate([sp_ref[...], sc_ref[...]], axis=0), KVW // LANES)
    qi = lax.broadcasted_iota(jnp.int32, (2 * BLK, GRP * BLK), 1) % BLK + BLK
    kj = lax.broadcasted_iota(jnp.int32, (2 * BLK, GRP * BLK), 0)
    dist = qi - kj
    valid = (dist >= 0) & (dist < BLK) & ((kj >= BLK) | (n > 0))
    return tq, tk, valid


def _attn_fwd(aq, akv, cos, sin, qg, kg, sinks, mq, mk):
    B, S, _ = aq.shape
    nb = S // BLK

    def body(q_ref, kvp_ref, kvc_ref, cp_ref, cc_ref, sp_ref, sc_ref, qg_ref, kg_ref, sk_ref, mq_ref, mk_ref, o_ref):
        tq, tk, valid = _attn_tables(cp_ref, cc_ref, sp_ref, sc_ref, pl.program_id(1))
        o_ref[...] = _attn_block(q_ref[...], kvp_ref[...], kvc_ref[...], qg_ref[...], kg_ref[...], sk_ref[...],
                                 tq, tk, mq_ref[...], mk_ref[...], valid)

    prev = lambda b, n: (b, jnp.maximum(n - 1, 0), 0)
    cur = lambda b, n: (b, n, 0)
    return pl.pallas_call(
        body, name="attn_fwd", grid=(B, nb),
        out_shape=jax.ShapeDtypeStruct((B, S, QW), F32),
        in_specs=[pl.BlockSpec((None, BLK, QW), cur), pl.BlockSpec((None, BLK, 2 * KVW), prev),
                  pl.BlockSpec((None, BLK, 2 * KVW), cur), pl.BlockSpec((None, BLK, LANES), prev),
                  pl.BlockSpec((None, BLK, LANES), cur), pl.BlockSpec((None, BLK, LANES), prev),
                  pl.BlockSpec((None, BLK, LANES), cur), _full((1, HD)), _full((1, HD)), _full((1, HQ)),
                  _full((QW, QW)), _full((KVW, KVW))],
        out_specs=pl.BlockSpec((None, BLK, QW), cur),
        compiler_params=_cparams(("parallel", "arbitrary")),
    )(aq, akv, akv, cos, cos, sin, sin, qg, kg, sinks, mq, mk)


def _attn_bwd(aq, akv, cos, sin, qg, kg, sinks, mq, mk, do):
    B, S, _ = aq.shape
    nb = S // BLK

    def body(q_ref, kvp_ref, kvc_ref, cp_ref, cc_ref, sp_ref, sc_ref, qg_ref, kg_ref, sk_ref, mq_ref, mk_ref, do_ref,
             dq_ref, dkv_ref, dqg_ref, dkg_ref, dsk_ref, carry):
        b, i = pl.program_id(0), pl.program_id(1)
        tq, tk, valid = _attn_tables(cp_ref, cc_ref, sp_ref, sc_ref, nb - 1 - i)
        fn = functools.partial(_attn_block, tq=tq, tk=tk, mq=mq_ref[...], mk=mk_ref[...], valid=valid)
        _, vjp = jax.vjp(fn, q_ref[...], kvp_ref[...], kvc_ref[...], qg_ref[...], kg_ref[...], sk_ref[...])
        dq, dkvp, dkvc, dqg, dkg, dsk = vjp(do_ref[...])

        @pl.when(i == 0)
        def _():
            carry[...] = jnp.zeros_like(carry)

        @pl.when((b == 0) & (i == 0))
        def _():
            dqg_ref[...] = jnp.zeros_like(dqg_ref)
            dkg_ref[...] = jnp.zeros_like(dkg_ref)
            dsk_ref[...] = jnp.zeros_like(dsk_ref)

        dq_ref[...] = dq.astype(BF16)
        dkv_ref[...] = (dkvc + carry[...]).astype(BF16)
        carry[...] = dkvp
        dqg_ref[...] += dqg
        dkg_ref[...] += dkg
        dsk_ref[...] += dsk

    prev = lambda b, i: (b, jnp.maximum(nb - 2 - i, 0), 0)
    cur = lambda b, i: (b, nb - 1 - i, 0)
    return pl.pallas_call(
        body, name="attn_bwd", grid=(B, nb),
        out_shape=[jax.ShapeDtypeStruct((B, S, QW), BF16), jax.ShapeDtypeStruct((B, S, 2 * KVW), BF16),
                   jax.ShapeDtypeStruct((1, HD), F32), jax.ShapeDtypeStruct((1, HD), F32),
                   jax.ShapeDtypeStruct((1, HQ), F32)],
        in_specs=[pl.BlockSpec((None, BLK, QW), cur), pl.BlockSpec((None, BLK, 2 * KVW), prev),
                  pl.BlockSpec((None, BLK, 2 * KVW), cur), pl.BlockSpec((None, BLK, LANES), prev),
                  pl.BlockSpec((None, BLK, LANES), cur), pl.BlockSpec((None, BLK, LANES), prev),
                  pl.BlockSpec((None, BLK, LANES), cur), _full((1, HD)), _full((1, HD)), _full((1, HQ)),
                  _full((QW, QW)), _full((KVW, KVW)), pl.BlockSpec((None, BLK, QW), cur)],
        out_specs=[pl.BlockSpec((None, BLK, QW), cur), pl.BlockSpec((None, BLK, 2 * KVW), cur),
                   _full((1, HD)), _full((1, HD)), _full((1, HQ))],
        scratch_shapes=[pltpu.VMEM((BLK, 2 * KVW), F32)],
        compiler_params=_cparams(("arbitrary", "arbitrary")),
    )(aq, akv, akv, cos, cos, sin, sin, qg, kg, sinks, mq, mk, do)


def _conv_taps(xe, w, rows):
    y = None
    for j in range(CONV):
        sh = pltpu.roll(xe, CONV - 1 - j, 0)[8:8 + rows, :] if j < CONV - 1 else xe[8:8 + rows, :]
        y = sh * w[j:j + 1, :] if y is None else y + sh * w[j:j + 1, :]
    return y


def _conv_fwd(xin, w):
    B, S, C = xin.shape
    tc = min(512, S)
    r8 = tc // 8

    def body(xp_ref, x_ref, w_ref, o_ref):
        i = pl.program_id(1)
        xp = jnp.where(i > 0, xp_ref[...], 0.0)
        xe = jnp.concatenate([xp, x_ref[...]], axis=0)
        o_ref[...] = _silu(_conv_taps(xe, w_ref[...], tc))

    return pl.pallas_call(
        body, name="conv_fwd", grid=(B, S // tc),
        out_shape=jax.ShapeDtypeStruct((B, S, C), F32),
        in_specs=[pl.BlockSpec((None, 8, C), lambda b, i: (b, jnp.maximum(i * r8 - 1, 0), 0)),
                  _rows(tc, C), _full((CONV, C))],
        out_specs=_rows(tc, C),
        compiler_params=_cparams(("parallel", "arbitrary")),
    )(xin, xin, w)


def _conv_bwd(xin, w, dy):
    B, S, C = xin.shape
    tc = min(512, S)
    r8 = tc // 8
    nt = S // tc

    def body(xp_ref, x_ref, xn_ref, dy_ref, dyn_ref, w_ref, dx_ref, dw_ref):
        b, i = pl.program_id(0), pl.program_id(1)
        w = w_ref[...]
        xp = jnp.where(i > 0, xp_ref[...], 0.0)
        xe = jnp.concatenate([xp, x_ref[...], xn_ref[...]], axis=0)
        pre = _conv_taps(xe, w, tc + 8)
        sg = _sigmoid(pre)
        dyn = jnp.where(i < nt - 1, dyn_ref[...], 0.0)
        dpre = jnp.concatenate([dy_ref[...], dyn], axis=0) * (sg * (1.0 + pre * (1.0 - sg)))
        dx = dpre[0:tc, :] * w[CONV - 1:CONV, :]
        for j in range(CONV - 1):
            dx = dx + pltpu.roll(dpre, tc + 8 - (CONV - 1 - j), 0)[0:tc, :] * w[j:j + 1, :]
        dx_ref[...] = dx.astype(BF16)
        dcur = dpre[0:tc, :]
        xe0 = xe[0:8 + tc, :]
        lane_row = lax.broadcasted_iota(jnp.int32, (CONV, C), 0)
        dw = jnp.zeros((CONV, C), F32)
        for j in range(CONV):
            sh = pltpu.roll(xe0, CONV - 1 - j, 0)[8:8 + tc, :] if j < CONV - 1 else xe0[8:8 + tc, :]
            dw = dw + jnp.where(lane_row == j, jnp.sum(sh * dcur, axis=0, keepdims=True), 0.0)

        @pl.when((b == 0) & (i == 0))
        def _():
            dw_ref[...] = jnp.zeros_like(dw_ref)

        dw_ref[...] += dw

    return pl.pallas_call(
        body, name="conv_bwd", grid=(B, nt),
        out_shape=[jax.ShapeDtypeStruct((B, S, C), BF16), jax.ShapeDtypeStruct((CONV, C), F32)],
        in_specs=[pl.BlockSpec((None, 8, C), lambda b, i: (b, jnp.maximum(i * r8 - 1, 0), 0)),
                  _rows(tc, C),
                  pl.BlockSpec((None, 8, C), lambda b, i: (b, jnp.minimum((i + 1) * r8, S // 8 - 1), 0)),
                  _rows(tc, C),
                  pl.BlockSpec((None, 8, C), lambda b, i: (b, jnp.minimum((i + 1) * r8, S // 8 - 1), 0)),
                  _full((CONV, C))],
        out_specs=[_rows(tc, C), _full((CONV, C))],
        compiler_params=_cparams(("arbitrary", "arbitrary")),
    )(xin, xin, xin, dy, dy, w)


def _softplus(x):
    return jnp.maximum(x, 0.0) + jnp.log1p(jnp.exp(-jnp.abs(x)))


_BMM = (((2,), (1,)), ((0,), (0,)))
_BMM_NT = (((2,), (2,)), ((0,), (0,)))
_BMM_TN = (((1,), (1,)), ((0,), (0,)))


def _bmm(a, b, dims=_BMM):
    return lax.dot_general(a.astype(BF16), b.astype(BF16), dims, preferred_element_type=F32)


def _split(a):
    hi = a.astype(BF16)
    return hi, (a - hi.astype(F32)).astype(BF16)


def _bmm3(a, b, dims=_BMM):
    ah, al = _split(a)
    bh, bl = _split(b)
    d = lambda p, q: lax.dot_general(p, q, dims, preferred_element_type=F32)
    return d(ah, bh) + (d(ah, bl) + d(al, bh))


TRI_BASE = 8


def _tri_inverse(L):
    ii = lax.broadcasted_iota(jnp.int32, (CH, CH), 0)
    jj = lax.broadcasted_iota(jnp.int32, (CH, CH), 1)
    same = lambda size: (ii // size) == (jj // size)
    diag = jnp.where(same(TRI_BASE), L, 0.0)
    X = (ii == jj).astype(F32) - diag
    P = diag
    n = 2
    while n < TRI_BASE:
        P = _bmm3(P, P)
        X = X + _bmm3(X, P)
        n *= 2
    size = TRI_BASE
    while size < CH:
        joint = jnp.where(same(2 * size) & jnp.logical_not(same(size)), L, 0.0)
        X = X - _bmm3(X, _bmm3(joint, X))
        size *= 2
    return X


@jax.custom_vjp
def _tri_inverse_known(L, T):
    return T


def _tri_inverse_known_fwd(L, T):
    return T, T


def _tri_inverse_known_bwd(T, dT):
    return -_bmm3(T, _bmm3(dT, T, _BMM_NT), _BMM_TN), jnp.zeros_like(T)


_tri_inverse_known.defvjp(_tri_inverse_known_fwd, _tri_inverse_known_bwd)


def _cumsum_rows(g):
    n = g.shape[0]
    ii = lax.broadcasted_iota(jnp.int32, (n, CH, CH), 1)
    jj = lax.broadcasted_iota(jnp.int32, (n, CH, CH), 2)
    tri = (ii >= jj).astype(BF16)
    g0 = g.astype(BF16)
    r1 = g - g0.astype(F32)
    g1 = r1.astype(BF16)
    g2 = (r1 - g1.astype(F32)).astype(BF16)
    d = lambda q: lax.dot_general(tri, q, _BMM, preferred_element_type=F32)
    return d(g0) + (d(g1) + d(g2))


def _row_sums(t):
    n, r, w = t.shape
    hi, lo = _split(t.reshape(n * r, w))
    ones = jnp.ones((w, w), BF16)
    s = jnp.dot(hi, ones, preferred_element_type=F32) + jnp.dot(lo, ones, preferred_element_type=F32)
    return s.reshape(n, r, w)


def _dn_prep(t_known, qr, kr, v, a_raw, b_raw, a_log, dt_b):
    n = qr.shape[0]
    ii = lax.broadcasted_iota(jnp.int32, (n, CH, CH), 1)
    jj = lax.broadcasted_iota(jnp.int32, (n, CH, CH), 2)
    incl, strict = ii >= jj, ii > jj
    q = qr * lax.rsqrt(_row_sums(qr * qr) + EPS) * (DK ** -0.5)
    k = kr * lax.rsqrt(_row_sums(kr * kr) + EPS)
    beta = _sigmoid(b_raw)
    g = -jnp.exp(a_log) * _softplus(a_raw + dt_b)
    gcb = _cumsum_rows(jnp.broadcast_to(g, (n, CH, DK)))
    gc = gcb[:, :, 0:1]
    gc_row = jnp.swapaxes(gcb, 1, 2)[:, 0:1, 0:CH]
    decay = jnp.where(incl, jnp.exp(jnp.where(incl, gc - gc_row, 0.0)), 0.0)
    kb = k * beta
    L = jnp.where(strict, _bmm(kb, k, _BMM_NT) * decay, 0.0)
    T = _tri_inverse(L) if t_known is None else _tri_inverse_known(L, t_known)
    eg = jnp.exp(gc)
    u = _bmm(T, v * beta)
    w = _bmm(T, kb * eg)
    a_in = _bmm(q, k, _BMM_NT) * decay
    g_last = gc[:, CH - 1:CH, :]
    return u, w, q * eg, k * jnp.exp(g_last - gc), a_in, jnp.exp(g_last), T


def _dn_step(S0, u, w, qd, kd, a_in, cd):
    r = _bmm(jnp.concatenate([w, qd], axis=1), S0)
    v_new = u - r[:, 0:CH, :]
    o = r[:, CH:2 * CH, :] + _bmm(a_in, v_new)
    S1 = S0 * cd + _bmm(kd, v_new, _BMM_TN)
    return o, S1


def _dn_stack(cq, ba, al, dt, G):
    cols = [[] for _ in range(7)]
    for c in range(G):
        rows = slice(CH * c, CH * (c + 1))
        for h in range(DH):
            parts = (cq[rows, DK * h:DK * (h + 1)], cq[rows, DNW + DK * h:DNW + DK * (h + 1)],
                     cq[rows, 2 * DNW + DK * h:2 * DNW + DK * (h + 1)], ba[rows, DH + h:DH + h + 1],
                     ba[rows, h:h + 1], al[:, h:h + 1], dt[:, h:h + 1])
            for col, p in zip(cols, parts):
                col.append(p)
    return tuple(jnp.stack(col) for col in cols)


def _dn_group(S, want):
    g = want
    while (S // CH) % g:
        g //= 2
    return g


def _dn_prep_fwd(cq, ba, a_log, dt_b):
    B, S, _ = cq.shape
    nc = S // CH
    G = _dn_group(S, 4)

    def body(cq_ref, ba_ref, al_ref, dt_ref, u_ref, w_ref, qd_ref, kd_ref, a_ref, t_ref, cd_ref):
        ops = _dn_stack(cq_ref[...], ba_ref[...], al_ref[...], dt_ref[...], G)
        u, w, qd, kd, a_in, cd, T = _dn_prep(None, *ops)
        lane4 = lax.broadcasted_iota(jnp.int32, (1, DH), 1)
        for c in range(G):
            rows = slice(CH * c, CH * (c + 1))
            cdrow = jnp.zeros((1, DH), F32)
            for h in range(DH):
                n = DH * c + h
                lanes = slice(DK * h, DK * (h + 1))
                u_ref[rows, lanes] = u[n]
                w_ref[rows, lanes] = w[n]
                qd_ref[rows, lanes] = qd[n]
                kd_ref[rows, lanes] = kd[n]
                a_ref[rows, CH * h:CH * (h + 1)] = a_in[n]
                t_ref[rows, CH * h:CH * (h + 1)] = T[n]
                cdrow = cdrow + jnp.where(lane4 == h, cd[n], 0.0)
            cd_ref[c] = cdrow

    wide = jax.ShapeDtypeStruct((B, S, DNW), F32)
    sq = jax.ShapeDtypeStruct((B, S, DH * CH), F32)
    return pl.pallas_call(
        body, name="dn_prep_fwd", grid=(B, nc // G),
        out_shape=[wide, wide, wide, wide, sq, sq, jax.ShapeDtypeStruct((B, nc, 1, DH), F32)],
        in_specs=[_rows(G * CH, CONVW), _rows(G * CH, 2 * DH), _full((1, DH)), _full((1, DH))],
        out_specs=[_rows(G * CH, DNW)] * 4 + [_rows(G * CH, DH * CH)] * 2
                  + [pl.BlockSpec((None, G, 1, DH), lambda b, i: (b, i, 0, 0))],
        compiler_params=_cparams(("parallel", "parallel")),
    )(cq, ba, a_log, dt_b)


def _dn_seq_specs(B, steps, gs, rev):
    at = (lambda i: steps - 1 - i) if rev else (lambda i: i)
    wide = pl.BlockSpec((B, gs * CH, DNW), lambda i: (0, at(i), 0))
    a_spec = pl.BlockSpec((B, gs * CH, DH * CH), lambda i: (0, at(i), 0))
    cd_spec = pl.BlockSpec((B, gs, 1, DH), lambda i: (0, at(i), 0, 0))
    st_spec = pl.BlockSpec((B, gs, DH, DK, DK), lambda i: (0, at(i), 0, 0, 0))
    return wide, a_spec, cd_spec, st_spec


def _dn_step_operands(B, c, u_ref, w_ref, qd_ref, kd_ref, a_ref, cd_ref):
    pairs = [(b, h) for b in range(B) for h in range(DH)]
    rows = slice(CH * c, CH * (c + 1))
    wide = lambda ref: jnp.stack([ref[b, rows, DK * h:DK * (h + 1)] for b, h in pairs])
    a_in = jnp.stack([a_ref[b, rows, CH * h:CH * (h + 1)] for b, h in pairs])
    cd = jnp.stack([cd_ref[b, c, :, h:h + 1] for b, h in pairs])
    return wide(u_ref), wide(w_ref), wide(qd_ref), wide(kd_ref), a_in, cd


def _dn_seq_fwd(u, w, qd, kd, a_in, cd):
    B, S, _ = u.shape
    nc = S // CH
    gs = _dn_group(S, 4)

    def body(u_ref, w_ref, qd_ref, kd_ref, a_ref, cd_ref, o_ref, st_ref, state):
        @pl.when(pl.program_id(0) == 0)
        def _():
            state[...] = jnp.zeros_like(state)

        S0 = state[...]
        for c in range(gs):
            for b in range(B):
                st_ref[b, c] = S0[DH * b:DH * (b + 1)]
            o, S0 = _dn_step(S0, *_dn_step_operands(B, c, u_ref, w_ref, qd_ref, kd_ref, a_ref, cd_ref))
            for b in range(B):
                for h in range(DH):
                    o_ref[b, CH * c:CH * (c + 1), DK * h:DK * (h + 1)] = o[DH * b + h]
        state[...] = S0

    wide, a_spec, cd_spec, st_spec = _dn_seq_specs(B, nc // gs, gs, False)
    return pl.pallas_call(
        body, name="dn_seq_fwd", grid=(nc // gs,),
        out_shape=[jax.ShapeDtypeStruct((B, S, DNW), F32), jax.ShapeDtypeStruct((B, nc, DH, DK, DK), F32)],
        in_specs=[wide, wide, wide, wide, a_spec, cd_spec],
        out_specs=[wide, st_spec],
        scratch_shapes=[pltpu.VMEM((B * DH, DK, DK), F32)],
        compiler_params=_cparams(("arbitrary",)),
    )(u, w, qd, kd, a_in, cd)


def _dn_seq_bwd(u, w, qd, kd, a_in, cd, states, do):
    B, S, _ = u.shape
    nc = S // CH
    gs = _dn_group(S, 4)

    def body(u_ref, w_ref, qd_ref, kd_ref, a_ref, cd_ref, st_ref, do_ref,
             du_ref, dw_ref, dqd_ref, dkd_ref, da_ref, dcd_ref, dstate):
        @pl.when(pl.program_id(0) == 0)
        def _():
            dstate[...] = jnp.zeros_like(dstate)

        lane4 = lax.broadcasted_iota(jnp.int32, (1, DH), 1)
        dS = dstate[...]
        for c in reversed(range(gs)):
            rows = slice(CH * c, CH * (c + 1))
            S0 = jnp.concatenate([st_ref[b, c] for b in range(B)], axis=0)
            do = jnp.stack([do_ref[b, rows, DK * h:DK * (h + 1)] for b in range(B) for h in range(DH)])
            _, vjp = jax.vjp(_dn_step, S0, *_dn_step_operands(B, c, u_ref, w_ref, qd_ref, kd_ref, a_ref, cd_ref))
            dS, du, dw, dqd, dkd, da, dcd = vjp((do, dS))
            for b in range(B):
                dcdrow = jnp.zeros((1, DH), F32)
                for h in range(DH):
                    n = DH * b + h
                    lanes = slice(DK * h, DK * (h + 1))
                    du_ref[b, rows, lanes] = du[n]
                    dw_ref[b, rows, lanes] = dw[n]
                    dqd_ref[b, rows, lanes] = dqd[n]
                    dkd_ref[b, rows, lanes] = dkd[n]
                    da_ref[b, rows, CH * h:CH * (h + 1)] = da[n]
                    dcdrow = dcdrow + jnp.where(lane4 == h, dcd[n], 0.0)
                dcd_ref[b, c] = dcdrow
        dstate[...] = dS

    wide, a_spec, cd_spec, st_spec = _dn_seq_specs(B, nc // gs, gs, True)
    sd = jax.ShapeDtypeStruct((B, S, DNW), F32)
    return pl.pallas_call(
        body, name="dn_seq_bwd", grid=(nc // gs,),
        out_shape=[sd, sd, sd, sd, jax.ShapeDtypeStruct((B, S, DH * CH), F32), jax.ShapeDtypeStruct((B, nc, 1, DH), F32)],
        in_specs=[wide, wide, wide, wide, a_spec, cd_spec, st_spec, wide],
        out_specs=[wide, wide, wide, wide, a_spec, cd_spec],
        scratch_shapes=[pltpu.VMEM((B * DH, DK, DK), F32)],
        compiler_params=_cparams(("arbitrary",)),
    )(u, w, qd, kd, a_in, cd, states, do)


def _dn_prep_bwd(cq, ba, a_log, dt_b, t_inv, du, dw, dqd, dkd, da, dcd):
    B, S, _ = cq.shape
    nc = S // CH
    G = _dn_group(S, 4)

    def body(cq_ref, ba_ref, al_ref, dt_ref, t_ref, du_ref, dw_ref, dqd_ref, dkd_ref, da_ref, dcd_ref,
             dcq_ref, dba_ref, dal_ref, ddt_ref):
        @pl.when((pl.program_id(0) == 0) & (pl.program_id(1) == 0))
        def _():
            dal_ref[...] = jnp.zeros_like(dal_ref)
            ddt_ref[...] = jnp.zeros_like(ddt_ref)

        pairs = [(c, h) for c in range(G) for h in range(DH)]
        rows = lambda c: slice(CH * c, CH * (c + 1))
        wide = lambda ref: jnp.stack([ref[rows(c), DK * h:DK * (h + 1)] for c, h in pairs])
        square = lambda ref: jnp.stack([ref[rows(c), CH * h:CH * (h + 1)] for c, h in pairs])
        ops = _dn_stack(cq_ref[...], ba_ref[...], al_ref[...], dt_ref[...], G)
        cots = (wide(du_ref), wide(dw_ref), wide(dqd_ref), wide(dkd_ref), square(da_ref),
                jnp.stack([dcd_ref[c][:, h:h + 1] for c, h in pairs]), jnp.zeros((len(pairs), CH, CH), F32))
        _, vjp = jax.vjp(functools.partial(_dn_prep, square(t_ref)), *ops)
        dq, dk, dv, dar, dbr, dl, dd = vjp(cots)
        lane8 = lax.broadcasted_iota(jnp.int32, (CH, 2 * DH), 1)
        lane4 = lax.broadcasted_iota(jnp.int32, (1, DH), 1)
        dal = jnp.zeros((1, DH), F32)
        ddt = jnp.zeros((1, DH), F32)
        for c in range(G):
            dba = jnp.zeros((CH, 2 * DH), F32)
            for h in range(DH):
                n = DH * c + h
                dcq_ref[rows(c), DK * h:DK * (h + 1)] = dq[n]
                dcq_ref[rows(c), DNW + DK * h:DNW + DK * (h + 1)] = dk[n]
                dcq_ref[rows(c), 2 * DNW + DK * h:2 * DNW + DK * (h + 1)] = dv[n]
                dba = dba + jnp.where(lane8 == h, dbr[n], 0.0) + jnp.where(lane8 == DH + h, dar[n], 0.0)
                dal = dal + jnp.where(lane4 == h, dl[n], 0.0)
                ddt = ddt + jnp.where(lane4 == h, dd[n], 0.0)
            dba_ref[rows(c), :] = dba.astype(BF16)
        dal_ref[...] += dal
        ddt_ref[...] += ddt

    return pl.pallas_call(
        body, name="dn_prep_bwd", grid=(B, nc // G),
        out_shape=[jax.ShapeDtypeStruct((B, S, CONVW), F32), jax.ShapeDtypeStruct((B, S, 2 * DH), BF16),
                   jax.ShapeDtypeStruct((1, DH), F32), jax.ShapeDtypeStruct((1, DH), F32)],
        in_specs=[_rows(G * CH, CONVW), _rows(G * CH, 2 * DH), _full((1, DH)), _full((1, DH)), _rows(G * CH, DH * CH)]
                 + [_rows(G * CH, DNW)] * 4 + [_rows(G * CH, DH * CH),
                                               pl.BlockSpec((None, G, 1, DH), lambda b, i: (b, i, 0, 0))],
        out_specs=[_rows(G * CH, CONVW), _rows(G * CH, 2 * DH), _full((1, DH)), _full((1, DH))],
        compiler_params=_cparams(("arbitrary", "arbitrary")),
    )(cq, ba, a_log, dt_b, t_inv, du, dw, dqd, dkd, da, dcd)


def _gated_norm(o, z, g):
    outs = []
    for h in range(DH):
        t = o[:, DK * h:DK * (h + 1)]
        r = lax.rsqrt(jnp.mean(t * t, axis=-1, keepdims=True) + EPS)
        outs.append(t * r * g * _silu(z[:, DK * h:DK * (h + 1)]))
    return jnp.concatenate(outs, axis=1)


def _mix_fwd(x, o_attn, o_dn, z, ga, gd, mod, dn_g, w_branch, w_out):
    B, S, _ = x.shape
    tm = _tile(S, 512)

    def body(x_ref, oa_ref, od_ref, z_ref, ga_ref, gd_ref, mod_ref, g_ref, wb_ref, wo_ref,
             x1_ref, mix_ref, mg_ref, ob_ref):
        oa = oa_ref[...].astype(BF16)
        od = _gated_norm(od_ref[...], z_ref[...], g_ref[...]).astype(BF16)
        ob_ref[0] = oa
        ob_ref[1] = od
        ya = jnp.dot(oa, wb_ref[0:QW, :], preferred_element_type=F32)
        yd = jnp.dot(od, wb_ref[QW:QW + DNW, :], preferred_element_type=F32)
        merged = (_sigmoid(ga_ref[...]) * ya + _sigmoid(gd_ref[...]) * yd).astype(BF16)
        mg_ref[...] = merged
        mix = jnp.dot(merged, wo_ref[...], preferred_element_type=F32)
        mix_ref[...] = mix
        x1_ref[...] = x_ref[...] + mod_ref[2:3, :] * mix

    return pl.pallas_call(
        body, name="mix_fwd", grid=(B, S // tm),
        out_shape=[jax.ShapeDtypeStruct((B, S, D), F32), jax.ShapeDtypeStruct((B, S, D), F32),
                   jax.ShapeDtypeStruct((B, S, D), BF16), jax.ShapeDtypeStruct((B, 2, S, QW), BF16)],
        in_specs=[_rows(tm, D), _rows(tm, QW), _rows(tm, DNW), _rows(tm, DNW), _rows(tm, D), _rows(tm, D),
                  _perb(6, D), _full((1, DK)), _resident(w_branch.shape), _resident(w_out.shape)],
        out_specs=[_rows(tm, D), _rows(tm, D), _rows(tm, D), _stacked(2, tm, QW)],
        compiler_params=_cparams(("parallel", "arbitrary")),
    )(x, o_attn, o_dn, z, ga, gd, mod, dn_g, w_branch, w_out)


def _mix_bwd(dx1, mix, o_attn, o_dn, z, ga, gd, mod, dn_g, w_branch, w_out):
    B, S, _ = dx1.shape
    tm = _tile(S)

    def body(dx1_ref, mix_ref, oa_ref, od_ref, z_ref, ga_ref, gd_ref, mod_ref, g_ref, wb_ref, wo_ref,
             dmix_ref, dyo_ref, dga_ref, dgd_ref, dz_ref, doa_ref, dod_ref, dgate_ref, dg_ref):
        b, i = pl.program_id(0), pl.program_id(1)
        dx1 = dx1_ref[...]
        dmix = (dx1 * mod_ref[2:3, :]).astype(BF16)
        dmix_ref[...] = dmix
        dgate = jnp.sum(dx1 * mix_ref[...], axis=0, keepdims=True)
        dmerged = _dot_nt(dmix, wo_ref[...])
        odn, gn_vjp = jax.vjp(_gated_norm, od_ref[...], z_ref[...], g_ref[...])
        ya = _dot(oa_ref[...], wb_ref[0:QW, :])
        yd = _dot(odn, wb_ref[QW:QW + DNW, :])
        sa, sd = _sigmoid(ga_ref[...]), _sigmoid(gd_ref[...])
        dya = (dmerged * sa).astype(BF16)
        dyd = (dmerged * sd).astype(BF16)
        dyo_ref[0] = dya
        dyo_ref[1] = dyd
        dga_ref[...] = (dmerged * ya * sa * (1.0 - sa)).astype(BF16)
        dgd_ref[...] = (dmerged * yd * sd * (1.0 - sd)).astype(BF16)
        doa_ref[...] = _dot_nt(dya, wb_ref[0:QW, :])
        dodn = _dot_nt(dyd, wb_ref[QW:QW + DNW, :])
        dod, dz, dg = gn_vjp(dodn)
        dod_ref[...] = dod
        dz_ref[...] = dz.astype(BF16)

        @pl.when(i == 0)
        def _():
            dgate_ref[...] = jnp.zeros_like(dgate_ref)

        @pl.when((b == 0) & (i == 0))
        def _():
            dg_ref[...] = jnp.zeros_like(dg_ref)

        dgate_ref[...] += dgate
        dg_ref[...] += dg

    return pl.pallas_call(
        body, name="mix_bwd", grid=(B, S // tm),
        out_shape=[jax.ShapeDtypeStruct((B, S, D), BF16), jax.ShapeDtypeStruct((B, 2, S, D), BF16),
                   jax.ShapeDtypeStruct((B, S, D), BF16), jax.ShapeDtypeStruct((B, S, D), BF16),
                   jax.ShapeDtypeStruct((B, S, DNW), BF16),
                   jax.ShapeDtypeStruct((B, S, QW), F32), jax.ShapeDtypeStruct((B, S, DNW), F32),
                   jax.ShapeDtypeStruct((B, 1, D), F32), jax.ShapeDtypeStruct((1, DK), F32)],
        in_specs=[_rows(tm, D), _rows(tm, D), _rows(tm, QW), _rows(tm, DNW), _rows(tm, DNW), _rows(tm, D),
                  _rows(tm, D), _perb(6, D), _full((1, DK)), _resident(w_branch.shape), _resident(w_out.shape)],
        out_specs=[_rows(tm, D), _stacked(2, tm, D), _rows(tm, D), _rows(tm, D), _rows(tm, DNW),
                   _rows(tm, QW), _rows(tm, DNW), _perb(1, D), _full((1, DK))],
        compiler_params=_cparams(("arbitrary", "arbitrary")),
    )(dx1, mix, o_attn, o_dn, z, ga, gd, mod, dn_g, w_branch, w_out)


GU_SHARD = 2 * FFN // N_DEV
GU_HALF = N_DEV // 2


def _ffn1_fwd(x1, mod, g2, w_gu):
    B, S, _ = x1.shape
    tm = _tile(S)

    def body(x_ref, mod_ref, g_ref, w_ref, h_ref, gate_ref, up_ref, act_ref):
        h = _rms_mod(x_ref[...], g_ref[...], mod_ref[4:5, :], mod_ref[3:4, :]).astype(BF16)
        h_ref[...] = h
        for j in range(GU_HALF):
            gate = jnp.dot(h, w_ref[j], preferred_element_type=F32)
            up = jnp.dot(h, w_ref[GU_HALF + j], preferred_element_type=F32)
            gate_ref[j] = gate
            up_ref[j] = up
            act_ref[j] = (_silu(gate) * up).astype(BF16)

    blk = lambda dt: jax.ShapeDtypeStruct((B, GU_HALF, S, GU_SHARD), dt)
    return pl.pallas_call(
        body, name="ffn1_fwd", grid=(B, S // tm),
        out_shape=[jax.ShapeDtypeStruct((B, S, D), BF16), blk(F32), blk(F32), blk(BF16)],
        in_specs=[_rows(tm, D), _perb(6, D), _full((1, D)), _resident(w_gu.shape)],
        out_specs=[_rows(tm, D)] + [_stacked(GU_HALF, tm, GU_SHARD)] * 3,
        compiler_params=_cparams(("parallel", "arbitrary")),
    )(x1, mod, g2, w_gu)


def _ffn2_fwd(act, x1, target, mod, w_down):
    B, S, _ = x1.shape
    tm = _tile(S, 512)

    def body(a_ref, x_ref, t_ref, mod_ref, w_ref, dy_ref, loss_ref, dgate_ref):
        b, i = pl.program_id(0), pl.program_id(1)
        y = jnp.dot(a_ref[0], w_ref[0], preferred_element_type=F32)
        for j in range(1, GU_HALF):
            y = y + jnp.dot(a_ref[j], w_ref[j], preferred_element_type=F32)
        err = x_ref[...] + mod_ref[5:6, :] * y - t_ref[...]
        dy = err * (1.0 / D)
        dy_ref[...] = dy

        @pl.when((b == 0) & (i == 0))
        def _():
            loss_ref[...] = jnp.zeros_like(loss_ref)

        @pl.when(i == 0)
        def _():
            dgate_ref[...] = jnp.zeros_like(dgate_ref)

        loss_ref[...] += (0.5 / D) * jnp.sum(err * err)
        dgate_ref[...] += jnp.sum(dy * y, axis=0, keepdims=True)

    return pl.pallas_call(
        body, name="ffn2_fwd", grid=(B, S // tm),
        out_shape=[jax.ShapeDtypeStruct((B, S, D), F32), jax.ShapeDtypeStruct((1, 128), F32),
                   jax.ShapeDtypeStruct((B, 1, D), F32)],
        in_specs=[_stacked(GU_HALF, tm, GU_SHARD), _rows(tm, D), _rows(tm, D), _perb(6, D), _resident(w_down.shape)],
        out_specs=[_rows(tm, D), _full((1, 128)), _perb(1, D)],
        compiler_params=_cparams(("arbitrary", "arbitrary")),
    )(act, x1, target, mod, w_down)


def _ffn2_bwd(dy, gate, up, mod, w_down):
    B, S, _ = dy.shape
    tm = _tile(S)

    def body(dy_ref, gate_ref, up_ref, mod_ref, w_ref, dgu_ref, dyg_ref):
        dyg = (dy_ref[...] * mod_ref[5:6, :]).astype(BF16)
        dyg_ref[...] = dyg
        for j in range(GU_HALF):
            dact = _dot_nt(dyg, w_ref[j])
            gate, up = gate_ref[j], up_ref[j]
            sg = _sigmoid(gate)
            dgu_ref[j] = (dact * up * (sg * (1.0 + gate * (1.0 - sg)))).astype(BF16)
            dgu_ref[GU_HALF + j] = (dact * (gate * sg)).astype(BF16)

    return pl.pallas_call(
        body, name="ffn2_bwd", grid=(B, S // tm),
        out_shape=[jax.ShapeDtypeStruct((B, N_DEV, S, GU_SHARD), BF16), jax.ShapeDtypeStruct((B, S, D), BF16)],
        in_specs=[_rows(tm, D), _stacked(GU_HALF, tm, GU_SHARD), _stacked(GU_HALF, tm, GU_SHARD), _perb(6, D),
                  _resident(w_down.shape)],
        out_specs=[_stacked(N_DEV, tm, GU_SHARD), _rows(tm, D)],
        compiler_params=_cparams(("parallel", "arbitrary")),
    )(dy, gate, up, mod, w_down)


def _ffn1_bwd(dgu, x1, dy, mod, g2, w_gu):
    B, S, _ = x1.shape
    tm = _tile(S, 512)

    def body(dgu_ref, x_ref, dy_ref, mod_ref, g_ref, w_ref, dx1_ref, dg_ref, dsc_ref, dsh_ref):
        b, i = pl.program_id(0), pl.program_id(1)
        dh = _dot_nt(dgu_ref[0], w_ref[0])
        for j in range(1, N_DEV):
            dh = dh + _dot_nt(dgu_ref[j], w_ref[j])
        _, vjp = jax.vjp(_rms_mod, x_ref[...], g_ref[...], mod_ref[4:5, :], mod_ref[3:4, :])
        dx, dg, dsc, dsh = vjp(dh)
        dx1_ref[...] = dy_ref[...] + dx

        @pl.when((b == 0) & (i == 0))
        def _():
            dg_ref[...] = jnp.zeros_like(dg_ref)

        @pl.when(i == 0)
        def _():
            dsc_ref[...] = jnp.zeros_like(dsc_ref)
            dsh_ref[...] = jnp.zeros_like(dsh_ref)

        dg_ref[...] += dg
        dsc_ref[...] += dsc
        dsh_ref[...] += dsh

    return pl.pallas_call(
        body, name="ffn1_bwd", grid=(B, S // tm),
        out_shape=[jax.ShapeDtypeStruct((B, S, D), F32), jax.ShapeDtypeStruct((1, D), F32),
                   jax.ShapeDtypeStruct((B, 1, D), F32), jax.ShapeDtypeStruct((B, 1, D), F32)],
        in_specs=[_stacked(N_DEV, tm, GU_SHARD), _rows(tm, D), _rows(tm, D), _perb(6, D), _full((1, D)),
                  _resident(w_gu.shape)],
        out_specs=[_rows(tm, D), _full((1, D)), _perb(1, D), _perb(1, D)],
        compiler_params=_cparams(("arbitrary", "arbitrary")),
    )(dgu, x1, dy, mod, g2, w_gu)


def _adamw(w, g, m, v, name):
    def body(w_ref, g_ref, m_ref, v_ref, d_ref, nm_ref, nv_ref):
        d_ref[...], nm_ref[...], nv_ref[...] = _adamw_math(w_ref[...], g_ref[...], m_ref[...], v_ref[...])

    sd = jax.ShapeDtypeStruct(w.shape, F32)
    return pl.pallas_call(body, name=name, out_shape=(sd, sd, sd), compiler_params=_cparams())(w, g, m, v)


def kernel(x, c, positions, ada_w, ada_b, norm1_g, w_in, conv_w, q_norm_g, k_norm_g, sinks, a_log, dt_bias, dn_norm_g, w_branch, w_out, norm2_g, w_gate_up, w_down, loss_target, m_ada_w, m_ada_b, m_norm1_g, m_w_in, m_conv_w, m_q_norm_g, m_k_norm_g, m_sinks, m_a_log, m_dt_bias, m_dn_norm_g, m_w_branch, m_w_out, m_norm2_g, m_w_gate_up, m_w_down, v_ada_w, v_ada_b, v_norm1_g, v_w_in, v_conv_w, v_q_norm_g, v_k_norm_g, v_sinks, v_a_log, v_dt_bias, v_dn_norm_g, v_w_branch, v_w_out, v_norm2_g, v_w_gate_up, v_w_down):
    B, S, _ = x.shape
    me = 4 * lax.axis_index("x") + 2 * lax.axis_index("y") + lax.axis_index("c")

    shards = [w[0].astype(BF16) for w in (w_in, w_branch, w_out, w_gate_up, w_down)]

    c_all = _all_gather_small(c, "gather_c").reshape(N_DEV * B, D)
    ncol = 6 * D // N_DEV
    mod_cols, cond_all = _ada_fwd(c_all, ada_w[0], lax.dynamic_slice(ada_b, (0, me * ncol), (1, ncol)))
    mod_all = _all_gather_small(mod_cols, "gather_mod").transpose(1, 0, 2).reshape(N_DEV * B, 6 * D)
    mod = lax.dynamic_slice(mod_all, (me * B, 0), (B, 6 * D)).reshape(B, 6, D)
    conv2 = conv_w.reshape(CONV, CONVW // N_DEV)
    conv_all = _all_gather_small(conv2, "gather_conv").transpose(1, 0, 2).reshape(CONV, CONVW)

    (w_in_b,) = _all_gather_big(shards[:1], "gather_w_in", after=(mod, conv_all))
    w_sems, w_srcs, w_lands, w_token = _copies_start(shards[1:], [_place_own(s, me) for s in shards[1:]], False,
                                                    w_in_b, "gather_rest_start")

    h1, aq, akv, dnx, ba, z, ga, gd = _inproj_fwd(x, mod, norm1_g + w_token[0, 0], w_in_b)
    invf, mean_q, mean_k = _attn_consts()
    rope_cos, rope_sin = _rope_tables(positions.reshape(B, S, 1), invf)
    o_attn = _attn_fwd(aq, akv, rope_cos, rope_sin, q_norm_g, k_norm_g, sinks, mean_q, mean_k)
    cq = _conv_fwd(dnx, conv_all)
    dn_u, dn_w, dn_qd, dn_kd, dn_a, dn_t, dn_cd = _dn_prep_fwd(cq, ba, a_log, dt_bias)
    o_dn, states = _dn_seq_fwd(dn_u, dn_w, dn_qd, dn_kd, dn_a, dn_cd)
    w_branch_g, w_out_g, w_gu_b, w_down_g = _copies_wait(w_sems, w_srcs, w_lands, o_dn, "gather_wait_rest")
    w_branch_f = w_branch_g.reshape(D, D)
    w_out_f = w_out_g.reshape(D, D)
    w_down_b = w_down_g.reshape(GU_HALF, GU_SHARD, D)
    x1, mix, merged, ob = _mix_fwd(x, o_attn, o_dn, z, ga, gd, mod, dn_norm_g, w_branch_f, w_out_f)
    h2, gate, up, act = _ffn1_fwd(x1, mod, norm2_g, w_gu_b)
    dy, loss_part, d_gate2 = _ffn2_fwd(act, x1, loss_target, mod, w_down_b)
    loss = lax.psum(loss_part[0, 0], ("x", "y", "c"))

    one = lambda t: t.reshape(B, 1, S, t.shape[-1])
    dgu, dyg = _ffn2_bwd(dy, gate, up, mod, w_down_b)
    g_w_down = _wgrad(act, one(dyg), "wgrad_down")
    dx1, d_n2g, d_scale2, d_shift2 = _ffn1_bwd(dgu, x1, dy, mod, norm2_g, w_gu_b)
    g_w_gu = _wgrad(one(h2), dgu, "wgrad_gate_up")
    ffn = _exchange_start([g_w_gu, g_w_down.reshape(N_DEV, FFN // N_DEV, D)], me, dx1, "exchange_ffn_start")
    dmix, dyo, dga, dgd, dz, d_oa, d_od, d_gate1, d_dng = _mix_bwd(
        dx1, mix, o_attn, o_dn, z, ga, gd, mod, dn_norm_g + ffn[3][0, 0], w_branch_f, w_out_f)
    d_dn = _dn_seq_bwd(dn_u, dn_w, dn_qd, dn_kd, dn_a, dn_cd, states, d_od)
    dcq, dba, d_alog, d_dtb = _dn_prep_bwd(cq, ba, a_log, dt_bias, dn_t, *d_dn)
    ddnx, d_conv = _conv_bwd(dnx, conv_all, dcq)
    daq, dakv, d_qg, d_kg, d_sinks = _attn_bwd(aq, akv, rope_cos, rope_sin, q_norm_g, k_norm_g, sinks, mean_q, mean_k, d_oa)
    dps = [daq, dakv, ddnx, dba, dz, dga, dgd]
    dblk, grad_x, d_n1g, d_scale1, d_shift1 = _inproj_bwd(x, mod, norm1_g, dx1, dps, w_in_b)

    dmod = jnp.concatenate([d_shift1, d_scale1, d_gate1, d_shift2, d_scale2, d_gate2], axis=2).reshape(B, 6 * D)
    small = jnp.concatenate([d_n1g, d_qg, d_kg, d_sinks, d_alog, d_dtb, d_dng, d_n2g, d_conv.reshape(1, CONV * CONVW)], axis=1)
    nsm = small.shape[1]
    width = -(-max(6 * D, nsm) // 128) * 128
    rows = jnp.concatenate([jnp.pad(dmod, ((0, 0), (0, width - 6 * D))), jnp.pad(small, ((0, 8 - B - 1), (0, width - nsm)))], axis=0)
    rows_all = _all_gather_small(rows, "gather_small")
    dmod_all = rows_all[:, 0:B, 0:6 * D].reshape(N_DEV * B, 6 * D)
    dmod_cols = lax.dynamic_slice(dmod_all, (0, me * ncol), (N_DEV * B, ncol))
    grad_ada_w, grad_ada_b, small_sum = _ada_bwd(cond_all, dmod_all, dmod_cols, rows_all[:, B, :])
    sizes = [D, HD, HD, HQ, DH, DH, DK, D]
    so = np.cumsum([0] + sizes)
    g_n1, g_qg, g_kg, g_sk, g_al, g_dt, g_dn, g_n2 = [small_sum[:, so[i]:so[i + 1]] for i in range(8)]
    g_conv_all = small_sum[:, so[8]:so[8] + CONV * CONVW].reshape(CONV, N_DEV, CONVW // N_DEV)
    grad_conv = lax.dynamic_slice(g_conv_all, (0, me, 0), (CONV, 1, CONVW // N_DEV)).reshape(CONV, CONVW // N_DEV)

    g_w_in = _wgrad(one(h1), dblk, "wgrad_in", after=small_sum)
    proj = _exchange_start([g_w_in], me, small_sum, "exchange_in_start")
    g_w_out = _wgrad(one(merged), one(dmix), "wgrad_out", after=proj[3])
    g_w_branch = _wgrad(ob, dyo, "wgrad_branch", after=proj[3])
    mixer = _exchange_start([g_w_branch.reshape(N_DEV, D // N_DEV, D), g_w_out.reshape(N_DEV, D // N_DEV, D)], me,
                            proj[3], "exchange_mix_start")

    upd, grads = {}, {}

    def finish(names, parts, weights):
        for nm, p, (w, m, v) in zip(names, parts, weights):
            grads[nm], *upd[nm] = _sum_adamw(p, w, m, v, "update_" + nm)

    finish(["w_gate_up", "w_down"], _copies_wait(*ffn[:3], mixer[3], "exchange_ffn_wait"),
           [(w_gate_up, m_w_gate_up, v_w_gate_up), (w_down, m_w_down, v_w_down)])
    finish(["w_in"], _copies_wait(*proj[:3], grads["w_gate_up"], "exchange_in_wait"), [(w_in, m_w_in, v_w_in)])
    finish(["w_branch", "w_out"], _copies_wait(*mixer[:3], grads["w_in"], "exchange_mix_wait"),
           [(w_branch, m_w_branch, v_w_branch), (w_out, m_w_out, v_w_out)])

    grads["ada_w"] = grad_ada_w.reshape(ada_w.shape)
    upd["ada_w"] = _adamw(ada_w, grads["ada_w"], m_ada_w, v_ada_w, "adamw_ada_w")
    small_names = ["ada_b", "norm1_g", "q_norm_g", "k_norm_g", "sinks", "a_log", "dt_bias", "dn_norm_g", "norm2_g", "conv_w"]
    small_w = [ada_b, norm1_g, q_norm_g, k_norm_g, sinks, a_log, dt_bias, dn_norm_g, norm2_g, conv_w]
    small_g = [grad_ada_b, g_n1, g_qg, g_kg, g_sk, g_al, g_dt, g_dn, g_n2, grad_conv]
    small_m = [m_ada_b, m_norm1_g, m_q_norm_g, m_k_norm_g, m_sinks, m_a_log, m_dt_bias, m_dn_norm_g, m_norm2_g, m_conv_w]
    small_v = [v_ada_b, v_norm1_g, v_q_norm_g, v_k_norm_g, v_sinks, v_a_log, v_dt_bias, v_dn_norm_g, v_norm2_g, v_conv_w]
    cat = lambda arrs: jnp.concatenate([a.reshape(1, -1) for a in arrs], axis=1)
    res = _adamw(cat(small_w), cat(small_g), cat(small_m), cat(small_v), "adamw_small")
    po = np.cumsum([0] + [int(np.prod(w.shape)) for w in small_w])
    for i, nm in enumerate(small_names):
        upd[nm] = tuple(r[:, po[i]:po[i + 1]].reshape(small_w[i].shape) for r in res)
        grads[nm] = small_g[i].reshape(small_w[i].shape)

    order = ["ada_w", "ada_b", "norm1_g", "w_in", "conv_w", "q_norm_g", "k_norm_g", "sinks", "a_log", "dt_bias",
             "dn_norm_g", "w_branch", "w_out", "norm2_g", "w_gate_up", "w_down"]
    return (loss, grad_x, *[grads[n] for n in order], *[upd[n][0] for n in order],
            *[upd[n][1] for n in order], *[upd[n][2] for n in order])
```

```python
import functools

import numpy as np
import jax
import jax.numpy as jnp
from jax import lax
from jax.experimental import pallas as pl
from jax.experimental.pallas import tpu as pltpu

F32 = jnp.float32
BF16 = jnp.bfloat16
HI = lax.Precision.HIGHEST

N_DEV = 8
D = 1024
HQ, HKV, HD = 8, 2, 64
GRP = HQ // HKV
BLK = 128
ROT = HD // 4
THETA = 500000.0
QW, KVW = HQ * HD, HKV * HD
DH, DK = 4, 128
CH = 64
DNW = DH * DK
CONV = 4
CONVW = 3 * DNW
FFN = 2816
EPS = 1e-6
IN_W = QW + 2 * KVW + CONVW + 2 * DH + DNW + 2 * D

LR, B1, B2, AEPS, WD, STEP = 0.001, 0.9, 0.999, 1e-08, 0.01, 10

VMEM_LIMIT = 56 * 1024 * 1024
MESH = pl.DeviceIdType.MESH


def _cparams(sem=None, vmem=VMEM_LIMIT):
    return pltpu.CompilerParams(dimension_semantics=sem, vmem_limit_bytes=vmem)


def _full(shape):
    n = len(shape)
    return pl.BlockSpec(shape, lambda *_: (0,) * n)


def _resident(shape):
    n = len(shape)
    return pl.BlockSpec(shape, lambda *_: (0,) * n, pipeline_mode=pl.Buffered(1))


def _rows(tm, w):
    return pl.BlockSpec((None, tm, w), lambda b, i: (b, i, 0))


def _stacked(n, tm, w):
    return pl.BlockSpec((None, n, tm, w), lambda b, i: (b, 0, i, 0))


def _perb(r, w):
    return pl.BlockSpec((None, r, w), lambda b, i: (b, 0, 0))


def _dot(a, b):
    return jnp.dot(a.astype(BF16), b.astype(BF16), preferred_element_type=F32)


def _dot_nt(a, b):
    return lax.dot_general(a.astype(BF16), b.astype(BF16), (((1,), (1,)), ((), ())), preferred_element_type=F32)


def _dot_tn(a, b):
    return lax.dot_general(a.astype(BF16), b.astype(BF16), (((0,), (0,)), ((), ())), preferred_element_type=F32)


def _dot_hi(a, b):
    return jnp.dot(a, b, preferred_element_type=F32, precision=HI)


def _sigmoid(x):
    return jax.nn.sigmoid(x)


def _silu(x):
    return x * jax.nn.sigmoid(x)


def _rms_mod(x, g, scale, shift):
    r = lax.rsqrt(jnp.mean(x * x, axis=-1, keepdims=True) + EPS)
    return (x * r * g) * (1.0 + scale) + shift


def _tile(S, rows=256):
    return min(rows, S)


def _peer(x, y, c, k):
    px = 1 - x if (k >> 2) & 1 else x
    py = 1 - y if (k >> 1) & 1 else y
    pc = 1 - c if k & 1 else c
    return px, py, pc


def _all_gather_small(v, name):
    r, n = v.shape

    def body(v_ref, out_ref, send_sems, recv_sems, local_sem):
        x, y, c = lax.axis_index("x"), lax.axis_index("y"), lax.axis_index("c")
        me = 4 * x + 2 * y + c
        mine = pltpu.make_async_copy(v_ref, out_ref.at[me], local_sem)
        mine.start()
        sends = []
        for k in range(1, N_DEV):
            cp = pltpu.make_async_remote_copy(
                src_ref=v_ref, dst_ref=out_ref.at[me], send_sem=send_sems.at[k - 1], recv_sem=recv_sems.at[k - 1],
                device_id=_peer(x, y, c, k), device_id_type=MESH)
            cp.start()
            sends.append(cp)
        for k in range(1, N_DEV):
            px, py, pc = _peer(x, y, c, k)
            pltpu.make_async_remote_copy(
                src_ref=v_ref, dst_ref=out_ref.at[4 * px + 2 * py + pc], send_sem=send_sems.at[k - 1],
                recv_sem=recv_sems.at[k - 1], device_id=(px, py, pc), device_id_type=MESH).wait_recv()
        for cp in sends:
            cp.wait_send()
        mine.wait()

    return pl.pallas_call(
        body, name=name,
        out_shape=jax.ShapeDtypeStruct((N_DEV, r, n), v.dtype),
        in_specs=[pl.BlockSpec(memory_space=pltpu.VMEM)],
        out_specs=pl.BlockSpec(memory_space=pltpu.VMEM),
        scratch_shapes=[pltpu.SemaphoreType.DMA((N_DEV - 1,)), pltpu.SemaphoreType.DMA((N_DEV - 1,)), pltpu.SemaphoreType.DMA],
    )(v)


def _all_gather_big(vs, name, after=()):
    na, nf = len(vs), len(after)

    def body(*refs):
        v_refs, out_refs = refs[:na], refs[na + nf:2 * na + nf]
        send_sems, recv_sems, local_sems = refs[2 * na + nf:]
        x, y, c = lax.axis_index("x"), lax.axis_index("y"), lax.axis_index("c")
        me, sibling = (x, y, c), (x, y, 1 - c)
        chips = [(1 - x, y), (x, 1 - y), (1 - x, 1 - y)]

        def rows(a, px, py, pc):
            return out_refs[a].at[4 * px + 2 * py + pc]

        def copy(a, k, block, to, src=None):
            return pltpu.make_async_remote_copy(
                src_ref=rows(a, *block) if src is None else src, dst_ref=rows(a, *block),
                send_sem=send_sems.at[7 * a + k], recv_sem=recv_sems.at[7 * a + k], device_id=to, device_id_type=MESH)

        mine = [pltpu.make_async_copy(v_refs[a], rows(a, *me), local_sems.at[a]) for a in range(na)]
        for cp in mine:
            cp.start()
        first = []
        for a in range(na):
            first.append(copy(a, 0, me, sibling, src=v_refs[a]))
            first += [copy(a, 1 + j, me, (*chip, c), src=v_refs[a]) for j, chip in enumerate(chips)]
        for cp in first:
            cp.start()
        passed = []
        for j, chip in enumerate(chips):
            for a in range(na):
                copy(a, 1 + j, (*chip, c), me).wait_recv()
                forward = copy(a, 4 + j, (*chip, c), sibling)
                forward.start()
                passed.append(forward)
        for a in range(na):
            copy(a, 0, sibling, me).wait_recv()
            for j, chip in enumerate(chips):
                copy(a, 4 + j, (*chip, 1 - c), me).wait_recv()
        for cp in first + passed:
            cp.wait_send()
        for cp in mine:
            cp.wait()

    return pl.pallas_call(
        body, name=name,
        out_shape=[jax.ShapeDtypeStruct((N_DEV,) + v.shape, v.dtype) for v in vs],
        in_specs=[pl.BlockSpec(memory_space=pl.ANY)] * (na + nf),
        out_specs=[pl.BlockSpec(memory_space=pl.ANY)] * na,
        scratch_shapes=[pltpu.SemaphoreType.DMA((7 * na,)), pltpu.SemaphoreType.DMA((7 * na,)),
                        pltpu.SemaphoreType.DMA((na,))],
    )(*vs, *after)


_HBM = pl.BlockSpec(memory_space=pltpu.HBM)
_SEM = pl.BlockSpec(memory_space=pltpu.SEMAPHORE)
_EFFECT = pltpu.SideEffectType.DATAFLOW_SIDE_EFFECTING


def _place_own(block, me):
    land = lax.empty((N_DEV,) + block.shape, block.dtype)
    return lax.dynamic_update_slice(land, block[None], (me,) + (0,) * block.ndim)


def _copies_start(srcs, lands, scatter, after, name):
    na = len(srcs)
    afters = tuple(after) if isinstance(after, (tuple, list)) else (after,)

    def body(*refs):
        src_refs, land_refs = refs[:na], refs[na:2 * na]
        sems = refs[2 * na + len(afters):4 * na + len(afters)]
        token = refs[-1]
        x, y, c = lax.axis_index("x"), lax.axis_index("y"), lax.axis_index("c")
        me = 4 * x + 2 * y + c
        for a in range(na):
            for k in range(1, N_DEV):
                px, py, pc = _peer(x, y, c, k)
                src = src_refs[a].at[4 * px + 2 * py + pc] if scatter else src_refs[a]
                pltpu.make_async_remote_copy(
                    src_ref=src, dst_ref=land_refs[a].at[me], send_sem=sems[2 * a], recv_sem=sems[2 * a + 1],
                    device_id=(px, py, pc), device_id_type=MESH).start()
        token[...] = jnp.zeros_like(token)

    hbm = lambda t: pltpu.HBM(t.shape, t.dtype)
    out = pl.pallas_call(
        body, name=name,
        out_shape=tuple([pltpu.SemaphoreType.DMA(())] * (2 * na) + [hbm(t) for t in srcs] + [hbm(t) for t in lands]
                        + [jax.ShapeDtypeStruct((8, 128), F32)]),
        in_specs=[_HBM] * (2 * na) + [pl.BlockSpec(memory_space=pl.ANY)] * len(afters),
        out_specs=tuple([_SEM] * (2 * na) + [_HBM] * (2 * na) + [pl.BlockSpec(memory_space=pltpu.VMEM)]),
        input_output_aliases={i: 2 * na + i for i in range(2 * na)},
        compiler_params=pltpu.CompilerParams(has_side_effects=_EFFECT),
    )(*[pltpu.with_memory_space_constraint(t, pltpu.HBM) for t in list(srcs) + list(lands)], *afters)
    return out[:2 * na], out[2 * na:3 * na], out[3 * na:4 * na], out[-1]


def _exchange_start(gs, me, after, name):
    own = [lax.dynamic_index_in_dim(g, me, 0, keepdims=False) for g in gs]
    return _copies_start(gs, [_place_own(o, me) for o in own], True, after, name)


def _copies_wait(sems, srcs, lands, after, name):
    na = len(srcs)

    def body(*refs):
        land_refs = refs[na:2 * na]
        sem_refs = refs[2 * na:4 * na]
        x, y, c = lax.axis_index("x"), lax.axis_index("y"), lax.axis_index("c")
        for a in range(na):
            seven = land_refs[a].at[pl.ds(0, N_DEV - 1)]
            copy = pltpu.make_async_remote_copy(
                src_ref=seven, dst_ref=seven, send_sem=sem_refs[2 * a], recv_sem=sem_refs[2 * a + 1],
                device_id=(x, y, c), device_id_type=MESH)
            copy.wait_send()
            copy.wait_recv()

    hbm = lambda t: pltpu.HBM(t.shape, t.dtype)
    out = pl.pallas_call(
        body, name=name,
        out_shape=tuple([hbm(t) for t in srcs] + [hbm(t) for t in lands]),
        in_specs=[_HBM] * (2 * na) + [_SEM] * (2 * na) + [pl.BlockSpec(memory_space=pl.ANY)],
        out_specs=tuple([_HBM] * (2 * na)),
        input_output_aliases={i: i for i in range(2 * na)},
        compiler_params=pltpu.CompilerParams(has_side_effects=_EFFECT),
    )(*srcs, *lands, *sems, after)
    return out[na:]


def _adamw_math(w, g, m, v):
    m = B1 * m + (1.0 - B1) * g
    v = B2 * v + (1.0 - B2) * (g * g)
    m_hat = m / (1.0 - B1 ** STEP)
    v_hat = v / (1.0 - B2 ** STEP)
    return -LR * (m_hat / (jnp.sqrt(v_hat) + AEPS) + WD * w), m, v


def _sum_adamw(parts, w, m, v, name):
    _, r, n = parts.shape
    tr = 256 if r % 256 == 0 else r

    def body(p_ref, w_ref, m_ref, v_ref, g_ref, d_ref, nm_ref, nv_ref):
        g = p_ref[0].astype(F32)
        for dev in range(1, N_DEV):
            g = g + p_ref[dev].astype(F32)
        g_ref[...] = g
        d_ref[...], nm_ref[...], nv_ref[...] = _adamw_math(w_ref[...], g, m_ref[...], v_ref[...])

    rows = pl.BlockSpec((None, tr, n), lambda i: (0, i, 0))
    sd = jax.ShapeDtypeStruct((1, r, n), F32)
    return pl.pallas_call(
        body, name=name, grid=(r // tr,), out_shape=(sd, sd, sd, sd),
        in_specs=[pl.BlockSpec((N_DEV, tr, n), lambda i: (0, i, 0)), rows, rows, rows],
        out_specs=(rows, rows, rows, rows),
        compiler_params=_cparams(("parallel",)),
    )(parts, w, m, v)


def _ada_fwd(c_all, ada_w, ada_b_cols):
    nb, ncol = c_all.shape[0], ada_w.shape[1]

    def body(c_ref, w_ref, b_ref, mod_ref, cond_ref):
        cond = _silu(c_ref[...])
        cond_ref[...] = cond
        mod_ref[...] = _dot_hi(cond, w_ref[...]) + b_ref[...]

    return pl.pallas_call(
        body, name="ada_fwd",
        out_shape=(jax.ShapeDtypeStruct((nb, ncol), F32), jax.ShapeDtypeStruct((nb, D), F32)),
        compiler_params=_cparams(),
    )(c_all, ada_w, ada_b_cols)


def _ada_bwd(cond_all, dmod_all, dmod_cols, smalls):
    ncol, nsm = dmod_cols.shape[1], smalls.shape[1]

    def body(cond_ref, dm_ref, dmc_ref, sm_ref, gw_ref, gb_ref, gs_ref):
        gw_ref[...] = lax.dot_general(cond_ref[...], dmc_ref[...], (((0,), (0,)), ((), ())),
                                      preferred_element_type=F32, precision=HI)
        gb_ref[...] = jnp.sum(dm_ref[...], axis=0, keepdims=True)
        gs_ref[...] = jnp.sum(sm_ref[...], axis=0, keepdims=True)

    return pl.pallas_call(
        body, name="ada_bwd",
        out_shape=(jax.ShapeDtypeStruct((D, ncol), F32), jax.ShapeDtypeStruct((1, 6 * D), F32),
                   jax.ShapeDtypeStruct((1, nsm), F32)),
        compiler_params=_cparams(),
    )(cond_all, dmod_all, dmod_cols, smalls)


IN_CUTS = (0, QW, QW + 2 * KVW, QW + 2 * KVW + CONVW, QW + 2 * KVW + CONVW + 2 * DH,
           QW + 2 * KVW + CONVW + 2 * DH + DNW, QW + 2 * KVW + CONVW + 2 * DH + DNW + D, IN_W)
IN_WIDTHS = tuple(b - a for a, b in zip(IN_CUTS[:-1], IN_CUTS[1:]))
IN_SHARD = IN_W // N_DEV


def _inproj_fwd(x, mod, g1, w_blk):
    B, S, _ = x.shape
    tm = _tile(S)

    def body(x_ref, mod_ref, g_ref, w_ref, h_ref, *o_refs):
        h = _rms_mod(x_ref[...], g_ref[...], mod_ref[1:2, :], mod_ref[0:1, :]).astype(BF16)
        h_ref[...] = h
        full = jnp.concatenate([jnp.dot(h, w_ref[j], preferred_element_type=F32) for j in range(N_DEV)], axis=1)
        for o_ref, lo, hi in zip(o_refs, IN_CUTS[:-1], IN_CUTS[1:]):
            o_ref[...] = full[:, lo:hi]

    return pl.pallas_call(
        body, name="inproj_fwd", grid=(B, S // tm),
        out_shape=[jax.ShapeDtypeStruct((B, S, D), BF16)] + [jax.ShapeDtypeStruct((B, S, w), F32) for w in IN_WIDTHS],
        in_specs=[_rows(tm, D), _perb(6, D), _full((1, D)), _resident(w_blk.shape)],
        out_specs=[_rows(tm, D)] + [_rows(tm, w) for w in IN_WIDTHS],
        compiler_params=_cparams(("parallel", "arbitrary")),
    )(x, mod, g1, w_blk)


def _inproj_bwd(x, mod, g1, dx1, dps, w_blk):
    B, S, _ = x.shape
    tm = _tile(S)
    n = len(dps)

    def body(x_ref, mod_ref, g_ref, dx1_ref, *refs):
        dp_refs, w_ref = refs[:n], refs[n]
        dblk_ref, gx_ref, dg_ref, dsc_ref, dsh_ref = refs[n + 1:]
        b, i = pl.program_id(0), pl.program_id(1)
        full = jnp.concatenate([r[...].astype(F32) for r in dp_refs], axis=1)
        dh = None
        for j in range(N_DEV):
            blk = full[:, IN_SHARD * j:IN_SHARD * (j + 1)].astype(BF16)
            dblk_ref[j] = blk
            t = _dot_nt(blk, w_ref[j])
            dh = t if dh is None else dh + t
        _, vjp = jax.vjp(_rms_mod, x_ref[...], g_ref[...], mod_ref[1:2, :], mod_ref[0:1, :])
        dx, dg, dsc, dsh = vjp(dh)
        gx_ref[...] = dx1_ref[...] + dx

        @pl.when((b == 0) & (i == 0))
        def _():
            dg_ref[...] = jnp.zeros_like(dg_ref)

        @pl.when(i == 0)
        def _():
            dsc_ref[...] = jnp.zeros_like(dsc_ref)
            dsh_ref[...] = jnp.zeros_like(dsh_ref)

        dg_ref[...] += dg
        dsc_ref[...] += dsc
        dsh_ref[...] += dsh

    return pl.pallas_call(
        body, name="inproj_bwd", grid=(B, S // tm),
        out_shape=[jax.ShapeDtypeStruct((B, N_DEV, S, IN_SHARD), BF16), jax.ShapeDtypeStruct((B, S, D), F32),
                   jax.ShapeDtypeStruct((1, D), F32), jax.ShapeDtypeStruct((B, 1, D), F32),
                   jax.ShapeDtypeStruct((B, 1, D), F32)],
        in_specs=[_rows(tm, D), _perb(6, D), _full((1, D)), _rows(tm, D)]
                 + [_rows(tm, w) for w in IN_WIDTHS] + [_resident(w_blk.shape)],
        out_specs=[pl.BlockSpec((None, N_DEV, tm, IN_SHARD), lambda b, i: (b, 0, i, 0)), _rows(tm, D),
                   _full((1, D)), _perb(1, D), _perb(1, D)],
        compiler_params=_cparams(("arbitrary", "arbitrary")),
    )(x, mod, g1, dx1, *dps, w_blk)


def _wgrad(a, b, name, after=None):
    B, na, S, K = a.shape
    nb, N = b.shape[1], b.shape[3]
    G = max(na, nb)
    tm = min(2048, S)
    nt = S // tm
    last = B * nt - 1

    def body(a_ref, b_ref, *rest):
        o_ref, acc = rest[-2:]
        t = pl.program_id(1)

        @pl.when(t == 0)
        def _():
            acc[...] = jnp.zeros_like(acc)

        acc[...] += lax.dot_general(a_ref[...], b_ref[...], (((0,), (0,)), ((), ())), preferred_element_type=F32)

        @pl.when(t == last)
        def _():
            o_ref[...] = acc[...].astype(BF16)

    return pl.pallas_call(
        body, name=name, grid=(G, B * nt),
        out_shape=jax.ShapeDtypeStruct((G, K, N), BF16),
        in_specs=[pl.BlockSpec((None, None, tm, K), lambda g, t: (t // nt, g if na > 1 else 0, t % nt, 0)),
                  pl.BlockSpec((None, None, tm, N), lambda g, t: (t // nt, g if nb > 1 else 0, t % nt, 0))]
                 + ([] if after is None else [pl.BlockSpec(memory_space=pl.ANY)]),
        out_specs=pl.BlockSpec((None, K, N), lambda g, t: (g, 0, 0)),
        scratch_shapes=[pltpu.VMEM((K, N), F32)],
        compiler_params=_cparams(("parallel", "arbitrary")),
    )(*((a, b) if after is None else (a, b, after)))


LANES = 128


def _attn_consts():
    inv_freq = THETA ** (-jnp.arange(0, ROT, 2, dtype=F32) / ROT)
    head = jnp.concatenate([inv_freq, inv_freq, jnp.zeros((HD - ROT,), F32)])
    invf = jnp.tile(head, LANES // HD)[None, :]
    mean_of = lambda w: jnp.asarray(np.kron(np.eye(w // HD), np.full((HD, HD), 1.0 / HD)), BF16)
    return invf, mean_of(QW), mean_of(KVW)


def _rope_tables(pos, invf):
    B, S, _ = pos.shape
    tr = min(1024, S)

    def body(p_ref, f_ref, c_ref, s_ref):
        ang = p_ref[...].astype(F32) * f_ref[...]
        c_ref[...] = jnp.cos(ang)
        s_ref[...] = jnp.sin(ang)

    sd = jax.ShapeDtypeStruct((B, S, LANES), F32)
    return pl.pallas_call(
        body, name="rope_tables", grid=(B, S // tr), out_shape=[sd, sd],
        in_specs=[_rows(tr, 1), _full((1, LANES))], out_specs=[_rows(tr, LANES), _rows(tr, LANES)],
        compiler_params=_cparams(("parallel", "parallel")),
    )(pos, invf)


def _rope_expand(cos, sin, reps):
    lane = lax.broadcasted_iota(jnp.int32, cos.shape, 1) % HD
    sa = jnp.where((lane >= ROT // 2) & (lane < ROT), sin, 0.0)
    sb = jnp.where(lane < ROT // 2, -sin, 0.0)
    rep = lambda t: jnp.concatenate([t] * reps, axis=1) if reps > 1 else t
    return rep(cos), rep(sa), rep(sb)


@jax.custom_vjp
def _rope(t, cos, sa, sb):
    w = t.shape[1]
    return t * cos + pltpu.roll(t, ROT // 2, 1) * sa + pltpu.roll(t, w - ROT // 2, 1) * sb


def _rope_fwd(t, cos, sa, sb):
    return _rope(t, cos, sa, sb), (cos, sa, sb)


def _rope_bwd(res, d):
    cos, sa, sb = res
    w = d.shape[1]
    dt = d * cos + pltpu.roll(d * sa, w - ROT // 2, 1) + pltpu.roll(d * sb, ROT // 2, 1)
    return dt, jnp.zeros_like(cos), jnp.zeros_like(sa), jnp.zeros_like(sb)


_rope.defvjp(_rope_fwd, _rope_bwd)


def _head_norm(t, g, mean_of):
    hi, lo = _split(t * t)
    ms = jnp.dot(hi, mean_of, preferred_element_type=F32) + jnp.dot(lo, mean_of, preferred_element_type=F32)
    return t * lax.rsqrt(ms + EPS) * g


def _attn_block(q, kvp, kvc, qg, kg, sinks, tq, tk, mq, mk, valid):
    qn = _rope(_head_norm(q, jnp.concatenate([qg] * HQ, axis=1), mq), *tq) * (HD ** -0.5)
    kv = jnp.concatenate([kvp, kvc], axis=0)
    kn = _rope(_head_norm(kv[:, 0:KVW], jnp.concatenate([kg] * HKV, axis=1), mk), *tk)
    per_tile = LANES // HD
    vT = jnp.transpose(kv[:, KVW:2 * KVW])
    qT = [jnp.transpose(qn[:, LANES * t:LANES * (t + 1)]) for t in range(QW // LANES)]
    head_T = lambda h: qT[h // per_tile][HD * (h % per_tile):HD * (h % per_tile + 1), :]
    none = jnp.zeros((HD, GRP * BLK), F32)
    o_T = []
    for j in range(HKV):
        q4T = jnp.concatenate([head_T(GRP * j + i) for i in range(GRP)], axis=1)
        sT = _dot(kn, jnp.concatenate([q4T, none] if j == 0 else [none, q4T], axis=0))
        sT = jnp.where(valid, sT, -1e30)
        sink = jnp.concatenate([jnp.broadcast_to(sinks[:, GRP * j + i:GRP * j + i + 1], (1, BLK)) for i in range(GRP)], axis=1)
        m = lax.stop_gradient(jnp.maximum(jnp.max(sT, axis=0, keepdims=True), sink))
        pT = jnp.exp(sT - m)
        den = jnp.sum(pT, axis=0, keepdims=True) + jnp.exp(sink - m)
        oT = _dot(vT[HD * j:HD * (j + 1), :], pT) * (1.0 / den)
        o_T += [oT[:, BLK * i:BLK * (i + 1)] for i in range(GRP)]
    return jnp.concatenate([jnp.transpose(jnp.concatenate(o_T[per_tile * t:per_tile * (t + 1)], axis=0))
                            for t in range(QW // LANES)], axis=1)


def _attn_tables(cp_ref, cc_ref, sp_ref, sc_ref, n):
    tq = _rope_expand(cc_ref[...], sc_ref[...], QW // LANES)
    tk = _rope_expand(jnp.concatenate([cp_ref[...], cc_ref[...]], axis=0),
                      jnp.concatenate([sp_ref[...], sc_ref[...]], axis=0), KVW // LANES)
    qi = lax.broadcasted_iota(jnp.int32, (2 * BLK, GRP * BLK), 1) % BLK + BLK
    kj = lax.broadcasted_iota(jnp.int32, (2 * BLK, GRP * BLK), 0)
    dist = qi - kj
    valid = (dist >= 0) & (dist < BLK) & ((kj >= BLK) | (n > 0))
    return tq, tk, valid


def _attn_fwd(aq, akv, cos, sin, qg, kg, sinks, mq, mk):
    B, S, _ = aq.shape
    nb = S // BLK

    def body(q_ref, kvp_ref, kvc_ref, cp_ref, cc_ref, sp_ref, sc_ref, qg_ref, kg_ref, sk_ref, mq_ref, mk_ref, o_ref):
        tq, tk, valid = _attn_tables(cp_ref, cc_ref, sp_ref, sc_ref, pl.program_id(1))
        o_ref[...] = _attn_block(q_ref[...], kvp_ref[...], kvc_ref[...], qg_ref[...], kg_ref[...], sk_ref[...],
                                 tq, tk, mq_ref[...], mk_ref[...], valid)

    prev = lambda b, n: (b, jnp.maximum(n - 1, 0), 0)
    cur = lambda b, n: (b, n, 0)
    return pl.pallas_call(
        body, name="attn_fwd", grid=(B, nb),
        out_shape=jax.ShapeDtypeStruct((B, S, QW), F32),
        in_specs=[pl.BlockSpec((None, BLK, QW), cur), pl.BlockSpec((None, BLK, 2 * KVW), prev),
                  pl.BlockSpec((None, BLK, 2 * KVW), cur), pl.BlockSpec((None, BLK, LANES), prev),
                  pl.BlockSpec((None, BLK, LANES), cur), pl.BlockSpec((None, BLK, LANES), prev),
                  pl.BlockSpec((None, BLK, LANES), cur), _full((1, HD)), _full((1, HD)), _full((1, HQ)),
                  _full((QW, QW)), _full((KVW, KVW))],
        out_specs=pl.BlockSpec((None, BLK, QW), cur),
        compiler_params=_cparams(("parallel", "arbitrary")),
    )(aq, akv, akv, cos, cos, sin, sin, qg, kg, sinks, mq, mk)


def _attn_bwd(aq, akv, cos, sin, qg, kg, sinks, mq, mk, do):
    B, S, _ = aq.shape
    nb = S // BLK

    def body(q_ref, kvp_ref, kvc_ref, cp_ref, cc_ref, sp_ref, sc_ref, qg_ref, kg_ref, sk_ref, mq_ref, mk_ref, do_ref,
             dq_ref, dkv_ref, dqg_ref, dkg_ref, dsk_ref, carry):
        b, i = pl.program_id(0), pl.program_id(1)
        tq, tk, valid = _attn_tables(cp_ref, cc_ref, sp_ref, sc_ref, nb - 1 - i)
        fn = functools.partial(_attn_block, tq=tq, tk=tk, mq=mq_ref[...], mk=mk_ref[...], valid=valid)
        _, vjp = jax.vjp(fn, q_ref[...], kvp_ref[...], kvc_ref[...], qg_ref[...], kg_ref[...], sk_ref[...])
        dq, dkvp, dkvc, dqg, dkg, dsk = vjp(do_ref[...])

        @pl.when(i == 0)
        def _():
            carry[...] = jnp.zeros_like(carry)

        @pl.when((b == 0) & (i == 0))
        def _():
            dqg_ref[...] = jnp.zeros_like(dqg_ref)
            dkg_ref[...] = jnp.zeros_like(dkg_ref)
            dsk_ref[...] = jnp.zeros_like(dsk_ref)

        dq_ref[...] = dq.astype(BF16)
        dkv_ref[...] = (dkvc + carry[...]).astype(BF16)
        carry[...] = dkvp
        dqg_ref[...] += dqg
        dkg_ref[...] += dkg
        dsk_ref[...] += dsk

    prev = lambda b, i: (b, jnp.maximum(nb - 2 - i, 0), 0)
    cur = lambda b, i: (b, nb - 1 - i, 0)
    return pl.pallas_call(
        body, name="attn_bwd", grid=(B, nb),
        out_shape=[jax.ShapeDtypeStruct((B, S, QW), BF16), jax.ShapeDtypeStruct((B, S, 2 * KVW), BF16),
                   jax.ShapeDtypeStruct((1, HD), F32), jax.ShapeDtypeStruct((1, HD), F32),
                   jax.ShapeDtypeStruct((1, HQ), F32)],
        in_specs=[pl.BlockSpec((None, BLK, QW), cur), pl.BlockSpec((None, BLK, 2 * KVW), prev),
                  pl.BlockSpec((None, BLK, 2 * KVW), cur), pl.BlockSpec((None, BLK, LANES), prev),
                  pl.BlockSpec((None, BLK, LANES), cur), pl.BlockSpec((None, BLK, LANES), prev),
                  pl.BlockSpec((None, BLK, LANES), cur), _full((1, HD)), _full((1, HD)), _full((1, HQ)),
                  _full((QW, QW)), _full((KVW, KVW)), pl.BlockSpec((None, BLK, QW), cur)],
        out_specs=[pl.BlockSpec((None, BLK, QW), cur), pl.BlockSpec((None, BLK, 2 * KVW), cur),
                   _full((1, HD)), _full((1, HD)), _full((1, HQ))],
        scratch_shapes=[pltpu.VMEM((BLK, 2 * KVW), F32)],
        compiler_params=_cparams(("arbitrary", "arbitrary")),
    )(aq, akv, akv, cos, cos, sin, sin, qg, kg, sinks, mq, mk, do)


def _conv_taps(xe, w, rows):
    y = None
    for j in range(CONV):
        sh = pltpu.roll(xe, CONV - 1 - j, 0)[8:8 + rows, :] if j < CONV - 1 else xe[8:8 + rows, :]
        y = sh * w[j:j + 1, :] if y is None else y + sh * w[j:j + 1, :]
    return y


def _conv_fwd(xin, w):
    B, S, C = xin.shape
    tc = min(512, S)
    r8 = tc // 8

    def body(xp_ref, x_ref, w_ref, o_ref):
        i = pl.program_id(1)
        xp = jnp.where(i > 0, xp_ref[...], 0.0)
        xe = jnp.concatenate([xp, x_ref[...]], axis=0)
        o_ref[...] = _silu(_conv_taps(xe, w_ref[...], tc))

    return pl.pallas_call(
        body, name="conv_fwd", grid=(B, S // tc),
        out_shape=jax.ShapeDtypeStruct((B, S, C), F32),
        in_specs=[pl.BlockSpec((None, 8, C), lambda b, i: (b, jnp.maximum(i * r8 - 1, 0), 0)),
                  _rows(tc, C), _full((CONV, C))],
        out_specs=_rows(tc, C),
        compiler_params=_cparams(("parallel", "arbitrary")),
    )(xin, xin, w)


def _conv_bwd(xin, w, dy):
    B, S, C = xin.shape
    tc = min(512, S)
    r8 = tc // 8
    nt = S // tc

    def body(xp_ref, x_ref, xn_ref, dy_ref, dyn_ref, w_ref, dx_ref, dw_ref):
        b, i = pl.program_id(0), pl.program_id(1)
        w = w_ref[...]
        xp = jnp.where(i > 0, xp_ref[...], 0.0)
        xe = jnp.concatenate([xp, x_ref[...], xn_ref[...]], axis=0)
        taps = [(pltpu.roll(xe, CONV - 1 - j, 0) if j < CONV - 1 else xe)[8:8 + tc + 8, :] for j in range(CONV)]
        pre = sum(t * w[j:j + 1, :] for j, t in enumerate(taps))
        sg = _sigmoid(pre)
        dyn = jnp.where(i < nt - 1, dyn_ref[...], 0.0)
        dpre = jnp.concatenate([dy_ref[...], dyn], axis=0) * (sg * (1.0 + pre * (1.0 - sg)))
        dx = dpre[0:tc, :] * w[CONV - 1:CONV, :]
        for j in range(CONV - 1):
            dx = dx + pltpu.roll(dpre, tc + 8 - (CONV - 1 - j), 0)[0:tc, :] * w[j:j + 1, :]
        dx_ref[...] = dx.astype(BF16)
        dcur = dpre[0:tc, :]
        lane_row = lax.broadcasted_iota(jnp.int32, (CONV, C), 0)
        dw = jnp.zeros((CONV, C), F32)
        for j in range(CONV):
            dw = dw + jnp.where(lane_row == j, jnp.sum(taps[j][0:tc, :] * dcur, axis=0, keepdims=True), 0.0)

        @pl.when((b == 0) & (i == 0))
        def _():
            dw_ref[...] = jnp.zeros_like(dw_ref)

        dw_ref[...] += dw

    return pl.pallas_call(
        body, name="conv_bwd", grid=(B, nt),
        out_shape=[jax.ShapeDtypeStruct((B, S, C), BF16), jax.ShapeDtypeStruct((CONV, C), F32)],
        in_specs=[pl.BlockSpec((None, 8, C), lambda b, i: (b, jnp.maximum(i * r8 - 1, 0), 0)),
                  _rows(tc, C),
                  pl.BlockSpec((None, 8, C), lambda b, i: (b, jnp.minimum((i + 1) * r8, S // 8 - 1), 0)),
                  _rows(tc, C),
                  pl.BlockSpec((None, 8, C), lambda b, i: (b, jnp.minimum((i + 1) * r8, S // 8 - 1), 0)),
                  _full((CONV, C))],
        out_specs=[_rows(tc, C), _full((CONV, C))],
        compiler_params=_cparams(("arbitrary", "arbitrary")),
    )(xin, xin, xin, dy, dy, w)


def _softplus(x):
    return jnp.maximum(x, 0.0) + jnp.log1p(jnp.exp(-jnp.abs(x)))


_BMM = (((2,), (1,)), ((0,), (0,)))
_BMM_NT = (((2,), (2,)), ((0,), (0,)))
_BMM_TN = (((1,), (1,)), ((0,), (0,)))


def _bmm(a, b, dims=_BMM):
    return lax.dot_general(a.astype(BF16), b.astype(BF16), dims, preferred_element_type=F32)


def _split(a):
    hi = a.astype(BF16)
    return hi, (a - hi.astype(F32)).astype(BF16)


def _bmm3(a, b, dims=_BMM):
    ah, al = _split(a)
    bh, bl = _split(b)
    d = lambda p, q: lax.dot_general(p, q, dims, preferred_element_type=F32)
    return d(ah, bh) + (d(ah, bl) + d(al, bh))


TRI_BASE = 8


def _tri_inverse(L):
    ii = lax.broadcasted_iota(jnp.int32, (CH, CH), 0)
    jj = lax.broadcasted_iota(jnp.int32, (CH, CH), 1)
    same = lambda size: (ii // size) == (jj // size)
    diag = jnp.where(same(TRI_BASE), L, 0.0)
    X = (ii == jj).astype(F32) - diag
    P = diag
    n = 2
    while n < TRI_BASE:
        P = _bmm3(P, P)
        X = X + _bmm3(X, P)
        n *= 2
    size = TRI_BASE
    while size < CH:
        joint = jnp.where(same(2 * size) & jnp.logical_not(same(size)), L, 0.0)
        X = X - _bmm3(X, _bmm3(joint, X))
        size *= 2
    return X


@jax.custom_vjp
def _tri_inverse_known(L, T):
    return T


def _tri_inverse_known_fwd(L, T):
    return T, T


def _tri_inverse_known_bwd(T, dT):
    return -_bmm3(T, _bmm3(dT, T, _BMM_NT), _BMM_TN), jnp.zeros_like(T)


_tri_inverse_known.defvjp(_tri_inverse_known_fwd, _tri_inverse_known_bwd)


def _cumsum_rows(g):
    n = g.shape[0]
    ii = lax.broadcasted_iota(jnp.int32, (n, CH, CH), 1)
    jj = lax.broadcasted_iota(jnp.int32, (n, CH, CH), 2)
    tri = (ii >= jj).astype(BF16)
    g0 = g.astype(BF16)
    r1 = g - g0.astype(F32)
    g1 = r1.astype(BF16)
    g2 = (r1 - g1.astype(F32)).astype(BF16)
    d = lambda q: lax.dot_general(tri, q, _BMM, preferred_element_type=F32)
    return d(g0) + (d(g1) + d(g2))


def _row_sums(t):
    n, r, w = t.shape
    hi, lo = _split(t.reshape(n * r, w))
    ones = jnp.ones((w, w), BF16)
    s = jnp.dot(hi, ones, preferred_element_type=F32) + jnp.dot(lo, ones, preferred_element_type=F32)
    return s.reshape(n, r, w)


def _dn_prep(t_known, qr, kr, v, a_raw, b_raw, a_log, dt_b):
    n = qr.shape[0]
    ii = lax.broadcasted_iota(jnp.int32, (n, CH, CH), 1)
    jj = lax.broadcasted_iota(jnp.int32, (n, CH, CH), 2)
    incl, strict = ii >= jj, ii > jj
    q = qr * lax.rsqrt(_row_sums(qr * qr) + EPS) * (DK ** -0.5)
    k = kr * lax.rsqrt(_row_sums(kr * kr) + EPS)
    beta = _sigmoid(b_raw)
    g = -jnp.exp(a_log) * _softplus(a_raw + dt_b)
    gcb = _cumsum_rows(jnp.broadcast_to(g, (n, CH, DK)))
    gc = gcb[:, :, 0:1]
    gc_row = jnp.swapaxes(gcb, 1, 2)[:, 0:1, 0:CH]
    decay = jnp.where(incl, jnp.exp(jnp.where(incl, gc - gc_row, 0.0)), 0.0)
    kb = k * beta
    L = jnp.where(strict, _bmm(kb, k, _BMM_NT) * decay, 0.0)
    T = _tri_inverse(L) if t_known is None else _tri_inverse_known(L, t_known)
    eg = jnp.exp(gc)
    u = _bmm(T, v * beta)
    w = _bmm(T, kb * eg)
    a_in = _bmm(q, k, _BMM_NT) * decay
    g_last = gc[:, CH - 1:CH, :]
    return u, w, q * eg, k * jnp.exp(g_last - gc), a_in, jnp.exp(g_last), T


def _dn_step(S0, u, w, qd, kd, a_in, cd):
    r = _bmm(jnp.concatenate([w, qd], axis=1), S0)
    v_new = u - r[:, 0:CH, :]
    o = r[:, CH:2 * CH, :] + _bmm(a_in, v_new)
    S1 = S0 * cd + _bmm(kd, v_new, _BMM_TN)
    return o, S1


def _dn_stack(cq, ba, al, dt, G):
    cols = [[] for _ in range(7)]
    for c in range(G):
        rows = slice(CH * c, CH * (c + 1))
        for h in range(DH):
            parts = (cq[rows, DK * h:DK * (h + 1)], cq[rows, DNW + DK * h:DNW + DK * (h + 1)],
                     cq[rows, 2 * DNW + DK * h:2 * DNW + DK * (h + 1)], ba[rows, DH + h:DH + h + 1],
                     ba[rows, h:h + 1], al[:, h:h + 1], dt[:, h:h + 1])
            for col, p in zip(cols, parts):
                col.append(p)
    return tuple(jnp.stack(col) for col in cols)


def _dn_group(S, want):
    g = want
    while (S // CH) % g:
        g //= 2
    return g


def _dn_prep_fwd(cq, ba, a_log, dt_b):
    B, S, _ = cq.shape
    nc = S // CH
    G = _dn_group(S, 4)

    def body(cq_ref, ba_ref, al_ref, dt_ref, u_ref, w_ref, qd_ref, kd_ref, a_ref, t_ref, cd_ref):
        ops = _dn_stack(cq_ref[...], ba_ref[...], al_ref[...], dt_ref[...], G)
        u, w, qd, kd, a_in, cd, T = _dn_prep(None, *ops)
        lane4 = lax.broadcasted_iota(jnp.int32, (1, DH), 1)
        for c in range(G):
            rows = slice(CH * c, CH * (c + 1))
            cdrow = jnp.zeros((1, DH), F32)
            for h in range(DH):
                n = DH * c + h
                lanes = slice(DK * h, DK * (h + 1))
                u_ref[rows, lanes] = u[n]
                w_ref[rows, lanes] = w[n]
                qd_ref[rows, lanes] = qd[n]
                kd_ref[rows, lanes] = kd[n]
                a_ref[rows, CH * h:CH * (h + 1)] = a_in[n]
                t_ref[rows, CH * h:CH * (h + 1)] = T[n]
                cdrow = cdrow + jnp.where(lane4 == h, cd[n], 0.0)
            cd_ref[c] = cdrow

    wide = jax.ShapeDtypeStruct((B, S, DNW), F32)
    sq = jax.ShapeDtypeStruct((B, S, DH * CH), F32)
    return pl.pallas_call(
        body, name="dn_prep_fwd", grid=(B, nc // G),
        out_shape=[wide, wide, wide, wide, sq, sq, jax.ShapeDtypeStruct((B, nc, 1, DH), F32)],
        in_specs=[_rows(G * CH, CONVW), _rows(G * CH, 2 * DH), _full((1, DH)), _full((1, DH))],
        out_specs=[_rows(G * CH, DNW)] * 4 + [_rows(G * CH, DH * CH)] * 2
                  + [pl.BlockSpec((None, G, 1, DH), lambda b, i: (b, i, 0, 0))],
        compiler_params=_cparams(("parallel", "parallel")),
    )(cq, ba, a_log, dt_b)


def _dn_seq_specs(B, steps, gs, rev):
    at = (lambda i: steps - 1 - i) if rev else (lambda i: i)
    wide = pl.BlockSpec((B, gs * CH, DNW), lambda i: (0, at(i), 0))
    a_spec = pl.BlockSpec((B, gs * CH, DH * CH), lambda i: (0, at(i), 0))
    cd_spec = pl.BlockSpec((B, gs, 1, DH), lambda i: (0, at(i), 0, 0))
    st_spec = pl.BlockSpec((B, gs, DH, DK, DK), lambda i: (0, at(i), 0, 0, 0))
    return wide, a_spec, cd_spec, st_spec


def _dn_step_operands(B, c, u_ref, w_ref, qd_ref, kd_ref, a_ref, cd_ref):
    pairs = [(b, h) for b in range(B) for h in range(DH)]
    rows = slice(CH * c, CH * (c + 1))
    wide = lambda ref: jnp.stack([ref[b, rows, DK * h:DK * (h + 1)] for b, h in pairs])
    a_in = jnp.stack([a_ref[b, rows, CH * h:CH * (h + 1)] for b, h in pairs])
    cd = jnp.stack([cd_ref[b, c, :, h:h + 1] for b, h in pairs])
    return wide(u_ref), wide(w_ref), wide(qd_ref), wide(kd_ref), a_in, cd


def _dn_seq_fwd(u, w, qd, kd, a_in, cd):
    B, S, _ = u.shape
    nc = S // CH
    gs = _dn_group(S, 8)

    def body(u_ref, w_ref, qd_ref, kd_ref, a_ref, cd_ref, o_ref, st_ref, state):
        @pl.when(pl.program_id(0) == 0)
        def _():
            state[...] = jnp.zeros_like(state)

        S0 = state[...]
        for c in range(gs):
            for b in range(B):
                st_ref[b, c] = S0[DH * b:DH * (b + 1)]
            o, S0 = _dn_step(S0, *_dn_step_operands(B, c, u_ref, w_ref, qd_ref, kd_ref, a_ref, cd_ref))
            for b in range(B):
                for h in range(DH):
                    o_ref[b, CH * c:CH * (c + 1), DK * h:DK * (h + 1)] = o[DH * b + h]
        state[...] = S0

    wide, a_spec, cd_spec, st_spec = _dn_seq_specs(B, nc // gs, gs, False)
    return pl.pallas_call(
        body, name="dn_seq_fwd", grid=(nc // gs,),
        out_shape=[jax.ShapeDtypeStruct((B, S, DNW), F32), jax.ShapeDtypeStruct((B, nc, DH, DK, DK), F32)],
        in_specs=[wide, wide, wide, wide, a_spec, cd_spec],
        out_specs=[wide, st_spec],
        scratch_shapes=[pltpu.VMEM((B * DH, DK, DK), F32)],
        compiler_params=_cparams(("arbitrary",)),
    )(u, w, qd, kd, a_in, cd)


def _dn_seq_bwd(u, w, qd, kd, a_in, cd, states, do):
    B, S, _ = u.shape
    nc = S // CH
    gs = _dn_group(S, 8)

    def body(u_ref, w_ref, qd_ref, kd_ref, a_ref, cd_ref, st_ref, do_ref,
             du_ref, dw_ref, dqd_ref, dkd_ref, da_ref, dcd_ref, dstate):
        @pl.when(pl.program_id(0) == 0)
        def _():
            dstate[...] = jnp.zeros_like(dstate)

        lane4 = lax.broadcasted_iota(jnp.int32, (1, DH), 1)
        dS = dstate[...]
        for c in reversed(range(gs)):
            rows = slice(CH * c, CH * (c + 1))
            S0 = jnp.concatenate([st_ref[b, c] for b in range(B)], axis=0)
            do = jnp.stack([do_ref[b, rows, DK * h:DK * (h + 1)] for b in range(B) for h in range(DH)])
            _, vjp = jax.vjp(_dn_step, S0, *_dn_step_operands(B, c, u_ref, w_ref, qd_ref, kd_ref, a_ref, cd_ref))
            dS, du, dw, dqd, dkd, da, dcd = vjp((do, dS))
            for b in range(B):
                dcdrow = jnp.zeros((1, DH), F32)
                for h in range(DH):
                    n = DH * b + h
                    lanes = slice(DK * h, DK * (h + 1))
                    du_ref[b, rows, lanes] = du[n]
                    dw_ref[b, rows, lanes] = dw[n]
                    dqd_ref[b, rows, lanes] = dqd[n]
                    dkd_ref[b, rows, lanes] = dkd[n]
                    da_ref[b, rows, CH * h:CH * (h + 1)] = da[n]
                    dcdrow = dcdrow + jnp.where(lane4 == h, dcd[n], 0.0)
                dcd_ref[b, c] = dcdrow
        dstate[...] = dS

    wide, a_spec, cd_spec, st_spec = _dn_seq_specs(B, nc // gs, gs, True)
    sd = jax.ShapeDtypeStruct((B, S, DNW), F32)
    return pl.pallas_call(
        body, name="dn_seq_bwd", grid=(nc // gs,),
        out_shape=[sd, sd, sd, sd, jax.ShapeDtypeStruct((B, S, DH * CH), F32), jax.ShapeDtypeStruct((B, nc, 1, DH), F32)],
        in_specs=[wide, wide, wide, wide, a_spec, cd_spec, st_spec, wide],
        out_specs=[wide, wide, wide, wide, a_spec, cd_spec],
        scratch_shapes=[pltpu.VMEM((B * DH, DK, DK), F32)],
        compiler_params=_cparams(("arbitrary",)),
    )(u, w, qd, kd, a_in, cd, states, do)


def _dn_prep_bwd(cq, ba, a_log, dt_b, t_inv, du, dw, dqd, dkd, da, dcd):
    B, S, _ = cq.shape
    nc = S // CH
    G = _dn_group(S, 4)

    def body(cq_ref, ba_ref, al_ref, dt_ref, t_ref, du_ref, dw_ref, dqd_ref, dkd_ref, da_ref, dcd_ref,
             dcq_ref, dba_ref, dal_ref, ddt_ref):
        @pl.when((pl.program_id(0) == 0) & (pl.program_id(1) == 0))
        def _():
            dal_ref[...] = jnp.zeros_like(dal_ref)
            ddt_ref[...] = jnp.zeros_like(ddt_ref)

        pairs = [(c, h) for c in range(G) for h in range(DH)]
        rows = lambda c: slice(CH * c, CH * (c + 1))
        wide = lambda ref: jnp.stack([ref[rows(c), DK * h:DK * (h + 1)] for c, h in pairs])
        square = lambda ref: jnp.stack([ref[rows(c), CH * h:CH * (h + 1)] for c, h in pairs])
        ops = _dn_stack(cq_ref[...], ba_ref[...], al_ref[...], dt_ref[...], G)
        cots = (wide(du_ref), wide(dw_ref), wide(dqd_ref), wide(dkd_ref), square(da_ref),
                jnp.stack([dcd_ref[c][:, h:h + 1] for c, h in pairs]), jnp.zeros((len(pairs), CH, CH), F32))
        _, vjp = jax.vjp(functools.partial(_dn_prep, square(t_ref)), *ops)
        dq, dk, dv, dar, dbr, dl, dd = vjp(cots)
        lane8 = lax.broadcasted_iota(jnp.int32, (CH, 2 * DH), 1)
        lane4 = lax.broadcasted_iota(jnp.int32, (1, DH), 1)
        dal = jnp.zeros((1, DH), F32)
        ddt = jnp.zeros((1, DH), F32)
        for c in range(G):
            dba = jnp.zeros((CH, 2 * DH), F32)
            for h in range(DH):
                n = DH * c + h
                dcq_ref[rows(c), DK * h:DK * (h + 1)] = dq[n]
                dcq_ref[rows(c), DNW + DK * h:DNW + DK * (h + 1)] = dk[n]
                dcq_ref[rows(c), 2 * DNW + DK * h:2 * DNW + DK * (h + 1)] = dv[n]
                dba = dba + jnp.where(lane8 == h, dbr[n], 0.0) + jnp.where(lane8 == DH + h, dar[n], 0.0)
                dal = dal + jnp.where(lane4 == h, dl[n], 0.0)
                ddt = ddt + jnp.where(lane4 == h, dd[n], 0.0)
            dba_ref[rows(c), :] = dba.astype(BF16)
        dal_ref[...] += dal
        ddt_ref[...] += ddt

    return pl.pallas_call(
        body, name="dn_prep_bwd", grid=(B, nc // G),
        out_shape=[jax.ShapeDtypeStruct((B, S, CONVW), F32), jax.ShapeDtypeStruct((B, S, 2 * DH), BF16),
                   jax.ShapeDtypeStruct((1, DH), F32), jax.ShapeDtypeStruct((1, DH), F32)],
        in_specs=[_rows(G * CH, CONVW), _rows(G * CH, 2 * DH), _full((1, DH)), _full((1, DH)), _rows(G * CH, DH * CH)]
                 + [_rows(G * CH, DNW)] * 4 + [_rows(G * CH, DH * CH),
                                               pl.BlockSpec((None, G, 1, DH), lambda b, i: (b, i, 0, 0))],
        out_specs=[_rows(G * CH, CONVW), _rows(G * CH, 2 * DH), _full((1, DH)), _full((1, DH))],
        compiler_params=_cparams(("arbitrary", "arbitrary")),
    )(cq, ba, a_log, dt_b, t_inv, du, dw, dqd, dkd, da, dcd)


def _gated_norm(o, z, g):
    outs = []
    for h in range(DH):
        t = o[:, DK * h:DK * (h + 1)]
        r = lax.rsqrt(jnp.mean(t * t, axis=-1, keepdims=True) + EPS)
        outs.append(t * r * g * _silu(z[:, DK * h:DK * (h + 1)]))
    return jnp.concatenate(outs, axis=1)


def _mix_fwd(x, o_attn, o_dn, z, ga, gd, mod, dn_g, w_branch, w_out):
    B, S, _ = x.shape
    tm = _tile(S, 512)

    def body(x_ref, oa_ref, od_ref, z_ref, ga_ref, gd_ref, mod_ref, g_ref, wb_ref, wo_ref,
             x1_ref, mix_ref, mg_ref, ob_ref):
        oa = oa_ref[...].astype(BF16)
        od = _gated_norm(od_ref[...], z_ref[...], g_ref[...]).astype(BF16)
        ob_ref[0] = oa
        ob_ref[1] = od
        ya = jnp.dot(oa, wb_ref[0:QW, :], preferred_element_type=F32)
        yd = jnp.dot(od, wb_ref[QW:QW + DNW, :], preferred_element_type=F32)
        merged = (_sigmoid(ga_ref[...]) * ya + _sigmoid(gd_ref[...]) * yd).astype(BF16)
        mg_ref[...] = merged
        mix = jnp.dot(merged, wo_ref[...], preferred_element_type=F32)
        mix_ref[...] = mix
        x1_ref[...] = x_ref[...] + mod_ref[2:3, :] * mix

    return pl.pallas_call(
        body, name="mix_fwd", grid=(B, S // tm),
        out_shape=[jax.ShapeDtypeStruct((B, S, D), F32), jax.ShapeDtypeStruct((B, S, D), F32),
                   jax.ShapeDtypeStruct((B, S, D), BF16), jax.ShapeDtypeStruct((B, 2, S, QW), BF16)],
        in_specs=[_rows(tm, D), _rows(tm, QW), _rows(tm, DNW), _rows(tm, DNW), _rows(tm, D), _rows(tm, D),
                  _perb(6, D), _full((1, DK)), _resident(w_branch.shape), _resident(w_out.shape)],
        out_specs=[_rows(tm, D), _rows(tm, D), _rows(tm, D), _stacked(2, tm, QW)],
        compiler_params=_cparams(("parallel", "arbitrary")),
    )(x, o_attn, o_dn, z, ga, gd, mod, dn_g, w_branch, w_out)


def _mix_bwd(dx1, mix, o_attn, o_dn, z, ga, gd, mod, dn_g, w_branch, w_out):
    B, S, _ = dx1.shape
    tm = _tile(S)

    def body(dx1_ref, mix_ref, oa_ref, od_ref, z_ref, ga_ref, gd_ref, mod_ref, g_ref, wb_ref, wo_ref,
             dmix_ref, dyo_ref, dga_ref, dgd_ref, dz_ref, doa_ref, dod_ref, dgate_ref, dg_ref):
        b, i = pl.program_id(0), pl.program_id(1)
        dx1 = dx1_ref[...]
        dmix = (dx1 * mod_ref[2:3, :]).astype(BF16)
        dmix_ref[...] = dmix
        dgate = jnp.sum(dx1 * mix_ref[...], axis=0, keepdims=True)
        dmerged = _dot_nt(dmix, wo_ref[...])
        odn, gn_vjp = jax.vjp(_gated_norm, od_ref[...], z_ref[...], g_ref[...])
        ya = _dot(oa_ref[...], wb_ref[0:QW, :])
        yd = _dot(odn, wb_ref[QW:QW + DNW, :])
        sa, sd = _sigmoid(ga_ref[...]), _sigmoid(gd_ref[...])
        dya = (dmerged * sa).astype(BF16)
        dyd = (dmerged * sd).astype(BF16)
        dyo_ref[0] = dya
        dyo_ref[1] = dyd
        dga_ref[...] = (dmerged * ya * sa * (1.0 - sa)).astype(BF16)
        dgd_ref[...] = (dmerged * yd * sd * (1.0 - sd)).astype(BF16)
        doa_ref[...] = _dot_nt(dya, wb_ref[0:QW, :])
        dodn = _dot_nt(dyd, wb_ref[QW:QW + DNW, :])
        dod, dz, dg = gn_vjp(dodn)
        dod_ref[...] = dod
        dz_ref[...] = dz.astype(BF16)

        @pl.when(i == 0)
        def _():
            dgate_ref[...] = jnp.zeros_like(dgate_ref)

        @pl.when((b == 0) & (i == 0))
        def _():
            dg_ref[...] = jnp.zeros_like(dg_ref)

        dgate_ref[...] += dgate
        dg_ref[...] += dg

    return pl.pallas_call(
        body, name="mix_bwd", grid=(B, S // tm),
        out_shape=[jax.ShapeDtypeStruct((B, S, D), BF16), jax.ShapeDtypeStruct((B, 2, S, D), BF16),
                   jax.ShapeDtypeStruct((B, S, D), BF16), jax.ShapeDtypeStruct((B, S, D), BF16),
                   jax.ShapeDtypeStruct((B, S, DNW), BF16),
                   jax.ShapeDtypeStruct((B, S, QW), F32), jax.ShapeDtypeStruct((B, S, DNW), F32),
                   jax.ShapeDtypeStruct((B, 1, D), F32), jax.ShapeDtypeStruct((1, DK), F32)],
        in_specs=[_rows(tm, D), _rows(tm, D), _rows(tm, QW), _rows(tm, DNW), _rows(tm, DNW), _rows(tm, D),
                  _rows(tm, D), _perb(6, D), _full((1, DK)), _resident(w_branch.shape), _resident(w_out.shape)],
        out_specs=[_rows(tm, D), _stacked(2, tm, D), _rows(tm, D), _rows(tm, D), _rows(tm, DNW),
                   _rows(tm, QW), _rows(tm, DNW), _perb(1, D), _full((1, DK))],
        compiler_params=_cparams(("arbitrary", "arbitrary")),
    )(dx1, mix, o_attn, o_dn, z, ga, gd, mod, dn_g, w_branch, w_out)


GU_SHARD = 2 * FFN // N_DEV
GU_HALF = N_DEV // 2


def _ffn1_fwd(x1, mod, g2, w_gu):
    B, S, _ = x1.shape
    tm = _tile(S)

    def body(x_ref, mod_ref, g_ref, w_ref, h_ref, gate_ref, up_ref, act_ref):
        h = _rms_mod(x_ref[...], g_ref[...], mod_ref[4:5, :], mod_ref[3:4, :]).astype(BF16)
        h_ref[...] = h
        for j in range(GU_HALF):
            gate = jnp.dot(h, w_ref[j], preferred_element_type=F32)
            up = jnp.dot(h, w_ref[GU_HALF + j], preferred_element_type=F32)
            gate_ref[j] = gate
            up_ref[j] = up
            act_ref[j] = (_silu(gate) * up).astype(BF16)

    blk = lambda dt: jax.ShapeDtypeStruct((B, GU_HALF, S, GU_SHARD), dt)
    return pl.pallas_call(
        body, name="ffn1_fwd", grid=(B, S // tm),
        out_shape=[jax.ShapeDtypeStruct((B, S, D), BF16), blk(F32), blk(F32), blk(BF16)],
        in_specs=[_rows(tm, D), _perb(6, D), _full((1, D)), _resident(w_gu.shape)],
        out_specs=[_rows(tm, D)] + [_stacked(GU_HALF, tm, GU_SHARD)] * 3,
        compiler_params=_cparams(("parallel", "arbitrary")),
    )(x1, mod, g2, w_gu)


def _ffn2_fwd(act, x1, target, mod, w_down):
    B, S, _ = x1.shape
    tm = _tile(S, 512)

    def body(a_ref, x_ref, t_ref, mod_ref, w_ref, dy_ref, loss_ref, dgate_ref):
        b, i = pl.program_id(0), pl.program_id(1)
        y = jnp.dot(a_ref[0], w_ref[0], preferred_element_type=F32)
        for j in range(1, GU_HALF):
            y = y + jnp.dot(a_ref[j], w_ref[j], preferred_element_type=F32)
        err = x_ref[...] + mod_ref[5:6, :] * y - t_ref[...]
        dy = err * (1.0 / D)
        dy_ref[...] = dy

        @pl.when((b == 0) & (i == 0))
        def _():
            loss_ref[...] = jnp.zeros_like(loss_ref)

        @pl.when(i == 0)
        def _():
            dgate_ref[...] = jnp.zeros_like(dgate_ref)

        loss_ref[...] += (0.5 / D) * jnp.sum(err * err)
        dgate_ref[...] += jnp.sum(dy * y, axis=0, keepdims=True)

    return pl.pallas_call(
        body, name="ffn2_fwd", grid=(B, S // tm),
        out_shape=[jax.ShapeDtypeStruct((B, S, D), F32), jax.ShapeDtypeStruct((1, 128), F32),
                   jax.ShapeDtypeStruct((B, 1, D), F32)],
        in_specs=[_stacked(GU_HALF, tm, GU_SHARD), _rows(tm, D), _rows(tm, D), _perb(6, D), _resident(w_down.shape)],
        out_specs=[_rows(tm, D), _full((1, 128)), _perb(1, D)],
        compiler_params=_cparams(("arbitrary", "arbitrary")),
    )(act, x1, target, mod, w_down)


def _ffn2_bwd(dy, gate, up, mod, w_down):
    B, S, _ = dy.shape
    tm = _tile(S)

    def body(dy_ref, gate_ref, up_ref, mod_ref, w_ref, dgu_ref, dyg_ref):
        dyg = (dy_ref[...] * mod_ref[5:6, :]).astype(BF16)
        dyg_ref[...] = dyg
        for j in range(GU_HALF):
            dact = _dot_nt(dyg, w_ref[j])
            gate, up = gate_ref[j], up_ref[j]
            sg = _sigmoid(gate)
            dgu_ref[j] = (dact * up * (sg * (1.0 + gate * (1.0 - sg)))).astype(BF16)
            dgu_ref[GU_HALF + j] = (dact * (gate * sg)).astype(BF16)

    return pl.pallas_call(
        body, name="ffn2_bwd", grid=(B, S // tm),
        out_shape=[jax.ShapeDtypeStruct((B, N_DEV, S, GU_SHARD), BF16), jax.ShapeDtypeStruct((B, S, D), BF16)],
        in_specs=[_rows(tm, D), _stacked(GU_HALF, tm, GU_SHARD), _stacked(GU_HALF, tm, GU_SHARD), _perb(6, D),
                  _resident(w_down.shape)],
        out_specs=[_stacked(N_DEV, tm, GU_SHARD), _rows(tm, D)],
        compiler_params=_cparams(("parallel", "arbitrary")),
    )(dy, gate, up, mod, w_down)


def _ffn1_bwd(dgu, x1, dy, mod, g2, w_gu):
    B, S, _ = x1.shape
    tm = _tile(S, 512)

    def body(dgu_ref, x_ref, dy_ref, mod_ref, g_ref, w_ref, dx1_ref, dg_ref, dsc_ref, dsh_ref):
        b, i = pl.program_id(0), pl.program_id(1)
        dh = _dot_nt(dgu_ref[0], w_ref[0])
        for j in range(1, N_DEV):
            dh = dh + _dot_nt(dgu_ref[j], w_ref[j])
        _, vjp = jax.vjp(_rms_mod, x_ref[...], g_ref[...], mod_ref[4:5, :], mod_ref[3:4, :])
        dx, dg, dsc, dsh = vjp(dh)
        dx1_ref[...] = dy_ref[...] + dx

        @pl.when((b == 0) & (i == 0))
        def _():
            dg_ref[...] = jnp.zeros_like(dg_ref)

        @pl.when(i == 0)
        def _():
            dsc_ref[...] = jnp.zeros_like(dsc_ref)
            dsh_ref[...] = jnp.zeros_like(dsh_ref)

        dg_ref[...] += dg
        dsc_ref[...] += dsc
        dsh_ref[...] += dsh

    return pl.pallas_call(
        body, name="ffn1_bwd", grid=(B, S // tm),
        out_shape=[jax.ShapeDtypeStruct((B, S, D), F32), jax.ShapeDtypeStruct((1, D), F32),
                   jax.ShapeDtypeStruct((B, 1, D), F32), jax.ShapeDtypeStruct((B, 1, D), F32)],
        in_specs=[_stacked(N_DEV, tm, GU_SHARD), _rows(tm, D), _rows(tm, D), _perb(6, D), _full((1, D)),
                  _resident(w_gu.shape)],
        out_specs=[_rows(tm, D), _full((1, D)), _perb(1, D), _perb(1, D)],
        compiler_params=_cparams(("arbitrary", "arbitrary")),
    )(dgu, x1, dy, mod, g2, w_gu)


def _adamw(w, g, m, v, name):
    def body(w_ref, g_ref, m_ref, v_ref, d_ref, nm_ref, nv_ref):
        d_ref[...], nm_ref[...], nv_ref[...] = _adamw_math(w_ref[...], g_ref[...], m_ref[...], v_ref[...])

    sd = jax.ShapeDtypeStruct(w.shape, F32)
    return pl.pallas_call(body, name=name, out_shape=(sd, sd, sd), compiler_params=_cparams())(w, g, m, v)


def kernel(x, c, positions, ada_w, ada_b, norm1_g, w_in, conv_w, q_norm_g, k_norm_g, sinks, a_log, dt_bias, dn_norm_g, w_branch, w_out, norm2_g, w_gate_up, w_down, loss_target, m_ada_w, m_ada_b, m_norm1_g, m_w_in, m_conv_w, m_q_norm_g, m_k_norm_g, m_sinks, m_a_log, m_dt_bias, m_dn_norm_g, m_w_branch, m_w_out, m_norm2_g, m_w_gate_up, m_w_down, v_ada_w, v_ada_b, v_norm1_g, v_w_in, v_conv_w, v_q_norm_g, v_k_norm_g, v_sinks, v_a_log, v_dt_bias, v_dn_norm_g, v_w_branch, v_w_out, v_norm2_g, v_w_gate_up, v_w_down):
    B, S, _ = x.shape
    me = 4 * lax.axis_index("x") + 2 * lax.axis_index("y") + lax.axis_index("c")

    shards = [w[0].astype(BF16) for w in (w_in, w_branch, w_out, w_gate_up, w_down)]

    c_all = _all_gather_small(c, "gather_c").reshape(N_DEV * B, D)
    ncol = 6 * D // N_DEV
    mod_cols, cond_all = _ada_fwd(c_all, ada_w[0], lax.dynamic_slice(ada_b, (0, me * ncol), (1, ncol)))
    mod_all = _all_gather_small(mod_cols, "gather_mod").transpose(1, 0, 2).reshape(N_DEV * B, 6 * D)
    mod = lax.dynamic_slice(mod_all, (me * B, 0), (B, 6 * D)).reshape(B, 6, D)
    conv2 = conv_w.reshape(CONV, CONVW // N_DEV)
    conv_all = _all_gather_small(conv2, "gather_conv").transpose(1, 0, 2).reshape(CONV, CONVW)

    (w_in_b,) = _all_gather_big(shards[:1], "gather_w_in", after=(mod, conv_all))
    w_sems, w_srcs, w_lands, w_token = _copies_start(shards[1:], [_place_own(s, me) for s in shards[1:]], False,
                                                    w_in_b, "gather_rest_start")

    h1, aq, akv, dnx, ba, z, ga, gd = _inproj_fwd(x, mod, norm1_g + w_token[0, 0], w_in_b)
    invf, mean_q, mean_k = _attn_consts()
    rope_cos, rope_sin = _rope_tables(positions.reshape(B, S, 1), invf)
    o_attn = _attn_fwd(aq, akv, rope_cos, rope_sin, q_norm_g, k_norm_g, sinks, mean_q, mean_k)
    cq = _conv_fwd(dnx, conv_all)
    dn_u, dn_w, dn_qd, dn_kd, dn_a, dn_t, dn_cd = _dn_prep_fwd(cq, ba, a_log, dt_bias)
    o_dn, states = _dn_seq_fwd(dn_u, dn_w, dn_qd, dn_kd, dn_a, dn_cd)
    w_branch_g, w_out_g, w_gu_b, w_down_g = _copies_wait(w_sems, w_srcs, w_lands, o_dn, "gather_wait_rest")
    w_branch_f = w_branch_g.reshape(D, D)
    w_out_f = w_out_g.reshape(D, D)
    w_down_b = w_down_g.reshape(GU_HALF, GU_SHARD, D)
    x1, mix, merged, ob = _mix_fwd(x, o_attn, o_dn, z, ga, gd, mod, dn_norm_g, w_branch_f, w_out_f)
    h2, gate, up, act = _ffn1_fwd(x1, mod, norm2_g, w_gu_b)
    dy, loss_part, d_gate2 = _ffn2_fwd(act, x1, loss_target, mod, w_down_b)
    loss = lax.psum(loss_part[0, 0], ("x", "y", "c"))

    one = lambda t: t.reshape(B, 1, S, t.shape[-1])
    dgu, dyg = _ffn2_bwd(dy, gate, up, mod, w_down_b)
    g_w_down = _wgrad(act, one(dyg), "wgrad_down")
    dx1, d_n2g, d_scale2, d_shift2 = _ffn1_bwd(dgu, x1, dy, mod, norm2_g, w_gu_b)
    g_w_gu = _wgrad(one(h2), dgu, "wgrad_gate_up")
    ffn = _exchange_start([g_w_gu, g_w_down.reshape(N_DEV, FFN // N_DEV, D)], me, dx1, "exchange_ffn_start")
    dmix, dyo, dga, dgd, dz, d_oa, d_od, d_gate1, d_dng = _mix_bwd(
        dx1, mix, o_attn, o_dn, z, ga, gd, mod, dn_norm_g + ffn[3][0, 0], w_branch_f, w_out_f)
    d_dn = _dn_seq_bwd(dn_u, dn_w, dn_qd, dn_kd, dn_a, dn_cd, states, d_od)
    dcq, dba, d_alog, d_dtb = _dn_prep_bwd(cq, ba, a_log, dt_bias, dn_t, *d_dn)
    ddnx, d_conv = _conv_bwd(dnx, conv_all, dcq)
    daq, dakv, d_qg, d_kg, d_sinks = _attn_bwd(aq, akv, rope_cos, rope_sin, q_norm_g, k_norm_g, sinks, mean_q, mean_k, d_oa)
    dps = [daq, dakv, ddnx, dba, dz, dga, dgd]
    dblk, grad_x, d_n1g, d_scale1, d_shift1 = _inproj_bwd(x, mod, norm1_g, dx1, dps, w_in_b)

    dmod = jnp.concatenate([d_shift1, d_scale1, d_gate1, d_shift2, d_scale2, d_gate2], axis=2).reshape(B, 6 * D)
    small = jnp.concatenate([d_n1g, d_qg, d_kg, d_sinks, d_alog, d_dtb, d_dng, d_n2g, d_conv.reshape(1, CONV * CONVW)], axis=1)
    nsm = small.shape[1]
    width = -(-max(6 * D, nsm) // 128) * 128
    rows = jnp.concatenate([jnp.pad(dmod, ((0, 0), (0, width - 6 * D))), jnp.pad(small, ((0, 8 - B - 1), (0, width - nsm)))], axis=0)
    rows_all = _all_gather_small(rows, "gather_small")
    dmod_all = rows_all[:, 0:B, 0:6 * D].reshape(N_DEV * B, 6 * D)
    dmod_cols = lax.dynamic_slice(dmod_all, (0, me * ncol), (N_DEV * B, ncol))
    grad_ada_w, grad_ada_b, small_sum = _ada_bwd(cond_all, dmod_all, dmod_cols, rows_all[:, B, :])
    sizes = [D, HD, HD, HQ, DH, DH, DK, D]
    so = np.cumsum([0] + sizes)
    g_n1, g_qg, g_kg, g_sk, g_al, g_dt, g_dn, g_n2 = [small_sum[:, so[i]:so[i + 1]] for i in range(8)]
    g_conv_all = small_sum[:, so[8]:so[8] + CONV * CONVW].reshape(CONV, N_DEV, CONVW // N_DEV)
    grad_conv = lax.dynamic_slice(g_conv_all, (0, me, 0), (CONV, 1, CONVW // N_DEV)).reshape(CONV, CONVW // N_DEV)

    g_w_in = _wgrad(one(h1), dblk, "wgrad_in", after=small_sum)
    proj = _exchange_start([g_w_in], me, small_sum, "exchange_in_start")
    g_w_out = _wgrad(one(merged), one(dmix), "wgrad_out", after=proj[3])
    g_w_branch = _wgrad(ob, dyo, "wgrad_branch", after=proj[3])
    mixer = _exchange_start([g_w_branch.reshape(N_DEV, D // N_DEV, D), g_w_out.reshape(N_DEV, D // N_DEV, D)], me,
                            proj[3], "exchange_mix_start")

    upd, grads = {}, {}

    def finish(names, parts, weights):
        for nm, p, (w, m, v) in zip(names, parts, weights):
            grads[nm], *upd[nm] = _sum_adamw(p, w, m, v, "update_" + nm)

    finish(["w_gate_up", "w_down"], _copies_wait(*ffn[:3], mixer[3], "exchange_ffn_wait"),
           [(w_gate_up, m_w_gate_up, v_w_gate_up), (w_down, m_w_down, v_w_down)])
    finish(["w_in"], _copies_wait(*proj[:3], grads["w_gate_up"], "exchange_in_wait"), [(w_in, m_w_in, v_w_in)])
    finish(["w_branch", "w_out"], _copies_wait(*mixer[:3], grads["w_in"], "exchange_mix_wait"),
           [(w_branch, m_w_branch, v_w_branch), (w_out, m_w_out, v_w_out)])

    grads["ada_w"] = grad_ada_w.reshape(ada_w.shape)
    upd["ada_w"] = _adamw(ada_w, grads["ada_w"], m_ada_w, v_ada_w, "adamw_ada_w")
    small_names = ["ada_b", "norm1_g", "q_norm_g", "k_norm_g", "sinks", "a_log", "dt_bias", "dn_norm_g", "norm2_g", "conv_w"]
    small_w = [ada_b, norm1_g, q_norm_g, k_norm_g, sinks, a_log, dt_bias, dn_norm_g, norm2_g, conv_w]
    small_g = [grad_ada_b, g_n1, g_qg, g_kg, g_sk, g_al, g_dt, g_dn, g_n2, grad_conv]
    small_m = [m_ada_b, m_norm1_g, m_q_norm_g, m_k_norm_g, m_sinks, m_a_log, m_dt_bias, m_dn_norm_g, m_norm2_g, m_conv_w]
    small_v = [v_ada_b, v_norm1_g, v_q_norm_g, v_k_norm_g, v_sinks, v_a_log, v_dt_bias, v_dn_norm_g, v_norm2_g, v_conv_w]
    cat = lambda arrs: jnp.concatenate([a.reshape(1, -1) for a in arrs], axis=1)
    res = _adamw(cat(small_w), cat(small_g), cat(small_m), cat(small_v), "adamw_small")
    po = np.cumsum([0] + [int(np.prod(w.shape)) for w in small_w])
    for i, nm in enumerate(small_names):
        upd[nm] = tuple(r[:, po[i]:po[i + 1]].reshape(small_w[i].shape) for r in res)
        grads[nm] = small_g[i].reshape(small_w[i].shape)

    order = ["ada_w", "ada_b", "norm1_g", "w_in", "conv_w", "q_norm_g", "k_norm_g", "sinks", "a_log", "dt_bias",
             "dn_norm_g", "w_branch", "w_out", "norm2_g", "w_gate_up", "w_down"]
    return (loss, grad_x, *[grads[n] for n in order], *[upd[n][0] for n in order],
            *[upd[n][1] for n in order], *[upd[n][2] for n in order])
```

```python
import functools

import numpy as np
import jax
import jax.numpy as jnp
from jax import lax
from jax.experimental import pallas as pl
from jax.experimental.pallas import tpu as pltpu

F32 = jnp.float32
BF16 = jnp.bfloat16
HI = lax.Precision.HIGHEST

N_DEV = 8
D = 1024
HQ, HKV, HD = 8, 2, 64
GRP = HQ // HKV
BLK = 128
ROT = HD // 4
THETA = 500000.0
QW, KVW = HQ * HD, HKV * HD
DH, DK = 4, 128
CH = 64
DNW = DH * DK
CONV = 4
CONVW = 3 * DNW
FFN = 2816
EPS = 1e-6
IN_W = QW + 2 * KVW + CONVW + 2 * DH + DNW + 2 * D

LR, B1, B2, AEPS, WD, STEP = 0.001, 0.9, 0.999, 1e-08, 0.01, 10

VMEM_LIMIT = 56 * 1024 * 1024
MESH = pl.DeviceIdType.MESH


def _cparams(sem=None, vmem=VMEM_LIMIT):
    return pltpu.CompilerParams(dimension_semantics=sem, vmem_limit_bytes=vmem)


def _full(shape):
    n = len(shape)
    return pl.BlockSpec(shape, lambda *_: (0,) * n)


def _resident(shape):
    n = len(shape)
    return pl.BlockSpec(shape, lambda *_: (0,) * n, pipeline_mode=pl.Buffered(1))


def _rows(tm, w):
    return pl.BlockSpec((None, tm, w), lambda b, i: (b, i, 0))


def _stacked(n, tm, w):
    return pl.BlockSpec((None, n, tm, w), lambda b, i: (b, 0, i, 0))


def _perb(r, w):
    return pl.BlockSpec((None, r, w), lambda b, i: (b, 0, 0))


def _dot(a, b):
    return jnp.dot(a.astype(BF16), b.astype(BF16), preferred_element_type=F32)


def _dot_nt(a, b):
    return lax.dot_general(a.astype(BF16), b.astype(BF16), (((1,), (1,)), ((), ())), preferred_element_type=F32)


def _dot_tn(a, b):
    return lax.dot_general(a.astype(BF16), b.astype(BF16), (((0,), (0,)), ((), ())), preferred_element_type=F32)


def _dot_hi(a, b):
    return jnp.dot(a, b, preferred_element_type=F32, precision=HI)


def _sigmoid(x):
    return jax.nn.sigmoid(x)


def _silu(x):
    return x * jax.nn.sigmoid(x)


def _rms_mod(x, g, scale, shift):
    r = lax.rsqrt(jnp.mean(x * x, axis=-1, keepdims=True) + EPS)
    return (x * r * g) * (1.0 + scale) + shift


def _tile(S, rows=256):
    return min(rows, S)


def _peer(x, y, c, k):
    px = 1 - x if (k >> 2) & 1 else x
    py = 1 - y if (k >> 1) & 1 else y
    pc = 1 - c if k & 1 else c
    return px, py, pc


def _all_gather_small(v, name):
    r, n = v.shape

    def body(v_ref, out_ref, send_sems, recv_sems, local_sem):
        x, y, c = lax.axis_index("x"), lax.axis_index("y"), lax.axis_index("c")
        me = 4 * x + 2 * y + c
        mine = pltpu.make_async_copy(v_ref, out_ref.at[me], local_sem)
        mine.start()
        sends = []
        for k in range(1, N_DEV):
            cp = pltpu.make_async_remote_copy(
                src_ref=v_ref, dst_ref=out_ref.at[me], send_sem=send_sems.at[k - 1], recv_sem=recv_sems.at[k - 1],
                device_id=_peer(x, y, c, k), device_id_type=MESH)
            cp.start()
            sends.append(cp)
        for k in range(1, N_DEV):
            px, py, pc = _peer(x, y, c, k)
            pltpu.make_async_remote_copy(
                src_ref=v_ref, dst_ref=out_ref.at[4 * px + 2 * py + pc], send_sem=send_sems.at[k - 1],
                recv_sem=recv_sems.at[k - 1], device_id=(px, py, pc), device_id_type=MESH).wait_recv()
        for cp in sends:
            cp.wait_send()
        mine.wait()

    return pl.pallas_call(
        body, name=name,
        out_shape=jax.ShapeDtypeStruct((N_DEV, r, n), v.dtype),
        in_specs=[pl.BlockSpec(memory_space=pltpu.VMEM)],
        out_specs=pl.BlockSpec(memory_space=pltpu.VMEM),
        scratch_shapes=[pltpu.SemaphoreType.DMA((N_DEV - 1,)), pltpu.SemaphoreType.DMA((N_DEV - 1,)), pltpu.SemaphoreType.DMA],
    )(v)


def _all_gather_big(vs, name, after=()):
    na, nf = len(vs), len(after)

    def body(*refs):
        v_refs, out_refs = refs[:na], refs[na + nf:2 * na + nf]
        send_sems, recv_sems, local_sems = refs[2 * na + nf:]
        x, y, c = lax.axis_index("x"), lax.axis_index("y"), lax.axis_index("c")
        me, sibling = (x, y, c), (x, y, 1 - c)
        chips = [(1 - x, y), (x, 1 - y), (1 - x, 1 - y)]

        def rows(a, px, py, pc):
            return out_refs[a].at[4 * px + 2 * py + pc]

        def copy(a, k, block, to, src=None):
            return pltpu.make_async_remote_copy(
                src_ref=rows(a, *block) if src is None else src, dst_ref=rows(a, *block),
                send_sem=send_sems.at[7 * a + k], recv_sem=recv_sems.at[7 * a + k], device_id=to, device_id_type=MESH)

        mine = [pltpu.make_async_copy(v_refs[a], rows(a, *me), local_sems.at[a]) for a in range(na)]
        for cp in mine:
            cp.start()
        first = []
        for a in range(na):
            first.append(copy(a, 0, me, sibling, src=v_refs[a]))
            first += [copy(a, 1 + j, me, (*chip, c), src=v_refs[a]) for j, chip in enumerate(chips)]
        for cp in first:
            cp.start()
        passed = []
        for j, chip in enumerate(chips):
            for a in range(na):
                copy(a, 1 + j, (*chip, c), me).wait_recv()
                forward = copy(a, 4 + j, (*chip, c), sibling)
                forward.start()
                passed.append(forward)
        for a in range(na):
            copy(a, 0, sibling, me).wait_recv()
            for j, chip in enumerate(chips):
                copy(a, 4 + j, (*chip, 1 - c), me).wait_recv()
        for cp in first + passed:
            cp.wait_send()
        for cp in mine:
            cp.wait()

    return pl.pallas_call(
        body, name=name,
        out_shape=[jax.ShapeDtypeStruct((N_DEV,) + v.shape, v.dtype) for v in vs],
        in_specs=[pl.BlockSpec(memory_space=pl.ANY)] * (na + nf),
        out_specs=[pl.BlockSpec(memory_space=pl.ANY)] * na,
        scratch_shapes=[pltpu.SemaphoreType.DMA((7 * na,)), pltpu.SemaphoreType.DMA((7 * na,)),
                        pltpu.SemaphoreType.DMA((na,))],
    )(*vs, *after)


_HBM = pl.BlockSpec(memory_space=pltpu.HBM)
_SEM = pl.BlockSpec(memory_space=pltpu.SEMAPHORE)
_EFFECT = pltpu.SideEffectType.DATAFLOW_SIDE_EFFECTING


def _place_own(block, me):
    land = lax.empty((N_DEV,) + block.shape, block.dtype)
    return lax.dynamic_update_slice(land, block[None], (me,) + (0,) * block.ndim)


def _copies_start(srcs, lands, scatter, after, name):
    na = len(srcs)
    afters = tuple(after) if isinstance(after, (tuple, list)) else (after,)

    def body(*refs):
        src_refs, land_refs = refs[:na], refs[na:2 * na]
        sems = refs[2 * na + len(afters):4 * na + len(afters)]
        token = refs[-1]
        x, y, c = lax.axis_index("x"), lax.axis_index("y"), lax.axis_index("c")
        me = 4 * x + 2 * y + c
        for a in range(na):
            for k in range(1, N_DEV):
                px, py, pc = _peer(x, y, c, k)
                src = src_refs[a].at[4 * px + 2 * py + pc] if scatter else src_refs[a]
                pltpu.make_async_remote_copy(
                    src_ref=src, dst_ref=land_refs[a].at[me], send_sem=sems[2 * a], recv_sem=sems[2 * a + 1],
                    device_id=(px, py, pc), device_id_type=MESH).start()
        token[...] = jnp.zeros_like(token)

    hbm = lambda t: pltpu.HBM(t.shape, t.dtype)
    out = pl.pallas_call(
        body, name=name,
        out_shape=tuple([pltpu.SemaphoreType.DMA(())] * (2 * na) + [hbm(t) for t in srcs] + [hbm(t) for t in lands]
                        + [jax.ShapeDtypeStruct((8, 128), F32)]),
        in_specs=[_HBM] * (2 * na) + [pl.BlockSpec(memory_space=pl.ANY)] * len(afters),
        out_specs=tuple([_SEM] * (2 * na) + [_HBM] * (2 * na) + [pl.BlockSpec(memory_space=pltpu.VMEM)]),
        input_output_aliases={i: 2 * na + i for i in range(2 * na)},
        compiler_params=pltpu.CompilerParams(has_side_effects=_EFFECT),
    )(*[pltpu.with_memory_space_constraint(t, pltpu.HBM) for t in list(srcs) + list(lands)], *afters)
    return out[:2 * na], out[2 * na:3 * na], out[3 * na:4 * na], out[-1]


def _exchange_start(gs, me, after, name):
    own = [lax.dynamic_index_in_dim(g, me, 0, keepdims=False) for g in gs]
    return _copies_start(gs, [_place_own(o, me) for o in own], True, after, name)


def _copies_wait(sems, srcs, lands, after, name):
    na = len(srcs)

    def body(*refs):
        land_refs = refs[na:2 * na]
        sem_refs = refs[2 * na:4 * na]
        x, y, c = lax.axis_index("x"), lax.axis_index("y"), lax.axis_index("c")
        for a in range(na):
            seven = land_refs[a].at[pl.ds(0, N_DEV - 1)]
            copy = pltpu.make_async_remote_copy(
                src_ref=seven, dst_ref=seven, send_sem=sem_refs[2 * a], recv_sem=sem_refs[2 * a + 1],
                device_id=(x, y, c), device_id_type=MESH)
            copy.wait_send()
            copy.wait_recv()

    hbm = lambda t: pltpu.HBM(t.shape, t.dtype)
    out = pl.pallas_call(
        body, name=name,
        out_shape=tuple([hbm(t) for t in srcs] + [hbm(t) for t in lands]),
        in_specs=[_HBM] * (2 * na) + [_SEM] * (2 * na) + [pl.BlockSpec(memory_space=pl.ANY)],
        out_specs=tuple([_HBM] * (2 * na)),
        input_output_aliases={i: i for i in range(2 * na)},
        compiler_params=pltpu.CompilerParams(has_side_effects=_EFFECT),
    )(*srcs, *lands, *sems, after)
    return out[na:]


def _adamw_math(w, g, m, v):
    m = B1 * m + (1.0 - B1) * g
    v = B2 * v + (1.0 - B2) * (g * g)
    m_hat = m / (1.0 - B1 ** STEP)
    v_hat = v / (1.0 - B2 ** STEP)
    return -LR * (m_hat / (jnp.sqrt(v_hat) + AEPS) + WD * w), m, v


def _sum_adamw(parts, w, m, v, name):
    _, r, n = parts.shape
    tr = 256 if r % 256 == 0 else r

    def body(p_ref, w_ref, m_ref, v_ref, g_ref, d_ref, nm_ref, nv_ref):
        g = p_ref[0].astype(F32)
        for dev in range(1, N_DEV):
            g = g + p_ref[dev].astype(F32)
        g_ref[...] = g
        d_ref[...], nm_ref[...], nv_ref[...] = _adamw_math(w_ref[...], g, m_ref[...], v_ref[...])

    rows = pl.BlockSpec((None, tr, n), lambda i: (0, i, 0))
    sd = jax.ShapeDtypeStruct((1, r, n), F32)
    return pl.pallas_call(
        body, name=name, grid=(r // tr,), out_shape=(sd, sd, sd, sd),
        in_specs=[pl.BlockSpec((N_DEV, tr, n), lambda i: (0, i, 0)), rows, rows, rows],
        out_specs=(rows, rows, rows, rows),
        compiler_params=_cparams(("parallel",)),
    )(parts, w, m, v)


def _ada_fwd(c_all, ada_w, ada_b_cols):
    nb, ncol = c_all.shape[0], ada_w.shape[1]

    def body(c_ref, w_ref, b_ref, mod_ref, cond_ref):
        cond = _silu(c_ref[...])
        cond_ref[...] = cond
        mod_ref[...] = _dot_hi(cond, w_ref[...]) + b_ref[...]

    return pl.pallas_call(
        body, name="ada_fwd",
        out_shape=(jax.ShapeDtypeStruct((nb, ncol), F32), jax.ShapeDtypeStruct((nb, D), F32)),
        compiler_params=_cparams(),
    )(c_all, ada_w, ada_b_cols)


def _ada_bwd(cond_all, dmod_all, dmod_cols, smalls):
    ncol, nsm = dmod_cols.shape[1], smalls.shape[1]

    def body(cond_ref, dm_ref, dmc_ref, sm_ref, gw_ref, gb_ref, gs_ref):
        gw_ref[...] = lax.dot_general(cond_ref[...], dmc_ref[...], (((0,), (0,)), ((), ())),
                                      preferred_element_type=F32, precision=HI)
        gb_ref[...] = jnp.sum(dm_ref[...], axis=0, keepdims=True)
        gs_ref[...] = jnp.sum(sm_ref[...], axis=0, keepdims=True)

    return pl.pallas_call(
        body, name="ada_bwd",
        out_shape=(jax.ShapeDtypeStruct((D, ncol), F32), jax.ShapeDtypeStruct((1, 6 * D), F32),
                   jax.ShapeDtypeStruct((1, nsm), F32)),
        compiler_params=_cparams(),
    )(cond_all, dmod_all, dmod_cols, smalls)


IN_CUTS = (0, QW, QW + 2 * KVW, QW + 2 * KVW + CONVW, QW + 2 * KVW + CONVW + 2 * DH,
           QW + 2 * KVW + CONVW + 2 * DH + DNW, QW + 2 * KVW + CONVW + 2 * DH + DNW + D, IN_W)
IN_WIDTHS = tuple(b - a for a, b in zip(IN_CUTS[:-1], IN_CUTS[1:]))
IN_SHARD = IN_W // N_DEV


def _inproj_fwd(x, mod, g1, w_blk):
    B, S, _ = x.shape
    tm = _tile(S)

    def body(x_ref, mod_ref, g_ref, w_ref, h_ref, *o_refs):
        h = _rms_mod(x_ref[...], g_ref[...], mod_ref[1:2, :], mod_ref[0:1, :]).astype(BF16)
        h_ref[...] = h
        full = jnp.concatenate([_dot_nt(h, w_ref[j]) for j in range(N_DEV)], axis=1)
        for o_ref, lo, hi in zip(o_refs, IN_CUTS[:-1], IN_CUTS[1:]):
            o_ref[...] = full[:, lo:hi]

    return pl.pallas_call(
        body, name="inproj_fwd", grid=(B, S // tm),
        out_shape=[jax.ShapeDtypeStruct((B, S, D), BF16)] + [jax.ShapeDtypeStruct((B, S, w), F32) for w in IN_WIDTHS],
        in_specs=[_rows(tm, D), _perb(6, D), _full((1, D)), _resident(w_blk.shape)],
        out_specs=[_rows(tm, D)] + [_rows(tm, w) for w in IN_WIDTHS],
        compiler_params=_cparams(("parallel", "arbitrary")),
    )(x, mod, g1, w_blk)


def _inproj_bwd(x, mod, g1, dx1, dps, w_blk):
    B, S, _ = x.shape
    tm = _tile(S)
    n = len(dps)

    def body(x_ref, mod_ref, g_ref, dx1_ref, *refs):
        dp_refs, w_ref = refs[:n], refs[n]
        dblk_ref, gx_ref, dg_ref, dsc_ref, dsh_ref = refs[n + 1:]
        b, i = pl.program_id(0), pl.program_id(1)
        full = jnp.concatenate([r[...].astype(F32) for r in dp_refs], axis=1)
        dh = None
        for j in range(N_DEV):
            blk = full[:, IN_SHARD * j:IN_SHARD * (j + 1)].astype(BF16)
            dblk_ref[j] = blk
            t = jnp.dot(blk, w_ref[j], preferred_element_type=F32)
            dh = t if dh is None else dh + t
        _, vjp = jax.vjp(_rms_mod, x_ref[...], g_ref[...], mod_ref[1:2, :], mod_ref[0:1, :])
        dx, dg, dsc, dsh = vjp(dh)
        gx_ref[...] = dx1_ref[...] + dx

        @pl.when((b == 0) & (i == 0))
        def _():
            dg_ref[...] = jnp.zeros_like(dg_ref)

        @pl.when(i == 0)
        def _():
            dsc_ref[...] = jnp.zeros_like(dsc_ref)
            dsh_ref[...] = jnp.zeros_like(dsh_ref)

        dg_ref[...] += dg
        dsc_ref[...] += dsc
        dsh_ref[...] += dsh

    return pl.pallas_call(
        body, name="inproj_bwd", grid=(B, S // tm),
        out_shape=[jax.ShapeDtypeStruct((B, N_DEV, S, IN_SHARD), BF16), jax.ShapeDtypeStruct((B, S, D), F32),
                   jax.ShapeDtypeStruct((1, D), F32), jax.ShapeDtypeStruct((B, 1, D), F32),
                   jax.ShapeDtypeStruct((B, 1, D), F32)],
        in_specs=[_rows(tm, D), _perb(6, D), _full((1, D)), _rows(tm, D)]
                 + [_rows(tm, w) for w in IN_WIDTHS] + [_resident(w_blk.shape)],
        out_specs=[pl.BlockSpec((None, N_DEV, tm, IN_SHARD), lambda b, i: (b, 0, i, 0)), _rows(tm, D),
                   _full((1, D)), _perb(1, D), _perb(1, D)],
        compiler_params=_cparams(("arbitrary", "arbitrary")),
    )(x, mod, g1, dx1, *dps, w_blk)


def _wgrad(a, b, name, after=None):
    B, na, S, K = a.shape
    nb, N = b.shape[1], b.shape[3]
    G = max(na, nb)
    tm = min(2048, S)
    nt = S // tm
    last = B * nt - 1

    def body(a_ref, b_ref, *rest):
        o_ref, acc = rest[-2:]
        t = pl.program_id(1)

        @pl.when(t == 0)
        def _():
            acc[...] = jnp.zeros_like(acc)

        acc[...] += lax.dot_general(a_ref[...], b_ref[...], (((0,), (0,)), ((), ())), preferred_element_type=F32)

        @pl.when(t == last)
        def _():
            o_ref[...] = acc[...].astype(BF16)

    return pl.pallas_call(
        body, name=name, grid=(G, B * nt),
        out_shape=jax.ShapeDtypeStruct((G, K, N), BF16),
        in_specs=[pl.BlockSpec((None, None, tm, K), lambda g, t: (t // nt, g if na > 1 else 0, t % nt, 0)),
                  pl.BlockSpec((None, None, tm, N), lambda g, t: (t // nt, g if nb > 1 else 0, t % nt, 0))]
                 + ([] if after is None else [pl.BlockSpec(memory_space=pl.ANY)]),
        out_specs=pl.BlockSpec((None, K, N), lambda g, t: (g, 0, 0)),
        scratch_shapes=[pltpu.VMEM((K, N), F32)],
        compiler_params=_cparams(("parallel", "arbitrary")),
    )(*((a, b) if after is None else (a, b, after)))


LANES = 128


def _attn_consts():
    inv_freq = THETA ** (-jnp.arange(0, ROT, 2, dtype=F32) / ROT)
    head = jnp.concatenate([inv_freq, inv_freq, jnp.zeros((HD - ROT,), F32)])
    invf = jnp.tile(head, LANES // HD)[None, :]
    mean_of = lambda w: jnp.asarray(np.kron(np.eye(w // HD), np.full((HD, HD), 1.0 / HD)), BF16)
    return invf, mean_of(QW), mean_of(KVW)


def _rope_tables(pos, invf):
    B, S, _ = pos.shape
    tr = min(1024, S)

    def body(p_ref, f_ref, c_ref, s_ref):
        ang = p_ref[...].astype(F32) * f_ref[...]
        c_ref[...] = jnp.cos(ang)
        s_ref[...] = jnp.sin(ang)

    sd = jax.ShapeDtypeStruct((B, S, LANES), F32)
    return pl.pallas_call(
        body, name="rope_tables", grid=(B, S // tr), out_shape=[sd, sd],
        in_specs=[_rows(tr, 1), _full((1, LANES))], out_specs=[_rows(tr, LANES), _rows(tr, LANES)],
        compiler_params=_cparams(("parallel", "parallel")),
    )(pos, invf)


def _rope_expand(cos, sin, reps):
    lane = lax.broadcasted_iota(jnp.int32, cos.shape, 1) % HD
    sa = jnp.where((lane >= ROT // 2) & (lane < ROT), sin, 0.0)
    sb = jnp.where(lane < ROT // 2, -sin, 0.0)
    rep = lambda t: jnp.concatenate([t] * reps, axis=1) if reps > 1 else t
    return rep(cos), rep(sa), rep(sb)


@jax.custom_vjp
def _rope(t, cos, sa, sb):
    w = t.shape[1]
    return t * cos + pltpu.roll(t, ROT // 2, 1) * sa + pltpu.roll(t, w - ROT // 2, 1) * sb


def _rope_fwd(t, cos, sa, sb):
    return _rope(t, cos, sa, sb), (cos, sa, sb)


def _rope_bwd(res, d):
    cos, sa, sb = res
    w = d.shape[1]
    dt = d * cos + pltpu.roll(d * sa, w - ROT // 2, 1) + pltpu.roll(d * sb, ROT // 2, 1)
    return dt, jnp.zeros_like(cos), jnp.zeros_like(sa), jnp.zeros_like(sb)


_rope.defvjp(_rope_fwd, _rope_bwd)


def _head_norm(t, g, mean_of):
    hi, lo = _split(t * t)
    ms = jnp.dot(hi, mean_of, preferred_element_type=F32) + jnp.dot(lo, mean_of, preferred_element_type=F32)
    return t * lax.rsqrt(ms + EPS) * g


def _attn_block(q, kvp, kvc, qg, kg, sinks, tq, tk, mq, mk, valid):
    qn = _rope(_head_norm(q, jnp.concatenate([qg] * HQ, axis=1), mq), *tq) * (HD ** -0.5)
    kv = jnp.concatenate([kvp, kvc], axis=0)
    kn = _rope(_head_norm(kv[:, 0:KVW], jnp.concatenate([kg] * HKV, axis=1), mk), *tk)
    per_tile = LANES // HD
    vT = jnp.transpose(kv[:, KVW:2 * KVW])
    qT = [jnp.transpose(qn[:, LANES * t:LANES * (t + 1)]) for t in range(QW // LANES)]
    head_T = lambda h: qT[h // per_tile][HD * (h % per_tile):HD * (h % per_tile + 1), :]
    none = jnp.zeros((HD, GRP * BLK), F32)
    o_T = []
    for j in range(HKV):
        q4T = jnp.concatenate([head_T(GRP * j + i) for i in range(GRP)], axis=1)
        sT = _dot(kn, jnp.concatenate([q4T, none] if j == 0 else [none, q4T], axis=0))
        sT = jnp.where(valid, sT, -1e30)
        sink = jnp.concatenate([jnp.broadcast_to(sinks[:, GRP * j + i:GRP * j + i + 1], (1, BLK)) for i in range(GRP)], axis=1)
        m = lax.stop_gradient(jnp.maximum(jnp.max(sT, axis=0, keepdims=True), sink))
        pT = jnp.exp(sT - m)
        den = jnp.sum(pT, axis=0, keepdims=True) + jnp.exp(sink - m)
        oT = _dot(vT[HD * j:HD * (j + 1), :], pT) * (1.0 / den)
        o_T += [oT[:, BLK * i:BLK * (i + 1)] for i in range(GRP)]
    return jnp.concatenate([jnp.transpose(jnp.concatenate(o_T[per_tile * t:per_tile * (t + 1)], axis=0))
                            for t in range(QW // LANES)], axis=1)


def _attn_tables(cp_ref, cc_ref, sp_ref, sc_ref, n):
    tq = _rope_expand(cc_ref[...], sc_ref[...], QW // LANES)
    tk = _rope_expand(jnp.concatenate([cp_ref[...], cc_ref[...]], axis=0),
                      jnp.concatenate([sp_ref[...], sc_ref[...]], axis=0), KVW // LANES)
    qi = lax.broadcasted_iota(jnp.int32, (2 * BLK, GRP * BLK), 1) % BLK + BLK
    kj = lax.broadcasted_iota(jnp.int32, (2 * BLK, GRP * BLK), 0)
    dist = qi - kj
    valid = (dist >= 0) & (dist < BLK) & ((kj >= BLK) | (n > 0))
    return tq, tk, valid


def _attn_fwd(aq, akv, cos, sin, qg, kg, sinks, mq, mk):
    B, S, _ = aq.shape
    nb = S // BLK

    def body(q_ref, kvp_ref, kvc_ref, cp_ref, cc_ref, sp_ref, sc_ref, qg_ref, kg_ref, sk_ref, mq_ref, mk_ref, o_ref):
        tq, tk, valid = _attn_tables(cp_ref, cc_ref, sp_ref, sc_ref, pl.program_id(1))
        o_ref[...] = _attn_block(q_ref[...], kvp_ref[...], kvc_ref[...], qg_ref[...], kg_ref[...], sk_ref[...],
                                 tq, tk, mq_ref[...], mk_ref[...], valid)

    prev = lambda b, n: (b, jnp.maximum(n - 1, 0), 0)
    cur = lambda b, n: (b, n, 0)
    return pl.pallas_call(
        body, name="attn_fwd", grid=(B, nb),
        out_shape=jax.ShapeDtypeStruct((B, S, QW), F32),
        in_specs=[pl.BlockSpec((None, BLK, QW), cur), pl.BlockSpec((None, BLK, 2 * KVW), prev),
                  pl.BlockSpec((None, BLK, 2 * KVW), cur), pl.BlockSpec((None, BLK, LANES), prev),
                  pl.BlockSpec((None, BLK, LANES), cur), pl.BlockSpec((None, BLK, LANES), prev),
                  pl.BlockSpec((None, BLK, LANES), cur), _full((1, HD)), _full((1, HD)), _full((1, HQ)),
                  _full((QW, QW)), _full((KVW, KVW))],
        out_specs=pl.BlockSpec((None, BLK, QW), cur),
        compiler_params=_cparams(("parallel", "arbitrary")),
    )(aq, akv, akv, cos, cos, sin, sin, qg, kg, sinks, mq, mk)


def _attn_bwd(aq, akv, cos, sin, qg, kg, sinks, mq, mk, do):
    B, S, _ = aq.shape
    nb = S // BLK

    def body(q_ref, kvp_ref, kvc_ref, cp_ref, cc_ref, sp_ref, sc_ref, qg_ref, kg_ref, sk_ref, mq_ref, mk_ref, do_ref,
             dq_ref, dkv_ref, dqg_ref, dkg_ref, dsk_ref, carry):
        b, i = pl.program_id(0), pl.program_id(1)
        tq, tk, valid = _attn_tables(cp_ref, cc_ref, sp_ref, sc_ref, nb - 1 - i)
        fn = functools.partial(_attn_block, tq=tq, tk=tk, mq=mq_ref[...], mk=mk_ref[...], valid=valid)
        _, vjp = jax.vjp(fn, q_ref[...], kvp_ref[...], kvc_ref[...], qg_ref[...], kg_ref[...], sk_ref[...])
        dq, dkvp, dkvc, dqg, dkg, dsk = vjp(do_ref[...])

        @pl.when(i == 0)
        def _():
            carry[...] = jnp.zeros_like(carry)

        @pl.when((b == 0) & (i == 0))
        def _():
            dqg_ref[...] = jnp.zeros_like(dqg_ref)
            dkg_ref[...] = jnp.zeros_like(dkg_ref)
            dsk_ref[...] = jnp.zeros_like(dsk_ref)

        dq_ref[...] = dq.astype(BF16)
        dkv_ref[...] = (dkvc + carry[...]).astype(BF16)
        carry[...] = dkvp
        dqg_ref[...] += dqg
        dkg_ref[...] += dkg
        dsk_ref[...] += dsk

    prev = lambda b, i: (b, jnp.maximum(nb - 2 - i, 0), 0)
    cur = lambda b, i: (b, nb - 1 - i, 0)
    return pl.pallas_call(
        body, name="attn_bwd", grid=(B, nb),
        out_shape=[jax.ShapeDtypeStruct((B, S, QW), BF16), jax.ShapeDtypeStruct((B, S, 2 * KVW), BF16),
                   jax.ShapeDtypeStruct((1, HD), F32), jax.ShapeDtypeStruct((1, HD), F32),
                   jax.ShapeDtypeStruct((1, HQ), F32)],
        in_specs=[pl.BlockSpec((None, BLK, QW), cur), pl.BlockSpec((None, BLK, 2 * KVW), prev),
                  pl.BlockSpec((None, BLK, 2 * KVW), cur), pl.BlockSpec((None, BLK, LANES), prev),
                  pl.BlockSpec((None, BLK, LANES), cur), pl.BlockSpec((None, BLK, LANES), prev),
                  pl.BlockSpec((None, BLK, LANES), cur), _full((1, HD)), _full((1, HD)), _full((1, HQ)),
                  _full((QW, QW)), _full((KVW, KVW)), pl.BlockSpec((None, BLK, QW), cur)],
        out_specs=[pl.BlockSpec((None, BLK, QW), cur), pl.BlockSpec((None, BLK, 2 * KVW), cur),
                   _full((1, HD)), _full((1, HD)), _full((1, HQ))],
        scratch_shapes=[pltpu.VMEM((BLK, 2 * KVW), F32)],
        compiler_params=_cparams(("arbitrary", "arbitrary")),
    )(aq, akv, akv, cos, cos, sin, sin, qg, kg, sinks, mq, mk, do)


def _conv_taps(xe, w, rows):
    y = None
    for j in range(CONV):
        sh = pltpu.roll(xe, CONV - 1 - j, 0)[8:8 + rows, :] if j < CONV - 1 else xe[8:8 + rows, :]
        y = sh * w[j:j + 1, :] if y is None else y + sh * w[j:j + 1, :]
    return y


def _conv_fwd(xin, w):
    B, S, C = xin.shape
    tc = min(512, S)
    r8 = tc // 8

    def body(xp_ref, x_ref, w_ref, o_ref):
        i = pl.program_id(1)
        xp = jnp.where(i > 0, xp_ref[...], 0.0)
        xe = jnp.concatenate([xp, x_ref[...]], axis=0)
        o_ref[...] = _silu(_conv_taps(xe, w_ref[...], tc))

    return pl.pallas_call(
        body, name="conv_fwd", grid=(B, S // tc),
        out_shape=jax.ShapeDtypeStruct((B, S, C), F32),
        in_specs=[pl.BlockSpec((None, 8, C), lambda b, i: (b, jnp.maximum(i * r8 - 1, 0), 0)),
                  _rows(tc, C), _full((CONV, C))],
        out_specs=_rows(tc, C),
        compiler_params=_cparams(("parallel", "arbitrary")),
    )(xin, xin, w)


def _conv_bwd(xin, w, dy):
    B, S, C = xin.shape
    tc = min(512, S)
    r8 = tc // 8
    nt = S // tc

    def body(xp_ref, x_ref, xn_ref, dy_ref, dyn_ref, w_ref, dx_ref, dw_ref):
        b, i = pl.program_id(0), pl.program_id(1)
        w = w_ref[...]
        xp = jnp.where(i > 0, xp_ref[...], 0.0)
        xe = jnp.concatenate([xp, x_ref[...], xn_ref[...]], axis=0)
        taps = [(pltpu.roll(xe, CONV - 1 - j, 0) if j < CONV - 1 else xe)[8:8 + tc + 8, :] for j in range(CONV)]
        pre = sum(t * w[j:j + 1, :] for j, t in enumerate(taps))
        sg = _sigmoid(pre)
        dyn = jnp.where(i < nt - 1, dyn_ref[...], 0.0)
        dpre = jnp.concatenate([dy_ref[...], dyn], axis=0) * (sg * (1.0 + pre * (1.0 - sg)))
        dx = dpre[0:tc, :] * w[CONV - 1:CONV, :]
        for j in range(CONV - 1):
            dx = dx + pltpu.roll(dpre, tc + 8 - (CONV - 1 - j), 0)[0:tc, :] * w[j:j + 1, :]
        dx_ref[...] = dx.astype(BF16)
        dcur = dpre[0:tc, :]
        lane_row = lax.broadcasted_iota(jnp.int32, (CONV, C), 0)
        dw = jnp.zeros((CONV, C), F32)
        for j in range(CONV):
            dw = dw + jnp.where(lane_row == j, jnp.sum(taps[j][0:tc, :] * dcur, axis=0, keepdims=True), 0.0)

        @pl.when((b == 0) & (i == 0))
        def _():
            dw_ref[...] = jnp.zeros_like(dw_ref)

        dw_ref[...] += dw

    return pl.pallas_call(
        body, name="conv_bwd", grid=(B, nt),
        out_shape=[jax.ShapeDtypeStruct((B, S, C), BF16), jax.ShapeDtypeStruct((CONV, C), F32)],
        in_specs=[pl.BlockSpec((None, 8, C), lambda b, i: (b, jnp.maximum(i * r8 - 1, 0), 0)),
                  _rows(tc, C),
                  pl.BlockSpec((None, 8, C), lambda b, i: (b, jnp.minimum((i + 1) * r8, S // 8 - 1), 0)),
                  _rows(tc, C),
                  pl.BlockSpec((None, 8, C), lambda b, i: (b, jnp.minimum((i + 1) * r8, S // 8 - 1), 0)),
                  _full((CONV, C))],
        out_specs=[_rows(tc, C), _full((CONV, C))],
        compiler_params=_cparams(("arbitrary", "arbitrary")),
    )(xin, xin, xin, dy, dy, w)


def _softplus(x):
    return jnp.maximum(x, 0.0) + jnp.log1p(jnp.exp(-jnp.abs(x)))


_BMM = (((2,), (1,)), ((0,), (0,)))
_BMM_NT = (((2,), (2,)), ((0,), (0,)))
_BMM_TN = (((1,), (1,)), ((0,), (0,)))


def _bmm(a, b, dims=_BMM):
    return lax.dot_general(a.astype(BF16), b.astype(BF16), dims, preferred_element_type=F32)


def _split(a):
    hi = a.astype(BF16)
    return hi, (a - hi.astype(F32)).astype(BF16)


def _bmm3(a, b, dims=_BMM):
    ah, al = _split(a)
    bh, bl = _split(b)
    d = lambda p, q: lax.dot_general(p, q, dims, preferred_element_type=F32)
    return d(ah, bh) + (d(ah, bl) + d(al, bh))


TRI_BASE = 8


def _tri_inverse(L):
    ii = lax.broadcasted_iota(jnp.int32, (CH, CH), 0)
    jj = lax.broadcasted_iota(jnp.int32, (CH, CH), 1)
    same = lambda size: (ii // size) == (jj // size)
    diag = jnp.where(same(TRI_BASE), L, 0.0)
    X = (ii == jj).astype(F32) - diag
    P = diag
    n = 2
    while n < TRI_BASE:
        P = _bmm3(P, P)
        X = X + _bmm3(X, P)
        n *= 2
    size = TRI_BASE
    while size < CH:
        joint = jnp.where(same(2 * size) & jnp.logical_not(same(size)), L, 0.0)
        X = X - _bmm3(X, _bmm3(joint, X))
        size *= 2
    return X


@jax.custom_vjp
def _tri_inverse_known(L, T):
    return T


def _tri_inverse_known_fwd(L, T):
    return T, T


def _tri_inverse_known_bwd(T, dT):
    return -_bmm3(T, _bmm3(dT, T, _BMM_NT), _BMM_TN), jnp.zeros_like(T)


_tri_inverse_known.defvjp(_tri_inverse_known_fwd, _tri_inverse_known_bwd)


def _cumsum_rows(g):
    n = g.shape[0]
    ii = lax.broadcasted_iota(jnp.int32, (n, CH, CH), 1)
    jj = lax.broadcasted_iota(jnp.int32, (n, CH, CH), 2)
    tri = (ii >= jj).astype(BF16)
    g0 = g.astype(BF16)
    r1 = g - g0.astype(F32)
    g1 = r1.astype(BF16)
    g2 = (r1 - g1.astype(F32)).astype(BF16)
    d = lambda q: lax.dot_general(tri, q, _BMM, preferred_element_type=F32)
    return d(g0) + (d(g1) + d(g2))


def _row_sums(t):
    n, r, w = t.shape
    hi, lo = _split(t.reshape(n * r, w))
    ones = jnp.ones((w, w), BF16)
    s = jnp.dot(hi, ones, preferred_element_type=F32) + jnp.dot(lo, ones, preferred_element_type=F32)
    return s.reshape(n, r, w)


def _dn_prep(t_known, qr, kr, v, a_raw, b_raw, a_log, dt_b):
    n = qr.shape[0]
    ii = lax.broadcasted_iota(jnp.int32, (n, CH, CH), 1)
    jj = lax.broadcasted_iota(jnp.int32, (n, CH, CH), 2)
    incl, strict = ii >= jj, ii > jj
    q = qr * lax.rsqrt(_row_sums(qr * qr) + EPS) * (DK ** -0.5)
    k = kr * lax.rsqrt(_row_sums(kr * kr) + EPS)
    beta = _sigmoid(b_raw)
    g = -jnp.exp(a_log) * _softplus(a_raw + dt_b)
    gcb = _cumsum_rows(jnp.broadcast_to(g, (n, CH, DK)))
    gc = gcb[:, :, 0:1]
    gc_row = jnp.swapaxes(gcb, 1, 2)[:, 0:1, 0:CH]
    decay = jnp.where(incl, jnp.exp(jnp.where(incl, gc - gc_row, 0.0)), 0.0)
    kb = k * beta
    L = jnp.where(strict, _bmm(kb, k, _BMM_NT) * decay, 0.0)
    T = _tri_inverse(L) if t_known is None else _tri_inverse_known(L, t_known)
    eg = jnp.exp(gc)
    u = _bmm(T, v * beta)
    w = _bmm(T, kb * eg)
    a_in = _bmm(q, k, _BMM_NT) * decay
    g_last = gc[:, CH - 1:CH, :]
    return u, w, q * eg, k * jnp.exp(g_last - gc), a_in, jnp.exp(g_last), T


def _dn_step(S0, u, w, qd, kd, a_in, cd):
    r = _bmm(jnp.concatenate([w, qd], axis=1), S0)
    v_new = u - r[:, 0:CH, :]
    o = r[:, CH:2 * CH, :] + _bmm(a_in, v_new)
    S1 = S0 * cd + _bmm(kd, v_new, _BMM_TN)
    return o, S1


def _dn_stack(cq, ba, al, dt, G):
    cols = [[] for _ in range(7)]
    for c in range(G):
        rows = slice(CH * c, CH * (c + 1))
        for h in range(DH):
            parts = (cq[rows, DK * h:DK * (h + 1)], cq[rows, DNW + DK * h:DNW + DK * (h + 1)],
                     cq[rows, 2 * DNW + DK * h:2 * DNW + DK * (h + 1)], ba[rows, DH + h:DH + h + 1],
                     ba[rows, h:h + 1], al[:, h:h + 1], dt[:, h:h + 1])
            for col, p in zip(cols, parts):
                col.append(p)
    return tuple(jnp.stack(col) for col in cols)


def _dn_group(S, want):
    g = want
    while (S // CH) % g:
        g //= 2
    return g


def _dn_prep_fwd(cq, ba, a_log, dt_b):
    B, S, _ = cq.shape
    nc = S // CH
    G = _dn_group(S, 4)

    def body(cq_ref, ba_ref, al_ref, dt_ref, u_ref, w_ref, qd_ref, kd_ref, a_ref, t_ref, cd_ref):
        ops = _dn_stack(cq_ref[...], ba_ref[...], al_ref[...], dt_ref[...], G)
        u, w, qd, kd, a_in, cd, T = _dn_prep(None, *ops)
        lane4 = lax.broadcasted_iota(jnp.int32, (1, DH), 1)
        for c in range(G):
            rows = slice(CH * c, CH * (c + 1))
            cdrow = jnp.zeros((1, DH), F32)
            for h in range(DH):
                n = DH * c + h
                lanes = slice(DK * h, DK * (h + 1))
                u_ref[rows, lanes] = u[n]
                w_ref[rows, lanes] = w[n]
                qd_ref[rows, lanes] = qd[n]
                kd_ref[rows, lanes] = kd[n]
                a_ref[rows, CH * h:CH * (h + 1)] = a_in[n]
                t_ref[rows, CH * h:CH * (h + 1)] = T[n]
                cdrow = cdrow + jnp.where(lane4 == h, cd[n], 0.0)
            cd_ref[c] = cdrow

    wide = jax.ShapeDtypeStruct((B, S, DNW), F32)
    sq = jax.ShapeDtypeStruct((B, S, DH * CH), F32)
    return pl.pallas_call(
        body, name="dn_prep_fwd", grid=(B, nc // G),
        out_shape=[wide, wide, wide, wide, sq, sq, jax.ShapeDtypeStruct((B, nc, 1, DH), F32)],
        in_specs=[_rows(G * CH, CONVW), _rows(G * CH, 2 * DH), _full((1, DH)), _full((1, DH))],
        out_specs=[_rows(G * CH, DNW)] * 4 + [_rows(G * CH, DH * CH)] * 2
                  + [pl.BlockSpec((None, G, 1, DH), lambda b, i: (b, i, 0, 0))],
        compiler_params=_cparams(("parallel", "parallel")),
    )(cq, ba, a_log, dt_b)


def _dn_seq_specs(B, steps, gs, rev):
    at = (lambda i: steps - 1 - i) if rev else (lambda i: i)
    wide = pl.BlockSpec((B, gs * CH, DNW), lambda i: (0, at(i), 0))
    a_spec = pl.BlockSpec((B, gs * CH, DH * CH), lambda i: (0, at(i), 0))
    cd_spec = pl.BlockSpec((B, gs, 1, DH), lambda i: (0, at(i), 0, 0))
    st_spec = pl.BlockSpec((B, gs, DH, DK, DK), lambda i: (0, at(i), 0, 0, 0))
    return wide, a_spec, cd_spec, st_spec


def _dn_step_operands(B, c, u_ref, w_ref, qd_ref, kd_ref, a_ref, cd_ref):
    pairs = [(b, h) for b in range(B) for h in range(DH)]
    rows = slice(CH * c, CH * (c + 1))
    wide = lambda ref: jnp.stack([ref[b, rows, DK * h:DK * (h + 1)] for b, h in pairs])
    a_in = jnp.stack([a_ref[b, rows, CH * h:CH * (h + 1)] for b, h in pairs])
    cd = jnp.stack([cd_ref[b, c, :, h:h + 1] for b, h in pairs])
    return wide(u_ref), wide(w_ref), wide(qd_ref), wide(kd_ref), a_in, cd


def _dn_seq_fwd(u, w, qd, kd, a_in, cd):
    B, S, _ = u.shape
    nc = S // CH
    gs = _dn_group(S, 8)

    def body(u_ref, w_ref, qd_ref, kd_ref, a_ref, cd_ref, o_ref, st_ref, state):
        @pl.when(pl.program_id(0) == 0)
        def _():
            state[...] = jnp.zeros_like(state)

        S0 = state[...]
        for c in range(gs):
            for b in range(B):
                st_ref[b, c] = S0[DH * b:DH * (b + 1)]
            o, S0 = _dn_step(S0, *_dn_step_operands(B, c, u_ref, w_ref, qd_ref, kd_ref, a_ref, cd_ref))
            for b in range(B):
                for h in range(DH):
                    o_ref[b, CH * c:CH * (c + 1), DK * h:DK * (h + 1)] = o[DH * b + h]
        state[...] = S0

    wide, a_spec, cd_spec, st_spec = _dn_seq_specs(B, nc // gs, gs, False)
    return pl.pallas_call(
        body, name="dn_seq_fwd", grid=(nc // gs,),
        out_shape=[jax.ShapeDtypeStruct((B, S, DNW), F32), jax.ShapeDtypeStruct((B, nc, DH, DK, DK), F32)],
        in_specs=[wide, wide, wide, wide, a_spec, cd_spec],
        out_specs=[wide, st_spec],
        scratch_shapes=[pltpu.VMEM((B * DH, DK, DK), F32)],
        compiler_params=_cparams(("arbitrary",)),
    )(u, w, qd, kd, a_in, cd)


def _dn_seq_bwd(u, w, qd, kd, a_in, cd, states, do):
    B, S, _ = u.shape
    nc = S // CH
    gs = _dn_group(S, 8)

    def body(u_ref, w_ref, qd_ref, kd_ref, a_ref, cd_ref, st_ref, do_ref,
             du_ref, dw_ref, dqd_ref, dkd_ref, da_ref, dcd_ref, dstate):
        @pl.when(pl.program_id(0) == 0)
        def _():
            dstate[...] = jnp.zeros_like(dstate)

        lane4 = lax.broadcasted_iota(jnp.int32, (1, DH), 1)
        dS = dstate[...]
        for c in reversed(range(gs)):
            rows = slice(CH * c, CH * (c + 1))
            S0 = jnp.concatenate([st_ref[b, c] for b in range(B)], axis=0)
            do = jnp.stack([do_ref[b, rows, DK * h:DK * (h + 1)] for b in range(B) for h in range(DH)])
            _, vjp = jax.vjp(_dn_step, S0, *_dn_step_operands(B, c, u_ref, w_ref, qd_ref, kd_ref, a_ref, cd_ref))
            dS, du, dw, dqd, dkd, da, dcd = vjp((do, dS))
            for b in range(B):
                dcdrow = jnp.zeros((1, DH), F32)
                for h in range(DH):
                    n = DH * b + h
                    lanes = slice(DK * h, DK * (h + 1))
                    du_ref[b, rows, lanes] = du[n]
                    dw_ref[b, rows, lanes] = dw[n]
                    dqd_ref[b, rows, lanes] = dqd[n]
                    dkd_ref[b, rows, lanes] = dkd[n]
                    da_ref[b, rows, CH * h:CH * (h + 1)] = da[n]
                    dcdrow = dcdrow + jnp.where(lane4 == h, dcd[n], 0.0)
                dcd_ref[b, c] = dcdrow
        dstate[...] = dS

    wide, a_spec, cd_spec, st_spec = _dn_seq_specs(B, nc // gs, gs, True)
    sd = jax.ShapeDtypeStruct((B, S, DNW), F32)
    return pl.pallas_call(
        body, name="dn_seq_bwd", grid=(nc // gs,),
        out_shape=[sd, sd, sd, sd, jax.ShapeDtypeStruct((B, S, DH * CH), F32), jax.ShapeDtypeStruct((B, nc, 1, DH), F32)],
        in_specs=[wide, wide, wide, wide, a_spec, cd_spec, st_spec, wide],
        out_specs=[wide, wide, wide, wide, a_spec, cd_spec],
        scratch_shapes=[pltpu.VMEM((B * DH, DK, DK), F32)],
        compiler_params=_cparams(("arbitrary",)),
    )(u, w, qd, kd, a_in, cd, states, do)


def _dn_prep_bwd(cq, ba, a_log, dt_b, t_inv, du, dw, dqd, dkd, da, dcd):
    B, S, _ = cq.shape
    nc = S // CH
    G = _dn_group(S, 4)

    def body(cq_ref, ba_ref, al_ref, dt_ref, t_ref, du_ref, dw_ref, dqd_ref, dkd_ref, da_ref, dcd_ref,
             dcq_ref, dba_ref, dal_ref, ddt_ref):
        @pl.when((pl.program_id(0) == 0) & (pl.program_id(1) == 0))
        def _():
            dal_ref[...] = jnp.zeros_like(dal_ref)
            ddt_ref[...] = jnp.zeros_like(ddt_ref)

        pairs = [(c, h) for c in range(G) for h in range(DH)]
        rows = lambda c: slice(CH * c, CH * (c + 1))
        wide = lambda ref: jnp.stack([ref[rows(c), DK * h:DK * (h + 1)] for c, h in pairs])
        square = lambda ref: jnp.stack([ref[rows(c), CH * h:CH * (h + 1)] for c, h in pairs])
        ops = _dn_stack(cq_ref[...], ba_ref[...], al_ref[...], dt_ref[...], G)
        cots = (wide(du_ref), wide(dw_ref), wide(dqd_ref), wide(dkd_ref), square(da_ref),
                jnp.stack([dcd_ref[c][:, h:h + 1] for c, h in pairs]), jnp.zeros((len(pairs), CH, CH), F32))
        _, vjp = jax.vjp(functools.partial(_dn_prep, square(t_ref)), *ops)
        dq, dk, dv, dar, dbr, dl, dd = vjp(cots)
        lane8 = lax.broadcasted_iota(jnp.int32, (CH, 2 * DH), 1)
        lane4 = lax.broadcasted_iota(jnp.int32, (1, DH), 1)
        dal = jnp.zeros((1, DH), F32)
        ddt = jnp.zeros((1, DH), F32)
        for c in range(G):
            dba = jnp.zeros((CH, 2 * DH), F32)
            for h in range(DH):
                n = DH * c + h
                dcq_ref[rows(c), DK * h:DK * (h + 1)] = dq[n]
                dcq_ref[rows(c), DNW + DK * h:DNW + DK * (h + 1)] = dk[n]
                dcq_ref[rows(c), 2 * DNW + DK * h:2 * DNW + DK * (h + 1)] = dv[n]
                dba = dba + jnp.where(lane8 == h, dbr[n], 0.0) + jnp.where(lane8 == DH + h, dar[n], 0.0)
                dal = dal + jnp.where(lane4 == h, dl[n], 0.0)
                ddt = ddt + jnp.where(lane4 == h, dd[n], 0.0)
            dba_ref[rows(c), :] = dba.astype(BF16)
        dal_ref[...] += dal
        ddt_ref[...] += ddt

    return pl.pallas_call(
        body, name="dn_prep_bwd", grid=(B, nc // G),
        out_shape=[jax.ShapeDtypeStruct((B, S, CONVW), F32), jax.ShapeDtypeStruct((B, S, 2 * DH), BF16),
                   jax.ShapeDtypeStruct((1, DH), F32), jax.ShapeDtypeStruct((1, DH), F32)],
        in_specs=[_rows(G * CH, CONVW), _rows(G * CH, 2 * DH), _full((1, DH)), _full((1, DH)), _rows(G * CH, DH * CH)]
                 + [_rows(G * CH, DNW)] * 4 + [_rows(G * CH, DH * CH),
                                               pl.BlockSpec((None, G, 1, DH), lambda b, i: (b, i, 0, 0))],
        out_specs=[_rows(G * CH, CONVW), _rows(G * CH, 2 * DH), _full((1, DH)), _full((1, DH))],
        compiler_params=_cparams(("arbitrary", "arbitrary")),
    )(cq, ba, a_log, dt_b, t_inv, du, dw, dqd, dkd, da, dcd)


def _gated_norm(o, z, g):
    outs = []
    for h in range(DH):
        t = o[:, DK * h:DK * (h + 1)]
        r = lax.rsqrt(jnp.mean(t * t, axis=-1, keepdims=True) + EPS)
        outs.append(t * r * g * _silu(z[:, DK * h:DK * (h + 1)]))
    return jnp.concatenate(outs, axis=1)


def _mix_fwd(x, o_attn, o_dn, z, ga, gd, mod, dn_g, w_branch, w_out):
    B, S, _ = x.shape
    tm = _tile(S, 512)

    def body(x_ref, oa_ref, od_ref, z_ref, ga_ref, gd_ref, mod_ref, g_ref, wb_ref, wo_ref,
             x1_ref, mix_ref, mg_ref, ob_ref):
        oa = oa_ref[...].astype(BF16)
        od = _gated_norm(od_ref[...], z_ref[...], g_ref[...]).astype(BF16)
        ob_ref[0] = oa
        ob_ref[1] = od
        ya = jnp.dot(oa, wb_ref[0:QW, :], preferred_element_type=F32)
        yd = jnp.dot(od, wb_ref[QW:QW + DNW, :], preferred_element_type=F32)
        merged = (_sigmoid(ga_ref[...]) * ya + _sigmoid(gd_ref[...]) * yd).astype(BF16)
        mg_ref[...] = merged
        mix = jnp.dot(merged, wo_ref[...], preferred_element_type=F32)
        mix_ref[...] = mix
        x1_ref[...] = x_ref[...] + mod_ref[2:3, :] * mix

    return pl.pallas_call(
        body, name="mix_fwd", grid=(B, S // tm),
        out_shape=[jax.ShapeDtypeStruct((B, S, D), F32), jax.ShapeDtypeStruct((B, S, D), F32),
                   jax.ShapeDtypeStruct((B, S, D), BF16), jax.ShapeDtypeStruct((B, 2, S, QW), BF16)],
        in_specs=[_rows(tm, D), _rows(tm, QW), _rows(tm, DNW), _rows(tm, DNW), _rows(tm, D), _rows(tm, D),
                  _perb(6, D), _full((1, DK)), _resident(w_branch.shape), _resident(w_out.shape)],
        out_specs=[_rows(tm, D), _rows(tm, D), _rows(tm, D), _stacked(2, tm, QW)],
        compiler_params=_cparams(("parallel", "arbitrary")),
    )(x, o_attn, o_dn, z, ga, gd, mod, dn_g, w_branch, w_out)


def _mix_bwd(dx1, mix, o_attn, o_dn, z, ga, gd, mod, dn_g, w_branch, w_out):
    B, S, _ = dx1.shape
    tm = _tile(S)

    def body(dx1_ref, mix_ref, oa_ref, od_ref, z_ref, ga_ref, gd_ref, mod_ref, g_ref, wb_ref, wo_ref,
             dmix_ref, dyo_ref, dga_ref, dgd_ref, dz_ref, doa_ref, dod_ref, dgate_ref, dg_ref):
        b, i = pl.program_id(0), pl.program_id(1)
        dx1 = dx1_ref[...]
        dmix = (dx1 * mod_ref[2:3, :]).astype(BF16)
        dmix_ref[...] = dmix
        dgate = jnp.sum(dx1 * mix_ref[...], axis=0, keepdims=True)
        dmerged = _dot_nt(dmix, wo_ref[...])
        odn, gn_vjp = jax.vjp(_gated_norm, od_ref[...], z_ref[...], g_ref[...])
        ya = _dot(oa_ref[...], wb_ref[0:QW, :])
        yd = _dot(odn, wb_ref[QW:QW + DNW, :])
        sa, sd = _sigmoid(ga_ref[...]), _sigmoid(gd_ref[...])
        dya = (dmerged * sa).astype(BF16)
        dyd = (dmerged * sd).astype(BF16)
        dyo_ref[0] = dya
        dyo_ref[1] = dyd
        dga_ref[...] = (dmerged * ya * sa * (1.0 - sa)).astype(BF16)
        dgd_ref[...] = (dmerged * yd * sd * (1.0 - sd)).astype(BF16)
        doa_ref[...] = _dot_nt(dya, wb_ref[0:QW, :])
        dodn = _dot_nt(dyd, wb_ref[QW:QW + DNW, :])
        dod, dz, dg = gn_vjp(dodn)
        dod_ref[...] = dod
        dz_ref[...] = dz.astype(BF16)

        @pl.when(i == 0)
        def _():
            dgate_ref[...] = jnp.zeros_like(dgate_ref)

        @pl.when((b == 0) & (i == 0))
        def _():
            dg_ref[...] = jnp.zeros_like(dg_ref)

        dgate_ref[...] += dgate
        dg_ref[...] += dg

    return pl.pallas_call(
        body, name="mix_bwd", grid=(B, S // tm),
        out_shape=[jax.ShapeDtypeStruct((B, S, D), BF16), jax.ShapeDtypeStruct((B, 2, S, D), BF16),
                   jax.ShapeDtypeStruct((B, S, D), BF16), jax.ShapeDtypeStruct((B, S, D), BF16),
                   jax.ShapeDtypeStruct((B, S, DNW), BF16),
                   jax.ShapeDtypeStruct((B, S, QW), F32), jax.ShapeDtypeStruct((B, S, DNW), F32),
                   jax.ShapeDtypeStruct((B, 1, D), F32), jax.ShapeDtypeStruct((1, DK), F32)],
        in_specs=[_rows(tm, D), _rows(tm, D), _rows(tm, QW), _rows(tm, DNW), _rows(tm, DNW), _rows(tm, D),
                  _rows(tm, D), _perb(6, D), _full((1, DK)), _resident(w_branch.shape), _resident(w_out.shape)],
        out_specs=[_rows(tm, D), _stacked(2, tm, D), _rows(tm, D), _rows(tm, D), _rows(tm, DNW),
                   _rows(tm, QW), _rows(tm, DNW), _perb(1, D), _full((1, DK))],
        compiler_params=_cparams(("arbitrary", "arbitrary")),
    )(dx1, mix, o_attn, o_dn, z, ga, gd, mod, dn_g, w_branch, w_out)


GU_SHARD = 2 * FFN // N_DEV
GU_HALF = N_DEV // 2


def _ffn1_fwd(x1, mod, g2, w_gu):
    B, S, _ = x1.shape
    tm = _tile(S)

    def body(x_ref, mod_ref, g_ref, w_ref, h_ref, gate_ref, up_ref, act_ref):
        h = _rms_mod(x_ref[...], g_ref[...], mod_ref[4:5, :], mod_ref[3:4, :]).astype(BF16)
        h_ref[...] = h
        for j in range(GU_HALF):
            gate = _dot_nt(h, w_ref[j])
            up = _dot_nt(h, w_ref[GU_HALF + j])
            gate_ref[j] = gate
            up_ref[j] = up
            act_ref[j] = (_silu(gate) * up).astype(BF16)

    blk = lambda dt: jax.ShapeDtypeStruct((B, GU_HALF, S, GU_SHARD), dt)
    return pl.pallas_call(
        body, name="ffn1_fwd", grid=(B, S // tm),
        out_shape=[jax.ShapeDtypeStruct((B, S, D), BF16), blk(F32), blk(F32), blk(BF16)],
        in_specs=[_rows(tm, D), _perb(6, D), _full((1, D)), _resident(w_gu.shape)],
        out_specs=[_rows(tm, D)] + [_stacked(GU_HALF, tm, GU_SHARD)] * 3,
        compiler_params=_cparams(("parallel", "arbitrary")),
    )(x1, mod, g2, w_gu)


def _ffn2_fwd(act, x1, target, mod, w_down):
    B, S, _ = x1.shape
    tm = _tile(S, 512)

    def body(a_ref, x_ref, t_ref, mod_ref, w_ref, dy_ref, loss_ref, dgate_ref):
        b, i = pl.program_id(0), pl.program_id(1)
        y = jnp.dot(a_ref[0], w_ref[0], preferred_element_type=F32)
        for j in range(1, GU_HALF):
            y = y + jnp.dot(a_ref[j], w_ref[j], preferred_element_type=F32)
        err = x_ref[...] + mod_ref[5:6, :] * y - t_ref[...]
        dy = err * (1.0 / D)
        dy_ref[...] = dy

        @pl.when((b == 0) & (i == 0))
        def _():
            loss_ref[...] = jnp.zeros_like(loss_ref)

        @pl.when(i == 0)
        def _():
            dgate_ref[...] = jnp.zeros_like(dgate_ref)

        loss_ref[...] += (0.5 / D) * jnp.sum(err * err)
        dgate_ref[...] += jnp.sum(dy * y, axis=0, keepdims=True)

    return pl.pallas_call(
        body, name="ffn2_fwd", grid=(B, S // tm),
        out_shape=[jax.ShapeDtypeStruct((B, S, D), F32), jax.ShapeDtypeStruct((1, 128), F32),
                   jax.ShapeDtypeStruct((B, 1, D), F32)],
        in_specs=[_stacked(GU_HALF, tm, GU_SHARD), _rows(tm, D), _rows(tm, D), _perb(6, D), _resident(w_down.shape)],
        out_specs=[_rows(tm, D), _full((1, 128)), _perb(1, D)],
        compiler_params=_cparams(("arbitrary", "arbitrary")),
    )(act, x1, target, mod, w_down)


def _ffn2_bwd(dy, gate, up, mod, w_down):
    B, S, _ = dy.shape
    tm = _tile(S)

    def body(dy_ref, gate_ref, up_ref, mod_ref, w_ref, dgu_ref, dyg_ref):
        dyg = (dy_ref[...] * mod_ref[5:6, :]).astype(BF16)
        dyg_ref[...] = dyg
        for j in range(GU_HALF):
            dact = _dot_nt(dyg, w_ref[j])
            gate, up = gate_ref[j], up_ref[j]
            sg = _sigmoid(gate)
            dgu_ref[j] = (dact * up * (sg * (1.0 + gate * (1.0 - sg)))).astype(BF16)
            dgu_ref[GU_HALF + j] = (dact * (gate * sg)).astype(BF16)

    return pl.pallas_call(
        body, name="ffn2_bwd", grid=(B, S // tm),
        out_shape=[jax.ShapeDtypeStruct((B, N_DEV, S, GU_SHARD), BF16), jax.ShapeDtypeStruct((B, S, D), BF16)],
        in_specs=[_rows(tm, D), _stacked(GU_HALF, tm, GU_SHARD), _stacked(GU_HALF, tm, GU_SHARD), _perb(6, D),
                  _resident(w_down.shape)],
        out_specs=[_stacked(N_DEV, tm, GU_SHARD), _rows(tm, D)],
        compiler_params=_cparams(("parallel", "arbitrary")),
    )(dy, gate, up, mod, w_down)


def _ffn1_bwd(dgu, x1, dy, mod, g2, w_gu):
    B, S, _ = x1.shape
    tm = _tile(S, 512)

    def body(dgu_ref, x_ref, dy_ref, mod_ref, g_ref, w_ref, dx1_ref, dg_ref, dsc_ref, dsh_ref):
        b, i = pl.program_id(0), pl.program_id(1)
        dh = jnp.dot(dgu_ref[0], w_ref[0], preferred_element_type=F32)
        for j in range(1, N_DEV):
            dh = dh + jnp.dot(dgu_ref[j], w_ref[j], preferred_element_type=F32)
        _, vjp = jax.vjp(_rms_mod, x_ref[...], g_ref[...], mod_ref[4:5, :], mod_ref[3:4, :])
        dx, dg, dsc, dsh = vjp(dh)
        dx1_ref[...] = dy_ref[...] + dx

        @pl.when((b == 0) & (i == 0))
        def _():
            dg_ref[...] = jnp.zeros_like(dg_ref)

        @pl.when(i == 0)
        def _():
            dsc_ref[...] = jnp.zeros_like(dsc_ref)
            dsh_ref[...] = jnp.zeros_like(dsh_ref)

        dg_ref[...] += dg
        dsc_ref[...] += dsc
        dsh_ref[...] += dsh

    return pl.pallas_call(
        body, name="ffn1_bwd", grid=(B, S // tm),
        out_shape=[jax.ShapeDtypeStruct((B, S, D), F32), jax.ShapeDtypeStruct((1, D), F32),
                   jax.ShapeDtypeStruct((B, 1, D), F32), jax.ShapeDtypeStruct((B, 1, D), F32)],
        in_specs=[_stacked(N_DEV, tm, GU_SHARD), _rows(tm, D), _rows(tm, D), _perb(6, D), _full((1, D)),
                  _resident(w_gu.shape)],
        out_specs=[_rows(tm, D), _full((1, D)), _perb(1, D), _perb(1, D)],
        compiler_params=_cparams(("arbitrary", "arbitrary")),
    )(dgu, x1, dy, mod, g2, w_gu)


def _adamw(w, g, m, v, name):
    def body(w_ref, g_ref, m_ref, v_ref, d_ref, nm_ref, nv_ref):
        d_ref[...], nm_ref[...], nv_ref[...] = _adamw_math(w_ref[...], g_ref[...], m_ref[...], v_ref[...])

    sd = jax.ShapeDtypeStruct(w.shape, F32)
    return pl.pallas_call(body, name=name, out_shape=(sd, sd, sd), compiler_params=_cparams())(w, g, m, v)


def kernel(x, c, positions, ada_w, ada_b, norm1_g, w_in, conv_w, q_norm_g, k_norm_g, sinks, a_log, dt_bias, dn_norm_g, w_branch, w_out, norm2_g, w_gate_up, w_down, loss_target, m_ada_w, m_ada_b, m_norm1_g, m_w_in, m_conv_w, m_q_norm_g, m_k_norm_g, m_sinks, m_a_log, m_dt_bias, m_dn_norm_g, m_w_branch, m_w_out, m_norm2_g, m_w_gate_up, m_w_down, v_ada_w, v_ada_b, v_norm1_g, v_w_in, v_conv_w, v_q_norm_g, v_k_norm_g, v_sinks, v_a_log, v_dt_bias, v_dn_norm_g, v_w_branch, v_w_out, v_norm2_g, v_w_gate_up, v_w_down):
    B, S, _ = x.shape
    me = 4 * lax.axis_index("x") + 2 * lax.axis_index("y") + lax.axis_index("c")

    tr = lambda t: jnp.swapaxes(t, 1, 2)
    shards = [w[0].astype(BF16) for w in (tr(w_in), w_branch, w_out, tr(w_gate_up), w_down)]

    c_all = _all_gather_small(c, "gather_c").reshape(N_DEV * B, D)
    ncol = 6 * D // N_DEV
    mod_cols, cond_all = _ada_fwd(c_all, ada_w[0], lax.dynamic_slice(ada_b, (0, me * ncol), (1, ncol)))
    mod_all = _all_gather_small(mod_cols, "gather_mod").transpose(1, 0, 2).reshape(N_DEV * B, 6 * D)
    mod = lax.dynamic_slice(mod_all, (me * B, 0), (B, 6 * D)).reshape(B, 6, D)
    conv2 = conv_w.reshape(CONV, CONVW // N_DEV)
    conv_all = _all_gather_small(conv2, "gather_conv").transpose(1, 0, 2).reshape(CONV, CONVW)

    (w_in_b,) = _all_gather_big(shards[:1], "gather_w_in", after=(mod, conv_all))
    w_sems, w_srcs, w_lands, w_token = _copies_start(shards[1:], [_place_own(s, me) for s in shards[1:]], False,
                                                    w_in_b, "gather_rest_start")

    h1, aq, akv, dnx, ba, z, ga, gd = _inproj_fwd(x, mod, norm1_g + w_token[0, 0], w_in_b)
    invf, mean_q, mean_k = _attn_consts()
    rope_cos, rope_sin = _rope_tables(positions.reshape(B, S, 1), invf)
    o_attn = _attn_fwd(aq, akv, rope_cos, rope_sin, q_norm_g, k_norm_g, sinks, mean_q, mean_k)
    cq = _conv_fwd(dnx, conv_all)
    dn_u, dn_w, dn_qd, dn_kd, dn_a, dn_t, dn_cd = _dn_prep_fwd(cq, ba, a_log, dt_bias)
    o_dn, states = _dn_seq_fwd(dn_u, dn_w, dn_qd, dn_kd, dn_a, dn_cd)
    w_branch_g, w_out_g, w_gu_b, w_down_g = _copies_wait(w_sems, w_srcs, w_lands, o_dn, "gather_wait_rest")
    w_branch_f = w_branch_g.reshape(D, D)
    w_out_f = w_out_g.reshape(D, D)
    w_down_b = w_down_g.reshape(GU_HALF, GU_SHARD, D)
    x1, mix, merged, ob = _mix_fwd(x, o_attn, o_dn, z, ga, gd, mod, dn_norm_g, w_branch_f, w_out_f)
    h2, gate, up, act = _ffn1_fwd(x1, mod, norm2_g, w_gu_b)
    dy, loss_part, d_gate2 = _ffn2_fwd(act, x1, loss_target, mod, w_down_b)
    loss = lax.psum(loss_part[0, 0], ("x", "y", "c"))

    one = lambda t: t.reshape(B, 1, S, t.shape[-1])
    dgu, dyg = _ffn2_bwd(dy, gate, up, mod, w_down_b)
    g_w_down = _wgrad(act, one(dyg), "wgrad_down")
    dx1, d_n2g, d_scale2, d_shift2 = _ffn1_bwd(dgu, x1, dy, mod, norm2_g, w_gu_b)
    g_w_gu = _wgrad(dgu, one(h2), "wgrad_gate_up")
    ffn = _exchange_start([g_w_gu, g_w_down.reshape(N_DEV, FFN // N_DEV, D)], me, dx1, "exchange_ffn_start")
    dmix, dyo, dga, dgd, dz, d_oa, d_od, d_gate1, d_dng = _mix_bwd(
        dx1, mix, o_attn, o_dn, z, ga, gd, mod, dn_norm_g + ffn[3][0, 0], w_branch_f, w_out_f)
    d_dn = _dn_seq_bwd(dn_u, dn_w, dn_qd, dn_kd, dn_a, dn_cd, states, d_od)
    dcq, dba, d_alog, d_dtb = _dn_prep_bwd(cq, ba, a_log, dt_bias, dn_t, *d_dn)
    ddnx, d_conv = _conv_bwd(dnx, conv_all, dcq)
    daq, dakv, d_qg, d_kg, d_sinks = _attn_bwd(aq, akv, rope_cos, rope_sin, q_norm_g, k_norm_g, sinks, mean_q, mean_k, d_oa)
    dps = [daq, dakv, ddnx, dba, dz, dga, dgd]
    dblk, grad_x, d_n1g, d_scale1, d_shift1 = _inproj_bwd(x, mod, norm1_g, dx1, dps, w_in_b)

    dmod = jnp.concatenate([d_shift1, d_scale1, d_gate1, d_shift2, d_scale2, d_gate2], axis=2).reshape(B, 6 * D)
    small = jnp.concatenate([d_n1g, d_qg, d_kg, d_sinks, d_alog, d_dtb, d_dng, d_n2g, d_conv.reshape(1, CONV * CONVW)], axis=1)
    nsm = small.shape[1]
    width = -(-max(6 * D, nsm) // 128) * 128
    rows = jnp.concatenate([jnp.pad(dmod, ((0, 0), (0, width - 6 * D))), jnp.pad(small, ((0, 8 - B - 1), (0, width - nsm)))], axis=0)
    rows_all = _all_gather_small(rows, "gather_small")
    dmod_all = rows_all[:, 0:B, 0:6 * D].reshape(N_DEV * B, 6 * D)
    dmod_cols = lax.dynamic_slice(dmod_all, (0, me * ncol), (N_DEV * B, ncol))
    grad_ada_w, grad_ada_b, small_sum = _ada_bwd(cond_all, dmod_all, dmod_cols, rows_all[:, B, :])
    sizes = [D, HD, HD, HQ, DH, DH, DK, D]
    so = np.cumsum([0] + sizes)
    g_n1, g_qg, g_kg, g_sk, g_al, g_dt, g_dn, g_n2 = [small_sum[:, so[i]:so[i + 1]] for i in range(8)]
    g_conv_all = small_sum[:, so[8]:so[8] + CONV * CONVW].reshape(CONV, N_DEV, CONVW // N_DEV)
    grad_conv = lax.dynamic_slice(g_conv_all, (0, me, 0), (CONV, 1, CONVW // N_DEV)).reshape(CONV, CONVW // N_DEV)

    g_w_in = _wgrad(dblk, one(h1), "wgrad_in", after=small_sum)
    proj = _exchange_start([g_w_in], me, small_sum, "exchange_in_start")
    g_w_out = _wgrad(one(merged), one(dmix), "wgrad_out", after=proj[3])
    g_w_branch = _wgrad(ob, dyo, "wgrad_branch", after=proj[3])
    mixer = _exchange_start([g_w_branch.reshape(N_DEV, D // N_DEV, D), g_w_out.reshape(N_DEV, D // N_DEV, D)], me,
                            proj[3], "exchange_mix_start")

    upd, grads = {}, {}

    def finish(names, parts, weights):
        for nm, p, (w, m, v) in zip(names, parts, weights):
            grads[nm], *upd[nm] = _sum_adamw(p, w, m, v, "update_" + nm)

    finish(["w_gate_up", "w_down"], _copies_wait(*ffn[:3], mixer[3], "exchange_ffn_wait"),
           [(tr(w_gate_up), tr(m_w_gate_up), tr(v_w_gate_up)), (w_down, m_w_down, v_w_down)])
    finish(["w_in"], _copies_wait(*proj[:3], grads["w_gate_up"], "exchange_in_wait"),
           [(tr(w_in), tr(m_w_in), tr(v_w_in))])
    finish(["w_branch", "w_out"], _copies_wait(*mixer[:3], grads["w_in"], "exchange_mix_wait"),
           [(w_branch, m_w_branch, v_w_branch), (w_out, m_w_out, v_w_out)])
    for nm in ("w_in", "w_gate_up"):
        grads[nm], upd[nm] = tr(grads[nm]), [tr(t) for t in upd[nm]]

    grads["ada_w"] = grad_ada_w.reshape(ada_w.shape)
    upd["ada_w"] = _adamw(ada_w, grads["ada_w"], m_ada_w, v_ada_w, "adamw_ada_w")
    small_names = ["ada_b", "norm1_g", "q_norm_g", "k_norm_g", "sinks", "a_log", "dt_bias", "dn_norm_g", "norm2_g", "conv_w"]
    small_w = [ada_b, norm1_g, q_norm_g, k_norm_g, sinks, a_log, dt_bias, dn_norm_g, norm2_g, conv_w]
    small_g = [grad_ada_b, g_n1, g_qg, g_kg, g_sk, g_al, g_dt, g_dn, g_n2, grad_conv]
    small_m = [m_ada_b, m_norm1_g, m_q_norm_g, m_k_norm_g, m_sinks, m_a_log, m_dt_bias, m_dn_norm_g, m_norm2_g, m_conv_w]
    small_v = [v_ada_b, v_norm1_g, v_q_norm_g, v_k_norm_g, v_sinks, v_a_log, v_dt_bias, v_dn_norm_g, v_norm2_g, v_conv_w]
    cat = lambda arrs: jnp.concatenate([a.reshape(1, -1) for a in arrs], axis=1)
    res = _adamw(cat(small_w), cat(small_g), cat(small_m), cat(small_v), "adamw_small")
    po = np.cumsum([0] + [int(np.prod(w.shape)) for w in small_w])
    for i, nm in enumerate(small_names):
        upd[nm] = tuple(r[:, po[i]:po[i + 1]].reshape(small_w[i].shape) for r in res)
        grads[nm] = small_g[i].reshape(small_w[i].shape)

    order = ["ada_w", "ada_b", "norm1_g", "w_in", "conv_w", "q_norm_g", "k_norm_g", "sinks", "a_log", "dt_bias",
             "dn_norm_g", "w_branch", "w_out", "norm2_g", "w_gate_up", "w_down"]
    return (loss, grad_x, *[grads[n] for n in order], *[upd[n][0] for n in order],
            *[upd[n][1] for n in order], *[upd[n][2] for n in order])
```

```python
import functools

import numpy as np
import jax
import jax.numpy as jnp
from jax import lax
from jax.experimental import pallas as pl
from jax.experimental.pallas import tpu as pltpu

F32 = jnp.float32
BF16 = jnp.bfloat16
HI = lax.Precision.HIGHEST

N_DEV = 8
D = 1024
HQ, HKV, HD = 8, 2, 64
GRP = HQ // HKV
BLK = 128
ROT = HD // 4
THETA = 500000.0
QW, KVW = HQ * HD, HKV * HD
DH, DK = 4, 128
CH = 64
DNW = DH * DK
CONV = 4
CONVW = 3 * DNW
FFN = 2816
EPS = 1e-6
IN_W = QW + 2 * KVW + CONVW + 2 * DH + DNW + 2 * D

LR, B1, B2, AEPS, WD, STEP = 0.001, 0.9, 0.999, 1e-08, 0.01, 10

VMEM_LIMIT = 56 * 1024 * 1024
MESH = pl.DeviceIdType.MESH


def _cparams(sem=None, vmem=VMEM_LIMIT):
    return pltpu.CompilerParams(dimension_semantics=sem, vmem_limit_bytes=vmem)


def _full(shape):
    n = len(shape)
    return pl.BlockSpec(shape, lambda *_: (0,) * n)


def _resident(shape):
    n = len(shape)
    return pl.BlockSpec(shape, lambda *_: (0,) * n, pipeline_mode=pl.Buffered(1))


def _rows(tm, w):
    return pl.BlockSpec((None, tm, w), lambda b, i: (b, i, 0))


def _stacked(n, tm, w):
    return pl.BlockSpec((None, n, tm, w), lambda b, i: (b, 0, i, 0))


def _perb(r, w):
    return pl.BlockSpec((None, r, w), lambda b, i: (b, 0, 0))


def _dot(a, b):
    return jnp.dot(a.astype(BF16), b.astype(BF16), preferred_element_type=F32)


def _dot_nt(a, b):
    return lax.dot_general(a.astype(BF16), b.astype(BF16), (((1,), (1,)), ((), ())), preferred_element_type=F32)


def _dot_tn(a, b):
    return lax.dot_general(a.astype(BF16), b.astype(BF16), (((0,), (0,)), ((), ())), preferred_element_type=F32)


def _dot_hi(a, b):
    return jnp.dot(a, b, preferred_element_type=F32, precision=HI)


def _sigmoid(x):
    return jax.nn.sigmoid(x)


def _silu(x):
    return x * jax.nn.sigmoid(x)


def _rms_mod(x, g, scale, shift):
    r = lax.rsqrt(jnp.mean(x * x, axis=-1, keepdims=True) + EPS)
    return (x * r * g) * (1.0 + scale) + shift


def _tile(S, rows=256):
    return min(rows, S)


def _peer(x, y, c, k):
    px = 1 - x if (k >> 2) & 1 else x
    py = 1 - y if (k >> 1) & 1 else y
    pc = 1 - c if k & 1 else c
    return px, py, pc


def _all_gather_small(v, name):
    r, n = v.shape

    def body(v_ref, out_ref, send_sems, recv_sems, local_sem):
        x, y, c = lax.axis_index("x"), lax.axis_index("y"), lax.axis_index("c")
        me = 4 * x + 2 * y + c
        mine = pltpu.make_async_copy(v_ref, out_ref.at[me], local_sem)
        mine.start()
        sends = []
        for k in range(1, N_DEV):
            cp = pltpu.make_async_remote_copy(
                src_ref=v_ref, dst_ref=out_ref.at[me], send_sem=send_sems.at[k - 1], recv_sem=recv_sems.at[k - 1],
                device_id=_peer(x, y, c, k), device_id_type=MESH)
            cp.start()
            sends.append(cp)
        for k in range(1, N_DEV):
            px, py, pc = _peer(x, y, c, k)
            pltpu.make_async_remote_copy(
                src_ref=v_ref, dst_ref=out_ref.at[4 * px + 2 * py + pc], send_sem=send_sems.at[k - 1],
                recv_sem=recv_sems.at[k - 1], device_id=(px, py, pc), device_id_type=MESH).wait_recv()
        for cp in sends:
            cp.wait_send()
        mine.wait()

    return pl.pallas_call(
        body, name=name,
        out_shape=jax.ShapeDtypeStruct((N_DEV, r, n), v.dtype),
        in_specs=[pl.BlockSpec(memory_space=pltpu.VMEM)],
        out_specs=pl.BlockSpec(memory_space=pltpu.VMEM),
        scratch_shapes=[pltpu.SemaphoreType.DMA((N_DEV - 1,)), pltpu.SemaphoreType.DMA((N_DEV - 1,)), pltpu.SemaphoreType.DMA],
    )(v)


def _all_gather_big(vs, name, after=()):
    na, nf = len(vs), len(after)

    def body(*refs):
        v_refs, out_refs = refs[:na], refs[na + nf:2 * na + nf]
        send_sems, recv_sems, local_sems = refs[2 * na + nf:]
        x, y, c = lax.axis_index("x"), lax.axis_index("y"), lax.axis_index("c")
        me, sibling = (x, y, c), (x, y, 1 - c)
        chips = [(1 - x, y), (x, 1 - y), (1 - x, 1 - y)]

        def rows(a, px, py, pc):
            return out_refs[a].at[4 * px + 2 * py + pc]

        def copy(a, k, block, to, src=None):
            return pltpu.make_async_remote_copy(
                src_ref=rows(a, *block) if src is None else src, dst_ref=rows(a, *block),
                send_sem=send_sems.at[7 * a + k], recv_sem=recv_sems.at[7 * a + k], device_id=to, device_id_type=MESH)

        mine = [pltpu.make_async_copy(v_refs[a], rows(a, *me), local_sems.at[a]) for a in range(na)]
        for cp in mine:
            cp.start()
        first = []
        for a in range(na):
            first.append(copy(a, 0, me, sibling, src=v_refs[a]))
            first += [copy(a, 1 + j, me, (*chip, c), src=v_refs[a]) for j, chip in enumerate(chips)]
        for cp in first:
            cp.start()
        passed = []
        for j, chip in enumerate(chips):
            for a in range(na):
                copy(a, 1 + j, (*chip, c), me).wait_recv()
                forward = copy(a, 4 + j, (*chip, c), sibling)
                forward.start()
                passed.append(forward)
        for a in range(na):
            copy(a, 0, sibling, me).wait_recv()
            for j, chip in enumerate(chips):
                copy(a, 4 + j, (*chip, 1 - c), me).wait_recv()
        for cp in first + passed:
            cp.wait_send()
        for cp in mine:
            cp.wait()

    return pl.pallas_call(
        body, name=name,
        out_shape=[jax.ShapeDtypeStruct((N_DEV,) + v.shape, v.dtype) for v in vs],
        in_specs=[pl.BlockSpec(memory_space=pl.ANY)] * (na + nf),
        out_specs=[pl.BlockSpec(memory_space=pl.ANY)] * na,
        scratch_shapes=[pltpu.SemaphoreType.DMA((7 * na,)), pltpu.SemaphoreType.DMA((7 * na,)),
                        pltpu.SemaphoreType.DMA((na,))],
    )(*vs, *after)


_HBM = pl.BlockSpec(memory_space=pltpu.HBM)
_SEM = pl.BlockSpec(memory_space=pltpu.SEMAPHORE)
_EFFECT = pltpu.SideEffectType.DATAFLOW_SIDE_EFFECTING


def _place_own(block, me):
    land = lax.empty((N_DEV,) + block.shape, block.dtype)
    return lax.dynamic_update_slice(land, block[None], (me,) + (0,) * block.ndim)


def _copies_start(srcs, lands, scatter, after, name):
    na = len(srcs)
    afters = tuple(after) if isinstance(after, (tuple, list)) else (after,)

    def body(*refs):
        src_refs, land_refs = refs[:na], refs[na:2 * na]
        sems = refs[2 * na + len(afters):4 * na + len(afters)]
        token = refs[-1]
        x, y, c = lax.axis_index("x"), lax.axis_index("y"), lax.axis_index("c")
        me = 4 * x + 2 * y + c
        for a in range(na):
            for k in range(1, N_DEV):
                px, py, pc = _peer(x, y, c, k)
                src = src_refs[a].at[4 * px + 2 * py + pc] if scatter else src_refs[a]
                pltpu.make_async_remote_copy(
                    src_ref=src, dst_ref=land_refs[a].at[me], send_sem=sems[2 * a], recv_sem=sems[2 * a + 1],
                    device_id=(px, py, pc), device_id_type=MESH).start()
        token[...] = jnp.zeros_like(token)

    hbm = lambda t: pltpu.HBM(t.shape, t.dtype)
    out = pl.pallas_call(
        body, name=name,
        out_shape=tuple([pltpu.SemaphoreType.DMA(())] * (2 * na) + [hbm(t) for t in srcs] + [hbm(t) for t in lands]
                        + [jax.ShapeDtypeStruct((8, 128), F32)]),
        in_specs=[_HBM] * (2 * na) + [pl.BlockSpec(memory_space=pl.ANY)] * len(afters),
        out_specs=tuple([_SEM] * (2 * na) + [_HBM] * (2 * na) + [pl.BlockSpec(memory_space=pltpu.VMEM)]),
        input_output_aliases={i: 2 * na + i for i in range(2 * na)},
        compiler_params=pltpu.CompilerParams(has_side_effects=_EFFECT),
    )(*[pltpu.with_memory_space_constraint(t, pltpu.HBM) for t in list(srcs) + list(lands)], *afters)
    return out[:2 * na], out[2 * na:3 * na], out[3 * na:4 * na], out[-1]


def _exchange_start(gs, me, after, name):
    own = [lax.dynamic_index_in_dim(g, me, 0, keepdims=False) for g in gs]
    return _copies_start(gs, [_place_own(o, me) for o in own], True, after, name)


def _copies_wait(sems, srcs, lands, after, name):
    na = len(srcs)

    def body(*refs):
        land_refs = refs[na:2 * na]
        sem_refs = refs[2 * na:4 * na]
        x, y, c = lax.axis_index("x"), lax.axis_index("y"), lax.axis_index("c")
        for a in range(na):
            seven = land_refs[a].at[pl.ds(0, N_DEV - 1)]
            copy = pltpu.make_async_remote_copy(
                src_ref=seven, dst_ref=seven, send_sem=sem_refs[2 * a], recv_sem=sem_refs[2 * a + 1],
                device_id=(x, y, c), device_id_type=MESH)
            copy.wait_send()
            copy.wait_recv()

    hbm = lambda t: pltpu.HBM(t.shape, t.dtype)
    out = pl.pallas_call(
        body, name=name,
        out_shape=tuple([hbm(t) for t in srcs] + [hbm(t) for t in lands]),
        in_specs=[_HBM] * (2 * na) + [_SEM] * (2 * na) + [pl.BlockSpec(memory_space=pl.ANY)],
        out_specs=tuple([_HBM] * (2 * na)),
        input_output_aliases={i: i for i in range(2 * na)},
        compiler_params=pltpu.CompilerParams(has_side_effects=_EFFECT),
    )(*srcs, *lands, *sems, after)
    return out[na:]


def _adamw_math(w, g, m, v):
    m = B1 * m + (1.0 - B1) * g
    v = B2 * v + (1.0 - B2) * (g * g)
    m_hat = m / (1.0 - B1 ** STEP)
    v_hat = v / (1.0 - B2 ** STEP)
    return -LR * (m_hat / (jnp.sqrt(v_hat) + AEPS) + WD * w), m, v


def _sum_adamw(parts, w, m, v, name):
    _, r, n = parts.shape
    tr = 256 if r % 256 == 0 else r

    def body(p_ref, w_ref, m_ref, v_ref, g_ref, d_ref, nm_ref, nv_ref):
        g = p_ref[0].astype(F32)
        for dev in range(1, N_DEV):
            g = g + p_ref[dev].astype(F32)
        g_ref[...] = g
        d_ref[...], nm_ref[...], nv_ref[...] = _adamw_math(w_ref[...], g, m_ref[...], v_ref[...])

    rows = pl.BlockSpec((None, tr, n), lambda i: (0, i, 0))
    sd = jax.ShapeDtypeStruct((1, r, n), F32)
    return pl.pallas_call(
        body, name=name, grid=(r // tr,), out_shape=(sd, sd, sd, sd),
        in_specs=[pl.BlockSpec((N_DEV, tr, n), lambda i: (0, i, 0)), rows, rows, rows],
        out_specs=(rows, rows, rows, rows),
        compiler_params=_cparams(("parallel",)),
    )(parts, w, m, v)


def _ada_fwd(c_all, ada_w, ada_b_cols):
    nb, ncol = c_all.shape[0], ada_w.shape[1]

    def body(c_ref, w_ref, b_ref, mod_ref, cond_ref):
        cond = _silu(c_ref[...])
        cond_ref[...] = cond
        mod_ref[...] = _dot_hi(cond, w_ref[...]) + b_ref[...]

    return pl.pallas_call(
        body, name="ada_fwd",
        out_shape=(jax.ShapeDtypeStruct((nb, ncol), F32), jax.ShapeDtypeStruct((nb, D), F32)),
        compiler_params=_cparams(),
    )(c_all, ada_w, ada_b_cols)


def _ada_bwd(cond_all, dmod_all, dmod_cols, smalls):
    ncol, nsm = dmod_cols.shape[1], smalls.shape[1]

    def body(cond_ref, dm_ref, dmc_ref, sm_ref, gw_ref, gb_ref, gs_ref):
        gw_ref[...] = lax.dot_general(cond_ref[...], dmc_ref[...], (((0,), (0,)), ((), ())),
                                      preferred_element_type=F32, precision=HI)
        gb_ref[...] = jnp.sum(dm_ref[...], axis=0, keepdims=True)
        gs_ref[...] = jnp.sum(sm_ref[...], axis=0, keepdims=True)

    return pl.pallas_call(
        body, name="ada_bwd",
        out_shape=(jax.ShapeDtypeStruct((D, ncol), F32), jax.ShapeDtypeStruct((1, 6 * D), F32),
                   jax.ShapeDtypeStruct((1, nsm), F32)),
        compiler_params=_cparams(),
    )(cond_all, dmod_all, dmod_cols, smalls)


IN_CUTS = (0, QW, QW + 2 * KVW, QW + 2 * KVW + CONVW, QW + 2 * KVW + CONVW + 2 * DH,
           QW + 2 * KVW + CONVW + 2 * DH + DNW, QW + 2 * KVW + CONVW + 2 * DH + DNW + D, IN_W)
IN_WIDTHS = tuple(b - a for a, b in zip(IN_CUTS[:-1], IN_CUTS[1:]))
IN_SHARD = IN_W // N_DEV


def _inproj_fwd(x, mod, g1, w_blk):
    B, S, _ = x.shape
    tm = _tile(S)

    def body(x_ref, mod_ref, g_ref, w_ref, h_ref, *o_refs):
        h = _rms_mod(x_ref[...], g_ref[...], mod_ref[1:2, :], mod_ref[0:1, :]).astype(BF16)
        h_ref[...] = h
        full = jnp.concatenate([_dot_nt(h, w_ref[j]) for j in range(N_DEV)], axis=1)
        for o_ref, lo, hi in zip(o_refs, IN_CUTS[:-1], IN_CUTS[1:]):
            o_ref[...] = full[:, lo:hi]

    return pl.pallas_call(
        body, name="inproj_fwd", grid=(B, S // tm),
        out_shape=[jax.ShapeDtypeStruct((B, S, D), BF16)] + [jax.ShapeDtypeStruct((B, S, w), F32) for w in IN_WIDTHS],
        in_specs=[_rows(tm, D), _perb(6, D), _full((1, D)), _resident(w_blk.shape)],
        out_specs=[_rows(tm, D)] + [_rows(tm, w) for w in IN_WIDTHS],
        compiler_params=_cparams(("parallel", "arbitrary")),
    )(x, mod, g1, w_blk)


def _inproj_bwd(x, mod, g1, dx1, dps, w_blk):
    B, S, _ = x.shape
    tm = _tile(S)
    n = len(dps)

    def body(x_ref, mod_ref, g_ref, dx1_ref, *refs):
        dp_refs, w_ref = refs[:n], refs[n]
        dblk_ref, gx_ref, dg_ref, dsc_ref, dsh_ref = refs[n + 1:]
        b, i = pl.program_id(0), pl.program_id(1)
        full = jnp.concatenate([r[...].astype(F32) for r in dp_refs], axis=1)
        dh = None
        for j in range(N_DEV):
            blk = full[:, IN_SHARD * j:IN_SHARD * (j + 1)].astype(BF16)
            dblk_ref[j] = blk
            t = jnp.dot(blk, w_ref[j], preferred_element_type=F32)
            dh = t if dh is None else dh + t
        _, vjp = jax.vjp(_rms_mod, x_ref[...], g_ref[...], mod_ref[1:2, :], mod_ref[0:1, :])
        dx, dg, dsc, dsh = vjp(dh)
        gx_ref[...] = dx1_ref[...] + dx

        @pl.when((b == 0) & (i == 0))
        def _():
            dg_ref[...] = jnp.zeros_like(dg_ref)

        @pl.when(i == 0)
        def _():
            dsc_ref[...] = jnp.zeros_like(dsc_ref)
            dsh_ref[...] = jnp.zeros_like(dsh_ref)

        dg_ref[...] += dg
        dsc_ref[...] += dsc
        dsh_ref[...] += dsh

    return pl.pallas_call(
        body, name="inproj_bwd", grid=(B, S // tm),
        out_shape=[jax.ShapeDtypeStruct((B, N_DEV, S, IN_SHARD), BF16), jax.ShapeDtypeStruct((B, S, D), F32),
                   jax.ShapeDtypeStruct((1, D), F32), jax.ShapeDtypeStruct((B, 1, D), F32),
                   jax.ShapeDtypeStruct((B, 1, D), F32)],
        in_specs=[_rows(tm, D), _perb(6, D), _full((1, D)), _rows(tm, D)]
                 + [_rows(tm, w) for w in IN_WIDTHS] + [_resident(w_blk.shape)],
        out_specs=[pl.BlockSpec((None, N_DEV, tm, IN_SHARD), lambda b, i: (b, 0, i, 0)), _rows(tm, D),
                   _full((1, D)), _perb(1, D), _perb(1, D)],
        compiler_params=_cparams(("arbitrary", "arbitrary")),
    )(x, mod, g1, dx1, *dps, w_blk)


def _wgrad(a, b, name, after=None):
    B, na, S, K = a.shape
    nb, N = b.shape[1], b.shape[3]
    G = max(na, nb)
    tm = min(2048, S)
    nt = S // tm
    last = B * nt - 1

    def body(a_ref, b_ref, *rest):
        o_ref, acc = rest[-2:]
        t = pl.program_id(1)

        @pl.when(t == 0)
        def _():
            acc[...] = jnp.zeros_like(acc)

        acc[...] += lax.dot_general(a_ref[...], b_ref[...], (((0,), (0,)), ((), ())), preferred_element_type=F32)

        @pl.when(t == last)
        def _():
            o_ref[...] = acc[...].astype(BF16)

    return pl.pallas_call(
        body, name=name, grid=(G, B * nt),
        out_shape=jax.ShapeDtypeStruct((G, K, N), BF16),
        in_specs=[pl.BlockSpec((None, None, tm, K), lambda g, t: (t // nt, g if na > 1 else 0, t % nt, 0)),
                  pl.BlockSpec((None, None, tm, N), lambda g, t: (t // nt, g if nb > 1 else 0, t % nt, 0))]
                 + ([] if after is None else [pl.BlockSpec(memory_space=pl.ANY)]),
        out_specs=pl.BlockSpec((None, K, N), lambda g, t: (g, 0, 0)),
        scratch_shapes=[pltpu.VMEM((K, N), F32)],
        compiler_params=_cparams(("parallel", "arbitrary")),
    )(*((a, b) if after is None else (a, b, after)))


LANES = 128


def _attn_consts():
    inv_freq = THETA ** (-jnp.arange(0, ROT, 2, dtype=F32) / ROT)
    head = jnp.concatenate([inv_freq, inv_freq, jnp.zeros((HD - ROT,), F32)])
    invf = jnp.tile(head, LANES // HD)[None, :]
    mean_of = lambda w: jnp.asarray(np.kron(np.eye(w // HD), np.full((HD, HD), 1.0 / HD)), BF16)
    return invf, mean_of(QW), mean_of(KVW)


def _rope_tables(pos, invf):
    B, S, _ = pos.shape
    tr = min(1024, S)

    def body(p_ref, f_ref, c_ref, s_ref):
        ang = p_ref[...].astype(F32) * f_ref[...]
        c_ref[...] = jnp.cos(ang)
        s_ref[...] = jnp.sin(ang)

    sd = jax.ShapeDtypeStruct((B, S, LANES), F32)
    return pl.pallas_call(
        body, name="rope_tables", grid=(B, S // tr), out_shape=[sd, sd],
        in_specs=[_rows(tr, 1), _full((1, LANES))], out_specs=[_rows(tr, LANES), _rows(tr, LANES)],
        compiler_params=_cparams(("parallel", "parallel")),
    )(pos, invf)


def _rope_expand(cos, sin, reps):
    lane = lax.broadcasted_iota(jnp.int32, cos.shape, 1) % HD
    sa = jnp.where((lane >= ROT // 2) & (lane < ROT), sin, 0.0)
    sb = jnp.where(lane < ROT // 2, -sin, 0.0)
    rep = lambda t: jnp.concatenate([t] * reps, axis=1) if reps > 1 else t
    return rep(cos), rep(sa), rep(sb)


@jax.custom_vjp
def _rope(t, cos, sa, sb):
    w = t.shape[1]
    return t * cos + pltpu.roll(t, ROT // 2, 1) * sa + pltpu.roll(t, w - ROT // 2, 1) * sb


def _rope_fwd(t, cos, sa, sb):
    return _rope(t, cos, sa, sb), (cos, sa, sb)


def _rope_bwd(res, d):
    cos, sa, sb = res
    w = d.shape[1]
    dt = d * cos + pltpu.roll(d * sa, w - ROT // 2, 1) + pltpu.roll(d * sb, ROT // 2, 1)
    return dt, jnp.zeros_like(cos), jnp.zeros_like(sa), jnp.zeros_like(sb)


_rope.defvjp(_rope_fwd, _rope_bwd)


def _head_norm(t, g, mean_of):
    hi, lo = _split(t * t)
    ms = jnp.dot(hi, mean_of, preferred_element_type=F32) + jnp.dot(lo, mean_of, preferred_element_type=F32)
    return t * lax.rsqrt(ms + EPS) * g


def _attn_block(q, kvp, kvc, qg, kg, sinks, tq, tk, mq, mk, valid):
    qn = _rope(_head_norm(q, jnp.concatenate([qg] * HQ, axis=1), mq), *tq) * (HD ** -0.5)
    kv = jnp.concatenate([kvp, kvc], axis=0)
    kn = _rope(_head_norm(kv[:, 0:KVW], jnp.concatenate([kg] * HKV, axis=1), mk), *tk)
    per_tile = LANES // HD
    vT = jnp.transpose(kv[:, KVW:2 * KVW])
    qT = [jnp.transpose(qn[:, LANES * t:LANES * (t + 1)]) for t in range(QW // LANES)]
    head_T = lambda h: qT[h // per_tile][HD * (h % per_tile):HD * (h % per_tile + 1), :]
    none = jnp.zeros((HD, GRP * BLK), F32)
    o_T = []
    for j in range(HKV):
        q4T = jnp.concatenate([head_T(GRP * j + i) for i in range(GRP)], axis=1)
        sT = _dot(kn, jnp.concatenate([q4T, none] if j == 0 else [none, q4T], axis=0))
        sT = jnp.where(valid, sT, -1e30)
        sink = jnp.concatenate([jnp.broadcast_to(sinks[:, GRP * j + i:GRP * j + i + 1], (1, BLK)) for i in range(GRP)], axis=1)
        m = lax.stop_gradient(jnp.maximum(jnp.max(sT, axis=0, keepdims=True), sink))
        pT = jnp.exp(sT - m)
        den = jnp.sum(pT, axis=0, keepdims=True) + jnp.exp(sink - m)
        oT = _dot(vT[HD * j:HD * (j + 1), :], pT) * (1.0 / den)
        o_T += [oT[:, BLK * i:BLK * (i + 1)] for i in range(GRP)]
    return jnp.concatenate([jnp.transpose(jnp.concatenate(o_T[per_tile * t:per_tile * (t + 1)], axis=0))
                            for t in range(QW // LANES)], axis=1)


def _attn_tables(cp_ref, cc_ref, sp_ref, sc_ref, n):
    tq = _rope_expand(cc_ref[...], sc_ref[...], QW // LANES)
    tk = _rope_expand(jnp.concatenate([cp_ref[...], cc_ref[...]], axis=0),
                      jnp.concatenate([sp_ref[...], sc_ref[...]], axis=0), KVW // LANES)
    qi = lax.broadcasted_iota(jnp.int32, (2 * BLK, GRP * BLK), 1) % BLK + BLK
    kj = lax.broadcasted_iota(jnp.int32, (2 * BLK, GRP * BLK), 0)
    dist = qi - kj
    valid = (dist >= 0) & (dist < BLK) & ((kj >= BLK) | (n > 0))
    return tq, tk, valid


def _attn_fwd(aq, akv, cos, sin, qg, kg, sinks, mq, mk):
    B, S, _ = aq.shape
    nb = S // BLK

    def body(q_ref, kvp_ref, kvc_ref, cp_ref, cc_ref, sp_ref, sc_ref, qg_ref, kg_ref, sk_ref, mq_ref, mk_ref, o_ref):
        tq, tk, valid = _attn_tables(cp_ref, cc_ref, sp_ref, sc_ref, pl.program_id(1))
        o_ref[...] = _attn_block(q_ref[...], kvp_ref[...], kvc_ref[...], qg_ref[...], kg_ref[...], sk_ref[...],
                                 tq, tk, mq_ref[...], mk_ref[...], valid)

    prev = lambda b, n: (b, jnp.maximum(n - 1, 0), 0)
    cur = lambda b, n: (b, n, 0)
    return pl.pallas_call(
        body, name="attn_fwd", grid=(B, nb),
        out_shape=jax.ShapeDtypeStruct((B, S, QW), F32),
        in_specs=[pl.BlockSpec((None, BLK, QW), cur), pl.BlockSpec((None, BLK, 2 * KVW), prev),
                  pl.BlockSpec((None, BLK, 2 * KVW), cur), pl.BlockSpec((None, BLK, LANES), prev),
                  pl.BlockSpec((None, BLK, LANES), cur), pl.BlockSpec((None, BLK, LANES), prev),
                  pl.BlockSpec((None, BLK, LANES), cur), _full((1, HD)), _full((1, HD)), _full((1, HQ)),
                  _full((QW, QW)), _full((KVW, KVW))],
        out_specs=pl.BlockSpec((None, BLK, QW), cur),
        compiler_params=_cparams(("parallel", "arbitrary")),
    )(aq, akv, akv, cos, cos, sin, sin, qg, kg, sinks, mq, mk)


def _attn_bwd(aq, akv, cos, sin, qg, kg, sinks, mq, mk, do):
    B, S, _ = aq.shape
    nb = S // BLK

    def body(q_ref, kvp_ref, kvc_ref, cp_ref, cc_ref, sp_ref, sc_ref, qg_ref, kg_ref, sk_ref, mq_ref, mk_ref, do_ref,
             dq_ref, dkv_ref, dqg_ref, dkg_ref, dsk_ref, carry):
        b, i = pl.program_id(0), pl.program_id(1)
        tq, tk, valid = _attn_tables(cp_ref, cc_ref, sp_ref, sc_ref, nb - 1 - i)
        fn = functools.partial(_attn_block, tq=tq, tk=tk, mq=mq_ref[...], mk=mk_ref[...], valid=valid)
        _, vjp = jax.vjp(fn, q_ref[...], kvp_ref[...], kvc_ref[...], qg_ref[...], kg_ref[...], sk_ref[...])
        dq, dkvp, dkvc, dqg, dkg, dsk = vjp(do_ref[...])

        @pl.when(i == 0)
        def _():
            carry[...] = jnp.zeros_like(carry)

        @pl.when((b == 0) & (i == 0))
        def _():
            dqg_ref[...] = jnp.zeros_like(dqg_ref)
            dkg_ref[...] = jnp.zeros_like(dkg_ref)
            dsk_ref[...] = jnp.zeros_like(dsk_ref)

        dq_ref[...] = dq.astype(BF16)
        dkv_ref[...] = (dkvc + carry[...]).astype(BF16)
        carry[...] = dkvp
        dqg_ref[...] += dqg
        dkg_ref[...] += dkg
        dsk_ref[...] += dsk

    prev = lambda b, i: (b, jnp.maximum(nb - 2 - i, 0), 0)
    cur = lambda b, i: (b, nb - 1 - i, 0)
    return pl.pallas_call(
        body, name="attn_bwd", grid=(B, nb),
        out_shape=[jax.ShapeDtypeStruct((B, S, QW), BF16), jax.ShapeDtypeStruct((B, S, 2 * KVW), BF16),
                   jax.ShapeDtypeStruct((1, HD), F32), jax.ShapeDtypeStruct((1, HD), F32),
                   jax.ShapeDtypeStruct((1, HQ), F32)],
        in_specs=[pl.BlockSpec((None, BLK, QW), cur), pl.BlockSpec((None, BLK, 2 * KVW), prev),
                  pl.BlockSpec((None, BLK, 2 * KVW), cur), pl.BlockSpec((None, BLK, LANES), prev),
                  pl.BlockSpec((None, BLK, LANES), cur), pl.BlockSpec((None, BLK, LANES), prev),
                  pl.BlockSpec((None, BLK, LANES), cur), _full((1, HD)), _full((1, HD)), _full((1, HQ)),
                  _full((QW, QW)), _full((KVW, KVW)), pl.BlockSpec((None, BLK, QW), cur)],
        out_specs=[pl.BlockSpec((None, BLK, QW), cur), pl.BlockSpec((None, BLK, 2 * KVW), cur),
                   _full((1, HD)), _full((1, HD)), _full((1, HQ))],
        scratch_shapes=[pltpu.VMEM((BLK, 2 * KVW), F32)],
        compiler_params=_cparams(("arbitrary", "arbitrary")),
    )(aq, akv, akv, cos, cos, sin, sin, qg, kg, sinks, mq, mk, do)


def _conv_taps(xe, w, rows):
    y = None
    for j in range(CONV):
        sh = pltpu.roll(xe, CONV - 1 - j, 0)[8:8 + rows, :] if j < CONV - 1 else xe[8:8 + rows, :]
        y = sh * w[j:j + 1, :] if y is None else y + sh * w[j:j + 1, :]
    return y


def _conv_bwd(xin, w, dy):
    B, S, C = xin.shape
    tc = min(512, S)
    r8 = tc // 8
    nt = S // tc

    def body(xp_ref, x_ref, xn_ref, dy_ref, dyn_ref, w_ref, dx_ref, dw_ref):
        b, i = pl.program_id(0), pl.program_id(1)
        w = w_ref[...]
        xp = jnp.where(i > 0, xp_ref[...], 0.0)
        xe = jnp.concatenate([xp, x_ref[...], xn_ref[...]], axis=0)
        taps = [(pltpu.roll(xe, CONV - 1 - j, 0) if j < CONV - 1 else xe)[8:8 + tc + 8, :] for j in range(CONV)]
        pre = sum(t * w[j:j + 1, :] for j, t in enumerate(taps))
        sg = _sigmoid(pre)
        dyn = jnp.where(i < nt - 1, dyn_ref[...], 0.0)
        dpre = jnp.concatenate([dy_ref[...], dyn], axis=0) * (sg * (1.0 + pre * (1.0 - sg)))
        dx = dpre[0:tc, :] * w[CONV - 1:CONV, :]
        for j in range(CONV - 1):
            dx = dx + pltpu.roll(dpre, tc + 8 - (CONV - 1 - j), 0)[0:tc, :] * w[j:j + 1, :]
        dx_ref[...] = dx.astype(BF16)
        dcur = dpre[0:tc, :]
        lane_row = lax.broadcasted_iota(jnp.int32, (CONV, C), 0)
        dw = jnp.zeros((CONV, C), F32)
        for j in range(CONV):
            dw = dw + jnp.where(lane_row == j, jnp.sum(taps[j][0:tc, :] * dcur, axis=0, keepdims=True), 0.0)

        @pl.when((b == 0) & (i == 0))
        def _():
            dw_ref[...] = jnp.zeros_like(dw_ref)

        dw_ref[...] += dw

    return pl.pallas_call(
        body, name="conv_bwd", grid=(B, nt),
        out_shape=[jax.ShapeDtypeStruct((B, S, C), BF16), jax.ShapeDtypeStruct((CONV, C), F32)],
        in_specs=[pl.BlockSpec((None, 8, C), lambda b, i: (b, jnp.maximum(i * r8 - 1, 0), 0)),
                  _rows(tc, C),
                  pl.BlockSpec((None, 8, C), lambda b, i: (b, jnp.minimum((i + 1) * r8, S // 8 - 1), 0)),
                  _rows(tc, C),
                  pl.BlockSpec((None, 8, C), lambda b, i: (b, jnp.minimum((i + 1) * r8, S // 8 - 1), 0)),
                  _full((CONV, C))],
        out_specs=[_rows(tc, C), _full((CONV, C))],
        compiler_params=_cparams(("arbitrary", "arbitrary")),
    )(xin, xin, xin, dy, dy, w)


def _softplus(x):
    return jnp.maximum(x, 0.0) + jnp.log1p(jnp.exp(-jnp.abs(x)))


_BMM = (((2,), (1,)), ((0,), (0,)))
_BMM_NT = (((2,), (2,)), ((0,), (0,)))
_BMM_TN = (((1,), (1,)), ((0,), (0,)))


def _bmm(a, b, dims=_BMM):
    return lax.dot_general(a.astype(BF16), b.astype(BF16), dims, preferred_element_type=F32)


def _split(a):
    hi = a.astype(BF16)
    return hi, (a - hi.astype(F32)).astype(BF16)


def _bmm3(a, b, dims=_BMM):
    ah, al = _split(a)
    bh, bl = _split(b)
    d = lambda p, q: lax.dot_general(p, q, dims, preferred_element_type=F32)
    return d(ah, bh) + (d(ah, bl) + d(al, bh))


TRI_BASE = 8


def _tri_inverse(L):
    ii = lax.broadcasted_iota(jnp.int32, (CH, CH), 0)
    jj = lax.broadcasted_iota(jnp.int32, (CH, CH), 1)
    same = lambda size: (ii // size) == (jj // size)
    diag = jnp.where(same(TRI_BASE), L, 0.0)
    X = (ii == jj).astype(F32) - diag
    P = diag
    n = 2
    while n < TRI_BASE:
        P = _bmm3(P, P)
        X = X + _bmm3(X, P)
        n *= 2
    size = TRI_BASE
    while size < CH:
        joint = jnp.where(same(2 * size) & jnp.logical_not(same(size)), L, 0.0)
        X = X - _bmm3(X, _bmm3(joint, X))
        size *= 2
    return X


@jax.custom_vjp
def _tri_inverse_known(L, T):
    return T


def _tri_inverse_known_fwd(L, T):
    return T, T


def _tri_inverse_known_bwd(T, dT):
    return -_bmm3(T, _bmm3(dT, T, _BMM_NT), _BMM_TN), jnp.zeros_like(T)


_tri_inverse_known.defvjp(_tri_inverse_known_fwd, _tri_inverse_known_bwd)


def _cumsum_rows(g):
    n = g.shape[0]
    ii = lax.broadcasted_iota(jnp.int32, (n, CH, CH), 1)
    jj = lax.broadcasted_iota(jnp.int32, (n, CH, CH), 2)
    tri = (ii >= jj).astype(BF16)
    g0 = g.astype(BF16)
    r1 = g - g0.astype(F32)
    g1 = r1.astype(BF16)
    g2 = (r1 - g1.astype(F32)).astype(BF16)
    d = lambda q: lax.dot_general(tri, q, _BMM, preferred_element_type=F32)
    return d(g0) + (d(g1) + d(g2))


def _row_sums(t):
    n, r, w = t.shape
    hi, lo = _split(t.reshape(n * r, w))
    ones = jnp.ones((w, w), BF16)
    s = jnp.dot(hi, ones, preferred_element_type=F32) + jnp.dot(lo, ones, preferred_element_type=F32)
    return s.reshape(n, r, w)


def _dn_prep(t_known, qr, kr, v, a_raw, b_raw, a_log, dt_b):
    n = qr.shape[0]
    ii = lax.broadcasted_iota(jnp.int32, (n, CH, CH), 1)
    jj = lax.broadcasted_iota(jnp.int32, (n, CH, CH), 2)
    incl, strict = ii >= jj, ii > jj
    q = qr * lax.rsqrt(_row_sums(qr * qr) + EPS) * (DK ** -0.5)
    k = kr * lax.rsqrt(_row_sums(kr * kr) + EPS)
    beta = _sigmoid(b_raw)
    g = -jnp.exp(a_log) * _softplus(a_raw + dt_b)
    gcb = _cumsum_rows(jnp.broadcast_to(g, (n, CH, DK)))
    gc = gcb[:, :, 0:1]
    gc_row = jnp.swapaxes(gcb, 1, 2)[:, 0:1, 0:CH]
    decay = jnp.where(incl, jnp.exp(jnp.where(incl, gc - gc_row, 0.0)), 0.0)
    kb = k * beta
    L = jnp.where(strict, _bmm(kb, k, _BMM_NT) * decay, 0.0)
    T = _tri_inverse(L) if t_known is None else _tri_inverse_known(L, t_known)
    eg = jnp.exp(gc)
    u = _bmm(T, v * beta)
    w = _bmm(T, kb * eg)
    a_in = _bmm(q, k, _BMM_NT) * decay
    g_last = gc[:, CH - 1:CH, :]
    return u, w, q * eg, k * jnp.exp(g_last - gc), a_in, jnp.exp(g_last), T


def _dn_step(S0, u, w, qd, kd, a_in, cd):
    r = _bmm(jnp.concatenate([w, qd], axis=1), S0)
    v_new = u - r[:, 0:CH, :]
    o = r[:, CH:2 * CH, :] + _bmm(a_in, v_new)
    S1 = S0 * cd + _bmm(kd, v_new, _BMM_TN)
    return o, S1


def _dn_stack(cq, ba, al, dt, G):
    cols = [[] for _ in range(7)]
    for c in range(G):
        rows = slice(CH * c, CH * (c + 1))
        for h in range(DH):
            parts = (cq[rows, DK * h:DK * (h + 1)], cq[rows, DNW + DK * h:DNW + DK * (h + 1)],
                     cq[rows, 2 * DNW + DK * h:2 * DNW + DK * (h + 1)], ba[rows, DH + h:DH + h + 1],
                     ba[rows, h:h + 1], al[:, h:h + 1], dt[:, h:h + 1])
            for col, p in zip(cols, parts):
                col.append(p)
    return tuple(jnp.stack(col) for col in cols)


def _dn_group(S, want):
    g = want
    while (S // CH) % g:
        g //= 2
    return g


def _dn_prep_fwd(xin, conv_w, ba, a_log, dt_b):
    B, S, _ = xin.shape
    nc = S // CH
    G = _dn_group(S, 4)
    r8 = G * CH // 8

    def body(xp_ref, x_ref, cw_ref, ba_ref, al_ref, dt_ref, cq_ref, u_ref, w_ref, qd_ref, kd_ref, a_ref, t_ref, cd_ref):
        xp = jnp.where(pl.program_id(1) > 0, xp_ref[...], 0.0)
        cq = _silu(_conv_taps(jnp.concatenate([xp, x_ref[...]], axis=0), cw_ref[...], G * CH))
        cq_ref[...] = cq
        ops = _dn_stack(cq, ba_ref[...], al_ref[...], dt_ref[...], G)
        u, w, qd, kd, a_in, cd, T = _dn_prep(None, *ops)
        lane4 = lax.broadcasted_iota(jnp.int32, (1, DH), 1)
        for c in range(G):
            rows = slice(CH * c, CH * (c + 1))
            cdrow = jnp.zeros((1, DH), F32)
            for h in range(DH):
                n = DH * c + h
                lanes = slice(DK * h, DK * (h + 1))
                u_ref[rows, lanes] = u[n]
                w_ref[rows, lanes] = w[n]
                qd_ref[rows, lanes] = qd[n]
                kd_ref[rows, lanes] = kd[n]
                a_ref[rows, CH * h:CH * (h + 1)] = a_in[n]
                t_ref[rows, CH * h:CH * (h + 1)] = T[n]
                cdrow = cdrow + jnp.where(lane4 == h, cd[n], 0.0)
            cd_ref[c] = cdrow

    wide = jax.ShapeDtypeStruct((B, S, DNW), F32)
    sq = jax.ShapeDtypeStruct((B, S, DH * CH), F32)
    return pl.pallas_call(
        body, name="dn_prep_fwd", grid=(B, nc // G),
        out_shape=[jax.ShapeDtypeStruct((B, S, CONVW), F32), wide, wide, wide, wide, sq, sq,
                   jax.ShapeDtypeStruct((B, nc, 1, DH), F32)],
        in_specs=[pl.BlockSpec((None, 8, CONVW), lambda b, i: (b, jnp.maximum(i * r8 - 1, 0), 0)),
                  _rows(G * CH, CONVW), _full((CONV, CONVW)), _rows(G * CH, 2 * DH), _full((1, DH)), _full((1, DH))],
        out_specs=[_rows(G * CH, CONVW)] + [_rows(G * CH, DNW)] * 4 + [_rows(G * CH, DH * CH)] * 2
                  + [pl.BlockSpec((None, G, 1, DH), lambda b, i: (b, i, 0, 0))],
        compiler_params=_cparams(("parallel", "arbitrary")),
    )(xin, xin, conv_w, ba, a_log, dt_b)


def _dn_seq_specs(B, steps, gs, rev):
    at = (lambda i: steps - 1 - i) if rev else (lambda i: i)
    wide = pl.BlockSpec((B, gs * CH, DNW), lambda i: (0, at(i), 0))
    a_spec = pl.BlockSpec((B, gs * CH, DH * CH), lambda i: (0, at(i), 0))
    cd_spec = pl.BlockSpec((B, gs, 1, DH), lambda i: (0, at(i), 0, 0))
    st_spec = pl.BlockSpec((B, gs, DH, DK, DK), lambda i: (0, at(i), 0, 0, 0))
    return wide, a_spec, cd_spec, st_spec


def _dn_step_operands(B, c, u_ref, w_ref, qd_ref, kd_ref, a_ref, cd_ref):
    pairs = [(b, h) for b in range(B) for h in range(DH)]
    rows = slice(CH * c, CH * (c + 1))
    wide = lambda ref: jnp.stack([ref[b, rows, DK * h:DK * (h + 1)] for b, h in pairs])
    a_in = jnp.stack([a_ref[b, rows, CH * h:CH * (h + 1)] for b, h in pairs])
    cd = jnp.stack([cd_ref[b, c, :, h:h + 1] for b, h in pairs])
    return wide(u_ref), wide(w_ref), wide(qd_ref), wide(kd_ref), a_in, cd


def _dn_seq_fwd(u, w, qd, kd, a_in, cd):
    B, S, _ = u.shape
    nc = S // CH
    gs = _dn_group(S, 8)

    def body(u_ref, w_ref, qd_ref, kd_ref, a_ref, cd_ref, o_ref, st_ref, state):
        @pl.when(pl.program_id(0) == 0)
        def _():
            state[...] = jnp.zeros_like(state)

        S0 = state[...]
        for c in range(gs):
            for b in range(B):
                st_ref[b, c] = S0[DH * b:DH * (b + 1)]
            o, S0 = _dn_step(S0, *_dn_step_operands(B, c, u_ref, w_ref, qd_ref, kd_ref, a_ref, cd_ref))
            for b in range(B):
                for h in range(DH):
                    o_ref[b, CH * c:CH * (c + 1), DK * h:DK * (h + 1)] = o[DH * b + h]
        state[...] = S0

    wide, a_spec, cd_spec, st_spec = _dn_seq_specs(B, nc // gs, gs, False)
    return pl.pallas_call(
        body, name="dn_seq_fwd", grid=(nc // gs,),
        out_shape=[jax.ShapeDtypeStruct((B, S, DNW), F32), jax.ShapeDtypeStruct((B, nc, DH, DK, DK), F32)],
        in_specs=[wide, wide, wide, wide, a_spec, cd_spec],
        out_specs=[wide, st_spec],
        scratch_shapes=[pltpu.VMEM((B * DH, DK, DK), F32)],
        compiler_params=_cparams(("arbitrary",)),
    )(u, w, qd, kd, a_in, cd)


def _dn_seq_bwd(u, w, qd, kd, a_in, cd, states, do):
    B, S, _ = u.shape
    nc = S // CH
    gs = _dn_group(S, 8)

    def body(u_ref, w_ref, qd_ref, kd_ref, a_ref, cd_ref, st_ref, do_ref,
             du_ref, dw_ref, dqd_ref, dkd_ref, da_ref, dcd_ref, dstate):
        @pl.when(pl.program_id(0) == 0)
        def _():
            dstate[...] = jnp.zeros_like(dstate)

        lane4 = lax.broadcasted_iota(jnp.int32, (1, DH), 1)
        dS = dstate[...]
        for c in reversed(range(gs)):
            rows = slice(CH * c, CH * (c + 1))
            S0 = jnp.concatenate([st_ref[b, c] for b in range(B)], axis=0)
            do = jnp.stack([do_ref[b, rows, DK * h:DK * (h + 1)] for b in range(B) for h in range(DH)])
            _, vjp = jax.vjp(_dn_step, S0, *_dn_step_operands(B, c, u_ref, w_ref, qd_ref, kd_ref, a_ref, cd_ref))
            dS, du, dw, dqd, dkd, da, dcd = vjp((do, dS))
            for b in range(B):
                dcdrow = jnp.zeros((1, DH), F32)
                for h in range(DH):
                    n = DH * b + h
                    lanes = slice(DK * h, DK * (h + 1))
                    du_ref[b, rows, lanes] = du[n]
                    dw_ref[b, rows, lanes] = dw[n]
                    dqd_ref[b, rows, lanes] = dqd[n]
                    dkd_ref[b, rows, lanes] = dkd[n]
                    da_ref[b, rows, CH * h:CH * (h + 1)] = da[n]
                    dcdrow = dcdrow + jnp.where(lane4 == h, dcd[n], 0.0)
                dcd_ref[b, c] = dcdrow
        dstate[...] = dS

    wide, a_spec, cd_spec, st_spec = _dn_seq_specs(B, nc // gs, gs, True)
    sd = jax.ShapeDtypeStruct((B, S, DNW), F32)
    return pl.pallas_call(
        body, name="dn_seq_bwd", grid=(nc // gs,),
        out_shape=[sd, sd, sd, sd, jax.ShapeDtypeStruct((B, S, DH * CH), F32), jax.ShapeDtypeStruct((B, nc, 1, DH), F32)],
        in_specs=[wide, wide, wide, wide, a_spec, cd_spec, st_spec, wide],
        out_specs=[wide, wide, wide, wide, a_spec, cd_spec],
        scratch_shapes=[pltpu.VMEM((B * DH, DK, DK), F32)],
        compiler_params=_cparams(("arbitrary",)),
    )(u, w, qd, kd, a_in, cd, states, do)


def _dn_prep_bwd(cq, ba, a_log, dt_b, t_inv, du, dw, dqd, dkd, da, dcd):
    B, S, _ = cq.shape
    nc = S // CH
    G = _dn_group(S, 4)

    def body(cq_ref, ba_ref, al_ref, dt_ref, t_ref, du_ref, dw_ref, dqd_ref, dkd_ref, da_ref, dcd_ref,
             dcq_ref, dba_ref, dal_ref, ddt_ref):
        @pl.when((pl.program_id(0) == 0) & (pl.program_id(1) == 0))
        def _():
            dal_ref[...] = jnp.zeros_like(dal_ref)
            ddt_ref[...] = jnp.zeros_like(ddt_ref)

        pairs = [(c, h) for c in range(G) for h in range(DH)]
        rows = lambda c: slice(CH * c, CH * (c + 1))
        wide = lambda ref: jnp.stack([ref[rows(c), DK * h:DK * (h + 1)] for c, h in pairs])
        square = lambda ref: jnp.stack([ref[rows(c), CH * h:CH * (h + 1)] for c, h in pairs])
        ops = _dn_stack(cq_ref[...], ba_ref[...], al_ref[...], dt_ref[...], G)
        cots = (wide(du_ref), wide(dw_ref), wide(dqd_ref), wide(dkd_ref), square(da_ref),
                jnp.stack([dcd_ref[c][:, h:h + 1] for c, h in pairs]), jnp.zeros((len(pairs), CH, CH), F32))
        _, vjp = jax.vjp(functools.partial(_dn_prep, square(t_ref)), *ops)
        dq, dk, dv, dar, dbr, dl, dd = vjp(cots)
        lane8 = lax.broadcasted_iota(jnp.int32, (CH, 2 * DH), 1)
        lane4 = lax.broadcasted_iota(jnp.int32, (1, DH), 1)
        dal = jnp.zeros((1, DH), F32)
        ddt = jnp.zeros((1, DH), F32)
        for c in range(G):
            dba = jnp.zeros((CH, 2 * DH), F32)
            for h in range(DH):
                n = DH * c + h
                dcq_ref[rows(c), DK * h:DK * (h + 1)] = dq[n]
                dcq_ref[rows(c), DNW + DK * h:DNW + DK * (h + 1)] = dk[n]
                dcq_ref[rows(c), 2 * DNW + DK * h:2 * DNW + DK * (h + 1)] = dv[n]
                dba = dba + jnp.where(lane8 == h, dbr[n], 0.0) + jnp.where(lane8 == DH + h, dar[n], 0.0)
                dal = dal + jnp.where(lane4 == h, dl[n], 0.0)
                ddt = ddt + jnp.where(lane4 == h, dd[n], 0.0)
            dba_ref[rows(c), :] = dba.astype(BF16)
        dal_ref[...] += dal
        ddt_ref[...] += ddt

    return pl.pallas_call(
        body, name="dn_prep_bwd", grid=(B, nc // G),
        out_shape=[jax.ShapeDtypeStruct((B, S, CONVW), F32), jax.ShapeDtypeStruct((B, S, 2 * DH), BF16),
                   jax.ShapeDtypeStruct((1, DH), F32), jax.ShapeDtypeStruct((1, DH), F32)],
        in_specs=[_rows(G * CH, CONVW), _rows(G * CH, 2 * DH), _full((1, DH)), _full((1, DH)), _rows(G * CH, DH * CH)]
                 + [_rows(G * CH, DNW)] * 4 + [_rows(G * CH, DH * CH),
                                               pl.BlockSpec((None, G, 1, DH), lambda b, i: (b, i, 0, 0))],
        out_specs=[_rows(G * CH, CONVW), _rows(G * CH, 2 * DH), _full((1, DH)), _full((1, DH))],
        compiler_params=_cparams(("arbitrary", "arbitrary")),
    )(cq, ba, a_log, dt_b, t_inv, du, dw, dqd, dkd, da, dcd)


def _gated_norm(o, z, g):
    outs = []
    for h in range(DH):
        t = o[:, DK * h:DK * (h + 1)]
        r = lax.rsqrt(jnp.mean(t * t, axis=-1, keepdims=True) + EPS)
        outs.append(t * r * g * _silu(z[:, DK * h:DK * (h + 1)]))
    return jnp.concatenate(outs, axis=1)


def _mix_fwd(x, o_attn, o_dn, z, ga, gd, mod, dn_g, w_branch, w_out):
    B, S, _ = x.shape
    tm = _tile(S, 512)

    def body(x_ref, oa_ref, od_ref, z_ref, ga_ref, gd_ref, mod_ref, g_ref, wb_ref, wo_ref,
             x1_ref, mix_ref, mg_ref, ob_ref):
        oa = oa_ref[...].astype(BF16)
        od = _gated_norm(od_ref[...], z_ref[...], g_ref[...]).astype(BF16)
        ob_ref[0] = oa
        ob_ref[1] = od
        ya = jnp.dot(oa, wb_ref[0:QW, :], preferred_element_type=F32)
        yd = jnp.dot(od, wb_ref[QW:QW + DNW, :], preferred_element_type=F32)
        merged = (_sigmoid(ga_ref[...]) * ya + _sigmoid(gd_ref[...]) * yd).astype(BF16)
        mg_ref[...] = merged
        mix = jnp.dot(merged, wo_ref[...], preferred_element_type=F32)
        mix_ref[...] = mix
        x1_ref[...] = x_ref[...] + mod_ref[2:3, :] * mix

    return pl.pallas_call(
        body, name="mix_fwd", grid=(B, S // tm),
        out_shape=[jax.ShapeDtypeStruct((B, S, D), F32), jax.ShapeDtypeStruct((B, S, D), F32),
                   jax.ShapeDtypeStruct((B, S, D), BF16), jax.ShapeDtypeStruct((B, 2, S, QW), BF16)],
        in_specs=[_rows(tm, D), _rows(tm, QW), _rows(tm, DNW), _rows(tm, DNW), _rows(tm, D), _rows(tm, D),
                  _perb(6, D), _full((1, DK)), _resident(w_branch.shape), _resident(w_out.shape)],
        out_specs=[_rows(tm, D), _rows(tm, D), _rows(tm, D), _stacked(2, tm, QW)],
        compiler_params=_cparams(("parallel", "arbitrary")),
    )(x, o_attn, o_dn, z, ga, gd, mod, dn_g, w_branch, w_out)


def _mix_bwd(dx1, mix, o_attn, o_dn, z, ga, gd, mod, dn_g, w_branch, w_out):
    B, S, _ = dx1.shape
    tm = _tile(S)

    def body(dx1_ref, mix_ref, oa_ref, od_ref, z_ref, ga_ref, gd_ref, mod_ref, g_ref, wb_ref, wo_ref,
             dmix_ref, dyo_ref, dga_ref, dgd_ref, dz_ref, doa_ref, dod_ref, dgate_ref, dg_ref):
        b, i = pl.program_id(0), pl.program_id(1)
        dx1 = dx1_ref[...]
        dmix = (dx1 * mod_ref[2:3, :]).astype(BF16)
        dmix_ref[...] = dmix
        dgate = jnp.sum(dx1 * mix_ref[...], axis=0, keepdims=True)
        dmerged = _dot_nt(dmix, wo_ref[...])
        odn, gn_vjp = jax.vjp(_gated_norm, od_ref[...], z_ref[...], g_ref[...])
        ya = _dot(oa_ref[...], wb_ref[0:QW, :])
        yd = _dot(odn, wb_ref[QW:QW + DNW, :])
        sa, sd = _sigmoid(ga_ref[...]), _sigmoid(gd_ref[...])
        dya = (dmerged * sa).astype(BF16)
        dyd = (dmerged * sd).astype(BF16)
        dyo_ref[0] = dya
        dyo_ref[1] = dyd
        dga_ref[...] = (dmerged * ya * sa * (1.0 - sa)).astype(BF16)
        dgd_ref[...] = (dmerged * yd * sd * (1.0 - sd)).astype(BF16)
        doa_ref[...] = _dot_nt(dya, wb_ref[0:QW, :])
        dodn = _dot_nt(dyd, wb_ref[QW:QW + DNW, :])
        dod, dz, dg = gn_vjp(dodn)
        dod_ref[...] = dod
        dz_ref[...] = dz.astype(BF16)

        @pl.when(i == 0)
        def _():
            dgate_ref[...] = jnp.zeros_like(dgate_ref)

        @pl.when((b == 0) & (i == 0))
        def _():
            dg_ref[...] = jnp.zeros_like(dg_ref)

        dgate_ref[...] += dgate
        dg_ref[...] += dg

    return pl.pallas_call(
        body, name="mix_bwd", grid=(B, S // tm),
        out_shape=[jax.ShapeDtypeStruct((B, S, D), BF16), jax.ShapeDtypeStruct((B, 2, S, D), BF16),
                   jax.ShapeDtypeStruct((B, S, D), BF16), jax.ShapeDtypeStruct((B, S, D), BF16),
                   jax.ShapeDtypeStruct((B, S, DNW), BF16),
                   jax.ShapeDtypeStruct((B, S, QW), F32), jax.ShapeDtypeStruct((B, S, DNW), F32),
                   jax.ShapeDtypeStruct((B, 1, D), F32), jax.ShapeDtypeStruct((1, DK), F32)],
        in_specs=[_rows(tm, D), _rows(tm, D), _rows(tm, QW), _rows(tm, DNW), _rows(tm, DNW), _rows(tm, D),
                  _rows(tm, D), _perb(6, D), _full((1, DK)), _resident(w_branch.shape), _resident(w_out.shape)],
        out_specs=[_rows(tm, D), _stacked(2, tm, D), _rows(tm, D), _rows(tm, D), _rows(tm, DNW),
                   _rows(tm, QW), _rows(tm, DNW), _perb(1, D), _full((1, DK))],
        compiler_params=_cparams(("arbitrary", "arbitrary")),
    )(dx1, mix, o_attn, o_dn, z, ga, gd, mod, dn_g, w_branch, w_out)


GU_SHARD = 2 * FFN // N_DEV
GU_HALF = N_DEV // 2


def _ffn1_fwd(x1, mod, g2, w_gu):
    B, S, _ = x1.shape
    tm = _tile(S)

    def body(x_ref, mod_ref, g_ref, w_ref, h_ref, gate_ref, up_ref, act_ref):
        h = _rms_mod(x_ref[...], g_ref[...], mod_ref[4:5, :], mod_ref[3:4, :]).astype(BF16)
        h_ref[...] = h
        for j in range(GU_HALF):
            gate = _dot_nt(h, w_ref[j])
            up = _dot_nt(h, w_ref[GU_HALF + j])
            gate_ref[j] = gate
            up_ref[j] = up
            act_ref[j] = (_silu(gate) * up).astype(BF16)

    blk = lambda dt: jax.ShapeDtypeStruct((B, GU_HALF, S, GU_SHARD), dt)
    return pl.pallas_call(
        body, name="ffn1_fwd", grid=(B, S // tm),
        out_shape=[jax.ShapeDtypeStruct((B, S, D), BF16), blk(F32), blk(F32), blk(BF16)],
        in_specs=[_rows(tm, D), _perb(6, D), _full((1, D)), _resident(w_gu.shape)],
        out_specs=[_rows(tm, D)] + [_stacked(GU_HALF, tm, GU_SHARD)] * 3,
        compiler_params=_cparams(("parallel", "arbitrary")),
    )(x1, mod, g2, w_gu)


def _ffn2_fwd(act, x1, target, mod, w_down):
    B, S, _ = x1.shape
    tm = _tile(S, 512)

    def body(a_ref, x_ref, t_ref, mod_ref, w_ref, dy_ref, loss_ref, dgate_ref):
        b, i = pl.program_id(0), pl.program_id(1)
        y = jnp.dot(a_ref[0], w_ref[0], preferred_element_type=F32)
        for j in range(1, GU_HALF):
            y = y + jnp.dot(a_ref[j], w_ref[j], preferred_element_type=F32)
        err = x_ref[...] + mod_ref[5:6, :] * y - t_ref[...]
        dy = err * (1.0 / D)
        dy_ref[...] = dy

        @pl.when((b == 0) & (i == 0))
        def _():
            loss_ref[...] = jnp.zeros_like(loss_ref)

        @pl.when(i == 0)
        def _():
            dgate_ref[...] = jnp.zeros_like(dgate_ref)

        loss_ref[...] += (0.5 / D) * jnp.sum(err * err)
        dgate_ref[...] += jnp.sum(dy * y, axis=0, keepdims=True)

    return pl.pallas_call(
        body, name="ffn2_fwd", grid=(B, S // tm),
        out_shape=[jax.ShapeDtypeStruct((B, S, D), F32), jax.ShapeDtypeStruct((1, 128), F32),
                   jax.ShapeDtypeStruct((B, 1, D), F32)],
        in_specs=[_stacked(GU_HALF, tm, GU_SHARD), _rows(tm, D), _rows(tm, D), _perb(6, D), _resident(w_down.shape)],
        out_specs=[_rows(tm, D), _full((1, 128)), _perb(1, D)],
        compiler_params=_cparams(("arbitrary", "arbitrary")),
    )(act, x1, target, mod, w_down)


def _ffn2_bwd(dy, gate, up, mod, w_down):
    B, S, _ = dy.shape
    tm = _tile(S)

    def body(dy_ref, gate_ref, up_ref, mod_ref, w_ref, dgu_ref, dyg_ref):
        dyg = (dy_ref[...] * mod_ref[5:6, :]).astype(BF16)
        dyg_ref[...] = dyg
        for j in range(GU_HALF):
            dact = _dot_nt(dyg, w_ref[j])
            gate, up = gate_ref[j], up_ref[j]
            sg = _sigmoid(gate)
            dgu_ref[j] = (dact * up * (sg * (1.0 + gate * (1.0 - sg)))).astype(BF16)
            dgu_ref[GU_HALF + j] = (dact * (gate * sg)).astype(BF16)

    return pl.pallas_call(
        body, name="ffn2_bwd", grid=(B, S // tm),
        out_shape=[jax.ShapeDtypeStruct((B, N_DEV, S, GU_SHARD), BF16), jax.ShapeDtypeStruct((B, S, D), BF16)],
        in_specs=[_rows(tm, D), _stacked(GU_HALF, tm, GU_SHARD), _stacked(GU_HALF, tm, GU_SHARD), _perb(6, D),
                  _resident(w_down.shape)],
        out_specs=[_stacked(N_DEV, tm, GU_SHARD), _rows(tm, D)],
        compiler_params=_cparams(("parallel", "arbitrary")),
    )(dy, gate, up, mod, w_down)


def _ffn1_bwd(dgu, x1, dy, mod, g2, w_gu):
    B, S, _ = x1.shape
    tm = _tile(S, 512)

    def body(dgu_ref, x_ref, dy_ref, mod_ref, g_ref, w_ref, dx1_ref, dg_ref, dsc_ref, dsh_ref):
        b, i = pl.program_id(0), pl.program_id(1)
        dh = jnp.dot(dgu_ref[0], w_ref[0], preferred_element_type=F32)
        for j in range(1, N_DEV):
            dh = dh + jnp.dot(dgu_ref[j], w_ref[j], preferred_element_type=F32)
        _, vjp = jax.vjp(_rms_mod, x_ref[...], g_ref[...], mod_ref[4:5, :], mod_ref[3:4, :])
        dx, dg, dsc, dsh = vjp(dh)
        dx1_ref[...] = dy_ref[...] + dx

        @pl.when((b == 0) & (i == 0))
        def _():
            dg_ref[...] = jnp.zeros_like(dg_ref)

        @pl.when(i == 0)
        def _():
            dsc_ref[...] = jnp.zeros_like(dsc_ref)
            dsh_ref[...] = jnp.zeros_like(dsh_ref)

        dg_ref[...] += dg
        dsc_ref[...] += dsc
        dsh_ref[...] += dsh

    return pl.pallas_call(
        body, name="ffn1_bwd", grid=(B, S // tm),
        out_shape=[jax.ShapeDtypeStruct((B, S, D), F32), jax.ShapeDtypeStruct((1, D), F32),
                   jax.ShapeDtypeStruct((B, 1, D), F32), jax.ShapeDtypeStruct((B, 1, D), F32)],
        in_specs=[_stacked(N_DEV, tm, GU_SHARD), _rows(tm, D), _rows(tm, D), _perb(6, D), _full((1, D)),
                  _resident(w_gu.shape)],
        out_specs=[_rows(tm, D), _full((1, D)), _perb(1, D), _perb(1, D)],
        compiler_params=_cparams(("arbitrary", "arbitrary")),
    )(dgu, x1, dy, mod, g2, w_gu)


def _adamw(w, g, m, v, name):
    def body(w_ref, g_ref, m_ref, v_ref, d_ref, nm_ref, nv_ref):
        d_ref[...], nm_ref[...], nv_ref[...] = _adamw_math(w_ref[...], g_ref[...], m_ref[...], v_ref[...])

    sd = jax.ShapeDtypeStruct(w.shape, F32)
    return pl.pallas_call(body, name=name, out_shape=(sd, sd, sd), compiler_params=_cparams())(w, g, m, v)


def kernel(x, c, positions, ada_w, ada_b, norm1_g, w_in, conv_w, q_norm_g, k_norm_g, sinks, a_log, dt_bias, dn_norm_g, w_branch, w_out, norm2_g, w_gate_up, w_down, loss_target, m_ada_w, m_ada_b, m_norm1_g, m_w_in, m_conv_w, m_q_norm_g, m_k_norm_g, m_sinks, m_a_log, m_dt_bias, m_dn_norm_g, m_w_branch, m_w_out, m_norm2_g, m_w_gate_up, m_w_down, v_ada_w, v_ada_b, v_norm1_g, v_w_in, v_conv_w, v_q_norm_g, v_k_norm_g, v_sinks, v_a_log, v_dt_bias, v_dn_norm_g, v_w_branch, v_w_out, v_norm2_g, v_w_gate_up, v_w_down):
    B, S, _ = x.shape
    me = 4 * lax.axis_index("x") + 2 * lax.axis_index("y") + lax.axis_index("c")

    tr = lambda t: jnp.swapaxes(t, 1, 2)
    shards = [w[0].astype(BF16) for w in (tr(w_in), w_branch, w_out, tr(w_gate_up), w_down)]

    c_all = _all_gather_small(c, "gather_c").reshape(N_DEV * B, D)
    ncol = 6 * D // N_DEV
    mod_cols, cond_all = _ada_fwd(c_all, ada_w[0], lax.dynamic_slice(ada_b, (0, me * ncol), (1, ncol)))
    mod_all = _all_gather_small(mod_cols, "gather_mod").transpose(1, 0, 2).reshape(N_DEV * B, 6 * D)
    mod = lax.dynamic_slice(mod_all, (me * B, 0), (B, 6 * D)).reshape(B, 6, D)
    conv2 = conv_w.reshape(CONV, CONVW // N_DEV)
    conv_all = _all_gather_small(conv2, "gather_conv").transpose(1, 0, 2).reshape(CONV, CONVW)

    (w_in_b,) = _all_gather_big(shards[:1], "gather_w_in", after=(mod, conv_all))
    w_sems, w_srcs, w_lands, w_token = _copies_start(shards[1:], [_place_own(s, me) for s in shards[1:]], False,
                                                    w_in_b, "gather_rest_start")

    h1, aq, akv, dnx, ba, z, ga, gd = _inproj_fwd(x, mod, norm1_g + w_token[0, 0], w_in_b)
    invf, mean_q, mean_k = _attn_consts()
    rope_cos, rope_sin = _rope_tables(positions.reshape(B, S, 1), invf)
    o_attn = _attn_fwd(aq, akv, rope_cos, rope_sin, q_norm_g, k_norm_g, sinks, mean_q, mean_k)
    cq, dn_u, dn_w, dn_qd, dn_kd, dn_a, dn_t, dn_cd = _dn_prep_fwd(dnx, conv_all, ba, a_log, dt_bias)
    o_dn, states = _dn_seq_fwd(dn_u, dn_w, dn_qd, dn_kd, dn_a, dn_cd)
    w_branch_g, w_out_g, w_gu_b, w_down_g = _copies_wait(w_sems, w_srcs, w_lands, o_dn, "gather_wait_rest")
    w_branch_f = w_branch_g.reshape(D, D)
    w_out_f = w_out_g.reshape(D, D)
    w_down_b = w_down_g.reshape(GU_HALF, GU_SHARD, D)
    x1, mix, merged, ob = _mix_fwd(x, o_attn, o_dn, z, ga, gd, mod, dn_norm_g, w_branch_f, w_out_f)
    h2, gate, up, act = _ffn1_fwd(x1, mod, norm2_g, w_gu_b)
    dy, loss_part, d_gate2 = _ffn2_fwd(act, x1, loss_target, mod, w_down_b)
    loss = lax.psum(loss_part[0, 0], ("x", "y", "c"))

    one = lambda t: t.reshape(B, 1, S, t.shape[-1])
    dgu, dyg = _ffn2_bwd(dy, gate, up, mod, w_down_b)
    g_w_down = _wgrad(act, one(dyg), "wgrad_down")
    dx1, d_n2g, d_scale2, d_shift2 = _ffn1_bwd(dgu, x1, dy, mod, norm2_g, w_gu_b)
    g_w_gu = _wgrad(dgu, one(h2), "wgrad_gate_up")
    ffn = _exchange_start([g_w_gu, g_w_down.reshape(N_DEV, FFN // N_DEV, D)], me, dx1, "exchange_ffn_start")
    dmix, dyo, dga, dgd, dz, d_oa, d_od, d_gate1, d_dng = _mix_bwd(
        dx1, mix, o_attn, o_dn, z, ga, gd, mod, dn_norm_g + ffn[3][0, 0], w_branch_f, w_out_f)
    d_dn = _dn_seq_bwd(dn_u, dn_w, dn_qd, dn_kd, dn_a, dn_cd, states, d_od)
    dcq, dba, d_alog, d_dtb = _dn_prep_bwd(cq, ba, a_log, dt_bias, dn_t, *d_dn)
    ddnx, d_conv = _conv_bwd(dnx, conv_all, dcq)
    daq, dakv, d_qg, d_kg, d_sinks = _attn_bwd(aq, akv, rope_cos, rope_sin, q_norm_g, k_norm_g, sinks, mean_q, mean_k, d_oa)
    dps = [daq, dakv, ddnx, dba, dz, dga, dgd]
    dblk, grad_x, d_n1g, d_scale1, d_shift1 = _inproj_bwd(x, mod, norm1_g, dx1, dps, w_in_b)

    dmod = jnp.concatenate([d_shift1, d_scale1, d_gate1, d_shift2, d_scale2, d_gate2], axis=2).reshape(B, 6 * D)
    small = jnp.concatenate([d_n1g, d_qg, d_kg, d_sinks, d_alog, d_dtb, d_dng, d_n2g, d_conv.reshape(1, CONV * CONVW)], axis=1)
    nsm = small.shape[1]
    width = -(-max(6 * D, nsm) // 128) * 128
    rows = jnp.concatenate([jnp.pad(dmod, ((0, 0), (0, width - 6 * D))), jnp.pad(small, ((0, 8 - B - 1), (0, width - nsm)))], axis=0)
    rows_all = _all_gather_small(rows, "gather_small")
    dmod_all = rows_all[:, 0:B, 0:6 * D].reshape(N_DEV * B, 6 * D)
    dmod_cols = lax.dynamic_slice(dmod_all, (0, me * ncol), (N_DEV * B, ncol))
    grad_ada_w, grad_ada_b, small_sum = _ada_bwd(cond_all, dmod_all, dmod_cols, rows_all[:, B, :])
    sizes = [D, HD, HD, HQ, DH, DH, DK, D]
    so = np.cumsum([0] + sizes)
    g_n1, g_qg, g_kg, g_sk, g_al, g_dt, g_dn, g_n2 = [small_sum[:, so[i]:so[i + 1]] for i in range(8)]
    g_conv_all = small_sum[:, so[8]:so[8] + CONV * CONVW].reshape(CONV, N_DEV, CONVW // N_DEV)
    grad_conv = lax.dynamic_slice(g_conv_all, (0, me, 0), (CONV, 1, CONVW // N_DEV)).reshape(CONV, CONVW // N_DEV)

    g_w_in = _wgrad(dblk, one(h1), "wgrad_in", after=small_sum)
    proj = _exchange_start([g_w_in], me, small_sum, "exchange_in_start")
    g_w_out = _wgrad(one(merged), one(dmix), "wgrad_out", after=proj[3])
    g_w_branch = _wgrad(ob, dyo, "wgrad_branch", after=proj[3])
    mixer = _exchange_start([g_w_branch.reshape(N_DEV, D // N_DEV, D), g_w_out.reshape(N_DEV, D // N_DEV, D)], me,
                            proj[3], "exchange_mix_start")

    upd, grads = {}, {}

    def finish(names, parts, weights):
        for nm, p, (w, m, v) in zip(names, parts, weights):
            grads[nm], *upd[nm] = _sum_adamw(p, w, m, v, "update_" + nm)

    finish(["w_gate_up", "w_down"], _copies_wait(*ffn[:3], mixer[3], "exchange_ffn_wait"),
           [(tr(w_gate_up), tr(m_w_gate_up), tr(v_w_gate_up)), (w_down, m_w_down, v_w_down)])
    finish(["w_in"], _copies_wait(*proj[:3], grads["w_gate_up"], "exchange_in_wait"),
           [(tr(w_in), tr(m_w_in), tr(v_w_in))])
    finish(["w_branch", "w_out"], _copies_wait(*mixer[:3], grads["w_in"], "exchange_mix_wait"),
           [(w_branch, m_w_branch, v_w_branch), (w_out, m_w_out, v_w_out)])
    for nm in ("w_in", "w_gate_up"):
        grads[nm], upd[nm] = tr(grads[nm]), [tr(t) for t in upd[nm]]

    grads["ada_w"] = grad_ada_w.reshape(ada_w.shape)
    upd["ada_w"] = _adamw(ada_w, grads["ada_w"], m_ada_w, v_ada_w, "adamw_ada_w")
    small_names = ["ada_b", "norm1_g", "q_norm_g", "k_norm_g", "sinks", "a_log", "dt_bias", "dn_norm_g", "norm2_g", "conv_w"]
    small_w = [ada_b, norm1_g, q_norm_g, k_norm_g, sinks, a_log, dt_bias, dn_norm_g, norm2_g, conv_w]
    small_g = [grad_ada_b, g_n1, g_qg, g_kg, g_sk, g_al, g_dt, g_dn, g_n2, grad_conv]
    small_m = [m_ada_b, m_norm1_g, m_q_norm_g, m_k_norm_g, m_sinks, m_a_log, m_dt_bias, m_dn_norm_g, m_norm2_g, m_conv_w]
    small_v = [v_ada_b, v_norm1_g, v_q_norm_g, v_k_norm_g, v_sinks, v_a_log, v_dt_bias, v_dn_norm_g, v_norm2_g, v_conv_w]
    cat = lambda arrs: jnp.concatenate([a.reshape(1, -1) for a in arrs], axis=1)
    res = _adamw(cat(small_w), cat(small_g), cat(small_m), cat(small_v), "adamw_small")
    po = np.cumsum([0] + [int(np.prod(w.shape)) for w in small_w])
    for i, nm in enumerate(small_names):
        upd[nm] = tuple(r[:, po[i]:po[i + 1]].reshape(small_w[i].shape) for r in res)
        grads[nm] = small_g[i].reshape(small_w[i].shape)

    order = ["ada_w", "ada_b", "norm1_g", "w_in", "conv_w", "q_norm_g", "k_norm_g", "sinks", "a_log", "dt_bias",
             "dn_norm_g", "w_branch", "w_out", "norm2_g", "w_gate_up", "w_down"]
    return (loss, grad_x, *[grads[n] for n in order], *[upd[n][0] for n in order],
            *[upd[n][1] for n in order], *[upd[n][2] for n in order])
```

```python
import functools

import numpy as np
import jax
import jax.numpy as jnp
from jax import lax
from jax.experimental import pallas as pl
from jax.experimental.pallas import tpu as pltpu

F32 = jnp.float32
BF16 = jnp.bfloat16
HI = lax.Precision.HIGHEST

N_DEV = 8
D = 1024
HQ, HKV, HD = 8, 2, 64
GRP = HQ // HKV
BLK = 128
ROT = HD // 4
THETA = 500000.0
QW, KVW = HQ * HD, HKV * HD
DH, DK = 4, 128
CH = 64
DNW = DH * DK
CONV = 4
CONVW = 3 * DNW
FFN = 2816
EPS = 1e-6
IN_W = QW + 2 * KVW + CONVW + 2 * DH + DNW + 2 * D

LR, B1, B2, AEPS, WD, STEP = 0.001, 0.9, 0.999, 1e-08, 0.01, 10

VMEM_LIMIT = 56 * 1024 * 1024
MESH = pl.DeviceIdType.MESH


def _cparams(sem=None, vmem=VMEM_LIMIT):
    return pltpu.CompilerParams(dimension_semantics=sem, vmem_limit_bytes=vmem)


def _full(shape):
    n = len(shape)
    return pl.BlockSpec(shape, lambda *_: (0,) * n)


def _resident(shape):
    n = len(shape)
    return pl.BlockSpec(shape, lambda *_: (0,) * n, pipeline_mode=pl.Buffered(1))


def _rows(tm, w):
    return pl.BlockSpec((None, tm, w), lambda b, i: (b, i, 0))


def _stacked(n, tm, w):
    return pl.BlockSpec((None, n, tm, w), lambda b, i: (b, 0, i, 0))


def _perb(r, w):
    return pl.BlockSpec((None, r, w), lambda b, i: (b, 0, 0))


def _dot(a, b):
    return jnp.dot(a.astype(BF16), b.astype(BF16), preferred_element_type=F32)


def _dot_nt(a, b):
    return lax.dot_general(a.astype(BF16), b.astype(BF16), (((1,), (1,)), ((), ())), preferred_element_type=F32)


def _dot_tn(a, b):
    return lax.dot_general(a.astype(BF16), b.astype(BF16), (((0,), (0,)), ((), ())), preferred_element_type=F32)


def _dot_hi(a, b):
    return jnp.dot(a, b, preferred_element_type=F32, precision=HI)


def _sigmoid(x):
    return jax.nn.sigmoid(x)


def _silu(x):
    return x * jax.nn.sigmoid(x)


def _rms_mod(x, g, scale, shift):
    r = lax.rsqrt(jnp.mean(x * x, axis=-1, keepdims=True) + EPS)
    return (x * r * g) * (1.0 + scale) + shift


def _tile(S, rows=256):
    return min(rows, S)


def _peer(x, y, c, k):
    px = 1 - x if (k >> 2) & 1 else x
    py = 1 - y if (k >> 1) & 1 else y
    pc = 1 - c if k & 1 else c
    return px, py, pc


def _all_gather_small(v, name):
    r, n = v.shape

    def body(v_ref, out_ref, send_sems, recv_sems, local_sem):
        x, y, c = lax.axis_index("x"), lax.axis_index("y"), lax.axis_index("c")
        me = 4 * x + 2 * y + c
        mine = pltpu.make_async_copy(v_ref, out_ref.at[me], local_sem)
        mine.start()
        sends = []
        for k in range(1, N_DEV):
            cp = pltpu.make_async_remote_copy(
                src_ref=v_ref, dst_ref=out_ref.at[me], send_sem=send_sems.at[k - 1], recv_sem=recv_sems.at[k - 1],
                device_id=_peer(x, y, c, k), device_id_type=MESH)
            cp.start()
            sends.append(cp)
        for k in range(1, N_DEV):
            px, py, pc = _peer(x, y, c, k)
            pltpu.make_async_remote_copy(
                src_ref=v_ref, dst_ref=out_ref.at[4 * px + 2 * py + pc], send_sem=send_sems.at[k - 1],
                recv_sem=recv_sems.at[k - 1], device_id=(px, py, pc), device_id_type=MESH).wait_recv()
        for cp in sends:
            cp.wait_send()
        mine.wait()

    return pl.pallas_call(
        body, name=name,
        out_shape=jax.ShapeDtypeStruct((N_DEV, r, n), v.dtype),
        in_specs=[pl.BlockSpec(memory_space=pltpu.VMEM)],
        out_specs=pl.BlockSpec(memory_space=pltpu.VMEM),
        scratch_shapes=[pltpu.SemaphoreType.DMA((N_DEV - 1,)), pltpu.SemaphoreType.DMA((N_DEV - 1,)), pltpu.SemaphoreType.DMA],
    )(v)


def _all_gather_big(vs, name, after=()):
    na, nf = len(vs), len(after)

    def body(*refs):
        v_refs, out_refs = refs[:na], refs[na + nf:2 * na + nf]
        send_sems, recv_sems, local_sems = refs[2 * na + nf:]
        x, y, c = lax.axis_index("x"), lax.axis_index("y"), lax.axis_index("c")
        me, sibling = (x, y, c), (x, y, 1 - c)
        chips = [(1 - x, y), (x, 1 - y), (1 - x, 1 - y)]

        def rows(a, px, py, pc):
            return out_refs[a].at[4 * px + 2 * py + pc]

        def copy(a, k, block, to, src=None):
            return pltpu.make_async_remote_copy(
                src_ref=rows(a, *block) if src is None else src, dst_ref=rows(a, *block),
                send_sem=send_sems.at[7 * a + k], recv_sem=recv_sems.at[7 * a + k], device_id=to, device_id_type=MESH)

        mine = [pltpu.make_async_copy(v_refs[a], rows(a, *me), local_sems.at[a]) for a in range(na)]
        for cp in mine:
            cp.start()
        first = []
        for a in range(na):
            first.append(copy(a, 0, me, sibling, src=v_refs[a]))
            first += [copy(a, 1 + j, me, (*chip, c), src=v_refs[a]) for j, chip in enumerate(chips)]
        for cp in first:
            cp.start()
        passed = []
        for j, chip in enumerate(chips):
            for a in range(na):
                copy(a, 1 + j, (*chip, c), me).wait_recv()
                forward = copy(a, 4 + j, (*chip, c), sibling)
                forward.start()
                passed.append(forward)
        for a in range(na):
            copy(a, 0, sibling, me).wait_recv()
            for j, chip in enumerate(chips):
                copy(a, 4 + j, (*chip, 1 - c), me).wait_recv()
        for cp in first + passed:
            cp.wait_send()
        for cp in mine:
            cp.wait()

    return pl.pallas_call(
        body, name=name,
        out_shape=[jax.ShapeDtypeStruct((N_DEV,) + v.shape, v.dtype) for v in vs],
        in_specs=[pl.BlockSpec(memory_space=pl.ANY)] * (na + nf),
        out_specs=[pl.BlockSpec(memory_space=pl.ANY)] * na,
        scratch_shapes=[pltpu.SemaphoreType.DMA((7 * na,)), pltpu.SemaphoreType.DMA((7 * na,)),
                        pltpu.SemaphoreType.DMA((na,))],
    )(*vs, *after)


_HBM = pl.BlockSpec(memory_space=pltpu.HBM)
_SEM = pl.BlockSpec(memory_space=pltpu.SEMAPHORE)
_EFFECT = pltpu.SideEffectType.DATAFLOW_SIDE_EFFECTING


def _place_own(block, me):
    land = lax.empty((N_DEV,) + block.shape, block.dtype)
    return lax.dynamic_update_slice(land, block[None], (me,) + (0,) * block.ndim)


def _copies_start(srcs, lands, scatter, after, name):
    na = len(srcs)
    afters = tuple(after) if isinstance(after, (tuple, list)) else (after,)

    def body(*refs):
        src_refs, land_refs = refs[:na], refs[na:2 * na]
        sems = refs[2 * na + len(afters):4 * na + len(afters)]
        token = refs[-1]
        x, y, c = lax.axis_index("x"), lax.axis_index("y"), lax.axis_index("c")
        me = 4 * x + 2 * y + c
        for a in range(na):
            for k in range(1, N_DEV):
                px, py, pc = _peer(x, y, c, k)
                src = src_refs[a].at[4 * px + 2 * py + pc] if scatter else src_refs[a]
                pltpu.make_async_remote_copy(
                    src_ref=src, dst_ref=land_refs[a].at[me], send_sem=sems[2 * a], recv_sem=sems[2 * a + 1],
                    device_id=(px, py, pc), device_id_type=MESH).start()
        token[...] = jnp.zeros_like(token)

    hbm = lambda t: pltpu.HBM(t.shape, t.dtype)
    out = pl.pallas_call(
        body, name=name,
        out_shape=tuple([pltpu.SemaphoreType.DMA(())] * (2 * na) + [hbm(t) for t in srcs] + [hbm(t) for t in lands]
                        + [jax.ShapeDtypeStruct((8, 128), F32)]),
        in_specs=[_HBM] * (2 * na) + [pl.BlockSpec(memory_space=pl.ANY)] * len(afters),
        out_specs=tuple([_SEM] * (2 * na) + [_HBM] * (2 * na) + [pl.BlockSpec(memory_space=pltpu.VMEM)]),
        input_output_aliases={i: 2 * na + i for i in range(2 * na)},
        compiler_params=pltpu.CompilerParams(has_side_effects=_EFFECT),
    )(*[pltpu.with_memory_space_constraint(t, pltpu.HBM) for t in list(srcs) + list(lands)], *afters)
    return out[:2 * na], out[2 * na:3 * na], out[3 * na:4 * na], out[-1]


def _exchange_start(gs, me, after, name):
    own = [lax.dynamic_index_in_dim(g, me, 0, keepdims=False) for g in gs]
    return _copies_start(gs, [_place_own(o, me) for o in own], True, after, name)


def _copies_wait(sems, srcs, lands, after, name):
    na = len(srcs)

    def body(*refs):
        land_refs = refs[na:2 * na]
        sem_refs = refs[2 * na:4 * na]
        x, y, c = lax.axis_index("x"), lax.axis_index("y"), lax.axis_index("c")
        for a in range(na):
            seven = land_refs[a].at[pl.ds(0, N_DEV - 1)]
            copy = pltpu.make_async_remote_copy(
                src_ref=seven, dst_ref=seven, send_sem=sem_refs[2 * a], recv_sem=sem_refs[2 * a + 1],
                device_id=(x, y, c), device_id_type=MESH)
            copy.wait_send()
            copy.wait_recv()

    hbm = lambda t: pltpu.HBM(t.shape, t.dtype)
    out = pl.pallas_call(
        body, name=name,
        out_shape=tuple([hbm(t) for t in srcs] + [hbm(t) for t in lands]),
        in_specs=[_HBM] * (2 * na) + [_SEM] * (2 * na) + [pl.BlockSpec(memory_space=pl.ANY)],
        out_specs=tuple([_HBM] * (2 * na)),
        input_output_aliases={i: i for i in range(2 * na)},
        compiler_params=pltpu.CompilerParams(has_side_effects=_EFFECT),
    )(*srcs, *lands, *sems, after)
    return out[na:]


def _adamw_math(w, g, m, v):
    m = B1 * m + (1.0 - B1) * g
    v = B2 * v + (1.0 - B2) * (g * g)
    m_hat = m / (1.0 - B1 ** STEP)
    v_hat = v / (1.0 - B2 ** STEP)
    return -LR * (m_hat / (jnp.sqrt(v_hat) + AEPS) + WD * w), m, v


def _sum_adamw(parts, w, m, v, name):
    _, r, n = parts.shape
    tr = 256 if r % 256 == 0 else r

    def body(p_ref, w_ref, m_ref, v_ref, g_ref, d_ref, nm_ref, nv_ref):
        g = p_ref[0].astype(F32)
        for dev in range(1, N_DEV):
            g = g + p_ref[dev].astype(F32)
        g_ref[...] = g
        d_ref[...], nm_ref[...], nv_ref[...] = _adamw_math(w_ref[...], g, m_ref[...], v_ref[...])

    rows = pl.BlockSpec((None, tr, n), lambda i: (0, i, 0))
    sd = jax.ShapeDtypeStruct((1, r, n), F32)
    return pl.pallas_call(
        body, name=name, grid=(r // tr,), out_shape=(sd, sd, sd, sd),
        in_specs=[pl.BlockSpec((N_DEV, tr, n), lambda i: (0, i, 0)), rows, rows, rows],
        out_specs=(rows, rows, rows, rows),
        compiler_params=_cparams(("parallel",)),
    )(parts, w, m, v)


def _ada_fwd(c_all, ada_w, ada_b_cols):
    nb, ncol = c_all.shape[0], ada_w.shape[1]

    def body(c_ref, w_ref, b_ref, mod_ref, cond_ref):
        cond = _silu(c_ref[...])
        cond_ref[...] = cond
        mod_ref[...] = _dot_hi(cond, w_ref[...]) + b_ref[...]

    return pl.pallas_call(
        body, name="ada_fwd",
        out_shape=(jax.ShapeDtypeStruct((nb, ncol), F32), jax.ShapeDtypeStruct((nb, D), F32)),
        compiler_params=_cparams(),
    )(c_all, ada_w, ada_b_cols)


def _ada_bwd(cond_all, dmod_all, dmod_cols, smalls):
    ncol, nsm = dmod_cols.shape[1], smalls.shape[1]

    def body(cond_ref, dm_ref, dmc_ref, sm_ref, gw_ref, gb_ref, gs_ref):
        gw_ref[...] = lax.dot_general(cond_ref[...], dmc_ref[...], (((0,), (0,)), ((), ())),
                                      preferred_element_type=F32, precision=HI)
        gb_ref[...] = jnp.sum(dm_ref[...], axis=0, keepdims=True)
        gs_ref[...] = jnp.sum(sm_ref[...], axis=0, keepdims=True)

    return pl.pallas_call(
        body, name="ada_bwd",
        out_shape=(jax.ShapeDtypeStruct((D, ncol), F32), jax.ShapeDtypeStruct((1, 6 * D), F32),
                   jax.ShapeDtypeStruct((1, nsm), F32)),
        compiler_params=_cparams(),
    )(cond_all, dmod_all, dmod_cols, smalls)


IN_CUTS = (0, QW, QW + 2 * KVW, QW + 2 * KVW + CONVW, QW + 2 * KVW + CONVW + 2 * DH,
           QW + 2 * KVW + CONVW + 2 * DH + DNW, QW + 2 * KVW + CONVW + 2 * DH + DNW + D, IN_W)
IN_WIDTHS = tuple(b - a for a, b in zip(IN_CUTS[:-1], IN_CUTS[1:]))
IN_SHARD = IN_W // N_DEV


def _inproj_fwd(x, mod, g1, w_blk):
    B, S, _ = x.shape
    tm = _tile(S)

    def body(x_ref, mod_ref, g_ref, w_ref, h_ref, *o_refs):
        h = _rms_mod(x_ref[...], g_ref[...], mod_ref[1:2, :], mod_ref[0:1, :]).astype(BF16)
        h_ref[...] = h
        full = jnp.concatenate([_dot_nt(h, w_ref[j]) for j in range(N_DEV)], axis=1)
        for o_ref, lo, hi in zip(o_refs, IN_CUTS[:-1], IN_CUTS[1:]):
            o_ref[...] = full[:, lo:hi]

    return pl.pallas_call(
        body, name="inproj_fwd", grid=(B, S // tm),
        out_shape=[jax.ShapeDtypeStruct((B, S, D), BF16)] + [jax.ShapeDtypeStruct((B, S, w), F32) for w in IN_WIDTHS],
        in_specs=[_rows(tm, D), _perb(6, D), _full((1, D)), _resident(w_blk.shape)],
        out_specs=[_rows(tm, D)] + [_rows(tm, w) for w in IN_WIDTHS],
        compiler_params=_cparams(("parallel", "arbitrary")),
    )(x, mod, g1, w_blk)


def _inproj_bwd(x, mod, g1, dx1, dps, w_blk):
    B, S, _ = x.shape
    tm = _tile(S)
    n = len(dps)

    def body(x_ref, mod_ref, g_ref, dx1_ref, *refs):
        dp_refs, w_ref = refs[:n], refs[n]
        dblk_ref, gx_ref, dg_ref, dsc_ref, dsh_ref = refs[n + 1:]
        b, i = pl.program_id(0), pl.program_id(1)
        full = jnp.concatenate([r[...].astype(F32) for r in dp_refs], axis=1)
        dh = None
        for j in range(N_DEV):
            blk = full[:, IN_SHARD * j:IN_SHARD * (j + 1)].astype(BF16)
            dblk_ref[j] = blk
            t = jnp.dot(blk, w_ref[j], preferred_element_type=F32)
            dh = t if dh is None else dh + t
        _, vjp = jax.vjp(_rms_mod, x_ref[...], g_ref[...], mod_ref[1:2, :], mod_ref[0:1, :])
        dx, dg, dsc, dsh = vjp(dh)
        gx_ref[...] = dx1_ref[...] + dx

        @pl.when((b == 0) & (i == 0))
        def _():
            dg_ref[...] = jnp.zeros_like(dg_ref)

        @pl.when(i == 0)
        def _():
            dsc_ref[...] = jnp.zeros_like(dsc_ref)
            dsh_ref[...] = jnp.zeros_like(dsh_ref)

        dg_ref[...] += dg
        dsc_ref[...] += dsc
        dsh_ref[...] += dsh

    return pl.pallas_call(
        body, name="inproj_bwd", grid=(B, S // tm),
        out_shape=[jax.ShapeDtypeStruct((B, N_DEV, S, IN_SHARD), BF16), jax.ShapeDtypeStruct((B, S, D), F32),
                   jax.ShapeDtypeStruct((1, D), F32), jax.ShapeDtypeStruct((B, 1, D), F32),
                   jax.ShapeDtypeStruct((B, 1, D), F32)],
        in_specs=[_rows(tm, D), _perb(6, D), _full((1, D)), _rows(tm, D)]
                 + [_rows(tm, w) for w in IN_WIDTHS] + [_resident(w_blk.shape)],
        out_specs=[pl.BlockSpec((None, N_DEV, tm, IN_SHARD), lambda b, i: (b, 0, i, 0)), _rows(tm, D),
                   _full((1, D)), _perb(1, D), _perb(1, D)],
        compiler_params=_cparams(("arbitrary", "arbitrary")),
    )(x, mod, g1, dx1, *dps, w_blk)


def _wgrad(a, b, name, after=None):
    B, na, S, K = a.shape
    nb, N = b.shape[1], b.shape[3]
    G = max(na, nb)
    tm = min(2048, S)
    nt = S // tm
    last = B * nt - 1

    def body(a_ref, b_ref, *rest):
        o_ref, acc = rest[-2:]
        t = pl.program_id(1)

        @pl.when(t == 0)
        def _():
            acc[...] = jnp.zeros_like(acc)

        acc[...] += lax.dot_general(a_ref[...], b_ref[...], (((0,), (0,)), ((), ())), preferred_element_type=F32)

        @pl.when(t == last)
        def _():
            o_ref[...] = acc[...].astype(BF16)

    return pl.pallas_call(
        body, name=name, grid=(G, B * nt),
        out_shape=jax.ShapeDtypeStruct((G, K, N), BF16),
        in_specs=[pl.BlockSpec((None, None, tm, K), lambda g, t: (t // nt, g if na > 1 else 0, t % nt, 0)),
                  pl.BlockSpec((None, None, tm, N), lambda g, t: (t // nt, g if nb > 1 else 0, t % nt, 0))]
                 + ([] if after is None else [pl.BlockSpec(memory_space=pl.ANY)]),
        out_specs=pl.BlockSpec((None, K, N), lambda g, t: (g, 0, 0)),
        scratch_shapes=[pltpu.VMEM((K, N), F32)],
        compiler_params=_cparams(("parallel", "arbitrary")),
    )(*((a, b) if after is None else (a, b, after)))


LANES = 128


def _attn_consts():
    inv_freq = THETA ** (-jnp.arange(0, ROT, 2, dtype=F32) / ROT)
    head = jnp.concatenate([inv_freq, inv_freq, jnp.zeros((HD - ROT,), F32)])
    invf = jnp.tile(head, LANES // HD)[None, :]
    mean_of = lambda w: jnp.asarray(np.kron(np.eye(w // HD), np.full((HD, HD), 1.0 / HD)), BF16)
    return invf, mean_of(QW), mean_of(KVW)


def _rope_tables(pos, invf):
    B, S, _ = pos.shape
    tr = min(1024, S)

    def body(p_ref, f_ref, c_ref, s_ref):
        ang = p_ref[...].astype(F32) * f_ref[...]
        c_ref[...] = jnp.cos(ang)
        s_ref[...] = jnp.sin(ang)

    sd = jax.ShapeDtypeStruct((B, S, LANES), F32)
    return pl.pallas_call(
        body, name="rope_tables", grid=(B, S // tr), out_shape=[sd, sd],
        in_specs=[_rows(tr, 1), _full((1, LANES))], out_specs=[_rows(tr, LANES), _rows(tr, LANES)],
        compiler_params=_cparams(("parallel", "parallel")),
    )(pos, invf)


def _rope_expand(cos, sin, reps):
    lane = lax.broadcasted_iota(jnp.int32, cos.shape, 1) % HD
    sa = jnp.where((lane >= ROT // 2) & (lane < ROT), sin, 0.0)
    sb = jnp.where(lane < ROT // 2, -sin, 0.0)
    rep = lambda t: jnp.concatenate([t] * reps, axis=1) if reps > 1 else t
    return rep(cos), rep(sa), rep(sb)


@jax.custom_vjp
def _rope(t, cos, sa, sb):
    w = t.shape[1]
    return t * cos + pltpu.roll(t, ROT // 2, 1) * sa + pltpu.roll(t, w - ROT // 2, 1) * sb


def _rope_fwd(t, cos, sa, sb):
    return _rope(t, cos, sa, sb), (cos, sa, sb)


def _rope_bwd(res, d):
    cos, sa, sb = res
    w = d.shape[1]
    dt = d * cos + pltpu.roll(d * sa, w - ROT // 2, 1) + pltpu.roll(d * sb, ROT // 2, 1)
    return dt, jnp.zeros_like(cos), jnp.zeros_like(sa), jnp.zeros_like(sb)


_rope.defvjp(_rope_fwd, _rope_bwd)


def _head_norm(t, g, mean_of):
    hi, lo = _split(t * t)
    ms = jnp.dot(hi, mean_of, preferred_element_type=F32) + jnp.dot(lo, mean_of, preferred_element_type=F32)
    return t * lax.rsqrt(ms + EPS) * g


def _attn_block(q, kvp, kvc, qg, kg, sinks, tq, tk, mq, mk, valid):
    qn = _rope(_head_norm(q, jnp.concatenate([qg] * HQ, axis=1), mq), *tq) * (HD ** -0.5)
    kv = jnp.concatenate([kvp, kvc], axis=0)
    kn = _rope(_head_norm(kv[:, 0:KVW], jnp.concatenate([kg] * HKV, axis=1), mk), *tk)
    per_tile = LANES // HD
    vT = jnp.transpose(kv[:, KVW:2 * KVW])
    qT = [jnp.transpose(qn[:, LANES * t:LANES * (t + 1)]) for t in range(QW // LANES)]
    head_T = lambda h: qT[h // per_tile][HD * (h % per_tile):HD * (h % per_tile + 1), :]
    none = jnp.zeros((HD, GRP * BLK), F32)
    o_T = []
    for j in range(HKV):
        q4T = jnp.concatenate([head_T(GRP * j + i) for i in range(GRP)], axis=1)
        sT = _dot(kn, jnp.concatenate([q4T, none] if j == 0 else [none, q4T], axis=0))
        sT = jnp.where(valid, sT, -1e30)
        sink = jnp.concatenate([jnp.broadcast_to(sinks[:, GRP * j + i:GRP * j + i + 1], (1, BLK)) for i in range(GRP)], axis=1)
        m = lax.stop_gradient(jnp.maximum(jnp.max(sT, axis=0, keepdims=True), sink))
        pT = jnp.exp(sT - m)
        den = jnp.sum(pT, axis=0, keepdims=True) + jnp.exp(sink - m)
        oT = _dot(vT[HD * j:HD * (j + 1), :], pT) * (1.0 / den)
        o_T += [oT[:, BLK * i:BLK * (i + 1)] for i in range(GRP)]
    return jnp.concatenate([jnp.transpose(jnp.concatenate(o_T[per_tile * t:per_tile * (t + 1)], axis=0))
                            for t in range(QW // LANES)], axis=1)


def _attn_tables(cp_ref, cc_ref, sp_ref, sc_ref, n):
    tq = _rope_expand(cc_ref[...], sc_ref[...], QW // LANES)
    tk = _rope_expand(jnp.concatenate([cp_ref[...], cc_ref[...]], axis=0),
                      jnp.concatenate([sp_ref[...], sc_ref[...]], axis=0), KVW // LANES)
    qi = lax.broadcasted_iota(jnp.int32, (2 * BLK, GRP * BLK), 1) % BLK + BLK
    kj = lax.broadcasted_iota(jnp.int32, (2 * BLK, GRP * BLK), 0)
    dist = qi - kj
    valid = (dist >= 0) & (dist < BLK) & ((kj >= BLK) | (n > 0))
    return tq, tk, valid


def _attn_fwd(aq, akv, cos, sin, qg, kg, sinks, mq, mk):
    B, S, _ = aq.shape
    nb = S // BLK

    def body(q_ref, kvp_ref, kvc_ref, cp_ref, cc_ref, sp_ref, sc_ref, qg_ref, kg_ref, sk_ref, mq_ref, mk_ref, o_ref):
        tq, tk, valid = _attn_tables(cp_ref, cc_ref, sp_ref, sc_ref, pl.program_id(1))
        o_ref[...] = _attn_block(q_ref[...], kvp_ref[...], kvc_ref[...], qg_ref[...], kg_ref[...], sk_ref[...],
                                 tq, tk, mq_ref[...], mk_ref[...], valid)

    prev = lambda b, n: (b, jnp.maximum(n - 1, 0), 0)
    cur = lambda b, n: (b, n, 0)
    return pl.pallas_call(
        body, name="attn_fwd", grid=(B, nb),
        out_shape=jax.ShapeDtypeStruct((B, S, QW), F32),
        in_specs=[pl.BlockSpec((None, BLK, QW), cur), pl.BlockSpec((None, BLK, 2 * KVW), prev),
                  pl.BlockSpec((None, BLK, 2 * KVW), cur), pl.BlockSpec((None, BLK, LANES), prev),
                  pl.BlockSpec((None, BLK, LANES), cur), pl.BlockSpec((None, BLK, LANES), prev),
                  pl.BlockSpec((None, BLK, LANES), cur), _full((1, HD)), _full((1, HD)), _full((1, HQ)),
                  _full((QW, QW)), _full((KVW, KVW))],
        out_specs=pl.BlockSpec((None, BLK, QW), cur),
        compiler_params=_cparams(("parallel", "arbitrary")),
    )(aq, akv, akv, cos, cos, sin, sin, qg, kg, sinks, mq, mk)


def _attn_bwd(aq, akv, cos, sin, qg, kg, sinks, mq, mk, do):
    B, S, _ = aq.shape
    nb = S // BLK

    def body(q_ref, kvp_ref, kvc_ref, cp_ref, cc_ref, sp_ref, sc_ref, qg_ref, kg_ref, sk_ref, mq_ref, mk_ref, do_ref,
             dq_ref, dkv_ref, dqg_ref, dkg_ref, dsk_ref, carry):
        b, i = pl.program_id(0), pl.program_id(1)
        tq, tk, valid = _attn_tables(cp_ref, cc_ref, sp_ref, sc_ref, nb - 1 - i)
        fn = functools.partial(_attn_block, tq=tq, tk=tk, mq=mq_ref[...], mk=mk_ref[...], valid=valid)
        _, vjp = jax.vjp(fn, q_ref[...], kvp_ref[...], kvc_ref[...], qg_ref[...], kg_ref[...], sk_ref[...])
        dq, dkvp, dkvc, dqg, dkg, dsk = vjp(do_ref[...])

        @pl.when(i == 0)
        def _():
            carry[...] = jnp.zeros_like(carry)

        @pl.when((b == 0) & (i == 0))
        def _():
            dqg_ref[...] = jnp.zeros_like(dqg_ref)
            dkg_ref[...] = jnp.zeros_like(dkg_ref)
            dsk_ref[...] = jnp.zeros_like(dsk_ref)

        dq_ref[...] = dq.astype(BF16)
        dkv_ref[...] = (dkvc + carry[...]).astype(BF16)
        carry[...] = dkvp
        dqg_ref[...] += dqg
        dkg_ref[...] += dkg
        dsk_ref[...] += dsk

    prev = lambda b, i: (b, jnp.maximum(nb - 2 - i, 0), 0)
    cur = lambda b, i: (b, nb - 1 - i, 0)
    return pl.pallas_call(
        body, name="attn_bwd", grid=(B, nb),
        out_shape=[jax.ShapeDtypeStruct((B, S, QW), BF16), jax.ShapeDtypeStruct((B, S, 2 * KVW), BF16),
                   jax.ShapeDtypeStruct((1, HD), F32), jax.ShapeDtypeStruct((1, HD), F32),
                   jax.ShapeDtypeStruct((1, HQ), F32)],
        in_specs=[pl.BlockSpec((None, BLK, QW), cur), pl.BlockSpec((None, BLK, 2 * KVW), prev),
                  pl.BlockSpec((None, BLK, 2 * KVW), cur), pl.BlockSpec((None, BLK, LANES), prev),
                  pl.BlockSpec((None, BLK, LANES), cur), pl.BlockSpec((None, BLK, LANES), prev),
                  pl.BlockSpec((None, BLK, LANES), cur), _full((1, HD)), _full((1, HD)), _full((1, HQ)),
                  _full((QW, QW)), _full((KVW, KVW)), pl.BlockSpec((None, BLK, QW), cur)],
        out_specs=[pl.BlockSpec((None, BLK, QW), cur), pl.BlockSpec((None, BLK, 2 * KVW), cur),
                   _full((1, HD)), _full((1, HD)), _full((1, HQ))],
        scratch_shapes=[pltpu.VMEM((BLK, 2 * KVW), F32)],
        compiler_params=_cparams(("arbitrary", "arbitrary")),
    )(aq, akv, akv, cos, cos, sin, sin, qg, kg, sinks, mq, mk, do)


def _conv_taps(xe, w, rows):
    y = None
    for j in range(CONV):
        sh = pltpu.roll(xe, CONV - 1 - j, 0)[8:8 + rows, :] if j < CONV - 1 else xe[8:8 + rows, :]
        y = sh * w[j:j + 1, :] if y is None else y + sh * w[j:j + 1, :]
    return y


def _conv_bwd(xin, w, dy):
    B, S, C = xin.shape
    tc = min(512, S)
    r8 = tc // 8
    nt = S // tc

    def body(xp_ref, x_ref, xn_ref, dy_ref, dyn_ref, w_ref, dx_ref, dw_ref):
        b, i = pl.program_id(0), pl.program_id(1)
        w = w_ref[...]
        xp = jnp.where(i > 0, xp_ref[...], 0.0)
        xe = jnp.concatenate([xp, x_ref[...], xn_ref[...]], axis=0)
        taps = [(pltpu.roll(xe, CONV - 1 - j, 0) if j < CONV - 1 else xe)[8:8 + tc + 8, :] for j in range(CONV)]
        pre = sum(t * w[j:j + 1, :] for j, t in enumerate(taps))
        sg = _sigmoid(pre)
        dyn = jnp.where(i < nt - 1, dyn_ref[...], 0.0)
        dpre = jnp.concatenate([dy_ref[...], dyn], axis=0) * (sg * (1.0 + pre * (1.0 - sg)))
        dx = dpre[0:tc, :] * w[CONV - 1:CONV, :]
        for j in range(CONV - 1):
            dx = dx + pltpu.roll(dpre, tc + 8 - (CONV - 1 - j), 0)[0:tc, :] * w[j:j + 1, :]
        dx_ref[...] = dx.astype(BF16)
        dcur = dpre[0:tc, :]
        lane_row = lax.broadcasted_iota(jnp.int32, (CONV, C), 0)
        dw = jnp.zeros((CONV, C), F32)
        for j in range(CONV):
            dw = dw + jnp.where(lane_row == j, jnp.sum(taps[j][0:tc, :] * dcur, axis=0, keepdims=True), 0.0)

        @pl.when((b == 0) & (i == 0))
        def _():
            dw_ref[...] = jnp.zeros_like(dw_ref)

        dw_ref[...] += dw

    return pl.pallas_call(
        body, name="conv_bwd", grid=(B, nt),
        out_shape=[jax.ShapeDtypeStruct((B, S, C), BF16), jax.ShapeDtypeStruct((CONV, C), F32)],
        in_specs=[pl.BlockSpec((None, 8, C), lambda b, i: (b, jnp.maximum(i * r8 - 1, 0), 0)),
                  _rows(tc, C),
                  pl.BlockSpec((None, 8, C), lambda b, i: (b, jnp.minimum((i + 1) * r8, S // 8 - 1), 0)),
                  _rows(tc, C),
                  pl.BlockSpec((None, 8, C), lambda b, i: (b, jnp.minimum((i + 1) * r8, S // 8 - 1), 0)),
                  _full((CONV, C))],
        out_specs=[_rows(tc, C), _full((CONV, C))],
        compiler_params=_cparams(("arbitrary", "arbitrary")),
    )(xin, xin, xin, dy, dy, w)


def _softplus(x):
    return jnp.maximum(x, 0.0) + jnp.log1p(jnp.exp(-jnp.abs(x)))


_BMM = (((2,), (1,)), ((0,), (0,)))
_BMM_NT = (((2,), (2,)), ((0,), (0,)))
_BMM_TN = (((1,), (1,)), ((0,), (0,)))


def _bmm(a, b, dims=_BMM):
    return lax.dot_general(a.astype(BF16), b.astype(BF16), dims, preferred_element_type=F32)


def _split(a):
    hi = a.astype(BF16)
    return hi, (a - hi.astype(F32)).astype(BF16)


def _bmm3(a, b, dims=_BMM):
    ah, al = _split(a)
    bh, bl = _split(b)
    d = lambda p, q: lax.dot_general(p, q, dims, preferred_element_type=F32)
    return d(ah, bh) + (d(ah, bl) + d(al, bh))


TRI_BASE = 8


def _tri_inverse(L):
    ii = lax.broadcasted_iota(jnp.int32, (CH, CH), 0)
    jj = lax.broadcasted_iota(jnp.int32, (CH, CH), 1)
    same = lambda size: (ii // size) == (jj // size)
    diag = jnp.where(same(TRI_BASE), L, 0.0)
    X = (ii == jj).astype(F32) - diag
    P = diag
    n = 2
    while n < TRI_BASE:
        P = _bmm3(P, P)
        X = X + _bmm3(X, P)
        n *= 2
    size = TRI_BASE
    while size < CH:
        joint = jnp.where(same(2 * size) & jnp.logical_not(same(size)), L, 0.0)
        X = X - _bmm3(X, _bmm3(joint, X))
        size *= 2
    return X


@jax.custom_vjp
def _tri_inverse_known(L, T):
    return T


def _tri_inverse_known_fwd(L, T):
    return T, T


def _tri_inverse_known_bwd(T, dT):
    Tt = jnp.swapaxes(T, 1, 2)
    return -_bmm(Tt, _bmm(dT, Tt)), jnp.zeros_like(T)


_tri_inverse_known.defvjp(_tri_inverse_known_fwd, _tri_inverse_known_bwd)


def _triangle(n, upper):
    ii = lax.broadcasted_iota(jnp.int32, (n, CH, CH), 1)
    jj = lax.broadcasted_iota(jnp.int32, (n, CH, CH), 2)
    return ((ii <= jj) if upper else (ii >= jj)).astype(BF16)


@jax.custom_vjp
def _cumsum_rows(g):
    g0 = g.astype(BF16)
    r1 = g - g0.astype(F32)
    g1 = r1.astype(BF16)
    g2 = (r1 - g1.astype(F32)).astype(BF16)
    tri = _triangle(g.shape[0], False)
    d = lambda q: lax.dot_general(tri, q, _BMM, preferred_element_type=F32)
    return d(g0) + (d(g1) + d(g2))


def _cumsum_rows_fwd(g):
    return _cumsum_rows(g), None


def _cumsum_rows_bwd(_, dy):
    hi, lo = _split(dy)
    tri = _triangle(dy.shape[0], True)
    d = lambda q: lax.dot_general(tri, q, _BMM, preferred_element_type=F32)
    return (d(hi) + d(lo),)


_cumsum_rows.defvjp(_cumsum_rows_fwd, _cumsum_rows_bwd)


def _row_sums(t):
    n, r, w = t.shape
    hi, lo = _split(t.reshape(n * r, w))
    ones = jnp.ones((w, w), BF16)
    s = jnp.dot(hi, ones, preferred_element_type=F32) + jnp.dot(lo, ones, preferred_element_type=F32)
    return s.reshape(n, r, w)


def _dn_prep(t_known, qr, kr, v, a_raw, b_raw, a_log, dt_b):
    n = qr.shape[0]
    ii = lax.broadcasted_iota(jnp.int32, (n, CH, CH), 1)
    jj = lax.broadcasted_iota(jnp.int32, (n, CH, CH), 2)
    incl, strict = ii >= jj, ii > jj
    q = qr * lax.rsqrt(_row_sums(qr * qr) + EPS) * (DK ** -0.5)
    k = kr * lax.rsqrt(_row_sums(kr * kr) + EPS)
    beta = _sigmoid(b_raw)
    g = -jnp.exp(a_log) * _softplus(a_raw + dt_b)
    gcb = _cumsum_rows(jnp.broadcast_to(g, (n, CH, DK)))
    gc = gcb[:, :, 0:1]
    gc_row = jnp.swapaxes(gcb, 1, 2)[:, 0:1, 0:CH]
    decay = jnp.where(incl, jnp.exp(jnp.where(incl, gc - gc_row, 0.0)), 0.0)
    kb = k * beta
    L = jnp.where(strict, _bmm(kb, k, _BMM_NT) * decay, 0.0)
    T = _tri_inverse(L) if t_known is None else _tri_inverse_known(L, t_known)
    eg = jnp.exp(gc)
    u = _bmm(T, v * beta)
    w = _bmm(T, kb * eg)
    a_in = _bmm(q, k, _BMM_NT) * decay
    g_last = gc[:, CH - 1:CH, :]
    return u, w, q * eg, k * jnp.exp(g_last - gc), a_in, jnp.exp(g_last), T


def _dn_step(S0, u, w, qd, kd, a_in, cd):
    r = _bmm(jnp.concatenate([w, qd], axis=1), S0)
    v_new = u - r[:, 0:CH, :]
    o = r[:, CH:2 * CH, :] + _bmm(a_in, v_new)
    S1 = S0 * cd + _bmm(kd, v_new, _BMM_TN)
    return o, S1


def _dn_stack(cq, ba, al, dt, G):
    cols = [[] for _ in range(7)]
    for c in range(G):
        rows = slice(CH * c, CH * (c + 1))
        for h in range(DH):
            parts = (cq[rows, DK * h:DK * (h + 1)], cq[rows, DNW + DK * h:DNW + DK * (h + 1)],
                     cq[rows, 2 * DNW + DK * h:2 * DNW + DK * (h + 1)], ba[rows, DH + h:DH + h + 1],
                     ba[rows, h:h + 1], al[:, h:h + 1], dt[:, h:h + 1])
            for col, p in zip(cols, parts):
                col.append(p)
    return tuple(jnp.stack(col) for col in cols)


def _dn_group(S, want):
    g = want
    while (S // CH) % g:
        g //= 2
    return g


def _dn_prep_fwd(xin, conv_w, ba, a_log, dt_b):
    B, S, _ = xin.shape
    nc = S // CH
    G = _dn_group(S, 4)
    r8 = G * CH // 8

    def body(xp_ref, x_ref, cw_ref, ba_ref, al_ref, dt_ref, cq_ref, u_ref, w_ref, qd_ref, kd_ref, a_ref, t_ref, cd_ref):
        xp = jnp.where(pl.program_id(1) > 0, xp_ref[...], 0.0)
        cq = _silu(_conv_taps(jnp.concatenate([xp, x_ref[...]], axis=0), cw_ref[...], G * CH))
        cq_ref[...] = cq
        ops = _dn_stack(cq, ba_ref[...], al_ref[...], dt_ref[...], G)
        u, w, qd, kd, a_in, cd, T = _dn_prep(None, *ops)
        lane4 = lax.broadcasted_iota(jnp.int32, (1, DH), 1)
        for c in range(G):
            rows = slice(CH * c, CH * (c + 1))
            cdrow = jnp.zeros((1, DH), F32)
            for h in range(DH):
                n = DH * c + h
                lanes = slice(DK * h, DK * (h + 1))
                u_ref[rows, lanes] = u[n]
                w_ref[rows, lanes] = w[n]
                qd_ref[rows, lanes] = qd[n]
                kd_ref[rows, lanes] = kd[n]
                a_ref[rows, CH * h:CH * (h + 1)] = a_in[n]
                t_ref[rows, CH * h:CH * (h + 1)] = T[n]
                cdrow = cdrow + jnp.where(lane4 == h, cd[n], 0.0)
            cd_ref[c] = cdrow

    wide = jax.ShapeDtypeStruct((B, S, DNW), F32)
    sq = jax.ShapeDtypeStruct((B, S, DH * CH), F32)
    return pl.pallas_call(
        body, name="dn_prep_fwd", grid=(B, nc // G),
        out_shape=[jax.ShapeDtypeStruct((B, S, CONVW), F32), wide, wide, wide, wide, sq, sq,
                   jax.ShapeDtypeStruct((B, nc, 1, DH), F32)],
        in_specs=[pl.BlockSpec((None, 8, CONVW), lambda b, i: (b, jnp.maximum(i * r8 - 1, 0), 0)),
                  _rows(G * CH, CONVW), _full((CONV, CONVW)), _rows(G * CH, 2 * DH), _full((1, DH)), _full((1, DH))],
        out_specs=[_rows(G * CH, CONVW)] + [_rows(G * CH, DNW)] * 4 + [_rows(G * CH, DH * CH)] * 2
                  + [pl.BlockSpec((None, G, 1, DH), lambda b, i: (b, i, 0, 0))],
        compiler_params=_cparams(("parallel", "arbitrary")),
    )(xin, xin, conv_w, ba, a_log, dt_b)


def _dn_seq_specs(B, steps, gs, rev):
    at = (lambda i: steps - 1 - i) if rev else (lambda i: i)
    wide = pl.BlockSpec((B, gs * CH, DNW), lambda i: (0, at(i), 0))
    a_spec = pl.BlockSpec((B, gs * CH, DH * CH), lambda i: (0, at(i), 0))
    cd_spec = pl.BlockSpec((B, gs, 1, DH), lambda i: (0, at(i), 0, 0))
    st_spec = pl.BlockSpec((B, gs, DH, DK, DK), lambda i: (0, at(i), 0, 0, 0))
    return wide, a_spec, cd_spec, st_spec


def _dn_step_operands(B, c, u_ref, w_ref, qd_ref, kd_ref, a_ref, cd_ref):
    pairs = [(b, h) for b in range(B) for h in range(DH)]
    rows = slice(CH * c, CH * (c + 1))
    wide = lambda ref: jnp.stack([ref[b, rows, DK * h:DK * (h + 1)] for b, h in pairs])
    a_in = jnp.stack([a_ref[b, rows, CH * h:CH * (h + 1)] for b, h in pairs])
    cd = jnp.stack([cd_ref[b, c, :, h:h + 1] for b, h in pairs])
    return wide(u_ref), wide(w_ref), wide(qd_ref), wide(kd_ref), a_in, cd


def _dn_seq_fwd(u, w, qd, kd, a_in, cd):
    B, S, _ = u.shape
    nc = S // CH
    gs = _dn_group(S, 8)

    def body(u_ref, w_ref, qd_ref, kd_ref, a_ref, cd_ref, o_ref, st_ref, state):
        @pl.when(pl.program_id(0) == 0)
        def _():
            state[...] = jnp.zeros_like(state)

        S0 = state[...]
        for c in range(gs):
            for b in range(B):
                st_ref[b, c] = S0[DH * b:DH * (b + 1)]
            o, S0 = _dn_step(S0, *_dn_step_operands(B, c, u_ref, w_ref, qd_ref, kd_ref, a_ref, cd_ref))
            for b in range(B):
                for h in range(DH):
                    o_ref[b, CH * c:CH * (c + 1), DK * h:DK * (h + 1)] = o[DH * b + h]
        state[...] = S0

    wide, a_spec, cd_spec, st_spec = _dn_seq_specs(B, nc // gs, gs, False)
    return pl.pallas_call(
        body, name="dn_seq_fwd", grid=(nc // gs,),
        out_shape=[jax.ShapeDtypeStruct((B, S, DNW), F32), jax.ShapeDtypeStruct((B, nc, DH, DK, DK), F32)],
        in_specs=[wide, wide, wide, wide, a_spec, cd_spec],
        out_specs=[wide, st_spec],
        scratch_shapes=[pltpu.VMEM((B * DH, DK, DK), F32)],
        compiler_params=_cparams(("arbitrary",)),
    )(u, w, qd, kd, a_in, cd)


def _dn_seq_bwd(u, w, qd, kd, a_in, cd, states, do):
    B, S, _ = u.shape
    nc = S // CH
    gs = _dn_group(S, 8)

    def body(u_ref, w_ref, qd_ref, kd_ref, a_ref, cd_ref, st_ref, do_ref,
             du_ref, dw_ref, dqd_ref, dkd_ref, da_ref, dcd_ref, dstate):
        @pl.when(pl.program_id(0) == 0)
        def _():
            dstate[...] = jnp.zeros_like(dstate)

        lane4 = lax.broadcasted_iota(jnp.int32, (1, DH), 1)
        dS = dstate[...]
        for c in reversed(range(gs)):
            rows = slice(CH * c, CH * (c + 1))
            S0 = jnp.concatenate([st_ref[b, c] for b in range(B)], axis=0)
            do = jnp.stack([do_ref[b, rows, DK * h:DK * (h + 1)] for b in range(B) for h in range(DH)])
            _, vjp = jax.vjp(_dn_step, S0, *_dn_step_operands(B, c, u_ref, w_ref, qd_ref, kd_ref, a_ref, cd_ref))
            dS, du, dw, dqd, dkd, da, dcd = vjp((do, dS))
            for b in range(B):
                dcdrow = jnp.zeros((1, DH), F32)
                for h in range(DH):
                    n = DH * b + h
                    lanes = slice(DK * h, DK * (h + 1))
                    du_ref[b, rows, lanes] = du[n]
                    dw_ref[b, rows, lanes] = dw[n]
                    dqd_ref[b, rows, lanes] = dqd[n]
                    dkd_ref[b, rows, lanes] = dkd[n]
                    da_ref[b, rows, CH * h:CH * (h + 1)] = da[n]
                    dcdrow = dcdrow + jnp.where(lane4 == h, dcd[n], 0.0)
                dcd_ref[b, c] = dcdrow
        dstate[...] = dS

    wide, a_spec, cd_spec, st_spec = _dn_seq_specs(B, nc // gs, gs, True)
    sd = jax.ShapeDtypeStruct((B, S, DNW), F32)
    return pl.pallas_call(
        body, name="dn_seq_bwd", grid=(nc // gs,),
        out_shape=[sd, sd, sd, sd, jax.ShapeDtypeStruct((B, S, DH * CH), F32), jax.ShapeDtypeStruct((B, nc, 1, DH), F32)],
        in_specs=[wide, wide, wide, wide, a_spec, cd_spec, st_spec, wide],
        out_specs=[wide, wide, wide, wide, a_spec, cd_spec],
        scratch_shapes=[pltpu.VMEM((B * DH, DK, DK), F32)],
        compiler_params=_cparams(("arbitrary",)),
    )(u, w, qd, kd, a_in, cd, states, do)


def _dn_prep_bwd(cq, ba, a_log, dt_b, t_inv, du, dw, dqd, dkd, da, dcd):
    B, S, _ = cq.shape
    nc = S // CH
    G = _dn_group(S, 4)

    def body(cq_ref, ba_ref, al_ref, dt_ref, t_ref, du_ref, dw_ref, dqd_ref, dkd_ref, da_ref, dcd_ref,
             dcq_ref, dba_ref, dal_ref, ddt_ref):
        @pl.when((pl.program_id(0) == 0) & (pl.program_id(1) == 0))
        def _():
            dal_ref[...] = jnp.zeros_like(dal_ref)
            ddt_ref[...] = jnp.zeros_like(ddt_ref)

        pairs = [(c, h) for c in range(G) for h in range(DH)]
        rows = lambda c: slice(CH * c, CH * (c + 1))
        wide = lambda ref: jnp.stack([ref[rows(c), DK * h:DK * (h + 1)] for c, h in pairs])
        square = lambda ref: jnp.stack([ref[rows(c), CH * h:CH * (h + 1)] for c, h in pairs])
        ops = _dn_stack(cq_ref[...], ba_ref[...], al_ref[...], dt_ref[...], G)
        cots = (wide(du_ref), wide(dw_ref), wide(dqd_ref), wide(dkd_ref), square(da_ref),
                jnp.stack([dcd_ref[c][:, h:h + 1] for c, h in pairs]), jnp.zeros((len(pairs), CH, CH), F32))
        _, vjp = jax.vjp(functools.partial(_dn_prep, square(t_ref)), *ops)
        dq, dk, dv, dar, dbr, dl, dd = vjp(cots)
        lane8 = lax.broadcasted_iota(jnp.int32, (CH, 2 * DH), 1)
        lane4 = lax.broadcasted_iota(jnp.int32, (1, DH), 1)
        dal = jnp.zeros((1, DH), F32)
        ddt = jnp.zeros((1, DH), F32)
        for c in range(G):
            dba = jnp.zeros((CH, 2 * DH), F32)
            for h in range(DH):
                n = DH * c + h
                dcq_ref[rows(c), DK * h:DK * (h + 1)] = dq[n]
                dcq_ref[rows(c), DNW + DK * h:DNW + DK * (h + 1)] = dk[n]
                dcq_ref[rows(c), 2 * DNW + DK * h:2 * DNW + DK * (h + 1)] = dv[n]
                dba = dba + jnp.where(lane8 == h, dbr[n], 0.0) + jnp.where(lane8 == DH + h, dar[n], 0.0)
                dal = dal + jnp.where(lane4 == h, dl[n], 0.0)
                ddt = ddt + jnp.where(lane4 == h, dd[n], 0.0)
            dba_ref[rows(c), :] = dba.astype(BF16)
        dal_ref[...] += dal
        ddt_ref[...] += ddt

    return pl.pallas_call(
        body, name="dn_prep_bwd", grid=(B, nc // G),
        out_shape=[jax.ShapeDtypeStruct((B, S, CONVW), F32), jax.ShapeDtypeStruct((B, S, 2 * DH), BF16),
                   jax.ShapeDtypeStruct((1, DH), F32), jax.ShapeDtypeStruct((1, DH), F32)],
        in_specs=[_rows(G * CH, CONVW), _rows(G * CH, 2 * DH), _full((1, DH)), _full((1, DH)), _rows(G * CH, DH * CH)]
                 + [_rows(G * CH, DNW)] * 4 + [_rows(G * CH, DH * CH),
                                               pl.BlockSpec((None, G, 1, DH), lambda b, i: (b, i, 0, 0))],
        out_specs=[_rows(G * CH, CONVW), _rows(G * CH, 2 * DH), _full((1, DH)), _full((1, DH))],
        compiler_params=_cparams(("arbitrary", "arbitrary")),
    )(cq, ba, a_log, dt_b, t_inv, du, dw, dqd, dkd, da, dcd)


def _gated_norm(o, z, g):
    outs = []
    for h in range(DH):
        t = o[:, DK * h:DK * (h + 1)]
        r = lax.rsqrt(jnp.mean(t * t, axis=-1, keepdims=True) + EPS)
        outs.append(t * r * g * _silu(z[:, DK * h:DK * (h + 1)]))
    return jnp.concatenate(outs, axis=1)


def _mix_fwd(x, o_attn, o_dn, z, ga, gd, mod, dn_g, w_branch, w_out):
    B, S, _ = x.shape
    tm = _tile(S, 512)

    def body(x_ref, oa_ref, od_ref, z_ref, ga_ref, gd_ref, mod_ref, g_ref, wb_ref, wo_ref,
             x1_ref, mix_ref, mg_ref, ob_ref):
        oa = oa_ref[...].astype(BF16)
        od = _gated_norm(od_ref[...], z_ref[...], g_ref[...]).astype(BF16)
        ob_ref[0] = oa
        ob_ref[1] = od
        ya = jnp.dot(oa, wb_ref[0:QW, :], preferred_element_type=F32)
        yd = jnp.dot(od, wb_ref[QW:QW + DNW, :], preferred_element_type=F32)
        merged = (_sigmoid(ga_ref[...]) * ya + _sigmoid(gd_ref[...]) * yd).astype(BF16)
        mg_ref[...] = merged
        mix = jnp.dot(merged, wo_ref[...], preferred_element_type=F32)
        mix_ref[...] = mix
        x1_ref[...] = x_ref[...] + mod_ref[2:3, :] * mix

    return pl.pallas_call(
        body, name="mix_fwd", grid=(B, S // tm),
        out_shape=[jax.ShapeDtypeStruct((B, S, D), F32), jax.ShapeDtypeStruct((B, S, D), F32),
                   jax.ShapeDtypeStruct((B, S, D), BF16), jax.ShapeDtypeStruct((B, 2, S, QW), BF16)],
        in_specs=[_rows(tm, D), _rows(tm, QW), _rows(tm, DNW), _rows(tm, DNW), _rows(tm, D), _rows(tm, D),
                  _perb(6, D), _full((1, DK)), _resident(w_branch.shape), _resident(w_out.shape)],
        out_specs=[_rows(tm, D), _rows(tm, D), _rows(tm, D), _stacked(2, tm, QW)],
        compiler_params=_cparams(("parallel", "arbitrary")),
    )(x, o_attn, o_dn, z, ga, gd, mod, dn_g, w_branch, w_out)


def _mix_bwd(dx1, mix, o_attn, o_dn, z, ga, gd, mod, dn_g, w_branch, w_out):
    B, S, _ = dx1.shape
    tm = _tile(S, 512)

    def body(dx1_ref, mix_ref, oa_ref, od_ref, z_ref, ga_ref, gd_ref, mod_ref, g_ref, wb_ref, wo_ref,
             dmix_ref, dyo_ref, dga_ref, dgd_ref, dz_ref, doa_ref, dod_ref, dgate_ref, dg_ref):
        b, i = pl.program_id(0), pl.program_id(1)
        dx1 = dx1_ref[...]
        dmix = (dx1 * mod_ref[2:3, :]).astype(BF16)
        dmix_ref[...] = dmix
        dgate = jnp.sum(dx1 * mix_ref[...], axis=0, keepdims=True)
        dmerged = _dot_nt(dmix, wo_ref[...])
        odn, gn_vjp = jax.vjp(_gated_norm, od_ref[...], z_ref[...], g_ref[...])
        ya = _dot(oa_ref[...], wb_ref[0:QW, :])
        yd = _dot(odn, wb_ref[QW:QW + DNW, :])
        sa, sd = _sigmoid(ga_ref[...]), _sigmoid(gd_ref[...])
        dya = (dmerged * sa).astype(BF16)
        dyd = (dmerged * sd).astype(BF16)
        dyo_ref[0] = dya
        dyo_ref[1] = dyd
        dga_ref[...] = (dmerged * ya * sa * (1.0 - sa)).astype(BF16)
        dgd_ref[...] = (dmerged * yd * sd * (1.0 - sd)).astype(BF16)
        doa_ref[...] = _dot_nt(dya, wb_ref[0:QW, :])
        dodn = _dot_nt(dyd, wb_ref[QW:QW + DNW, :])
        dod, dz, dg = gn_vjp(dodn)
        dod_ref[...] = dod
        dz_ref[...] = dz.astype(BF16)

        @pl.when(i == 0)
        def _():
            dgate_ref[...] = jnp.zeros_like(dgate_ref)

        @pl.when((b == 0) & (i == 0))
        def _():
            dg_ref[...] = jnp.zeros_like(dg_ref)

        dgate_ref[...] += dgate
        dg_ref[...] += dg

    return pl.pallas_call(
        body, name="mix_bwd", grid=(B, S // tm),
        out_shape=[jax.ShapeDtypeStruct((B, S, D), BF16), jax.ShapeDtypeStruct((B, 2, S, D), BF16),
                   jax.ShapeDtypeStruct((B, S, D), BF16), jax.ShapeDtypeStruct((B, S, D), BF16),
                   jax.ShapeDtypeStruct((B, S, DNW), BF16),
                   jax.ShapeDtypeStruct((B, S, QW), F32), jax.ShapeDtypeStruct((B, S, DNW), F32),
                   jax.ShapeDtypeStruct((B, 1, D), F32), jax.ShapeDtypeStruct((1, DK), F32)],
        in_specs=[_rows(tm, D), _rows(tm, D), _rows(tm, QW), _rows(tm, DNW), _rows(tm, DNW), _rows(tm, D),
                  _rows(tm, D), _perb(6, D), _full((1, DK)), _resident(w_branch.shape), _resident(w_out.shape)],
        out_specs=[_rows(tm, D), _stacked(2, tm, D), _rows(tm, D), _rows(tm, D), _rows(tm, DNW),
                   _rows(tm, QW), _rows(tm, DNW), _perb(1, D), _full((1, DK))],
        compiler_params=_cparams(("arbitrary", "arbitrary")),
    )(dx1, mix, o_attn, o_dn, z, ga, gd, mod, dn_g, w_branch, w_out)


GU_SHARD = 2 * FFN // N_DEV
GU_HALF = N_DEV // 2


def _ffn1_fwd(x1, mod, g2, w_gu):
    B, S, _ = x1.shape
    tm = _tile(S)

    def body(x_ref, mod_ref, g_ref, w_ref, h_ref, gate_ref, up_ref, act_ref):
        h = _rms_mod(x_ref[...], g_ref[...], mod_ref[4:5, :], mod_ref[3:4, :]).astype(BF16)
        h_ref[...] = h
        for j in range(GU_HALF):
            gate = _dot_nt(h, w_ref[j])
            up = _dot_nt(h, w_ref[GU_HALF + j])
            gate_ref[j] = gate
            up_ref[j] = up
            act_ref[j] = (_silu(gate) * up).astype(BF16)

    blk = lambda dt: jax.ShapeDtypeStruct((B, GU_HALF, S, GU_SHARD), dt)
    return pl.pallas_call(
        body, name="ffn1_fwd", grid=(B, S // tm),
        out_shape=[jax.ShapeDtypeStruct((B, S, D), BF16), blk(F32), blk(F32), blk(BF16)],
        in_specs=[_rows(tm, D), _perb(6, D), _full((1, D)), _resident(w_gu.shape)],
        out_specs=[_rows(tm, D)] + [_stacked(GU_HALF, tm, GU_SHARD)] * 3,
        compiler_params=_cparams(("parallel", "arbitrary")),
    )(x1, mod, g2, w_gu)


def _ffn2_fwd(act, x1, target, mod, w_down):
    B, S, _ = x1.shape
    tm = _tile(S, 512)

    def body(a_ref, x_ref, t_ref, mod_ref, w_ref, dy_ref, loss_ref, dgate_ref):
        b, i = pl.program_id(0), pl.program_id(1)
        y = jnp.dot(a_ref[0], w_ref[0], preferred_element_type=F32)
        for j in range(1, GU_HALF):
            y = y + jnp.dot(a_ref[j], w_ref[j], preferred_element_type=F32)
        err = x_ref[...] + mod_ref[5:6, :] * y - t_ref[...]
        dy = err * (1.0 / D)
        dy_ref[...] = dy

        @pl.when((b == 0) & (i == 0))
        def _():
            loss_ref[...] = jnp.zeros_like(loss_ref)

        @pl.when(i == 0)
        def _():
            dgate_ref[...] = jnp.zeros_like(dgate_ref)

        loss_ref[...] += (0.5 / D) * jnp.sum(err * err)
        dgate_ref[...] += jnp.sum(dy * y, axis=0, keepdims=True)

    return pl.pallas_call(
        body, name="ffn2_fwd", grid=(B, S // tm),
        out_shape=[jax.ShapeDtypeStruct((B, S, D), F32), jax.ShapeDtypeStruct((1, 128), F32),
                   jax.ShapeDtypeStruct((B, 1, D), F32)],
        in_specs=[_stacked(GU_HALF, tm, GU_SHARD), _rows(tm, D), _rows(tm, D), _perb(6, D), _resident(w_down.shape)],
        out_specs=[_rows(tm, D), _full((1, 128)), _perb(1, D)],
        compiler_params=_cparams(("arbitrary", "arbitrary")),
    )(act, x1, target, mod, w_down)


def _ffn2_bwd(dy, gate, up, mod, w_down):
    B, S, _ = dy.shape
    tm = _tile(S)

    def body(dy_ref, gate_ref, up_ref, mod_ref, w_ref, dgu_ref, dyg_ref):
        dyg = (dy_ref[...] * mod_ref[5:6, :]).astype(BF16)
        dyg_ref[...] = dyg
        for j in range(GU_HALF):
            dact = _dot_nt(dyg, w_ref[j])
            gate, up = gate_ref[j], up_ref[j]
            sg = _sigmoid(gate)
            dgu_ref[j] = (dact * up * (sg * (1.0 + gate * (1.0 - sg)))).astype(BF16)
            dgu_ref[GU_HALF + j] = (dact * (gate * sg)).astype(BF16)

    return pl.pallas_call(
        body, name="ffn2_bwd", grid=(B, S // tm),
        out_shape=[jax.ShapeDtypeStruct((B, N_DEV, S, GU_SHARD), BF16), jax.ShapeDtypeStruct((B, S, D), BF16)],
        in_specs=[_rows(tm, D), _stacked(GU_HALF, tm, GU_SHARD), _stacked(GU_HALF, tm, GU_SHARD), _perb(6, D),
                  _resident(w_down.shape)],
        out_specs=[_stacked(N_DEV, tm, GU_SHARD), _rows(tm, D)],
        compiler_params=_cparams(("parallel", "arbitrary")),
    )(dy, gate, up, mod, w_down)


def _ffn1_bwd(dgu, x1, dy, mod, g2, w_gu):
    B, S, _ = x1.shape
    tm = _tile(S, 512)

    def body(dgu_ref, x_ref, dy_ref, mod_ref, g_ref, w_ref, dx1_ref, dg_ref, dsc_ref, dsh_ref):
        b, i = pl.program_id(0), pl.program_id(1)
        dh = jnp.dot(dgu_ref[0], w_ref[0], preferred_element_type=F32)
        for j in range(1, N_DEV):
            dh = dh + jnp.dot(dgu_ref[j], w_ref[j], preferred_element_type=F32)
        _, vjp = jax.vjp(_rms_mod, x_ref[...], g_ref[...], mod_ref[4:5, :], mod_ref[3:4, :])
        dx, dg, dsc, dsh = vjp(dh)
        dx1_ref[...] = dy_ref[...] + dx

        @pl.when((b == 0) & (i == 0))
        def _():
            dg_ref[...] = jnp.zeros_like(dg_ref)

        @pl.when(i == 0)
        def _():
            dsc_ref[...] = jnp.zeros_like(dsc_ref)
            dsh_ref[...] = jnp.zeros_like(dsh_ref)

        dg_ref[...] += dg
        dsc_ref[...] += dsc
        dsh_ref[...] += dsh

    return pl.pallas_call(
        body, name="ffn1_bwd", grid=(B, S // tm),
        out_shape=[jax.ShapeDtypeStruct((B, S, D), F32), jax.ShapeDtypeStruct((1, D), F32),
                   jax.ShapeDtypeStruct((B, 1, D), F32), jax.ShapeDtypeStruct((B, 1, D), F32)],
        in_specs=[_stacked(N_DEV, tm, GU_SHARD), _rows(tm, D), _rows(tm, D), _perb(6, D), _full((1, D)),
                  _resident(w_gu.shape)],
        out_specs=[_rows(tm, D), _full((1, D)), _perb(1, D), _perb(1, D)],
        compiler_params=_cparams(("arbitrary", "arbitrary")),
    )(dgu, x1, dy, mod, g2, w_gu)


def _adamw(w, g, m, v, name):
    def body(w_ref, g_ref, m_ref, v_ref, d_ref, nm_ref, nv_ref):
        d_ref[...], nm_ref[...], nv_ref[...] = _adamw_math(w_ref[...], g_ref[...], m_ref[...], v_ref[...])

    sd = jax.ShapeDtypeStruct(w.shape, F32)
    return pl.pallas_call(body, name=name, out_shape=(sd, sd, sd), compiler_params=_cparams())(w, g, m, v)


def kernel(x, c, positions, ada_w, ada_b, norm1_g, w_in, conv_w, q_norm_g, k_norm_g, sinks, a_log, dt_bias, dn_norm_g, w_branch, w_out, norm2_g, w_gate_up, w_down, loss_target, m_ada_w, m_ada_b, m_norm1_g, m_w_in, m_conv_w, m_q_norm_g, m_k_norm_g, m_sinks, m_a_log, m_dt_bias, m_dn_norm_g, m_w_branch, m_w_out, m_norm2_g, m_w_gate_up, m_w_down, v_ada_w, v_ada_b, v_norm1_g, v_w_in, v_conv_w, v_q_norm_g, v_k_norm_g, v_sinks, v_a_log, v_dt_bias, v_dn_norm_g, v_w_branch, v_w_out, v_norm2_g, v_w_gate_up, v_w_down):
    B, S, _ = x.shape
    me = 4 * lax.axis_index("x") + 2 * lax.axis_index("y") + lax.axis_index("c")

    tr = lambda t: jnp.swapaxes(t, 1, 2)
    shards = [w[0].astype(BF16) for w in (tr(w_in), w_branch, w_out, tr(w_gate_up), w_down)]

    c_all = _all_gather_small(c, "gather_c").reshape(N_DEV * B, D)
    ncol = 6 * D // N_DEV
    mod_cols, cond_all = _ada_fwd(c_all, ada_w[0], lax.dynamic_slice(ada_b, (0, me * ncol), (1, ncol)))
    mod_all = _all_gather_small(mod_cols, "gather_mod").transpose(1, 0, 2).reshape(N_DEV * B, 6 * D)
    mod = lax.dynamic_slice(mod_all, (me * B, 0), (B, 6 * D)).reshape(B, 6, D)
    conv2 = conv_w.reshape(CONV, CONVW // N_DEV)
    conv_all = _all_gather_small(conv2, "gather_conv").transpose(1, 0, 2).reshape(CONV, CONVW)

    (w_in_b,) = _all_gather_big(shards[:1], "gather_w_in", after=(mod, conv_all))
    w_sems, w_srcs, w_lands, w_token = _copies_start(shards[1:], [_place_own(s, me) for s in shards[1:]], False,
                                                    w_in_b, "gather_rest_start")

    h1, aq, akv, dnx, ba, z, ga, gd = _inproj_fwd(x, mod, norm1_g + w_token[0, 0], w_in_b)
    invf, mean_q, mean_k = _attn_consts()
    rope_cos, rope_sin = _rope_tables(positions.reshape(B, S, 1), invf)
    o_attn = _attn_fwd(aq, akv, rope_cos, rope_sin, q_norm_g, k_norm_g, sinks, mean_q, mean_k)
    cq, dn_u, dn_w, dn_qd, dn_kd, dn_a, dn_t, dn_cd = _dn_prep_fwd(dnx, conv_all, ba, a_log, dt_bias)
    o_dn, states = _dn_seq_fwd(dn_u, dn_w, dn_qd, dn_kd, dn_a, dn_cd)
    w_branch_g, w_out_g, w_gu_b, w_down_g = _copies_wait(w_sems, w_srcs, w_lands, o_dn, "gather_wait_rest")
    w_branch_f = w_branch_g.reshape(D, D)
    w_out_f = w_out_g.reshape(D, D)
    w_down_b = w_down_g.reshape(GU_HALF, GU_SHARD, D)
    x1, mix, merged, ob = _mix_fwd(x, o_attn, o_dn, z, ga, gd, mod, dn_norm_g, w_branch_f, w_out_f)
    h2, gate, up, act = _ffn1_fwd(x1, mod, norm2_g, w_gu_b)
    dy, loss_part, d_gate2 = _ffn2_fwd(act, x1, loss_target, mod, w_down_b)
    loss = lax.psum(loss_part[0, 0], ("x", "y", "c"))

    one = lambda t: t.reshape(B, 1, S, t.shape[-1])
    dgu, dyg = _ffn2_bwd(dy, gate, up, mod, w_down_b)
    g_w_down = _wgrad(act, one(dyg), "wgrad_down")
    dx1, d_n2g, d_scale2, d_shift2 = _ffn1_bwd(dgu, x1, dy, mod, norm2_g, w_gu_b)
    g_w_gu = _wgrad(dgu, one(h2), "wgrad_gate_up")
    ffn = _exchange_start([g_w_gu, g_w_down.reshape(N_DEV, FFN // N_DEV, D)], me, dx1, "exchange_ffn_start")
    dmix, dyo, dga, dgd, dz, d_oa, d_od, d_gate1, d_dng = _mix_bwd(
        dx1, mix, o_attn, o_dn, z, ga, gd, mod, dn_norm_g + ffn[3][0, 0], w_branch_f, w_out_f)
    d_dn = _dn_seq_bwd(dn_u, dn_w, dn_qd, dn_kd, dn_a, dn_cd, states, d_od)
    dcq, dba, d_alog, d_dtb = _dn_prep_bwd(cq, ba, a_log, dt_bias, dn_t, *d_dn)
    ddnx, d_conv = _conv_bwd(dnx, conv_all, dcq)
    daq, dakv, d_qg, d_kg, d_sinks = _attn_bwd(aq, akv, rope_cos, rope_sin, q_norm_g, k_norm_g, sinks, mean_q, mean_k, d_oa)
    dps = [daq, dakv, ddnx, dba, dz, dga, dgd]
    dblk, grad_x, d_n1g, d_scale1, d_shift1 = _inproj_bwd(x, mod, norm1_g, dx1, dps, w_in_b)

    dmod = jnp.concatenate([d_shift1, d_scale1, d_gate1, d_shift2, d_scale2, d_gate2], axis=2).reshape(B, 6 * D)
    small = jnp.concatenate([d_n1g, d_qg, d_kg, d_sinks, d_alog, d_dtb, d_dng, d_n2g, d_conv.reshape(1, CONV * CONVW)], axis=1)
    nsm = small.shape[1]
    width = -(-max(6 * D, nsm) // 128) * 128
    rows = jnp.concatenate([jnp.pad(dmod, ((0, 0), (0, width - 6 * D))), jnp.pad(small, ((0, 8 - B - 1), (0, width - nsm)))], axis=0)
    rows_all = _all_gather_small(rows, "gather_small")
    dmod_all = rows_all[:, 0:B, 0:6 * D].reshape(N_DEV * B, 6 * D)
    dmod_cols = lax.dynamic_slice(dmod_all, (0, me * ncol), (N_DEV * B, ncol))
    grad_ada_w, grad_ada_b, small_sum = _ada_bwd(cond_all, dmod_all, dmod_cols, rows_all[:, B, :])
    sizes = [D, HD, HD, HQ, DH, DH, DK, D]
    so = np.cumsum([0] + sizes)
    g_n1, g_qg, g_kg, g_sk, g_al, g_dt, g_dn, g_n2 = [small_sum[:, so[i]:so[i + 1]] for i in range(8)]
    g_conv_all = small_sum[:, so[8]:so[8] + CONV * CONVW].reshape(CONV, N_DEV, CONVW // N_DEV)
    grad_conv = lax.dynamic_slice(g_conv_all, (0, me, 0), (CONV, 1, CONVW // N_DEV)).reshape(CONV, CONVW // N_DEV)

    g_w_in = _wgrad(dblk, one(h1), "wgrad_in", after=small_sum)
    proj = _exchange_start([g_w_in], me, small_sum, "exchange_in_start")
    g_w_out = _wgrad(one(merged), one(dmix), "wgrad_out", after=proj[3])
    g_w_branch = _wgrad(ob, dyo, "wgrad_branch", after=proj[3])
    mixer = _exchange_start([g_w_branch.reshape(N_DEV, D // N_DEV, D), g_w_out.reshape(N_DEV, D // N_DEV, D)], me,
                            proj[3], "exchange_mix_start")

    upd, grads = {}, {}

    def finish(names, parts, weights):
        for nm, p, (w, m, v) in zip(names, parts, weights):
            grads[nm], *upd[nm] = _sum_adamw(p, w, m, v, "update_" + nm)

    finish(["w_gate_up", "w_down"], _copies_wait(*ffn[:3], mixer[3], "exchange_ffn_wait"),
           [(tr(w_gate_up), tr(m_w_gate_up), tr(v_w_gate_up)), (w_down, m_w_down, v_w_down)])
    finish(["w_in"], _copies_wait(*proj[:3], grads["w_gate_up"], "exchange_in_wait"),
           [(tr(w_in), tr(m_w_in), tr(v_w_in))])
    finish(["w_branch", "w_out"], _copies_wait(*mixer[:3], grads["w_in"], "exchange_mix_wait"),
           [(w_branch, m_w_branch, v_w_branch), (w_out, m_w_out, v_w_out)])
    for nm in ("w_in", "w_gate_up"):
        grads[nm], upd[nm] = tr(grads[nm]), [tr(t) for t in upd[nm]]

    grads["ada_w"] = grad_ada_w.reshape(ada_w.shape)
    upd["ada_w"] = _adamw(ada_w, grads["ada_w"], m_ada_w, v_ada_w, "adamw_ada_w")
    small_names = ["ada_b", "norm1_g", "q_norm_g", "k_norm_g", "sinks", "a_log", "dt_bias", "dn_norm_g", "norm2_g", "conv_w"]
    small_w = [ada_b, norm1_g, q_norm_g, k_norm_g, sinks, a_log, dt_bias, dn_norm_g, norm2_g, conv_w]
    small_g = [grad_ada_b, g_n1, g_qg, g_kg, g_sk, g_al, g_dt, g_dn, g_n2, grad_conv]
    small_m = [m_ada_b, m_norm1_g, m_q_norm_g, m_k_norm_g, m_sinks, m_a_log, m_dt_bias, m_dn_norm_g, m_norm2_g, m_conv_w]
    small_v = [v_ada_b, v_norm1_g, v_q_norm_g, v_k_norm_g, v_sinks, v_a_log, v_dt_bias, v_dn_norm_g, v_norm2_g, v_conv_w]
    cat = lambda arrs: jnp.concatenate([a.reshape(1, -1) for a in arrs], axis=1)
    res = _adamw(cat(small_w), cat(small_g), cat(small_m), cat(small_v), "adamw_small")
    po = np.cumsum([0] + [int(np.prod(w.shape)) for w in small_w])
    for i, nm in enumerate(small_names):
        upd[nm] = tuple(r[:, po[i]:po[i + 1]].reshape(small_w[i].shape) for r in res)
        grads[nm] = small_g[i].reshape(small_w[i].shape)

    order = ["ada_w", "ada_b", "norm1_g", "w_in", "conv_w", "q_norm_g", "k_norm_g", "sinks", "a_log", "dt_bias",
             "dn_norm_g", "w_branch", "w_out", "norm2_g", "w_gate_up", "w_down"]
    return (loss, grad_x, *[grads[n] for n in order], *[upd[n][0] for n in order],
            *[upd[n][1] for n in order], *[upd[n][2] for n in order])
```

```python
import functools

import numpy as np
import jax
import jax.numpy as jnp
from jax import lax
from jax.experimental import pallas as pl
from jax.experimental.pallas import tpu as pltpu

F32 = jnp.float32
BF16 = jnp.bfloat16
HI = lax.Precision.HIGHEST

N_DEV = 8
D = 1024
HQ, HKV, HD = 8, 2, 64
GRP = HQ // HKV
BLK = 128
ROT = HD // 4
THETA = 500000.0
QW, KVW = HQ * HD, HKV * HD
DH, DK = 4, 128
CH = 64
DNW = DH * DK
CONV = 4
CONVW = 3 * DNW
FFN = 2816
EPS = 1e-6
IN_W = QW + 2 * KVW + CONVW + 2 * DH + DNW + 2 * D

LR, B1, B2, AEPS, WD, STEP = 0.001, 0.9, 0.999, 1e-08, 0.01, 10

VMEM_LIMIT = 56 * 1024 * 1024
MESH = pl.DeviceIdType.MESH


def _cparams(sem=None, vmem=VMEM_LIMIT):
    return pltpu.CompilerParams(dimension_semantics=sem, vmem_limit_bytes=vmem)


def _full(shape):
    n = len(shape)
    return pl.BlockSpec(shape, lambda *_: (0,) * n)


def _resident(shape):
    n = len(shape)
    return pl.BlockSpec(shape, lambda *_: (0,) * n, pipeline_mode=pl.Buffered(1))


def _rows(tm, w):
    return pl.BlockSpec((None, tm, w), lambda b, i: (b, i, 0))


def _stacked(n, tm, w):
    return pl.BlockSpec((None, n, tm, w), lambda b, i: (b, 0, i, 0))


def _perb(r, w):
    return pl.BlockSpec((None, r, w), lambda b, i: (b, 0, 0))


def _dot(a, b):
    return jnp.dot(a.astype(BF16), b.astype(BF16), preferred_element_type=F32)


def _dot_nt(a, b):
    return lax.dot_general(a.astype(BF16), b.astype(BF16), (((1,), (1,)), ((), ())), preferred_element_type=F32)


def _dot_tn(a, b):
    return lax.dot_general(a.astype(BF16), b.astype(BF16), (((0,), (0,)), ((), ())), preferred_element_type=F32)


def _dot_hi(a, b):
    return jnp.dot(a, b, preferred_element_type=F32, precision=HI)


def _sigmoid(x):
    return jax.nn.sigmoid(x)


def _silu(x):
    return x * jax.nn.sigmoid(x)


def _rms_mod(x, g, scale, shift):
    r = lax.rsqrt(jnp.mean(x * x, axis=-1, keepdims=True) + EPS)
    return (x * r * g) * (1.0 + scale) + shift


def _tile(S, rows=256):
    return min(rows, S)


def _peer(x, y, c, k):
    px = 1 - x if (k >> 2) & 1 else x
    py = 1 - y if (k >> 1) & 1 else y
    pc = 1 - c if k & 1 else c
    return px, py, pc


def _all_gather_small(v, name):
    r, n = v.shape

    def body(v_ref, out_ref, send_sems, recv_sems, local_sem):
        x, y, c = lax.axis_index("x"), lax.axis_index("y"), lax.axis_index("c")
        me = 4 * x + 2 * y + c
        mine = pltpu.make_async_copy(v_ref, out_ref.at[me], local_sem)
        mine.start()
        sends = []
        for k in range(1, N_DEV):
            cp = pltpu.make_async_remote_copy(
                src_ref=v_ref, dst_ref=out_ref.at[me], send_sem=send_sems.at[k - 1], recv_sem=recv_sems.at[k - 1],
                device_id=_peer(x, y, c, k), device_id_type=MESH)
            cp.start()
            sends.append(cp)
        for k in range(1, N_DEV):
            px, py, pc = _peer(x, y, c, k)
            pltpu.make_async_remote_copy(
                src_ref=v_ref, dst_ref=out_ref.at[4 * px + 2 * py + pc], send_sem=send_sems.at[k - 1],
                recv_sem=recv_sems.at[k - 1], device_id=(px, py, pc), device_id_type=MESH).wait_recv()
        for cp in sends:
            cp.wait_send()
        mine.wait()

    return pl.pallas_call(
        body, name=name,
        out_shape=jax.ShapeDtypeStruct((N_DEV, r, n), v.dtype),
        in_specs=[pl.BlockSpec(memory_space=pltpu.VMEM)],
        out_specs=pl.BlockSpec(memory_space=pltpu.VMEM),
        scratch_shapes=[pltpu.SemaphoreType.DMA((N_DEV - 1,)), pltpu.SemaphoreType.DMA((N_DEV - 1,)), pltpu.SemaphoreType.DMA],
    )(v)


def _all_gather_big(vs, name, after=()):
    na, nf = len(vs), len(after)

    def body(*refs):
        v_refs, out_refs = refs[:na], refs[na + nf:2 * na + nf]
        send_sems, recv_sems, local_sems = refs[2 * na + nf:]
        x, y, c = lax.axis_index("x"), lax.axis_index("y"), lax.axis_index("c")
        me, sibling = (x, y, c), (x, y, 1 - c)
        chips = [(1 - x, y), (x, 1 - y), (1 - x, 1 - y)]

        def rows(a, px, py, pc):
            return out_refs[a].at[4 * px + 2 * py + pc]

        def copy(a, k, block, to, src=None):
            return pltpu.make_async_remote_copy(
                src_ref=rows(a, *block) if src is None else src, dst_ref=rows(a, *block),
                send_sem=send_sems.at[7 * a + k], recv_sem=recv_sems.at[7 * a + k], device_id=to, device_id_type=MESH)

        mine = [pltpu.make_async_copy(v_refs[a], rows(a, *me), local_sems.at[a]) for a in range(na)]
        for cp in mine:
            cp.start()
        first = []
        for a in range(na):
            first.append(copy(a, 0, me, sibling, src=v_refs[a]))
            first += [copy(a, 1 + j, me, (*chip, c), src=v_refs[a]) for j, chip in enumerate(chips)]
        for cp in first:
            cp.start()
        passed = []
        for j, chip in enumerate(chips):
            for a in range(na):
                copy(a, 1 + j, (*chip, c), me).wait_recv()
                forward = copy(a, 4 + j, (*chip, c), sibling)
                forward.start()
                passed.append(forward)
        for a in range(na):
            copy(a, 0, sibling, me).wait_recv()
            for j, chip in enumerate(chips):
                copy(a, 4 + j, (*chip, 1 - c), me).wait_recv()
        for cp in first + passed:
            cp.wait_send()
        for cp in mine:
            cp.wait()

    return pl.pallas_call(
        body, name=name,
        out_shape=[jax.ShapeDtypeStruct((N_DEV,) + v.shape, v.dtype) for v in vs],
        in_specs=[pl.BlockSpec(memory_space=pl.ANY)] * (na + nf),
        out_specs=[pl.BlockSpec(memory_space=pl.ANY)] * na,
        scratch_shapes=[pltpu.SemaphoreType.DMA((7 * na,)), pltpu.SemaphoreType.DMA((7 * na,)),
                        pltpu.SemaphoreType.DMA((na,))],
    )(*vs, *after)


_HBM = pl.BlockSpec(memory_space=pltpu.HBM)
_SEM = pl.BlockSpec(memory_space=pltpu.SEMAPHORE)
_EFFECT = pltpu.SideEffectType.DATAFLOW_SIDE_EFFECTING


def _place_own(block, me):
    land = lax.empty((N_DEV,) + block.shape, block.dtype)
    return lax.dynamic_update_slice(land, block[None], (me,) + (0,) * block.ndim)


def _copies_start(srcs, lands, scatter, after, name):
    na = len(srcs)
    afters = tuple(after) if isinstance(after, (tuple, list)) else (after,)

    def body(*refs):
        src_refs, land_refs = refs[:na], refs[na:2 * na]
        sems = refs[2 * na + len(afters):4 * na + len(afters)]
        token = refs[-1]
        x, y, c = lax.axis_index("x"), lax.axis_index("y"), lax.axis_index("c")
        me = 4 * x + 2 * y + c
        for a in range(na):
            for k in range(1, N_DEV):
                px, py, pc = _peer(x, y, c, k)
                src = src_refs[a].at[4 * px + 2 * py + pc] if scatter else src_refs[a]
                pltpu.make_async_remote_copy(
                    src_ref=src, dst_ref=land_refs[a].at[me], send_sem=sems[2 * a], recv_sem=sems[2 * a + 1],
                    device_id=(px, py, pc), device_id_type=MESH).start()
        token[...] = jnp.zeros_like(token)

    hbm = lambda t: pltpu.HBM(t.shape, t.dtype)
    out = pl.pallas_call(
        body, name=name,
        out_shape=tuple([pltpu.SemaphoreType.DMA(())] * (2 * na) + [hbm(t) for t in srcs] + [hbm(t) for t in lands]
                        + [jax.ShapeDtypeStruct((8, 128), F32)]),
        in_specs=[_HBM] * (2 * na) + [pl.BlockSpec(memory_space=pl.ANY)] * len(afters),
        out_specs=tuple([_SEM] * (2 * na) + [_HBM] * (2 * na) + [pl.BlockSpec(memory_space=pltpu.VMEM)]),
        input_output_aliases={i: 2 * na + i for i in range(2 * na)},
        compiler_params=pltpu.CompilerParams(has_side_effects=_EFFECT),
    )(*[pltpu.with_memory_space_constraint(t, pltpu.HBM) for t in list(srcs) + list(lands)], *afters)
    return out[:2 * na], out[2 * na:3 * na], out[3 * na:4 * na], out[-1]


def _exchange_start(gs, me, after, name):
    own = [lax.dynamic_index_in_dim(g, me, 0, keepdims=False) for g in gs]
    return _copies_start(gs, [_place_own(o, me) for o in own], True, after, name)


def _copies_wait(sems, srcs, lands, after, name):
    na = len(srcs)

    def body(*refs):
        land_refs = refs[na:2 * na]
        sem_refs = refs[2 * na:4 * na]
        x, y, c = lax.axis_index("x"), lax.axis_index("y"), lax.axis_index("c")
        for a in range(na):
            seven = land_refs[a].at[pl.ds(0, N_DEV - 1)]
            copy = pltpu.make_async_remote_copy(
                src_ref=seven, dst_ref=seven, send_sem=sem_refs[2 * a], recv_sem=sem_refs[2 * a + 1],
                device_id=(x, y, c), device_id_type=MESH)
            copy.wait_send()
            copy.wait_recv()

    hbm = lambda t: pltpu.HBM(t.shape, t.dtype)
    out = pl.pallas_call(
        body, name=name,
        out_shape=tuple([hbm(t) for t in srcs] + [hbm(t) for t in lands]),
        in_specs=[_HBM] * (2 * na) + [_SEM] * (2 * na) + [pl.BlockSpec(memory_space=pl.ANY)],
        out_specs=tuple([_HBM] * (2 * na)),
        input_output_aliases={i: i for i in range(2 * na)},
        compiler_params=pltpu.CompilerParams(has_side_effects=_EFFECT),
    )(*srcs, *lands, *sems, after)
    return out[na:]


def _adamw_math(w, g, m, v):
    m = B1 * m + (1.0 - B1) * g
    v = B2 * v + (1.0 - B2) * (g * g)
    m_hat = m / (1.0 - B1 ** STEP)
    v_hat = v / (1.0 - B2 ** STEP)
    return -LR * (m_hat / (jnp.sqrt(v_hat) + AEPS) + WD * w), m, v


def _sum_adamw(parts, w, m, v, name):
    _, r, n = parts.shape
    tr = 256 if r % 256 == 0 else r

    def body(p_ref, w_ref, m_ref, v_ref, g_ref, d_ref, nm_ref, nv_ref):
        g = p_ref[0].astype(F32)
        for dev in range(1, N_DEV):
            g = g + p_ref[dev].astype(F32)
        g_ref[...] = g
        d_ref[...], nm_ref[...], nv_ref[...] = _adamw_math(w_ref[...], g, m_ref[...], v_ref[...])

    rows = pl.BlockSpec((None, tr, n), lambda i: (0, i, 0))
    sd = jax.ShapeDtypeStruct((1, r, n), F32)
    return pl.pallas_call(
        body, name=name, grid=(r // tr,), out_shape=(sd, sd, sd, sd),
        in_specs=[pl.BlockSpec((N_DEV, tr, n), lambda i: (0, i, 0)), rows, rows, rows],
        out_specs=(rows, rows, rows, rows),
        compiler_params=_cparams(("parallel",)),
    )(parts, w, m, v)


def _ada_fwd(c_all, ada_w, ada_b_cols):
    nb, ncol = c_all.shape[0], ada_w.shape[1]

    def body(c_ref, w_ref, b_ref, mod_ref, cond_ref):
        cond = _silu(c_ref[...])
        cond_ref[...] = cond
        mod_ref[...] = _dot_hi(cond, w_ref[...]) + b_ref[...]

    return pl.pallas_call(
        body, name="ada_fwd",
        out_shape=(jax.ShapeDtypeStruct((nb, ncol), F32), jax.ShapeDtypeStruct((nb, D), F32)),
        compiler_params=_cparams(),
    )(c_all, ada_w, ada_b_cols)


def _ada_bwd(cond_all, dmod_all, dmod_cols, smalls):
    ncol, nsm = dmod_cols.shape[1], smalls.shape[1]

    def body(cond_ref, dm_ref, dmc_ref, sm_ref, gw_ref, gb_ref, gs_ref):
        gw_ref[...] = lax.dot_general(cond_ref[...], dmc_ref[...], (((0,), (0,)), ((), ())),
                                      preferred_element_type=F32, precision=HI)
        gb_ref[...] = jnp.sum(dm_ref[...], axis=0, keepdims=True)
        gs_ref[...] = jnp.sum(sm_ref[...], axis=0, keepdims=True)

    return pl.pallas_call(
        body, name="ada_bwd",
        out_shape=(jax.ShapeDtypeStruct((D, ncol), F32), jax.ShapeDtypeStruct((1, 6 * D), F32),
                   jax.ShapeDtypeStruct((1, nsm), F32)),
        compiler_params=_cparams(),
    )(cond_all, dmod_all, dmod_cols, smalls)


IN_CUTS = (0, QW, QW + 2 * KVW, QW + 2 * KVW + CONVW, QW + 2 * KVW + CONVW + 2 * DH,
           QW + 2 * KVW + CONVW + 2 * DH + DNW, QW + 2 * KVW + CONVW + 2 * DH + DNW + D, IN_W)
IN_WIDTHS = tuple(b - a for a, b in zip(IN_CUTS[:-1], IN_CUTS[1:]))
IN_SHARD = IN_W // N_DEV


def _inproj_fwd(x, mod, g1, w_blk):
    B, S, _ = x.shape
    tm = _tile(S)

    def body(x_ref, mod_ref, g_ref, w_ref, h_ref, *o_refs):
        h = _rms_mod(x_ref[...], g_ref[...], mod_ref[1:2, :], mod_ref[0:1, :]).astype(BF16)
        h_ref[...] = h
        full = jnp.concatenate([_dot_nt(h, w_ref[j]) for j in range(N_DEV)], axis=1)
        for o_ref, lo, hi in zip(o_refs, IN_CUTS[:-1], IN_CUTS[1:]):
            o_ref[...] = full[:, lo:hi]

    return pl.pallas_call(
        body, name="inproj_fwd", grid=(B, S // tm),
        out_shape=[jax.ShapeDtypeStruct((B, S, D), BF16)] + [jax.ShapeDtypeStruct((B, S, w), F32) for w in IN_WIDTHS],
        in_specs=[_rows(tm, D), _perb(6, D), _full((1, D)), _resident(w_blk.shape)],
        out_specs=[_rows(tm, D)] + [_rows(tm, w) for w in IN_WIDTHS],
        compiler_params=_cparams(("parallel", "arbitrary")),
    )(x, mod, g1, w_blk)


def _inproj_bwd(x, mod, g1, dx1, dps, w_blk):
    B, S, _ = x.shape
    tm = _tile(S)
    n = len(dps)

    def body(x_ref, mod_ref, g_ref, dx1_ref, *refs):
        dp_refs, w_ref = refs[:n], refs[n]
        dblk_ref, gx_ref, dg_ref, dsc_ref, dsh_ref = refs[n + 1:]
        b, i = pl.program_id(0), pl.program_id(1)
        full = jnp.concatenate([r[...].astype(F32) for r in dp_refs], axis=1)
        dh = None
        for j in range(N_DEV):
            blk = full[:, IN_SHARD * j:IN_SHARD * (j + 1)].astype(BF16)
            dblk_ref[j] = blk
            t = jnp.dot(blk, w_ref[j], preferred_element_type=F32)
            dh = t if dh is None else dh + t
        _, vjp = jax.vjp(_rms_mod, x_ref[...], g_ref[...], mod_ref[1:2, :], mod_ref[0:1, :])
        dx, dg, dsc, dsh = vjp(dh)
        gx_ref[...] = dx1_ref[...] + dx

        @pl.when((b == 0) & (i == 0))
        def _():
            dg_ref[...] = jnp.zeros_like(dg_ref)

        @pl.when(i == 0)
        def _():
            dsc_ref[...] = jnp.zeros_like(dsc_ref)
            dsh_ref[...] = jnp.zeros_like(dsh_ref)

        dg_ref[...] += dg
        dsc_ref[...] += dsc
        dsh_ref[...] += dsh

    return pl.pallas_call(
        body, name="inproj_bwd", grid=(B, S // tm),
        out_shape=[jax.ShapeDtypeStruct((B, N_DEV, S, IN_SHARD), BF16), jax.ShapeDtypeStruct((B, S, D), F32),
                   jax.ShapeDtypeStruct((1, D), F32), jax.ShapeDtypeStruct((B, 1, D), F32),
                   jax.ShapeDtypeStruct((B, 1, D), F32)],
        in_specs=[_rows(tm, D), _perb(6, D), _full((1, D)), _rows(tm, D)]
                 + [_rows(tm, w) for w in IN_WIDTHS] + [_resident(w_blk.shape)],
        out_specs=[pl.BlockSpec((None, N_DEV, tm, IN_SHARD), lambda b, i: (b, 0, i, 0)), _rows(tm, D),
                   _full((1, D)), _perb(1, D), _perb(1, D)],
        compiler_params=_cparams(("arbitrary", "arbitrary")),
    )(x, mod, g1, dx1, *dps, w_blk)


def _wgrad(a, b, name, after=None):
    B, na, S, K = a.shape
    nb, N = b.shape[1], b.shape[3]
    G = max(na, nb)
    tm = min(2048, S)
    nt = S // tm
    last = B * nt - 1

    def body(a_ref, b_ref, *rest):
        o_ref, acc = rest[-2:]
        t = pl.program_id(1)

        @pl.when(t == 0)
        def _():
            acc[...] = jnp.zeros_like(acc)

        acc[...] += lax.dot_general(a_ref[...], b_ref[...], (((0,), (0,)), ((), ())), preferred_element_type=F32)

        @pl.when(t == last)
        def _():
            o_ref[...] = acc[...].astype(BF16)

    return pl.pallas_call(
        body, name=name, grid=(G, B * nt),
        out_shape=jax.ShapeDtypeStruct((G, K, N), BF16),
        in_specs=[pl.BlockSpec((None, None, tm, K), lambda g, t: (t // nt, g if na > 1 else 0, t % nt, 0)),
                  pl.BlockSpec((None, None, tm, N), lambda g, t: (t // nt, g if nb > 1 else 0, t % nt, 0))]
                 + ([] if after is None else [pl.BlockSpec(memory_space=pl.ANY)]),
        out_specs=pl.BlockSpec((None, K, N), lambda g, t: (g, 0, 0)),
        scratch_shapes=[pltpu.VMEM((K, N), F32)],
        compiler_params=_cparams(("parallel", "arbitrary")),
    )(*((a, b) if after is None else (a, b, after)))


LANES = 128


def _attn_consts():
    inv_freq = THETA ** (-jnp.arange(0, ROT, 2, dtype=F32) / ROT)
    head = jnp.concatenate([inv_freq, inv_freq, jnp.zeros((HD - ROT,), F32)])
    invf = jnp.tile(head, LANES // HD)[None, :]
    mean_of = lambda w: jnp.asarray(np.kron(np.eye(w // HD), np.full((HD, HD), 1.0 / HD)), BF16)
    return invf, mean_of(QW), mean_of(KVW)


def _rope_tables(pos, invf):
    B, S, _ = pos.shape
    tr = min(1024, S)

    def body(p_ref, f_ref, c_ref, s_ref):
        ang = p_ref[...].astype(F32) * f_ref[...]
        c_ref[...] = jnp.cos(ang)
        s_ref[...] = jnp.sin(ang)

    sd = jax.ShapeDtypeStruct((B, S, LANES), F32)
    return pl.pallas_call(
        body, name="rope_tables", grid=(B, S // tr), out_shape=[sd, sd],
        in_specs=[_rows(tr, 1), _full((1, LANES))], out_specs=[_rows(tr, LANES), _rows(tr, LANES)],
        compiler_params=_cparams(("parallel", "parallel")),
    )(pos, invf)


def _rope_expand(cos, sin, reps):
    lane = lax.broadcasted_iota(jnp.int32, cos.shape, 1) % HD
    sa = jnp.where((lane >= ROT // 2) & (lane < ROT), sin, 0.0)
    sb = jnp.where(lane < ROT // 2, -sin, 0.0)
    rep = lambda t: jnp.concatenate([t] * reps, axis=1) if reps > 1 else t
    return rep(cos), rep(sa), rep(sb)


@jax.custom_vjp
def _rope(t, cos, sa, sb):
    w = t.shape[1]
    return t * cos + pltpu.roll(t, ROT // 2, 1) * sa + pltpu.roll(t, w - ROT // 2, 1) * sb


def _rope_fwd(t, cos, sa, sb):
    return _rope(t, cos, sa, sb), (cos, sa, sb)


def _rope_bwd(res, d):
    cos, sa, sb = res
    w = d.shape[1]
    dt = d * cos + pltpu.roll(d * sa, w - ROT // 2, 1) + pltpu.roll(d * sb, ROT // 2, 1)
    return dt, jnp.zeros_like(cos), jnp.zeros_like(sa), jnp.zeros_like(sb)


_rope.defvjp(_rope_fwd, _rope_bwd)


def _head_norm(t, g, mean_of):
    hi, lo = _split(t * t)
    ms = jnp.dot(hi, mean_of, preferred_element_type=F32) + jnp.dot(lo, mean_of, preferred_element_type=F32)
    return t * lax.rsqrt(ms + EPS) * g


def _attn_block(q, kvp, kvc, qg, kg, sinks, tq, tk, mq, mk, valid):
    qn = _rope(_head_norm(q, jnp.concatenate([qg] * HQ, axis=1), mq), *tq) * (HD ** -0.5)
    kv = jnp.concatenate([kvp, kvc], axis=0)
    kn = _rope(_head_norm(kv[:, 0:KVW], jnp.concatenate([kg] * HKV, axis=1), mk), *tk)
    per_tile = LANES // HD
    vT = jnp.transpose(kv[:, KVW:2 * KVW])
    qT = [jnp.transpose(qn[:, LANES * t:LANES * (t + 1)]) for t in range(QW // LANES)]
    head_T = lambda h: qT[h // per_tile][HD * (h % per_tile):HD * (h % per_tile + 1), :]
    none = jnp.zeros((HD, GRP * BLK), F32)
    o_T = []
    for j in range(HKV):
        q4T = jnp.concatenate([head_T(GRP * j + i) for i in range(GRP)], axis=1)
        sT = _dot(kn, jnp.concatenate([q4T, none] if j == 0 else [none, q4T], axis=0))
        sT = jnp.where(valid, sT, -1e30)
        sink = jnp.concatenate([jnp.broadcast_to(sinks[:, GRP * j + i:GRP * j + i + 1], (1, BLK)) for i in range(GRP)], axis=1)
        m = lax.stop_gradient(jnp.maximum(jnp.max(sT, axis=0, keepdims=True), sink))
        pT = jnp.exp(sT - m)
        den = jnp.sum(pT, axis=0, keepdims=True) + jnp.exp(sink - m)
        oT = _dot(vT[HD * j:HD * (j + 1), :], pT) * (1.0 / den)
        o_T += [oT[:, BLK * i:BLK * (i + 1)] for i in range(GRP)]
    return jnp.concatenate([jnp.transpose(jnp.concatenate(o_T[per_tile * t:per_tile * (t + 1)], axis=0))
                            for t in range(QW // LANES)], axis=1)


def _attn_tables(cp_ref, cc_ref, sp_ref, sc_ref, n):
    tq = _rope_expand(cc_ref[...], sc_ref[...], QW // LANES)
    tk = _rope_expand(jnp.concatenate([cp_ref[...], cc_ref[...]], axis=0),
                      jnp.concatenate([sp_ref[...], sc_ref[...]], axis=0), KVW // LANES)
    qi = lax.broadcasted_iota(jnp.int32, (2 * BLK, GRP * BLK), 1) % BLK + BLK
    kj = lax.broadcasted_iota(jnp.int32, (2 * BLK, GRP * BLK), 0)
    dist = qi - kj
    valid = (dist >= 0) & (dist < BLK) & ((kj >= BLK) | (n > 0))
    return tq, tk, valid


def _attn_fwd(aq, akv, cos, sin, qg, kg, sinks, mq, mk):
    B, S, _ = aq.shape
    nb = S // BLK

    def body(q_ref, kvp_ref, kvc_ref, cp_ref, cc_ref, sp_ref, sc_ref, qg_ref, kg_ref, sk_ref, mq_ref, mk_ref, o_ref):
        tq, tk, valid = _attn_tables(cp_ref, cc_ref, sp_ref, sc_ref, pl.program_id(1))
        o_ref[...] = _attn_block(q_ref[...], kvp_ref[...], kvc_ref[...], qg_ref[...], kg_ref[...], sk_ref[...],
                                 tq, tk, mq_ref[...], mk_ref[...], valid)

    prev = lambda b, n: (b, jnp.maximum(n - 1, 0), 0)
    cur = lambda b, n: (b, n, 0)
    return pl.pallas_call(
        body, name="attn_fwd", grid=(B, nb),
        out_shape=jax.ShapeDtypeStruct((B, S, QW), F32),
        in_specs=[pl.BlockSpec((None, BLK, QW), cur), pl.BlockSpec((None, BLK, 2 * KVW), prev),
                  pl.BlockSpec((None, BLK, 2 * KVW), cur), pl.BlockSpec((None, BLK, LANES), prev),
                  pl.BlockSpec((None, BLK, LANES), cur), pl.BlockSpec((None, BLK, LANES), prev),
                  pl.BlockSpec((None, BLK, LANES), cur), _full((1, HD)), _full((1, HD)), _full((1, HQ)),
                  _full((QW, QW)), _full((KVW, KVW))],
        out_specs=pl.BlockSpec((None, BLK, QW), cur),
        compiler_params=_cparams(("parallel", "arbitrary")),
    )(aq, akv, akv, cos, cos, sin, sin, qg, kg, sinks, mq, mk)


def _attn_bwd(aq, akv, cos, sin, qg, kg, sinks, mq, mk, do):
    B, S, _ = aq.shape
    nb = S // BLK

    def body(q_ref, kvp_ref, kvc_ref, cp_ref, cc_ref, sp_ref, sc_ref, qg_ref, kg_ref, sk_ref, mq_ref, mk_ref, do_ref,
             dq_ref, dkv_ref, dqg_ref, dkg_ref, dsk_ref, carry):
        b, i = pl.program_id(0), pl.program_id(1)
        tq, tk, valid = _attn_tables(cp_ref, cc_ref, sp_ref, sc_ref, nb - 1 - i)
        fn = functools.partial(_attn_block, tq=tq, tk=tk, mq=mq_ref[...], mk=mk_ref[...], valid=valid)
        _, vjp = jax.vjp(fn, q_ref[...], kvp_ref[...], kvc_ref[...], qg_ref[...], kg_ref[...], sk_ref[...])
        dq, dkvp, dkvc, dqg, dkg, dsk = vjp(do_ref[...])

        @pl.when(i == 0)
        def _():
            carry[...] = jnp.zeros_like(carry)

        @pl.when((b == 0) & (i == 0))
        def _():
            dqg_ref[...] = jnp.zeros_like(dqg_ref)
            dkg_ref[...] = jnp.zeros_like(dkg_ref)
            dsk_ref[...] = jnp.zeros_like(dsk_ref)

        dq_ref[...] = dq.astype(BF16)
        dkv_ref[...] = (dkvc + carry[...]).astype(BF16)
        carry[...] = dkvp
        dqg_ref[...] += dqg
        dkg_ref[...] += dkg
        dsk_ref[...] += dsk

    prev = lambda b, i: (b, jnp.maximum(nb - 2 - i, 0), 0)
    cur = lambda b, i: (b, nb - 1 - i, 0)
    return pl.pallas_call(
        body, name="attn_bwd", grid=(B, nb),
        out_shape=[jax.ShapeDtypeStruct((B, S, QW), BF16), jax.ShapeDtypeStruct((B, S, 2 * KVW), BF16),
                   jax.ShapeDtypeStruct((1, HD), F32), jax.ShapeDtypeStruct((1, HD), F32),
                   jax.ShapeDtypeStruct((1, HQ), F32)],
        in_specs=[pl.BlockSpec((None, BLK, QW), cur), pl.BlockSpec((None, BLK, 2 * KVW), prev),
                  pl.BlockSpec((None, BLK, 2 * KVW), cur), pl.BlockSpec((None, BLK, LANES), prev),
                  pl.BlockSpec((None, BLK, LANES), cur), pl.BlockSpec((None, BLK, LANES), prev),
                  pl.BlockSpec((None, BLK, LANES), cur), _full((1, HD)), _full((1, HD)), _full((1, HQ)),
                  _full((QW, QW)), _full((KVW, KVW)), pl.BlockSpec((None, BLK, QW), cur)],
        out_specs=[pl.BlockSpec((None, BLK, QW), cur), pl.BlockSpec((None, BLK, 2 * KVW), cur),
                   _full((1, HD)), _full((1, HD)), _full((1, HQ))],
        scratch_shapes=[pltpu.VMEM((BLK, 2 * KVW), F32)],
        compiler_params=_cparams(("arbitrary", "arbitrary")),
    )(aq, akv, akv, cos, cos, sin, sin, qg, kg, sinks, mq, mk, do)


def _conv_taps(xe, w, rows):
    y = None
    for j in range(CONV):
        sh = pltpu.roll(xe, CONV - 1 - j, 0)[8:8 + rows, :] if j < CONV - 1 else xe[8:8 + rows, :]
        y = sh * w[j:j + 1, :] if y is None else y + sh * w[j:j + 1, :]
    return y


def _softplus(x):
    return jnp.maximum(x, 0.0) + jnp.log1p(jnp.exp(-jnp.abs(x)))


_BMM = (((2,), (1,)), ((0,), (0,)))
_BMM_NT = (((2,), (2,)), ((0,), (0,)))
_BMM_TN = (((1,), (1,)), ((0,), (0,)))


def _bmm(a, b, dims=_BMM):
    return lax.dot_general(a.astype(BF16), b.astype(BF16), dims, preferred_element_type=F32)


def _split(a):
    hi = a.astype(BF16)
    return hi, (a - hi.astype(F32)).astype(BF16)


def _bmm3(a, b, dims=_BMM):
    ah, al = _split(a)
    bh, bl = _split(b)
    d = lambda p, q: lax.dot_general(p, q, dims, preferred_element_type=F32)
    return d(ah, bh) + (d(ah, bl) + d(al, bh))


TRI_BASE = 8


def _tri_inverse(L):
    ii = lax.broadcasted_iota(jnp.int32, (CH, CH), 0)
    jj = lax.broadcasted_iota(jnp.int32, (CH, CH), 1)
    same = lambda size: (ii // size) == (jj // size)
    diag = jnp.where(same(TRI_BASE), L, 0.0)
    X = (ii == jj).astype(F32) - diag
    P = diag
    n = 2
    while n < TRI_BASE:
        P = _bmm3(P, P)
        X = X + _bmm3(X, P)
        n *= 2
    size = TRI_BASE
    while size < CH:
        joint = jnp.where(same(2 * size) & jnp.logical_not(same(size)), L, 0.0)
        X = X - _bmm3(X, _bmm3(joint, X))
        size *= 2
    return X


@jax.custom_vjp
def _tri_inverse_known(L, T):
    return T


def _tri_inverse_known_fwd(L, T):
    return T, T


def _tri_inverse_known_bwd(T, dT):
    Tt = jnp.swapaxes(T, 1, 2)
    return -_bmm(Tt, _bmm(dT, Tt)), jnp.zeros_like(T)


_tri_inverse_known.defvjp(_tri_inverse_known_fwd, _tri_inverse_known_bwd)


def _triangle(n, upper):
    ii = lax.broadcasted_iota(jnp.int32, (n, CH, CH), 1)
    jj = lax.broadcasted_iota(jnp.int32, (n, CH, CH), 2)
    return ((ii <= jj) if upper else (ii >= jj)).astype(BF16)


@jax.custom_vjp
def _cumsum_rows(g):
    g0 = g.astype(BF16)
    r1 = g - g0.astype(F32)
    g1 = r1.astype(BF16)
    g2 = (r1 - g1.astype(F32)).astype(BF16)
    tri = _triangle(g.shape[0], False)
    d = lambda q: lax.dot_general(tri, q, _BMM, preferred_element_type=F32)
    return d(g0) + (d(g1) + d(g2))


def _cumsum_rows_fwd(g):
    return _cumsum_rows(g), None


def _cumsum_rows_bwd(_, dy):
    hi, lo = _split(dy)
    tri = _triangle(dy.shape[0], True)
    d = lambda q: lax.dot_general(tri, q, _BMM, preferred_element_type=F32)
    return (d(hi) + d(lo),)


_cumsum_rows.defvjp(_cumsum_rows_fwd, _cumsum_rows_bwd)


def _row_sums(t):
    n, r, w = t.shape
    hi, lo = _split(t.reshape(n * r, w))
    ones = jnp.ones((w, w), BF16)
    s = jnp.dot(hi, ones, preferred_element_type=F32) + jnp.dot(lo, ones, preferred_element_type=F32)
    return s.reshape(n, r, w)


def _dn_prep(t_known, qr, kr, v, a_raw, b_raw, a_log, dt_b):
    n = qr.shape[0]
    ii = lax.broadcasted_iota(jnp.int32, (n, CH, CH), 1)
    jj = lax.broadcasted_iota(jnp.int32, (n, CH, CH), 2)
    incl, strict = ii >= jj, ii > jj
    q = qr * lax.rsqrt(_row_sums(qr * qr) + EPS) * (DK ** -0.5)
    k = kr * lax.rsqrt(_row_sums(kr * kr) + EPS)
    beta = _sigmoid(b_raw)
    g = -jnp.exp(a_log) * _softplus(a_raw + dt_b)
    gcb = _cumsum_rows(jnp.broadcast_to(g, (n, CH, DK)))
    gc = gcb[:, :, 0:1]
    gc_row = jnp.swapaxes(gcb, 1, 2)[:, 0:1, 0:CH]
    decay = jnp.where(incl, jnp.exp(jnp.where(incl, gc - gc_row, 0.0)), 0.0)
    kb = k * beta
    L = jnp.where(strict, _bmm(kb, k, _BMM_NT) * decay, 0.0)
    T = _tri_inverse(L) if t_known is None else _tri_inverse_known(L, t_known)
    eg = jnp.exp(gc)
    u = _bmm(T, v * beta)
    w = _bmm(T, kb * eg)
    a_in = _bmm(q, k, _BMM_NT) * decay
    g_last = gc[:, CH - 1:CH, :]
    return u, w, q * eg, k * jnp.exp(g_last - gc), a_in, jnp.exp(g_last), T


def _dn_step(S0, u, w, qd, kd, a_in, cd):
    r = _bmm(jnp.concatenate([w, qd], axis=1), S0)
    v_new = u - r[:, 0:CH, :]
    o = r[:, CH:2 * CH, :] + _bmm(a_in, v_new)
    S1 = S0 * cd + _bmm(kd, v_new, _BMM_TN)
    return o, S1


def _dn_stack(cq, ba, al, dt, G):
    cols = [[] for _ in range(7)]
    for c in range(G):
        rows = slice(CH * c, CH * (c + 1))
        for h in range(DH):
            parts = (cq[rows, DK * h:DK * (h + 1)], cq[rows, DNW + DK * h:DNW + DK * (h + 1)],
                     cq[rows, 2 * DNW + DK * h:2 * DNW + DK * (h + 1)], ba[rows, DH + h:DH + h + 1],
                     ba[rows, h:h + 1], al[:, h:h + 1], dt[:, h:h + 1])
            for col, p in zip(cols, parts):
                col.append(p)
    return tuple(jnp.stack(col) for col in cols)


def _dn_group(S, want):
    g = want
    while (S // CH) % g:
        g //= 2
    return g


def _dn_prep_fwd(xin, conv_w, ba, a_log, dt_b):
    B, S, _ = xin.shape
    nc = S // CH
    G = _dn_group(S, 4)
    r8 = G * CH // 8

    def body(xp_ref, x_ref, cw_ref, ba_ref, al_ref, dt_ref, cq_ref, u_ref, w_ref, qd_ref, kd_ref, a_ref, t_ref, cd_ref):
        xp = jnp.where(pl.program_id(1) > 0, xp_ref[...], 0.0)
        cq = _silu(_conv_taps(jnp.concatenate([xp, x_ref[...]], axis=0), cw_ref[...], G * CH))
        cq_ref[...] = cq
        ops = _dn_stack(cq, ba_ref[...], al_ref[...], dt_ref[...], G)
        u, w, qd, kd, a_in, cd, T = _dn_prep(None, *ops)
        lane4 = lax.broadcasted_iota(jnp.int32, (1, DH), 1)
        for c in range(G):
            rows = slice(CH * c, CH * (c + 1))
            cdrow = jnp.zeros((1, DH), F32)
            for h in range(DH):
                n = DH * c + h
                lanes = slice(DK * h, DK * (h + 1))
                u_ref[rows, lanes] = u[n]
                w_ref[rows, lanes] = w[n]
                qd_ref[rows, lanes] = qd[n]
                kd_ref[rows, lanes] = kd[n]
                a_ref[rows, CH * h:CH * (h + 1)] = a_in[n]
                t_ref[rows, CH * h:CH * (h + 1)] = T[n]
                cdrow = cdrow + jnp.where(lane4 == h, cd[n], 0.0)
            cd_ref[c] = cdrow

    wide = jax.ShapeDtypeStruct((B, S, DNW), F32)
    sq = jax.ShapeDtypeStruct((B, S, DH * CH), F32)
    return pl.pallas_call(
        body, name="dn_prep_fwd", grid=(B, nc // G),
        out_shape=[jax.ShapeDtypeStruct((B, S, CONVW), F32), wide, wide, wide, wide, sq, sq,
                   jax.ShapeDtypeStruct((B, nc, 1, DH), F32)],
        in_specs=[pl.BlockSpec((None, 8, CONVW), lambda b, i: (b, jnp.maximum(i * r8 - 1, 0), 0)),
                  _rows(G * CH, CONVW), _full((CONV, CONVW)), _rows(G * CH, 2 * DH), _full((1, DH)), _full((1, DH))],
        out_specs=[_rows(G * CH, CONVW)] + [_rows(G * CH, DNW)] * 4 + [_rows(G * CH, DH * CH)] * 2
                  + [pl.BlockSpec((None, G, 1, DH), lambda b, i: (b, i, 0, 0))],
        compiler_params=_cparams(("parallel", "arbitrary")),
    )(xin, xin, conv_w, ba, a_log, dt_b)


def _dn_seq_specs(B, steps, gs, rev):
    at = (lambda i: steps - 1 - i) if rev else (lambda i: i)
    wide = pl.BlockSpec((B, gs * CH, DNW), lambda i: (0, at(i), 0))
    a_spec = pl.BlockSpec((B, gs * CH, DH * CH), lambda i: (0, at(i), 0))
    cd_spec = pl.BlockSpec((B, gs, 1, DH), lambda i: (0, at(i), 0, 0))
    st_spec = pl.BlockSpec((B, gs, DH, DK, DK), lambda i: (0, at(i), 0, 0, 0))
    return wide, a_spec, cd_spec, st_spec


def _dn_step_operands(B, c, u_ref, w_ref, qd_ref, kd_ref, a_ref, cd_ref):
    pairs = [(b, h) for b in range(B) for h in range(DH)]
    rows = slice(CH * c, CH * (c + 1))
    wide = lambda ref: jnp.stack([ref[b, rows, DK * h:DK * (h + 1)] for b, h in pairs])
    a_in = jnp.stack([a_ref[b, rows, CH * h:CH * (h + 1)] for b, h in pairs])
    cd = jnp.stack([cd_ref[b, c, :, h:h + 1] for b, h in pairs])
    return wide(u_ref), wide(w_ref), wide(qd_ref), wide(kd_ref), a_in, cd


def _dn_seq_fwd(u, w, qd, kd, a_in, cd):
    B, S, _ = u.shape
    nc = S // CH
    gs = _dn_group(S, 8)

    def body(u_ref, w_ref, qd_ref, kd_ref, a_ref, cd_ref, o_ref, st_ref, state):
        @pl.when(pl.program_id(0) == 0)
        def _():
            state[...] = jnp.zeros_like(state)

        S0 = state[...]
        for c in range(gs):
            for b in range(B):
                st_ref[b, c] = S0[DH * b:DH * (b + 1)]
            o, S0 = _dn_step(S0, *_dn_step_operands(B, c, u_ref, w_ref, qd_ref, kd_ref, a_ref, cd_ref))
            for b in range(B):
                for h in range(DH):
                    o_ref[b, CH * c:CH * (c + 1), DK * h:DK * (h + 1)] = o[DH * b + h]
        state[...] = S0

    wide, a_spec, cd_spec, st_spec = _dn_seq_specs(B, nc // gs, gs, False)
    return pl.pallas_call(
        body, name="dn_seq_fwd", grid=(nc // gs,),
        out_shape=[jax.ShapeDtypeStruct((B, S, DNW), F32), jax.ShapeDtypeStruct((B, nc, DH, DK, DK), F32)],
        in_specs=[wide, wide, wide, wide, a_spec, cd_spec],
        out_specs=[wide, st_spec],
        scratch_shapes=[pltpu.VMEM((B * DH, DK, DK), F32)],
        compiler_params=_cparams(("arbitrary",)),
    )(u, w, qd, kd, a_in, cd)


def _dn_seq_bwd(u, w, qd, kd, a_in, cd, states, do):
    B, S, _ = u.shape
    nc = S // CH
    gs = _dn_group(S, 8)

    def body(u_ref, w_ref, qd_ref, kd_ref, a_ref, cd_ref, st_ref, do_ref,
             du_ref, dw_ref, dqd_ref, dkd_ref, da_ref, dcd_ref, dstate):
        @pl.when(pl.program_id(0) == 0)
        def _():
            dstate[...] = jnp.zeros_like(dstate)

        lane4 = lax.broadcasted_iota(jnp.int32, (1, DH), 1)
        dS = dstate[...]
        for c in reversed(range(gs)):
            rows = slice(CH * c, CH * (c + 1))
            S0 = jnp.concatenate([st_ref[b, c] for b in range(B)], axis=0)
            do = jnp.stack([do_ref[b, rows, DK * h:DK * (h + 1)] for b in range(B) for h in range(DH)])
            _, vjp = jax.vjp(_dn_step, S0, *_dn_step_operands(B, c, u_ref, w_ref, qd_ref, kd_ref, a_ref, cd_ref))
            dS, du, dw, dqd, dkd, da, dcd = vjp((do, dS))
            for b in range(B):
                dcdrow = jnp.zeros((1, DH), F32)
                for h in range(DH):
                    n = DH * b + h
                    lanes = slice(DK * h, DK * (h + 1))
                    du_ref[b, rows, lanes] = du[n]
                    dw_ref[b, rows, lanes] = dw[n]
                    dqd_ref[b, rows, lanes] = dqd[n]
                    dkd_ref[b, rows, lanes] = dkd[n]
                    da_ref[b, rows, CH * h:CH * (h + 1)] = da[n]
                    dcdrow = dcdrow + jnp.where(lane4 == h, dcd[n], 0.0)
                dcd_ref[b, c] = dcdrow
        dstate[...] = dS

    wide, a_spec, cd_spec, st_spec = _dn_seq_specs(B, nc // gs, gs, True)
    sd = jax.ShapeDtypeStruct((B, S, DNW), F32)
    return pl.pallas_call(
        body, name="dn_seq_bwd", grid=(nc // gs,),
        out_shape=[sd, sd, sd, sd, jax.ShapeDtypeStruct((B, S, DH * CH), F32), jax.ShapeDtypeStruct((B, nc, 1, DH), F32)],
        in_specs=[wide, wide, wide, wide, a_spec, cd_spec, st_spec, wide],
        out_specs=[wide, wide, wide, wide, a_spec, cd_spec],
        scratch_shapes=[pltpu.VMEM((B * DH, DK, DK), F32)],
        compiler_params=_cparams(("arbitrary",)),
    )(u, w, qd, kd, a_in, cd, states, do)


def _dn_prep_bwd(xin, conv_w, cq, ba, a_log, dt_b, t_inv, du, dw, dqd, dkd, da, dcd):
    B, S, _ = cq.shape
    nc = S // CH
    G = _dn_group(S, 4)
    R = G * CH
    nblk = nc // G
    r8 = R // 8

    def body(xp_ref, x_ref, cw_ref, cq_ref, ba_ref, al_ref, dt_ref, t_ref, du_ref, dw_ref, dqd_ref, dkd_ref, da_ref, dcd_ref,
             dx_ref, dcw_ref, dba_ref, dal_ref, ddt_ref, carry):
        i = pl.program_id(1)

        @pl.when((pl.program_id(0) == 0) & (i == 0))
        def _():
            dal_ref[...] = jnp.zeros_like(dal_ref)
            ddt_ref[...] = jnp.zeros_like(ddt_ref)
            dcw_ref[...] = jnp.zeros_like(dcw_ref)

        @pl.when(i == 0)
        def _():
            carry[...] = jnp.zeros_like(carry)

        pairs = [(c, h) for c in range(G) for h in range(DH)]
        rows = lambda c: slice(CH * c, CH * (c + 1))
        wide = lambda ref: jnp.stack([ref[rows(c), DK * h:DK * (h + 1)] for c, h in pairs])
        square = lambda ref: jnp.stack([ref[rows(c), CH * h:CH * (h + 1)] for c, h in pairs])
        ops = _dn_stack(cq_ref[...], ba_ref[...], al_ref[...], dt_ref[...], G)
        cots = (wide(du_ref), wide(dw_ref), wide(dqd_ref), wide(dkd_ref), square(da_ref),
                jnp.stack([dcd_ref[c][:, h:h + 1] for c, h in pairs]), jnp.zeros((len(pairs), CH, CH), F32))
        _, vjp = jax.vjp(functools.partial(_dn_prep, square(t_ref)), *ops)
        dq, dk, dv, dar, dbr, dl, dd = vjp(cots)
        lane8 = lax.broadcasted_iota(jnp.int32, (CH, 2 * DH), 1)
        lane4 = lax.broadcasted_iota(jnp.int32, (1, DH), 1)
        dal = jnp.zeros((1, DH), F32)
        ddt = jnp.zeros((1, DH), F32)
        for c in range(G):
            dba = jnp.zeros((CH, 2 * DH), F32)
            for h in range(DH):
                n = DH * c + h
                dba = dba + jnp.where(lane8 == h, dbr[n], 0.0) + jnp.where(lane8 == DH + h, dar[n], 0.0)
                dal = dal + jnp.where(lane4 == h, dl[n], 0.0)
                ddt = ddt + jnp.where(lane4 == h, dd[n], 0.0)
            dba_ref[rows(c), :] = dba.astype(BF16)
        dal_ref[...] += dal
        ddt_ref[...] += ddt

        dcq = jnp.concatenate([jnp.concatenate([t[DH * c + h] for t in (dq, dk, dv) for h in range(DH)], axis=1)
                               for c in range(G)], axis=0)
        w = cw_ref[...]
        xp = jnp.where(i < nblk - 1, xp_ref[...], 0.0)
        xe = jnp.concatenate([xp, x_ref[...]], axis=0)
        taps = [(pltpu.roll(xe, CONV - 1 - j, 0) if j < CONV - 1 else xe)[8:8 + R, :] for j in range(CONV)]
        pre = sum(t * w[j:j + 1, :] for j, t in enumerate(taps))
        sg = _sigmoid(pre)
        dpre = dcq * (sg * (1.0 + pre * (1.0 - sg)))
        ext = jnp.concatenate([dpre, carry[...]], axis=0)
        dx = dpre * w[CONV - 1:CONV, :]
        for j in range(CONV - 1):
            dx = dx + pltpu.roll(ext, R + 8 - (CONV - 1 - j), 0)[0:R, :] * w[j:j + 1, :]
        dx_ref[...] = dx.astype(BF16)
        carry[...] = dpre[0:8, :]
        lane_row = lax.broadcasted_iota(jnp.int32, (CONV, CONVW), 0)
        dcw = jnp.zeros((CONV, CONVW), F32)
        for j in range(CONV):
            dcw = dcw + jnp.where(lane_row == j, jnp.sum(taps[j] * dpre, axis=0, keepdims=True), 0.0)
        dcw_ref[...] += dcw

    rev = lambda w: pl.BlockSpec((None, R, w), lambda b, i: (b, nblk - 1 - i, 0))
    return pl.pallas_call(
        body, name="dn_prep_bwd", grid=(B, nblk),
        out_shape=[jax.ShapeDtypeStruct((B, S, CONVW), BF16), jax.ShapeDtypeStruct((CONV, CONVW), F32),
                   jax.ShapeDtypeStruct((B, S, 2 * DH), BF16), jax.ShapeDtypeStruct((1, DH), F32),
                   jax.ShapeDtypeStruct((1, DH), F32)],
        in_specs=[pl.BlockSpec((None, 8, CONVW), lambda b, i: (b, jnp.maximum((nblk - 1 - i) * r8 - 1, 0), 0)),
                  rev(CONVW), _full((CONV, CONVW)), rev(CONVW), rev(2 * DH), _full((1, DH)), _full((1, DH)), rev(DH * CH)]
                 + [rev(DNW)] * 4 + [rev(DH * CH), pl.BlockSpec((None, G, 1, DH), lambda b, i: (b, nblk - 1 - i, 0, 0))],
        out_specs=[rev(CONVW), _full((CONV, CONVW)), rev(2 * DH), _full((1, DH)), _full((1, DH))],
        scratch_shapes=[pltpu.VMEM((8, CONVW), F32)],
        compiler_params=_cparams(("arbitrary", "arbitrary")),
    )(xin, xin, conv_w, cq, ba, a_log, dt_b, t_inv, du, dw, dqd, dkd, da, dcd)


def _gated_norm(o, z, g):
    outs = []
    for h in range(DH):
        t = o[:, DK * h:DK * (h + 1)]
        r = lax.rsqrt(jnp.mean(t * t, axis=-1, keepdims=True) + EPS)
        outs.append(t * r * g * _silu(z[:, DK * h:DK * (h + 1)]))
    return jnp.concatenate(outs, axis=1)


def _mix_fwd(x, o_attn, o_dn, z, ga, gd, mod, dn_g, w_branch, w_out):
    B, S, _ = x.shape
    tm = _tile(S, 512)

    def body(x_ref, oa_ref, od_ref, z_ref, ga_ref, gd_ref, mod_ref, g_ref, wb_ref, wo_ref,
             x1_ref, mix_ref, mg_ref, ob_ref):
        oa = oa_ref[...].astype(BF16)
        od = _gated_norm(od_ref[...], z_ref[...], g_ref[...]).astype(BF16)
        ob_ref[0] = oa
        ob_ref[1] = od
        ya = jnp.dot(oa, wb_ref[0:QW, :], preferred_element_type=F32)
        yd = jnp.dot(od, wb_ref[QW:QW + DNW, :], preferred_element_type=F32)
        merged = (_sigmoid(ga_ref[...]) * ya + _sigmoid(gd_ref[...]) * yd).astype(BF16)
        mg_ref[...] = merged
        mix = jnp.dot(merged, wo_ref[...], preferred_element_type=F32)
        mix_ref[...] = mix
        x1_ref[...] = x_ref[...] + mod_ref[2:3, :] * mix

    return pl.pallas_call(
        body, name="mix_fwd", grid=(B, S // tm),
        out_shape=[jax.ShapeDtypeStruct((B, S, D), F32), jax.ShapeDtypeStruct((B, S, D), F32),
                   jax.ShapeDtypeStruct((B, S, D), BF16), jax.ShapeDtypeStruct((B, 2, S, QW), BF16)],
        in_specs=[_rows(tm, D), _rows(tm, QW), _rows(tm, DNW), _rows(tm, DNW), _rows(tm, D), _rows(tm, D),
                  _perb(6, D), _full((1, DK)), _resident(w_branch.shape), _resident(w_out.shape)],
        out_specs=[_rows(tm, D), _rows(tm, D), _rows(tm, D), _stacked(2, tm, QW)],
        compiler_params=_cparams(("parallel", "arbitrary")),
    )(x, o_attn, o_dn, z, ga, gd, mod, dn_g, w_branch, w_out)


def _mix_bwd(dx1, mix, o_attn, o_dn, z, ga, gd, mod, dn_g, w_branch, w_out):
    B, S, _ = dx1.shape
    tm = _tile(S, 512)

    def body(dx1_ref, mix_ref, oa_ref, od_ref, z_ref, ga_ref, gd_ref, mod_ref, g_ref, wb_ref, wo_ref,
             dmix_ref, dyo_ref, dga_ref, dgd_ref, dz_ref, doa_ref, dod_ref, dgate_ref, dg_ref):
        b, i = pl.program_id(0), pl.program_id(1)
        dx1 = dx1_ref[...]
        dmix = (dx1 * mod_ref[2:3, :]).astype(BF16)
        dmix_ref[...] = dmix
        dgate = jnp.sum(dx1 * mix_ref[...], axis=0, keepdims=True)
        dmerged = _dot_nt(dmix, wo_ref[...])
        odn, gn_vjp = jax.vjp(_gated_norm, od_ref[...], z_ref[...], g_ref[...])
        ya = _dot(oa_ref[...], wb_ref[0:QW, :])
        yd = _dot(odn, wb_ref[QW:QW + DNW, :])
        sa, sd = _sigmoid(ga_ref[...]), _sigmoid(gd_ref[...])
        dya = (dmerged * sa).astype(BF16)
        dyd = (dmerged * sd).astype(BF16)
        dyo_ref[0] = dya
        dyo_ref[1] = dyd
        dga_ref[...] = (dmerged * ya * sa * (1.0 - sa)).astype(BF16)
        dgd_ref[...] = (dmerged * yd * sd * (1.0 - sd)).astype(BF16)
        doa_ref[...] = _dot_nt(dya, wb_ref[0:QW, :])
        dodn = _dot_nt(dyd, wb_ref[QW:QW + DNW, :])
        dod, dz, dg = gn_vjp(dodn)
        dod_ref[...] = dod
        dz_ref[...] = dz.astype(BF16)

        @pl.when(i == 0)
        def _():
            dgate_ref[...] = jnp.zeros_like(dgate_ref)

        @pl.when((b == 0) & (i == 0))
        def _():
            dg_ref[...] = jnp.zeros_like(dg_ref)

        dgate_ref[...] += dgate
        dg_ref[...] += dg

    return pl.pallas_call(
        body, name="mix_bwd", grid=(B, S // tm),
        out_shape=[jax.ShapeDtypeStruct((B, S, D), BF16), jax.ShapeDtypeStruct((B, 2, S, D), BF16),
                   jax.ShapeDtypeStruct((B, S, D), BF16), jax.ShapeDtypeStruct((B, S, D), BF16),
                   jax.ShapeDtypeStruct((B, S, DNW), BF16),
                   jax.ShapeDtypeStruct((B, S, QW), F32), jax.ShapeDtypeStruct((B, S, DNW), F32),
                   jax.ShapeDtypeStruct((B, 1, D), F32), jax.ShapeDtypeStruct((1, DK), F32)],
        in_specs=[_rows(tm, D), _rows(tm, D), _rows(tm, QW), _rows(tm, DNW), _rows(tm, DNW), _rows(tm, D),
                  _rows(tm, D), _perb(6, D), _full((1, DK)), _resident(w_branch.shape), _resident(w_out.shape)],
        out_specs=[_rows(tm, D), _stacked(2, tm, D), _rows(tm, D), _rows(tm, D), _rows(tm, DNW),
                   _rows(tm, QW), _rows(tm, DNW), _perb(1, D), _full((1, DK))],
        compiler_params=_cparams(("arbitrary", "arbitrary")),
    )(dx1, mix, o_attn, o_dn, z, ga, gd, mod, dn_g, w_branch, w_out)


GU_SHARD = 2 * FFN // N_DEV
GU_HALF = N_DEV // 2


def _ffn1_fwd(x1, mod, g2, w_gu):
    B, S, _ = x1.shape
    tm = _tile(S)

    def body(x_ref, mod_ref, g_ref, w_ref, h_ref, gate_ref, up_ref, act_ref):
        h = _rms_mod(x_ref[...], g_ref[...], mod_ref[4:5, :], mod_ref[3:4, :]).astype(BF16)
        h_ref[...] = h
        for j in range(GU_HALF):
            gate = _dot_nt(h, w_ref[j])
            up = _dot_nt(h, w_ref[GU_HALF + j])
            gate_ref[j] = gate
            up_ref[j] = up
            act_ref[j] = (_silu(gate) * up).astype(BF16)

    blk = lambda dt: jax.ShapeDtypeStruct((B, GU_HALF, S, GU_SHARD), dt)
    return pl.pallas_call(
        body, name="ffn1_fwd", grid=(B, S // tm),
        out_shape=[jax.ShapeDtypeStruct((B, S, D), BF16), blk(F32), blk(F32), blk(BF16)],
        in_specs=[_rows(tm, D), _perb(6, D), _full((1, D)), _resident(w_gu.shape)],
        out_specs=[_rows(tm, D)] + [_stacked(GU_HALF, tm, GU_SHARD)] * 3,
        compiler_params=_cparams(("parallel", "arbitrary")),
    )(x1, mod, g2, w_gu)


def _ffn2_fwd(act, x1, target, mod, w_down):
    B, S, _ = x1.shape
    tm = _tile(S, 512)

    def body(a_ref, x_ref, t_ref, mod_ref, w_ref, dy_ref, loss_ref, dgate_ref):
        b, i = pl.program_id(0), pl.program_id(1)
        y = jnp.dot(a_ref[0], w_ref[0], preferred_element_type=F32)
        for j in range(1, GU_HALF):
            y = y + jnp.dot(a_ref[j], w_ref[j], preferred_element_type=F32)
        err = x_ref[...] + mod_ref[5:6, :] * y - t_ref[...]
        dy = err * (1.0 / D)
        dy_ref[...] = dy

        @pl.when((b == 0) & (i == 0))
        def _():
            loss_ref[...] = jnp.zeros_like(loss_ref)

        @pl.when(i == 0)
        def _():
            dgate_ref[...] = jnp.zeros_like(dgate_ref)

        loss_ref[...] += (0.5 / D) * jnp.sum(err * err)
        dgate_ref[...] += jnp.sum(dy * y, axis=0, keepdims=True)

    return pl.pallas_call(
        body, name="ffn2_fwd", grid=(B, S // tm),
        out_shape=[jax.ShapeDtypeStruct((B, S, D), F32), jax.ShapeDtypeStruct((1, 128), F32),
                   jax.ShapeDtypeStruct((B, 1, D), F32)],
        in_specs=[_stacked(GU_HALF, tm, GU_SHARD), _rows(tm, D), _rows(tm, D), _perb(6, D), _resident(w_down.shape)],
        out_specs=[_rows(tm, D), _full((1, 128)), _perb(1, D)],
        compiler_params=_cparams(("arbitrary", "arbitrary")),
    )(act, x1, target, mod, w_down)


def _ffn2_bwd(dy, gate, up, mod, w_down):
    B, S, _ = dy.shape
    tm = _tile(S)

    def body(dy_ref, gate_ref, up_ref, mod_ref, w_ref, dgu_ref, dyg_ref):
        dyg = (dy_ref[...] * mod_ref[5:6, :]).astype(BF16)
        dyg_ref[...] = dyg
        for j in range(GU_HALF):
            dact = _dot_nt(dyg, w_ref[j])
            gate, up = gate_ref[j], up_ref[j]
            sg = _sigmoid(gate)
            dgu_ref[j] = (dact * up * (sg * (1.0 + gate * (1.0 - sg)))).astype(BF16)
            dgu_ref[GU_HALF + j] = (dact * (gate * sg)).astype(BF16)

    return pl.pallas_call(
        body, name="ffn2_bwd", grid=(B, S // tm),
        out_shape=[jax.ShapeDtypeStruct((B, N_DEV, S, GU_SHARD), BF16), jax.ShapeDtypeStruct((B, S, D), BF16)],
        in_specs=[_rows(tm, D), _stacked(GU_HALF, tm, GU_SHARD), _stacked(GU_HALF, tm, GU_SHARD), _perb(6, D),
                  _resident(w_down.shape)],
        out_specs=[_stacked(N_DEV, tm, GU_SHARD), _rows(tm, D)],
        compiler_params=_cparams(("parallel", "arbitrary")),
    )(dy, gate, up, mod, w_down)


def _ffn1_bwd(dgu, x1, dy, mod, g2, w_gu):
    B, S, _ = x1.shape
    tm = _tile(S, 512)

    def body(dgu_ref, x_ref, dy_ref, mod_ref, g_ref, w_ref, dx1_ref, dg_ref, dsc_ref, dsh_ref):
        b, i = pl.program_id(0), pl.program_id(1)
        dh = jnp.dot(dgu_ref[0], w_ref[0], preferred_element_type=F32)
        for j in range(1, N_DEV):
            dh = dh + jnp.dot(dgu_ref[j], w_ref[j], preferred_element_type=F32)
        _, vjp = jax.vjp(_rms_mod, x_ref[...], g_ref[...], mod_ref[4:5, :], mod_ref[3:4, :])
        dx, dg, dsc, dsh = vjp(dh)
        dx1_ref[...] = dy_ref[...] + dx

        @pl.when((b == 0) & (i == 0))
        def _():
            dg_ref[...] = jnp.zeros_like(dg_ref)

        @pl.when(i == 0)
        def _():
            dsc_ref[...] = jnp.zeros_like(dsc_ref)
            dsh_ref[...] = jnp.zeros_like(dsh_ref)

        dg_ref[...] += dg
        dsc_ref[...] += dsc
        dsh_ref[...] += dsh

    return pl.pallas_call(
        body, name="ffn1_bwd", grid=(B, S // tm),
        out_shape=[jax.ShapeDtypeStruct((B, S, D), F32), jax.ShapeDtypeStruct((1, D), F32),
                   jax.ShapeDtypeStruct((B, 1, D), F32), jax.ShapeDtypeStruct((B, 1, D), F32)],
        in_specs=[_stacked(N_DEV, tm, GU_SHARD), _rows(tm, D), _rows(tm, D), _perb(6, D), _full((1, D)),
                  _resident(w_gu.shape)],
        out_specs=[_rows(tm, D), _full((1, D)), _perb(1, D), _perb(1, D)],
        compiler_params=_cparams(("arbitrary", "arbitrary")),
    )(dgu, x1, dy, mod, g2, w_gu)


def _adamw(w, g, m, v, name):
    def body(w_ref, g_ref, m_ref, v_ref, d_ref, nm_ref, nv_ref):
        d_ref[...], nm_ref[...], nv_ref[...] = _adamw_math(w_ref[...], g_ref[...], m_ref[...], v_ref[...])

    sd = jax.ShapeDtypeStruct(w.shape, F32)
    return pl.pallas_call(body, name=name, out_shape=(sd, sd, sd), compiler_params=_cparams())(w, g, m, v)


def kernel(x, c, positions, ada_w, ada_b, norm1_g, w_in, conv_w, q_norm_g, k_norm_g, sinks, a_log, dt_bias, dn_norm_g, w_branch, w_out, norm2_g, w_gate_up, w_down, loss_target, m_ada_w, m_ada_b, m_norm1_g, m_w_in, m_conv_w, m_q_norm_g, m_k_norm_g, m_sinks, m_a_log, m_dt_bias, m_dn_norm_g, m_w_branch, m_w_out, m_norm2_g, m_w_gate_up, m_w_down, v_ada_w, v_ada_b, v_norm1_g, v_w_in, v_conv_w, v_q_norm_g, v_k_norm_g, v_sinks, v_a_log, v_dt_bias, v_dn_norm_g, v_w_branch, v_w_out, v_norm2_g, v_w_gate_up, v_w_down):
    B, S, _ = x.shape
    me = 4 * lax.axis_index("x") + 2 * lax.axis_index("y") + lax.axis_index("c")

    tr = lambda t: jnp.swapaxes(t, 1, 2)
    shards = [w[0].astype(BF16) for w in (tr(w_in), w_branch, w_out, tr(w_gate_up), w_down)]

    c_all = _all_gather_small(c, "gather_c").reshape(N_DEV * B, D)
    ncol = 6 * D // N_DEV
    mod_cols, cond_all = _ada_fwd(c_all, ada_w[0], lax.dynamic_slice(ada_b, (0, me * ncol), (1, ncol)))
    mod_all = _all_gather_small(mod_cols, "gather_mod").transpose(1, 0, 2).reshape(N_DEV * B, 6 * D)
    mod = lax.dynamic_slice(mod_all, (me * B, 0), (B, 6 * D)).reshape(B, 6, D)
    conv2 = conv_w.reshape(CONV, CONVW // N_DEV)
    conv_all = _all_gather_small(conv2, "gather_conv").transpose(1, 0, 2).reshape(CONV, CONVW)

    (w_in_b,) = _all_gather_big(shards[:1], "gather_w_in", after=(mod, conv_all))
    w_sems, w_srcs, w_lands, w_token = _copies_start(shards[1:], [_place_own(s, me) for s in shards[1:]], False,
                                                    w_in_b, "gather_rest_start")

    h1, aq, akv, dnx, ba, z, ga, gd = _inproj_fwd(x, mod, norm1_g + w_token[0, 0], w_in_b)
    invf, mean_q, mean_k = _attn_consts()
    rope_cos, rope_sin = _rope_tables(positions.reshape(B, S, 1), invf)
    o_attn = _attn_fwd(aq, akv, rope_cos, rope_sin, q_norm_g, k_norm_g, sinks, mean_q, mean_k)
    cq, dn_u, dn_w, dn_qd, dn_kd, dn_a, dn_t, dn_cd = _dn_prep_fwd(dnx, conv_all, ba, a_log, dt_bias)
    o_dn, states = _dn_seq_fwd(dn_u, dn_w, dn_qd, dn_kd, dn_a, dn_cd)
    w_branch_g, w_out_g, w_gu_b, w_down_g = _copies_wait(w_sems, w_srcs, w_lands, o_dn, "gather_wait_rest")
    w_branch_f = w_branch_g.reshape(D, D)
    w_out_f = w_out_g.reshape(D, D)
    w_down_b = w_down_g.reshape(GU_HALF, GU_SHARD, D)
    x1, mix, merged, ob = _mix_fwd(x, o_attn, o_dn, z, ga, gd, mod, dn_norm_g, w_branch_f, w_out_f)
    h2, gate, up, act = _ffn1_fwd(x1, mod, norm2_g, w_gu_b)
    dy, loss_part, d_gate2 = _ffn2_fwd(act, x1, loss_target, mod, w_down_b)
    loss = lax.psum(loss_part[0, 0], ("x", "y", "c"))

    one = lambda t: t.reshape(B, 1, S, t.shape[-1])
    dgu, dyg = _ffn2_bwd(dy, gate, up, mod, w_down_b)
    g_w_down = _wgrad(act, one(dyg), "wgrad_down")
    dx1, d_n2g, d_scale2, d_shift2 = _ffn1_bwd(dgu, x1, dy, mod, norm2_g, w_gu_b)
    g_w_gu = _wgrad(dgu, one(h2), "wgrad_gate_up")
    ffn = _exchange_start([g_w_gu, g_w_down.reshape(N_DEV, FFN // N_DEV, D)], me, dx1, "exchange_ffn_start")
    dmix, dyo, dga, dgd, dz, d_oa, d_od, d_gate1, d_dng = _mix_bwd(
        dx1, mix, o_attn, o_dn, z, ga, gd, mod, dn_norm_g + ffn[3][0, 0], w_branch_f, w_out_f)
    d_dn = _dn_seq_bwd(dn_u, dn_w, dn_qd, dn_kd, dn_a, dn_cd, states, d_od)
    ddnx, d_conv, dba, d_alog, d_dtb = _dn_prep_bwd(dnx, conv_all, cq, ba, a_log, dt_bias, dn_t, *d_dn)
    daq, dakv, d_qg, d_kg, d_sinks = _attn_bwd(aq, akv, rope_cos, rope_sin, q_norm_g, k_norm_g, sinks, mean_q, mean_k, d_oa)
    dps = [daq, dakv, ddnx, dba, dz, dga, dgd]
    dblk, grad_x, d_n1g, d_scale1, d_shift1 = _inproj_bwd(x, mod, norm1_g, dx1, dps, w_in_b)

    dmod = jnp.concatenate([d_shift1, d_scale1, d_gate1, d_shift2, d_scale2, d_gate2], axis=2).reshape(B, 6 * D)
    small = jnp.concatenate([d_n1g, d_qg, d_kg, d_sinks, d_alog, d_dtb, d_dng, d_n2g, d_conv.reshape(1, CONV * CONVW)], axis=1)
    nsm = small.shape[1]
    width = -(-max(6 * D, nsm) // 128) * 128
    rows = jnp.concatenate([jnp.pad(dmod, ((0, 0), (0, width - 6 * D))), jnp.pad(small, ((0, 8 - B - 1), (0, width - nsm)))], axis=0)
    rows_all = _all_gather_small(rows, "gather_small")
    dmod_all = rows_all[:, 0:B, 0:6 * D].reshape(N_DEV * B, 6 * D)
    dmod_cols = lax.dynamic_slice(dmod_all, (0, me * ncol), (N_DEV * B, ncol))
    grad_ada_w, grad_ada_b, small_sum = _ada_bwd(cond_all, dmod_all, dmod_cols, rows_all[:, B, :])
    sizes = [D, HD, HD, HQ, DH, DH, DK, D]
    so = np.cumsum([0] + sizes)
    g_n1, g_qg, g_kg, g_sk, g_al, g_dt, g_dn, g_n2 = [small_sum[:, so[i]:so[i + 1]] for i in range(8)]
    g_conv_all = small_sum[:, so[8]:so[8] + CONV * CONVW].reshape(CONV, N_DEV, CONVW // N_DEV)
    grad_conv = lax.dynamic_slice(g_conv_all, (0, me, 0), (CONV, 1, CONVW // N_DEV)).reshape(CONV, CONVW // N_DEV)

    g_w_in = _wgrad(dblk, one(h1), "wgrad_in", after=small_sum)
    proj = _exchange_start([g_w_in], me, small_sum, "exchange_in_start")
    g_w_out = _wgrad(one(merged), one(dmix), "wgrad_out", after=proj[3])
    g_w_branch = _wgrad(ob, dyo, "wgrad_branch", after=proj[3])
    mixer = _exchange_start([g_w_branch.reshape(N_DEV, D // N_DEV, D), g_w_out.reshape(N_DEV, D // N_DEV, D)], me,
                            proj[3], "exchange_mix_start")

    upd, grads = {}, {}

    def finish(names, parts, weights):
        for nm, p, (w, m, v) in zip(names, parts, weights):
            grads[nm], *upd[nm] = _sum_adamw(p, w, m, v, "update_" + nm)

    finish(["w_gate_up", "w_down"], _copies_wait(*ffn[:3], mixer[3], "exchange_ffn_wait"),
           [(tr(w_gate_up), tr(m_w_gate_up), tr(v_w_gate_up)), (w_down, m_w_down, v_w_down)])
    finish(["w_in"], _copies_wait(*proj[:3], grads["w_gate_up"], "exchange_in_wait"),
           [(tr(w_in), tr(m_w_in), tr(v_w_in))])
    finish(["w_branch", "w_out"], _copies_wait(*mixer[:3], grads["w_in"], "exchange_mix_wait"),
           [(w_branch, m_w_branch, v_w_branch), (w_out, m_w_out, v_w_out)])
    for nm in ("w_in", "w_gate_up"):
        grads[nm], upd[nm] = tr(grads[nm]), [tr(t) for t in upd[nm]]

    grads["ada_w"] = grad_ada_w.reshape(ada_w.shape)
    upd["ada_w"] = _adamw(ada_w, grads["ada_w"], m_ada_w, v_ada_w, "adamw_ada_w")
    small_names = ["ada_b", "norm1_g", "q_norm_g", "k_norm_g", "sinks", "a_log", "dt_bias", "dn_norm_g", "norm2_g", "conv_w"]
    small_w = [ada_b, norm1_g, q_norm_g, k_norm_g, sinks, a_log, dt_bias, dn_norm_g, norm2_g, conv_w]
    small_g = [grad_ada_b, g_n1, g_qg, g_kg, g_sk, g_al, g_dt, g_dn, g_n2, grad_conv]
    small_m = [m_ada_b, m_norm1_g, m_q_norm_g, m_k_norm_g, m_sinks, m_a_log, m_dt_bias, m_dn_norm_g, m_norm2_g, m_conv_w]
    small_v = [v_ada_b, v_norm1_g, v_q_norm_g, v_k_norm_g, v_sinks, v_a_log, v_dt_bias, v_dn_norm_g, v_norm2_g, v_conv_w]
    cat = lambda arrs: jnp.concatenate([a.reshape(1, -1) for a in arrs], axis=1)
    res = _adamw(cat(small_w), cat(small_g), cat(small_m), cat(small_v), "adamw_small")
    po = np.cumsum([0] + [int(np.prod(w.shape)) for w in small_w])
    for i, nm in enumerate(small_names):
        upd[nm] = tuple(r[:, po[i]:po[i + 1]].reshape(small_w[i].shape) for r in res)
        grads[nm] = small_g[i].reshape(small_w[i].shape)

    order = ["ada_w", "ada_b", "norm1_g", "w_in", "conv_w", "q_norm_g", "k_norm_g", "sinks", "a_log", "dt_bias",
             "dn_norm_g", "w_branch", "w_out", "norm2_g", "w_gate_up", "w_down"]
    return (loss, grad_x, *[grads[n] for n in order], *[upd[n][0] for n in order],
            *[upd[n][1] for n in order], *[upd[n][2] for n in order])
```

```python
import functools

import numpy as np
import jax
import jax.numpy as jnp
from jax import lax
from jax.experimental import pallas as pl
from jax.experimental.pallas import tpu as pltpu

F32 = jnp.float32
BF16 = jnp.bfloat16
HI = lax.Precision.HIGHEST

N_DEV = 8
D = 1024
HQ, HKV, HD = 8, 2, 64
GRP = HQ // HKV
BLK = 128
ROT = HD // 4
THETA = 500000.0
QW, KVW = HQ * HD, HKV * HD
DH, DK = 4, 128
CH = 64
DNW = DH * DK
CONV = 4
CONVW = 3 * DNW
FFN = 2816
EPS = 1e-6
IN_W = QW + 2 * KVW + CONVW + 2 * DH + DNW + 2 * D

LR, B1, B2, AEPS, WD, STEP = 0.001, 0.9, 0.999, 1e-08, 0.01, 10

VMEM_LIMIT = 56 * 1024 * 1024
MESH = pl.DeviceIdType.MESH


def _cparams(sem=None, vmem=VMEM_LIMIT):
    return pltpu.CompilerParams(dimension_semantics=sem, vmem_limit_bytes=vmem)


def _full(shape):
    n = len(shape)
    return pl.BlockSpec(shape, lambda *_: (0,) * n)


def _resident(shape):
    n = len(shape)
    return pl.BlockSpec(shape, lambda *_: (0,) * n, pipeline_mode=pl.Buffered(1))


def _rows(tm, w):
    return pl.BlockSpec((None, tm, w), lambda b, i: (b, i, 0))


def _stacked(n, tm, w):
    return pl.BlockSpec((None, n, tm, w), lambda b, i: (b, 0, i, 0))


def _perb(r, w):
    return pl.BlockSpec((None, r, w), lambda b, i: (b, 0, 0))


def _dot(a, b):
    return jnp.dot(a.astype(BF16), b.astype(BF16), preferred_element_type=F32)


def _dot_nt(a, b):
    return lax.dot_general(a.astype(BF16), b.astype(BF16), (((1,), (1,)), ((), ())), preferred_element_type=F32)


def _dot_tn(a, b):
    return lax.dot_general(a.astype(BF16), b.astype(BF16), (((0,), (0,)), ((), ())), preferred_element_type=F32)


def _dot_hi(a, b):
    return jnp.dot(a, b, preferred_element_type=F32, precision=HI)


def _sigmoid(x):
    return jax.nn.sigmoid(x)


def _silu(x):
    return x * jax.nn.sigmoid(x)


def _rms_mod(x, g, scale, shift):
    r = lax.rsqrt(jnp.mean(x * x, axis=-1, keepdims=True) + EPS)
    return (x * r * g) * (1.0 + scale) + shift


def _tile(S, rows=256):
    return min(rows, S)


def _peer(x, y, c, k):
    px = 1 - x if (k >> 2) & 1 else x
    py = 1 - y if (k >> 1) & 1 else y
    pc = 1 - c if k & 1 else c
    return px, py, pc


def _all_gather_small(v, name):
    r, n = v.shape

    def body(v_ref, out_ref, send_sems, recv_sems, local_sem):
        x, y, c = lax.axis_index("x"), lax.axis_index("y"), lax.axis_index("c")
        me = 4 * x + 2 * y + c
        mine = pltpu.make_async_copy(v_ref, out_ref.at[me], local_sem)
        mine.start()
        sends = []
        for k in range(1, N_DEV):
            cp = pltpu.make_async_remote_copy(
                src_ref=v_ref, dst_ref=out_ref.at[me], send_sem=send_sems.at[k - 1], recv_sem=recv_sems.at[k - 1],
                device_id=_peer(x, y, c, k), device_id_type=MESH)
            cp.start()
            sends.append(cp)
        for k in range(1, N_DEV):
            px, py, pc = _peer(x, y, c, k)
            pltpu.make_async_remote_copy(
                src_ref=v_ref, dst_ref=out_ref.at[4 * px + 2 * py + pc], send_sem=send_sems.at[k - 1],
                recv_sem=recv_sems.at[k - 1], device_id=(px, py, pc), device_id_type=MESH).wait_recv()
        for cp in sends:
            cp.wait_send()
        mine.wait()

    return pl.pallas_call(
        body, name=name,
        out_shape=jax.ShapeDtypeStruct((N_DEV, r, n), v.dtype),
        in_specs=[pl.BlockSpec(memory_space=pltpu.VMEM)],
        out_specs=pl.BlockSpec(memory_space=pltpu.VMEM),
        scratch_shapes=[pltpu.SemaphoreType.DMA((N_DEV - 1,)), pltpu.SemaphoreType.DMA((N_DEV - 1,)), pltpu.SemaphoreType.DMA],
    )(v)


def _all_gather_big(vs, name, after=()):
    na, nf = len(vs), len(after)

    def body(*refs):
        v_refs, out_refs = refs[:na], refs[na + nf:2 * na + nf]
        send_sems, recv_sems, local_sems = refs[2 * na + nf:]
        x, y, c = lax.axis_index("x"), lax.axis_index("y"), lax.axis_index("c")
        me, sibling = (x, y, c), (x, y, 1 - c)
        chips = [(1 - x, y), (x, 1 - y), (1 - x, 1 - y)]

        def rows(a, px, py, pc):
            return out_refs[a].at[4 * px + 2 * py + pc]

        def copy(a, k, block, to, src=None):
            return pltpu.make_async_remote_copy(
                src_ref=rows(a, *block) if src is None else src, dst_ref=rows(a, *block),
                send_sem=send_sems.at[7 * a + k], recv_sem=recv_sems.at[7 * a + k], device_id=to, device_id_type=MESH)

        mine = [pltpu.make_async_copy(v_refs[a], rows(a, *me), local_sems.at[a]) for a in range(na)]
        for cp in mine:
            cp.start()
        first = []
        for a in range(na):
            first.append(copy(a, 0, me, sibling, src=v_refs[a]))
            first += [copy(a, 1 + j, me, (*chip, c), src=v_refs[a]) for j, chip in enumerate(chips)]
        for cp in first:
            cp.start()
        passed = []
        for j, chip in enumerate(chips):
            for a in range(na):
                copy(a, 1 + j, (*chip, c), me).wait_recv()
                forward = copy(a, 4 + j, (*chip, c), sibling)
                forward.start()
                passed.append(forward)
        for a in range(na):
            copy(a, 0, sibling, me).wait_recv()
            for j, chip in enumerate(chips):
                copy(a, 4 + j, (*chip, 1 - c), me).wait_recv()
        for cp in first + passed:
            cp.wait_send()
        for cp in mine:
            cp.wait()

    return pl.pallas_call(
        body, name=name,
        out_shape=[jax.ShapeDtypeStruct((N_DEV,) + v.shape, v.dtype) for v in vs],
        in_specs=[pl.BlockSpec(memory_space=pl.ANY)] * (na + nf),
        out_specs=[pl.BlockSpec(memory_space=pl.ANY)] * na,
        scratch_shapes=[pltpu.SemaphoreType.DMA((7 * na,)), pltpu.SemaphoreType.DMA((7 * na,)),
                        pltpu.SemaphoreType.DMA((na,))],
    )(*vs, *after)


_HBM = pl.BlockSpec(memory_space=pltpu.HBM)
_SEM = pl.BlockSpec(memory_space=pltpu.SEMAPHORE)
_EFFECT = pltpu.SideEffectType.DATAFLOW_SIDE_EFFECTING


def _place_own(block, me):
    land = lax.empty((N_DEV,) + block.shape, block.dtype)
    return lax.dynamic_update_slice(land, block[None], (me,) + (0,) * block.ndim)


def _copies_start(srcs, lands, scatter, after, name):
    na = len(srcs)
    afters = tuple(after) if isinstance(after, (tuple, list)) else (after,)

    def body(*refs):
        src_refs, land_refs = refs[:na], refs[na:2 * na]
        sems = refs[2 * na + len(afters):4 * na + len(afters)]
        token = refs[-1]
        x, y, c = lax.axis_index("x"), lax.axis_index("y"), lax.axis_index("c")
        me = 4 * x + 2 * y + c
        for a in range(na):
            for k in range(1, N_DEV):
                px, py, pc = _peer(x, y, c, k)
                src = src_refs[a].at[4 * px + 2 * py + pc] if scatter else src_refs[a]
                pltpu.make_async_remote_copy(
                    src_ref=src, dst_ref=land_refs[a].at[me], send_sem=sems[2 * a], recv_sem=sems[2 * a + 1],
                    device_id=(px, py, pc), device_id_type=MESH).start()
        token[...] = jnp.zeros_like(token)

    hbm = lambda t: pltpu.HBM(t.shape, t.dtype)
    out = pl.pallas_call(
        body, name=name,
        out_shape=tuple([pltpu.SemaphoreType.DMA(())] * (2 * na) + [hbm(t) for t in srcs] + [hbm(t) for t in lands]
                        + [jax.ShapeDtypeStruct((8, 128), F32)]),
        in_specs=[_HBM] * (2 * na) + [pl.BlockSpec(memory_space=pl.ANY)] * len(afters),
        out_specs=tuple([_SEM] * (2 * na) + [_HBM] * (2 * na) + [pl.BlockSpec(memory_space=pltpu.VMEM)]),
        input_output_aliases={i: 2 * na + i for i in range(2 * na)},
        compiler_params=pltpu.CompilerParams(has_side_effects=_EFFECT),
    )(*[pltpu.with_memory_space_constraint(t, pltpu.HBM) for t in list(srcs) + list(lands)], *afters)
    return out[:2 * na], out[2 * na:3 * na], out[3 * na:4 * na], out[-1]


def _exchange_start(gs, me, after, name):
    own = [lax.dynamic_index_in_dim(g, me, 0, keepdims=False) for g in gs]
    return _copies_start(gs, [_place_own(o, me) for o in own], True, after, name)


def _copies_wait(sems, srcs, lands, after, name):
    na = len(srcs)

    def body(*refs):
        land_refs = refs[na:2 * na]
        sem_refs = refs[2 * na:4 * na]
        x, y, c = lax.axis_index("x"), lax.axis_index("y"), lax.axis_index("c")
        for a in range(na):
            seven = land_refs[a].at[pl.ds(0, N_DEV - 1)]
            copy = pltpu.make_async_remote_copy(
                src_ref=seven, dst_ref=seven, send_sem=sem_refs[2 * a], recv_sem=sem_refs[2 * a + 1],
                device_id=(x, y, c), device_id_type=MESH)
            copy.wait_send()
            copy.wait_recv()

    hbm = lambda t: pltpu.HBM(t.shape, t.dtype)
    out = pl.pallas_call(
        body, name=name,
        out_shape=tuple([hbm(t) for t in srcs] + [hbm(t) for t in lands]),
        in_specs=[_HBM] * (2 * na) + [_SEM] * (2 * na) + [pl.BlockSpec(memory_space=pl.ANY)],
        out_specs=tuple([_HBM] * (2 * na)),
        input_output_aliases={i: i for i in range(2 * na)},
        compiler_params=pltpu.CompilerParams(has_side_effects=_EFFECT),
    )(*srcs, *lands, *sems, after)
    return out[na:]


def _adamw_math(w, g, m, v):
    m = B1 * m + (1.0 - B1) * g
    v = B2 * v + (1.0 - B2) * (g * g)
    m_hat = m / (1.0 - B1 ** STEP)
    v_hat = v / (1.0 - B2 ** STEP)
    return -LR * (m_hat / (jnp.sqrt(v_hat) + AEPS) + WD * w), m, v


def _sum_adamw(parts, w, m, v, name):
    _, r, n = parts.shape
    tr = 256 if r % 256 == 0 else r

    def body(p_ref, w_ref, m_ref, v_ref, g_ref, d_ref, nm_ref, nv_ref):
        g = p_ref[0].astype(F32)
        for dev in range(1, N_DEV):
            g = g + p_ref[dev].astype(F32)
        g_ref[...] = g
        d_ref[...], nm_ref[...], nv_ref[...] = _adamw_math(w_ref[...], g, m_ref[...], v_ref[...])

    rows = pl.BlockSpec((None, tr, n), lambda i: (0, i, 0))
    sd = jax.ShapeDtypeStruct((1, r, n), F32)
    return pl.pallas_call(
        body, name=name, grid=(r // tr,), out_shape=(sd, sd, sd, sd),
        in_specs=[pl.BlockSpec((N_DEV, tr, n), lambda i: (0, i, 0)), rows, rows, rows],
        out_specs=(rows, rows, rows, rows),
        compiler_params=_cparams(("parallel",)),
    )(parts, w, m, v)


def _ada_fwd(c_all, ada_w, ada_b_cols):
    nb, ncol = c_all.shape[0], ada_w.shape[1]

    def body(c_ref, w_ref, b_ref, mod_ref, cond_ref):
        cond = _silu(c_ref[...])
        cond_ref[...] = cond
        mod_ref[...] = _dot_hi(cond, w_ref[...]) + b_ref[...]

    return pl.pallas_call(
        body, name="ada_fwd",
        out_shape=(jax.ShapeDtypeStruct((nb, ncol), F32), jax.ShapeDtypeStruct((nb, D), F32)),
        compiler_params=_cparams(),
    )(c_all, ada_w, ada_b_cols)


def _ada_bwd(cond_all, dmod_all, dmod_cols, smalls):
    ncol, nsm = dmod_cols.shape[1], smalls.shape[1]

    def body(cond_ref, dm_ref, dmc_ref, sm_ref, gw_ref, gb_ref, gs_ref):
        gw_ref[...] = lax.dot_general(cond_ref[...], dmc_ref[...], (((0,), (0,)), ((), ())),
                                      preferred_element_type=F32, precision=HI)
        gb_ref[...] = jnp.sum(dm_ref[...], axis=0, keepdims=True)
        gs_ref[...] = jnp.sum(sm_ref[...], axis=0, keepdims=True)

    return pl.pallas_call(
        body, name="ada_bwd",
        out_shape=(jax.ShapeDtypeStruct((D, ncol), F32), jax.ShapeDtypeStruct((1, 6 * D), F32),
                   jax.ShapeDtypeStruct((1, nsm), F32)),
        compiler_params=_cparams(),
    )(cond_all, dmod_all, dmod_cols, smalls)


IN_CUTS = (0, QW, QW + 2 * KVW, QW + 2 * KVW + CONVW, QW + 2 * KVW + CONVW + 2 * DH,
           QW + 2 * KVW + CONVW + 2 * DH + DNW, QW + 2 * KVW + CONVW + 2 * DH + DNW + D, IN_W)
IN_WIDTHS = tuple(b - a for a, b in zip(IN_CUTS[:-1], IN_CUTS[1:]))
IN_SHARD = IN_W // N_DEV


def _inproj_fwd(x, mod, g1, w_blk):
    B, S, _ = x.shape
    tm = _tile(S)

    def body(x_ref, mod_ref, g_ref, w_ref, h_ref, *o_refs):
        h = _rms_mod(x_ref[...], g_ref[...], mod_ref[1:2, :], mod_ref[0:1, :]).astype(BF16)
        h_ref[...] = h
        full = jnp.concatenate([_dot_nt(h, w_ref[j]) for j in range(N_DEV)], axis=1)
        for o_ref, lo, hi in zip(o_refs, IN_CUTS[:-1], IN_CUTS[1:]):
            o_ref[...] = full[:, lo:hi]

    return pl.pallas_call(
        body, name="inproj_fwd", grid=(B, S // tm),
        out_shape=[jax.ShapeDtypeStruct((B, S, D), BF16)] + [jax.ShapeDtypeStruct((B, S, w), F32) for w in IN_WIDTHS],
        in_specs=[_rows(tm, D), _perb(6, D), _full((1, D)), _resident(w_blk.shape)],
        out_specs=[_rows(tm, D)] + [_rows(tm, w) for w in IN_WIDTHS],
        compiler_params=_cparams(("parallel", "arbitrary")),
    )(x, mod, g1, w_blk)


def _inproj_bwd(x, mod, g1, dx1, dps, w_blk):
    B, S, _ = x.shape
    tm = _tile(S)
    n = len(dps)

    def body(x_ref, mod_ref, g_ref, dx1_ref, *refs):
        dp_refs, w_ref = refs[:n], refs[n]
        dblk_ref, gx_ref, dg_ref, dsc_ref, dsh_ref = refs[n + 1:]
        b, i = pl.program_id(0), pl.program_id(1)
        full = jnp.concatenate([r[...].astype(F32) for r in dp_refs], axis=1)
        dh = None
        for j in range(N_DEV):
            blk = full[:, IN_SHARD * j:IN_SHARD * (j + 1)].astype(BF16)
            dblk_ref[j] = blk
            t = jnp.dot(blk, w_ref[j], preferred_element_type=F32)
            dh = t if dh is None else dh + t
        _, vjp = jax.vjp(_rms_mod, x_ref[...], g_ref[...], mod_ref[1:2, :], mod_ref[0:1, :])
        dx, dg, dsc, dsh = vjp(dh)
        gx_ref[...] = dx1_ref[...] + dx

        @pl.when((b == 0) & (i == 0))
        def _():
            dg_ref[...] = jnp.zeros_like(dg_ref)

        @pl.when(i == 0)
        def _():
            dsc_ref[...] = jnp.zeros_like(dsc_ref)
            dsh_ref[...] = jnp.zeros_like(dsh_ref)

        dg_ref[...] += dg
        dsc_ref[...] += dsc
        dsh_ref[...] += dsh

    return pl.pallas_call(
        body, name="inproj_bwd", grid=(B, S // tm),
        out_shape=[jax.ShapeDtypeStruct((B, N_DEV, S, IN_SHARD), BF16), jax.ShapeDtypeStruct((B, S, D), F32),
                   jax.ShapeDtypeStruct((1, D), F32), jax.ShapeDtypeStruct((B, 1, D), F32),
                   jax.ShapeDtypeStruct((B, 1, D), F32)],
        in_specs=[_rows(tm, D), _perb(6, D), _full((1, D)), _rows(tm, D)]
                 + [_rows(tm, w) for w in IN_WIDTHS] + [_resident(w_blk.shape)],
        out_specs=[pl.BlockSpec((None, N_DEV, tm, IN_SHARD), lambda b, i: (b, 0, i, 0)), _rows(tm, D),
                   _full((1, D)), _perb(1, D), _perb(1, D)],
        compiler_params=_cparams(("arbitrary", "arbitrary")),
    )(x, mod, g1, dx1, *dps, w_blk)


def _wgrad(a, b, name, after=None):
    B, na, S, K = a.shape
    nb, N = b.shape[1], b.shape[3]
    G = max(na, nb)
    tm = min(4096, S)
    nt = S // tm
    last = B * nt - 1

    def body(a_ref, b_ref, *rest):
        o_ref, acc = rest[-2:]
        t = pl.program_id(1)

        @pl.when(t == 0)
        def _():
            acc[...] = jnp.zeros_like(acc)

        acc[...] += lax.dot_general(a_ref[...], b_ref[...], (((0,), (0,)), ((), ())), preferred_element_type=F32)

        @pl.when(t == last)
        def _():
            o_ref[...] = acc[...].astype(BF16)

    return pl.pallas_call(
        body, name=name, grid=(G, B * nt),
        out_shape=jax.ShapeDtypeStruct((G, K, N), BF16),
        in_specs=[pl.BlockSpec((None, None, tm, K), lambda g, t: (t // nt, g if na > 1 else 0, t % nt, 0)),
                  pl.BlockSpec((None, None, tm, N), lambda g, t: (t // nt, g if nb > 1 else 0, t % nt, 0))]
                 + ([] if after is None else [pl.BlockSpec(memory_space=pl.ANY)]),
        out_specs=pl.BlockSpec((None, K, N), lambda g, t: (g, 0, 0)),
        scratch_shapes=[pltpu.VMEM((K, N), F32)],
        compiler_params=_cparams(("parallel", "arbitrary")),
    )(*((a, b) if after is None else (a, b, after)))


LANES = 128


def _attn_consts():
    inv_freq = THETA ** (-jnp.arange(0, ROT, 2, dtype=F32) / ROT)
    head = jnp.concatenate([inv_freq, inv_freq, jnp.zeros((HD - ROT,), F32)])
    invf = jnp.tile(head, LANES // HD)[None, :]
    mean_of = lambda w: jnp.asarray(np.kron(np.eye(w // HD), np.full((HD, HD), 1.0 / HD)), BF16)
    return invf, mean_of(QW), mean_of(KVW)


def _rope_tables(pos, invf):
    B, S, _ = pos.shape
    tr = min(1024, S)

    def body(p_ref, f_ref, c_ref, s_ref):
        ang = p_ref[...].astype(F32) * f_ref[...]
        c_ref[...] = jnp.cos(ang)
        s_ref[...] = jnp.sin(ang)

    sd = jax.ShapeDtypeStruct((B, S, LANES), F32)
    return pl.pallas_call(
        body, name="rope_tables", grid=(B, S // tr), out_shape=[sd, sd],
        in_specs=[_rows(tr, 1), _full((1, LANES))], out_specs=[_rows(tr, LANES), _rows(tr, LANES)],
        compiler_params=_cparams(("parallel", "parallel")),
    )(pos, invf)


def _rope_expand(cos, sin, reps):
    lane = lax.broadcasted_iota(jnp.int32, cos.shape, 1) % HD
    sa = jnp.where((lane >= ROT // 2) & (lane < ROT), sin, 0.0)
    sb = jnp.where(lane < ROT // 2, -sin, 0.0)
    rep = lambda t: jnp.concatenate([t] * reps, axis=1) if reps > 1 else t
    return rep(cos), rep(sa), rep(sb)


@jax.custom_vjp
def _rope(t, cos, sa, sb):
    w = t.shape[1]
    return t * cos + pltpu.roll(t, ROT // 2, 1) * sa + pltpu.roll(t, w - ROT // 2, 1) * sb


def _rope_fwd(t, cos, sa, sb):
    return _rope(t, cos, sa, sb), (cos, sa, sb)


def _rope_bwd(res, d):
    cos, sa, sb = res
    w = d.shape[1]
    dt = d * cos + pltpu.roll(d * sa, w - ROT // 2, 1) + pltpu.roll(d * sb, ROT // 2, 1)
    return dt, jnp.zeros_like(cos), jnp.zeros_like(sa), jnp.zeros_like(sb)


_rope.defvjp(_rope_fwd, _rope_bwd)


def _head_norm(t, g, mean_of):
    hi, lo = _split(t * t)
    ms = jnp.dot(hi, mean_of, preferred_element_type=F32) + jnp.dot(lo, mean_of, preferred_element_type=F32)
    return t * lax.rsqrt(ms + EPS) * g


def _attn_block(q, kvp, kvc, qg, kg, sinks, tq, tk, mq, mk, valid):
    qn = _rope(_head_norm(q, jnp.concatenate([qg] * HQ, axis=1), mq), *tq) * (HD ** -0.5)
    kv = jnp.concatenate([kvp, kvc], axis=0)
    kn = _rope(_head_norm(kv[:, 0:KVW], jnp.concatenate([kg] * HKV, axis=1), mk), *tk)
    per_tile = LANES // HD
    vT = jnp.transpose(kv[:, KVW:2 * KVW])
    qT = [jnp.transpose(qn[:, LANES * t:LANES * (t + 1)]) for t in range(QW // LANES)]
    head_T = lambda h: qT[h // per_tile][HD * (h % per_tile):HD * (h % per_tile + 1), :]
    none = jnp.zeros((HD, GRP * BLK), F32)
    o_T = []
    for j in range(HKV):
        q4T = jnp.concatenate([head_T(GRP * j + i) for i in range(GRP)], axis=1)
        sT = _dot(kn, jnp.concatenate([q4T, none] if j == 0 else [none, q4T], axis=0))
        sT = jnp.where(valid, sT, -1e30)
        sink = jnp.concatenate([jnp.broadcast_to(sinks[:, GRP * j + i:GRP * j + i + 1], (1, BLK)) for i in range(GRP)], axis=1)
        m = lax.stop_gradient(jnp.maximum(jnp.max(sT, axis=0, keepdims=True), sink))
        pT = jnp.exp(sT - m)
        den = jnp.sum(pT, axis=0, keepdims=True) + jnp.exp(sink - m)
        oT = _dot(vT[HD * j:HD * (j + 1), :], pT) * (1.0 / den)
        o_T += [oT[:, BLK * i:BLK * (i + 1)] for i in range(GRP)]
    return jnp.concatenate([jnp.transpose(jnp.concatenate(o_T[per_tile * t:per_tile * (t + 1)], axis=0))
                            for t in range(QW // LANES)], axis=1)


def _attn_tables(cp_ref, cc_ref, sp_ref, sc_ref, n):
    tq = _rope_expand(cc_ref[...], sc_ref[...], QW // LANES)
    tk = _rope_expand(jnp.concatenate([cp_ref[...], cc_ref[...]], axis=0),
                      jnp.concatenate([sp_ref[...], sc_ref[...]], axis=0), KVW // LANES)
    qi = lax.broadcasted_iota(jnp.int32, (2 * BLK, GRP * BLK), 1) % BLK + BLK
    kj = lax.broadcasted_iota(jnp.int32, (2 * BLK, GRP * BLK), 0)
    dist = qi - kj
    valid = (dist >= 0) & (dist < BLK) & ((kj >= BLK) | (n > 0))
    return tq, tk, valid


def _attn_fwd(aq, akv, cos, sin, qg, kg, sinks, mq, mk):
    B, S, _ = aq.shape
    nb = S // BLK

    def body(q_ref, kvp_ref, kvc_ref, cp_ref, cc_ref, sp_ref, sc_ref, qg_ref, kg_ref, sk_ref, mq_ref, mk_ref, o_ref):
        tq, tk, valid = _attn_tables(cp_ref, cc_ref, sp_ref, sc_ref, pl.program_id(1))
        o_ref[...] = _attn_block(q_ref[...], kvp_ref[...], kvc_ref[...], qg_ref[...], kg_ref[...], sk_ref[...],
                                 tq, tk, mq_ref[...], mk_ref[...], valid)

    prev = lambda b, n: (b, jnp.maximum(n - 1, 0), 0)
    cur = lambda b, n: (b, n, 0)
    return pl.pallas_call(
        body, name="attn_fwd", grid=(B, nb),
        out_shape=jax.ShapeDtypeStruct((B, S, QW), F32),
        in_specs=[pl.BlockSpec((None, BLK, QW), cur), pl.BlockSpec((None, BLK, 2 * KVW), prev),
                  pl.BlockSpec((None, BLK, 2 * KVW), cur), pl.BlockSpec((None, BLK, LANES), prev),
                  pl.BlockSpec((None, BLK, LANES), cur), pl.BlockSpec((None, BLK, LANES), prev),
                  pl.BlockSpec((None, BLK, LANES), cur), _full((1, HD)), _full((1, HD)), _full((1, HQ)),
                  _full((QW, QW)), _full((KVW, KVW))],
        out_specs=pl.BlockSpec((None, BLK, QW), cur),
        compiler_params=_cparams(("parallel", "arbitrary")),
    )(aq, akv, akv, cos, cos, sin, sin, qg, kg, sinks, mq, mk)


def _attn_bwd(aq, akv, cos, sin, qg, kg, sinks, mq, mk, do):
    B, S, _ = aq.shape
    nb = S // BLK

    def body(q_ref, kvp_ref, kvc_ref, cp_ref, cc_ref, sp_ref, sc_ref, qg_ref, kg_ref, sk_ref, mq_ref, mk_ref, do_ref,
             dq_ref, dkv_ref, dqg_ref, dkg_ref, dsk_ref, carry):
        b, i = pl.program_id(0), pl.program_id(1)
        tq, tk, valid = _attn_tables(cp_ref, cc_ref, sp_ref, sc_ref, nb - 1 - i)
        fn = functools.partial(_attn_block, tq=tq, tk=tk, mq=mq_ref[...], mk=mk_ref[...], valid=valid)
        _, vjp = jax.vjp(fn, q_ref[...], kvp_ref[...], kvc_ref[...], qg_ref[...], kg_ref[...], sk_ref[...])
        dq, dkvp, dkvc, dqg, dkg, dsk = vjp(do_ref[...])

        @pl.when(i == 0)
        def _():
            carry[...] = jnp.zeros_like(carry)

        @pl.when((b == 0) & (i == 0))
        def _():
            dqg_ref[...] = jnp.zeros_like(dqg_ref)
            dkg_ref[...] = jnp.zeros_like(dkg_ref)
            dsk_ref[...] = jnp.zeros_like(dsk_ref)

        dq_ref[...] = dq.astype(BF16)
        dkv_ref[...] = (dkvc + carry[...]).astype(BF16)
        carry[...] = dkvp
        dqg_ref[...] += dqg
        dkg_ref[...] += dkg
        dsk_ref[...] += dsk

    prev = lambda b, i: (b, jnp.maximum(nb - 2 - i, 0), 0)
    cur = lambda b, i: (b, nb - 1 - i, 0)
    return pl.pallas_call(
        body, name="attn_bwd", grid=(B, nb),
        out_shape=[jax.ShapeDtypeStruct((B, S, QW), BF16), jax.ShapeDtypeStruct((B, S, 2 * KVW), BF16),
                   jax.ShapeDtypeStruct((1, HD), F32), jax.ShapeDtypeStruct((1, HD), F32),
                   jax.ShapeDtypeStruct((1, HQ), F32)],
        in_specs=[pl.BlockSpec((None, BLK, QW), cur), pl.BlockSpec((None, BLK, 2 * KVW), prev),
                  pl.BlockSpec((None, BLK, 2 * KVW), cur), pl.BlockSpec((None, BLK, LANES), prev),
                  pl.BlockSpec((None, BLK, LANES), cur), pl.BlockSpec((None, BLK, LANES), prev),
                  pl.BlockSpec((None, BLK, LANES), cur), _full((1, HD)), _full((1, HD)), _full((1, HQ)),
                  _full((QW, QW)), _full((KVW, KVW)), pl.BlockSpec((None, BLK, QW), cur)],
        out_specs=[pl.BlockSpec((None, BLK, QW), cur), pl.BlockSpec((None, BLK, 2 * KVW), cur),
                   _full((1, HD)), _full((1, HD)), _full((1, HQ))],
        scratch_shapes=[pltpu.VMEM((BLK, 2 * KVW), F32)],
        compiler_params=_cparams(("arbitrary", "arbitrary")),
    )(aq, akv, akv, cos, cos, sin, sin, qg, kg, sinks, mq, mk, do)


def _conv_taps(xe, w, rows):
    y = None
    for j in range(CONV):
        sh = pltpu.roll(xe, CONV - 1 - j, 0)[8:8 + rows, :] if j < CONV - 1 else xe[8:8 + rows, :]
        y = sh * w[j:j + 1, :] if y is None else y + sh * w[j:j + 1, :]
    return y


def _softplus(x):
    return jnp.maximum(x, 0.0) + jnp.log1p(jnp.exp(-jnp.abs(x)))


_BMM = (((2,), (1,)), ((0,), (0,)))
_BMM_NT = (((2,), (2,)), ((0,), (0,)))
_BMM_TN = (((1,), (1,)), ((0,), (0,)))


def _bmm(a, b, dims=_BMM):
    return lax.dot_general(a.astype(BF16), b.astype(BF16), dims, preferred_element_type=F32)


def _split(a):
    hi = a.astype(BF16)
    return hi, (a - hi.astype(F32)).astype(BF16)


def _bmm3(a, b, dims=_BMM):
    ah, al = _split(a)
    bh, bl = _split(b)
    d = lambda p, q: lax.dot_general(p, q, dims, preferred_element_type=F32)
    return d(ah, bh) + (d(ah, bl) + d(al, bh))


TRI_BASE = 8


def _tri_inverse(L):
    ii = lax.broadcasted_iota(jnp.int32, (CH, CH), 0)
    jj = lax.broadcasted_iota(jnp.int32, (CH, CH), 1)
    same = lambda size: (ii // size) == (jj // size)
    diag = jnp.where(same(TRI_BASE), L, 0.0)
    X = (ii == jj).astype(F32) - diag
    P = diag
    n = 2
    while n < TRI_BASE:
        P = _bmm3(P, P)
        X = X + _bmm3(X, P)
        n *= 2
    size = TRI_BASE
    while size < CH:
        joint = jnp.where(same(2 * size) & jnp.logical_not(same(size)), L, 0.0)
        X = X - _bmm3(X, _bmm3(joint, X))
        size *= 2
    return X


@jax.custom_vjp
def _tri_inverse_known(L, T):
    return T


def _tri_inverse_known_fwd(L, T):
    return T, T


def _tri_inverse_known_bwd(T, dT):
    Tt = jnp.swapaxes(T, 1, 2)
    return -_bmm(Tt, _bmm(dT, Tt)), jnp.zeros_like(T)


_tri_inverse_known.defvjp(_tri_inverse_known_fwd, _tri_inverse_known_bwd)


def _triangle(n, upper):
    ii = lax.broadcasted_iota(jnp.int32, (n, CH, CH), 1)
    jj = lax.broadcasted_iota(jnp.int32, (n, CH, CH), 2)
    return ((ii <= jj) if upper else (ii >= jj)).astype(BF16)


@jax.custom_vjp
def _cumsum_rows(g):
    g0 = g.astype(BF16)
    r1 = g - g0.astype(F32)
    g1 = r1.astype(BF16)
    g2 = (r1 - g1.astype(F32)).astype(BF16)
    tri = _triangle(g.shape[0], False)
    d = lambda q: lax.dot_general(tri, q, _BMM, preferred_element_type=F32)
    return d(g0) + (d(g1) + d(g2))


def _cumsum_rows_fwd(g):
    return _cumsum_rows(g), None


def _cumsum_rows_bwd(_, dy):
    hi, lo = _split(dy)
    tri = _triangle(dy.shape[0], True)
    d = lambda q: lax.dot_general(tri, q, _BMM, preferred_element_type=F32)
    return (d(hi) + d(lo),)


_cumsum_rows.defvjp(_cumsum_rows_fwd, _cumsum_rows_bwd)


def _row_sums(t):
    n, r, w = t.shape
    hi, lo = _split(t.reshape(n * r, w))
    ones = jnp.ones((w, w), BF16)
    s = jnp.dot(hi, ones, preferred_element_type=F32) + jnp.dot(lo, ones, preferred_element_type=F32)
    return s.reshape(n, r, w)


def _dn_prep(t_known, qr, kr, v, a_raw, b_raw, a_log, dt_b):
    n = qr.shape[0]
    ii = lax.broadcasted_iota(jnp.int32, (n, CH, CH), 1)
    jj = lax.broadcasted_iota(jnp.int32, (n, CH, CH), 2)
    incl, strict = ii >= jj, ii > jj
    q = qr * lax.rsqrt(_row_sums(qr * qr) + EPS) * (DK ** -0.5)
    k = kr * lax.rsqrt(_row_sums(kr * kr) + EPS)
    beta = _sigmoid(b_raw)
    g = -jnp.exp(a_log) * _softplus(a_raw + dt_b)
    gcb = _cumsum_rows(jnp.broadcast_to(g, (n, CH, DK)))
    gc = gcb[:, :, 0:1]
    gc_row = jnp.swapaxes(gcb, 1, 2)[:, 0:1, 0:CH]
    decay = jnp.where(incl, jnp.exp(jnp.where(incl, gc - gc_row, 0.0)), 0.0)
    kb = k * beta
    L = jnp.where(strict, _bmm(kb, k, _BMM_NT) * decay, 0.0)
    T = _tri_inverse(L) if t_known is None else _tri_inverse_known(L, t_known)
    eg = jnp.exp(gc)
    u = _bmm(T, v * beta)
    w = _bmm(T, kb * eg)
    a_in = _bmm(q, k, _BMM_NT) * decay
    g_last = gc[:, CH - 1:CH, :]
    return u, w, q * eg, k * jnp.exp(g_last - gc), a_in, jnp.exp(g_last), T


def _dn_step(S0, u, w, qd, kd, a_in, cd):
    r = _bmm(jnp.concatenate([w, qd], axis=1), S0)
    v_new = u - r[:, 0:CH, :]
    o = r[:, CH:2 * CH, :] + _bmm(a_in, v_new)
    S1 = S0 * cd + _bmm(kd, v_new, _BMM_TN)
    return o, S1


def _dn_stack(cq, ba, al, dt, G):
    cols = [[] for _ in range(7)]
    for c in range(G):
        rows = slice(CH * c, CH * (c + 1))
        for h in range(DH):
            parts = (cq[rows, DK * h:DK * (h + 1)], cq[rows, DNW + DK * h:DNW + DK * (h + 1)],
                     cq[rows, 2 * DNW + DK * h:2 * DNW + DK * (h + 1)], ba[rows, DH + h:DH + h + 1],
                     ba[rows, h:h + 1], al[:, h:h + 1], dt[:, h:h + 1])
            for col, p in zip(cols, parts):
                col.append(p)
    return tuple(jnp.stack(col) for col in cols)


def _dn_group(S, want):
    g = want
    while (S // CH) % g:
        g //= 2
    return g


def _dn_prep_fwd(xin, conv_w, ba, a_log, dt_b):
    B, S, _ = xin.shape
    nc = S // CH
    G = _dn_group(S, 4)
    r8 = G * CH // 8

    def body(xp_ref, x_ref, cw_ref, ba_ref, al_ref, dt_ref, cq_ref, u_ref, w_ref, qd_ref, kd_ref, a_ref, t_ref, cd_ref):
        xp = jnp.where(pl.program_id(1) > 0, xp_ref[...], 0.0)
        cq = _silu(_conv_taps(jnp.concatenate([xp, x_ref[...]], axis=0), cw_ref[...], G * CH))
        cq_ref[...] = cq
        ops = _dn_stack(cq, ba_ref[...], al_ref[...], dt_ref[...], G)
        u, w, qd, kd, a_in, cd, T = _dn_prep(None, *ops)
        lane4 = lax.broadcasted_iota(jnp.int32, (1, DH), 1)
        for c in range(G):
            rows = slice(CH * c, CH * (c + 1))
            cdrow = jnp.zeros((1, DH), F32)
            for h in range(DH):
                n = DH * c + h
                lanes = slice(DK * h, DK * (h + 1))
                u_ref[rows, lanes] = u[n]
                w_ref[rows, lanes] = w[n]
                qd_ref[rows, lanes] = qd[n]
                kd_ref[rows, lanes] = kd[n]
                a_ref[rows, CH * h:CH * (h + 1)] = a_in[n]
                t_ref[rows, CH * h:CH * (h + 1)] = T[n]
                cdrow = cdrow + jnp.where(lane4 == h, cd[n], 0.0)
            cd_ref[c] = cdrow

    wide = jax.ShapeDtypeStruct((B, S, DNW), F32)
    sq = jax.ShapeDtypeStruct((B, S, DH * CH), F32)
    return pl.pallas_call(
        body, name="dn_prep_fwd", grid=(B, nc // G),
        out_shape=[jax.ShapeDtypeStruct((B, S, CONVW), F32), wide, wide, wide, wide, sq, sq,
                   jax.ShapeDtypeStruct((B, nc, 1, DH), F32)],
        in_specs=[pl.BlockSpec((None, 8, CONVW), lambda b, i: (b, jnp.maximum(i * r8 - 1, 0), 0)),
                  _rows(G * CH, CONVW), _full((CONV, CONVW)), _rows(G * CH, 2 * DH), _full((1, DH)), _full((1, DH))],
        out_specs=[_rows(G * CH, CONVW)] + [_rows(G * CH, DNW)] * 4 + [_rows(G * CH, DH * CH)] * 2
                  + [pl.BlockSpec((None, G, 1, DH), lambda b, i: (b, i, 0, 0))],
        compiler_params=_cparams(("parallel", "arbitrary")),
    )(xin, xin, conv_w, ba, a_log, dt_b)


def _dn_seq_specs(B, steps, gs, rev):
    at = (lambda i: steps - 1 - i) if rev else (lambda i: i)
    wide = pl.BlockSpec((B, gs * CH, DNW), lambda i: (0, at(i), 0))
    a_spec = pl.BlockSpec((B, gs * CH, DH * CH), lambda i: (0, at(i), 0))
    cd_spec = pl.BlockSpec((B, gs, 1, DH), lambda i: (0, at(i), 0, 0))
    st_spec = pl.BlockSpec((B, gs, DH, DK, DK), lambda i: (0, at(i), 0, 0, 0))
    return wide, a_spec, cd_spec, st_spec


def _dn_step_operands(B, c, u_ref, w_ref, qd_ref, kd_ref, a_ref, cd_ref):
    pairs = [(b, h) for b in range(B) for h in range(DH)]
    rows = slice(CH * c, CH * (c + 1))
    wide = lambda ref: jnp.stack([ref[b, rows, DK * h:DK * (h + 1)] for b, h in pairs])
    a_in = jnp.stack([a_ref[b, rows, CH * h:CH * (h + 1)] for b, h in pairs])
    cd = jnp.stack([cd_ref[b, c, :, h:h + 1] for b, h in pairs])
    return wide(u_ref), wide(w_ref), wide(qd_ref), wide(kd_ref), a_in, cd


def _dn_seq_fwd(u, w, qd, kd, a_in, cd):
    B, S, _ = u.shape
    nc = S // CH
    gs = _dn_group(S, 8)

    def body(u_ref, w_ref, qd_ref, kd_ref, a_ref, cd_ref, o_ref, st_ref, state):
        @pl.when(pl.program_id(0) == 0)
        def _():
            state[...] = jnp.zeros_like(state)

        S0 = state[...]
        for c in range(gs):
            for b in range(B):
                st_ref[b, c] = S0[DH * b:DH * (b + 1)]
            o, S0 = _dn_step(S0, *_dn_step_operands(B, c, u_ref, w_ref, qd_ref, kd_ref, a_ref, cd_ref))
            for b in range(B):
                for h in range(DH):
                    o_ref[b, CH * c:CH * (c + 1), DK * h:DK * (h + 1)] = o[DH * b + h]
        state[...] = S0

    wide, a_spec, cd_spec, st_spec = _dn_seq_specs(B, nc // gs, gs, False)
    return pl.pallas_call(
        body, name="dn_seq_fwd", grid=(nc // gs,),
        out_shape=[jax.ShapeDtypeStruct((B, S, DNW), F32), jax.ShapeDtypeStruct((B, nc, DH, DK, DK), F32)],
        in_specs=[wide, wide, wide, wide, a_spec, cd_spec],
        out_specs=[wide, st_spec],
        scratch_shapes=[pltpu.VMEM((B * DH, DK, DK), F32)],
        compiler_params=_cparams(("arbitrary",)),
    )(u, w, qd, kd, a_in, cd)


def _dn_seq_bwd(u, w, qd, kd, a_in, cd, states, do):
    B, S, _ = u.shape
    nc = S // CH
    gs = _dn_group(S, 8)

    def body(u_ref, w_ref, qd_ref, kd_ref, a_ref, cd_ref, st_ref, do_ref,
             du_ref, dw_ref, dqd_ref, dkd_ref, da_ref, dcd_ref, dstate):
        @pl.when(pl.program_id(0) == 0)
        def _():
            dstate[...] = jnp.zeros_like(dstate)

        lane4 = lax.broadcasted_iota(jnp.int32, (1, DH), 1)
        dS = dstate[...]
        for c in reversed(range(gs)):
            rows = slice(CH * c, CH * (c + 1))
            S0 = jnp.concatenate([st_ref[b, c] for b in range(B)], axis=0)
            do = jnp.stack([do_ref[b, rows, DK * h:DK * (h + 1)] for b in range(B) for h in range(DH)])
            _, vjp = jax.vjp(_dn_step, S0, *_dn_step_operands(B, c, u_ref, w_ref, qd_ref, kd_ref, a_ref, cd_ref))
            dS, du, dw, dqd, dkd, da, dcd = vjp((do, dS))
            for b in range(B):
                dcdrow = jnp.zeros((1, DH), F32)
                for h in range(DH):
                    n = DH * b + h
                    lanes = slice(DK * h, DK * (h + 1))
                    du_ref[b, rows, lanes] = du[n]
                    dw_ref[b, rows, lanes] = dw[n]
                    dqd_ref[b, rows, lanes] = dqd[n]
                    dkd_ref[b, rows, lanes] = dkd[n]
                    da_ref[b, rows, CH * h:CH * (h + 1)] = da[n]
                    dcdrow = dcdrow + jnp.where(lane4 == h, dcd[n], 0.0)
                dcd_ref[b, c] = dcdrow
        dstate[...] = dS

    wide, a_spec, cd_spec, st_spec = _dn_seq_specs(B, nc // gs, gs, True)
    sd = jax.ShapeDtypeStruct((B, S, DNW), F32)
    return pl.pallas_call(
        body, name="dn_seq_bwd", grid=(nc // gs,),
        out_shape=[sd, sd, sd, sd, jax.ShapeDtypeStruct((B, S, DH * CH), F32), jax.ShapeDtypeStruct((B, nc, 1, DH), F32)],
        in_specs=[wide, wide, wide, wide, a_spec, cd_spec, st_spec, wide],
        out_specs=[wide, wide, wide, wide, a_spec, cd_spec],
        scratch_shapes=[pltpu.VMEM((B * DH, DK, DK), F32)],
        compiler_params=_cparams(("arbitrary",)),
    )(u, w, qd, kd, a_in, cd, states, do)


def _dn_prep_bwd(xin, conv_w, cq, ba, a_log, dt_b, t_inv, du, dw, dqd, dkd, da, dcd):
    B, S, _ = cq.shape
    nc = S // CH
    G = _dn_group(S, 4)
    R = G * CH
    nblk = nc // G
    r8 = R // 8

    def body(xp_ref, x_ref, cw_ref, cq_ref, ba_ref, al_ref, dt_ref, t_ref, du_ref, dw_ref, dqd_ref, dkd_ref, da_ref, dcd_ref,
             dx_ref, dcw_ref, dba_ref, dal_ref, ddt_ref, carry):
        i = pl.program_id(1)

        @pl.when((pl.program_id(0) == 0) & (i == 0))
        def _():
            dal_ref[...] = jnp.zeros_like(dal_ref)
            ddt_ref[...] = jnp.zeros_like(ddt_ref)
            dcw_ref[...] = jnp.zeros_like(dcw_ref)

        @pl.when(i == 0)
        def _():
            carry[...] = jnp.zeros_like(carry)

        pairs = [(c, h) for c in range(G) for h in range(DH)]
        rows = lambda c: slice(CH * c, CH * (c + 1))
        wide = lambda ref: jnp.stack([ref[rows(c), DK * h:DK * (h + 1)] for c, h in pairs])
        square = lambda ref: jnp.stack([ref[rows(c), CH * h:CH * (h + 1)] for c, h in pairs])
        ops = _dn_stack(cq_ref[...], ba_ref[...], al_ref[...], dt_ref[...], G)
        cots = (wide(du_ref), wide(dw_ref), wide(dqd_ref), wide(dkd_ref), square(da_ref),
                jnp.stack([dcd_ref[c][:, h:h + 1] for c, h in pairs]), jnp.zeros((len(pairs), CH, CH), F32))
        _, vjp = jax.vjp(functools.partial(_dn_prep, square(t_ref)), *ops)
        dq, dk, dv, dar, dbr, dl, dd = vjp(cots)
        lane8 = lax.broadcasted_iota(jnp.int32, (CH, 2 * DH), 1)
        lane4 = lax.broadcasted_iota(jnp.int32, (1, DH), 1)
        dal = jnp.zeros((1, DH), F32)
        ddt = jnp.zeros((1, DH), F32)
        for c in range(G):
            dba = jnp.zeros((CH, 2 * DH), F32)
            for h in range(DH):
                n = DH * c + h
                dba = dba + jnp.where(lane8 == h, dbr[n], 0.0) + jnp.where(lane8 == DH + h, dar[n], 0.0)
                dal = dal + jnp.where(lane4 == h, dl[n], 0.0)
                ddt = ddt + jnp.where(lane4 == h, dd[n], 0.0)
            dba_ref[rows(c), :] = dba.astype(BF16)
        dal_ref[...] += dal
        ddt_ref[...] += ddt

        dcq = jnp.concatenate([jnp.concatenate([t[DH * c + h] for t in (dq, dk, dv) for h in range(DH)], axis=1)
                               for c in range(G)], axis=0)
        w = cw_ref[...]
        xp = jnp.where(i < nblk - 1, xp_ref[...], 0.0)
        xe = jnp.concatenate([xp, x_ref[...]], axis=0)
        taps = [(pltpu.roll(xe, CONV - 1 - j, 0) if j < CONV - 1 else xe)[8:8 + R, :] for j in range(CONV)]
        pre = sum(t * w[j:j + 1, :] for j, t in enumerate(taps))
        sg = _sigmoid(pre)
        dpre = dcq * (sg * (1.0 + pre * (1.0 - sg)))
        ext = jnp.concatenate([dpre, carry[...]], axis=0)
        dx = dpre * w[CONV - 1:CONV, :]
        for j in range(CONV - 1):
            dx = dx + pltpu.roll(ext, R + 8 - (CONV - 1 - j), 0)[0:R, :] * w[j:j + 1, :]
        dx_ref[...] = dx.astype(BF16)
        carry[...] = dpre[0:8, :]
        lane_row = lax.broadcasted_iota(jnp.int32, (CONV, CONVW), 0)
        dcw = jnp.zeros((CONV, CONVW), F32)
        for j in range(CONV):
            dcw = dcw + jnp.where(lane_row == j, jnp.sum(taps[j] * dpre, axis=0, keepdims=True), 0.0)
        dcw_ref[...] += dcw

    rev = lambda w: pl.BlockSpec((None, R, w), lambda b, i: (b, nblk - 1 - i, 0))
    return pl.pallas_call(
        body, name="dn_prep_bwd", grid=(B, nblk),
        out_shape=[jax.ShapeDtypeStruct((B, S, CONVW), BF16), jax.ShapeDtypeStruct((CONV, CONVW), F32),
                   jax.ShapeDtypeStruct((B, S, 2 * DH), BF16), jax.ShapeDtypeStruct((1, DH), F32),
                   jax.ShapeDtypeStruct((1, DH), F32)],
        in_specs=[pl.BlockSpec((None, 8, CONVW), lambda b, i: (b, jnp.maximum((nblk - 1 - i) * r8 - 1, 0), 0)),
                  rev(CONVW), _full((CONV, CONVW)), rev(CONVW), rev(2 * DH), _full((1, DH)), _full((1, DH)), rev(DH * CH)]
                 + [rev(DNW)] * 4 + [rev(DH * CH), pl.BlockSpec((None, G, 1, DH), lambda b, i: (b, nblk - 1 - i, 0, 0))],
        out_specs=[rev(CONVW), _full((CONV, CONVW)), rev(2 * DH), _full((1, DH)), _full((1, DH))],
        scratch_shapes=[pltpu.VMEM((8, CONVW), F32)],
        compiler_params=_cparams(("arbitrary", "arbitrary")),
    )(xin, xin, conv_w, cq, ba, a_log, dt_b, t_inv, du, dw, dqd, dkd, da, dcd)


def _gated_norm(o, z, g):
    outs = []
    for h in range(DH):
        t = o[:, DK * h:DK * (h + 1)]
        r = lax.rsqrt(jnp.mean(t * t, axis=-1, keepdims=True) + EPS)
        outs.append(t * r * g * _silu(z[:, DK * h:DK * (h + 1)]))
    return jnp.concatenate(outs, axis=1)


def _mix_fwd(x, o_attn, o_dn, z, ga, gd, mod, dn_g, w_branch, w_out):
    B, S, _ = x.shape
    tm = _tile(S, 512)

    def body(x_ref, oa_ref, od_ref, z_ref, ga_ref, gd_ref, mod_ref, g_ref, wb_ref, wo_ref,
             x1_ref, mix_ref, mg_ref, ob_ref):
        oa = oa_ref[...].astype(BF16)
        od = _gated_norm(od_ref[...], z_ref[...], g_ref[...]).astype(BF16)
        ob_ref[0] = oa
        ob_ref[1] = od
        ya = jnp.dot(oa, wb_ref[0:QW, :], preferred_element_type=F32)
        yd = jnp.dot(od, wb_ref[QW:QW + DNW, :], preferred_element_type=F32)
        merged = (_sigmoid(ga_ref[...]) * ya + _sigmoid(gd_ref[...]) * yd).astype(BF16)
        mg_ref[...] = merged
        mix = jnp.dot(merged, wo_ref[...], preferred_element_type=F32)
        mix_ref[...] = mix
        x1_ref[...] = x_ref[...] + mod_ref[2:3, :] * mix

    return pl.pallas_call(
        body, name="mix_fwd", grid=(B, S // tm),
        out_shape=[jax.ShapeDtypeStruct((B, S, D), F32), jax.ShapeDtypeStruct((B, S, D), F32),
                   jax.ShapeDtypeStruct((B, S, D), BF16), jax.ShapeDtypeStruct((B, 2, S, QW), BF16)],
        in_specs=[_rows(tm, D), _rows(tm, QW), _rows(tm, DNW), _rows(tm, DNW), _rows(tm, D), _rows(tm, D),
                  _perb(6, D), _full((1, DK)), _resident(w_branch.shape), _resident(w_out.shape)],
        out_specs=[_rows(tm, D), _rows(tm, D), _rows(tm, D), _stacked(2, tm, QW)],
        compiler_params=_cparams(("parallel", "arbitrary")),
    )(x, o_attn, o_dn, z, ga, gd, mod, dn_g, w_branch, w_out)


def _mix_bwd(dx1, mix, o_attn, o_dn, z, ga, gd, mod, dn_g, w_branch, w_out):
    B, S, _ = dx1.shape
    tm = _tile(S, 512)

    def body(dx1_ref, mix_ref, oa_ref, od_ref, z_ref, ga_ref, gd_ref, mod_ref, g_ref, wb_ref, wo_ref,
             dmix_ref, dyo_ref, dga_ref, dgd_ref, dz_ref, doa_ref, dod_ref, dgate_ref, dg_ref):
        b, i = pl.program_id(0), pl.program_id(1)
        dx1 = dx1_ref[...]
        dmix = (dx1 * mod_ref[2:3, :]).astype(BF16)
        dmix_ref[...] = dmix
        dgate = jnp.sum(dx1 * mix_ref[...], axis=0, keepdims=True)
        dmerged = _dot_nt(dmix, wo_ref[...])
        odn, gn_vjp = jax.vjp(_gated_norm, od_ref[...], z_ref[...], g_ref[...])
        ya = _dot(oa_ref[...], wb_ref[0:QW, :])
        yd = _dot(odn, wb_ref[QW:QW + DNW, :])
        sa, sd = _sigmoid(ga_ref[...]), _sigmoid(gd_ref[...])
        dya = (dmerged * sa).astype(BF16)
        dyd = (dmerged * sd).astype(BF16)
        dyo_ref[0] = dya
        dyo_ref[1] = dyd
        dga_ref[...] = (dmerged * ya * sa * (1.0 - sa)).astype(BF16)
        dgd_ref[...] = (dmerged * yd * sd * (1.0 - sd)).astype(BF16)
        doa_ref[...] = _dot_nt(dya, wb_ref[0:QW, :])
        dodn = _dot_nt(dyd, wb_ref[QW:QW + DNW, :])
        dod, dz, dg = gn_vjp(dodn)
        dod_ref[...] = dod
        dz_ref[...] = dz.astype(BF16)

        @pl.when(i == 0)
        def _():
            dgate_ref[...] = jnp.zeros_like(dgate_ref)

        @pl.when((b == 0) & (i == 0))
        def _():
            dg_ref[...] = jnp.zeros_like(dg_ref)

        dgate_ref[...] += dgate
        dg_ref[...] += dg

    return pl.pallas_call(
        body, name="mix_bwd", grid=(B, S // tm),
        out_shape=[jax.ShapeDtypeStruct((B, S, D), BF16), jax.ShapeDtypeStruct((B, 2, S, D), BF16),
                   jax.ShapeDtypeStruct((B, S, D), BF16), jax.ShapeDtypeStruct((B, S, D), BF16),
                   jax.ShapeDtypeStruct((B, S, DNW), BF16),
                   jax.ShapeDtypeStruct((B, S, QW), F32), jax.ShapeDtypeStruct((B, S, DNW), F32),
                   jax.ShapeDtypeStruct((B, 1, D), F32), jax.ShapeDtypeStruct((1, DK), F32)],
        in_specs=[_rows(tm, D), _rows(tm, D), _rows(tm, QW), _rows(tm, DNW), _rows(tm, DNW), _rows(tm, D),
                  _rows(tm, D), _perb(6, D), _full((1, DK)), _resident(w_branch.shape), _resident(w_out.shape)],
        out_specs=[_rows(tm, D), _stacked(2, tm, D), _rows(tm, D), _rows(tm, D), _rows(tm, DNW),
                   _rows(tm, QW), _rows(tm, DNW), _perb(1, D), _full((1, DK))],
        compiler_params=_cparams(("arbitrary", "arbitrary")),
    )(dx1, mix, o_attn, o_dn, z, ga, gd, mod, dn_g, w_branch, w_out)


GU_SHARD = 2 * FFN // N_DEV
GU_HALF = N_DEV // 2


def _ffn1_fwd(x1, mod, g2, w_gu):
    B, S, _ = x1.shape
    tm = _tile(S)

    def body(x_ref, mod_ref, g_ref, w_ref, h_ref, dgate_ref, dup_ref, act_ref):
        h = _rms_mod(x_ref[...], g_ref[...], mod_ref[4:5, :], mod_ref[3:4, :]).astype(BF16)
        h_ref[...] = h
        for j in range(GU_HALF):
            gate = _dot_nt(h, w_ref[j])
            up = _dot_nt(h, w_ref[GU_HALF + j])
            sg = _sigmoid(gate)
            silu = gate * sg
            dgate_ref[j] = up * (sg * (1.0 + gate * (1.0 - sg)))
            dup_ref[j] = silu
            act_ref[j] = (silu * up).astype(BF16)

    blk = lambda dt: jax.ShapeDtypeStruct((B, GU_HALF, S, GU_SHARD), dt)
    return pl.pallas_call(
        body, name="ffn1_fwd", grid=(B, S // tm),
        out_shape=[jax.ShapeDtypeStruct((B, S, D), BF16), blk(F32), blk(F32), blk(BF16)],
        in_specs=[_rows(tm, D), _perb(6, D), _full((1, D)), _resident(w_gu.shape)],
        out_specs=[_rows(tm, D)] + [_stacked(GU_HALF, tm, GU_SHARD)] * 3,
        compiler_params=_cparams(("parallel", "arbitrary")),
    )(x1, mod, g2, w_gu)


def _ffn2_fwd(act, x1, target, mod, w_down):
    B, S, _ = x1.shape
    tm = _tile(S, 512)

    def body(a_ref, x_ref, t_ref, mod_ref, w_ref, dy_ref, loss_ref, dgate_ref):
        b, i = pl.program_id(0), pl.program_id(1)
        y = jnp.dot(a_ref[0], w_ref[0], preferred_element_type=F32)
        for j in range(1, GU_HALF):
            y = y + jnp.dot(a_ref[j], w_ref[j], preferred_element_type=F32)
        err = x_ref[...] + mod_ref[5:6, :] * y - t_ref[...]
        dy = err * (1.0 / D)
        dy_ref[...] = dy

        @pl.when((b == 0) & (i == 0))
        def _():
            loss_ref[...] = jnp.zeros_like(loss_ref)

        @pl.when(i == 0)
        def _():
            dgate_ref[...] = jnp.zeros_like(dgate_ref)

        loss_ref[...] += (0.5 / D) * jnp.sum(err * err)
        dgate_ref[...] += jnp.sum(dy * y, axis=0, keepdims=True)

    return pl.pallas_call(
        body, name="ffn2_fwd", grid=(B, S // tm),
        out_shape=[jax.ShapeDtypeStruct((B, S, D), F32), jax.ShapeDtypeStruct((1, 128), F32),
                   jax.ShapeDtypeStruct((B, 1, D), F32)],
        in_specs=[_stacked(GU_HALF, tm, GU_SHARD), _rows(tm, D), _rows(tm, D), _perb(6, D), _resident(w_down.shape)],
        out_specs=[_rows(tm, D), _full((1, 128)), _perb(1, D)],
        compiler_params=_cparams(("arbitrary", "arbitrary")),
    )(act, x1, target, mod, w_down)


def _ffn2_bwd(dy, act_dgate, act_dup, mod, w_down):
    B, S, _ = dy.shape
    tm = _tile(S)

    def body(dy_ref, dgate_ref, dup_ref, mod_ref, w_ref, dgu_ref, dyg_ref):
        dyg = (dy_ref[...] * mod_ref[5:6, :]).astype(BF16)
        dyg_ref[...] = dyg
        for j in range(GU_HALF):
            dact = _dot_nt(dyg, w_ref[j])
            dgu_ref[j] = (dact * dgate_ref[j]).astype(BF16)
            dgu_ref[GU_HALF + j] = (dact * dup_ref[j]).astype(BF16)

    return pl.pallas_call(
        body, name="ffn2_bwd", grid=(B, S // tm),
        out_shape=[jax.ShapeDtypeStruct((B, N_DEV, S, GU_SHARD), BF16), jax.ShapeDtypeStruct((B, S, D), BF16)],
        in_specs=[_rows(tm, D), _stacked(GU_HALF, tm, GU_SHARD), _stacked(GU_HALF, tm, GU_SHARD), _perb(6, D),
                  _resident(w_down.shape)],
        out_specs=[_stacked(N_DEV, tm, GU_SHARD), _rows(tm, D)],
        compiler_params=_cparams(("parallel", "arbitrary")),
    )(dy, act_dgate, act_dup, mod, w_down)


def _ffn1_bwd(dgu, x1, dy, mod, g2, w_gu):
    B, S, _ = x1.shape
    tm = _tile(S, 512)

    def body(dgu_ref, x_ref, dy_ref, mod_ref, g_ref, w_ref, dx1_ref, dg_ref, dsc_ref, dsh_ref):
        b, i = pl.program_id(0), pl.program_id(1)
        dh = jnp.dot(dgu_ref[0], w_ref[0], preferred_element_type=F32)
        for j in range(1, N_DEV):
            dh = dh + jnp.dot(dgu_ref[j], w_ref[j], preferred_element_type=F32)
        _, vjp = jax.vjp(_rms_mod, x_ref[...], g_ref[...], mod_ref[4:5, :], mod_ref[3:4, :])
        dx, dg, dsc, dsh = vjp(dh)
        dx1_ref[...] = dy_ref[...] + dx

        @pl.when((b == 0) & (i == 0))
        def _():
            dg_ref[...] = jnp.zeros_like(dg_ref)

        @pl.when(i == 0)
        def _():
            dsc_ref[...] = jnp.zeros_like(dsc_ref)
            dsh_ref[...] = jnp.zeros_like(dsh_ref)

        dg_ref[...] += dg
        dsc_ref[...] += dsc
        dsh_ref[...] += dsh

    return pl.pallas_call(
        body, name="ffn1_bwd", grid=(B, S // tm),
        out_shape=[jax.ShapeDtypeStruct((B, S, D), F32), jax.ShapeDtypeStruct((1, D), F32),
                   jax.ShapeDtypeStruct((B, 1, D), F32), jax.ShapeDtypeStruct((B, 1, D), F32)],
        in_specs=[_stacked(N_DEV, tm, GU_SHARD), _rows(tm, D), _rows(tm, D), _perb(6, D), _full((1, D)),
                  _resident(w_gu.shape)],
        out_specs=[_rows(tm, D), _full((1, D)), _perb(1, D), _perb(1, D)],
        compiler_params=_cparams(("arbitrary", "arbitrary")),
    )(dgu, x1, dy, mod, g2, w_gu)


def _adamw(w, g, m, v, name):
    def body(w_ref, g_ref, m_ref, v_ref, d_ref, nm_ref, nv_ref):
        d_ref[...], nm_ref[...], nv_ref[...] = _adamw_math(w_ref[...], g_ref[...], m_ref[...], v_ref[...])

    sd = jax.ShapeDtypeStruct(w.shape, F32)
    return pl.pallas_call(body, name=name, out_shape=(sd, sd, sd), compiler_params=_cparams())(w, g, m, v)


def kernel(x, c, positions, ada_w, ada_b, norm1_g, w_in, conv_w, q_norm_g, k_norm_g, sinks, a_log, dt_bias, dn_norm_g, w_branch, w_out, norm2_g, w_gate_up, w_down, loss_target, m_ada_w, m_ada_b, m_norm1_g, m_w_in, m_conv_w, m_q_norm_g, m_k_norm_g, m_sinks, m_a_log, m_dt_bias, m_dn_norm_g, m_w_branch, m_w_out, m_norm2_g, m_w_gate_up, m_w_down, v_ada_w, v_ada_b, v_norm1_g, v_w_in, v_conv_w, v_q_norm_g, v_k_norm_g, v_sinks, v_a_log, v_dt_bias, v_dn_norm_g, v_w_branch, v_w_out, v_norm2_g, v_w_gate_up, v_w_down):
    B, S, _ = x.shape
    me = 4 * lax.axis_index("x") + 2 * lax.axis_index("y") + lax.axis_index("c")

    tr = lambda t: jnp.swapaxes(t, 1, 2)
    shards = [w[0].astype(BF16) for w in (tr(w_in), w_branch, w_out, tr(w_gate_up), w_down)]

    c_all = _all_gather_small(c, "gather_c").reshape(N_DEV * B, D)
    ncol = 6 * D // N_DEV
    mod_cols, cond_all = _ada_fwd(c_all, ada_w[0], lax.dynamic_slice(ada_b, (0, me * ncol), (1, ncol)))
    mod_all = _all_gather_small(mod_cols, "gather_mod").transpose(1, 0, 2).reshape(N_DEV * B, 6 * D)
    mod = lax.dynamic_slice(mod_all, (me * B, 0), (B, 6 * D)).reshape(B, 6, D)
    conv2 = conv_w.reshape(CONV, CONVW // N_DEV)
    conv_all = _all_gather_small(conv2, "gather_conv").transpose(1, 0, 2).reshape(CONV, CONVW)

    (w_in_b,) = _all_gather_big(shards[:1], "gather_w_in", after=(mod, conv_all))
    w_sems, w_srcs, w_lands, w_token = _copies_start(shards[1:], [_place_own(s, me) for s in shards[1:]], False,
                                                    w_in_b, "gather_rest_start")

    h1, aq, akv, dnx, ba, z, ga, gd = _inproj_fwd(x, mod, norm1_g + w_token[0, 0], w_in_b)
    invf, mean_q, mean_k = _attn_consts()
    rope_cos, rope_sin = _rope_tables(positions.reshape(B, S, 1), invf)
    o_attn = _attn_fwd(aq, akv, rope_cos, rope_sin, q_norm_g, k_norm_g, sinks, mean_q, mean_k)
    cq, dn_u, dn_w, dn_qd, dn_kd, dn_a, dn_t, dn_cd = _dn_prep_fwd(dnx, conv_all, ba, a_log, dt_bias)
    o_dn, states = _dn_seq_fwd(dn_u, dn_w, dn_qd, dn_kd, dn_a, dn_cd)
    w_branch_g, w_out_g, w_gu_b, w_down_g = _copies_wait(w_sems, w_srcs, w_lands, o_dn, "gather_wait_rest")
    w_branch_f = w_branch_g.reshape(D, D)
    w_out_f = w_out_g.reshape(D, D)
    w_down_b = w_down_g.reshape(GU_HALF, GU_SHARD, D)
    x1, mix, merged, ob = _mix_fwd(x, o_attn, o_dn, z, ga, gd, mod, dn_norm_g, w_branch_f, w_out_f)
    h2, act_dgate, act_dup, act = _ffn1_fwd(x1, mod, norm2_g, w_gu_b)
    dy, loss_part, d_gate2 = _ffn2_fwd(act, x1, loss_target, mod, w_down_b)
    loss = lax.psum(loss_part[0, 0], ("x", "y", "c"))

    one = lambda t: t.reshape(B, 1, S, t.shape[-1])
    dgu, dyg = _ffn2_bwd(dy, act_dgate, act_dup, mod, w_down_b)
    g_w_down = _wgrad(act, one(dyg), "wgrad_down")
    dx1, d_n2g, d_scale2, d_shift2 = _ffn1_bwd(dgu, x1, dy, mod, norm2_g, w_gu_b)
    g_w_gu = _wgrad(dgu, one(h2), "wgrad_gate_up")
    ffn = _exchange_start([g_w_gu, g_w_down.reshape(N_DEV, FFN // N_DEV, D)], me, dx1, "exchange_ffn_start")
    dmix, dyo, dga, dgd, dz, d_oa, d_od, d_gate1, d_dng = _mix_bwd(
        dx1, mix, o_attn, o_dn, z, ga, gd, mod, dn_norm_g + ffn[3][0, 0], w_branch_f, w_out_f)
    d_dn = _dn_seq_bwd(dn_u, dn_w, dn_qd, dn_kd, dn_a, dn_cd, states, d_od)
    ddnx, d_conv, dba, d_alog, d_dtb = _dn_prep_bwd(dnx, conv_all, cq, ba, a_log, dt_bias, dn_t, *d_dn)
    daq, dakv, d_qg, d_kg, d_sinks = _attn_bwd(aq, akv, rope_cos, rope_sin, q_norm_g, k_norm_g, sinks, mean_q, mean_k, d_oa)
    dps = [daq, dakv, ddnx, dba, dz, dga, dgd]
    dblk, grad_x, d_n1g, d_scale1, d_shift1 = _inproj_bwd(x, mod, norm1_g, dx1, dps, w_in_b)

    dmod = jnp.concatenate([d_shift1, d_scale1, d_gate1, d_shift2, d_scale2, d_gate2], axis=2).reshape(B, 6 * D)
    small = jnp.concatenate([d_n1g, d_qg, d_kg, d_sinks, d_alog, d_dtb, d_dng, d_n2g, d_conv.reshape(1, CONV * CONVW)], axis=1)
    nsm = small.shape[1]
    width = -(-max(6 * D, nsm) // 128) * 128
    rows = jnp.concatenate([jnp.pad(dmod, ((0, 0), (0, width - 6 * D))), jnp.pad(small, ((0, 8 - B - 1), (0, width - nsm)))], axis=0)
    rows_all = _all_gather_small(rows, "gather_small")
    dmod_all = rows_all[:, 0:B, 0:6 * D].reshape(N_DEV * B, 6 * D)
    dmod_cols = lax.dynamic_slice(dmod_all, (0, me * ncol), (N_DEV * B, ncol))
    grad_ada_w, grad_ada_b, small_sum = _ada_bwd(cond_all, dmod_all, dmod_cols, rows_all[:, B, :])
    sizes = [D, HD, HD, HQ, DH, DH, DK, D]
    so = np.cumsum([0] + sizes)
    g_n1, g_qg, g_kg, g_sk, g_al, g_dt, g_dn, g_n2 = [small_sum[:, so[i]:so[i + 1]] for i in range(8)]
    g_conv_all = small_sum[:, so[8]:so[8] + CONV * CONVW].reshape(CONV, N_DEV, CONVW // N_DEV)
    grad_conv = lax.dynamic_slice(g_conv_all, (0, me, 0), (CONV, 1, CONVW // N_DEV)).reshape(CONV, CONVW // N_DEV)

    g_w_in = _wgrad(dblk, one(h1), "wgrad_in", after=small_sum)
    proj = _exchange_start([g_w_in], me, small_sum, "exchange_in_start")
    g_w_out = _wgrad(one(merged), one(dmix), "wgrad_out", after=proj[3])
    g_w_branch = _wgrad(ob, dyo, "wgrad_branch", after=proj[3])
    mixer = _exchange_start([g_w_branch.reshape(N_DEV, D // N_DEV, D), g_w_out.reshape(N_DEV, D // N_DEV, D)], me,
                            proj[3], "exchange_mix_start")

    upd, grads = {}, {}

    def finish(names, parts, weights):
        for nm, p, (w, m, v) in zip(names, parts, weights):
            grads[nm], *upd[nm] = _sum_adamw(p, w, m, v, "update_" + nm)

    finish(["w_gate_up", "w_down"], _copies_wait(*ffn[:3], mixer[3], "exchange_ffn_wait"),
           [(tr(w_gate_up), tr(m_w_gate_up), tr(v_w_gate_up)), (w_down, m_w_down, v_w_down)])
    finish(["w_in"], _copies_wait(*proj[:3], grads["w_gate_up"], "exchange_in_wait"),
           [(tr(w_in), tr(m_w_in), tr(v_w_in))])
    finish(["w_branch", "w_out"], _copies_wait(*mixer[:3], grads["w_in"], "exchange_mix_wait"),
           [(w_branch, m_w_branch, v_w_branch), (w_out, m_w_out, v_w_out)])
    for nm in ("w_in", "w_gate_up"):
        grads[nm], upd[nm] = tr(grads[nm]), [tr(t) for t in upd[nm]]

    grads["ada_w"] = grad_ada_w.reshape(ada_w.shape)
    upd["ada_w"] = _adamw(ada_w, grads["ada_w"], m_ada_w, v_ada_w, "adamw_ada_w")
    small_names = ["ada_b", "norm1_g", "q_norm_g", "k_norm_g", "sinks", "a_log", "dt_bias", "dn_norm_g", "norm2_g", "conv_w"]
    small_w = [ada_b, norm1_g, q_norm_g, k_norm_g, sinks, a_log, dt_bias, dn_norm_g, norm2_g, conv_w]
    small_g = [grad_ada_b, g_n1, g_qg, g_kg, g_sk, g_al, g_dt, g_dn, g_n2, grad_conv]
    small_m = [m_ada_b, m_norm1_g, m_q_norm_g, m_k_norm_g, m_sinks, m_a_log, m_dt_bias, m_dn_norm_g, m_norm2_g, m_conv_w]
    small_v = [v_ada_b, v_norm1_g, v_q_norm_g, v_k_norm_g, v_sinks, v_a_log, v_dt_bias, v_dn_norm_g, v_norm2_g, v_conv_w]
    cat = lambda arrs: jnp.concatenate([a.reshape(1, -1) for a in arrs], axis=1)
    res = _adamw(cat(small_w), cat(small_g), cat(small_m), cat(small_v), "adamw_small")
    po = np.cumsum([0] + [int(np.prod(w.shape)) for w in small_w])
    for i, nm in enumerate(small_names):
        upd[nm] = tuple(r[:, po[i]:po[i + 1]].reshape(small_w[i].shape) for r in res)
        grads[nm] = small_g[i].reshape(small_w[i].shape)

    order = ["ada_w", "ada_b", "norm1_g", "w_in", "conv_w", "q_norm_g", "k_norm_g", "sinks", "a_log", "dt_bias",
             "dn_norm_g", "w_branch", "w_out", "norm2_g", "w_gate_up", "w_down"]
    return (loss, grad_x, *[grads[n] for n in order], *[upd[n][0] for n in order],
            *[upd[n][1] for n in order], *[upd[n][2] for n in order])
```

```python
import functools

import numpy as np
import jax
import jax.numpy as jnp
from jax import lax
from jax.experimental import pallas as pl
from jax.experimental.pallas import tpu as pltpu

F32 = jnp.float32
BF16 = jnp.bfloat16
HI = lax.Precision.HIGHEST

N_DEV = 8
D = 1024
HQ, HKV, HD = 8, 2, 64
GRP = HQ // HKV
BLK = 128
ROT = HD // 4
THETA = 500000.0
QW, KVW = HQ * HD, HKV * HD
DH, DK = 4, 128
CH = 64
DNW = DH * DK
CONV = 4
CONVW = 3 * DNW
FFN = 2816
EPS = 1e-6
IN_W = QW + 2 * KVW + CONVW + 2 * DH + DNW + 2 * D

LR, B1, B2, AEPS, WD, STEP = 0.001, 0.9, 0.999, 1e-08, 0.01, 10

VMEM_LIMIT = 56 * 1024 * 1024
MESH = pl.DeviceIdType.MESH


def _cparams(sem=None, vmem=VMEM_LIMIT):
    return pltpu.CompilerParams(dimension_semantics=sem, vmem_limit_bytes=vmem)


def _full(shape):
    n = len(shape)
    return pl.BlockSpec(shape, lambda *_: (0,) * n)


def _resident(shape):
    n = len(shape)
    return pl.BlockSpec(shape, lambda *_: (0,) * n, pipeline_mode=pl.Buffered(1))


def _rows(tm, w):
    return pl.BlockSpec((None, tm, w), lambda b, i: (b, i, 0))


def _stacked(n, tm, w):
    return pl.BlockSpec((None, n, tm, w), lambda b, i: (b, 0, i, 0))


def _perb(r, w):
    return pl.BlockSpec((None, r, w), lambda b, i: (b, 0, 0))


def _dot(a, b):
    return jnp.dot(a.astype(BF16), b.astype(BF16), preferred_element_type=F32)


def _dot_nt(a, b):
    return lax.dot_general(a.astype(BF16), b.astype(BF16), (((1,), (1,)), ((), ())), preferred_element_type=F32)


def _dot_tn(a, b):
    return lax.dot_general(a.astype(BF16), b.astype(BF16), (((0,), (0,)), ((), ())), preferred_element_type=F32)


def _dot_hi(a, b):
    return jnp.dot(a, b, preferred_element_type=F32, precision=HI)


def _sigmoid(x):
    return jax.nn.sigmoid(x)


def _silu(x):
    return x * jax.nn.sigmoid(x)


def _rms_mod(x, g, scale, shift):
    r = lax.rsqrt(jnp.mean(x * x, axis=-1, keepdims=True) + EPS)
    return (x * r * g) * (1.0 + scale) + shift


def _tile(S, rows=256):
    return min(rows, S)


def _peer(x, y, c, k):
    px = 1 - x if (k >> 2) & 1 else x
    py = 1 - y if (k >> 1) & 1 else y
    pc = 1 - c if k & 1 else c
    return px, py, pc


def _all_gather_small(v, name):
    r, n = v.shape

    def body(v_ref, out_ref, send_sems, recv_sems, local_sem):
        x, y, c = lax.axis_index("x"), lax.axis_index("y"), lax.axis_index("c")
        me = 4 * x + 2 * y + c
        mine = pltpu.make_async_copy(v_ref, out_ref.at[me], local_sem)
        mine.start()
        sends = []
        for k in range(1, N_DEV):
            cp = pltpu.make_async_remote_copy(
                src_ref=v_ref, dst_ref=out_ref.at[me], send_sem=send_sems.at[k - 1], recv_sem=recv_sems.at[k - 1],
                device_id=_peer(x, y, c, k), device_id_type=MESH)
            cp.start()
            sends.append(cp)
        for k in range(1, N_DEV):
            px, py, pc = _peer(x, y, c, k)
            pltpu.make_async_remote_copy(
                src_ref=v_ref, dst_ref=out_ref.at[4 * px + 2 * py + pc], send_sem=send_sems.at[k - 1],
                recv_sem=recv_sems.at[k - 1], device_id=(px, py, pc), device_id_type=MESH).wait_recv()
        for cp in sends:
            cp.wait_send()
        mine.wait()

    return pl.pallas_call(
        body, name=name,
        out_shape=jax.ShapeDtypeStruct((N_DEV, r, n), v.dtype),
        in_specs=[pl.BlockSpec(memory_space=pltpu.VMEM)],
        out_specs=pl.BlockSpec(memory_space=pltpu.VMEM),
        scratch_shapes=[pltpu.SemaphoreType.DMA((N_DEV - 1,)), pltpu.SemaphoreType.DMA((N_DEV - 1,)), pltpu.SemaphoreType.DMA],
    )(v)


def _all_gather_big(vs, name, after=()):
    na, nf = len(vs), len(after)

    def body(*refs):
        v_refs, out_refs = refs[:na], refs[na + nf:2 * na + nf]
        send_sems, recv_sems, local_sems = refs[2 * na + nf:]
        x, y, c = lax.axis_index("x"), lax.axis_index("y"), lax.axis_index("c")
        me, sibling = (x, y, c), (x, y, 1 - c)
        chips = [(1 - x, y), (x, 1 - y), (1 - x, 1 - y)]

        def rows(a, px, py, pc):
            return out_refs[a].at[4 * px + 2 * py + pc]

        def copy(a, k, block, to, src=None):
            return pltpu.make_async_remote_copy(
                src_ref=rows(a, *block) if src is None else src, dst_ref=rows(a, *block),
                send_sem=send_sems.at[7 * a + k], recv_sem=recv_sems.at[7 * a + k], device_id=to, device_id_type=MESH)

        mine = [pltpu.make_async_copy(v_refs[a], rows(a, *me), local_sems.at[a]) for a in range(na)]
        for cp in mine:
            cp.start()
        first = []
        for a in range(na):
            first.append(copy(a, 0, me, sibling, src=v_refs[a]))
            first += [copy(a, 1 + j, me, (*chip, c), src=v_refs[a]) for j, chip in enumerate(chips)]
        for cp in first:
            cp.start()
        passed = []
        for j, chip in enumerate(chips):
            for a in range(na):
                copy(a, 1 + j, (*chip, c), me).wait_recv()
                forward = copy(a, 4 + j, (*chip, c), sibling)
                forward.start()
                passed.append(forward)
        for a in range(na):
            copy(a, 0, sibling, me).wait_recv()
            for j, chip in enumerate(chips):
                copy(a, 4 + j, (*chip, 1 - c), me).wait_recv()
        for cp in first + passed:
            cp.wait_send()
        for cp in mine:
            cp.wait()

    return pl.pallas_call(
        body, name=name,
        out_shape=[jax.ShapeDtypeStruct((N_DEV,) + v.shape, v.dtype) for v in vs],
        in_specs=[pl.BlockSpec(memory_space=pl.ANY)] * (na + nf),
        out_specs=[pl.BlockSpec(memory_space=pl.ANY)] * na,
        scratch_shapes=[pltpu.SemaphoreType.DMA((7 * na,)), pltpu.SemaphoreType.DMA((7 * na,)),
                        pltpu.SemaphoreType.DMA((na,))],
    )(*vs, *after)


_HBM = pl.BlockSpec(memory_space=pltpu.HBM)
_SEM = pl.BlockSpec(memory_space=pltpu.SEMAPHORE)
_EFFECT = pltpu.SideEffectType.DATAFLOW_SIDE_EFFECTING


def _place_own(block, me):
    land = lax.empty((N_DEV,) + block.shape, block.dtype)
    return lax.dynamic_update_slice(land, block[None], (me,) + (0,) * block.ndim)


def _copies_start(srcs, lands, scatter, after, name):
    na = len(srcs)
    afters = tuple(after) if isinstance(after, (tuple, list)) else (after,)

    def body(*refs):
        src_refs, land_refs = refs[:na], refs[na:2 * na]
        sems = refs[2 * na + len(afters):4 * na + len(afters)]
        token = refs[-1]
        x, y, c = lax.axis_index("x"), lax.axis_index("y"), lax.axis_index("c")
        me = 4 * x + 2 * y + c
        for a in range(na):
            for k in range(1, N_DEV):
                px, py, pc = _peer(x, y, c, k)
                src = src_refs[a].at[4 * px + 2 * py + pc] if scatter else src_refs[a]
                pltpu.make_async_remote_copy(
                    src_ref=src, dst_ref=land_refs[a].at[me], send_sem=sems[2 * a], recv_sem=sems[2 * a + 1],
                    device_id=(px, py, pc), device_id_type=MESH).start()
        token[...] = jnp.zeros_like(token)

    hbm = lambda t: pltpu.HBM(t.shape, t.dtype)
    out = pl.pallas_call(
        body, name=name,
        out_shape=tuple([pltpu.SemaphoreType.DMA(())] * (2 * na) + [hbm(t) for t in srcs] + [hbm(t) for t in lands]
                        + [jax.ShapeDtypeStruct((8, 128), F32)]),
        in_specs=[_HBM] * (2 * na) + [pl.BlockSpec(memory_space=pl.ANY)] * len(afters),
        out_specs=tuple([_SEM] * (2 * na) + [_HBM] * (2 * na) + [pl.BlockSpec(memory_space=pltpu.VMEM)]),
        input_output_aliases={i: 2 * na + i for i in range(2 * na)},
        compiler_params=pltpu.CompilerParams(has_side_effects=_EFFECT),
    )(*[pltpu.with_memory_space_constraint(t, pltpu.HBM) for t in list(srcs) + list(lands)], *afters)
    return out[:2 * na], out[2 * na:3 * na], out[3 * na:4 * na], out[-1]


def _exchange_start(gs, me, after, name):
    own = [lax.dynamic_index_in_dim(g, me, 0, keepdims=False) for g in gs]
    return _copies_start(gs, [_place_own(o, me) for o in own], True, after, name)


def _copies_wait(sems, srcs, lands, after, name):
    na = len(srcs)

    def body(*refs):
        land_refs = refs[na:2 * na]
        sem_refs = refs[2 * na:4 * na]
        x, y, c = lax.axis_index("x"), lax.axis_index("y"), lax.axis_index("c")
        for a in range(na):
            seven = land_refs[a].at[pl.ds(0, N_DEV - 1)]
            copy = pltpu.make_async_remote_copy(
                src_ref=seven, dst_ref=seven, send_sem=sem_refs[2 * a], recv_sem=sem_refs[2 * a + 1],
                device_id=(x, y, c), device_id_type=MESH)
            copy.wait_send()
            copy.wait_recv()

    hbm = lambda t: pltpu.HBM(t.shape, t.dtype)
    out = pl.pallas_call(
        body, name=name,
        out_shape=tuple([hbm(t) for t in srcs] + [hbm(t) for t in lands]),
        in_specs=[_HBM] * (2 * na) + [_SEM] * (2 * na) + [pl.BlockSpec(memory_space=pl.ANY)],
        out_specs=tuple([_HBM] * (2 * na)),
        input_output_aliases={i: i for i in range(2 * na)},
        compiler_params=pltpu.CompilerParams(has_side_effects=_EFFECT),
    )(*srcs, *lands, *sems, after)
    return out[na:]


def _adamw_math(w, g, m, v):
    m = B1 * m + (1.0 - B1) * g
    v = B2 * v + (1.0 - B2) * (g * g)
    m_hat = m / (1.0 - B1 ** STEP)
    v_hat = v / (1.0 - B2 ** STEP)
    return -LR * (m_hat / (jnp.sqrt(v_hat) + AEPS) + WD * w), m, v


def _sum_adamw(parts, w, m, v, name):
    _, r, n = parts.shape
    tr = 256 if r % 256 == 0 else r

    def body(p_ref, w_ref, m_ref, v_ref, g_ref, d_ref, nm_ref, nv_ref):
        g = p_ref[0].astype(F32)
        for dev in range(1, N_DEV):
            g = g + p_ref[dev].astype(F32)
        g_ref[...] = g
        d_ref[...], nm_ref[...], nv_ref[...] = _adamw_math(w_ref[...], g, m_ref[...], v_ref[...])

    rows = pl.BlockSpec((None, tr, n), lambda i: (0, i, 0))
    sd = jax.ShapeDtypeStruct((1, r, n), F32)
    return pl.pallas_call(
        body, name=name, grid=(r // tr,), out_shape=(sd, sd, sd, sd),
        in_specs=[pl.BlockSpec((N_DEV, tr, n), lambda i: (0, i, 0)), rows, rows, rows],
        out_specs=(rows, rows, rows, rows),
        compiler_params=_cparams(("parallel",)),
    )(parts, w, m, v)


def _ada_fwd(c_all, ada_w, ada_b_cols):
    nb, ncol = c_all.shape[0], ada_w.shape[1]

    def body(c_ref, w_ref, b_ref, mod_ref, cond_ref):
        cond = _silu(c_ref[...])
        cond_ref[...] = cond
        mod_ref[...] = _dot_hi(cond, w_ref[...]) + b_ref[...]

    return pl.pallas_call(
        body, name="ada_fwd",
        out_shape=(jax.ShapeDtypeStruct((nb, ncol), F32), jax.ShapeDtypeStruct((nb, D), F32)),
        compiler_params=_cparams(),
    )(c_all, ada_w, ada_b_cols)


def _ada_bwd(cond_all, dmod_all, dmod_cols, smalls):
    ncol, nsm = dmod_cols.shape[1], smalls.shape[1]

    def body(cond_ref, dm_ref, dmc_ref, sm_ref, gw_ref, gb_ref, gs_ref):
        gw_ref[...] = lax.dot_general(cond_ref[...], dmc_ref[...], (((0,), (0,)), ((), ())),
                                      preferred_element_type=F32, precision=HI)
        gb_ref[...] = jnp.sum(dm_ref[...], axis=0, keepdims=True)
        gs_ref[...] = jnp.sum(sm_ref[...], axis=0, keepdims=True)

    return pl.pallas_call(
        body, name="ada_bwd",
        out_shape=(jax.ShapeDtypeStruct((D, ncol), F32), jax.ShapeDtypeStruct((1, 6 * D), F32),
                   jax.ShapeDtypeStruct((1, nsm), F32)),
        compiler_params=_cparams(),
    )(cond_all, dmod_all, dmod_cols, smalls)


IN_CUTS = (0, QW, QW + 2 * KVW, QW + 2 * KVW + CONVW, QW + 2 * KVW + CONVW + 2 * DH,
           QW + 2 * KVW + CONVW + 2 * DH + DNW, QW + 2 * KVW + CONVW + 2 * DH + DNW + D, IN_W)
IN_WIDTHS = tuple(b - a for a, b in zip(IN_CUTS[:-1], IN_CUTS[1:]))
IN_SHARD = IN_W // N_DEV


def _inproj_fwd(x, mod, g1, w_blk):
    B, S, _ = x.shape
    tm = _tile(S)

    def body(x_ref, mod_ref, g_ref, w_ref, h_ref, *o_refs):
        h = _rms_mod(x_ref[...], g_ref[...], mod_ref[1:2, :], mod_ref[0:1, :]).astype(BF16)
        h_ref[...] = h
        full = jnp.concatenate([_dot_nt(h, w_ref[j]) for j in range(N_DEV)], axis=1)
        for o_ref, lo, hi in zip(o_refs, IN_CUTS[:-1], IN_CUTS[1:]):
            o_ref[...] = full[:, lo:hi]

    return pl.pallas_call(
        body, name="inproj_fwd", grid=(B, S // tm),
        out_shape=[jax.ShapeDtypeStruct((B, S, D), BF16)] + [jax.ShapeDtypeStruct((B, S, w), F32) for w in IN_WIDTHS],
        in_specs=[_rows(tm, D), _perb(6, D), _full((1, D)), _resident(w_blk.shape)],
        out_specs=[_rows(tm, D)] + [_rows(tm, w) for w in IN_WIDTHS],
        compiler_params=_cparams(("parallel", "arbitrary")),
    )(x, mod, g1, w_blk)


def _inproj_bwd(x, mod, g1, dx1, dps, w_blk):
    B, S, _ = x.shape
    tm = _tile(S)
    n = len(dps)

    def body(x_ref, mod_ref, g_ref, dx1_ref, *refs):
        dp_refs, w_ref = refs[:n], refs[n]
        dblk_ref, gx_ref, dg_ref, dsc_ref, dsh_ref = refs[n + 1:]
        b, i = pl.program_id(0), pl.program_id(1)
        full = jnp.concatenate([r[...].astype(F32) for r in dp_refs], axis=1)
        dh = None
        for j in range(N_DEV):
            blk = full[:, IN_SHARD * j:IN_SHARD * (j + 1)].astype(BF16)
            dblk_ref[j] = blk
            t = jnp.dot(blk, w_ref[j], preferred_element_type=F32)
            dh = t if dh is None else dh + t
        _, vjp = jax.vjp(_rms_mod, x_ref[...], g_ref[...], mod_ref[1:2, :], mod_ref[0:1, :])
        dx, dg, dsc, dsh = vjp(dh)
        gx_ref[...] = dx1_ref[...] + dx

        @pl.when((b == 0) & (i == 0))
        def _():
            dg_ref[...] = jnp.zeros_like(dg_ref)

        @pl.when(i == 0)
        def _():
            dsc_ref[...] = jnp.zeros_like(dsc_ref)
            dsh_ref[...] = jnp.zeros_like(dsh_ref)

        dg_ref[...] += dg
        dsc_ref[...] += dsc
        dsh_ref[...] += dsh

    return pl.pallas_call(
        body, name="inproj_bwd", grid=(B, S // tm),
        out_shape=[jax.ShapeDtypeStruct((B, N_DEV, S, IN_SHARD), BF16), jax.ShapeDtypeStruct((B, S, D), F32),
                   jax.ShapeDtypeStruct((1, D), F32), jax.ShapeDtypeStruct((B, 1, D), F32),
                   jax.ShapeDtypeStruct((B, 1, D), F32)],
        in_specs=[_rows(tm, D), _perb(6, D), _full((1, D)), _rows(tm, D)]
                 + [_rows(tm, w) for w in IN_WIDTHS] + [_resident(w_blk.shape)],
        out_specs=[pl.BlockSpec((None, N_DEV, tm, IN_SHARD), lambda b, i: (b, 0, i, 0)), _rows(tm, D),
                   _full((1, D)), _perb(1, D), _perb(1, D)],
        compiler_params=_cparams(("arbitrary", "arbitrary")),
    )(x, mod, g1, dx1, *dps, w_blk)


def _wgrad(a, b, name, after=None):
    B, na, S, K = a.shape
    nb, N = b.shape[1], b.shape[3]
    G = max(na, nb)
    tm = min(4096, S)
    nt = S // tm
    last = B * nt - 1

    def body(a_ref, b_ref, *rest):
        o_ref, acc = rest[-2:]
        t = pl.program_id(1)

        @pl.when(t == 0)
        def _():
            acc[...] = jnp.zeros_like(acc)

        acc[...] += lax.dot_general(a_ref[...], b_ref[...], (((0,), (0,)), ((), ())), preferred_element_type=F32)

        @pl.when(t == last)
        def _():
            o_ref[...] = acc[...].astype(BF16)

    return pl.pallas_call(
        body, name=name, grid=(G, B * nt),
        out_shape=jax.ShapeDtypeStruct((G, K, N), BF16),
        in_specs=[pl.BlockSpec((None, None, tm, K), lambda g, t: (t // nt, g if na > 1 else 0, t % nt, 0)),
                  pl.BlockSpec((None, None, tm, N), lambda g, t: (t // nt, g if nb > 1 else 0, t % nt, 0))]
                 + ([] if after is None else [pl.BlockSpec(memory_space=pl.ANY)]),
        out_specs=pl.BlockSpec((None, K, N), lambda g, t: (g, 0, 0)),
        scratch_shapes=[pltpu.VMEM((K, N), F32)],
        compiler_params=_cparams(("parallel", "arbitrary")),
    )(*((a, b) if after is None else (a, b, after)))


LANES = 128


def _attn_consts():
    inv_freq = THETA ** (-jnp.arange(0, ROT, 2, dtype=F32) / ROT)
    head = jnp.concatenate([inv_freq, inv_freq, jnp.zeros((HD - ROT,), F32)])
    invf = jnp.tile(head, LANES // HD)[None, :]
    mean_of = lambda w: jnp.asarray(np.kron(np.eye(w // HD), np.full((HD, HD), 1.0 / HD)), BF16)
    return invf, mean_of(QW), mean_of(KVW)


def _rope_tables(pos, invf):
    B, S, _ = pos.shape
    tr = min(1024, S)

    def body(p_ref, f_ref, c_ref, s_ref):
        ang = p_ref[...].astype(F32) * f_ref[...]
        c_ref[...] = jnp.cos(ang)
        s_ref[...] = jnp.sin(ang)

    sd = jax.ShapeDtypeStruct((B, S, LANES), F32)
    return pl.pallas_call(
        body, name="rope_tables", grid=(B, S // tr), out_shape=[sd, sd],
        in_specs=[_rows(tr, 1), _full((1, LANES))], out_specs=[_rows(tr, LANES), _rows(tr, LANES)],
        compiler_params=_cparams(("parallel", "parallel")),
    )(pos, invf)


def _rope_expand(cos, sin, reps):
    lane = lax.broadcasted_iota(jnp.int32, cos.shape, 1) % HD
    sa = jnp.where((lane >= ROT // 2) & (lane < ROT), sin, 0.0)
    sb = jnp.where(lane < ROT // 2, -sin, 0.0)
    rep = lambda t: jnp.concatenate([t] * reps, axis=1) if reps > 1 else t
    return rep(cos), rep(sa), rep(sb)


@jax.custom_vjp
def _rope(t, cos, sa, sb):
    w = t.shape[1]
    return t * cos + pltpu.roll(t, ROT // 2, 1) * sa + pltpu.roll(t, w - ROT // 2, 1) * sb


def _rope_fwd(t, cos, sa, sb):
    return _rope(t, cos, sa, sb), (cos, sa, sb)


def _rope_bwd(res, d):
    cos, sa, sb = res
    w = d.shape[1]
    dt = d * cos + pltpu.roll(d * sa, w - ROT // 2, 1) + pltpu.roll(d * sb, ROT // 2, 1)
    return dt, jnp.zeros_like(cos), jnp.zeros_like(sa), jnp.zeros_like(sb)


_rope.defvjp(_rope_fwd, _rope_bwd)


def _head_norm(t, g, mean_of):
    hi, lo = _split(t * t)
    ms = jnp.dot(hi, mean_of, preferred_element_type=F32) + jnp.dot(lo, mean_of, preferred_element_type=F32)
    return t * lax.rsqrt(ms + EPS) * g


def _attn_block(q, kvp, kvc, qg, kg, sinks, tq, tk, mq, mk, valid):
    qn = _rope(_head_norm(q, jnp.concatenate([qg] * HQ, axis=1), mq), *tq) * (HD ** -0.5)
    kv = jnp.concatenate([kvp, kvc], axis=0)
    kn = _rope(_head_norm(kv[:, 0:KVW], jnp.concatenate([kg] * HKV, axis=1), mk), *tk)
    per_tile = LANES // HD
    vT = jnp.transpose(kv[:, KVW:2 * KVW])
    qT = [jnp.transpose(qn[:, LANES * t:LANES * (t + 1)]) for t in range(QW // LANES)]
    head_T = lambda h: qT[h // per_tile][HD * (h % per_tile):HD * (h % per_tile + 1), :]
    none = jnp.zeros((HD, GRP * BLK), F32)
    o_T = []
    for j in range(HKV):
        q4T = jnp.concatenate([head_T(GRP * j + i) for i in range(GRP)], axis=1)
        sT = _dot(kn, jnp.concatenate([q4T, none] if j == 0 else [none, q4T], axis=0))
        sT = jnp.where(valid, sT, -1e30)
        sink = jnp.concatenate([jnp.broadcast_to(sinks[:, GRP * j + i:GRP * j + i + 1], (1, BLK)) for i in range(GRP)], axis=1)
        m = lax.stop_gradient(jnp.maximum(jnp.max(sT, axis=0, keepdims=True), sink))
        pT = jnp.exp(sT - m)
        den = jnp.sum(pT, axis=0, keepdims=True) + jnp.exp(sink - m)
        oT = _dot(vT[HD * j:HD * (j + 1), :], pT) * (1.0 / den)
        o_T += [oT[:, BLK * i:BLK * (i + 1)] for i in range(GRP)]
    return jnp.concatenate([jnp.transpose(jnp.concatenate(o_T[per_tile * t:per_tile * (t + 1)], axis=0))
                            for t in range(QW // LANES)], axis=1)


def _attn_tables(cp_ref, cc_ref, sp_ref, sc_ref, n):
    tq = _rope_expand(cc_ref[...], sc_ref[...], QW // LANES)
    tk = _rope_expand(jnp.concatenate([cp_ref[...], cc_ref[...]], axis=0),
                      jnp.concatenate([sp_ref[...], sc_ref[...]], axis=0), KVW // LANES)
    qi = lax.broadcasted_iota(jnp.int32, (2 * BLK, GRP * BLK), 1) % BLK + BLK
    kj = lax.broadcasted_iota(jnp.int32, (2 * BLK, GRP * BLK), 0)
    dist = qi - kj
    valid = (dist >= 0) & (dist < BLK) & ((kj >= BLK) | (n > 0))
    return tq, tk, valid


def _attn_fwd(aq, akv, cos, sin, qg, kg, sinks, mq, mk):
    B, S, _ = aq.shape
    nb = S // BLK

    def body(q_ref, kvp_ref, kvc_ref, cp_ref, cc_ref, sp_ref, sc_ref, qg_ref, kg_ref, sk_ref, mq_ref, mk_ref, o_ref):
        tq, tk, valid = _attn_tables(cp_ref, cc_ref, sp_ref, sc_ref, pl.program_id(1))
        o_ref[...] = _attn_block(q_ref[...], kvp_ref[...], kvc_ref[...], qg_ref[...], kg_ref[...], sk_ref[...],
                                 tq, tk, mq_ref[...], mk_ref[...], valid)

    prev = lambda b, n: (b, jnp.maximum(n - 1, 0), 0)
    cur = lambda b, n: (b, n, 0)
    return pl.pallas_call(
        body, name="attn_fwd", grid=(B, nb),
        out_shape=jax.ShapeDtypeStruct((B, S, QW), F32),
        in_specs=[pl.BlockSpec((None, BLK, QW), cur), pl.BlockSpec((None, BLK, 2 * KVW), prev),
                  pl.BlockSpec((None, BLK, 2 * KVW), cur), pl.BlockSpec((None, BLK, LANES), prev),
                  pl.BlockSpec((None, BLK, LANES), cur), pl.BlockSpec((None, BLK, LANES), prev),
                  pl.BlockSpec((None, BLK, LANES), cur), _full((1, HD)), _full((1, HD)), _full((1, HQ)),
                  _full((QW, QW)), _full((KVW, KVW))],
        out_specs=pl.BlockSpec((None, BLK, QW), cur),
        compiler_params=_cparams(("parallel", "arbitrary")),
    )(aq, akv, akv, cos, cos, sin, sin, qg, kg, sinks, mq, mk)


def _attn_bwd(aq, akv, cos, sin, qg, kg, sinks, mq, mk, do):
    B, S, _ = aq.shape
    nb = S // BLK

    def body(q_ref, kvp_ref, kvc_ref, cp_ref, cc_ref, sp_ref, sc_ref, qg_ref, kg_ref, sk_ref, mq_ref, mk_ref, do_ref,
             dq_ref, dkv_ref, dqg_ref, dkg_ref, dsk_ref, carry):
        b, i = pl.program_id(0), pl.program_id(1)
        tq, tk, valid = _attn_tables(cp_ref, cc_ref, sp_ref, sc_ref, nb - 1 - i)
        fn = functools.partial(_attn_block, tq=tq, tk=tk, mq=mq_ref[...], mk=mk_ref[...], valid=valid)
        _, vjp = jax.vjp(fn, q_ref[...], kvp_ref[...], kvc_ref[...], qg_ref[...], kg_ref[...], sk_ref[...])
        dq, dkvp, dkvc, dqg, dkg, dsk = vjp(do_ref[...])

        @pl.when(i == 0)
        def _():
            carry[...] = jnp.zeros_like(carry)

        @pl.when((b == 0) & (i == 0))
        def _():
            dqg_ref[...] = jnp.zeros_like(dqg_ref)
            dkg_ref[...] = jnp.zeros_like(dkg_ref)
            dsk_ref[...] = jnp.zeros_like(dsk_ref)

        dq_ref[...] = dq.astype(BF16)
        dkv_ref[...] = (dkvc + carry[...]).astype(BF16)
        carry[...] = dkvp
        dqg_ref[...] += dqg
        dkg_ref[...] += dkg
        dsk_ref[...] += dsk

    prev = lambda b, i: (b, jnp.maximum(nb - 2 - i, 0), 0)
    cur = lambda b, i: (b, nb - 1 - i, 0)
    return pl.pallas_call(
        body, name="attn_bwd", grid=(B, nb),
        out_shape=[jax.ShapeDtypeStruct((B, S, QW), BF16), jax.ShapeDtypeStruct((B, S, 2 * KVW), BF16),
                   jax.ShapeDtypeStruct((1, HD), F32), jax.ShapeDtypeStruct((1, HD), F32),
                   jax.ShapeDtypeStruct((1, HQ), F32)],
        in_specs=[pl.BlockSpec((None, BLK, QW), cur), pl.BlockSpec((None, BLK, 2 * KVW), prev),
                  pl.BlockSpec((None, BLK, 2 * KVW), cur), pl.BlockSpec((None, BLK, LANES), prev),
                  pl.BlockSpec((None, BLK, LANES), cur), pl.BlockSpec((None, BLK, LANES), prev),
                  pl.BlockSpec((None, BLK, LANES), cur), _full((1, HD)), _full((1, HD)), _full((1, HQ)),
                  _full((QW, QW)), _full((KVW, KVW)), pl.BlockSpec((None, BLK, QW), cur)],
        out_specs=[pl.BlockSpec((None, BLK, QW), cur), pl.BlockSpec((None, BLK, 2 * KVW), cur),
                   _full((1, HD)), _full((1, HD)), _full((1, HQ))],
        scratch_shapes=[pltpu.VMEM((BLK, 2 * KVW), F32)],
        compiler_params=_cparams(("arbitrary", "arbitrary")),
    )(aq, akv, akv, cos, cos, sin, sin, qg, kg, sinks, mq, mk, do)


def _conv_taps(xe, w, rows):
    y = None
    for j in range(CONV):
        sh = pltpu.roll(xe, CONV - 1 - j, 0)[8:8 + rows, :] if j < CONV - 1 else xe[8:8 + rows, :]
        y = sh * w[j:j + 1, :] if y is None else y + sh * w[j:j + 1, :]
    return y


def _softplus(x):
    return jnp.maximum(x, 0.0) + jnp.log1p(jnp.exp(-jnp.abs(x)))


_BMM = (((2,), (1,)), ((0,), (0,)))
_BMM_NT = (((2,), (2,)), ((0,), (0,)))
_BMM_TN = (((1,), (1,)), ((0,), (0,)))


def _bmm(a, b, dims=_BMM):
    return lax.dot_general(a.astype(BF16), b.astype(BF16), dims, preferred_element_type=F32)


def _split(a):
    hi = a.astype(BF16)
    return hi, (a - hi.astype(F32)).astype(BF16)


def _bmm3(a, b, dims=_BMM):
    ah, al = _split(a)
    bh, bl = _split(b)
    d = lambda p, q: lax.dot_general(p, q, dims, preferred_element_type=F32)
    return d(ah, bh) + (d(ah, bl) + d(al, bh))


TRI_BASE = 8


def _tri_inverse(L):
    ii = lax.broadcasted_iota(jnp.int32, (CH, CH), 0)
    jj = lax.broadcasted_iota(jnp.int32, (CH, CH), 1)
    same = lambda size: (ii // size) == (jj // size)
    diag = jnp.where(same(TRI_BASE), L, 0.0)
    X = (ii == jj).astype(F32) - diag
    P = diag
    n = 2
    while n < TRI_BASE:
        P = _bmm3(P, P)
        X = X + _bmm3(X, P)
        n *= 2
    size = TRI_BASE
    while size < CH:
        joint = jnp.where(same(2 * size) & jnp.logical_not(same(size)), L, 0.0)
        X = X - _bmm3(X, _bmm3(joint, X))
        size *= 2
    return X


@jax.custom_vjp
def _tri_inverse_known(L, T):
    return T


def _tri_inverse_known_fwd(L, T):
    return T, T


def _tri_inverse_known_bwd(T, dT):
    Tt = jnp.swapaxes(T, 1, 2)
    return -_bmm(Tt, _bmm(dT, Tt)), jnp.zeros_like(T)


_tri_inverse_known.defvjp(_tri_inverse_known_fwd, _tri_inverse_known_bwd)


def _triangle(n, upper):
    ii = lax.broadcasted_iota(jnp.int32, (n, CH, CH), 1)
    jj = lax.broadcasted_iota(jnp.int32, (n, CH, CH), 2)
    return ((ii <= jj) if upper else (ii >= jj)).astype(BF16)


@jax.custom_vjp
def _cumsum_rows(g):
    g0 = g.astype(BF16)
    r1 = g - g0.astype(F32)
    g1 = r1.astype(BF16)
    g2 = (r1 - g1.astype(F32)).astype(BF16)
    tri = _triangle(g.shape[0], False)
    d = lambda q: lax.dot_general(tri, q, _BMM, preferred_element_type=F32)
    return d(g0) + (d(g1) + d(g2))


def _cumsum_rows_fwd(g):
    return _cumsum_rows(g), None


def _cumsum_rows_bwd(_, dy):
    hi, lo = _split(dy)
    tri = _triangle(dy.shape[0], True)
    d = lambda q: lax.dot_general(tri, q, _BMM, preferred_element_type=F32)
    return (d(hi) + d(lo),)


_cumsum_rows.defvjp(_cumsum_rows_fwd, _cumsum_rows_bwd)


def _row_sums(t):
    n, r, w = t.shape
    hi, lo = _split(t.reshape(n * r, w))
    ones = jnp.ones((w, w), BF16)
    s = jnp.dot(hi, ones, preferred_element_type=F32) + jnp.dot(lo, ones, preferred_element_type=F32)
    return s.reshape(n, r, w)


def _dn_prep(t_known, qr, kr, v, a_raw, b_raw, a_log, dt_b):
    n = qr.shape[0]
    ii = lax.broadcasted_iota(jnp.int32, (n, CH, CH), 1)
    jj = lax.broadcasted_iota(jnp.int32, (n, CH, CH), 2)
    incl, strict = ii >= jj, ii > jj
    q = qr * lax.rsqrt(_row_sums(qr * qr) + EPS) * (DK ** -0.5)
    k = kr * lax.rsqrt(_row_sums(kr * kr) + EPS)
    beta = _sigmoid(b_raw)
    g = -jnp.exp(a_log) * _softplus(a_raw + dt_b)
    gcb = _cumsum_rows(jnp.broadcast_to(g, (n, CH, DK)))
    gc = gcb[:, :, 0:1]
    gc_row = jnp.swapaxes(gcb, 1, 2)[:, 0:1, 0:CH]
    decay = jnp.where(incl, jnp.exp(jnp.where(incl, gc - gc_row, 0.0)), 0.0)
    kb = k * beta
    L = jnp.where(strict, _bmm(kb, k, _BMM_NT) * decay, 0.0)
    T = _tri_inverse(L) if t_known is None else _tri_inverse_known(L, t_known)
    eg = jnp.exp(gc)
    u = _bmm(T, v * beta)
    w = _bmm(T, kb * eg)
    a_in = _bmm(q, k, _BMM_NT) * decay
    g_last = gc[:, CH - 1:CH, :]
    return u, w, q * eg, k * jnp.exp(g_last - gc), a_in, jnp.exp(g_last), T


def _dn_step(S0, u, w, qd, kd, a_in, cd):
    r = _bmm(jnp.concatenate([w, qd], axis=1), S0)
    v_new = u - r[:, 0:CH, :]
    o = r[:, CH:2 * CH, :] + _bmm(a_in, v_new)
    S1 = S0 * cd + _bmm(kd, v_new, _BMM_TN)
    return o, S1


def _dn_stack(cq, ba, al, dt, G):
    cols = [[] for _ in range(7)]
    for c in range(G):
        rows = slice(CH * c, CH * (c + 1))
        for h in range(DH):
            parts = (cq[rows, DK * h:DK * (h + 1)], cq[rows, DNW + DK * h:DNW + DK * (h + 1)],
                     cq[rows, 2 * DNW + DK * h:2 * DNW + DK * (h + 1)], ba[rows, DH + h:DH + h + 1],
                     ba[rows, h:h + 1], al[:, h:h + 1], dt[:, h:h + 1])
            for col, p in zip(cols, parts):
                col.append(p)
    return tuple(jnp.stack(col) for col in cols)


def _dn_group(S, want):
    g = want
    while (S // CH) % g:
        g //= 2
    return g


def _dn_prep_fwd(xin, conv_w, ba, a_log, dt_b):
    B, S, _ = xin.shape
    nc = S // CH
    G = _dn_group(S, 8)
    r8 = G * CH // 8

    def body(xp_ref, x_ref, cw_ref, ba_ref, al_ref, dt_ref, cq_ref, u_ref, w_ref, qd_ref, kd_ref, a_ref, t_ref, cd_ref):
        xp = jnp.where(pl.program_id(1) > 0, xp_ref[...], 0.0)
        cq = _silu(_conv_taps(jnp.concatenate([xp, x_ref[...]], axis=0), cw_ref[...], G * CH))
        cq_ref[...] = cq
        ops = _dn_stack(cq, ba_ref[...], al_ref[...], dt_ref[...], G)
        u, w, qd, kd, a_in, cd, T = _dn_prep(None, *ops)
        lane4 = lax.broadcasted_iota(jnp.int32, (1, DH), 1)
        for c in range(G):
            rows = slice(CH * c, CH * (c + 1))
            cdrow = jnp.zeros((1, DH), F32)
            for h in range(DH):
                n = DH * c + h
                lanes = slice(DK * h, DK * (h + 1))
                u_ref[rows, lanes] = u[n]
                w_ref[rows, lanes] = w[n]
                qd_ref[rows, lanes] = qd[n]
                kd_ref[rows, lanes] = kd[n]
                a_ref[rows, CH * h:CH * (h + 1)] = a_in[n]
                t_ref[rows, CH * h:CH * (h + 1)] = T[n]
                cdrow = cdrow + jnp.where(lane4 == h, cd[n], 0.0)
            cd_ref[c] = cdrow

    wide = jax.ShapeDtypeStruct((B, S, DNW), F32)
    sq = jax.ShapeDtypeStruct((B, S, DH * CH), F32)
    return pl.pallas_call(
        body, name="dn_prep_fwd", grid=(B, nc // G),
        out_shape=[jax.ShapeDtypeStruct((B, S, CONVW), F32), wide, wide, wide, wide, sq, sq,
                   jax.ShapeDtypeStruct((B, nc, 1, DH), F32)],
        in_specs=[pl.BlockSpec((None, 8, CONVW), lambda b, i: (b, jnp.maximum(i * r8 - 1, 0), 0)),
                  _rows(G * CH, CONVW), _full((CONV, CONVW)), _rows(G * CH, 2 * DH), _full((1, DH)), _full((1, DH))],
        out_specs=[_rows(G * CH, CONVW)] + [_rows(G * CH, DNW)] * 4 + [_rows(G * CH, DH * CH)] * 2
                  + [pl.BlockSpec((None, G, 1, DH), lambda b, i: (b, i, 0, 0))],
        compiler_params=_cparams(("parallel", "arbitrary")),
    )(xin, xin, conv_w, ba, a_log, dt_b)


def _dn_seq_specs(B, steps, gs, rev):
    at = (lambda i: steps - 1 - i) if rev else (lambda i: i)
    wide = pl.BlockSpec((B, gs * CH, DNW), lambda i: (0, at(i), 0))
    a_spec = pl.BlockSpec((B, gs * CH, DH * CH), lambda i: (0, at(i), 0))
    cd_spec = pl.BlockSpec((B, gs, 1, DH), lambda i: (0, at(i), 0, 0))
    st_spec = pl.BlockSpec((B, gs, DH, DK, DK), lambda i: (0, at(i), 0, 0, 0))
    return wide, a_spec, cd_spec, st_spec


def _dn_step_operands(B, c, u_ref, w_ref, qd_ref, kd_ref, a_ref, cd_ref):
    pairs = [(b, h) for b in range(B) for h in range(DH)]
    rows = slice(CH * c, CH * (c + 1))
    wide = lambda ref: jnp.stack([ref[b, rows, DK * h:DK * (h + 1)] for b, h in pairs])
    a_in = jnp.stack([a_ref[b, rows, CH * h:CH * (h + 1)] for b, h in pairs])
    cd = jnp.stack([cd_ref[b, c, :, h:h + 1] for b, h in pairs])
    return wide(u_ref), wide(w_ref), wide(qd_ref), wide(kd_ref), a_in, cd


def _dn_seq_fwd(u, w, qd, kd, a_in, cd):
    B, S, _ = u.shape
    nc = S // CH
    gs = _dn_group(S, 8)

    def body(u_ref, w_ref, qd_ref, kd_ref, a_ref, cd_ref, o_ref, st_ref, state):
        @pl.when(pl.program_id(0) == 0)
        def _():
            state[...] = jnp.zeros_like(state)

        S0 = state[...]
        for c in range(gs):
            for b in range(B):
                st_ref[b, c] = S0[DH * b:DH * (b + 1)]
            o, S0 = _dn_step(S0, *_dn_step_operands(B, c, u_ref, w_ref, qd_ref, kd_ref, a_ref, cd_ref))
            for b in range(B):
                for h in range(DH):
                    o_ref[b, CH * c:CH * (c + 1), DK * h:DK * (h + 1)] = o[DH * b + h]
        state[...] = S0

    wide, a_spec, cd_spec, st_spec = _dn_seq_specs(B, nc // gs, gs, False)
    return pl.pallas_call(
        body, name="dn_seq_fwd", grid=(nc // gs,),
        out_shape=[jax.ShapeDtypeStruct((B, S, DNW), F32), jax.ShapeDtypeStruct((B, nc, DH, DK, DK), F32)],
        in_specs=[wide, wide, wide, wide, a_spec, cd_spec],
        out_specs=[wide, st_spec],
        scratch_shapes=[pltpu.VMEM((B * DH, DK, DK), F32)],
        compiler_params=_cparams(("arbitrary",)),
    )(u, w, qd, kd, a_in, cd)


def _dn_seq_bwd(u, w, qd, kd, a_in, cd, states, do):
    B, S, _ = u.shape
    nc = S // CH
    gs = _dn_group(S, 8)

    def body(u_ref, w_ref, qd_ref, kd_ref, a_ref, cd_ref, st_ref, do_ref,
             du_ref, dw_ref, dqd_ref, dkd_ref, da_ref, dcd_ref, dstate):
        @pl.when(pl.program_id(0) == 0)
        def _():
            dstate[...] = jnp.zeros_like(dstate)

        lane4 = lax.broadcasted_iota(jnp.int32, (1, DH), 1)
        dS = dstate[...]
        for c in reversed(range(gs)):
            rows = slice(CH * c, CH * (c + 1))
            S0 = jnp.concatenate([st_ref[b, c] for b in range(B)], axis=0)
            do = jnp.stack([do_ref[b, rows, DK * h:DK * (h + 1)] for b in range(B) for h in range(DH)])
            _, vjp = jax.vjp(_dn_step, S0, *_dn_step_operands(B, c, u_ref, w_ref, qd_ref, kd_ref, a_ref, cd_ref))
            dS, du, dw, dqd, dkd, da, dcd = vjp((do, dS))
            for b in range(B):
                dcdrow = jnp.zeros((1, DH), F32)
                for h in range(DH):
                    n = DH * b + h
                    lanes = slice(DK * h, DK * (h + 1))
                    du_ref[b, rows, lanes] = du[n]
                    dw_ref[b, rows, lanes] = dw[n]
                    dqd_ref[b, rows, lanes] = dqd[n]
                    dkd_ref[b, rows, lanes] = dkd[n]
                    da_ref[b, rows, CH * h:CH * (h + 1)] = da[n]
                    dcdrow = dcdrow + jnp.where(lane4 == h, dcd[n], 0.0)
                dcd_ref[b, c] = dcdrow
        dstate[...] = dS

    wide, a_spec, cd_spec, st_spec = _dn_seq_specs(B, nc // gs, gs, True)
    sd = jax.ShapeDtypeStruct((B, S, DNW), F32)
    return pl.pallas_call(
        body, name="dn_seq_bwd", grid=(nc // gs,),
        out_shape=[sd, sd, sd, sd, jax.ShapeDtypeStruct((B, S, DH * CH), F32), jax.ShapeDtypeStruct((B, nc, 1, DH), F32)],
        in_specs=[wide, wide, wide, wide, a_spec, cd_spec, st_spec, wide],
        out_specs=[wide, wide, wide, wide, a_spec, cd_spec],
        scratch_shapes=[pltpu.VMEM((B * DH, DK, DK), F32)],
        compiler_params=_cparams(("arbitrary",)),
    )(u, w, qd, kd, a_in, cd, states, do)


def _dn_prep_bwd(xin, conv_w, cq, ba, a_log, dt_b, t_inv, du, dw, dqd, dkd, da, dcd):
    B, S, _ = cq.shape
    nc = S // CH
    G = _dn_group(S, 8)
    R = G * CH
    nblk = nc // G
    r8 = R // 8

    def body(xp_ref, x_ref, cw_ref, cq_ref, ba_ref, al_ref, dt_ref, t_ref, du_ref, dw_ref, dqd_ref, dkd_ref, da_ref, dcd_ref,
             dx_ref, dcw_ref, dba_ref, dal_ref, ddt_ref, carry):
        i = pl.program_id(1)

        @pl.when((pl.program_id(0) == 0) & (i == 0))
        def _():
            dal_ref[...] = jnp.zeros_like(dal_ref)
            ddt_ref[...] = jnp.zeros_like(ddt_ref)
            dcw_ref[...] = jnp.zeros_like(dcw_ref)

        @pl.when(i == 0)
        def _():
            carry[...] = jnp.zeros_like(carry)

        pairs = [(c, h) for c in range(G) for h in range(DH)]
        rows = lambda c: slice(CH * c, CH * (c + 1))
        wide = lambda ref: jnp.stack([ref[rows(c), DK * h:DK * (h + 1)] for c, h in pairs])
        square = lambda ref: jnp.stack([ref[rows(c), CH * h:CH * (h + 1)] for c, h in pairs])
        ops = _dn_stack(cq_ref[...], ba_ref[...], al_ref[...], dt_ref[...], G)
        cots = (wide(du_ref), wide(dw_ref), wide(dqd_ref), wide(dkd_ref), square(da_ref),
                jnp.stack([dcd_ref[c][:, h:h + 1] for c, h in pairs]), jnp.zeros((len(pairs), CH, CH), F32))
        _, vjp = jax.vjp(functools.partial(_dn_prep, square(t_ref)), *ops)
        dq, dk, dv, dar, dbr, dl, dd = vjp(cots)
        lane8 = lax.broadcasted_iota(jnp.int32, (CH, 2 * DH), 1)
        lane4 = lax.broadcasted_iota(jnp.int32, (1, DH), 1)
        dal = jnp.zeros((1, DH), F32)
        ddt = jnp.zeros((1, DH), F32)
        for c in range(G):
            dba = jnp.zeros((CH, 2 * DH), F32)
            for h in range(DH):
                n = DH * c + h
                dba = dba + jnp.where(lane8 == h, dbr[n], 0.0) + jnp.where(lane8 == DH + h, dar[n], 0.0)
                dal = dal + jnp.where(lane4 == h, dl[n], 0.0)
                ddt = ddt + jnp.where(lane4 == h, dd[n], 0.0)
            dba_ref[rows(c), :] = dba.astype(BF16)
        dal_ref[...] += dal
        ddt_ref[...] += ddt

        dcq = jnp.concatenate([jnp.concatenate([t[DH * c + h] for t in (dq, dk, dv) for h in range(DH)], axis=1)
                               for c in range(G)], axis=0)
        w = cw_ref[...]
        xp = jnp.where(i < nblk - 1, xp_ref[...], 0.0)
        xe = jnp.concatenate([xp, x_ref[...]], axis=0)
        taps = [(pltpu.roll(xe, CONV - 1 - j, 0) if j < CONV - 1 else xe)[8:8 + R, :] for j in range(CONV)]
        pre = sum(t * w[j:j + 1, :] for j, t in enumerate(taps))
        sg = _sigmoid(pre)
        dpre = dcq * (sg * (1.0 + pre * (1.0 - sg)))
        ext = jnp.concatenate([dpre, carry[...]], axis=0)
        dx = dpre * w[CONV - 1:CONV, :]
        for j in range(CONV - 1):
            dx = dx + pltpu.roll(ext, R + 8 - (CONV - 1 - j), 0)[0:R, :] * w[j:j + 1, :]
        dx_ref[...] = dx.astype(BF16)
        carry[...] = dpre[0:8, :]
        lane_row = lax.broadcasted_iota(jnp.int32, (CONV, CONVW), 0)
        dcw = jnp.zeros((CONV, CONVW), F32)
        for j in range(CONV):
            dcw = dcw + jnp.where(lane_row == j, jnp.sum(taps[j] * dpre, axis=0, keepdims=True), 0.0)
        dcw_ref[...] += dcw

    rev = lambda w: pl.BlockSpec((None, R, w), lambda b, i: (b, nblk - 1 - i, 0))
    return pl.pallas_call(
        body, name="dn_prep_bwd", grid=(B, nblk),
        out_shape=[jax.ShapeDtypeStruct((B, S, CONVW), BF16), jax.ShapeDtypeStruct((CONV, CONVW), F32),
                   jax.ShapeDtypeStruct((B, S, 2 * DH), BF16), jax.ShapeDtypeStruct((1, DH), F32),
                   jax.ShapeDtypeStruct((1, DH), F32)],
        in_specs=[pl.BlockSpec((None, 8, CONVW), lambda b, i: (b, jnp.maximum((nblk - 1 - i) * r8 - 1, 0), 0)),
                  rev(CONVW), _full((CONV, CONVW)), rev(CONVW), rev(2 * DH), _full((1, DH)), _full((1, DH)), rev(DH * CH)]
                 + [rev(DNW)] * 4 + [rev(DH * CH), pl.BlockSpec((None, G, 1, DH), lambda b, i: (b, nblk - 1 - i, 0, 0))],
        out_specs=[rev(CONVW), _full((CONV, CONVW)), rev(2 * DH), _full((1, DH)), _full((1, DH))],
        scratch_shapes=[pltpu.VMEM((8, CONVW), F32)],
        compiler_params=_cparams(("arbitrary", "arbitrary")),
    )(xin, xin, conv_w, cq, ba, a_log, dt_b, t_inv, du, dw, dqd, dkd, da, dcd)


def _gated_norm(o, z, g):
    outs = []
    for h in range(DH):
        t = o[:, DK * h:DK * (h + 1)]
        r = lax.rsqrt(jnp.mean(t * t, axis=-1, keepdims=True) + EPS)
        outs.append(t * r * g * _silu(z[:, DK * h:DK * (h + 1)]))
    return jnp.concatenate(outs, axis=1)


def _mix_fwd(x, o_attn, o_dn, z, ga, gd, mod, dn_g, w_branch, w_out):
    B, S, _ = x.shape
    tm = _tile(S, 512)

    def body(x_ref, oa_ref, od_ref, z_ref, ga_ref, gd_ref, mod_ref, g_ref, wb_ref, wo_ref,
             x1_ref, mix_ref, mg_ref, ob_ref):
        oa = oa_ref[...].astype(BF16)
        od = _gated_norm(od_ref[...], z_ref[...], g_ref[...]).astype(BF16)
        ob_ref[0] = oa
        ob_ref[1] = od
        ya = jnp.dot(oa, wb_ref[0:QW, :], preferred_element_type=F32)
        yd = jnp.dot(od, wb_ref[QW:QW + DNW, :], preferred_element_type=F32)
        merged = (_sigmoid(ga_ref[...]) * ya + _sigmoid(gd_ref[...]) * yd).astype(BF16)
        mg_ref[...] = merged
        mix = jnp.dot(merged, wo_ref[...], preferred_element_type=F32)
        mix_ref[...] = mix
        x1_ref[...] = x_ref[...] + mod_ref[2:3, :] * mix

    return pl.pallas_call(
        body, name="mix_fwd", grid=(B, S // tm),
        out_shape=[jax.ShapeDtypeStruct((B, S, D), F32), jax.ShapeDtypeStruct((B, S, D), F32),
                   jax.ShapeDtypeStruct((B, S, D), BF16), jax.ShapeDtypeStruct((B, 2, S, QW), BF16)],
        in_specs=[_rows(tm, D), _rows(tm, QW), _rows(tm, DNW), _rows(tm, DNW), _rows(tm, D), _rows(tm, D),
                  _perb(6, D), _full((1, DK)), _resident(w_branch.shape), _resident(w_out.shape)],
        out_specs=[_rows(tm, D), _rows(tm, D), _rows(tm, D), _stacked(2, tm, QW)],
        compiler_params=_cparams(("parallel", "arbitrary")),
    )(x, o_attn, o_dn, z, ga, gd, mod, dn_g, w_branch, w_out)


def _mix_bwd(dx1, mix, o_attn, o_dn, z, ga, gd, mod, dn_g, w_branch, w_out):
    B, S, _ = dx1.shape
    tm = _tile(S, 512)

    def body(dx1_ref, mix_ref, oa_ref, od_ref, z_ref, ga_ref, gd_ref, mod_ref, g_ref, wb_ref, wo_ref,
             dmix_ref, dyo_ref, dga_ref, dgd_ref, dz_ref, doa_ref, dod_ref, dgate_ref, dg_ref):
        b, i = pl.program_id(0), pl.program_id(1)
        dx1 = dx1_ref[...]
        dmix = (dx1 * mod_ref[2:3, :]).astype(BF16)
        dmix_ref[...] = dmix
        dgate = jnp.sum(dx1 * mix_ref[...], axis=0, keepdims=True)
        dmerged = _dot_nt(dmix, wo_ref[...])
        odn, gn_vjp = jax.vjp(_gated_norm, od_ref[...], z_ref[...], g_ref[...])
        ya = _dot(oa_ref[...], wb_ref[0:QW, :])
        yd = _dot(odn, wb_ref[QW:QW + DNW, :])
        sa, sd = _sigmoid(ga_ref[...]), _sigmoid(gd_ref[...])
        dya = (dmerged * sa).astype(BF16)
        dyd = (dmerged * sd).astype(BF16)
        dyo_ref[0] = dya
        dyo_ref[1] = dyd
        dga_ref[...] = (dmerged * ya * sa * (1.0 - sa)).astype(BF16)
        dgd_ref[...] = (dmerged * yd * sd * (1.0 - sd)).astype(BF16)
        doa_ref[...] = _dot_nt(dya, wb_ref[0:QW, :])
        dodn = _dot_nt(dyd, wb_ref[QW:QW + DNW, :])
        dod, dz, dg = gn_vjp(dodn)
        dod_ref[...] = dod
        dz_ref[...] = dz.astype(BF16)

        @pl.when(i == 0)
        def _():
            dgate_ref[...] = jnp.zeros_like(dgate_ref)

        @pl.when((b == 0) & (i == 0))
        def _():
            dg_ref[...] = jnp.zeros_like(dg_ref)

        dgate_ref[...] += dgate
        dg_ref[...] += dg

    return pl.pallas_call(
        body, name="mix_bwd", grid=(B, S // tm),
        out_shape=[jax.ShapeDtypeStruct((B, S, D), BF16), jax.ShapeDtypeStruct((B, 2, S, D), BF16),
                   jax.ShapeDtypeStruct((B, S, D), BF16), jax.ShapeDtypeStruct((B, S, D), BF16),
                   jax.ShapeDtypeStruct((B, S, DNW), BF16),
                   jax.ShapeDtypeStruct((B, S, QW), F32), jax.ShapeDtypeStruct((B, S, DNW), F32),
                   jax.ShapeDtypeStruct((B, 1, D), F32), jax.ShapeDtypeStruct((1, DK), F32)],
        in_specs=[_rows(tm, D), _rows(tm, D), _rows(tm, QW), _rows(tm, DNW), _rows(tm, DNW), _rows(tm, D),
                  _rows(tm, D), _perb(6, D), _full((1, DK)), _resident(w_branch.shape), _resident(w_out.shape)],
        out_specs=[_rows(tm, D), _stacked(2, tm, D), _rows(tm, D), _rows(tm, D), _rows(tm, DNW),
                   _rows(tm, QW), _rows(tm, DNW), _perb(1, D), _full((1, DK))],
        compiler_params=_cparams(("arbitrary", "arbitrary")),
    )(dx1, mix, o_attn, o_dn, z, ga, gd, mod, dn_g, w_branch, w_out)


GU_SHARD = 2 * FFN // N_DEV
GU_HALF = N_DEV // 2


def _ffn1_fwd(x1, mod, g2, w_gu):
    B, S, _ = x1.shape
    tm = _tile(S)

    def body(x_ref, mod_ref, g_ref, w_ref, h_ref, dgate_ref, dup_ref, act_ref):
        h = _rms_mod(x_ref[...], g_ref[...], mod_ref[4:5, :], mod_ref[3:4, :]).astype(BF16)
        h_ref[...] = h
        for j in range(GU_HALF):
            gate = _dot_nt(h, w_ref[j])
            up = _dot_nt(h, w_ref[GU_HALF + j])
            sg = _sigmoid(gate)
            silu = gate * sg
            dgate_ref[j] = up * (sg * (1.0 + gate * (1.0 - sg)))
            dup_ref[j] = silu
            act_ref[j] = (silu * up).astype(BF16)

    blk = lambda dt: jax.ShapeDtypeStruct((B, GU_HALF, S, GU_SHARD), dt)
    return pl.pallas_call(
        body, name="ffn1_fwd", grid=(B, S // tm),
        out_shape=[jax.ShapeDtypeStruct((B, S, D), BF16), blk(F32), blk(F32), blk(BF16)],
        in_specs=[_rows(tm, D), _perb(6, D), _full((1, D)), _resident(w_gu.shape)],
        out_specs=[_rows(tm, D)] + [_stacked(GU_HALF, tm, GU_SHARD)] * 3,
        compiler_params=_cparams(("parallel", "arbitrary")),
    )(x1, mod, g2, w_gu)


def _ffn2_fwd(act, x1, target, mod, w_down):
    B, S, _ = x1.shape
    tm = _tile(S, 512)

    def body(a_ref, x_ref, t_ref, mod_ref, w_ref, dy_ref, loss_ref, dgate_ref):
        b, i = pl.program_id(0), pl.program_id(1)
        y = jnp.dot(a_ref[0], w_ref[0], preferred_element_type=F32)
        for j in range(1, GU_HALF):
            y = y + jnp.dot(a_ref[j], w_ref[j], preferred_element_type=F32)
        err = x_ref[...] + mod_ref[5:6, :] * y - t_ref[...]
        dy = err * (1.0 / D)
        dy_ref[...] = dy

        @pl.when((b == 0) & (i == 0))
        def _():
            loss_ref[...] = jnp.zeros_like(loss_ref)

        @pl.when(i == 0)
        def _():
            dgate_ref[...] = jnp.zeros_like(dgate_ref)

        loss_ref[...] += (0.5 / D) * jnp.sum(err * err)
        dgate_ref[...] += jnp.sum(dy * y, axis=0, keepdims=True)

    return pl.pallas_call(
        body, name="ffn2_fwd", grid=(B, S // tm),
        out_shape=[jax.ShapeDtypeStruct((B, S, D), F32), jax.ShapeDtypeStruct((1, 128), F32),
                   jax.ShapeDtypeStruct((B, 1, D), F32)],
        in_specs=[_stacked(GU_HALF, tm, GU_SHARD), _rows(tm, D), _rows(tm, D), _perb(6, D), _resident(w_down.shape)],
        out_specs=[_rows(tm, D), _full((1, 128)), _perb(1, D)],
        compiler_params=_cparams(("arbitrary", "arbitrary")),
    )(act, x1, target, mod, w_down)


def _ffn2_bwd(dy, act_dgate, act_dup, mod, w_down):
    B, S, _ = dy.shape
    tm = _tile(S)

    def body(dy_ref, dgate_ref, dup_ref, mod_ref, w_ref, dgu_ref, dyg_ref):
        dyg = (dy_ref[...] * mod_ref[5:6, :]).astype(BF16)
        dyg_ref[...] = dyg
        for j in range(GU_HALF):
            dact = _dot_nt(dyg, w_ref[j])
            dgu_ref[j] = (dact * dgate_ref[j]).astype(BF16)
            dgu_ref[GU_HALF + j] = (dact * dup_ref[j]).astype(BF16)

    return pl.pallas_call(
        body, name="ffn2_bwd", grid=(B, S // tm),
        out_shape=[jax.ShapeDtypeStruct((B, N_DEV, S, GU_SHARD), BF16), jax.ShapeDtypeStruct((B, S, D), BF16)],
        in_specs=[_rows(tm, D), _stacked(GU_HALF, tm, GU_SHARD), _stacked(GU_HALF, tm, GU_SHARD), _perb(6, D),
                  _resident(w_down.shape)],
        out_specs=[_stacked(N_DEV, tm, GU_SHARD), _rows(tm, D)],
        compiler_params=_cparams(("parallel", "arbitrary")),
    )(dy, act_dgate, act_dup, mod, w_down)


def _ffn1_bwd(dgu, x1, dy, mod, g2, w_gu):
    B, S, _ = x1.shape
    tm = _tile(S, 512)

    def body(dgu_ref, x_ref, dy_ref, mod_ref, g_ref, w_ref, dx1_ref, dg_ref, dsc_ref, dsh_ref):
        b, i = pl.program_id(0), pl.program_id(1)
        dh = jnp.dot(dgu_ref[0], w_ref[0], preferred_element_type=F32)
        for j in range(1, N_DEV):
            dh = dh + jnp.dot(dgu_ref[j], w_ref[j], preferred_element_type=F32)
        _, vjp = jax.vjp(_rms_mod, x_ref[...], g_ref[...], mod_ref[4:5, :], mod_ref[3:4, :])
        dx, dg, dsc, dsh = vjp(dh)
        dx1_ref[...] = dy_ref[...] + dx

        @pl.when((b == 0) & (i == 0))
        def _():
            dg_ref[...] = jnp.zeros_like(dg_ref)

        @pl.when(i == 0)
        def _():
            dsc_ref[...] = jnp.zeros_like(dsc_ref)
            dsh_ref[...] = jnp.zeros_like(dsh_ref)

        dg_ref[...] += dg
        dsc_ref[...] += dsc
        dsh_ref[...] += dsh

    return pl.pallas_call(
        body, name="ffn1_bwd", grid=(B, S // tm),
        out_shape=[jax.ShapeDtypeStruct((B, S, D), F32), jax.ShapeDtypeStruct((1, D), F32),
                   jax.ShapeDtypeStruct((B, 1, D), F32), jax.ShapeDtypeStruct((B, 1, D), F32)],
        in_specs=[_stacked(N_DEV, tm, GU_SHARD), _rows(tm, D), _rows(tm, D), _perb(6, D), _full((1, D)),
                  _resident(w_gu.shape)],
        out_specs=[_rows(tm, D), _full((1, D)), _perb(1, D), _perb(1, D)],
        compiler_params=_cparams(("arbitrary", "arbitrary")),
    )(dgu, x1, dy, mod, g2, w_gu)


def _adamw(w, g, m, v, name):
    def body(w_ref, g_ref, m_ref, v_ref, d_ref, nm_ref, nv_ref):
        d_ref[...], nm_ref[...], nv_ref[...] = _adamw_math(w_ref[...], g_ref[...], m_ref[...], v_ref[...])

    sd = jax.ShapeDtypeStruct(w.shape, F32)
    return pl.pallas_call(body, name=name, out_shape=(sd, sd, sd), compiler_params=_cparams())(w, g, m, v)


def kernel(x, c, positions, ada_w, ada_b, norm1_g, w_in, conv_w, q_norm_g, k_norm_g, sinks, a_log, dt_bias, dn_norm_g, w_branch, w_out, norm2_g, w_gate_up, w_down, loss_target, m_ada_w, m_ada_b, m_norm1_g, m_w_in, m_conv_w, m_q_norm_g, m_k_norm_g, m_sinks, m_a_log, m_dt_bias, m_dn_norm_g, m_w_branch, m_w_out, m_norm2_g, m_w_gate_up, m_w_down, v_ada_w, v_ada_b, v_norm1_g, v_w_in, v_conv_w, v_q_norm_g, v_k_norm_g, v_sinks, v_a_log, v_dt_bias, v_dn_norm_g, v_w_branch, v_w_out, v_norm2_g, v_w_gate_up, v_w_down):
    B, S, _ = x.shape
    me = 4 * lax.axis_index("x") + 2 * lax.axis_index("y") + lax.axis_index("c")

    tr = lambda t: jnp.swapaxes(t, 1, 2)
    shards = [w[0].astype(BF16) for w in (tr(w_in), w_branch, w_out, tr(w_gate_up), w_down)]

    c_all = _all_gather_small(c, "gather_c").reshape(N_DEV * B, D)
    ncol = 6 * D // N_DEV
    mod_cols, cond_all = _ada_fwd(c_all, ada_w[0], lax.dynamic_slice(ada_b, (0, me * ncol), (1, ncol)))
    mod_all = _all_gather_small(mod_cols, "gather_mod").transpose(1, 0, 2).reshape(N_DEV * B, 6 * D)
    mod = lax.dynamic_slice(mod_all, (me * B, 0), (B, 6 * D)).reshape(B, 6, D)
    conv2 = conv_w.reshape(CONV, CONVW // N_DEV)
    conv_all = _all_gather_small(conv2, "gather_conv").transpose(1, 0, 2).reshape(CONV, CONVW)

    (w_in_b,) = _all_gather_big(shards[:1], "gather_w_in", after=(mod, conv_all))
    w_sems, w_srcs, w_lands, w_token = _copies_start(shards[1:], [_place_own(s, me) for s in shards[1:]], False,
                                                    w_in_b, "gather_rest_start")

    h1, aq, akv, dnx, ba, z, ga, gd = _inproj_fwd(x, mod, norm1_g + w_token[0, 0], w_in_b)
    invf, mean_q, mean_k = _attn_consts()
    rope_cos, rope_sin = _rope_tables(positions.reshape(B, S, 1), invf)
    o_attn = _attn_fwd(aq, akv, rope_cos, rope_sin, q_norm_g, k_norm_g, sinks, mean_q, mean_k)
    cq, dn_u, dn_w, dn_qd, dn_kd, dn_a, dn_t, dn_cd = _dn_prep_fwd(dnx, conv_all, ba, a_log, dt_bias)
    o_dn, states = _dn_seq_fwd(dn_u, dn_w, dn_qd, dn_kd, dn_a, dn_cd)
    w_branch_g, w_out_g, w_gu_b, w_down_g = _copies_wait(w_sems, w_srcs, w_lands, o_dn, "gather_wait_rest")
    w_branch_f = w_branch_g.reshape(D, D)
    w_out_f = w_out_g.reshape(D, D)
    w_down_b = w_down_g.reshape(GU_HALF, GU_SHARD, D)
    x1, mix, merged, ob = _mix_fwd(x, o_attn, o_dn, z, ga, gd, mod, dn_norm_g, w_branch_f, w_out_f)
    h2, act_dgate, act_dup, act = _ffn1_fwd(x1, mod, norm2_g, w_gu_b)
    dy, loss_part, d_gate2 = _ffn2_fwd(act, x1, loss_target, mod, w_down_b)
    loss = lax.psum(loss_part[0, 0], ("x", "y", "c"))

    one = lambda t: t.reshape(B, 1, S, t.shape[-1])
    dgu, dyg = _ffn2_bwd(dy, act_dgate, act_dup, mod, w_down_b)
    g_w_down = _wgrad(act, one(dyg), "wgrad_down")
    dx1, d_n2g, d_scale2, d_shift2 = _ffn1_bwd(dgu, x1, dy, mod, norm2_g, w_gu_b)
    g_w_gu = _wgrad(dgu, one(h2), "wgrad_gate_up")
    ffn = _exchange_start([g_w_gu, g_w_down.reshape(N_DEV, FFN // N_DEV, D)], me, dx1, "exchange_ffn_start")
    dmix, dyo, dga, dgd, dz, d_oa, d_od, d_gate1, d_dng = _mix_bwd(
        dx1, mix, o_attn, o_dn, z, ga, gd, mod, dn_norm_g + ffn[3][0, 0], w_branch_f, w_out_f)
    d_dn = _dn_seq_bwd(dn_u, dn_w, dn_qd, dn_kd, dn_a, dn_cd, states, d_od)
    ddnx, d_conv, dba, d_alog, d_dtb = _dn_prep_bwd(dnx, conv_all, cq, ba, a_log, dt_bias, dn_t, *d_dn)
    daq, dakv, d_qg, d_kg, d_sinks = _attn_bwd(aq, akv, rope_cos, rope_sin, q_norm_g, k_norm_g, sinks, mean_q, mean_k, d_oa)
    dps = [daq, dakv, ddnx, dba, dz, dga, dgd]
    dblk, grad_x, d_n1g, d_scale1, d_shift1 = _inproj_bwd(x, mod, norm1_g, dx1, dps, w_in_b)

    dmod = jnp.concatenate([d_shift1, d_scale1, d_gate1, d_shift2, d_scale2, d_gate2], axis=2).reshape(B, 6 * D)
    small = jnp.concatenate([d_n1g, d_qg, d_kg, d_sinks, d_alog, d_dtb, d_dng, d_n2g, d_conv.reshape(1, CONV * CONVW)], axis=1)
    nsm = small.shape[1]
    width = -(-max(6 * D, nsm) // 128) * 128
    rows = jnp.concatenate([jnp.pad(dmod, ((0, 0), (0, width - 6 * D))), jnp.pad(small, ((0, 8 - B - 1), (0, width - nsm)))], axis=0)
    rows_all = _all_gather_small(rows, "gather_small")
    dmod_all = rows_all[:, 0:B, 0:6 * D].reshape(N_DEV * B, 6 * D)
    dmod_cols = lax.dynamic_slice(dmod_all, (0, me * ncol), (N_DEV * B, ncol))
    grad_ada_w, grad_ada_b, small_sum = _ada_bwd(cond_all, dmod_all, dmod_cols, rows_all[:, B, :])
    sizes = [D, HD, HD, HQ, DH, DH, DK, D]
    so = np.cumsum([0] + sizes)
    g_n1, g_qg, g_kg, g_sk, g_al, g_dt, g_dn, g_n2 = [small_sum[:, so[i]:so[i + 1]] for i in range(8)]
    g_conv_all = small_sum[:, so[8]:so[8] + CONV * CONVW].reshape(CONV, N_DEV, CONVW // N_DEV)
    grad_conv = lax.dynamic_slice(g_conv_all, (0, me, 0), (CONV, 1, CONVW // N_DEV)).reshape(CONV, CONVW // N_DEV)

    g_w_in = _wgrad(dblk, one(h1), "wgrad_in", after=small_sum)
    proj = _exchange_start([g_w_in], me, small_sum, "exchange_in_start")
    g_w_out = _wgrad(one(merged), one(dmix), "wgrad_out", after=proj[3])
    g_w_branch = _wgrad(ob, dyo, "wgrad_branch", after=proj[3])
    mixer = _exchange_start([g_w_branch.reshape(N_DEV, D // N_DEV, D), g_w_out.reshape(N_DEV, D // N_DEV, D)], me,
                            proj[3], "exchange_mix_start")

    upd, grads = {}, {}

    def finish(names, parts, weights):
        for nm, p, (w, m, v) in zip(names, parts, weights):
            grads[nm], *upd[nm] = _sum_adamw(p, w, m, v, "update_" + nm)

    finish(["w_gate_up", "w_down"], _copies_wait(*ffn[:3], mixer[3], "exchange_ffn_wait"),
           [(tr(w_gate_up), tr(m_w_gate_up), tr(v_w_gate_up)), (w_down, m_w_down, v_w_down)])
    finish(["w_in"], _copies_wait(*proj[:3], grads["w_gate_up"], "exchange_in_wait"),
           [(tr(w_in), tr(m_w_in), tr(v_w_in))])
    finish(["w_branch", "w_out"], _copies_wait(*mixer[:3], grads["w_in"], "exchange_mix_wait"),
           [(w_branch, m_w_branch, v_w_branch), (w_out, m_w_out, v_w_out)])
    for nm in ("w_in", "w_gate_up"):
        grads[nm], upd[nm] = tr(grads[nm]), [tr(t) for t in upd[nm]]

    grads["ada_w"] = grad_ada_w.reshape(ada_w.shape)
    upd["ada_w"] = _adamw(ada_w, grads["ada_w"], m_ada_w, v_ada_w, "adamw_ada_w")
    small_names = ["ada_b", "norm1_g", "q_norm_g", "k_norm_g", "sinks", "a_log", "dt_bias", "dn_norm_g", "norm2_g", "conv_w"]
    small_w = [ada_b, norm1_g, q_norm_g, k_norm_g, sinks, a_log, dt_bias, dn_norm_g, norm2_g, conv_w]
    small_g = [grad_ada_b, g_n1, g_qg, g_kg, g_sk, g_al, g_dt, g_dn, g_n2, grad_conv]
    small_m = [m_ada_b, m_norm1_g, m_q_norm_g, m_k_norm_g, m_sinks, m_a_log, m_dt_bias, m_dn_norm_g, m_norm2_g, m_conv_w]
    small_v = [v_ada_b, v_norm1_g, v_q_norm_g, v_k_norm_g, v_sinks, v_a_log, v_dt_bias, v_dn_norm_g, v_norm2_g, v_conv_w]
    cat = lambda arrs: jnp.concatenate([a.reshape(1, -1) for a in arrs], axis=1)
    res = _adamw(cat(small_w), cat(small_g), cat(small_m), cat(small_v), "adamw_small")
    po = np.cumsum([0] + [int(np.prod(w.shape)) for w in small_w])
    for i, nm in enumerate(small_names):
        upd[nm] = tuple(r[:, po[i]:po[i + 1]].reshape(small_w[i].shape) for r in res)
        grads[nm] = small_g[i].reshape(small_w[i].shape)

    order = ["ada_w", "ada_b", "norm1_g", "w_in", "conv_w", "q_norm_g", "k_norm_g", "sinks", "a_log", "dt_bias",
             "dn_norm_g", "w_branch", "w_out", "norm2_g", "w_gate_up", "w_down"]
    return (loss, grad_x, *[grads[n] for n in order], *[upd[n][0] for n in order],
            *[upd[n][1] for n in order], *[upd[n][2] for n in order])
```

```python
import functools

import numpy as np
import jax
import jax.numpy as jnp
from jax import lax
from jax.experimental import pallas as pl
from jax.experimental.pallas import tpu as pltpu

F32 = jnp.float32
BF16 = jnp.bfloat16
HI = lax.Precision.HIGHEST

N_DEV = 8
D = 1024
HQ, HKV, HD = 8, 2, 64
GRP = HQ // HKV
BLK = 128
ROT = HD // 4
THETA = 500000.0
QW, KVW = HQ * HD, HKV * HD
DH, DK = 4, 128
CH = 64
DNW = DH * DK
CONV = 4
CONVW = 3 * DNW
FFN = 2816
EPS = 1e-6
IN_W = QW + 2 * KVW + CONVW + 2 * DH + DNW + 2 * D

LR, B1, B2, AEPS, WD, STEP = 0.001, 0.9, 0.999, 1e-08, 0.01, 10

VMEM_LIMIT = 56 * 1024 * 1024
MESH = pl.DeviceIdType.MESH


def _cparams(sem=None, vmem=VMEM_LIMIT):
    return pltpu.CompilerParams(dimension_semantics=sem, vmem_limit_bytes=vmem)


def _full(shape):
    n = len(shape)
    return pl.BlockSpec(shape, lambda *_: (0,) * n)


def _resident(shape):
    n = len(shape)
    return pl.BlockSpec(shape, lambda *_: (0,) * n, pipeline_mode=pl.Buffered(1))


def _rows(tm, w):
    return pl.BlockSpec((None, tm, w), lambda b, i: (b, i, 0))


def _stacked(n, tm, w):
    return pl.BlockSpec((None, n, tm, w), lambda b, i: (b, 0, i, 0))


def _perb(r, w):
    return pl.BlockSpec((None, r, w), lambda b, i: (b, 0, 0))


def _dot(a, b):
    return jnp.dot(a.astype(BF16), b.astype(BF16), preferred_element_type=F32)


def _dot_nt(a, b):
    return lax.dot_general(a.astype(BF16), b.astype(BF16), (((1,), (1,)), ((), ())), preferred_element_type=F32)


def _dot_tn(a, b):
    return lax.dot_general(a.astype(BF16), b.astype(BF16), (((0,), (0,)), ((), ())), preferred_element_type=F32)


def _dot_hi(a, b):
    return jnp.dot(a, b, preferred_element_type=F32, precision=HI)


def _sigmoid(x):
    return jax.nn.sigmoid(x)


def _silu(x):
    return x * jax.nn.sigmoid(x)


def _rms_mod(x, g, scale, shift):
    r = lax.rsqrt(jnp.mean(x * x, axis=-1, keepdims=True) + EPS)
    return (x * r * g) * (1.0 + scale) + shift


def _tile(S, rows=256):
    return min(rows, S)


def _peer(x, y, c, k):
    px = 1 - x if (k >> 2) & 1 else x
    py = 1 - y if (k >> 1) & 1 else y
    pc = 1 - c if k & 1 else c
    return px, py, pc


def _all_gather_small(v, name):
    r, n = v.shape

    def body(v_ref, out_ref, send_sems, recv_sems, local_sem):
        x, y, c = lax.axis_index("x"), lax.axis_index("y"), lax.axis_index("c")
        me = 4 * x + 2 * y + c
        mine = pltpu.make_async_copy(v_ref, out_ref.at[me], local_sem)
        mine.start()
        sends = []
        for k in range(1, N_DEV):
            cp = pltpu.make_async_remote_copy(
                src_ref=v_ref, dst_ref=out_ref.at[me], send_sem=send_sems.at[k - 1], recv_sem=recv_sems.at[k - 1],
                device_id=_peer(x, y, c, k), device_id_type=MESH)
            cp.start()
            sends.append(cp)
        for k in range(1, N_DEV):
            px, py, pc = _peer(x, y, c, k)
            pltpu.make_async_remote_copy(
                src_ref=v_ref, dst_ref=out_ref.at[4 * px + 2 * py + pc], send_sem=send_sems.at[k - 1],
                recv_sem=recv_sems.at[k - 1], device_id=(px, py, pc), device_id_type=MESH).wait_recv()
        for cp in sends:
            cp.wait_send()
        mine.wait()

    return pl.pallas_call(
        body, name=name,
        out_shape=jax.ShapeDtypeStruct((N_DEV, r, n), v.dtype),
        in_specs=[pl.BlockSpec(memory_space=pltpu.VMEM)],
        out_specs=pl.BlockSpec(memory_space=pltpu.VMEM),
        scratch_shapes=[pltpu.SemaphoreType.DMA((N_DEV - 1,)), pltpu.SemaphoreType.DMA((N_DEV - 1,)), pltpu.SemaphoreType.DMA],
    )(v)


def _all_gather_big(vs, name, after=()):
    na, nf = len(vs), len(after)

    def body(*refs):
        v_refs, out_refs = refs[:na], refs[na + nf:2 * na + nf]
        send_sems, recv_sems, local_sems = refs[2 * na + nf:]
        x, y, c = lax.axis_index("x"), lax.axis_index("y"), lax.axis_index("c")
        me, sibling = (x, y, c), (x, y, 1 - c)
        chips = [(1 - x, y), (x, 1 - y), (1 - x, 1 - y)]

        def rows(a, px, py, pc):
            return out_refs[a].at[4 * px + 2 * py + pc]

        def copy(a, k, block, to, src=None):
            return pltpu.make_async_remote_copy(
                src_ref=rows(a, *block) if src is None else src, dst_ref=rows(a, *block),
                send_sem=send_sems.at[7 * a + k], recv_sem=recv_sems.at[7 * a + k], device_id=to, device_id_type=MESH)

        mine = [pltpu.make_async_copy(v_refs[a], rows(a, *me), local_sems.at[a]) for a in range(na)]
        for cp in mine:
            cp.start()
        first = []
        for a in range(na):
            first.append(copy(a, 0, me, sibling, src=v_refs[a]))
            first += [copy(a, 1 + j, me, (*chip, c), src=v_refs[a]) for j, chip in enumerate(chips)]
        for cp in first:
            cp.start()
        passed = []
        for j, chip in enumerate(chips):
            for a in range(na):
                copy(a, 1 + j, (*chip, c), me).wait_recv()
                forward = copy(a, 4 + j, (*chip, c), sibling)
                forward.start()
                passed.append(forward)
        for a in range(na):
            copy(a, 0, sibling, me).wait_recv()
            for j, chip in enumerate(chips):
                copy(a, 4 + j, (*chip, 1 - c), me).wait_recv()
        for cp in first + passed:
            cp.wait_send()
        for cp in mine:
            cp.wait()

    return pl.pallas_call(
        body, name=name,
        out_shape=[jax.ShapeDtypeStruct((N_DEV,) + v.shape, v.dtype) for v in vs],
        in_specs=[pl.BlockSpec(memory_space=pl.ANY)] * (na + nf),
        out_specs=[pl.BlockSpec(memory_space=pl.ANY)] * na,
        scratch_shapes=[pltpu.SemaphoreType.DMA((7 * na,)), pltpu.SemaphoreType.DMA((7 * na,)),
                        pltpu.SemaphoreType.DMA((na,))],
    )(*vs, *after)


_HBM = pl.BlockSpec(memory_space=pltpu.HBM)
_SEM = pl.BlockSpec(memory_space=pltpu.SEMAPHORE)
_EFFECT = pltpu.SideEffectType.DATAFLOW_SIDE_EFFECTING


def _place_own(block, me):
    land = lax.empty((N_DEV,) + block.shape, block.dtype)
    return lax.dynamic_update_slice(land, block[None], (me,) + (0,) * block.ndim)


def _copies_start(srcs, lands, scatter, after, name):
    na = len(srcs)
    afters = tuple(after) if isinstance(after, (tuple, list)) else (after,)

    def body(*refs):
        src_refs, land_refs = refs[:na], refs[na:2 * na]
        sems = refs[2 * na + len(afters):4 * na + len(afters)]
        token = refs[-1]
        x, y, c = lax.axis_index("x"), lax.axis_index("y"), lax.axis_index("c")
        me = 4 * x + 2 * y + c
        for a in range(na):
            for k in range(1, N_DEV):
                px, py, pc = _peer(x, y, c, k)
                src = src_refs[a].at[4 * px + 2 * py + pc] if scatter else src_refs[a]
                pltpu.make_async_remote_copy(
                    src_ref=src, dst_ref=land_refs[a].at[me], send_sem=sems[2 * a], recv_sem=sems[2 * a + 1],
                    device_id=(px, py, pc), device_id_type=MESH).start()
        token[...] = jnp.zeros_like(token)

    hbm = lambda t: pltpu.HBM(t.shape, t.dtype)
    out = pl.pallas_call(
        body, name=name,
        out_shape=tuple([pltpu.SemaphoreType.DMA(())] * (2 * na) + [hbm(t) for t in srcs] + [hbm(t) for t in lands]
                        + [jax.ShapeDtypeStruct((8, 128), F32)]),
        in_specs=[_HBM] * (2 * na) + [pl.BlockSpec(memory_space=pl.ANY)] * len(afters),
        out_specs=tuple([_SEM] * (2 * na) + [_HBM] * (2 * na) + [pl.BlockSpec(memory_space=pltpu.VMEM)]),
        input_output_aliases={i: 2 * na + i for i in range(2 * na)},
        compiler_params=pltpu.CompilerParams(has_side_effects=_EFFECT),
    )(*[pltpu.with_memory_space_constraint(t, pltpu.HBM) for t in list(srcs) + list(lands)], *afters)
    return out[:2 * na], out[2 * na:3 * na], out[3 * na:4 * na], out[-1]


def _exchange_start(gs, me, after, name):
    own = [lax.dynamic_index_in_dim(g, me, 0, keepdims=False) for g in gs]
    return _copies_start(gs, [_place_own(o, me) for o in own], True, after, name)


def _copies_wait(sems, srcs, lands, after, name):
    na = len(srcs)

    def body(*refs):
        land_refs = refs[na:2 * na]
        sem_refs = refs[2 * na:4 * na]
        x, y, c = lax.axis_index("x"), lax.axis_index("y"), lax.axis_index("c")
        for a in range(na):
            seven = land_refs[a].at[pl.ds(0, N_DEV - 1)]
            copy = pltpu.make_async_remote_copy(
                src_ref=seven, dst_ref=seven, send_sem=sem_refs[2 * a], recv_sem=sem_refs[2 * a + 1],
                device_id=(x, y, c), device_id_type=MESH)
            copy.wait_send()
            copy.wait_recv()

    hbm = lambda t: pltpu.HBM(t.shape, t.dtype)
    out = pl.pallas_call(
        body, name=name,
        out_shape=tuple([hbm(t) for t in srcs] + [hbm(t) for t in lands]),
        in_specs=[_HBM] * (2 * na) + [_SEM] * (2 * na) + [pl.BlockSpec(memory_space=pl.ANY)],
        out_specs=tuple([_HBM] * (2 * na)),
        input_output_aliases={i: i for i in range(2 * na)},
        compiler_params=pltpu.CompilerParams(has_side_effects=_EFFECT),
    )(*srcs, *lands, *sems, after)
    return out[na:]


def _adamw_math(w, g, m, v):
    m = B1 * m + (1.0 - B1) * g
    v = B2 * v + (1.0 - B2) * (g * g)
    m_hat = m / (1.0 - B1 ** STEP)
    v_hat = v / (1.0 - B2 ** STEP)
    return -LR * (m_hat / (jnp.sqrt(v_hat) + AEPS) + WD * w), m, v


def _sum_adamw(parts, w, m, v, name):
    _, r, n = parts.shape
    tr = 256 if r % 256 == 0 else r

    def body(p_ref, w_ref, m_ref, v_ref, g_ref, d_ref, nm_ref, nv_ref):
        g = p_ref[0].astype(F32)
        for dev in range(1, N_DEV):
            g = g + p_ref[dev].astype(F32)
        g_ref[...] = g
        d_ref[...], nm_ref[...], nv_ref[...] = _adamw_math(w_ref[...], g, m_ref[...], v_ref[...])

    rows = pl.BlockSpec((None, tr, n), lambda i: (0, i, 0))
    sd = jax.ShapeDtypeStruct((1, r, n), F32)
    return pl.pallas_call(
        body, name=name, grid=(r // tr,), out_shape=(sd, sd, sd, sd),
        in_specs=[pl.BlockSpec((N_DEV, tr, n), lambda i: (0, i, 0)), rows, rows, rows],
        out_specs=(rows, rows, rows, rows),
        compiler_params=_cparams(("parallel",)),
    )(parts, w, m, v)


def _ada_fwd(c_all, ada_w, ada_b_cols):
    nb, ncol = c_all.shape[0], ada_w.shape[1]

    def body(c_ref, w_ref, b_ref, mod_ref, cond_ref):
        cond = _silu(c_ref[...])
        cond_ref[...] = cond
        mod_ref[...] = _dot_hi(cond, w_ref[...]) + b_ref[...]

    return pl.pallas_call(
        body, name="ada_fwd",
        out_shape=(jax.ShapeDtypeStruct((nb, ncol), F32), jax.ShapeDtypeStruct((nb, D), F32)),
        compiler_params=_cparams(),
    )(c_all, ada_w, ada_b_cols)


def _ada_bwd(cond_all, dmod_all, dmod_cols, smalls):
    ncol, nsm = dmod_cols.shape[1], smalls.shape[1]

    def body(cond_ref, dm_ref, dmc_ref, sm_ref, gw_ref, gb_ref, gs_ref):
        gw_ref[...] = lax.dot_general(cond_ref[...], dmc_ref[...], (((0,), (0,)), ((), ())),
                                      preferred_element_type=F32, precision=HI)
        gb_ref[...] = jnp.sum(dm_ref[...], axis=0, keepdims=True)
        gs_ref[...] = jnp.sum(sm_ref[...], axis=0, keepdims=True)

    return pl.pallas_call(
        body, name="ada_bwd",
        out_shape=(jax.ShapeDtypeStruct((D, ncol), F32), jax.ShapeDtypeStruct((1, 6 * D), F32),
                   jax.ShapeDtypeStruct((1, nsm), F32)),
        compiler_params=_cparams(),
    )(cond_all, dmod_all, dmod_cols, smalls)


IN_CUTS = (0, QW, QW + 2 * KVW, QW + 2 * KVW + CONVW, QW + 2 * KVW + CONVW + 2 * DH,
           QW + 2 * KVW + CONVW + 2 * DH + DNW, QW + 2 * KVW + CONVW + 2 * DH + DNW + D, IN_W)
IN_WIDTHS = tuple(b - a for a, b in zip(IN_CUTS[:-1], IN_CUTS[1:]))
IN_SHARD = IN_W // N_DEV


def _inproj_fwd(x, mod, g1, w_t):
    B, S, _ = x.shape
    tm = _tile(S)

    def body(x_ref, mod_ref, g_ref, w_ref, h_ref, *o_refs):
        h = _rms_mod(x_ref[...], g_ref[...], mod_ref[1:2, :], mod_ref[0:1, :]).astype(BF16)
        h_ref[...] = h
        full = _dot_nt(h, w_ref[...])
        for o_ref, lo, hi in zip(o_refs, IN_CUTS[:-1], IN_CUTS[1:]):
            o_ref[...] = full[:, lo:hi]

    return pl.pallas_call(
        body, name="inproj_fwd", grid=(B, S // tm),
        out_shape=[jax.ShapeDtypeStruct((B, S, D), BF16)] + [jax.ShapeDtypeStruct((B, S, w), F32) for w in IN_WIDTHS],
        in_specs=[_rows(tm, D), _perb(6, D), _full((1, D)), _resident(w_t.shape)],
        out_specs=[_rows(tm, D)] + [_rows(tm, w) for w in IN_WIDTHS],
        compiler_params=_cparams(("parallel", "arbitrary")),
    )(x, mod, g1, w_t)


def _inproj_bwd(x, mod, g1, dx1, dps, w_t):
    B, S, _ = x.shape
    tm = _tile(S)
    n = len(dps)

    def body(x_ref, mod_ref, g_ref, dx1_ref, *refs):
        dp_refs, w_ref = refs[:n], refs[n]
        dblk_ref, gx_ref, dg_ref, dsc_ref, dsh_ref = refs[n + 1:]
        b, i = pl.program_id(0), pl.program_id(1)
        full = jnp.concatenate([r[...].astype(F32) for r in dp_refs], axis=1)
        for j in range(N_DEV):
            dblk_ref[j] = full[:, IN_SHARD * j:IN_SHARD * (j + 1)].astype(BF16)
        dh = jnp.dot(full.astype(BF16), w_ref[...], preferred_element_type=F32)
        _, vjp = jax.vjp(_rms_mod, x_ref[...], g_ref[...], mod_ref[1:2, :], mod_ref[0:1, :])
        dx, dg, dsc, dsh = vjp(dh)
        gx_ref[...] = dx1_ref[...] + dx

        @pl.when((b == 0) & (i == 0))
        def _():
            dg_ref[...] = jnp.zeros_like(dg_ref)

        @pl.when(i == 0)
        def _():
            dsc_ref[...] = jnp.zeros_like(dsc_ref)
            dsh_ref[...] = jnp.zeros_like(dsh_ref)

        dg_ref[...] += dg
        dsc_ref[...] += dsc
        dsh_ref[...] += dsh

    return pl.pallas_call(
        body, name="inproj_bwd", grid=(B, S // tm),
        out_shape=[jax.ShapeDtypeStruct((B, N_DEV, S, IN_SHARD), BF16), jax.ShapeDtypeStruct((B, S, D), F32),
                   jax.ShapeDtypeStruct((1, D), F32), jax.ShapeDtypeStruct((B, 1, D), F32),
                   jax.ShapeDtypeStruct((B, 1, D), F32)],
        in_specs=[_rows(tm, D), _perb(6, D), _full((1, D)), _rows(tm, D)]
                 + [_rows(tm, w) for w in IN_WIDTHS] + [_resident(w_t.shape)],
        out_specs=[pl.BlockSpec((None, N_DEV, tm, IN_SHARD), lambda b, i: (b, 0, i, 0)), _rows(tm, D),
                   _full((1, D)), _perb(1, D), _perb(1, D)],
        compiler_params=_cparams(("arbitrary", "arbitrary")),
    )(x, mod, g1, dx1, *dps, w_t)


def _wgrad(a, b, name, after=None):
    B, na, S, K = a.shape
    nb, N = b.shape[1], b.shape[3]
    G = max(na, nb)
    tm = min(4096, S)
    nt = S // tm
    last = B * nt - 1

    def body(a_ref, b_ref, *rest):
        o_ref, acc = rest[-2:]
        t = pl.program_id(1)

        @pl.when(t == 0)
        def _():
            acc[...] = jnp.zeros_like(acc)

        acc[...] += lax.dot_general(a_ref[...], b_ref[...], (((0,), (0,)), ((), ())), preferred_element_type=F32)

        @pl.when(t == last)
        def _():
            o_ref[...] = acc[...].astype(BF16)

    return pl.pallas_call(
        body, name=name, grid=(G, B * nt),
        out_shape=jax.ShapeDtypeStruct((G, K, N), BF16),
        in_specs=[pl.BlockSpec((None, None, tm, K), lambda g, t: (t // nt, g if na > 1 else 0, t % nt, 0)),
                  pl.BlockSpec((None, None, tm, N), lambda g, t: (t // nt, g if nb > 1 else 0, t % nt, 0))]
                 + ([] if after is None else [pl.BlockSpec(memory_space=pl.ANY)]),
        out_specs=pl.BlockSpec((None, K, N), lambda g, t: (g, 0, 0)),
        scratch_shapes=[pltpu.VMEM((K, N), F32)],
        compiler_params=_cparams(("parallel", "arbitrary")),
    )(*((a, b) if after is None else (a, b, after)))


LANES = 128


def _attn_consts():
    inv_freq = THETA ** (-jnp.arange(0, ROT, 2, dtype=F32) / ROT)
    head = jnp.concatenate([inv_freq, inv_freq, jnp.zeros((HD - ROT,), F32)])
    invf = jnp.tile(head, LANES // HD)[None, :]
    mean_of = lambda w: jnp.asarray(np.kron(np.eye(w // HD), np.full((HD, HD), 1.0 / HD)), BF16)
    return invf, mean_of(QW), mean_of(KVW)


def _rope_tables(pos, invf):
    B, S, _ = pos.shape
    tr = min(1024, S)

    def body(p_ref, f_ref, c_ref, s_ref):
        ang = p_ref[...].astype(F32) * f_ref[...]
        c_ref[...] = jnp.cos(ang)
        s_ref[...] = jnp.sin(ang)

    sd = jax.ShapeDtypeStruct((B, S, LANES), F32)
    return pl.pallas_call(
        body, name="rope_tables", grid=(B, S // tr), out_shape=[sd, sd],
        in_specs=[_rows(tr, 1), _full((1, LANES))], out_specs=[_rows(tr, LANES), _rows(tr, LANES)],
        compiler_params=_cparams(("parallel", "parallel")),
    )(pos, invf)


def _rope_expand(cos, sin, reps):
    lane = lax.broadcasted_iota(jnp.int32, cos.shape, 1) % HD
    sa = jnp.where((lane >= ROT // 2) & (lane < ROT), sin, 0.0)
    sb = jnp.where(lane < ROT // 2, -sin, 0.0)
    rep = lambda t: jnp.concatenate([t] * reps, axis=1) if reps > 1 else t
    return rep(cos), rep(sa), rep(sb)


@jax.custom_vjp
def _rope(t, cos, sa, sb):
    w = t.shape[1]
    return t * cos + pltpu.roll(t, ROT // 2, 1) * sa + pltpu.roll(t, w - ROT // 2, 1) * sb


def _rope_fwd(t, cos, sa, sb):
    return _rope(t, cos, sa, sb), (cos, sa, sb)


def _rope_bwd(res, d):
    cos, sa, sb = res
    w = d.shape[1]
    dt = d * cos + pltpu.roll(d * sa, w - ROT // 2, 1) + pltpu.roll(d * sb, ROT // 2, 1)
    return dt, jnp.zeros_like(cos), jnp.zeros_like(sa), jnp.zeros_like(sb)


_rope.defvjp(_rope_fwd, _rope_bwd)


def _head_norm(t, g, mean_of):
    hi, lo = _split(t * t)
    ms = jnp.dot(hi, mean_of, preferred_element_type=F32) + jnp.dot(lo, mean_of, preferred_element_type=F32)
    return t * lax.rsqrt(ms + EPS) * g


def _attn_block(q, kvp, kvc, qg, kg, sinks, tq, tk, mq, mk, valid):
    qn = _rope(_head_norm(q, jnp.concatenate([qg] * HQ, axis=1), mq), *tq) * (HD ** -0.5)
    kv = jnp.concatenate([kvp, kvc], axis=0)
    kn = _rope(_head_norm(kv[:, 0:KVW], jnp.concatenate([kg] * HKV, axis=1), mk), *tk)
    per_tile = LANES // HD
    vT = jnp.transpose(kv[:, KVW:2 * KVW])
    qT = [jnp.transpose(qn[:, LANES * t:LANES * (t + 1)]) for t in range(QW // LANES)]
    head_T = lambda h: qT[h // per_tile][HD * (h % per_tile):HD * (h % per_tile + 1), :]
    none = jnp.zeros((HD, GRP * BLK), F32)
    o_T = []
    for j in range(HKV):
        q4T = jnp.concatenate([head_T(GRP * j + i) for i in range(GRP)], axis=1)
        sT = _dot(kn, jnp.concatenate([q4T, none] if j == 0 else [none, q4T], axis=0))
        sT = jnp.where(valid, sT, -1e30)
        sink = jnp.concatenate([jnp.broadcast_to(sinks[:, GRP * j + i:GRP * j + i + 1], (1, BLK)) for i in range(GRP)], axis=1)
        m = lax.stop_gradient(jnp.maximum(jnp.max(sT, axis=0, keepdims=True), sink))
        pT = jnp.exp(sT - m)
        den = jnp.sum(pT, axis=0, keepdims=True) + jnp.exp(sink - m)
        oT = _dot(vT[HD * j:HD * (j + 1), :], pT) * (1.0 / den)
        o_T += [oT[:, BLK * i:BLK * (i + 1)] for i in range(GRP)]
    return jnp.concatenate([jnp.transpose(jnp.concatenate(o_T[per_tile * t:per_tile * (t + 1)], axis=0))
                            for t in range(QW // LANES)], axis=1)


def _attn_tables(cp_ref, cc_ref, sp_ref, sc_ref, n):
    tq = _rope_expand(cc_ref[...], sc_ref[...], QW // LANES)
    tk = _rope_expand(jnp.concatenate([cp_ref[...], cc_ref[...]], axis=0),
                      jnp.concatenate([sp_ref[...], sc_ref[...]], axis=0), KVW // LANES)
    qi = lax.broadcasted_iota(jnp.int32, (2 * BLK, GRP * BLK), 1) % BLK + BLK
    kj = lax.broadcasted_iota(jnp.int32, (2 * BLK, GRP * BLK), 0)
    dist = qi - kj
    valid = (dist >= 0) & (dist < BLK) & ((kj >= BLK) | (n > 0))
    return tq, tk, valid


def _attn_fwd(aq, akv, cos, sin, qg, kg, sinks, mq, mk):
    B, S, _ = aq.shape
    nb = S // BLK

    def body(q_ref, kvp_ref, kvc_ref, cp_ref, cc_ref, sp_ref, sc_ref, qg_ref, kg_ref, sk_ref, mq_ref, mk_ref, o_ref):
        tq, tk, valid = _attn_tables(cp_ref, cc_ref, sp_ref, sc_ref, pl.program_id(1))
        o_ref[...] = _attn_block(q_ref[...], kvp_ref[...], kvc_ref[...], qg_ref[...], kg_ref[...], sk_ref[...],
                                 tq, tk, mq_ref[...], mk_ref[...], valid)

    prev = lambda b, n: (b, jnp.maximum(n - 1, 0), 0)
    cur = lambda b, n: (b, n, 0)
    return pl.pallas_call(
        body, name="attn_fwd", grid=(B, nb),
        out_shape=jax.ShapeDtypeStruct((B, S, QW), F32),
        in_specs=[pl.BlockSpec((None, BLK, QW), cur), pl.BlockSpec((None, BLK, 2 * KVW), prev),
                  pl.BlockSpec((None, BLK, 2 * KVW), cur), pl.BlockSpec((None, BLK, LANES), prev),
                  pl.BlockSpec((None, BLK, LANES), cur), pl.BlockSpec((None, BLK, LANES), prev),
                  pl.BlockSpec((None, BLK, LANES), cur), _full((1, HD)), _full((1, HD)), _full((1, HQ)),
                  _full((QW, QW)), _full((KVW, KVW))],
        out_specs=pl.BlockSpec((None, BLK, QW), cur),
        compiler_params=_cparams(("parallel", "arbitrary")),
    )(aq, akv, akv, cos, cos, sin, sin, qg, kg, sinks, mq, mk)


def _attn_bwd(aq, akv, cos, sin, qg, kg, sinks, mq, mk, do):
    B, S, _ = aq.shape
    nb = S // BLK

    def body(q_ref, kvp_ref, kvc_ref, cp_ref, cc_ref, sp_ref, sc_ref, qg_ref, kg_ref, sk_ref, mq_ref, mk_ref, do_ref,
             dq_ref, dkv_ref, dqg_ref, dkg_ref, dsk_ref, carry):
        b, i = pl.program_id(0), pl.program_id(1)
        tq, tk, valid = _attn_tables(cp_ref, cc_ref, sp_ref, sc_ref, nb - 1 - i)
        fn = functools.partial(_attn_block, tq=tq, tk=tk, mq=mq_ref[...], mk=mk_ref[...], valid=valid)
        _, vjp = jax.vjp(fn, q_ref[...], kvp_ref[...], kvc_ref[...], qg_ref[...], kg_ref[...], sk_ref[...])
        dq, dkvp, dkvc, dqg, dkg, dsk = vjp(do_ref[...])

        @pl.when(i == 0)
        def _():
            carry[...] = jnp.zeros_like(carry)

        @pl.when((b == 0) & (i == 0))
        def _():
            dqg_ref[...] = jnp.zeros_like(dqg_ref)
            dkg_ref[...] = jnp.zeros_like(dkg_ref)
            dsk_ref[...] = jnp.zeros_like(dsk_ref)

        dq_ref[...] = dq.astype(BF16)
        dkv_ref[...] = (dkvc + carry[...]).astype(BF16)
        carry[...] = dkvp
        dqg_ref[...] += dqg
        dkg_ref[...] += dkg
        dsk_ref[...] += dsk

    prev = lambda b, i: (b, jnp.maximum(nb - 2 - i, 0), 0)
    cur = lambda b, i: (b, nb - 1 - i, 0)
    return pl.pallas_call(
        body, name="attn_bwd", grid=(B, nb),
        out_shape=[jax.ShapeDtypeStruct((B, S, QW), BF16), jax.ShapeDtypeStruct((B, S, 2 * KVW), BF16),
                   jax.ShapeDtypeStruct((1, HD), F32), jax.ShapeDtypeStruct((1, HD), F32),
                   jax.ShapeDtypeStruct((1, HQ), F32)],
        in_specs=[pl.BlockSpec((None, BLK, QW), cur), pl.BlockSpec((None, BLK, 2 * KVW), prev),
                  pl.BlockSpec((None, BLK, 2 * KVW), cur), pl.BlockSpec((None, BLK, LANES), prev),
                  pl.BlockSpec((None, BLK, LANES), cur), pl.BlockSpec((None, BLK, LANES), prev),
                  pl.BlockSpec((None, BLK, LANES), cur), _full((1, HD)), _full((1, HD)), _full((1, HQ)),
                  _full((QW, QW)), _full((KVW, KVW)), pl.BlockSpec((None, BLK, QW), cur)],
        out_specs=[pl.BlockSpec((None, BLK, QW), cur), pl.BlockSpec((None, BLK, 2 * KVW), cur),
                   _full((1, HD)), _full((1, HD)), _full((1, HQ))],
        scratch_shapes=[pltpu.VMEM((BLK, 2 * KVW), F32)],
        compiler_params=_cparams(("arbitrary", "arbitrary")),
    )(aq, akv, akv, cos, cos, sin, sin, qg, kg, sinks, mq, mk, do)


def _conv_taps(xe, w, rows):
    y = None
    for j in range(CONV):
        sh = pltpu.roll(xe, CONV - 1 - j, 0)[8:8 + rows, :] if j < CONV - 1 else xe[8:8 + rows, :]
        y = sh * w[j:j + 1, :] if y is None else y + sh * w[j:j + 1, :]
    return y


def _softplus(x):
    return jnp.maximum(x, 0.0) + jnp.log1p(jnp.exp(-jnp.abs(x)))


_BMM = (((2,), (1,)), ((0,), (0,)))
_BMM_NT = (((2,), (2,)), ((0,), (0,)))
_BMM_TN = (((1,), (1,)), ((0,), (0,)))


def _bmm(a, b, dims=_BMM):
    return lax.dot_general(a.astype(BF16), b.astype(BF16), dims, preferred_element_type=F32)


def _split(a):
    hi = a.astype(BF16)
    return hi, (a - hi.astype(F32)).astype(BF16)


def _bmm3(a, b, dims=_BMM):
    ah, al = _split(a)
    bh, bl = _split(b)
    d = lambda p, q: lax.dot_general(p, q, dims, preferred_element_type=F32)
    return d(ah, bh) + (d(ah, bl) + d(al, bh))


TRI_BASE = 8


def _tri_inverse(L):
    ii = lax.broadcasted_iota(jnp.int32, (CH, CH), 0)
    jj = lax.broadcasted_iota(jnp.int32, (CH, CH), 1)
    same = lambda size: (ii // size) == (jj // size)
    diag = jnp.where(same(TRI_BASE), L, 0.0)
    X = (ii == jj).astype(F32) - diag
    P = diag
    n = 2
    while n < TRI_BASE:
        P = _bmm3(P, P)
        X = X + _bmm3(X, P)
        n *= 2
    size = TRI_BASE
    while size < CH:
        joint = jnp.where(same(2 * size) & jnp.logical_not(same(size)), L, 0.0)
        X = X - _bmm3(X, _bmm3(joint, X))
        size *= 2
    return X


@jax.custom_vjp
def _tri_inverse_known(L, T):
    return T


def _tri_inverse_known_fwd(L, T):
    return T, T


def _tri_inverse_known_bwd(T, dT):
    Tt = jnp.swapaxes(T, 1, 2)
    return -_bmm(Tt, _bmm(dT, Tt)), jnp.zeros_like(T)


_tri_inverse_known.defvjp(_tri_inverse_known_fwd, _tri_inverse_known_bwd)


def _triangle(n, upper):
    ii = lax.broadcasted_iota(jnp.int32, (n, CH, CH), 1)
    jj = lax.broadcasted_iota(jnp.int32, (n, CH, CH), 2)
    return ((ii <= jj) if upper else (ii >= jj)).astype(BF16)


@jax.custom_vjp
def _cumsum_rows(g):
    g0 = g.astype(BF16)
    r1 = g - g0.astype(F32)
    g1 = r1.astype(BF16)
    g2 = (r1 - g1.astype(F32)).astype(BF16)
    tri = _triangle(g.shape[0], False)
    d = lambda q: lax.dot_general(tri, q, _BMM, preferred_element_type=F32)
    return d(g0) + (d(g1) + d(g2))


def _cumsum_rows_fwd(g):
    return _cumsum_rows(g), None


def _cumsum_rows_bwd(_, dy):
    hi, lo = _split(dy)
    tri = _triangle(dy.shape[0], True)
    d = lambda q: lax.dot_general(tri, q, _BMM, preferred_element_type=F32)
    return (d(hi) + d(lo),)


_cumsum_rows.defvjp(_cumsum_rows_fwd, _cumsum_rows_bwd)


def _row_sums(t):
    n, r, w = t.shape
    hi, lo = _split(t.reshape(n * r, w))
    ones = jnp.ones((w, w), BF16)
    s = jnp.dot(hi, ones, preferred_element_type=F32) + jnp.dot(lo, ones, preferred_element_type=F32)
    return s.reshape(n, r, w)


def _dn_prep(t_known, qr, kr, v, a_raw, b_raw, a_log, dt_b):
    n = qr.shape[0]
    ii = lax.broadcasted_iota(jnp.int32, (n, CH, CH), 1)
    jj = lax.broadcasted_iota(jnp.int32, (n, CH, CH), 2)
    incl, strict = ii >= jj, ii > jj
    q = qr * lax.rsqrt(_row_sums(qr * qr) + EPS) * (DK ** -0.5)
    k = kr * lax.rsqrt(_row_sums(kr * kr) + EPS)
    beta = _sigmoid(b_raw)
    g = -jnp.exp(a_log) * _softplus(a_raw + dt_b)
    gcb = _cumsum_rows(jnp.broadcast_to(g, (n, CH, DK)))
    gc = gcb[:, :, 0:1]
    gc_row = jnp.swapaxes(gcb, 1, 2)[:, 0:1, 0:CH]
    decay = jnp.where(incl, jnp.exp(jnp.where(incl, gc - gc_row, 0.0)), 0.0)
    kb = k * beta
    L = jnp.where(strict, _bmm(kb, k, _BMM_NT) * decay, 0.0)
    T = _tri_inverse(L) if t_known is None else _tri_inverse_known(L, t_known)
    eg = jnp.exp(gc)
    u = _bmm(T, v * beta)
    w = _bmm(T, kb * eg)
    a_in = _bmm(q, k, _BMM_NT) * decay
    g_last = gc[:, CH - 1:CH, :]
    return u, w, q * eg, k * jnp.exp(g_last - gc), a_in, jnp.exp(g_last), T


def _dn_step(S0, u, w, qd, kd, a_in, cd):
    r = _bmm(jnp.concatenate([w, qd], axis=1), S0)
    v_new = u - r[:, 0:CH, :]
    o = r[:, CH:2 * CH, :] + _bmm(a_in, v_new)
    S1 = S0 * cd + _bmm(kd, v_new, _BMM_TN)
    return o, S1


def _dn_stack(cq, ba, al, dt, G):
    cols = [[] for _ in range(7)]
    for c in range(G):
        rows = slice(CH * c, CH * (c + 1))
        for h in range(DH):
            parts = (cq[rows, DK * h:DK * (h + 1)], cq[rows, DNW + DK * h:DNW + DK * (h + 1)],
                     cq[rows, 2 * DNW + DK * h:2 * DNW + DK * (h + 1)], ba[rows, DH + h:DH + h + 1],
                     ba[rows, h:h + 1], al[:, h:h + 1], dt[:, h:h + 1])
            for col, p in zip(cols, parts):
                col.append(p)
    return tuple(jnp.stack(col) for col in cols)


def _dn_group(S, want):
    g = want
    while (S // CH) % g:
        g //= 2
    return g


def _dn_prep_fwd(xin, conv_w, ba, a_log, dt_b):
    B, S, _ = xin.shape
    nc = S // CH
    G = _dn_group(S, 8)
    r8 = G * CH // 8

    def body(xp_ref, x_ref, cw_ref, ba_ref, al_ref, dt_ref, cq_ref, u_ref, w_ref, qd_ref, kd_ref, a_ref, t_ref, cd_ref):
        xp = jnp.where(pl.program_id(1) > 0, xp_ref[...], 0.0)
        cq = _silu(_conv_taps(jnp.concatenate([xp, x_ref[...]], axis=0), cw_ref[...], G * CH))
        cq_ref[...] = cq
        ops = _dn_stack(cq, ba_ref[...], al_ref[...], dt_ref[...], G)
        u, w, qd, kd, a_in, cd, T = _dn_prep(None, *ops)
        lane4 = lax.broadcasted_iota(jnp.int32, (1, DH), 1)
        for c in range(G):
            rows = slice(CH * c, CH * (c + 1))
            cdrow = jnp.zeros((1, DH), F32)
            for h in range(DH):
                n = DH * c + h
                lanes = slice(DK * h, DK * (h + 1))
                u_ref[rows, lanes] = u[n]
                w_ref[rows, lanes] = w[n]
                qd_ref[rows, lanes] = qd[n]
                kd_ref[rows, lanes] = kd[n]
                a_ref[rows, CH * h:CH * (h + 1)] = a_in[n]
                t_ref[rows, CH * h:CH * (h + 1)] = T[n]
                cdrow = cdrow + jnp.where(lane4 == h, cd[n], 0.0)
            cd_ref[c] = cdrow

    wide = jax.ShapeDtypeStruct((B, S, DNW), F32)
    sq = jax.ShapeDtypeStruct((B, S, DH * CH), F32)
    return pl.pallas_call(
        body, name="dn_prep_fwd", grid=(B, nc // G),
        out_shape=[jax.ShapeDtypeStruct((B, S, CONVW), F32), wide, wide, wide, wide, sq, sq,
                   jax.ShapeDtypeStruct((B, nc, 1, DH), F32)],
        in_specs=[pl.BlockSpec((None, 8, CONVW), lambda b, i: (b, jnp.maximum(i * r8 - 1, 0), 0)),
                  _rows(G * CH, CONVW), _full((CONV, CONVW)), _rows(G * CH, 2 * DH), _full((1, DH)), _full((1, DH))],
        out_specs=[_rows(G * CH, CONVW)] + [_rows(G * CH, DNW)] * 4 + [_rows(G * CH, DH * CH)] * 2
                  + [pl.BlockSpec((None, G, 1, DH), lambda b, i: (b, i, 0, 0))],
        compiler_params=_cparams(("parallel", "arbitrary")),
    )(xin, xin, conv_w, ba, a_log, dt_b)


def _dn_seq_specs(B, steps, gs, rev):
    at = (lambda i: steps - 1 - i) if rev else (lambda i: i)
    wide = pl.BlockSpec((B, gs * CH, DNW), lambda i: (0, at(i), 0))
    a_spec = pl.BlockSpec((B, gs * CH, DH * CH), lambda i: (0, at(i), 0))
    cd_spec = pl.BlockSpec((B, gs, 1, DH), lambda i: (0, at(i), 0, 0))
    st_spec = pl.BlockSpec((B, gs, DH, DK, DK), lambda i: (0, at(i), 0, 0, 0))
    return wide, a_spec, cd_spec, st_spec


def _dn_step_operands(B, c, u_ref, w_ref, qd_ref, kd_ref, a_ref, cd_ref):
    pairs = [(b, h) for b in range(B) for h in range(DH)]
    rows = slice(CH * c, CH * (c + 1))
    wide = lambda ref: jnp.stack([ref[b, rows, DK * h:DK * (h + 1)] for b, h in pairs])
    a_in = jnp.stack([a_ref[b, rows, CH * h:CH * (h + 1)] for b, h in pairs])
    cd = jnp.stack([cd_ref[b, c, :, h:h + 1] for b, h in pairs])
    return wide(u_ref), wide(w_ref), wide(qd_ref), wide(kd_ref), a_in, cd


def _dn_seq_fwd(u, w, qd, kd, a_in, cd):
    B, S, _ = u.shape
    nc = S // CH
    gs = _dn_group(S, 8)

    def body(u_ref, w_ref, qd_ref, kd_ref, a_ref, cd_ref, o_ref, st_ref, state):
        @pl.when(pl.program_id(0) == 0)
        def _():
            state[...] = jnp.zeros_like(state)

        S0 = state[...]
        for c in range(gs):
            for b in range(B):
                st_ref[b, c] = S0[DH * b:DH * (b + 1)]
            o, S0 = _dn_step(S0, *_dn_step_operands(B, c, u_ref, w_ref, qd_ref, kd_ref, a_ref, cd_ref))
            for b in range(B):
                for h in range(DH):
                    o_ref[b, CH * c:CH * (c + 1), DK * h:DK * (h + 1)] = o[DH * b + h]
        state[...] = S0

    wide, a_spec, cd_spec, st_spec = _dn_seq_specs(B, nc // gs, gs, False)
    return pl.pallas_call(
        body, name="dn_seq_fwd", grid=(nc // gs,),
        out_shape=[jax.ShapeDtypeStruct((B, S, DNW), F32), jax.ShapeDtypeStruct((B, nc, DH, DK, DK), F32)],
        in_specs=[wide, wide, wide, wide, a_spec, cd_spec],
        out_specs=[wide, st_spec],
        scratch_shapes=[pltpu.VMEM((B * DH, DK, DK), F32)],
        compiler_params=_cparams(("arbitrary",)),
    )(u, w, qd, kd, a_in, cd)


def _dn_seq_bwd(u, w, qd, kd, a_in, cd, states, do):
    B, S, _ = u.shape
    nc = S // CH
    gs = _dn_group(S, 8)

    def body(u_ref, w_ref, qd_ref, kd_ref, a_ref, cd_ref, st_ref, do_ref,
             du_ref, dw_ref, dqd_ref, dkd_ref, da_ref, dcd_ref, dstate):
        @pl.when(pl.program_id(0) == 0)
        def _():
            dstate[...] = jnp.zeros_like(dstate)

        lane4 = lax.broadcasted_iota(jnp.int32, (1, DH), 1)
        dS = dstate[...]
        for c in reversed(range(gs)):
            rows = slice(CH * c, CH * (c + 1))
            S0 = jnp.concatenate([st_ref[b, c] for b in range(B)], axis=0)
            do = jnp.stack([do_ref[b, rows, DK * h:DK * (h + 1)] for b in range(B) for h in range(DH)])
            _, vjp = jax.vjp(_dn_step, S0, *_dn_step_operands(B, c, u_ref, w_ref, qd_ref, kd_ref, a_ref, cd_ref))
            dS, du, dw, dqd, dkd, da, dcd = vjp((do, dS))
            for b in range(B):
                dcdrow = jnp.zeros((1, DH), F32)
                for h in range(DH):
                    n = DH * b + h
                    lanes = slice(DK * h, DK * (h + 1))
                    du_ref[b, rows, lanes] = du[n]
                    dw_ref[b, rows, lanes] = dw[n]
                    dqd_ref[b, rows, lanes] = dqd[n]
                    dkd_ref[b, rows, lanes] = dkd[n]
                    da_ref[b, rows, CH * h:CH * (h + 1)] = da[n]
                    dcdrow = dcdrow + jnp.where(lane4 == h, dcd[n], 0.0)
                dcd_ref[b, c] = dcdrow
        dstate[...] = dS

    wide, a_spec, cd_spec, st_spec = _dn_seq_specs(B, nc // gs, gs, True)
    sd = jax.ShapeDtypeStruct((B, S, DNW), F32)
    return pl.pallas_call(
        body, name="dn_seq_bwd", grid=(nc // gs,),
        out_shape=[sd, sd, sd, sd, jax.ShapeDtypeStruct((B, S, DH * CH), F32), jax.ShapeDtypeStruct((B, nc, 1, DH), F32)],
        in_specs=[wide, wide, wide, wide, a_spec, cd_spec, st_spec, wide],
        out_specs=[wide, wide, wide, wide, a_spec, cd_spec],
        scratch_shapes=[pltpu.VMEM((B * DH, DK, DK), F32)],
        compiler_params=_cparams(("arbitrary",)),
    )(u, w, qd, kd, a_in, cd, states, do)


def _dn_prep_bwd(xin, conv_w, cq, ba, a_log, dt_b, t_inv, du, dw, dqd, dkd, da, dcd):
    B, S, _ = cq.shape
    nc = S // CH
    G = _dn_group(S, 8)
    R = G * CH
    nblk = nc // G
    r8 = R // 8

    def body(xp_ref, x_ref, cw_ref, cq_ref, ba_ref, al_ref, dt_ref, t_ref, du_ref, dw_ref, dqd_ref, dkd_ref, da_ref, dcd_ref,
             dx_ref, dcw_ref, dba_ref, dal_ref, ddt_ref, carry):
        i = pl.program_id(1)

        @pl.when((pl.program_id(0) == 0) & (i == 0))
        def _():
            dal_ref[...] = jnp.zeros_like(dal_ref)
            ddt_ref[...] = jnp.zeros_like(ddt_ref)
            dcw_ref[...] = jnp.zeros_like(dcw_ref)

        @pl.when(i == 0)
        def _():
            carry[...] = jnp.zeros_like(carry)

        pairs = [(c, h) for c in range(G) for h in range(DH)]
        rows = lambda c: slice(CH * c, CH * (c + 1))
        wide = lambda ref: jnp.stack([ref[rows(c), DK * h:DK * (h + 1)] for c, h in pairs])
        square = lambda ref: jnp.stack([ref[rows(c), CH * h:CH * (h + 1)] for c, h in pairs])
        ops = _dn_stack(cq_ref[...], ba_ref[...], al_ref[...], dt_ref[...], G)
        cots = (wide(du_ref), wide(dw_ref), wide(dqd_ref), wide(dkd_ref), square(da_ref),
                jnp.stack([dcd_ref[c][:, h:h + 1] for c, h in pairs]), jnp.zeros((len(pairs), CH, CH), F32))
        _, vjp = jax.vjp(functools.partial(_dn_prep, square(t_ref)), *ops)
        dq, dk, dv, dar, dbr, dl, dd = vjp(cots)
        lane8 = lax.broadcasted_iota(jnp.int32, (CH, 2 * DH), 1)
        lane4 = lax.broadcasted_iota(jnp.int32, (1, DH), 1)
        dal = jnp.zeros((1, DH), F32)
        ddt = jnp.zeros((1, DH), F32)
        for c in range(G):
            dba = jnp.zeros((CH, 2 * DH), F32)
            for h in range(DH):
                n = DH * c + h
                dba = dba + jnp.where(lane8 == h, dbr[n], 0.0) + jnp.where(lane8 == DH + h, dar[n], 0.0)
                dal = dal + jnp.where(lane4 == h, dl[n], 0.0)
                ddt = ddt + jnp.where(lane4 == h, dd[n], 0.0)
            dba_ref[rows(c), :] = dba.astype(BF16)
        dal_ref[...] += dal
        ddt_ref[...] += ddt

        dcq = jnp.concatenate([jnp.concatenate([t[DH * c + h] for t in (dq, dk, dv) for h in range(DH)], axis=1)
                               for c in range(G)], axis=0)
        w = cw_ref[...]
        xp = jnp.where(i < nblk - 1, xp_ref[...], 0.0)
        xe = jnp.concatenate([xp, x_ref[...]], axis=0)
        taps = [(pltpu.roll(xe, CONV - 1 - j, 0) if j < CONV - 1 else xe)[8:8 + R, :] for j in range(CONV)]
        pre = sum(t * w[j:j + 1, :] for j, t in enumerate(taps))
        sg = _sigmoid(pre)
        dpre = dcq * (sg * (1.0 + pre * (1.0 - sg)))
        ext = jnp.concatenate([dpre, carry[...]], axis=0)
        dx = dpre * w[CONV - 1:CONV, :]
        for j in range(CONV - 1):
            dx = dx + pltpu.roll(ext, R + 8 - (CONV - 1 - j), 0)[0:R, :] * w[j:j + 1, :]
        dx_ref[...] = dx.astype(BF16)
        carry[...] = dpre[0:8, :]
        lane_row = lax.broadcasted_iota(jnp.int32, (CONV, CONVW), 0)
        dcw = jnp.zeros((CONV, CONVW), F32)
        for j in range(CONV):
            dcw = dcw + jnp.where(lane_row == j, jnp.sum(taps[j] * dpre, axis=0, keepdims=True), 0.0)
        dcw_ref[...] += dcw

    rev = lambda w: pl.BlockSpec((None, R, w), lambda b, i: (b, nblk - 1 - i, 0))
    return pl.pallas_call(
        body, name="dn_prep_bwd", grid=(B, nblk),
        out_shape=[jax.ShapeDtypeStruct((B, S, CONVW), BF16), jax.ShapeDtypeStruct((CONV, CONVW), F32),
                   jax.ShapeDtypeStruct((B, S, 2 * DH), BF16), jax.ShapeDtypeStruct((1, DH), F32),
                   jax.ShapeDtypeStruct((1, DH), F32)],
        in_specs=[pl.BlockSpec((None, 8, CONVW), lambda b, i: (b, jnp.maximum((nblk - 1 - i) * r8 - 1, 0), 0)),
                  rev(CONVW), _full((CONV, CONVW)), rev(CONVW), rev(2 * DH), _full((1, DH)), _full((1, DH)), rev(DH * CH)]
                 + [rev(DNW)] * 4 + [rev(DH * CH), pl.BlockSpec((None, G, 1, DH), lambda b, i: (b, nblk - 1 - i, 0, 0))],
        out_specs=[rev(CONVW), _full((CONV, CONVW)), rev(2 * DH), _full((1, DH)), _full((1, DH))],
        scratch_shapes=[pltpu.VMEM((8, CONVW), F32)],
        compiler_params=_cparams(("arbitrary", "arbitrary")),
    )(xin, xin, conv_w, cq, ba, a_log, dt_b, t_inv, du, dw, dqd, dkd, da, dcd)


def _gated_norm(o, z, g):
    outs = []
    for h in range(DH):
        t = o[:, DK * h:DK * (h + 1)]
        r = lax.rsqrt(jnp.mean(t * t, axis=-1, keepdims=True) + EPS)
        outs.append(t * r * g * _silu(z[:, DK * h:DK * (h + 1)]))
    return jnp.concatenate(outs, axis=1)


def _mix_fwd(x, o_attn, o_dn, z, ga, gd, mod, dn_g, w_branch, w_out):
    B, S, _ = x.shape
    tm = _tile(S, 512)

    def body(x_ref, oa_ref, od_ref, z_ref, ga_ref, gd_ref, mod_ref, g_ref, wb_ref, wo_ref,
             x1_ref, mix_ref, mg_ref, ob_ref):
        oa = oa_ref[...].astype(BF16)
        od = _gated_norm(od_ref[...], z_ref[...], g_ref[...]).astype(BF16)
        ob_ref[0] = oa
        ob_ref[1] = od
        ya = jnp.dot(oa, wb_ref[0:QW, :], preferred_element_type=F32)
        yd = jnp.dot(od, wb_ref[QW:QW + DNW, :], preferred_element_type=F32)
        merged = (_sigmoid(ga_ref[...]) * ya + _sigmoid(gd_ref[...]) * yd).astype(BF16)
        mg_ref[...] = merged
        mix = jnp.dot(merged, wo_ref[...], preferred_element_type=F32)
        mix_ref[...] = mix
        x1_ref[...] = x_ref[...] + mod_ref[2:3, :] * mix

    return pl.pallas_call(
        body, name="mix_fwd", grid=(B, S // tm),
        out_shape=[jax.ShapeDtypeStruct((B, S, D), F32), jax.ShapeDtypeStruct((B, S, D), F32),
                   jax.ShapeDtypeStruct((B, S, D), BF16), jax.ShapeDtypeStruct((B, 2, S, QW), BF16)],
        in_specs=[_rows(tm, D), _rows(tm, QW), _rows(tm, DNW), _rows(tm, DNW), _rows(tm, D), _rows(tm, D),
                  _perb(6, D), _full((1, DK)), _resident(w_branch.shape), _resident(w_out.shape)],
        out_specs=[_rows(tm, D), _rows(tm, D), _rows(tm, D), _stacked(2, tm, QW)],
        compiler_params=_cparams(("parallel", "arbitrary")),
    )(x, o_attn, o_dn, z, ga, gd, mod, dn_g, w_branch, w_out)


def _mix_bwd(dx1, mix, o_attn, o_dn, z, ga, gd, mod, dn_g, w_branch, w_out):
    B, S, _ = dx1.shape
    tm = _tile(S, 512)

    def body(dx1_ref, mix_ref, oa_ref, od_ref, z_ref, ga_ref, gd_ref, mod_ref, g_ref, wb_ref, wo_ref,
             dmix_ref, dyo_ref, dga_ref, dgd_ref, dz_ref, doa_ref, dod_ref, dgate_ref, dg_ref):
        b, i = pl.program_id(0), pl.program_id(1)
        dx1 = dx1_ref[...]
        dmix = (dx1 * mod_ref[2:3, :]).astype(BF16)
        dmix_ref[...] = dmix
        dgate = jnp.sum(dx1 * mix_ref[...], axis=0, keepdims=True)
        dmerged = _dot_nt(dmix, wo_ref[...])
        odn, gn_vjp = jax.vjp(_gated_norm, od_ref[...], z_ref[...], g_ref[...])
        ya = _dot(oa_ref[...], wb_ref[0:QW, :])
        yd = _dot(odn, wb_ref[QW:QW + DNW, :])
        sa, sd = _sigmoid(ga_ref[...]), _sigmoid(gd_ref[...])
        dya = (dmerged * sa).astype(BF16)
        dyd = (dmerged * sd).astype(BF16)
        dyo_ref[0] = dya
        dyo_ref[1] = dyd
        dga_ref[...] = (dmerged * ya * sa * (1.0 - sa)).astype(BF16)
        dgd_ref[...] = (dmerged * yd * sd * (1.0 - sd)).astype(BF16)
        doa_ref[...] = _dot_nt(dya, wb_ref[0:QW, :])
        dodn = _dot_nt(dyd, wb_ref[QW:QW + DNW, :])
        dod, dz, dg = gn_vjp(dodn)
        dod_ref[...] = dod
        dz_ref[...] = dz.astype(BF16)

        @pl.when(i == 0)
        def _():
            dgate_ref[...] = jnp.zeros_like(dgate_ref)

        @pl.when((b == 0) & (i == 0))
        def _():
            dg_ref[...] = jnp.zeros_like(dg_ref)

        dgate_ref[...] += dgate
        dg_ref[...] += dg

    return pl.pallas_call(
        body, name="mix_bwd", grid=(B, S // tm),
        out_shape=[jax.ShapeDtypeStruct((B, S, D), BF16), jax.ShapeDtypeStruct((B, 2, S, D), BF16),
                   jax.ShapeDtypeStruct((B, S, D), BF16), jax.ShapeDtypeStruct((B, S, D), BF16),
                   jax.ShapeDtypeStruct((B, S, DNW), BF16),
                   jax.ShapeDtypeStruct((B, S, QW), F32), jax.ShapeDtypeStruct((B, S, DNW), F32),
                   jax.ShapeDtypeStruct((B, 1, D), F32), jax.ShapeDtypeStruct((1, DK), F32)],
        in_specs=[_rows(tm, D), _rows(tm, D), _rows(tm, QW), _rows(tm, DNW), _rows(tm, DNW), _rows(tm, D),
                  _rows(tm, D), _perb(6, D), _full((1, DK)), _resident(w_branch.shape), _resident(w_out.shape)],
        out_specs=[_rows(tm, D), _stacked(2, tm, D), _rows(tm, D), _rows(tm, D), _rows(tm, DNW),
                   _rows(tm, QW), _rows(tm, DNW), _perb(1, D), _full((1, DK))],
        compiler_params=_cparams(("arbitrary", "arbitrary")),
    )(dx1, mix, o_attn, o_dn, z, ga, gd, mod, dn_g, w_branch, w_out)


GU_SHARD = 2 * FFN // N_DEV
GU_HALF = N_DEV // 2


def _ffn1_fwd(x1, mod, g2, w_gu):
    B, S, _ = x1.shape
    tm = _tile(S)

    def body(x_ref, mod_ref, g_ref, w_ref, h_ref, dgate_ref, dup_ref, act_ref):
        h = _rms_mod(x_ref[...], g_ref[...], mod_ref[4:5, :], mod_ref[3:4, :]).astype(BF16)
        h_ref[...] = h
        for j in range(GU_HALF):
            gate = _dot_nt(h, w_ref[j])
            up = _dot_nt(h, w_ref[GU_HALF + j])
            sg = _sigmoid(gate)
            silu = gate * sg
            dgate_ref[j] = up * (sg * (1.0 + gate * (1.0 - sg)))
            dup_ref[j] = silu
            act_ref[j] = (silu * up).astype(BF16)

    blk = lambda dt: jax.ShapeDtypeStruct((B, GU_HALF, S, GU_SHARD), dt)
    return pl.pallas_call(
        body, name="ffn1_fwd", grid=(B, S // tm),
        out_shape=[jax.ShapeDtypeStruct((B, S, D), BF16), blk(F32), blk(F32), blk(BF16)],
        in_specs=[_rows(tm, D), _perb(6, D), _full((1, D)), _resident(w_gu.shape)],
        out_specs=[_rows(tm, D)] + [_stacked(GU_HALF, tm, GU_SHARD)] * 3,
        compiler_params=_cparams(("parallel", "arbitrary")),
    )(x1, mod, g2, w_gu)


def _ffn2_fwd(act, x1, target, mod, w_down):
    B, S, _ = x1.shape
    tm = _tile(S, 512)

    def body(a_ref, x_ref, t_ref, mod_ref, w_ref, dy_ref, loss_ref, dgate_ref):
        b, i = pl.program_id(0), pl.program_id(1)
        y = jnp.dot(a_ref[0], w_ref[0], preferred_element_type=F32)
        for j in range(1, GU_HALF):
            y = y + jnp.dot(a_ref[j], w_ref[j], preferred_element_type=F32)
        err = x_ref[...] + mod_ref[5:6, :] * y - t_ref[...]
        dy = err * (1.0 / D)
        dy_ref[...] = dy

        @pl.when((b == 0) & (i == 0))
        def _():
            loss_ref[...] = jnp.zeros_like(loss_ref)

        @pl.when(i == 0)
        def _():
            dgate_ref[...] = jnp.zeros_like(dgate_ref)

        loss_ref[...] += (0.5 / D) * jnp.sum(err * err)
        dgate_ref[...] += jnp.sum(dy * y, axis=0, keepdims=True)

    return pl.pallas_call(
        body, name="ffn2_fwd", grid=(B, S // tm),
        out_shape=[jax.ShapeDtypeStruct((B, S, D), F32), jax.ShapeDtypeStruct((1, 128), F32),
                   jax.ShapeDtypeStruct((B, 1, D), F32)],
        in_specs=[_stacked(GU_HALF, tm, GU_SHARD), _rows(tm, D), _rows(tm, D), _perb(6, D), _resident(w_down.shape)],
        out_specs=[_rows(tm, D), _full((1, 128)), _perb(1, D)],
        compiler_params=_cparams(("arbitrary", "arbitrary")),
    )(act, x1, target, mod, w_down)


def _ffn2_bwd(dy, act_dgate, act_dup, mod, w_down):
    B, S, _ = dy.shape
    tm = _tile(S)

    def body(dy_ref, dgate_ref, dup_ref, mod_ref, w_ref, dgu_ref, dyg_ref):
        dyg = (dy_ref[...] * mod_ref[5:6, :]).astype(BF16)
        dyg_ref[...] = dyg
        for j in range(GU_HALF):
            dact = _dot_nt(dyg, w_ref[j])
            dgu_ref[j] = (dact * dgate_ref[j]).astype(BF16)
            dgu_ref[GU_HALF + j] = (dact * dup_ref[j]).astype(BF16)

    return pl.pallas_call(
        body, name="ffn2_bwd", grid=(B, S // tm),
        out_shape=[jax.ShapeDtypeStruct((B, N_DEV, S, GU_SHARD), BF16), jax.ShapeDtypeStruct((B, S, D), BF16)],
        in_specs=[_rows(tm, D), _stacked(GU_HALF, tm, GU_SHARD), _stacked(GU_HALF, tm, GU_SHARD), _perb(6, D),
                  _resident(w_down.shape)],
        out_specs=[_stacked(N_DEV, tm, GU_SHARD), _rows(tm, D)],
        compiler_params=_cparams(("parallel", "arbitrary")),
    )(dy, act_dgate, act_dup, mod, w_down)


def _ffn1_bwd(dgu, x1, dy, mod, g2, w_gu):
    B, S, _ = x1.shape
    tm = _tile(S, 512)

    def body(dgu_ref, x_ref, dy_ref, mod_ref, g_ref, w_ref, dx1_ref, dg_ref, dsc_ref, dsh_ref):
        b, i = pl.program_id(0), pl.program_id(1)
        dh = jnp.dot(dgu_ref[0], w_ref[0], preferred_element_type=F32)
        for j in range(1, N_DEV):
            dh = dh + jnp.dot(dgu_ref[j], w_ref[j], preferred_element_type=F32)
        _, vjp = jax.vjp(_rms_mod, x_ref[...], g_ref[...], mod_ref[4:5, :], mod_ref[3:4, :])
        dx, dg, dsc, dsh = vjp(dh)
        dx1_ref[...] = dy_ref[...] + dx

        @pl.when((b == 0) & (i == 0))
        def _():
            dg_ref[...] = jnp.zeros_like(dg_ref)

        @pl.when(i == 0)
        def _():
            dsc_ref[...] = jnp.zeros_like(dsc_ref)
            dsh_ref[...] = jnp.zeros_like(dsh_ref)

        dg_ref[...] += dg
        dsc_ref[...] += dsc
        dsh_ref[...] += dsh

    return pl.pallas_call(
        body, name="ffn1_bwd", grid=(B, S // tm),
        out_shape=[jax.ShapeDtypeStruct((B, S, D), F32), jax.ShapeDtypeStruct((1, D), F32),
                   jax.ShapeDtypeStruct((B, 1, D), F32), jax.ShapeDtypeStruct((B, 1, D), F32)],
        in_specs=[_stacked(N_DEV, tm, GU_SHARD), _rows(tm, D), _rows(tm, D), _perb(6, D), _full((1, D)),
                  _resident(w_gu.shape)],
        out_specs=[_rows(tm, D), _full((1, D)), _perb(1, D), _perb(1, D)],
        compiler_params=_cparams(("arbitrary", "arbitrary")),
    )(dgu, x1, dy, mod, g2, w_gu)


def _adamw(w, g, m, v, name):
    def body(w_ref, g_ref, m_ref, v_ref, d_ref, nm_ref, nv_ref):
        d_ref[...], nm_ref[...], nv_ref[...] = _adamw_math(w_ref[...], g_ref[...], m_ref[...], v_ref[...])

    sd = jax.ShapeDtypeStruct(w.shape, F32)
    return pl.pallas_call(body, name=name, out_shape=(sd, sd, sd), compiler_params=_cparams())(w, g, m, v)


def kernel(x, c, positions, ada_w, ada_b, norm1_g, w_in, conv_w, q_norm_g, k_norm_g, sinks, a_log, dt_bias, dn_norm_g, w_branch, w_out, norm2_g, w_gate_up, w_down, loss_target, m_ada_w, m_ada_b, m_norm1_g, m_w_in, m_conv_w, m_q_norm_g, m_k_norm_g, m_sinks, m_a_log, m_dt_bias, m_dn_norm_g, m_w_branch, m_w_out, m_norm2_g, m_w_gate_up, m_w_down, v_ada_w, v_ada_b, v_norm1_g, v_w_in, v_conv_w, v_q_norm_g, v_k_norm_g, v_sinks, v_a_log, v_dt_bias, v_dn_norm_g, v_w_branch, v_w_out, v_norm2_g, v_w_gate_up, v_w_down):
    B, S, _ = x.shape
    me = 4 * lax.axis_index("x") + 2 * lax.axis_index("y") + lax.axis_index("c")

    tr = lambda t: jnp.swapaxes(t, 1, 2)
    shards = [w[0].astype(BF16) for w in (tr(w_in), w_branch, w_out, tr(w_gate_up), w_down)]

    c_all = _all_gather_small(c, "gather_c").reshape(N_DEV * B, D)
    ncol = 6 * D // N_DEV
    mod_cols, cond_all = _ada_fwd(c_all, ada_w[0], lax.dynamic_slice(ada_b, (0, me * ncol), (1, ncol)))
    mod_all = _all_gather_small(mod_cols, "gather_mod").transpose(1, 0, 2).reshape(N_DEV * B, 6 * D)
    mod = lax.dynamic_slice(mod_all, (me * B, 0), (B, 6 * D)).reshape(B, 6, D)
    conv2 = conv_w.reshape(CONV, CONVW // N_DEV)
    conv_all = _all_gather_small(conv2, "gather_conv").transpose(1, 0, 2).reshape(CONV, CONVW)

    (w_in_b,) = _all_gather_big(shards[:1], "gather_w_in", after=(mod, conv_all))
    w_sems, w_srcs, w_lands, w_token = _copies_start(shards[1:], [_place_own(s, me) for s in shards[1:]], False,
                                                    w_in_b, "gather_rest_start")

    w_in_t = w_in_b.reshape(IN_W, D)
    h1, aq, akv, dnx, ba, z, ga, gd = _inproj_fwd(x, mod, norm1_g + w_token[0, 0], w_in_t)
    invf, mean_q, mean_k = _attn_consts()
    rope_cos, rope_sin = _rope_tables(positions.reshape(B, S, 1), invf)
    o_attn = _attn_fwd(aq, akv, rope_cos, rope_sin, q_norm_g, k_norm_g, sinks, mean_q, mean_k)
    cq, dn_u, dn_w, dn_qd, dn_kd, dn_a, dn_t, dn_cd = _dn_prep_fwd(dnx, conv_all, ba, a_log, dt_bias)
    o_dn, states = _dn_seq_fwd(dn_u, dn_w, dn_qd, dn_kd, dn_a, dn_cd)
    w_branch_g, w_out_g, w_gu_b, w_down_g = _copies_wait(w_sems, w_srcs, w_lands, o_dn, "gather_wait_rest")
    w_branch_f = w_branch_g.reshape(D, D)
    w_out_f = w_out_g.reshape(D, D)
    w_down_b = w_down_g.reshape(GU_HALF, GU_SHARD, D)
    x1, mix, merged, ob = _mix_fwd(x, o_attn, o_dn, z, ga, gd, mod, dn_norm_g, w_branch_f, w_out_f)
    h2, act_dgate, act_dup, act = _ffn1_fwd(x1, mod, norm2_g, w_gu_b)
    dy, loss_part, d_gate2 = _ffn2_fwd(act, x1, loss_target, mod, w_down_b)
    loss = lax.psum(loss_part[0, 0], ("x", "y", "c"))

    one = lambda t: t.reshape(B, 1, S, t.shape[-1])
    dgu, dyg = _ffn2_bwd(dy, act_dgate, act_dup, mod, w_down_b)
    g_w_down = _wgrad(act, one(dyg), "wgrad_down")
    dx1, d_n2g, d_scale2, d_shift2 = _ffn1_bwd(dgu, x1, dy, mod, norm2_g, w_gu_b)
    g_w_gu = _wgrad(dgu, one(h2), "wgrad_gate_up")
    ffn = _exchange_start([g_w_gu, g_w_down.reshape(N_DEV, FFN // N_DEV, D)], me, dx1, "exchange_ffn_start")
    dmix, dyo, dga, dgd, dz, d_oa, d_od, d_gate1, d_dng = _mix_bwd(
        dx1, mix, o_attn, o_dn, z, ga, gd, mod, dn_norm_g + ffn[3][0, 0], w_branch_f, w_out_f)
    d_dn = _dn_seq_bwd(dn_u, dn_w, dn_qd, dn_kd, dn_a, dn_cd, states, d_od)
    ddnx, d_conv, dba, d_alog, d_dtb = _dn_prep_bwd(dnx, conv_all, cq, ba, a_log, dt_bias, dn_t, *d_dn)
    daq, dakv, d_qg, d_kg, d_sinks = _attn_bwd(aq, akv, rope_cos, rope_sin, q_norm_g, k_norm_g, sinks, mean_q, mean_k, d_oa)
    dps = [daq, dakv, ddnx, dba, dz, dga, dgd]
    dblk, grad_x, d_n1g, d_scale1, d_shift1 = _inproj_bwd(x, mod, norm1_g, dx1, dps, w_in_t)

    dmod = jnp.concatenate([d_shift1, d_scale1, d_gate1, d_shift2, d_scale2, d_gate2], axis=2).reshape(B, 6 * D)
    small = jnp.concatenate([d_n1g, d_qg, d_kg, d_sinks, d_alog, d_dtb, d_dng, d_n2g, d_conv.reshape(1, CONV * CONVW)], axis=1)
    nsm = small.shape[1]
    width = -(-max(6 * D, nsm) // 128) * 128
    rows = jnp.concatenate([jnp.pad(dmod, ((0, 0), (0, width - 6 * D))), jnp.pad(small, ((0, 8 - B - 1), (0, width - nsm)))], axis=0)
    rows_all = _all_gather_small(rows, "gather_small")
    dmod_all = rows_all[:, 0:B, 0:6 * D].reshape(N_DEV * B, 6 * D)
    dmod_cols = lax.dynamic_slice(dmod_all, (0, me * ncol), (N_DEV * B, ncol))
    grad_ada_w, grad_ada_b, small_sum = _ada_bwd(cond_all, dmod_all, dmod_cols, rows_all[:, B, :])
    sizes = [D, HD, HD, HQ, DH, DH, DK, D]
    so = np.cumsum([0] + sizes)
    g_n1, g_qg, g_kg, g_sk, g_al, g_dt, g_dn, g_n2 = [small_sum[:, so[i]:so[i + 1]] for i in range(8)]
    g_conv_all = small_sum[:, so[8]:so[8] + CONV * CONVW].reshape(CONV, N_DEV, CONVW // N_DEV)
    grad_conv = lax.dynamic_slice(g_conv_all, (0, me, 0), (CONV, 1, CONVW // N_DEV)).reshape(CONV, CONVW // N_DEV)

    g_w_in = _wgrad(dblk, one(h1), "wgrad_in", after=small_sum)
    proj = _exchange_start([g_w_in], me, small_sum, "exchange_in_start")
    g_w_out = _wgrad(one(merged), one(dmix), "wgrad_out", after=proj[3])
    g_w_branch = _wgrad(ob, dyo, "wgrad_branch", after=proj[3])
    mixer = _exchange_start([g_w_branch.reshape(N_DEV, D // N_DEV, D), g_w_out.reshape(N_DEV, D // N_DEV, D)], me,
                            proj[3], "exchange_mix_start")

    upd, grads = {}, {}

    def finish(names, parts, weights):
        for nm, p, (w, m, v) in zip(names, parts, weights):
            grads[nm], *upd[nm] = _sum_adamw(p, w, m, v, "update_" + nm)

    finish(["w_gate_up", "w_down"], _copies_wait(*ffn[:3], mixer[3], "exchange_ffn_wait"),
           [(tr(w_gate_up), tr(m_w_gate_up), tr(v_w_gate_up)), (w_down, m_w_down, v_w_down)])
    finish(["w_in"], _copies_wait(*proj[:3], grads["w_gate_up"], "exchange_in_wait"),
           [(tr(w_in), tr(m_w_in), tr(v_w_in))])
    finish(["w_branch", "w_out"], _copies_wait(*mixer[:3], grads["w_in"], "exchange_mix_wait"),
           [(w_branch, m_w_branch, v_w_branch), (w_out, m_w_out, v_w_out)])
    for nm in ("w_in", "w_gate_up"):
        grads[nm], upd[nm] = tr(grads[nm]), [tr(t) for t in upd[nm]]

    grads["ada_w"] = grad_ada_w.reshape(ada_w.shape)
    upd["ada_w"] = _adamw(ada_w, grads["ada_w"], m_ada_w, v_ada_w, "adamw_ada_w")
    small_names = ["ada_b", "norm1_g", "q_norm_g", "k_norm_g", "sinks", "a_log", "dt_bias", "dn_norm_g", "norm2_g", "conv_w"]
    small_w = [ada_b, norm1_g, q_norm_g, k_norm_g, sinks, a_log, dt_bias, dn_norm_g, norm2_g, conv_w]
    small_g = [grad_ada_b, g_n1, g_qg, g_kg, g_sk, g_al, g_dt, g_dn, g_n2, grad_conv]
    small_m = [m_ada_b, m_norm1_g, m_q_norm_g, m_k_norm_g, m_sinks, m_a_log, m_dt_bias, m_dn_norm_g, m_norm2_g, m_conv_w]
    small_v = [v_ada_b, v_norm1_g, v_q_norm_g, v_k_norm_g, v_sinks, v_a_log, v_dt_bias, v_dn_norm_g, v_norm2_g, v_conv_w]
    cat = lambda arrs: jnp.concatenate([a.reshape(1, -1) for a in arrs], axis=1)
    res = _adamw(cat(small_w), cat(small_g), cat(small_m), cat(small_v), "adamw_small")
    po = np.cumsum([0] + [int(np.prod(w.shape)) for w in small_w])
    for i, nm in enumerate(small_names):
        upd[nm] = tuple(r[:, po[i]:po[i + 1]].reshape(small_w[i].shape) for r in res)
        grads[nm] = small_g[i].reshape(small_w[i].shape)

    order = ["ada_w", "ada_b", "norm1_g", "w_in", "conv_w", "q_norm_g", "k_norm_g", "sinks", "a_log", "dt_bias",
             "dn_norm_g", "w_branch", "w_out", "norm2_g", "w_gate_up", "w_down"]
    return (loss, grad_x, *[grads[n] for n in order], *[upd[n][0] for n in order],
            *[upd[n][1] for n in order], *[upd[n][2] for n in order])
```

```python
import functools

import numpy as np
import jax
import jax.numpy as jnp
from jax import lax
from jax.experimental import pallas as pl
from jax.experimental.pallas import tpu as pltpu

F32 = jnp.float32
BF16 = jnp.bfloat16
HI = lax.Precision.HIGHEST

N_DEV = 8
D = 1024
HQ, HKV, HD = 8, 2, 64
GRP = HQ // HKV
BLK = 128
ROT = HD // 4
THETA = 500000.0
QW, KVW = HQ * HD, HKV * HD
DH, DK = 4, 128
CH = 64
DNW = DH * DK
CONV = 4
CONVW = 3 * DNW
FFN = 2816
EPS = 1e-6
IN_W = QW + 2 * KVW + CONVW + 2 * DH + DNW + 2 * D

LR, B1, B2, AEPS, WD, STEP = 0.001, 0.9, 0.999, 1e-08, 0.01, 10

VMEM_LIMIT = 56 * 1024 * 1024
MESH = pl.DeviceIdType.MESH


def _cparams(sem=None, vmem=VMEM_LIMIT):
    return pltpu.CompilerParams(dimension_semantics=sem, vmem_limit_bytes=vmem)


def _full(shape):
    n = len(shape)
    return pl.BlockSpec(shape, lambda *_: (0,) * n)


def _resident(shape):
    n = len(shape)
    return pl.BlockSpec(shape, lambda *_: (0,) * n, pipeline_mode=pl.Buffered(1))


def _rows(tm, w):
    return pl.BlockSpec((None, tm, w), lambda b, i: (b, i, 0))


def _stacked(n, tm, w):
    return pl.BlockSpec((None, n, tm, w), lambda b, i: (b, 0, i, 0))


def _perb(r, w):
    return pl.BlockSpec((None, r, w), lambda b, i: (b, 0, 0))


def _dot(a, b):
    return jnp.dot(a.astype(BF16), b.astype(BF16), preferred_element_type=F32)


def _dot_nt(a, b):
    return lax.dot_general(a.astype(BF16), b.astype(BF16), (((1,), (1,)), ((), ())), preferred_element_type=F32)


def _dot_tn(a, b):
    return lax.dot_general(a.astype(BF16), b.astype(BF16), (((0,), (0,)), ((), ())), preferred_element_type=F32)


def _dot_hi(a, b):
    return jnp.dot(a, b, preferred_element_type=F32, precision=HI)


def _sigmoid(x):
    return jax.nn.sigmoid(x)


def _silu(x):
    return x * jax.nn.sigmoid(x)


def _rms_mod(x, g, scale, shift):
    r = lax.rsqrt(jnp.mean(x * x, axis=-1, keepdims=True) + EPS)
    return (x * r * g) * (1.0 + scale) + shift


def _tile(S, rows=256):
    return min(rows, S)


def _peer(x, y, c, k):
    px = 1 - x if (k >> 2) & 1 else x
    py = 1 - y if (k >> 1) & 1 else y
    pc = 1 - c if k & 1 else c
    return px, py, pc


def _all_gather_small(v, name):
    r, n = v.shape

    def body(v_ref, out_ref, send_sems, recv_sems, local_sem):
        x, y, c = lax.axis_index("x"), lax.axis_index("y"), lax.axis_index("c")
        me = 4 * x + 2 * y + c
        mine = pltpu.make_async_copy(v_ref, out_ref.at[me], local_sem)
        mine.start()
        sends = []
        for k in range(1, N_DEV):
            cp = pltpu.make_async_remote_copy(
                src_ref=v_ref, dst_ref=out_ref.at[me], send_sem=send_sems.at[k - 1], recv_sem=recv_sems.at[k - 1],
                device_id=_peer(x, y, c, k), device_id_type=MESH)
            cp.start()
            sends.append(cp)
        for k in range(1, N_DEV):
            px, py, pc = _peer(x, y, c, k)
            pltpu.make_async_remote_copy(
                src_ref=v_ref, dst_ref=out_ref.at[4 * px + 2 * py + pc], send_sem=send_sems.at[k - 1],
                recv_sem=recv_sems.at[k - 1], device_id=(px, py, pc), device_id_type=MESH).wait_recv()
        for cp in sends:
            cp.wait_send()
        mine.wait()

    return pl.pallas_call(
        body, name=name,
        out_shape=jax.ShapeDtypeStruct((N_DEV, r, n), v.dtype),
        in_specs=[pl.BlockSpec(memory_space=pltpu.VMEM)],
        out_specs=pl.BlockSpec(memory_space=pltpu.VMEM),
        scratch_shapes=[pltpu.SemaphoreType.DMA((N_DEV - 1,)), pltpu.SemaphoreType.DMA((N_DEV - 1,)), pltpu.SemaphoreType.DMA],
    )(v)


def _all_gather_big(vs, name, after=(), side=None):
    na, nf = len(vs), len(after)
    side_fn, side_in, side_out = side if side is not None else (None, (), ())
    ns, no = len(side_in), len(side_out)

    def body(*refs):
        v_refs, out_refs = refs[:na], refs[na + nf + ns:2 * na + nf + ns]
        send_sems, recv_sems, local_sems = refs[2 * na + nf + ns + no:]
        x, y, c = lax.axis_index("x"), lax.axis_index("y"), lax.axis_index("c")
        me, sibling = (x, y, c), (x, y, 1 - c)
        chips = [(1 - x, y), (x, 1 - y), (1 - x, 1 - y)]

        def rows(a, px, py, pc):
            return out_refs[a].at[4 * px + 2 * py + pc]

        def copy(a, k, block, to, src=None):
            return pltpu.make_async_remote_copy(
                src_ref=rows(a, *block) if src is None else src, dst_ref=rows(a, *block),
                send_sem=send_sems.at[7 * a + k], recv_sem=recv_sems.at[7 * a + k], device_id=to, device_id_type=MESH)

        mine = [pltpu.make_async_copy(v_refs[a], rows(a, *me), local_sems.at[a]) for a in range(na)]
        for cp in mine:
            cp.start()
        first = []
        for a in range(na):
            first.append(copy(a, 0, me, sibling, src=v_refs[a]))
            first += [copy(a, 1 + j, me, (*chip, c), src=v_refs[a]) for j, chip in enumerate(chips)]
        for cp in first:
            cp.start()
        if side_fn is not None:
            side_fn(refs[na + nf:na + nf + ns], refs[2 * na + nf + ns:2 * na + nf + ns + no])
        passed = []
        for j, chip in enumerate(chips):
            for a in range(na):
                copy(a, 1 + j, (*chip, c), me).wait_recv()
                forward = copy(a, 4 + j, (*chip, c), sibling)
                forward.start()
                passed.append(forward)
        for a in range(na):
            copy(a, 0, sibling, me).wait_recv()
            for j, chip in enumerate(chips):
                copy(a, 4 + j, (*chip, 1 - c), me).wait_recv()
        for cp in first + passed:
            cp.wait_send()
        for cp in mine:
            cp.wait()

    return pl.pallas_call(
        body, name=name,
        out_shape=[jax.ShapeDtypeStruct((N_DEV,) + v.shape, v.dtype) for v in vs] + list(side_out),
        in_specs=[pl.BlockSpec(memory_space=pl.ANY)] * (na + nf) + [pl.BlockSpec(memory_space=pltpu.VMEM)] * ns,
        out_specs=[pl.BlockSpec(memory_space=pl.ANY)] * na + [pl.BlockSpec(memory_space=pltpu.VMEM)] * no,
        scratch_shapes=[pltpu.SemaphoreType.DMA((7 * na,)), pltpu.SemaphoreType.DMA((7 * na,)),
                        pltpu.SemaphoreType.DMA((na,))],
        compiler_params=pltpu.CompilerParams(vmem_limit_bytes=VMEM_LIMIT),
    )(*vs, *after, *side_in)


_HBM = pl.BlockSpec(memory_space=pltpu.HBM)
_SEM = pl.BlockSpec(memory_space=pltpu.SEMAPHORE)
_EFFECT = pltpu.SideEffectType.DATAFLOW_SIDE_EFFECTING


def _place_own(block, me):
    land = lax.empty((N_DEV,) + block.shape, block.dtype)
    return lax.dynamic_update_slice(land, block[None], (me,) + (0,) * block.ndim)


def _copies_start(srcs, lands, scatter, after, name):
    na = len(srcs)
    afters = tuple(after) if isinstance(after, (tuple, list)) else (after,)

    def body(*refs):
        src_refs, land_refs = refs[:na], refs[na:2 * na]
        sems = refs[2 * na + len(afters):4 * na + len(afters)]
        token = refs[-1]
        x, y, c = lax.axis_index("x"), lax.axis_index("y"), lax.axis_index("c")
        me = 4 * x + 2 * y + c
        for a in range(na):
            for k in range(1, N_DEV):
                px, py, pc = _peer(x, y, c, k)
                src = src_refs[a].at[4 * px + 2 * py + pc] if scatter else src_refs[a]
                pltpu.make_async_remote_copy(
                    src_ref=src, dst_ref=land_refs[a].at[me], send_sem=sems[2 * a], recv_sem=sems[2 * a + 1],
                    device_id=(px, py, pc), device_id_type=MESH).start()
        token[...] = jnp.zeros_like(token)

    hbm = lambda t: pltpu.HBM(t.shape, t.dtype)
    out = pl.pallas_call(
        body, name=name,
        out_shape=tuple([pltpu.SemaphoreType.DMA(())] * (2 * na) + [hbm(t) for t in srcs] + [hbm(t) for t in lands]
                        + [jax.ShapeDtypeStruct((8, 128), F32)]),
        in_specs=[_HBM] * (2 * na) + [pl.BlockSpec(memory_space=pl.ANY)] * len(afters),
        out_specs=tuple([_SEM] * (2 * na) + [_HBM] * (2 * na) + [pl.BlockSpec(memory_space=pltpu.VMEM)]),
        input_output_aliases={i: 2 * na + i for i in range(2 * na)},
        compiler_params=pltpu.CompilerParams(has_side_effects=_EFFECT),
    )(*[pltpu.with_memory_space_constraint(t, pltpu.HBM) for t in list(srcs) + list(lands)], *afters)
    return out[:2 * na], out[2 * na:3 * na], out[3 * na:4 * na], out[-1]


def _exchange_start(gs, me, after, name):
    own = [lax.dynamic_index_in_dim(g, me, 0, keepdims=False) for g in gs]
    return _copies_start(gs, [_place_own(o, me) for o in own], True, after, name)


def _copies_wait(sems, srcs, lands, after, name):
    na = len(srcs)

    def body(*refs):
        land_refs = refs[na:2 * na]
        sem_refs = refs[2 * na:4 * na]
        x, y, c = lax.axis_index("x"), lax.axis_index("y"), lax.axis_index("c")
        for a in range(na):
            seven = land_refs[a].at[pl.ds(0, N_DEV - 1)]
            copy = pltpu.make_async_remote_copy(
                src_ref=seven, dst_ref=seven, send_sem=sem_refs[2 * a], recv_sem=sem_refs[2 * a + 1],
                device_id=(x, y, c), device_id_type=MESH)
            copy.wait_send()
            copy.wait_recv()

    hbm = lambda t: pltpu.HBM(t.shape, t.dtype)
    out = pl.pallas_call(
        body, name=name,
        out_shape=tuple([hbm(t) for t in srcs] + [hbm(t) for t in lands]),
        in_specs=[_HBM] * (2 * na) + [_SEM] * (2 * na) + [pl.BlockSpec(memory_space=pl.ANY)],
        out_specs=tuple([_HBM] * (2 * na)),
        input_output_aliases={i: i for i in range(2 * na)},
        compiler_params=pltpu.CompilerParams(has_side_effects=_EFFECT),
    )(*srcs, *lands, *sems, after)
    return out[na:]


def _adamw_math(w, g, m, v):
    m = B1 * m + (1.0 - B1) * g
    v = B2 * v + (1.0 - B2) * (g * g)
    m_hat = m / (1.0 - B1 ** STEP)
    v_hat = v / (1.0 - B2 ** STEP)
    return -LR * (m_hat / (jnp.sqrt(v_hat) + AEPS) + WD * w), m, v


def _sum_adamw(parts, w, m, v, name):
    _, r, n = parts.shape
    tr = 256 if r % 256 == 0 else r

    def body(p_ref, w_ref, m_ref, v_ref, g_ref, d_ref, nm_ref, nv_ref):
        g = p_ref[0].astype(F32)
        for dev in range(1, N_DEV):
            g = g + p_ref[dev].astype(F32)
        g_ref[...] = g
        d_ref[...], nm_ref[...], nv_ref[...] = _adamw_math(w_ref[...], g, m_ref[...], v_ref[...])

    rows = pl.BlockSpec((None, tr, n), lambda i: (0, i, 0))
    sd = jax.ShapeDtypeStruct((1, r, n), F32)
    return pl.pallas_call(
        body, name=name, grid=(r // tr,), out_shape=(sd, sd, sd, sd),
        in_specs=[pl.BlockSpec((N_DEV, tr, n), lambda i: (0, i, 0)), rows, rows, rows],
        out_specs=(rows, rows, rows, rows),
        compiler_params=_cparams(("parallel",)),
    )(parts, w, m, v)


def _ada_fwd(c_all, ada_w, ada_b_cols):
    nb, ncol = c_all.shape[0], ada_w.shape[1]

    def body(c_ref, w_ref, b_ref, mod_ref, cond_ref):
        cond = _silu(c_ref[...])
        cond_ref[...] = cond
        mod_ref[...] = _dot_hi(cond, w_ref[...]) + b_ref[...]

    return pl.pallas_call(
        body, name="ada_fwd",
        out_shape=(jax.ShapeDtypeStruct((nb, ncol), F32), jax.ShapeDtypeStruct((nb, D), F32)),
        compiler_params=_cparams(),
    )(c_all, ada_w, ada_b_cols)


def _ada_bwd(cond_all, dmod_all, dmod_cols, smalls):
    ncol, nsm = dmod_cols.shape[1], smalls.shape[1]

    def body(cond_ref, dm_ref, dmc_ref, sm_ref, gw_ref, gb_ref, gs_ref):
        gw_ref[...] = lax.dot_general(cond_ref[...], dmc_ref[...], (((0,), (0,)), ((), ())),
                                      preferred_element_type=F32, precision=HI)
        gb_ref[...] = jnp.sum(dm_ref[...], axis=0, keepdims=True)
        gs_ref[...] = jnp.sum(sm_ref[...], axis=0, keepdims=True)

    return pl.pallas_call(
        body, name="ada_bwd",
        out_shape=(jax.ShapeDtypeStruct((D, ncol), F32), jax.ShapeDtypeStruct((1, 6 * D), F32),
                   jax.ShapeDtypeStruct((1, nsm), F32)),
        compiler_params=_cparams(),
    )(cond_all, dmod_all, dmod_cols, smalls)


IN_CUTS = (0, QW, QW + 2 * KVW, QW + 2 * KVW + CONVW, QW + 2 * KVW + CONVW + 2 * DH,
           QW + 2 * KVW + CONVW + 2 * DH + DNW, QW + 2 * KVW + CONVW + 2 * DH + DNW + D, IN_W)
IN_WIDTHS = tuple(b - a for a, b in zip(IN_CUTS[:-1], IN_CUTS[1:]))
IN_SHARD = IN_W // N_DEV


def _inproj_fwd(x, mod, g1, w_t):
    B, S, _ = x.shape
    tm = _tile(S)

    def body(x_ref, mod_ref, g_ref, w_ref, h_ref, *o_refs):
        h = _rms_mod(x_ref[...], g_ref[...], mod_ref[1:2, :], mod_ref[0:1, :]).astype(BF16)
        h_ref[...] = h
        full = _dot_nt(h, w_ref[...])
        for o_ref, lo, hi in zip(o_refs, IN_CUTS[:-1], IN_CUTS[1:]):
            o_ref[...] = full[:, lo:hi]

    return pl.pallas_call(
        body, name="inproj_fwd", grid=(B, S // tm),
        out_shape=[jax.ShapeDtypeStruct((B, S, D), BF16)] + [jax.ShapeDtypeStruct((B, S, w), F32) for w in IN_WIDTHS],
        in_specs=[_rows(tm, D), _perb(6, D), _full((1, D)), _resident(w_t.shape)],
        out_specs=[_rows(tm, D)] + [_rows(tm, w) for w in IN_WIDTHS],
        compiler_params=_cparams(("parallel", "arbitrary")),
    )(x, mod, g1, w_t)


def _inproj_bwd(x, mod, g1, dx1, dps, w_t):
    B, S, _ = x.shape
    tm = _tile(S)
    n = len(dps)

    def body(x_ref, mod_ref, g_ref, dx1_ref, *refs):
        dp_refs, w_ref = refs[:n], refs[n]
        dblk_ref, gx_ref, dg_ref, dsc_ref, dsh_ref = refs[n + 1:]
        b, i = pl.program_id(0), pl.program_id(1)
        full = jnp.concatenate([r[...].astype(F32) for r in dp_refs], axis=1)
        for j in range(N_DEV):
            dblk_ref[j] = full[:, IN_SHARD * j:IN_SHARD * (j + 1)].astype(BF16)
        dh = jnp.dot(full.astype(BF16), w_ref[...], preferred_element_type=F32)
        _, vjp = jax.vjp(_rms_mod, x_ref[...], g_ref[...], mod_ref[1:2, :], mod_ref[0:1, :])
        dx, dg, dsc, dsh = vjp(dh)
        gx_ref[...] = dx1_ref[...] + dx

        @pl.when((b == 0) & (i == 0))
        def _():
            dg_ref[...] = jnp.zeros_like(dg_ref)

        @pl.when(i == 0)
        def _():
            dsc_ref[...] = jnp.zeros_like(dsc_ref)
            dsh_ref[...] = jnp.zeros_like(dsh_ref)

        dg_ref[...] += dg
        dsc_ref[...] += dsc
        dsh_ref[...] += dsh

    return pl.pallas_call(
        body, name="inproj_bwd", grid=(B, S // tm),
        out_shape=[jax.ShapeDtypeStruct((B, N_DEV, S, IN_SHARD), BF16), jax.ShapeDtypeStruct((B, S, D), F32),
                   jax.ShapeDtypeStruct((1, D), F32), jax.ShapeDtypeStruct((B, 1, D), F32),
                   jax.ShapeDtypeStruct((B, 1, D), F32)],
        in_specs=[_rows(tm, D), _perb(6, D), _full((1, D)), _rows(tm, D)]
                 + [_rows(tm, w) for w in IN_WIDTHS] + [_resident(w_t.shape)],
        out_specs=[pl.BlockSpec((None, N_DEV, tm, IN_SHARD), lambda b, i: (b, 0, i, 0)), _rows(tm, D),
                   _full((1, D)), _perb(1, D), _perb(1, D)],
        compiler_params=_cparams(("arbitrary", "arbitrary")),
    )(x, mod, g1, dx1, *dps, w_t)


def _wgrad(a, b, name, after=None):
    B, na, S, K = a.shape
    nb, N = b.shape[1], b.shape[3]
    G = max(na, nb)
    tm = min(4096, S)
    nt = S // tm
    last = B * nt - 1

    def body(a_ref, b_ref, *rest):
        o_ref, acc = rest[-2:]
        t = pl.program_id(1)

        @pl.when(t == 0)
        def _():
            acc[...] = jnp.zeros_like(acc)

        acc[...] += lax.dot_general(a_ref[...], b_ref[...], (((0,), (0,)), ((), ())), preferred_element_type=F32)

        @pl.when(t == last)
        def _():
            o_ref[...] = acc[...].astype(BF16)

    return pl.pallas_call(
        body, name=name, grid=(G, B * nt),
        out_shape=jax.ShapeDtypeStruct((G, K, N), BF16),
        in_specs=[pl.BlockSpec((None, None, tm, K), lambda g, t: (t // nt, g if na > 1 else 0, t % nt, 0)),
                  pl.BlockSpec((None, None, tm, N), lambda g, t: (t // nt, g if nb > 1 else 0, t % nt, 0))]
                 + ([] if after is None else [pl.BlockSpec(memory_space=pl.ANY)]),
        out_specs=pl.BlockSpec((None, K, N), lambda g, t: (g, 0, 0)),
        scratch_shapes=[pltpu.VMEM((K, N), F32)],
        compiler_params=_cparams(("parallel", "arbitrary")),
    )(*((a, b) if after is None else (a, b, after)))


LANES = 128


def _attn_consts():
    inv_freq = THETA ** (-jnp.arange(0, ROT, 2, dtype=F32) / ROT)
    head = jnp.concatenate([inv_freq, inv_freq, jnp.zeros((HD - ROT,), F32)])
    invf = jnp.tile(head, LANES // HD)[None, :]
    mean_of = lambda w: jnp.asarray(np.kron(np.eye(w // HD), np.full((HD, HD), 1.0 / HD)), BF16)
    return invf, mean_of(QW), mean_of(KVW)


def _rope_tables_side(pos, invf):
    B, S, _ = pos.shape
    tr = min(512, S)

    def fn(ins, outs):
        p_ref, f_ref = ins
        c_ref, s_ref = outs
        for b in range(B):
            for r in range(0, S, tr):
                ang = p_ref[b, r:r + tr, :].astype(F32) * f_ref[...]
                c_ref[b, r:r + tr, :] = jnp.cos(ang)
                s_ref[b, r:r + tr, :] = jnp.sin(ang)

    sd = jax.ShapeDtypeStruct((B, S, LANES), F32)
    return fn, (pos, invf), (sd, sd)


def _rope_expand(cos, sin, reps):
    lane = lax.broadcasted_iota(jnp.int32, cos.shape, 1) % HD
    sa = jnp.where((lane >= ROT // 2) & (lane < ROT), sin, 0.0)
    sb = jnp.where(lane < ROT // 2, -sin, 0.0)
    rep = lambda t: jnp.concatenate([t] * reps, axis=1) if reps > 1 else t
    return rep(cos), rep(sa), rep(sb)


@jax.custom_vjp
def _rope(t, cos, sa, sb):
    w = t.shape[1]
    return t * cos + pltpu.roll(t, ROT // 2, 1) * sa + pltpu.roll(t, w - ROT // 2, 1) * sb


def _rope_fwd(t, cos, sa, sb):
    return _rope(t, cos, sa, sb), (cos, sa, sb)


def _rope_bwd(res, d):
    cos, sa, sb = res
    w = d.shape[1]
    dt = d * cos + pltpu.roll(d * sa, w - ROT // 2, 1) + pltpu.roll(d * sb, ROT // 2, 1)
    return dt, jnp.zeros_like(cos), jnp.zeros_like(sa), jnp.zeros_like(sb)


_rope.defvjp(_rope_fwd, _rope_bwd)


def _head_norm(t, g, mean_of):
    hi, lo = _split(t * t)
    ms = jnp.dot(hi, mean_of, preferred_element_type=F32) + jnp.dot(lo, mean_of, preferred_element_type=F32)
    return t * lax.rsqrt(ms + EPS) * g


def _attn_block(q, kvp, kvc, qg, kg, sinks, tq, tk, mq, mk, valid):
    qn = _rope(_head_norm(q, jnp.concatenate([qg] * HQ, axis=1), mq), *tq) * (HD ** -0.5)
    kv = jnp.concatenate([kvp, kvc], axis=0)
    kn = _rope(_head_norm(kv[:, 0:KVW], jnp.concatenate([kg] * HKV, axis=1), mk), *tk)
    per_tile = LANES // HD
    vT = jnp.transpose(kv[:, KVW:2 * KVW])
    qT = [jnp.transpose(qn[:, LANES * t:LANES * (t + 1)]) for t in range(QW // LANES)]
    head_T = lambda h: qT[h // per_tile][HD * (h % per_tile):HD * (h % per_tile + 1), :]
    none = jnp.zeros((HD, GRP * BLK), F32)
    o_T = []
    for j in range(HKV):
        q4T = jnp.concatenate([head_T(GRP * j + i) for i in range(GRP)], axis=1)
        sT = _dot(kn, jnp.concatenate([q4T, none] if j == 0 else [none, q4T], axis=0))
        sT = jnp.where(valid, sT, -1e30)
        sink = jnp.concatenate([jnp.broadcast_to(sinks[:, GRP * j + i:GRP * j + i + 1], (1, BLK)) for i in range(GRP)], axis=1)
        m = lax.stop_gradient(jnp.maximum(jnp.max(sT, axis=0, keepdims=True), sink))
        pT = jnp.exp(sT - m)
        den = jnp.sum(pT, axis=0, keepdims=True) + jnp.exp(sink - m)
        oT = _dot(vT[HD * j:HD * (j + 1), :], pT) * (1.0 / den)
        o_T += [oT[:, BLK * i:BLK * (i + 1)] for i in range(GRP)]
    return jnp.concatenate([jnp.transpose(jnp.concatenate(o_T[per_tile * t:per_tile * (t + 1)], axis=0))
                            for t in range(QW // LANES)], axis=1)


def _attn_tables(cp_ref, cc_ref, sp_ref, sc_ref, n):
    tq = _rope_expand(cc_ref[...], sc_ref[...], QW // LANES)
    tk = _rope_expand(jnp.concatenate([cp_ref[...], cc_ref[...]], axis=0),
                      jnp.concatenate([sp_ref[...], sc_ref[...]], axis=0), KVW // LANES)
    qi = lax.broadcasted_iota(jnp.int32, (2 * BLK, GRP * BLK), 1) % BLK + BLK
    kj = lax.broadcasted_iota(jnp.int32, (2 * BLK, GRP * BLK), 0)
    dist = qi - kj
    valid = (dist >= 0) & (dist < BLK) & ((kj >= BLK) | (n > 0))
    return tq, tk, valid


def _attn_fwd(aq, akv, cos, sin, qg, kg, sinks, mq, mk):
    B, S, _ = aq.shape
    nb = S // BLK

    def body(q_ref, kvp_ref, kvc_ref, cp_ref, cc_ref, sp_ref, sc_ref, qg_ref, kg_ref, sk_ref, mq_ref, mk_ref, o_ref):
        tq, tk, valid = _attn_tables(cp_ref, cc_ref, sp_ref, sc_ref, pl.program_id(1))
        o_ref[...] = _attn_block(q_ref[...], kvp_ref[...], kvc_ref[...], qg_ref[...], kg_ref[...], sk_ref[...],
                                 tq, tk, mq_ref[...], mk_ref[...], valid)

    prev = lambda b, n: (b, jnp.maximum(n - 1, 0), 0)
    cur = lambda b, n: (b, n, 0)
    return pl.pallas_call(
        body, name="attn_fwd", grid=(B, nb),
        out_shape=jax.ShapeDtypeStruct((B, S, QW), F32),
        in_specs=[pl.BlockSpec((None, BLK, QW), cur), pl.BlockSpec((None, BLK, 2 * KVW), prev),
                  pl.BlockSpec((None, BLK, 2 * KVW), cur), pl.BlockSpec((None, BLK, LANES), prev),
                  pl.BlockSpec((None, BLK, LANES), cur), pl.BlockSpec((None, BLK, LANES), prev),
                  pl.BlockSpec((None, BLK, LANES), cur), _full((1, HD)), _full((1, HD)), _full((1, HQ)),
                  _full((QW, QW)), _full((KVW, KVW))],
        out_specs=pl.BlockSpec((None, BLK, QW), cur),
        compiler_params=_cparams(("parallel", "arbitrary")),
    )(aq, akv, akv, cos, cos, sin, sin, qg, kg, sinks, mq, mk)


def _attn_bwd(aq, akv, cos, sin, qg, kg, sinks, mq, mk, do):
    B, S, _ = aq.shape
    nb = S // BLK

    def body(q_ref, kvp_ref, kvc_ref, cp_ref, cc_ref, sp_ref, sc_ref, qg_ref, kg_ref, sk_ref, mq_ref, mk_ref, do_ref,
             dq_ref, dkv_ref, dqg_ref, dkg_ref, dsk_ref, carry):
        b, i = pl.program_id(0), pl.program_id(1)
        tq, tk, valid = _attn_tables(cp_ref, cc_ref, sp_ref, sc_ref, nb - 1 - i)
        fn = functools.partial(_attn_block, tq=tq, tk=tk, mq=mq_ref[...], mk=mk_ref[...], valid=valid)
        _, vjp = jax.vjp(fn, q_ref[...], kvp_ref[...], kvc_ref[...], qg_ref[...], kg_ref[...], sk_ref[...])
        dq, dkvp, dkvc, dqg, dkg, dsk = vjp(do_ref[...])

        @pl.when(i == 0)
        def _():
            carry[...] = jnp.zeros_like(carry)

        @pl.when((b == 0) & (i == 0))
        def _():
            dqg_ref[...] = jnp.zeros_like(dqg_ref)
            dkg_ref[...] = jnp.zeros_like(dkg_ref)
            dsk_ref[...] = jnp.zeros_like(dsk_ref)

        dq_ref[...] = dq.astype(BF16)
        dkv_ref[...] = (dkvc + carry[...]).astype(BF16)
        carry[...] = dkvp
        dqg_ref[...] += dqg
        dkg_ref[...] += dkg
        dsk_ref[...] += dsk

    prev = lambda b, i: (b, jnp.maximum(nb - 2 - i, 0), 0)
    cur = lambda b, i: (b, nb - 1 - i, 0)
    return pl.pallas_call(
        body, name="attn_bwd", grid=(B, nb),
        out_shape=[jax.ShapeDtypeStruct((B, S, QW), BF16), jax.ShapeDtypeStruct((B, S, 2 * KVW), BF16),
                   jax.ShapeDtypeStruct((1, HD), F32), jax.ShapeDtypeStruct((1, HD), F32),
                   jax.ShapeDtypeStruct((1, HQ), F32)],
        in_specs=[pl.BlockSpec((None, BLK, QW), cur), pl.BlockSpec((None, BLK, 2 * KVW), prev),
                  pl.BlockSpec((None, BLK, 2 * KVW), cur), pl.BlockSpec((None, BLK, LANES), prev),
                  pl.BlockSpec((None, BLK, LANES), cur), pl.BlockSpec((None, BLK, LANES), prev),
                  pl.BlockSpec((None, BLK, LANES), cur), _full((1, HD)), _full((1, HD)), _full((1, HQ)),
                  _full((QW, QW)), _full((KVW, KVW)), pl.BlockSpec((None, BLK, QW), cur)],
        out_specs=[pl.BlockSpec((None, BLK, QW), cur), pl.BlockSpec((None, BLK, 2 * KVW), cur),
                   _full((1, HD)), _full((1, HD)), _full((1, HQ))],
        scratch_shapes=[pltpu.VMEM((BLK, 2 * KVW), F32)],
        compiler_params=_cparams(("arbitrary", "arbitrary")),
    )(aq, akv, akv, cos, cos, sin, sin, qg, kg, sinks, mq, mk, do)


def _conv_taps(xe, w, rows):
    y = None
    for j in range(CONV):
        sh = pltpu.roll(xe, CONV - 1 - j, 0)[8:8 + rows, :] if j < CONV - 1 else xe[8:8 + rows, :]
        y = sh * w[j:j + 1, :] if y is None else y + sh * w[j:j + 1, :]
    return y


def _softplus(x):
    return jnp.maximum(x, 0.0) + jnp.log1p(jnp.exp(-jnp.abs(x)))


_BMM = (((2,), (1,)), ((0,), (0,)))
_BMM_NT = (((2,), (2,)), ((0,), (0,)))
_BMM_TN = (((1,), (1,)), ((0,), (0,)))


def _bmm(a, b, dims=_BMM):
    return lax.dot_general(a.astype(BF16), b.astype(BF16), dims, preferred_element_type=F32)


def _split(a):
    hi = a.astype(BF16)
    return hi, (a - hi.astype(F32)).astype(BF16)


def _bmm3(a, b, dims=_BMM):
    ah, al = _split(a)
    bh, bl = _split(b)
    d = lambda p, q: lax.dot_general(p, q, dims, preferred_element_type=F32)
    return d(ah, bh) + (d(ah, bl) + d(al, bh))


TRI_BASE = 8


def _tri_inverse(L):
    ii = lax.broadcasted_iota(jnp.int32, (CH, CH), 0)
    jj = lax.broadcasted_iota(jnp.int32, (CH, CH), 1)
    same = lambda size: (ii // size) == (jj // size)
    diag = jnp.where(same(TRI_BASE), L, 0.0)
    X = (ii == jj).astype(F32) - diag
    P = diag
    n = 2
    while n < TRI_BASE:
        P = _bmm3(P, P)
        X = X + _bmm3(X, P)
        n *= 2
    size = TRI_BASE
    while size < CH:
        joint = jnp.where(same(2 * size) & jnp.logical_not(same(size)), L, 0.0)
        X = X - _bmm3(X, _bmm3(joint, X))
        size *= 2
    return X


@jax.custom_vjp
def _tri_inverse_known(L, T):
    return T


def _tri_inverse_known_fwd(L, T):
    return T, T


def _tri_inverse_known_bwd(T, dT):
    Tt = jnp.swapaxes(T, 1, 2)
    return -_bmm(Tt, _bmm(dT, Tt)), jnp.zeros_like(T)


_tri_inverse_known.defvjp(_tri_inverse_known_fwd, _tri_inverse_known_bwd)


def _triangle(n, upper):
    ii = lax.broadcasted_iota(jnp.int32, (n, CH, CH), 1)
    jj = lax.broadcasted_iota(jnp.int32, (n, CH, CH), 2)
    return ((ii <= jj) if upper else (ii >= jj)).astype(BF16)


@jax.custom_vjp
def _cumsum_rows(g):
    g0 = g.astype(BF16)
    r1 = g - g0.astype(F32)
    g1 = r1.astype(BF16)
    g2 = (r1 - g1.astype(F32)).astype(BF16)
    tri = _triangle(g.shape[0], False)
    d = lambda q: lax.dot_general(tri, q, _BMM, preferred_element_type=F32)
    return d(g0) + (d(g1) + d(g2))


def _cumsum_rows_fwd(g):
    return _cumsum_rows(g), None


def _cumsum_rows_bwd(_, dy):
    hi, lo = _split(dy)
    tri = _triangle(dy.shape[0], True)
    d = lambda q: lax.dot_general(tri, q, _BMM, preferred_element_type=F32)
    return (d(hi) + d(lo),)


_cumsum_rows.defvjp(_cumsum_rows_fwd, _cumsum_rows_bwd)


def _row_sums(t):
    n, r, w = t.shape
    hi, lo = _split(t.reshape(n * r, w))
    ones = jnp.ones((w, w), BF16)
    s = jnp.dot(hi, ones, preferred_element_type=F32) + jnp.dot(lo, ones, preferred_element_type=F32)
    return s.reshape(n, r, w)


def _dn_prep(t_known, qr, kr, v, a_raw, b_raw, a_log, dt_b):
    n = qr.shape[0]
    ii = lax.broadcasted_iota(jnp.int32, (n, CH, CH), 1)
    jj = lax.broadcasted_iota(jnp.int32, (n, CH, CH), 2)
    incl, strict = ii >= jj, ii > jj
    q = qr * lax.rsqrt(_row_sums(qr * qr) + EPS) * (DK ** -0.5)
    k = kr * lax.rsqrt(_row_sums(kr * kr) + EPS)
    beta = _sigmoid(b_raw)
    g = -jnp.exp(a_log) * _softplus(a_raw + dt_b)
    gcb = _cumsum_rows(jnp.broadcast_to(g, (n, CH, DK)))
    gc = gcb[:, :, 0:1]
    gc_row = jnp.swapaxes(gcb, 1, 2)[:, 0:1, 0:CH]
    decay = jnp.where(incl, jnp.exp(jnp.where(incl, gc - gc_row, 0.0)), 0.0)
    kb = k * beta
    L = jnp.where(strict, _bmm(kb, k, _BMM_NT) * decay, 0.0)
    T = _tri_inverse(L) if t_known is None else _tri_inverse_known(L, t_known)
    eg = jnp.exp(gc)
    u = _bmm(T, v * beta)
    w = _bmm(T, kb * eg)
    a_in = _bmm(q, k, _BMM_NT) * decay
    g_last = gc[:, CH - 1:CH, :]
    return u, w, q * eg, k * jnp.exp(g_last - gc), a_in, jnp.exp(g_last), T


def _dn_step(S0, u, w, qd, kd, a_in, cd):
    r = _bmm(jnp.concatenate([w, qd], axis=1), S0)
    v_new = u - r[:, 0:CH, :]
    o = r[:, CH:2 * CH, :] + _bmm(a_in, v_new)
    S1 = S0 * cd + _bmm(kd, v_new, _BMM_TN)
    return o, S1


def _dn_stack(cq, ba, al, dt, G):
    cols = [[] for _ in range(7)]
    for c in range(G):
        rows = slice(CH * c, CH * (c + 1))
        for h in range(DH):
            parts = (cq[rows, DK * h:DK * (h + 1)], cq[rows, DNW + DK * h:DNW + DK * (h + 1)],
                     cq[rows, 2 * DNW + DK * h:2 * DNW + DK * (h + 1)], ba[rows, DH + h:DH + h + 1],
                     ba[rows, h:h + 1], al[:, h:h + 1], dt[:, h:h + 1])
            for col, p in zip(cols, parts):
                col.append(p)
    return tuple(jnp.stack(col) for col in cols)


def _dn_group(S, want):
    g = want
    while (S // CH) % g:
        g //= 2
    return g


def _dn_prep_fwd(xin, conv_w, ba, a_log, dt_b):
    B, S, _ = xin.shape
    nc = S // CH
    G = _dn_group(S, 8)
    r8 = G * CH // 8

    def body(xp_ref, x_ref, cw_ref, ba_ref, al_ref, dt_ref, cq_ref, u_ref, w_ref, qd_ref, kd_ref, a_ref, t_ref, cd_ref):
        xp = jnp.where(pl.program_id(1) > 0, xp_ref[...], 0.0)
        cq = _silu(_conv_taps(jnp.concatenate([xp, x_ref[...]], axis=0), cw_ref[...], G * CH))
        cq_ref[...] = cq
        ops = _dn_stack(cq, ba_ref[...], al_ref[...], dt_ref[...], G)
        u, w, qd, kd, a_in, cd, T = _dn_prep(None, *ops)
        lane4 = lax.broadcasted_iota(jnp.int32, (1, DH), 1)
        for c in range(G):
            rows = slice(CH * c, CH * (c + 1))
            cdrow = jnp.zeros((1, DH), F32)
            for h in range(DH):
                n = DH * c + h
                lanes = slice(DK * h, DK * (h + 1))
                u_ref[rows, lanes] = u[n]
                w_ref[rows, lanes] = w[n]
                qd_ref[rows, lanes] = qd[n]
                kd_ref[rows, lanes] = kd[n]
                a_ref[rows, CH * h:CH * (h + 1)] = a_in[n]
                t_ref[rows, CH * h:CH * (h + 1)] = T[n]
                cdrow = cdrow + jnp.where(lane4 == h, cd[n], 0.0)
            cd_ref[c] = cdrow

    wide = jax.ShapeDtypeStruct((B, S, DNW), F32)
    sq = jax.ShapeDtypeStruct((B, S, DH * CH), F32)
    return pl.pallas_call(
        body, name="dn_prep_fwd", grid=(B, nc // G),
        out_shape=[jax.ShapeDtypeStruct((B, S, CONVW), F32), wide, wide, wide, wide, sq, sq,
                   jax.ShapeDtypeStruct((B, nc, 1, DH), F32)],
        in_specs=[pl.BlockSpec((None, 8, CONVW), lambda b, i: (b, jnp.maximum(i * r8 - 1, 0), 0)),
                  _rows(G * CH, CONVW), _full((CONV, CONVW)), _rows(G * CH, 2 * DH), _full((1, DH)), _full((1, DH))],
        out_specs=[_rows(G * CH, CONVW)] + [_rows(G * CH, DNW)] * 4 + [_rows(G * CH, DH * CH)] * 2
                  + [pl.BlockSpec((None, G, 1, DH), lambda b, i: (b, i, 0, 0))],
        compiler_params=_cparams(("parallel", "arbitrary")),
    )(xin, xin, conv_w, ba, a_log, dt_b)


def _dn_seq_specs(B, steps, gs, rev):
    at = (lambda i: steps - 1 - i) if rev else (lambda i: i)
    wide = pl.BlockSpec((B, gs * CH, DNW), lambda i: (0, at(i), 0))
    a_spec = pl.BlockSpec((B, gs * CH, DH * CH), lambda i: (0, at(i), 0))
    cd_spec = pl.BlockSpec((B, gs, 1, DH), lambda i: (0, at(i), 0, 0))
    st_spec = pl.BlockSpec((B, gs, DH, DK, DK), lambda i: (0, at(i), 0, 0, 0))
    return wide, a_spec, cd_spec, st_spec


def _dn_step_operands(B, c, u_ref, w_ref, qd_ref, kd_ref, a_ref, cd_ref):
    pairs = [(b, h) for b in range(B) for h in range(DH)]
    rows = slice(CH * c, CH * (c + 1))
    wide = lambda ref: jnp.stack([ref[b, rows, DK * h:DK * (h + 1)] for b, h in pairs])
    a_in = jnp.stack([a_ref[b, rows, CH * h:CH * (h + 1)] for b, h in pairs])
    cd = jnp.stack([cd_ref[b, c, :, h:h + 1] for b, h in pairs])
    return wide(u_ref), wide(w_ref), wide(qd_ref), wide(kd_ref), a_in, cd


def _dn_seq_fwd(u, w, qd, kd, a_in, cd):
    B, S, _ = u.shape
    nc = S // CH
    gs = _dn_group(S, 8)

    def body(u_ref, w_ref, qd_ref, kd_ref, a_ref, cd_ref, o_ref, st_ref, state):
        @pl.when(pl.program_id(0) == 0)
        def _():
            state[...] = jnp.zeros_like(state)

        S0 = state[...]
        for c in range(gs):
            for b in range(B):
                st_ref[b, c] = S0[DH * b:DH * (b + 1)]
            o, S0 = _dn_step(S0, *_dn_step_operands(B, c, u_ref, w_ref, qd_ref, kd_ref, a_ref, cd_ref))
            for b in range(B):
                for h in range(DH):
                    o_ref[b, CH * c:CH * (c + 1), DK * h:DK * (h + 1)] = o[DH * b + h]
        state[...] = S0

    wide, a_spec, cd_spec, st_spec = _dn_seq_specs(B, nc // gs, gs, False)
    return pl.pallas_call(
        body, name="dn_seq_fwd", grid=(nc // gs,),
        out_shape=[jax.ShapeDtypeStruct((B, S, DNW), F32), jax.ShapeDtypeStruct((B, nc, DH, DK, DK), F32)],
        in_specs=[wide, wide, wide, wide, a_spec, cd_spec],
        out_specs=[wide, st_spec],
        scratch_shapes=[pltpu.VMEM((B * DH, DK, DK), F32)],
        compiler_params=_cparams(("arbitrary",)),
    )(u, w, qd, kd, a_in, cd)


def _dn_seq_bwd(u, w, qd, kd, a_in, cd, states, do):
    B, S, _ = u.shape
    nc = S // CH
    gs = _dn_group(S, 8)

    def body(u_ref, w_ref, qd_ref, kd_ref, a_ref, cd_ref, st_ref, do_ref,
             du_ref, dw_ref, dqd_ref, dkd_ref, da_ref, dcd_ref, dstate):
        @pl.when(pl.program_id(0) == 0)
        def _():
            dstate[...] = jnp.zeros_like(dstate)

        lane4 = lax.broadcasted_iota(jnp.int32, (1, DH), 1)
        dS = dstate[...]
        for c in reversed(range(gs)):
            rows = slice(CH * c, CH * (c + 1))
            S0 = jnp.concatenate([st_ref[b, c] for b in range(B)], axis=0)
            do = jnp.stack([do_ref[b, rows, DK * h:DK * (h + 1)] for b in range(B) for h in range(DH)])
            _, vjp = jax.vjp(_dn_step, S0, *_dn_step_operands(B, c, u_ref, w_ref, qd_ref, kd_ref, a_ref, cd_ref))
            dS, du, dw, dqd, dkd, da, dcd = vjp((do, dS))
            for b in range(B):
                dcdrow = jnp.zeros((1, DH), F32)
                for h in range(DH):
                    n = DH * b + h
                    lanes = slice(DK * h, DK * (h + 1))
                    du_ref[b, rows, lanes] = du[n]
                    dw_ref[b, rows, lanes] = dw[n]
                    dqd_ref[b, rows, lanes] = dqd[n]
                    dkd_ref[b, rows, lanes] = dkd[n]
                    da_ref[b, rows, CH * h:CH * (h + 1)] = da[n]
                    dcdrow = dcdrow + jnp.where(lane4 == h, dcd[n], 0.0)
                dcd_ref[b, c] = dcdrow
        dstate[...] = dS

    wide, a_spec, cd_spec, st_spec = _dn_seq_specs(B, nc // gs, gs, True)
    sd = jax.ShapeDtypeStruct((B, S, DNW), F32)
    return pl.pallas_call(
        body, name="dn_seq_bwd", grid=(nc // gs,),
        out_shape=[sd, sd, sd, sd, jax.ShapeDtypeStruct((B, S, DH * CH), F32), jax.ShapeDtypeStruct((B, nc, 1, DH), F32)],
        in_specs=[wide, wide, wide, wide, a_spec, cd_spec, st_spec, wide],
        out_specs=[wide, wide, wide, wide, a_spec, cd_spec],
        scratch_shapes=[pltpu.VMEM((B * DH, DK, DK), F32)],
        compiler_params=_cparams(("arbitrary",)),
    )(u, w, qd, kd, a_in, cd, states, do)


def _dn_prep_bwd(xin, conv_w, cq, ba, a_log, dt_b, t_inv, du, dw, dqd, dkd, da, dcd):
    B, S, _ = cq.shape
    nc = S // CH
    G = _dn_group(S, 8)
    R = G * CH
    nblk = nc // G
    r8 = R // 8

    def body(xp_ref, x_ref, cw_ref, cq_ref, ba_ref, al_ref, dt_ref, t_ref, du_ref, dw_ref, dqd_ref, dkd_ref, da_ref, dcd_ref,
             dx_ref, dcw_ref, dba_ref, dal_ref, ddt_ref, carry):
        i = pl.program_id(1)

        @pl.when((pl.program_id(0) == 0) & (i == 0))
        def _():
            dal_ref[...] = jnp.zeros_like(dal_ref)
            ddt_ref[...] = jnp.zeros_like(ddt_ref)
            dcw_ref[...] = jnp.zeros_like(dcw_ref)

        @pl.when(i == 0)
        def _():
            carry[...] = jnp.zeros_like(carry)

        pairs = [(c, h) for c in range(G) for h in range(DH)]
        rows = lambda c: slice(CH * c, CH * (c + 1))
        wide = lambda ref: jnp.stack([ref[rows(c), DK * h:DK * (h + 1)] for c, h in pairs])
        square = lambda ref: jnp.stack([ref[rows(c), CH * h:CH * (h + 1)] for c, h in pairs])
        ops = _dn_stack(cq_ref[...], ba_ref[...], al_ref[...], dt_ref[...], G)
        cots = (wide(du_ref), wide(dw_ref), wide(dqd_ref), wide(dkd_ref), square(da_ref),
                jnp.stack([dcd_ref[c][:, h:h + 1] for c, h in pairs]), jnp.zeros((len(pairs), CH, CH), F32))
        _, vjp = jax.vjp(functools.partial(_dn_prep, square(t_ref)), *ops)
        dq, dk, dv, dar, dbr, dl, dd = vjp(cots)
        lane8 = lax.broadcasted_iota(jnp.int32, (CH, 2 * DH), 1)
        lane4 = lax.broadcasted_iota(jnp.int32, (1, DH), 1)
        dal = jnp.zeros((1, DH), F32)
        ddt = jnp.zeros((1, DH), F32)
        for c in range(G):
            dba = jnp.zeros((CH, 2 * DH), F32)
            for h in range(DH):
                n = DH * c + h
                dba = dba + jnp.where(lane8 == h, dbr[n], 0.0) + jnp.where(lane8 == DH + h, dar[n], 0.0)
                dal = dal + jnp.where(lane4 == h, dl[n], 0.0)
                ddt = ddt + jnp.where(lane4 == h, dd[n], 0.0)
            dba_ref[rows(c), :] = dba.astype(BF16)
        dal_ref[...] += dal
        ddt_ref[...] += ddt

        dcq = jnp.concatenate([jnp.concatenate([t[DH * c + h] for t in (dq, dk, dv) for h in range(DH)], axis=1)
                               for c in range(G)], axis=0)
        w = cw_ref[...]
        xp = jnp.where(i < nblk - 1, xp_ref[...], 0.0)
        xe = jnp.concatenate([xp, x_ref[...]], axis=0)
        taps = [(pltpu.roll(xe, CONV - 1 - j, 0) if j < CONV - 1 else xe)[8:8 + R, :] for j in range(CONV)]
        pre = sum(t * w[j:j + 1, :] for j, t in enumerate(taps))
        sg = _sigmoid(pre)
        dpre = dcq * (sg * (1.0 + pre * (1.0 - sg)))
        ext = jnp.concatenate([dpre, carry[...]], axis=0)
        dx = dpre * w[CONV - 1:CONV, :]
        for j in range(CONV - 1):
            dx = dx + pltpu.roll(ext, R + 8 - (CONV - 1 - j), 0)[0:R, :] * w[j:j + 1, :]
        dx_ref[...] = dx.astype(BF16)
        carry[...] = dpre[0:8, :]
        lane_row = lax.broadcasted_iota(jnp.int32, (CONV, CONVW), 0)
        dcw = jnp.zeros((CONV, CONVW), F32)
        for j in range(CONV):
            dcw = dcw + jnp.where(lane_row == j, jnp.sum(taps[j] * dpre, axis=0, keepdims=True), 0.0)
        dcw_ref[...] += dcw

    rev = lambda w: pl.BlockSpec((None, R, w), lambda b, i: (b, nblk - 1 - i, 0))
    return pl.pallas_call(
        body, name="dn_prep_bwd", grid=(B, nblk),
        out_shape=[jax.ShapeDtypeStruct((B, S, CONVW), BF16), jax.ShapeDtypeStruct((CONV, CONVW), F32),
                   jax.ShapeDtypeStruct((B, S, 2 * DH), BF16), jax.ShapeDtypeStruct((1, DH), F32),
                   jax.ShapeDtypeStruct((1, DH), F32)],
        in_specs=[pl.BlockSpec((None, 8, CONVW), lambda b, i: (b, jnp.maximum((nblk - 1 - i) * r8 - 1, 0), 0)),
                  rev(CONVW), _full((CONV, CONVW)), rev(CONVW), rev(2 * DH), _full((1, DH)), _full((1, DH)), rev(DH * CH)]
                 + [rev(DNW)] * 4 + [rev(DH * CH), pl.BlockSpec((None, G, 1, DH), lambda b, i: (b, nblk - 1 - i, 0, 0))],
        out_specs=[rev(CONVW), _full((CONV, CONVW)), rev(2 * DH), _full((1, DH)), _full((1, DH))],
        scratch_shapes=[pltpu.VMEM((8, CONVW), F32)],
        compiler_params=_cparams(("arbitrary", "arbitrary")),
    )(xin, xin, conv_w, cq, ba, a_log, dt_b, t_inv, du, dw, dqd, dkd, da, dcd)


def _gated_norm(o, z, g):
    outs = []
    for h in range(DH):
        t = o[:, DK * h:DK * (h + 1)]
        r = lax.rsqrt(jnp.mean(t * t, axis=-1, keepdims=True) + EPS)
        outs.append(t * r * g * _silu(z[:, DK * h:DK * (h + 1)]))
    return jnp.concatenate(outs, axis=1)


def _mix_fwd(x, o_attn, o_dn, z, ga, gd, mod, dn_g, w_branch, w_out):
    B, S, _ = x.shape
    tm = _tile(S, 512)

    def body(x_ref, oa_ref, od_ref, z_ref, ga_ref, gd_ref, mod_ref, g_ref, wb_ref, wo_ref,
             x1_ref, mix_ref, mg_ref, ob_ref):
        oa = oa_ref[...].astype(BF16)
        od = _gated_norm(od_ref[...], z_ref[...], g_ref[...]).astype(BF16)
        ob_ref[0] = oa
        ob_ref[1] = od
        ya = jnp.dot(oa, wb_ref[0:QW, :], preferred_element_type=F32)
        yd = jnp.dot(od, wb_ref[QW:QW + DNW, :], preferred_element_type=F32)
        merged = (_sigmoid(ga_ref[...]) * ya + _sigmoid(gd_ref[...]) * yd).astype(BF16)
        mg_ref[...] = merged
        mix = jnp.dot(merged, wo_ref[...], preferred_element_type=F32)
        mix_ref[...] = mix
        x1_ref[...] = x_ref[...] + mod_ref[2:3, :] * mix

    return pl.pallas_call(
        body, name="mix_fwd", grid=(B, S // tm),
        out_shape=[jax.ShapeDtypeStruct((B, S, D), F32), jax.ShapeDtypeStruct((B, S, D), F32),
                   jax.ShapeDtypeStruct((B, S, D), BF16), jax.ShapeDtypeStruct((B, 2, S, QW), BF16)],
        in_specs=[_rows(tm, D), _rows(tm, QW), _rows(tm, DNW), _rows(tm, DNW), _rows(tm, D), _rows(tm, D),
                  _perb(6, D), _full((1, DK)), _resident(w_branch.shape), _resident(w_out.shape)],
        out_specs=[_rows(tm, D), _rows(tm, D), _rows(tm, D), _stacked(2, tm, QW)],
        compiler_params=_cparams(("parallel", "arbitrary")),
    )(x, o_attn, o_dn, z, ga, gd, mod, dn_g, w_branch, w_out)


def _mix_bwd(dx1, mix, o_attn, o_dn, z, ga, gd, mod, dn_g, w_branch, w_out):
    B, S, _ = dx1.shape
    tm = _tile(S, 512)

    def body(dx1_ref, mix_ref, oa_ref, od_ref, z_ref, ga_ref, gd_ref, mod_ref, g_ref, wb_ref, wo_ref,
             dmix_ref, dyo_ref, dga_ref, dgd_ref, dz_ref, doa_ref, dod_ref, dgate_ref, dg_ref):
        b, i = pl.program_id(0), pl.program_id(1)
        dx1 = dx1_ref[...]
        dmix = (dx1 * mod_ref[2:3, :]).astype(BF16)
        dmix_ref[...] = dmix
        dgate = jnp.sum(dx1 * mix_ref[...], axis=0, keepdims=True)
        dmerged = _dot_nt(dmix, wo_ref[...])
        odn, gn_vjp = jax.vjp(_gated_norm, od_ref[...], z_ref[...], g_ref[...])
        ya = _dot(oa_ref[...], wb_ref[0:QW, :])
        yd = _dot(odn, wb_ref[QW:QW + DNW, :])
        sa, sd = _sigmoid(ga_ref[...]), _sigmoid(gd_ref[...])
        dya = (dmerged * sa).astype(BF16)
        dyd = (dmerged * sd).astype(BF16)
        dyo_ref[0] = dya
        dyo_ref[1] = dyd
        dga_ref[...] = (dmerged * ya * sa * (1.0 - sa)).astype(BF16)
        dgd_ref[...] = (dmerged * yd * sd * (1.0 - sd)).astype(BF16)
        doa_ref[...] = _dot_nt(dya, wb_ref[0:QW, :])
        dodn = _dot_nt(dyd, wb_ref[QW:QW + DNW, :])
        dod, dz, dg = gn_vjp(dodn)
        dod_ref[...] = dod
        dz_ref[...] = dz.astype(BF16)

        @pl.when(i == 0)
        def _():
            dgate_ref[...] = jnp.zeros_like(dgate_ref)

        @pl.when((b == 0) & (i == 0))
        def _():
            dg_ref[...] = jnp.zeros_like(dg_ref)

        dgate_ref[...] += dgate
        dg_ref[...] += dg

    return pl.pallas_call(
        body, name="mix_bwd", grid=(B, S // tm),
        out_shape=[jax.ShapeDtypeStruct((B, S, D), BF16), jax.ShapeDtypeStruct((B, 2, S, D), BF16),
                   jax.ShapeDtypeStruct((B, S, D), BF16), jax.ShapeDtypeStruct((B, S, D), BF16),
                   jax.ShapeDtypeStruct((B, S, DNW), BF16),
                   jax.ShapeDtypeStruct((B, S, QW), F32), jax.ShapeDtypeStruct((B, S, DNW), F32),
                   jax.ShapeDtypeStruct((B, 1, D), F32), jax.ShapeDtypeStruct((1, DK), F32)],
        in_specs=[_rows(tm, D), _rows(tm, D), _rows(tm, QW), _rows(tm, DNW), _rows(tm, DNW), _rows(tm, D),
                  _rows(tm, D), _perb(6, D), _full((1, DK)), _resident(w_branch.shape), _resident(w_out.shape)],
        out_specs=[_rows(tm, D), _stacked(2, tm, D), _rows(tm, D), _rows(tm, D), _rows(tm, DNW),
                   _rows(tm, QW), _rows(tm, DNW), _perb(1, D), _full((1, DK))],
        compiler_params=_cparams(("arbitrary", "arbitrary")),
    )(dx1, mix, o_attn, o_dn, z, ga, gd, mod, dn_g, w_branch, w_out)


GU_SHARD = 2 * FFN // N_DEV
GU_HALF = N_DEV // 2


def _ffn1_fwd(x1, mod, g2, w_gu):
    B, S, _ = x1.shape
    tm = _tile(S)

    def body(x_ref, mod_ref, g_ref, w_ref, h_ref, dgate_ref, dup_ref, act_ref):
        h = _rms_mod(x_ref[...], g_ref[...], mod_ref[4:5, :], mod_ref[3:4, :]).astype(BF16)
        h_ref[...] = h
        for j in range(GU_HALF):
            gate = _dot_nt(h, w_ref[j])
            up = _dot_nt(h, w_ref[GU_HALF + j])
            sg = _sigmoid(gate)
            silu = gate * sg
            dgate_ref[j] = up * (sg * (1.0 + gate * (1.0 - sg)))
            dup_ref[j] = silu
            act_ref[j] = (silu * up).astype(BF16)

    blk = lambda dt: jax.ShapeDtypeStruct((B, GU_HALF, S, GU_SHARD), dt)
    return pl.pallas_call(
        body, name="ffn1_fwd", grid=(B, S // tm),
        out_shape=[jax.ShapeDtypeStruct((B, S, D), BF16), blk(F32), blk(F32), blk(BF16)],
        in_specs=[_rows(tm, D), _perb(6, D), _full((1, D)), _resident(w_gu.shape)],
        out_specs=[_rows(tm, D)] + [_stacked(GU_HALF, tm, GU_SHARD)] * 3,
        compiler_params=_cparams(("parallel", "arbitrary")),
    )(x1, mod, g2, w_gu)


def _ffn2_fwd(act, x1, target, mod, w_down):
    B, S, _ = x1.shape
    tm = _tile(S, 512)

    def body(a_ref, x_ref, t_ref, mod_ref, w_ref, dy_ref, loss_ref, dgate_ref):
        b, i = pl.program_id(0), pl.program_id(1)
        y = jnp.dot(a_ref[0], w_ref[0], preferred_element_type=F32)
        for j in range(1, GU_HALF):
            y = y + jnp.dot(a_ref[j], w_ref[j], preferred_element_type=F32)
        err = x_ref[...] + mod_ref[5:6, :] * y - t_ref[...]
        dy = err * (1.0 / D)
        dy_ref[...] = dy

        @pl.when((b == 0) & (i == 0))
        def _():
            loss_ref[...] = jnp.zeros_like(loss_ref)

        @pl.when(i == 0)
        def _():
            dgate_ref[...] = jnp.zeros_like(dgate_ref)

        loss_ref[...] += (0.5 / D) * jnp.sum(err * err)
        dgate_ref[...] += jnp.sum(dy * y, axis=0, keepdims=True)

    return pl.pallas_call(
        body, name="ffn2_fwd", grid=(B, S // tm),
        out_shape=[jax.ShapeDtypeStruct((B, S, D), F32), jax.ShapeDtypeStruct((1, 128), F32),
                   jax.ShapeDtypeStruct((B, 1, D), F32)],
        in_specs=[_stacked(GU_HALF, tm, GU_SHARD), _rows(tm, D), _rows(tm, D), _perb(6, D), _resident(w_down.shape)],
        out_specs=[_rows(tm, D), _full((1, 128)), _perb(1, D)],
        compiler_params=_cparams(("arbitrary", "arbitrary")),
    )(act, x1, target, mod, w_down)


def _ffn2_bwd(dy, act_dgate, act_dup, mod, w_down):
    B, S, _ = dy.shape
    tm = _tile(S)

    def body(dy_ref, dgate_ref, dup_ref, mod_ref, w_ref, dgu_ref, dyg_ref):
        dyg = (dy_ref[...] * mod_ref[5:6, :]).astype(BF16)
        dyg_ref[...] = dyg
        for j in range(GU_HALF):
            dact = _dot_nt(dyg, w_ref[j])
            dgu_ref[j] = (dact * dgate_ref[j]).astype(BF16)
            dgu_ref[GU_HALF + j] = (dact * dup_ref[j]).astype(BF16)

    return pl.pallas_call(
        body, name="ffn2_bwd", grid=(B, S // tm),
        out_shape=[jax.ShapeDtypeStruct((B, N_DEV, S, GU_SHARD), BF16), jax.ShapeDtypeStruct((B, S, D), BF16)],
        in_specs=[_rows(tm, D), _stacked(GU_HALF, tm, GU_SHARD), _stacked(GU_HALF, tm, GU_SHARD), _perb(6, D),
                  _resident(w_down.shape)],
        out_specs=[_stacked(N_DEV, tm, GU_SHARD), _rows(tm, D)],
        compiler_params=_cparams(("parallel", "arbitrary")),
    )(dy, act_dgate, act_dup, mod, w_down)


def _ffn1_bwd(dgu, x1, dy, mod, g2, w_gu):
    B, S, _ = x1.shape
    tm = _tile(S, 512)

    def body(dgu_ref, x_ref, dy_ref, mod_ref, g_ref, w_ref, dx1_ref, dg_ref, dsc_ref, dsh_ref):
        b, i = pl.program_id(0), pl.program_id(1)
        dh = jnp.dot(dgu_ref[0], w_ref[0], preferred_element_type=F32)
        for j in range(1, N_DEV):
            dh = dh + jnp.dot(dgu_ref[j], w_ref[j], preferred_element_type=F32)
        _, vjp = jax.vjp(_rms_mod, x_ref[...], g_ref[...], mod_ref[4:5, :], mod_ref[3:4, :])
        dx, dg, dsc, dsh = vjp(dh)
        dx1_ref[...] = dy_ref[...] + dx

        @pl.when((b == 0) & (i == 0))
        def _():
            dg_ref[...] = jnp.zeros_like(dg_ref)

        @pl.when(i == 0)
        def _():
            dsc_ref[...] = jnp.zeros_like(dsc_ref)
            dsh_ref[...] = jnp.zeros_like(dsh_ref)

        dg_ref[...] += dg
        dsc_ref[...] += dsc
        dsh_ref[...] += dsh

    return pl.pallas_call(
        body, name="ffn1_bwd", grid=(B, S // tm),
        out_shape=[jax.ShapeDtypeStruct((B, S, D), F32), jax.ShapeDtypeStruct((1, D), F32),
                   jax.ShapeDtypeStruct((B, 1, D), F32), jax.ShapeDtypeStruct((B, 1, D), F32)],
        in_specs=[_stacked(N_DEV, tm, GU_SHARD), _rows(tm, D), _rows(tm, D), _perb(6, D), _full((1, D)),
                  _resident(w_gu.shape)],
        out_specs=[_rows(tm, D), _full((1, D)), _perb(1, D), _perb(1, D)],
        compiler_params=_cparams(("arbitrary", "arbitrary")),
    )(dgu, x1, dy, mod, g2, w_gu)


def _adamw(w, g, m, v, name):
    def body(w_ref, g_ref, m_ref, v_ref, d_ref, nm_ref, nv_ref):
        d_ref[...], nm_ref[...], nv_ref[...] = _adamw_math(w_ref[...], g_ref[...], m_ref[...], v_ref[...])

    sd = jax.ShapeDtypeStruct(w.shape, F32)
    return pl.pallas_call(body, name=name, out_shape=(sd, sd, sd), compiler_params=_cparams())(w, g, m, v)


def kernel(x, c, positions, ada_w, ada_b, norm1_g, w_in, conv_w, q_norm_g, k_norm_g, sinks, a_log, dt_bias, dn_norm_g, w_branch, w_out, norm2_g, w_gate_up, w_down, loss_target, m_ada_w, m_ada_b, m_norm1_g, m_w_in, m_conv_w, m_q_norm_g, m_k_norm_g, m_sinks, m_a_log, m_dt_bias, m_dn_norm_g, m_w_branch, m_w_out, m_norm2_g, m_w_gate_up, m_w_down, v_ada_w, v_ada_b, v_norm1_g, v_w_in, v_conv_w, v_q_norm_g, v_k_norm_g, v_sinks, v_a_log, v_dt_bias, v_dn_norm_g, v_w_branch, v_w_out, v_norm2_g, v_w_gate_up, v_w_down):
    B, S, _ = x.shape
    me = 4 * lax.axis_index("x") + 2 * lax.axis_index("y") + lax.axis_index("c")

    tr = lambda t: jnp.swapaxes(t, 1, 2)
    shards = [w[0].astype(BF16) for w in (tr(w_in), w_branch, w_out, tr(w_gate_up), w_down)]

    c_all = _all_gather_small(c, "gather_c").reshape(N_DEV * B, D)
    ncol = 6 * D // N_DEV
    mod_cols, cond_all = _ada_fwd(c_all, ada_w[0], lax.dynamic_slice(ada_b, (0, me * ncol), (1, ncol)))
    mod_all = _all_gather_small(mod_cols, "gather_mod").transpose(1, 0, 2).reshape(N_DEV * B, 6 * D)
    mod = lax.dynamic_slice(mod_all, (me * B, 0), (B, 6 * D)).reshape(B, 6, D)
    conv2 = conv_w.reshape(CONV, CONVW // N_DEV)
    conv_all = _all_gather_small(conv2, "gather_conv").transpose(1, 0, 2).reshape(CONV, CONVW)

    invf, mean_q, mean_k = _attn_consts()
    w_in_b, rope_cos, rope_sin = _all_gather_big(shards[:1], "gather_w_in", after=(mod, conv_all),
                                                 side=_rope_tables_side(positions.reshape(B, S, 1), invf))
    w_sems, w_srcs, w_lands, w_token = _copies_start(shards[1:], [_place_own(s, me) for s in shards[1:]], False,
                                                    w_in_b, "gather_rest_start")

    w_in_t = w_in_b.reshape(IN_W, D)
    h1, aq, akv, dnx, ba, z, ga, gd = _inproj_fwd(x, mod, norm1_g + w_token[0, 0], w_in_t)
    o_attn = _attn_fwd(aq, akv, rope_cos, rope_sin, q_norm_g, k_norm_g, sinks, mean_q, mean_k)
    cq, dn_u, dn_w, dn_qd, dn_kd, dn_a, dn_t, dn_cd = _dn_prep_fwd(dnx, conv_all, ba, a_log, dt_bias)
    o_dn, states = _dn_seq_fwd(dn_u, dn_w, dn_qd, dn_kd, dn_a, dn_cd)
    w_branch_g, w_out_g, w_gu_b, w_down_g = _copies_wait(w_sems, w_srcs, w_lands, o_dn, "gather_wait_rest")
    w_branch_f = w_branch_g.reshape(D, D)
    w_out_f = w_out_g.reshape(D, D)
    w_down_b = w_down_g.reshape(GU_HALF, GU_SHARD, D)
    x1, mix, merged, ob = _mix_fwd(x, o_attn, o_dn, z, ga, gd, mod, dn_norm_g, w_branch_f, w_out_f)
    h2, act_dgate, act_dup, act = _ffn1_fwd(x1, mod, norm2_g, w_gu_b)
    dy, loss_part, d_gate2 = _ffn2_fwd(act, x1, loss_target, mod, w_down_b)
    loss = lax.psum(loss_part[0, 0], ("x", "y", "c"))

    one = lambda t: t.reshape(B, 1, S, t.shape[-1])
    dgu, dyg = _ffn2_bwd(dy, act_dgate, act_dup, mod, w_down_b)
    g_w_down = _wgrad(act, one(dyg), "wgrad_down")
    dx1, d_n2g, d_scale2, d_shift2 = _ffn1_bwd(dgu, x1, dy, mod, norm2_g, w_gu_b)
    g_w_gu = _wgrad(dgu, one(h2), "wgrad_gate_up")
    ffn = _exchange_start([g_w_gu, g_w_down.reshape(N_DEV, FFN // N_DEV, D)], me, dx1, "exchange_ffn_start")
    dmix, dyo, dga, dgd, dz, d_oa, d_od, d_gate1, d_dng = _mix_bwd(
        dx1, mix, o_attn, o_dn, z, ga, gd, mod, dn_norm_g + ffn[3][0, 0], w_branch_f, w_out_f)
    d_dn = _dn_seq_bwd(dn_u, dn_w, dn_qd, dn_kd, dn_a, dn_cd, states, d_od)
    ddnx, d_conv, dba, d_alog, d_dtb = _dn_prep_bwd(dnx, conv_all, cq, ba, a_log, dt_bias, dn_t, *d_dn)
    daq, dakv, d_qg, d_kg, d_sinks = _attn_bwd(aq, akv, rope_cos, rope_sin, q_norm_g, k_norm_g, sinks, mean_q, mean_k, d_oa)
    dps = [daq, dakv, ddnx, dba, dz, dga, dgd]
    dblk, grad_x, d_n1g, d_scale1, d_shift1 = _inproj_bwd(x, mod, norm1_g, dx1, dps, w_in_t)

    dmod = jnp.concatenate([d_shift1, d_scale1, d_gate1, d_shift2, d_scale2, d_gate2], axis=2).reshape(B, 6 * D)
    small = jnp.concatenate([d_n1g, d_qg, d_kg, d_sinks, d_alog, d_dtb, d_dng, d_n2g, d_conv.reshape(1, CONV * CONVW)], axis=1)
    nsm = small.shape[1]
    width = -(-max(6 * D, nsm) // 128) * 128
    rows = jnp.concatenate([jnp.pad(dmod, ((0, 0), (0, width - 6 * D))), jnp.pad(small, ((0, 8 - B - 1), (0, width - nsm)))], axis=0)
    rows_all = _all_gather_small(rows, "gather_small")
    dmod_all = rows_all[:, 0:B, 0:6 * D].reshape(N_DEV * B, 6 * D)
    dmod_cols = lax.dynamic_slice(dmod_all, (0, me * ncol), (N_DEV * B, ncol))
    grad_ada_w, grad_ada_b, small_sum = _ada_bwd(cond_all, dmod_all, dmod_cols, rows_all[:, B, :])
    sizes = [D, HD, HD, HQ, DH, DH, DK, D]
    so = np.cumsum([0] + sizes)
    g_n1, g_qg, g_kg, g_sk, g_al, g_dt, g_dn, g_n2 = [small_sum[:, so[i]:so[i + 1]] for i in range(8)]
    g_conv_all = small_sum[:, so[8]:so[8] + CONV * CONVW].reshape(CONV, N_DEV, CONVW // N_DEV)
    grad_conv = lax.dynamic_slice(g_conv_all, (0, me, 0), (CONV, 1, CONVW // N_DEV)).reshape(CONV, CONVW // N_DEV)

    g_w_in = _wgrad(dblk, one(h1), "wgrad_in", after=small_sum)
    proj = _exchange_start([g_w_in], me, small_sum, "exchange_in_start")
    g_w_out = _wgrad(one(merged), one(dmix), "wgrad_out", after=proj[3])
    g_w_branch = _wgrad(ob, dyo, "wgrad_branch", after=proj[3])
    mixer = _exchange_start([g_w_branch.reshape(N_DEV, D // N_DEV, D), g_w_out.reshape(N_DEV, D // N_DEV, D)], me,
                            proj[3], "exchange_mix_start")

    upd, grads = {}, {}

    def finish(names, parts, weights):
        for nm, p, (w, m, v) in zip(names, parts, weights):
            grads[nm], *upd[nm] = _sum_adamw(p, w, m, v, "update_" + nm)

    finish(["w_gate_up", "w_down"], _copies_wait(*ffn[:3], mixer[3], "exchange_ffn_wait"),
           [(tr(w_gate_up), tr(m_w_gate_up), tr(v_w_gate_up)), (w_down, m_w_down, v_w_down)])
    finish(["w_in"], _copies_wait(*proj[:3], grads["w_gate_up"], "exchange_in_wait"),
           [(tr(w_in), tr(m_w_in), tr(v_w_in))])
    finish(["w_branch", "w_out"], _copies_wait(*mixer[:3], grads["w_in"], "exchange_mix_wait"),
           [(w_branch, m_w_branch, v_w_branch), (w_out, m_w_out, v_w_out)])
    for nm in ("w_in", "w_gate_up"):
        grads[nm], upd[nm] = tr(grads[nm]), [tr(t) for t in upd[nm]]

    grads["ada_w"] = grad_ada_w.reshape(ada_w.shape)
    upd["ada_w"] = _adamw(ada_w, grads["ada_w"], m_ada_w, v_ada_w, "adamw_ada_w")
    small_names = ["ada_b", "norm1_g", "q_norm_g", "k_norm_g", "sinks", "a_log", "dt_bias", "dn_norm_g", "norm2_g", "conv_w"]
    small_w = [ada_b, norm1_g, q_norm_g, k_norm_g, sinks, a_log, dt_bias, dn_norm_g, norm2_g, conv_w]
    small_g = [grad_ada_b, g_n1, g_qg, g_kg, g_sk, g_al, g_dt, g_dn, g_n2, grad_conv]
    small_m = [m_ada_b, m_norm1_g, m_q_norm_g, m_k_norm_g, m_sinks, m_a_log, m_dt_bias, m_dn_norm_g, m_norm2_g, m_conv_w]
    small_v = [v_ada_b, v_norm1_g, v_q_norm_g, v_k_norm_g, v_sinks, v_a_log, v_dt_bias, v_dn_norm_g, v_norm2_g, v_conv_w]
    cat = lambda arrs: jnp.concatenate([a.reshape(1, -1) for a in arrs], axis=1)
    res = _adamw(cat(small_w), cat(small_g), cat(small_m), cat(small_v), "adamw_small")
    po = np.cumsum([0] + [int(np.prod(w.shape)) for w in small_w])
    for i, nm in enumerate(small_names):
        upd[nm] = tuple(r[:, po[i]:po[i + 1]].reshape(small_w[i].shape) for r in res)
        grads[nm] = small_g[i].reshape(small_w[i].shape)

    order = ["ada_w", "ada_b", "norm1_g", "w_in", "conv_w", "q_norm_g", "k_norm_g", "sinks", "a_log", "dt_bias",
             "dn_norm_g", "w_branch", "w_out", "norm2_g", "w_gate_up", "w_down"]
    return (loss, grad_x, *[grads[n] for n in order], *[upd[n][0] for n in order],
            *[upd[n][1] for n in order], *[upd[n][2] for n in order])
```

```python
import functools

import numpy as np
import jax
import jax.numpy as jnp
from jax import lax
from jax.experimental import pallas as pl
from jax.experimental.pallas import tpu as pltpu

F32 = jnp.float32
BF16 = jnp.bfloat16
HI = lax.Precision.HIGHEST

N_DEV = 8
D = 1024
HQ, HKV, HD = 8, 2, 64
GRP = HQ // HKV
BLK = 128
ROT = HD // 4
THETA = 500000.0
QW, KVW = HQ * HD, HKV * HD
DH, DK = 4, 128
CH = 64
DNW = DH * DK
CONV = 4
CONVW = 3 * DNW
FFN = 2816
EPS = 1e-6
IN_W = QW + 2 * KVW + CONVW + 2 * DH + DNW + 2 * D

LR, B1, B2, AEPS, WD, STEP = 0.001, 0.9, 0.999, 1e-08, 0.01, 10

VMEM_LIMIT = 56 * 1024 * 1024
MESH = pl.DeviceIdType.MESH


def _cparams(sem=None, vmem=VMEM_LIMIT):
    return pltpu.CompilerParams(dimension_semantics=sem, vmem_limit_bytes=vmem)


def _full(shape):
    n = len(shape)
    return pl.BlockSpec(shape, lambda *_: (0,) * n)


def _resident(shape):
    n = len(shape)
    return pl.BlockSpec(shape, lambda *_: (0,) * n, pipeline_mode=pl.Buffered(1))


def _rows(tm, w):
    return pl.BlockSpec((None, tm, w), lambda b, i: (b, i, 0))


def _stacked(n, tm, w):
    return pl.BlockSpec((None, n, tm, w), lambda b, i: (b, 0, i, 0))


def _perb(r, w):
    return pl.BlockSpec((None, r, w), lambda b, i: (b, 0, 0))


def _dot(a, b):
    return jnp.dot(a.astype(BF16), b.astype(BF16), preferred_element_type=F32)


def _dot_nt(a, b):
    return lax.dot_general(a.astype(BF16), b.astype(BF16), (((1,), (1,)), ((), ())), preferred_element_type=F32)


def _dot_tn(a, b):
    return lax.dot_general(a.astype(BF16), b.astype(BF16), (((0,), (0,)), ((), ())), preferred_element_type=F32)


def _dot_hi(a, b):
    return jnp.dot(a, b, preferred_element_type=F32, precision=HI)


def _sigmoid(x):
    return jax.nn.sigmoid(x)


def _silu(x):
    return x * jax.nn.sigmoid(x)


def _rms_mod(x, g, scale, shift):
    r = lax.rsqrt(jnp.mean(x * x, axis=-1, keepdims=True) + EPS)
    return (x * r * g) * (1.0 + scale) + shift


def _tile(S, rows=256):
    return min(rows, S)


def _peer(x, y, c, k):
    px = 1 - x if (k >> 2) & 1 else x
    py = 1 - y if (k >> 1) & 1 else y
    pc = 1 - c if k & 1 else c
    return px, py, pc


def _all_gather_small(v, name):
    r, n = v.shape

    def body(v_ref, out_ref, send_sems, recv_sems, local_sem):
        x, y, c = lax.axis_index("x"), lax.axis_index("y"), lax.axis_index("c")
        me = 4 * x + 2 * y + c
        mine = pltpu.make_async_copy(v_ref, out_ref.at[me], local_sem)
        mine.start()
        sends = []
        for k in range(1, N_DEV):
            cp = pltpu.make_async_remote_copy(
                src_ref=v_ref, dst_ref=out_ref.at[me], send_sem=send_sems.at[k - 1], recv_sem=recv_sems.at[k - 1],
                device_id=_peer(x, y, c, k), device_id_type=MESH)
            cp.start()
            sends.append(cp)
        for k in range(1, N_DEV):
            px, py, pc = _peer(x, y, c, k)
            pltpu.make_async_remote_copy(
                src_ref=v_ref, dst_ref=out_ref.at[4 * px + 2 * py + pc], send_sem=send_sems.at[k - 1],
                recv_sem=recv_sems.at[k - 1], device_id=(px, py, pc), device_id_type=MESH).wait_recv()
        for cp in sends:
            cp.wait_send()
        mine.wait()

    return pl.pallas_call(
        body, name=name,
        out_shape=jax.ShapeDtypeStruct((N_DEV, r, n), v.dtype),
        in_specs=[pl.BlockSpec(memory_space=pltpu.VMEM)],
        out_specs=pl.BlockSpec(memory_space=pltpu.VMEM),
        scratch_shapes=[pltpu.SemaphoreType.DMA((N_DEV - 1,)), pltpu.SemaphoreType.DMA((N_DEV - 1,)), pltpu.SemaphoreType.DMA],
    )(v)


def _all_gather_big(vs, name, after=(), side=None):
    na, nf = len(vs), len(after)
    side_fn, side_in, side_out = side if side is not None else (None, (), ())
    ns, no = len(side_in), len(side_out)

    def body(*refs):
        v_refs, out_refs = refs[:na], refs[na + nf + ns:2 * na + nf + ns]
        send_sems, recv_sems, local_sems = refs[2 * na + nf + ns + no:]
        x, y, c = lax.axis_index("x"), lax.axis_index("y"), lax.axis_index("c")
        me, sibling = (x, y, c), (x, y, 1 - c)
        chips = [(1 - x, y), (x, 1 - y), (1 - x, 1 - y)]

        def rows(a, px, py, pc):
            return out_refs[a].at[4 * px + 2 * py + pc]

        def copy(a, k, block, to, src=None):
            return pltpu.make_async_remote_copy(
                src_ref=rows(a, *block) if src is None else src, dst_ref=rows(a, *block),
                send_sem=send_sems.at[7 * a + k], recv_sem=recv_sems.at[7 * a + k], device_id=to, device_id_type=MESH)

        mine = [pltpu.make_async_copy(v_refs[a], rows(a, *me), local_sems.at[a]) for a in range(na)]
        for cp in mine:
            cp.start()
        first = []
        for a in range(na):
            first.append(copy(a, 0, me, sibling, src=v_refs[a]))
            first += [copy(a, 1 + j, me, (*chip, c), src=v_refs[a]) for j, chip in enumerate(chips)]
        for cp in first:
            cp.start()
        if side_fn is not None:
            side_fn(refs[na + nf:na + nf + ns], refs[2 * na + nf + ns:2 * na + nf + ns + no])
        passed = []
        for j, chip in enumerate(chips):
            for a in range(na):
                copy(a, 1 + j, (*chip, c), me).wait_recv()
                forward = copy(a, 4 + j, (*chip, c), sibling)
                forward.start()
                passed.append(forward)
        for a in range(na):
            copy(a, 0, sibling, me).wait_recv()
            for j, chip in enumerate(chips):
                copy(a, 4 + j, (*chip, 1 - c), me).wait_recv()
        for cp in first + passed:
            cp.wait_send()
        for cp in mine:
            cp.wait()

    return pl.pallas_call(
        body, name=name,
        out_shape=[jax.ShapeDtypeStruct((N_DEV,) + v.shape, v.dtype) for v in vs] + list(side_out),
        in_specs=[pl.BlockSpec(memory_space=pl.ANY)] * (na + nf) + [pl.BlockSpec(memory_space=pltpu.VMEM)] * ns,
        out_specs=[pl.BlockSpec(memory_space=pl.ANY)] * na + [pl.BlockSpec(memory_space=pltpu.VMEM)] * no,
        scratch_shapes=[pltpu.SemaphoreType.DMA((7 * na,)), pltpu.SemaphoreType.DMA((7 * na,)),
                        pltpu.SemaphoreType.DMA((na,))],
        compiler_params=pltpu.CompilerParams(vmem_limit_bytes=VMEM_LIMIT),
    )(*vs, *after, *side_in)


_HBM = pl.BlockSpec(memory_space=pltpu.HBM)
_SEM = pl.BlockSpec(memory_space=pltpu.SEMAPHORE)
_EFFECT = pltpu.SideEffectType.DATAFLOW_SIDE_EFFECTING


def _place_own(block, me):
    land = lax.empty((N_DEV,) + block.shape, block.dtype)
    return lax.dynamic_update_slice(land, block[None], (me,) + (0,) * block.ndim)


def _copies_start(srcs, lands, scatter, after, name):
    na = len(srcs)
    afters = tuple(after) if isinstance(after, (tuple, list)) else (after,)

    def body(*refs):
        src_refs, land_refs = refs[:na], refs[na:2 * na]
        sems = refs[2 * na + len(afters):4 * na + len(afters)]
        token = refs[-1]
        x, y, c = lax.axis_index("x"), lax.axis_index("y"), lax.axis_index("c")
        me = 4 * x + 2 * y + c
        for a in range(na):
            for k in range(1, N_DEV):
                px, py, pc = _peer(x, y, c, k)
                src = src_refs[a].at[4 * px + 2 * py + pc] if scatter else src_refs[a]
                pltpu.make_async_remote_copy(
                    src_ref=src, dst_ref=land_refs[a].at[me], send_sem=sems[2 * a], recv_sem=sems[2 * a + 1],
                    device_id=(px, py, pc), device_id_type=MESH).start()
        token[...] = jnp.zeros_like(token)

    hbm = lambda t: pltpu.HBM(t.shape, t.dtype)
    out = pl.pallas_call(
        body, name=name,
        out_shape=tuple([pltpu.SemaphoreType.DMA(())] * (2 * na) + [hbm(t) for t in srcs] + [hbm(t) for t in lands]
                        + [jax.ShapeDtypeStruct((8, 128), F32)]),
        in_specs=[_HBM] * (2 * na) + [pl.BlockSpec(memory_space=pl.ANY)] * len(afters),
        out_specs=tuple([_SEM] * (2 * na) + [_HBM] * (2 * na) + [pl.BlockSpec(memory_space=pltpu.VMEM)]),
        input_output_aliases={i: 2 * na + i for i in range(2 * na)},
        compiler_params=pltpu.CompilerParams(has_side_effects=_EFFECT),
    )(*[pltpu.with_memory_space_constraint(t, pltpu.HBM) for t in list(srcs) + list(lands)], *afters)
    return out[:2 * na], out[2 * na:3 * na], out[3 * na:4 * na], out[-1]


def _exchange_start(gs, me, after, name):
    own = [lax.dynamic_index_in_dim(g, me, 0, keepdims=False) for g in gs]
    return _copies_start(gs, [_place_own(o, me) for o in own], True, after, name)


def _copies_wait(sems, srcs, lands, after, name):
    na = len(srcs)

    def body(*refs):
        land_refs = refs[na:2 * na]
        sem_refs = refs[2 * na:4 * na]
        x, y, c = lax.axis_index("x"), lax.axis_index("y"), lax.axis_index("c")
        for a in range(na):
            seven = land_refs[a].at[pl.ds(0, N_DEV - 1)]
            copy = pltpu.make_async_remote_copy(
                src_ref=seven, dst_ref=seven, send_sem=sem_refs[2 * a], recv_sem=sem_refs[2 * a + 1],
                device_id=(x, y, c), device_id_type=MESH)
            copy.wait_send()
            copy.wait_recv()

    hbm = lambda t: pltpu.HBM(t.shape, t.dtype)
    out = pl.pallas_call(
        body, name=name,
        out_shape=tuple([hbm(t) for t in srcs] + [hbm(t) for t in lands]),
        in_specs=[_HBM] * (2 * na) + [_SEM] * (2 * na) + [pl.BlockSpec(memory_space=pl.ANY)],
        out_specs=tuple([_HBM] * (2 * na)),
        input_output_aliases={i: i for i in range(2 * na)},
        compiler_params=pltpu.CompilerParams(has_side_effects=_EFFECT),
    )(*srcs, *lands, *sems, after)
    return out[na:]


def _adamw_math(w, g, m, v):
    m = B1 * m + (1.0 - B1) * g
    v = B2 * v + (1.0 - B2) * (g * g)
    m_hat = m / (1.0 - B1 ** STEP)
    v_hat = v / (1.0 - B2 ** STEP)
    return -LR * (m_hat / (jnp.sqrt(v_hat) + AEPS) + WD * w), m, v


def _sum_adamw(parts, w, m, v, name):
    parts = list(parts) if isinstance(parts, (list, tuple)) else [parts]
    r, n = w.shape[1], w.shape[2]
    tr = 256 if r % 256 == 0 else r
    npart = len(parts)

    def body(*refs):
        p_refs = refs[:npart]
        w_ref, m_ref, v_ref, g_ref, d_ref, nm_ref, nv_ref = refs[npart:]
        pieces = []
        for p_ref in p_refs:
            g = p_ref[0].astype(F32)
            for dev in range(1, N_DEV):
                g = g + p_ref[dev].astype(F32)
            pieces.append(g)
        g = pieces[0] if npart == 1 else jnp.concatenate(pieces, axis=1)
        g_ref[...] = g
        d_ref[...], nm_ref[...], nv_ref[...] = _adamw_math(w_ref[...], g, m_ref[...], v_ref[...])

    rows = pl.BlockSpec((None, tr, n), lambda i: (0, i, 0))
    sd = jax.ShapeDtypeStruct((1, r, n), F32)
    return pl.pallas_call(
        body, name=name, grid=(r // tr,), out_shape=(sd, sd, sd, sd),
        in_specs=[pl.BlockSpec((N_DEV, tr, p.shape[2]), lambda i: (0, i, 0)) for p in parts] + [rows, rows, rows],
        out_specs=(rows, rows, rows, rows),
        compiler_params=_cparams(("parallel",)),
    )(*parts, w, m, v)


def _ada_fwd(c_all, ada_w, ada_b_cols):
    nb, ncol = c_all.shape[0], ada_w.shape[1]

    def body(c_ref, w_ref, b_ref, mod_ref, cond_ref):
        cond = _silu(c_ref[...])
        cond_ref[...] = cond
        mod_ref[...] = _dot_hi(cond, w_ref[...]) + b_ref[...]

    return pl.pallas_call(
        body, name="ada_fwd",
        out_shape=(jax.ShapeDtypeStruct((nb, ncol), F32), jax.ShapeDtypeStruct((nb, D), F32)),
        compiler_params=_cparams(),
    )(c_all, ada_w, ada_b_cols)


def _ada_bwd(cond_all, dmod_all, dmod_cols, smalls):
    ncol, nsm = dmod_cols.shape[1], smalls.shape[1]

    def body(cond_ref, dm_ref, dmc_ref, sm_ref, gw_ref, gb_ref, gs_ref):
        gw_ref[...] = lax.dot_general(cond_ref[...], dmc_ref[...], (((0,), (0,)), ((), ())),
                                      preferred_element_type=F32, precision=HI)
        gb_ref[...] = jnp.sum(dm_ref[...], axis=0, keepdims=True)
        gs_ref[...] = jnp.sum(sm_ref[...], axis=0, keepdims=True)

    return pl.pallas_call(
        body, name="ada_bwd",
        out_shape=(jax.ShapeDtypeStruct((D, ncol), F32), jax.ShapeDtypeStruct((1, 6 * D), F32),
                   jax.ShapeDtypeStruct((1, nsm), F32)),
        compiler_params=_cparams(),
    )(cond_all, dmod_all, dmod_cols, smalls)


IN_CUTS = (0, QW, QW + 2 * KVW, QW + 2 * KVW + CONVW, QW + 2 * KVW + CONVW + 2 * DH,
           QW + 2 * KVW + CONVW + 2 * DH + DNW, QW + 2 * KVW + CONVW + 2 * DH + DNW + D, IN_W)
IN_WIDTHS = tuple(b - a for a, b in zip(IN_CUTS[:-1], IN_CUTS[1:]))
IN_SHARD = IN_W // N_DEV


def _inproj_fwd(x, mod, g1, w_t):
    B, S, _ = x.shape
    tm = _tile(S)

    def body(x_ref, mod_ref, g_ref, w_ref, h_ref, *o_refs):
        h = _rms_mod(x_ref[...], g_ref[...], mod_ref[1:2, :], mod_ref[0:1, :]).astype(BF16)
        h_ref[...] = h
        full = _dot_nt(h, w_ref[...])
        for o_ref, lo, hi in zip(o_refs, IN_CUTS[:-1], IN_CUTS[1:]):
            o_ref[...] = full[:, lo:hi]

    return pl.pallas_call(
        body, name="inproj_fwd", grid=(B, S // tm),
        out_shape=[jax.ShapeDtypeStruct((B, S, D), BF16)] + [jax.ShapeDtypeStruct((B, S, w), F32) for w in IN_WIDTHS],
        in_specs=[_rows(tm, D), _perb(6, D), _full((1, D)), _resident(w_t.shape)],
        out_specs=[_rows(tm, D)] + [_rows(tm, w) for w in IN_WIDTHS],
        compiler_params=_cparams(("parallel", "arbitrary")),
    )(x, mod, g1, w_t)


def _inproj_bwd(x, mod, g1, dx1, dps, w_t):
    B, S, _ = x.shape
    tm = _tile(S)
    n = len(dps)

    def body(x_ref, mod_ref, g_ref, dx1_ref, *refs):
        dp_refs, w_ref = refs[:n], refs[n]
        dblk_ref, gx_ref, dg_ref, dsc_ref, dsh_ref = refs[n + 1:]
        b, i = pl.program_id(0), pl.program_id(1)
        full = jnp.concatenate([r[...].astype(F32) for r in dp_refs], axis=1)
        for j in range(N_DEV):
            dblk_ref[j] = full[:, IN_SHARD * j:IN_SHARD * (j + 1)].astype(BF16)
        dh = jnp.dot(full.astype(BF16), w_ref[...], preferred_element_type=F32)
        _, vjp = jax.vjp(_rms_mod, x_ref[...], g_ref[...], mod_ref[1:2, :], mod_ref[0:1, :])
        dx, dg, dsc, dsh = vjp(dh)
        gx_ref[...] = dx1_ref[...] + dx

        @pl.when((b == 0) & (i == 0))
        def _():
            dg_ref[...] = jnp.zeros_like(dg_ref)

        @pl.when(i == 0)
        def _():
            dsc_ref[...] = jnp.zeros_like(dsc_ref)
            dsh_ref[...] = jnp.zeros_like(dsh_ref)

        dg_ref[...] += dg
        dsc_ref[...] += dsc
        dsh_ref[...] += dsh

    return pl.pallas_call(
        body, name="inproj_bwd", grid=(B, S // tm),
        out_shape=[jax.ShapeDtypeStruct((B, N_DEV, S, IN_SHARD), BF16), jax.ShapeDtypeStruct((B, S, D), F32),
                   jax.ShapeDtypeStruct((1, D), F32), jax.ShapeDtypeStruct((B, 1, D), F32),
                   jax.ShapeDtypeStruct((B, 1, D), F32)],
        in_specs=[_rows(tm, D), _perb(6, D), _full((1, D)), _rows(tm, D)]
                 + [_rows(tm, w) for w in IN_WIDTHS] + [_resident(w_t.shape)],
        out_specs=[pl.BlockSpec((None, N_DEV, tm, IN_SHARD), lambda b, i: (b, 0, i, 0)), _rows(tm, D),
                   _full((1, D)), _perb(1, D), _perb(1, D)],
        compiler_params=_cparams(("arbitrary", "arbitrary")),
    )(x, mod, g1, dx1, *dps, w_t)


def _wgrad(a, b, name, after=None, b_lanes=None):
    B, na, S, K = a.shape
    nb, N = b.shape[1], b.shape[3]
    lane_blk = 0
    if b_lanes is not None:
        lane_blk, N = b_lanes
    G = max(na, nb)
    tm = min(4096, S)
    nt = S // tm
    last = B * nt - 1

    def body(a_ref, b_ref, *rest):
        o_ref, acc = rest[-2:]
        t = pl.program_id(1)

        @pl.when(t == 0)
        def _():
            acc[...] = jnp.zeros_like(acc)

        acc[...] += lax.dot_general(a_ref[...], b_ref[...], (((0,), (0,)), ((), ())), preferred_element_type=F32)

        @pl.when(t == last)
        def _():
            o_ref[...] = acc[...].astype(BF16)

    return pl.pallas_call(
        body, name=name, grid=(G, B * nt),
        out_shape=jax.ShapeDtypeStruct((G, K, N), BF16),
        in_specs=[pl.BlockSpec((None, None, tm, K), lambda g, t: (t // nt, g if na > 1 else 0, t % nt, 0)),
                  pl.BlockSpec((None, None, tm, N), lambda g, t: (t // nt, g if nb > 1 else 0, t % nt, lane_blk))]
                 + ([] if after is None else [pl.BlockSpec(memory_space=pl.ANY)]),
        out_specs=pl.BlockSpec((None, K, N), lambda g, t: (g, 0, 0)),
        scratch_shapes=[pltpu.VMEM((K, N), F32)],
        compiler_params=_cparams(("parallel", "arbitrary")),
    )(*((a, b) if after is None else (a, b, after)))


LANES = 128


def _attn_consts():
    inv_freq = THETA ** (-jnp.arange(0, ROT, 2, dtype=F32) / ROT)
    head = jnp.concatenate([inv_freq, inv_freq, jnp.zeros((HD - ROT,), F32)])
    invf = jnp.tile(head, LANES // HD)[None, :]
    mean_of = lambda w: jnp.asarray(np.kron(np.eye(w // HD), np.full((HD, HD), 1.0 / HD)), BF16)
    return invf, mean_of(QW), mean_of(KVW)


def _rope_tables_side(pos, invf):
    B, S, _ = pos.shape
    tr = min(512, S)

    def fn(ins, outs):
        p_ref, f_ref = ins
        c_ref, s_ref = outs
        for b in range(B):
            for r in range(0, S, tr):
                ang = p_ref[b, r:r + tr, :].astype(F32) * f_ref[...]
                c_ref[b, r:r + tr, :] = jnp.cos(ang)
                s_ref[b, r:r + tr, :] = jnp.sin(ang)

    sd = jax.ShapeDtypeStruct((B, S, LANES), F32)
    return fn, (pos, invf), (sd, sd)


def _rope_expand(cos, sin, reps):
    lane = lax.broadcasted_iota(jnp.int32, cos.shape, 1) % HD
    sa = jnp.where((lane >= ROT // 2) & (lane < ROT), sin, 0.0)
    sb = jnp.where(lane < ROT // 2, -sin, 0.0)
    rep = lambda t: jnp.concatenate([t] * reps, axis=1) if reps > 1 else t
    return rep(cos), rep(sa), rep(sb)


@jax.custom_vjp
def _rope(t, cos, sa, sb):
    w = t.shape[1]
    return t * cos + pltpu.roll(t, ROT // 2, 1) * sa + pltpu.roll(t, w - ROT // 2, 1) * sb


def _rope_fwd(t, cos, sa, sb):
    return _rope(t, cos, sa, sb), (cos, sa, sb)


def _rope_bwd(res, d):
    cos, sa, sb = res
    w = d.shape[1]
    dt = d * cos + pltpu.roll(d * sa, w - ROT // 2, 1) + pltpu.roll(d * sb, ROT // 2, 1)
    return dt, jnp.zeros_like(cos), jnp.zeros_like(sa), jnp.zeros_like(sb)


_rope.defvjp(_rope_fwd, _rope_bwd)


def _head_norm(t, g, mean_of):
    hi, lo = _split(t * t)
    ms = jnp.dot(hi, mean_of, preferred_element_type=F32) + jnp.dot(lo, mean_of, preferred_element_type=F32)
    return t * lax.rsqrt(ms + EPS) * g


def _attn_block(q, kvp, kvc, qg, kg, sinks, tq, tk, mq, mk, valid):
    qn = _rope(_head_norm(q, jnp.concatenate([qg] * HQ, axis=1), mq), *tq) * (HD ** -0.5)
    kv = jnp.concatenate([kvp, kvc], axis=0)
    kn = _rope(_head_norm(kv[:, 0:KVW], jnp.concatenate([kg] * HKV, axis=1), mk), *tk)
    per_tile = LANES // HD
    vT = jnp.transpose(kv[:, KVW:2 * KVW])
    qT = [jnp.transpose(qn[:, LANES * t:LANES * (t + 1)]) for t in range(QW // LANES)]
    head_T = lambda h: qT[h // per_tile][HD * (h % per_tile):HD * (h % per_tile + 1), :]
    none = jnp.zeros((HD, GRP * BLK), F32)
    o_T = []
    for j in range(HKV):
        q4T = jnp.concatenate([head_T(GRP * j + i) for i in range(GRP)], axis=1)
        sT = _dot(kn, jnp.concatenate([q4T, none] if j == 0 else [none, q4T], axis=0))
        sT = jnp.where(valid, sT, -1e30)
        sink = jnp.concatenate([jnp.broadcast_to(sinks[:, GRP * j + i:GRP * j + i + 1], (1, BLK)) for i in range(GRP)], axis=1)
        m = lax.stop_gradient(jnp.maximum(jnp.max(sT, axis=0, keepdims=True), sink))
        pT = jnp.exp(sT - m)
        den = jnp.sum(pT, axis=0, keepdims=True) + jnp.exp(sink - m)
        oT = _dot(vT[HD * j:HD * (j + 1), :], pT) * (1.0 / den)
        o_T += [oT[:, BLK * i:BLK * (i + 1)] for i in range(GRP)]
    return jnp.concatenate([jnp.transpose(jnp.concatenate(o_T[per_tile * t:per_tile * (t + 1)], axis=0))
                            for t in range(QW // LANES)], axis=1)


def _attn_tables(cp_ref, cc_ref, sp_ref, sc_ref, n):
    tq = _rope_expand(cc_ref[...], sc_ref[...], QW // LANES)
    tk = _rope_expand(jnp.concatenate([cp_ref[...], cc_ref[...]], axis=0),
                      jnp.concatenate([sp_ref[...], sc_ref[...]], axis=0), KVW // LANES)
    qi = lax.broadcasted_iota(jnp.int32, (2 * BLK, GRP * BLK), 1) % BLK + BLK
    kj = lax.broadcasted_iota(jnp.int32, (2 * BLK, GRP * BLK), 0)
    dist = qi - kj
    valid = (dist >= 0) & (dist < BLK) & ((kj >= BLK) | (n > 0))
    return tq, tk, valid


def _attn_fwd(aq, akv, cos, sin, qg, kg, sinks, mq, mk):
    B, S, _ = aq.shape
    nb = S // BLK

    def body(q_ref, kvp_ref, kvc_ref, cp_ref, cc_ref, sp_ref, sc_ref, qg_ref, kg_ref, sk_ref, mq_ref, mk_ref, o_ref):
        tq, tk, valid = _attn_tables(cp_ref, cc_ref, sp_ref, sc_ref, pl.program_id(1))
        o_ref[...] = _attn_block(q_ref[...], kvp_ref[...], kvc_ref[...], qg_ref[...], kg_ref[...], sk_ref[...],
                                 tq, tk, mq_ref[...], mk_ref[...], valid)

    prev = lambda b, n: (b, jnp.maximum(n - 1, 0), 0)
    cur = lambda b, n: (b, n, 0)
    return pl.pallas_call(
        body, name="attn_fwd", grid=(B, nb),
        out_shape=jax.ShapeDtypeStruct((B, S, QW), F32),
        in_specs=[pl.BlockSpec((None, BLK, QW), cur), pl.BlockSpec((None, BLK, 2 * KVW), prev),
                  pl.BlockSpec((None, BLK, 2 * KVW), cur), pl.BlockSpec((None, BLK, LANES), prev),
                  pl.BlockSpec((None, BLK, LANES), cur), pl.BlockSpec((None, BLK, LANES), prev),
                  pl.BlockSpec((None, BLK, LANES), cur), _full((1, HD)), _full((1, HD)), _full((1, HQ)),
                  _full((QW, QW)), _full((KVW, KVW))],
        out_specs=pl.BlockSpec((None, BLK, QW), cur),
        compiler_params=_cparams(("parallel", "arbitrary")),
    )(aq, akv, akv, cos, cos, sin, sin, qg, kg, sinks, mq, mk)


def _attn_bwd(aq, akv, cos, sin, qg, kg, sinks, mq, mk, do):
    B, S, _ = aq.shape
    nb = S // BLK

    def body(q_ref, kvp_ref, kvc_ref, cp_ref, cc_ref, sp_ref, sc_ref, qg_ref, kg_ref, sk_ref, mq_ref, mk_ref, do_ref,
             dq_ref, dkv_ref, dqg_ref, dkg_ref, dsk_ref, carry):
        b, i = pl.program_id(0), pl.program_id(1)
        tq, tk, valid = _attn_tables(cp_ref, cc_ref, sp_ref, sc_ref, nb - 1 - i)
        fn = functools.partial(_attn_block, tq=tq, tk=tk, mq=mq_ref[...], mk=mk_ref[...], valid=valid)
        _, vjp = jax.vjp(fn, q_ref[...], kvp_ref[...], kvc_ref[...], qg_ref[...], kg_ref[...], sk_ref[...])
        dq, dkvp, dkvc, dqg, dkg, dsk = vjp(do_ref[...])

        @pl.when(i == 0)
        def _():
            carry[...] = jnp.zeros_like(carry)

        @pl.when((b == 0) & (i == 0))
        def _():
            dqg_ref[...] = jnp.zeros_like(dqg_ref)
            dkg_ref[...] = jnp.zeros_like(dkg_ref)
            dsk_ref[...] = jnp.zeros_like(dsk_ref)

        dq_ref[...] = dq.astype(BF16)
        dkv_ref[...] = (dkvc + carry[...]).astype(BF16)
        carry[...] = dkvp
        dqg_ref[...] += dqg
        dkg_ref[...] += dkg
        dsk_ref[...] += dsk

    prev = lambda b, i: (b, jnp.maximum(nb - 2 - i, 0), 0)
    cur = lambda b, i: (b, nb - 1 - i, 0)
    return pl.pallas_call(
        body, name="attn_bwd", grid=(B, nb),
        out_shape=[jax.ShapeDtypeStruct((B, S, QW), BF16), jax.ShapeDtypeStruct((B, S, 2 * KVW), BF16),
                   jax.ShapeDtypeStruct((1, HD), F32), jax.ShapeDtypeStruct((1, HD), F32),
                   jax.ShapeDtypeStruct((1, HQ), F32)],
        in_specs=[pl.BlockSpec((None, BLK, QW), cur), pl.BlockSpec((None, BLK, 2 * KVW), prev),
                  pl.BlockSpec((None, BLK, 2 * KVW), cur), pl.BlockSpec((None, BLK, LANES), prev),
                  pl.BlockSpec((None, BLK, LANES), cur), pl.BlockSpec((None, BLK, LANES), prev),
                  pl.BlockSpec((None, BLK, LANES), cur), _full((1, HD)), _full((1, HD)), _full((1, HQ)),
                  _full((QW, QW)), _full((KVW, KVW)), pl.BlockSpec((None, BLK, QW), cur)],
        out_specs=[pl.BlockSpec((None, BLK, QW), cur), pl.BlockSpec((None, BLK, 2 * KVW), cur),
                   _full((1, HD)), _full((1, HD)), _full((1, HQ))],
        scratch_shapes=[pltpu.VMEM((BLK, 2 * KVW), F32)],
        compiler_params=_cparams(("arbitrary", "arbitrary")),
    )(aq, akv, akv, cos, cos, sin, sin, qg, kg, sinks, mq, mk, do)


def _conv_taps(xe, w, rows):
    y = None
    for j in range(CONV):
        sh = pltpu.roll(xe, CONV - 1 - j, 0)[8:8 + rows, :] if j < CONV - 1 else xe[8:8 + rows, :]
        y = sh * w[j:j + 1, :] if y is None else y + sh * w[j:j + 1, :]
    return y


def _softplus(x):
    return jnp.maximum(x, 0.0) + jnp.log1p(jnp.exp(-jnp.abs(x)))


_BMM = (((2,), (1,)), ((0,), (0,)))
_BMM_NT = (((2,), (2,)), ((0,), (0,)))
_BMM_TN = (((1,), (1,)), ((0,), (0,)))


def _bmm(a, b, dims=_BMM):
    return lax.dot_general(a.astype(BF16), b.astype(BF16), dims, preferred_element_type=F32)


def _split(a):
    hi = a.astype(BF16)
    return hi, (a - hi.astype(F32)).astype(BF16)


def _bmm3(a, b, dims=_BMM):
    ah, al = _split(a)
    bh, bl = _split(b)
    d = lambda p, q: lax.dot_general(p, q, dims, preferred_element_type=F32)
    return d(ah, bh) + (d(ah, bl) + d(al, bh))


TRI_BASE = 8


def _tri_inverse(L):
    ii = lax.broadcasted_iota(jnp.int32, (CH, CH), 0)
    jj = lax.broadcasted_iota(jnp.int32, (CH, CH), 1)
    same = lambda size: (ii // size) == (jj // size)
    diag = jnp.where(same(TRI_BASE), L, 0.0)
    X = (ii == jj).astype(F32) - diag
    P = diag
    n = 2
    while n < TRI_BASE:
        P = _bmm3(P, P)
        X = X + _bmm3(X, P)
        n *= 2
    size = TRI_BASE
    while size < CH:
        joint = jnp.where(same(2 * size) & jnp.logical_not(same(size)), L, 0.0)
        X = X - _bmm3(X, _bmm3(joint, X))
        size *= 2
    return X


@jax.custom_vjp
def _tri_inverse_known(L, T):
    return T


def _tri_inverse_known_fwd(L, T):
    return T, T


def _tri_inverse_known_bwd(T, dT):
    Tt = jnp.swapaxes(T, 1, 2)
    return -_bmm(Tt, _bmm(dT, Tt)), jnp.zeros_like(T)


_tri_inverse_known.defvjp(_tri_inverse_known_fwd, _tri_inverse_known_bwd)


def _triangle(n, upper):
    ii = lax.broadcasted_iota(jnp.int32, (n, CH, CH), 1)
    jj = lax.broadcasted_iota(jnp.int32, (n, CH, CH), 2)
    return ((ii <= jj) if upper else (ii >= jj)).astype(BF16)


@jax.custom_vjp
def _cumsum_rows(g):
    g0 = g.astype(BF16)
    r1 = g - g0.astype(F32)
    g1 = r1.astype(BF16)
    g2 = (r1 - g1.astype(F32)).astype(BF16)
    tri = _triangle(g.shape[0], False)
    d = lambda q: lax.dot_general(tri, q, _BMM, preferred_element_type=F32)
    return d(g0) + (d(g1) + d(g2))


def _cumsum_rows_fwd(g):
    return _cumsum_rows(g), None


def _cumsum_rows_bwd(_, dy):
    hi, lo = _split(dy)
    tri = _triangle(dy.shape[0], True)
    d = lambda q: lax.dot_general(tri, q, _BMM, preferred_element_type=F32)
    return (d(hi) + d(lo),)


_cumsum_rows.defvjp(_cumsum_rows_fwd, _cumsum_rows_bwd)


def _row_sums(t):
    n, r, w = t.shape
    hi, lo = _split(t.reshape(n * r, w))
    ones = jnp.ones((w, w), BF16)
    s = jnp.dot(hi, ones, preferred_element_type=F32) + jnp.dot(lo, ones, preferred_element_type=F32)
    return s.reshape(n, r, w)


def _dn_prep(t_known, qr, kr, v, a_raw, b_raw, a_log, dt_b):
    n = qr.shape[0]
    ii = lax.broadcasted_iota(jnp.int32, (n, CH, CH), 1)
    jj = lax.broadcasted_iota(jnp.int32, (n, CH, CH), 2)
    incl, strict = ii >= jj, ii > jj
    q = qr * lax.rsqrt(_row_sums(qr * qr) + EPS) * (DK ** -0.5)
    k = kr * lax.rsqrt(_row_sums(kr * kr) + EPS)
    beta = _sigmoid(b_raw)
    g = -jnp.exp(a_log) * _softplus(a_raw + dt_b)
    gcb = _cumsum_rows(jnp.broadcast_to(g, (n, CH, DK)))
    gc = gcb[:, :, 0:1]
    gc_row = jnp.swapaxes(gcb, 1, 2)[:, 0:1, 0:CH]
    decay = jnp.where(incl, jnp.exp(jnp.where(incl, gc - gc_row, 0.0)), 0.0)
    kb = k * beta
    L = jnp.where(strict, _bmm(kb, k, _BMM_NT) * decay, 0.0)
    T = _tri_inverse(L) if t_known is None else _tri_inverse_known(L, t_known)
    eg = jnp.exp(gc)
    u = _bmm(T, v * beta)
    w = _bmm(T, kb * eg)
    a_in = _bmm(q, k, _BMM_NT) * decay
    g_last = gc[:, CH - 1:CH, :]
    return u, w, q * eg, k * jnp.exp(g_last - gc), a_in, jnp.exp(g_last), T


def _dn_step(S0, u, w, qd, kd, a_in, cd):
    r = _bmm(jnp.concatenate([w, qd], axis=1), S0)
    v_new = u - r[:, 0:CH, :]
    o = r[:, CH:2 * CH, :] + _bmm(a_in, v_new)
    S1 = S0 * cd + _bmm(kd, v_new, _BMM_TN)
    return o, S1


def _dn_stack(cq, ba, al, dt, G):
    cols = [[] for _ in range(7)]
    for c in range(G):
        rows = slice(CH * c, CH * (c + 1))
        for h in range(DH):
            parts = (cq[rows, DK * h:DK * (h + 1)], cq[rows, DNW + DK * h:DNW + DK * (h + 1)],
                     cq[rows, 2 * DNW + DK * h:2 * DNW + DK * (h + 1)], ba[rows, DH + h:DH + h + 1],
                     ba[rows, h:h + 1], al[:, h:h + 1], dt[:, h:h + 1])
            for col, p in zip(cols, parts):
                col.append(p)
    return tuple(jnp.stack(col) for col in cols)


def _dn_group(S, want):
    g = want
    while (S // CH) % g:
        g //= 2
    return g


def _dn_prep_fwd(xin, conv_w, ba, a_log, dt_b):
    B, S, _ = xin.shape
    nc = S // CH
    G = _dn_group(S, 8)
    r8 = G * CH // 8

    def body(xp_ref, x_ref, cw_ref, ba_ref, al_ref, dt_ref, cq_ref, u_ref, w_ref, qd_ref, kd_ref, a_ref, t_ref, cd_ref):
        xp = jnp.where(pl.program_id(1) > 0, xp_ref[...], 0.0)
        cq = _silu(_conv_taps(jnp.concatenate([xp, x_ref[...]], axis=0), cw_ref[...], G * CH))
        cq_ref[...] = cq
        ops = _dn_stack(cq, ba_ref[...], al_ref[...], dt_ref[...], G)
        u, w, qd, kd, a_in, cd, T = _dn_prep(None, *ops)
        lane4 = lax.broadcasted_iota(jnp.int32, (1, DH), 1)
        for c in range(G):
            rows = slice(CH * c, CH * (c + 1))
            cdrow = jnp.zeros((1, DH), F32)
            for h in range(DH):
                n = DH * c + h
                lanes = slice(DK * h, DK * (h + 1))
                u_ref[rows, lanes] = u[n]
                w_ref[rows, lanes] = w[n]
                qd_ref[rows, lanes] = qd[n]
                kd_ref[rows, lanes] = kd[n]
                a_ref[rows, CH * h:CH * (h + 1)] = a_in[n]
                t_ref[rows, CH * h:CH * (h + 1)] = T[n]
                cdrow = cdrow + jnp.where(lane4 == h, cd[n], 0.0)
            cd_ref[c] = cdrow

    wide = jax.ShapeDtypeStruct((B, S, DNW), F32)
    sq = jax.ShapeDtypeStruct((B, S, DH * CH), F32)
    return pl.pallas_call(
        body, name="dn_prep_fwd", grid=(B, nc // G),
        out_shape=[jax.ShapeDtypeStruct((B, S, CONVW), F32), wide, wide, wide, wide, sq, sq,
                   jax.ShapeDtypeStruct((B, nc, 1, DH), F32)],
        in_specs=[pl.BlockSpec((None, 8, CONVW), lambda b, i: (b, jnp.maximum(i * r8 - 1, 0), 0)),
                  _rows(G * CH, CONVW), _full((CONV, CONVW)), _rows(G * CH, 2 * DH), _full((1, DH)), _full((1, DH))],
        out_specs=[_rows(G * CH, CONVW)] + [_rows(G * CH, DNW)] * 4 + [_rows(G * CH, DH * CH)] * 2
                  + [pl.BlockSpec((None, G, 1, DH), lambda b, i: (b, i, 0, 0))],
        compiler_params=_cparams(("parallel", "arbitrary")),
    )(xin, xin, conv_w, ba, a_log, dt_b)


def _dn_seq_specs(B, steps, gs, rev):
    at = (lambda i: steps - 1 - i) if rev else (lambda i: i)
    wide = pl.BlockSpec((B, gs * CH, DNW), lambda i: (0, at(i), 0))
    a_spec = pl.BlockSpec((B, gs * CH, DH * CH), lambda i: (0, at(i), 0))
    cd_spec = pl.BlockSpec((B, gs, 1, DH), lambda i: (0, at(i), 0, 0))
    st_spec = pl.BlockSpec((B, gs, DH, DK, DK), lambda i: (0, at(i), 0, 0, 0))
    return wide, a_spec, cd_spec, st_spec


def _dn_step_operands(B, c, u_ref, w_ref, qd_ref, kd_ref, a_ref, cd_ref):
    pairs = [(b, h) for b in range(B) for h in range(DH)]
    rows = slice(CH * c, CH * (c + 1))
    wide = lambda ref: jnp.stack([ref[b, rows, DK * h:DK * (h + 1)] for b, h in pairs])
    a_in = jnp.stack([a_ref[b, rows, CH * h:CH * (h + 1)] for b, h in pairs])
    cd = jnp.stack([cd_ref[b, c, :, h:h + 1] for b, h in pairs])
    return wide(u_ref), wide(w_ref), wide(qd_ref), wide(kd_ref), a_in, cd


def _dn_seq_fwd(u, w, qd, kd, a_in, cd):
    B, S, _ = u.shape
    nc = S // CH
    gs = _dn_group(S, 8)

    def body(u_ref, w_ref, qd_ref, kd_ref, a_ref, cd_ref, o_ref, st_ref, state):
        @pl.when(pl.program_id(0) == 0)
        def _():
            state[...] = jnp.zeros_like(state)

        S0 = state[...]
        for c in range(gs):
            for b in range(B):
                st_ref[b, c] = S0[DH * b:DH * (b + 1)]
            o, S0 = _dn_step(S0, *_dn_step_operands(B, c, u_ref, w_ref, qd_ref, kd_ref, a_ref, cd_ref))
            for b in range(B):
                for h in range(DH):
                    o_ref[b, CH * c:CH * (c + 1), DK * h:DK * (h + 1)] = o[DH * b + h]
        state[...] = S0

    wide, a_spec, cd_spec, st_spec = _dn_seq_specs(B, nc // gs, gs, False)
    return pl.pallas_call(
        body, name="dn_seq_fwd", grid=(nc // gs,),
        out_shape=[jax.ShapeDtypeStruct((B, S, DNW), F32), jax.ShapeDtypeStruct((B, nc, DH, DK, DK), F32)],
        in_specs=[wide, wide, wide, wide, a_spec, cd_spec],
        out_specs=[wide, st_spec],
        scratch_shapes=[pltpu.VMEM((B * DH, DK, DK), F32)],
        compiler_params=_cparams(("arbitrary",)),
    )(u, w, qd, kd, a_in, cd)


def _dn_seq_bwd(u, w, qd, kd, a_in, cd, states, do):
    B, S, _ = u.shape
    nc = S // CH
    gs = _dn_group(S, 8)

    def body(u_ref, w_ref, qd_ref, kd_ref, a_ref, cd_ref, st_ref, do_ref,
             du_ref, dw_ref, dqd_ref, dkd_ref, da_ref, dcd_ref, dstate):
        @pl.when(pl.program_id(0) == 0)
        def _():
            dstate[...] = jnp.zeros_like(dstate)

        lane4 = lax.broadcasted_iota(jnp.int32, (1, DH), 1)
        dS = dstate[...]
        for c in reversed(range(gs)):
            rows = slice(CH * c, CH * (c + 1))
            S0 = jnp.concatenate([st_ref[b, c] for b in range(B)], axis=0)
            do = jnp.stack([do_ref[b, rows, DK * h:DK * (h + 1)] for b in range(B) for h in range(DH)])
            _, vjp = jax.vjp(_dn_step, S0, *_dn_step_operands(B, c, u_ref, w_ref, qd_ref, kd_ref, a_ref, cd_ref))
            dS, du, dw, dqd, dkd, da, dcd = vjp((do, dS))
            for b in range(B):
                dcdrow = jnp.zeros((1, DH), F32)
                for h in range(DH):
                    n = DH * b + h
                    lanes = slice(DK * h, DK * (h + 1))
                    du_ref[b, rows, lanes] = du[n]
                    dw_ref[b, rows, lanes] = dw[n]
                    dqd_ref[b, rows, lanes] = dqd[n]
                    dkd_ref[b, rows, lanes] = dkd[n]
                    da_ref[b, rows, CH * h:CH * (h + 1)] = da[n]
                    dcdrow = dcdrow + jnp.where(lane4 == h, dcd[n], 0.0)
                dcd_ref[b, c] = dcdrow
        dstate[...] = dS

    wide, a_spec, cd_spec, st_spec = _dn_seq_specs(B, nc // gs, gs, True)
    sd = jax.ShapeDtypeStruct((B, S, DNW), F32)
    return pl.pallas_call(
        body, name="dn_seq_bwd", grid=(nc // gs,),
        out_shape=[sd, sd, sd, sd, jax.ShapeDtypeStruct((B, S, DH * CH), F32), jax.ShapeDtypeStruct((B, nc, 1, DH), F32)],
        in_specs=[wide, wide, wide, wide, a_spec, cd_spec, st_spec, wide],
        out_specs=[wide, wide, wide, wide, a_spec, cd_spec],
        scratch_shapes=[pltpu.VMEM((B * DH, DK, DK), F32)],
        compiler_params=_cparams(("arbitrary",)),
    )(u, w, qd, kd, a_in, cd, states, do)


def _dn_prep_bwd(xin, conv_w, cq, ba, a_log, dt_b, t_inv, du, dw, dqd, dkd, da, dcd):
    B, S, _ = cq.shape
    nc = S // CH
    G = _dn_group(S, 8)
    R = G * CH
    nblk = nc // G
    r8 = R // 8

    def body(xp_ref, x_ref, cw_ref, cq_ref, ba_ref, al_ref, dt_ref, t_ref, du_ref, dw_ref, dqd_ref, dkd_ref, da_ref, dcd_ref,
             dx_ref, dcw_ref, dba_ref, dal_ref, ddt_ref, carry):
        i = pl.program_id(1)

        @pl.when((pl.program_id(0) == 0) & (i == 0))
        def _():
            dal_ref[...] = jnp.zeros_like(dal_ref)
            ddt_ref[...] = jnp.zeros_like(ddt_ref)
            dcw_ref[...] = jnp.zeros_like(dcw_ref)

        @pl.when(i == 0)
        def _():
            carry[...] = jnp.zeros_like(carry)

        pairs = [(c, h) for c in range(G) for h in range(DH)]
        rows = lambda c: slice(CH * c, CH * (c + 1))
        wide = lambda ref: jnp.stack([ref[rows(c), DK * h:DK * (h + 1)] for c, h in pairs])
        square = lambda ref: jnp.stack([ref[rows(c), CH * h:CH * (h + 1)] for c, h in pairs])
        ops = _dn_stack(cq_ref[...], ba_ref[...], al_ref[...], dt_ref[...], G)
        cots = (wide(du_ref), wide(dw_ref), wide(dqd_ref), wide(dkd_ref), square(da_ref),
                jnp.stack([dcd_ref[c][:, h:h + 1] for c, h in pairs]), jnp.zeros((len(pairs), CH, CH), F32))
        _, vjp = jax.vjp(functools.partial(_dn_prep, square(t_ref)), *ops)
        dq, dk, dv, dar, dbr, dl, dd = vjp(cots)
        lane8 = lax.broadcasted_iota(jnp.int32, (CH, 2 * DH), 1)
        lane4 = lax.broadcasted_iota(jnp.int32, (1, DH), 1)
        dal = jnp.zeros((1, DH), F32)
        ddt = jnp.zeros((1, DH), F32)
        for c in range(G):
            dba = jnp.zeros((CH, 2 * DH), F32)
            for h in range(DH):
                n = DH * c + h
                dba = dba + jnp.where(lane8 == h, dbr[n], 0.0) + jnp.where(lane8 == DH + h, dar[n], 0.0)
                dal = dal + jnp.where(lane4 == h, dl[n], 0.0)
                ddt = ddt + jnp.where(lane4 == h, dd[n], 0.0)
            dba_ref[rows(c), :] = dba.astype(BF16)
        dal_ref[...] += dal
        ddt_ref[...] += ddt

        dcq = jnp.concatenate([jnp.concatenate([t[DH * c + h] for t in (dq, dk, dv) for h in range(DH)], axis=1)
                               for c in range(G)], axis=0)
        w = cw_ref[...]
        xp = jnp.where(i < nblk - 1, xp_ref[...], 0.0)
        xe = jnp.concatenate([xp, x_ref[...]], axis=0)
        taps = [(pltpu.roll(xe, CONV - 1 - j, 0) if j < CONV - 1 else xe)[8:8 + R, :] for j in range(CONV)]
        pre = sum(t * w[j:j + 1, :] for j, t in enumerate(taps))
        sg = _sigmoid(pre)
        dpre = dcq * (sg * (1.0 + pre * (1.0 - sg)))
        ext = jnp.concatenate([dpre, carry[...]], axis=0)
        dx = dpre * w[CONV - 1:CONV, :]
        for j in range(CONV - 1):
            dx = dx + pltpu.roll(ext, R + 8 - (CONV - 1 - j), 0)[0:R, :] * w[j:j + 1, :]
        dx_ref[...] = dx.astype(BF16)
        carry[...] = dpre[0:8, :]
        lane_row = lax.broadcasted_iota(jnp.int32, (CONV, CONVW), 0)
        dcw = jnp.zeros((CONV, CONVW), F32)
        for j in range(CONV):
            dcw = dcw + jnp.where(lane_row == j, jnp.sum(taps[j] * dpre, axis=0, keepdims=True), 0.0)
        dcw_ref[...] += dcw

    rev = lambda w: pl.BlockSpec((None, R, w), lambda b, i: (b, nblk - 1 - i, 0))
    return pl.pallas_call(
        body, name="dn_prep_bwd", grid=(B, nblk),
        out_shape=[jax.ShapeDtypeStruct((B, S, CONVW), BF16), jax.ShapeDtypeStruct((CONV, CONVW), F32),
                   jax.ShapeDtypeStruct((B, S, 2 * DH), BF16), jax.ShapeDtypeStruct((1, DH), F32),
                   jax.ShapeDtypeStruct((1, DH), F32)],
        in_specs=[pl.BlockSpec((None, 8, CONVW), lambda b, i: (b, jnp.maximum((nblk - 1 - i) * r8 - 1, 0), 0)),
                  rev(CONVW), _full((CONV, CONVW)), rev(CONVW), rev(2 * DH), _full((1, DH)), _full((1, DH)), rev(DH * CH)]
                 + [rev(DNW)] * 4 + [rev(DH * CH), pl.BlockSpec((None, G, 1, DH), lambda b, i: (b, nblk - 1 - i, 0, 0))],
        out_specs=[rev(CONVW), _full((CONV, CONVW)), rev(2 * DH), _full((1, DH)), _full((1, DH))],
        scratch_shapes=[pltpu.VMEM((8, CONVW), F32)],
        compiler_params=_cparams(("arbitrary", "arbitrary")),
    )(xin, xin, conv_w, cq, ba, a_log, dt_b, t_inv, du, dw, dqd, dkd, da, dcd)


def _gated_norm(o, z, g):
    outs = []
    for h in range(DH):
        t = o[:, DK * h:DK * (h + 1)]
        r = lax.rsqrt(jnp.mean(t * t, axis=-1, keepdims=True) + EPS)
        outs.append(t * r * g * _silu(z[:, DK * h:DK * (h + 1)]))
    return jnp.concatenate(outs, axis=1)


def _mix_fwd(x, o_attn, o_dn, z, ga, gd, mod, dn_g, w_branch, w_out):
    B, S, _ = x.shape
    tm = _tile(S, 512)

    def body(x_ref, oa_ref, od_ref, z_ref, ga_ref, gd_ref, mod_ref, g_ref, wb_ref, wo_ref,
             x1_ref, mix_ref, mg_ref, ob_ref):
        oa = oa_ref[...].astype(BF16)
        od = _gated_norm(od_ref[...], z_ref[...], g_ref[...]).astype(BF16)
        ob_ref[0] = oa
        ob_ref[1] = od
        ya = jnp.dot(oa, wb_ref[0:QW, :], preferred_element_type=F32)
        yd = jnp.dot(od, wb_ref[QW:QW + DNW, :], preferred_element_type=F32)
        merged = (_sigmoid(ga_ref[...]) * ya + _sigmoid(gd_ref[...]) * yd).astype(BF16)
        mg_ref[...] = merged
        mix = jnp.dot(merged, wo_ref[...], preferred_element_type=F32)
        mix_ref[...] = mix
        x1_ref[...] = x_ref[...] + mod_ref[2:3, :] * mix

    return pl.pallas_call(
        body, name="mix_fwd", grid=(B, S // tm),
        out_shape=[jax.ShapeDtypeStruct((B, S, D), F32), jax.ShapeDtypeStruct((B, S, D), F32),
                   jax.ShapeDtypeStruct((B, S, D), BF16), jax.ShapeDtypeStruct((B, 2, S, QW), BF16)],
        in_specs=[_rows(tm, D), _rows(tm, QW), _rows(tm, DNW), _rows(tm, DNW), _rows(tm, D), _rows(tm, D),
                  _perb(6, D), _full((1, DK)), _resident(w_branch.shape), _resident(w_out.shape)],
        out_specs=[_rows(tm, D), _rows(tm, D), _rows(tm, D), _stacked(2, tm, QW)],
        compiler_params=_cparams(("parallel", "arbitrary")),
    )(x, o_attn, o_dn, z, ga, gd, mod, dn_g, w_branch, w_out)


def _mix_bwd(dx1, mix, o_attn, o_dn, z, ga, gd, mod, dn_g, w_branch, w_out):
    B, S, _ = dx1.shape
    tm = _tile(S, 512)

    def body(dx1_ref, mix_ref, oa_ref, od_ref, z_ref, ga_ref, gd_ref, mod_ref, g_ref, wb_ref, wo_ref,
             dmix_ref, dyo_ref, dga_ref, dgd_ref, dz_ref, doa_ref, dod_ref, dgate_ref, dg_ref):
        b, i = pl.program_id(0), pl.program_id(1)
        dx1 = dx1_ref[...]
        dmix = (dx1 * mod_ref[2:3, :]).astype(BF16)
        dmix_ref[...] = dmix
        dgate = jnp.sum(dx1 * mix_ref[...], axis=0, keepdims=True)
        dmerged = _dot_nt(dmix, wo_ref[...])
        odn, gn_vjp = jax.vjp(_gated_norm, od_ref[...], z_ref[...], g_ref[...])
        ya = _dot(oa_ref[...], wb_ref[0:QW, :])
        yd = _dot(odn, wb_ref[QW:QW + DNW, :])
        sa, sd = _sigmoid(ga_ref[...]), _sigmoid(gd_ref[...])
        dya = (dmerged * sa).astype(BF16)
        dyd = (dmerged * sd).astype(BF16)
        dyo_ref[0] = dya
        dyo_ref[1] = dyd
        dga_ref[...] = (dmerged * ya * sa * (1.0 - sa)).astype(BF16)
        dgd_ref[...] = (dmerged * yd * sd * (1.0 - sd)).astype(BF16)
        doa_ref[...] = _dot_nt(dya, wb_ref[0:QW, :])
        dodn = _dot_nt(dyd, wb_ref[QW:QW + DNW, :])
        dod, dz, dg = gn_vjp(dodn)
        dod_ref[...] = dod
        dz_ref[...] = dz.astype(BF16)

        @pl.when(i == 0)
        def _():
            dgate_ref[...] = jnp.zeros_like(dgate_ref)

        @pl.when((b == 0) & (i == 0))
        def _():
            dg_ref[...] = jnp.zeros_like(dg_ref)

        dgate_ref[...] += dgate
        dg_ref[...] += dg

    return pl.pallas_call(
        body, name="mix_bwd", grid=(B, S // tm),
        out_shape=[jax.ShapeDtypeStruct((B, S, D), BF16), jax.ShapeDtypeStruct((B, 2, S, D), BF16),
                   jax.ShapeDtypeStruct((B, S, D), BF16), jax.ShapeDtypeStruct((B, S, D), BF16),
                   jax.ShapeDtypeStruct((B, S, DNW), BF16),
                   jax.ShapeDtypeStruct((B, S, QW), F32), jax.ShapeDtypeStruct((B, S, DNW), F32),
                   jax.ShapeDtypeStruct((B, 1, D), F32), jax.ShapeDtypeStruct((1, DK), F32)],
        in_specs=[_rows(tm, D), _rows(tm, D), _rows(tm, QW), _rows(tm, DNW), _rows(tm, DNW), _rows(tm, D),
                  _rows(tm, D), _perb(6, D), _full((1, DK)), _resident(w_branch.shape), _resident(w_out.shape)],
        out_specs=[_rows(tm, D), _stacked(2, tm, D), _rows(tm, D), _rows(tm, D), _rows(tm, DNW),
                   _rows(tm, QW), _rows(tm, DNW), _perb(1, D), _full((1, DK))],
        compiler_params=_cparams(("arbitrary", "arbitrary")),
    )(dx1, mix, o_attn, o_dn, z, ga, gd, mod, dn_g, w_branch, w_out)


GU_SHARD = 2 * FFN // N_DEV
GU_HALF = N_DEV // 2


def _ffn1_fwd(x1, mod, g2, w_gu):
    B, S, _ = x1.shape
    tm = _tile(S)

    def body(x_ref, mod_ref, g_ref, w_ref, h_ref, dgate_ref, dup_ref, act_ref):
        h = _rms_mod(x_ref[...], g_ref[...], mod_ref[4:5, :], mod_ref[3:4, :]).astype(BF16)
        h_ref[...] = h
        for j in range(GU_HALF):
            gate = _dot_nt(h, w_ref[j])
            up = _dot_nt(h, w_ref[GU_HALF + j])
            sg = _sigmoid(gate)
            silu = gate * sg
            dgate_ref[j] = up * (sg * (1.0 + gate * (1.0 - sg)))
            dup_ref[j] = silu
            act_ref[j] = (silu * up).astype(BF16)

    blk = lambda dt: jax.ShapeDtypeStruct((B, GU_HALF, S, GU_SHARD), dt)
    return pl.pallas_call(
        body, name="ffn1_fwd", grid=(B, S // tm),
        out_shape=[jax.ShapeDtypeStruct((B, S, D), BF16), blk(F32), blk(F32), blk(BF16)],
        in_specs=[_rows(tm, D), _perb(6, D), _full((1, D)), _resident(w_gu.shape)],
        out_specs=[_rows(tm, D)] + [_stacked(GU_HALF, tm, GU_SHARD)] * 3,
        compiler_params=_cparams(("parallel", "arbitrary")),
    )(x1, mod, g2, w_gu)


def _ffn2_fwd(act, x1, target, mod, w_down):
    B, S, _ = x1.shape
    tm = _tile(S, 512)

    def body(a_ref, x_ref, t_ref, mod_ref, w_ref, dy_ref, loss_ref, dgate_ref):
        b, i = pl.program_id(0), pl.program_id(1)
        y = jnp.dot(a_ref[0], w_ref[0], preferred_element_type=F32)
        for j in range(1, GU_HALF):
            y = y + jnp.dot(a_ref[j], w_ref[j], preferred_element_type=F32)
        err = x_ref[...] + mod_ref[5:6, :] * y - t_ref[...]
        dy = err * (1.0 / D)
        dy_ref[...] = dy

        @pl.when((b == 0) & (i == 0))
        def _():
            loss_ref[...] = jnp.zeros_like(loss_ref)

        @pl.when(i == 0)
        def _():
            dgate_ref[...] = jnp.zeros_like(dgate_ref)

        loss_ref[...] += (0.5 / D) * jnp.sum(err * err)
        dgate_ref[...] += jnp.sum(dy * y, axis=0, keepdims=True)

    return pl.pallas_call(
        body, name="ffn2_fwd", grid=(B, S // tm),
        out_shape=[jax.ShapeDtypeStruct((B, S, D), F32), jax.ShapeDtypeStruct((1, 128), F32),
                   jax.ShapeDtypeStruct((B, 1, D), F32)],
        in_specs=[_stacked(GU_HALF, tm, GU_SHARD), _rows(tm, D), _rows(tm, D), _perb(6, D), _resident(w_down.shape)],
        out_specs=[_rows(tm, D), _full((1, 128)), _perb(1, D)],
        compiler_params=_cparams(("arbitrary", "arbitrary")),
    )(act, x1, target, mod, w_down)


def _ffn2_bwd(dy, act_dgate, act_dup, mod, w_down):
    B, S, _ = dy.shape
    tm = _tile(S)

    def body(dy_ref, dgate_ref, dup_ref, mod_ref, w_ref, dgu_ref, dyg_ref):
        dyg = (dy_ref[...] * mod_ref[5:6, :]).astype(BF16)
        dyg_ref[...] = dyg
        for j in range(GU_HALF):
            dact = _dot_nt(dyg, w_ref[j])
            dgu_ref[j] = (dact * dgate_ref[j]).astype(BF16)
            dgu_ref[GU_HALF + j] = (dact * dup_ref[j]).astype(BF16)

    return pl.pallas_call(
        body, name="ffn2_bwd", grid=(B, S // tm),
        out_shape=[jax.ShapeDtypeStruct((B, N_DEV, S, GU_SHARD), BF16), jax.ShapeDtypeStruct((B, S, D), BF16)],
        in_specs=[_rows(tm, D), _stacked(GU_HALF, tm, GU_SHARD), _stacked(GU_HALF, tm, GU_SHARD), _perb(6, D),
                  _resident(w_down.shape)],
        out_specs=[_stacked(N_DEV, tm, GU_SHARD), _rows(tm, D)],
        compiler_params=_cparams(("parallel", "arbitrary")),
    )(dy, act_dgate, act_dup, mod, w_down)


def _ffn1_bwd(dgu, x1, dy, mod, g2, w_gu):
    B, S, _ = x1.shape
    tm = _tile(S, 512)

    def body(dgu_ref, x_ref, dy_ref, mod_ref, g_ref, w_ref, dx1_ref, dg_ref, dsc_ref, dsh_ref):
        b, i = pl.program_id(0), pl.program_id(1)
        dh = jnp.dot(dgu_ref[0], w_ref[0], preferred_element_type=F32)
        for j in range(1, N_DEV):
            dh = dh + jnp.dot(dgu_ref[j], w_ref[j], preferred_element_type=F32)
        _, vjp = jax.vjp(_rms_mod, x_ref[...], g_ref[...], mod_ref[4:5, :], mod_ref[3:4, :])
        dx, dg, dsc, dsh = vjp(dh)
        dx1_ref[...] = dy_ref[...] + dx

        @pl.when((b == 0) & (i == 0))
        def _():
            dg_ref[...] = jnp.zeros_like(dg_ref)

        @pl.when(i == 0)
        def _():
            dsc_ref[...] = jnp.zeros_like(dsc_ref)
            dsh_ref[...] = jnp.zeros_like(dsh_ref)

        dg_ref[...] += dg
        dsc_ref[...] += dsc
        dsh_ref[...] += dsh

    return pl.pallas_call(
        body, name="ffn1_bwd", grid=(B, S // tm),
        out_shape=[jax.ShapeDtypeStruct((B, S, D), F32), jax.ShapeDtypeStruct((1, D), F32),
                   jax.ShapeDtypeStruct((B, 1, D), F32), jax.ShapeDtypeStruct((B, 1, D), F32)],
        in_specs=[_stacked(N_DEV, tm, GU_SHARD), _rows(tm, D), _rows(tm, D), _perb(6, D), _full((1, D)),
                  _resident(w_gu.shape)],
        out_specs=[_rows(tm, D), _full((1, D)), _perb(1, D), _perb(1, D)],
        compiler_params=_cparams(("arbitrary", "arbitrary")),
    )(dgu, x1, dy, mod, g2, w_gu)


def _adamw(w, g, m, v, name):
    def body(w_ref, g_ref, m_ref, v_ref, d_ref, nm_ref, nv_ref):
        d_ref[...], nm_ref[...], nv_ref[...] = _adamw_math(w_ref[...], g_ref[...], m_ref[...], v_ref[...])

    sd = jax.ShapeDtypeStruct(w.shape, F32)
    return pl.pallas_call(body, name=name, out_shape=(sd, sd, sd), compiler_params=_cparams())(w, g, m, v)


def kernel(x, c, positions, ada_w, ada_b, norm1_g, w_in, conv_w, q_norm_g, k_norm_g, sinks, a_log, dt_bias, dn_norm_g, w_branch, w_out, norm2_g, w_gate_up, w_down, loss_target, m_ada_w, m_ada_b, m_norm1_g, m_w_in, m_conv_w, m_q_norm_g, m_k_norm_g, m_sinks, m_a_log, m_dt_bias, m_dn_norm_g, m_w_branch, m_w_out, m_norm2_g, m_w_gate_up, m_w_down, v_ada_w, v_ada_b, v_norm1_g, v_w_in, v_conv_w, v_q_norm_g, v_k_norm_g, v_sinks, v_a_log, v_dt_bias, v_dn_norm_g, v_w_branch, v_w_out, v_norm2_g, v_w_gate_up, v_w_down):
    B, S, _ = x.shape
    me = 4 * lax.axis_index("x") + 2 * lax.axis_index("y") + lax.axis_index("c")

    tr = lambda t: jnp.swapaxes(t, 1, 2)
    shards = [w[0].astype(BF16) for w in (tr(w_in), w_branch, w_out, tr(w_gate_up), w_down)]

    c_all = _all_gather_small(c, "gather_c").reshape(N_DEV * B, D)
    ncol = 6 * D // N_DEV
    mod_cols, cond_all = _ada_fwd(c_all, ada_w[0], lax.dynamic_slice(ada_b, (0, me * ncol), (1, ncol)))
    mod_all = _all_gather_small(mod_cols, "gather_mod").transpose(1, 0, 2).reshape(N_DEV * B, 6 * D)
    mod = lax.dynamic_slice(mod_all, (me * B, 0), (B, 6 * D)).reshape(B, 6, D)
    conv2 = conv_w.reshape(CONV, CONVW // N_DEV)
    conv_all = _all_gather_small(conv2, "gather_conv").transpose(1, 0, 2).reshape(CONV, CONVW)

    invf, mean_q, mean_k = _attn_consts()
    w_in_b, rope_cos, rope_sin = _all_gather_big(shards[:1], "gather_w_in", after=(mod, conv_all),
                                                 side=_rope_tables_side(positions.reshape(B, S, 1), invf))
    w_sems, w_srcs, w_lands, w_token = _copies_start(shards[1:], [_place_own(s, me) for s in shards[1:]], False,
                                                    w_in_b, "gather_rest_start")

    w_in_t = w_in_b.reshape(IN_W, D)
    h1, aq, akv, dnx, ba, z, ga, gd = _inproj_fwd(x, mod, norm1_g + w_token[0, 0], w_in_t)
    o_attn = _attn_fwd(aq, akv, rope_cos, rope_sin, q_norm_g, k_norm_g, sinks, mean_q, mean_k)
    cq, dn_u, dn_w, dn_qd, dn_kd, dn_a, dn_t, dn_cd = _dn_prep_fwd(dnx, conv_all, ba, a_log, dt_bias)
    o_dn, states = _dn_seq_fwd(dn_u, dn_w, dn_qd, dn_kd, dn_a, dn_cd)
    w_branch_g, w_out_g, w_gu_b, w_down_g = _copies_wait(w_sems, w_srcs, w_lands, o_dn, "gather_wait_rest")
    w_branch_f = w_branch_g.reshape(D, D)
    w_out_f = w_out_g.reshape(D, D)
    w_down_b = w_down_g.reshape(GU_HALF, GU_SHARD, D)
    x1, mix, merged, ob = _mix_fwd(x, o_attn, o_dn, z, ga, gd, mod, dn_norm_g, w_branch_f, w_out_f)
    h2, act_dgate, act_dup, act = _ffn1_fwd(x1, mod, norm2_g, w_gu_b)
    dy, loss_part, d_gate2 = _ffn2_fwd(act, x1, loss_target, mod, w_down_b)
    loss = lax.psum(loss_part[0, 0], ("x", "y", "c"))

    one = lambda t: t.reshape(B, 1, S, t.shape[-1])
    dgu, dyg = _ffn2_bwd(dy, act_dgate, act_dup, mod, w_down_b)
    g_w_down = _wgrad(act, one(dyg), "wgrad_down")
    dx1, d_n2g, d_scale2, d_shift2 = _ffn1_bwd(dgu, x1, dy, mod, norm2_g, w_gu_b)
    g_w_gu = _wgrad(dgu, one(h2), "wgrad_gate_up")
    ffn = _exchange_start([g_w_gu, g_w_down.reshape(N_DEV, FFN // N_DEV, D)], me, dx1, "exchange_ffn_start")
    dmix, dyo, dga, dgd, dz, d_oa, d_od, d_gate1, d_dng = _mix_bwd(
        dx1, mix, o_attn, o_dn, z, ga, gd, mod, dn_norm_g + ffn[3][0, 0], w_branch_f, w_out_f)
    d_dn = _dn_seq_bwd(dn_u, dn_w, dn_qd, dn_kd, dn_a, dn_cd, states, d_od)
    ddnx, d_conv, dba, d_alog, d_dtb = _dn_prep_bwd(dnx, conv_all, cq, ba, a_log, dt_bias, dn_t, *d_dn)
    daq, dakv, d_qg, d_kg, d_sinks = _attn_bwd(aq, akv, rope_cos, rope_sin, q_norm_g, k_norm_g, sinks, mean_q, mean_k, d_oa)
    dps = [daq, dakv, ddnx, dba, dz, dga, dgd]
    dblk, grad_x, d_n1g, d_scale1, d_shift1 = _inproj_bwd(x, mod, norm1_g, dx1, dps, w_in_t)

    dmod = jnp.concatenate([d_shift1, d_scale1, d_gate1, d_shift2, d_scale2, d_gate2], axis=2).reshape(B, 6 * D)
    small = jnp.concatenate([d_n1g, d_qg, d_kg, d_sinks, d_alog, d_dtb, d_dng, d_n2g, d_conv.reshape(1, CONV * CONVW)], axis=1)
    nsm = small.shape[1]
    width = -(-max(6 * D, nsm) // 128) * 128
    rows = jnp.concatenate([jnp.pad(dmod, ((0, 0), (0, width - 6 * D))), jnp.pad(small, ((0, 8 - B - 1), (0, width - nsm)))], axis=0)
    rows_all = _all_gather_small(rows, "gather_small")
    dmod_all = rows_all[:, 0:B, 0:6 * D].reshape(N_DEV * B, 6 * D)
    dmod_cols = lax.dynamic_slice(dmod_all, (0, me * ncol), (N_DEV * B, ncol))
    grad_ada_w, grad_ada_b, small_sum = _ada_bwd(cond_all, dmod_all, dmod_cols, rows_all[:, B, :])
    sizes = [D, HD, HD, HQ, DH, DH, DK, D]
    so = np.cumsum([0] + sizes)
    g_n1, g_qg, g_kg, g_sk, g_al, g_dt, g_dn, g_n2 = [small_sum[:, so[i]:so[i + 1]] for i in range(8)]
    g_conv_all = small_sum[:, so[8]:so[8] + CONV * CONVW].reshape(CONV, N_DEV, CONVW // N_DEV)
    grad_conv = lax.dynamic_slice(g_conv_all, (0, me, 0), (CONV, 1, CONVW // N_DEV)).reshape(CONV, CONVW // N_DEV)

    half = D // 2
    g_w_in_a = _wgrad(dblk, one(h1), "wgrad_in_a", after=small_sum, b_lanes=(0, half))
    proj_a = _exchange_start([g_w_in_a], me, small_sum, "exchange_in_a_start")
    g_w_in_b = _wgrad(dblk, one(h1), "wgrad_in_b", after=proj_a[3], b_lanes=(1, half))
    proj = _exchange_start([g_w_in_b], me, proj_a[3], "exchange_in_b_start")
    g_w_out = _wgrad(one(merged), one(dmix), "wgrad_out", after=proj[3])
    g_w_branch = _wgrad(ob, dyo, "wgrad_branch", after=proj[3])
    mixer = _exchange_start([g_w_branch.reshape(N_DEV, D // N_DEV, D), g_w_out.reshape(N_DEV, D // N_DEV, D)], me,
                            proj[3], "exchange_mix_start")

    upd, grads = {}, {}

    def finish(names, parts, weights):
        for nm, p, (w, m, v) in zip(names, parts, weights):
            grads[nm], *upd[nm] = _sum_adamw(p, w, m, v, "update_" + nm)
        return grads[names[-1]]

    finish(["w_gate_up", "w_down"], _copies_wait(*ffn[:3], mixer[3], "exchange_ffn_wait"),
           [(tr(w_gate_up), tr(m_w_gate_up), tr(v_w_gate_up)), (w_down, m_w_down, v_w_down)])
    (in_a,) = _copies_wait(*proj_a[:3], grads["w_gate_up"], "exchange_in_a_wait")
    (in_b,) = _copies_wait(*proj[:3], in_a, "exchange_in_b_wait")
    finish(["w_in"], [[in_a, in_b]], [(tr(w_in), tr(m_w_in), tr(v_w_in))])
    finish(["w_branch", "w_out"], _copies_wait(*mixer[:3], grads["w_in"], "exchange_mix_wait"),
           [(w_branch, m_w_branch, v_w_branch), (w_out, m_w_out, v_w_out)])
    for nm in ("w_in", "w_gate_up"):
        grads[nm], upd[nm] = tr(grads[nm]), [tr(t) for t in upd[nm]]

    grads["ada_w"] = grad_ada_w.reshape(ada_w.shape)
    upd["ada_w"] = _adamw(ada_w, grads["ada_w"], m_ada_w, v_ada_w, "adamw_ada_w")
    small_names = ["ada_b", "norm1_g", "q_norm_g", "k_norm_g", "sinks", "a_log", "dt_bias", "dn_norm_g", "norm2_g", "conv_w"]
    small_w = [ada_b, norm1_g, q_norm_g, k_norm_g, sinks, a_log, dt_bias, dn_norm_g, norm2_g, conv_w]
    small_g = [grad_ada_b, g_n1, g_qg, g_kg, g_sk, g_al, g_dt, g_dn, g_n2, grad_conv]
    small_m = [m_ada_b, m_norm1_g, m_q_norm_g, m_k_norm_g, m_sinks, m_a_log, m_dt_bias, m_dn_norm_g, m_norm2_g, m_conv_w]
    small_v = [v_ada_b, v_norm1_g, v_q_norm_g, v_k_norm_g, v_sinks, v_a_log, v_dt_bias, v_dn_norm_g, v_norm2_g, v_conv_w]
    cat = lambda arrs: jnp.concatenate([a.reshape(1, -1) for a in arrs], axis=1)
    res = _adamw(cat(small_w), cat(small_g), cat(small_m), cat(small_v), "adamw_small")
    po = np.cumsum([0] + [int(np.prod(w.shape)) for w in small_w])
    for i, nm in enumerate(small_names):
        upd[nm] = tuple(r[:, po[i]:po[i + 1]].reshape(small_w[i].shape) for r in res)
        grads[nm] = small_g[i].reshape(small_w[i].shape)

    order = ["ada_w", "ada_b", "norm1_g", "w_in", "conv_w", "q_norm_g", "k_norm_g", "sinks", "a_log", "dt_bias",
             "dn_norm_g", "w_branch", "w_out", "norm2_g", "w_gate_up", "w_down"]
    return (loss, grad_x, *[grads[n] for n in order], *[upd[n][0] for n in order],
            *[upd[n][1] for n in order], *[upd[n][2] for n in order])
```

```python
import functools

import numpy as np
import jax
import jax.numpy as jnp
from jax import lax
from jax.experimental import pallas as pl
from jax.experimental.pallas import tpu as pltpu

F32 = jnp.float32
BF16 = jnp.bfloat16
HI = lax.Precision.HIGHEST

N_DEV = 8
D = 1024
HQ, HKV, HD = 8, 2, 64
GRP = HQ // HKV
BLK = 128
ROT = HD // 4
THETA = 500000.0
QW, KVW = HQ * HD, HKV * HD
DH, DK = 4, 128
CH = 64
DNW = DH * DK
CONV = 4
CONVW = 3 * DNW
FFN = 2816
EPS = 1e-6
IN_W = QW + 2 * KVW + CONVW + 2 * DH + DNW + 2 * D

LR, B1, B2, AEPS, WD, STEP = 0.001, 0.9, 0.999, 1e-08, 0.01, 10

VMEM_LIMIT = 56 * 1024 * 1024
MESH = pl.DeviceIdType.MESH


def _cparams(sem=None, vmem=VMEM_LIMIT):
    return pltpu.CompilerParams(dimension_semantics=sem, vmem_limit_bytes=vmem)


def _full(shape):
    n = len(shape)
    return pl.BlockSpec(shape, lambda *_: (0,) * n)


def _resident(shape):
    n = len(shape)
    return pl.BlockSpec(shape, lambda *_: (0,) * n, pipeline_mode=pl.Buffered(1))


def _rows(tm, w):
    return pl.BlockSpec((None, tm, w), lambda b, i: (b, i, 0))


def _stacked(n, tm, w):
    return pl.BlockSpec((None, n, tm, w), lambda b, i: (b, 0, i, 0))


def _perb(r, w):
    return pl.BlockSpec((None, r, w), lambda b, i: (b, 0, 0))


def _dot(a, b):
    return jnp.dot(a.astype(BF16), b.astype(BF16), preferred_element_type=F32)


def _dot_nt(a, b):
    return lax.dot_general(a.astype(BF16), b.astype(BF16), (((1,), (1,)), ((), ())), preferred_element_type=F32)


def _dot_tn(a, b):
    return lax.dot_general(a.astype(BF16), b.astype(BF16), (((0,), (0,)), ((), ())), preferred_element_type=F32)


def _dot_hi(a, b):
    return jnp.dot(a, b, preferred_element_type=F32, precision=HI)


def _sigmoid(x):
    return jax.nn.sigmoid(x)


def _silu(x):
    return x * jax.nn.sigmoid(x)


def _rms_mod(x, g, scale, shift):
    r = lax.rsqrt(jnp.mean(x * x, axis=-1, keepdims=True) + EPS)
    return (x * r * g) * (1.0 + scale) + shift


def _tile(S, rows=256):
    return min(rows, S)


def _peer(x, y, c, k):
    px = 1 - x if (k >> 2) & 1 else x
    py = 1 - y if (k >> 1) & 1 else y
    pc = 1 - c if k & 1 else c
    return px, py, pc


def _all_gather_small(v, name):
    r, n = v.shape

    def body(v_ref, out_ref, send_sems, recv_sems, local_sem):
        x, y, c = lax.axis_index("x"), lax.axis_index("y"), lax.axis_index("c")
        me = 4 * x + 2 * y + c
        mine = pltpu.make_async_copy(v_ref, out_ref.at[me], local_sem)
        mine.start()
        sends = []
        for k in range(1, N_DEV):
            cp = pltpu.make_async_remote_copy(
                src_ref=v_ref, dst_ref=out_ref.at[me], send_sem=send_sems.at[k - 1], recv_sem=recv_sems.at[k - 1],
                device_id=_peer(x, y, c, k), device_id_type=MESH)
            cp.start()
            sends.append(cp)
        for k in range(1, N_DEV):
            px, py, pc = _peer(x, y, c, k)
            pltpu.make_async_remote_copy(
                src_ref=v_ref, dst_ref=out_ref.at[4 * px + 2 * py + pc], send_sem=send_sems.at[k - 1],
                recv_sem=recv_sems.at[k - 1], device_id=(px, py, pc), device_id_type=MESH).wait_recv()
        for cp in sends:
            cp.wait_send()
        mine.wait()

    return pl.pallas_call(
        body, name=name,
        out_shape=jax.ShapeDtypeStruct((N_DEV, r, n), v.dtype),
        in_specs=[pl.BlockSpec(memory_space=pltpu.VMEM)],
        out_specs=pl.BlockSpec(memory_space=pltpu.VMEM),
        scratch_shapes=[pltpu.SemaphoreType.DMA((N_DEV - 1,)), pltpu.SemaphoreType.DMA((N_DEV - 1,)), pltpu.SemaphoreType.DMA],
    )(v)


def _all_gather_big(vs, name, after=(), side=None):
    na, nf = len(vs), len(after)
    side_fn, side_in, side_out = side if side is not None else (None, (), ())
    ns, no = len(side_in), len(side_out)

    def body(*refs):
        v_refs, out_refs = refs[:na], refs[na + nf + ns:2 * na + nf + ns]
        send_sems, recv_sems, local_sems = refs[2 * na + nf + ns + no:]
        x, y, c = lax.axis_index("x"), lax.axis_index("y"), lax.axis_index("c")
        me, sibling = (x, y, c), (x, y, 1 - c)
        chips = [(1 - x, y), (x, 1 - y), (1 - x, 1 - y)]

        def rows(a, px, py, pc):
            return out_refs[a].at[4 * px + 2 * py + pc]

        def copy(a, k, block, to, src=None):
            return pltpu.make_async_remote_copy(
                src_ref=rows(a, *block) if src is None else src, dst_ref=rows(a, *block),
                send_sem=send_sems.at[7 * a + k], recv_sem=recv_sems.at[7 * a + k], device_id=to, device_id_type=MESH)

        mine = [pltpu.make_async_copy(v_refs[a], rows(a, *me), local_sems.at[a]) for a in range(na)]
        for cp in mine:
            cp.start()
        first = []
        for a in range(na):
            first.append(copy(a, 0, me, sibling, src=v_refs[a]))
            first += [copy(a, 1 + j, me, (*chip, c), src=v_refs[a]) for j, chip in enumerate(chips)]
        for cp in first:
            cp.start()
        if side_fn is not None:
            side_fn(refs[na + nf:na + nf + ns], refs[2 * na + nf + ns:2 * na + nf + ns + no])
        passed = []
        for j, chip in enumerate(chips):
            for a in range(na):
                copy(a, 1 + j, (*chip, c), me).wait_recv()
                forward = copy(a, 4 + j, (*chip, c), sibling)
                forward.start()
                passed.append(forward)
        for a in range(na):
            copy(a, 0, sibling, me).wait_recv()
            for j, chip in enumerate(chips):
                copy(a, 4 + j, (*chip, 1 - c), me).wait_recv()
        for cp in first + passed:
            cp.wait_send()
        for cp in mine:
            cp.wait()

    return pl.pallas_call(
        body, name=name,
        out_shape=[jax.ShapeDtypeStruct((N_DEV,) + v.shape, v.dtype) for v in vs] + list(side_out),
        in_specs=[pl.BlockSpec(memory_space=pl.ANY)] * (na + nf) + [pl.BlockSpec(memory_space=pltpu.VMEM)] * ns,
        out_specs=[pl.BlockSpec(memory_space=pl.ANY)] * na + [pl.BlockSpec(memory_space=pltpu.VMEM)] * no,
        scratch_shapes=[pltpu.SemaphoreType.DMA((7 * na,)), pltpu.SemaphoreType.DMA((7 * na,)),
                        pltpu.SemaphoreType.DMA((na,))],
        compiler_params=pltpu.CompilerParams(vmem_limit_bytes=VMEM_LIMIT),
    )(*vs, *after, *side_in)


_HBM = pl.BlockSpec(memory_space=pltpu.HBM)
_SEM = pl.BlockSpec(memory_space=pltpu.SEMAPHORE)
_EFFECT = pltpu.SideEffectType.DATAFLOW_SIDE_EFFECTING


def _place_own(block, me):
    land = lax.empty((N_DEV,) + block.shape, block.dtype)
    return lax.dynamic_update_slice(land, block[None], (me,) + (0,) * block.ndim)


def _copies_start(srcs, lands, scatter, after, name):
    na = len(srcs)
    afters = tuple(after) if isinstance(after, (tuple, list)) else (after,)

    def body(*refs):
        src_refs, land_refs = refs[:na], refs[na:2 * na]
        sems = refs[2 * na + len(afters):4 * na + len(afters)]
        token = refs[-1]
        x, y, c = lax.axis_index("x"), lax.axis_index("y"), lax.axis_index("c")
        me = 4 * x + 2 * y + c
        for a in range(na):
            for k in range(1, N_DEV):
                px, py, pc = _peer(x, y, c, k)
                src = src_refs[a].at[4 * px + 2 * py + pc] if scatter else src_refs[a]
                pltpu.make_async_remote_copy(
                    src_ref=src, dst_ref=land_refs[a].at[me], send_sem=sems[2 * a], recv_sem=sems[2 * a + 1],
                    device_id=(px, py, pc), device_id_type=MESH).start()
        token[...] = jnp.zeros_like(token)

    hbm = lambda t: pltpu.HBM(t.shape, t.dtype)
    out = pl.pallas_call(
        body, name=name,
        out_shape=tuple([pltpu.SemaphoreType.DMA(())] * (2 * na) + [hbm(t) for t in srcs] + [hbm(t) for t in lands]
                        + [jax.ShapeDtypeStruct((8, 128), F32)]),
        in_specs=[_HBM] * (2 * na) + [pl.BlockSpec(memory_space=pl.ANY)] * len(afters),
        out_specs=tuple([_SEM] * (2 * na) + [_HBM] * (2 * na) + [pl.BlockSpec(memory_space=pltpu.VMEM)]),
        input_output_aliases={i: 2 * na + i for i in range(2 * na)},
        compiler_params=pltpu.CompilerParams(has_side_effects=_EFFECT),
    )(*[pltpu.with_memory_space_constraint(t, pltpu.HBM) for t in list(srcs) + list(lands)], *afters)
    return out[:2 * na], out[2 * na:3 * na], out[3 * na:4 * na], out[-1]


def _exchange_start(gs, me, after, name):
    own = [lax.dynamic_index_in_dim(g, me, 0, keepdims=False) for g in gs]
    return _copies_start(gs, [_place_own(o, me) for o in own], True, after, name)


def _copies_wait(sems, srcs, lands, after, name):
    na = len(srcs)

    def body(*refs):
        land_refs = refs[na:2 * na]
        sem_refs = refs[2 * na:4 * na]
        x, y, c = lax.axis_index("x"), lax.axis_index("y"), lax.axis_index("c")
        for a in range(na):
            seven = land_refs[a].at[pl.ds(0, N_DEV - 1)]
            copy = pltpu.make_async_remote_copy(
                src_ref=seven, dst_ref=seven, send_sem=sem_refs[2 * a], recv_sem=sem_refs[2 * a + 1],
                device_id=(x, y, c), device_id_type=MESH)
            copy.wait_send()
            copy.wait_recv()

    hbm = lambda t: pltpu.HBM(t.shape, t.dtype)
    out = pl.pallas_call(
        body, name=name,
        out_shape=tuple([hbm(t) for t in srcs] + [hbm(t) for t in lands]),
        in_specs=[_HBM] * (2 * na) + [_SEM] * (2 * na) + [pl.BlockSpec(memory_space=pl.ANY)],
        out_specs=tuple([_HBM] * (2 * na)),
        input_output_aliases={i: i for i in range(2 * na)},
        compiler_params=pltpu.CompilerParams(has_side_effects=_EFFECT),
    )(*srcs, *lands, *sems, after)
    return out[na:]


def _adamw_math(w, g, m, v):
    m = B1 * m + (1.0 - B1) * g
    v = B2 * v + (1.0 - B2) * (g * g)
    m_hat = m / (1.0 - B1 ** STEP)
    v_hat = v / (1.0 - B2 ** STEP)
    return -LR * (m_hat / (jnp.sqrt(v_hat) + AEPS) + WD * w), m, v


def _sum_adamw(parts, w, m, v, name):
    parts = list(parts) if isinstance(parts, (list, tuple)) else [parts]
    r, n = w.shape[1], w.shape[2]
    tr = 256 if r % 256 == 0 else r
    npart = len(parts)

    def body(*refs):
        p_refs = refs[:npart]
        w_ref, m_ref, v_ref, g_ref, d_ref, nm_ref, nv_ref = refs[npart:]
        pieces = []
        for p_ref in p_refs:
            g = p_ref[0].astype(F32)
            for dev in range(1, N_DEV):
                g = g + p_ref[dev].astype(F32)
            pieces.append(g)
        g = pieces[0] if npart == 1 else jnp.concatenate(pieces, axis=1)
        g_ref[...] = g
        d_ref[...], nm_ref[...], nv_ref[...] = _adamw_math(w_ref[...], g, m_ref[...], v_ref[...])

    rows = pl.BlockSpec((None, tr, n), lambda i: (0, i, 0))
    sd = jax.ShapeDtypeStruct((1, r, n), F32)
    return pl.pallas_call(
        body, name=name, grid=(r // tr,), out_shape=(sd, sd, sd, sd),
        in_specs=[pl.BlockSpec((N_DEV, tr, p.shape[2]), lambda i: (0, i, 0)) for p in parts] + [rows, rows, rows],
        out_specs=(rows, rows, rows, rows),
        compiler_params=_cparams(("parallel",)),
    )(*parts, w, m, v)


def _ada_fwd(c_all, ada_w, ada_b_cols):
    nb, ncol = c_all.shape[0], ada_w.shape[1]

    def body(c_ref, w_ref, b_ref, mod_ref, cond_ref):
        cond = _silu(c_ref[...])
        cond_ref[...] = cond
        mod_ref[...] = _dot_hi(cond, w_ref[...]) + b_ref[...]

    return pl.pallas_call(
        body, name="ada_fwd",
        out_shape=(jax.ShapeDtypeStruct((nb, ncol), F32), jax.ShapeDtypeStruct((nb, D), F32)),
        compiler_params=_cparams(),
    )(c_all, ada_w, ada_b_cols)


def _ada_bwd(cond_all, dmod_all, dmod_cols, smalls):
    ncol, nsm = dmod_cols.shape[1], smalls.shape[1]

    def body(cond_ref, dm_ref, dmc_ref, sm_ref, gw_ref, gb_ref, gs_ref):
        gw_ref[...] = lax.dot_general(cond_ref[...], dmc_ref[...], (((0,), (0,)), ((), ())),
                                      preferred_element_type=F32, precision=HI)
        gb_ref[...] = jnp.sum(dm_ref[...], axis=0, keepdims=True)
        gs_ref[...] = jnp.sum(sm_ref[...], axis=0, keepdims=True)

    return pl.pallas_call(
        body, name="ada_bwd",
        out_shape=(jax.ShapeDtypeStruct((D, ncol), F32), jax.ShapeDtypeStruct((1, 6 * D), F32),
                   jax.ShapeDtypeStruct((1, nsm), F32)),
        compiler_params=_cparams(),
    )(cond_all, dmod_all, dmod_cols, smalls)


IN_CUTS = (0, QW, QW + 2 * KVW, QW + 2 * KVW + CONVW, QW + 2 * KVW + CONVW + 2 * DH,
           QW + 2 * KVW + CONVW + 2 * DH + DNW, QW + 2 * KVW + CONVW + 2 * DH + DNW + D, IN_W)
IN_WIDTHS = tuple(b - a for a, b in zip(IN_CUTS[:-1], IN_CUTS[1:]))
IN_SHARD = IN_W // N_DEV


def _inproj_fwd(x, mod, g1, w_t):
    B, S, _ = x.shape
    tm = _tile(S)

    def body(x_ref, mod_ref, g_ref, w_ref, h_ref, *o_refs):
        h = _rms_mod(x_ref[...], g_ref[...], mod_ref[1:2, :], mod_ref[0:1, :]).astype(BF16)
        h_ref[...] = h
        full = _dot_nt(h, w_ref[...])
        for o_ref, lo, hi in zip(o_refs, IN_CUTS[:-1], IN_CUTS[1:]):
            o_ref[...] = full[:, lo:hi]

    return pl.pallas_call(
        body, name="inproj_fwd", grid=(B, S // tm),
        out_shape=[jax.ShapeDtypeStruct((B, S, D), BF16)] + [jax.ShapeDtypeStruct((B, S, w), F32) for w in IN_WIDTHS],
        in_specs=[_rows(tm, D), _perb(6, D), _full((1, D)), _resident(w_t.shape)],
        out_specs=[_rows(tm, D)] + [_rows(tm, w) for w in IN_WIDTHS],
        compiler_params=_cparams(("parallel", "arbitrary")),
    )(x, mod, g1, w_t)


def _inproj_bwd(x, mod, g1, dx1, dps, w_t):
    B, S, _ = x.shape
    tm = _tile(S)
    n = len(dps)

    def body(x_ref, mod_ref, g_ref, dx1_ref, *refs):
        dp_refs, w_ref = refs[:n], refs[n]
        dblk_ref, gx_ref, dg_ref, dsc_ref, dsh_ref = refs[n + 1:]
        b, i = pl.program_id(0), pl.program_id(1)
        full = jnp.concatenate([r[...].astype(F32) for r in dp_refs], axis=1)
        for j in range(N_DEV):
            dblk_ref[j] = full[:, IN_SHARD * j:IN_SHARD * (j + 1)].astype(BF16)
        dh = jnp.dot(full.astype(BF16), w_ref[...], preferred_element_type=F32)
        _, vjp = jax.vjp(_rms_mod, x_ref[...], g_ref[...], mod_ref[1:2, :], mod_ref[0:1, :])
        dx, dg, dsc, dsh = vjp(dh)
        gx_ref[...] = dx1_ref[...] + dx

        @pl.when((b == 0) & (i == 0))
        def _():
            dg_ref[...] = jnp.zeros_like(dg_ref)

        @pl.when(i == 0)
        def _():
            dsc_ref[...] = jnp.zeros_like(dsc_ref)
            dsh_ref[...] = jnp.zeros_like(dsh_ref)

        dg_ref[...] += dg
        dsc_ref[...] += dsc
        dsh_ref[...] += dsh

    return pl.pallas_call(
        body, name="inproj_bwd", grid=(B, S // tm),
        out_shape=[jax.ShapeDtypeStruct((B, N_DEV, S, IN_SHARD), BF16), jax.ShapeDtypeStruct((B, S, D), F32),
                   jax.ShapeDtypeStruct((1, D), F32), jax.ShapeDtypeStruct((B, 1, D), F32),
                   jax.ShapeDtypeStruct((B, 1, D), F32)],
        in_specs=[_rows(tm, D), _perb(6, D), _full((1, D)), _rows(tm, D)]
                 + [_rows(tm, w) for w in IN_WIDTHS] + [_resident(w_t.shape)],
        out_specs=[pl.BlockSpec((None, N_DEV, tm, IN_SHARD), lambda b, i: (b, 0, i, 0)), _rows(tm, D),
                   _full((1, D)), _perb(1, D), _perb(1, D)],
        compiler_params=_cparams(("arbitrary", "arbitrary")),
    )(x, mod, g1, dx1, *dps, w_t)


def _wgrad(a, b, name, after=None, b_lanes=None):
    B, na, S, K = a.shape
    nb, N = b.shape[1], b.shape[3]
    lane_blk = 0
    if b_lanes is not None:
        lane_blk, N = b_lanes
    G = max(na, nb)
    tm = min(4096, S)
    nt = S // tm
    last = B * nt - 1

    def body(a_ref, b_ref, *rest):
        o_ref, acc = rest[-2:]
        t = pl.program_id(1)

        @pl.when(t == 0)
        def _():
            acc[...] = jnp.zeros_like(acc)

        acc[...] += lax.dot_general(a_ref[...], b_ref[...], (((0,), (0,)), ((), ())), preferred_element_type=F32)

        @pl.when(t == last)
        def _():
            o_ref[...] = acc[...].astype(BF16)

    return pl.pallas_call(
        body, name=name, grid=(G, B * nt),
        out_shape=jax.ShapeDtypeStruct((G, K, N), BF16),
        in_specs=[pl.BlockSpec((None, None, tm, K), lambda g, t: (t // nt, g if na > 1 else 0, t % nt, 0)),
                  pl.BlockSpec((None, None, tm, N), lambda g, t: (t // nt, g if nb > 1 else 0, t % nt, lane_blk))]
                 + ([] if after is None else [pl.BlockSpec(memory_space=pl.ANY)]),
        out_specs=pl.BlockSpec((None, K, N), lambda g, t: (g, 0, 0)),
        scratch_shapes=[pltpu.VMEM((K, N), F32)],
        compiler_params=_cparams(("parallel", "arbitrary")),
    )(*((a, b) if after is None else (a, b, after)))


LANES = 128


def _attn_consts():
    inv_freq = THETA ** (-jnp.arange(0, ROT, 2, dtype=F32) / ROT)
    head = jnp.concatenate([inv_freq, inv_freq, jnp.zeros((HD - ROT,), F32)])
    invf = jnp.tile(head, LANES // HD)[None, :]
    mean_of = lambda w: jnp.asarray(np.kron(np.eye(w // HD), np.full((HD, HD), 1.0 / HD)), BF16)
    return invf, mean_of(QW), mean_of(KVW)


def _rope_tables_side(pos, invf):
    B, S, _ = pos.shape
    tr = min(512, S)

    def fn(ins, outs):
        p_ref, f_ref = ins
        c_ref, s_ref = outs
        for b in range(B):
            for r in range(0, S, tr):
                ang = p_ref[b, r:r + tr, :].astype(F32) * f_ref[...]
                c_ref[b, r:r + tr, :] = jnp.cos(ang)
                s_ref[b, r:r + tr, :] = jnp.sin(ang)

    sd = jax.ShapeDtypeStruct((B, S, LANES), F32)
    return fn, (pos, invf), (sd, sd)


def _rope_expand(cos, sin, reps):
    lane = lax.broadcasted_iota(jnp.int32, cos.shape, 1) % HD
    sa = jnp.where((lane >= ROT // 2) & (lane < ROT), sin, 0.0)
    sb = jnp.where(lane < ROT // 2, -sin, 0.0)
    rep = lambda t: jnp.concatenate([t] * reps, axis=1) if reps > 1 else t
    return rep(cos), rep(sa), rep(sb)


@jax.custom_vjp
def _rope(t, cos, sa, sb):
    w = t.shape[1]
    return t * cos + pltpu.roll(t, ROT // 2, 1) * sa + pltpu.roll(t, w - ROT // 2, 1) * sb


def _rope_fwd(t, cos, sa, sb):
    return _rope(t, cos, sa, sb), (cos, sa, sb)


def _rope_bwd(res, d):
    cos, sa, sb = res
    w = d.shape[1]
    dt = d * cos + pltpu.roll(d * sa, w - ROT // 2, 1) + pltpu.roll(d * sb, ROT // 2, 1)
    return dt, jnp.zeros_like(cos), jnp.zeros_like(sa), jnp.zeros_like(sb)


_rope.defvjp(_rope_fwd, _rope_bwd)


def _head_norm(t, g, mean_of):
    hi, lo = _split(t * t)
    ms = jnp.dot(hi, mean_of, preferred_element_type=F32) + jnp.dot(lo, mean_of, preferred_element_type=F32)
    return t * lax.rsqrt(ms + EPS) * g


def _attn_block(q, kvp, kvc, qg, kg, sinks, tq, tk, mq, mk, valid):
    qn = _rope(_head_norm(q, jnp.concatenate([qg] * HQ, axis=1), mq), *tq) * (HD ** -0.5)
    kv = jnp.concatenate([kvp, kvc], axis=0)
    kn = _rope(_head_norm(kv[:, 0:KVW], jnp.concatenate([kg] * HKV, axis=1), mk), *tk)
    per_tile = LANES // HD
    vT = jnp.transpose(kv[:, KVW:2 * KVW])
    qT = [jnp.transpose(qn[:, LANES * t:LANES * (t + 1)]) for t in range(QW // LANES)]
    head_T = lambda h: qT[h // per_tile][HD * (h % per_tile):HD * (h % per_tile + 1), :]
    none = jnp.zeros((HD, GRP * BLK), F32)
    o_T = []
    for j in range(HKV):
        q4T = jnp.concatenate([head_T(GRP * j + i) for i in range(GRP)], axis=1)
        sT = _dot(kn, jnp.concatenate([q4T, none] if j == 0 else [none, q4T], axis=0))
        sT = jnp.where(valid, sT, -1e30)
        sink = jnp.concatenate([jnp.broadcast_to(sinks[:, GRP * j + i:GRP * j + i + 1], (1, BLK)) for i in range(GRP)], axis=1)
        m = lax.stop_gradient(jnp.maximum(jnp.max(sT, axis=0, keepdims=True), sink))
        pT = jnp.exp(sT - m)
        den = jnp.sum(pT, axis=0, keepdims=True) + jnp.exp(sink - m)
        oT = _dot(vT[HD * j:HD * (j + 1), :], pT) * (1.0 / den)
        o_T += [oT[:, BLK * i:BLK * (i + 1)] for i in range(GRP)]
    return jnp.concatenate([jnp.transpose(jnp.concatenate(o_T[per_tile * t:per_tile * (t + 1)], axis=0))
                            for t in range(QW // LANES)], axis=1)


def _attn_tables(cp_ref, cc_ref, sp_ref, sc_ref, n):
    tq = _rope_expand(cc_ref[...], sc_ref[...], QW // LANES)
    tk = _rope_expand(jnp.concatenate([cp_ref[...], cc_ref[...]], axis=0),
                      jnp.concatenate([sp_ref[...], sc_ref[...]], axis=0), KVW // LANES)
    qi = lax.broadcasted_iota(jnp.int32, (2 * BLK, GRP * BLK), 1) % BLK + BLK
    kj = lax.broadcasted_iota(jnp.int32, (2 * BLK, GRP * BLK), 0)
    dist = qi - kj
    valid = (dist >= 0) & (dist < BLK) & ((kj >= BLK) | (n > 0))
    return tq, tk, valid


def _attn_fwd(aq, akv, cos, sin, qg, kg, sinks, mq, mk):
    B, S, _ = aq.shape
    nb = S // BLK

    def body(q_ref, kvp_ref, kvc_ref, cp_ref, cc_ref, sp_ref, sc_ref, qg_ref, kg_ref, sk_ref, mq_ref, mk_ref, o_ref):
        tq, tk, valid = _attn_tables(cp_ref, cc_ref, sp_ref, sc_ref, pl.program_id(1))
        o_ref[...] = _attn_block(q_ref[...], kvp_ref[...], kvc_ref[...], qg_ref[...], kg_ref[...], sk_ref[...],
                                 tq, tk, mq_ref[...], mk_ref[...], valid)

    prev = lambda b, n: (b, jnp.maximum(n - 1, 0), 0)
    cur = lambda b, n: (b, n, 0)
    return pl.pallas_call(
        body, name="attn_fwd", grid=(B, nb),
        out_shape=jax.ShapeDtypeStruct((B, S, QW), F32),
        in_specs=[pl.BlockSpec((None, BLK, QW), cur), pl.BlockSpec((None, BLK, 2 * KVW), prev),
                  pl.BlockSpec((None, BLK, 2 * KVW), cur), pl.BlockSpec((None, BLK, LANES), prev),
                  pl.BlockSpec((None, BLK, LANES), cur), pl.BlockSpec((None, BLK, LANES), prev),
                  pl.BlockSpec((None, BLK, LANES), cur), _full((1, HD)), _full((1, HD)), _full((1, HQ)),
                  _full((QW, QW)), _full((KVW, KVW))],
        out_specs=pl.BlockSpec((None, BLK, QW), cur),
        compiler_params=_cparams(("parallel", "arbitrary")),
    )(aq, akv, akv, cos, cos, sin, sin, qg, kg, sinks, mq, mk)


def _attn_bwd(aq, akv, cos, sin, qg, kg, sinks, mq, mk, do):
    B, S, _ = aq.shape
    nb = S // BLK

    def body(q_ref, kvp_ref, kvc_ref, cp_ref, cc_ref, sp_ref, sc_ref, qg_ref, kg_ref, sk_ref, mq_ref, mk_ref, do_ref,
             dq_ref, dkv_ref, dqg_ref, dkg_ref, dsk_ref, carry):
        b, i = pl.program_id(0), pl.program_id(1)
        tq, tk, valid = _attn_tables(cp_ref, cc_ref, sp_ref, sc_ref, nb - 1 - i)
        fn = functools.partial(_attn_block, tq=tq, tk=tk, mq=mq_ref[...], mk=mk_ref[...], valid=valid)
        _, vjp = jax.vjp(fn, q_ref[...], kvp_ref[...], kvc_ref[...], qg_ref[...], kg_ref[...], sk_ref[...])
        dq, dkvp, dkvc, dqg, dkg, dsk = vjp(do_ref[...])

        @pl.when(i == 0)
        def _():
            carry[...] = jnp.zeros_like(carry)

        @pl.when((b == 0) & (i == 0))
        def _():
            dqg_ref[...] = jnp.zeros_like(dqg_ref)
            dkg_ref[...] = jnp.zeros_like(dkg_ref)
            dsk_ref[...] = jnp.zeros_like(dsk_ref)

        dq_ref[...] = dq.astype(BF16)
        dkv_ref[...] = (dkvc + carry[...]).astype(BF16)
        carry[...] = dkvp
        dqg_ref[...] += dqg
        dkg_ref[...] += dkg
        dsk_ref[...] += dsk

    prev = lambda b, i: (b, jnp.maximum(nb - 2 - i, 0), 0)
    cur = lambda b, i: (b, nb - 1 - i, 0)
    return pl.pallas_call(
        body, name="attn_bwd", grid=(B, nb),
        out_shape=[jax.ShapeDtypeStruct((B, S, QW), BF16), jax.ShapeDtypeStruct((B, S, 2 * KVW), BF16),
                   jax.ShapeDtypeStruct((1, HD), F32), jax.ShapeDtypeStruct((1, HD), F32),
                   jax.ShapeDtypeStruct((1, HQ), F32)],
        in_specs=[pl.BlockSpec((None, BLK, QW), cur), pl.BlockSpec((None, BLK, 2 * KVW), prev),
                  pl.BlockSpec((None, BLK, 2 * KVW), cur), pl.BlockSpec((None, BLK, LANES), prev),
                  pl.BlockSpec((None, BLK, LANES), cur), pl.BlockSpec((None, BLK, LANES), prev),
                  pl.BlockSpec((None, BLK, LANES), cur), _full((1, HD)), _full((1, HD)), _full((1, HQ)),
                  _full((QW, QW)), _full((KVW, KVW)), pl.BlockSpec((None, BLK, QW), cur)],
        out_specs=[pl.BlockSpec((None, BLK, QW), cur), pl.BlockSpec((None, BLK, 2 * KVW), cur),
                   _full((1, HD)), _full((1, HD)), _full((1, HQ))],
        scratch_shapes=[pltpu.VMEM((BLK, 2 * KVW), F32)],
        compiler_params=_cparams(("arbitrary", "arbitrary")),
    )(aq, akv, akv, cos, cos, sin, sin, qg, kg, sinks, mq, mk, do)


def _conv_taps(xe, w, rows):
    y = None
    for j in range(CONV):
        sh = pltpu.roll(xe, CONV - 1 - j, 0)[8:8 + rows, :] if j < CONV - 1 else xe[8:8 + rows, :]
        y = sh * w[j:j + 1, :] if y is None else y + sh * w[j:j + 1, :]
    return y


def _softplus(x):
    return jnp.maximum(x, 0.0) + jnp.log1p(jnp.exp(-jnp.abs(x)))


_BMM = (((2,), (1,)), ((0,), (0,)))
_BMM_NT = (((2,), (2,)), ((0,), (0,)))
_BMM_TN = (((1,), (1,)), ((0,), (0,)))


def _bmm(a, b, dims=_BMM):
    return lax.dot_general(a.astype(BF16), b.astype(BF16), dims, preferred_element_type=F32)


def _split(a):
    hi = a.astype(BF16)
    return hi, (a - hi.astype(F32)).astype(BF16)


def _bmm3(a, b, dims=_BMM):
    ah, al = _split(a)
    bh, bl = _split(b)
    d = lambda p, q: lax.dot_general(p, q, dims, preferred_element_type=F32)
    return d(ah, bh) + (d(ah, bl) + d(al, bh))


TRI_BASE = 8


def _tri_inverse(L):
    ii = lax.broadcasted_iota(jnp.int32, (CH, CH), 0)
    jj = lax.broadcasted_iota(jnp.int32, (CH, CH), 1)
    same = lambda size: (ii // size) == (jj // size)
    diag = jnp.where(same(TRI_BASE), L, 0.0)
    X = (ii == jj).astype(F32) - diag
    P = diag
    n = 2
    while n < TRI_BASE:
        P = _bmm3(P, P)
        X = X + _bmm3(X, P)
        n *= 2
    size = TRI_BASE
    while size < CH:
        joint = jnp.where(same(2 * size) & jnp.logical_not(same(size)), L, 0.0)
        X = X - _bmm3(X, _bmm3(joint, X))
        size *= 2
    return X


@jax.custom_vjp
def _tri_inverse_known(L, T):
    return T


def _tri_inverse_known_fwd(L, T):
    return T, T


def _tri_inverse_known_bwd(T, dT):
    Tt = jnp.swapaxes(T, 1, 2)
    return -_bmm(Tt, _bmm(dT, Tt)), jnp.zeros_like(T)


_tri_inverse_known.defvjp(_tri_inverse_known_fwd, _tri_inverse_known_bwd)


def _triangle(n, upper):
    ii = lax.broadcasted_iota(jnp.int32, (n, CH, CH), 1)
    jj = lax.broadcasted_iota(jnp.int32, (n, CH, CH), 2)
    return ((ii <= jj) if upper else (ii >= jj)).astype(BF16)


@jax.custom_vjp
def _cumsum_rows(g):
    g0 = g.astype(BF16)
    r1 = g - g0.astype(F32)
    g1 = r1.astype(BF16)
    g2 = (r1 - g1.astype(F32)).astype(BF16)
    tri = _triangle(g.shape[0], False)
    d = lambda q: lax.dot_general(tri, q, _BMM, preferred_element_type=F32)
    return d(g0) + (d(g1) + d(g2))


def _cumsum_rows_fwd(g):
    return _cumsum_rows(g), None


def _cumsum_rows_bwd(_, dy):
    hi, lo = _split(dy)
    tri = _triangle(dy.shape[0], True)
    d = lambda q: lax.dot_general(tri, q, _BMM, preferred_element_type=F32)
    return (d(hi) + d(lo),)


_cumsum_rows.defvjp(_cumsum_rows_fwd, _cumsum_rows_bwd)


def _row_sums(t):
    n, r, w = t.shape
    hi, lo = _split(t.reshape(n * r, w))
    ones = jnp.ones((w, w), BF16)
    s = jnp.dot(hi, ones, preferred_element_type=F32) + jnp.dot(lo, ones, preferred_element_type=F32)
    return s.reshape(n, r, w)


def _dn_prep(t_known, qr, kr, v, a_raw, b_raw, a_log, dt_b):
    n = qr.shape[0]
    ii = lax.broadcasted_iota(jnp.int32, (n, CH, CH), 1)
    jj = lax.broadcasted_iota(jnp.int32, (n, CH, CH), 2)
    incl, strict = ii >= jj, ii > jj
    q = qr * lax.rsqrt(_row_sums(qr * qr) + EPS) * (DK ** -0.5)
    k = kr * lax.rsqrt(_row_sums(kr * kr) + EPS)
    beta = _sigmoid(b_raw)
    g = -jnp.exp(a_log) * _softplus(a_raw + dt_b)
    gcb = _cumsum_rows(jnp.broadcast_to(g, (n, CH, DK)))
    gc = gcb[:, :, 0:1]
    gc_row = jnp.swapaxes(gcb, 1, 2)[:, 0:1, 0:CH]
    decay = jnp.where(incl, jnp.exp(jnp.where(incl, gc - gc_row, 0.0)), 0.0)
    kb = k * beta
    L = jnp.where(strict, _bmm(kb, k, _BMM_NT) * decay, 0.0)
    T = _tri_inverse(L) if t_known is None else _tri_inverse_known(L, t_known)
    eg = jnp.exp(gc)
    u = _bmm(T, v * beta)
    w = _bmm(T, kb * eg)
    a_in = _bmm(q, k, _BMM_NT) * decay
    g_last = gc[:, CH - 1:CH, :]
    return u, w, q * eg, k * jnp.exp(g_last - gc), a_in, jnp.exp(g_last), T


def _dn_step(S0, u, w, qd, kd, a_in, cd):
    r = _bmm(jnp.concatenate([w, qd], axis=1), S0)
    v_new = u - r[:, 0:CH, :]
    o = r[:, CH:2 * CH, :] + _bmm(a_in, v_new)
    S1 = S0 * cd + _bmm(kd, v_new, _BMM_TN)
    return o, S1


def _dn_stack(cq, ba, al, dt, G):
    cols = [[] for _ in range(7)]
    for c in range(G):
        rows = slice(CH * c, CH * (c + 1))
        for h in range(DH):
            parts = (cq[rows, DK * h:DK * (h + 1)], cq[rows, DNW + DK * h:DNW + DK * (h + 1)],
                     cq[rows, 2 * DNW + DK * h:2 * DNW + DK * (h + 1)], ba[rows, DH + h:DH + h + 1],
                     ba[rows, h:h + 1], al[:, h:h + 1], dt[:, h:h + 1])
            for col, p in zip(cols, parts):
                col.append(p)
    return tuple(jnp.stack(col) for col in cols)


def _dn_group(S, want):
    g = want
    while (S // CH) % g:
        g //= 2
    return g


def _dn_prep_fwd(xin, conv_w, ba, a_log, dt_b):
    B, S, _ = xin.shape
    nc = S // CH
    G = _dn_group(S, 8)
    r8 = G * CH // 8

    def body(xp_ref, x_ref, cw_ref, ba_ref, al_ref, dt_ref, cq_ref, u_ref, w_ref, qd_ref, kd_ref, a_ref, t_ref, cd_ref):
        xp = jnp.where(pl.program_id(1) > 0, xp_ref[...], 0.0)
        cq = _silu(_conv_taps(jnp.concatenate([xp, x_ref[...]], axis=0), cw_ref[...], G * CH))
        cq_ref[...] = cq
        ops = _dn_stack(cq, ba_ref[...], al_ref[...], dt_ref[...], G)
        u, w, qd, kd, a_in, cd, T = _dn_prep(None, *ops)
        lane4 = lax.broadcasted_iota(jnp.int32, (1, DH), 1)
        for c in range(G):
            rows = slice(CH * c, CH * (c + 1))
            cdrow = jnp.zeros((1, DH), F32)
            for h in range(DH):
                n = DH * c + h
                lanes = slice(DK * h, DK * (h + 1))
                u_ref[rows, lanes] = u[n]
                w_ref[rows, lanes] = w[n]
                qd_ref[rows, lanes] = qd[n]
                kd_ref[rows, lanes] = kd[n]
                a_ref[rows, CH * h:CH * (h + 1)] = a_in[n]
                t_ref[rows, CH * h:CH * (h + 1)] = T[n]
                cdrow = cdrow + jnp.where(lane4 == h, cd[n], 0.0)
            cd_ref[c] = cdrow

    wide = jax.ShapeDtypeStruct((B, S, DNW), F32)
    sq = jax.ShapeDtypeStruct((B, S, DH * CH), F32)
    return pl.pallas_call(
        body, name="dn_prep_fwd", grid=(B, nc // G),
        out_shape=[jax.ShapeDtypeStruct((B, S, CONVW), F32), wide, wide, wide, wide, sq, sq,
                   jax.ShapeDtypeStruct((B, nc, 1, DH), F32)],
        in_specs=[pl.BlockSpec((None, 8, CONVW), lambda b, i: (b, jnp.maximum(i * r8 - 1, 0), 0)),
                  _rows(G * CH, CONVW), _full((CONV, CONVW)), _rows(G * CH, 2 * DH), _full((1, DH)), _full((1, DH))],
        out_specs=[_rows(G * CH, CONVW)] + [_rows(G * CH, DNW)] * 4 + [_rows(G * CH, DH * CH)] * 2
                  + [pl.BlockSpec((None, G, 1, DH), lambda b, i: (b, i, 0, 0))],
        compiler_params=_cparams(("parallel", "arbitrary")),
    )(xin, xin, conv_w, ba, a_log, dt_b)


def _dn_seq_specs(B, steps, gs, rev):
    at = (lambda i: steps - 1 - i) if rev else (lambda i: i)
    wide = pl.BlockSpec((B, gs * CH, DNW), lambda i: (0, at(i), 0))
    a_spec = pl.BlockSpec((B, gs * CH, DH * CH), lambda i: (0, at(i), 0))
    cd_spec = pl.BlockSpec((B, gs, 1, DH), lambda i: (0, at(i), 0, 0))
    st_spec = pl.BlockSpec((B, gs, DH, DK, DK), lambda i: (0, at(i), 0, 0, 0))
    return wide, a_spec, cd_spec, st_spec


def _dn_step_operands(B, c, u_ref, w_ref, qd_ref, kd_ref, a_ref, cd_ref):
    pairs = [(b, h) for b in range(B) for h in range(DH)]
    rows = slice(CH * c, CH * (c + 1))
    wide = lambda ref: jnp.stack([ref[b, rows, DK * h:DK * (h + 1)] for b, h in pairs])
    a_in = jnp.stack([a_ref[b, rows, CH * h:CH * (h + 1)] for b, h in pairs])
    cd = jnp.stack([cd_ref[b, c, :, h:h + 1] for b, h in pairs])
    return wide(u_ref), wide(w_ref), wide(qd_ref), wide(kd_ref), a_in, cd


def _dn_seq_fwd(u, w, qd, kd, a_in, cd):
    B, S, _ = u.shape
    nc = S // CH
    gs = _dn_group(S, 8)

    def body(u_ref, w_ref, qd_ref, kd_ref, a_ref, cd_ref, o_ref, st_ref, state):
        @pl.when(pl.program_id(0) == 0)
        def _():
            state[...] = jnp.zeros_like(state)

        S0 = state[...]
        for c in range(gs):
            for b in range(B):
                st_ref[b, c] = S0[DH * b:DH * (b + 1)]
            o, S0 = _dn_step(S0, *_dn_step_operands(B, c, u_ref, w_ref, qd_ref, kd_ref, a_ref, cd_ref))
            for b in range(B):
                for h in range(DH):
                    o_ref[b, CH * c:CH * (c + 1), DK * h:DK * (h + 1)] = o[DH * b + h]
        state[...] = S0

    wide, a_spec, cd_spec, st_spec = _dn_seq_specs(B, nc // gs, gs, False)
    return pl.pallas_call(
        body, name="dn_seq_fwd", grid=(nc // gs,),
        out_shape=[jax.ShapeDtypeStruct((B, S, DNW), F32), jax.ShapeDtypeStruct((B, nc, DH, DK, DK), F32)],
        in_specs=[wide, wide, wide, wide, a_spec, cd_spec],
        out_specs=[wide, st_spec],
        scratch_shapes=[pltpu.VMEM((B * DH, DK, DK), F32)],
        compiler_params=_cparams(("arbitrary",)),
    )(u, w, qd, kd, a_in, cd)


def _dn_seq_bwd(u, w, qd, kd, a_in, cd, states, do):
    B, S, _ = u.shape
    nc = S // CH
    gs = _dn_group(S, 8)

    def body(u_ref, w_ref, qd_ref, kd_ref, a_ref, cd_ref, st_ref, do_ref,
             du_ref, dw_ref, dqd_ref, dkd_ref, da_ref, dcd_ref, dstate):
        @pl.when(pl.program_id(0) == 0)
        def _():
            dstate[...] = jnp.zeros_like(dstate)

        lane4 = lax.broadcasted_iota(jnp.int32, (1, DH), 1)
        dS = dstate[...]
        for c in reversed(range(gs)):
            rows = slice(CH * c, CH * (c + 1))
            S0 = jnp.concatenate([st_ref[b, c] for b in range(B)], axis=0)
            do = jnp.stack([do_ref[b, rows, DK * h:DK * (h + 1)] for b in range(B) for h in range(DH)])
            _, vjp = jax.vjp(_dn_step, S0, *_dn_step_operands(B, c, u_ref, w_ref, qd_ref, kd_ref, a_ref, cd_ref))
            dS, du, dw, dqd, dkd, da, dcd = vjp((do, dS))
            for b in range(B):
                dcdrow = jnp.zeros((1, DH), F32)
                for h in range(DH):
                    n = DH * b + h
                    lanes = slice(DK * h, DK * (h + 1))
                    du_ref[b, rows, lanes] = du[n]
                    dw_ref[b, rows, lanes] = dw[n]
                    dqd_ref[b, rows, lanes] = dqd[n]
                    dkd_ref[b, rows, lanes] = dkd[n]
                    da_ref[b, rows, CH * h:CH * (h + 1)] = da[n]
                    dcdrow = dcdrow + jnp.where(lane4 == h, dcd[n], 0.0)
                dcd_ref[b, c] = dcdrow
        dstate[...] = dS

    wide, a_spec, cd_spec, st_spec = _dn_seq_specs(B, nc // gs, gs, True)
    sd = jax.ShapeDtypeStruct((B, S, DNW), F32)
    return pl.pallas_call(
        body, name="dn_seq_bwd", grid=(nc // gs,),
        out_shape=[sd, sd, sd, sd, jax.ShapeDtypeStruct((B, S, DH * CH), F32), jax.ShapeDtypeStruct((B, nc, 1, DH), F32)],
        in_specs=[wide, wide, wide, wide, a_spec, cd_spec, st_spec, wide],
        out_specs=[wide, wide, wide, wide, a_spec, cd_spec],
        scratch_shapes=[pltpu.VMEM((B * DH, DK, DK), F32)],
        compiler_params=_cparams(("arbitrary",)),
    )(u, w, qd, kd, a_in, cd, states, do)


def _dn_prep_bwd(xin, conv_w, cq, ba, a_log, dt_b, t_inv, du, dw, dqd, dkd, da, dcd):
    B, S, _ = cq.shape
    nc = S // CH
    G = _dn_group(S, 8)
    R = G * CH
    nblk = nc // G
    r8 = R // 8

    def body(xp_ref, x_ref, cw_ref, cq_ref, ba_ref, al_ref, dt_ref, t_ref, du_ref, dw_ref, dqd_ref, dkd_ref, da_ref, dcd_ref,
             dx_ref, dcw_ref, dba_ref, dal_ref, ddt_ref, carry):
        i = pl.program_id(1)

        @pl.when((pl.program_id(0) == 0) & (i == 0))
        def _():
            dal_ref[...] = jnp.zeros_like(dal_ref)
            ddt_ref[...] = jnp.zeros_like(ddt_ref)
            dcw_ref[...] = jnp.zeros_like(dcw_ref)

        @pl.when(i == 0)
        def _():
            carry[...] = jnp.zeros_like(carry)

        pairs = [(c, h) for c in range(G) for h in range(DH)]
        rows = lambda c: slice(CH * c, CH * (c + 1))
        wide = lambda ref: jnp.stack([ref[rows(c), DK * h:DK * (h + 1)] for c, h in pairs])
        square = lambda ref: jnp.stack([ref[rows(c), CH * h:CH * (h + 1)] for c, h in pairs])
        ops = _dn_stack(cq_ref[...], ba_ref[...], al_ref[...], dt_ref[...], G)
        cots = (wide(du_ref), wide(dw_ref), wide(dqd_ref), wide(dkd_ref), square(da_ref),
                jnp.stack([dcd_ref[c][:, h:h + 1] for c, h in pairs]), jnp.zeros((len(pairs), CH, CH), F32))
        _, vjp = jax.vjp(functools.partial(_dn_prep, square(t_ref)), *ops)
        dq, dk, dv, dar, dbr, dl, dd = vjp(cots)
        lane8 = lax.broadcasted_iota(jnp.int32, (CH, 2 * DH), 1)
        lane4 = lax.broadcasted_iota(jnp.int32, (1, DH), 1)
        dal = jnp.zeros((1, DH), F32)
        ddt = jnp.zeros((1, DH), F32)
        for c in range(G):
            dba = jnp.zeros((CH, 2 * DH), F32)
            for h in range(DH):
                n = DH * c + h
                dba = dba + jnp.where(lane8 == h, dbr[n], 0.0) + jnp.where(lane8 == DH + h, dar[n], 0.0)
                dal = dal + jnp.where(lane4 == h, dl[n], 0.0)
                ddt = ddt + jnp.where(lane4 == h, dd[n], 0.0)
            dba_ref[rows(c), :] = dba.astype(BF16)
        dal_ref[...] += dal
        ddt_ref[...] += ddt

        dcq = jnp.concatenate([jnp.concatenate([t[DH * c + h] for t in (dq, dk, dv) for h in range(DH)], axis=1)
                               for c in range(G)], axis=0)
        w = cw_ref[...]
        xp = jnp.where(i < nblk - 1, xp_ref[...], 0.0)
        xe = jnp.concatenate([xp, x_ref[...]], axis=0)
        taps = [(pltpu.roll(xe, CONV - 1 - j, 0) if j < CONV - 1 else xe)[8:8 + R, :] for j in range(CONV)]
        pre = sum(t * w[j:j + 1, :] for j, t in enumerate(taps))
        sg = _sigmoid(pre)
        dpre = dcq * (sg * (1.0 + pre * (1.0 - sg)))
        ext = jnp.concatenate([dpre, carry[...]], axis=0)
        dx = dpre * w[CONV - 1:CONV, :]
        for j in range(CONV - 1):
            dx = dx + pltpu.roll(ext, R + 8 - (CONV - 1 - j), 0)[0:R, :] * w[j:j + 1, :]
        dx_ref[...] = dx.astype(BF16)
        carry[...] = dpre[0:8, :]
        lane_row = lax.broadcasted_iota(jnp.int32, (CONV, CONVW), 0)
        dcw = jnp.zeros((CONV, CONVW), F32)
        for j in range(CONV):
            dcw = dcw + jnp.where(lane_row == j, jnp.sum(taps[j] * dpre, axis=0, keepdims=True), 0.0)
        dcw_ref[...] += dcw

    rev = lambda w: pl.BlockSpec((None, R, w), lambda b, i: (b, nblk - 1 - i, 0))
    return pl.pallas_call(
        body, name="dn_prep_bwd", grid=(B, nblk),
        out_shape=[jax.ShapeDtypeStruct((B, S, CONVW), BF16), jax.ShapeDtypeStruct((CONV, CONVW), F32),
                   jax.ShapeDtypeStruct((B, S, 2 * DH), BF16), jax.ShapeDtypeStruct((1, DH), F32),
                   jax.ShapeDtypeStruct((1, DH), F32)],
        in_specs=[pl.BlockSpec((None, 8, CONVW), lambda b, i: (b, jnp.maximum((nblk - 1 - i) * r8 - 1, 0), 0)),
                  rev(CONVW), _full((CONV, CONVW)), rev(CONVW), rev(2 * DH), _full((1, DH)), _full((1, DH)), rev(DH * CH)]
                 + [rev(DNW)] * 4 + [rev(DH * CH), pl.BlockSpec((None, G, 1, DH), lambda b, i: (b, nblk - 1 - i, 0, 0))],
        out_specs=[rev(CONVW), _full((CONV, CONVW)), rev(2 * DH), _full((1, DH)), _full((1, DH))],
        scratch_shapes=[pltpu.VMEM((8, CONVW), F32)],
        compiler_params=_cparams(("arbitrary", "arbitrary")),
    )(xin, xin, conv_w, cq, ba, a_log, dt_b, t_inv, du, dw, dqd, dkd, da, dcd)


def _gated_norm(o, z, g):
    outs = []
    for h in range(DH):
        t = o[:, DK * h:DK * (h + 1)]
        r = lax.rsqrt(jnp.mean(t * t, axis=-1, keepdims=True) + EPS)
        outs.append(t * r * g * _silu(z[:, DK * h:DK * (h + 1)]))
    return jnp.concatenate(outs, axis=1)


def _mix_fwd(x, o_attn, o_dn, z, ga, gd, mod, dn_g, w_branch, w_out):
    B, S, _ = x.shape
    tm = _tile(S, 512)

    def body(x_ref, oa_ref, od_ref, z_ref, ga_ref, gd_ref, mod_ref, g_ref, wb_ref, wo_ref,
             x1_ref, mix_ref, mg_ref, ob_ref):
        oa = oa_ref[...].astype(BF16)
        od = _gated_norm(od_ref[...], z_ref[...], g_ref[...]).astype(BF16)
        ob_ref[0] = oa
        ob_ref[1] = od
        ya = jnp.dot(oa, wb_ref[0:QW, :], preferred_element_type=F32)
        yd = jnp.dot(od, wb_ref[QW:QW + DNW, :], preferred_element_type=F32)
        merged = (_sigmoid(ga_ref[...]) * ya + _sigmoid(gd_ref[...]) * yd).astype(BF16)
        mg_ref[...] = merged
        mix = jnp.dot(merged, wo_ref[...], preferred_element_type=F32)
        mix_ref[...] = mix
        x1_ref[...] = x_ref[...] + mod_ref[2:3, :] * mix

    return pl.pallas_call(
        body, name="mix_fwd", grid=(B, S // tm),
        out_shape=[jax.ShapeDtypeStruct((B, S, D), F32), jax.ShapeDtypeStruct((B, S, D), F32),
                   jax.ShapeDtypeStruct((B, S, D), BF16), jax.ShapeDtypeStruct((B, 2, S, QW), BF16)],
        in_specs=[_rows(tm, D), _rows(tm, QW), _rows(tm, DNW), _rows(tm, DNW), _rows(tm, D), _rows(tm, D),
                  _perb(6, D), _full((1, DK)), _resident(w_branch.shape), _resident(w_out.shape)],
        out_specs=[_rows(tm, D), _rows(tm, D), _rows(tm, D), _stacked(2, tm, QW)],
        compiler_params=_cparams(("parallel", "arbitrary")),
    )(x, o_attn, o_dn, z, ga, gd, mod, dn_g, w_branch, w_out)


def _mix_bwd(dx1, mix, o_attn, o_dn, z, ga, gd, mod, dn_g, w_branch, w_out):
    B, S, _ = dx1.shape
    tm = _tile(S, 512)

    def body(dx1_ref, mix_ref, oa_ref, od_ref, z_ref, ga_ref, gd_ref, mod_ref, g_ref, wb_ref, wo_ref,
             dmix_ref, dyo_ref, dga_ref, dgd_ref, dz_ref, doa_ref, dod_ref, dgate_ref, dg_ref):
        b, i = pl.program_id(0), pl.program_id(1)
        dx1 = dx1_ref[...]
        dmix = (dx1 * mod_ref[2:3, :]).astype(BF16)
        dmix_ref[...] = dmix
        dgate = jnp.sum(dx1 * mix_ref[...], axis=0, keepdims=True)
        dmerged = _dot_nt(dmix, wo_ref[...])
        odn, gn_vjp = jax.vjp(_gated_norm, od_ref[...], z_ref[...], g_ref[...])
        ya = _dot(oa_ref[...], wb_ref[0:QW, :])
        yd = _dot(odn, wb_ref[QW:QW + DNW, :])
        sa, sd = _sigmoid(ga_ref[...]), _sigmoid(gd_ref[...])
        dya = (dmerged * sa).astype(BF16)
        dyd = (dmerged * sd).astype(BF16)
        dyo_ref[0] = dya
        dyo_ref[1] = dyd
        dga_ref[...] = (dmerged * ya * sa * (1.0 - sa)).astype(BF16)
        dgd_ref[...] = (dmerged * yd * sd * (1.0 - sd)).astype(BF16)
        doa_ref[...] = _dot_nt(dya, wb_ref[0:QW, :])
        dodn = _dot_nt(dyd, wb_ref[QW:QW + DNW, :])
        dod, dz, dg = gn_vjp(dodn)
        dod_ref[...] = dod
        dz_ref[...] = dz.astype(BF16)

        @pl.when(i == 0)
        def _():
            dgate_ref[...] = jnp.zeros_like(dgate_ref)

        @pl.when((b == 0) & (i == 0))
        def _():
            dg_ref[...] = jnp.zeros_like(dg_ref)

        dgate_ref[...] += dgate
        dg_ref[...] += dg

    return pl.pallas_call(
        body, name="mix_bwd", grid=(B, S // tm),
        out_shape=[jax.ShapeDtypeStruct((B, S, D), BF16), jax.ShapeDtypeStruct((B, 2, S, D), BF16),
                   jax.ShapeDtypeStruct((B, S, D), BF16), jax.ShapeDtypeStruct((B, S, D), BF16),
                   jax.ShapeDtypeStruct((B, S, DNW), BF16),
                   jax.ShapeDtypeStruct((B, S, QW), F32), jax.ShapeDtypeStruct((B, S, DNW), F32),
                   jax.ShapeDtypeStruct((B, 1, D), F32), jax.ShapeDtypeStruct((1, DK), F32)],
        in_specs=[_rows(tm, D), _rows(tm, D), _rows(tm, QW), _rows(tm, DNW), _rows(tm, DNW), _rows(tm, D),
                  _rows(tm, D), _perb(6, D), _full((1, DK)), _resident(w_branch.shape), _resident(w_out.shape)],
        out_specs=[_rows(tm, D), _stacked(2, tm, D), _rows(tm, D), _rows(tm, D), _rows(tm, DNW),
                   _rows(tm, QW), _rows(tm, DNW), _perb(1, D), _full((1, DK))],
        compiler_params=_cparams(("arbitrary", "arbitrary")),
    )(dx1, mix, o_attn, o_dn, z, ga, gd, mod, dn_g, w_branch, w_out)


GU_SHARD = 2 * FFN // N_DEV
GU_HALF = N_DEV // 2


def _ffn1_fwd(x1, mod, g2, w_gu):
    B, S, _ = x1.shape
    tm = _tile(S, 512)

    def body(x_ref, mod_ref, g_ref, w_ref, h_ref, dgate_ref, dup_ref, act_ref):
        h = _rms_mod(x_ref[...], g_ref[...], mod_ref[4:5, :], mod_ref[3:4, :]).astype(BF16)
        h_ref[...] = h
        for j in range(GU_HALF):
            gate = _dot_nt(h, w_ref[j])
            up = _dot_nt(h, w_ref[GU_HALF + j])
            sg = _sigmoid(gate)
            silu = gate * sg
            dgate_ref[j] = up * (sg * (1.0 + gate * (1.0 - sg)))
            dup_ref[j] = silu
            act_ref[j] = (silu * up).astype(BF16)

    blk = lambda dt: jax.ShapeDtypeStruct((B, GU_HALF, S, GU_SHARD), dt)
    return pl.pallas_call(
        body, name="ffn1_fwd", grid=(B, S // tm),
        out_shape=[jax.ShapeDtypeStruct((B, S, D), BF16), blk(F32), blk(F32), blk(BF16)],
        in_specs=[_rows(tm, D), _perb(6, D), _full((1, D)), _resident(w_gu.shape)],
        out_specs=[_rows(tm, D)] + [_stacked(GU_HALF, tm, GU_SHARD)] * 3,
        compiler_params=_cparams(("parallel", "arbitrary")),
    )(x1, mod, g2, w_gu)


def _ffn2_fwd(act, x1, target, mod, w_down):
    B, S, _ = x1.shape
    tm = _tile(S, 512)

    def body(a_ref, x_ref, t_ref, mod_ref, w_ref, dy_ref, loss_ref, dgate_ref):
        b, i = pl.program_id(0), pl.program_id(1)
        y = jnp.dot(a_ref[0], w_ref[0], preferred_element_type=F32)
        for j in range(1, GU_HALF):
            y = y + jnp.dot(a_ref[j], w_ref[j], preferred_element_type=F32)
        err = x_ref[...] + mod_ref[5:6, :] * y - t_ref[...]
        dy = err * (1.0 / D)
        dy_ref[...] = dy

        @pl.when((b == 0) & (i == 0))
        def _():
            loss_ref[...] = jnp.zeros_like(loss_ref)

        @pl.when(i == 0)
        def _():
            dgate_ref[...] = jnp.zeros_like(dgate_ref)

        loss_ref[...] += (0.5 / D) * jnp.sum(err * err)
        dgate_ref[...] += jnp.sum(dy * y, axis=0, keepdims=True)

    return pl.pallas_call(
        body, name="ffn2_fwd", grid=(B, S // tm),
        out_shape=[jax.ShapeDtypeStruct((B, S, D), F32), jax.ShapeDtypeStruct((1, 128), F32),
                   jax.ShapeDtypeStruct((B, 1, D), F32)],
        in_specs=[_stacked(GU_HALF, tm, GU_SHARD), _rows(tm, D), _rows(tm, D), _perb(6, D), _resident(w_down.shape)],
        out_specs=[_rows(tm, D), _full((1, 128)), _perb(1, D)],
        compiler_params=_cparams(("arbitrary", "arbitrary")),
    )(act, x1, target, mod, w_down)


def _ffn2_bwd(dy, act_dgate, act_dup, mod, w_down):
    B, S, _ = dy.shape
    tm = _tile(S, 512)

    def body(dy_ref, dgate_ref, dup_ref, mod_ref, w_ref, dgu_ref, dyg_ref):
        dyg = (dy_ref[...] * mod_ref[5:6, :]).astype(BF16)
        dyg_ref[...] = dyg
        for j in range(GU_HALF):
            dact = _dot_nt(dyg, w_ref[j])
            dgu_ref[j] = (dact * dgate_ref[j]).astype(BF16)
            dgu_ref[GU_HALF + j] = (dact * dup_ref[j]).astype(BF16)

    return pl.pallas_call(
        body, name="ffn2_bwd", grid=(B, S // tm),
        out_shape=[jax.ShapeDtypeStruct((B, N_DEV, S, GU_SHARD), BF16), jax.ShapeDtypeStruct((B, S, D), BF16)],
        in_specs=[_rows(tm, D), _stacked(GU_HALF, tm, GU_SHARD), _stacked(GU_HALF, tm, GU_SHARD), _perb(6, D),
                  _resident(w_down.shape)],
        out_specs=[_stacked(N_DEV, tm, GU_SHARD), _rows(tm, D)],
        compiler_params=_cparams(("parallel", "arbitrary")),
    )(dy, act_dgate, act_dup, mod, w_down)


def _ffn1_bwd(dgu, x1, dy, mod, g2, w_gu):
    B, S, _ = x1.shape
    tm = _tile(S, 512)

    def body(dgu_ref, x_ref, dy_ref, mod_ref, g_ref, w_ref, dx1_ref, dg_ref, dsc_ref, dsh_ref):
        b, i = pl.program_id(0), pl.program_id(1)
        dh = jnp.dot(dgu_ref[0], w_ref[0], preferred_element_type=F32)
        for j in range(1, N_DEV):
            dh = dh + jnp.dot(dgu_ref[j], w_ref[j], preferred_element_type=F32)
        _, vjp = jax.vjp(_rms_mod, x_ref[...], g_ref[...], mod_ref[4:5, :], mod_ref[3:4, :])
        dx, dg, dsc, dsh = vjp(dh)
        dx1_ref[...] = dy_ref[...] + dx

        @pl.when((b == 0) & (i == 0))
        def _():
            dg_ref[...] = jnp.zeros_like(dg_ref)

        @pl.when(i == 0)
        def _():
            dsc_ref[...] = jnp.zeros_like(dsc_ref)
            dsh_ref[...] = jnp.zeros_like(dsh_ref)

        dg_ref[...] += dg
        dsc_ref[...] += dsc
        dsh_ref[...] += dsh

    return pl.pallas_call(
        body, name="ffn1_bwd", grid=(B, S // tm),
        out_shape=[jax.ShapeDtypeStruct((B, S, D), F32), jax.ShapeDtypeStruct((1, D), F32),
                   jax.ShapeDtypeStruct((B, 1, D), F32), jax.ShapeDtypeStruct((B, 1, D), F32)],
        in_specs=[_stacked(N_DEV, tm, GU_SHARD), _rows(tm, D), _rows(tm, D), _perb(6, D), _full((1, D)),
                  _resident(w_gu.shape)],
        out_specs=[_rows(tm, D), _full((1, D)), _perb(1, D), _perb(1, D)],
        compiler_params=_cparams(("arbitrary", "arbitrary")),
    )(dgu, x1, dy, mod, g2, w_gu)


def _adamw(w, g, m, v, name):
    def body(w_ref, g_ref, m_ref, v_ref, d_ref, nm_ref, nv_ref):
        d_ref[...], nm_ref[...], nv_ref[...] = _adamw_math(w_ref[...], g_ref[...], m_ref[...], v_ref[...])

    sd = jax.ShapeDtypeStruct(w.shape, F32)
    return pl.pallas_call(body, name=name, out_shape=(sd, sd, sd), compiler_params=_cparams())(w, g, m, v)


def kernel(x, c, positions, ada_w, ada_b, norm1_g, w_in, conv_w, q_norm_g, k_norm_g, sinks, a_log, dt_bias, dn_norm_g, w_branch, w_out, norm2_g, w_gate_up, w_down, loss_target, m_ada_w, m_ada_b, m_norm1_g, m_w_in, m_conv_w, m_q_norm_g, m_k_norm_g, m_sinks, m_a_log, m_dt_bias, m_dn_norm_g, m_w_branch, m_w_out, m_norm2_g, m_w_gate_up, m_w_down, v_ada_w, v_ada_b, v_norm1_g, v_w_in, v_conv_w, v_q_norm_g, v_k_norm_g, v_sinks, v_a_log, v_dt_bias, v_dn_norm_g, v_w_branch, v_w_out, v_norm2_g, v_w_gate_up, v_w_down):
    B, S, _ = x.shape
    me = 4 * lax.axis_index("x") + 2 * lax.axis_index("y") + lax.axis_index("c")

    tr = lambda t: jnp.swapaxes(t, 1, 2)
    shards = [w[0].astype(BF16) for w in (tr(w_in), w_branch, w_out, tr(w_gate_up), w_down)]

    c_all = _all_gather_small(c, "gather_c").reshape(N_DEV * B, D)
    ncol = 6 * D // N_DEV
    mod_cols, cond_all = _ada_fwd(c_all, ada_w[0], lax.dynamic_slice(ada_b, (0, me * ncol), (1, ncol)))
    mod_all = _all_gather_small(mod_cols, "gather_mod").transpose(1, 0, 2).reshape(N_DEV * B, 6 * D)
    mod = lax.dynamic_slice(mod_all, (me * B, 0), (B, 6 * D)).reshape(B, 6, D)
    conv2 = conv_w.reshape(CONV, CONVW // N_DEV)
    conv_all = _all_gather_small(conv2, "gather_conv").transpose(1, 0, 2).reshape(CONV, CONVW)

    invf, mean_q, mean_k = _attn_consts()
    w_in_b, rope_cos, rope_sin = _all_gather_big(shards[:1], "gather_w_in", after=(mod, conv_all),
                                                 side=_rope_tables_side(positions.reshape(B, S, 1), invf))
    w_sems, w_srcs, w_lands, w_token = _copies_start(shards[1:], [_place_own(s, me) for s in shards[1:]], False,
                                                    w_in_b, "gather_rest_start")

    w_in_t = w_in_b.reshape(IN_W, D)
    h1, aq, akv, dnx, ba, z, ga, gd = _inproj_fwd(x, mod, norm1_g + w_token[0, 0], w_in_t)
    o_attn = _attn_fwd(aq, akv, rope_cos, rope_sin, q_norm_g, k_norm_g, sinks, mean_q, mean_k)
    cq, dn_u, dn_w, dn_qd, dn_kd, dn_a, dn_t, dn_cd = _dn_prep_fwd(dnx, conv_all, ba, a_log, dt_bias)
    o_dn, states = _dn_seq_fwd(dn_u, dn_w, dn_qd, dn_kd, dn_a, dn_cd)
    w_branch_g, w_out_g, w_gu_b, w_down_g = _copies_wait(w_sems, w_srcs, w_lands, o_dn, "gather_wait_rest")
    w_branch_f = w_branch_g.reshape(D, D)
    w_out_f = w_out_g.reshape(D, D)
    w_down_b = w_down_g.reshape(GU_HALF, GU_SHARD, D)
    x1, mix, merged, ob = _mix_fwd(x, o_attn, o_dn, z, ga, gd, mod, dn_norm_g, w_branch_f, w_out_f)
    h2, act_dgate, act_dup, act = _ffn1_fwd(x1, mod, norm2_g, w_gu_b)
    dy, loss_part, d_gate2 = _ffn2_fwd(act, x1, loss_target, mod, w_down_b)
    loss = lax.psum(loss_part[0, 0], ("x", "y", "c"))

    one = lambda t: t.reshape(B, 1, S, t.shape[-1])
    dgu, dyg = _ffn2_bwd(dy, act_dgate, act_dup, mod, w_down_b)
    g_w_down = _wgrad(act, one(dyg), "wgrad_down")
    dx1, d_n2g, d_scale2, d_shift2 = _ffn1_bwd(dgu, x1, dy, mod, norm2_g, w_gu_b)
    g_w_gu = _wgrad(dgu, one(h2), "wgrad_gate_up")
    ffn = _exchange_start([g_w_gu, g_w_down.reshape(N_DEV, FFN // N_DEV, D)], me, dx1, "exchange_ffn_start")
    dmix, dyo, dga, dgd, dz, d_oa, d_od, d_gate1, d_dng = _mix_bwd(
        dx1, mix, o_attn, o_dn, z, ga, gd, mod, dn_norm_g + ffn[3][0, 0], w_branch_f, w_out_f)
    d_dn = _dn_seq_bwd(dn_u, dn_w, dn_qd, dn_kd, dn_a, dn_cd, states, d_od)
    ddnx, d_conv, dba, d_alog, d_dtb = _dn_prep_bwd(dnx, conv_all, cq, ba, a_log, dt_bias, dn_t, *d_dn)
    daq, dakv, d_qg, d_kg, d_sinks = _attn_bwd(aq, akv, rope_cos, rope_sin, q_norm_g, k_norm_g, sinks, mean_q, mean_k, d_oa)
    dps = [daq, dakv, ddnx, dba, dz, dga, dgd]
    dblk, grad_x, d_n1g, d_scale1, d_shift1 = _inproj_bwd(x, mod, norm1_g, dx1, dps, w_in_t)

    dmod = jnp.concatenate([d_shift1, d_scale1, d_gate1, d_shift2, d_scale2, d_gate2], axis=2).reshape(B, 6 * D)
    small = jnp.concatenate([d_n1g, d_qg, d_kg, d_sinks, d_alog, d_dtb, d_dng, d_n2g, d_conv.reshape(1, CONV * CONVW)], axis=1)
    nsm = small.shape[1]
    width = -(-max(6 * D, nsm) // 128) * 128
    rows = jnp.concatenate([jnp.pad(dmod, ((0, 0), (0, width - 6 * D))), jnp.pad(small, ((0, 8 - B - 1), (0, width - nsm)))], axis=0)
    rows_all = _all_gather_small(rows, "gather_small")
    dmod_all = rows_all[:, 0:B, 0:6 * D].reshape(N_DEV * B, 6 * D)
    dmod_cols = lax.dynamic_slice(dmod_all, (0, me * ncol), (N_DEV * B, ncol))
    grad_ada_w, grad_ada_b, small_sum = _ada_bwd(cond_all, dmod_all, dmod_cols, rows_all[:, B, :])
    sizes = [D, HD, HD, HQ, DH, DH, DK, D]
    so = np.cumsum([0] + sizes)
    g_n1, g_qg, g_kg, g_sk, g_al, g_dt, g_dn, g_n2 = [small_sum[:, so[i]:so[i + 1]] for i in range(8)]
    g_conv_all = small_sum[:, so[8]:so[8] + CONV * CONVW].reshape(CONV, N_DEV, CONVW // N_DEV)
    grad_conv = lax.dynamic_slice(g_conv_all, (0, me, 0), (CONV, 1, CONVW // N_DEV)).reshape(CONV, CONVW // N_DEV)

    half = D // 2
    g_w_in_a = _wgrad(dblk, one(h1), "wgrad_in_a", after=small_sum, b_lanes=(0, half))
    proj_a = _exchange_start([g_w_in_a], me, small_sum, "exchange_in_a_start")
    g_w_in_b = _wgrad(dblk, one(h1), "wgrad_in_b", after=proj_a[3], b_lanes=(1, half))
    proj = _exchange_start([g_w_in_b], me, proj_a[3], "exchange_in_b_start")
    g_w_out = _wgrad(one(merged), one(dmix), "wgrad_out", after=proj[3])
    g_w_branch = _wgrad(ob, dyo, "wgrad_branch", after=proj[3])
    mixer = _exchange_start([g_w_branch.reshape(N_DEV, D // N_DEV, D), g_w_out.reshape(N_DEV, D // N_DEV, D)], me,
                            proj[3], "exchange_mix_start")

    upd, grads = {}, {}

    def finish(names, parts, weights):
        for nm, p, (w, m, v) in zip(names, parts, weights):
            grads[nm], *upd[nm] = _sum_adamw(p, w, m, v, "update_" + nm)
        return grads[names[-1]]

    finish(["w_gate_up", "w_down"], _copies_wait(*ffn[:3], mixer[3], "exchange_ffn_wait"),
           [(tr(w_gate_up), tr(m_w_gate_up), tr(v_w_gate_up)), (w_down, m_w_down, v_w_down)])
    (in_a,) = _copies_wait(*proj_a[:3], grads["w_gate_up"], "exchange_in_a_wait")
    (in_b,) = _copies_wait(*proj[:3], in_a, "exchange_in_b_wait")
    finish(["w_in"], [[in_a, in_b]], [(tr(w_in), tr(m_w_in), tr(v_w_in))])
    finish(["w_branch", "w_out"], _copies_wait(*mixer[:3], grads["w_in"], "exchange_mix_wait"),
           [(w_branch, m_w_branch, v_w_branch), (w_out, m_w_out, v_w_out)])
    for nm in ("w_in", "w_gate_up"):
        grads[nm], upd[nm] = tr(grads[nm]), [tr(t) for t in upd[nm]]

    grads["ada_w"] = grad_ada_w.reshape(ada_w.shape)
    upd["ada_w"] = _adamw(ada_w, grads["ada_w"], m_ada_w, v_ada_w, "adamw_ada_w")
    small_names = ["ada_b", "norm1_g", "q_norm_g", "k_norm_g", "sinks", "a_log", "dt_bias", "dn_norm_g", "norm2_g", "conv_w"]
    small_w = [ada_b, norm1_g, q_norm_g, k_norm_g, sinks, a_log, dt_bias, dn_norm_g, norm2_g, conv_w]
    small_g = [grad_ada_b, g_n1, g_qg, g_kg, g_sk, g_al, g_dt, g_dn, g_n2, grad_conv]
    small_m = [m_ada_b, m_norm1_g, m_q_norm_g, m_k_norm_g, m_sinks, m_a_log, m_dt_bias, m_dn_norm_g, m_norm2_g, m_conv_w]
    small_v = [v_ada_b, v_norm1_g, v_q_norm_g, v_k_norm_g, v_sinks, v_a_log, v_dt_bias, v_dn_norm_g, v_norm2_g, v_conv_w]
    cat = lambda arrs: jnp.concatenate([a.reshape(1, -1) for a in arrs], axis=1)
    res = _adamw(cat(small_w), cat(small_g), cat(small_m), cat(small_v), "adamw_small")
    po = np.cumsum([0] + [int(np.prod(w.shape)) for w in small_w])
    for i, nm in enumerate(small_names):
        upd[nm] = tuple(r[:, po[i]:po[i + 1]].reshape(small_w[i].shape) for r in res)
        grads[nm] = small_g[i].reshape(small_w[i].shape)

    order = ["ada_w", "ada_b", "norm1_g", "w_in", "conv_w", "q_norm_g", "k_norm_g", "sinks", "a_log", "dt_bias",
             "dn_norm_g", "w_branch", "w_out", "norm2_g", "w_gate_up", "w_down"]
    return (loss, grad_x, *[grads[n] for n in order], *[upd[n][0] for n in order],
            *[upd[n][1] for n in order], *[upd[n][2] for n in order])
```

```python
import functools

import numpy as np
import jax
import jax.numpy as jnp
from jax import lax
from jax.experimental import pallas as pl
from jax.experimental.pallas import tpu as pltpu

F32 = jnp.float32
BF16 = jnp.bfloat16
HI = lax.Precision.HIGHEST

N_DEV = 8
D = 1024
HQ, HKV, HD = 8, 2, 64
GRP = HQ // HKV
BLK = 128
ROT = HD // 4
THETA = 500000.0
QW, KVW = HQ * HD, HKV * HD
DH, DK = 4, 128
CH = 64
DNW = DH * DK
CONV = 4
CONVW = 3 * DNW
FFN = 2816
EPS = 1e-6
IN_W = QW + 2 * KVW + CONVW + 2 * DH + DNW + 2 * D

LR, B1, B2, AEPS, WD, STEP = 0.001, 0.9, 0.999, 1e-08, 0.01, 10

VMEM_LIMIT = 56 * 1024 * 1024
MESH = pl.DeviceIdType.MESH


def _cparams(sem=None, vmem=VMEM_LIMIT):
    return pltpu.CompilerParams(dimension_semantics=sem, vmem_limit_bytes=vmem)


def _full(shape):
    n = len(shape)
    return pl.BlockSpec(shape, lambda *_: (0,) * n)


def _resident(shape):
    n = len(shape)
    return pl.BlockSpec(shape, lambda *_: (0,) * n, pipeline_mode=pl.Buffered(1))


def _rows(tm, w):
    return pl.BlockSpec((None, tm, w), lambda b, i: (b, i, 0))


def _stacked(n, tm, w):
    return pl.BlockSpec((None, n, tm, w), lambda b, i: (b, 0, i, 0))


def _perb(r, w):
    return pl.BlockSpec((None, r, w), lambda b, i: (b, 0, 0))


def _dot(a, b):
    return jnp.dot(a.astype(BF16), b.astype(BF16), preferred_element_type=F32)


def _dot_nt(a, b):
    return lax.dot_general(a.astype(BF16), b.astype(BF16), (((1,), (1,)), ((), ())), preferred_element_type=F32)


def _dot_tn(a, b):
    return lax.dot_general(a.astype(BF16), b.astype(BF16), (((0,), (0,)), ((), ())), preferred_element_type=F32)


def _dot_hi(a, b):
    return jnp.dot(a, b, preferred_element_type=F32, precision=HI)


def _sigmoid(x):
    return jax.nn.sigmoid(x)


def _silu(x):
    return x * jax.nn.sigmoid(x)


def _rms_mod(x, g, scale, shift):
    r = lax.rsqrt(jnp.mean(x * x, axis=-1, keepdims=True) + EPS)
    return (x * r * g) * (1.0 + scale) + shift


def _tile(S, rows=256):
    return min(rows, S)


def _peer(x, y, c, k):
    px = 1 - x if (k >> 2) & 1 else x
    py = 1 - y if (k >> 1) & 1 else y
    pc = 1 - c if k & 1 else c
    return px, py, pc


def _all_gather_small(v, name):
    r, n = v.shape

    def body(v_ref, out_ref, send_sems, recv_sems, local_sem):
        x, y, c = lax.axis_index("x"), lax.axis_index("y"), lax.axis_index("c")
        me = 4 * x + 2 * y + c
        mine = pltpu.make_async_copy(v_ref, out_ref.at[me], local_sem)
        mine.start()
        sends = []
        for k in range(1, N_DEV):
            cp = pltpu.make_async_remote_copy(
                src_ref=v_ref, dst_ref=out_ref.at[me], send_sem=send_sems.at[k - 1], recv_sem=recv_sems.at[k - 1],
                device_id=_peer(x, y, c, k), device_id_type=MESH)
            cp.start()
            sends.append(cp)
        for k in range(1, N_DEV):
            px, py, pc = _peer(x, y, c, k)
            pltpu.make_async_remote_copy(
                src_ref=v_ref, dst_ref=out_ref.at[4 * px + 2 * py + pc], send_sem=send_sems.at[k - 1],
                recv_sem=recv_sems.at[k - 1], device_id=(px, py, pc), device_id_type=MESH).wait_recv()
        for cp in sends:
            cp.wait_send()
        mine.wait()

    return pl.pallas_call(
        body, name=name,
        out_shape=jax.ShapeDtypeStruct((N_DEV, r, n), v.dtype),
        in_specs=[pl.BlockSpec(memory_space=pltpu.VMEM)],
        out_specs=pl.BlockSpec(memory_space=pltpu.VMEM),
        scratch_shapes=[pltpu.SemaphoreType.DMA((N_DEV - 1,)), pltpu.SemaphoreType.DMA((N_DEV - 1,)), pltpu.SemaphoreType.DMA],
    )(v)


def _all_gather_big(vs, name, after=(), side=None):
    na, nf = len(vs), len(after)
    side_fn, side_in, side_out = side if side is not None else (None, (), ())
    ns, no = len(side_in), len(side_out)

    def body(*refs):
        v_refs, out_refs = refs[:na], refs[na + nf + ns:2 * na + nf + ns]
        send_sems, recv_sems, local_sems = refs[2 * na + nf + ns + no:]
        x, y, c = lax.axis_index("x"), lax.axis_index("y"), lax.axis_index("c")
        me, sibling = (x, y, c), (x, y, 1 - c)
        chips = [(1 - x, y), (x, 1 - y), (1 - x, 1 - y)]

        def rows(a, px, py, pc):
            return out_refs[a].at[4 * px + 2 * py + pc]

        def copy(a, k, block, to, src=None):
            return pltpu.make_async_remote_copy(
                src_ref=rows(a, *block) if src is None else src, dst_ref=rows(a, *block),
                send_sem=send_sems.at[7 * a + k], recv_sem=recv_sems.at[7 * a + k], device_id=to, device_id_type=MESH)

        mine = [pltpu.make_async_copy(v_refs[a], rows(a, *me), local_sems.at[a]) for a in range(na)]
        for cp in mine:
            cp.start()
        first = []
        for a in range(na):
            first.append(copy(a, 0, me, sibling, src=v_refs[a]))
            first += [copy(a, 1 + j, me, (*chip, c), src=v_refs[a]) for j, chip in enumerate(chips)]
        for cp in first:
            cp.start()
        if side_fn is not None:
            side_fn(refs[na + nf:na + nf + ns], refs[2 * na + nf + ns:2 * na + nf + ns + no])
        passed = []
        for j, chip in enumerate(chips):
            for a in range(na):
                copy(a, 1 + j, (*chip, c), me).wait_recv()
                forward = copy(a, 4 + j, (*chip, c), sibling)
                forward.start()
                passed.append(forward)
        for a in range(na):
            copy(a, 0, sibling, me).wait_recv()
            for j, chip in enumerate(chips):
                copy(a, 4 + j, (*chip, 1 - c), me).wait_recv()
        for cp in first + passed:
            cp.wait_send()
        for cp in mine:
            cp.wait()

    return pl.pallas_call(
        body, name=name,
        out_shape=[jax.ShapeDtypeStruct((N_DEV,) + v.shape, v.dtype) for v in vs] + list(side_out),
        in_specs=[pl.BlockSpec(memory_space=pl.ANY)] * (na + nf) + [pl.BlockSpec(memory_space=pltpu.VMEM)] * ns,
        out_specs=[pl.BlockSpec(memory_space=pl.ANY)] * na + [pl.BlockSpec(memory_space=pltpu.VMEM)] * no,
        scratch_shapes=[pltpu.SemaphoreType.DMA((7 * na,)), pltpu.SemaphoreType.DMA((7 * na,)),
                        pltpu.SemaphoreType.DMA((na,))],
        compiler_params=pltpu.CompilerParams(vmem_limit_bytes=VMEM_LIMIT),
    )(*vs, *after, *side_in)


_HBM = pl.BlockSpec(memory_space=pltpu.HBM)
_SEM = pl.BlockSpec(memory_space=pltpu.SEMAPHORE)
_EFFECT = pltpu.SideEffectType.DATAFLOW_SIDE_EFFECTING


def _place_own(block, me):
    land = lax.empty((N_DEV,) + block.shape, block.dtype)
    return lax.dynamic_update_slice(land, block[None], (me,) + (0,) * block.ndim)


def _copies_start(srcs, lands, scatter, after, name):
    na = len(srcs)
    afters = tuple(after) if isinstance(after, (tuple, list)) else (after,)

    def body(*refs):
        src_refs, land_refs = refs[:na], refs[na:2 * na]
        sems = refs[2 * na + len(afters):4 * na + len(afters)]
        token = refs[-1]
        x, y, c = lax.axis_index("x"), lax.axis_index("y"), lax.axis_index("c")
        me = 4 * x + 2 * y + c
        for a in range(na):
            for k in range(1, N_DEV):
                px, py, pc = _peer(x, y, c, k)
                src = src_refs[a].at[4 * px + 2 * py + pc] if scatter else src_refs[a]
                pltpu.make_async_remote_copy(
                    src_ref=src, dst_ref=land_refs[a].at[me], send_sem=sems[2 * a], recv_sem=sems[2 * a + 1],
                    device_id=(px, py, pc), device_id_type=MESH).start()
        token[...] = jnp.zeros_like(token)

    hbm = lambda t: pltpu.HBM(t.shape, t.dtype)
    out = pl.pallas_call(
        body, name=name,
        out_shape=tuple([pltpu.SemaphoreType.DMA(())] * (2 * na) + [hbm(t) for t in srcs] + [hbm(t) for t in lands]
                        + [jax.ShapeDtypeStruct((8, 128), F32)]),
        in_specs=[_HBM] * (2 * na) + [pl.BlockSpec(memory_space=pl.ANY)] * len(afters),
        out_specs=tuple([_SEM] * (2 * na) + [_HBM] * (2 * na) + [pl.BlockSpec(memory_space=pltpu.VMEM)]),
        input_output_aliases={i: 2 * na + i for i in range(2 * na)},
        compiler_params=pltpu.CompilerParams(has_side_effects=_EFFECT),
    )(*[pltpu.with_memory_space_constraint(t, pltpu.HBM) for t in list(srcs) + list(lands)], *afters)
    return out[:2 * na], out[2 * na:3 * na], out[3 * na:4 * na], out[-1]


def _exchange_start(gs, me, after, name):
    own = [lax.dynamic_index_in_dim(g, me, 0, keepdims=False) for g in gs]
    return _copies_start(gs, [_place_own(o, me) for o in own], True, after, name)


def _copies_wait(sems, srcs, lands, after, name):
    na = len(srcs)

    def body(*refs):
        land_refs = refs[na:2 * na]
        sem_refs = refs[2 * na:4 * na]
        x, y, c = lax.axis_index("x"), lax.axis_index("y"), lax.axis_index("c")
        for a in range(na):
            seven = land_refs[a].at[pl.ds(0, N_DEV - 1)]
            copy = pltpu.make_async_remote_copy(
                src_ref=seven, dst_ref=seven, send_sem=sem_refs[2 * a], recv_sem=sem_refs[2 * a + 1],
                device_id=(x, y, c), device_id_type=MESH)
            copy.wait_send()
            copy.wait_recv()

    hbm = lambda t: pltpu.HBM(t.shape, t.dtype)
    out = pl.pallas_call(
        body, name=name,
        out_shape=tuple([hbm(t) for t in srcs] + [hbm(t) for t in lands]),
        in_specs=[_HBM] * (2 * na) + [_SEM] * (2 * na) + [pl.BlockSpec(memory_space=pl.ANY)],
        out_specs=tuple([_HBM] * (2 * na)),
        input_output_aliases={i: i for i in range(2 * na)},
        compiler_params=pltpu.CompilerParams(has_side_effects=_EFFECT),
    )(*srcs, *lands, *sems, after)
    return out[na:]


def _adamw_math(w, g, m, v):
    m = B1 * m + (1.0 - B1) * g
    v = B2 * v + (1.0 - B2) * (g * g)
    m_hat = m / (1.0 - B1 ** STEP)
    v_hat = v / (1.0 - B2 ** STEP)
    return -LR * (m_hat / (jnp.sqrt(v_hat) + AEPS) + WD * w), m, v


def _sum_adamw(parts, w, m, v, name):
    parts = list(parts) if isinstance(parts, (list, tuple)) else [parts]
    r, n = w.shape[1], w.shape[2]
    tr = 256 if r % 256 == 0 else r
    npart = len(parts)

    def body(*refs):
        p_refs = refs[:npart]
        w_ref, m_ref, v_ref, g_ref, d_ref, nm_ref, nv_ref = refs[npart:]
        pieces = []
        for p_ref in p_refs:
            g = p_ref[0].astype(F32)
            for dev in range(1, N_DEV):
                g = g + p_ref[dev].astype(F32)
            pieces.append(g)
        g = pieces[0] if npart == 1 else jnp.concatenate(pieces, axis=1)
        g_ref[...] = g
        d_ref[...], nm_ref[...], nv_ref[...] = _adamw_math(w_ref[...], g, m_ref[...], v_ref[...])

    rows = pl.BlockSpec((None, tr, n), lambda i: (0, i, 0))
    sd = jax.ShapeDtypeStruct((1, r, n), F32)
    return pl.pallas_call(
        body, name=name, grid=(r // tr,), out_shape=(sd, sd, sd, sd),
        in_specs=[pl.BlockSpec((N_DEV, tr, p.shape[2]), lambda i: (0, i, 0)) for p in parts] + [rows, rows, rows],
        out_specs=(rows, rows, rows, rows),
        compiler_params=_cparams(("parallel",)),
    )(*parts, w, m, v)


def _ada_fwd(c_all, ada_w, ada_b_cols):
    nb, ncol = c_all.shape[0], ada_w.shape[1]

    def body(c_ref, w_ref, b_ref, mod_ref, cond_ref):
        cond = _silu(c_ref[...])
        cond_ref[...] = cond
        mod_ref[...] = _dot_hi(cond, w_ref[...]) + b_ref[...]

    return pl.pallas_call(
        body, name="ada_fwd",
        out_shape=(jax.ShapeDtypeStruct((nb, ncol), F32), jax.ShapeDtypeStruct((nb, D), F32)),
        compiler_params=_cparams(),
    )(c_all, ada_w, ada_b_cols)


def _ada_bwd(cond_all, dmod_all, dmod_cols, smalls):
    ncol, nsm = dmod_cols.shape[1], smalls.shape[1]

    def body(cond_ref, dm_ref, dmc_ref, sm_ref, gw_ref, gb_ref, gs_ref):
        gw_ref[...] = lax.dot_general(cond_ref[...], dmc_ref[...], (((0,), (0,)), ((), ())),
                                      preferred_element_type=F32, precision=HI)
        gb_ref[...] = jnp.sum(dm_ref[...], axis=0, keepdims=True)
        gs_ref[...] = jnp.sum(sm_ref[...], axis=0, keepdims=True)

    return pl.pallas_call(
        body, name="ada_bwd",
        out_shape=(jax.ShapeDtypeStruct((D, ncol), F32), jax.ShapeDtypeStruct((1, 6 * D), F32),
                   jax.ShapeDtypeStruct((1, nsm), F32)),
        compiler_params=_cparams(),
    )(cond_all, dmod_all, dmod_cols, smalls)


IN_CUTS = (0, QW, QW + 2 * KVW, QW + 2 * KVW + CONVW, QW + 2 * KVW + CONVW + 2 * DH,
           QW + 2 * KVW + CONVW + 2 * DH + DNW, QW + 2 * KVW + CONVW + 2 * DH + DNW + D, IN_W)
IN_WIDTHS = tuple(b - a for a, b in zip(IN_CUTS[:-1], IN_CUTS[1:]))
IN_SHARD = IN_W // N_DEV


def _inproj_fwd(x, mod, g1, w_t):
    B, S, _ = x.shape
    tm = _tile(S)

    def body(x_ref, mod_ref, g_ref, w_ref, h_ref, *o_refs):
        h = _rms_mod(x_ref[...], g_ref[...], mod_ref[1:2, :], mod_ref[0:1, :]).astype(BF16)
        h_ref[...] = h
        full = _dot_nt(h, w_ref[...])
        for o_ref, lo, hi in zip(o_refs, IN_CUTS[:-1], IN_CUTS[1:]):
            o_ref[...] = full[:, lo:hi]

    return pl.pallas_call(
        body, name="inproj_fwd", grid=(B, S // tm),
        out_shape=[jax.ShapeDtypeStruct((B, S, D), BF16)] + [jax.ShapeDtypeStruct((B, S, w), F32) for w in IN_WIDTHS],
        in_specs=[_rows(tm, D), _perb(6, D), _full((1, D)), _resident(w_t.shape)],
        out_specs=[_rows(tm, D)] + [_rows(tm, w) for w in IN_WIDTHS],
        compiler_params=_cparams(("parallel", "arbitrary")),
    )(x, mod, g1, w_t)


def _inproj_bwd(x, mod, g1, dx1, dps, w_t):
    B, S, _ = x.shape
    tm = _tile(S)
    n = len(dps)

    def body(x_ref, mod_ref, g_ref, dx1_ref, *refs):
        dp_refs, w_ref = refs[:n], refs[n]
        dblk_ref, gx_ref, dg_ref, dsc_ref, dsh_ref = refs[n + 1:]
        b, i = pl.program_id(0), pl.program_id(1)
        full = jnp.concatenate([r[...].astype(F32) for r in dp_refs], axis=1)
        for j in range(N_DEV):
            dblk_ref[j] = full[:, IN_SHARD * j:IN_SHARD * (j + 1)].astype(BF16)
        dh = jnp.dot(full.astype(BF16), w_ref[...], preferred_element_type=F32)
        _, vjp = jax.vjp(_rms_mod, x_ref[...], g_ref[...], mod_ref[1:2, :], mod_ref[0:1, :])
        dx, dg, dsc, dsh = vjp(dh)
        gx_ref[...] = dx1_ref[...] + dx

        @pl.when((b == 0) & (i == 0))
        def _():
            dg_ref[...] = jnp.zeros_like(dg_ref)

        @pl.when(i == 0)
        def _():
            dsc_ref[...] = jnp.zeros_like(dsc_ref)
            dsh_ref[...] = jnp.zeros_like(dsh_ref)

        dg_ref[...] += dg
        dsc_ref[...] += dsc
        dsh_ref[...] += dsh

    return pl.pallas_call(
        body, name="inproj_bwd", grid=(B, S // tm),
        out_shape=[jax.ShapeDtypeStruct((B, N_DEV, S, IN_SHARD), BF16), jax.ShapeDtypeStruct((B, S, D), F32),
                   jax.ShapeDtypeStruct((1, D), F32), jax.ShapeDtypeStruct((B, 1, D), F32),
                   jax.ShapeDtypeStruct((B, 1, D), F32)],
        in_specs=[_rows(tm, D), _perb(6, D), _full((1, D)), _rows(tm, D)]
                 + [_rows(tm, w) for w in IN_WIDTHS] + [_resident(w_t.shape)],
        out_specs=[pl.BlockSpec((None, N_DEV, tm, IN_SHARD), lambda b, i: (b, 0, i, 0)), _rows(tm, D),
                   _full((1, D)), _perb(1, D), _perb(1, D)],
        compiler_params=_cparams(("arbitrary", "arbitrary")),
    )(x, mod, g1, dx1, *dps, w_t)


def _wgrad(a, b, name, after=None, b_lanes=None):
    B, na, S, K = a.shape
    nb, N = b.shape[1], b.shape[3]
    lane_blk = 0
    if b_lanes is not None:
        lane_blk, N = b_lanes
    G = max(na, nb)
    tm = min(4096, S)
    nt = S // tm
    last = B * nt - 1

    def body(a_ref, b_ref, *rest):
        o_ref, acc = rest[-2:]
        t = pl.program_id(1)

        @pl.when(t == 0)
        def _():
            acc[...] = jnp.zeros_like(acc)

        acc[...] += lax.dot_general(a_ref[...], b_ref[...], (((0,), (0,)), ((), ())), preferred_element_type=F32)

        @pl.when(t == last)
        def _():
            o_ref[...] = acc[...].astype(BF16)

    return pl.pallas_call(
        body, name=name, grid=(G, B * nt),
        out_shape=jax.ShapeDtypeStruct((G, K, N), BF16),
        in_specs=[pl.BlockSpec((None, None, tm, K), lambda g, t: (t // nt, g if na > 1 else 0, t % nt, 0)),
                  pl.BlockSpec((None, None, tm, N), lambda g, t: (t // nt, g if nb > 1 else 0, t % nt, lane_blk))]
                 + ([] if after is None else [pl.BlockSpec(memory_space=pl.ANY)]),
        out_specs=pl.BlockSpec((None, K, N), lambda g, t: (g, 0, 0)),
        scratch_shapes=[pltpu.VMEM((K, N), F32)],
        compiler_params=_cparams(("parallel", "arbitrary")),
    )(*((a, b) if after is None else (a, b, after)))


LANES = 128


def _attn_consts():
    inv_freq = THETA ** (-jnp.arange(0, ROT, 2, dtype=F32) / ROT)
    head = jnp.concatenate([inv_freq, inv_freq, jnp.zeros((HD - ROT,), F32)])
    invf = jnp.tile(head, LANES // HD)[None, :]
    mean_of = lambda w: jnp.asarray(np.kron(np.eye(w // HD), np.full((HD, HD), 1.0 / HD)), BF16)
    return invf, mean_of(QW), mean_of(KVW)


def _rope_tables_side(pos, invf):
    B, S, _ = pos.shape
    tr = min(512, S)

    def fn(ins, outs):
        p_ref, f_ref = ins
        c_ref, s_ref = outs
        for b in range(B):
            for r in range(0, S, tr):
                ang = p_ref[b, r:r + tr, :].astype(F32) * f_ref[...]
                c_ref[b, r:r + tr, :] = jnp.cos(ang)
                s_ref[b, r:r + tr, :] = jnp.sin(ang)

    sd = jax.ShapeDtypeStruct((B, S, LANES), F32)
    return fn, (pos, invf), (sd, sd)


def _rope_expand(cos, sin, reps):
    lane = lax.broadcasted_iota(jnp.int32, cos.shape, 1) % HD
    sa = jnp.where((lane >= ROT // 2) & (lane < ROT), sin, 0.0)
    sb = jnp.where(lane < ROT // 2, -sin, 0.0)
    rep = lambda t: jnp.concatenate([t] * reps, axis=1) if reps > 1 else t
    return rep(cos), rep(sa), rep(sb)


@jax.custom_vjp
def _rope(t, cos, sa, sb):
    w = t.shape[1]
    return t * cos + pltpu.roll(t, ROT // 2, 1) * sa + pltpu.roll(t, w - ROT // 2, 1) * sb


def _rope_fwd(t, cos, sa, sb):
    return _rope(t, cos, sa, sb), (cos, sa, sb)


def _rope_bwd(res, d):
    cos, sa, sb = res
    w = d.shape[1]
    dt = d * cos + pltpu.roll(d * sa, w - ROT // 2, 1) + pltpu.roll(d * sb, ROT // 2, 1)
    return dt, jnp.zeros_like(cos), jnp.zeros_like(sa), jnp.zeros_like(sb)


_rope.defvjp(_rope_fwd, _rope_bwd)


def _head_norm(t, g, mean_of):
    hi, lo = _split(t * t)
    ms = jnp.dot(hi, mean_of, preferred_element_type=F32) + jnp.dot(lo, mean_of, preferred_element_type=F32)
    return t * lax.rsqrt(ms + EPS) * g


def _attn_block(q, kvp, kvc, qg, kg, sinks, tq, tk, mq, mk, valid):
    qn = _rope(_head_norm(q, jnp.concatenate([qg] * HQ, axis=1), mq), *tq) * (HD ** -0.5)
    kv = jnp.concatenate([kvp, kvc], axis=0)
    kn = _rope(_head_norm(kv[:, 0:KVW], jnp.concatenate([kg] * HKV, axis=1), mk), *tk)
    per_tile = LANES // HD
    vT = jnp.transpose(kv[:, KVW:2 * KVW])
    qT = [jnp.transpose(qn[:, LANES * t:LANES * (t + 1)]) for t in range(QW // LANES)]
    head_T = lambda h: qT[h // per_tile][HD * (h % per_tile):HD * (h % per_tile + 1), :]
    none = jnp.zeros((HD, GRP * BLK), F32)
    o_T = []
    for j in range(HKV):
        q4T = jnp.concatenate([head_T(GRP * j + i) for i in range(GRP)], axis=1)
        sT = _dot(kn, jnp.concatenate([q4T, none] if j == 0 else [none, q4T], axis=0))
        sT = jnp.where(valid, sT, -1e30)
        sink = jnp.concatenate([jnp.broadcast_to(sinks[:, GRP * j + i:GRP * j + i + 1], (1, BLK)) for i in range(GRP)], axis=1)
        m = lax.stop_gradient(jnp.maximum(jnp.max(sT, axis=0, keepdims=True), sink))
        pT = jnp.exp(sT - m)
        den = jnp.sum(pT, axis=0, keepdims=True) + jnp.exp(sink - m)
        oT = _dot(vT[HD * j:HD * (j + 1), :], pT) * (1.0 / den)
        o_T += [oT[:, BLK * i:BLK * (i + 1)] for i in range(GRP)]
    return jnp.concatenate([jnp.transpose(jnp.concatenate(o_T[per_tile * t:per_tile * (t + 1)], axis=0))
                            for t in range(QW // LANES)], axis=1)


def _attn_tables(cp_ref, cc_ref, sp_ref, sc_ref, n):
    tq = _rope_expand(cc_ref[...], sc_ref[...], QW // LANES)
    tk = _rope_expand(jnp.concatenate([cp_ref[...], cc_ref[...]], axis=0),
                      jnp.concatenate([sp_ref[...], sc_ref[...]], axis=0), KVW // LANES)
    qi = lax.broadcasted_iota(jnp.int32, (2 * BLK, GRP * BLK), 1) % BLK + BLK
    kj = lax.broadcasted_iota(jnp.int32, (2 * BLK, GRP * BLK), 0)
    dist = qi - kj
    valid = (dist >= 0) & (dist < BLK) & ((kj >= BLK) | (n > 0))
    return tq, tk, valid


def _attn_fwd(aq, akv, cos, sin, qg, kg, sinks, mq, mk):
    B, S, _ = aq.shape
    nb = S // BLK

    def body(q_ref, kvp_ref, kvc_ref, cp_ref, cc_ref, sp_ref, sc_ref, qg_ref, kg_ref, sk_ref, mq_ref, mk_ref, o_ref):
        tq, tk, valid = _attn_tables(cp_ref, cc_ref, sp_ref, sc_ref, pl.program_id(1))
        o_ref[...] = _attn_block(q_ref[...], kvp_ref[...], kvc_ref[...], qg_ref[...], kg_ref[...], sk_ref[...],
                                 tq, tk, mq_ref[...], mk_ref[...], valid)

    prev = lambda b, n: (b, jnp.maximum(n - 1, 0), 0)
    cur = lambda b, n: (b, n, 0)
    return pl.pallas_call(
        body, name="attn_fwd", grid=(B, nb),
        out_shape=jax.ShapeDtypeStruct((B, S, QW), F32),
        in_specs=[pl.BlockSpec((None, BLK, QW), cur), pl.BlockSpec((None, BLK, 2 * KVW), prev),
                  pl.BlockSpec((None, BLK, 2 * KVW), cur), pl.BlockSpec((None, BLK, LANES), prev),
                  pl.BlockSpec((None, BLK, LANES), cur), pl.BlockSpec((None, BLK, LANES), prev),
                  pl.BlockSpec((None, BLK, LANES), cur), _full((1, HD)), _full((1, HD)), _full((1, HQ)),
                  _full((QW, QW)), _full((KVW, KVW))],
        out_specs=pl.BlockSpec((None, BLK, QW), cur),
        compiler_params=_cparams(("parallel", "arbitrary")),
    )(aq, akv, akv, cos, cos, sin, sin, qg, kg, sinks, mq, mk)


def _attn_bwd(aq, akv, cos, sin, qg, kg, sinks, mq, mk, do):
    B, S, _ = aq.shape
    nb = S // BLK

    def body(q_ref, kvp_ref, kvc_ref, cp_ref, cc_ref, sp_ref, sc_ref, qg_ref, kg_ref, sk_ref, mq_ref, mk_ref, do_ref,
             dq_ref, dkv_ref, dqg_ref, dkg_ref, dsk_ref, carry):
        b, i = pl.program_id(0), pl.program_id(1)
        tq, tk, valid = _attn_tables(cp_ref, cc_ref, sp_ref, sc_ref, nb - 1 - i)
        fn = functools.partial(_attn_block, tq=tq, tk=tk, mq=mq_ref[...], mk=mk_ref[...], valid=valid)
        _, vjp = jax.vjp(fn, q_ref[...], kvp_ref[...], kvc_ref[...], qg_ref[...], kg_ref[...], sk_ref[...])
        dq, dkvp, dkvc, dqg, dkg, dsk = vjp(do_ref[...])

        @pl.when(i == 0)
        def _():
            carry[...] = jnp.zeros_like(carry)

        @pl.when((b == 0) & (i == 0))
        def _():
            dqg_ref[...] = jnp.zeros_like(dqg_ref)
            dkg_ref[...] = jnp.zeros_like(dkg_ref)
            dsk_ref[...] = jnp.zeros_like(dsk_ref)

        dq_ref[...] = dq.astype(BF16)
        dkv_ref[...] = (dkvc + carry[...]).astype(BF16)
        carry[...] = dkvp
        dqg_ref[...] += dqg
        dkg_ref[...] += dkg
        dsk_ref[...] += dsk

    prev = lambda b, i: (b, jnp.maximum(nb - 2 - i, 0), 0)
    cur = lambda b, i: (b, nb - 1 - i, 0)
    return pl.pallas_call(
        body, name="attn_bwd", grid=(B, nb),
        out_shape=[jax.ShapeDtypeStruct((B, S, QW), BF16), jax.ShapeDtypeStruct((B, S, 2 * KVW), BF16),
                   jax.ShapeDtypeStruct((1, HD), F32), jax.ShapeDtypeStruct((1, HD), F32),
                   jax.ShapeDtypeStruct((1, HQ), F32)],
        in_specs=[pl.BlockSpec((None, BLK, QW), cur), pl.BlockSpec((None, BLK, 2 * KVW), prev),
                  pl.BlockSpec((None, BLK, 2 * KVW), cur), pl.BlockSpec((None, BLK, LANES), prev),
                  pl.BlockSpec((None, BLK, LANES), cur), pl.BlockSpec((None, BLK, LANES), prev),
                  pl.BlockSpec((None, BLK, LANES), cur), _full((1, HD)), _full((1, HD)), _full((1, HQ)),
                  _full((QW, QW)), _full((KVW, KVW)), pl.BlockSpec((None, BLK, QW), cur)],
        out_specs=[pl.BlockSpec((None, BLK, QW), cur), pl.BlockSpec((None, BLK, 2 * KVW), cur),
                   _full((1, HD)), _full((1, HD)), _full((1, HQ))],
        scratch_shapes=[pltpu.VMEM((BLK, 2 * KVW), F32)],
        compiler_params=_cparams(("arbitrary", "arbitrary")),
    )(aq, akv, akv, cos, cos, sin, sin, qg, kg, sinks, mq, mk, do)


def _conv_taps(xe, w, rows):
    y = None
    for j in range(CONV):
        sh = pltpu.roll(xe, CONV - 1 - j, 0)[8:8 + rows, :] if j < CONV - 1 else xe[8:8 + rows, :]
        y = sh * w[j:j + 1, :] if y is None else y + sh * w[j:j + 1, :]
    return y


def _softplus(x):
    return jnp.maximum(x, 0.0) + jnp.log1p(jnp.exp(-jnp.abs(x)))


_BMM = (((2,), (1,)), ((0,), (0,)))
_BMM_NT = (((2,), (2,)), ((0,), (0,)))
_BMM_TN = (((1,), (1,)), ((0,), (0,)))


def _bmm(a, b, dims=_BMM):
    return lax.dot_general(a.astype(BF16), b.astype(BF16), dims, preferred_element_type=F32)


def _split(a):
    hi = a.astype(BF16)
    return hi, (a - hi.astype(F32)).astype(BF16)


def _bmm3(a, b, dims=_BMM):
    ah, al = _split(a)
    bh, bl = _split(b)
    d = lambda p, q: lax.dot_general(p, q, dims, preferred_element_type=F32)
    return d(ah, bh) + (d(ah, bl) + d(al, bh))


TRI_BASE = 8


def _tri_inverse(L):
    ii = lax.broadcasted_iota(jnp.int32, (CH, CH), 0)
    jj = lax.broadcasted_iota(jnp.int32, (CH, CH), 1)
    same = lambda size: (ii // size) == (jj // size)
    diag = jnp.where(same(TRI_BASE), L, 0.0)
    X = (ii == jj).astype(F32) - diag
    P = diag
    n = 2
    while n < TRI_BASE:
        P = _bmm3(P, P)
        X = X + _bmm3(X, P)
        n *= 2
    size = TRI_BASE
    while size < CH:
        joint = jnp.where(same(2 * size) & jnp.logical_not(same(size)), L, 0.0)
        X = X - _bmm3(X, _bmm3(joint, X))
        size *= 2
    return X


@jax.custom_vjp
def _tri_inverse_known(L, T):
    return T


def _tri_inverse_known_fwd(L, T):
    return T, T


def _tri_inverse_known_bwd(T, dT):
    Tt = jnp.swapaxes(T, 1, 2)
    return -_bmm(Tt, _bmm(dT, Tt)), jnp.zeros_like(T)


_tri_inverse_known.defvjp(_tri_inverse_known_fwd, _tri_inverse_known_bwd)


def _triangle(n, upper):
    ii = lax.broadcasted_iota(jnp.int32, (n, CH, CH), 1)
    jj = lax.broadcasted_iota(jnp.int32, (n, CH, CH), 2)
    return ((ii <= jj) if upper else (ii >= jj)).astype(BF16)


@jax.custom_vjp
def _cumsum_rows(g):
    g0 = g.astype(BF16)
    r1 = g - g0.astype(F32)
    g1 = r1.astype(BF16)
    g2 = (r1 - g1.astype(F32)).astype(BF16)
    tri = _triangle(g.shape[0], False)
    d = lambda q: lax.dot_general(tri, q, _BMM, preferred_element_type=F32)
    return d(g0) + (d(g1) + d(g2))


def _cumsum_rows_fwd(g):
    return _cumsum_rows(g), None


def _cumsum_rows_bwd(_, dy):
    hi, lo = _split(dy)
    tri = _triangle(dy.shape[0], True)
    d = lambda q: lax.dot_general(tri, q, _BMM, preferred_element_type=F32)
    return (d(hi) + d(lo),)


_cumsum_rows.defvjp(_cumsum_rows_fwd, _cumsum_rows_bwd)


def _row_sums(t):
    n, r, w = t.shape
    hi, lo = _split(t.reshape(n * r, w))
    ones = jnp.ones((w, w), BF16)
    s = jnp.dot(hi, ones, preferred_element_type=F32) + jnp.dot(lo, ones, preferred_element_type=F32)
    return s.reshape(n, r, w)


def _dn_prep(t_known, qr, kr, v, a_raw, b_raw, a_log, dt_b):
    n = qr.shape[0]
    ii = lax.broadcasted_iota(jnp.int32, (n, CH, CH), 1)
    jj = lax.broadcasted_iota(jnp.int32, (n, CH, CH), 2)
    incl, strict = ii >= jj, ii > jj
    q = qr * lax.rsqrt(_row_sums(qr * qr) + EPS) * (DK ** -0.5)
    k = kr * lax.rsqrt(_row_sums(kr * kr) + EPS)
    beta = _sigmoid(b_raw)
    g = -jnp.exp(a_log) * _softplus(a_raw + dt_b)
    gcb = _cumsum_rows(jnp.broadcast_to(g, (n, CH, DK)))
    gc = gcb[:, :, 0:1]
    gc_row = jnp.swapaxes(gcb, 1, 2)[:, 0:1, 0:CH]
    decay = jnp.where(incl, jnp.exp(jnp.where(incl, gc - gc_row, 0.0)), 0.0)
    kb = k * beta
    L = jnp.where(strict, _bmm(kb, k, _BMM_NT) * decay, 0.0)
    T = _tri_inverse(L) if t_known is None else _tri_inverse_known(L, t_known)
    eg = jnp.exp(gc)
    u = _bmm(T, v * beta)
    w = _bmm(T, kb * eg)
    a_in = _bmm(q, k, _BMM_NT) * decay
    g_last = gc[:, CH - 1:CH, :]
    return u, w, q * eg, k * jnp.exp(g_last - gc), a_in, jnp.exp(g_last), T


def _dn_step(S0, u, w, qd, kd, a_in, cd):
    r = _bmm(jnp.concatenate([w, qd], axis=1), S0)
    v_new = u - r[:, 0:CH, :]
    o = r[:, CH:2 * CH, :] + _bmm(a_in, v_new)
    S1 = S0 * cd + _bmm(kd, v_new, _BMM_TN)
    return o, S1


def _dn_stack(cq, ba, al, dt, G):
    cols = [[] for _ in range(7)]
    for c in range(G):
        rows = slice(CH * c, CH * (c + 1))
        for h in range(DH):
            parts = (cq[rows, DK * h:DK * (h + 1)], cq[rows, DNW + DK * h:DNW + DK * (h + 1)],
                     cq[rows, 2 * DNW + DK * h:2 * DNW + DK * (h + 1)], ba[rows, DH + h:DH + h + 1],
                     ba[rows, h:h + 1], al[:, h:h + 1], dt[:, h:h + 1])
            for col, p in zip(cols, parts):
                col.append(p)
    return tuple(jnp.stack(col) for col in cols)


def _dn_group(S, want):
    g = want
    while (S // CH) % g:
        g //= 2
    return g


def _dn_prep_fwd(xin, conv_w, ba, a_log, dt_b):
    B, S, _ = xin.shape
    nc = S // CH
    G = _dn_group(S, 8)
    r8 = G * CH // 8

    def body(xp_ref, x_ref, cw_ref, ba_ref, al_ref, dt_ref, cq_ref, u_ref, w_ref, qd_ref, kd_ref, a_ref, t_ref, cd_ref):
        xp = jnp.where(pl.program_id(1) > 0, xp_ref[...], 0.0)
        cq = _silu(_conv_taps(jnp.concatenate([xp, x_ref[...]], axis=0), cw_ref[...], G * CH))
        cq_ref[...] = cq
        ops = _dn_stack(cq, ba_ref[...], al_ref[...], dt_ref[...], G)
        u, w, qd, kd, a_in, cd, T = _dn_prep(None, *ops)
        lane4 = lax.broadcasted_iota(jnp.int32, (1, DH), 1)
        for c in range(G):
            rows = slice(CH * c, CH * (c + 1))
            cdrow = jnp.zeros((1, DH), F32)
            for h in range(DH):
                n = DH * c + h
                lanes = slice(DK * h, DK * (h + 1))
                u_ref[rows, lanes] = u[n]
                w_ref[rows, lanes] = w[n]
                qd_ref[rows, lanes] = qd[n]
                kd_ref[rows, lanes] = kd[n]
                a_ref[rows, CH * h:CH * (h + 1)] = a_in[n]
                t_ref[rows, CH * h:CH * (h + 1)] = T[n]
                cdrow = cdrow + jnp.where(lane4 == h, cd[n], 0.0)
            cd_ref[c] = cdrow

    wide = jax.ShapeDtypeStruct((B, S, DNW), F32)
    sq = jax.ShapeDtypeStruct((B, S, DH * CH), F32)
    return pl.pallas_call(
        body, name="dn_prep_fwd", grid=(B, nc // G),
        out_shape=[jax.ShapeDtypeStruct((B, S, CONVW), F32), wide, wide, wide, wide, sq, sq,
                   jax.ShapeDtypeStruct((B, nc, 1, DH), F32)],
        in_specs=[pl.BlockSpec((None, 8, CONVW), lambda b, i: (b, jnp.maximum(i * r8 - 1, 0), 0)),
                  _rows(G * CH, CONVW), _full((CONV, CONVW)), _rows(G * CH, 2 * DH), _full((1, DH)), _full((1, DH))],
        out_specs=[_rows(G * CH, CONVW)] + [_rows(G * CH, DNW)] * 4 + [_rows(G * CH, DH * CH)] * 2
                  + [pl.BlockSpec((None, G, 1, DH), lambda b, i: (b, i, 0, 0))],
        compiler_params=_cparams(("parallel", "arbitrary")),
    )(xin, xin, conv_w, ba, a_log, dt_b)


def _dn_seq_specs(B, steps, gs, rev):
    at = (lambda i: steps - 1 - i) if rev else (lambda i: i)
    wide = pl.BlockSpec((B, gs * CH, DNW), lambda i: (0, at(i), 0))
    a_spec = pl.BlockSpec((B, gs * CH, DH * CH), lambda i: (0, at(i), 0))
    cd_spec = pl.BlockSpec((B, gs, 1, DH), lambda i: (0, at(i), 0, 0))
    st_spec = pl.BlockSpec((B, gs, DH, DK, DK), lambda i: (0, at(i), 0, 0, 0))
    return wide, a_spec, cd_spec, st_spec


def _dn_step_operands(B, c, u_ref, w_ref, qd_ref, kd_ref, a_ref, cd_ref):
    pairs = [(b, h) for b in range(B) for h in range(DH)]
    rows = slice(CH * c, CH * (c + 1))
    wide = lambda ref: jnp.stack([ref[b, rows, DK * h:DK * (h + 1)] for b, h in pairs])
    a_in = jnp.stack([a_ref[b, rows, CH * h:CH * (h + 1)] for b, h in pairs])
    cd = jnp.stack([cd_ref[b, c, :, h:h + 1] for b, h in pairs])
    return wide(u_ref), wide(w_ref), wide(qd_ref), wide(kd_ref), a_in, cd


def _dn_seq_fwd(u, w, qd, kd, a_in, cd):
    B, S, _ = u.shape
    nc = S // CH
    gs = _dn_group(S, 8)

    def body(u_ref, w_ref, qd_ref, kd_ref, a_ref, cd_ref, o_ref, st_ref, state):
        @pl.when(pl.program_id(0) == 0)
        def _():
            state[...] = jnp.zeros_like(state)

        S0 = state[...]
        for c in range(gs):
            for b in range(B):
                st_ref[b, c] = S0[DH * b:DH * (b + 1)]
            o, S0 = _dn_step(S0, *_dn_step_operands(B, c, u_ref, w_ref, qd_ref, kd_ref, a_ref, cd_ref))
            for b in range(B):
                for h in range(DH):
                    o_ref[b, CH * c:CH * (c + 1), DK * h:DK * (h + 1)] = o[DH * b + h]
        state[...] = S0

    wide, a_spec, cd_spec, st_spec = _dn_seq_specs(B, nc // gs, gs, False)
    return pl.pallas_call(
        body, name="dn_seq_fwd", grid=(nc // gs,),
        out_shape=[jax.ShapeDtypeStruct((B, S, DNW), F32), jax.ShapeDtypeStruct((B, nc, DH, DK, DK), F32)],
        in_specs=[wide, wide, wide, wide, a_spec, cd_spec],
        out_specs=[wide, st_spec],
        scratch_shapes=[pltpu.VMEM((B * DH, DK, DK), F32)],
        compiler_params=_cparams(("arbitrary",)),
    )(u, w, qd, kd, a_in, cd)


def _dn_seq_bwd(u, w, qd, kd, a_in, cd, states, do):
    B, S, _ = u.shape
    nc = S // CH
    gs = _dn_group(S, 4)

    def body(u_ref, w_ref, qd_ref, kd_ref, a_ref, cd_ref, st_ref, do_ref,
             du_ref, dw_ref, dqd_ref, dkd_ref, da_ref, dcd_ref, dstate):
        @pl.when(pl.program_id(0) == 0)
        def _():
            dstate[...] = jnp.zeros_like(dstate)

        lane4 = lax.broadcasted_iota(jnp.int32, (1, DH), 1)
        dS = dstate[...]
        for c in reversed(range(gs)):
            rows = slice(CH * c, CH * (c + 1))
            S0 = jnp.concatenate([st_ref[b, c] for b in range(B)], axis=0)
            do = jnp.stack([do_ref[b, rows, DK * h:DK * (h + 1)] for b in range(B) for h in range(DH)])
            _, vjp = jax.vjp(_dn_step, S0, *_dn_step_operands(B, c, u_ref, w_ref, qd_ref, kd_ref, a_ref, cd_ref))
            dS, du, dw, dqd, dkd, da, dcd = vjp((do, dS))
            for b in range(B):
                dcdrow = jnp.zeros((1, DH), F32)
                for h in range(DH):
                    n = DH * b + h
                    lanes = slice(DK * h, DK * (h + 1))
                    du_ref[b, rows, lanes] = du[n]
                    dw_ref[b, rows, lanes] = dw[n]
                    dqd_ref[b, rows, lanes] = dqd[n]
                    dkd_ref[b, rows, lanes] = dkd[n]
                    da_ref[b, rows, CH * h:CH * (h + 1)] = da[n]
                    dcdrow = dcdrow + jnp.where(lane4 == h, dcd[n], 0.0)
                dcd_ref[b, c] = dcdrow
        dstate[...] = dS

    wide, a_spec, cd_spec, st_spec = _dn_seq_specs(B, nc // gs, gs, True)
    sd = jax.ShapeDtypeStruct((B, S, DNW), F32)
    return pl.pallas_call(
        body, name="dn_seq_bwd", grid=(nc // gs,),
        out_shape=[sd, sd, sd, sd, jax.ShapeDtypeStruct((B, S, DH * CH), F32), jax.ShapeDtypeStruct((B, nc, 1, DH), F32)],
        in_specs=[wide, wide, wide, wide, a_spec, cd_spec, st_spec, wide],
        out_specs=[wide, wide, wide, wide, a_spec, cd_spec],
        scratch_shapes=[pltpu.VMEM((B * DH, DK, DK), F32)],
        compiler_params=_cparams(("arbitrary",)),
    )(u, w, qd, kd, a_in, cd, states, do)


def _dn_prep_bwd(xin, conv_w, cq, ba, a_log, dt_b, t_inv, du, dw, dqd, dkd, da, dcd):
    B, S, _ = cq.shape
    nc = S // CH
    G = _dn_group(S, 8)
    R = G * CH
    nblk = nc // G
    r8 = R // 8

    def body(xp_ref, x_ref, cw_ref, cq_ref, ba_ref, al_ref, dt_ref, t_ref, du_ref, dw_ref, dqd_ref, dkd_ref, da_ref, dcd_ref,
             dx_ref, dcw_ref, dba_ref, dal_ref, ddt_ref, carry):
        i = pl.program_id(1)

        @pl.when((pl.program_id(0) == 0) & (i == 0))
        def _():
            dal_ref[...] = jnp.zeros_like(dal_ref)
            ddt_ref[...] = jnp.zeros_like(ddt_ref)
            dcw_ref[...] = jnp.zeros_like(dcw_ref)

        @pl.when(i == 0)
        def _():
            carry[...] = jnp.zeros_like(carry)

        pairs = [(c, h) for c in range(G) for h in range(DH)]
        rows = lambda c: slice(CH * c, CH * (c + 1))
        wide = lambda ref: jnp.stack([ref[rows(c), DK * h:DK * (h + 1)] for c, h in pairs])
        square = lambda ref: jnp.stack([ref[rows(c), CH * h:CH * (h + 1)] for c, h in pairs])
        ops = _dn_stack(cq_ref[...], ba_ref[...], al_ref[...], dt_ref[...], G)
        cots = (wide(du_ref), wide(dw_ref), wide(dqd_ref), wide(dkd_ref), square(da_ref),
                jnp.stack([dcd_ref[c][:, h:h + 1] for c, h in pairs]), jnp.zeros((len(pairs), CH, CH), F32))
        _, vjp = jax.vjp(functools.partial(_dn_prep, square(t_ref)), *ops)
        dq, dk, dv, dar, dbr, dl, dd = vjp(cots)
        lane8 = lax.broadcasted_iota(jnp.int32, (CH, 2 * DH), 1)
        lane4 = lax.broadcasted_iota(jnp.int32, (1, DH), 1)
        dal = jnp.zeros((1, DH), F32)
        ddt = jnp.zeros((1, DH), F32)
        for c in range(G):
            dba = jnp.zeros((CH, 2 * DH), F32)
            for h in range(DH):
                n = DH * c + h
                dba = dba + jnp.where(lane8 == h, dbr[n], 0.0) + jnp.where(lane8 == DH + h, dar[n], 0.0)
                dal = dal + jnp.where(lane4 == h, dl[n], 0.0)
                ddt = ddt + jnp.where(lane4 == h, dd[n], 0.0)
            dba_ref[rows(c), :] = dba.astype(BF16)
        dal_ref[...] += dal
        ddt_ref[...] += ddt

        dcq = jnp.concatenate([jnp.concatenate([t[DH * c + h] for t in (dq, dk, dv) for h in range(DH)], axis=1)
                               for c in range(G)], axis=0)
        w = cw_ref[...]
        xp = jnp.where(i < nblk - 1, xp_ref[...], 0.0)
        xe = jnp.concatenate([xp, x_ref[...]], axis=0)
        taps = [(pltpu.roll(xe, CONV - 1 - j, 0) if j < CONV - 1 else xe)[8:8 + R, :] for j in range(CONV)]
        pre = sum(t * w[j:j + 1, :] for j, t in enumerate(taps))
        sg = _sigmoid(pre)
        dpre = dcq * (sg * (1.0 + pre * (1.0 - sg)))
        ext = jnp.concatenate([dpre, carry[...]], axis=0)
        dx = dpre * w[CONV - 1:CONV, :]
        for j in range(CONV - 1):
            dx = dx + pltpu.roll(ext, R + 8 - (CONV - 1 - j), 0)[0:R, :] * w[j:j + 1, :]
        dx_ref[...] = dx.astype(BF16)
        carry[...] = dpre[0:8, :]
        lane_row = lax.broadcasted_iota(jnp.int32, (CONV, CONVW), 0)
        dcw = jnp.zeros((CONV, CONVW), F32)
        for j in range(CONV):
            dcw = dcw + jnp.where(lane_row == j, jnp.sum(taps[j] * dpre, axis=0, keepdims=True), 0.0)
        dcw_ref[...] += dcw

    rev = lambda w: pl.BlockSpec((None, R, w), lambda b, i: (b, nblk - 1 - i, 0))
    return pl.pallas_call(
        body, name="dn_prep_bwd", grid=(B, nblk),
        out_shape=[jax.ShapeDtypeStruct((B, S, CONVW), BF16), jax.ShapeDtypeStruct((CONV, CONVW), F32),
                   jax.ShapeDtypeStruct((B, S, 2 * DH), BF16), jax.ShapeDtypeStruct((1, DH), F32),
                   jax.ShapeDtypeStruct((1, DH), F32)],
        in_specs=[pl.BlockSpec((None, 8, CONVW), lambda b, i: (b, jnp.maximum((nblk - 1 - i) * r8 - 1, 0), 0)),
                  rev(CONVW), _full((CONV, CONVW)), rev(CONVW), rev(2 * DH), _full((1, DH)), _full((1, DH)), rev(DH * CH)]
                 + [rev(DNW)] * 4 + [rev(DH * CH), pl.BlockSpec((None, G, 1, DH), lambda b, i: (b, nblk - 1 - i, 0, 0))],
        out_specs=[rev(CONVW), _full((CONV, CONVW)), rev(2 * DH), _full((1, DH)), _full((1, DH))],
        scratch_shapes=[pltpu.VMEM((8, CONVW), F32)],
        compiler_params=_cparams(("arbitrary", "arbitrary")),
    )(xin, xin, conv_w, cq, ba, a_log, dt_b, t_inv, du, dw, dqd, dkd, da, dcd)


def _gated_norm(o, z, g):
    outs = []
    for h in range(DH):
        t = o[:, DK * h:DK * (h + 1)]
        r = lax.rsqrt(jnp.mean(t * t, axis=-1, keepdims=True) + EPS)
        outs.append(t * r * g * _silu(z[:, DK * h:DK * (h + 1)]))
    return jnp.concatenate(outs, axis=1)


def _mix_fwd(x, o_attn, o_dn, z, ga, gd, mod, dn_g, w_branch, w_out):
    B, S, _ = x.shape
    tm = _tile(S, 512)

    def body(x_ref, oa_ref, od_ref, z_ref, ga_ref, gd_ref, mod_ref, g_ref, wb_ref, wo_ref,
             x1_ref, mix_ref, mg_ref, ob_ref):
        oa = oa_ref[...].astype(BF16)
        od = _gated_norm(od_ref[...], z_ref[...], g_ref[...]).astype(BF16)
        ob_ref[0] = oa
        ob_ref[1] = od
        ya = jnp.dot(oa, wb_ref[0:QW, :], preferred_element_type=F32)
        yd = jnp.dot(od, wb_ref[QW:QW + DNW, :], preferred_element_type=F32)
        merged = (_sigmoid(ga_ref[...]) * ya + _sigmoid(gd_ref[...]) * yd).astype(BF16)
        mg_ref[...] = merged
        mix = jnp.dot(merged, wo_ref[...], preferred_element_type=F32)
        mix_ref[...] = mix
        x1_ref[...] = x_ref[...] + mod_ref[2:3, :] * mix

    return pl.pallas_call(
        body, name="mix_fwd", grid=(B, S // tm),
        out_shape=[jax.ShapeDtypeStruct((B, S, D), F32), jax.ShapeDtypeStruct((B, S, D), F32),
                   jax.ShapeDtypeStruct((B, S, D), BF16), jax.ShapeDtypeStruct((B, 2, S, QW), BF16)],
        in_specs=[_rows(tm, D), _rows(tm, QW), _rows(tm, DNW), _rows(tm, DNW), _rows(tm, D), _rows(tm, D),
                  _perb(6, D), _full((1, DK)), _resident(w_branch.shape), _resident(w_out.shape)],
        out_specs=[_rows(tm, D), _rows(tm, D), _rows(tm, D), _stacked(2, tm, QW)],
        compiler_params=_cparams(("parallel", "arbitrary")),
    )(x, o_attn, o_dn, z, ga, gd, mod, dn_g, w_branch, w_out)


def _mix_bwd(dx1, mix, o_attn, o_dn, z, ga, gd, mod, dn_g, w_branch, w_out):
    B, S, _ = dx1.shape
    tm = _tile(S, 512)

    def body(dx1_ref, mix_ref, oa_ref, od_ref, z_ref, ga_ref, gd_ref, mod_ref, g_ref, wb_ref, wo_ref,
             dmix_ref, dyo_ref, dga_ref, dgd_ref, dz_ref, doa_ref, dod_ref, dgate_ref, dg_ref):
        b, i = pl.program_id(0), pl.program_id(1)
        dx1 = dx1_ref[...]
        dmix = (dx1 * mod_ref[2:3, :]).astype(BF16)
        dmix_ref[...] = dmix
        dgate = jnp.sum(dx1 * mix_ref[...], axis=0, keepdims=True)
        dmerged = _dot_nt(dmix, wo_ref[...])
        odn, gn_vjp = jax.vjp(_gated_norm, od_ref[...], z_ref[...], g_ref[...])
        ya = _dot(oa_ref[...], wb_ref[0:QW, :])
        yd = _dot(odn, wb_ref[QW:QW + DNW, :])
        sa, sd = _sigmoid(ga_ref[...]), _sigmoid(gd_ref[...])
        dya = (dmerged * sa).astype(BF16)
        dyd = (dmerged * sd).astype(BF16)
        dyo_ref[0] = dya
        dyo_ref[1] = dyd
        dga_ref[...] = (dmerged * ya * sa * (1.0 - sa)).astype(BF16)
        dgd_ref[...] = (dmerged * yd * sd * (1.0 - sd)).astype(BF16)
        doa_ref[...] = _dot_nt(dya, wb_ref[0:QW, :])
        dodn = _dot_nt(dyd, wb_ref[QW:QW + DNW, :])
        dod, dz, dg = gn_vjp(dodn)
        dod_ref[...] = dod
        dz_ref[...] = dz.astype(BF16)

        @pl.when(i == 0)
        def _():
            dgate_ref[...] = jnp.zeros_like(dgate_ref)

        @pl.when((b == 0) & (i == 0))
        def _():
            dg_ref[...] = jnp.zeros_like(dg_ref)

        dgate_ref[...] += dgate
        dg_ref[...] += dg

    return pl.pallas_call(
        body, name="mix_bwd", grid=(B, S // tm),
        out_shape=[jax.ShapeDtypeStruct((B, S, D), BF16), jax.ShapeDtypeStruct((B, 2, S, D), BF16),
                   jax.ShapeDtypeStruct((B, S, D), BF16), jax.ShapeDtypeStruct((B, S, D), BF16),
                   jax.ShapeDtypeStruct((B, S, DNW), BF16),
                   jax.ShapeDtypeStruct((B, S, QW), F32), jax.ShapeDtypeStruct((B, S, DNW), F32),
                   jax.ShapeDtypeStruct((B, 1, D), F32), jax.ShapeDtypeStruct((1, DK), F32)],
        in_specs=[_rows(tm, D), _rows(tm, D), _rows(tm, QW), _rows(tm, DNW), _rows(tm, DNW), _rows(tm, D),
                  _rows(tm, D), _perb(6, D), _full((1, DK)), _resident(w_branch.shape), _resident(w_out.shape)],
        out_specs=[_rows(tm, D), _stacked(2, tm, D), _rows(tm, D), _rows(tm, D), _rows(tm, DNW),
                   _rows(tm, QW), _rows(tm, DNW), _perb(1, D), _full((1, DK))],
        compiler_params=_cparams(("arbitrary", "arbitrary")),
    )(dx1, mix, o_attn, o_dn, z, ga, gd, mod, dn_g, w_branch, w_out)


GU_SHARD = 2 * FFN // N_DEV
GU_HALF = N_DEV // 2


def _ffn1_fwd(x1, mod, g2, w_gu):
    B, S, _ = x1.shape
    tm = _tile(S)

    def body(x_ref, mod_ref, g_ref, w_ref, h_ref, dgate_ref, dup_ref, act_ref):
        h = _rms_mod(x_ref[...], g_ref[...], mod_ref[4:5, :], mod_ref[3:4, :]).astype(BF16)
        h_ref[...] = h
        for j in range(GU_HALF):
            gate = _dot_nt(h, w_ref[j])
            up = _dot_nt(h, w_ref[GU_HALF + j])
            sg = _sigmoid(gate)
            silu = gate * sg
            dgate_ref[j] = up * (sg * (1.0 + gate * (1.0 - sg)))
            dup_ref[j] = silu
            act_ref[j] = (silu * up).astype(BF16)

    blk = lambda dt: jax.ShapeDtypeStruct((B, GU_HALF, S, GU_SHARD), dt)
    return pl.pallas_call(
        body, name="ffn1_fwd", grid=(B, S // tm),
        out_shape=[jax.ShapeDtypeStruct((B, S, D), BF16), blk(F32), blk(F32), blk(BF16)],
        in_specs=[_rows(tm, D), _perb(6, D), _full((1, D)), _resident(w_gu.shape)],
        out_specs=[_rows(tm, D)] + [_stacked(GU_HALF, tm, GU_SHARD)] * 3,
        compiler_params=_cparams(("parallel", "arbitrary")),
    )(x1, mod, g2, w_gu)


def _ffn2_fwd(act, x1, target, mod, w_down):
    B, S, _ = x1.shape
    tm = _tile(S, 512)

    def body(a_ref, x_ref, t_ref, mod_ref, w_ref, dy_ref, loss_ref, dgate_ref):
        b, i = pl.program_id(0), pl.program_id(1)
        y = jnp.dot(a_ref[0], w_ref[0], preferred_element_type=F32)
        for j in range(1, GU_HALF):
            y = y + jnp.dot(a_ref[j], w_ref[j], preferred_element_type=F32)
        err = x_ref[...] + mod_ref[5:6, :] * y - t_ref[...]
        dy = err * (1.0 / D)
        dy_ref[...] = dy

        @pl.when((b == 0) & (i == 0))
        def _():
            loss_ref[...] = jnp.zeros_like(loss_ref)

        @pl.when(i == 0)
        def _():
            dgate_ref[...] = jnp.zeros_like(dgate_ref)

        loss_ref[...] += (0.5 / D) * jnp.sum(err * err)
        dgate_ref[...] += jnp.sum(dy * y, axis=0, keepdims=True)

    return pl.pallas_call(
        body, name="ffn2_fwd", grid=(B, S // tm),
        out_shape=[jax.ShapeDtypeStruct((B, S, D), F32), jax.ShapeDtypeStruct((1, 128), F32),
                   jax.ShapeDtypeStruct((B, 1, D), F32)],
        in_specs=[_stacked(GU_HALF, tm, GU_SHARD), _rows(tm, D), _rows(tm, D), _perb(6, D), _resident(w_down.shape)],
        out_specs=[_rows(tm, D), _full((1, 128)), _perb(1, D)],
        compiler_params=_cparams(("arbitrary", "arbitrary")),
    )(act, x1, target, mod, w_down)


def _ffn2_bwd(dy, act_dgate, act_dup, mod, w_down):
    B, S, _ = dy.shape
    tm = _tile(S, 512)

    def body(dy_ref, dgate_ref, dup_ref, mod_ref, w_ref, dgu_ref, dyg_ref):
        dyg = (dy_ref[...] * mod_ref[5:6, :]).astype(BF16)
        dyg_ref[...] = dyg
        for j in range(GU_HALF):
            dact = _dot_nt(dyg, w_ref[j])
            dgu_ref[j] = (dact * dgate_ref[j]).astype(BF16)
            dgu_ref[GU_HALF + j] = (dact * dup_ref[j]).astype(BF16)

    return pl.pallas_call(
        body, name="ffn2_bwd", grid=(B, S // tm),
        out_shape=[jax.ShapeDtypeStruct((B, N_DEV, S, GU_SHARD), BF16), jax.ShapeDtypeStruct((B, S, D), BF16)],
        in_specs=[_rows(tm, D), _stacked(GU_HALF, tm, GU_SHARD), _stacked(GU_HALF, tm, GU_SHARD), _perb(6, D),
                  _resident(w_down.shape)],
        out_specs=[_stacked(N_DEV, tm, GU_SHARD), _rows(tm, D)],
        compiler_params=_cparams(("parallel", "arbitrary")),
    )(dy, act_dgate, act_dup, mod, w_down)


def _ffn1_bwd(dgu, x1, dy, mod, g2, w_gu):
    B, S, _ = x1.shape
    tm = _tile(S, 512)

    def body(dgu_ref, x_ref, dy_ref, mod_ref, g_ref, w_ref, dx1_ref, dg_ref, dsc_ref, dsh_ref):
        b, i = pl.program_id(0), pl.program_id(1)
        dh = jnp.dot(dgu_ref[0], w_ref[0], preferred_element_type=F32)
        for j in range(1, N_DEV):
            dh = dh + jnp.dot(dgu_ref[j], w_ref[j], preferred_element_type=F32)
        _, vjp = jax.vjp(_rms_mod, x_ref[...], g_ref[...], mod_ref[4:5, :], mod_ref[3:4, :])
        dx, dg, dsc, dsh = vjp(dh)
        dx1_ref[...] = dy_ref[...] + dx

        @pl.when((b == 0) & (i == 0))
        def _():
            dg_ref[...] = jnp.zeros_like(dg_ref)

        @pl.when(i == 0)
        def _():
            dsc_ref[...] = jnp.zeros_like(dsc_ref)
            dsh_ref[...] = jnp.zeros_like(dsh_ref)

        dg_ref[...] += dg
        dsc_ref[...] += dsc
        dsh_ref[...] += dsh

    return pl.pallas_call(
        body, name="ffn1_bwd", grid=(B, S // tm),
        out_shape=[jax.ShapeDtypeStruct((B, S, D), F32), jax.ShapeDtypeStruct((1, D), F32),
                   jax.ShapeDtypeStruct((B, 1, D), F32), jax.ShapeDtypeStruct((B, 1, D), F32)],
        in_specs=[_stacked(N_DEV, tm, GU_SHARD), _rows(tm, D), _rows(tm, D), _perb(6, D), _full((1, D)),
                  _resident(w_gu.shape)],
        out_specs=[_rows(tm, D), _full((1, D)), _perb(1, D), _perb(1, D)],
        compiler_params=_cparams(("arbitrary", "arbitrary")),
    )(dgu, x1, dy, mod, g2, w_gu)


def _adamw(w, g, m, v, name):
    def body(w_ref, g_ref, m_ref, v_ref, d_ref, nm_ref, nv_ref):
        d_ref[...], nm_ref[...], nv_ref[...] = _adamw_math(w_ref[...], g_ref[...], m_ref[...], v_ref[...])

    sd = jax.ShapeDtypeStruct(w.shape, F32)
    return pl.pallas_call(body, name=name, out_shape=(sd, sd, sd), compiler_params=_cparams())(w, g, m, v)


def kernel(x, c, positions, ada_w, ada_b, norm1_g, w_in, conv_w, q_norm_g, k_norm_g, sinks, a_log, dt_bias, dn_norm_g, w_branch, w_out, norm2_g, w_gate_up, w_down, loss_target, m_ada_w, m_ada_b, m_norm1_g, m_w_in, m_conv_w, m_q_norm_g, m_k_norm_g, m_sinks, m_a_log, m_dt_bias, m_dn_norm_g, m_w_branch, m_w_out, m_norm2_g, m_w_gate_up, m_w_down, v_ada_w, v_ada_b, v_norm1_g, v_w_in, v_conv_w, v_q_norm_g, v_k_norm_g, v_sinks, v_a_log, v_dt_bias, v_dn_norm_g, v_w_branch, v_w_out, v_norm2_g, v_w_gate_up, v_w_down):
    B, S, _ = x.shape
    me = 4 * lax.axis_index("x") + 2 * lax.axis_index("y") + lax.axis_index("c")

    tr = lambda t: jnp.swapaxes(t, 1, 2)
    shards = [w[0].astype(BF16) for w in (tr(w_in), w_branch, w_out, tr(w_gate_up), w_down)]

    c_all = _all_gather_small(c, "gather_c").reshape(N_DEV * B, D)
    ncol = 6 * D // N_DEV
    mod_cols, cond_all = _ada_fwd(c_all, ada_w[0], lax.dynamic_slice(ada_b, (0, me * ncol), (1, ncol)))
    mod_all = _all_gather_small(mod_cols, "gather_mod").transpose(1, 0, 2).reshape(N_DEV * B, 6 * D)
    mod = lax.dynamic_slice(mod_all, (me * B, 0), (B, 6 * D)).reshape(B, 6, D)
    conv2 = conv_w.reshape(CONV, CONVW // N_DEV)
    conv_all = _all_gather_small(conv2, "gather_conv").transpose(1, 0, 2).reshape(CONV, CONVW)

    invf, mean_q, mean_k = _attn_consts()
    w_in_b, rope_cos, rope_sin = _all_gather_big(shards[:1], "gather_w_in", after=(mod, conv_all),
                                                 side=_rope_tables_side(positions.reshape(B, S, 1), invf))
    w_sems, w_srcs, w_lands, w_token = _copies_start(shards[1:], [_place_own(s, me) for s in shards[1:]], False,
                                                    w_in_b, "gather_rest_start")

    w_in_t = w_in_b.reshape(IN_W, D)
    h1, aq, akv, dnx, ba, z, ga, gd = _inproj_fwd(x, mod, norm1_g + w_token[0, 0], w_in_t)
    o_attn = _attn_fwd(aq, akv, rope_cos, rope_sin, q_norm_g, k_norm_g, sinks, mean_q, mean_k)
    cq, dn_u, dn_w, dn_qd, dn_kd, dn_a, dn_t, dn_cd = _dn_prep_fwd(dnx, conv_all, ba, a_log, dt_bias)
    o_dn, states = _dn_seq_fwd(dn_u, dn_w, dn_qd, dn_kd, dn_a, dn_cd)
    w_branch_g, w_out_g, w_gu_b, w_down_g = _copies_wait(w_sems, w_srcs, w_lands, o_dn, "gather_wait_rest")
    w_branch_f = w_branch_g.reshape(D, D)
    w_out_f = w_out_g.reshape(D, D)
    w_down_b = w_down_g.reshape(GU_HALF, GU_SHARD, D)
    x1, mix, merged, ob = _mix_fwd(x, o_attn, o_dn, z, ga, gd, mod, dn_norm_g, w_branch_f, w_out_f)
    h2, act_dgate, act_dup, act = _ffn1_fwd(x1, mod, norm2_g, w_gu_b)
    dy, loss_part, d_gate2 = _ffn2_fwd(act, x1, loss_target, mod, w_down_b)
    loss = lax.psum(loss_part[0, 0], ("x", "y", "c"))

    one = lambda t: t.reshape(B, 1, S, t.shape[-1])
    dgu, dyg = _ffn2_bwd(dy, act_dgate, act_dup, mod, w_down_b)
    g_w_down = _wgrad(act, one(dyg), "wgrad_down")
    dx1, d_n2g, d_scale2, d_shift2 = _ffn1_bwd(dgu, x1, dy, mod, norm2_g, w_gu_b)
    g_w_gu = _wgrad(dgu, one(h2), "wgrad_gate_up")
    ffn = _exchange_start([g_w_gu, g_w_down.reshape(N_DEV, FFN // N_DEV, D)], me, dx1, "exchange_ffn_start")
    dmix, dyo, dga, dgd, dz, d_oa, d_od, d_gate1, d_dng = _mix_bwd(
        dx1, mix, o_attn, o_dn, z, ga, gd, mod, dn_norm_g + ffn[3][0, 0], w_branch_f, w_out_f)
    d_dn = _dn_seq_bwd(dn_u, dn_w, dn_qd, dn_kd, dn_a, dn_cd, states, d_od)
    ddnx, d_conv, dba, d_alog, d_dtb = _dn_prep_bwd(dnx, conv_all, cq, ba, a_log, dt_bias, dn_t, *d_dn)
    daq, dakv, d_qg, d_kg, d_sinks = _attn_bwd(aq, akv, rope_cos, rope_sin, q_norm_g, k_norm_g, sinks, mean_q, mean_k, d_oa)
    dps = [daq, dakv, ddnx, dba, dz, dga, dgd]
    dblk, grad_x, d_n1g, d_scale1, d_shift1 = _inproj_bwd(x, mod, norm1_g, dx1, dps, w_in_t)

    dmod = jnp.concatenate([d_shift1, d_scale1, d_gate1, d_shift2, d_scale2, d_gate2], axis=2).reshape(B, 6 * D)
    small = jnp.concatenate([d_n1g, d_qg, d_kg, d_sinks, d_alog, d_dtb, d_dng, d_n2g, d_conv.reshape(1, CONV * CONVW)], axis=1)
    nsm = small.shape[1]
    width = -(-max(6 * D, nsm) // 128) * 128
    rows = jnp.concatenate([jnp.pad(dmod, ((0, 0), (0, width - 6 * D))), jnp.pad(small, ((0, 8 - B - 1), (0, width - nsm)))], axis=0)
    rows_all = _all_gather_small(rows, "gather_small")
    dmod_all = rows_all[:, 0:B, 0:6 * D].reshape(N_DEV * B, 6 * D)
    dmod_cols = lax.dynamic_slice(dmod_all, (0, me * ncol), (N_DEV * B, ncol))
    grad_ada_w, grad_ada_b, small_sum = _ada_bwd(cond_all, dmod_all, dmod_cols, rows_all[:, B, :])
    sizes = [D, HD, HD, HQ, DH, DH, DK, D]
    so = np.cumsum([0] + sizes)
    g_n1, g_qg, g_kg, g_sk, g_al, g_dt, g_dn, g_n2 = [small_sum[:, so[i]:so[i + 1]] for i in range(8)]
    g_conv_all = small_sum[:, so[8]:so[8] + CONV * CONVW].reshape(CONV, N_DEV, CONVW // N_DEV)
    grad_conv = lax.dynamic_slice(g_conv_all, (0, me, 0), (CONV, 1, CONVW // N_DEV)).reshape(CONV, CONVW // N_DEV)

    half = D // 2
    g_w_in_a = _wgrad(dblk, one(h1), "wgrad_in_a", after=small_sum, b_lanes=(0, half))
    proj_a = _exchange_start([g_w_in_a], me, small_sum, "exchange_in_a_start")
    g_w_in_b = _wgrad(dblk, one(h1), "wgrad_in_b", after=proj_a[3], b_lanes=(1, half))
    proj = _exchange_start([g_w_in_b], me, proj_a[3], "exchange_in_b_start")
    g_w_out = _wgrad(one(merged), one(dmix), "wgrad_out", after=proj[3])
    g_w_branch = _wgrad(ob, dyo, "wgrad_branch", after=proj[3])
    mixer = _exchange_start([g_w_branch.reshape(N_DEV, D // N_DEV, D), g_w_out.reshape(N_DEV, D // N_DEV, D)], me,
                            proj[3], "exchange_mix_start")

    upd, grads = {}, {}

    def finish(names, parts, weights):
        for nm, p, (w, m, v) in zip(names, parts, weights):
            grads[nm], *upd[nm] = _sum_adamw(p, w, m, v, "update_" + nm)
        return grads[names[-1]]

    finish(["w_gate_up", "w_down"], _copies_wait(*ffn[:3], mixer[3], "exchange_ffn_wait"),
           [(tr(w_gate_up), tr(m_w_gate_up), tr(v_w_gate_up)), (w_down, m_w_down, v_w_down)])
    (in_a,) = _copies_wait(*proj_a[:3], grads["w_gate_up"], "exchange_in_a_wait")
    (in_b,) = _copies_wait(*proj[:3], in_a, "exchange_in_b_wait")
    finish(["w_in"], [[in_a, in_b]], [(tr(w_in), tr(m_w_in), tr(v_w_in))])
    finish(["w_branch", "w_out"], _copies_wait(*mixer[:3], grads["w_in"], "exchange_mix_wait"),
           [(w_branch, m_w_branch, v_w_branch), (w_out, m_w_out, v_w_out)])
    for nm in ("w_in", "w_gate_up"):
        grads[nm], upd[nm] = tr(grads[nm]), [tr(t) for t in upd[nm]]

    grads["ada_w"] = grad_ada_w.reshape(ada_w.shape)
    upd["ada_w"] = _adamw(ada_w, grads["ada_w"], m_ada_w, v_ada_w, "adamw_ada_w")
    small_names = ["ada_b", "norm1_g", "q_norm_g", "k_norm_g", "sinks", "a_log", "dt_bias", "dn_norm_g", "norm2_g", "conv_w"]
    small_w = [ada_b, norm1_g, q_norm_g, k_norm_g, sinks, a_log, dt_bias, dn_norm_g, norm2_g, conv_w]
    small_g = [grad_ada_b, g_n1, g_qg, g_kg, g_sk, g_al, g_dt, g_dn, g_n2, grad_conv]
    small_m = [m_ada_b, m_norm1_g, m_q_norm_g, m_k_norm_g, m_sinks, m_a_log, m_dt_bias, m_dn_norm_g, m_norm2_g, m_conv_w]
    small_v = [v_ada_b, v_norm1_g, v_q_norm_g, v_k_norm_g, v_sinks, v_a_log, v_dt_bias, v_dn_norm_g, v_norm2_g, v_conv_w]
    cat = lambda arrs: jnp.concatenate([a.reshape(1, -1) for a in arrs], axis=1)
    res = _adamw(cat(small_w), cat(small_g), cat(small_m), cat(small_v), "adamw_small")
    po = np.cumsum([0] + [int(np.prod(w.shape)) for w in small_w])
    for i, nm in enumerate(small_names):
        upd[nm] = tuple(r[:, po[i]:po[i + 1]].reshape(small_w[i].shape) for r in res)
        grads[nm] = small_g[i].reshape(small_w[i].shape)

    order = ["ada_w", "ada_b", "norm1_g", "w_in", "conv_w", "q_norm_g", "k_norm_g", "sinks", "a_log", "dt_bias",
             "dn_norm_g", "w_branch", "w_out", "norm2_g", "w_gate_up", "w_down"]
    return (loss, grad_x, *[grads[n] for n in order], *[upd[n][0] for n in order],
            *[upd[n][1] for n in order], *[upd[n][2] for n in order])
```

```python
import functools

import numpy as np
import jax
import jax.numpy as jnp
from jax import lax
from jax.experimental import pallas as pl
from jax.experimental.pallas import tpu as pltpu

F32 = jnp.float32
BF16 = jnp.bfloat16
HI = lax.Precision.HIGHEST

N_DEV = 8
D = 1024
HQ, HKV, HD = 8, 2, 64
GRP = HQ // HKV
BLK = 128
ROT = HD // 4
THETA = 500000.0
QW, KVW = HQ * HD, HKV * HD
DH, DK = 4, 128
CH = 64
DNW = DH * DK
CONV = 4
CONVW = 3 * DNW
FFN = 2816
EPS = 1e-6
IN_W = QW + 2 * KVW + CONVW + 2 * DH + DNW + 2 * D

LR, B1, B2, AEPS, WD, STEP = 0.001, 0.9, 0.999, 1e-08, 0.01, 10

VMEM_LIMIT = 56 * 1024 * 1024
MESH = pl.DeviceIdType.MESH


def _cparams(sem=None, vmem=VMEM_LIMIT):
    return pltpu.CompilerParams(dimension_semantics=sem, vmem_limit_bytes=vmem)


def _full(shape):
    n = len(shape)
    return pl.BlockSpec(shape, lambda *_: (0,) * n)


def _resident(shape):
    n = len(shape)
    return pl.BlockSpec(shape, lambda *_: (0,) * n, pipeline_mode=pl.Buffered(1))


def _rows(tm, w):
    return pl.BlockSpec((None, tm, w), lambda b, i: (b, i, 0))


def _stacked(n, tm, w):
    return pl.BlockSpec((None, n, tm, w), lambda b, i: (b, 0, i, 0))


def _perb(r, w):
    return pl.BlockSpec((None, r, w), lambda b, i: (b, 0, 0))


def _dot(a, b):
    return jnp.dot(a.astype(BF16), b.astype(BF16), preferred_element_type=F32)


def _dot_nt(a, b):
    return lax.dot_general(a.astype(BF16), b.astype(BF16), (((1,), (1,)), ((), ())), preferred_element_type=F32)


def _dot_tn(a, b):
    return lax.dot_general(a.astype(BF16), b.astype(BF16), (((0,), (0,)), ((), ())), preferred_element_type=F32)


def _dot_hi(a, b):
    return jnp.dot(a, b, preferred_element_type=F32, precision=HI)


def _sigmoid(x):
    return jax.nn.sigmoid(x)


def _silu(x):
    return x * jax.nn.sigmoid(x)


def _rms_mod(x, g, scale, shift):
    r = lax.rsqrt(jnp.mean(x * x, axis=-1, keepdims=True) + EPS)
    return (x * r * g) * (1.0 + scale) + shift


def _tile(S, rows=256):
    return min(rows, S)


def _peer(x, y, c, k):
    px = 1 - x if (k >> 2) & 1 else x
    py = 1 - y if (k >> 1) & 1 else y
    pc = 1 - c if k & 1 else c
    return px, py, pc


def _all_gather_small(v, name):
    r, n = v.shape

    def body(v_ref, out_ref, send_sems, recv_sems, local_sem):
        x, y, c = lax.axis_index("x"), lax.axis_index("y"), lax.axis_index("c")
        me = 4 * x + 2 * y + c
        mine = pltpu.make_async_copy(v_ref, out_ref.at[me], local_sem)
        mine.start()
        sends = []
        for k in range(1, N_DEV):
            cp = pltpu.make_async_remote_copy(
                src_ref=v_ref, dst_ref=out_ref.at[me], send_sem=send_sems.at[k - 1], recv_sem=recv_sems.at[k - 1],
                device_id=_peer(x, y, c, k), device_id_type=MESH)
            cp.start()
            sends.append(cp)
        for k in range(1, N_DEV):
            px, py, pc = _peer(x, y, c, k)
            pltpu.make_async_remote_copy(
                src_ref=v_ref, dst_ref=out_ref.at[4 * px + 2 * py + pc], send_sem=send_sems.at[k - 1],
                recv_sem=recv_sems.at[k - 1], device_id=(px, py, pc), device_id_type=MESH).wait_recv()
        for cp in sends:
            cp.wait_send()
        mine.wait()

    return pl.pallas_call(
        body, name=name,
        out_shape=jax.ShapeDtypeStruct((N_DEV, r, n), v.dtype),
        in_specs=[pl.BlockSpec(memory_space=pltpu.VMEM)],
        out_specs=pl.BlockSpec(memory_space=pltpu.VMEM),
        scratch_shapes=[pltpu.SemaphoreType.DMA((N_DEV - 1,)), pltpu.SemaphoreType.DMA((N_DEV - 1,)), pltpu.SemaphoreType.DMA],
    )(v)


def _all_gather_big(vs, name, after=(), side=None):
    na, nf = len(vs), len(after)
    side_fn, side_in, side_out = side if side is not None else (None, (), ())
    ns, no = len(side_in), len(side_out)

    def body(*refs):
        v_refs, out_refs = refs[:na], refs[na + nf + ns:2 * na + nf + ns]
        send_sems, recv_sems, local_sems = refs[2 * na + nf + ns + no:]
        x, y, c = lax.axis_index("x"), lax.axis_index("y"), lax.axis_index("c")
        me, sibling = (x, y, c), (x, y, 1 - c)
        chips = [(1 - x, y), (x, 1 - y), (1 - x, 1 - y)]

        def rows(a, px, py, pc):
            return out_refs[a].at[4 * px + 2 * py + pc]

        def copy(a, k, block, to, src=None):
            return pltpu.make_async_remote_copy(
                src_ref=rows(a, *block) if src is None else src, dst_ref=rows(a, *block),
                send_sem=send_sems.at[7 * a + k], recv_sem=recv_sems.at[7 * a + k], device_id=to, device_id_type=MESH)

        mine = [pltpu.make_async_copy(v_refs[a], rows(a, *me), local_sems.at[a]) for a in range(na)]
        for cp in mine:
            cp.start()
        first = []
        for a in range(na):
            first.append(copy(a, 0, me, sibling, src=v_refs[a]))
            first += [copy(a, 1 + j, me, (*chip, c), src=v_refs[a]) for j, chip in enumerate(chips)]
        for cp in first:
            cp.start()
        if side_fn is not None:
            side_fn(refs[na + nf:na + nf + ns], refs[2 * na + nf + ns:2 * na + nf + ns + no])
        passed = []
        for j, chip in enumerate(chips):
            for a in range(na):
                copy(a, 1 + j, (*chip, c), me).wait_recv()
                forward = copy(a, 4 + j, (*chip, c), sibling)
                forward.start()
                passed.append(forward)
        for a in range(na):
            copy(a, 0, sibling, me).wait_recv()
            for j, chip in enumerate(chips):
                copy(a, 4 + j, (*chip, 1 - c), me).wait_recv()
        for cp in first + passed:
            cp.wait_send()
        for cp in mine:
            cp.wait()

    return pl.pallas_call(
        body, name=name,
        out_shape=[jax.ShapeDtypeStruct((N_DEV,) + v.shape, v.dtype) for v in vs] + list(side_out),
        in_specs=[pl.BlockSpec(memory_space=pl.ANY)] * (na + nf) + [pl.BlockSpec(memory_space=pltpu.VMEM)] * ns,
        out_specs=[pl.BlockSpec(memory_space=pl.ANY)] * na + [pl.BlockSpec(memory_space=pltpu.VMEM)] * no,
        scratch_shapes=[pltpu.SemaphoreType.DMA((7 * na,)), pltpu.SemaphoreType.DMA((7 * na,)),
                        pltpu.SemaphoreType.DMA((na,))],
        compiler_params=pltpu.CompilerParams(vmem_limit_bytes=VMEM_LIMIT),
    )(*vs, *after, *side_in)


_HBM = pl.BlockSpec(memory_space=pltpu.HBM)
_SEM = pl.BlockSpec(memory_space=pltpu.SEMAPHORE)
_EFFECT = pltpu.SideEffectType.DATAFLOW_SIDE_EFFECTING


def _place_own(block, me):
    land = lax.empty((N_DEV,) + block.shape, block.dtype)
    return lax.dynamic_update_slice(land, block[None], (me,) + (0,) * block.ndim)


def _copies_start(srcs, lands, scatter, after, name):
    na = len(srcs)
    afters = tuple(after) if isinstance(after, (tuple, list)) else (after,)

    def body(*refs):
        src_refs, land_refs = refs[:na], refs[na:2 * na]
        sems = refs[2 * na + len(afters):4 * na + len(afters)]
        token = refs[-1]
        x, y, c = lax.axis_index("x"), lax.axis_index("y"), lax.axis_index("c")
        me = 4 * x + 2 * y + c
        for a in range(na):
            for k in range(1, N_DEV):
                px, py, pc = _peer(x, y, c, k)
                src = src_refs[a].at[4 * px + 2 * py + pc] if scatter else src_refs[a]
                pltpu.make_async_remote_copy(
                    src_ref=src, dst_ref=land_refs[a].at[me], send_sem=sems[2 * a], recv_sem=sems[2 * a + 1],
                    device_id=(px, py, pc), device_id_type=MESH).start()
        token[...] = jnp.zeros_like(token)

    hbm = lambda t: pltpu.HBM(t.shape, t.dtype)
    out = pl.pallas_call(
        body, name=name,
        out_shape=tuple([pltpu.SemaphoreType.DMA(())] * (2 * na) + [hbm(t) for t in srcs] + [hbm(t) for t in lands]
                        + [jax.ShapeDtypeStruct((8, 128), F32)]),
        in_specs=[_HBM] * (2 * na) + [pl.BlockSpec(memory_space=pl.ANY)] * len(afters),
        out_specs=tuple([_SEM] * (2 * na) + [_HBM] * (2 * na) + [pl.BlockSpec(memory_space=pltpu.VMEM)]),
        input_output_aliases={i: 2 * na + i for i in range(2 * na)},
        compiler_params=pltpu.CompilerParams(has_side_effects=_EFFECT),
    )(*[pltpu.with_memory_space_constraint(t, pltpu.HBM) for t in list(srcs) + list(lands)], *afters)
    return out[:2 * na], out[2 * na:3 * na], out[3 * na:4 * na], out[-1]


def _exchange_start(gs, me, after, name):
    own = [lax.dynamic_index_in_dim(g, me, 0, keepdims=False) for g in gs]
    return _copies_start(gs, [_place_own(o, me) for o in own], True, after, name)


def _copies_wait(sems, srcs, lands, after, name):
    na = len(srcs)

    def body(*refs):
        land_refs = refs[na:2 * na]
        sem_refs = refs[2 * na:4 * na]
        x, y, c = lax.axis_index("x"), lax.axis_index("y"), lax.axis_index("c")
        for a in range(na):
            seven = land_refs[a].at[pl.ds(0, N_DEV - 1)]
            copy = pltpu.make_async_remote_copy(
                src_ref=seven, dst_ref=seven, send_sem=sem_refs[2 * a], recv_sem=sem_refs[2 * a + 1],
                device_id=(x, y, c), device_id_type=MESH)
            copy.wait_send()
            copy.wait_recv()

    hbm = lambda t: pltpu.HBM(t.shape, t.dtype)
    out = pl.pallas_call(
        body, name=name,
        out_shape=tuple([hbm(t) for t in srcs] + [hbm(t) for t in lands]),
        in_specs=[_HBM] * (2 * na) + [_SEM] * (2 * na) + [pl.BlockSpec(memory_space=pl.ANY)],
        out_specs=tuple([_HBM] * (2 * na)),
        input_output_aliases={i: i for i in range(2 * na)},
        compiler_params=pltpu.CompilerParams(has_side_effects=_EFFECT),
    )(*srcs, *lands, *sems, after)
    return out[na:]


def _adamw_math(w, g, m, v):
    m = B1 * m + (1.0 - B1) * g
    v = B2 * v + (1.0 - B2) * (g * g)
    m_hat = m / (1.0 - B1 ** STEP)
    v_hat = v / (1.0 - B2 ** STEP)
    return -LR * (m_hat / (jnp.sqrt(v_hat) + AEPS) + WD * w), m, v


def _sum_adamw(parts, w, m, v, name):
    parts = list(parts) if isinstance(parts, (list, tuple)) else [parts]
    r, n = w.shape[1], w.shape[2]
    tr = 256 if r % 256 == 0 else r
    npart = len(parts)

    def body(*refs):
        p_refs = refs[:npart]
        w_ref, m_ref, v_ref, g_ref, d_ref, nm_ref, nv_ref = refs[npart:]
        pieces = []
        for p_ref in p_refs:
            g = p_ref[0].astype(F32)
            for dev in range(1, N_DEV):
                g = g + p_ref[dev].astype(F32)
            pieces.append(g)
        g = pieces[0] if npart == 1 else jnp.concatenate(pieces, axis=1)
        g_ref[...] = g
        d_ref[...], nm_ref[...], nv_ref[...] = _adamw_math(w_ref[...], g, m_ref[...], v_ref[...])

    rows = pl.BlockSpec((None, tr, n), lambda i: (0, i, 0))
    sd = jax.ShapeDtypeStruct((1, r, n), F32)
    return pl.pallas_call(
        body, name=name, grid=(r // tr,), out_shape=(sd, sd, sd, sd),
        in_specs=[pl.BlockSpec((N_DEV, tr, p.shape[2]), lambda i: (0, i, 0)) for p in parts] + [rows, rows, rows],
        out_specs=(rows, rows, rows, rows),
        compiler_params=_cparams(("parallel",)),
    )(*parts, w, m, v)


def _ada_fwd(c_all, ada_w, ada_b_cols):
    nb, ncol = c_all.shape[0], ada_w.shape[1]

    def body(c_ref, w_ref, b_ref, mod_ref, cond_ref):
        cond = _silu(c_ref[...])
        cond_ref[...] = cond
        mod_ref[...] = _dot_hi(cond, w_ref[...]) + b_ref[...]

    return pl.pallas_call(
        body, name="ada_fwd",
        out_shape=(jax.ShapeDtypeStruct((nb, ncol), F32), jax.ShapeDtypeStruct((nb, D), F32)),
        compiler_params=_cparams(),
    )(c_all, ada_w, ada_b_cols)


def _ada_bwd(cond_all, dmod_all, dmod_cols, smalls):
    ncol, nsm = dmod_cols.shape[1], smalls.shape[1]

    def body(cond_ref, dm_ref, dmc_ref, sm_ref, gw_ref, gb_ref, gs_ref):
        gw_ref[...] = lax.dot_general(cond_ref[...], dmc_ref[...], (((0,), (0,)), ((), ())),
                                      preferred_element_type=F32, precision=HI)
        gb_ref[...] = jnp.sum(dm_ref[...], axis=0, keepdims=True)
        gs_ref[...] = jnp.sum(sm_ref[...], axis=0, keepdims=True)

    return pl.pallas_call(
        body, name="ada_bwd",
        out_shape=(jax.ShapeDtypeStruct((D, ncol), F32), jax.ShapeDtypeStruct((1, 6 * D), F32),
                   jax.ShapeDtypeStruct((1, nsm), F32)),
        compiler_params=_cparams(),
    )(cond_all, dmod_all, dmod_cols, smalls)


IN_CUTS = (0, QW, QW + 2 * KVW, QW + 2 * KVW + CONVW, QW + 2 * KVW + CONVW + 2 * DH,
           QW + 2 * KVW + CONVW + 2 * DH + DNW, QW + 2 * KVW + CONVW + 2 * DH + DNW + D, IN_W)
IN_WIDTHS = tuple(b - a for a, b in zip(IN_CUTS[:-1], IN_CUTS[1:]))
IN_SHARD = IN_W // N_DEV


def _inproj_fwd(x, mod, g1, w_t):
    B, S, _ = x.shape
    tm = _tile(S, 512)

    def body(x_ref, mod_ref, g_ref, w_ref, h_ref, *o_refs):
        h = _rms_mod(x_ref[...], g_ref[...], mod_ref[1:2, :], mod_ref[0:1, :]).astype(BF16)
        h_ref[...] = h
        full = _dot_nt(h, w_ref[...])
        for o_ref, lo, hi in zip(o_refs, IN_CUTS[:-1], IN_CUTS[1:]):
            o_ref[...] = full[:, lo:hi]

    return pl.pallas_call(
        body, name="inproj_fwd", grid=(B, S // tm),
        out_shape=[jax.ShapeDtypeStruct((B, S, D), BF16)] + [jax.ShapeDtypeStruct((B, S, w), F32) for w in IN_WIDTHS],
        in_specs=[_rows(tm, D), _perb(6, D), _full((1, D)), _resident(w_t.shape)],
        out_specs=[_rows(tm, D)] + [_rows(tm, w) for w in IN_WIDTHS],
        compiler_params=_cparams(("parallel", "arbitrary")),
    )(x, mod, g1, w_t)


def _inproj_bwd(x, mod, g1, dx1, dps, w_t):
    B, S, _ = x.shape
    tm = _tile(S)
    n = len(dps)

    def body(x_ref, mod_ref, g_ref, dx1_ref, *refs):
        dp_refs, w_ref = refs[:n], refs[n]
        dblk_ref, gx_ref, dg_ref, dsc_ref, dsh_ref = refs[n + 1:]
        b, i = pl.program_id(0), pl.program_id(1)
        full = jnp.concatenate([r[...].astype(F32) for r in dp_refs], axis=1)
        for j in range(N_DEV):
            dblk_ref[j] = full[:, IN_SHARD * j:IN_SHARD * (j + 1)].astype(BF16)
        dh = jnp.dot(full.astype(BF16), w_ref[...], preferred_element_type=F32)
        _, vjp = jax.vjp(_rms_mod, x_ref[...], g_ref[...], mod_ref[1:2, :], mod_ref[0:1, :])
        dx, dg, dsc, dsh = vjp(dh)
        gx_ref[...] = dx1_ref[...] + dx

        @pl.when((b == 0) & (i == 0))
        def _():
            dg_ref[...] = jnp.zeros_like(dg_ref)

        @pl.when(i == 0)
        def _():
            dsc_ref[...] = jnp.zeros_like(dsc_ref)
            dsh_ref[...] = jnp.zeros_like(dsh_ref)

        dg_ref[...] += dg
        dsc_ref[...] += dsc
        dsh_ref[...] += dsh

    return pl.pallas_call(
        body, name="inproj_bwd", grid=(B, S // tm),
        out_shape=[jax.ShapeDtypeStruct((B, N_DEV, S, IN_SHARD), BF16), jax.ShapeDtypeStruct((B, S, D), F32),
                   jax.ShapeDtypeStruct((1, D), F32), jax.ShapeDtypeStruct((B, 1, D), F32),
                   jax.ShapeDtypeStruct((B, 1, D), F32)],
        in_specs=[_rows(tm, D), _perb(6, D), _full((1, D)), _rows(tm, D)]
                 + [_rows(tm, w) for w in IN_WIDTHS] + [_resident(w_t.shape)],
        out_specs=[pl.BlockSpec((None, N_DEV, tm, IN_SHARD), lambda b, i: (b, 0, i, 0)), _rows(tm, D),
                   _full((1, D)), _perb(1, D), _perb(1, D)],
        compiler_params=_cparams(("arbitrary", "arbitrary")),
    )(x, mod, g1, dx1, *dps, w_t)


def _wgrad(a, b, name, after=None, b_lanes=None):
    B, na, S, K = a.shape
    nb, N = b.shape[1], b.shape[3]
    lane_blk = 0
    if b_lanes is not None:
        lane_blk, N = b_lanes
    G = max(na, nb)
    tm = min(4096, S)
    nt = S // tm
    last = B * nt - 1

    def body(a_ref, b_ref, *rest):
        o_ref, acc = rest[-2:]
        t = pl.program_id(1)

        @pl.when(t == 0)
        def _():
            acc[...] = jnp.zeros_like(acc)

        acc[...] += lax.dot_general(a_ref[...], b_ref[...], (((0,), (0,)), ((), ())), preferred_element_type=F32)

        @pl.when(t == last)
        def _():
            o_ref[...] = acc[...].astype(BF16)

    return pl.pallas_call(
        body, name=name, grid=(G, B * nt),
        out_shape=jax.ShapeDtypeStruct((G, K, N), BF16),
        in_specs=[pl.BlockSpec((None, None, tm, K), lambda g, t: (t // nt, g if na > 1 else 0, t % nt, 0)),
                  pl.BlockSpec((None, None, tm, N), lambda g, t: (t // nt, g if nb > 1 else 0, t % nt, lane_blk))]
                 + ([] if after is None else [pl.BlockSpec(memory_space=pl.ANY)]),
        out_specs=pl.BlockSpec((None, K, N), lambda g, t: (g, 0, 0)),
        scratch_shapes=[pltpu.VMEM((K, N), F32)],
        compiler_params=_cparams(("parallel", "arbitrary")),
    )(*((a, b) if after is None else (a, b, after)))


LANES = 128


def _attn_consts():
    inv_freq = THETA ** (-jnp.arange(0, ROT, 2, dtype=F32) / ROT)
    head = jnp.concatenate([inv_freq, inv_freq, jnp.zeros((HD - ROT,), F32)])
    invf = jnp.tile(head, LANES // HD)[None, :]
    mean_of = lambda w: jnp.asarray(np.kron(np.eye(w // HD), np.full((HD, HD), 1.0 / HD)), BF16)
    return invf, mean_of(QW), mean_of(KVW)


def _rope_tables_side(pos, invf):
    B, S, _ = pos.shape
    tr = min(512, S)

    def fn(ins, outs):
        p_ref, f_ref = ins
        c_ref, s_ref = outs
        for b in range(B):
            for r in range(0, S, tr):
                ang = p_ref[b, r:r + tr, :].astype(F32) * f_ref[...]
                c_ref[b, r:r + tr, :] = jnp.cos(ang)
                s_ref[b, r:r + tr, :] = jnp.sin(ang)

    sd = jax.ShapeDtypeStruct((B, S, LANES), F32)
    return fn, (pos, invf), (sd, sd)


def _rope_expand(cos, sin, reps):
    lane = lax.broadcasted_iota(jnp.int32, cos.shape, 1) % HD
    sa = jnp.where((lane >= ROT // 2) & (lane < ROT), sin, 0.0)
    sb = jnp.where(lane < ROT // 2, -sin, 0.0)
    rep = lambda t: jnp.concatenate([t] * reps, axis=1) if reps > 1 else t
    return rep(cos), rep(sa), rep(sb)


@jax.custom_vjp
def _rope(t, cos, sa, sb):
    w = t.shape[1]
    return t * cos + pltpu.roll(t, ROT // 2, 1) * sa + pltpu.roll(t, w - ROT // 2, 1) * sb


def _rope_fwd(t, cos, sa, sb):
    return _rope(t, cos, sa, sb), (cos, sa, sb)


def _rope_bwd(res, d):
    cos, sa, sb = res
    w = d.shape[1]
    dt = d * cos + pltpu.roll(d * sa, w - ROT // 2, 1) + pltpu.roll(d * sb, ROT // 2, 1)
    return dt, jnp.zeros_like(cos), jnp.zeros_like(sa), jnp.zeros_like(sb)


_rope.defvjp(_rope_fwd, _rope_bwd)


def _head_norm(t, g, mean_of):
    hi, lo = _split(t * t)
    ms = jnp.dot(hi, mean_of, preferred_element_type=F32) + jnp.dot(lo, mean_of, preferred_element_type=F32)
    return t * lax.rsqrt(ms + EPS) * g


def _attn_block(q, kvp, kvc, qg, kg, sinks, tq, tk, mq, mk, valid):
    qn = _rope(_head_norm(q, jnp.concatenate([qg] * HQ, axis=1), mq), *tq) * (HD ** -0.5)
    kv = jnp.concatenate([kvp, kvc], axis=0)
    kn = _rope(_head_norm(kv[:, 0:KVW], jnp.concatenate([kg] * HKV, axis=1), mk), *tk)
    per_tile = LANES // HD
    vT = jnp.transpose(kv[:, KVW:2 * KVW])
    qT = [jnp.transpose(qn[:, LANES * t:LANES * (t + 1)]) for t in range(QW // LANES)]
    head_T = lambda h: qT[h // per_tile][HD * (h % per_tile):HD * (h % per_tile + 1), :]
    none = jnp.zeros((HD, GRP * BLK), F32)
    o_T = []
    for j in range(HKV):
        q4T = jnp.concatenate([head_T(GRP * j + i) for i in range(GRP)], axis=1)
        sT = _dot(kn, jnp.concatenate([q4T, none] if j == 0 else [none, q4T], axis=0))
        sT = jnp.where(valid, sT, -1e30)
        sink = jnp.concatenate([jnp.broadcast_to(sinks[:, GRP * j + i:GRP * j + i + 1], (1, BLK)) for i in range(GRP)], axis=1)
        m = lax.stop_gradient(jnp.maximum(jnp.max(sT, axis=0, keepdims=True), sink))
        pT = jnp.exp(sT - m)
        den = jnp.sum(pT, axis=0, keepdims=True) + jnp.exp(sink - m)
        oT = _dot(vT[HD * j:HD * (j + 1), :], pT) * (1.0 / den)
        o_T += [oT[:, BLK * i:BLK * (i + 1)] for i in range(GRP)]
    return jnp.concatenate([jnp.transpose(jnp.concatenate(o_T[per_tile * t:per_tile * (t + 1)], axis=0))
                            for t in range(QW // LANES)], axis=1)


def _attn_tables(cp_ref, cc_ref, sp_ref, sc_ref, n):
    tq = _rope_expand(cc_ref[...], sc_ref[...], QW // LANES)
    tk = _rope_expand(jnp.concatenate([cp_ref[...], cc_ref[...]], axis=0),
                      jnp.concatenate([sp_ref[...], sc_ref[...]], axis=0), KVW // LANES)
    qi = lax.broadcasted_iota(jnp.int32, (2 * BLK, GRP * BLK), 1) % BLK + BLK
    kj = lax.broadcasted_iota(jnp.int32, (2 * BLK, GRP * BLK), 0)
    dist = qi - kj
    valid = (dist >= 0) & (dist < BLK) & ((kj >= BLK) | (n > 0))
    return tq, tk, valid


def _attn_fwd(aq, akv, cos, sin, qg, kg, sinks, mq, mk):
    B, S, _ = aq.shape
    nb = S // BLK

    def body(q_ref, kvp_ref, kvc_ref, cp_ref, cc_ref, sp_ref, sc_ref, qg_ref, kg_ref, sk_ref, mq_ref, mk_ref, o_ref):
        tq, tk, valid = _attn_tables(cp_ref, cc_ref, sp_ref, sc_ref, pl.program_id(1))
        o_ref[...] = _attn_block(q_ref[...], kvp_ref[...], kvc_ref[...], qg_ref[...], kg_ref[...], sk_ref[...],
                                 tq, tk, mq_ref[...], mk_ref[...], valid)

    prev = lambda b, n: (b, jnp.maximum(n - 1, 0), 0)
    cur = lambda b, n: (b, n, 0)
    return pl.pallas_call(
        body, name="attn_fwd", grid=(B, nb),
        out_shape=jax.ShapeDtypeStruct((B, S, QW), F32),
        in_specs=[pl.BlockSpec((None, BLK, QW), cur), pl.BlockSpec((None, BLK, 2 * KVW), prev),
                  pl.BlockSpec((None, BLK, 2 * KVW), cur), pl.BlockSpec((None, BLK, LANES), prev),
                  pl.BlockSpec((None, BLK, LANES), cur), pl.BlockSpec((None, BLK, LANES), prev),
                  pl.BlockSpec((None, BLK, LANES), cur), _full((1, HD)), _full((1, HD)), _full((1, HQ)),
                  _full((QW, QW)), _full((KVW, KVW))],
        out_specs=pl.BlockSpec((None, BLK, QW), cur),
        compiler_params=_cparams(("parallel", "arbitrary")),
    )(aq, akv, akv, cos, cos, sin, sin, qg, kg, sinks, mq, mk)


def _attn_bwd(aq, akv, cos, sin, qg, kg, sinks, mq, mk, do):
    B, S, _ = aq.shape
    nb = S // BLK

    def body(q_ref, kvp_ref, kvc_ref, cp_ref, cc_ref, sp_ref, sc_ref, qg_ref, kg_ref, sk_ref, mq_ref, mk_ref, do_ref,
             dq_ref, dkv_ref, dqg_ref, dkg_ref, dsk_ref, carry):
        b, i = pl.program_id(0), pl.program_id(1)
        tq, tk, valid = _attn_tables(cp_ref, cc_ref, sp_ref, sc_ref, nb - 1 - i)
        fn = functools.partial(_attn_block, tq=tq, tk=tk, mq=mq_ref[...], mk=mk_ref[...], valid=valid)
        _, vjp = jax.vjp(fn, q_ref[...], kvp_ref[...], kvc_ref[...], qg_ref[...], kg_ref[...], sk_ref[...])
        dq, dkvp, dkvc, dqg, dkg, dsk = vjp(do_ref[...])

        @pl.when(i == 0)
        def _():
            carry[...] = jnp.zeros_like(carry)

        @pl.when((b == 0) & (i == 0))
        def _():
            dqg_ref[...] = jnp.zeros_like(dqg_ref)
            dkg_ref[...] = jnp.zeros_like(dkg_ref)
            dsk_ref[...] = jnp.zeros_like(dsk_ref)

        dq_ref[...] = dq.astype(BF16)
        dkv_ref[...] = (dkvc + carry[...]).astype(BF16)
        carry[...] = dkvp
        dqg_ref[...] += dqg
        dkg_ref[...] += dkg
        dsk_ref[...] += dsk

    prev = lambda b, i: (b, jnp.maximum(nb - 2 - i, 0), 0)
    cur = lambda b, i: (b, nb - 1 - i, 0)
    return pl.pallas_call(
        body, name="attn_bwd", grid=(B, nb),
        out_shape=[jax.ShapeDtypeStruct((B, S, QW), BF16), jax.ShapeDtypeStruct((B, S, 2 * KVW), BF16),
                   jax.ShapeDtypeStruct((1, HD), F32), jax.ShapeDtypeStruct((1, HD), F32),
                   jax.ShapeDtypeStruct((1, HQ), F32)],
        in_specs=[pl.BlockSpec((None, BLK, QW), cur), pl.BlockSpec((None, BLK, 2 * KVW), prev),
                  pl.BlockSpec((None, BLK, 2 * KVW), cur), pl.BlockSpec((None, BLK, LANES), prev),
                  pl.BlockSpec((None, BLK, LANES), cur), pl.BlockSpec((None, BLK, LANES), prev),
                  pl.BlockSpec((None, BLK, LANES), cur), _full((1, HD)), _full((1, HD)), _full((1, HQ)),
                  _full((QW, QW)), _full((KVW, KVW)), pl.BlockSpec((None, BLK, QW), cur)],
        out_specs=[pl.BlockSpec((None, BLK, QW), cur), pl.BlockSpec((None, BLK, 2 * KVW), cur),
                   _full((1, HD)), _full((1, HD)), _full((1, HQ))],
        scratch_shapes=[pltpu.VMEM((BLK, 2 * KVW), F32)],
        compiler_params=_cparams(("arbitrary", "arbitrary")),
    )(aq, akv, akv, cos, cos, sin, sin, qg, kg, sinks, mq, mk, do)


def _conv_taps(xe, w, rows):
    y = None
    for j in range(CONV):
        sh = pltpu.roll(xe, CONV - 1 - j, 0)[8:8 + rows, :] if j < CONV - 1 else xe[8:8 + rows, :]
        y = sh * w[j:j + 1, :] if y is None else y + sh * w[j:j + 1, :]
    return y


def _softplus(x):
    return jnp.maximum(x, 0.0) + jnp.log1p(jnp.exp(-jnp.abs(x)))


_BMM = (((2,), (1,)), ((0,), (0,)))
_BMM_NT = (((2,), (2,)), ((0,), (0,)))
_BMM_TN = (((1,), (1,)), ((0,), (0,)))


def _bmm(a, b, dims=_BMM):
    return lax.dot_general(a.astype(BF16), b.astype(BF16), dims, preferred_element_type=F32)


def _split(a):
    hi = a.astype(BF16)
    return hi, (a - hi.astype(F32)).astype(BF16)


def _bmm3(a, b, dims=_BMM):
    ah, al = _split(a)
    bh, bl = _split(b)
    d = lambda p, q: lax.dot_general(p, q, dims, preferred_element_type=F32)
    return d(ah, bh) + (d(ah, bl) + d(al, bh))


TRI_BASE = 8


def _tri_inverse(L):
    ii = lax.broadcasted_iota(jnp.int32, (CH, CH), 0)
    jj = lax.broadcasted_iota(jnp.int32, (CH, CH), 1)
    same = lambda size: (ii // size) == (jj // size)
    diag = jnp.where(same(TRI_BASE), L, 0.0)
    X = (ii == jj).astype(F32) - diag
    P = diag
    n = 2
    while n < TRI_BASE:
        P = _bmm3(P, P)
        X = X + _bmm3(X, P)
        n *= 2
    size = TRI_BASE
    while size < CH:
        joint = jnp.where(same(2 * size) & jnp.logical_not(same(size)), L, 0.0)
        X = X - _bmm3(X, _bmm3(joint, X))
        size *= 2
    return X


@jax.custom_vjp
def _tri_inverse_known(L, T):
    return T


def _tri_inverse_known_fwd(L, T):
    return T, T


def _tri_inverse_known_bwd(T, dT):
    Tt = jnp.swapaxes(T, 1, 2)
    return -_bmm(Tt, _bmm(dT, Tt)), jnp.zeros_like(T)


_tri_inverse_known.defvjp(_tri_inverse_known_fwd, _tri_inverse_known_bwd)


def _triangle(n, upper):
    ii = lax.broadcasted_iota(jnp.int32, (n, CH, CH), 1)
    jj = lax.broadcasted_iota(jnp.int32, (n, CH, CH), 2)
    return ((ii <= jj) if upper else (ii >= jj)).astype(BF16)


@jax.custom_vjp
def _cumsum_rows(g):
    g0 = g.astype(BF16)
    r1 = g - g0.astype(F32)
    g1 = r1.astype(BF16)
    g2 = (r1 - g1.astype(F32)).astype(BF16)
    tri = _triangle(g.shape[0], False)
    d = lambda q: lax.dot_general(tri, q, _BMM, preferred_element_type=F32)
    return d(g0) + (d(g1) + d(g2))


def _cumsum_rows_fwd(g):
    return _cumsum_rows(g), None


def _cumsum_rows_bwd(_, dy):
    hi, lo = _split(dy)
    tri = _triangle(dy.shape[0], True)
    d = lambda q: lax.dot_general(tri, q, _BMM, preferred_element_type=F32)
    return (d(hi) + d(lo),)


_cumsum_rows.defvjp(_cumsum_rows_fwd, _cumsum_rows_bwd)


def _row_sums(t):
    n, r, w = t.shape
    hi, lo = _split(t.reshape(n * r, w))
    ones = jnp.ones((w, w), BF16)
    s = jnp.dot(hi, ones, preferred_element_type=F32) + jnp.dot(lo, ones, preferred_element_type=F32)
    return s.reshape(n, r, w)


def _dn_prep(t_known, qr, kr, v, a_raw, b_raw, a_log, dt_b):
    n = qr.shape[0]
    ii = lax.broadcasted_iota(jnp.int32, (n, CH, CH), 1)
    jj = lax.broadcasted_iota(jnp.int32, (n, CH, CH), 2)
    incl, strict = ii >= jj, ii > jj
    q = qr * lax.rsqrt(_row_sums(qr * qr) + EPS) * (DK ** -0.5)
    k = kr * lax.rsqrt(_row_sums(kr * kr) + EPS)
    beta = _sigmoid(b_raw)
    g = -jnp.exp(a_log) * _softplus(a_raw + dt_b)
    gcb = _cumsum_rows(jnp.broadcast_to(g, (n, CH, DK)))
    gc = gcb[:, :, 0:1]
    gc_row = jnp.swapaxes(gcb, 1, 2)[:, 0:1, 0:CH]
    decay = jnp.where(incl, jnp.exp(jnp.where(incl, gc - gc_row, 0.0)), 0.0)
    kb = k * beta
    L = jnp.where(strict, _bmm(kb, k, _BMM_NT) * decay, 0.0)
    T = _tri_inverse(L) if t_known is None else _tri_inverse_known(L, t_known)
    eg = jnp.exp(gc)
    u = _bmm(T, v * beta)
    w = _bmm(T, kb * eg)
    a_in = _bmm(q, k, _BMM_NT) * decay
    g_last = gc[:, CH - 1:CH, :]
    return u, w, q * eg, k * jnp.exp(g_last - gc), a_in, jnp.exp(g_last), T


def _dn_step(S0, u, w, qd, kd, a_in, cd):
    r = _bmm(jnp.concatenate([w, qd], axis=1), S0)
    v_new = u - r[:, 0:CH, :]
    o = r[:, CH:2 * CH, :] + _bmm(a_in, v_new)
    S1 = S0 * cd + _bmm(kd, v_new, _BMM_TN)
    return o, S1


def _dn_stack(cq, ba, al, dt, G):
    cols = [[] for _ in range(7)]
    for c in range(G):
        rows = slice(CH * c, CH * (c + 1))
        for h in range(DH):
            parts = (cq[rows, DK * h:DK * (h + 1)], cq[rows, DNW + DK * h:DNW + DK * (h + 1)],
                     cq[rows, 2 * DNW + DK * h:2 * DNW + DK * (h + 1)], ba[rows, DH + h:DH + h + 1],
                     ba[rows, h:h + 1], al[:, h:h + 1], dt[:, h:h + 1])
            for col, p in zip(cols, parts):
                col.append(p)
    return tuple(jnp.stack(col) for col in cols)


def _dn_group(S, want):
    g = want
    while (S // CH) % g:
        g //= 2
    return g


def _dn_prep_fwd(xin, conv_w, ba, a_log, dt_b):
    B, S, _ = xin.shape
    nc = S // CH
    G = _dn_group(S, 8)
    r8 = G * CH // 8

    def body(xp_ref, x_ref, cw_ref, ba_ref, al_ref, dt_ref, cq_ref, u_ref, w_ref, qd_ref, kd_ref, a_ref, t_ref, cd_ref):
        xp = jnp.where(pl.program_id(1) > 0, xp_ref[...], 0.0)
        cq = _silu(_conv_taps(jnp.concatenate([xp, x_ref[...]], axis=0), cw_ref[...], G * CH))
        cq_ref[...] = cq
        ops = _dn_stack(cq, ba_ref[...], al_ref[...], dt_ref[...], G)
        u, w, qd, kd, a_in, cd, T = _dn_prep(None, *ops)
        lane4 = lax.broadcasted_iota(jnp.int32, (1, DH), 1)
        for c in range(G):
            rows = slice(CH * c, CH * (c + 1))
            cdrow = jnp.zeros((1, DH), F32)
            for h in range(DH):
                n = DH * c + h
                lanes = slice(DK * h, DK * (h + 1))
                u_ref[rows, lanes] = u[n]
                w_ref[rows, lanes] = w[n]
                qd_ref[rows, lanes] = qd[n]
                kd_ref[rows, lanes] = kd[n]
                a_ref[rows, CH * h:CH * (h + 1)] = a_in[n]
                t_ref[rows, CH * h:CH * (h + 1)] = T[n]
                cdrow = cdrow + jnp.where(lane4 == h, cd[n], 0.0)
            cd_ref[c] = cdrow

    wide = jax.ShapeDtypeStruct((B, S, DNW), F32)
    sq = jax.ShapeDtypeStruct((B, S, DH * CH), F32)
    return pl.pallas_call(
        body, name="dn_prep_fwd", grid=(B, nc // G),
        out_shape=[jax.ShapeDtypeStruct((B, S, CONVW), F32), wide, wide, wide, wide, sq, sq,
                   jax.ShapeDtypeStruct((B, nc, 1, DH), F32)],
        in_specs=[pl.BlockSpec((None, 8, CONVW), lambda b, i: (b, jnp.maximum(i * r8 - 1, 0), 0)),
                  _rows(G * CH, CONVW), _full((CONV, CONVW)), _rows(G * CH, 2 * DH), _full((1, DH)), _full((1, DH))],
        out_specs=[_rows(G * CH, CONVW)] + [_rows(G * CH, DNW)] * 4 + [_rows(G * CH, DH * CH)] * 2
                  + [pl.BlockSpec((None, G, 1, DH), lambda b, i: (b, i, 0, 0))],
        compiler_params=_cparams(("parallel", "arbitrary")),
    )(xin, xin, conv_w, ba, a_log, dt_b)


def _dn_seq_specs(B, steps, gs, rev):
    at = (lambda i: steps - 1 - i) if rev else (lambda i: i)
    wide = pl.BlockSpec((B, gs * CH, DNW), lambda i: (0, at(i), 0))
    a_spec = pl.BlockSpec((B, gs * CH, DH * CH), lambda i: (0, at(i), 0))
    cd_spec = pl.BlockSpec((B, gs, 1, DH), lambda i: (0, at(i), 0, 0))
    st_spec = pl.BlockSpec((B, gs, DH, DK, DK), lambda i: (0, at(i), 0, 0, 0))
    return wide, a_spec, cd_spec, st_spec


def _dn_step_operands(B, c, u_ref, w_ref, qd_ref, kd_ref, a_ref, cd_ref):
    pairs = [(b, h) for b in range(B) for h in range(DH)]
    rows = slice(CH * c, CH * (c + 1))
    wide = lambda ref: jnp.stack([ref[b, rows, DK * h:DK * (h + 1)] for b, h in pairs])
    a_in = jnp.stack([a_ref[b, rows, CH * h:CH * (h + 1)] for b, h in pairs])
    cd = jnp.stack([cd_ref[b, c, :, h:h + 1] for b, h in pairs])
    return wide(u_ref), wide(w_ref), wide(qd_ref), wide(kd_ref), a_in, cd


def _dn_seq_fwd(u, w, qd, kd, a_in, cd):
    B, S, _ = u.shape
    nc = S // CH
    gs = _dn_group(S, 8)

    def body(u_ref, w_ref, qd_ref, kd_ref, a_ref, cd_ref, o_ref, st_ref, state):
        @pl.when(pl.program_id(0) == 0)
        def _():
            state[...] = jnp.zeros_like(state)

        S0 = state[...]
        for c in range(gs):
            for b in range(B):
                st_ref[b, c] = S0[DH * b:DH * (b + 1)]
            o, S0 = _dn_step(S0, *_dn_step_operands(B, c, u_ref, w_ref, qd_ref, kd_ref, a_ref, cd_ref))
            for b in range(B):
                for h in range(DH):
                    o_ref[b, CH * c:CH * (c + 1), DK * h:DK * (h + 1)] = o[DH * b + h]
        state[...] = S0

    wide, a_spec, cd_spec, st_spec = _dn_seq_specs(B, nc // gs, gs, False)
    return pl.pallas_call(
        body, name="dn_seq_fwd", grid=(nc // gs,),
        out_shape=[jax.ShapeDtypeStruct((B, S, DNW), F32), jax.ShapeDtypeStruct((B, nc, DH, DK, DK), F32)],
        in_specs=[wide, wide, wide, wide, a_spec, cd_spec],
        out_specs=[wide, st_spec],
        scratch_shapes=[pltpu.VMEM((B * DH, DK, DK), F32)],
        compiler_params=_cparams(("arbitrary",)),
    )(u, w, qd, kd, a_in, cd)


def _dn_seq_bwd(u, w, qd, kd, a_in, cd, states, do):
    B, S, _ = u.shape
    nc = S // CH
    gs = _dn_group(S, 4)

    def body(u_ref, w_ref, qd_ref, kd_ref, a_ref, cd_ref, st_ref, do_ref,
             du_ref, dw_ref, dqd_ref, dkd_ref, da_ref, dcd_ref, dstate):
        @pl.when(pl.program_id(0) == 0)
        def _():
            dstate[...] = jnp.zeros_like(dstate)

        lane4 = lax.broadcasted_iota(jnp.int32, (1, DH), 1)
        dS = dstate[...]
        for c in reversed(range(gs)):
            rows = slice(CH * c, CH * (c + 1))
            S0 = jnp.concatenate([st_ref[b, c] for b in range(B)], axis=0)
            do = jnp.stack([do_ref[b, rows, DK * h:DK * (h + 1)] for b in range(B) for h in range(DH)])
            _, vjp = jax.vjp(_dn_step, S0, *_dn_step_operands(B, c, u_ref, w_ref, qd_ref, kd_ref, a_ref, cd_ref))
            dS, du, dw, dqd, dkd, da, dcd = vjp((do, dS))
            for b in range(B):
                dcdrow = jnp.zeros((1, DH), F32)
                for h in range(DH):
                    n = DH * b + h
                    lanes = slice(DK * h, DK * (h + 1))
                    du_ref[b, rows, lanes] = du[n]
                    dw_ref[b, rows, lanes] = dw[n]
                    dqd_ref[b, rows, lanes] = dqd[n]
                    dkd_ref[b, rows, lanes] = dkd[n]
                    da_ref[b, rows, CH * h:CH * (h + 1)] = da[n]
                    dcdrow = dcdrow + jnp.where(lane4 == h, dcd[n], 0.0)
                dcd_ref[b, c] = dcdrow
        dstate[...] = dS

    wide, a_spec, cd_spec, st_spec = _dn_seq_specs(B, nc // gs, gs, True)
    sd = jax.ShapeDtypeStruct((B, S, DNW), F32)
    return pl.pallas_call(
        body, name="dn_seq_bwd", grid=(nc // gs,),
        out_shape=[sd, sd, sd, sd, jax.ShapeDtypeStruct((B, S, DH * CH), F32), jax.ShapeDtypeStruct((B, nc, 1, DH), F32)],
        in_specs=[wide, wide, wide, wide, a_spec, cd_spec, st_spec, wide],
        out_specs=[wide, wide, wide, wide, a_spec, cd_spec],
        scratch_shapes=[pltpu.VMEM((B * DH, DK, DK), F32)],
        compiler_params=_cparams(("arbitrary",)),
    )(u, w, qd, kd, a_in, cd, states, do)


def _dn_prep_bwd(xin, conv_w, cq, ba, a_log, dt_b, t_inv, du, dw, dqd, dkd, da, dcd):
    B, S, _ = cq.shape
    nc = S // CH
    G = _dn_group(S, 8)
    R = G * CH
    nblk = nc // G
    r8 = R // 8

    def body(xp_ref, x_ref, cw_ref, cq_ref, ba_ref, al_ref, dt_ref, t_ref, du_ref, dw_ref, dqd_ref, dkd_ref, da_ref, dcd_ref,
             dx_ref, dcw_ref, dba_ref, dal_ref, ddt_ref, carry):
        i = pl.program_id(1)

        @pl.when((pl.program_id(0) == 0) & (i == 0))
        def _():
            dal_ref[...] = jnp.zeros_like(dal_ref)
            ddt_ref[...] = jnp.zeros_like(ddt_ref)
            dcw_ref[...] = jnp.zeros_like(dcw_ref)

        @pl.when(i == 0)
        def _():
            carry[...] = jnp.zeros_like(carry)

        pairs = [(c, h) for c in range(G) for h in range(DH)]
        rows = lambda c: slice(CH * c, CH * (c + 1))
        wide = lambda ref: jnp.stack([ref[rows(c), DK * h:DK * (h + 1)] for c, h in pairs])
        square = lambda ref: jnp.stack([ref[rows(c), CH * h:CH * (h + 1)] for c, h in pairs])
        ops = _dn_stack(cq_ref[...], ba_ref[...], al_ref[...], dt_ref[...], G)
        cots = (wide(du_ref), wide(dw_ref), wide(dqd_ref), wide(dkd_ref), square(da_ref),
                jnp.stack([dcd_ref[c][:, h:h + 1] for c, h in pairs]), jnp.zeros((len(pairs), CH, CH), F32))
        _, vjp = jax.vjp(functools.partial(_dn_prep, square(t_ref)), *ops)
        dq, dk, dv, dar, dbr, dl, dd = vjp(cots)
        lane8 = lax.broadcasted_iota(jnp.int32, (CH, 2 * DH), 1)
        lane4 = lax.broadcasted_iota(jnp.int32, (1, DH), 1)
        dal = jnp.zeros((1, DH), F32)
        ddt = jnp.zeros((1, DH), F32)
        for c in range(G):
            dba = jnp.zeros((CH, 2 * DH), F32)
            for h in range(DH):
                n = DH * c + h
                dba = dba + jnp.where(lane8 == h, dbr[n], 0.0) + jnp.where(lane8 == DH + h, dar[n], 0.0)
                dal = dal + jnp.where(lane4 == h, dl[n], 0.0)
                ddt = ddt + jnp.where(lane4 == h, dd[n], 0.0)
            dba_ref[rows(c), :] = dba.astype(BF16)
        dal_ref[...] += dal
        ddt_ref[...] += ddt

        dcq = jnp.concatenate([jnp.concatenate([t[DH * c + h] for t in (dq, dk, dv) for h in range(DH)], axis=1)
                               for c in range(G)], axis=0)
        w = cw_ref[...]
        xp = jnp.where(i < nblk - 1, xp_ref[...], 0.0)
        xe = jnp.concatenate([xp, x_ref[...]], axis=0)
        taps = [(pltpu.roll(xe, CONV - 1 - j, 0) if j < CONV - 1 else xe)[8:8 + R, :] for j in range(CONV)]
        pre = sum(t * w[j:j + 1, :] for j, t in enumerate(taps))
        sg = _sigmoid(pre)
        dpre = dcq * (sg * (1.0 + pre * (1.0 - sg)))
        ext = jnp.concatenate([dpre, carry[...]], axis=0)
        dx = dpre * w[CONV - 1:CONV, :]
        for j in range(CONV - 1):
            dx = dx + pltpu.roll(ext, R + 8 - (CONV - 1 - j), 0)[0:R, :] * w[j:j + 1, :]
        dx_ref[...] = dx.astype(BF16)
        carry[...] = dpre[0:8, :]
        lane_row = lax.broadcasted_iota(jnp.int32, (CONV, CONVW), 0)
        dcw = jnp.zeros((CONV, CONVW), F32)
        for j in range(CONV):
            dcw = dcw + jnp.where(lane_row == j, jnp.sum(taps[j] * dpre, axis=0, keepdims=True), 0.0)
        dcw_ref[...] += dcw

    rev = lambda w: pl.BlockSpec((None, R, w), lambda b, i: (b, nblk - 1 - i, 0))
    return pl.pallas_call(
        body, name="dn_prep_bwd", grid=(B, nblk),
        out_shape=[jax.ShapeDtypeStruct((B, S, CONVW), BF16), jax.ShapeDtypeStruct((CONV, CONVW), F32),
                   jax.ShapeDtypeStruct((B, S, 2 * DH), BF16), jax.ShapeDtypeStruct((1, DH), F32),
                   jax.ShapeDtypeStruct((1, DH), F32)],
        in_specs=[pl.BlockSpec((None, 8, CONVW), lambda b, i: (b, jnp.maximum((nblk - 1 - i) * r8 - 1, 0), 0)),
                  rev(CONVW), _full((CONV, CONVW)), rev(CONVW), rev(2 * DH), _full((1, DH)), _full((1, DH)), rev(DH * CH)]
                 + [rev(DNW)] * 4 + [rev(DH * CH), pl.BlockSpec((None, G, 1, DH), lambda b, i: (b, nblk - 1 - i, 0, 0))],
        out_specs=[rev(CONVW), _full((CONV, CONVW)), rev(2 * DH), _full((1, DH)), _full((1, DH))],
        scratch_shapes=[pltpu.VMEM((8, CONVW), F32)],
        compiler_params=_cparams(("arbitrary", "arbitrary")),
    )(xin, xin, conv_w, cq, ba, a_log, dt_b, t_inv, du, dw, dqd, dkd, da, dcd)


def _gated_norm(o, z, g):
    outs = []
    for h in range(DH):
        t = o[:, DK * h:DK * (h + 1)]
        r = lax.rsqrt(jnp.mean(t * t, axis=-1, keepdims=True) + EPS)
        outs.append(t * r * g * _silu(z[:, DK * h:DK * (h + 1)]))
    return jnp.concatenate(outs, axis=1)


def _mix_fwd(x, o_attn, o_dn, z, ga, gd, mod, dn_g, w_branch, w_out):
    B, S, _ = x.shape
    tm = _tile(S, 512)

    def body(x_ref, oa_ref, od_ref, z_ref, ga_ref, gd_ref, mod_ref, g_ref, wb_ref, wo_ref,
             x1_ref, mix_ref, mg_ref, ob_ref):
        oa = oa_ref[...].astype(BF16)
        od = _gated_norm(od_ref[...], z_ref[...], g_ref[...]).astype(BF16)
        ob_ref[0] = oa
        ob_ref[1] = od
        ya = jnp.dot(oa, wb_ref[0:QW, :], preferred_element_type=F32)
        yd = jnp.dot(od, wb_ref[QW:QW + DNW, :], preferred_element_type=F32)
        merged = (_sigmoid(ga_ref[...]) * ya + _sigmoid(gd_ref[...]) * yd).astype(BF16)
        mg_ref[...] = merged
        mix = jnp.dot(merged, wo_ref[...], preferred_element_type=F32)
        mix_ref[...] = mix
        x1_ref[...] = x_ref[...] + mod_ref[2:3, :] * mix

    return pl.pallas_call(
        body, name="mix_fwd", grid=(B, S // tm),
        out_shape=[jax.ShapeDtypeStruct((B, S, D), F32), jax.ShapeDtypeStruct((B, S, D), F32),
                   jax.ShapeDtypeStruct((B, S, D), BF16), jax.ShapeDtypeStruct((B, 2, S, QW), BF16)],
        in_specs=[_rows(tm, D), _rows(tm, QW), _rows(tm, DNW), _rows(tm, DNW), _rows(tm, D), _rows(tm, D),
                  _perb(6, D), _full((1, DK)), _resident(w_branch.shape), _resident(w_out.shape)],
        out_specs=[_rows(tm, D), _rows(tm, D), _rows(tm, D), _stacked(2, tm, QW)],
        compiler_params=_cparams(("parallel", "arbitrary")),
    )(x, o_attn, o_dn, z, ga, gd, mod, dn_g, w_branch, w_out)


def _mix_bwd(dx1, mix, o_attn, o_dn, z, ga, gd, mod, dn_g, w_branch, w_out):
    B, S, _ = dx1.shape
    tm = _tile(S, 512)

    def body(dx1_ref, mix_ref, oa_ref, od_ref, z_ref, ga_ref, gd_ref, mod_ref, g_ref, wb_ref, wo_ref,
             dmix_ref, dyo_ref, dga_ref, dgd_ref, dz_ref, doa_ref, dod_ref, dgate_ref, dg_ref):
        b, i = pl.program_id(0), pl.program_id(1)
        dx1 = dx1_ref[...]
        dmix = (dx1 * mod_ref[2:3, :]).astype(BF16)
        dmix_ref[...] = dmix
        dgate = jnp.sum(dx1 * mix_ref[...], axis=0, keepdims=True)
        dmerged = _dot_nt(dmix, wo_ref[...])
        odn, gn_vjp = jax.vjp(_gated_norm, od_ref[...], z_ref[...], g_ref[...])
        ya = _dot(oa_ref[...], wb_ref[0:QW, :])
        yd = _dot(odn, wb_ref[QW:QW + DNW, :])
        sa, sd = _sigmoid(ga_ref[...]), _sigmoid(gd_ref[...])
        dya = (dmerged * sa).astype(BF16)
        dyd = (dmerged * sd).astype(BF16)
        dyo_ref[0] = dya
        dyo_ref[1] = dyd
        dga_ref[...] = (dmerged * ya * sa * (1.0 - sa)).astype(BF16)
        dgd_ref[...] = (dmerged * yd * sd * (1.0 - sd)).astype(BF16)
        doa_ref[...] = _dot_nt(dya, wb_ref[0:QW, :])
        dodn = _dot_nt(dyd, wb_ref[QW:QW + DNW, :])
        dod, dz, dg = gn_vjp(dodn)
        dod_ref[...] = dod
        dz_ref[...] = dz.astype(BF16)

        @pl.when(i == 0)
        def _():
            dgate_ref[...] = jnp.zeros_like(dgate_ref)

        @pl.when((b == 0) & (i == 0))
        def _():
            dg_ref[...] = jnp.zeros_like(dg_ref)

        dgate_ref[...] += dgate
        dg_ref[...] += dg

    return pl.pallas_call(
        body, name="mix_bwd", grid=(B, S // tm),
        out_shape=[jax.ShapeDtypeStruct((B, S, D), BF16), jax.ShapeDtypeStruct((B, 2, S, D), BF16),
                   jax.ShapeDtypeStruct((B, S, D), BF16), jax.ShapeDtypeStruct((B, S, D), BF16),
                   jax.ShapeDtypeStruct((B, S, DNW), BF16),
                   jax.ShapeDtypeStruct((B, S, QW), F32), jax.ShapeDtypeStruct((B, S, DNW), F32),
                   jax.ShapeDtypeStruct((B, 1, D), F32), jax.ShapeDtypeStruct((1, DK), F32)],
        in_specs=[_rows(tm, D), _rows(tm, D), _rows(tm, QW), _rows(tm, DNW), _rows(tm, DNW), _rows(tm, D),
                  _rows(tm, D), _perb(6, D), _full((1, DK)), _resident(w_branch.shape), _resident(w_out.shape)],
        out_specs=[_rows(tm, D), _stacked(2, tm, D), _rows(tm, D), _rows(tm, D), _rows(tm, DNW),
                   _rows(tm, QW), _rows(tm, DNW), _perb(1, D), _full((1, DK))],
        compiler_params=_cparams(("arbitrary", "arbitrary")),
    )(dx1, mix, o_attn, o_dn, z, ga, gd, mod, dn_g, w_branch, w_out)


GU_SHARD = 2 * FFN // N_DEV
GU_HALF = N_DEV // 2


def _ffn1_fwd(x1, mod, g2, w_gu):
    B, S, _ = x1.shape
    tm = _tile(S)

    def body(x_ref, mod_ref, g_ref, w_ref, h_ref, dgate_ref, dup_ref, act_ref):
        h = _rms_mod(x_ref[...], g_ref[...], mod_ref[4:5, :], mod_ref[3:4, :]).astype(BF16)
        h_ref[...] = h
        for j in range(GU_HALF):
            gate = _dot_nt(h, w_ref[j])
            up = _dot_nt(h, w_ref[GU_HALF + j])
            sg = _sigmoid(gate)
            silu = gate * sg
            dgate_ref[j] = up * (sg * (1.0 + gate * (1.0 - sg)))
            dup_ref[j] = silu
            act_ref[j] = (silu * up).astype(BF16)

    blk = lambda dt: jax.ShapeDtypeStruct((B, GU_HALF, S, GU_SHARD), dt)
    return pl.pallas_call(
        body, name="ffn1_fwd", grid=(B, S // tm),
        out_shape=[jax.ShapeDtypeStruct((B, S, D), BF16), blk(F32), blk(F32), blk(BF16)],
        in_specs=[_rows(tm, D), _perb(6, D), _full((1, D)), _resident(w_gu.shape)],
        out_specs=[_rows(tm, D)] + [_stacked(GU_HALF, tm, GU_SHARD)] * 3,
        compiler_params=_cparams(("parallel", "arbitrary")),
    )(x1, mod, g2, w_gu)


def _ffn2_fwd(act, x1, target, mod, w_down):
    B, S, _ = x1.shape
    tm = _tile(S, 512)

    def body(a_ref, x_ref, t_ref, mod_ref, w_ref, dy_ref, loss_ref, dgate_ref):
        b, i = pl.program_id(0), pl.program_id(1)
        y = jnp.dot(a_ref[0], w_ref[0], preferred_element_type=F32)
        for j in range(1, GU_HALF):
            y = y + jnp.dot(a_ref[j], w_ref[j], preferred_element_type=F32)
        err = x_ref[...] + mod_ref[5:6, :] * y - t_ref[...]
        dy = err * (1.0 / D)
        dy_ref[...] = dy

        @pl.when((b == 0) & (i == 0))
        def _():
            loss_ref[...] = jnp.zeros_like(loss_ref)

        @pl.when(i == 0)
        def _():
            dgate_ref[...] = jnp.zeros_like(dgate_ref)

        loss_ref[...] += (0.5 / D) * jnp.sum(err * err)
        dgate_ref[...] += jnp.sum(dy * y, axis=0, keepdims=True)

    return pl.pallas_call(
        body, name="ffn2_fwd", grid=(B, S // tm),
        out_shape=[jax.ShapeDtypeStruct((B, S, D), F32), jax.ShapeDtypeStruct((1, 128), F32),
                   jax.ShapeDtypeStruct((B, 1, D), F32)],
        in_specs=[_stacked(GU_HALF, tm, GU_SHARD), _rows(tm, D), _rows(tm, D), _perb(6, D), _resident(w_down.shape)],
        out_specs=[_rows(tm, D), _full((1, 128)), _perb(1, D)],
        compiler_params=_cparams(("arbitrary", "arbitrary")),
    )(act, x1, target, mod, w_down)


def _ffn2_bwd(dy, act_dgate, act_dup, mod, w_down):
    B, S, _ = dy.shape
    tm = _tile(S, 512)

    def body(dy_ref, dgate_ref, dup_ref, mod_ref, w_ref, dgu_ref, dyg_ref):
        dyg = (dy_ref[...] * mod_ref[5:6, :]).astype(BF16)
        dyg_ref[...] = dyg
        for j in range(GU_HALF):
            dact = _dot_nt(dyg, w_ref[j])
            dgu_ref[j] = (dact * dgate_ref[j]).astype(BF16)
            dgu_ref[GU_HALF + j] = (dact * dup_ref[j]).astype(BF16)

    return pl.pallas_call(
        body, name="ffn2_bwd", grid=(B, S // tm),
        out_shape=[jax.ShapeDtypeStruct((B, N_DEV, S, GU_SHARD), BF16), jax.ShapeDtypeStruct((B, S, D), BF16)],
        in_specs=[_rows(tm, D), _stacked(GU_HALF, tm, GU_SHARD), _stacked(GU_HALF, tm, GU_SHARD), _perb(6, D),
                  _resident(w_down.shape)],
        out_specs=[_stacked(N_DEV, tm, GU_SHARD), _rows(tm, D)],
        compiler_params=_cparams(("parallel", "arbitrary")),
    )(dy, act_dgate, act_dup, mod, w_down)


def _ffn1_bwd(dgu, x1, dy, mod, g2, w_gu):
    B, S, _ = x1.shape
    tm = _tile(S, 512)

    def body(dgu_ref, x_ref, dy_ref, mod_ref, g_ref, w_ref, dx1_ref, dg_ref, dsc_ref, dsh_ref):
        b, i = pl.program_id(0), pl.program_id(1)
        dh = jnp.dot(dgu_ref[0], w_ref[0], preferred_element_type=F32)
        for j in range(1, N_DEV):
            dh = dh + jnp.dot(dgu_ref[j], w_ref[j], preferred_element_type=F32)
        _, vjp = jax.vjp(_rms_mod, x_ref[...], g_ref[...], mod_ref[4:5, :], mod_ref[3:4, :])
        dx, dg, dsc, dsh = vjp(dh)
        dx1_ref[...] = dy_ref[...] + dx

        @pl.when((b == 0) & (i == 0))
        def _():
            dg_ref[...] = jnp.zeros_like(dg_ref)

        @pl.when(i == 0)
        def _():
            dsc_ref[...] = jnp.zeros_like(dsc_ref)
            dsh_ref[...] = jnp.zeros_like(dsh_ref)

        dg_ref[...] += dg
        dsc_ref[...] += dsc
        dsh_ref[...] += dsh

    return pl.pallas_call(
        body, name="ffn1_bwd", grid=(B, S // tm),
        out_shape=[jax.ShapeDtypeStruct((B, S, D), F32), jax.ShapeDtypeStruct((1, D), F32),
                   jax.ShapeDtypeStruct((B, 1, D), F32), jax.ShapeDtypeStruct((B, 1, D), F32)],
        in_specs=[_stacked(N_DEV, tm, GU_SHARD), _rows(tm, D), _rows(tm, D), _perb(6, D), _full((1, D)),
                  _resident(w_gu.shape)],
        out_specs=[_rows(tm, D), _full((1, D)), _perb(1, D), _perb(1, D)],
        compiler_params=_cparams(("arbitrary", "arbitrary")),
    )(dgu, x1, dy, mod, g2, w_gu)


def _adamw(w, g, m, v, name):
    def body(w_ref, g_ref, m_ref, v_ref, d_ref, nm_ref, nv_ref):
        d_ref[...], nm_ref[...], nv_ref[...] = _adamw_math(w_ref[...], g_ref[...], m_ref[...], v_ref[...])

    sd = jax.ShapeDtypeStruct(w.shape, F32)
    return pl.pallas_call(body, name=name, out_shape=(sd, sd, sd), compiler_params=_cparams())(w, g, m, v)


def kernel(x, c, positions, ada_w, ada_b, norm1_g, w_in, conv_w, q_norm_g, k_norm_g, sinks, a_log, dt_bias, dn_norm_g, w_branch, w_out, norm2_g, w_gate_up, w_down, loss_target, m_ada_w, m_ada_b, m_norm1_g, m_w_in, m_conv_w, m_q_norm_g, m_k_norm_g, m_sinks, m_a_log, m_dt_bias, m_dn_norm_g, m_w_branch, m_w_out, m_norm2_g, m_w_gate_up, m_w_down, v_ada_w, v_ada_b, v_norm1_g, v_w_in, v_conv_w, v_q_norm_g, v_k_norm_g, v_sinks, v_a_log, v_dt_bias, v_dn_norm_g, v_w_branch, v_w_out, v_norm2_g, v_w_gate_up, v_w_down):
    B, S, _ = x.shape
    me = 4 * lax.axis_index("x") + 2 * lax.axis_index("y") + lax.axis_index("c")

    tr = lambda t: jnp.swapaxes(t, 1, 2)
    shards = [w[0].astype(BF16) for w in (tr(w_in), w_branch, w_out, tr(w_gate_up), w_down)]

    c_all = _all_gather_small(c, "gather_c").reshape(N_DEV * B, D)
    ncol = 6 * D // N_DEV
    mod_cols, cond_all = _ada_fwd(c_all, ada_w[0], lax.dynamic_slice(ada_b, (0, me * ncol), (1, ncol)))
    mod_all = _all_gather_small(mod_cols, "gather_mod").transpose(1, 0, 2).reshape(N_DEV * B, 6 * D)
    mod = lax.dynamic_slice(mod_all, (me * B, 0), (B, 6 * D)).reshape(B, 6, D)
    conv2 = conv_w.reshape(CONV, CONVW // N_DEV)
    conv_all = _all_gather_small(conv2, "gather_conv").transpose(1, 0, 2).reshape(CONV, CONVW)

    invf, mean_q, mean_k = _attn_consts()
    w_in_b, rope_cos, rope_sin = _all_gather_big(shards[:1], "gather_w_in", after=(mod, conv_all),
                                                 side=_rope_tables_side(positions.reshape(B, S, 1), invf))
    w_sems, w_srcs, w_lands, w_token = _copies_start(shards[1:], [_place_own(s, me) for s in shards[1:]], False,
                                                    w_in_b, "gather_rest_start")

    w_in_t = w_in_b.reshape(IN_W, D)
    h1, aq, akv, dnx, ba, z, ga, gd = _inproj_fwd(x, mod, norm1_g + w_token[0, 0], w_in_t)
    o_attn = _attn_fwd(aq, akv, rope_cos, rope_sin, q_norm_g, k_norm_g, sinks, mean_q, mean_k)
    cq, dn_u, dn_w, dn_qd, dn_kd, dn_a, dn_t, dn_cd = _dn_prep_fwd(dnx, conv_all, ba, a_log, dt_bias)
    o_dn, states = _dn_seq_fwd(dn_u, dn_w, dn_qd, dn_kd, dn_a, dn_cd)
    w_branch_g, w_out_g, w_gu_b, w_down_g = _copies_wait(w_sems, w_srcs, w_lands, o_dn, "gather_wait_rest")
    w_branch_f = w_branch_g.reshape(D, D)
    w_out_f = w_out_g.reshape(D, D)
    w_down_b = w_down_g.reshape(GU_HALF, GU_SHARD, D)
    x1, mix, merged, ob = _mix_fwd(x, o_attn, o_dn, z, ga, gd, mod, dn_norm_g, w_branch_f, w_out_f)
    h2, act_dgate, act_dup, act = _ffn1_fwd(x1, mod, norm2_g, w_gu_b)
    dy, loss_part, d_gate2 = _ffn2_fwd(act, x1, loss_target, mod, w_down_b)
    loss = lax.psum(loss_part[0, 0], ("x", "y", "c"))

    one = lambda t: t.reshape(B, 1, S, t.shape[-1])
    dgu, dyg = _ffn2_bwd(dy, act_dgate, act_dup, mod, w_down_b)
    g_w_down = _wgrad(act, one(dyg), "wgrad_down")
    dx1, d_n2g, d_scale2, d_shift2 = _ffn1_bwd(dgu, x1, dy, mod, norm2_g, w_gu_b)
    g_w_gu = _wgrad(dgu, one(h2), "wgrad_gate_up")
    ffn = _exchange_start([g_w_gu, g_w_down.reshape(N_DEV, FFN // N_DEV, D)], me, dx1, "exchange_ffn_start")
    dmix, dyo, dga, dgd, dz, d_oa, d_od, d_gate1, d_dng = _mix_bwd(
        dx1, mix, o_attn, o_dn, z, ga, gd, mod, dn_norm_g + ffn[3][0, 0], w_branch_f, w_out_f)
    d_dn = _dn_seq_bwd(dn_u, dn_w, dn_qd, dn_kd, dn_a, dn_cd, states, d_od)
    ddnx, d_conv, dba, d_alog, d_dtb = _dn_prep_bwd(dnx, conv_all, cq, ba, a_log, dt_bias, dn_t, *d_dn)
    daq, dakv, d_qg, d_kg, d_sinks = _attn_bwd(aq, akv, rope_cos, rope_sin, q_norm_g, k_norm_g, sinks, mean_q, mean_k, d_oa)
    dps = [daq, dakv, ddnx, dba, dz, dga, dgd]
    dblk, grad_x, d_n1g, d_scale1, d_shift1 = _inproj_bwd(x, mod, norm1_g, dx1, dps, w_in_t)

    dmod = jnp.concatenate([d_shift1, d_scale1, d_gate1, d_shift2, d_scale2, d_gate2], axis=2).reshape(B, 6 * D)
    small = jnp.concatenate([d_n1g, d_qg, d_kg, d_sinks, d_alog, d_dtb, d_dng, d_n2g, d_conv.reshape(1, CONV * CONVW)], axis=1)
    nsm = small.shape[1]
    width = -(-max(6 * D, nsm) // 128) * 128
    rows = jnp.concatenate([jnp.pad(dmod, ((0, 0), (0, width - 6 * D))), jnp.pad(small, ((0, 8 - B - 1), (0, width - nsm)))], axis=0)
    rows_all = _all_gather_small(rows, "gather_small")
    dmod_all = rows_all[:, 0:B, 0:6 * D].reshape(N_DEV * B, 6 * D)
    dmod_cols = lax.dynamic_slice(dmod_all, (0, me * ncol), (N_DEV * B, ncol))
    grad_ada_w, grad_ada_b, small_sum = _ada_bwd(cond_all, dmod_all, dmod_cols, rows_all[:, B, :])
    sizes = [D, HD, HD, HQ, DH, DH, DK, D]
    so = np.cumsum([0] + sizes)
    g_n1, g_qg, g_kg, g_sk, g_al, g_dt, g_dn, g_n2 = [small_sum[:, so[i]:so[i + 1]] for i in range(8)]
    g_conv_all = small_sum[:, so[8]:so[8] + CONV * CONVW].reshape(CONV, N_DEV, CONVW // N_DEV)
    grad_conv = lax.dynamic_slice(g_conv_all, (0, me, 0), (CONV, 1, CONVW // N_DEV)).reshape(CONV, CONVW // N_DEV)

    half = D // 2
    g_w_in_a = _wgrad(dblk, one(h1), "wgrad_in_a", after=small_sum, b_lanes=(0, half))
    proj_a = _exchange_start([g_w_in_a], me, small_sum, "exchange_in_a_start")
    g_w_in_b = _wgrad(dblk, one(h1), "wgrad_in_b", after=proj_a[3], b_lanes=(1, half))
    proj = _exchange_start([g_w_in_b], me, proj_a[3], "exchange_in_b_start")
    g_w_out = _wgrad(one(merged), one(dmix), "wgrad_out", after=proj[3])
    g_w_branch = _wgrad(ob, dyo, "wgrad_branch", after=proj[3])
    mixer = _exchange_start([g_w_branch.reshape(N_DEV, D // N_DEV, D), g_w_out.reshape(N_DEV, D // N_DEV, D)], me,
                            proj[3], "exchange_mix_start")

    upd, grads = {}, {}

    def finish(names, parts, weights):
        for nm, p, (w, m, v) in zip(names, parts, weights):
            grads[nm], *upd[nm] = _sum_adamw(p, w, m, v, "update_" + nm)
        return grads[names[-1]]

    finish(["w_gate_up", "w_down"], _copies_wait(*ffn[:3], mixer[3], "exchange_ffn_wait"),
           [(tr(w_gate_up), tr(m_w_gate_up), tr(v_w_gate_up)), (w_down, m_w_down, v_w_down)])
    (in_a,) = _copies_wait(*proj_a[:3], grads["w_gate_up"], "exchange_in_a_wait")
    (in_b,) = _copies_wait(*proj[:3], in_a, "exchange_in_b_wait")
    finish(["w_in"], [[in_a, in_b]], [(tr(w_in), tr(m_w_in), tr(v_w_in))])
    finish(["w_branch", "w_out"], _copies_wait(*mixer[:3], grads["w_in"], "exchange_mix_wait"),
           [(w_branch, m_w_branch, v_w_branch), (w_out, m_w_out, v_w_out)])
    for nm in ("w_in", "w_gate_up"):
        grads[nm], upd[nm] = tr(grads[nm]), [tr(t) for t in upd[nm]]

    grads["ada_w"] = grad_ada_w.reshape(ada_w.shape)
    upd["ada_w"] = _adamw(ada_w, grads["ada_w"], m_ada_w, v_ada_w, "adamw_ada_w")
    small_names = ["ada_b", "norm1_g", "q_norm_g", "k_norm_g", "sinks", "a_log", "dt_bias", "dn_norm_g", "norm2_g", "conv_w"]
    small_w = [ada_b, norm1_g, q_norm_g, k_norm_g, sinks, a_log, dt_bias, dn_norm_g, norm2_g, conv_w]
    small_g = [grad_ada_b, g_n1, g_qg, g_kg, g_sk, g_al, g_dt, g_dn, g_n2, grad_conv]
    small_m = [m_ada_b, m_norm1_g, m_q_norm_g, m_k_norm_g, m_sinks, m_a_log, m_dt_bias, m_dn_norm_g, m_norm2_g, m_conv_w]
    small_v = [v_ada_b, v_norm1_g, v_q_norm_g, v_k_norm_g, v_sinks, v_a_log, v_dt_bias, v_dn_norm_g, v_norm2_g, v_conv_w]
    cat = lambda arrs: jnp.concatenate([a.reshape(1, -1) for a in arrs], axis=1)
    res = _adamw(cat(small_w), cat(small_g), cat(small_m), cat(small_v), "adamw_small")
    po = np.cumsum([0] + [int(np.prod(w.shape)) for w in small_w])
    for i, nm in enumerate(small_names):
        upd[nm] = tuple(r[:, po[i]:po[i + 1]].reshape(small_w[i].shape) for r in res)
        grads[nm] = small_g[i].reshape(small_w[i].shape)

    order = ["ada_w", "ada_b", "norm1_g", "w_in", "conv_w", "q_norm_g", "k_norm_g", "sinks", "a_log", "dt_bias",
             "dn_norm_g", "w_branch", "w_out", "norm2_g", "w_gate_up", "w_down"]
    return (loss, grad_x, *[grads[n] for n in order], *[upd[n][0] for n in order],
            *[upd[n][1] for n in order], *[upd[n][2] for n in order])
```

```python
import functools

import numpy as np
import jax
import jax.numpy as jnp
from jax import lax
from jax.experimental import pallas as pl
from jax.experimental.pallas import tpu as pltpu

F32 = jnp.float32
BF16 = jnp.bfloat16
HI = lax.Precision.HIGHEST

N_DEV = 8
D = 1024
HQ, HKV, HD = 8, 2, 64
GRP = HQ // HKV
BLK = 128
ROT = HD // 4
THETA = 500000.0
QW, KVW = HQ * HD, HKV * HD
DH, DK = 4, 128
CH = 64
DNW = DH * DK
CONV = 4
CONVW = 3 * DNW
FFN = 2816
EPS = 1e-6
IN_W = QW + 2 * KVW + CONVW + 2 * DH + DNW + 2 * D

LR, B1, B2, AEPS, WD, STEP = 0.001, 0.9, 0.999, 1e-08, 0.01, 10

VMEM_LIMIT = 56 * 1024 * 1024
MESH = pl.DeviceIdType.MESH


def _cparams(sem=None, vmem=VMEM_LIMIT):
    return pltpu.CompilerParams(dimension_semantics=sem, vmem_limit_bytes=vmem)


def _full(shape):
    n = len(shape)
    return pl.BlockSpec(shape, lambda *_: (0,) * n)


def _resident(shape):
    n = len(shape)
    return pl.BlockSpec(shape, lambda *_: (0,) * n, pipeline_mode=pl.Buffered(1))


def _rows(tm, w):
    return pl.BlockSpec((None, tm, w), lambda b, i: (b, i, 0))


def _stacked(n, tm, w):
    return pl.BlockSpec((None, n, tm, w), lambda b, i: (b, 0, i, 0))


def _perb(r, w):
    return pl.BlockSpec((None, r, w), lambda b, i: (b, 0, 0))


def _dot(a, b):
    return jnp.dot(a.astype(BF16), b.astype(BF16), preferred_element_type=F32)


def _dot_nt(a, b):
    return lax.dot_general(a.astype(BF16), b.astype(BF16), (((1,), (1,)), ((), ())), preferred_element_type=F32)


def _dot_tn(a, b):
    return lax.dot_general(a.astype(BF16), b.astype(BF16), (((0,), (0,)), ((), ())), preferred_element_type=F32)


def _dot_hi(a, b):
    return jnp.dot(a, b, preferred_element_type=F32, precision=HI)


def _sigmoid(x):
    return jax.nn.sigmoid(x)


def _silu(x):
    return x * jax.nn.sigmoid(x)


def _rms_mod(x, g, scale, shift):
    r = lax.rsqrt(jnp.mean(x * x, axis=-1, keepdims=True) + EPS)
    return (x * r * g) * (1.0 + scale) + shift


def _tile(S, rows=256):
    return min(rows, S)


def _peer(x, y, c, k):
    px = 1 - x if (k >> 2) & 1 else x
    py = 1 - y if (k >> 1) & 1 else y
    pc = 1 - c if k & 1 else c
    return px, py, pc


def _all_gather_small(v, name):
    r, n = v.shape

    def body(v_ref, out_ref, send_sems, recv_sems, local_sem):
        x, y, c = lax.axis_index("x"), lax.axis_index("y"), lax.axis_index("c")
        me = 4 * x + 2 * y + c
        mine = pltpu.make_async_copy(v_ref, out_ref.at[me], local_sem)
        mine.start()
        sends = []
        for k in range(1, N_DEV):
            cp = pltpu.make_async_remote_copy(
                src_ref=v_ref, dst_ref=out_ref.at[me], send_sem=send_sems.at[k - 1], recv_sem=recv_sems.at[k - 1],
                device_id=_peer(x, y, c, k), device_id_type=MESH)
            cp.start()
            sends.append(cp)
        for k in range(1, N_DEV):
            px, py, pc = _peer(x, y, c, k)
            pltpu.make_async_remote_copy(
                src_ref=v_ref, dst_ref=out_ref.at[4 * px + 2 * py + pc], send_sem=send_sems.at[k - 1],
                recv_sem=recv_sems.at[k - 1], device_id=(px, py, pc), device_id_type=MESH).wait_recv()
        for cp in sends:
            cp.wait_send()
        mine.wait()

    return pl.pallas_call(
        body, name=name,
        out_shape=jax.ShapeDtypeStruct((N_DEV, r, n), v.dtype),
        in_specs=[pl.BlockSpec(memory_space=pltpu.VMEM)],
        out_specs=pl.BlockSpec(memory_space=pltpu.VMEM),
        scratch_shapes=[pltpu.SemaphoreType.DMA((N_DEV - 1,)), pltpu.SemaphoreType.DMA((N_DEV - 1,)), pltpu.SemaphoreType.DMA],
    )(v)


def _all_gather_big(vs, name, after=(), side=None):
    na, nf = len(vs), len(after)
    side_fn, side_in, side_out = side if side is not None else (None, (), ())
    ns, no = len(side_in), len(side_out)

    def body(*refs):
        v_refs, out_refs = refs[:na], refs[na + nf + ns:2 * na + nf + ns]
        send_sems, recv_sems, local_sems = refs[2 * na + nf + ns + no:]
        x, y, c = lax.axis_index("x"), lax.axis_index("y"), lax.axis_index("c")
        me, sibling = (x, y, c), (x, y, 1 - c)
        chips = [(1 - x, y), (x, 1 - y), (1 - x, 1 - y)]

        def rows(a, px, py, pc):
            return out_refs[a].at[4 * px + 2 * py + pc]

        def copy(a, k, block, to, src=None):
            return pltpu.make_async_remote_copy(
                src_ref=rows(a, *block) if src is None else src, dst_ref=rows(a, *block),
                send_sem=send_sems.at[7 * a + k], recv_sem=recv_sems.at[7 * a + k], device_id=to, device_id_type=MESH)

        mine = [pltpu.make_async_copy(v_refs[a], rows(a, *me), local_sems.at[a]) for a in range(na)]
        for cp in mine:
            cp.start()
        first = []
        for a in range(na):
            first.append(copy(a, 0, me, sibling, src=v_refs[a]))
            first += [copy(a, 1 + j, me, (*chip, c), src=v_refs[a]) for j, chip in enumerate(chips)]
        for cp in first:
            cp.start()
        if side_fn is not None:
            side_fn(refs[na + nf:na + nf + ns], refs[2 * na + nf + ns:2 * na + nf + ns + no])
        passed = []
        for j, chip in enumerate(chips):
            for a in range(na):
                copy(a, 1 + j, (*chip, c), me).wait_recv()
                forward = copy(a, 4 + j, (*chip, c), sibling)
                forward.start()
                passed.append(forward)
        for a in range(na):
            copy(a, 0, sibling, me).wait_recv()
            for j, chip in enumerate(chips):
                copy(a, 4 + j, (*chip, 1 - c), me).wait_recv()
        for cp in first + passed:
            cp.wait_send()
        for cp in mine:
            cp.wait()

    return pl.pallas_call(
        body, name=name,
        out_shape=[jax.ShapeDtypeStruct((N_DEV,) + v.shape, v.dtype) for v in vs] + list(side_out),
        in_specs=[pl.BlockSpec(memory_space=pl.ANY)] * (na + nf) + [pl.BlockSpec(memory_space=pltpu.VMEM)] * ns,
        out_specs=[pl.BlockSpec(memory_space=pl.ANY)] * na + [pl.BlockSpec(memory_space=pltpu.VMEM)] * no,
        scratch_shapes=[pltpu.SemaphoreType.DMA((7 * na,)), pltpu.SemaphoreType.DMA((7 * na,)),
                        pltpu.SemaphoreType.DMA((na,))],
        compiler_params=pltpu.CompilerParams(vmem_limit_bytes=VMEM_LIMIT),
    )(*vs, *after, *side_in)


_HBM = pl.BlockSpec(memory_space=pltpu.HBM)
_SEM = pl.BlockSpec(memory_space=pltpu.SEMAPHORE)
_EFFECT = pltpu.SideEffectType.DATAFLOW_SIDE_EFFECTING


def _place_own(block, me):
    land = lax.empty((N_DEV,) + block.shape, block.dtype)
    return lax.dynamic_update_slice(land, block[None], (me,) + (0,) * block.ndim)


def _copies_start(srcs, lands, scatter, after, name):
    na = len(srcs)
    afters = tuple(after) if isinstance(after, (tuple, list)) else (after,)

    def body(*refs):
        src_refs, land_refs = refs[:na], refs[na:2 * na]
        sems = refs[2 * na + len(afters):4 * na + len(afters)]
        token = refs[-1]
        x, y, c = lax.axis_index("x"), lax.axis_index("y"), lax.axis_index("c")
        me = 4 * x + 2 * y + c
        for a in range(na):
            for k in range(1, N_DEV):
                px, py, pc = _peer(x, y, c, k)
                src = src_refs[a].at[4 * px + 2 * py + pc] if scatter else src_refs[a]
                pltpu.make_async_remote_copy(
                    src_ref=src, dst_ref=land_refs[a].at[me], send_sem=sems[2 * a], recv_sem=sems[2 * a + 1],
                    device_id=(px, py, pc), device_id_type=MESH).start()
        token[...] = jnp.zeros_like(token)

    hbm = lambda t: pltpu.HBM(t.shape, t.dtype)
    out = pl.pallas_call(
        body, name=name,
        out_shape=tuple([pltpu.SemaphoreType.DMA(())] * (2 * na) + [hbm(t) for t in srcs] + [hbm(t) for t in lands]
                        + [jax.ShapeDtypeStruct((8, 128), F32)]),
        in_specs=[_HBM] * (2 * na) + [pl.BlockSpec(memory_space=pl.ANY)] * len(afters),
        out_specs=tuple([_SEM] * (2 * na) + [_HBM] * (2 * na) + [pl.BlockSpec(memory_space=pltpu.VMEM)]),
        input_output_aliases={i: 2 * na + i for i in range(2 * na)},
        compiler_params=pltpu.CompilerParams(has_side_effects=_EFFECT),
    )(*[pltpu.with_memory_space_constraint(t, pltpu.HBM) for t in list(srcs) + list(lands)], *afters)
    return out[:2 * na], out[2 * na:3 * na], out[3 * na:4 * na], out[-1]


def _exchange_start(gs, me, after, name):
    own = [lax.dynamic_index_in_dim(g, me, 0, keepdims=False) for g in gs]
    return _copies_start(gs, [_place_own(o, me) for o in own], True, after, name)


def _copies_wait(sems, srcs, lands, after, name):
    na = len(srcs)

    def body(*refs):
        land_refs = refs[na:2 * na]
        sem_refs = refs[2 * na:4 * na]
        x, y, c = lax.axis_index("x"), lax.axis_index("y"), lax.axis_index("c")
        for a in range(na):
            seven = land_refs[a].at[pl.ds(0, N_DEV - 1)]
            copy = pltpu.make_async_remote_copy(
                src_ref=seven, dst_ref=seven, send_sem=sem_refs[2 * a], recv_sem=sem_refs[2 * a + 1],
                device_id=(x, y, c), device_id_type=MESH)
            copy.wait_send()
            copy.wait_recv()

    hbm = lambda t: pltpu.HBM(t.shape, t.dtype)
    out = pl.pallas_call(
        body, name=name,
        out_shape=tuple([hbm(t) for t in srcs] + [hbm(t) for t in lands]),
        in_specs=[_HBM] * (2 * na) + [_SEM] * (2 * na) + [pl.BlockSpec(memory_space=pl.ANY)],
        out_specs=tuple([_HBM] * (2 * na)),
        input_output_aliases={i: i for i in range(2 * na)},
        compiler_params=pltpu.CompilerParams(has_side_effects=_EFFECT),
    )(*srcs, *lands, *sems, after)
    return out[na:]


def _adamw_math(w, g, m, v):
    m = B1 * m + (1.0 - B1) * g
    v = B2 * v + (1.0 - B2) * (g * g)
    m_hat = m / (1.0 - B1 ** STEP)
    v_hat = v / (1.0 - B2 ** STEP)
    return -LR * (m_hat / (jnp.sqrt(v_hat) + AEPS) + WD * w), m, v


def _sum_adamw(parts, w, m, v, name):
    parts = list(parts) if isinstance(parts, (list, tuple)) else [parts]
    r, n = w.shape[1], w.shape[2]
    tr = 256 if r % 256 == 0 else r
    npart = len(parts)

    def body(*refs):
        p_refs = refs[:npart]
        w_ref, m_ref, v_ref, g_ref, d_ref, nm_ref, nv_ref = refs[npart:]
        pieces = []
        for p_ref in p_refs:
            g = p_ref[0].astype(F32)
            for dev in range(1, N_DEV):
                g = g + p_ref[dev].astype(F32)
            pieces.append(g)
        g = pieces[0] if npart == 1 else jnp.concatenate(pieces, axis=1)
        g_ref[...] = g
        d_ref[...], nm_ref[...], nv_ref[...] = _adamw_math(w_ref[...], g, m_ref[...], v_ref[...])

    rows = pl.BlockSpec((None, tr, n), lambda i: (0, i, 0))
    sd = jax.ShapeDtypeStruct((1, r, n), F32)
    return pl.pallas_call(
        body, name=name, grid=(r // tr,), out_shape=(sd, sd, sd, sd),
        in_specs=[pl.BlockSpec((N_DEV, tr, p.shape[2]), lambda i: (0, i, 0)) for p in parts] + [rows, rows, rows],
        out_specs=(rows, rows, rows, rows),
        compiler_params=_cparams(("parallel",)),
    )(*parts, w, m, v)


def _ada_fwd(c_all, ada_w, ada_b_cols):
    nb, ncol = c_all.shape[0], ada_w.shape[1]

    def body(c_ref, w_ref, b_ref, mod_ref, cond_ref):
        cond = _silu(c_ref[...])
        cond_ref[...] = cond
        mod_ref[...] = _dot_hi(cond, w_ref[...]) + b_ref[...]

    return pl.pallas_call(
        body, name="ada_fwd",
        out_shape=(jax.ShapeDtypeStruct((nb, ncol), F32), jax.ShapeDtypeStruct((nb, D), F32)),
        compiler_params=_cparams(),
    )(c_all, ada_w, ada_b_cols)


def _ada_bwd(cond_all, dmod_all, dmod_cols, smalls):
    ncol, nsm = dmod_cols.shape[1], smalls.shape[1]

    def body(cond_ref, dm_ref, dmc_ref, sm_ref, gw_ref, gb_ref, gs_ref):
        gw_ref[...] = lax.dot_general(cond_ref[...], dmc_ref[...], (((0,), (0,)), ((), ())),
                                      preferred_element_type=F32, precision=HI)
        gb_ref[...] = jnp.sum(dm_ref[...], axis=0, keepdims=True)
        gs_ref[...] = jnp.sum(sm_ref[...], axis=0, keepdims=True)

    return pl.pallas_call(
        body, name="ada_bwd",
        out_shape=(jax.ShapeDtypeStruct((D, ncol), F32), jax.ShapeDtypeStruct((1, 6 * D), F32),
                   jax.ShapeDtypeStruct((1, nsm), F32)),
        compiler_params=_cparams(),
    )(cond_all, dmod_all, dmod_cols, smalls)


IN_CUTS = (0, QW, QW + 2 * KVW, QW + 2 * KVW + CONVW, QW + 2 * KVW + CONVW + 2 * DH,
           QW + 2 * KVW + CONVW + 2 * DH + DNW, QW + 2 * KVW + CONVW + 2 * DH + DNW + D, IN_W)
IN_WIDTHS = tuple(b - a for a, b in zip(IN_CUTS[:-1], IN_CUTS[1:]))
IN_SHARD = IN_W // N_DEV


def _inproj_fwd(x, mod, g1, w_t):
    B, S, _ = x.shape
    tm = _tile(S, 512)

    def body(x_ref, mod_ref, g_ref, w_ref, h_ref, *o_refs):
        h = _rms_mod(x_ref[...], g_ref[...], mod_ref[1:2, :], mod_ref[0:1, :]).astype(BF16)
        h_ref[...] = h
        full = _dot_nt(h, w_ref[...])
        for o_ref, lo, hi in zip(o_refs, IN_CUTS[:-1], IN_CUTS[1:]):
            o_ref[...] = full[:, lo:hi]

    return pl.pallas_call(
        body, name="inproj_fwd", grid=(B, S // tm),
        out_shape=[jax.ShapeDtypeStruct((B, S, D), BF16)] + [jax.ShapeDtypeStruct((B, S, w), F32) for w in IN_WIDTHS],
        in_specs=[_rows(tm, D), _perb(6, D), _full((1, D)), _resident(w_t.shape)],
        out_specs=[_rows(tm, D)] + [_rows(tm, w) for w in IN_WIDTHS],
        compiler_params=_cparams(("parallel", "arbitrary")),
    )(x, mod, g1, w_t)


def _inproj_bwd(x, mod, g1, dx1, dps, w_t):
    B, S, _ = x.shape
    tm = _tile(S, 512)
    n = len(dps)

    def body(x_ref, mod_ref, g_ref, dx1_ref, *refs):
        dp_refs, w_ref = refs[:n], refs[n]
        dblk_ref, gx_ref, dg_ref, dsc_ref, dsh_ref = refs[n + 1:]
        b, i = pl.program_id(0), pl.program_id(1)
        full = jnp.concatenate([r[...].astype(F32) for r in dp_refs], axis=1)
        for j in range(N_DEV):
            dblk_ref[j] = full[:, IN_SHARD * j:IN_SHARD * (j + 1)].astype(BF16)
        dh = jnp.dot(full.astype(BF16), w_ref[...], preferred_element_type=F32)
        _, vjp = jax.vjp(_rms_mod, x_ref[...], g_ref[...], mod_ref[1:2, :], mod_ref[0:1, :])
        dx, dg, dsc, dsh = vjp(dh)
        gx_ref[...] = dx1_ref[...] + dx

        @pl.when((b == 0) & (i == 0))
        def _():
            dg_ref[...] = jnp.zeros_like(dg_ref)

        @pl.when(i == 0)
        def _():
            dsc_ref[...] = jnp.zeros_like(dsc_ref)
            dsh_ref[...] = jnp.zeros_like(dsh_ref)

        dg_ref[...] += dg
        dsc_ref[...] += dsc
        dsh_ref[...] += dsh

    return pl.pallas_call(
        body, name="inproj_bwd", grid=(B, S // tm),
        out_shape=[jax.ShapeDtypeStruct((B, N_DEV, S, IN_SHARD), BF16), jax.ShapeDtypeStruct((B, S, D), F32),
                   jax.ShapeDtypeStruct((1, D), F32), jax.ShapeDtypeStruct((B, 1, D), F32),
                   jax.ShapeDtypeStruct((B, 1, D), F32)],
        in_specs=[_rows(tm, D), _perb(6, D), _full((1, D)), _rows(tm, D)]
                 + [_rows(tm, w) for w in IN_WIDTHS] + [_resident(w_t.shape)],
        out_specs=[pl.BlockSpec((None, N_DEV, tm, IN_SHARD), lambda b, i: (b, 0, i, 0)), _rows(tm, D),
                   _full((1, D)), _perb(1, D), _perb(1, D)],
        compiler_params=_cparams(("arbitrary", "arbitrary")),
    )(x, mod, g1, dx1, *dps, w_t)


def _wgrad(a, b, name, after=None, b_lanes=None):
    B, na, S, K = a.shape
    nb, N = b.shape[1], b.shape[3]
    lane_blk = 0
    if b_lanes is not None:
        lane_blk, N = b_lanes
    G = max(na, nb)
    tm = min(4096, S)
    nt = S // tm
    last = B * nt - 1

    def body(a_ref, b_ref, *rest):
        o_ref, acc = rest[-2:]
        t = pl.program_id(1)

        @pl.when(t == 0)
        def _():
            acc[...] = jnp.zeros_like(acc)

        acc[...] += lax.dot_general(a_ref[...], b_ref[...], (((0,), (0,)), ((), ())), preferred_element_type=F32)

        @pl.when(t == last)
        def _():
            o_ref[...] = acc[...].astype(BF16)

    return pl.pallas_call(
        body, name=name, grid=(G, B * nt),
        out_shape=jax.ShapeDtypeStruct((G, K, N), BF16),
        in_specs=[pl.BlockSpec((None, None, tm, K), lambda g, t: (t // nt, g if na > 1 else 0, t % nt, 0)),
                  pl.BlockSpec((None, None, tm, N), lambda g, t: (t // nt, g if nb > 1 else 0, t % nt, lane_blk))]
                 + ([] if after is None else [pl.BlockSpec(memory_space=pl.ANY)]),
        out_specs=pl.BlockSpec((None, K, N), lambda g, t: (g, 0, 0)),
        scratch_shapes=[pltpu.VMEM((K, N), F32)],
        compiler_params=_cparams(("parallel", "arbitrary")),
    )(*((a, b) if after is None else (a, b, after)))


LANES = 128


def _attn_consts():
    inv_freq = THETA ** (-jnp.arange(0, ROT, 2, dtype=F32) / ROT)
    head = jnp.concatenate([inv_freq, inv_freq, jnp.zeros((HD - ROT,), F32)])
    invf = jnp.tile(head, LANES // HD)[None, :]
    mean_of = lambda w: jnp.asarray(np.kron(np.eye(w // HD), np.full((HD, HD), 1.0 / HD)), BF16)
    return invf, mean_of(QW), mean_of(KVW)


def _rope_tables_side(pos, invf):
    B, S, _ = pos.shape
    tr = min(512, S)

    def fn(ins, outs):
        p_ref, f_ref = ins
        c_ref, s_ref = outs
        for b in range(B):
            for r in range(0, S, tr):
                ang = p_ref[b, r:r + tr, :].astype(F32) * f_ref[...]
                c_ref[b, r:r + tr, :] = jnp.cos(ang)
                s_ref[b, r:r + tr, :] = jnp.sin(ang)

    sd = jax.ShapeDtypeStruct((B, S, LANES), F32)
    return fn, (pos, invf), (sd, sd)


def _rope_expand(cos, sin, reps):
    lane = lax.broadcasted_iota(jnp.int32, cos.shape, 1) % HD
    sa = jnp.where((lane >= ROT // 2) & (lane < ROT), sin, 0.0)
    sb = jnp.where(lane < ROT // 2, -sin, 0.0)
    rep = lambda t: jnp.concatenate([t] * reps, axis=1) if reps > 1 else t
    return rep(cos), rep(sa), rep(sb)


@jax.custom_vjp
def _rope(t, cos, sa, sb):
    w = t.shape[1]
    return t * cos + pltpu.roll(t, ROT // 2, 1) * sa + pltpu.roll(t, w - ROT // 2, 1) * sb


def _rope_fwd(t, cos, sa, sb):
    return _rope(t, cos, sa, sb), (cos, sa, sb)


def _rope_bwd(res, d):
    cos, sa, sb = res
    w = d.shape[1]
    dt = d * cos + pltpu.roll(d * sa, w - ROT // 2, 1) + pltpu.roll(d * sb, ROT // 2, 1)
    return dt, jnp.zeros_like(cos), jnp.zeros_like(sa), jnp.zeros_like(sb)


_rope.defvjp(_rope_fwd, _rope_bwd)


def _head_norm(t, g, mean_of):
    hi, lo = _split(t * t)
    ms = jnp.dot(hi, mean_of, preferred_element_type=F32) + jnp.dot(lo, mean_of, preferred_element_type=F32)
    return t * lax.rsqrt(ms + EPS) * g


def _attn_block(q, kvp, kvc, qg, kg, sinks, tq, tk, mq, mk, valid):
    qn = _rope(_head_norm(q, jnp.concatenate([qg] * HQ, axis=1), mq), *tq) * (HD ** -0.5)
    kv = jnp.concatenate([kvp, kvc], axis=0)
    kn = _rope(_head_norm(kv[:, 0:KVW], jnp.concatenate([kg] * HKV, axis=1), mk), *tk)
    per_tile = LANES // HD
    vT = jnp.transpose(kv[:, KVW:2 * KVW])
    qT = [jnp.transpose(qn[:, LANES * t:LANES * (t + 1)]) for t in range(QW // LANES)]
    head_T = lambda h: qT[h // per_tile][HD * (h % per_tile):HD * (h % per_tile + 1), :]
    none = jnp.zeros((HD, GRP * BLK), F32)
    o_T = []
    for j in range(HKV):
        q4T = jnp.concatenate([head_T(GRP * j + i) for i in range(GRP)], axis=1)
        sT = _dot(kn, jnp.concatenate([q4T, none] if j == 0 else [none, q4T], axis=0))
        sT = jnp.where(valid, sT, -1e30)
        sink = jnp.concatenate([jnp.broadcast_to(sinks[:, GRP * j + i:GRP * j + i + 1], (1, BLK)) for i in range(GRP)], axis=1)
        m = lax.stop_gradient(jnp.maximum(jnp.max(sT, axis=0, keepdims=True), sink))
        pT = jnp.exp(sT - m)
        den = jnp.sum(pT, axis=0, keepdims=True) + jnp.exp(sink - m)
        oT = _dot(vT[HD * j:HD * (j + 1), :], pT) * (1.0 / den)
        o_T += [oT[:, BLK * i:BLK * (i + 1)] for i in range(GRP)]
    return jnp.concatenate([jnp.transpose(jnp.concatenate(o_T[per_tile * t:per_tile * (t + 1)], axis=0))
                            for t in range(QW // LANES)], axis=1)


def _attn_tables(cp_ref, cc_ref, sp_ref, sc_ref, n):
    tq = _rope_expand(cc_ref[...], sc_ref[...], QW // LANES)
    tk = _rope_expand(jnp.concatenate([cp_ref[...], cc_ref[...]], axis=0),
                      jnp.concatenate([sp_ref[...], sc_ref[...]], axis=0), KVW // LANES)
    qi = lax.broadcasted_iota(jnp.int32, (2 * BLK, GRP * BLK), 1) % BLK + BLK
    kj = lax.broadcasted_iota(jnp.int32, (2 * BLK, GRP * BLK), 0)
    dist = qi - kj
    valid = (dist >= 0) & (dist < BLK) & ((kj >= BLK) | (n > 0))
    return tq, tk, valid


def _attn_fwd(aq, akv, cos, sin, qg, kg, sinks, mq, mk):
    B, S, _ = aq.shape
    nb = S // BLK

    def body(q_ref, kvp_ref, kvc_ref, cp_ref, cc_ref, sp_ref, sc_ref, qg_ref, kg_ref, sk_ref, mq_ref, mk_ref, o_ref):
        tq, tk, valid = _attn_tables(cp_ref, cc_ref, sp_ref, sc_ref, pl.program_id(1))
        o_ref[...] = _attn_block(q_ref[...], kvp_ref[...], kvc_ref[...], qg_ref[...], kg_ref[...], sk_ref[...],
                                 tq, tk, mq_ref[...], mk_ref[...], valid)

    prev = lambda b, n: (b, jnp.maximum(n - 1, 0), 0)
    cur = lambda b, n: (b, n, 0)
    return pl.pallas_call(
        body, name="attn_fwd", grid=(B, nb),
        out_shape=jax.ShapeDtypeStruct((B, S, QW), F32),
        in_specs=[pl.BlockSpec((None, BLK, QW), cur), pl.BlockSpec((None, BLK, 2 * KVW), prev),
                  pl.BlockSpec((None, BLK, 2 * KVW), cur), pl.BlockSpec((None, BLK, LANES), prev),
                  pl.BlockSpec((None, BLK, LANES), cur), pl.BlockSpec((None, BLK, LANES), prev),
                  pl.BlockSpec((None, BLK, LANES), cur), _full((1, HD)), _full((1, HD)), _full((1, HQ)),
                  _full((QW, QW)), _full((KVW, KVW))],
        out_specs=pl.BlockSpec((None, BLK, QW), cur),
        compiler_params=_cparams(("parallel", "arbitrary")),
    )(aq, akv, akv, cos, cos, sin, sin, qg, kg, sinks, mq, mk)


def _attn_bwd(aq, akv, cos, sin, qg, kg, sinks, mq, mk, do):
    B, S, _ = aq.shape
    nb = S // BLK

    def body(q_ref, kvp_ref, kvc_ref, cp_ref, cc_ref, sp_ref, sc_ref, qg_ref, kg_ref, sk_ref, mq_ref, mk_ref, do_ref,
             dq_ref, dkv_ref, dqg_ref, dkg_ref, dsk_ref, carry):
        b, i = pl.program_id(0), pl.program_id(1)
        tq, tk, valid = _attn_tables(cp_ref, cc_ref, sp_ref, sc_ref, nb - 1 - i)
        fn = functools.partial(_attn_block, tq=tq, tk=tk, mq=mq_ref[...], mk=mk_ref[...], valid=valid)
        _, vjp = jax.vjp(fn, q_ref[...], kvp_ref[...], kvc_ref[...], qg_ref[...], kg_ref[...], sk_ref[...])
        dq, dkvp, dkvc, dqg, dkg, dsk = vjp(do_ref[...])

        @pl.when(i == 0)
        def _():
            carry[...] = jnp.zeros_like(carry)

        @pl.when((b == 0) & (i == 0))
        def _():
            dqg_ref[...] = jnp.zeros_like(dqg_ref)
            dkg_ref[...] = jnp.zeros_like(dkg_ref)
            dsk_ref[...] = jnp.zeros_like(dsk_ref)

        dq_ref[...] = dq.astype(BF16)
        dkv_ref[...] = (dkvc + carry[...]).astype(BF16)
        carry[...] = dkvp
        dqg_ref[...] += dqg
        dkg_ref[...] += dkg
        dsk_ref[...] += dsk

    prev = lambda b, i: (b, jnp.maximum(nb - 2 - i, 0), 0)
    cur = lambda b, i: (b, nb - 1 - i, 0)
    return pl.pallas_call(
        body, name="attn_bwd", grid=(B, nb),
        out_shape=[jax.ShapeDtypeStruct((B, S, QW), BF16), jax.ShapeDtypeStruct((B, S, 2 * KVW), BF16),
                   jax.ShapeDtypeStruct((1, HD), F32), jax.ShapeDtypeStruct((1, HD), F32),
                   jax.ShapeDtypeStruct((1, HQ), F32)],
        in_specs=[pl.BlockSpec((None, BLK, QW), cur), pl.BlockSpec((None, BLK, 2 * KVW), prev),
                  pl.BlockSpec((None, BLK, 2 * KVW), cur), pl.BlockSpec((None, BLK, LANES), prev),
                  pl.BlockSpec((None, BLK, LANES), cur), pl.BlockSpec((None, BLK, LANES), prev),
                  pl.BlockSpec((None, BLK, LANES), cur), _full((1, HD)), _full((1, HD)), _full((1, HQ)),
                  _full((QW, QW)), _full((KVW, KVW)), pl.BlockSpec((None, BLK, QW), cur)],
        out_specs=[pl.BlockSpec((None, BLK, QW), cur), pl.BlockSpec((None, BLK, 2 * KVW), cur),
                   _full((1, HD)), _full((1, HD)), _full((1, HQ))],
        scratch_shapes=[pltpu.VMEM((BLK, 2 * KVW), F32)],
        compiler_params=_cparams(("arbitrary", "arbitrary")),
    )(aq, akv, akv, cos, cos, sin, sin, qg, kg, sinks, mq, mk, do)


def _conv_taps(xe, w, rows):
    y = None
    for j in range(CONV):
        sh = pltpu.roll(xe, CONV - 1 - j, 0)[8:8 + rows, :] if j < CONV - 1 else xe[8:8 + rows, :]
        y = sh * w[j:j + 1, :] if y is None else y + sh * w[j:j + 1, :]
    return y


def _softplus(x):
    return jnp.maximum(x, 0.0) + jnp.log1p(jnp.exp(-jnp.abs(x)))


_BMM = (((2,), (1,)), ((0,), (0,)))
_BMM_NT = (((2,), (2,)), ((0,), (0,)))
_BMM_TN = (((1,), (1,)), ((0,), (0,)))


def _bmm(a, b, dims=_BMM):
    return lax.dot_general(a.astype(BF16), b.astype(BF16), dims, preferred_element_type=F32)


def _split(a):
    hi = a.astype(BF16)
    return hi, (a - hi.astype(F32)).astype(BF16)


def _bmm3(a, b, dims=_BMM):
    ah, al = _split(a)
    bh, bl = _split(b)
    d = lambda p, q: lax.dot_general(p, q, dims, preferred_element_type=F32)
    return d(ah, bh) + (d(ah, bl) + d(al, bh))


TRI_BASE = 8


def _tri_inverse(L):
    ii = lax.broadcasted_iota(jnp.int32, (CH, CH), 0)
    jj = lax.broadcasted_iota(jnp.int32, (CH, CH), 1)
    same = lambda size: (ii // size) == (jj // size)
    diag = jnp.where(same(TRI_BASE), L, 0.0)
    X = (ii == jj).astype(F32) - diag
    P = diag
    n = 2
    while n < TRI_BASE:
        P = _bmm3(P, P)
        X = X + _bmm3(X, P)
        n *= 2
    size = TRI_BASE
    while size < CH:
        joint = jnp.where(same(2 * size) & jnp.logical_not(same(size)), L, 0.0)
        X = X - _bmm3(X, _bmm3(joint, X))
        size *= 2
    return X


@jax.custom_vjp
def _tri_inverse_known(L, T):
    return T


def _tri_inverse_known_fwd(L, T):
    return T, T


def _tri_inverse_known_bwd(T, dT):
    Tt = jnp.swapaxes(T, 1, 2)
    return -_bmm(Tt, _bmm(dT, Tt)), jnp.zeros_like(T)


_tri_inverse_known.defvjp(_tri_inverse_known_fwd, _tri_inverse_known_bwd)


def _triangle(n, upper):
    ii = lax.broadcasted_iota(jnp.int32, (n, CH, CH), 1)
    jj = lax.broadcasted_iota(jnp.int32, (n, CH, CH), 2)
    return ((ii <= jj) if upper else (ii >= jj)).astype(BF16)


@jax.custom_vjp
def _cumsum_rows(g):
    g0 = g.astype(BF16)
    r1 = g - g0.astype(F32)
    g1 = r1.astype(BF16)
    g2 = (r1 - g1.astype(F32)).astype(BF16)
    tri = _triangle(g.shape[0], False)
    d = lambda q: lax.dot_general(tri, q, _BMM, preferred_element_type=F32)
    return d(g0) + (d(g1) + d(g2))


def _cumsum_rows_fwd(g):
    return _cumsum_rows(g), None


def _cumsum_rows_bwd(_, dy):
    hi, lo = _split(dy)
    tri = _triangle(dy.shape[0], True)
    d = lambda q: lax.dot_general(tri, q, _BMM, preferred_element_type=F32)
    return (d(hi) + d(lo),)


_cumsum_rows.defvjp(_cumsum_rows_fwd, _cumsum_rows_bwd)


def _row_sums(t):
    n, r, w = t.shape
    hi, lo = _split(t.reshape(n * r, w))
    ones = jnp.ones((w, w), BF16)
    s = jnp.dot(hi, ones, preferred_element_type=F32) + jnp.dot(lo, ones, preferred_element_type=F32)
    return s.reshape(n, r, w)


def _dn_prep(t_known, qr, kr, v, a_raw, b_raw, a_log, dt_b):
    n = qr.shape[0]
    ii = lax.broadcasted_iota(jnp.int32, (n, CH, CH), 1)
    jj = lax.broadcasted_iota(jnp.int32, (n, CH, CH), 2)
    incl, strict = ii >= jj, ii > jj
    q = qr * lax.rsqrt(_row_sums(qr * qr) + EPS) * (DK ** -0.5)
    k = kr * lax.rsqrt(_row_sums(kr * kr) + EPS)
    beta = _sigmoid(b_raw)
    g = -jnp.exp(a_log) * _softplus(a_raw + dt_b)
    gcb = _cumsum_rows(jnp.broadcast_to(g, (n, CH, DK)))
    gc = gcb[:, :, 0:1]
    gc_row = jnp.swapaxes(gcb, 1, 2)[:, 0:1, 0:CH]
    decay = jnp.where(incl, jnp.exp(jnp.where(incl, gc - gc_row, 0.0)), 0.0)
    kb = k * beta
    L = jnp.where(strict, _bmm(kb, k, _BMM_NT) * decay, 0.0)
    T = _tri_inverse(L) if t_known is None else _tri_inverse_known(L, t_known)
    eg = jnp.exp(gc)
    u = _bmm(T, v * beta)
    w = _bmm(T, kb * eg)
    a_in = _bmm(q, k, _BMM_NT) * decay
    g_last = gc[:, CH - 1:CH, :]
    return u, w, q * eg, k * jnp.exp(g_last - gc), a_in, jnp.exp(g_last), T


def _dn_step(S0, u, w, qd, kd, a_in, cd):
    r = _bmm(jnp.concatenate([w, qd], axis=1), S0)
    v_new = u - r[:, 0:CH, :]
    o = r[:, CH:2 * CH, :] + _bmm(a_in, v_new)
    S1 = S0 * cd + _bmm(kd, v_new, _BMM_TN)
    return o, S1


def _dn_stack(cq, ba, al, dt, G):
    cols = [[] for _ in range(7)]
    for c in range(G):
        rows = slice(CH * c, CH * (c + 1))
        for h in range(DH):
            parts = (cq[rows, DK * h:DK * (h + 1)], cq[rows, DNW + DK * h:DNW + DK * (h + 1)],
                     cq[rows, 2 * DNW + DK * h:2 * DNW + DK * (h + 1)], ba[rows, DH + h:DH + h + 1],
                     ba[rows, h:h + 1], al[:, h:h + 1], dt[:, h:h + 1])
            for col, p in zip(cols, parts):
                col.append(p)
    return tuple(jnp.stack(col) for col in cols)


def _dn_group(S, want):
    g = want
    while (S // CH) % g:
        g //= 2
    return g


def _dn_prep_fwd(xin, conv_w, ba, a_log, dt_b):
    B, S, _ = xin.shape
    nc = S // CH
    G = _dn_group(S, 8)
    r8 = G * CH // 8

    def body(xp_ref, x_ref, cw_ref, ba_ref, al_ref, dt_ref, cq_ref, u_ref, w_ref, qd_ref, kd_ref, a_ref, t_ref, cd_ref):
        xp = jnp.where(pl.program_id(1) > 0, xp_ref[...], 0.0)
        cq = _silu(_conv_taps(jnp.concatenate([xp, x_ref[...]], axis=0), cw_ref[...], G * CH))
        cq_ref[...] = cq
        ops = _dn_stack(cq, ba_ref[...], al_ref[...], dt_ref[...], G)
        u, w, qd, kd, a_in, cd, T = _dn_prep(None, *ops)
        lane4 = lax.broadcasted_iota(jnp.int32, (1, DH), 1)
        for c in range(G):
            rows = slice(CH * c, CH * (c + 1))
            cdrow = jnp.zeros((1, DH), F32)
            for h in range(DH):
                n = DH * c + h
                lanes = slice(DK * h, DK * (h + 1))
                u_ref[rows, lanes] = u[n]
                w_ref[rows, lanes] = w[n]
                qd_ref[rows, lanes] = qd[n]
                kd_ref[rows, lanes] = kd[n]
                a_ref[rows, CH * h:CH * (h + 1)] = a_in[n]
                t_ref[rows, CH * h:CH * (h + 1)] = T[n]
                cdrow = cdrow + jnp.where(lane4 == h, cd[n], 0.0)
            cd_ref[c] = cdrow

    wide = jax.ShapeDtypeStruct((B, S, DNW), F32)
    sq = jax.ShapeDtypeStruct((B, S, DH * CH), F32)
    return pl.pallas_call(
        body, name="dn_prep_fwd", grid=(B, nc // G),
        out_shape=[jax.ShapeDtypeStruct((B, S, CONVW), F32), wide, wide, wide, wide, sq, sq,
                   jax.ShapeDtypeStruct((B, nc, 1, DH), F32)],
        in_specs=[pl.BlockSpec((None, 8, CONVW), lambda b, i: (b, jnp.maximum(i * r8 - 1, 0), 0)),
                  _rows(G * CH, CONVW), _full((CONV, CONVW)), _rows(G * CH, 2 * DH), _full((1, DH)), _full((1, DH))],
        out_specs=[_rows(G * CH, CONVW)] + [_rows(G * CH, DNW)] * 4 + [_rows(G * CH, DH * CH)] * 2
                  + [pl.BlockSpec((None, G, 1, DH), lambda b, i: (b, i, 0, 0))],
        compiler_params=_cparams(("parallel", "arbitrary")),
    )(xin, xin, conv_w, ba, a_log, dt_b)


def _dn_seq_specs(B, steps, gs, rev):
    at = (lambda i: steps - 1 - i) if rev else (lambda i: i)
    wide = pl.BlockSpec((B, gs * CH, DNW), lambda i: (0, at(i), 0))
    a_spec = pl.BlockSpec((B, gs * CH, DH * CH), lambda i: (0, at(i), 0))
    cd_spec = pl.BlockSpec((B, gs, 1, DH), lambda i: (0, at(i), 0, 0))
    st_spec = pl.BlockSpec((B, gs, DH, DK, DK), lambda i: (0, at(i), 0, 0, 0))
    return wide, a_spec, cd_spec, st_spec


def _dn_step_operands(B, c, u_ref, w_ref, qd_ref, kd_ref, a_ref, cd_ref):
    pairs = [(b, h) for b in range(B) for h in range(DH)]
    rows = slice(CH * c, CH * (c + 1))
    wide = lambda ref: jnp.stack([ref[b, rows, DK * h:DK * (h + 1)] for b, h in pairs])
    a_in = jnp.stack([a_ref[b, rows, CH * h:CH * (h + 1)] for b, h in pairs])
    cd = jnp.stack([cd_ref[b, c, :, h:h + 1] for b, h in pairs])
    return wide(u_ref), wide(w_ref), wide(qd_ref), wide(kd_ref), a_in, cd


def _dn_seq_fwd(u, w, qd, kd, a_in, cd):
    B, S, _ = u.shape
    nc = S // CH
    gs = _dn_group(S, 8)

    def body(u_ref, w_ref, qd_ref, kd_ref, a_ref, cd_ref, o_ref, st_ref, state):
        @pl.when(pl.program_id(0) == 0)
        def _():
            state[...] = jnp.zeros_like(state)

        S0 = state[...]
        for c in range(gs):
            for b in range(B):
                st_ref[b, c] = S0[DH * b:DH * (b + 1)]
            o, S0 = _dn_step(S0, *_dn_step_operands(B, c, u_ref, w_ref, qd_ref, kd_ref, a_ref, cd_ref))
            for b in range(B):
                for h in range(DH):
                    o_ref[b, CH * c:CH * (c + 1), DK * h:DK * (h + 1)] = o[DH * b + h]
        state[...] = S0

    wide, a_spec, cd_spec, st_spec = _dn_seq_specs(B, nc // gs, gs, False)
    return pl.pallas_call(
        body, name="dn_seq_fwd", grid=(nc // gs,),
        out_shape=[jax.ShapeDtypeStruct((B, S, DNW), F32), jax.ShapeDtypeStruct((B, nc, DH, DK, DK), F32)],
        in_specs=[wide, wide, wide, wide, a_spec, cd_spec],
        out_specs=[wide, st_spec],
        scratch_shapes=[pltpu.VMEM((B * DH, DK, DK), F32)],
        compiler_params=_cparams(("arbitrary",)),
    )(u, w, qd, kd, a_in, cd)


def _dn_seq_bwd(u, w, qd, kd, a_in, cd, states, do):
    B, S, _ = u.shape
    nc = S // CH
    gs = _dn_group(S, 4)

    def body(u_ref, w_ref, qd_ref, kd_ref, a_ref, cd_ref, st_ref, do_ref,
             du_ref, dw_ref, dqd_ref, dkd_ref, da_ref, dcd_ref, dstate):
        @pl.when(pl.program_id(0) == 0)
        def _():
            dstate[...] = jnp.zeros_like(dstate)

        lane4 = lax.broadcasted_iota(jnp.int32, (1, DH), 1)
        dS = dstate[...]
        for c in reversed(range(gs)):
            rows = slice(CH * c, CH * (c + 1))
            S0 = jnp.concatenate([st_ref[b, c] for b in range(B)], axis=0)
            do = jnp.stack([do_ref[b, rows, DK * h:DK * (h + 1)] for b in range(B) for h in range(DH)])
            _, vjp = jax.vjp(_dn_step, S0, *_dn_step_operands(B, c, u_ref, w_ref, qd_ref, kd_ref, a_ref, cd_ref))
            dS, du, dw, dqd, dkd, da, dcd = vjp((do, dS))
            for b in range(B):
                dcdrow = jnp.zeros((1, DH), F32)
                for h in range(DH):
                    n = DH * b + h
                    lanes = slice(DK * h, DK * (h + 1))
                    du_ref[b, rows, lanes] = du[n]
                    dw_ref[b, rows, lanes] = dw[n]
                    dqd_ref[b, rows, lanes] = dqd[n]
                    dkd_ref[b, rows, lanes] = dkd[n]
                    da_ref[b, rows, CH * h:CH * (h + 1)] = da[n]
                    dcdrow = dcdrow + jnp.where(lane4 == h, dcd[n], 0.0)
                dcd_ref[b, c] = dcdrow
        dstate[...] = dS

    wide, a_spec, cd_spec, st_spec = _dn_seq_specs(B, nc // gs, gs, True)
    sd = jax.ShapeDtypeStruct((B, S, DNW), F32)
    return pl.pallas_call(
        body, name="dn_seq_bwd", grid=(nc // gs,),
        out_shape=[sd, sd, sd, sd, jax.ShapeDtypeStruct((B, S, DH * CH), F32), jax.ShapeDtypeStruct((B, nc, 1, DH), F32)],
        in_specs=[wide, wide, wide, wide, a_spec, cd_spec, st_spec, wide],
        out_specs=[wide, wide, wide, wide, a_spec, cd_spec],
        scratch_shapes=[pltpu.VMEM((B * DH, DK, DK), F32)],
        compiler_params=_cparams(("arbitrary",)),
    )(u, w, qd, kd, a_in, cd, states, do)


def _dn_prep_bwd(xin, conv_w, cq, ba, a_log, dt_b, t_inv, du, dw, dqd, dkd, da, dcd):
    B, S, _ = cq.shape
    nc = S // CH
    G = _dn_group(S, 8)
    R = G * CH
    nblk = nc // G
    r8 = R // 8

    def body(xp_ref, x_ref, cw_ref, cq_ref, ba_ref, al_ref, dt_ref, t_ref, du_ref, dw_ref, dqd_ref, dkd_ref, da_ref, dcd_ref,
             dx_ref, dcw_ref, dba_ref, dal_ref, ddt_ref, carry):
        i = pl.program_id(1)

        @pl.when((pl.program_id(0) == 0) & (i == 0))
        def _():
            dal_ref[...] = jnp.zeros_like(dal_ref)
            ddt_ref[...] = jnp.zeros_like(ddt_ref)
            dcw_ref[...] = jnp.zeros_like(dcw_ref)

        @pl.when(i == 0)
        def _():
            carry[...] = jnp.zeros_like(carry)

        pairs = [(c, h) for c in range(G) for h in range(DH)]
        rows = lambda c: slice(CH * c, CH * (c + 1))
        wide = lambda ref: jnp.stack([ref[rows(c), DK * h:DK * (h + 1)] for c, h in pairs])
        square = lambda ref: jnp.stack([ref[rows(c), CH * h:CH * (h + 1)] for c, h in pairs])
        ops = _dn_stack(cq_ref[...], ba_ref[...], al_ref[...], dt_ref[...], G)
        cots = (wide(du_ref), wide(dw_ref), wide(dqd_ref), wide(dkd_ref), square(da_ref),
                jnp.stack([dcd_ref[c][:, h:h + 1] for c, h in pairs]), jnp.zeros((len(pairs), CH, CH), F32))
        _, vjp = jax.vjp(functools.partial(_dn_prep, square(t_ref)), *ops)
        dq, dk, dv, dar, dbr, dl, dd = vjp(cots)
        lane8 = lax.broadcasted_iota(jnp.int32, (CH, 2 * DH), 1)
        lane4 = lax.broadcasted_iota(jnp.int32, (1, DH), 1)
        dal = jnp.zeros((1, DH), F32)
        ddt = jnp.zeros((1, DH), F32)
        for c in range(G):
            dba = jnp.zeros((CH, 2 * DH), F32)
            for h in range(DH):
                n = DH * c + h
                dba = dba + jnp.where(lane8 == h, dbr[n], 0.0) + jnp.where(lane8 == DH + h, dar[n], 0.0)
                dal = dal + jnp.where(lane4 == h, dl[n], 0.0)
                ddt = ddt + jnp.where(lane4 == h, dd[n], 0.0)
            dba_ref[rows(c), :] = dba.astype(BF16)
        dal_ref[...] += dal
        ddt_ref[...] += ddt

        dcq = jnp.concatenate([jnp.concatenate([t[DH * c + h] for t in (dq, dk, dv) for h in range(DH)], axis=1)
                               for c in range(G)], axis=0)
        w = cw_ref[...]
        xp = jnp.where(i < nblk - 1, xp_ref[...], 0.0)
        xe = jnp.concatenate([xp, x_ref[...]], axis=0)
        taps = [(pltpu.roll(xe, CONV - 1 - j, 0) if j < CONV - 1 else xe)[8:8 + R, :] for j in range(CONV)]
        pre = sum(t * w[j:j + 1, :] for j, t in enumerate(taps))
        sg = _sigmoid(pre)
        dpre = dcq * (sg * (1.0 + pre * (1.0 - sg)))
        ext = jnp.concatenate([dpre, carry[...]], axis=0)
        dx = dpre * w[CONV - 1:CONV, :]
        for j in range(CONV - 1):
            dx = dx + pltpu.roll(ext, R + 8 - (CONV - 1 - j), 0)[0:R, :] * w[j:j + 1, :]
        dx_ref[...] = dx.astype(BF16)
        carry[...] = dpre[0:8, :]
        lane_row = lax.broadcasted_iota(jnp.int32, (CONV, CONVW), 0)
        dcw = jnp.zeros((CONV, CONVW), F32)
        for j in range(CONV):
            dcw = dcw + jnp.where(lane_row == j, jnp.sum(taps[j] * dpre, axis=0, keepdims=True), 0.0)
        dcw_ref[...] += dcw

    rev = lambda w: pl.BlockSpec((None, R, w), lambda b, i: (b, nblk - 1 - i, 0))
    return pl.pallas_call(
        body, name="dn_prep_bwd", grid=(B, nblk),
        out_shape=[jax.ShapeDtypeStruct((B, S, CONVW), BF16), jax.ShapeDtypeStruct((CONV, CONVW), F32),
                   jax.ShapeDtypeStruct((B, S, 2 * DH), BF16), jax.ShapeDtypeStruct((1, DH), F32),
                   jax.ShapeDtypeStruct((1, DH), F32)],
        in_specs=[pl.BlockSpec((None, 8, CONVW), lambda b, i: (b, jnp.maximum((nblk - 1 - i) * r8 - 1, 0), 0)),
                  rev(CONVW), _full((CONV, CONVW)), rev(CONVW), rev(2 * DH), _full((1, DH)), _full((1, DH)), rev(DH * CH)]
                 + [rev(DNW)] * 4 + [rev(DH * CH), pl.BlockSpec((None, G, 1, DH), lambda b, i: (b, nblk - 1 - i, 0, 0))],
        out_specs=[rev(CONVW), _full((CONV, CONVW)), rev(2 * DH), _full((1, DH)), _full((1, DH))],
        scratch_shapes=[pltpu.VMEM((8, CONVW), F32)],
        compiler_params=_cparams(("arbitrary", "arbitrary")),
    )(xin, xin, conv_w, cq, ba, a_log, dt_b, t_inv, du, dw, dqd, dkd, da, dcd)


def _gated_norm(o, z, g):
    outs = []
    for h in range(DH):
        t = o[:, DK * h:DK * (h + 1)]
        r = lax.rsqrt(jnp.mean(t * t, axis=-1, keepdims=True) + EPS)
        outs.append(t * r * g * _silu(z[:, DK * h:DK * (h + 1)]))
    return jnp.concatenate(outs, axis=1)


def _mix_fwd(x, o_attn, o_dn, z, ga, gd, mod, dn_g, w_branch, w_out):
    B, S, _ = x.shape
    tm = _tile(S, 512)

    def body(x_ref, oa_ref, od_ref, z_ref, ga_ref, gd_ref, mod_ref, g_ref, wb_ref, wo_ref,
             x1_ref, mix_ref, mg_ref, ob_ref):
        oa = oa_ref[...].astype(BF16)
        od = _gated_norm(od_ref[...], z_ref[...], g_ref[...]).astype(BF16)
        ob_ref[0] = oa
        ob_ref[1] = od
        ya = jnp.dot(oa, wb_ref[0:QW, :], preferred_element_type=F32)
        yd = jnp.dot(od, wb_ref[QW:QW + DNW, :], preferred_element_type=F32)
        merged = (_sigmoid(ga_ref[...]) * ya + _sigmoid(gd_ref[...]) * yd).astype(BF16)
        mg_ref[...] = merged
        mix = jnp.dot(merged, wo_ref[...], preferred_element_type=F32)
        mix_ref[...] = mix
        x1_ref[...] = x_ref[...] + mod_ref[2:3, :] * mix

    return pl.pallas_call(
        body, name="mix_fwd", grid=(B, S // tm),
        out_shape=[jax.ShapeDtypeStruct((B, S, D), F32), jax.ShapeDtypeStruct((B, S, D), F32),
                   jax.ShapeDtypeStruct((B, S, D), BF16), jax.ShapeDtypeStruct((B, 2, S, QW), BF16)],
        in_specs=[_rows(tm, D), _rows(tm, QW), _rows(tm, DNW), _rows(tm, DNW), _rows(tm, D), _rows(tm, D),
                  _perb(6, D), _full((1, DK)), _resident(w_branch.shape), _resident(w_out.shape)],
        out_specs=[_rows(tm, D), _rows(tm, D), _rows(tm, D), _stacked(2, tm, QW)],
        compiler_params=_cparams(("parallel", "arbitrary")),
    )(x, o_attn, o_dn, z, ga, gd, mod, dn_g, w_branch, w_out)


def _mix_bwd(dx1, mix, o_attn, o_dn, z, ga, gd, mod, dn_g, w_branch, w_out):
    B, S, _ = dx1.shape
    tm = _tile(S, 512)

    def body(dx1_ref, mix_ref, oa_ref, od_ref, z_ref, ga_ref, gd_ref, mod_ref, g_ref, wb_ref, wo_ref,
             dmix_ref, dyo_ref, dga_ref, dgd_ref, dz_ref, doa_ref, dod_ref, dgate_ref, dg_ref):
        b, i = pl.program_id(0), pl.program_id(1)
        dx1 = dx1_ref[...]
        dmix = (dx1 * mod_ref[2:3, :]).astype(BF16)
        dmix_ref[...] = dmix
        dgate = jnp.sum(dx1 * mix_ref[...], axis=0, keepdims=True)
        dmerged = _dot_nt(dmix, wo_ref[...])
        odn, gn_vjp = jax.vjp(_gated_norm, od_ref[...], z_ref[...], g_ref[...])
        ya = _dot(oa_ref[...], wb_ref[0:QW, :])
        yd = _dot(odn, wb_ref[QW:QW + DNW, :])
        sa, sd = _sigmoid(ga_ref[...]), _sigmoid(gd_ref[...])
        dya = (dmerged * sa).astype(BF16)
        dyd = (dmerged * sd).astype(BF16)
        dyo_ref[0] = dya
        dyo_ref[1] = dyd
        dga_ref[...] = (dmerged * ya * sa * (1.0 - sa)).astype(BF16)
        dgd_ref[...] = (dmerged * yd * sd * (1.0 - sd)).astype(BF16)
        doa_ref[...] = _dot_nt(dya, wb_ref[0:QW, :])
        dodn = _dot_nt(dyd, wb_ref[QW:QW + DNW, :])
        dod, dz, dg = gn_vjp(dodn)
        dod_ref[...] = dod
        dz_ref[...] = dz.astype(BF16)

        @pl.when(i == 0)
        def _():
            dgate_ref[...] = jnp.zeros_like(dgate_ref)

        @pl.when((b == 0) & (i == 0))
        def _():
            dg_ref[...] = jnp.zeros_like(dg_ref)

        dgate_ref[...] += dgate
        dg_ref[...] += dg

    return pl.pallas_call(
        body, name="mix_bwd", grid=(B, S // tm),
        out_shape=[jax.ShapeDtypeStruct((B, S, D), BF16), jax.ShapeDtypeStruct((B, 2, S, D), BF16),
                   jax.ShapeDtypeStruct((B, S, D), BF16), jax.ShapeDtypeStruct((B, S, D), BF16),
                   jax.ShapeDtypeStruct((B, S, DNW), BF16),
                   jax.ShapeDtypeStruct((B, S, QW), F32), jax.ShapeDtypeStruct((B, S, DNW), F32),
                   jax.ShapeDtypeStruct((B, 1, D), F32), jax.ShapeDtypeStruct((1, DK), F32)],
        in_specs=[_rows(tm, D), _rows(tm, D), _rows(tm, QW), _rows(tm, DNW), _rows(tm, DNW), _rows(tm, D),
                  _rows(tm, D), _perb(6, D), _full((1, DK)), _resident(w_branch.shape), _resident(w_out.shape)],
        out_specs=[_rows(tm, D), _stacked(2, tm, D), _rows(tm, D), _rows(tm, D), _rows(tm, DNW),
                   _rows(tm, QW), _rows(tm, DNW), _perb(1, D), _full((1, DK))],
        compiler_params=_cparams(("arbitrary", "arbitrary")),
    )(dx1, mix, o_attn, o_dn, z, ga, gd, mod, dn_g, w_branch, w_out)


GU_SHARD = 2 * FFN // N_DEV
GU_HALF = N_DEV // 2


def _ffn1_fwd(x1, mod, g2, w_gu):
    B, S, _ = x1.shape
    tm = _tile(S)

    def body(x_ref, mod_ref, g_ref, w_ref, h_ref, dgate_ref, dup_ref, act_ref):
        h = _rms_mod(x_ref[...], g_ref[...], mod_ref[4:5, :], mod_ref[3:4, :]).astype(BF16)
        h_ref[...] = h
        for j in range(GU_HALF):
            gate = _dot_nt(h, w_ref[j])
            up = _dot_nt(h, w_ref[GU_HALF + j])
            sg = _sigmoid(gate)
            silu = gate * sg
            dgate_ref[j] = up * (sg * (1.0 + gate * (1.0 - sg)))
            dup_ref[j] = silu
            act_ref[j] = (silu * up).astype(BF16)

    blk = lambda dt: jax.ShapeDtypeStruct((B, GU_HALF, S, GU_SHARD), dt)
    return pl.pallas_call(
        body, name="ffn1_fwd", grid=(B, S // tm),
        out_shape=[jax.ShapeDtypeStruct((B, S, D), BF16), blk(F32), blk(F32), blk(BF16)],
        in_specs=[_rows(tm, D), _perb(6, D), _full((1, D)), _resident(w_gu.shape)],
        out_specs=[_rows(tm, D)] + [_stacked(GU_HALF, tm, GU_SHARD)] * 3,
        compiler_params=_cparams(("parallel", "arbitrary")),
    )(x1, mod, g2, w_gu)


def _ffn2_fwd(act, x1, target, mod, w_down):
    B, S, _ = x1.shape
    tm = _tile(S, 512)

    def body(a_ref, x_ref, t_ref, mod_ref, w_ref, dy_ref, loss_ref, dgate_ref):
        b, i = pl.program_id(0), pl.program_id(1)
        y = jnp.dot(a_ref[0], w_ref[0], preferred_element_type=F32)
        for j in range(1, GU_HALF):
            y = y + jnp.dot(a_ref[j], w_ref[j], preferred_element_type=F32)
        err = x_ref[...] + mod_ref[5:6, :] * y - t_ref[...]
        dy = err * (1.0 / D)
        dy_ref[...] = dy

        @pl.when((b == 0) & (i == 0))
        def _():
            loss_ref[...] = jnp.zeros_like(loss_ref)

        @pl.when(i == 0)
        def _():
            dgate_ref[...] = jnp.zeros_like(dgate_ref)

        loss_ref[...] += (0.5 / D) * jnp.sum(err * err)
        dgate_ref[...] += jnp.sum(dy * y, axis=0, keepdims=True)

    return pl.pallas_call(
        body, name="ffn2_fwd", grid=(B, S // tm),
        out_shape=[jax.ShapeDtypeStruct((B, S, D), F32), jax.ShapeDtypeStruct((1, 128), F32),
                   jax.ShapeDtypeStruct((B, 1, D), F32)],
        in_specs=[_stacked(GU_HALF, tm, GU_SHARD), _rows(tm, D), _rows(tm, D), _perb(6, D), _resident(w_down.shape)],
        out_specs=[_rows(tm, D), _full((1, 128)), _perb(1, D)],
        compiler_params=_cparams(("arbitrary", "arbitrary")),
    )(act, x1, target, mod, w_down)


def _ffn2_bwd(dy, act_dgate, act_dup, mod, w_down):
    B, S, _ = dy.shape
    tm = _tile(S, 512)

    def body(dy_ref, dgate_ref, dup_ref, mod_ref, w_ref, dgu_ref, dyg_ref):
        dyg = (dy_ref[...] * mod_ref[5:6, :]).astype(BF16)
        dyg_ref[...] = dyg
        for j in range(GU_HALF):
            dact = _dot_nt(dyg, w_ref[j])
            dgu_ref[j] = (dact * dgate_ref[j]).astype(BF16)
            dgu_ref[GU_HALF + j] = (dact * dup_ref[j]).astype(BF16)

    return pl.pallas_call(
        body, name="ffn2_bwd", grid=(B, S // tm),
        out_shape=[jax.ShapeDtypeStruct((B, N_DEV, S, GU_SHARD), BF16), jax.ShapeDtypeStruct((B, S, D), BF16)],
        in_specs=[_rows(tm, D), _stacked(GU_HALF, tm, GU_SHARD), _stacked(GU_HALF, tm, GU_SHARD), _perb(6, D),
                  _resident(w_down.shape)],
        out_specs=[_stacked(N_DEV, tm, GU_SHARD), _rows(tm, D)],
        compiler_params=_cparams(("parallel", "arbitrary")),
    )(dy, act_dgate, act_dup, mod, w_down)


def _ffn1_bwd(dgu, x1, dy, mod, g2, w_gu):
    B, S, _ = x1.shape
    tm = _tile(S, 512)

    def body(dgu_ref, x_ref, dy_ref, mod_ref, g_ref, w_ref, dx1_ref, dg_ref, dsc_ref, dsh_ref):
        b, i = pl.program_id(0), pl.program_id(1)
        dh = jnp.dot(dgu_ref[0], w_ref[0], preferred_element_type=F32)
        for j in range(1, N_DEV):
            dh = dh + jnp.dot(dgu_ref[j], w_ref[j], preferred_element_type=F32)
        _, vjp = jax.vjp(_rms_mod, x_ref[...], g_ref[...], mod_ref[4:5, :], mod_ref[3:4, :])
        dx, dg, dsc, dsh = vjp(dh)
        dx1_ref[...] = dy_ref[...] + dx

        @pl.when((b == 0) & (i == 0))
        def _():
            dg_ref[...] = jnp.zeros_like(dg_ref)

        @pl.when(i == 0)
        def _():
            dsc_ref[...] = jnp.zeros_like(dsc_ref)
            dsh_ref[...] = jnp.zeros_like(dsh_ref)

        dg_ref[...] += dg
        dsc_ref[...] += dsc
        dsh_ref[...] += dsh

    return pl.pallas_call(
        body, name="ffn1_bwd", grid=(B, S // tm),
        out_shape=[jax.ShapeDtypeStruct((B, S, D), F32), jax.ShapeDtypeStruct((1, D), F32),
                   jax.ShapeDtypeStruct((B, 1, D), F32), jax.ShapeDtypeStruct((B, 1, D), F32)],
        in_specs=[_stacked(N_DEV, tm, GU_SHARD), _rows(tm, D), _rows(tm, D), _perb(6, D), _full((1, D)),
                  _resident(w_gu.shape)],
        out_specs=[_rows(tm, D), _full((1, D)), _perb(1, D), _perb(1, D)],
        compiler_params=_cparams(("arbitrary", "arbitrary")),
    )(dgu, x1, dy, mod, g2, w_gu)


def _adamw(w, g, m, v, name):
    def body(w_ref, g_ref, m_ref, v_ref, d_ref, nm_ref, nv_ref):
        d_ref[...], nm_ref[...], nv_ref[...] = _adamw_math(w_ref[...], g_ref[...], m_ref[...], v_ref[...])

    sd = jax.ShapeDtypeStruct(w.shape, F32)
    return pl.pallas_call(body, name=name, out_shape=(sd, sd, sd), compiler_params=_cparams())(w, g, m, v)


def kernel(x, c, positions, ada_w, ada_b, norm1_g, w_in, conv_w, q_norm_g, k_norm_g, sinks, a_log, dt_bias, dn_norm_g, w_branch, w_out, norm2_g, w_gate_up, w_down, loss_target, m_ada_w, m_ada_b, m_norm1_g, m_w_in, m_conv_w, m_q_norm_g, m_k_norm_g, m_sinks, m_a_log, m_dt_bias, m_dn_norm_g, m_w_branch, m_w_out, m_norm2_g, m_w_gate_up, m_w_down, v_ada_w, v_ada_b, v_norm1_g, v_w_in, v_conv_w, v_q_norm_g, v_k_norm_g, v_sinks, v_a_log, v_dt_bias, v_dn_norm_g, v_w_branch, v_w_out, v_norm2_g, v_w_gate_up, v_w_down):
    B, S, _ = x.shape
    me = 4 * lax.axis_index("x") + 2 * lax.axis_index("y") + lax.axis_index("c")

    tr = lambda t: jnp.swapaxes(t, 1, 2)
    shards = [w[0].astype(BF16) for w in (tr(w_in), w_branch, w_out, tr(w_gate_up), w_down)]

    c_all = _all_gather_small(c, "gather_c").reshape(N_DEV * B, D)
    ncol = 6 * D // N_DEV
    mod_cols, cond_all = _ada_fwd(c_all, ada_w[0], lax.dynamic_slice(ada_b, (0, me * ncol), (1, ncol)))
    mod_all = _all_gather_small(mod_cols, "gather_mod").transpose(1, 0, 2).reshape(N_DEV * B, 6 * D)
    mod = lax.dynamic_slice(mod_all, (me * B, 0), (B, 6 * D)).reshape(B, 6, D)
    conv2 = conv_w.reshape(CONV, CONVW // N_DEV)
    conv_all = _all_gather_small(conv2, "gather_conv").transpose(1, 0, 2).reshape(CONV, CONVW)

    invf, mean_q, mean_k = _attn_consts()
    w_in_b, rope_cos, rope_sin = _all_gather_big(shards[:1], "gather_w_in", after=(mod, conv_all),
                                                 side=_rope_tables_side(positions.reshape(B, S, 1), invf))
    w_sems, w_srcs, w_lands, w_token = _copies_start(shards[1:], [_place_own(s, me) for s in shards[1:]], False,
                                                    w_in_b, "gather_rest_start")

    w_in_t = w_in_b.reshape(IN_W, D)
    h1, aq, akv, dnx, ba, z, ga, gd = _inproj_fwd(x, mod, norm1_g + w_token[0, 0], w_in_t)
    o_attn = _attn_fwd(aq, akv, rope_cos, rope_sin, q_norm_g, k_norm_g, sinks, mean_q, mean_k)
    cq, dn_u, dn_w, dn_qd, dn_kd, dn_a, dn_t, dn_cd = _dn_prep_fwd(dnx, conv_all, ba, a_log, dt_bias)
    o_dn, states = _dn_seq_fwd(dn_u, dn_w, dn_qd, dn_kd, dn_a, dn_cd)
    w_branch_g, w_out_g, w_gu_b, w_down_g = _copies_wait(w_sems, w_srcs, w_lands, o_dn, "gather_wait_rest")
    w_branch_f = w_branch_g.reshape(D, D)
    w_out_f = w_out_g.reshape(D, D)
    w_down_b = w_down_g.reshape(GU_HALF, GU_SHARD, D)
    x1, mix, merged, ob = _mix_fwd(x, o_attn, o_dn, z, ga, gd, mod, dn_norm_g, w_branch_f, w_out_f)
    h2, act_dgate, act_dup, act = _ffn1_fwd(x1, mod, norm2_g, w_gu_b)
    dy, loss_part, d_gate2 = _ffn2_fwd(act, x1, loss_target, mod, w_down_b)
    loss = lax.psum(loss_part[0, 0], ("x", "y", "c"))

    one = lambda t: t.reshape(B, 1, S, t.shape[-1])
    dgu, dyg = _ffn2_bwd(dy, act_dgate, act_dup, mod, w_down_b)
    g_w_down = _wgrad(act, one(dyg), "wgrad_down")
    dx1, d_n2g, d_scale2, d_shift2 = _ffn1_bwd(dgu, x1, dy, mod, norm2_g, w_gu_b)
    g_w_gu = _wgrad(dgu, one(h2), "wgrad_gate_up")
    ffn = _exchange_start([g_w_gu, g_w_down.reshape(N_DEV, FFN // N_DEV, D)], me, dx1, "exchange_ffn_start")
    dmix, dyo, dga, dgd, dz, d_oa, d_od, d_gate1, d_dng = _mix_bwd(
        dx1, mix, o_attn, o_dn, z, ga, gd, mod, dn_norm_g + ffn[3][0, 0], w_branch_f, w_out_f)
    d_dn = _dn_seq_bwd(dn_u, dn_w, dn_qd, dn_kd, dn_a, dn_cd, states, d_od)
    ddnx, d_conv, dba, d_alog, d_dtb = _dn_prep_bwd(dnx, conv_all, cq, ba, a_log, dt_bias, dn_t, *d_dn)
    daq, dakv, d_qg, d_kg, d_sinks = _attn_bwd(aq, akv, rope_cos, rope_sin, q_norm_g, k_norm_g, sinks, mean_q, mean_k, d_oa)
    dps = [daq, dakv, ddnx, dba, dz, dga, dgd]
    dblk, grad_x, d_n1g, d_scale1, d_shift1 = _inproj_bwd(x, mod, norm1_g, dx1, dps, w_in_t)

    dmod = jnp.concatenate([d_shift1, d_scale1, d_gate1, d_shift2, d_scale2, d_gate2], axis=2).reshape(B, 6 * D)
    small = jnp.concatenate([d_n1g, d_qg, d_kg, d_sinks, d_alog, d_dtb, d_dng, d_n2g, d_conv.reshape(1, CONV * CONVW)], axis=1)
    nsm = small.shape[1]
    width = -(-max(6 * D, nsm) // 128) * 128
    rows = jnp.concatenate([jnp.pad(dmod, ((0, 0), (0, width - 6 * D))), jnp.pad(small, ((0, 8 - B - 1), (0, width - nsm)))], axis=0)
    rows_all = _all_gather_small(rows, "gather_small")
    dmod_all = rows_all[:, 0:B, 0:6 * D].reshape(N_DEV * B, 6 * D)
    dmod_cols = lax.dynamic_slice(dmod_all, (0, me * ncol), (N_DEV * B, ncol))
    grad_ada_w, grad_ada_b, small_sum = _ada_bwd(cond_all, dmod_all, dmod_cols, rows_all[:, B, :])
    sizes = [D, HD, HD, HQ, DH, DH, DK, D]
    so = np.cumsum([0] + sizes)
    g_n1, g_qg, g_kg, g_sk, g_al, g_dt, g_dn, g_n2 = [small_sum[:, so[i]:so[i + 1]] for i in range(8)]
    g_conv_all = small_sum[:, so[8]:so[8] + CONV * CONVW].reshape(CONV, N_DEV, CONVW // N_DEV)
    grad_conv = lax.dynamic_slice(g_conv_all, (0, me, 0), (CONV, 1, CONVW // N_DEV)).reshape(CONV, CONVW // N_DEV)

    half = D // 2
    g_w_in_a = _wgrad(dblk, one(h1), "wgrad_in_a", after=small_sum, b_lanes=(0, half))
    proj_a = _exchange_start([g_w_in_a], me, small_sum, "exchange_in_a_start")
    g_w_in_b = _wgrad(dblk, one(h1), "wgrad_in_b", after=proj_a[3], b_lanes=(1, half))
    proj = _exchange_start([g_w_in_b], me, proj_a[3], "exchange_in_b_start")
    g_w_out = _wgrad(one(merged), one(dmix), "wgrad_out", after=proj[3])
    g_w_branch = _wgrad(ob, dyo, "wgrad_branch", after=proj[3])
    mixer = _exchange_start([g_w_branch.reshape(N_DEV, D // N_DEV, D), g_w_out.reshape(N_DEV, D // N_DEV, D)], me,
                            proj[3], "exchange_mix_start")

    upd, grads = {}, {}

    def finish(names, parts, weights):
        for nm, p, (w, m, v) in zip(names, parts, weights):
            grads[nm], *upd[nm] = _sum_adamw(p, w, m, v, "update_" + nm)
        return grads[names[-1]]

    finish(["w_gate_up", "w_down"], _copies_wait(*ffn[:3], mixer[3], "exchange_ffn_wait"),
           [(tr(w_gate_up), tr(m_w_gate_up), tr(v_w_gate_up)), (w_down, m_w_down, v_w_down)])
    (in_a,) = _copies_wait(*proj_a[:3], grads["w_gate_up"], "exchange_in_a_wait")
    (in_b,) = _copies_wait(*proj[:3], in_a, "exchange_in_b_wait")
    finish(["w_in"], [[in_a, in_b]], [(tr(w_in), tr(m_w_in), tr(v_w_in))])
    finish(["w_branch", "w_out"], _copies_wait(*mixer[:3], grads["w_in"], "exchange_mix_wait"),
           [(w_branch, m_w_branch, v_w_branch), (w_out, m_w_out, v_w_out)])
    for nm in ("w_in", "w_gate_up"):
        grads[nm], upd[nm] = tr(grads[nm]), [tr(t) for t in upd[nm]]

    grads["ada_w"] = grad_ada_w.reshape(ada_w.shape)
    upd["ada_w"] = _adamw(ada_w, grads["ada_w"], m_ada_w, v_ada_w, "adamw_ada_w")
    small_names = ["ada_b", "norm1_g", "q_norm_g", "k_norm_g", "sinks", "a_log", "dt_bias", "dn_norm_g", "norm2_g", "conv_w"]
    small_w = [ada_b, norm1_g, q_norm_g, k_norm_g, sinks, a_log, dt_bias, dn_norm_g, norm2_g, conv_w]
    small_g = [grad_ada_b, g_n1, g_qg, g_kg, g_sk, g_al, g_dt, g_dn, g_n2, grad_conv]
    small_m = [m_ada_b, m_norm1_g, m_q_norm_g, m_k_norm_g, m_sinks, m_a_log, m_dt_bias, m_dn_norm_g, m_norm2_g, m_conv_w]
    small_v = [v_ada_b, v_norm1_g, v_q_norm_g, v_k_norm_g, v_sinks, v_a_log, v_dt_bias, v_dn_norm_g, v_norm2_g, v_conv_w]
    cat = lambda arrs: jnp.concatenate([a.reshape(1, -1) for a in arrs], axis=1)
    res = _adamw(cat(small_w), cat(small_g), cat(small_m), cat(small_v), "adamw_small")
    po = np.cumsum([0] + [int(np.prod(w.shape)) for w in small_w])
    for i, nm in enumerate(small_names):
        upd[nm] = tuple(r[:, po[i]:po[i + 1]].reshape(small_w[i].shape) for r in res)
        grads[nm] = small_g[i].reshape(small_w[i].shape)

    order = ["ada_w", "ada_b", "norm1_g", "w_in", "conv_w", "q_norm_g", "k_norm_g", "sinks", "a_log", "dt_bias",
             "dn_norm_g", "w_branch", "w_out", "norm2_g", "w_gate_up", "w_down"]
    return (loss, grad_x, *[grads[n] for n in order], *[upd[n][0] for n in order],
            *[upd[n][1] for n in order], *[upd[n][2] for n in order])
```

```python
import functools

import numpy as np
import jax
import jax.numpy as jnp
from jax import lax
from jax.experimental import pallas as pl
from jax.experimental.pallas import tpu as pltpu

F32 = jnp.float32
BF16 = jnp.bfloat16
HI = lax.Precision.HIGHEST

N_DEV = 8
D = 1024
HQ, HKV, HD = 8, 2, 64
GRP = HQ // HKV
BLK = 128
ROT = HD // 4
THETA = 500000.0
QW, KVW = HQ * HD, HKV * HD
DH, DK = 4, 128
CH = 64
DNW = DH * DK
CONV = 4
CONVW = 3 * DNW
FFN = 2816
EPS = 1e-6
IN_W = QW + 2 * KVW + CONVW + 2 * DH + DNW + 2 * D

LR, B1, B2, AEPS, WD, STEP = 0.001, 0.9, 0.999, 1e-08, 0.01, 10

VMEM_LIMIT = 56 * 1024 * 1024
MESH = pl.DeviceIdType.MESH


def _cparams(sem=None, vmem=VMEM_LIMIT):
    return pltpu.CompilerParams(dimension_semantics=sem, vmem_limit_bytes=vmem)


def _full(shape):
    n = len(shape)
    return pl.BlockSpec(shape, lambda *_: (0,) * n)


def _resident(shape):
    n = len(shape)
    return pl.BlockSpec(shape, lambda *_: (0,) * n, pipeline_mode=pl.Buffered(1))


def _rows(tm, w):
    return pl.BlockSpec((None, tm, w), lambda b, i: (b, i, 0))


def _stacked(n, tm, w):
    return pl.BlockSpec((None, n, tm, w), lambda b, i: (b, 0, i, 0))


def _perb(r, w):
    return pl.BlockSpec((None, r, w), lambda b, i: (b, 0, 0))


def _dot(a, b):
    return jnp.dot(a.astype(BF16), b.astype(BF16), preferred_element_type=F32)


def _dot_nt(a, b):
    return lax.dot_general(a.astype(BF16), b.astype(BF16), (((1,), (1,)), ((), ())), preferred_element_type=F32)


def _dot_tn(a, b):
    return lax.dot_general(a.astype(BF16), b.astype(BF16), (((0,), (0,)), ((), ())), preferred_element_type=F32)


def _dot_hi(a, b):
    return jnp.dot(a, b, preferred_element_type=F32, precision=HI)


def _sigmoid(x):
    return jax.nn.sigmoid(x)


def _silu(x):
    return x * jax.nn.sigmoid(x)


def _rms_mod(x, g, scale, shift):
    r = lax.rsqrt(jnp.mean(x * x, axis=-1, keepdims=True) + EPS)
    return (x * r * g) * (1.0 + scale) + shift


def _tile(S, rows=256):
    return min(rows, S)


def _peer(x, y, c, k):
    px = 1 - x if (k >> 2) & 1 else x
    py = 1 - y if (k >> 1) & 1 else y
    pc = 1 - c if k & 1 else c
    return px, py, pc


def _all_gather_small(v, name):
    r, n = v.shape

    def body(v_ref, out_ref, send_sems, recv_sems, local_sem):
        x, y, c = lax.axis_index("x"), lax.axis_index("y"), lax.axis_index("c")
        me = 4 * x + 2 * y + c
        mine = pltpu.make_async_copy(v_ref, out_ref.at[me], local_sem)
        mine.start()
        sends = []
        for k in range(1, N_DEV):
            cp = pltpu.make_async_remote_copy(
                src_ref=v_ref, dst_ref=out_ref.at[me], send_sem=send_sems.at[k - 1], recv_sem=recv_sems.at[k - 1],
                device_id=_peer(x, y, c, k), device_id_type=MESH)
            cp.start()
            sends.append(cp)
        for k in range(1, N_DEV):
            px, py, pc = _peer(x, y, c, k)
            pltpu.make_async_remote_copy(
                src_ref=v_ref, dst_ref=out_ref.at[4 * px + 2 * py + pc], send_sem=send_sems.at[k - 1],
                recv_sem=recv_sems.at[k - 1], device_id=(px, py, pc), device_id_type=MESH).wait_recv()
        for cp in sends:
            cp.wait_send()
        mine.wait()

    return pl.pallas_call(
        body, name=name,
        out_shape=jax.ShapeDtypeStruct((N_DEV, r, n), v.dtype),
        in_specs=[pl.BlockSpec(memory_space=pltpu.VMEM)],
        out_specs=pl.BlockSpec(memory_space=pltpu.VMEM),
        scratch_shapes=[pltpu.SemaphoreType.DMA((N_DEV - 1,)), pltpu.SemaphoreType.DMA((N_DEV - 1,)), pltpu.SemaphoreType.DMA],
    )(v)


def _all_gather_big(vs, name, after=(), side=None):
    na, nf = len(vs), len(after)
    side_fn, side_in, side_out = side if side is not None else (None, (), ())
    ns, no = len(side_in), len(side_out)

    def body(*refs):
        v_refs, out_refs = refs[:na], refs[na + nf + ns:2 * na + nf + ns]
        send_sems, recv_sems, local_sems = refs[2 * na + nf + ns + no:]
        x, y, c = lax.axis_index("x"), lax.axis_index("y"), lax.axis_index("c")
        me, sibling = (x, y, c), (x, y, 1 - c)
        chips = [(1 - x, y), (x, 1 - y), (1 - x, 1 - y)]

        def rows(a, px, py, pc):
            return out_refs[a].at[4 * px + 2 * py + pc]

        def copy(a, k, block, to, src=None):
            return pltpu.make_async_remote_copy(
                src_ref=rows(a, *block) if src is None else src, dst_ref=rows(a, *block),
                send_sem=send_sems.at[7 * a + k], recv_sem=recv_sems.at[7 * a + k], device_id=to, device_id_type=MESH)

        mine = [pltpu.make_async_copy(v_refs[a], rows(a, *me), local_sems.at[a]) for a in range(na)]
        for cp in mine:
            cp.start()
        first = []
        for a in range(na):
            first.append(copy(a, 0, me, sibling, src=v_refs[a]))
            first += [copy(a, 1 + j, me, (*chip, c), src=v_refs[a]) for j, chip in enumerate(chips)]
        for cp in first:
            cp.start()
        if side_fn is not None:
            side_fn(refs[na + nf:na + nf + ns], refs[2 * na + nf + ns:2 * na + nf + ns + no])
        passed = []
        for j, chip in enumerate(chips):
            for a in range(na):
                copy(a, 1 + j, (*chip, c), me).wait_recv()
                forward = copy(a, 4 + j, (*chip, c), sibling)
                forward.start()
                passed.append(forward)
        for a in range(na):
            copy(a, 0, sibling, me).wait_recv()
            for j, chip in enumerate(chips):
                copy(a, 4 + j, (*chip, 1 - c), me).wait_recv()
        for cp in first + passed:
            cp.wait_send()
        for cp in mine:
            cp.wait()

    return pl.pallas_call(
        body, name=name,
        out_shape=[jax.ShapeDtypeStruct((N_DEV,) + v.shape, v.dtype) for v in vs] + list(side_out),
        in_specs=[pl.BlockSpec(memory_space=pl.ANY)] * (na + nf) + [pl.BlockSpec(memory_space=pltpu.VMEM)] * ns,
        out_specs=[pl.BlockSpec(memory_space=pl.ANY)] * na + [pl.BlockSpec(memory_space=pltpu.VMEM)] * no,
        scratch_shapes=[pltpu.SemaphoreType.DMA((7 * na,)), pltpu.SemaphoreType.DMA((7 * na,)),
                        pltpu.SemaphoreType.DMA((na,))],
        compiler_params=pltpu.CompilerParams(vmem_limit_bytes=VMEM_LIMIT),
    )(*vs, *after, *side_in)


_HBM = pl.BlockSpec(memory_space=pltpu.HBM)
_SEM = pl.BlockSpec(memory_space=pltpu.SEMAPHORE)
_EFFECT = pltpu.SideEffectType.DATAFLOW_SIDE_EFFECTING


def _place_own(block, me):
    land = lax.empty((N_DEV,) + block.shape, block.dtype)
    return lax.dynamic_update_slice(land, block[None], (me,) + (0,) * block.ndim)


def _copies_start(srcs, lands, scatter, after, name):
    na = len(srcs)
    afters = tuple(after) if isinstance(after, (tuple, list)) else (after,)

    def body(*refs):
        src_refs, land_refs = refs[:na], refs[na:2 * na]
        sems = refs[2 * na + len(afters):4 * na + len(afters)]
        token = refs[-1]
        x, y, c = lax.axis_index("x"), lax.axis_index("y"), lax.axis_index("c")
        me = 4 * x + 2 * y + c
        for a in range(na):
            for k in range(1, N_DEV):
                px, py, pc = _peer(x, y, c, k)
                src = src_refs[a].at[4 * px + 2 * py + pc] if scatter else src_refs[a]
                pltpu.make_async_remote_copy(
                    src_ref=src, dst_ref=land_refs[a].at[me], send_sem=sems[2 * a], recv_sem=sems[2 * a + 1],
                    device_id=(px, py, pc), device_id_type=MESH).start()
        token[...] = jnp.zeros_like(token)

    hbm = lambda t: pltpu.HBM(t.shape, t.dtype)
    out = pl.pallas_call(
        body, name=name,
        out_shape=tuple([pltpu.SemaphoreType.DMA(())] * (2 * na) + [hbm(t) for t in srcs] + [hbm(t) for t in lands]
                        + [jax.ShapeDtypeStruct((8, 128), F32)]),
        in_specs=[_HBM] * (2 * na) + [pl.BlockSpec(memory_space=pl.ANY)] * len(afters),
        out_specs=tuple([_SEM] * (2 * na) + [_HBM] * (2 * na) + [pl.BlockSpec(memory_space=pltpu.VMEM)]),
        input_output_aliases={i: 2 * na + i for i in range(2 * na)},
        compiler_params=pltpu.CompilerParams(has_side_effects=_EFFECT),
    )(*[pltpu.with_memory_space_constraint(t, pltpu.HBM) for t in list(srcs) + list(lands)], *afters)
    return out[:2 * na], out[2 * na:3 * na], out[3 * na:4 * na], out[-1]


def _exchange_start(gs, me, after, name):
    own = [lax.dynamic_index_in_dim(g, me, 0, keepdims=False) for g in gs]
    return _copies_start(gs, [_place_own(o, me) for o in own], True, after, name)


def _copies_wait(sems, srcs, lands, after, name):
    na = len(srcs)

    def body(*refs):
        land_refs = refs[na:2 * na]
        sem_refs = refs[2 * na:4 * na]
        x, y, c = lax.axis_index("x"), lax.axis_index("y"), lax.axis_index("c")
        for a in range(na):
            seven = land_refs[a].at[pl.ds(0, N_DEV - 1)]
            copy = pltpu.make_async_remote_copy(
                src_ref=seven, dst_ref=seven, send_sem=sem_refs[2 * a], recv_sem=sem_refs[2 * a + 1],
                device_id=(x, y, c), device_id_type=MESH)
            copy.wait_send()
            copy.wait_recv()

    hbm = lambda t: pltpu.HBM(t.shape, t.dtype)
    out = pl.pallas_call(
        body, name=name,
        out_shape=tuple([hbm(t) for t in srcs] + [hbm(t) for t in lands]),
        in_specs=[_HBM] * (2 * na) + [_SEM] * (2 * na) + [pl.BlockSpec(memory_space=pl.ANY)],
        out_specs=tuple([_HBM] * (2 * na)),
        input_output_aliases={i: i for i in range(2 * na)},
        compiler_params=pltpu.CompilerParams(has_side_effects=_EFFECT),
    )(*srcs, *lands, *sems, after)
    return out[na:]


def _adamw_math(w, g, m, v):
    m = B1 * m + (1.0 - B1) * g
    v = B2 * v + (1.0 - B2) * (g * g)
    m_hat = m / (1.0 - B1 ** STEP)
    v_hat = v / (1.0 - B2 ** STEP)
    return -LR * (m_hat / (jnp.sqrt(v_hat) + AEPS) + WD * w), m, v


def _sum_adamw(parts, w, m, v, name):
    parts = list(parts) if isinstance(parts, (list, tuple)) else [parts]
    r, n = w.shape[1], w.shape[2]
    tr = 256 if r % 256 == 0 else r
    npart = len(parts)

    def body(*refs):
        p_refs = refs[:npart]
        w_ref, m_ref, v_ref, g_ref, d_ref, nm_ref, nv_ref = refs[npart:]
        pieces = []
        for p_ref in p_refs:
            g = p_ref[0].astype(F32)
            for dev in range(1, N_DEV):
                g = g + p_ref[dev].astype(F32)
            pieces.append(g)
        g = pieces[0] if npart == 1 else jnp.concatenate(pieces, axis=1)
        g_ref[...] = g
        d_ref[...], nm_ref[...], nv_ref[...] = _adamw_math(w_ref[...], g, m_ref[...], v_ref[...])

    rows = pl.BlockSpec((None, tr, n), lambda i: (0, i, 0))
    sd = jax.ShapeDtypeStruct((1, r, n), F32)
    return pl.pallas_call(
        body, name=name, grid=(r // tr,), out_shape=(sd, sd, sd, sd),
        in_specs=[pl.BlockSpec((N_DEV, tr, p.shape[2]), lambda i: (0, i, 0)) for p in parts] + [rows, rows, rows],
        out_specs=(rows, rows, rows, rows),
        compiler_params=_cparams(("parallel",)),
    )(*parts, w, m, v)


def _ada_fwd(c_all, ada_w, ada_b_cols):
    nb, ncol = c_all.shape[0], ada_w.shape[1]

    def body(c_ref, w_ref, b_ref, mod_ref, cond_ref):
        cond = _silu(c_ref[...])
        cond_ref[...] = cond
        mod_ref[...] = _dot_hi(cond, w_ref[...]) + b_ref[...]

    return pl.pallas_call(
        body, name="ada_fwd",
        out_shape=(jax.ShapeDtypeStruct((nb, ncol), F32), jax.ShapeDtypeStruct((nb, D), F32)),
        compiler_params=_cparams(),
    )(c_all, ada_w, ada_b_cols)


def _ada_bwd(cond_all, dmod_all, dmod_cols, smalls):
    ncol, nsm = dmod_cols.shape[1], smalls.shape[1]

    def body(cond_ref, dm_ref, dmc_ref, sm_ref, gw_ref, gb_ref, gs_ref):
        gw_ref[...] = lax.dot_general(cond_ref[...], dmc_ref[...], (((0,), (0,)), ((), ())),
                                      preferred_element_type=F32, precision=HI)
        gb_ref[...] = jnp.sum(dm_ref[...], axis=0, keepdims=True)
        gs_ref[...] = jnp.sum(sm_ref[...], axis=0, keepdims=True)

    return pl.pallas_call(
        body, name="ada_bwd",
        out_shape=(jax.ShapeDtypeStruct((D, ncol), F32), jax.ShapeDtypeStruct((1, 6 * D), F32),
                   jax.ShapeDtypeStruct((1, nsm), F32)),
        compiler_params=_cparams(),
    )(cond_all, dmod_all, dmod_cols, smalls)


IN_CUTS = (0, QW, QW + 2 * KVW, QW + 2 * KVW + CONVW, QW + 2 * KVW + CONVW + 2 * DH,
           QW + 2 * KVW + CONVW + 2 * DH + DNW, QW + 2 * KVW + CONVW + 2 * DH + DNW + D, IN_W)
IN_WIDTHS = tuple(b - a for a, b in zip(IN_CUTS[:-1], IN_CUTS[1:]))
IN_SHARD = IN_W // N_DEV


def _inproj_fwd(x, mod, g1, w_t):
    B, S, _ = x.shape
    tm = _tile(S, 512)

    def body(x_ref, mod_ref, g_ref, w_ref, h_ref, *o_refs):
        h = _rms_mod(x_ref[...], g_ref[...], mod_ref[1:2, :], mod_ref[0:1, :]).astype(BF16)
        h_ref[...] = h
        full = _dot_nt(h, w_ref[...])
        for o_ref, lo, hi in zip(o_refs, IN_CUTS[:-1], IN_CUTS[1:]):
            o_ref[...] = full[:, lo:hi]

    return pl.pallas_call(
        body, name="inproj_fwd", grid=(B, S // tm),
        out_shape=[jax.ShapeDtypeStruct((B, S, D), BF16)] + [jax.ShapeDtypeStruct((B, S, w), F32) for w in IN_WIDTHS],
        in_specs=[_rows(tm, D), _perb(6, D), _full((1, D)), _resident(w_t.shape)],
        out_specs=[_rows(tm, D)] + [_rows(tm, w) for w in IN_WIDTHS],
        compiler_params=_cparams(("parallel", "arbitrary")),
    )(x, mod, g1, w_t)


def _inproj_bwd(x, mod, g1, dx1, dps, w_t):
    B, S, _ = x.shape
    tm = _tile(S, 512)
    n = len(dps)

    def body(x_ref, mod_ref, g_ref, dx1_ref, *refs):
        dp_refs, w_ref = refs[:n], refs[n]
        dblk_ref, gx_ref, dg_ref, dsc_ref, dsh_ref = refs[n + 1:]
        b, i = pl.program_id(0), pl.program_id(1)
        full = jnp.concatenate([r[...].astype(F32) for r in dp_refs], axis=1)
        for j in range(N_DEV):
            dblk_ref[j] = full[:, IN_SHARD * j:IN_SHARD * (j + 1)].astype(BF16)
        dh = jnp.dot(full.astype(BF16), w_ref[...], preferred_element_type=F32)
        _, vjp = jax.vjp(_rms_mod, x_ref[...], g_ref[...], mod_ref[1:2, :], mod_ref[0:1, :])
        dx, dg, dsc, dsh = vjp(dh)
        gx_ref[...] = dx1_ref[...] + dx

        @pl.when((b == 0) & (i == 0))
        def _():
            dg_ref[...] = jnp.zeros_like(dg_ref)

        @pl.when(i == 0)
        def _():
            dsc_ref[...] = jnp.zeros_like(dsc_ref)
            dsh_ref[...] = jnp.zeros_like(dsh_ref)

        dg_ref[...] += dg
        dsc_ref[...] += dsc
        dsh_ref[...] += dsh

    return pl.pallas_call(
        body, name="inproj_bwd", grid=(B, S // tm),
        out_shape=[jax.ShapeDtypeStruct((B, N_DEV, S, IN_SHARD), BF16), jax.ShapeDtypeStruct((B, S, D), F32),
                   jax.ShapeDtypeStruct((1, D), F32), jax.ShapeDtypeStruct((B, 1, D), F32),
                   jax.ShapeDtypeStruct((B, 1, D), F32)],
        in_specs=[_rows(tm, D), _perb(6, D), _full((1, D)), _rows(tm, D)]
                 + [_rows(tm, w) for w in IN_WIDTHS] + [_resident(w_t.shape)],
        out_specs=[pl.BlockSpec((None, N_DEV, tm, IN_SHARD), lambda b, i: (b, 0, i, 0)), _rows(tm, D),
                   _full((1, D)), _perb(1, D), _perb(1, D)],
        compiler_params=_cparams(("arbitrary", "arbitrary")),
    )(x, mod, g1, dx1, *dps, w_t)


def _wgrad(a, b, name, after=None, b_lanes=None):
    B, na, S, K = a.shape
    nb, N = b.shape[1], b.shape[3]
    lane_blk = 0
    if b_lanes is not None:
        lane_blk, N = b_lanes
    G = max(na, nb)
    tm = min(4096, S)
    nt = S // tm
    last = B * nt - 1

    def body(a_ref, b_ref, *rest):
        o_ref, acc = rest[-2:]
        t = pl.program_id(1)

        @pl.when(t == 0)
        def _():
            acc[...] = jnp.zeros_like(acc)

        acc[...] += lax.dot_general(a_ref[...], b_ref[...], (((0,), (0,)), ((), ())), preferred_element_type=F32)

        @pl.when(t == last)
        def _():
            o_ref[...] = acc[...].astype(BF16)

    return pl.pallas_call(
        body, name=name, grid=(G, B * nt),
        out_shape=jax.ShapeDtypeStruct((G, K, N), BF16),
        in_specs=[pl.BlockSpec((None, None, tm, K), lambda g, t: (t // nt, g if na > 1 else 0, t % nt, 0)),
                  pl.BlockSpec((None, None, tm, N), lambda g, t: (t // nt, g if nb > 1 else 0, t % nt, lane_blk))]
                 + ([] if after is None else [pl.BlockSpec(memory_space=pl.ANY)]),
        out_specs=pl.BlockSpec((None, K, N), lambda g, t: (g, 0, 0)),
        scratch_shapes=[pltpu.VMEM((K, N), F32)],
        compiler_params=_cparams(("parallel", "arbitrary")),
    )(*((a, b) if after is None else (a, b, after)))


LANES = 128


def _attn_consts():
    inv_freq = THETA ** (-jnp.arange(0, ROT, 2, dtype=F32) / ROT)
    head = jnp.concatenate([inv_freq, inv_freq, jnp.zeros((HD - ROT,), F32)])
    invf = jnp.tile(head, LANES // HD)[None, :]
    mean_of = lambda w: jnp.asarray(np.kron(np.eye(w // HD), np.full((HD, HD), 1.0 / HD)), BF16)
    return invf, mean_of(QW), mean_of(KVW)


def _rope_tables_side(pos, invf):
    B, S, _ = pos.shape
    tr = min(512, S)

    def fn(ins, outs):
        p_ref, f_ref = ins
        c_ref, s_ref = outs
        for b in range(B):
            for r in range(0, S, tr):
                ang = p_ref[b, r:r + tr, :].astype(F32) * f_ref[...]
                c_ref[b, r:r + tr, :] = jnp.cos(ang)
                s_ref[b, r:r + tr, :] = jnp.sin(ang)

    sd = jax.ShapeDtypeStruct((B, S, LANES), F32)
    return fn, (pos, invf), (sd, sd)


def _rope_expand(cos, sin, reps):
    lane = lax.broadcasted_iota(jnp.int32, cos.shape, 1) % HD
    sa = jnp.where((lane >= ROT // 2) & (lane < ROT), sin, 0.0)
    sb = jnp.where(lane < ROT // 2, -sin, 0.0)
    rep = lambda t: jnp.concatenate([t] * reps, axis=1) if reps > 1 else t
    return rep(cos), rep(sa), rep(sb)


@jax.custom_vjp
def _rope(t, cos, sa, sb):
    w = t.shape[1]
    return t * cos + pltpu.roll(t, ROT // 2, 1) * sa + pltpu.roll(t, w - ROT // 2, 1) * sb


def _rope_fwd(t, cos, sa, sb):
    return _rope(t, cos, sa, sb), (cos, sa, sb)


def _rope_bwd(res, d):
    cos, sa, sb = res
    w = d.shape[1]
    dt = d * cos + pltpu.roll(d * sa, w - ROT // 2, 1) + pltpu.roll(d * sb, ROT // 2, 1)
    return dt, jnp.zeros_like(cos), jnp.zeros_like(sa), jnp.zeros_like(sb)


_rope.defvjp(_rope_fwd, _rope_bwd)


def _head_norm(t, g, mean_of):
    hi, lo = _split(t * t)
    ms = jnp.dot(hi, mean_of, preferred_element_type=F32) + jnp.dot(lo, mean_of, preferred_element_type=F32)
    return t * lax.rsqrt(ms + EPS) * g


def _attn_block(q, kvp, kvc, qg, kg, sinks, tq, tk, mq, mk, valid):
    qn = _rope(_head_norm(q, jnp.concatenate([qg] * HQ, axis=1), mq), *tq) * (HD ** -0.5)
    kv = jnp.concatenate([kvp, kvc], axis=0)
    kn = _rope(_head_norm(kv[:, 0:KVW], jnp.concatenate([kg] * HKV, axis=1), mk), *tk)
    per_tile = LANES // HD
    vT = jnp.transpose(kv[:, KVW:2 * KVW])
    qT = [jnp.transpose(qn[:, LANES * t:LANES * (t + 1)]) for t in range(QW // LANES)]
    head_T = lambda h: qT[h // per_tile][HD * (h % per_tile):HD * (h % per_tile + 1), :]
    none = jnp.zeros((HD, GRP * BLK), F32)
    o_T = []
    for j in range(HKV):
        q4T = jnp.concatenate([head_T(GRP * j + i) for i in range(GRP)], axis=1)
        sT = _dot(kn, jnp.concatenate([q4T, none] if j == 0 else [none, q4T], axis=0))
        sT = jnp.where(valid, sT, -1e30)
        sink = jnp.concatenate([jnp.broadcast_to(sinks[:, GRP * j + i:GRP * j + i + 1], (1, BLK)) for i in range(GRP)], axis=1)
        m = lax.stop_gradient(jnp.maximum(jnp.max(sT, axis=0, keepdims=True), sink))
        pT = jnp.exp(sT - m)
        den = jnp.sum(pT, axis=0, keepdims=True) + jnp.exp(sink - m)
        oT = _dot(vT[HD * j:HD * (j + 1), :], pT) * (1.0 / den)
        o_T += [oT[:, BLK * i:BLK * (i + 1)] for i in range(GRP)]
    return jnp.concatenate([jnp.transpose(jnp.concatenate(o_T[per_tile * t:per_tile * (t + 1)], axis=0))
                            for t in range(QW // LANES)], axis=1)


def _attn_tables(cp_ref, cc_ref, sp_ref, sc_ref, n):
    tq = _rope_expand(cc_ref[...], sc_ref[...], QW // LANES)
    tk = _rope_expand(jnp.concatenate([cp_ref[...], cc_ref[...]], axis=0),
                      jnp.concatenate([sp_ref[...], sc_ref[...]], axis=0), KVW // LANES)
    qi = lax.broadcasted_iota(jnp.int32, (2 * BLK, GRP * BLK), 1) % BLK + BLK
    kj = lax.broadcasted_iota(jnp.int32, (2 * BLK, GRP * BLK), 0)
    dist = qi - kj
    valid = (dist >= 0) & (dist < BLK) & ((kj >= BLK) | (n > 0))
    return tq, tk, valid


def _attn_fwd(aq, akv, cos, sin, qg, kg, sinks, mq, mk):
    B, S, _ = aq.shape
    nb = S // BLK

    def body(q_ref, kvp_ref, kvc_ref, cp_ref, cc_ref, sp_ref, sc_ref, qg_ref, kg_ref, sk_ref, mq_ref, mk_ref, o_ref):
        tq, tk, valid = _attn_tables(cp_ref, cc_ref, sp_ref, sc_ref, pl.program_id(1))
        o_ref[...] = _attn_block(q_ref[...], kvp_ref[...], kvc_ref[...], qg_ref[...], kg_ref[...], sk_ref[...],
                                 tq, tk, mq_ref[...], mk_ref[...], valid)

    prev = lambda b, n: (b, jnp.maximum(n - 1, 0), 0)
    cur = lambda b, n: (b, n, 0)
    return pl.pallas_call(
        body, name="attn_fwd", grid=(B, nb),
        out_shape=jax.ShapeDtypeStruct((B, S, QW), F32),
        in_specs=[pl.BlockSpec((None, BLK, QW), cur), pl.BlockSpec((None, BLK, 2 * KVW), prev),
                  pl.BlockSpec((None, BLK, 2 * KVW), cur), pl.BlockSpec((None, BLK, LANES), prev),
                  pl.BlockSpec((None, BLK, LANES), cur), pl.BlockSpec((None, BLK, LANES), prev),
                  pl.BlockSpec((None, BLK, LANES), cur), _full((1, HD)), _full((1, HD)), _full((1, HQ)),
                  _full((QW, QW)), _full((KVW, KVW))],
        out_specs=pl.BlockSpec((None, BLK, QW), cur),
        compiler_params=_cparams(("parallel", "arbitrary")),
    )(aq, akv, akv, cos, cos, sin, sin, qg, kg, sinks, mq, mk)


def _attn_bwd(aq, akv, cos, sin, qg, kg, sinks, mq, mk, do):
    B, S, _ = aq.shape
    nb = S // BLK

    def body(q_ref, kvp_ref, kvc_ref, cp_ref, cc_ref, sp_ref, sc_ref, qg_ref, kg_ref, sk_ref, mq_ref, mk_ref, do_ref,
             dq_ref, dkv_ref, dqg_ref, dkg_ref, dsk_ref, carry):
        b, i = pl.program_id(0), pl.program_id(1)
        tq, tk, valid = _attn_tables(cp_ref, cc_ref, sp_ref, sc_ref, nb - 1 - i)
        fn = functools.partial(_attn_block, tq=tq, tk=tk, mq=mq_ref[...], mk=mk_ref[...], valid=valid)
        _, vjp = jax.vjp(fn, q_ref[...], kvp_ref[...], kvc_ref[...], qg_ref[...], kg_ref[...], sk_ref[...])
        dq, dkvp, dkvc, dqg, dkg, dsk = vjp(do_ref[...])

        @pl.when(i == 0)
        def _():
            carry[...] = jnp.zeros_like(carry)

        @pl.when((b == 0) & (i == 0))
        def _():
            dqg_ref[...] = jnp.zeros_like(dqg_ref)
            dkg_ref[...] = jnp.zeros_like(dkg_ref)
            dsk_ref[...] = jnp.zeros_like(dsk_ref)

        dq_ref[...] = dq.astype(BF16)
        dkv_ref[...] = (dkvc + carry[...]).astype(BF16)
        carry[...] = dkvp
        dqg_ref[...] += dqg
        dkg_ref[...] += dkg
        dsk_ref[...] += dsk

    prev = lambda b, i: (b, jnp.maximum(nb - 2 - i, 0), 0)
    cur = lambda b, i: (b, nb - 1 - i, 0)
    return pl.pallas_call(
        body, name="attn_bwd", grid=(B, nb),
        out_shape=[jax.ShapeDtypeStruct((B, S, QW), BF16), jax.ShapeDtypeStruct((B, S, 2 * KVW), BF16),
                   jax.ShapeDtypeStruct((1, HD), F32), jax.ShapeDtypeStruct((1, HD), F32),
                   jax.ShapeDtypeStruct((1, HQ), F32)],
        in_specs=[pl.BlockSpec((None, BLK, QW), cur), pl.BlockSpec((None, BLK, 2 * KVW), prev),
                  pl.BlockSpec((None, BLK, 2 * KVW), cur), pl.BlockSpec((None, BLK, LANES), prev),
                  pl.BlockSpec((None, BLK, LANES), cur), pl.BlockSpec((None, BLK, LANES), prev),
                  pl.BlockSpec((None, BLK, LANES), cur), _full((1, HD)), _full((1, HD)), _full((1, HQ)),
                  _full((QW, QW)), _full((KVW, KVW)), pl.BlockSpec((None, BLK, QW), cur)],
        out_specs=[pl.BlockSpec((None, BLK, QW), cur), pl.BlockSpec((None, BLK, 2 * KVW), cur),
                   _full((1, HD)), _full((1, HD)), _full((1, HQ))],
        scratch_shapes=[pltpu.VMEM((BLK, 2 * KVW), F32)],
        compiler_params=_cparams(("arbitrary", "arbitrary")),
    )(aq, akv, akv, cos, cos, sin, sin, qg, kg, sinks, mq, mk, do)


def _conv_taps(xe, w, rows):
    y = None
    for j in range(CONV):
        sh = pltpu.roll(xe, CONV - 1 - j, 0)[8:8 + rows, :] if j < CONV - 1 else xe[8:8 + rows, :]
        y = sh * w[j:j + 1, :] if y is None else y + sh * w[j:j + 1, :]
    return y


def _softplus(x):
    return jnp.maximum(x, 0.0) + jnp.log1p(jnp.exp(-jnp.abs(x)))


_BMM = (((2,), (1,)), ((0,), (0,)))
_BMM_NT = (((2,), (2,)), ((0,), (0,)))
_BMM_TN = (((1,), (1,)), ((0,), (0,)))


def _bmm(a, b, dims=_BMM):
    return lax.dot_general(a.astype(BF16), b.astype(BF16), dims, preferred_element_type=F32)


def _split(a):
    hi = a.astype(BF16)
    return hi, (a - hi.astype(F32)).astype(BF16)


def _bmm3(a, b, dims=_BMM):
    ah, al = _split(a)
    bh, bl = _split(b)
    d = lambda p, q: lax.dot_general(p, q, dims, preferred_element_type=F32)
    return d(ah, bh) + (d(ah, bl) + d(al, bh))


TRI_BASE = 8


def _tri_inverse(L):
    ii = lax.broadcasted_iota(jnp.int32, (CH, CH), 0)
    jj = lax.broadcasted_iota(jnp.int32, (CH, CH), 1)
    same = lambda size: (ii // size) == (jj // size)
    diag = jnp.where(same(TRI_BASE), L, 0.0)
    X = (ii == jj).astype(F32) - diag
    P = diag
    n = 2
    while n < TRI_BASE:
        P = _bmm3(P, P)
        X = X + _bmm3(X, P)
        n *= 2
    size = TRI_BASE
    while size < CH:
        joint = jnp.where(same(2 * size) & jnp.logical_not(same(size)), L, 0.0)
        X = X - _bmm3(X, _bmm3(joint, X))
        size *= 2
    return X


@jax.custom_vjp
def _tri_inverse_known(L, T):
    return T


def _tri_inverse_known_fwd(L, T):
    return T, T


def _tri_inverse_known_bwd(T, dT):
    Tt = jnp.swapaxes(T, 1, 2)
    return -_bmm(Tt, _bmm(dT, Tt)), jnp.zeros_like(T)


_tri_inverse_known.defvjp(_tri_inverse_known_fwd, _tri_inverse_known_bwd)


def _triangle(n, upper):
    ii = lax.broadcasted_iota(jnp.int32, (n, CH, CH), 1)
    jj = lax.broadcasted_iota(jnp.int32, (n, CH, CH), 2)
    return ((ii <= jj) if upper else (ii >= jj)).astype(BF16)


@jax.custom_vjp
def _cumsum_rows(g):
    g0 = g.astype(BF16)
    r1 = g - g0.astype(F32)
    g1 = r1.astype(BF16)
    g2 = (r1 - g1.astype(F32)).astype(BF16)
    tri = _triangle(g.shape[0], False)
    d = lambda q: lax.dot_general(tri, q, _BMM, preferred_element_type=F32)
    return d(g0) + (d(g1) + d(g2))


def _cumsum_rows_fwd(g):
    return _cumsum_rows(g), None


def _cumsum_rows_bwd(_, dy):
    hi, lo = _split(dy)
    tri = _triangle(dy.shape[0], True)
    d = lambda q: lax.dot_general(tri, q, _BMM, preferred_element_type=F32)
    return (d(hi) + d(lo),)


_cumsum_rows.defvjp(_cumsum_rows_fwd, _cumsum_rows_bwd)


def _row_sums(t):
    n, r, w = t.shape
    hi, lo = _split(t.reshape(n * r, w))
    ones = jnp.ones((w, w), BF16)
    s = jnp.dot(hi, ones, preferred_element_type=F32) + jnp.dot(lo, ones, preferred_element_type=F32)
    return s.reshape(n, r, w)


def _dn_prep(t_known, qr, kr, v, a_raw, b_raw, a_log, dt_b):
    n = qr.shape[0]
    ii = lax.broadcasted_iota(jnp.int32, (n, CH, CH), 1)
    jj = lax.broadcasted_iota(jnp.int32, (n, CH, CH), 2)
    incl, strict = ii >= jj, ii > jj
    q = qr * lax.rsqrt(_row_sums(qr * qr) + EPS) * (DK ** -0.5)
    k = kr * lax.rsqrt(_row_sums(kr * kr) + EPS)
    beta = _sigmoid(b_raw)
    g = -jnp.exp(a_log) * _softplus(a_raw + dt_b)
    gcb = _cumsum_rows(jnp.broadcast_to(g, (n, CH, DK)))
    gc = gcb[:, :, 0:1]
    gc_row = jnp.swapaxes(gcb, 1, 2)[:, 0:1, 0:CH]
    decay = jnp.where(incl, jnp.exp(jnp.where(incl, gc - gc_row, 0.0)), 0.0)
    kb = k * beta
    L = jnp.where(strict, _bmm(kb, k, _BMM_NT) * decay, 0.0)
    T = _tri_inverse(L) if t_known is None else _tri_inverse_known(L, t_known)
    eg = jnp.exp(gc)
    u = _bmm(T, v * beta)
    w = _bmm(T, kb * eg)
    a_in = _bmm(q, k, _BMM_NT) * decay
    g_last = gc[:, CH - 1:CH, :]
    return u, w, q * eg, k * jnp.exp(g_last - gc), a_in, jnp.exp(g_last), T


def _dn_step(S0, u, w, qd, kd, a_in, cd):
    r = _bmm(jnp.concatenate([w, qd], axis=1), S0)
    v_new = u - r[:, 0:CH, :]
    o = r[:, CH:2 * CH, :] + _bmm(a_in, v_new)
    S1 = S0 * cd + _bmm(kd, v_new, _BMM_TN)
    return o, S1


def _dn_stack(cq, ba, al, dt, G):
    cols = [[] for _ in range(7)]
    for c in range(G):
        rows = slice(CH * c, CH * (c + 1))
        for h in range(DH):
            parts = (cq[rows, DK * h:DK * (h + 1)], cq[rows, DNW + DK * h:DNW + DK * (h + 1)],
                     cq[rows, 2 * DNW + DK * h:2 * DNW + DK * (h + 1)], ba[rows, DH + h:DH + h + 1],
                     ba[rows, h:h + 1], al[:, h:h + 1], dt[:, h:h + 1])
            for col, p in zip(cols, parts):
                col.append(p)
    return tuple(jnp.stack(col) for col in cols)


def _dn_group(S, want):
    g = want
    while (S // CH) % g:
        g //= 2
    return g


def _dn_prep_fwd(xin, conv_w, ba, a_log, dt_b):
    B, S, _ = xin.shape
    nc = S // CH
    G = _dn_group(S, 8)
    r8 = G * CH // 8

    def body(xp_ref, x_ref, cw_ref, ba_ref, al_ref, dt_ref, cq_ref, u_ref, w_ref, qd_ref, kd_ref, a_ref, t_ref, cd_ref):
        xp = jnp.where(pl.program_id(1) > 0, xp_ref[...], 0.0)
        cq = _silu(_conv_taps(jnp.concatenate([xp, x_ref[...]], axis=0), cw_ref[...], G * CH))
        cq_ref[...] = cq
        ops = _dn_stack(cq, ba_ref[...], al_ref[...], dt_ref[...], G)
        u, w, qd, kd, a_in, cd, T = _dn_prep(None, *ops)
        lane4 = lax.broadcasted_iota(jnp.int32, (1, DH), 1)
        for c in range(G):
            rows = slice(CH * c, CH * (c + 1))
            cdrow = jnp.zeros((1, DH), F32)
            for h in range(DH):
                n = DH * c + h
                lanes = slice(DK * h, DK * (h + 1))
                u_ref[rows, lanes] = u[n]
                w_ref[rows, lanes] = w[n]
                qd_ref[rows, lanes] = qd[n]
                kd_ref[rows, lanes] = kd[n]
                a_ref[rows, CH * h:CH * (h + 1)] = a_in[n]
                t_ref[rows, CH * h:CH * (h + 1)] = T[n]
                cdrow = cdrow + jnp.where(lane4 == h, cd[n], 0.0)
            cd_ref[c] = cdrow

    wide = jax.ShapeDtypeStruct((B, S, DNW), F32)
    sq = jax.ShapeDtypeStruct((B, S, DH * CH), F32)
    return pl.pallas_call(
        body, name="dn_prep_fwd", grid=(B, nc // G),
        out_shape=[jax.ShapeDtypeStruct((B, S, CONVW), F32), wide, wide, wide, wide, sq, sq,
                   jax.ShapeDtypeStruct((B, nc, 1, DH), F32)],
        in_specs=[pl.BlockSpec((None, 8, CONVW), lambda b, i: (b, jnp.maximum(i * r8 - 1, 0), 0)),
                  _rows(G * CH, CONVW), _full((CONV, CONVW)), _rows(G * CH, 2 * DH), _full((1, DH)), _full((1, DH))],
        out_specs=[_rows(G * CH, CONVW)] + [_rows(G * CH, DNW)] * 4 + [_rows(G * CH, DH * CH)] * 2
                  + [pl.BlockSpec((None, G, 1, DH), lambda b, i: (b, i, 0, 0))],
        compiler_params=_cparams(("parallel", "arbitrary")),
    )(xin, xin, conv_w, ba, a_log, dt_b)


def _dn_seq_specs(B, steps, gs, rev):
    at = (lambda i: steps - 1 - i) if rev else (lambda i: i)
    wide = pl.BlockSpec((B, gs * CH, DNW), lambda i: (0, at(i), 0))
    a_spec = pl.BlockSpec((B, gs * CH, DH * CH), lambda i: (0, at(i), 0))
    cd_spec = pl.BlockSpec((B, gs, 1, DH), lambda i: (0, at(i), 0, 0))
    st_spec = pl.BlockSpec((B, gs, DH, DK, DK), lambda i: (0, at(i), 0, 0, 0))
    return wide, a_spec, cd_spec, st_spec


def _dn_step_operands(B, c, u_ref, w_ref, qd_ref, kd_ref, a_ref, cd_ref):
    pairs = [(b, h) for b in range(B) for h in range(DH)]
    rows = slice(CH * c, CH * (c + 1))
    wide = lambda ref: jnp.stack([ref[b, rows, DK * h:DK * (h + 1)] for b, h in pairs])
    a_in = jnp.stack([a_ref[b, rows, CH * h:CH * (h + 1)] for b, h in pairs])
    cd = jnp.stack([cd_ref[b, c, :, h:h + 1] for b, h in pairs])
    return wide(u_ref), wide(w_ref), wide(qd_ref), wide(kd_ref), a_in, cd


def _dn_seq_fwd(u, w, qd, kd, a_in, cd):
    B, S, _ = u.shape
    nc = S // CH
    gs = _dn_group(S, 8)

    def body(u_ref, w_ref, qd_ref, kd_ref, a_ref, cd_ref, o_ref, st_ref, state):
        @pl.when(pl.program_id(0) == 0)
        def _():
            state[...] = jnp.zeros_like(state)

        S0 = state[...]
        for c in range(gs):
            for b in range(B):
                st_ref[b, c] = S0[DH * b:DH * (b + 1)]
            o, S0 = _dn_step(S0, *_dn_step_operands(B, c, u_ref, w_ref, qd_ref, kd_ref, a_ref, cd_ref))
            for b in range(B):
                for h in range(DH):
                    o_ref[b, CH * c:CH * (c + 1), DK * h:DK * (h + 1)] = o[DH * b + h]
        state[...] = S0

    wide, a_spec, cd_spec, st_spec = _dn_seq_specs(B, nc // gs, gs, False)
    return pl.pallas_call(
        body, name="dn_seq_fwd", grid=(nc // gs,),
        out_shape=[jax.ShapeDtypeStruct((B, S, DNW), F32), jax.ShapeDtypeStruct((B, nc, DH, DK, DK), F32)],
        in_specs=[wide, wide, wide, wide, a_spec, cd_spec],
        out_specs=[wide, st_spec],
        scratch_shapes=[pltpu.VMEM((B * DH, DK, DK), F32)],
        compiler_params=_cparams(("arbitrary",)),
    )(u, w, qd, kd, a_in, cd)


def _dn_seq_bwd(u, w, qd, kd, a_in, cd, states, do):
    B, S, _ = u.shape
    nc = S // CH
    gs = _dn_group(S, 4)

    def body(u_ref, w_ref, qd_ref, kd_ref, a_ref, cd_ref, st_ref, do_ref,
             du_ref, dw_ref, dqd_ref, dkd_ref, da_ref, dcd_ref, dstate):
        @pl.when(pl.program_id(0) == 0)
        def _():
            dstate[...] = jnp.zeros_like(dstate)

        lane4 = lax.broadcasted_iota(jnp.int32, (1, DH), 1)
        dS = dstate[...]
        for c in reversed(range(gs)):
            rows = slice(CH * c, CH * (c + 1))
            S0 = jnp.concatenate([st_ref[b, c] for b in range(B)], axis=0)
            do = jnp.stack([do_ref[b, rows, DK * h:DK * (h + 1)] for b in range(B) for h in range(DH)])
            _, vjp = jax.vjp(_dn_step, S0, *_dn_step_operands(B, c, u_ref, w_ref, qd_ref, kd_ref, a_ref, cd_ref))
            dS, du, dw, dqd, dkd, da, dcd = vjp((do, dS))
            for b in range(B):
                dcdrow = jnp.zeros((1, DH), F32)
                for h in range(DH):
                    n = DH * b + h
                    lanes = slice(DK * h, DK * (h + 1))
                    du_ref[b, rows, lanes] = du[n]
                    dw_ref[b, rows, lanes] = dw[n]
                    dqd_ref[b, rows, lanes] = dqd[n]
                    dkd_ref[b, rows, lanes] = dkd[n]
                    da_ref[b, rows, CH * h:CH * (h + 1)] = da[n]
                    dcdrow = dcdrow + jnp.where(lane4 == h, dcd[n], 0.0)
                dcd_ref[b, c] = dcdrow
        dstate[...] = dS

    wide, a_spec, cd_spec, st_spec = _dn_seq_specs(B, nc // gs, gs, True)
    sd = jax.ShapeDtypeStruct((B, S, DNW), F32)
    return pl.pallas_call(
        body, name="dn_seq_bwd", grid=(nc // gs,),
        out_shape=[sd, sd, sd, sd, jax.ShapeDtypeStruct((B, S, DH * CH), F32), jax.ShapeDtypeStruct((B, nc, 1, DH), F32)],
        in_specs=[wide, wide, wide, wide, a_spec, cd_spec, st_spec, wide],
        out_specs=[wide, wide, wide, wide, a_spec, cd_spec],
        scratch_shapes=[pltpu.VMEM((B * DH, DK, DK), F32)],
        compiler_params=_cparams(("arbitrary",)),
    )(u, w, qd, kd, a_in, cd, states, do)


def _dn_prep_bwd(xin, conv_w, cq, ba, a_log, dt_b, t_inv, du, dw, dqd, dkd, da, dcd):
    B, S, _ = cq.shape
    nc = S // CH
    G = _dn_group(S, 8)
    R = G * CH
    nblk = nc // G
    r8 = R // 8

    def body(xp_ref, x_ref, cw_ref, cq_ref, ba_ref, al_ref, dt_ref, t_ref, du_ref, dw_ref, dqd_ref, dkd_ref, da_ref, dcd_ref,
             dx_ref, dcw_ref, dba_ref, dal_ref, ddt_ref, carry):
        i = pl.program_id(1)

        @pl.when((pl.program_id(0) == 0) & (i == 0))
        def _():
            dal_ref[...] = jnp.zeros_like(dal_ref)
            ddt_ref[...] = jnp.zeros_like(ddt_ref)
            dcw_ref[...] = jnp.zeros_like(dcw_ref)

        @pl.when(i == 0)
        def _():
            carry[...] = jnp.zeros_like(carry)

        pairs = [(c, h) for c in range(G) for h in range(DH)]
        rows = lambda c: slice(CH * c, CH * (c + 1))
        wide = lambda ref: jnp.stack([ref[rows(c), DK * h:DK * (h + 1)] for c, h in pairs])
        square = lambda ref: jnp.stack([ref[rows(c), CH * h:CH * (h + 1)] for c, h in pairs])
        ops = _dn_stack(cq_ref[...], ba_ref[...], al_ref[...], dt_ref[...], G)
        cots = (wide(du_ref), wide(dw_ref), wide(dqd_ref), wide(dkd_ref), square(da_ref),
                jnp.stack([dcd_ref[c][:, h:h + 1] for c, h in pairs]), jnp.zeros((len(pairs), CH, CH), F32))
        _, vjp = jax.vjp(functools.partial(_dn_prep, square(t_ref)), *ops)
        dq, dk, dv, dar, dbr, dl, dd = vjp(cots)
        lane8 = lax.broadcasted_iota(jnp.int32, (CH, 2 * DH), 1)
        lane4 = lax.broadcasted_iota(jnp.int32, (1, DH), 1)
        dal = jnp.zeros((1, DH), F32)
        ddt = jnp.zeros((1, DH), F32)
        for c in range(G):
            dba = jnp.zeros((CH, 2 * DH), F32)
            for h in range(DH):
                n = DH * c + h
                dba = dba + jnp.where(lane8 == h, dbr[n], 0.0) + jnp.where(lane8 == DH + h, dar[n], 0.0)
                dal = dal + jnp.where(lane4 == h, dl[n], 0.0)
                ddt = ddt + jnp.where(lane4 == h, dd[n], 0.0)
            dba_ref[rows(c), :] = dba.astype(BF16)
        dal_ref[...] += dal
        ddt_ref[...] += ddt

        dcq = jnp.concatenate([jnp.concatenate([t[DH * c + h] for t in (dq, dk, dv) for h in range(DH)], axis=1)
                               for c in range(G)], axis=0)
        w = cw_ref[...]
        xp = jnp.where(i < nblk - 1, xp_ref[...], 0.0)
        xe = jnp.concatenate([xp, x_ref[...]], axis=0)
        taps = [(pltpu.roll(xe, CONV - 1 - j, 0) if j < CONV - 1 else xe)[8:8 + R, :] for j in range(CONV)]
        pre = sum(t * w[j:j + 1, :] for j, t in enumerate(taps))
        sg = _sigmoid(pre)
        dpre = dcq * (sg * (1.0 + pre * (1.0 - sg)))
        ext = jnp.concatenate([dpre, carry[...]], axis=0)
        dx = dpre * w[CONV - 1:CONV, :]
        for j in range(CONV - 1):
            dx = dx + pltpu.roll(ext, R + 8 - (CONV - 1 - j), 0)[0:R, :] * w[j:j + 1, :]
        dx_ref[...] = dx.astype(BF16)
        carry[...] = dpre[0:8, :]
        lane_row = lax.broadcasted_iota(jnp.int32, (CONV, CONVW), 0)
        dcw = jnp.zeros((CONV, CONVW), F32)
        for j in range(CONV):
            dcw = dcw + jnp.where(lane_row == j, jnp.sum(taps[j] * dpre, axis=0, keepdims=True), 0.0)
        dcw_ref[...] += dcw

    rev = lambda w: pl.BlockSpec((None, R, w), lambda b, i: (b, nblk - 1 - i, 0))
    return pl.pallas_call(
        body, name="dn_prep_bwd", grid=(B, nblk),
        out_shape=[jax.ShapeDtypeStruct((B, S, CONVW), BF16), jax.ShapeDtypeStruct((CONV, CONVW), F32),
                   jax.ShapeDtypeStruct((B, S, 2 * DH), BF16), jax.ShapeDtypeStruct((1, DH), F32),
                   jax.ShapeDtypeStruct((1, DH), F32)],
        in_specs=[pl.BlockSpec((None, 8, CONVW), lambda b, i: (b, jnp.maximum((nblk - 1 - i) * r8 - 1, 0), 0)),
                  rev(CONVW), _full((CONV, CONVW)), rev(CONVW), rev(2 * DH), _full((1, DH)), _full((1, DH)), rev(DH * CH)]
                 + [rev(DNW)] * 4 + [rev(DH * CH), pl.BlockSpec((None, G, 1, DH), lambda b, i: (b, nblk - 1 - i, 0, 0))],
        out_specs=[rev(CONVW), _full((CONV, CONVW)), rev(2 * DH), _full((1, DH)), _full((1, DH))],
        scratch_shapes=[pltpu.VMEM((8, CONVW), F32)],
        compiler_params=_cparams(("arbitrary", "arbitrary")),
    )(xin, xin, conv_w, cq, ba, a_log, dt_b, t_inv, du, dw, dqd, dkd, da, dcd)


def _gated_norm(o, z, g):
    outs = []
    for h in range(DH):
        t = o[:, DK * h:DK * (h + 1)]
        r = lax.rsqrt(jnp.mean(t * t, axis=-1, keepdims=True) + EPS)
        outs.append(t * r * g * _silu(z[:, DK * h:DK * (h + 1)]))
    return jnp.concatenate(outs, axis=1)


def _mix_fwd(x, o_attn, o_dn, z, ga, gd, mod, dn_g, w_branch, w_out):
    B, S, _ = x.shape
    tm = _tile(S, 512)

    def body(x_ref, oa_ref, od_ref, z_ref, ga_ref, gd_ref, mod_ref, g_ref, wb_ref, wo_ref,
             x1_ref, mix_ref, mg_ref, ob_ref):
        oa = oa_ref[...].astype(BF16)
        od = _gated_norm(od_ref[...], z_ref[...], g_ref[...]).astype(BF16)
        ob_ref[0] = oa
        ob_ref[1] = od
        ya = jnp.dot(oa, wb_ref[0:QW, :], preferred_element_type=F32)
        yd = jnp.dot(od, wb_ref[QW:QW + DNW, :], preferred_element_type=F32)
        merged = (_sigmoid(ga_ref[...]) * ya + _sigmoid(gd_ref[...]) * yd).astype(BF16)
        mg_ref[...] = merged
        mix = jnp.dot(merged, wo_ref[...], preferred_element_type=F32)
        mix_ref[...] = mix
        x1_ref[...] = x_ref[...] + mod_ref[2:3, :] * mix

    return pl.pallas_call(
        body, name="mix_fwd", grid=(B, S // tm),
        out_shape=[jax.ShapeDtypeStruct((B, S, D), F32), jax.ShapeDtypeStruct((B, S, D), F32),
                   jax.ShapeDtypeStruct((B, S, D), BF16), jax.ShapeDtypeStruct((B, 2, S, QW), BF16)],
        in_specs=[_rows(tm, D), _rows(tm, QW), _rows(tm, DNW), _rows(tm, DNW), _rows(tm, D), _rows(tm, D),
                  _perb(6, D), _full((1, DK)), _resident(w_branch.shape), _resident(w_out.shape)],
        out_specs=[_rows(tm, D), _rows(tm, D), _rows(tm, D), _stacked(2, tm, QW)],
        compiler_params=_cparams(("parallel", "arbitrary")),
    )(x, o_attn, o_dn, z, ga, gd, mod, dn_g, w_branch, w_out)


def _mix_bwd(dx1, mix, o_attn, o_dn, z, ga, gd, mod, dn_g, w_branch, w_out):
    B, S, _ = dx1.shape
    tm = _tile(S, 512)

    def body(dx1_ref, mix_ref, oa_ref, od_ref, z_ref, ga_ref, gd_ref, mod_ref, g_ref, wb_ref, wo_ref,
             dmix_ref, dyo_ref, dga_ref, dgd_ref, dz_ref, doa_ref, dod_ref, dgate_ref, dg_ref):
        b, i = pl.program_id(0), pl.program_id(1)
        dx1 = dx1_ref[...]
        dmix = (dx1 * mod_ref[2:3, :]).astype(BF16)
        dmix_ref[...] = dmix
        dgate = jnp.sum(dx1 * mix_ref[...], axis=0, keepdims=True)
        dmerged = _dot_nt(dmix, wo_ref[...])
        odn, gn_vjp = jax.vjp(_gated_norm, od_ref[...], z_ref[...], g_ref[...])
        ya = _dot(oa_ref[...], wb_ref[0:QW, :])
        yd = _dot(odn, wb_ref[QW:QW + DNW, :])
        sa, sd = _sigmoid(ga_ref[...]), _sigmoid(gd_ref[...])
        dya = (dmerged * sa).astype(BF16)
        dyd = (dmerged * sd).astype(BF16)
        dyo_ref[0] = dya
        dyo_ref[1] = dyd
        dga_ref[...] = (dmerged * ya * sa * (1.0 - sa)).astype(BF16)
        dgd_ref[...] = (dmerged * yd * sd * (1.0 - sd)).astype(BF16)
        doa_ref[...] = _dot_nt(dya, wb_ref[0:QW, :])
        dodn = _dot_nt(dyd, wb_ref[QW:QW + DNW, :])
        dod, dz, dg = gn_vjp(dodn)
        dod_ref[...] = dod
        dz_ref[...] = dz.astype(BF16)

        @pl.when(i == 0)
        def _():
            dgate_ref[...] = jnp.zeros_like(dgate_ref)

        @pl.when((b == 0) & (i == 0))
        def _():
            dg_ref[...] = jnp.zeros_like(dg_ref)

        dgate_ref[...] += dgate
        dg_ref[...] += dg

    return pl.pallas_call(
        body, name="mix_bwd", grid=(B, S // tm),
        out_shape=[jax.ShapeDtypeStruct((B, S, D), BF16), jax.ShapeDtypeStruct((B, 2, S, D), BF16),
                   jax.ShapeDtypeStruct((B, S, D), BF16), jax.ShapeDtypeStruct((B, S, D), BF16),
                   jax.ShapeDtypeStruct((B, S, DNW), BF16),
                   jax.ShapeDtypeStruct((B, S, QW), F32), jax.ShapeDtypeStruct((B, S, DNW), F32),
                   jax.ShapeDtypeStruct((B, 1, D), F32), jax.ShapeDtypeStruct((1, DK), F32)],
        in_specs=[_rows(tm, D), _rows(tm, D), _rows(tm, QW), _rows(tm, DNW), _rows(tm, DNW), _rows(tm, D),
                  _rows(tm, D), _perb(6, D), _full((1, DK)), _resident(w_branch.shape), _resident(w_out.shape)],
        out_specs=[_rows(tm, D), _stacked(2, tm, D), _rows(tm, D), _rows(tm, D), _rows(tm, DNW),
                   _rows(tm, QW), _rows(tm, DNW), _perb(1, D), _full((1, DK))],
        compiler_params=_cparams(("arbitrary", "arbitrary")),
    )(dx1, mix, o_attn, o_dn, z, ga, gd, mod, dn_g, w_branch, w_out)


GU_SHARD = 2 * FFN // N_DEV
GU_HALF = N_DEV // 2


def _ffn1_fwd(x1, mod, g2, w_gu):
    B, S, _ = x1.shape
    tm = _tile(S)

    def body(x_ref, mod_ref, g_ref, w_ref, h_ref, dgate_ref, dup_ref, act_ref):
        h = _rms_mod(x_ref[...], g_ref[...], mod_ref[4:5, :], mod_ref[3:4, :]).astype(BF16)
        h_ref[...] = h
        for j in range(GU_HALF):
            gate = _dot_nt(h, w_ref[j])
            up = _dot_nt(h, w_ref[GU_HALF + j])
            sg = _sigmoid(gate)
            silu = gate * sg
            dgate_ref[j] = up * (sg * (1.0 + gate * (1.0 - sg)))
            dup_ref[j] = silu
            act_ref[j] = (silu * up).astype(BF16)

    blk = lambda dt: jax.ShapeDtypeStruct((B, GU_HALF, S, GU_SHARD), dt)
    return pl.pallas_call(
        body, name="ffn1_fwd", grid=(B, S // tm),
        out_shape=[jax.ShapeDtypeStruct((B, S, D), BF16), blk(F32), blk(F32), blk(BF16)],
        in_specs=[_rows(tm, D), _perb(6, D), _full((1, D)), _resident(w_gu.shape)],
        out_specs=[_rows(tm, D)] + [_stacked(GU_HALF, tm, GU_SHARD)] * 3,
        compiler_params=_cparams(("parallel", "arbitrary")),
    )(x1, mod, g2, w_gu)


def _ffn2_fwd(act, x1, target, mod, w_down):
    B, S, _ = x1.shape
    tm = _tile(S, 512)

    def body(a_ref, x_ref, t_ref, mod_ref, w_ref, dy_ref, loss_ref, dgate_ref):
        b, i = pl.program_id(0), pl.program_id(1)
        y = jnp.dot(a_ref[0], w_ref[0], preferred_element_type=F32)
        for j in range(1, GU_HALF):
            y = y + jnp.dot(a_ref[j], w_ref[j], preferred_element_type=F32)
        err = x_ref[...] + mod_ref[5:6, :] * y - t_ref[...]
        dy = err * (1.0 / D)
        dy_ref[...] = dy

        @pl.when((b == 0) & (i == 0))
        def _():
            loss_ref[...] = jnp.zeros_like(loss_ref)

        @pl.when(i == 0)
        def _():
            dgate_ref[...] = jnp.zeros_like(dgate_ref)

        loss_ref[...] += (0.5 / D) * jnp.sum(err * err)
        dgate_ref[...] += jnp.sum(dy * y, axis=0, keepdims=True)

    return pl.pallas_call(
        body, name="ffn2_fwd", grid=(B, S // tm),
        out_shape=[jax.ShapeDtypeStruct((B, S, D), F32), jax.ShapeDtypeStruct((1, 128), F32),
                   jax.ShapeDtypeStruct((B, 1, D), F32)],
        in_specs=[_stacked(GU_HALF, tm, GU_SHARD), _rows(tm, D), _rows(tm, D), _perb(6, D), _resident(w_down.shape)],
        out_specs=[_rows(tm, D), _full((1, 128)), _perb(1, D)],
        compiler_params=_cparams(("arbitrary", "arbitrary")),
    )(act, x1, target, mod, w_down)


def _ffn2_bwd(dy, act_dgate, act_dup, mod, w_down):
    B, S, _ = dy.shape
    tm = _tile(S, 512)

    def body(dy_ref, dgate_ref, dup_ref, mod_ref, w_ref, dgu_ref, dyg_ref):
        dyg = (dy_ref[...] * mod_ref[5:6, :]).astype(BF16)
        dyg_ref[...] = dyg
        for j in range(GU_HALF):
            dact = _dot_nt(dyg, w_ref[j])
            dgu_ref[j] = (dact * dgate_ref[j]).astype(BF16)
            dgu_ref[GU_HALF + j] = (dact * dup_ref[j]).astype(BF16)

    return pl.pallas_call(
        body, name="ffn2_bwd", grid=(B, S // tm),
        out_shape=[jax.ShapeDtypeStruct((B, N_DEV, S, GU_SHARD), BF16), jax.ShapeDtypeStruct((B, S, D), BF16)],
        in_specs=[_rows(tm, D), _stacked(GU_HALF, tm, GU_SHARD), _stacked(GU_HALF, tm, GU_SHARD), _perb(6, D),
                  _resident(w_down.shape)],
        out_specs=[_stacked(N_DEV, tm, GU_SHARD), _rows(tm, D)],
        compiler_params=_cparams(("parallel", "arbitrary")),
    )(dy, act_dgate, act_dup, mod, w_down)


def _ffn1_bwd(dgu, x1, dy, mod, g2, w_gu):
    B, S, _ = x1.shape
    tm = _tile(S, 512)

    def body(dgu_ref, x_ref, dy_ref, mod_ref, g_ref, w_ref, dx1_ref, dg_ref, dsc_ref, dsh_ref):
        b, i = pl.program_id(0), pl.program_id(1)
        dh = jnp.dot(dgu_ref[0], w_ref[0], preferred_element_type=F32)
        for j in range(1, N_DEV):
            dh = dh + jnp.dot(dgu_ref[j], w_ref[j], preferred_element_type=F32)
        _, vjp = jax.vjp(_rms_mod, x_ref[...], g_ref[...], mod_ref[4:5, :], mod_ref[3:4, :])
        dx, dg, dsc, dsh = vjp(dh)
        dx1_ref[...] = dy_ref[...] + dx

        @pl.when((b == 0) & (i == 0))
        def _():
            dg_ref[...] = jnp.zeros_like(dg_ref)

        @pl.when(i == 0)
        def _():
            dsc_ref[...] = jnp.zeros_like(dsc_ref)
            dsh_ref[...] = jnp.zeros_like(dsh_ref)

        dg_ref[...] += dg
        dsc_ref[...] += dsc
        dsh_ref[...] += dsh

    return pl.pallas_call(
        body, name="ffn1_bwd", grid=(B, S // tm),
        out_shape=[jax.ShapeDtypeStruct((B, S, D), F32), jax.ShapeDtypeStruct((1, D), F32),
                   jax.ShapeDtypeStruct((B, 1, D), F32), jax.ShapeDtypeStruct((B, 1, D), F32)],
        in_specs=[_stacked(N_DEV, tm, GU_SHARD), _rows(tm, D), _rows(tm, D), _perb(6, D), _full((1, D)),
                  _resident(w_gu.shape)],
        out_specs=[_rows(tm, D), _full((1, D)), _perb(1, D), _perb(1, D)],
        compiler_params=_cparams(("arbitrary", "arbitrary")),
    )(dgu, x1, dy, mod, g2, w_gu)


def _adamw(w, g, m, v, name):
    def body(w_ref, g_ref, m_ref, v_ref, d_ref, nm_ref, nv_ref):
        d_ref[...], nm_ref[...], nv_ref[...] = _adamw_math(w_ref[...], g_ref[...], m_ref[...], v_ref[...])

    sd = jax.ShapeDtypeStruct(w.shape, F32)
    return pl.pallas_call(body, name=name, out_shape=(sd, sd, sd), compiler_params=_cparams())(w, g, m, v)


def kernel(x, c, positions, ada_w, ada_b, norm1_g, w_in, conv_w, q_norm_g, k_norm_g, sinks, a_log, dt_bias, dn_norm_g, w_branch, w_out, norm2_g, w_gate_up, w_down, loss_target, m_ada_w, m_ada_b, m_norm1_g, m_w_in, m_conv_w, m_q_norm_g, m_k_norm_g, m_sinks, m_a_log, m_dt_bias, m_dn_norm_g, m_w_branch, m_w_out, m_norm2_g, m_w_gate_up, m_w_down, v_ada_w, v_ada_b, v_norm1_g, v_w_in, v_conv_w, v_q_norm_g, v_k_norm_g, v_sinks, v_a_log, v_dt_bias, v_dn_norm_g, v_w_branch, v_w_out, v_norm2_g, v_w_gate_up, v_w_down):
    B, S, _ = x.shape
    me = 4 * lax.axis_index("x") + 2 * lax.axis_index("y") + lax.axis_index("c")

    tr = lambda t: jnp.swapaxes(t, 1, 2)
    shards = [w[0].astype(BF16) for w in (tr(w_in), w_branch, w_out, tr(w_gate_up), w_down)]

    conv2 = conv_w.reshape(CONV, CONVW // N_DEV)
    first = jnp.concatenate([c, jnp.pad(conv2, ((0, 8 - B - CONV), (0, D - CONVW // N_DEV)))], axis=0)
    first_all = _all_gather_small(first, "gather_c")
    c_all = first_all[:, 0:B, :].reshape(N_DEV * B, D)
    conv_all = first_all[:, B:B + CONV, 0:CONVW // N_DEV].transpose(1, 0, 2).reshape(CONV, CONVW)
    ncol = 6 * D // N_DEV
    mod_cols, cond_all = _ada_fwd(c_all, ada_w[0], lax.dynamic_slice(ada_b, (0, me * ncol), (1, ncol)))
    mod_all = _all_gather_small(mod_cols, "gather_mod").transpose(1, 0, 2).reshape(N_DEV * B, 6 * D)
    mod = lax.dynamic_slice(mod_all, (me * B, 0), (B, 6 * D)).reshape(B, 6, D)

    invf, mean_q, mean_k = _attn_consts()
    w_in_b, rope_cos, rope_sin = _all_gather_big(shards[:1], "gather_w_in", after=(mod, conv_all),
                                                 side=_rope_tables_side(positions.reshape(B, S, 1), invf))
    w_sems, w_srcs, w_lands, w_token = _copies_start(shards[1:], [_place_own(s, me) for s in shards[1:]], False,
                                                    w_in_b, "gather_rest_start")

    w_in_t = w_in_b.reshape(IN_W, D)
    h1, aq, akv, dnx, ba, z, ga, gd = _inproj_fwd(x, mod, norm1_g + w_token[0, 0], w_in_t)
    o_attn = _attn_fwd(aq, akv, rope_cos, rope_sin, q_norm_g, k_norm_g, sinks, mean_q, mean_k)
    cq, dn_u, dn_w, dn_qd, dn_kd, dn_a, dn_t, dn_cd = _dn_prep_fwd(dnx, conv_all, ba, a_log, dt_bias)
    o_dn, states = _dn_seq_fwd(dn_u, dn_w, dn_qd, dn_kd, dn_a, dn_cd)
    w_branch_g, w_out_g, w_gu_b, w_down_g = _copies_wait(w_sems, w_srcs, w_lands, o_dn, "gather_wait_rest")
    w_branch_f = w_branch_g.reshape(D, D)
    w_out_f = w_out_g.reshape(D, D)
    w_down_b = w_down_g.reshape(GU_HALF, GU_SHARD, D)
    x1, mix, merged, ob = _mix_fwd(x, o_attn, o_dn, z, ga, gd, mod, dn_norm_g, w_branch_f, w_out_f)
    h2, act_dgate, act_dup, act = _ffn1_fwd(x1, mod, norm2_g, w_gu_b)
    dy, loss_part, d_gate2 = _ffn2_fwd(act, x1, loss_target, mod, w_down_b)

    one = lambda t: t.reshape(B, 1, S, t.shape[-1])
    dgu, dyg = _ffn2_bwd(dy, act_dgate, act_dup, mod, w_down_b)
    g_w_down = _wgrad(act, one(dyg), "wgrad_down")
    dx1, d_n2g, d_scale2, d_shift2 = _ffn1_bwd(dgu, x1, dy, mod, norm2_g, w_gu_b)
    g_w_gu = _wgrad(dgu, one(h2), "wgrad_gate_up")
    ffn = _exchange_start([g_w_gu, g_w_down.reshape(N_DEV, FFN // N_DEV, D)], me, dx1, "exchange_ffn_start")
    dmix, dyo, dga, dgd, dz, d_oa, d_od, d_gate1, d_dng = _mix_bwd(
        dx1, mix, o_attn, o_dn, z, ga, gd, mod, dn_norm_g + ffn[3][0, 0], w_branch_f, w_out_f)
    d_dn = _dn_seq_bwd(dn_u, dn_w, dn_qd, dn_kd, dn_a, dn_cd, states, d_od)
    ddnx, d_conv, dba, d_alog, d_dtb = _dn_prep_bwd(dnx, conv_all, cq, ba, a_log, dt_bias, dn_t, *d_dn)
    daq, dakv, d_qg, d_kg, d_sinks = _attn_bwd(aq, akv, rope_cos, rope_sin, q_norm_g, k_norm_g, sinks, mean_q, mean_k, d_oa)
    dps = [daq, dakv, ddnx, dba, dz, dga, dgd]
    dblk, grad_x, d_n1g, d_scale1, d_shift1 = _inproj_bwd(x, mod, norm1_g, dx1, dps, w_in_t)

    dmod = jnp.concatenate([d_shift1, d_scale1, d_gate1, d_shift2, d_scale2, d_gate2], axis=2).reshape(B, 6 * D)
    small = jnp.concatenate([d_n1g, d_qg, d_kg, d_sinks, d_alog, d_dtb, d_dng, d_n2g, d_conv.reshape(1, CONV * CONVW)], axis=1)
    nsm = small.shape[1]
    width = -(-max(6 * D, nsm) // 128) * 128
    tail = jnp.pad(loss_part[:, 0:1], ((0, 8 - B - 2), (0, width - 1)))
    rows = jnp.concatenate([jnp.pad(dmod, ((0, 0), (0, width - 6 * D))), jnp.pad(small, ((0, 0), (0, width - nsm))), tail], axis=0)
    rows_all = _all_gather_small(rows, "gather_small")
    loss = jnp.sum(rows_all[:, B + 1, 0])
    dmod_all = rows_all[:, 0:B, 0:6 * D].reshape(N_DEV * B, 6 * D)
    dmod_cols = lax.dynamic_slice(dmod_all, (0, me * ncol), (N_DEV * B, ncol))
    grad_ada_w, grad_ada_b, small_sum = _ada_bwd(cond_all, dmod_all, dmod_cols, rows_all[:, B, :])
    sizes = [D, HD, HD, HQ, DH, DH, DK, D]
    so = np.cumsum([0] + sizes)
    g_n1, g_qg, g_kg, g_sk, g_al, g_dt, g_dn, g_n2 = [small_sum[:, so[i]:so[i + 1]] for i in range(8)]
    g_conv_all = small_sum[:, so[8]:so[8] + CONV * CONVW].reshape(CONV, N_DEV, CONVW // N_DEV)
    grad_conv = lax.dynamic_slice(g_conv_all, (0, me, 0), (CONV, 1, CONVW // N_DEV)).reshape(CONV, CONVW // N_DEV)

    half = D // 2
    g_w_in_a = _wgrad(dblk, one(h1), "wgrad_in_a", after=small_sum, b_lanes=(0, half))
    proj_a = _exchange_start([g_w_in_a], me, small_sum, "exchange_in_a_start")
    g_w_in_b = _wgrad(dblk, one(h1), "wgrad_in_b", after=proj_a[3], b_lanes=(1, half))
    proj = _exchange_start([g_w_in_b], me, proj_a[3], "exchange_in_b_start")
    g_w_out = _wgrad(one(merged), one(dmix), "wgrad_out", after=proj[3])
    g_w_branch = _wgrad(ob, dyo, "wgrad_branch", after=proj[3])
    mixer = _exchange_start([g_w_branch.reshape(N_DEV, D // N_DEV, D), g_w_out.reshape(N_DEV, D // N_DEV, D)], me,
                            proj[3], "exchange_mix_start")

    upd, grads = {}, {}

    def finish(names, parts, weights):
        for nm, p, (w, m, v) in zip(names, parts, weights):
            grads[nm], *upd[nm] = _sum_adamw(p, w, m, v, "update_" + nm)
        return grads[names[-1]]

    finish(["w_gate_up", "w_down"], _copies_wait(*ffn[:3], mixer[3], "exchange_ffn_wait"),
           [(tr(w_gate_up), tr(m_w_gate_up), tr(v_w_gate_up)), (w_down, m_w_down, v_w_down)])
    (in_a,) = _copies_wait(*proj_a[:3], grads["w_gate_up"], "exchange_in_a_wait")
    (in_b,) = _copies_wait(*proj[:3], in_a, "exchange_in_b_wait")
    finish(["w_in"], [[in_a, in_b]], [(tr(w_in), tr(m_w_in), tr(v_w_in))])
    finish(["w_branch", "w_out"], _copies_wait(*mixer[:3], grads["w_in"], "exchange_mix_wait"),
           [(w_branch, m_w_branch, v_w_branch), (w_out, m_w_out, v_w_out)])
    for nm in ("w_in", "w_gate_up"):
        grads[nm], upd[nm] = tr(grads[nm]), [tr(t) for t in upd[nm]]

    grads["ada_w"] = grad_ada_w.reshape(ada_w.shape)
    upd["ada_w"] = _adamw(ada_w, grads["ada_w"], m_ada_w, v_ada_w, "adamw_ada_w")
    small_names = ["ada_b", "norm1_g", "q_norm_g", "k_norm_g", "sinks", "a_log", "dt_bias", "dn_norm_g", "norm2_g", "conv_w"]
    small_w = [ada_b, norm1_g, q_norm_g, k_norm_g, sinks, a_log, dt_bias, dn_norm_g, norm2_g, conv_w]
    small_g = [grad_ada_b, g_n1, g_qg, g_kg, g_sk, g_al, g_dt, g_dn, g_n2, grad_conv]
    small_m = [m_ada_b, m_norm1_g, m_q_norm_g, m_k_norm_g, m_sinks, m_a_log, m_dt_bias, m_dn_norm_g, m_norm2_g, m_conv_w]
    small_v = [v_ada_b, v_norm1_g, v_q_norm_g, v_k_norm_g, v_sinks, v_a_log, v_dt_bias, v_dn_norm_g, v_norm2_g, v_conv_w]
    cat = lambda arrs: jnp.concatenate([a.reshape(1, -1) for a in arrs], axis=1)
    res = _adamw(cat(small_w), cat(small_g), cat(small_m), cat(small_v), "adamw_small")
    po = np.cumsum([0] + [int(np.prod(w.shape)) for w in small_w])
    for i, nm in enumerate(small_names):
        upd[nm] = tuple(r[:, po[i]:po[i + 1]].reshape(small_w[i].shape) for r in res)
        grads[nm] = small_g[i].reshape(small_w[i].shape)

    order = ["ada_w", "ada_b", "norm1_g", "w_in", "conv_w", "q_norm_g", "k_norm_g", "sinks", "a_log", "dt_bias",
             "dn_norm_g", "w_branch", "w_out", "norm2_g", "w_gate_up", "w_down"]
    return (loss, grad_x, *[grads[n] for n in order], *[upd[n][0] for n in order],
            *[upd[n][1] for n in order], *[upd[n][2] for n in order])
```
